```python
import math
import jax, jax.numpy as jnp
from jax import lax
import numpy as np

D_MODEL = 1024
BATCH = 8
SEQ = 4096
DEPTH = 2

N_EVEN = (DEPTH + 1) // 2
N_ODD = DEPTH // 2

HGRN_DIM = D_MODEL // 2
HGRN_HEAD_DIM = 128
HGRN_HEADS = HGRN_DIM // HGRN_HEAD_DIM
HGRN_CHUNK = 64

S5_DIM = D_MODEL - HGRN_DIM
S5_GROUP = 16
S5_GROUPS = S5_DIM // S5_GROUP
S5_STATE = 64
S5_DT_MIN = 1e-3
S5_DT_MAX = 1e-1

EVEN_IN = 4 * HGRN_DIM + S5_DIM

MLA_HEADS = 8
MLA_Q_RANK = 384
MLA_KV_RANK = 256
MLA_NOPE = 128
MLA_ROPE = 64
MLA_V = 128
MLA_QK = MLA_NOPE + MLA_ROPE
ODD_IN = MLA_Q_RANK + MLA_KV_RANK + MLA_ROPE
ROPE_THETA = 10000.0
Q_BLOCK = 128

D_FF = 2816
CONV_W = 3
EPS = 1e-6

kernel_name = "hybrid_hgrn2_s5_mla_convffn"


def rmsnorm(x, g):
    xf = x.astype(jnp.float32)
    y = xf * lax.rsqrt(jnp.mean(xf * xf, axis=-1, keepdims=True) + EPS)
    return (y * g.astype(jnp.float32)).astype(x.dtype)


def hgrn2(q, f, i, g, lb, norm_g):
    f32 = jnp.float32
    B_, S_, _ = q.shape
    H, Dh, C = HGRN_HEADS, HGRN_HEAD_DIM, HGRN_CHUNK
    N = S_ // C
    lb = lb.astype(f32)
    forget = lb + (1.0 - lb) * jax.nn.sigmoid(f.astype(f32))
    key_in = 1.0 - forget

    def heads(t):
        return t.astype(f32).reshape(B_, N, C, H, Dh).transpose(0, 3, 1, 2, 4)

    qh, kh, vh = heads(q), heads(key_in), heads(i)
    b = jnp.cumsum(heads(jnp.log(forget)), axis=3)
    b_last = b[:, :, :, -1:, :]
    qd = qh * jnp.exp(b)
    kd = kh * jnp.exp(-b)
    causal = jnp.tril(jnp.ones((C, C), dtype=bool))
    att = jnp.where(causal, jnp.einsum('bhntd,bhnsd->bhnts', qd, kd), 0.0)
    o_intra = jnp.einsum('bhnts,bhnsv->bhntv', att, vh)
    d_state = jnp.einsum('bhnsd,bhnsv->bhndv', kh * jnp.exp(b_last - b), vh)
    decay = jnp.exp(b_last[:, :, :, 0, :])

    def step(state, inp):
        ds_n, dec_n = inp
        return dec_n[..., None] * state + ds_n, state

    s0 = jnp.zeros((B_, H, Dh, Dh), f32)
    _, s_start = lax.scan(step, s0, (d_state.transpose(2, 0, 1, 3, 4), decay.transpose(2, 0, 1, 3)))
    s_start = s_start.transpose(1, 2, 0, 3, 4)
    o = o_intra + jnp.einsum('bhntd,bhndv->bhntv', qd, s_start)
    o = o.transpose(0, 2, 3, 1, 4).reshape(B_, S_, H, Dh)
    o = o * lax.rsqrt(jnp.mean(o * o, axis=-1, keepdims=True) + EPS) * norm_g.astype(f32).reshape(H, Dh)
    return o.reshape(B_, S_, HGRN_DIM) * jax.nn.silu(g.astype(f32))


def s5(u, a_re, a_im, log_dt, b_re, b_im, c_re, c_im, d_skip, w_glu, b_glu):
    f32 = jnp.float32
    B_, S_, _ = u.shape
    G, P, Hc = S5_GROUPS, S5_STATE, S5_GROUP
    uf = u.astype(f32).reshape(B_, S_, G, Hc)
    ar, ai = a_re.astype(f32), a_im.astype(f32)
    dt = jnp.exp(log_dt.astype(f32))[:, None]
    mag = jnp.exp(ar * dt)
    abar_re, abar_im = mag * jnp.cos(ai * dt), mag * jnp.sin(ai * dt)
    den = ar * ar + ai * ai
    xr, xi = abar_re - 1.0, abar_im
    coef_re = ((xr * ar + xi * ai) / den)[..., None]
    coef_im = ((xi * ar - xr * ai) / den)[..., None]
    br, bi = b_re.astype(f32), b_im.astype(f32)
    bb_re = coef_re * br - coef_im * bi
    bb_im = coef_re * bi + coef_im * br
    bu_re = jnp.einsum('bsgh,gph->bsgp', uf, bb_re)
    bu_im = jnp.einsum('bsgh,gph->bsgp', uf, bb_im)
    a_seq_re = jnp.broadcast_to(abar_re, (1, S_, G, P))
    a_seq_im = jnp.broadcast_to(abar_im, (1, S_, G, P))

    def combine(left, right):
        a1r, a1i, b1r, b1i = left
        a2r, a2i, b2r, b2i = right
        return (a2r * a1r - a2i * a1i, a2r * a1i + a2i * a1r,
                a2r * b1r - a2i * b1i + b2r, a2r * b1i + a2i * b1r + b2i)

    _, _, s_re, s_im = lax.associative_scan(combine, (a_seq_re, a_seq_im, bu_re, bu_im), axis=1)
    y = (jnp.einsum('bsgp,ghp->bsgh', s_re, c_re.astype(f32))
         - jnp.einsum('bsgp,ghp->bsgh', s_im, c_im.astype(f32))
         + d_skip.astype(f32).reshape(G, Hc) * uf)
    z = jax.nn.gelu(y.reshape(B_, S_, S5_DIM))
    return z * jax.nn.sigmoid(z @ w_glu.astype(f32) + b_glu.astype(f32))


def even_mixer(hn, w_in, lb, hgrn_norm_g, a_re, a_im, log_dt, b_re, b_im, c_re, c_im,
               d_skip, w_glu, b_glu, w_out):
    proj = hn @ w_in
    q, f, i, g, u = jnp.split(proj, [HGRN_DIM, 2 * HGRN_DIM, 3 * HGRN_DIM, 4 * HGRN_DIM], axis=-1)
    y_a = hgrn2(q, f, i, g, lb, hgrn_norm_g).astype(hn.dtype)
    y_b = s5(u, a_re, a_im, log_dt, b_re, b_im, c_re, c_im, d_skip, w_glu, b_glu).astype(hn.dtype)
    return jnp.concatenate([y_a, y_b], axis=-1) @ w_out


def rotate(x, cos, sin):
    x1, x2 = jnp.split(x, 2, axis=-1)
    return jnp.concatenate([x1 * cos - x2 * sin, x1 * sin + x2 * cos], axis=-1)


def mla(hn, positions, w_in, q_norm_g, w_uq, kv_norm_g, w_ukv, w_out):
    B_, S_, _ = hn.shape
    H = MLA_HEADS
    proj = hn @ w_in
    cq, ckv, k_rope = jnp.split(proj, [MLA_Q_RANK, MLA_Q_RANK + MLA_KV_RANK], axis=-1)
    q = (rmsnorm(cq, q_norm_g) @ w_uq).reshape(B_, S_, H, MLA_QK)
    kv = (rmsnorm(ckv, kv_norm_g) @ w_ukv).reshape(B_, S_, H, MLA_NOPE + MLA_V)
    q_nope, q_rope = jnp.split(q, [MLA_NOPE], axis=-1)
    k_nope, v = jnp.split(kv, [MLA_NOPE], axis=-1)
    freqs = ROPE_THETA ** (-jnp.arange(0, MLA_ROPE, 2, dtype=jnp.float32) / MLA_ROPE)
    ang = positions.astype(jnp.float32)[..., None] * freqs
    cos = jnp.cos(ang)[:, :, None, :].astype(hn.dtype)
    sin = jnp.sin(ang)[:, :, None, :].astype(hn.dtype)
    q_rope = rotate(q_rope, cos, sin)
    k_rope = rotate(k_rope[:, :, None, :], cos, sin)
    q = jnp.concatenate([q_nope, q_rope], axis=-1)
    k = jnp.concatenate([k_nope, jnp.broadcast_to(k_rope, (B_, S_, H, MLA_ROPE))], axis=-1)
    scale = MLA_QK ** -0.5
    nb = S_ // Q_BLOCK
    qb = q.reshape(B_, nb, Q_BLOCK, H, MLA_QK).transpose(1, 0, 2, 3, 4)
    kpos = jnp.arange(S_)

    def attend(args):
        q_blk, blk = args
        s = jnp.einsum('bqhd,bkhd->bhqk', q_blk, k).astype(jnp.float32) * scale
        qpos = blk * Q_BLOCK + jnp.arange(Q_BLOCK)
        s = jnp.where(kpos[None, :] <= qpos[:, None], s, -jnp.inf)
        p = jax.nn.softmax(s, axis=-1).astype(v.dtype)
        return jnp.einsum('bhqk,bkhd->bqhd', p, v)

    o = lax.map(attend, (qb, jnp.arange(nb)))
    o = o.transpose(1, 0, 2, 3, 4).reshape(B_, S_, H * MLA_V)
    return o @ w_out


def conv_ffn(hn, w_in, conv_w, conv_b, w_out):
    a, u = jnp.split(hn @ w_in, 2, axis=-1)
    a = lax.conv_general_dilated(a, conv_w[:, None, :], window_strides=(1,),
                                 padding=[(CONV_W - 1, 0)],
                                 dimension_numbers=('NWC', 'WIO', 'NWC'),
                                 feature_group_count=D_FF) + conv_b
    return (jax.nn.silu(a) * u) @ w_out


def _fwd_setup_inputs(seed: int = 0) -> dict:
    key = jax.random.key(seed)
    ks = iter(jax.random.split(key, 40))
    nrm = lambda shape, s: jax.random.normal(next(ks), shape, jnp.float32) * s
    D = D_MODEL
    G, P, Hc = S5_GROUPS, S5_STATE, S5_GROUP
    x = nrm((BATCH, SEQ, D), 1.0)
    offset = jax.random.randint(next(ks), (BATCH, 1), 0, SEQ)
    positions = (offset + jnp.arange(SEQ)[None, :]).astype(jnp.int32)
    return {
        "x": x,
        "positions": positions,
        "norm_mix_g": 1.0 + nrm((DEPTH, D), 0.02),
        "norm_ffn_g": 1.0 + nrm((DEPTH, D), 0.02),
        "final_norm_g": 1.0 + nrm((D,), 0.02),
        "even_w_in": nrm((N_EVEN, D, EVEN_IN), D ** -0.5),
        "hgrn_lb_logits": nrm((N_EVEN + 1, HGRN_DIM), 0.1),
        "hgrn_norm_g": 1.0 + nrm((N_EVEN, HGRN_DIM), 0.02),
        "s5_a_re": -0.5 + nrm((N_EVEN, G, P), 0.01),
        "s5_a_im": jnp.pi * jnp.arange(P, dtype=jnp.float32) + nrm((N_EVEN, G, P), 0.01),
        "s5_log_dt": jax.random.uniform(next(ks), (N_EVEN, G), jnp.float32,
                                        math.log(S5_DT_MIN), math.log(S5_DT_MAX)),
        "s5_b_re": nrm((N_EVEN, G, P, Hc), (2 * Hc) ** -0.5),
        "s5_b_im": nrm((N_EVEN, G, P, Hc), (2 * Hc) ** -0.5),
        "s5_c_re": nrm((N_EVEN, G, Hc, P), (2 * P) ** -0.5),
        "s5_c_im": nrm((N_EVEN, G, Hc, P), (2 * P) ** -0.5),
        "s5_d": nrm((N_EVEN, S5_DIM), 1.0),
        "s5_w_glu": nrm((N_EVEN, S5_DIM, S5_DIM), S5_DIM ** -0.5),
        "s5_b_glu": nrm((N_EVEN, S5_DIM), 0.02),
        "even_w_out": nrm((N_EVEN, HGRN_DIM + S5_DIM, D), (HGRN_DIM + S5_DIM) ** -0.5),
        "odd_w_in": nrm((N_ODD, D, ODD_IN), D ** -0.5),
        "mla_q_norm_g": 1.0 + nrm((N_ODD, MLA_Q_RANK), 0.02),
        "mla_w_uq": nrm((N_ODD, MLA_Q_RANK, MLA_HEADS * MLA_QK), MLA_Q_RANK ** -0.5),
        "mla_kv_norm_g": 1.0 + nrm((N_ODD, MLA_KV_RANK), 0.02),
        "mla_w_ukv": nrm((N_ODD, MLA_KV_RANK, MLA_HEADS * (MLA_NOPE + MLA_V)), MLA_KV_RANK ** -0.5),
        "odd_w_out": nrm((N_ODD, MLA_HEADS * MLA_V, D), (MLA_HEADS * MLA_V) ** -0.5),
        "ffn_w_in": nrm((DEPTH, D, 2 * D_FF), D ** -0.5),
        "ffn_conv_w": nrm((DEPTH, CONV_W, D_FF), CONV_W ** -0.5),
        "ffn_conv_b": nrm((DEPTH, D_FF), 0.02),
        "ffn_w_out": nrm((DEPTH, D_FF, D), D_FF ** -0.5),
    }


def _fwd_reference(x, positions, norm_mix_g, norm_ffn_g, final_norm_g,
              even_w_in, hgrn_lb_logits, hgrn_norm_g,
              s5_a_re, s5_a_im, s5_log_dt, s5_b_re, s5_b_im, s5_c_re, s5_c_im,
              s5_d, s5_w_glu, s5_b_glu, even_w_out,
              odd_w_in, mla_q_norm_g, mla_w_uq, mla_kv_norm_g, mla_w_ukv, odd_w_out,
              ffn_w_in, ffn_conv_w, ffn_conv_b, ffn_w_out):
    lower_bounds = jnp.cumsum(jax.nn.softmax(hgrn_lb_logits.astype(jnp.float32), axis=0), axis=0)
    h = x
    for layer in range(DEPTH):
        j = layer // 2
        hn = rmsnorm(h, norm_mix_g[layer])
        if layer % 2 == 0:
            mix = even_mixer(hn, even_w_in[j], lower_bounds[j], hgrn_norm_g[j],
                             s5_a_re[j], s5_a_im[j], s5_log_dt[j], s5_b_re[j], s5_b_im[j],
                             s5_c_re[j], s5_c_im[j], s5_d[j], s5_w_glu[j], s5_b_glu[j],
                             even_w_out[j])
        else:
            mix = mla(hn, positions, odd_w_in[j], mla_q_norm_g[j], mla_w_uq[j],
                      mla_kv_norm_g[j], mla_w_ukv[j], odd_w_out[j])
        h = h + mix
        h = h + conv_ffn(rmsnorm(h, norm_ffn_g[layer]), ffn_w_in[layer], ffn_conv_w[layer],
                         ffn_conv_b[layer], ffn_w_out[layer])
    return rmsnorm(h, final_norm_g)


import jax as _jax
import jax.numpy as _jnp

TWIN_FORMAT = 'train_step'
FWD_PARAMS = ['x', 'positions', 'norm_mix_g', 'norm_ffn_g', 'final_norm_g', 'even_w_in', 'hgrn_lb_logits', 'hgrn_norm_g', 's5_a_re', 's5_a_im', 's5_log_dt', 's5_b_re', 's5_b_im', 's5_c_re', 's5_c_im', 's5_d', 's5_w_glu', 's5_b_glu', 'even_w_out', 'odd_w_in', 'mla_q_norm_g', 'mla_w_uq', 'mla_kv_norm_g', 'mla_w_ukv', 'odd_w_out', 'ffn_w_in', 'ffn_conv_w', 'ffn_conv_b', 'ffn_w_out']
TWIN_WEIGHTS = ['norm_mix_g', 'norm_ffn_g', 'final_norm_g', 'even_w_in', 'hgrn_lb_logits', 'hgrn_norm_g', 's5_a_re', 's5_a_im', 's5_log_dt', 's5_b_re', 's5_b_im', 's5_c_re', 's5_c_im', 's5_d', 's5_w_glu', 's5_b_glu', 'even_w_out', 'odd_w_in', 'mla_q_norm_g', 'mla_w_uq', 'mla_kv_norm_g', 'mla_w_ukv', 'odd_w_out', 'ffn_w_in', 'ffn_conv_w', 'ffn_conv_b', 'ffn_w_out']
TWIN_DIFF_INPUT = 'x'
TWIN_INPUTS = ['x', 'positions', 'norm_mix_g', 'norm_ffn_g', 'final_norm_g', 'even_w_in', 'hgrn_lb_logits', 'hgrn_norm_g', 's5_a_re', 's5_a_im', 's5_log_dt', 's5_b_re', 's5_b_im', 's5_c_re', 's5_c_im', 's5_d', 's5_w_glu', 's5_b_glu', 'even_w_out', 'odd_w_in', 'mla_q_norm_g', 'mla_w_uq', 'mla_kv_norm_g', 'mla_w_ukv', 'odd_w_out', 'ffn_w_in', 'ffn_conv_w', 'ffn_conv_b', 'ffn_w_out', 'loss_target', 'm_norm_mix_g', 'm_norm_ffn_g', 'm_final_norm_g', 'm_even_w_in', 'm_hgrn_lb_logits', 'm_hgrn_norm_g', 'm_s5_a_re', 'm_s5_a_im', 'm_s5_log_dt', 'm_s5_b_re', 'm_s5_b_im', 'm_s5_c_re', 'm_s5_c_im', 'm_s5_d', 'm_s5_w_glu', 'm_s5_b_glu', 'm_even_w_out', 'm_odd_w_in', 'm_mla_q_norm_g', 'm_mla_w_uq', 'm_mla_kv_norm_g', 'm_mla_w_ukv', 'm_odd_w_out', 'm_ffn_w_in', 'm_ffn_conv_w', 'm_ffn_conv_b', 'm_ffn_w_out', 'v_norm_mix_g', 'v_norm_ffn_g', 'v_final_norm_g', 'v_even_w_in', 'v_hgrn_lb_logits', 'v_hgrn_norm_g', 'v_s5_a_re', 'v_s5_a_im', 'v_s5_log_dt', 'v_s5_b_re', 'v_s5_b_im', 'v_s5_c_re', 'v_s5_c_im', 'v_s5_d', 'v_s5_w_glu', 'v_s5_b_glu', 'v_even_w_out', 'v_odd_w_in', 'v_mla_q_norm_g', 'v_mla_w_uq', 'v_mla_kv_norm_g', 'v_mla_w_ukv', 'v_odd_w_out', 'v_ffn_w_in', 'v_ffn_conv_w', 'v_ffn_conv_b', 'v_ffn_w_out']
TWIN_OUTPUTS = ['loss', 'grad_x', 'grad_norm_mix_g', 'grad_norm_ffn_g', 'grad_final_norm_g', 'grad_even_w_in', 'grad_hgrn_lb_logits', 'grad_hgrn_norm_g', 'grad_s5_a_re', 'grad_s5_a_im', 'grad_s5_log_dt', 'grad_s5_b_re', 'grad_s5_b_im', 'grad_s5_c_re', 'grad_s5_c_im', 'grad_s5_d', 'grad_s5_w_glu', 'grad_s5_b_glu', 'grad_even_w_out', 'grad_odd_w_in', 'grad_mla_q_norm_g', 'grad_mla_w_uq', 'grad_mla_kv_norm_g', 'grad_mla_w_ukv', 'grad_odd_w_out', 'grad_ffn_w_in', 'grad_ffn_conv_w', 'grad_ffn_conv_b', 'grad_ffn_w_out', 'delta_norm_mix_g', 'delta_norm_ffn_g', 'delta_final_norm_g', 'delta_even_w_in', 'delta_hgrn_lb_logits', 'delta_hgrn_norm_g', 'delta_s5_a_re', 'delta_s5_a_im', 'delta_s5_log_dt', 'delta_s5_b_re', 'delta_s5_b_im', 'delta_s5_c_re', 'delta_s5_c_im', 'delta_s5_d', 'delta_s5_w_glu', 'delta_s5_b_glu', 'delta_even_w_out', 'delta_odd_w_in', 'delta_mla_q_norm_g', 'delta_mla_w_uq', 'delta_mla_kv_norm_g', 'delta_mla_w_ukv', 'delta_odd_w_out', 'delta_ffn_w_in', 'delta_ffn_conv_w', 'delta_ffn_conv_b', 'delta_ffn_w_out', 'new_m_norm_mix_g', 'new_m_norm_ffn_g', 'new_m_final_norm_g', 'new_m_even_w_in', 'new_m_hgrn_lb_logits', 'new_m_hgrn_norm_g', 'new_m_s5_a_re', 'new_m_s5_a_im', 'new_m_s5_log_dt', 'new_m_s5_b_re', 'new_m_s5_b_im', 'new_m_s5_c_re', 'new_m_s5_c_im', 'new_m_s5_d', 'new_m_s5_w_glu', 'new_m_s5_b_glu', 'new_m_even_w_out', 'new_m_odd_w_in', 'new_m_mla_q_norm_g', 'new_m_mla_w_uq', 'new_m_mla_kv_norm_g', 'new_m_mla_w_ukv', 'new_m_odd_w_out', 'new_m_ffn_w_in', 'new_m_ffn_conv_w', 'new_m_ffn_conv_b', 'new_m_ffn_w_out', 'new_v_norm_mix_g', 'new_v_norm_ffn_g', 'new_v_final_norm_g', 'new_v_even_w_in', 'new_v_hgrn_lb_logits', 'new_v_hgrn_norm_g', 'new_v_s5_a_re', 'new_v_s5_a_im', 'new_v_s5_log_dt', 'new_v_s5_b_re', 'new_v_s5_b_im', 'new_v_s5_c_re', 'new_v_s5_c_im', 'new_v_s5_d', 'new_v_s5_w_glu', 'new_v_s5_b_glu', 'new_v_even_w_out', 'new_v_odd_w_in', 'new_v_mla_q_norm_g', 'new_v_mla_w_uq', 'new_v_mla_kv_norm_g', 'new_v_mla_w_ukv', 'new_v_odd_w_out', 'new_v_ffn_w_in', 'new_v_ffn_conv_w', 'new_v_ffn_conv_b', 'new_v_ffn_w_out']
TWIN_LEAF_KINDS = {'loss': 'loss', 'grad_x': 'grad_x', 'grad_norm_mix_g': 'grad_w', 'grad_norm_ffn_g': 'grad_w', 'grad_final_norm_g': 'grad_w', 'grad_even_w_in': 'grad_w', 'grad_hgrn_lb_logits': 'grad_w', 'grad_hgrn_norm_g': 'grad_w', 'grad_s5_a_re': 'grad_w', 'grad_s5_a_im': 'grad_w', 'grad_s5_log_dt': 'grad_w', 'grad_s5_b_re': 'grad_w', 'grad_s5_b_im': 'grad_w', 'grad_s5_c_re': 'grad_w', 'grad_s5_c_im': 'grad_w', 'grad_s5_d': 'grad_w', 'grad_s5_w_glu': 'grad_w', 'grad_s5_b_glu': 'grad_w', 'grad_even_w_out': 'grad_w', 'grad_odd_w_in': 'grad_w', 'grad_mla_q_norm_g': 'grad_w', 'grad_mla_w_uq': 'grad_w', 'grad_mla_kv_norm_g': 'grad_w', 'grad_mla_w_ukv': 'grad_w', 'grad_odd_w_out': 'grad_w', 'grad_ffn_w_in': 'grad_w', 'grad_ffn_conv_w': 'grad_w', 'grad_ffn_conv_b': 'grad_w', 'grad_ffn_w_out': 'grad_w', 'delta_norm_mix_g': 'delta_w', 'delta_norm_ffn_g': 'delta_w', 'delta_final_norm_g': 'delta_w', 'delta_even_w_in': 'delta_w', 'delta_hgrn_lb_logits': 'delta_w', 'delta_hgrn_norm_g': 'delta_w', 'delta_s5_a_re': 'delta_w', 'delta_s5_a_im': 'delta_w', 'delta_s5_log_dt': 'delta_w', 'delta_s5_b_re': 'delta_w', 'delta_s5_b_im': 'delta_w', 'delta_s5_c_re': 'delta_w', 'delta_s5_c_im': 'delta_w', 'delta_s5_d': 'delta_w', 'delta_s5_w_glu': 'delta_w', 'delta_s5_b_glu': 'delta_w', 'delta_even_w_out': 'delta_w', 'delta_odd_w_in': 'delta_w', 'delta_mla_q_norm_g': 'delta_w', 'delta_mla_w_uq': 'delta_w', 'delta_mla_kv_norm_g': 'delta_w', 'delta_mla_w_ukv': 'delta_w', 'delta_odd_w_out': 'delta_w', 'delta_ffn_w_in': 'delta_w', 'delta_ffn_conv_w': 'delta_w', 'delta_ffn_conv_b': 'delta_w', 'delta_ffn_w_out': 'delta_w', 'new_m_norm_mix_g': 'new_m', 'new_m_norm_ffn_g': 'new_m', 'new_m_final_norm_g': 'new_m', 'new_m_even_w_in': 'new_m', 'new_m_hgrn_lb_logits': 'new_m', 'new_m_hgrn_norm_g': 'new_m', 'new_m_s5_a_re': 'new_m', 'new_m_s5_a_im': 'new_m', 'new_m_s5_log_dt': 'new_m', 'new_m_s5_b_re': 'new_m', 'new_m_s5_b_im': 'new_m', 'new_m_s5_c_re': 'new_m', 'new_m_s5_c_im': 'new_m', 'new_m_s5_d': 'new_m', 'new_m_s5_w_glu': 'new_m', 'new_m_s5_b_glu': 'new_m', 'new_m_even_w_out': 'new_m', 'new_m_odd_w_in': 'new_m', 'new_m_mla_q_norm_g': 'new_m', 'new_m_mla_w_uq': 'new_m', 'new_m_mla_kv_norm_g': 'new_m', 'new_m_mla_w_ukv': 'new_m', 'new_m_odd_w_out': 'new_m', 'new_m_ffn_w_in': 'new_m', 'new_m_ffn_conv_w': 'new_m', 'new_m_ffn_conv_b': 'new_m', 'new_m_ffn_w_out': 'new_m', 'new_v_norm_mix_g': 'new_v', 'new_v_norm_ffn_g': 'new_v', 'new_v_final_norm_g': 'new_v', 'new_v_even_w_in': 'new_v', 'new_v_hgrn_lb_logits': 'new_v', 'new_v_hgrn_norm_g': 'new_v', 'new_v_s5_a_re': 'new_v', 'new_v_s5_a_im': 'new_v', 'new_v_s5_log_dt': 'new_v', 'new_v_s5_b_re': 'new_v', 'new_v_s5_b_im': 'new_v', 'new_v_s5_c_re': 'new_v', 'new_v_s5_c_im': 'new_v', 'new_v_s5_d': 'new_v', 'new_v_s5_w_glu': 'new_v', 'new_v_s5_b_glu': 'new_v', 'new_v_even_w_out': 'new_v', 'new_v_odd_w_in': 'new_v', 'new_v_mla_q_norm_g': 'new_v', 'new_v_mla_w_uq': 'new_v', 'new_v_mla_kv_norm_g': 'new_v', 'new_v_mla_w_ukv': 'new_v', 'new_v_odd_w_out': 'new_v', 'new_v_ffn_w_in': 'new_v', 'new_v_ffn_conv_w': 'new_v', 'new_v_ffn_conv_b': 'new_v', 'new_v_ffn_w_out': 'new_v'}


def _forward(args):
    return _fwd_reference(*[args[k] for k in FWD_PARAMS])


def _output_shape():
    out = _jax.eval_shape(lambda: _forward(_fwd_setup_inputs(0)))
    return out.shape, out.dtype

N_MICROBATCH = 1
ADAM_LR = 0.001
ADAM_B1 = 0.9
ADAM_B2 = 0.999
ADAM_EPS = 1e-08
ADAM_WD = 0.01
ADAM_STEP = 10
PER_EXAMPLE_BATCH_AXIS = {'x': 0, 'positions': 0, 'loss_target': 0}
SHARED_INPUTS = []
_WEIGHT_DTYPES = {'norm_mix_g': _jnp.float32, 'norm_ffn_g': _jnp.float32, 'final_norm_g': _jnp.float32, 'even_w_in': _jnp.float32, 'hgrn_lb_logits': _jnp.float32, 'hgrn_norm_g': _jnp.float32, 's5_a_re': _jnp.float32, 's5_a_im': _jnp.float32, 's5_log_dt': _jnp.float32, 's5_b_re': _jnp.float32, 's5_b_im': _jnp.float32, 's5_c_re': _jnp.float32, 's5_c_im': _jnp.float32, 's5_d': _jnp.float32, 's5_w_glu': _jnp.float32, 's5_b_glu': _jnp.float32, 'even_w_out': _jnp.float32, 'odd_w_in': _jnp.float32, 'mla_q_norm_g': _jnp.float32, 'mla_w_uq': _jnp.float32, 'mla_kv_norm_g': _jnp.float32, 'mla_w_ukv': _jnp.float32, 'odd_w_out': _jnp.float32, 'ffn_w_in': _jnp.float32, 'ffn_conv_w': _jnp.float32, 'ffn_conv_b': _jnp.float32, 'ffn_w_out': _jnp.float32}
MOMENT_SCALE = {'norm_mix_g': 1.439352e-01, 'norm_ffn_g': 1.226131e-01, 'final_norm_g': 3.195798e+01, 'even_w_in': 1.203975e-01, 'hgrn_lb_logits': 8.350754e-02, 'hgrn_norm_g': 1.199344e-01, 's5_a_re': 3.164811e-03, 's5_a_im': 2.839896e-03, 's5_log_dt': 3.819049e+00, 's5_b_re': 2.325411e-03, 's5_b_im': 2.188304e-03, 's5_c_re': 4.735024e-03, 's5_c_im': 4.425777e-03, 's5_d': 7.983253e-02, 's5_w_glu': 2.169792e-02, 's5_b_glu': 3.400543e-02, 'even_w_out': 9.673246e-02, 'odd_w_in': 6.085160e-02, 'mla_q_norm_g': 4.251430e-02, 'mla_w_uq': 2.153074e-02, 'mla_kv_norm_g': 8.236481e-02, 'mla_w_ukv': 2.931440e-02, 'odd_w_out': 3.551875e-02, 'ffn_w_in': 5.239098e-02, 'ffn_conv_w': 5.323765e-02, 'ffn_conv_b': 5.178996e-02, 'ffn_w_out': 8.549421e-02}


def _to_microbatches(a, axis):
    t = _jnp.moveaxis(a, axis, 0)
    t = t.reshape((N_MICROBATCH, t.shape[0] // N_MICROBATCH) + t.shape[1:])
    return _jnp.moveaxis(t, 1, axis + 1)


def setup_inputs(seed: int = 0) -> dict:
    inp = _fwd_setup_inputs(seed)
    key = _jax.random.fold_in(_jax.random.key(seed), 7919)
    shape, _ = _output_shape()
    out = dict(inp)
    out["loss_target"] = _jax.random.normal(_jax.random.fold_in(key, 0), shape, _jnp.float32)
    for i, name in enumerate(TWIN_WEIGHTS):
        w = inp[name].astype(_jnp.float32)
        if MOMENT_SCALE is None:
            s = _jnp.sqrt(_jnp.mean(_jnp.square(w)) + 1e-30)
        else:
            s = MOMENT_SCALE[name]
        km, kv = _jax.random.split(_jax.random.fold_in(key, i + 1))
        out[name] = w
        out["m_" + name] = s * _jax.random.normal(km, w.shape, _jnp.float32)
        out["v_" + name] = (s * s) * _jax.random.uniform(kv, w.shape, _jnp.float32, 0.5, 1.5)
    if N_MICROBATCH > 1:
        for name, axis in PER_EXAMPLE_BATCH_AXIS.items():
            out[name] = _to_microbatches(out[name], axis)
    return {'x': out['x'], 'positions': out['positions'], 'norm_mix_g': out['norm_mix_g'], 'norm_ffn_g': out['norm_ffn_g'], 'final_norm_g': out['final_norm_g'], 'even_w_in': out['even_w_in'], 'hgrn_lb_logits': out['hgrn_lb_logits'], 'hgrn_norm_g': out['hgrn_norm_g'], 's5_a_re': out['s5_a_re'], 's5_a_im': out['s5_a_im'], 's5_log_dt': out['s5_log_dt'], 's5_b_re': out['s5_b_re'], 's5_b_im': out['s5_b_im'], 's5_c_re': out['s5_c_re'], 's5_c_im': out['s5_c_im'], 's5_d': out['s5_d'], 's5_w_glu': out['s5_w_glu'], 's5_b_glu': out['s5_b_glu'], 'even_w_out': out['even_w_out'], 'odd_w_in': out['odd_w_in'], 'mla_q_norm_g': out['mla_q_norm_g'], 'mla_w_uq': out['mla_w_uq'], 'mla_kv_norm_g': out['mla_kv_norm_g'], 'mla_w_ukv': out['mla_w_ukv'], 'odd_w_out': out['odd_w_out'], 'ffn_w_in': out['ffn_w_in'], 'ffn_conv_w': out['ffn_conv_w'], 'ffn_conv_b': out['ffn_conv_b'], 'ffn_w_out': out['ffn_w_out'], 'loss_target': out['loss_target'], 'm_norm_mix_g': out['m_norm_mix_g'], 'm_norm_ffn_g': out['m_norm_ffn_g'], 'm_final_norm_g': out['m_final_norm_g'], 'm_even_w_in': out['m_even_w_in'], 'm_hgrn_lb_logits': out['m_hgrn_lb_logits'], 'm_hgrn_norm_g': out['m_hgrn_norm_g'], 'm_s5_a_re': out['m_s5_a_re'], 'm_s5_a_im': out['m_s5_a_im'], 'm_s5_log_dt': out['m_s5_log_dt'], 'm_s5_b_re': out['m_s5_b_re'], 'm_s5_b_im': out['m_s5_b_im'], 'm_s5_c_re': out['m_s5_c_re'], 'm_s5_c_im': out['m_s5_c_im'], 'm_s5_d': out['m_s5_d'], 'm_s5_w_glu': out['m_s5_w_glu'], 'm_s5_b_glu': out['m_s5_b_glu'], 'm_even_w_out': out['m_even_w_out'], 'm_odd_w_in': out['m_odd_w_in'], 'm_mla_q_norm_g': out['m_mla_q_norm_g'], 'm_mla_w_uq': out['m_mla_w_uq'], 'm_mla_kv_norm_g': out['m_mla_kv_norm_g'], 'm_mla_w_ukv': out['m_mla_w_ukv'], 'm_odd_w_out': out['m_odd_w_out'], 'm_ffn_w_in': out['m_ffn_w_in'], 'm_ffn_conv_w': out['m_ffn_conv_w'], 'm_ffn_conv_b': out['m_ffn_conv_b'], 'm_ffn_w_out': out['m_ffn_w_out'], 'v_norm_mix_g': out['v_norm_mix_g'], 'v_norm_ffn_g': out['v_norm_ffn_g'], 'v_final_norm_g': out['v_final_norm_g'], 'v_even_w_in': out['v_even_w_in'], 'v_hgrn_lb_logits': out['v_hgrn_lb_logits'], 'v_hgrn_norm_g': out['v_hgrn_norm_g'], 'v_s5_a_re': out['v_s5_a_re'], 'v_s5_a_im': out['v_s5_a_im'], 'v_s5_log_dt': out['v_s5_log_dt'], 'v_s5_b_re': out['v_s5_b_re'], 'v_s5_b_im': out['v_s5_b_im'], 'v_s5_c_re': out['v_s5_c_re'], 'v_s5_c_im': out['v_s5_c_im'], 'v_s5_d': out['v_s5_d'], 'v_s5_w_glu': out['v_s5_w_glu'], 'v_s5_b_glu': out['v_s5_b_glu'], 'v_even_w_out': out['v_even_w_out'], 'v_odd_w_in': out['v_odd_w_in'], 'v_mla_q_norm_g': out['v_mla_q_norm_g'], 'v_mla_w_uq': out['v_mla_w_uq'], 'v_mla_kv_norm_g': out['v_mla_kv_norm_g'], 'v_mla_w_ukv': out['v_mla_w_ukv'], 'v_odd_w_out': out['v_odd_w_out'], 'v_ffn_w_in': out['v_ffn_w_in'], 'v_ffn_conv_w': out['v_ffn_conv_w'], 'v_ffn_conv_b': out['v_ffn_conv_b'], 'v_ffn_w_out': out['v_ffn_w_out']}


def _loss(weights, diff, rest, loss_target):
    with _jax.named_scope("forward"):
        args = {**rest, TWIN_DIFF_INPUT: diff, **{k: w.astype(_WEIGHT_DTYPES[k]) for k, w in weights.items()}}
        y = _forward(args)
    with _jax.named_scope("loss_head"):
        err = _jnp.square(y.astype(_jnp.float32) - loss_target)
        return 0.5 * _jnp.sum(_jnp.mean(err, axis=-1)) if err.ndim else 0.5 * err


def _adamw(w, g, m, v):
    m = ADAM_B1 * m + (1.0 - ADAM_B1) * g
    v = ADAM_B2 * v + (1.0 - ADAM_B2) * _jnp.square(g)
    m_hat = m / (1.0 - ADAM_B1 ** ADAM_STEP)
    v_hat = v / (1.0 - ADAM_B2 ** ADAM_STEP)
    delta = -ADAM_LR * (m_hat / (_jnp.sqrt(v_hat) + ADAM_EPS) + ADAM_WD * w)
    return delta, m, v


def reference(x, positions, norm_mix_g, norm_ffn_g, final_norm_g, even_w_in, hgrn_lb_logits, hgrn_norm_g, s5_a_re, s5_a_im, s5_log_dt, s5_b_re, s5_b_im, s5_c_re, s5_c_im, s5_d, s5_w_glu, s5_b_glu, even_w_out, odd_w_in, mla_q_norm_g, mla_w_uq, mla_kv_norm_g, mla_w_ukv, odd_w_out, ffn_w_in, ffn_conv_w, ffn_conv_b, ffn_w_out, loss_target, m_norm_mix_g, m_norm_ffn_g, m_final_norm_g, m_even_w_in, m_hgrn_lb_logits, m_hgrn_norm_g, m_s5_a_re, m_s5_a_im, m_s5_log_dt, m_s5_b_re, m_s5_b_im, m_s5_c_re, m_s5_c_im, m_s5_d, m_s5_w_glu, m_s5_b_glu, m_even_w_out, m_odd_w_in, m_mla_q_norm_g, m_mla_w_uq, m_mla_kv_norm_g, m_mla_w_ukv, m_odd_w_out, m_ffn_w_in, m_ffn_conv_w, m_ffn_conv_b, m_ffn_w_out, v_norm_mix_g, v_norm_ffn_g, v_final_norm_g, v_even_w_in, v_hgrn_lb_logits, v_hgrn_norm_g, v_s5_a_re, v_s5_a_im, v_s5_log_dt, v_s5_b_re, v_s5_b_im, v_s5_c_re, v_s5_c_im, v_s5_d, v_s5_w_glu, v_s5_b_glu, v_even_w_out, v_odd_w_in, v_mla_q_norm_g, v_mla_w_uq, v_mla_kv_norm_g, v_mla_w_ukv, v_odd_w_out, v_ffn_w_in, v_ffn_conv_w, v_ffn_conv_b, v_ffn_w_out):
    given = dict(x=x, positions=positions, norm_mix_g=norm_mix_g, norm_ffn_g=norm_ffn_g, final_norm_g=final_norm_g, even_w_in=even_w_in, hgrn_lb_logits=hgrn_lb_logits, hgrn_norm_g=hgrn_norm_g, s5_a_re=s5_a_re, s5_a_im=s5_a_im, s5_log_dt=s5_log_dt, s5_b_re=s5_b_re, s5_b_im=s5_b_im, s5_c_re=s5_c_re, s5_c_im=s5_c_im, s5_d=s5_d, s5_w_glu=s5_w_glu, s5_b_glu=s5_b_glu, even_w_out=even_w_out, odd_w_in=odd_w_in, mla_q_norm_g=mla_q_norm_g, mla_w_uq=mla_w_uq, mla_kv_norm_g=mla_kv_norm_g, mla_w_ukv=mla_w_ukv, odd_w_out=odd_w_out, ffn_w_in=ffn_w_in, ffn_conv_w=ffn_conv_w, ffn_conv_b=ffn_conv_b, ffn_w_out=ffn_w_out, loss_target=loss_target, m_norm_mix_g=m_norm_mix_g, m_norm_ffn_g=m_norm_ffn_g, m_final_norm_g=m_final_norm_g, m_even_w_in=m_even_w_in, m_hgrn_lb_logits=m_hgrn_lb_logits, m_hgrn_norm_g=m_hgrn_norm_g, m_s5_a_re=m_s5_a_re, m_s5_a_im=m_s5_a_im, m_s5_log_dt=m_s5_log_dt, m_s5_b_re=m_s5_b_re, m_s5_b_im=m_s5_b_im, m_s5_c_re=m_s5_c_re, m_s5_c_im=m_s5_c_im, m_s5_d=m_s5_d, m_s5_w_glu=m_s5_w_glu, m_s5_b_glu=m_s5_b_glu, m_even_w_out=m_even_w_out, m_odd_w_in=m_odd_w_in, m_mla_q_norm_g=m_mla_q_norm_g, m_mla_w_uq=m_mla_w_uq, m_mla_kv_norm_g=m_mla_kv_norm_g, m_mla_w_ukv=m_mla_w_ukv, m_odd_w_out=m_odd_w_out, m_ffn_w_in=m_ffn_w_in, m_ffn_conv_w=m_ffn_conv_w, m_ffn_conv_b=m_ffn_conv_b, m_ffn_w_out=m_ffn_w_out, v_norm_mix_g=v_norm_mix_g, v_norm_ffn_g=v_norm_ffn_g, v_final_norm_g=v_final_norm_g, v_even_w_in=v_even_w_in, v_hgrn_lb_logits=v_hgrn_lb_logits, v_hgrn_norm_g=v_hgrn_norm_g, v_s5_a_re=v_s5_a_re, v_s5_a_im=v_s5_a_im, v_s5_log_dt=v_s5_log_dt, v_s5_b_re=v_s5_b_re, v_s5_b_im=v_s5_b_im, v_s5_c_re=v_s5_c_re, v_s5_c_im=v_s5_c_im, v_s5_d=v_s5_d, v_s5_w_glu=v_s5_w_glu, v_s5_b_glu=v_s5_b_glu, v_even_w_out=v_even_w_out, v_odd_w_in=v_odd_w_in, v_mla_q_norm_g=v_mla_q_norm_g, v_mla_w_uq=v_mla_w_uq, v_mla_kv_norm_g=v_mla_kv_norm_g, v_mla_w_ukv=v_mla_w_ukv, v_odd_w_out=v_odd_w_out, v_ffn_w_in=v_ffn_w_in, v_ffn_conv_w=v_ffn_conv_w, v_ffn_conv_b=v_ffn_conv_b, v_ffn_w_out=v_ffn_w_out)
    weights = {n: given[n] for n in TWIN_WEIGHTS}
    shared = {n: given[n] for n in SHARED_INPUTS}
    per_example = {n: given[n] for n in ['x', 'positions']}
    grad_fn = _jax.value_and_grad(_loss, argnums=(0, 1))

    def one_microbatch(ex, loss_target):
        ex = dict(ex)
        diff = ex.pop(TWIN_DIFF_INPUT)
        return grad_fn(weights, diff, {**shared, **ex}, loss_target)

    if N_MICROBATCH == 1:
        loss, (grad_w, grad_x) = one_microbatch(per_example, given["loss_target"])
    else:
        def body(carry, xs):
            loss_sum, grad_sum = carry
            l_k, (gw_k, gx_k) = one_microbatch(xs[0], xs[1])
            with _jax.named_scope("update"):
                return (loss_sum + l_k, _jax.tree.map(_jnp.add, grad_sum, gw_k)), gx_k

        init = (_jnp.zeros((), _jnp.float32), _jax.tree.map(_jnp.zeros_like, weights))
        (loss, grad_w), grad_x = _jax.lax.scan(body, init, (per_example, given["loss_target"]))
    with _jax.named_scope("update"):
        delta_w, new_m, new_v = {}, {}, {}
        for n in TWIN_WEIGHTS:
            delta_w[n], new_m[n], new_v[n] = _adamw(weights[n], grad_w[n], given["m_" + n], given["v_" + n])
    return (loss, grad_x, *[grad_w[n] for n in TWIN_WEIGHTS], *[delta_w[n] for n in TWIN_WEIGHTS],
            *[new_m[n] for n in TWIN_WEIGHTS], *[new_v[n] for n in TWIN_WEIGHTS])
```

```python
import functools
import math

import jax
import jax.numpy as jnp
from jax import lax
from jax.experimental import pallas as pl
from jax.experimental.pallas import tpu as pltpu

f32, bf16 = jnp.float32, jnp.bfloat16
EPS = 1e-6
LANES = 128
SUBLANES = 8
VMEM_BYTES = 48 * 1024 * 1024
HGRN_CHUNK = 64
HGRN_HEADS = 4
S5_GROUPS, S5_STATE, S5_GROUP = 32, 64, 16
S5_N = S5_GROUPS * S5_STATE
S5_SEG = SUBLANES
MLA_HEADS, MLA_NOPE, MLA_ROPE, MLA_V = 8, 128, 64, 128
MLA_QK = MLA_NOPE + MLA_ROPE
MLA_Q_RANK, MLA_KV_RANK = 384, 256
ROPE_THETA = 10000.0
D_FF = 2816
ADAM_LR, ADAM_B1, ADAM_B2, ADAM_EPS, ADAM_WD, ADAM_STEP = 0.001, 0.9, 0.999, 1e-08, 0.01, 10
MESH = pl.DeviceIdType.MESH
HI = lax.Precision.HIGHEST


def _cp(*dims):
    return pltpu.CompilerParams(dimension_semantics=dims if dims else None, vmem_limit_bytes=VMEM_BYTES)


def _tile(n, t):
    if n <= t:
        return n
    c = (t // LANES) * LANES
    while c >= LANES:
        if n % c == 0:
            return c
        c -= LANES
    return n


def _dot(a, b, dn=None, precision=None):
    if dn is None:
        dn = (((a.ndim - 1,), (0,)), ((), ()))
    return lax.dot_general(a, b, dn, preferred_element_type=f32, precision=precision)


NT = (((1,), (1,)), ((), ()))
TN = (((0,), (0,)), ((), ()))


def _bdot(a, b, dn=None):
    return _dot(a.astype(bf16), b.astype(bf16), dn)


def _mm(a, b, *, name, ta=False, tb=False, out_dtype=f32, res=None, also_bf16=False, tm=1024, tn=1024, tk=1024):
    M, K = (a.shape[1], a.shape[0]) if ta else a.shape
    N = b.shape[0] if tb else b.shape[1]
    tm, tn, tk = _tile(M, tm), _tile(N, tn), _tile(K, tk)
    nk = K // tk
    dn = (((0 if ta else 1,), (1 if tb else 0,)), ((), ()))

    def body(*refs):
        a_ref, b_ref = refs[0], refs[1]
        r_ref = refs[2] if res is not None else None
        outs = refs[3 if res is not None else 2:-1]
        acc = refs[-1]
        k = pl.program_id(2)
        p = _bdot(a_ref[...], b_ref[...], dn)

        @pl.when(k == 0)
        def _():
            acc[...] = p

        @pl.when(k > 0)
        def _():
            acc[...] += p

        @pl.when(k == nk - 1)
        def _():
            r = acc[...]
            if r_ref is not None:
                r = r + r_ref[...]
            outs[0][...] = r.astype(out_dtype)
            if also_bf16:
                outs[1][...] = r.astype(bf16)

    a_spec = pl.BlockSpec((tk, tm), lambda i, j, k: (k, i)) if ta else pl.BlockSpec((tm, tk), lambda i, j, k: (i, k))
    b_spec = pl.BlockSpec((tn, tk), lambda i, j, k: (j, k)) if tb else pl.BlockSpec((tk, tn), lambda i, j, k: (k, j))
    o_spec = pl.BlockSpec((tm, tn), lambda i, j, k: (i, j))
    in_specs, args = [a_spec, b_spec], [a, b]
    if res is not None:
        in_specs.append(o_spec)
        args.append(res)
    out_shape = [jax.ShapeDtypeStruct((M, N), out_dtype)]
    out_specs = [o_spec]
    if also_bf16:
        out_shape.append(jax.ShapeDtypeStruct((M, N), bf16))
        out_specs.append(o_spec)
    out = pl.pallas_call(
        body, name=name, grid=(M // tm, N // tn, nk), in_specs=in_specs, out_specs=out_specs, out_shape=out_shape,
        scratch_shapes=[pltpu.VMEM((tm, tn), f32)], compiler_params=_cp("parallel", "parallel", "arbitrary"),
    )(*args)
    return out if also_bf16 else out[0]


def _rms_fwd(x, g, *, name, col=0, width=None, tm=512):
    T = x.shape[0]
    width = x.shape[1] if width is None else width
    tm = _tile(T, tm)

    def body(x_ref, g_ref, o_ref):
        xv = x_ref[...]
        r = lax.rsqrt(jnp.mean(xv * xv, axis=-1, keepdims=True) + EPS)
        o_ref[...] = (xv * r * g_ref[...]).astype(bf16)

    return pl.pallas_call(
        body, name=name, grid=(T // tm,),
        in_specs=[pl.BlockSpec((tm, width), lambda i: (i, col)), pl.BlockSpec((1, width), lambda i: (0, 0))],
        out_specs=pl.BlockSpec((tm, width), lambda i: (i, 0)), out_shape=jax.ShapeDtypeStruct((T, width), bf16),
        compiler_params=_cp("parallel"),
    )(x, g)


def _rms_bwd_math(xv, g, dy):
    r = lax.rsqrt(jnp.mean(xv * xv, axis=-1, keepdims=True) + EPS)
    xh = xv * r
    dxh = dy * g
    dx = r * (dxh - xh * jnp.mean(dxh * xh, axis=-1, keepdims=True))
    dg = jnp.sum(dy * xh, axis=0, keepdims=True)
    return dx, dg


def _rms_bwd(x, g, dy, res=None, *, name, tm=512):
    T, D = x.shape
    tm = _tile(T, tm)

    def body(*refs):
        x_ref, g_ref, dy_ref = refs[:3]
        r_ref = refs[3] if res is not None else None
        dx_ref, dg_ref = refs[-2:]
        dx, dg = _rms_bwd_math(x_ref[...], g_ref[...], dy_ref[...].astype(f32))
        if r_ref is not None:
            dx = dx + r_ref[...]
        dx_ref[...] = dx

        @pl.when(pl.program_id(0) == 0)
        def _():
            dg_ref[...] = dg

        @pl.when(pl.program_id(0) > 0)
        def _():
            dg_ref[...] += dg

    row = pl.BlockSpec((tm, D), lambda i: (i, 0))
    vec = pl.BlockSpec((1, D), lambda i: (0, 0))
    in_specs, args = [row, vec, row], [x, g, dy]
    if res is not None:
        in_specs.append(row)
        args.append(res)
    return pl.pallas_call(
        body, name=name, grid=(T // tm,), in_specs=in_specs, out_specs=[row, vec],
        out_shape=[jax.ShapeDtypeStruct((T, D), f32), jax.ShapeDtypeStruct((1, D), f32)],
        compiler_params=_cp("arbitrary"),
    )(*args)


def _loss_head(h, g, target, *, tm=512):
    T, D = h.shape
    tm = _tile(T, tm)

    def body(h_ref, g_ref, t_ref, loss_ref, dh_ref, dg_ref):
        hv, gv = h_ref[...], g_ref[...]
        r = lax.rsqrt(jnp.mean(hv * hv, axis=-1, keepdims=True) + EPS)
        e = hv * r * gv - t_ref[...]
        part = 0.5 * jnp.sum(jnp.mean(e * e, axis=-1, keepdims=True), axis=0, keepdims=True)
        dx, dg = _rms_bwd_math(hv, gv, e * (1.0 / D))
        dh_ref[...] = dx

        @pl.when(pl.program_id(0) == 0)
        def _():
            loss_ref[...] = part
            dg_ref[...] = dg

        @pl.when(pl.program_id(0) > 0)
        def _():
            loss_ref[...] += part
            dg_ref[...] += dg

    row = pl.BlockSpec((tm, D), lambda i: (i, 0))
    vec = pl.BlockSpec((1, D), lambda i: (0, 0))
    return pl.pallas_call(
        body, name="loss_head", grid=(T // tm,), in_specs=[row, vec, row],
        out_specs=[pl.BlockSpec((1, 1), lambda i: (0, 0)), row, vec],
        out_shape=[jax.ShapeDtypeStruct((1, 1), f32), jax.ShapeDtypeStruct((T, D), f32), jax.ShapeDtypeStruct((1, D), f32)],
        compiler_params=_cp("arbitrary"),
    )(h, g, target)


def _shift_down(v, k):
    rows = lax.broadcasted_iota(jnp.int32, v.shape, 0)
    return jnp.where(rows < k, 0.0, pltpu.roll(v, k, 0))


def _shift_up(v, k):
    n = v.shape[0]
    rows = lax.broadcasted_iota(jnp.int32, v.shape, 0)
    return jnp.where(rows >= n - k, 0.0, pltpu.roll(v, n - k, 0))


def _ffn_mid_fwd(au, cw, cb, *, name):
    T = au.shape[0]
    F = au.shape[1] // 2
    nb = F // LANES

    def body(a_ref, u_ref, w_ref, b_ref, z_ref):
        a = a_ref[...]
        w = w_ref[...]
        ac = w[0:1] * _shift_down(a, 2) + w[1:2] * _shift_down(a, 1) + w[2:3] * a + b_ref[...]
        z_ref[...] = (ac * jax.nn.sigmoid(ac) * u_ref[...]).astype(bf16)

    return pl.pallas_call(
        body, name=name, grid=(nb,),
        in_specs=[pl.BlockSpec((T, LANES), lambda j: (0, j)), pl.BlockSpec((T, LANES), lambda j: (0, nb + j)),
                  pl.BlockSpec((3, LANES), lambda j: (0, j)), pl.BlockSpec((1, LANES), lambda j: (0, j))],
        out_specs=pl.BlockSpec((T, LANES), lambda j: (0, j)), out_shape=jax.ShapeDtypeStruct((T, F), bf16),
        compiler_params=_cp("parallel"),
    )(au, au, cw, cb)


def _ffn_mid_bwd(au, cw, cb, dz, *, name):
    T = au.shape[0]
    F = au.shape[1] // 2
    nb = F // LANES

    def body(a_ref, u_ref, w_ref, b_ref, dz_ref, da_ref, du_ref, dw_ref, db_ref):
        a = a_ref[...]
        w = w_ref[...]
        a2, a1 = _shift_down(a, 2), _shift_down(a, 1)
        ac = w[0:1] * a2 + w[1:2] * a1 + w[2:3] * a + b_ref[...]
        sg = jax.nn.sigmoid(ac)
        dz = dz_ref[...].astype(f32)
        du_ref[...] = (dz * ac * sg).astype(bf16)
        dac = dz * u_ref[...] * sg * (1.0 + ac * (1.0 - sg))
        da_ref[...] = (w[2:3] * dac + w[1:2] * _shift_up(dac, 1) + w[0:1] * _shift_up(dac, 2)).astype(bf16)
        rows = lax.broadcasted_iota(jnp.int32, (3, LANES), 0)
        s0 = jnp.sum(dac * a2, axis=0, keepdims=True)
        s1 = jnp.sum(dac * a1, axis=0, keepdims=True)
        s2 = jnp.sum(dac * a, axis=0, keepdims=True)
        dw_ref[...] = jnp.where(rows == 0, s0, jnp.where(rows == 1, s1, s2))
        db_ref[...] = jnp.sum(dac, axis=0, keepdims=True)

    col = lambda off: pl.BlockSpec((T, LANES), lambda j: (0, off + j))
    da, du, dw, db = pl.pallas_call(
        body, name=name, grid=(nb,),
        in_specs=[col(0), col(nb), pl.BlockSpec((3, LANES), lambda j: (0, j)), pl.BlockSpec((1, LANES), lambda j: (0, j)), col(0)],
        out_specs=[col(0), col(0), pl.BlockSpec((3, LANES), lambda j: (0, j)), pl.BlockSpec((1, LANES), lambda j: (0, j))],
        out_shape=[jax.ShapeDtypeStruct((T, F), bf16), jax.ShapeDtypeStruct((T, F), bf16),
                   jax.ShapeDtypeStruct((3, F), f32), jax.ShapeDtypeStruct((1, F), f32)],
        compiler_params=_cp("parallel"),
    )(au, au, cw, cb, dz)
    return jnp.concatenate([da, du], axis=1), dw, db


def _hgrn_lb(l):
    m = jnp.max(l, axis=0, keepdims=True)
    e = jnp.exp(l - m)
    return e[0:1] / jnp.sum(e, axis=0, keepdims=True)


def _hgrn_chunk(q, fx, lb):
    C = q.shape[0]
    sg = jax.nn.sigmoid(fx)
    F = lb + (1.0 - lb) * sg
    k = 1.0 - F
    logF = jnp.log(F)
    r = lax.broadcasted_iota(jnp.int32, (C, C), 0)
    c = lax.broadcasted_iota(jnp.int32, (C, C), 1)
    tril = (r >= c)
    b = _dot(tril.astype(f32), logF, precision=HI)
    bl = jnp.sum(logF, axis=0, keepdims=True)
    eb = jnp.exp(b)
    enb = jnp.exp(-b)
    elb = jnp.exp(bl - b)
    return dict(sg=sg, F=F, k=k, b=b, bl=bl, eb=eb, enb=enb, elb=elb, qd=q * eb, kd=k * enb, kl=k * elb, tril=tril)


def _hgrn_fwd(proj, lbl, ng, *, rb=512):
    T = proj.shape[0]
    rb = min(rb, T)
    cpb = rb // HGRN_CHUNK
    nblk = T // rb
    H = HGRN_HEADS

    def body(q_ref, f_ref, i_ref, g_ref, lbl_ref, ng_ref, y_ref, st_ref, S):
        @pl.when(pl.program_id(1) == 0)
        def _():
            S[...] = jnp.zeros_like(S)

        lb = _hgrn_lb(lbl_ref[...])
        ngv = ng_ref[...]
        for c in range(cpb):
            sl = pl.ds(c * HGRN_CHUNK, HGRN_CHUNK)
            v, gx = i_ref[sl, :], g_ref[sl, :]
            ch = _hgrn_chunk(q_ref[sl, :], f_ref[sl, :], lb)
            att = jnp.where(ch["tril"], _bdot(ch["qd"], ch["kd"], NT), 0.0)
            St = S[...]
            st_ref[0, c] = St
            o = _bdot(att, v) + _bdot(ch["qd"], St, NT)
            S[...] = St * jnp.exp(ch["bl"]) + _bdot(v, ch["kl"], TN)
            r = lax.rsqrt(jnp.mean(o * o, axis=-1, keepdims=True) + EPS)
            y_ref[sl, :] = (o * r * ngv * (gx * jax.nn.sigmoid(gx))).astype(bf16)

    col = lambda off: pl.BlockSpec((rb, LANES), lambda h, n: (n, off + h))
    return pl.pallas_call(
        body, name="hgrn_fwd", grid=(H, nblk),
        in_specs=[col(0), col(H), col(2 * H), col(3 * H), pl.BlockSpec((2, LANES), lambda h, n: (0, h)),
                  pl.BlockSpec((1, LANES), lambda h, n: (0, h))],
        out_specs=[pl.BlockSpec((rb, LANES), lambda h, n: (n, h)),
                   pl.BlockSpec((1, cpb, LANES, LANES), lambda h, n: (h, n, 0, 0))],
        out_shape=[jax.ShapeDtypeStruct((T, H * LANES), bf16),
                   jax.ShapeDtypeStruct((H, T // HGRN_CHUNK, LANES, LANES), f32)],
        scratch_shapes=[pltpu.VMEM((LANES, LANES), f32)], compiler_params=_cp("parallel", "arbitrary"),
    )(proj, proj, proj, proj, lbl, ng)


def _hgrn_bwd(proj, lbl, ng, states, dy, *, rb=512):
    T = proj.shape[0]
    rb = min(rb, T)
    cpb = rb // HGRN_CHUNK
    nblk = T // rb
    H = HGRN_HEADS
    C = HGRN_CHUNK

    def body(q_ref, f_ref, i_ref, g_ref, lbl_ref, ng_ref, st_ref, dy_ref,
             dq_ref, df_ref, di_ref, dg_ref, dl_ref, dng_ref, dS, dlb_acc, dng_acc):
        n = pl.program_id(1)

        @pl.when(n == 0)
        def _():
            dS[...] = jnp.zeros_like(dS)
            dlb_acc[...] = jnp.zeros_like(dlb_acc)
            dng_acc[...] = jnp.zeros_like(dng_acc)

        lb = _hgrn_lb(lbl_ref[...])
        ngv = ng_ref[...]
        r_i = lax.broadcasted_iota(jnp.int32, (C, C), 0)
        c_i = lax.broadcasted_iota(jnp.int32, (C, C), 1)
        triu = (c_i >= r_i).astype(f32)
        for c in reversed(range(cpb)):
            sl = pl.ds(c * C, C)
            q, v, gx = q_ref[sl, :], i_ref[sl, :], g_ref[sl, :]
            ch = _hgrn_chunk(q, f_ref[sl, :], lb)
            qd, kd, kl = ch["qd"], ch["kd"], ch["kl"]
            att = jnp.where(ch["tril"], _bdot(qd, kd, NT), 0.0)
            St = st_ref[0, c]
            o = _bdot(att, v) + _bdot(qd, St, NT)
            r = lax.rsqrt(jnp.mean(o * o, axis=-1, keepdims=True) + EPS)
            on = o * r
            sgg = jax.nn.sigmoid(gx)
            gate = gx * sgg
            dyv = dy_ref[sl, :].astype(f32)
            dg_ref[sl, :] = (dyv * on * ngv * sgg * (1.0 + gx * (1.0 - sgg))).astype(bf16)
            dng_acc[...] += jnp.sum(dyv * on * gate, axis=0, keepdims=True)
            don = dyv * ngv * gate
            do = r * (don - on * jnp.mean(don * on, axis=-1, keepdims=True))
            dSt = dS[...]
            dA = jnp.where(ch["tril"], _bdot(do, v, NT), 0.0)
            dv = _bdot(att, do, TN) + _bdot(kl, dSt, NT)
            dqd = _bdot(dA, kd) + _bdot(do, St)
            dkd = _bdot(dA, qd, TN)
            dkl = _bdot(v, dSt)
            dec = jnp.exp(ch["bl"])
            ddec = jnp.sum(St * dSt, axis=0, keepdims=True)
            dS[...] = _bdot(do, qd, TN) + dSt * dec
            dB = dqd * qd - dkd * kd - dkl * kl
            dbl = jnp.sum(dkl * kl, axis=0, keepdims=True) + ddec * dec
            dk = dkd * ch["enb"] + dkl * ch["elb"]
            dlogF = _dot(triu, dB, precision=HI) + dbl
            dF = dlogF / ch["F"] - dk
            sg = ch["sg"]
            dq_ref[sl, :] = (dqd * ch["eb"]).astype(bf16)
            di_ref[sl, :] = dv.astype(bf16)
            df_ref[sl, :] = (dF * (1.0 - lb) * sg * (1.0 - sg)).astype(bf16)
            dlb_acc[...] += jnp.sum(dF * (1.0 - sg), axis=0, keepdims=True)

        @pl.when(n == nblk - 1)
        def _():
            dl0 = dlb_acc[...] * lb * (1.0 - lb)
            rows = lax.broadcasted_iota(jnp.int32, (2, LANES), 0)
            dl_ref[...] = jnp.where(rows == 0, dl0, -dl0)
            dng_ref[...] = dng_acc[...]

    col = lambda off: pl.BlockSpec((rb, LANES), lambda h, n: (nblk - 1 - n, off + h))
    vec = lambda rows: pl.BlockSpec((rows, LANES), lambda h, n: (0, h))
    outc = pl.BlockSpec((rb, LANES), lambda h, n: (nblk - 1 - n, h))
    tok = jax.ShapeDtypeStruct((T, H * LANES), bf16)
    return pl.pallas_call(
        body, name="hgrn_bwd", grid=(H, nblk),
        in_specs=[col(0), col(H), col(2 * H), col(3 * H), vec(2), vec(1),
                  pl.BlockSpec((1, cpb, LANES, LANES), lambda h, n: (h, nblk - 1 - n, 0, 0)), col(0)],
        out_specs=[outc, outc, outc, outc, vec(2), vec(1)],
        out_shape=[tok, tok, tok, tok, jax.ShapeDtypeStruct((2, H * LANES), f32), jax.ShapeDtypeStruct((1, H * LANES), f32)],
        scratch_shapes=[pltpu.VMEM((LANES, LANES), f32), pltpu.VMEM((1, LANES), f32), pltpu.VMEM((1, LANES), f32)],
        compiler_params=_cp("parallel", "arbitrary"),
    )(proj, proj, proj, proj, lbl, ng, states, dy)


def _s5_disc_math(ar, ai, ldt, br, bi):
    dt = jnp.exp(ldt)
    mag = jnp.exp(ar * dt)
    abr, abi = mag * jnp.cos(ai * dt), mag * jnp.sin(ai * dt)
    den = ar * ar + ai * ai
    xr, xi = abr - 1.0, abi
    cr = (xr * ar + xi * ai) / den
    ci = (xi * ar - xr * ai) / den
    return abr, abi, cr * br - ci * bi, cr * bi + ci * br


def _s5_disc_fwd(ar, ai, ldt, br, bi):
    def body(ar_ref, ai_ref, ldt_ref, br_ref, bi_ref, o0, o1, o2, o3):
        outs = _s5_disc_math(ar_ref[...], ai_ref[...], ldt_ref[...], br_ref[...], bi_ref[...])
        for o, v in zip((o0, o1, o2, o3), outs):
            o[...] = v

    return pl.pallas_call(
        body, name="s5_disc_fwd",
        out_shape=[jax.ShapeDtypeStruct(ar.shape, f32)] * 2 + [jax.ShapeDtypeStruct(br.shape, f32)] * 2,
    )(ar, ai, ldt, br, bi)


def _s5_disc_bwd(ar, ai, ldt, br, bi, cts):
    def body(ar_ref, ai_ref, ldt_ref, br_ref, bi_ref, c0, c1, c2, c3, o0, o1, o2, o3, o4):
        _, vjp = jax.vjp(_s5_disc_math, ar_ref[...], ai_ref[...], ldt_ref[...], br_ref[...], bi_ref[...])
        for o, v in zip((o0, o1, o2, o3, o4), vjp((c0[...], c1[...], c2[...], c3[...]))):
            o[...] = v

    return pl.pallas_call(
        body, name="s5_disc_bwd",
        out_shape=[jax.ShapeDtypeStruct(ar.shape, f32)] * 3 + [jax.ShapeDtypeStruct(br.shape, f32)] * 2,
    )(ar, ai, ldt, br, bi, *cts)


S5_LC = 512
S5_NLC = S5_N // S5_LC
S5_UB = 4


def _cmul(ar, ai, xr, xi):
    return ar * xr - ai * xi, ar * xi + ai * xr


def _cpow(ar, ai, n):
    rr, ri = None, None
    br, bi = ar, ai
    while n:
        if n & 1:
            rr, ri = (br, bi) if rr is None else _cmul(rr, ri, br, bi)
        n >>= 1
        if n:
            br, bi = _cmul(br, bi, br, bi)
    return rr, ri


def _s5_bu(u_ref, bre_ref, bim_ref, xr, xi):
    for k in range(S5_UB):
        uk = u_ref[:, k * LANES:(k + 1) * LANES].astype(bf16)
        xr[:, k * S5_LC:(k + 1) * S5_LC] = _dot(uk, bre_ref[k])
        xi[:, k * S5_LC:(k + 1) * S5_LC] = _dot(uk, bim_ref[k])


def _s5_scan(xr, xi, sr, si, ar_ref, ai_ref, nsteps, store):
    for c in range(S5_NLC):
        cs = slice(c * S5_LC, (c + 1) * S5_LC)
        a_r = jnp.broadcast_to(ar_ref[:, cs], (S5_SEG, S5_LC))
        a_i = jnp.broadcast_to(ai_ref[:, cs], (S5_SEG, S5_LC))

        def step(j, carry, cs=cs, a_r=a_r, a_i=a_i):
            pr, pi = carry
            rows = pl.ds(pl.multiple_of(j * S5_SEG, S5_SEG), S5_SEG)
            nr = a_r * pr - a_i * pi + xr[rows, cs]
            ni = a_r * pi + a_i * pr + xi[rows, cs]
            if store:
                xr[rows, cs] = nr
                xi[rows, cs] = ni
            return nr, ni

        fr, fi = lax.fori_loop(0, nsteps, step, (sr[:, cs], si[:, cs]))
        sr[:, cs] = fr
        si[:, cs] = fi


def _s5_rscan(dr, di, xr, xi, s0r, s0i, gr, gi, acc_r, acc_i, ar_ref, ai_ref, nsteps):
    for c in range(S5_NLC):
        cs = slice(c * S5_LC, (c + 1) * S5_LC)
        a_r = jnp.broadcast_to(ar_ref[:, cs], (S5_SEG, S5_LC))
        a_i = jnp.broadcast_to(ai_ref[:, cs], (S5_SEG, S5_LC))

        def step(jj, carry, cs=cs, a_r=a_r, a_i=a_i):
            pr, pi, cr, ci = carry
            j = nsteps - 1 - jj
            rows = pl.ds(pl.multiple_of(j * S5_SEG, S5_SEG), S5_SEG)
            nr = dr[rows, cs] + a_r * pr + a_i * pi
            ni = di[rows, cs] + a_r * pi - a_i * pr
            dr[rows, cs] = nr
            di[rows, cs] = ni
            if acc_r is not None:
                prev = pl.ds(pl.multiple_of(jnp.maximum(j - 1, 0) * S5_SEG, S5_SEG), S5_SEG)
                first = j == 0
                pr_s = jnp.where(first, s0r[:, cs], xr[prev, cs])
                pi_s = jnp.where(first, s0i[:, cs], xi[prev, cs])
                cr = cr + nr * pr_s + ni * pi_s
                ci = ci - nr * pi_s + ni * pr_s
            return nr, ni, cr, ci

        z = jnp.zeros((S5_SEG, S5_LC), f32)
        init = (gr[:, cs], gi[:, cs], z, z)
        fr, fi, cr, ci = lax.fori_loop(0, nsteps, step, init)
        gr[:, cs] = fr
        gi[:, cs] = fi
        if acc_r is not None:
            acc_r[:, cs] += cr
            acc_i[:, cs] += ci


def _s5_seg_carry(fr, fi, ar, ai, seg_len, reverse):
    pr, pi = _cpow(ar, ai if not reverse else -ai, seg_len)
    rows = lax.broadcasted_iota(jnp.int32, fr.shape, 0)
    cr, ci = jnp.zeros_like(fr), jnp.zeros_like(fi)
    sh = (S5_SEG - 1) if reverse else 1
    fr_s, fi_s = pltpu.roll(fr, sh, 0), pltpu.roll(fi, sh, 0)
    order = range(S5_SEG - 2, -1, -1) if reverse else range(1, S5_SEG)
    for r in order:
        c_r, c_i = pltpu.roll(cr, sh, 0), pltpu.roll(ci, sh, 0)
        m_r, m_i = _cmul(pr, pi, c_r, c_i)
        cr = jnp.where(rows == r, m_r + fr_s, cr)
        ci = jnp.where(rows == r, m_i + fi_s, ci)
    return cr, ci


def _gelu_parts(y):
    c0 = math.sqrt(2.0 / math.pi)
    t = jnp.tanh(c0 * (y + 0.044715 * y * y * y))
    z = 0.5 * y * (1.0 + t)
    dz = 0.5 * (1.0 + t) + 0.5 * y * (1.0 - t * t) * c0 * (1.0 + 3.0 * 0.044715 * y * y)
    return z, dz


def _s5_y(xr, xi, u_ref, cre_ref, cim_ref, d_ref):
    ys = []
    for k in range(S5_UB):
        cs = slice(k * S5_LC, (k + 1) * S5_LC)
        ys.append(_bdot(xr[:, cs], cre_ref[k]) - _bdot(xi[:, cs], cim_ref[k]))
    return jnp.concatenate(ys, axis=1) + d_ref[...] * u_ref[...]


def _s5_specs(T, rb, rev=False):
    nblk = T // rb
    blk = (lambda i: (nblk - 1 - i, 0)) if rev else (lambda i: (i, 0))
    tok = pl.BlockSpec((rb, 4 * LANES), blk)
    bmat = pl.BlockSpec((S5_UB, LANES, S5_LC), lambda i: (0, 0, 0))
    cmat = pl.BlockSpec((S5_UB, S5_LC, LANES), lambda i: (0, 0, 0))
    avec = pl.BlockSpec((1, S5_N), lambda i: (0, 0))
    seg = pl.BlockSpec((S5_SEG, S5_N), lambda i: (0, 0))
    cvec = pl.BlockSpec((1, 4 * LANES), lambda i: (0, 0))
    s0 = pl.BlockSpec((1, S5_SEG, S5_N), (lambda i: (nblk - 1 - i, 0, 0)) if rev else (lambda i: (i, 0, 0)))
    return dict(tok=tok, bmat=bmat, cmat=cmat, avec=avec, seg=seg, cvec=cvec, s0=s0, nblk=nblk)


def _s5_final(u, bre, bim, ar, ai, *, rb):
    T = u.shape[0]
    sp = _s5_specs(T, rb)

    def body(u_ref, bre_ref, bim_ref, ar_ref, ai_ref, fr_ref, fi_ref, xr, xi):
        @pl.when(pl.program_id(0) == 0)
        def _():
            fr_ref[...] = jnp.zeros_like(fr_ref)
            fi_ref[...] = jnp.zeros_like(fi_ref)

        _s5_bu(u_ref, bre_ref, bim_ref, xr, xi)
        _s5_scan(xr, xi, fr_ref, fi_ref, ar_ref, ai_ref, rb // S5_SEG, False)

    return pl.pallas_call(
        body, name="s5_final", grid=(sp["nblk"],),
        in_specs=[sp["tok"], sp["bmat"], sp["bmat"], sp["avec"], sp["avec"]], out_specs=[sp["seg"], sp["seg"]],
        out_shape=[jax.ShapeDtypeStruct((S5_SEG, S5_N), f32)] * 2,
        scratch_shapes=[pltpu.VMEM((rb, S5_N), f32)] * 2, compiler_params=_cp("arbitrary"),
    )(u, bre, bim, ar, ai)


def _s5_fwd(u, bre, bim, ar, ai, fr, fi, cre, cim, dsk, wg, bg, *, rb):
    T = u.shape[0]
    sp = _s5_specs(T, rb)
    seg_len = T // S5_SEG

    def body(u_ref, bre_ref, bim_ref, ar_ref, ai_ref, fr_ref, fi_ref, cre_ref, cim_ref, d_ref, wg_ref, bg_ref,
             o_ref, s0r_ref, s0i_ref, xr, xi, sr, si):
        @pl.when(pl.program_id(0) == 0)
        def _():
            i_r, i_i = _s5_seg_carry(fr_ref[...], fi_ref[...], ar_ref[...], ai_ref[...], seg_len, False)
            sr[...] = i_r
            si[...] = i_i

        s0r_ref[0] = sr[...]
        s0i_ref[0] = si[...]
        _s5_bu(u_ref, bre_ref, bim_ref, xr, xi)
        _s5_scan(xr, xi, sr, si, ar_ref, ai_ref, rb // S5_SEG, True)
        y = _s5_y(xr, xi, u_ref, cre_ref, cim_ref, d_ref)
        z, _ = _gelu_parts(y)
        v = _bdot(z, wg_ref[...]) + bg_ref[...]
        o_ref[...] = (z * jax.nn.sigmoid(v)).astype(bf16)

    wspec = pl.BlockSpec((4 * LANES, 4 * LANES), lambda i: (0, 0))
    return pl.pallas_call(
        body, name="s5_fwd", grid=(sp["nblk"],),
        in_specs=[sp["tok"], sp["bmat"], sp["bmat"], sp["avec"], sp["avec"], sp["seg"], sp["seg"], sp["cmat"], sp["cmat"],
                  sp["cvec"], wspec, sp["cvec"]],
        out_specs=[sp["tok"], sp["s0"], sp["s0"]],
        out_shape=[jax.ShapeDtypeStruct((T, 4 * LANES), bf16)] + [jax.ShapeDtypeStruct((sp["nblk"], S5_SEG, S5_N), f32)] * 2,
        scratch_shapes=[pltpu.VMEM((rb, S5_N), f32)] * 2 + [pltpu.VMEM((S5_SEG, S5_N), f32)] * 2,
        compiler_params=_cp("arbitrary"),
    )(u, bre, bim, ar, ai, fr, fi, cre, cim, dsk, wg, bg)


def _s5_bwd_a(u, bre, bim, ar, ai, s0r, s0i, cre, cim, cret, cimt, dsk, wg, bg, dout, *, rb):
    T = u.shape[0]
    sp = _s5_specs(T, rb, rev=True)

    def body(u_ref, bre_ref, bim_ref, ar_ref, ai_ref, s0r_ref, s0i_ref, cre_ref, cim_ref, cret_ref, cimt_ref,
             d_ref, wg_ref, bg_ref, do_ref, dy_ref, glr_ref, gli_ref, dcre_ref, dcim_ref, dd_ref, dwg_ref, dbg_ref,
             xr, xi, dr, di, sr, si):
        @pl.when(pl.program_id(0) == 0)
        def _():
            for r in (glr_ref, gli_ref, dcre_ref, dcim_ref, dd_ref, dwg_ref, dbg_ref):
                r[...] = jnp.zeros_like(r)

        sr[...] = s0r_ref[0]
        si[...] = s0i_ref[0]
        _s5_bu(u_ref, bre_ref, bim_ref, xr, xi)
        _s5_scan(xr, xi, sr, si, ar_ref, ai_ref, rb // S5_SEG, True)
        uv = u_ref[...]
        y = _s5_y(xr, xi, u_ref, cre_ref, cim_ref, d_ref)
        z, gz = _gelu_parts(y)
        v = _bdot(z, wg_ref[...]) + bg_ref[...]
        sg = jax.nn.sigmoid(v)
        dov = do_ref[...].astype(f32)
        dv = dov * z * sg * (1.0 - sg)
        dz = dov * sg + _bdot(dv, wg_ref[...], NT)
        dy = dz * gz
        dy_ref[...] = dy
        dwg_ref[...] += _bdot(z, dv, TN)
        dbg_ref[...] += jnp.sum(dv, axis=0, keepdims=True)
        dd_ref[...] += jnp.sum(dy * uv, axis=0, keepdims=True)
        for k in range(S5_UB):
            cs = slice(k * S5_LC, (k + 1) * S5_LC)
            dyk = dy[:, k * LANES:(k + 1) * LANES]
            dcre_ref[k] += _bdot(xr[:, cs], dyk, TN)
            dcim_ref[k] -= _bdot(xi[:, cs], dyk, TN)
            dr[:, cs] = _bdot(dyk, cret_ref[k])
            di[:, cs] = -_bdot(dyk, cimt_ref[k])
        _s5_rscan(dr, di, None, None, None, None, glr_ref, gli_ref, None, None, ar_ref, ai_ref, rb // S5_SEG)

    wspec = pl.BlockSpec((4 * LANES, 4 * LANES), lambda i: (0, 0))
    return pl.pallas_call(
        body, name="s5_bwd_a", grid=(sp["nblk"],),
        in_specs=[sp["tok"], sp["bmat"], sp["bmat"], sp["avec"], sp["avec"], sp["s0"], sp["s0"], sp["cmat"], sp["cmat"],
                  sp["bmat"], sp["bmat"], sp["cvec"], wspec, sp["cvec"], sp["tok"]],
        out_specs=[sp["tok"], sp["seg"], sp["seg"], sp["cmat"], sp["cmat"], sp["cvec"], wspec, sp["cvec"]],
        out_shape=[jax.ShapeDtypeStruct((T, 4 * LANES), f32)] + [jax.ShapeDtypeStruct((S5_SEG, S5_N), f32)] * 2
        + [jax.ShapeDtypeStruct((S5_UB, S5_LC, LANES), f32)] * 2
        + [jax.ShapeDtypeStruct((1, 4 * LANES), f32), jax.ShapeDtypeStruct((4 * LANES, 4 * LANES), f32),
           jax.ShapeDtypeStruct((1, 4 * LANES), f32)],
        scratch_shapes=[pltpu.VMEM((rb, S5_N), f32)] * 4 + [pltpu.VMEM((S5_SEG, S5_N), f32)] * 2,
        compiler_params=_cp("arbitrary"),
    )(u, bre, bim, ar, ai, s0r, s0i, cre, cim, cret, cimt, dsk, wg, bg, dout)


def _s5_bwd_b(u, bre, bim, bret, bimt, ar, ai, s0r, s0i, glr, gli, cret, cimt, dsk, dy, *, rb):
    T = u.shape[0]
    sp = _s5_specs(T, rb, rev=True)
    seg_len = T // S5_SEG
    nblk = sp["nblk"]

    def body(u_ref, bre_ref, bim_ref, bret_ref, bimt_ref, ar_ref, ai_ref, s0r_ref, s0i_ref, glr_ref, gli_ref,
             cret_ref, cimt_ref, d_ref, dy_ref, du_ref, dbre_ref, dbim_ref, dar_ref, dai_ref,
             xr, xi, dr, di, sr, si, gr, gi, acc_r, acc_i):
        @pl.when(pl.program_id(0) == 0)
        def _():
            x_r, x_i = _s5_seg_carry(glr_ref[...], gli_ref[...], ar_ref[...], ai_ref[...], seg_len, True)
            gr[...] = x_r
            gi[...] = x_i
            acc_r[...] = jnp.zeros_like(acc_r)
            acc_i[...] = jnp.zeros_like(acc_i)
            dbre_ref[...] = jnp.zeros_like(dbre_ref)
            dbim_ref[...] = jnp.zeros_like(dbim_ref)

        sr[...] = s0r_ref[0]
        si[...] = s0i_ref[0]
        _s5_bu(u_ref, bre_ref, bim_ref, xr, xi)
        _s5_scan(xr, xi, sr, si, ar_ref, ai_ref, rb // S5_SEG, True)
        dy = dy_ref[...]
        for k in range(S5_UB):
            cs = slice(k * S5_LC, (k + 1) * S5_LC)
            dyk = dy[:, k * LANES:(k + 1) * LANES]
            dr[:, cs] = _bdot(dyk, cret_ref[k])
            di[:, cs] = -_bdot(dyk, cimt_ref[k])
        sr[...] = s0r_ref[0]
        si[...] = s0i_ref[0]
        _s5_rscan(dr, di, xr, xi, sr, si, gr, gi, acc_r, acc_i, ar_ref, ai_ref, rb // S5_SEG)
        dus = []
        for k in range(S5_UB):
            cs = slice(k * S5_LC, (k + 1) * S5_LC)
            uk = u_ref[:, k * LANES:(k + 1) * LANES]
            dbre_ref[k] += _bdot(uk, dr[:, cs], TN)
            dbim_ref[k] += _bdot(uk, di[:, cs], TN)
            dus.append(_bdot(dr[:, cs], bret_ref[k]) + _bdot(di[:, cs], bimt_ref[k]))
        du_ref[...] = (jnp.concatenate(dus, axis=1) + d_ref[...] * dy).astype(bf16)

        @pl.when(pl.program_id(0) == nblk - 1)
        def _():
            dar_ref[...] = jnp.sum(acc_r[...], axis=0, keepdims=True)
            dai_ref[...] = jnp.sum(acc_i[...], axis=0, keepdims=True)

    return pl.pallas_call(
        body, name="s5_bwd_b", grid=(nblk,),
        in_specs=[sp["tok"], sp["bmat"], sp["bmat"], sp["cmat"], sp["cmat"], sp["avec"], sp["avec"], sp["s0"], sp["s0"],
                  sp["seg"], sp["seg"], sp["bmat"], sp["bmat"], sp["cvec"], sp["tok"]],
        out_specs=[sp["tok"], sp["bmat"], sp["bmat"], sp["avec"], sp["avec"]],
        out_shape=[jax.ShapeDtypeStruct((T, 4 * LANES), bf16)] + [jax.ShapeDtypeStruct((S5_UB, LANES, S5_LC), f32)] * 2
        + [jax.ShapeDtypeStruct((1, S5_N), f32)] * 2,
        scratch_shapes=[pltpu.VMEM((rb, S5_N), f32)] * 4 + [pltpu.VMEM((S5_SEG, S5_N), f32)] * 6,
        compiler_params=_cp("arbitrary"),
    )(u, bre, bim, bret, bimt, ar, ai, s0r, s0i, glr, gli, cret, cimt, dsk, dy)


def _blockdiag(w, transpose=False):
    if transpose:
        w = jnp.swapaxes(w, 1, 2)
    g, a, b = w.shape
    eye = jnp.eye(8, dtype=w.dtype)
    return jnp.einsum("kgab,gj->kgajb", w.reshape(4, 8, a, b), eye).reshape(4, 8 * a, 8 * b)


def _blockdiag_t(m, a, b):
    eye = jnp.eye(8, dtype=m.dtype)
    return jnp.einsum("kgajb,gj->kgab", m.reshape(4, 8, a, 8, b), eye).reshape(32, a, b)


ROT = MLA_ROPE // 2


def _rope_tables(positions):
    freqs = ROPE_THETA ** (-jnp.arange(0, MLA_ROPE, 2, dtype=f32) / MLA_ROPE)
    ang = positions.astype(f32)[:, None] * freqs
    cos, sin, z = jnp.cos(ang), jnp.sin(ang), jnp.zeros_like(ang)
    return (jnp.concatenate([cos, cos, z, z], axis=1), jnp.concatenate([-sin, z, z, z], axis=1),
            jnp.concatenate([z, sin, z, z], axis=1))


def _rot(x, c, sa, sb):
    return x * c + pltpu.roll(x, LANES - ROT, 1) * sa + pltpu.roll(x, ROT, 1) * sb


def _rot_t(dy, c, sa, sb):
    return dy * c + pltpu.roll(dy * sa, ROT, 1) + pltpu.roll(dy * sb, LANES - ROT, 1)


def _rms(xv, g):
    return xv * lax.rsqrt(jnp.mean(xv * xv, axis=-1, keepdims=True) + EPS) * g


QW, KVW = MLA_Q_RANK, MLA_KV_RANK
ODD_PAD = QW + KVW + LANES


def _mla_prep_fwd(proj, qg, kvg, tabs, *, tm=512):
    T = proj.shape[0]
    tm = _tile(T, tm)

    def body(p_ref, qg_ref, kvg_ref, c_ref, sa_ref, sb_ref, cq_ref, ckv_ref, kr_ref):
        cq_ref[...] = _rms(p_ref[:, :QW], qg_ref[...]).astype(bf16)
        ckv_ref[...] = _rms(p_ref[:, QW:QW + KVW], kvg_ref[...]).astype(bf16)
        kr_ref[...] = _rot(p_ref[:, QW + KVW:], c_ref[...], sa_ref[...], sb_ref[...]).astype(bf16)

    row = lambda w: pl.BlockSpec((tm, w), lambda i: (i, 0))
    vec = lambda w: pl.BlockSpec((1, w), lambda i: (0, 0))
    return pl.pallas_call(
        body, name="mla_prep_fwd", grid=(T // tm,),
        in_specs=[row(ODD_PAD), vec(QW), vec(KVW), row(LANES), row(LANES), row(LANES)],
        out_specs=[row(QW), row(KVW), row(LANES)],
        out_shape=[jax.ShapeDtypeStruct((T, QW), bf16), jax.ShapeDtypeStruct((T, KVW), bf16),
                   jax.ShapeDtypeStruct((T, LANES), bf16)],
        compiler_params=_cp("parallel"),
    )(proj, qg, kvg, *tabs)


def _mla_prep_bwd(proj, qg, kvg, tabs, dcqn, dckvn, dkr_heads, *, tm=512):
    T = proj.shape[0]
    tm = _tile(T, tm)

    def body(p_ref, qg_ref, kvg_ref, c_ref, sa_ref, sb_ref, dcq_ref, dckv_ref, dkr_ref, dp_ref, dqg_ref, dkvg_ref):
        dcq, dqg = _rms_bwd_math(p_ref[:, :QW], qg_ref[...], dcq_ref[...])
        dckv, dkvg = _rms_bwd_math(p_ref[:, QW:QW + KVW], kvg_ref[...], dckv_ref[...])
        dk = dkr_ref[:, :LANES]
        for h in range(1, MLA_HEADS):
            dk = dk + dkr_ref[:, h * LANES:(h + 1) * LANES]
        dkr = _rot_t(dk, c_ref[...], sa_ref[...], sb_ref[...])
        dp_ref[...] = jnp.concatenate([dcq, dckv, dkr], axis=1).astype(bf16)

        @pl.when(pl.program_id(0) == 0)
        def _():
            dqg_ref[...] = dqg
            dkvg_ref[...] = dkvg

        @pl.when(pl.program_id(0) > 0)
        def _():
            dqg_ref[...] += dqg
            dkvg_ref[...] += dkvg

    row = lambda w: pl.BlockSpec((tm, w), lambda i: (i, 0))
    vec = lambda w: pl.BlockSpec((1, w), lambda i: (0, 0))
    return pl.pallas_call(
        body, name="mla_prep_bwd", grid=(T // tm,),
        in_specs=[row(ODD_PAD), vec(QW), vec(KVW), row(LANES), row(LANES), row(LANES), row(QW), row(KVW),
                  row(MLA_HEADS * LANES)],
        out_specs=[row(ODD_PAD), vec(QW), vec(KVW)],
        out_shape=[jax.ShapeDtypeStruct((T, ODD_PAD), bf16), jax.ShapeDtypeStruct((1, QW), f32),
                   jax.ShapeDtypeStruct((1, KVW), f32)],
        compiler_params=_cp("arbitrary"),
    )(proj, qg, kvg, *tabs, dcqn, dckvn, dkr_heads)


HQ = 2 * LANES
QK_SCALE = MLA_QK ** -0.5


def _q_post(q, tabs, *, transpose, name, tm=512):
    T = q.shape[0]
    tm = _tile(T, tm)

    def body(q_ref, c_ref, sa_ref, sb_ref, o_ref):
        qv = q_ref[...].astype(f32)
        rope = (_rot_t if transpose else _rot)(qv[:, LANES:], c_ref[...], sa_ref[...], sb_ref[...])
        o_ref[...] = (jnp.concatenate([qv[:, :LANES], rope], axis=1) * QK_SCALE).astype(bf16)

    tab = pl.BlockSpec((tm, LANES), lambda i, h: (i, 0))
    blk = pl.BlockSpec((tm, HQ), lambda i, h: (i, h))
    return pl.pallas_call(
        body, name=name, grid=(T // tm, MLA_HEADS), in_specs=[blk, tab, tab, tab], out_specs=blk,
        out_shape=jax.ShapeDtypeStruct(q.shape, bf16), compiler_params=_cp("parallel", "parallel"),
    )(q, *tabs)


def _causal_mask(i, j, tq, tk):
    r = lax.broadcasted_iota(jnp.int32, (tq, tk), 0) + i * tq
    c = lax.broadcasted_iota(jnp.int32, (tq, tk), 1) + j * tk
    return c <= r


def _flash_fwd(q, kv, kr, *, tb=512):
    T = q.shape[0]
    tb = _tile(T, tb)
    nb = T // tb
    H = MLA_HEADS

    def body(q_ref, kn_ref, v_ref, kr_ref, o_ref, lse_ref, m_s, l_s, acc):
        i, j = pl.program_id(1), pl.program_id(2)

        @pl.when(j == 0)
        def _():
            m_s[...] = jnp.full_like(m_s, -jnp.inf)
            l_s[...] = jnp.zeros_like(l_s)
            acc[...] = jnp.zeros_like(acc)

        @pl.when(j <= i)
        def _():
            k = jnp.concatenate([kn_ref[...], kr_ref[...]], axis=1)
            s = _dot(q_ref[...], k, NT)
            s = jnp.where(_causal_mask(i, j, tb, tb), s, -jnp.inf)
            m_new = jnp.maximum(m_s[...], jnp.max(s, axis=-1, keepdims=True))
            alpha = jnp.exp(m_s[...] - m_new)
            p = jnp.exp(s - m_new)
            l_s[...] = alpha * l_s[...] + jnp.sum(p, axis=-1, keepdims=True)
            acc[...] = alpha * acc[...] + _dot(p.astype(bf16), v_ref[...])
            m_s[...] = m_new

        @pl.when(j == i)
        def _():
            o_ref[...] = (acc[...] / l_s[...]).astype(bf16)
            lse_ref[0] = m_s[...] + jnp.log(l_s[...])

    kblk = lambda off: pl.BlockSpec((tb, LANES), lambda h, i, j: (jnp.minimum(j, i), 2 * h + off))
    return pl.pallas_call(
        body, name="flash_fwd", grid=(H, nb, nb),
        in_specs=[pl.BlockSpec((tb, HQ), lambda h, i, j: (i, h)), kblk(0), kblk(1),
                  pl.BlockSpec((tb, LANES), lambda h, i, j: (jnp.minimum(j, i), 0))],
        out_specs=[pl.BlockSpec((tb, LANES), lambda h, i, j: (i, h)), pl.BlockSpec((1, tb, 1), lambda h, i, j: (h, i, 0))],
        out_shape=[jax.ShapeDtypeStruct((T, H * LANES), bf16), jax.ShapeDtypeStruct((H, T, 1), f32)],
        scratch_shapes=[pltpu.VMEM((tb, 1), f32), pltpu.VMEM((tb, 1), f32), pltpu.VMEM((tb, LANES), f32)],
        compiler_params=_cp("parallel", "parallel", "arbitrary"),
    )(q, kv, kv, kr)


def _attn_delta(o, do, *, tm=512):
    T = o.shape[0]
    tm = _tile(T, tm)

    def body(o_ref, do_ref, d_ref):
        d_ref[0] = jnp.sum(o_ref[...].astype(f32) * do_ref[...], axis=-1, keepdims=True)

    blk = pl.BlockSpec((tm, LANES), lambda h, i: (i, h))
    return pl.pallas_call(
        body, name="attn_delta", grid=(MLA_HEADS, T // tm), in_specs=[blk, blk],
        out_specs=pl.BlockSpec((1, tm, 1), lambda h, i: (h, i, 0)),
        out_shape=jax.ShapeDtypeStruct((MLA_HEADS, T, 1), f32), compiler_params=_cp("parallel", "parallel"),
    )(o, do)


def _flash_p_ds(q_ref, kn_ref, v_ref, kr_ref, do_ref, lse_ref, dl_ref, i, j, tb):
    k = jnp.concatenate([kn_ref[...], kr_ref[...]], axis=1)
    s = _dot(q_ref[...], k, NT)
    p = jnp.where(_causal_mask(i, j, tb, tb), jnp.exp(s - lse_ref[0]), 0.0)
    dp = _bdot(do_ref[...], v_ref[...], NT)
    ds = p * (dp - dl_ref[0])
    return k, p, ds


def _flash_bwd_kv(q, kv, kr, do, lse, delta, *, tb=512):
    T = q.shape[0]
    tb = _tile(T, tb)
    nb = T // tb
    H = MLA_HEADS

    def body(q_ref, kn_ref, v_ref, kr_ref, do_ref, lse_ref, dl_ref, dkv_ref, dkr_ref, dk_acc, dv_acc):
        j, ii = pl.program_id(1), pl.program_id(2)
        i = jnp.maximum(ii, j)

        @pl.when(ii == 0)
        def _():
            dk_acc[...] = jnp.zeros_like(dk_acc)
            dv_acc[...] = jnp.zeros_like(dv_acc)

        @pl.when(ii >= j)
        def _():
            _, p, ds = _flash_p_ds(q_ref, kn_ref, v_ref, kr_ref, do_ref, lse_ref, dl_ref, i, j, tb)
            dv_acc[...] += _bdot(p, do_ref[...], TN)
            dk_acc[...] += _bdot(ds, q_ref[...], TN)

        @pl.when(ii == nb - 1)
        def _():
            dkv_ref[...] = jnp.concatenate([dk_acc[:, :LANES], dv_acc[...]], axis=1).astype(bf16)
            dkr_ref[...] = dk_acc[:, LANES:]

    qi = lambda h, j, i: jnp.maximum(i, j)
    kblk = lambda off: pl.BlockSpec((tb, LANES), lambda h, j, i: (j, 2 * h + off))
    vec = pl.BlockSpec((1, tb, 1), lambda h, j, i: (h, qi(h, j, i), 0))
    return pl.pallas_call(
        body, name="flash_bwd_kv", grid=(H, nb, nb),
        in_specs=[pl.BlockSpec((tb, HQ), lambda h, j, i: (qi(h, j, i), h)), kblk(0), kblk(1),
                  pl.BlockSpec((tb, LANES), lambda h, j, i: (j, 0)),
                  pl.BlockSpec((tb, LANES), lambda h, j, i: (qi(h, j, i), h)), vec, vec],
        out_specs=[pl.BlockSpec((tb, HQ), lambda h, j, i: (j, h)), pl.BlockSpec((tb, LANES), lambda h, j, i: (j, h))],
        out_shape=[jax.ShapeDtypeStruct((T, H * HQ), bf16), jax.ShapeDtypeStruct((T, H * LANES), f32)],
        scratch_shapes=[pltpu.VMEM((tb, HQ), f32), pltpu.VMEM((tb, LANES), f32)],
        compiler_params=_cp("parallel", "parallel", "arbitrary"),
    )(q, kv, kv, kr, do, lse, delta)


def _flash_bwd_q(q, kv, kr, do, lse, delta, *, tb=512):
    T = q.shape[0]
    tb = _tile(T, tb)
    nb = T // tb
    H = MLA_HEADS

    def body(q_ref, kn_ref, v_ref, kr_ref, do_ref, lse_ref, dl_ref, dq_ref, acc):
        i, j = pl.program_id(1), pl.program_id(2)

        @pl.when(j == 0)
        def _():
            acc[...] = jnp.zeros_like(acc)

        @pl.when(j <= i)
        def _():
            k, _, ds = _flash_p_ds(q_ref, kn_ref, v_ref, kr_ref, do_ref, lse_ref, dl_ref, i, j, tb)
            acc[...] += _bdot(ds, k)

        @pl.when(j == i)
        def _():
            dq_ref[...] = acc[...]

    kj = lambda h, i, j: jnp.minimum(j, i)
    kblk = lambda off: pl.BlockSpec((tb, LANES), lambda h, i, j: (kj(h, i, j), 2 * h + off))
    vec = pl.BlockSpec((1, tb, 1), lambda h, i, j: (h, i, 0))
    return pl.pallas_call(
        body, name="flash_bwd_q", grid=(H, nb, nb),
        in_specs=[pl.BlockSpec((tb, HQ), lambda h, i, j: (i, h)), kblk(0), kblk(1),
                  pl.BlockSpec((tb, LANES), lambda h, i, j: (kj(h, i, j), 0)),
                  pl.BlockSpec((tb, LANES), lambda h, i, j: (i, h)), vec, vec],
        out_specs=pl.BlockSpec((tb, HQ), lambda h, i, j: (i, h)),
        out_shape=jax.ShapeDtypeStruct((T, H * HQ), f32),
        scratch_shapes=[pltpu.VMEM((tb, HQ), f32)], compiler_params=_cp("parallel", "parallel", "arbitrary"),
    )(q, kv, kv, kr, do, lse, delta)


HBM_SPEC = pl.BlockSpec(memory_space=pltpu.HBM)
N_CHIPS = 4
N_DEV = 8

BIG = {"even_w_in": 1, "s5_w_glu": 0, "even_w_out": 0, "odd_w_in": 0, "mla_w_uq": 1, "mla_w_ukv": 1, "odd_w_out": 0,
       "ffn_w_in": 2, "ffn_w_out": 1}
LAYERED = ("ffn_w_in", "ffn_w_out")


def _place():
    x, y, c = lax.axis_index("x"), lax.axis_index("y"), lax.axis_index("c")
    chips = [(1 - x, y), (x, 1 - y), (1 - x, 1 - y)]
    return x, y, c, chips


def _slab(ref, axis, k, size):
    start = pl.multiple_of(k * size, size if axis == 0 else LANES)
    idx = [slice(None)] * len(ref.shape)
    idx[axis] = pl.ds(start, size)
    return ref.at[tuple(idx)]


def _gather_weights(shards, small):
    names = list(shards)
    jobs = []
    for n, name in enumerate(names):
        s = shards[name]
        if name in LAYERED:
            for l in range(s.shape[0]):
                ax = BIG[name] - 1
                full = tuple(d * (N_CHIPS if a == ax else 1) for a, d in enumerate(s.shape[1:]))
                jobs.append((n, l, ax, s.shape[1:][ax], full))
        else:
            ax = BIG[name]
            full = tuple(d * (N_CHIPS if a == ax else 1) for a, d in enumerate(s.shape))
            jobs.append((n, None, ax, s.shape[ax], full))
    nj = len(jobs)

    def body(*refs):
        ins, small_ref = refs[:len(names)], refs[len(names)]
        outs, small_out = refs[len(names) + 1:len(names) + 1 + nj], refs[len(names) + 1 + nj]
        send, recv, loc = refs[-3:]
        x, y, c, chips = _place()
        k = 2 * x + y
        local, pushed = [], []

        def src_of(a):
            if a == nj:
                return small_ref
            n, l = jobs[a][0], jobs[a][1]
            return ins[n] if l is None else ins[n].at[l]

        def dst_of(a, kk):
            return small_out.at[kk] if a == nj else _slab(outs[a], jobs[a][2], kk, jobs[a][3])

        def push(a, j, kk, chip):
            return pltpu.make_async_remote_copy(src_ref=src_of(a), dst_ref=dst_of(a, kk), send_sem=send.at[a, j],
                                                recv_sem=recv.at[a, j], device_id=(*chip, c), device_id_type=MESH)

        for a in range(nj + 1):
            local.append(pltpu.make_async_copy(src_of(a), dst_of(a, k), loc.at[a]))
            local[-1].start()
            for j, chip in enumerate(chips):
                pushed.append(push(a, j, k, chip))
                pushed[-1].start()
        for a in range(nj + 1):
            for j, (cx, cy) in enumerate(chips):
                push(a, j, 2 * cx + cy, (cx, cy)).wait_recv()
        for cp in pushed:
            cp.wait_send()
        for cp in local:
            cp.wait()

    out_shape = [jax.ShapeDtypeStruct(j[4], shards[names[j[0]]].dtype) for j in jobs]
    out_shape.append(jax.ShapeDtypeStruct((N_CHIPS,) + small.shape, small.dtype))
    res = pl.pallas_call(
        body, name="gather_weights", in_specs=[HBM_SPEC] * (len(names) + 1), out_specs=[HBM_SPEC] * (nj + 1),
        out_shape=out_shape,
        scratch_shapes=[pltpu.SemaphoreType.DMA((nj + 1, 3)), pltpu.SemaphoreType.DMA((nj + 1, 3)),
                        pltpu.SemaphoreType.DMA((nj + 1,))],
    )(*[shards[n] for n in names], small)
    full = {}
    for a, (n, l, _, _, _) in enumerate(jobs):
        name = names[n]
        full[name] = res[a] if l is None else full.get(name, ()) + (res[a],)
    return full, res[nj]


def _scatter_grads(grads, small):
    names = list(grads)
    jobs = []
    ins = []
    for name in names:
        g = grads[name]
        ax = BIG[name] - (1 if name in LAYERED else 0)
        for l, gl in enumerate(g if name in LAYERED else (g,)):
            blk = tuple(d // (N_CHIPS if a == ax else 1) for a, d in enumerate(gl.shape))
            jobs.append((len(ins), l if name in LAYERED else None, ax, blk[ax], blk, name))
            ins.append(gl)
    nj = len(jobs)
    out_names = list(dict.fromkeys(j[5] for j in jobs))
    masks = list(range(1, N_DEV))

    def body(*refs):
        in_refs, small_ref = refs[:nj], refs[nj]
        outs = dict(zip(out_names, refs[nj + 1:nj + 1 + len(out_names)]))
        small_out = refs[nj + 1 + len(out_names)]
        send, recv, ssend, srecv, loc = refs[-5:]
        x, y, c, chips = _place()
        d = 4 * x + 2 * y + c

        def push(a, j, cx, cy):
            _, l, ax, size, _, name = jobs[a]
            land = outs[name].at[j] if l is None else outs[name].at[j, l]
            return pltpu.make_async_remote_copy(src_ref=_slab(in_refs[a], ax, 2 * cx + cy, size), dst_ref=land,
                                                send_sem=send.at[a, j], recv_sem=recv.at[a, j],
                                                device_id=(cx, cy, c), device_id_type=MESH)

        def flip(v, bit):
            return 1 - v if bit else v

        def spush(m, row):
            peer = (flip(x, m & 4), flip(y, m & 2), flip(c, m & 1))
            return pltpu.make_async_remote_copy(src_ref=small_ref, dst_ref=small_out.at[row], send_sem=ssend.at[m - 1],
                                                recv_sem=srecv.at[m - 1], device_id=peer, device_id_type=MESH)

        mine = pltpu.make_async_copy(small_ref, small_out.at[d], loc)
        mine.start()
        pushed = [spush(m, d) for m in masks] + [push(a, j, cx, cy) for a in range(nj) for j, (cx, cy) in enumerate(chips)]
        for cp in pushed:
            cp.start()
        for m in masks:
            spush(m, d ^ m).wait_recv()
        for a in range(nj):
            for j, (cx, cy) in enumerate(chips):
                push(a, j, cx, cy).wait_recv()
        for cp in pushed:
            cp.wait_send()
        mine.wait()

    out_shape = []
    for name in out_names:
        js = [j for j in jobs if j[5] == name]
        lead = (3,) if js[0][1] is None else (3, len(js))
        out_shape.append(jax.ShapeDtypeStruct(lead + js[0][4], bf16))
    out_shape.append(jax.ShapeDtypeStruct((N_DEV,) + small.shape, small.dtype))
    res = pl.pallas_call(
        body, name="scatter_grads", in_specs=[HBM_SPEC] * (nj + 1), out_specs=[HBM_SPEC] * len(out_shape),
        out_shape=out_shape,
        scratch_shapes=[pltpu.SemaphoreType.DMA((nj, 3)), pltpu.SemaphoreType.DMA((nj, 3)),
                        pltpu.SemaphoreType.DMA((N_DEV - 1,)), pltpu.SemaphoreType.DMA((N_DEV - 1,)), pltpu.SemaphoreType.DMA],
    )(*ins, small)
    return dict(zip(out_names, res[:-1])), res[-1]


def _swap_with_sibling(parts):
    names = list(parts)

    def body(*refs):
        n = len(names)
        ins, outs, send, recv = refs[:n], refs[n:2 * n], refs[-2], refs[-1]
        x, y, c, _ = _place()
        cps = [pltpu.make_async_remote_copy(src_ref=ins[a], dst_ref=outs[a], send_sem=send.at[a], recv_sem=recv.at[a],
                                            device_id=(x, y, 1 - c), device_id_type=MESH) for a in range(n)]
        for cp in cps:
            cp.start()
        for cp in cps:
            cp.wait_recv()
        for cp in cps:
            cp.wait_send()

    res = pl.pallas_call(
        body, name="swap_with_sibling", in_specs=[HBM_SPEC] * len(names), out_specs=[HBM_SPEC] * len(names),
        out_shape=[jax.ShapeDtypeStruct(parts[n].shape, parts[n].dtype) for n in names],
        scratch_shapes=[pltpu.SemaphoreType.DMA((len(names),)), pltpu.SemaphoreType.DMA((len(names),))],
    )(*[parts[n] for n in names])
    return dict(zip(names, res))


ELEMENTWISE_BLOCK_BYTES = 1 << 20


def _rows(r, c):
    for t in (512, 256, 128, 64, 32, 16, 8):
        if r % t == 0 and t * c * 4 <= ELEMENTWISE_BLOCK_BYTES:
            return t
    return r


def _sum4(own, recv, *, name):
    R, C = own.shape
    tm = _rows(R, C)

    def body(o_ref, r_ref, out_ref):
        out_ref[...] = ((o_ref[...] + r_ref[0].astype(f32)) + r_ref[1].astype(f32)) + r_ref[2].astype(f32)

    return pl.pallas_call(
        body, name=name, grid=(R // tm,),
        in_specs=[pl.BlockSpec((tm, C), lambda i: (i, 0)), pl.BlockSpec((3, tm, C), lambda i: (0, i, 0))],
        out_specs=pl.BlockSpec((tm, C), lambda i: (i, 0)), out_shape=jax.ShapeDtypeStruct((R, C), f32),
        compiler_params=_cp("parallel"),
    )(own, recv)


def _adamw(w, m, v, parts, *, name):
    R, C = w.shape
    tm = _rows(R, C)
    npart = len(parts)

    def body(*refs):
        w_ref, m_ref, v_ref = refs[:3]
        g_ref, d_ref, m2_ref, v2_ref = refs[3 + npart:]
        g = refs[3][...]
        for p_ref in refs[4:3 + npart]:
            g = g + p_ref[...]
        m2 = ADAM_B1 * m_ref[...] + (1.0 - ADAM_B1) * g
        v2 = ADAM_B2 * v_ref[...] + (1.0 - ADAM_B2) * (g * g)
        m_hat = m2 / (1.0 - ADAM_B1 ** ADAM_STEP)
        v_hat = v2 / (1.0 - ADAM_B2 ** ADAM_STEP)
        g_ref[...] = g
        d_ref[...] = -ADAM_LR * (m_hat / (jnp.sqrt(v_hat) + ADAM_EPS) + ADAM_WD * w_ref[...])
        m2_ref[...] = m2
        v2_ref[...] = v2

    blk = pl.BlockSpec((tm, C), lambda i: (i, 0))
    return pl.pallas_call(
        body, name=name, grid=(R // tm,),
        in_specs=[blk] * (3 + npart), out_specs=[blk] * 4,
        out_shape=[jax.ShapeDtypeStruct((R, C), f32)] * 4, compiler_params=_cp("parallel"),
    )(w, m, v, *parts)


def _pad_odd(w):
    return jnp.pad(w, ((0, 0), (0, ODD_PAD - w.shape[1])))


def _uq_cat(w):
    r = w.shape[0]
    return jnp.pad(w.reshape(r, MLA_HEADS, MLA_QK), ((0, 0), (0, 0), (0, HQ - MLA_QK))).reshape(r, MLA_HEADS * HQ)


def _uq_uncat(w):
    r = w.shape[0]
    return w.reshape(r, MLA_HEADS, HQ)[:, :, :MLA_QK].reshape(r, MLA_HEADS * MLA_QK)


def _to_segments(v):
    T, C = v.shape
    return v.reshape(S5_SEG, T // S5_SEG, C).transpose(1, 0, 2).reshape(T, C)


def _from_segments(v):
    T, C = v.shape
    return v.reshape(T // S5_SEG, S5_SEG, C).transpose(1, 0, 2).reshape(T, C)


def _s5_rb(T):
    return min(512, T)


def _ffn_fwd(h, g, w_in, cw, cb, w_out, tag):
    hn = _rms_fwd(h, g, name=f"ffn{tag}_norm")
    au = _mm(hn, w_in, name=f"ffn{tag}_in", tn=1408)
    z = _ffn_mid_fwd(au, cw, cb, name=f"ffn{tag}_mid")
    return _mm(z, w_out, res=h, name=f"ffn{tag}_out", tk=1408), (hn, au, z)


def _ffn_bwd(h, g, w_in, cw, cb, w_out, saved, dh, tag):
    hn, au, z = saved
    dz = _mm(dh, w_out, tb=True, name=f"ffn{tag}_dz", tn=1408)
    dw_out = _mm(z, dh, ta=True, also_bf16=True, name=f"ffn{tag}_dwout", tm=1408)
    dau, dcw, dcb = _ffn_mid_bwd(au, cw, cb, dz, name=f"ffn{tag}_dmid")
    dhn = _mm(dau, w_in, tb=True, name=f"ffn{tag}_dhn", tk=1408)
    dw_in = _mm(hn, dau, ta=True, also_bf16=True, name=f"ffn{tag}_dwin", tn=1408)
    dh_in, dg = _rms_bwd(h, g, dhn, dh, name=f"ffn{tag}_dnorm")
    return dh_in, dg, dw_in, dcw, dcb, dw_out


def _local_step(x, positions, target, W, P):
    T = x.shape[0]
    rb = _s5_rb(T)
    row = lambda v: v.reshape(1, -1)
    g_mix, g_ffn = P["norm_mix_g"], P["norm_ffn_g"]
    lbl, hng = P["hgrn_lb_logits"], P["hgrn_norm_g"]
    dsk, bg = P["s5_d"], P["s5_b_glu"]
    qg, kvg = P["mla_q_norm_g"], P["mla_kv_norm_g"]
    cw, cb = P["ffn_conv_w"], P["ffn_conv_b"]

    col = lambda v: v.reshape(S5_N, 1)
    disc_in = (col(P["s5_a_re"]), col(P["s5_a_im"]), col(jnp.repeat(P["s5_log_dt"].reshape(S5_GROUPS), S5_STATE)),
               P["s5_b_re"].reshape(S5_N, S5_GROUP), P["s5_b_im"].reshape(S5_N, S5_GROUP))
    abr, abi, bbr, bbi = _s5_disc_fwd(*disc_in)
    ar, ai = abr.reshape(1, S5_N), abi.reshape(1, S5_N)
    bbr3, bbi3 = bbr.reshape(S5_GROUPS, S5_STATE, S5_GROUP), bbi.reshape(S5_GROUPS, S5_STATE, S5_GROUP)
    bre, bim = _blockdiag(bbr3, True).astype(bf16), _blockdiag(bbi3, True).astype(bf16)
    bret, bimt = _blockdiag(bbr3).astype(bf16), _blockdiag(bbi3).astype(bf16)
    c_re, c_im = P["s5_c_re"].reshape(S5_GROUPS, S5_GROUP, S5_STATE), P["s5_c_im"].reshape(S5_GROUPS, S5_GROUP, S5_STATE)
    cre, cim = _blockdiag(c_re, True).astype(bf16), _blockdiag(c_im, True).astype(bf16)
    cret, cimt = _blockdiag(c_re).astype(bf16), _blockdiag(c_im).astype(bf16)

    hn0 = _rms_fwd(x, g_mix[0:1], name="mix0_norm")
    proj_e = _mm(hn0, W["even_w_in"], name="even_in", tn=1280)
    ya, states = _hgrn_fwd(proj_e, lbl, hng)
    u_seg = _to_segments(proj_e[:, 4 * 512:])
    fr, fi = _s5_final(u_seg, bre, bim, ar, ai, rb=rb)
    yb_seg, s0r, s0i = _s5_fwd(u_seg, bre, bim, ar, ai, fr, fi, cre, cim, dsk, W["s5_w_glu"], bg, rb=rb)
    ycat = jnp.concatenate([ya, _from_segments(yb_seg)], axis=1)
    h1 = _mm(ycat, W["even_w_out"], res=x, name="even_out")
    h2, ffn0 = _ffn_fwd(h1, g_ffn[0:1], W["ffn_w_in"][0], cw[0], cb[0:1], W["ffn_w_out"][0], 0)

    tabs = _rope_tables(positions)
    hn2 = _rms_fwd(h2, g_mix[1:2], name="mix1_norm")
    proj_o = _mm(hn2, W["odd_w_in"], name="odd_in")
    cqn, ckvn, kr = _mla_prep_fwd(proj_o, qg, kvg, tabs)
    q = _q_post(_mm(cqn, W["mla_w_uq"], name="mla_uq"), tabs, transpose=False, name="q_post")
    kvb = _mm(ckvn, W["mla_w_ukv"], out_dtype=bf16, name="mla_ukv")
    o, lse = _flash_fwd(q, kvb, kr)
    h3 = _mm(o, W["odd_w_out"], res=h2, name="odd_out")
    h4, ffn1 = _ffn_fwd(h3, g_ffn[1:2], W["ffn_w_in"][1], cw[1], cb[1:2], W["ffn_w_out"][1], 1)
    loss, dh4, dg_final = _loss_head(h4, row(P["final_norm_g"]), target)

    dh3, dg_ffn1, dw_fin1, dcw1, dcb1, dw_fout1 = _ffn_bwd(
        h3, g_ffn[1:2], W["ffn_w_in"][1], cw[1], cb[1:2], W["ffn_w_out"][1], ffn1, dh4, 1)
    do = _mm(dh3, W["odd_w_out"], tb=True, name="odd_do")
    dw_oout = _mm(o, dh3, ta=True, also_bf16=True, name="odd_dwout")
    delta = _attn_delta(o, do)
    dkv, dkr_h = _flash_bwd_kv(q, kvb, kr, do, lse, delta)
    dq = _q_post(_flash_bwd_q(q, kvb, kr, do, lse, delta), tabs, transpose=True, name="dq_post")
    dw_uq = _mm(cqn, dq, ta=True, also_bf16=True, name="mla_dwuq")
    dcqn = _mm(dq, W["mla_w_uq"], tb=True, name="mla_dcq")
    dw_ukv = _mm(ckvn, dkv, ta=True, also_bf16=True, name="mla_dwukv")
    dckvn = _mm(dkv, W["mla_w_ukv"], tb=True, name="mla_dckv")
    dproj_o, dqg, dkvg = _mla_prep_bwd(proj_o, qg, kvg, tabs, dcqn, dckvn, dkr_h)
    dhn2 = _mm(dproj_o, W["odd_w_in"], tb=True, name="odd_dhn")
    dw_oin = _mm(hn2, dproj_o, ta=True, also_bf16=True, name="odd_dwin")
    dh2, dg_mix1 = _rms_bwd(h2, g_mix[1:2], dhn2, dh3, name="mix1_dnorm")

    dh1, dg_ffn0, dw_fin0, dcw0, dcb0, dw_fout0 = _ffn_bwd(
        h1, g_ffn[0:1], W["ffn_w_in"][0], cw[0], cb[0:1], W["ffn_w_out"][0], ffn0, dh2, 0)
    dycat = _mm(dh1, W["even_w_out"], tb=True, name="even_dy")
    dw_eout = _mm(ycat, dh1, ta=True, also_bf16=True, name="even_dwout")
    dq_h, df_h, di_h, dg_h, dlbl, dhng = _hgrn_bwd(proj_e, lbl, hng, states, dycat)
    dyb_seg = _to_segments(dycat[:, 512:])
    dy_s5, glr, gli, dcre, dcim, dd, dwg, dbg = _s5_bwd_a(
        u_seg, bre, bim, ar, ai, s0r, s0i, cre, cim, cret, cimt, dsk, W["s5_w_glu"], bg, dyb_seg, rb=rb)
    du_seg, dbre, dbim, dar, dai = _s5_bwd_b(
        u_seg, bre, bim, bret, bimt, ar, ai, s0r, s0i, glr, gli, cret, cimt, dsk, dy_s5, rb=rb)
    dproj_e = jnp.concatenate([dq_h, df_h, di_h, dg_h, _from_segments(du_seg)], axis=1)
    dhn0 = _mm(dproj_e, W["even_w_in"], tb=True, name="even_dhn", tk=1280)
    dw_ein = _mm(hn0, dproj_e, ta=True, also_bf16=True, name="even_dwin", tn=1280)
    dx, dg_mix0 = _rms_bwd(x, g_mix[0:1], dhn0, dh1, name="mix0_dnorm")

    unblk = lambda m, a, b: jnp.swapaxes(_blockdiag_t(m, a, b), 1, 2)
    dbbr = unblk(dbre, S5_GROUP, S5_STATE).reshape(S5_N, S5_GROUP)
    dbbi = unblk(dbim, S5_GROUP, S5_STATE).reshape(S5_N, S5_GROUP)
    d_ar, d_ai, d_ldt, d_br, d_bi = _s5_disc_bwd(*disc_in, (dar.reshape(S5_N, 1), dai.reshape(S5_N, 1), dbbr, dbbi))
    wdwg = (dwg, dwg.astype(bf16))

    big = {"even_w_in": dw_ein, "s5_w_glu": wdwg, "even_w_out": dw_eout, "odd_w_in": dw_oin, "mla_w_uq": dw_uq,
           "mla_w_ukv": dw_ukv, "odd_w_out": dw_oout, "ffn_w_in": (dw_fin0, dw_fin1), "ffn_w_out": (dw_fout0, dw_fout1)}
    small = {
        "norm_mix_g": jnp.concatenate([dg_mix0, dg_mix1], axis=0),
        "norm_ffn_g": jnp.concatenate([dg_ffn0, dg_ffn1], axis=0),
        "final_norm_g": dg_final.reshape(-1),
        "hgrn_lb_logits": dlbl, "hgrn_norm_g": dhng,
        "s5_a_re": d_ar.reshape(1, S5_GROUPS, S5_STATE), "s5_a_im": d_ai.reshape(1, S5_GROUPS, S5_STATE),
        "s5_log_dt": d_ldt.reshape(S5_GROUPS, S5_STATE).sum(axis=1).reshape(1, S5_GROUPS),
        "s5_b_re": d_br.reshape(1, S5_GROUPS, S5_STATE, S5_GROUP), "s5_b_im": d_bi.reshape(1, S5_GROUPS, S5_STATE, S5_GROUP),
        "s5_c_re": unblk(dcre, S5_STATE, S5_GROUP).reshape(1, S5_GROUPS, S5_GROUP, S5_STATE),
        "s5_c_im": unblk(dcim, S5_STATE, S5_GROUP).reshape(1, S5_GROUPS, S5_GROUP, S5_STATE),
        "s5_d": dd, "s5_b_glu": dbg, "mla_q_norm_g": dqg, "mla_kv_norm_g": dkvg,
        "ffn_conv_w": jnp.stack([dcw0, dcw1]), "ffn_conv_b": jnp.concatenate([dcb0, dcb1], axis=0),
    }
    return loss, dx, big, small


WEIGHTS = ["norm_mix_g", "norm_ffn_g", "final_norm_g", "even_w_in", "hgrn_lb_logits", "hgrn_norm_g", "s5_a_re", "s5_a_im",
           "s5_log_dt", "s5_b_re", "s5_b_im", "s5_c_re", "s5_c_im", "s5_d", "s5_w_glu", "s5_b_glu", "even_w_out", "odd_w_in",
           "mla_q_norm_g", "mla_w_uq", "mla_kv_norm_g", "mla_w_ukv", "odd_w_out", "ffn_w_in", "ffn_conv_w", "ffn_conv_b",
           "ffn_w_out"]
SMALL_SHARDED = {"mla_q_norm_g": 1, "mla_kv_norm_g": 1, "ffn_conv_w": 2}
SMALL = [n for n in WEIGHTS if n not in BIG]


def _pack(arrays):
    flat = jnp.concatenate([a.reshape(-1) for a in arrays])
    n = flat.shape[0]
    tile = SUBLANES * LANES
    return jnp.pad(flat, (0, -n % tile)).reshape(-1, LANES)


def _unpack(block, shapes):
    flat, out, off = block.reshape(-1), [], 0
    for s in shapes:
        n = math.prod(s)
        out.append(flat[off:off + n].reshape(s))
        off += n
    return out


def kernel(x, positions, norm_mix_g, norm_ffn_g, final_norm_g, even_w_in, hgrn_lb_logits, hgrn_norm_g, s5_a_re, s5_a_im, s5_log_dt, s5_b_re, s5_b_im, s5_c_re, s5_c_im, s5_d, s5_w_glu, s5_b_glu, even_w_out, odd_w_in, mla_q_norm_g, mla_w_uq, mla_kv_norm_g, mla_w_ukv, odd_w_out, ffn_w_in, ffn_conv_w, ffn_conv_b, ffn_w_out, loss_target, m_norm_mix_g, m_norm_ffn_g, m_final_norm_g, m_even_w_in, m_hgrn_lb_logits, m_hgrn_norm_g, m_s5_a_re, m_s5_a_im, m_s5_log_dt, m_s5_b_re, m_s5_b_im, m_s5_c_re, m_s5_c_im, m_s5_d, m_s5_w_glu, m_s5_b_glu, m_even_w_out, m_odd_w_in, m_mla_q_norm_g, m_mla_w_uq, m_mla_kv_norm_g, m_mla_w_ukv, m_odd_w_out, m_ffn_w_in, m_ffn_conv_w, m_ffn_conv_b, m_ffn_w_out, v_norm_mix_g, v_norm_ffn_g, v_final_norm_g, v_even_w_in, v_hgrn_lb_logits, v_hgrn_norm_g, v_s5_a_re, v_s5_a_im, v_s5_log_dt, v_s5_b_re, v_s5_b_im, v_s5_c_re, v_s5_c_im, v_s5_d, v_s5_w_glu, v_s5_b_glu, v_even_w_out, v_odd_w_in, v_mla_q_norm_g, v_mla_w_uq, v_mla_kv_norm_g, v_mla_w_ukv, v_odd_w_out, v_ffn_w_in, v_ffn_conv_w, v_ffn_conv_b, v_ffn_w_out):
    args = dict(locals())
    w = {n: args[n] for n in WEIGHTS}
    m = {n: args["m_" + n] for n in WEIGHTS}
    v = {n: args["v_" + n] for n in WEIGHTS}
    k = 2 * lax.axis_index("x") + lax.axis_index("y")

    def block(name, a):
        return a if name in LAYERED else a[0]

    shards = {n: block(n, w[n]).astype(bf16) for n in BIG}
    small_sh_shapes = [w[n].shape for n in SMALL_SHARDED]
    Wf, small_all = _gather_weights(shards, _pack([w[n] for n in SMALL_SHARDED]))
    per_chip = [_unpack(small_all[c], small_sh_shapes) for c in range(N_CHIPS)]
    P = {n: w[n] for n in SMALL if n not in SMALL_SHARDED}
    for i, (n, ax) in enumerate(SMALL_SHARDED.items()):
        P[n] = jnp.concatenate([per_chip[c][i] for c in range(N_CHIPS)], axis=ax)
    P["mla_q_norm_g"], P["mla_kv_norm_g"] = P["mla_q_norm_g"].reshape(1, -1), P["mla_kv_norm_g"].reshape(1, -1)
    W = dict(Wf)
    W["odd_w_in"] = _pad_odd(Wf["odd_w_in"])
    W["mla_w_uq"] = _uq_cat(Wf["mla_w_uq"])

    loss, dx, big, small = _local_step(x[0], positions[0], loss_target[0], W, P)
    loss = lax.psum(loss[0, 0], ("x", "y", "c"))

    fix = {"odd_w_in": lambda g: g[:, :odd_w_in.shape[2]], "mla_w_uq": _uq_uncat}
    g32, g16 = {}, {}
    for n in BIG:
        pairs = big[n] if n in LAYERED else (big[n],)
        f = fix.get(n, lambda g: g)
        g32[n] = tuple(f(p[0]) for p in pairs)
        g16[n] = tuple(f(p[1]) for p in pairs) if n in LAYERED else f(pairs[0][1])
    small_full = {n: small[n].reshape(tuple(d * (N_CHIPS if a == SMALL_SHARDED.get(n, -1) else 1) for a, d in enumerate(w[n].shape)))
                  for n in SMALL}
    small_shapes = [small_full[n].shape for n in SMALL]
    recv, small_all = _scatter_grads(g16, _pack([small_full[n] for n in SMALL]))

    own, part = {}, {}
    for n in BIG:
        ax = BIG[n] - (1 if n in LAYERED else 0)
        size = block(n, w[n]).shape[BIG[n]]
        sl = [lax.dynamic_slice_in_dim(g, k * size, size, axis=ax) for g in g32[n]]
        own[n] = jnp.stack(sl) if n in LAYERED else sl[0]
        C = own[n].shape[-1]
        part[n] = _sum4(own[n].reshape(-1, C), recv[n].reshape(3, -1, C), name=f"sum4_{n}")
    other = _swap_with_sibling(part)

    out = {}
    for n in BIG:
        C = part[n].shape[-1]
        res = _adamw(w[n].reshape(-1, C), m[n].reshape(-1, C), v[n].reshape(-1, C), [part[n], other[n]], name=f"adamw_{n}")
        out[n] = [r.reshape(w[n].shape) for r in res]
    per_dev = [_unpack(small_all[d], small_shapes) for d in range(N_DEV)]
    parts = []
    for d in range(N_DEV):
        mine = []
        for i, n in enumerate(SMALL):
            g = per_dev[d][i]
            if n in SMALL_SHARDED:
                ax = SMALL_SHARDED[n]
                g = lax.dynamic_slice_in_dim(g, k * w[n].shape[ax], w[n].shape[ax], axis=ax)
            mine.append(g)
        parts.append(_pack(mine))
    res = _adamw(_pack([w[n] for n in SMALL]), _pack([m[n] for n in SMALL]), _pack([v[n] for n in SMALL]), parts,
                 name="adamw_small")
    unpacked = [_unpack(r, [w[n].shape for n in SMALL]) for r in res]
    for i, n in enumerate(SMALL):
        out[n] = [u[i] for u in unpacked]

    return (loss, dx[None], *[out[n][0] for n in WEIGHTS], *[out[n][1] for n in WEIGHTS],
            *[out[n][2] for n in WEIGHTS], *[out[n][3] for n in WEIGHTS])
```

```python
import functools
import math

import jax
import jax.numpy as jnp
from jax import lax
from jax.experimental import pallas as pl
from jax.experimental.pallas import tpu as pltpu

f32, bf16 = jnp.float32, jnp.bfloat16
EPS = 1e-6
LANES = 128
SUBLANES = 8
VMEM_BYTES = 48 * 1024 * 1024
HGRN_CHUNK = 64
HGRN_HEADS = 4
S5_GROUPS, S5_STATE, S5_GROUP = 32, 64, 16
S5_N = S5_GROUPS * S5_STATE
S5_SEG = SUBLANES
MLA_HEADS, MLA_NOPE, MLA_ROPE, MLA_V = 8, 128, 64, 128
MLA_QK = MLA_NOPE + MLA_ROPE
MLA_Q_RANK, MLA_KV_RANK = 384, 256
ROPE_THETA = 10000.0
D_FF = 2816
ADAM_LR, ADAM_B1, ADAM_B2, ADAM_EPS, ADAM_WD, ADAM_STEP = 0.001, 0.9, 0.999, 1e-08, 0.01, 10
MESH = pl.DeviceIdType.MESH
HI = lax.Precision.HIGHEST


def _cp(*dims):
    return pltpu.CompilerParams(dimension_semantics=dims if dims else None, vmem_limit_bytes=VMEM_BYTES)


def _tile(n, t):
    if n <= t:
        return n
    c = (t // LANES) * LANES
    while c >= LANES:
        if n % c == 0:
            return c
        c -= LANES
    return n


def _dot(a, b, dn=None, precision=None):
    if dn is None:
        dn = (((a.ndim - 1,), (0,)), ((), ()))
    return lax.dot_general(a, b, dn, preferred_element_type=f32, precision=precision)


NT = (((1,), (1,)), ((), ()))
TN = (((0,), (0,)), ((), ()))


def _bdot(a, b, dn=None):
    return _dot(a.astype(bf16), b.astype(bf16), dn)


def _mm(a, b, *, name, ta=False, tb=False, out_dtype=f32, res=None, also_bf16=False, tm=1024, tn=1024, tk=1024):
    M, K = (a.shape[1], a.shape[0]) if ta else a.shape
    N = b.shape[0] if tb else b.shape[1]
    tm, tn, tk = _tile(M, tm), _tile(N, tn), _tile(K, tk)
    nk = K // tk
    dn = (((0 if ta else 1,), (1 if tb else 0,)), ((), ()))

    def body(*refs):
        a_ref, b_ref = refs[0], refs[1]
        r_ref = refs[2] if res is not None else None
        outs = refs[3 if res is not None else 2:-1]
        acc = refs[-1]
        k = pl.program_id(2)
        p = _bdot(a_ref[...], b_ref[...], dn)

        @pl.when(k == 0)
        def _():
            acc[...] = p

        @pl.when(k > 0)
        def _():
            acc[...] += p

        @pl.when(k == nk - 1)
        def _():
            r = acc[...]
            if r_ref is not None:
                r = r + r_ref[...]
            outs[0][...] = r.astype(out_dtype)
            if also_bf16:
                outs[1][...] = r.astype(bf16)

    a_spec = pl.BlockSpec((tk, tm), lambda i, j, k: (k, i)) if ta else pl.BlockSpec((tm, tk), lambda i, j, k: (i, k))
    b_spec = pl.BlockSpec((tn, tk), lambda i, j, k: (j, k)) if tb else pl.BlockSpec((tk, tn), lambda i, j, k: (k, j))
    o_spec = pl.BlockSpec((tm, tn), lambda i, j, k: (i, j))
    in_specs, args = [a_spec, b_spec], [a, b]
    if res is not None:
        in_specs.append(o_spec)
        args.append(res)
    out_shape = [jax.ShapeDtypeStruct((M, N), out_dtype)]
    out_specs = [o_spec]
    if also_bf16:
        out_shape.append(jax.ShapeDtypeStruct((M, N), bf16))
        out_specs.append(o_spec)
    out = pl.pallas_call(
        body, name=name, grid=(M // tm, N // tn, nk), in_specs=in_specs, out_specs=out_specs, out_shape=out_shape,
        scratch_shapes=[pltpu.VMEM((tm, tn), f32)], compiler_params=_cp("parallel", "parallel", "arbitrary"),
    )(*args)
    return out if also_bf16 else out[0]


def _rms_fwd(x, g, *, name, col=0, width=None, tm=512):
    T = x.shape[0]
    width = x.shape[1] if width is None else width
    tm = _tile(T, tm)

    def body(x_ref, g_ref, o_ref):
        xv = x_ref[...]
        r = lax.rsqrt(jnp.mean(xv * xv, axis=-1, keepdims=True) + EPS)
        o_ref[...] = (xv * r * g_ref[...]).astype(bf16)

    return pl.pallas_call(
        body, name=name, grid=(T // tm,),
        in_specs=[pl.BlockSpec((tm, width), lambda i: (i, col)), pl.BlockSpec((1, width), lambda i: (0, 0))],
        out_specs=pl.BlockSpec((tm, width), lambda i: (i, 0)), out_shape=jax.ShapeDtypeStruct((T, width), bf16),
        compiler_params=_cp("parallel"),
    )(x, g)


def _rms_bwd_math(xv, g, dy):
    r = lax.rsqrt(jnp.mean(xv * xv, axis=-1, keepdims=True) + EPS)
    xh = xv * r
    dxh = dy * g
    dx = r * (dxh - xh * jnp.mean(dxh * xh, axis=-1, keepdims=True))
    dg = jnp.sum(dy * xh, axis=0, keepdims=True)
    return dx, dg


def _rms_bwd(x, g, dy, res=None, *, name, tm=512):
    T, D = x.shape
    tm = _tile(T, tm)

    def body(*refs):
        x_ref, g_ref, dy_ref = refs[:3]
        r_ref = refs[3] if res is not None else None
        dx_ref, dg_ref = refs[-2:]
        dx, dg = _rms_bwd_math(x_ref[...], g_ref[...], dy_ref[...].astype(f32))
        if r_ref is not None:
            dx = dx + r_ref[...]
        dx_ref[...] = dx

        @pl.when(pl.program_id(0) == 0)
        def _():
            dg_ref[...] = dg

        @pl.when(pl.program_id(0) > 0)
        def _():
            dg_ref[...] += dg

    row = pl.BlockSpec((tm, D), lambda i: (i, 0))
    vec = pl.BlockSpec((1, D), lambda i: (0, 0))
    in_specs, args = [row, vec, row], [x, g, dy]
    if res is not None:
        in_specs.append(row)
        args.append(res)
    return pl.pallas_call(
        body, name=name, grid=(T // tm,), in_specs=in_specs, out_specs=[row, vec],
        out_shape=[jax.ShapeDtypeStruct((T, D), f32), jax.ShapeDtypeStruct((1, D), f32)],
        compiler_params=_cp("arbitrary"),
    )(*args)


def _loss_head(h, g, target, *, tm=512):
    T, D = h.shape
    tm = _tile(T, tm)

    def body(h_ref, g_ref, t_ref, loss_ref, dh_ref, dg_ref):
        hv, gv = h_ref[...], g_ref[...]
        r = lax.rsqrt(jnp.mean(hv * hv, axis=-1, keepdims=True) + EPS)
        e = hv * r * gv - t_ref[...]
        part = 0.5 * jnp.sum(jnp.mean(e * e, axis=-1, keepdims=True), axis=0, keepdims=True)
        dx, dg = _rms_bwd_math(hv, gv, e * (1.0 / D))
        dh_ref[...] = dx

        @pl.when(pl.program_id(0) == 0)
        def _():
            loss_ref[...] = part
            dg_ref[...] = dg

        @pl.when(pl.program_id(0) > 0)
        def _():
            loss_ref[...] += part
            dg_ref[...] += dg

    row = pl.BlockSpec((tm, D), lambda i: (i, 0))
    vec = pl.BlockSpec((1, D), lambda i: (0, 0))
    return pl.pallas_call(
        body, name="loss_head", grid=(T // tm,), in_specs=[row, vec, row],
        out_specs=[pl.BlockSpec((1, 1), lambda i: (0, 0)), row, vec],
        out_shape=[jax.ShapeDtypeStruct((1, 1), f32), jax.ShapeDtypeStruct((T, D), f32), jax.ShapeDtypeStruct((1, D), f32)],
        compiler_params=_cp("arbitrary"),
    )(h, g, target)


def _shift_down(v, k):
    rows = lax.broadcasted_iota(jnp.int32, v.shape, 0)
    return jnp.where(rows < k, 0.0, pltpu.roll(v, k, 0))


def _shift_up(v, k):
    n = v.shape[0]
    rows = lax.broadcasted_iota(jnp.int32, v.shape, 0)
    return jnp.where(rows >= n - k, 0.0, pltpu.roll(v, n - k, 0))


def _ffn_mid_fwd(au, cw, cb, *, name):
    T = au.shape[0]
    F = au.shape[1] // 2
    nb = F // LANES

    def body(a_ref, u_ref, w_ref, b_ref, z_ref):
        a = a_ref[...]
        w = w_ref[...]
        ac = w[0:1] * _shift_down(a, 2) + w[1:2] * _shift_down(a, 1) + w[2:3] * a + b_ref[...]
        z_ref[...] = (ac * jax.nn.sigmoid(ac) * u_ref[...]).astype(bf16)

    return pl.pallas_call(
        body, name=name, grid=(nb,),
        in_specs=[pl.BlockSpec((T, LANES), lambda j: (0, j)), pl.BlockSpec((T, LANES), lambda j: (0, nb + j)),
                  pl.BlockSpec((3, LANES), lambda j: (0, j)), pl.BlockSpec((1, LANES), lambda j: (0, j))],
        out_specs=pl.BlockSpec((T, LANES), lambda j: (0, j)), out_shape=jax.ShapeDtypeStruct((T, F), bf16),
        compiler_params=_cp("parallel"),
    )(au, au, cw, cb)


def _ffn_mid_bwd(au, cw, cb, dz, *, name):
    T = au.shape[0]
    F = au.shape[1] // 2
    nb = F // LANES

    def body(a_ref, u_ref, w_ref, b_ref, dz_ref, da_ref, du_ref, dw_ref, db_ref):
        a = a_ref[...]
        w = w_ref[...]
        a2, a1 = _shift_down(a, 2), _shift_down(a, 1)
        ac = w[0:1] * a2 + w[1:2] * a1 + w[2:3] * a + b_ref[...]
        sg = jax.nn.sigmoid(ac)
        dz = dz_ref[...].astype(f32)
        du_ref[...] = (dz * ac * sg).astype(bf16)
        dac = dz * u_ref[...] * sg * (1.0 + ac * (1.0 - sg))
        da_ref[...] = (w[2:3] * dac + w[1:2] * _shift_up(dac, 1) + w[0:1] * _shift_up(dac, 2)).astype(bf16)
        rows = lax.broadcasted_iota(jnp.int32, (3, LANES), 0)
        s0 = jnp.sum(dac * a2, axis=0, keepdims=True)
        s1 = jnp.sum(dac * a1, axis=0, keepdims=True)
        s2 = jnp.sum(dac * a, axis=0, keepdims=True)
        dw_ref[...] = jnp.where(rows == 0, s0, jnp.where(rows == 1, s1, s2))
        db_ref[...] = jnp.sum(dac, axis=0, keepdims=True)

    col = lambda off: pl.BlockSpec((T, LANES), lambda j: (0, off + j))
    da, du, dw, db = pl.pallas_call(
        body, name=name, grid=(nb,),
        in_specs=[col(0), col(nb), pl.BlockSpec((3, LANES), lambda j: (0, j)), pl.BlockSpec((1, LANES), lambda j: (0, j)), col(0)],
        out_specs=[col(0), col(0), pl.BlockSpec((3, LANES), lambda j: (0, j)), pl.BlockSpec((1, LANES), lambda j: (0, j))],
        out_shape=[jax.ShapeDtypeStruct((T, F), bf16), jax.ShapeDtypeStruct((T, F), bf16),
                   jax.ShapeDtypeStruct((3, F), f32), jax.ShapeDtypeStruct((1, F), f32)],
        compiler_params=_cp("parallel"),
    )(au, au, cw, cb, dz)
    return jnp.concatenate([da, du], axis=1), dw, db


def _hgrn_lb(l):
    m = jnp.max(l, axis=0, keepdims=True)
    e = jnp.exp(l - m)
    return e[0:1] / jnp.sum(e, axis=0, keepdims=True)


def _hgrn_chunk(q, fx, lb):
    C = q.shape[0]
    sg = jax.nn.sigmoid(fx)
    F = lb + (1.0 - lb) * sg
    k = 1.0 - F
    logF = jnp.log(F)
    r = lax.broadcasted_iota(jnp.int32, (C, C), 0)
    c = lax.broadcasted_iota(jnp.int32, (C, C), 1)
    tril = (r >= c)
    b = _dot(tril.astype(f32), logF, precision=HI)
    bl = jnp.sum(logF, axis=0, keepdims=True)
    eb = jnp.exp(b)
    enb = jnp.exp(-b)
    elb = jnp.exp(bl - b)
    return dict(sg=sg, F=F, k=k, b=b, bl=bl, eb=eb, enb=enb, elb=elb, qd=q * eb, kd=k * enb, kl=k * elb, tril=tril)


def _hgrn_fwd(proj, lbl, ng, *, rb=512):
    T = proj.shape[0]
    rb = min(rb, T)
    cpb = rb // HGRN_CHUNK
    nblk = T // rb
    H = HGRN_HEADS

    def body(q_ref, f_ref, i_ref, g_ref, lbl_ref, ng_ref, y_ref, st_ref, S):
        @pl.when(pl.program_id(1) == 0)
        def _():
            S[...] = jnp.zeros_like(S)

        lb = _hgrn_lb(lbl_ref[...])
        ngv = ng_ref[...]
        for c in range(cpb):
            sl = pl.ds(c * HGRN_CHUNK, HGRN_CHUNK)
            v, gx = i_ref[sl, :], g_ref[sl, :]
            ch = _hgrn_chunk(q_ref[sl, :], f_ref[sl, :], lb)
            att = jnp.where(ch["tril"], _bdot(ch["qd"], ch["kd"], NT), 0.0)
            St = S[...]
            st_ref[0, c] = St
            o = _bdot(att, v) + _bdot(ch["qd"], St, NT)
            S[...] = St * jnp.exp(ch["bl"]) + _bdot(v, ch["kl"], TN)
            r = lax.rsqrt(jnp.mean(o * o, axis=-1, keepdims=True) + EPS)
            y_ref[sl, :] = (o * r * ngv * (gx * jax.nn.sigmoid(gx))).astype(bf16)

    col = lambda off: pl.BlockSpec((rb, LANES), lambda h, n: (n, off + h))
    return pl.pallas_call(
        body, name="hgrn_fwd", grid=(H, nblk),
        in_specs=[col(0), col(H), col(2 * H), col(3 * H), pl.BlockSpec((2, LANES), lambda h, n: (0, h)),
                  pl.BlockSpec((1, LANES), lambda h, n: (0, h))],
        out_specs=[pl.BlockSpec((rb, LANES), lambda h, n: (n, h)),
                   pl.BlockSpec((1, cpb, LANES, LANES), lambda h, n: (h, n, 0, 0))],
        out_shape=[jax.ShapeDtypeStruct((T, H * LANES), bf16),
                   jax.ShapeDtypeStruct((H, T // HGRN_CHUNK, LANES, LANES), f32)],
        scratch_shapes=[pltpu.VMEM((LANES, LANES), f32)], compiler_params=_cp("parallel", "arbitrary"),
    )(proj, proj, proj, proj, lbl, ng)


def _hgrn_bwd(proj, lbl, ng, states, dy, *, rb=512):
    T = proj.shape[0]
    rb = min(rb, T)
    cpb = rb // HGRN_CHUNK
    nblk = T // rb
    H = HGRN_HEADS
    C = HGRN_CHUNK

    def body(q_ref, f_ref, i_ref, g_ref, lbl_ref, ng_ref, st_ref, dy_ref,
             dq_ref, df_ref, di_ref, dg_ref, dl_ref, dng_ref, dS, dlb_acc, dng_acc):
        n = pl.program_id(1)

        @pl.when(n == 0)
        def _():
            dS[...] = jnp.zeros_like(dS)
            dlb_acc[...] = jnp.zeros_like(dlb_acc)
            dng_acc[...] = jnp.zeros_like(dng_acc)

        lb = _hgrn_lb(lbl_ref[...])
        ngv = ng_ref[...]
        r_i = lax.broadcasted_iota(jnp.int32, (C, C), 0)
        c_i = lax.broadcasted_iota(jnp.int32, (C, C), 1)
        triu = (c_i >= r_i).astype(f32)
        for c in reversed(range(cpb)):
            sl = pl.ds(c * C, C)
            q, v, gx = q_ref[sl, :], i_ref[sl, :], g_ref[sl, :]
            ch = _hgrn_chunk(q, f_ref[sl, :], lb)
            qd, kd, kl = ch["qd"], ch["kd"], ch["kl"]
            att = jnp.where(ch["tril"], _bdot(qd, kd, NT), 0.0)
            St = st_ref[0, c]
            o = _bdot(att, v) + _bdot(qd, St, NT)
            r = lax.rsqrt(jnp.mean(o * o, axis=-1, keepdims=True) + EPS)
            on = o * r
            sgg = jax.nn.sigmoid(gx)
            gate = gx * sgg
            dyv = dy_ref[sl, :].astype(f32)
            dg_ref[sl, :] = (dyv * on * ngv * sgg * (1.0 + gx * (1.0 - sgg))).astype(bf16)
            dng_acc[...] += jnp.sum(dyv * on * gate, axis=0, keepdims=True)
            don = dyv * ngv * gate
            do = r * (don - on * jnp.mean(don * on, axis=-1, keepdims=True))
            dSt = dS[...]
            dA = jnp.where(ch["tril"], _bdot(do, v, NT), 0.0)
            dv = _bdot(att, do, TN) + _bdot(kl, dSt, NT)
            dqd = _bdot(dA, kd) + _bdot(do, St)
            dkd = _bdot(dA, qd, TN)
            dkl = _bdot(v, dSt)
            dec = jnp.exp(ch["bl"])
            ddec = jnp.sum(St * dSt, axis=0, keepdims=True)
            dS[...] = _bdot(do, qd, TN) + dSt * dec
            dB = dqd * qd - dkd * kd - dkl * kl
            dbl = jnp.sum(dkl * kl, axis=0, keepdims=True) + ddec * dec
            dk = dkd * ch["enb"] + dkl * ch["elb"]
            dlogF = _dot(triu, dB, precision=HI) + dbl
            dF = dlogF / ch["F"] - dk
            sg = ch["sg"]
            dq_ref[sl, :] = (dqd * ch["eb"]).astype(bf16)
            di_ref[sl, :] = dv.astype(bf16)
            df_ref[sl, :] = (dF * (1.0 - lb) * sg * (1.0 - sg)).astype(bf16)
            dlb_acc[...] += jnp.sum(dF * (1.0 - sg), axis=0, keepdims=True)

        @pl.when(n == nblk - 1)
        def _():
            dl0 = dlb_acc[...] * lb * (1.0 - lb)
            rows = lax.broadcasted_iota(jnp.int32, (2, LANES), 0)
            dl_ref[...] = jnp.where(rows == 0, dl0, -dl0)
            dng_ref[...] = dng_acc[...]

    col = lambda off: pl.BlockSpec((rb, LANES), lambda h, n: (nblk - 1 - n, off + h))
    vec = lambda rows: pl.BlockSpec((rows, LANES), lambda h, n: (0, h))
    outc = pl.BlockSpec((rb, LANES), lambda h, n: (nblk - 1 - n, h))
    tok = jax.ShapeDtypeStruct((T, H * LANES), bf16)
    return pl.pallas_call(
        body, name="hgrn_bwd", grid=(H, nblk),
        in_specs=[col(0), col(H), col(2 * H), col(3 * H), vec(2), vec(1),
                  pl.BlockSpec((1, cpb, LANES, LANES), lambda h, n: (h, nblk - 1 - n, 0, 0)), col(0)],
        out_specs=[outc, outc, outc, outc, vec(2), vec(1)],
        out_shape=[tok, tok, tok, tok, jax.ShapeDtypeStruct((2, H * LANES), f32), jax.ShapeDtypeStruct((1, H * LANES), f32)],
        scratch_shapes=[pltpu.VMEM((LANES, LANES), f32), pltpu.VMEM((1, LANES), f32), pltpu.VMEM((1, LANES), f32)],
        compiler_params=_cp("parallel", "arbitrary"),
    )(proj, proj, proj, proj, lbl, ng, states, dy)


def _s5_disc_math(ar, ai, ldt, br, bi):
    dt = jnp.exp(ldt)
    mag = jnp.exp(ar * dt)
    abr, abi = mag * jnp.cos(ai * dt), mag * jnp.sin(ai * dt)
    den = ar * ar + ai * ai
    xr, xi = abr - 1.0, abi
    cr = (xr * ar + xi * ai) / den
    ci = (xi * ar - xr * ai) / den
    return abr, abi, cr * br - ci * bi, cr * bi + ci * br


def _s5_disc_fwd(ar, ai, ldt, br, bi):
    def body(ar_ref, ai_ref, ldt_ref, br_ref, bi_ref, o0, o1, o2, o3):
        outs = _s5_disc_math(ar_ref[...], ai_ref[...], ldt_ref[...], br_ref[...], bi_ref[...])
        for o, v in zip((o0, o1, o2, o3), outs):
            o[...] = v

    return pl.pallas_call(
        body, name="s5_disc_fwd",
        out_shape=[jax.ShapeDtypeStruct(ar.shape, f32)] * 2 + [jax.ShapeDtypeStruct(br.shape, f32)] * 2,
    )(ar, ai, ldt, br, bi)


def _s5_disc_bwd(ar, ai, ldt, br, bi, cts):
    def body(ar_ref, ai_ref, ldt_ref, br_ref, bi_ref, c0, c1, c2, c3, o0, o1, o2, o3, o4):
        _, vjp = jax.vjp(_s5_disc_math, ar_ref[...], ai_ref[...], ldt_ref[...], br_ref[...], bi_ref[...])
        for o, v in zip((o0, o1, o2, o3, o4), vjp((c0[...], c1[...], c2[...], c3[...]))):
            o[...] = v

    return pl.pallas_call(
        body, name="s5_disc_bwd",
        out_shape=[jax.ShapeDtypeStruct(ar.shape, f32)] * 3 + [jax.ShapeDtypeStruct(br.shape, f32)] * 2,
    )(ar, ai, ldt, br, bi, *cts)


S5_LC = 512
S5_NLC = S5_N // S5_LC
S5_UB = 4


def _cmul(ar, ai, xr, xi):
    return ar * xr - ai * xi, ar * xi + ai * xr


def _cpow(ar, ai, n):
    rr, ri = None, None
    br, bi = ar, ai
    while n:
        if n & 1:
            rr, ri = (br, bi) if rr is None else _cmul(rr, ri, br, bi)
        n >>= 1
        if n:
            br, bi = _cmul(br, bi, br, bi)
    return rr, ri


def _s5_bu(u_ref, bre_ref, bim_ref, xr, xi):
    for k in range(S5_UB):
        uk = u_ref[:, k * LANES:(k + 1) * LANES].astype(bf16)
        xr[:, k * S5_LC:(k + 1) * S5_LC] = _dot(uk, bre_ref[k])
        xi[:, k * S5_LC:(k + 1) * S5_LC] = _dot(uk, bim_ref[k])


def _s5_scan(xr, xi, sr, si, ar_ref, ai_ref, nsteps, store):
    for c in range(S5_NLC):
        cs = slice(c * S5_LC, (c + 1) * S5_LC)
        a_r = jnp.broadcast_to(ar_ref[:, cs], (S5_SEG, S5_LC))
        a_i = jnp.broadcast_to(ai_ref[:, cs], (S5_SEG, S5_LC))

        def step(j, carry, cs=cs, a_r=a_r, a_i=a_i):
            pr, pi = carry
            rows = pl.ds(pl.multiple_of(j * S5_SEG, S5_SEG), S5_SEG)
            nr = a_r * pr - a_i * pi + xr[rows, cs]
            ni = a_r * pi + a_i * pr + xi[rows, cs]
            if store:
                xr[rows, cs] = nr
                xi[rows, cs] = ni
            return nr, ni

        fr, fi = lax.fori_loop(0, nsteps, step, (sr[:, cs], si[:, cs]))
        sr[:, cs] = fr
        si[:, cs] = fi


def _s5_rscan(dr, di, xr, xi, s0r, s0i, gr, gi, acc_r, acc_i, ar_ref, ai_ref, nsteps):
    for c in range(S5_NLC):
        cs = slice(c * S5_LC, (c + 1) * S5_LC)
        a_r = jnp.broadcast_to(ar_ref[:, cs], (S5_SEG, S5_LC))
        a_i = jnp.broadcast_to(ai_ref[:, cs], (S5_SEG, S5_LC))

        def step(jj, carry, cs=cs, a_r=a_r, a_i=a_i):
            pr, pi, cr, ci = carry
            j = nsteps - 1 - jj
            rows = pl.ds(pl.multiple_of(j * S5_SEG, S5_SEG), S5_SEG)
            nr = dr[rows, cs] + a_r * pr + a_i * pi
            ni = di[rows, cs] + a_r * pi - a_i * pr
            dr[rows, cs] = nr
            di[rows, cs] = ni
            if acc_r is not None:
                prev = pl.ds(pl.multiple_of(jnp.maximum(j - 1, 0) * S5_SEG, S5_SEG), S5_SEG)
                first = j == 0
                pr_s = jnp.where(first, s0r[:, cs], xr[prev, cs])
                pi_s = jnp.where(first, s0i[:, cs], xi[prev, cs])
                cr = cr + nr * pr_s + ni * pi_s
                ci = ci - nr * pi_s + ni * pr_s
            return nr, ni, cr, ci

        z = jnp.zeros((S5_SEG, S5_LC), f32)
        init = (gr[:, cs], gi[:, cs], z, z)
        fr, fi, cr, ci = lax.fori_loop(0, nsteps, step, init)
        gr[:, cs] = fr
        gi[:, cs] = fi
        if acc_r is not None:
            acc_r[:, cs] += cr
            acc_i[:, cs] += ci


def _s5_seg_carry(fr, fi, ar, ai, seg_len, reverse):
    pr, pi = _cpow(ar, ai if not reverse else -ai, seg_len)
    rows = lax.broadcasted_iota(jnp.int32, fr.shape, 0)
    cr, ci = jnp.zeros_like(fr), jnp.zeros_like(fi)
    sh = (S5_SEG - 1) if reverse else 1
    fr_s, fi_s = pltpu.roll(fr, sh, 0), pltpu.roll(fi, sh, 0)
    order = range(S5_SEG - 2, -1, -1) if reverse else range(1, S5_SEG)
    for r in order:
        c_r, c_i = pltpu.roll(cr, sh, 0), pltpu.roll(ci, sh, 0)
        m_r, m_i = _cmul(pr, pi, c_r, c_i)
        cr = jnp.where(rows == r, m_r + fr_s, cr)
        ci = jnp.where(rows == r, m_i + fi_s, ci)
    return cr, ci


def _gelu_parts(y):
    c0 = math.sqrt(2.0 / math.pi)
    t = jnp.tanh(c0 * (y + 0.044715 * y * y * y))
    z = 0.5 * y * (1.0 + t)
    dz = 0.5 * (1.0 + t) + 0.5 * y * (1.0 - t * t) * c0 * (1.0 + 3.0 * 0.044715 * y * y)
    return z, dz


def _s5_y(xr, xi, u_ref, cre_ref, cim_ref, d_ref):
    ys = []
    for k in range(S5_UB):
        cs = slice(k * S5_LC, (k + 1) * S5_LC)
        ys.append(_bdot(xr[:, cs], cre_ref[k]) - _bdot(xi[:, cs], cim_ref[k]))
    return jnp.concatenate(ys, axis=1) + d_ref[...] * u_ref[...]


def _s5_specs(T, rb, rev=False):
    nblk = T // rb
    blk = (lambda i: (nblk - 1 - i, 0)) if rev else (lambda i: (i, 0))
    tok = pl.BlockSpec((rb, 4 * LANES), blk)
    bmat = pl.BlockSpec((S5_UB, LANES, S5_LC), lambda i: (0, 0, 0))
    cmat = pl.BlockSpec((S5_UB, S5_LC, LANES), lambda i: (0, 0, 0))
    avec = pl.BlockSpec((1, S5_N), lambda i: (0, 0))
    seg = pl.BlockSpec((S5_SEG, S5_N), lambda i: (0, 0))
    cvec = pl.BlockSpec((1, 4 * LANES), lambda i: (0, 0))
    s0 = pl.BlockSpec((1, S5_SEG, S5_N), (lambda i: (nblk - 1 - i, 0, 0)) if rev else (lambda i: (i, 0, 0)))
    return dict(tok=tok, bmat=bmat, cmat=cmat, avec=avec, seg=seg, cvec=cvec, s0=s0, nblk=nblk)


def _s5_final(u, bre, bim, ar, ai, *, rb):
    T = u.shape[0]
    sp = _s5_specs(T, rb)

    def body(u_ref, bre_ref, bim_ref, ar_ref, ai_ref, fr_ref, fi_ref, xr, xi):
        @pl.when(pl.program_id(0) == 0)
        def _():
            fr_ref[...] = jnp.zeros_like(fr_ref)
            fi_ref[...] = jnp.zeros_like(fi_ref)

        _s5_bu(u_ref, bre_ref, bim_ref, xr, xi)
        _s5_scan(xr, xi, fr_ref, fi_ref, ar_ref, ai_ref, rb // S5_SEG, False)

    return pl.pallas_call(
        body, name="s5_final", grid=(sp["nblk"],),
        in_specs=[sp["tok"], sp["bmat"], sp["bmat"], sp["avec"], sp["avec"]], out_specs=[sp["seg"], sp["seg"]],
        out_shape=[jax.ShapeDtypeStruct((S5_SEG, S5_N), f32)] * 2,
        scratch_shapes=[pltpu.VMEM((rb, S5_N), f32)] * 2, compiler_params=_cp("arbitrary"),
    )(u, bre, bim, ar, ai)


def _s5_fwd(u, bre, bim, ar, ai, fr, fi, cre, cim, dsk, wg, bg, *, rb):
    T = u.shape[0]
    sp = _s5_specs(T, rb)
    seg_len = T // S5_SEG

    def body(u_ref, bre_ref, bim_ref, ar_ref, ai_ref, fr_ref, fi_ref, cre_ref, cim_ref, d_ref, wg_ref, bg_ref,
             o_ref, s0r_ref, s0i_ref, xr, xi, sr, si):
        @pl.when(pl.program_id(0) == 0)
        def _():
            i_r, i_i = _s5_seg_carry(fr_ref[...], fi_ref[...], ar_ref[...], ai_ref[...], seg_len, False)
            sr[...] = i_r
            si[...] = i_i

        s0r_ref[0] = sr[...]
        s0i_ref[0] = si[...]
        _s5_bu(u_ref, bre_ref, bim_ref, xr, xi)
        _s5_scan(xr, xi, sr, si, ar_ref, ai_ref, rb // S5_SEG, True)
        y = _s5_y(xr, xi, u_ref, cre_ref, cim_ref, d_ref)
        z, _ = _gelu_parts(y)
        v = _bdot(z, wg_ref[...]) + bg_ref[...]
        o_ref[...] = (z * jax.nn.sigmoid(v)).astype(bf16)

    wspec = pl.BlockSpec((4 * LANES, 4 * LANES), lambda i: (0, 0))
    return pl.pallas_call(
        body, name="s5_fwd", grid=(sp["nblk"],),
        in_specs=[sp["tok"], sp["bmat"], sp["bmat"], sp["avec"], sp["avec"], sp["seg"], sp["seg"], sp["cmat"], sp["cmat"],
                  sp["cvec"], wspec, sp["cvec"]],
        out_specs=[sp["tok"], sp["s0"], sp["s0"]],
        out_shape=[jax.ShapeDtypeStruct((T, 4 * LANES), bf16)] + [jax.ShapeDtypeStruct((sp["nblk"], S5_SEG, S5_N), f32)] * 2,
        scratch_shapes=[pltpu.VMEM((rb, S5_N), f32)] * 2 + [pltpu.VMEM((S5_SEG, S5_N), f32)] * 2,
        compiler_params=_cp("arbitrary"),
    )(u, bre, bim, ar, ai, fr, fi, cre, cim, dsk, wg, bg)


def _s5_bwd_a(u, bre, bim, ar, ai, s0r, s0i, cre, cim, cret, cimt, dsk, wg, bg, dout, *, rb):
    T = u.shape[0]
    sp = _s5_specs(T, rb, rev=True)

    def body(u_ref, bre_ref, bim_ref, ar_ref, ai_ref, s0r_ref, s0i_ref, cre_ref, cim_ref, cret_ref, cimt_ref,
             d_ref, wg_ref, bg_ref, do_ref, dy_ref, glr_ref, gli_ref, dcre_ref, dcim_ref, dd_ref, dwg_ref, dbg_ref,
             xr, xi, dr, di, sr, si):
        @pl.when(pl.program_id(0) == 0)
        def _():
            for r in (glr_ref, gli_ref, dcre_ref, dcim_ref, dd_ref, dwg_ref, dbg_ref):
                r[...] = jnp.zeros_like(r)

        sr[...] = s0r_ref[0]
        si[...] = s0i_ref[0]
        _s5_bu(u_ref, bre_ref, bim_ref, xr, xi)
        _s5_scan(xr, xi, sr, si, ar_ref, ai_ref, rb // S5_SEG, True)
        uv = u_ref[...]
        y = _s5_y(xr, xi, u_ref, cre_ref, cim_ref, d_ref)
        z, gz = _gelu_parts(y)
        v = _bdot(z, wg_ref[...]) + bg_ref[...]
        sg = jax.nn.sigmoid(v)
        dov = do_ref[...].astype(f32)
        dv = dov * z * sg * (1.0 - sg)
        dz = dov * sg + _bdot(dv, wg_ref[...], NT)
        dy = dz * gz
        dy_ref[...] = dy
        dwg_ref[...] += _bdot(z, dv, TN)
        dbg_ref[...] += jnp.sum(dv, axis=0, keepdims=True)
        dd_ref[...] += jnp.sum(dy * uv, axis=0, keepdims=True)
        for k in range(S5_UB):
            cs = slice(k * S5_LC, (k + 1) * S5_LC)
            dyk = dy[:, k * LANES:(k + 1) * LANES]
            dcre_ref[k] += _bdot(xr[:, cs], dyk, TN)
            dcim_ref[k] -= _bdot(xi[:, cs], dyk, TN)
            dr[:, cs] = _bdot(dyk, cret_ref[k])
            di[:, cs] = -_bdot(dyk, cimt_ref[k])
        _s5_rscan(dr, di, None, None, None, None, glr_ref, gli_ref, None, None, ar_ref, ai_ref, rb // S5_SEG)

    wspec = pl.BlockSpec((4 * LANES, 4 * LANES), lambda i: (0, 0))
    return pl.pallas_call(
        body, name="s5_bwd_a", grid=(sp["nblk"],),
        in_specs=[sp["tok"], sp["bmat"], sp["bmat"], sp["avec"], sp["avec"], sp["s0"], sp["s0"], sp["cmat"], sp["cmat"],
                  sp["bmat"], sp["bmat"], sp["cvec"], wspec, sp["cvec"], sp["tok"]],
        out_specs=[sp["tok"], sp["seg"], sp["seg"], sp["cmat"], sp["cmat"], sp["cvec"], wspec, sp["cvec"]],
        out_shape=[jax.ShapeDtypeStruct((T, 4 * LANES), f32)] + [jax.ShapeDtypeStruct((S5_SEG, S5_N), f32)] * 2
        + [jax.ShapeDtypeStruct((S5_UB, S5_LC, LANES), f32)] * 2
        + [jax.ShapeDtypeStruct((1, 4 * LANES), f32), jax.ShapeDtypeStruct((4 * LANES, 4 * LANES), f32),
           jax.ShapeDtypeStruct((1, 4 * LANES), f32)],
        scratch_shapes=[pltpu.VMEM((rb, S5_N), f32)] * 4 + [pltpu.VMEM((S5_SEG, S5_N), f32)] * 2,
        compiler_params=_cp("arbitrary"),
    )(u, bre, bim, ar, ai, s0r, s0i, cre, cim, cret, cimt, dsk, wg, bg, dout)


def _s5_bwd_b(u, bre, bim, bret, bimt, ar, ai, s0r, s0i, glr, gli, cret, cimt, dsk, dy, *, rb):
    T = u.shape[0]
    sp = _s5_specs(T, rb, rev=True)
    seg_len = T // S5_SEG
    nblk = sp["nblk"]

    def body(u_ref, bre_ref, bim_ref, bret_ref, bimt_ref, ar_ref, ai_ref, s0r_ref, s0i_ref, glr_ref, gli_ref,
             cret_ref, cimt_ref, d_ref, dy_ref, du_ref, dbre_ref, dbim_ref, dar_ref, dai_ref,
             xr, xi, dr, di, sr, si, gr, gi, acc_r, acc_i):
        @pl.when(pl.program_id(0) == 0)
        def _():
            x_r, x_i = _s5_seg_carry(glr_ref[...], gli_ref[...], ar_ref[...], ai_ref[...], seg_len, True)
            gr[...] = x_r
            gi[...] = x_i
            acc_r[...] = jnp.zeros_like(acc_r)
            acc_i[...] = jnp.zeros_like(acc_i)
            dbre_ref[...] = jnp.zeros_like(dbre_ref)
            dbim_ref[...] = jnp.zeros_like(dbim_ref)

        sr[...] = s0r_ref[0]
        si[...] = s0i_ref[0]
        _s5_bu(u_ref, bre_ref, bim_ref, xr, xi)
        _s5_scan(xr, xi, sr, si, ar_ref, ai_ref, rb // S5_SEG, True)
        dy = dy_ref[...]
        for k in range(S5_UB):
            cs = slice(k * S5_LC, (k + 1) * S5_LC)
            dyk = dy[:, k * LANES:(k + 1) * LANES]
            dr[:, cs] = _bdot(dyk, cret_ref[k])
            di[:, cs] = -_bdot(dyk, cimt_ref[k])
        sr[...] = s0r_ref[0]
        si[...] = s0i_ref[0]
        _s5_rscan(dr, di, xr, xi, sr, si, gr, gi, acc_r, acc_i, ar_ref, ai_ref, rb // S5_SEG)
        dus = []
        for k in range(S5_UB):
            cs = slice(k * S5_LC, (k + 1) * S5_LC)
            uk = u_ref[:, k * LANES:(k + 1) * LANES]
            dbre_ref[k] += _bdot(uk, dr[:, cs], TN)
            dbim_ref[k] += _bdot(uk, di[:, cs], TN)
            dus.append(_bdot(dr[:, cs], bret_ref[k]) + _bdot(di[:, cs], bimt_ref[k]))
        du_ref[...] = (jnp.concatenate(dus, axis=1) + d_ref[...] * dy).astype(bf16)

        @pl.when(pl.program_id(0) == nblk - 1)
        def _():
            dar_ref[...] = jnp.sum(acc_r[...], axis=0, keepdims=True)
            dai_ref[...] = jnp.sum(acc_i[...], axis=0, keepdims=True)

    return pl.pallas_call(
        body, name="s5_bwd_b", grid=(nblk,),
        in_specs=[sp["tok"], sp["bmat"], sp["bmat"], sp["cmat"], sp["cmat"], sp["avec"], sp["avec"], sp["s0"], sp["s0"],
                  sp["seg"], sp["seg"], sp["bmat"], sp["bmat"], sp["cvec"], sp["tok"]],
        out_specs=[sp["tok"], sp["bmat"], sp["bmat"], sp["avec"], sp["avec"]],
        out_shape=[jax.ShapeDtypeStruct((T, 4 * LANES), bf16)] + [jax.ShapeDtypeStruct((S5_UB, LANES, S5_LC), f32)] * 2
        + [jax.ShapeDtypeStruct((1, S5_N), f32)] * 2,
        scratch_shapes=[pltpu.VMEM((rb, S5_N), f32)] * 4 + [pltpu.VMEM((S5_SEG, S5_N), f32)] * 6,
        compiler_params=_cp("arbitrary"),
    )(u, bre, bim, bret, bimt, ar, ai, s0r, s0i, glr, gli, cret, cimt, dsk, dy)


def _blockdiag(w, transpose=False):
    if transpose:
        w = jnp.swapaxes(w, 1, 2)
    g, a, b = w.shape
    eye = jnp.eye(8, dtype=w.dtype)
    return jnp.einsum("kgab,gj->kgajb", w.reshape(4, 8, a, b), eye).reshape(4, 8 * a, 8 * b)


def _blockdiag_t(m, a, b):
    eye = jnp.eye(8, dtype=m.dtype)
    return jnp.einsum("kgajb,gj->kgab", m.reshape(4, 8, a, 8, b), eye).reshape(32, a, b)


ROT = MLA_ROPE // 2


def _rope_tables(positions):
    freqs = ROPE_THETA ** (-jnp.arange(0, MLA_ROPE, 2, dtype=f32) / MLA_ROPE)
    ang = positions.astype(f32)[:, None] * freqs
    cos, sin, z = jnp.cos(ang), jnp.sin(ang), jnp.zeros_like(ang)
    return (jnp.concatenate([cos, cos, z, z], axis=1), jnp.concatenate([-sin, z, z, z], axis=1),
            jnp.concatenate([z, sin, z, z], axis=1))


def _rot(x, c, sa, sb):
    return x * c + pltpu.roll(x, LANES - ROT, 1) * sa + pltpu.roll(x, ROT, 1) * sb


def _rot_t(dy, c, sa, sb):
    return dy * c + pltpu.roll(dy * sa, ROT, 1) + pltpu.roll(dy * sb, LANES - ROT, 1)


def _rms(xv, g):
    return xv * lax.rsqrt(jnp.mean(xv * xv, axis=-1, keepdims=True) + EPS) * g


QW, KVW = MLA_Q_RANK, MLA_KV_RANK
ODD_PAD = QW + KVW + LANES


def _mla_prep_fwd(proj, qg, kvg, tabs, *, tm=512):
    T = proj.shape[0]
    tm = _tile(T, tm)

    def body(p_ref, qg_ref, kvg_ref, c_ref, sa_ref, sb_ref, cq_ref, ckv_ref, kr_ref):
        cq_ref[...] = _rms(p_ref[:, :QW], qg_ref[...]).astype(bf16)
        ckv_ref[...] = _rms(p_ref[:, QW:QW + KVW], kvg_ref[...]).astype(bf16)
        kr_ref[...] = _rot(p_ref[:, QW + KVW:], c_ref[...], sa_ref[...], sb_ref[...]).astype(bf16)

    row = lambda w: pl.BlockSpec((tm, w), lambda i: (i, 0))
    vec = lambda w: pl.BlockSpec((1, w), lambda i: (0, 0))
    return pl.pallas_call(
        body, name="mla_prep_fwd", grid=(T // tm,),
        in_specs=[row(ODD_PAD), vec(QW), vec(KVW), row(LANES), row(LANES), row(LANES)],
        out_specs=[row(QW), row(KVW), row(LANES)],
        out_shape=[jax.ShapeDtypeStruct((T, QW), bf16), jax.ShapeDtypeStruct((T, KVW), bf16),
                   jax.ShapeDtypeStruct((T, LANES), bf16)],
        compiler_params=_cp("parallel"),
    )(proj, qg, kvg, *tabs)


def _mla_prep_bwd(proj, qg, kvg, tabs, dcqn, dckvn, dkr_heads, *, tm=512):
    T = proj.shape[0]
    tm = _tile(T, tm)

    def body(p_ref, qg_ref, kvg_ref, c_ref, sa_ref, sb_ref, dcq_ref, dckv_ref, dkr_ref, dp_ref, dqg_ref, dkvg_ref):
        dcq, dqg = _rms_bwd_math(p_ref[:, :QW], qg_ref[...], dcq_ref[...])
        dckv, dkvg = _rms_bwd_math(p_ref[:, QW:QW + KVW], kvg_ref[...], dckv_ref[...])
        dk = dkr_ref[:, :LANES]
        for h in range(1, MLA_HEADS):
            dk = dk + dkr_ref[:, h * LANES:(h + 1) * LANES]
        dkr = _rot_t(dk, c_ref[...], sa_ref[...], sb_ref[...])
        dp_ref[...] = jnp.concatenate([dcq, dckv, dkr], axis=1).astype(bf16)

        @pl.when(pl.program_id(0) == 0)
        def _():
            dqg_ref[...] = dqg
            dkvg_ref[...] = dkvg

        @pl.when(pl.program_id(0) > 0)
        def _():
            dqg_ref[...] += dqg
            dkvg_ref[...] += dkvg

    row = lambda w: pl.BlockSpec((tm, w), lambda i: (i, 0))
    vec = lambda w: pl.BlockSpec((1, w), lambda i: (0, 0))
    return pl.pallas_call(
        body, name="mla_prep_bwd", grid=(T // tm,),
        in_specs=[row(ODD_PAD), vec(QW), vec(KVW), row(LANES), row(LANES), row(LANES), row(QW), row(KVW),
                  row(MLA_HEADS * LANES)],
        out_specs=[row(ODD_PAD), vec(QW), vec(KVW)],
        out_shape=[jax.ShapeDtypeStruct((T, ODD_PAD), bf16), jax.ShapeDtypeStruct((1, QW), f32),
                   jax.ShapeDtypeStruct((1, KVW), f32)],
        compiler_params=_cp("arbitrary"),
    )(proj, qg, kvg, *tabs, dcqn, dckvn, dkr_heads)


HQ = 2 * LANES
QK_SCALE = MLA_QK ** -0.5


def _q_post(q, tabs, *, transpose, name, tm=512):
    T = q.shape[0]
    tm = _tile(T, tm)

    def body(q_ref, c_ref, sa_ref, sb_ref, o_ref):
        qv = q_ref[...].astype(f32)
        rope = (_rot_t if transpose else _rot)(qv[:, LANES:], c_ref[...], sa_ref[...], sb_ref[...])
        o_ref[...] = (jnp.concatenate([qv[:, :LANES], rope], axis=1) * QK_SCALE).astype(bf16)

    tab = pl.BlockSpec((tm, LANES), lambda i, h: (i, 0))
    blk = pl.BlockSpec((tm, HQ), lambda i, h: (i, h))
    return pl.pallas_call(
        body, name=name, grid=(T // tm, MLA_HEADS), in_specs=[blk, tab, tab, tab], out_specs=blk,
        out_shape=jax.ShapeDtypeStruct(q.shape, bf16), compiler_params=_cp("parallel", "parallel"),
    )(q, *tabs)


def _causal_mask(i, j, tq, tk):
    r = lax.broadcasted_iota(jnp.int32, (tq, tk), 0) + i * tq
    c = lax.broadcasted_iota(jnp.int32, (tq, tk), 1) + j * tk
    return c <= r


def _flash_fwd(q, kv, kr, *, tb=512):
    T = q.shape[0]
    tb = _tile(T, tb)
    nb = T // tb
    H = MLA_HEADS

    def body(q_ref, kn_ref, v_ref, kr_ref, o_ref, lse_ref, m_s, l_s, acc):
        i, j = pl.program_id(1), pl.program_id(2)

        @pl.when(j == 0)
        def _():
            m_s[...] = jnp.full_like(m_s, -jnp.inf)
            l_s[...] = jnp.zeros_like(l_s)
            acc[...] = jnp.zeros_like(acc)

        @pl.when(j <= i)
        def _():
            k = jnp.concatenate([kn_ref[...], kr_ref[...]], axis=1)
            s = _dot(q_ref[...], k, NT)
            s = jnp.where(_causal_mask(i, j, tb, tb), s, -jnp.inf)
            m_new = jnp.maximum(m_s[...], jnp.max(s, axis=-1, keepdims=True))
            alpha = jnp.exp(m_s[...] - m_new)
            p = jnp.exp(s - m_new)
            l_s[...] = alpha * l_s[...] + jnp.sum(p, axis=-1, keepdims=True)
            acc[...] = alpha * acc[...] + _dot(p.astype(bf16), v_ref[...])
            m_s[...] = m_new

        @pl.when(j == i)
        def _():
            o_ref[...] = (acc[...] / l_s[...]).astype(bf16)
            lse_ref[0] = m_s[...] + jnp.log(l_s[...])

    kblk = lambda off: pl.BlockSpec((tb, LANES), lambda h, i, j: (jnp.minimum(j, i), 2 * h + off))
    return pl.pallas_call(
        body, name="flash_fwd", grid=(H, nb, nb),
        in_specs=[pl.BlockSpec((tb, HQ), lambda h, i, j: (i, h)), kblk(0), kblk(1),
                  pl.BlockSpec((tb, LANES), lambda h, i, j: (jnp.minimum(j, i), 0))],
        out_specs=[pl.BlockSpec((tb, LANES), lambda h, i, j: (i, h)), pl.BlockSpec((1, tb, 1), lambda h, i, j: (h, i, 0))],
        out_shape=[jax.ShapeDtypeStruct((T, H * LANES), bf16), jax.ShapeDtypeStruct((H, T, 1), f32)],
        scratch_shapes=[pltpu.VMEM((tb, 1), f32), pltpu.VMEM((tb, 1), f32), pltpu.VMEM((tb, LANES), f32)],
        compiler_params=_cp("parallel", "parallel", "arbitrary"),
    )(q, kv, kv, kr)


def _attn_delta(o, do, *, tm=512):
    T = o.shape[0]
    tm = _tile(T, tm)

    def body(o_ref, do_ref, d_ref):
        d_ref[0] = jnp.sum(o_ref[...].astype(f32) * do_ref[...], axis=-1, keepdims=True)

    blk = pl.BlockSpec((tm, LANES), lambda h, i: (i, h))
    return pl.pallas_call(
        body, name="attn_delta", grid=(MLA_HEADS, T // tm), in_specs=[blk, blk],
        out_specs=pl.BlockSpec((1, tm, 1), lambda h, i: (h, i, 0)),
        out_shape=jax.ShapeDtypeStruct((MLA_HEADS, T, 1), f32), compiler_params=_cp("parallel", "parallel"),
    )(o, do)


def _flash_p_ds(q_ref, kn_ref, v_ref, kr_ref, do_ref, lse_ref, dl_ref, i, j, tb):
    k = jnp.concatenate([kn_ref[...], kr_ref[...]], axis=1)
    s = _dot(q_ref[...], k, NT)
    p = jnp.where(_causal_mask(i, j, tb, tb), jnp.exp(s - lse_ref[0]), 0.0)
    dp = _bdot(do_ref[...], v_ref[...], NT)
    ds = p * (dp - dl_ref[0])
    return k, p, ds


def _flash_bwd_kv(q, kv, kr, do, lse, delta, *, tb=512):
    T = q.shape[0]
    tb = _tile(T, tb)
    nb = T // tb
    H = MLA_HEADS

    def body(q_ref, kn_ref, v_ref, kr_ref, do_ref, lse_ref, dl_ref, dkv_ref, dkr_ref, dk_acc, dv_acc):
        j, ii = pl.program_id(1), pl.program_id(2)
        i = jnp.maximum(ii, j)

        @pl.when(ii == 0)
        def _():
            dk_acc[...] = jnp.zeros_like(dk_acc)
            dv_acc[...] = jnp.zeros_like(dv_acc)

        @pl.when(ii >= j)
        def _():
            _, p, ds = _flash_p_ds(q_ref, kn_ref, v_ref, kr_ref, do_ref, lse_ref, dl_ref, i, j, tb)
            dv_acc[...] += _bdot(p, do_ref[...], TN)
            dk_acc[...] += _bdot(ds, q_ref[...], TN)

        @pl.when(ii == nb - 1)
        def _():
            dkv_ref[...] = jnp.concatenate([dk_acc[:, :LANES], dv_acc[...]], axis=1).astype(bf16)
            dkr_ref[...] = dk_acc[:, LANES:]

    qi = lambda h, j, i: jnp.maximum(i, j)
    kblk = lambda off: pl.BlockSpec((tb, LANES), lambda h, j, i: (j, 2 * h + off))
    vec = pl.BlockSpec((1, tb, 1), lambda h, j, i: (h, qi(h, j, i), 0))
    return pl.pallas_call(
        body, name="flash_bwd_kv", grid=(H, nb, nb),
        in_specs=[pl.BlockSpec((tb, HQ), lambda h, j, i: (qi(h, j, i), h)), kblk(0), kblk(1),
                  pl.BlockSpec((tb, LANES), lambda h, j, i: (j, 0)),
                  pl.BlockSpec((tb, LANES), lambda h, j, i: (qi(h, j, i), h)), vec, vec],
        out_specs=[pl.BlockSpec((tb, HQ), lambda h, j, i: (j, h)), pl.BlockSpec((tb, LANES), lambda h, j, i: (j, h))],
        out_shape=[jax.ShapeDtypeStruct((T, H * HQ), bf16), jax.ShapeDtypeStruct((T, H * LANES), f32)],
        scratch_shapes=[pltpu.VMEM((tb, HQ), f32), pltpu.VMEM((tb, LANES), f32)],
        compiler_params=_cp("parallel", "parallel", "arbitrary"),
    )(q, kv, kv, kr, do, lse, delta)


def _flash_bwd_q(q, kv, kr, do, lse, delta, *, tb=512):
    T = q.shape[0]
    tb = _tile(T, tb)
    nb = T // tb
    H = MLA_HEADS

    def body(q_ref, kn_ref, v_ref, kr_ref, do_ref, lse_ref, dl_ref, dq_ref, acc):
        i, j = pl.program_id(1), pl.program_id(2)

        @pl.when(j == 0)
        def _():
            acc[...] = jnp.zeros_like(acc)

        @pl.when(j <= i)
        def _():
            k, _, ds = _flash_p_ds(q_ref, kn_ref, v_ref, kr_ref, do_ref, lse_ref, dl_ref, i, j, tb)
            acc[...] += _bdot(ds, k)

        @pl.when(j == i)
        def _():
            dq_ref[...] = acc[...]

    kj = lambda h, i, j: jnp.minimum(j, i)
    kblk = lambda off: pl.BlockSpec((tb, LANES), lambda h, i, j: (kj(h, i, j), 2 * h + off))
    vec = pl.BlockSpec((1, tb, 1), lambda h, i, j: (h, i, 0))
    return pl.pallas_call(
        body, name="flash_bwd_q", grid=(H, nb, nb),
        in_specs=[pl.BlockSpec((tb, HQ), lambda h, i, j: (i, h)), kblk(0), kblk(1),
                  pl.BlockSpec((tb, LANES), lambda h, i, j: (kj(h, i, j), 0)),
                  pl.BlockSpec((tb, LANES), lambda h, i, j: (i, h)), vec, vec],
        out_specs=pl.BlockSpec((tb, HQ), lambda h, i, j: (i, h)),
        out_shape=jax.ShapeDtypeStruct((T, H * HQ), f32),
        scratch_shapes=[pltpu.VMEM((tb, HQ), f32)], compiler_params=_cp("parallel", "parallel", "arbitrary"),
    )(q, kv, kv, kr, do, lse, delta)


HBM_SPEC = pl.BlockSpec(memory_space=pltpu.HBM)
N_CHIPS = 4
N_DEV = 8

BIG = {"even_w_in": 1, "s5_w_glu": 0, "even_w_out": 0, "odd_w_in": 0, "mla_w_uq": 1, "mla_w_ukv": 1, "odd_w_out": 0,
       "ffn_w_in": 2, "ffn_w_out": 1}
LAYERED = ("ffn_w_in", "ffn_w_out")
GROUPS = {"even": ("even_w_in", "s5_w_glu", "even_w_out"), "ffn0": LAYERED,
          "odd": ("odd_w_in", "mla_w_uq", "mla_w_ukv", "odd_w_out"), "ffn1": LAYERED}
GROUP_LAYER = {"ffn0": 0, "ffn1": 1}


def _place():
    x, y, c = lax.axis_index("x"), lax.axis_index("y"), lax.axis_index("c")
    chips = [(1 - x, y), (x, 1 - y), (1 - x, 1 - y)]
    return x, y, c, chips


def _slab(ref, axis, k, size):
    start = pl.multiple_of(k * size, size if axis == 0 else LANES)
    idx = [slice(None)] * len(ref.shape)
    idx[axis] = pl.ds(start, size)
    return ref.at[tuple(idx)]


SEM_SPEC = pl.BlockSpec(memory_space=pltpu.SEMAPHORE)
ANY_SPEC = pl.BlockSpec(memory_space=pl.ANY)
EFFECT = pltpu.SideEffectType.DATAFLOW_SIDE_EFFECTING


def _hbm(a):
    return pltpu.with_memory_space_constraint(a, pltpu.HBM)


class _Gather:
    copies = 3

    def __init__(self, axis, size):
        self.axis, self.size = axis, size

    def view(self, land, kk):
        return land.at[kk] if self.axis is None else _slab(land, self.axis, kk, self.size)

    def own(self, land, place):
        return self.view(land, 2 * place[0] + place[1])

    def sends(self, src, land, place):
        x, y, c, chips = place
        return [(src, self.own(land, place), (*chip, c)) for chip in chips]

    def recvs(self, land, place):
        return [self.view(land, 2 * cx + cy) for cx, cy in place[3]]


class _Scatter:
    copies = 3

    def __init__(self, axis, size, layer=None):
        self.axis, self.size, self.layer = axis, size, layer

    def row(self, land, j):
        return land.at[j] if self.layer is None else land.at[j, self.layer]

    def sends(self, src, land, place):
        c, chips = place[2], place[3]
        return [(_slab(src, self.axis, 2 * cx + cy, self.size), self.row(land, j), (cx, cy, c))
                for j, (cx, cy) in enumerate(chips)]

    def recvs(self, land, place):
        return [self.row(land, j) for j in range(3)]


class _ToAll:
    copies = N_DEV - 1

    def own(self, land, place):
        x, y, c, _ = place
        return land.at[4 * x + 2 * y + c]

    def sends(self, src, land, place):
        x, y, c, _ = place
        flip = lambda v, bit: 1 - v if bit else v
        return [(src, self.own(land, place), (flip(x, m & 4), flip(y, m & 2), flip(c, m & 1))) for m in range(1, N_DEV)]

    def recvs(self, land, place):
        x, y, c, _ = place
        d = 4 * x + 2 * y + c
        return [land.at[d ^ m] for m in range(1, N_DEV)]


def _sem_base(routes):
    base = [0]
    for r in routes:
        base.append(base[-1] + r.copies)
    return base


def _push_start(name, items):
    n = len(items)
    base = _sem_base([it[0] for it in items])

    def body(*refs):
        srcs, lands, send, recv, token = refs[:n], refs[n:2 * n], refs[2 * n], refs[2 * n + 1], refs[-1]
        place = _place()
        for i, (route, _, _) in enumerate(items):
            for j, (s, d, dev) in enumerate(route.sends(srcs[i], lands[i], place)):
                pltpu.make_async_remote_copy(src_ref=s, dst_ref=d, send_sem=send.at[base[i] + j], recv_sem=recv.at[base[i] + j],
                                             device_id=dev, device_id_type=MESH).start()
        token[...] = jnp.zeros_like(token)

    arrays = [_hbm(it[1]) for it in items] + [_hbm(it[2]) for it in items]
    res = pl.pallas_call(
        body, name=name,
        out_shape=[pltpu.SemaphoreType.DMA((base[-1],)), pltpu.SemaphoreType.DMA((base[-1],))]
        + [pltpu.HBM(a.shape, a.dtype) for a in arrays] + [jax.ShapeDtypeStruct((SUBLANES, LANES), f32)],
        in_specs=[HBM_SPEC] * (2 * n), out_specs=[SEM_SPEC, SEM_SPEC] + [HBM_SPEC] * (2 * n) + [pl.BlockSpec(memory_space=pltpu.VMEM)],
        input_output_aliases={i: 2 + i for i in range(2 * n)},
        compiler_params=pltpu.CompilerParams(has_side_effects=EFFECT),
    )(*arrays)
    return (res[0], res[1]), list(res[2:2 + n]), list(res[2 + n:2 + 2 * n]), res[-1]


def _push_wait(name, groups, after):
    arrays, index = [], {}
    for routes, _, srcs, lands in groups:
        for a in list(srcs) + list(lands):
            if id(a) not in index:
                index[id(a)] = len(arrays)
                arrays.append(a)
    na, ng = len(arrays), len(groups)

    def body(*refs):
        arr, sems = refs[:na], refs[na:na + 2 * ng]
        place = _place()
        for g, (routes, _, srcs, lands) in enumerate(groups):
            send, recv = sems[2 * g], sems[2 * g + 1]
            base = _sem_base(routes)
            for i, route in enumerate(routes):
                src, land = arr[index[id(srcs[i])]], arr[index[id(lands[i])]]
                for j, ((s, d, dev), mine) in enumerate(zip(route.sends(src, land, place), route.recvs(land, place))):
                    cp = pltpu.make_async_remote_copy(src_ref=s, dst_ref=mine, send_sem=send.at[base[i] + j],
                                                      recv_sem=recv.at[base[i] + j], device_id=dev,
                                                      device_id_type=MESH)
                    cp.wait_send()
                    cp.wait_recv()

    sem_args = [s for g in groups for s in g[1]]
    res = pl.pallas_call(
        body, name=name, out_shape=[pltpu.HBM(a.shape, a.dtype) for a in arrays],
        in_specs=[HBM_SPEC] * na + [SEM_SPEC] * (2 * ng) + [ANY_SPEC], out_specs=[HBM_SPEC] * na,
        input_output_aliases={i: i for i in range(na)},
        compiler_params=pltpu.CompilerParams(has_side_effects=EFFECT),
    )(*arrays, *sem_args, after)
    return [[res[index[id(a)]] for a in g[3]] for g in groups]


def _place_own(items):
    n = len(items)

    def body(*refs):
        ins, outs, sem = refs[:n], refs[n:2 * n], refs[-1]
        place = _place()
        cps = [pltpu.make_async_copy(ins[i], items[i][0].own(outs[i], place), sem.at[i]) for i in range(n)]
        for cp in cps:
            cp.start()
        for cp in cps:
            cp.wait()

    return pl.pallas_call(
        body, name="place_own", in_specs=[HBM_SPEC] * n, out_specs=[HBM_SPEC] * n,
        out_shape=[jax.ShapeDtypeStruct(it[2], it[1].dtype) for it in items],
        scratch_shapes=[pltpu.SemaphoreType.DMA((n,))],
    )(*[it[1] for it in items])


def _swap_with_sibling(parts, tag):
    names = list(parts)

    def body(*refs):
        n = len(names)
        ins, outs, send, recv = refs[:n], refs[n:2 * n], refs[-2], refs[-1]
        x, y, c, _ = _place()
        cps = [pltpu.make_async_remote_copy(src_ref=ins[a], dst_ref=outs[a], send_sem=send.at[a], recv_sem=recv.at[a],
                                            device_id=(x, y, 1 - c), device_id_type=MESH) for a in range(n)]
        for cp in cps:
            cp.start()
        for cp in cps:
            cp.wait_recv()
        for cp in cps:
            cp.wait_send()

    res = pl.pallas_call(
        body, name=f"swap_with_sibling_{tag}", in_specs=[HBM_SPEC] * len(names), out_specs=[HBM_SPEC] * len(names),
        out_shape=[jax.ShapeDtypeStruct(parts[n].shape, parts[n].dtype) for n in names],
        scratch_shapes=[pltpu.SemaphoreType.DMA((len(names),)), pltpu.SemaphoreType.DMA((len(names),))],
    )(*[parts[n] for n in names])
    return dict(zip(names, res))


ELEMENTWISE_BLOCK_BYTES = 1 << 20


def _rows(r, c):
    for t in (512, 256, 128, 64, 32, 16, 8):
        if r % t == 0 and t * c * 4 <= ELEMENTWISE_BLOCK_BYTES:
            return t
    return r


def _sum4(owns, axis, recv, kidx, *, name):
    L = len(owns)
    R, C = recv.shape[2:]
    tm = _rows(R, C)
    nr = R // tm

    def body(k_ref, *refs):
        own_refs, r_ref, out_ref = refs[:L], refs[L], refs[L + 1]
        for li in range(L):
            @pl.when(pl.program_id(0) == li)
            def _(o_ref=own_refs[li]):
                out_ref[...] = ((o_ref[...] + r_ref[0, 0].astype(f32)) + r_ref[1, 0].astype(f32)) + r_ref[2, 0].astype(f32)

    own_map = (lambda l, i, k: (i, k[0])) if axis == 1 else (lambda l, i, k: (k[0] * nr + i, 0))
    return pl.pallas_call(
        body, name=name, out_shape=jax.ShapeDtypeStruct((L * R, C), f32),
        grid_spec=pltpu.PrefetchScalarGridSpec(
            num_scalar_prefetch=1, grid=(L, nr),
            in_specs=[pl.BlockSpec((tm, C), own_map)] * L + [pl.BlockSpec((3, 1, tm, C), lambda l, i, k: (0, l, i, 0))],
            out_specs=pl.BlockSpec((tm, C), lambda l, i, k: (l * nr + i, 0))),
        compiler_params=_cp("parallel", "parallel"),
    )(kidx, *owns, recv)


def _adamw(w, m, v, parts, *, name):
    R, C = w.shape
    tm = _rows(R, C)
    npart = len(parts)

    def body(*refs):
        w_ref, m_ref, v_ref = refs[:3]
        g_ref, d_ref, m2_ref, v2_ref = refs[3 + npart:]
        g = refs[3][...]
        for p_ref in refs[4:3 + npart]:
            g = g + p_ref[...]
        m2 = ADAM_B1 * m_ref[...] + (1.0 - ADAM_B1) * g
        v2 = ADAM_B2 * v_ref[...] + (1.0 - ADAM_B2) * (g * g)
        m_hat = m2 / (1.0 - ADAM_B1 ** ADAM_STEP)
        v_hat = v2 / (1.0 - ADAM_B2 ** ADAM_STEP)
        g_ref[...] = g
        d_ref[...] = -ADAM_LR * (m_hat / (jnp.sqrt(v_hat) + ADAM_EPS) + ADAM_WD * w_ref[...])
        m2_ref[...] = m2
        v2_ref[...] = v2

    blk = pl.BlockSpec((tm, C), lambda i: (i, 0))
    return pl.pallas_call(
        body, name=name, grid=(R // tm,),
        in_specs=[blk] * (3 + npart), out_specs=[blk] * 4,
        out_shape=[jax.ShapeDtypeStruct((R, C), f32)] * 4, compiler_params=_cp("parallel"),
    )(w, m, v, *parts)


def _pad_odd(w):
    return jnp.pad(w, ((0, 0), (0, ODD_PAD - w.shape[1])))


def _uq_cat(w):
    r = w.shape[0]
    return jnp.pad(w.reshape(r, MLA_HEADS, MLA_QK), ((0, 0), (0, 0), (0, HQ - MLA_QK))).reshape(r, MLA_HEADS * HQ)


def _uq_uncat(w):
    r = w.shape[0]
    return w.reshape(r, MLA_HEADS, HQ)[:, :, :MLA_QK].reshape(r, MLA_HEADS * MLA_QK)


def _to_segments(v):
    T, C = v.shape
    return v.reshape(S5_SEG, T // S5_SEG, C).transpose(1, 0, 2).reshape(T, C)


def _from_segments(v):
    T, C = v.shape
    return v.reshape(T // S5_SEG, S5_SEG, C).transpose(1, 0, 2).reshape(T, C)


def _s5_rb(T):
    return min(512, T)


def _ffn_fwd(h, g, w_in, cw, cb, w_out, tag):
    hn = _rms_fwd(h, g, name=f"ffn{tag}_norm")
    au = _mm(hn, w_in, name=f"ffn{tag}_in", tn=1408)
    z = _ffn_mid_fwd(au, cw, cb, name=f"ffn{tag}_mid")
    return _mm(z, w_out, res=h, name=f"ffn{tag}_out", tk=1408), (hn, au, z)


def _ffn_bwd(h, g, w_in, cw, cb, w_out, saved, dh, tag):
    hn, au, z = saved
    dz = _mm(dh, w_out, tb=True, name=f"ffn{tag}_dz", tn=1408)
    dw_out = _mm(z, dh, ta=True, also_bf16=True, name=f"ffn{tag}_dwout", tm=1408)
    dau, dcw, dcb = _ffn_mid_bwd(au, cw, cb, dz, name=f"ffn{tag}_dmid")
    dhn = _mm(dau, w_in, tb=True, name=f"ffn{tag}_dhn", tk=1408)
    dw_in = _mm(hn, dau, ta=True, also_bf16=True, name=f"ffn{tag}_dwin", tn=1408)
    dh_in, dg = _rms_bwd(h, g, dhn, dh, name=f"ffn{tag}_dnorm")
    return dh_in, dg, dw_in, dcw, dcb, dw_out


def _local_step(x, positions, target, get_w, P, put_g):
    T = x.shape[0]
    rb = _s5_rb(T)
    row = lambda v: v.reshape(1, -1)
    g_mix, g_ffn = P["norm_mix_g"], P["norm_ffn_g"]
    lbl, hng = P["hgrn_lb_logits"], P["hgrn_norm_g"]
    dsk, bg = P["s5_d"], P["s5_b_glu"]
    qg, kvg = P["mla_q_norm_g"], P["mla_kv_norm_g"]
    cw, cb = P["ffn_conv_w"], P["ffn_conv_b"]

    col = lambda v: v.reshape(S5_N, 1)
    disc_in = (col(P["s5_a_re"]), col(P["s5_a_im"]), col(jnp.repeat(P["s5_log_dt"].reshape(S5_GROUPS), S5_STATE)),
               P["s5_b_re"].reshape(S5_N, S5_GROUP), P["s5_b_im"].reshape(S5_N, S5_GROUP))
    abr, abi, bbr, bbi = _s5_disc_fwd(*disc_in)
    ar, ai = abr.reshape(1, S5_N), abi.reshape(1, S5_N)
    bbr3, bbi3 = bbr.reshape(S5_GROUPS, S5_STATE, S5_GROUP), bbi.reshape(S5_GROUPS, S5_STATE, S5_GROUP)
    bre, bim = _blockdiag(bbr3, True).astype(bf16), _blockdiag(bbi3, True).astype(bf16)
    bret, bimt = _blockdiag(bbr3).astype(bf16), _blockdiag(bbi3).astype(bf16)
    c_re, c_im = P["s5_c_re"].reshape(S5_GROUPS, S5_GROUP, S5_STATE), P["s5_c_im"].reshape(S5_GROUPS, S5_GROUP, S5_STATE)
    cre, cim = _blockdiag(c_re, True).astype(bf16), _blockdiag(c_im, True).astype(bf16)
    cret, cimt = _blockdiag(c_re).astype(bf16), _blockdiag(c_im).astype(bf16)

    hn0 = _rms_fwd(x, g_mix[0:1], name="mix0_norm")
    We = get_w("even", hn0)
    proj_e = _mm(hn0, We["even_w_in"], name="even_in", tn=1280)
    ya, states = _hgrn_fwd(proj_e, lbl, hng)
    u_seg = _to_segments(proj_e[:, 4 * 512:])
    fr, fi = _s5_final(u_seg, bre, bim, ar, ai, rb=rb)
    yb_seg, s0r, s0i = _s5_fwd(u_seg, bre, bim, ar, ai, fr, fi, cre, cim, dsk, We["s5_w_glu"], bg, rb=rb)
    ycat = jnp.concatenate([ya, _from_segments(yb_seg)], axis=1)
    h1 = _mm(ycat, We["even_w_out"], res=x, name="even_out")
    Wf0 = get_w("ffn0", h1)
    h2, ffn0 = _ffn_fwd(h1, g_ffn[0:1], Wf0["ffn_w_in"], cw[0], cb[0:1], Wf0["ffn_w_out"], 0)

    tabs = _rope_tables(positions)
    hn2 = _rms_fwd(h2, g_mix[1:2], name="mix1_norm")
    Wo = get_w("odd", hn2)
    proj_o = _mm(hn2, Wo["odd_w_in"], name="odd_in")
    cqn, ckvn, kr = _mla_prep_fwd(proj_o, qg, kvg, tabs)
    q = _q_post(_mm(cqn, Wo["mla_w_uq"], name="mla_uq"), tabs, transpose=False, name="q_post")
    kvb = _mm(ckvn, Wo["mla_w_ukv"], out_dtype=bf16, name="mla_ukv")
    o, lse = _flash_fwd(q, kvb, kr)
    h3 = _mm(o, Wo["odd_w_out"], res=h2, name="odd_out")
    Wf1 = get_w("ffn1", h3)
    h4, ffn1 = _ffn_fwd(h3, g_ffn[1:2], Wf1["ffn_w_in"], cw[1], cb[1:2], Wf1["ffn_w_out"], 1)
    loss, dh4, dg_final = _loss_head(h4, row(P["final_norm_g"]), target)

    dh3, dg_ffn1, dw_fin1, dcw1, dcb1, dw_fout1 = _ffn_bwd(
        h3, g_ffn[1:2], Wf1["ffn_w_in"], cw[1], cb[1:2], Wf1["ffn_w_out"], ffn1, dh4, 1)
    put_g("ffn1", {"ffn_w_in": dw_fin1, "ffn_w_out": dw_fout1})
    do = _mm(dh3, Wo["odd_w_out"], tb=True, name="odd_do")
    dw_oout = _mm(o, dh3, ta=True, also_bf16=True, name="odd_dwout")
    delta = _attn_delta(o, do)
    dkv, dkr_h = _flash_bwd_kv(q, kvb, kr, do, lse, delta)
    dq = _q_post(_flash_bwd_q(q, kvb, kr, do, lse, delta), tabs, transpose=True, name="dq_post")
    dw_uq = _mm(cqn, dq, ta=True, also_bf16=True, name="mla_dwuq")
    dcqn = _mm(dq, Wo["mla_w_uq"], tb=True, name="mla_dcq")
    dw_ukv = _mm(ckvn, dkv, ta=True, also_bf16=True, name="mla_dwukv")
    dckvn = _mm(dkv, Wo["mla_w_ukv"], tb=True, name="mla_dckv")
    dproj_o, dqg, dkvg = _mla_prep_bwd(proj_o, qg, kvg, tabs, dcqn, dckvn, dkr_h)
    dhn2 = _mm(dproj_o, Wo["odd_w_in"], tb=True, name="odd_dhn")
    dw_oin = _mm(hn2, dproj_o, ta=True, also_bf16=True, name="odd_dwin")
    put_g("odd", {"odd_w_in": dw_oin, "mla_w_uq": dw_uq, "mla_w_ukv": dw_ukv, "odd_w_out": dw_oout})
    dh2, dg_mix1 = _rms_bwd(h2, g_mix[1:2], dhn2, dh3, name="mix1_dnorm")

    dh1, dg_ffn0, dw_fin0, dcw0, dcb0, dw_fout0 = _ffn_bwd(
        h1, g_ffn[0:1], Wf0["ffn_w_in"], cw[0], cb[0:1], Wf0["ffn_w_out"], ffn0, dh2, 0)
    put_g("ffn0", {"ffn_w_in": dw_fin0, "ffn_w_out": dw_fout0})
    dycat = _mm(dh1, We["even_w_out"], tb=True, name="even_dy")
    dw_eout = _mm(ycat, dh1, ta=True, also_bf16=True, name="even_dwout")
    dq_h, df_h, di_h, dg_h, dlbl, dhng = _hgrn_bwd(proj_e, lbl, hng, states, dycat)
    dyb_seg = _to_segments(dycat[:, 512:])
    dy_s5, glr, gli, dcre, dcim, dd, dwg, dbg = _s5_bwd_a(
        u_seg, bre, bim, ar, ai, s0r, s0i, cre, cim, cret, cimt, dsk, We["s5_w_glu"], bg, dyb_seg, rb=rb)
    du_seg, dbre, dbim, dar, dai = _s5_bwd_b(
        u_seg, bre, bim, bret, bimt, ar, ai, s0r, s0i, glr, gli, cret, cimt, dsk, dy_s5, rb=rb)
    dproj_e = jnp.concatenate([dq_h, df_h, di_h, dg_h, _from_segments(du_seg)], axis=1)
    dhn0 = _mm(dproj_e, We["even_w_in"], tb=True, name="even_dhn", tk=1280)
    dw_ein = _mm(hn0, dproj_e, ta=True, also_bf16=True, name="even_dwin", tn=1280)
    dx, dg_mix0 = _rms_bwd(x, g_mix[0:1], dhn0, dh1, name="mix0_dnorm")

    unblk = lambda m, a, b: jnp.swapaxes(_blockdiag_t(m, a, b), 1, 2)
    dbbr = unblk(dbre, S5_GROUP, S5_STATE).reshape(S5_N, S5_GROUP)
    dbbi = unblk(dbim, S5_GROUP, S5_STATE).reshape(S5_N, S5_GROUP)
    d_ar, d_ai, d_ldt, d_br, d_bi = _s5_disc_bwd(*disc_in, (dar.reshape(S5_N, 1), dai.reshape(S5_N, 1), dbbr, dbbi))
    small = {
        "norm_mix_g": jnp.concatenate([dg_mix0, dg_mix1], axis=0),
        "norm_ffn_g": jnp.concatenate([dg_ffn0, dg_ffn1], axis=0),
        "final_norm_g": dg_final.reshape(-1),
        "hgrn_lb_logits": dlbl, "hgrn_norm_g": dhng,
        "s5_a_re": d_ar.reshape(1, S5_GROUPS, S5_STATE), "s5_a_im": d_ai.reshape(1, S5_GROUPS, S5_STATE),
        "s5_log_dt": d_ldt.reshape(S5_GROUPS, S5_STATE).sum(axis=1).reshape(1, S5_GROUPS),
        "s5_b_re": d_br.reshape(1, S5_GROUPS, S5_STATE, S5_GROUP), "s5_b_im": d_bi.reshape(1, S5_GROUPS, S5_STATE, S5_GROUP),
        "s5_c_re": unblk(dcre, S5_STATE, S5_GROUP).reshape(1, S5_GROUPS, S5_GROUP, S5_STATE),
        "s5_c_im": unblk(dcim, S5_STATE, S5_GROUP).reshape(1, S5_GROUPS, S5_GROUP, S5_STATE),
        "s5_d": dd, "s5_b_glu": dbg, "mla_q_norm_g": dqg, "mla_kv_norm_g": dkvg,
        "ffn_conv_w": jnp.stack([dcw0, dcw1]), "ffn_conv_b": jnp.concatenate([dcb0, dcb1], axis=0),
    }
    put_g("even", {"even_w_in": dw_ein, "s5_w_glu": (dwg, dwg.astype(bf16)), "even_w_out": dw_eout}, small)
    return loss, dx


WEIGHTS = ["norm_mix_g", "norm_ffn_g", "final_norm_g", "even_w_in", "hgrn_lb_logits", "hgrn_norm_g", "s5_a_re", "s5_a_im",
           "s5_log_dt", "s5_b_re", "s5_b_im", "s5_c_re", "s5_c_im", "s5_d", "s5_w_glu", "s5_b_glu", "even_w_out", "odd_w_in",
           "mla_q_norm_g", "mla_w_uq", "mla_kv_norm_g", "mla_w_ukv", "odd_w_out", "ffn_w_in", "ffn_conv_w", "ffn_conv_b",
           "ffn_w_out"]
SMALL_SHARDED = {"mla_q_norm_g": 1, "mla_kv_norm_g": 1, "ffn_conv_w": 2}
SMALL = [n for n in WEIGHTS if n not in BIG]


def _pack(arrays):
    flat = jnp.concatenate([a.reshape(-1) for a in arrays])
    n = flat.shape[0]
    tile = SUBLANES * LANES
    return jnp.pad(flat, (0, -n % tile)).reshape(-1, LANES)


def _unpack(block, shapes):
    flat, out, off = block.reshape(-1), [], 0
    for s in shapes:
        n = math.prod(s)
        out.append(flat[off:off + n].reshape(s))
        off += n
    return out


def kernel(x, positions, norm_mix_g, norm_ffn_g, final_norm_g, even_w_in, hgrn_lb_logits, hgrn_norm_g, s5_a_re, s5_a_im, s5_log_dt, s5_b_re, s5_b_im, s5_c_re, s5_c_im, s5_d, s5_w_glu, s5_b_glu, even_w_out, odd_w_in, mla_q_norm_g, mla_w_uq, mla_kv_norm_g, mla_w_ukv, odd_w_out, ffn_w_in, ffn_conv_w, ffn_conv_b, ffn_w_out, loss_target, m_norm_mix_g, m_norm_ffn_g, m_final_norm_g, m_even_w_in, m_hgrn_lb_logits, m_hgrn_norm_g, m_s5_a_re, m_s5_a_im, m_s5_log_dt, m_s5_b_re, m_s5_b_im, m_s5_c_re, m_s5_c_im, m_s5_d, m_s5_w_glu, m_s5_b_glu, m_even_w_out, m_odd_w_in, m_mla_q_norm_g, m_mla_w_uq, m_mla_kv_norm_g, m_mla_w_ukv, m_odd_w_out, m_ffn_w_in, m_ffn_conv_w, m_ffn_conv_b, m_ffn_w_out, v_norm_mix_g, v_norm_ffn_g, v_final_norm_g, v_even_w_in, v_hgrn_lb_logits, v_hgrn_norm_g, v_s5_a_re, v_s5_a_im, v_s5_log_dt, v_s5_b_re, v_s5_b_im, v_s5_c_re, v_s5_c_im, v_s5_d, v_s5_w_glu, v_s5_b_glu, v_even_w_out, v_odd_w_in, v_mla_q_norm_g, v_mla_w_uq, v_mla_kv_norm_g, v_mla_w_ukv, v_odd_w_out, v_ffn_w_in, v_ffn_conv_w, v_ffn_conv_b, v_ffn_w_out):
    args = dict(locals())
    w = {n: args[n] for n in WEIGHTS}
    m = {n: args["m_" + n] for n in WEIGHTS}
    v = {n: args["v_" + n] for n in WEIGHTS}
    k = 2 * lax.axis_index("x") + lax.axis_index("y")
    kidx = k.reshape(1).astype(jnp.int32)
    axis2d = lambda n: BIG[n] - (1 if n in LAYERED else 0)
    slab = lambda n: w[n].shape[1 + axis2d(n)]

    small_sh_shapes = [w[n].shape for n in SMALL_SHARDED]
    gathers, tokens = {}, []
    for group, names in GROUPS.items():
        layer = GROUP_LAYER.get(group, 0)
        items = [(_Gather(axis2d(n), slab(n)), w[n][layer].astype(bf16)) for n in names]
        if group == "even":
            items.append((_Gather(None, None), _pack([w[n] for n in SMALL_SHARDED])))
        shapes = [(N_CHIPS,) + b.shape if r.axis is None else
                  tuple(d * (N_CHIPS if a == r.axis else 1) for a, d in enumerate(b.shape)) for r, b in items]
        lands = _place_own([(r, b, s) for (r, b), s in zip(items, shapes)])
        sems, srcs, lands, token = _push_start(f"gather_start_{group}", [(r, b, l) for (r, b), l in zip(items, lands)])
        gathers[group] = ([r for r, _ in items], sems, srcs, lands)
        tokens.append(token[0, 0])
    started = functools.reduce(jnp.add, tokens)

    def landed(group, after):
        return _push_wait(f"gather_wait_{group}", [gathers[group]], after)[0]

    even = landed("even", (started + norm_mix_g[0, 0]).reshape(1))
    per_chip = [_unpack(even[-1][c], small_sh_shapes) for c in range(N_CHIPS)]
    P = {n: w[n] for n in SMALL if n not in SMALL_SHARDED}
    for i, (n, ax) in enumerate(SMALL_SHARDED.items()):
        P[n] = jnp.concatenate([per_chip[c][i] for c in range(N_CHIPS)], axis=ax)
    P["mla_q_norm_g"], P["mla_kv_norm_g"] = P["mla_q_norm_g"].reshape(1, -1), P["mla_kv_norm_g"].reshape(1, -1)
    fix_w = {"odd_w_in": _pad_odd, "mla_w_uq": _uq_cat}

    def get_w(group, after):
        full = even if group == "even" else landed(group, after)
        return {n: fix_w.get(n, lambda a: a)(a) for n, a in zip(GROUPS[group], full)}

    fix_g = {"odd_w_in": lambda g: g[:, :odd_w_in.shape[2]], "mla_w_uq": _uq_uncat}
    g32, scatters, land_now, small_shapes = {}, {}, {}, []

    def put_g(group, grads, small=None):
        layer = GROUP_LAYER.get(group)
        routes, srcs, names = [], [], list(grads)
        for n in names:
            f = fix_g.get(n, lambda g: g)
            g32.setdefault(n, {})[layer or 0] = f(grads[n][0])
            routes.append(_Scatter(axis2d(n), slab(n), layer if n in LAYERED else None))
            srcs.append(f(grads[n][1]))
            if n not in land_now:
                land_now[n] = lax.empty((3,) + w[n].shape[0 if n in LAYERED else 1:], bf16)
        if small is not None:
            full = [small[n].reshape(tuple(d * (N_CHIPS if a == SMALL_SHARDED.get(n, -1) else 1)
                                           for a, d in enumerate(w[n].shape))) for n in SMALL]
            small_shapes.extend(a.shape for a in full)
            vec = _pack(full)
            names.append("small")
            routes.append(_ToAll())
            srcs.append(vec)
            land_now["small"] = _place_own([(routes[-1], vec, (N_DEV,) + vec.shape)])[0]
        sems, srcs, lands, _ = _push_start(f"scatter_start_{group}", [(r, s, land_now[n]) for r, s, n in zip(routes, srcs, names)])
        land_now.update(zip(names, lands))
        scatters[group] = (routes, sems, srcs, names)

    loss, dx = _local_step(x[0], positions[0], loss_target[0], get_w, P, put_g)
    loss = lax.psum(loss[0, 0], ("x", "y", "c"))

    out = {}

    def finish(tag, groups, after):
        waits = [(scatters[g][0], scatters[g][1], scatters[g][2], [land_now[n] for n in scatters[g][3]]) for g in groups]
        for g, lands in zip(groups, _push_wait(f"scatter_wait_{tag}", waits, after)):
            land_now.update(zip(scatters[g][3], lands))
        names = [n for n in dict.fromkeys(n for g in groups for n in scatters[g][3]) if n != "small"]
        part = {}
        for n in names:
            recv = land_now[n] if n in LAYERED else land_now[n][:, None]
            part[n] = _sum4([g32[n][l] for l in sorted(g32[n])], axis2d(n), recv, kidx, name=f"sum4_{n}")
        other = _swap_with_sibling(part, tag)
        for n in names:
            C = part[n].shape[-1]
            res = _adamw(w[n].reshape(-1, C), m[n].reshape(-1, C), v[n].reshape(-1, C), [part[n], other[n]], name=f"adamw_{n}")
            out[n] = [r.reshape(w[n].shape) for r in res]
        return res[0]

    last = finish("a", ["ffn1", "odd", "ffn0"], dx)
    finish("b", ["even"], last)

    small_all = land_now["small"]
    per_dev = [_unpack(small_all[d], small_shapes) for d in range(N_DEV)]
    parts = []
    for d in range(N_DEV):
        mine = []
        for i, n in enumerate(SMALL):
            g = per_dev[d][i]
            if n in SMALL_SHARDED:
                ax = SMALL_SHARDED[n]
                g = lax.dynamic_slice_in_dim(g, k * w[n].shape[ax], w[n].shape[ax], axis=ax)
            mine.append(g)
        parts.append(_pack(mine))
    res = _adamw(_pack([w[n] for n in SMALL]), _pack([m[n] for n in SMALL]), _pack([v[n] for n in SMALL]), parts,
                 name="adamw_small")
    unpacked = [_unpack(r, [w[n].shape for n in SMALL]) for r in res]
    for i, n in enumerate(SMALL):
        out[n] = [u[i] for u in unpacked]

    return (loss, dx[None], *[out[n][0] for n in WEIGHTS], *[out[n][1] for n in WEIGHTS],
            *[out[n][2] for n in WEIGHTS], *[out[n][3] for n in WEIGHTS])
```

```python
import functools
import math

import jax
import jax.numpy as jnp
from jax import lax
from jax.experimental import pallas as pl
from jax.experimental.pallas import tpu as pltpu

f32, bf16 = jnp.float32, jnp.bfloat16
EPS = 1e-6
LANES = 128
SUBLANES = 8
VMEM_BYTES = 48 * 1024 * 1024
HGRN_CHUNK = 64
HGRN_HEADS = 4
S5_GROUPS, S5_STATE, S5_GROUP = 32, 64, 16
S5_N = S5_GROUPS * S5_STATE
S5_SEG = SUBLANES
MLA_HEADS, MLA_NOPE, MLA_ROPE, MLA_V = 8, 128, 64, 128
MLA_QK = MLA_NOPE + MLA_ROPE
MLA_Q_RANK, MLA_KV_RANK = 384, 256
ROPE_THETA = 10000.0
D_FF = 2816
ADAM_LR, ADAM_B1, ADAM_B2, ADAM_EPS, ADAM_WD, ADAM_STEP = 0.001, 0.9, 0.999, 1e-08, 0.01, 10
MESH = pl.DeviceIdType.MESH
HI = lax.Precision.HIGHEST


def _cp(*dims):
    return pltpu.CompilerParams(dimension_semantics=dims if dims else None, vmem_limit_bytes=VMEM_BYTES)


def _tile(n, t):
    if n <= t:
        return n
    c = (t // LANES) * LANES
    while c >= LANES:
        if n % c == 0:
            return c
        c -= LANES
    return n


def _dot(a, b, dn=None, precision=None):
    if dn is None:
        dn = (((a.ndim - 1,), (0,)), ((), ()))
    return lax.dot_general(a, b, dn, preferred_element_type=f32, precision=precision)


NT = (((1,), (1,)), ((), ()))
TN = (((0,), (0,)), ((), ()))


def _bdot(a, b, dn=None):
    return _dot(a.astype(bf16), b.astype(bf16), dn)


def _mm(a, b, *, name, ta=False, tb=False, out_dtype=f32, res=None, also_bf16=False, tm=1024, tn=1024, tk=1024, dep=None):
    M, K = (a.shape[1], a.shape[0]) if ta else a.shape
    N = b.shape[0] if tb else b.shape[1]
    tm, tn, tk = _tile(M, tm), _tile(N, tn), _tile(K, tk)
    nk = K // tk
    dn = (((0 if ta else 1,), (1 if tb else 0,)), ((), ()))

    def body(*refs):
        a_ref, b_ref = refs[0], refs[1]
        r_ref = refs[2] if res is not None else None
        nin = 2 + (res is not None) + (dep is not None)
        outs = refs[nin:-1]
        acc = refs[-1]
        k = pl.program_id(2)
        p = _bdot(a_ref[...], b_ref[...], dn)

        @pl.when(k == 0)
        def _():
            acc[...] = p

        @pl.when(k > 0)
        def _():
            acc[...] += p

        @pl.when(k == nk - 1)
        def _():
            r = acc[...]
            if r_ref is not None:
                r = r + r_ref[...]
            outs[0][...] = r.astype(out_dtype)
            if also_bf16:
                outs[1][...] = r.astype(bf16)

    a_spec = pl.BlockSpec((tk, tm), lambda i, j, k: (k, i)) if ta else pl.BlockSpec((tm, tk), lambda i, j, k: (i, k))
    b_spec = pl.BlockSpec((tn, tk), lambda i, j, k: (j, k)) if tb else pl.BlockSpec((tk, tn), lambda i, j, k: (k, j))
    o_spec = pl.BlockSpec((tm, tn), lambda i, j, k: (i, j))
    in_specs, args = [a_spec, b_spec], [a, b]
    if res is not None:
        in_specs.append(o_spec)
        args.append(res)
    if dep is not None:
        in_specs.append(pl.BlockSpec(memory_space=pl.ANY))
        args.append(dep)
    out_shape = [jax.ShapeDtypeStruct((M, N), out_dtype)]
    out_specs = [o_spec]
    if also_bf16:
        out_shape.append(jax.ShapeDtypeStruct((M, N), bf16))
        out_specs.append(o_spec)
    out = pl.pallas_call(
        body, name=name, grid=(M // tm, N // tn, nk), in_specs=in_specs, out_specs=out_specs, out_shape=out_shape,
        scratch_shapes=[pltpu.VMEM((tm, tn), f32)], compiler_params=_cp("parallel", "parallel", "arbitrary"),
    )(*args)
    return out if also_bf16 else out[0]


def _rms_fwd(x, g, *, name, col=0, width=None, tm=512):
    T = x.shape[0]
    width = x.shape[1] if width is None else width
    tm = _tile(T, tm)

    def body(x_ref, g_ref, o_ref):
        xv = x_ref[...]
        r = lax.rsqrt(jnp.mean(xv * xv, axis=-1, keepdims=True) + EPS)
        o_ref[...] = (xv * r * g_ref[...]).astype(bf16)

    return pl.pallas_call(
        body, name=name, grid=(T // tm,),
        in_specs=[pl.BlockSpec((tm, width), lambda i: (i, col)), pl.BlockSpec((1, width), lambda i: (0, 0))],
        out_specs=pl.BlockSpec((tm, width), lambda i: (i, 0)), out_shape=jax.ShapeDtypeStruct((T, width), bf16),
        compiler_params=_cp("parallel"),
    )(x, g)


def _rms_bwd_math(xv, g, dy):
    r = lax.rsqrt(jnp.mean(xv * xv, axis=-1, keepdims=True) + EPS)
    xh = xv * r
    dxh = dy * g
    dx = r * (dxh - xh * jnp.mean(dxh * xh, axis=-1, keepdims=True))
    dg = jnp.sum(dy * xh, axis=0, keepdims=True)
    return dx, dg


def _rms_bwd(x, g, dy, res=None, *, name, tm=512):
    T, D = x.shape
    tm = _tile(T, tm)

    def body(*refs):
        x_ref, g_ref, dy_ref = refs[:3]
        r_ref = refs[3] if res is not None else None
        dx_ref, dg_ref = refs[-2:]
        dx, dg = _rms_bwd_math(x_ref[...], g_ref[...], dy_ref[...].astype(f32))
        if r_ref is not None:
            dx = dx + r_ref[...]
        dx_ref[...] = dx

        @pl.when(pl.program_id(0) == 0)
        def _():
            dg_ref[...] = dg

        @pl.when(pl.program_id(0) > 0)
        def _():
            dg_ref[...] += dg

    row = pl.BlockSpec((tm, D), lambda i: (i, 0))
    vec = pl.BlockSpec((1, D), lambda i: (0, 0))
    in_specs, args = [row, vec, row], [x, g, dy]
    if res is not None:
        in_specs.append(row)
        args.append(res)
    return pl.pallas_call(
        body, name=name, grid=(T // tm,), in_specs=in_specs, out_specs=[row, vec],
        out_shape=[jax.ShapeDtypeStruct((T, D), f32), jax.ShapeDtypeStruct((1, D), f32)],
        compiler_params=_cp("arbitrary"),
    )(*args)


def _loss_head(h, g, target, *, tm=512):
    T, D = h.shape
    tm = _tile(T, tm)

    def body(h_ref, g_ref, t_ref, loss_ref, dh_ref, dg_ref):
        hv, gv = h_ref[...], g_ref[...]
        r = lax.rsqrt(jnp.mean(hv * hv, axis=-1, keepdims=True) + EPS)
        e = hv * r * gv - t_ref[...]
        part = 0.5 * jnp.sum(jnp.mean(e * e, axis=-1, keepdims=True), axis=0, keepdims=True)
        dx, dg = _rms_bwd_math(hv, gv, e * (1.0 / D))
        dh_ref[...] = dx

        @pl.when(pl.program_id(0) == 0)
        def _():
            loss_ref[...] = part
            dg_ref[...] = dg

        @pl.when(pl.program_id(0) > 0)
        def _():
            loss_ref[...] += part
            dg_ref[...] += dg

    row = pl.BlockSpec((tm, D), lambda i: (i, 0))
    vec = pl.BlockSpec((1, D), lambda i: (0, 0))
    return pl.pallas_call(
        body, name="loss_head", grid=(T // tm,), in_specs=[row, vec, row],
        out_specs=[pl.BlockSpec((1, 1), lambda i: (0, 0)), row, vec],
        out_shape=[jax.ShapeDtypeStruct((1, 1), f32), jax.ShapeDtypeStruct((T, D), f32), jax.ShapeDtypeStruct((1, D), f32)],
        compiler_params=_cp("arbitrary"),
    )(h, g, target)


def _shift_down(v, k):
    rows = lax.broadcasted_iota(jnp.int32, v.shape, 0)
    return jnp.where(rows < k, 0.0, pltpu.roll(v, k, 0))


def _shift_up(v, k):
    n = v.shape[0]
    rows = lax.broadcasted_iota(jnp.int32, v.shape, 0)
    return jnp.where(rows >= n - k, 0.0, pltpu.roll(v, n - k, 0))


def _ffn_mid_fwd(au, cw, cb, *, name):
    T = au.shape[0]
    F = au.shape[1] // 2
    nb = F // LANES

    def body(a_ref, u_ref, w_ref, b_ref, z_ref):
        a = a_ref[...]
        w = w_ref[...]
        ac = w[0:1] * _shift_down(a, 2) + w[1:2] * _shift_down(a, 1) + w[2:3] * a + b_ref[...]
        z_ref[...] = (ac * jax.nn.sigmoid(ac) * u_ref[...]).astype(bf16)

    return pl.pallas_call(
        body, name=name, grid=(nb,),
        in_specs=[pl.BlockSpec((T, LANES), lambda j: (0, j)), pl.BlockSpec((T, LANES), lambda j: (0, nb + j)),
                  pl.BlockSpec((3, LANES), lambda j: (0, j)), pl.BlockSpec((1, LANES), lambda j: (0, j))],
        out_specs=pl.BlockSpec((T, LANES), lambda j: (0, j)), out_shape=jax.ShapeDtypeStruct((T, F), bf16),
        compiler_params=_cp("parallel"),
    )(au, au, cw, cb)


def _ffn_mid_bwd(au, cw, cb, dz, *, name):
    T = au.shape[0]
    F = au.shape[1] // 2
    nb = F // LANES

    def body(a_ref, u_ref, w_ref, b_ref, dz_ref, da_ref, du_ref, dw_ref, db_ref):
        a = a_ref[...]
        w = w_ref[...]
        a2, a1 = _shift_down(a, 2), _shift_down(a, 1)
        ac = w[0:1] * a2 + w[1:2] * a1 + w[2:3] * a + b_ref[...]
        sg = jax.nn.sigmoid(ac)
        dz = dz_ref[...].astype(f32)
        du_ref[...] = (dz * ac * sg).astype(bf16)
        dac = dz * u_ref[...] * sg * (1.0 + ac * (1.0 - sg))
        da_ref[...] = (w[2:3] * dac + w[1:2] * _shift_up(dac, 1) + w[0:1] * _shift_up(dac, 2)).astype(bf16)
        rows = lax.broadcasted_iota(jnp.int32, (3, LANES), 0)
        s0 = jnp.sum(dac * a2, axis=0, keepdims=True)
        s1 = jnp.sum(dac * a1, axis=0, keepdims=True)
        s2 = jnp.sum(dac * a, axis=0, keepdims=True)
        dw_ref[...] = jnp.where(rows == 0, s0, jnp.where(rows == 1, s1, s2))
        db_ref[...] = jnp.sum(dac, axis=0, keepdims=True)

    col = lambda off: pl.BlockSpec((T, LANES), lambda j: (0, off + j))
    da, du, dw, db = pl.pallas_call(
        body, name=name, grid=(nb,),
        in_specs=[col(0), col(nb), pl.BlockSpec((3, LANES), lambda j: (0, j)), pl.BlockSpec((1, LANES), lambda j: (0, j)), col(0)],
        out_specs=[col(0), col(0), pl.BlockSpec((3, LANES), lambda j: (0, j)), pl.BlockSpec((1, LANES), lambda j: (0, j))],
        out_shape=[jax.ShapeDtypeStruct((T, F), bf16), jax.ShapeDtypeStruct((T, F), bf16),
                   jax.ShapeDtypeStruct((3, F), f32), jax.ShapeDtypeStruct((1, F), f32)],
        compiler_params=_cp("parallel"),
    )(au, au, cw, cb, dz)
    return jnp.concatenate([da, du], axis=1), dw, db


def _hgrn_lb(l):
    m = jnp.max(l, axis=0, keepdims=True)
    e = jnp.exp(l - m)
    return e[0:1] / jnp.sum(e, axis=0, keepdims=True)


def _hgrn_chunk(q, fx, lb):
    C = q.shape[0]
    sg = jax.nn.sigmoid(fx)
    F = lb + (1.0 - lb) * sg
    k = 1.0 - F
    logF = jnp.log(F)
    r = lax.broadcasted_iota(jnp.int32, (C, C), 0)
    c = lax.broadcasted_iota(jnp.int32, (C, C), 1)
    tril = (r >= c)
    b = _dot(tril.astype(f32), logF, precision=HI)
    bl = jnp.sum(logF, axis=0, keepdims=True)
    eb = jnp.exp(b)
    enb = jnp.exp(-b)
    elb = jnp.exp(bl - b)
    return dict(sg=sg, F=F, k=k, b=b, bl=bl, eb=eb, enb=enb, elb=elb, qd=q * eb, kd=k * enb, kl=k * elb, tril=tril)


def _hgrn_fwd(proj, lbl, ng, *, rb=512):
    T = proj.shape[0]
    rb = min(rb, T)
    cpb = rb // HGRN_CHUNK
    nblk = T // rb
    H = HGRN_HEADS

    def body(q_ref, f_ref, i_ref, g_ref, lbl_ref, ng_ref, y_ref, st_ref, S):
        @pl.when(pl.program_id(1) == 0)
        def _():
            S[...] = jnp.zeros_like(S)

        lb = _hgrn_lb(lbl_ref[...])
        ngv = ng_ref[...]
        for c in range(cpb):
            sl = pl.ds(c * HGRN_CHUNK, HGRN_CHUNK)
            v, gx = i_ref[sl, :], g_ref[sl, :]
            ch = _hgrn_chunk(q_ref[sl, :], f_ref[sl, :], lb)
            att = jnp.where(ch["tril"], _bdot(ch["qd"], ch["kd"], NT), 0.0)
            St = S[...]
            st_ref[0, c] = St
            o = _bdot(att, v) + _bdot(ch["qd"], St, NT)
            S[...] = St * jnp.exp(ch["bl"]) + _bdot(v, ch["kl"], TN)
            r = lax.rsqrt(jnp.mean(o * o, axis=-1, keepdims=True) + EPS)
            y_ref[sl, :] = (o * r * ngv * (gx * jax.nn.sigmoid(gx))).astype(bf16)

    col = lambda off: pl.BlockSpec((rb, LANES), lambda h, n: (n, off + h))
    return pl.pallas_call(
        body, name="hgrn_fwd", grid=(H, nblk),
        in_specs=[col(0), col(H), col(2 * H), col(3 * H), pl.BlockSpec((2, LANES), lambda h, n: (0, h)),
                  pl.BlockSpec((1, LANES), lambda h, n: (0, h))],
        out_specs=[pl.BlockSpec((rb, LANES), lambda h, n: (n, h)),
                   pl.BlockSpec((1, cpb, LANES, LANES), lambda h, n: (h, n, 0, 0))],
        out_shape=[jax.ShapeDtypeStruct((T, H * LANES), bf16),
                   jax.ShapeDtypeStruct((H, T // HGRN_CHUNK, LANES, LANES), f32)],
        scratch_shapes=[pltpu.VMEM((LANES, LANES), f32)], compiler_params=_cp("parallel", "arbitrary"),
    )(proj, proj, proj, proj, lbl, ng)


def _hgrn_bwd(proj, lbl, ng, states, dy, *, rb=512):
    T = proj.shape[0]
    rb = min(rb, T)
    cpb = rb // HGRN_CHUNK
    nblk = T // rb
    H = HGRN_HEADS
    C = HGRN_CHUNK

    def body(q_ref, f_ref, i_ref, g_ref, lbl_ref, ng_ref, st_ref, dy_ref,
             dq_ref, df_ref, di_ref, dg_ref, dl_ref, dng_ref, dS, dlb_acc, dng_acc):
        n = pl.program_id(1)

        @pl.when(n == 0)
        def _():
            dS[...] = jnp.zeros_like(dS)
            dlb_acc[...] = jnp.zeros_like(dlb_acc)
            dng_acc[...] = jnp.zeros_like(dng_acc)

        lb = _hgrn_lb(lbl_ref[...])
        ngv = ng_ref[...]
        r_i = lax.broadcasted_iota(jnp.int32, (C, C), 0)
        c_i = lax.broadcasted_iota(jnp.int32, (C, C), 1)
        triu = (c_i >= r_i).astype(f32)
        for c in reversed(range(cpb)):
            sl = pl.ds(c * C, C)
            q, v, gx = q_ref[sl, :], i_ref[sl, :], g_ref[sl, :]
            ch = _hgrn_chunk(q, f_ref[sl, :], lb)
            qd, kd, kl = ch["qd"], ch["kd"], ch["kl"]
            att = jnp.where(ch["tril"], _bdot(qd, kd, NT), 0.0)
            St = st_ref[0, c]
            o = _bdot(att, v) + _bdot(qd, St, NT)
            r = lax.rsqrt(jnp.mean(o * o, axis=-1, keepdims=True) + EPS)
            on = o * r
            sgg = jax.nn.sigmoid(gx)
            gate = gx * sgg
            dyv = dy_ref[sl, :].astype(f32)
            dg_ref[sl, :] = (dyv * on * ngv * sgg * (1.0 + gx * (1.0 - sgg))).astype(bf16)
            dng_acc[...] += jnp.sum(dyv * on * gate, axis=0, keepdims=True)
            don = dyv * ngv * gate
            do = r * (don - on * jnp.mean(don * on, axis=-1, keepdims=True))
            dSt = dS[...]
            dA = jnp.where(ch["tril"], _bdot(do, v, NT), 0.0)
            dv = _bdot(att, do, TN) + _bdot(kl, dSt, NT)
            dqd = _bdot(dA, kd) + _bdot(do, St)
            dkd = _bdot(dA, qd, TN)
            dkl = _bdot(v, dSt)
            dec = jnp.exp(ch["bl"])
            ddec = jnp.sum(St * dSt, axis=0, keepdims=True)
            dS[...] = _bdot(do, qd, TN) + dSt * dec
            dB = dqd * qd - dkd * kd - dkl * kl
            dbl = jnp.sum(dkl * kl, axis=0, keepdims=True) + ddec * dec
            dk = dkd * ch["enb"] + dkl * ch["elb"]
            dlogF = _dot(triu, dB, precision=HI) + dbl
            dF = dlogF / ch["F"] - dk
            sg = ch["sg"]
            dq_ref[sl, :] = (dqd * ch["eb"]).astype(bf16)
            di_ref[sl, :] = dv.astype(bf16)
            df_ref[sl, :] = (dF * (1.0 - lb) * sg * (1.0 - sg)).astype(bf16)
            dlb_acc[...] += jnp.sum(dF * (1.0 - sg), axis=0, keepdims=True)

        @pl.when(n == nblk - 1)
        def _():
            dl0 = dlb_acc[...] * lb * (1.0 - lb)
            rows = lax.broadcasted_iota(jnp.int32, (2, LANES), 0)
            dl_ref[...] = jnp.where(rows == 0, dl0, -dl0)
            dng_ref[...] = dng_acc[...]

    col = lambda off: pl.BlockSpec((rb, LANES), lambda h, n: (nblk - 1 - n, off + h))
    vec = lambda rows: pl.BlockSpec((rows, LANES), lambda h, n: (0, h))
    outc = pl.BlockSpec((rb, LANES), lambda h, n: (nblk - 1 - n, h))
    tok = jax.ShapeDtypeStruct((T, H * LANES), bf16)
    return pl.pallas_call(
        body, name="hgrn_bwd", grid=(H, nblk),
        in_specs=[col(0), col(H), col(2 * H), col(3 * H), vec(2), vec(1),
                  pl.BlockSpec((1, cpb, LANES, LANES), lambda h, n: (h, nblk - 1 - n, 0, 0)), col(0)],
        out_specs=[outc, outc, outc, outc, vec(2), vec(1)],
        out_shape=[tok, tok, tok, tok, jax.ShapeDtypeStruct((2, H * LANES), f32), jax.ShapeDtypeStruct((1, H * LANES), f32)],
        scratch_shapes=[pltpu.VMEM((LANES, LANES), f32), pltpu.VMEM((1, LANES), f32), pltpu.VMEM((1, LANES), f32)],
        compiler_params=_cp("parallel", "arbitrary"),
    )(proj, proj, proj, proj, lbl, ng, states, dy)


def _s5_disc_math(ar, ai, ldt, br, bi):
    dt = jnp.exp(ldt)
    mag = jnp.exp(ar * dt)
    abr, abi = mag * jnp.cos(ai * dt), mag * jnp.sin(ai * dt)
    den = ar * ar + ai * ai
    xr, xi = abr - 1.0, abi
    cr = (xr * ar + xi * ai) / den
    ci = (xi * ar - xr * ai) / den
    return abr, abi, cr * br - ci * bi, cr * bi + ci * br


def _s5_disc_fwd(ar, ai, ldt, br, bi):
    def body(ar_ref, ai_ref, ldt_ref, br_ref, bi_ref, o0, o1, o2, o3):
        outs = _s5_disc_math(ar_ref[...], ai_ref[...], ldt_ref[...], br_ref[...], bi_ref[...])
        for o, v in zip((o0, o1, o2, o3), outs):
            o[...] = v

    return pl.pallas_call(
        body, name="s5_disc_fwd",
        out_shape=[jax.ShapeDtypeStruct(ar.shape, f32)] * 2 + [jax.ShapeDtypeStruct(br.shape, f32)] * 2,
    )(ar, ai, ldt, br, bi)


def _s5_disc_bwd(ar, ai, ldt, br, bi, cts):
    def body(ar_ref, ai_ref, ldt_ref, br_ref, bi_ref, c0, c1, c2, c3, o0, o1, o2, o3, o4):
        _, vjp = jax.vjp(_s5_disc_math, ar_ref[...], ai_ref[...], ldt_ref[...], br_ref[...], bi_ref[...])
        for o, v in zip((o0, o1, o2, o3, o4), vjp((c0[...], c1[...], c2[...], c3[...]))):
            o[...] = v

    return pl.pallas_call(
        body, name="s5_disc_bwd",
        out_shape=[jax.ShapeDtypeStruct(ar.shape, f32)] * 3 + [jax.ShapeDtypeStruct(br.shape, f32)] * 2,
    )(ar, ai, ldt, br, bi, *cts)


S5_LC = 512
S5_NLC = S5_N // S5_LC
S5_UB = 4


def _cmul(ar, ai, xr, xi):
    return ar * xr - ai * xi, ar * xi + ai * xr


def _cpow(ar, ai, n):
    rr, ri = None, None
    br, bi = ar, ai
    while n:
        if n & 1:
            rr, ri = (br, bi) if rr is None else _cmul(rr, ri, br, bi)
        n >>= 1
        if n:
            br, bi = _cmul(br, bi, br, bi)
    return rr, ri


def _s5_bu(u_ref, bre_ref, bim_ref, xr, xi):
    for k in range(S5_UB):
        uk = u_ref[:, k * LANES:(k + 1) * LANES].astype(bf16)
        xr[:, k * S5_LC:(k + 1) * S5_LC] = _dot(uk, bre_ref[k])
        xi[:, k * S5_LC:(k + 1) * S5_LC] = _dot(uk, bim_ref[k])


def _s5_scan(xr, xi, sr, si, ar_ref, ai_ref, nsteps, store):
    for c in range(S5_NLC):
        cs = slice(c * S5_LC, (c + 1) * S5_LC)
        a_r = jnp.broadcast_to(ar_ref[:, cs], (S5_SEG, S5_LC))
        a_i = jnp.broadcast_to(ai_ref[:, cs], (S5_SEG, S5_LC))

        def step(j, carry, cs=cs, a_r=a_r, a_i=a_i):
            pr, pi = carry
            rows = pl.ds(pl.multiple_of(j * S5_SEG, S5_SEG), S5_SEG)
            nr = a_r * pr - a_i * pi + xr[rows, cs]
            ni = a_r * pi + a_i * pr + xi[rows, cs]
            if store:
                xr[rows, cs] = nr
                xi[rows, cs] = ni
            return nr, ni

        fr, fi = lax.fori_loop(0, nsteps, step, (sr[:, cs], si[:, cs]))
        sr[:, cs] = fr
        si[:, cs] = fi


def _s5_rscan(dr, di, xr, xi, s0r, s0i, gr, gi, acc_r, acc_i, ar_ref, ai_ref, nsteps):
    for c in range(S5_NLC):
        cs = slice(c * S5_LC, (c + 1) * S5_LC)
        a_r = jnp.broadcast_to(ar_ref[:, cs], (S5_SEG, S5_LC))
        a_i = jnp.broadcast_to(ai_ref[:, cs], (S5_SEG, S5_LC))

        def step(jj, carry, cs=cs, a_r=a_r, a_i=a_i):
            pr, pi, cr, ci = carry
            j = nsteps - 1 - jj
            rows = pl.ds(pl.multiple_of(j * S5_SEG, S5_SEG), S5_SEG)
            nr = dr[rows, cs] + a_r * pr + a_i * pi
            ni = di[rows, cs] + a_r * pi - a_i * pr
            dr[rows, cs] = nr
            di[rows, cs] = ni
            if acc_r is not None:
                prev = pl.ds(pl.multiple_of(jnp.maximum(j - 1, 0) * S5_SEG, S5_SEG), S5_SEG)
                first = j == 0
                pr_s = jnp.where(first, s0r[:, cs], xr[prev, cs])
                pi_s = jnp.where(first, s0i[:, cs], xi[prev, cs])
                cr = cr + nr * pr_s + ni * pi_s
                ci = ci - nr * pi_s + ni * pr_s
            return nr, ni, cr, ci

        z = jnp.zeros((S5_SEG, S5_LC), f32)
        init = (gr[:, cs], gi[:, cs], z, z)
        fr, fi, cr, ci = lax.fori_loop(0, nsteps, step, init)
        gr[:, cs] = fr
        gi[:, cs] = fi
        if acc_r is not None:
            acc_r[:, cs] += cr
            acc_i[:, cs] += ci


def _s5_seg_carry(fr, fi, ar, ai, seg_len, reverse):
    pr, pi = _cpow(ar, ai if not reverse else -ai, seg_len)
    rows = lax.broadcasted_iota(jnp.int32, fr.shape, 0)
    cr, ci = jnp.zeros_like(fr), jnp.zeros_like(fi)
    sh = (S5_SEG - 1) if reverse else 1
    fr_s, fi_s = pltpu.roll(fr, sh, 0), pltpu.roll(fi, sh, 0)
    order = range(S5_SEG - 2, -1, -1) if reverse else range(1, S5_SEG)
    for r in order:
        c_r, c_i = pltpu.roll(cr, sh, 0), pltpu.roll(ci, sh, 0)
        m_r, m_i = _cmul(pr, pi, c_r, c_i)
        cr = jnp.where(rows == r, m_r + fr_s, cr)
        ci = jnp.where(rows == r, m_i + fi_s, ci)
    return cr, ci


def _gelu_parts(y):
    c0 = math.sqrt(2.0 / math.pi)
    t = jnp.tanh(c0 * (y + 0.044715 * y * y * y))
    z = 0.5 * y * (1.0 + t)
    dz = 0.5 * (1.0 + t) + 0.5 * y * (1.0 - t * t) * c0 * (1.0 + 3.0 * 0.044715 * y * y)
    return z, dz


def _s5_y(xr, xi, u_ref, cre_ref, cim_ref, d_ref):
    ys = []
    for k in range(S5_UB):
        cs = slice(k * S5_LC, (k + 1) * S5_LC)
        ys.append(_bdot(xr[:, cs], cre_ref[k]) - _bdot(xi[:, cs], cim_ref[k]))
    return jnp.concatenate(ys, axis=1) + d_ref[...] * u_ref[...]


def _s5_specs(T, rb, rev=False):
    nblk = T // rb
    blk = (lambda i: (nblk - 1 - i, 0)) if rev else (lambda i: (i, 0))
    tok = pl.BlockSpec((rb, 4 * LANES), blk)
    bmat = pl.BlockSpec((S5_UB, LANES, S5_LC), lambda i: (0, 0, 0))
    cmat = pl.BlockSpec((S5_UB, S5_LC, LANES), lambda i: (0, 0, 0))
    avec = pl.BlockSpec((1, S5_N), lambda i: (0, 0))
    seg = pl.BlockSpec((S5_SEG, S5_N), lambda i: (0, 0))
    cvec = pl.BlockSpec((1, 4 * LANES), lambda i: (0, 0))
    s0 = pl.BlockSpec((1, S5_SEG, S5_N), (lambda i: (nblk - 1 - i, 0, 0)) if rev else (lambda i: (i, 0, 0)))
    return dict(tok=tok, bmat=bmat, cmat=cmat, avec=avec, seg=seg, cvec=cvec, s0=s0, nblk=nblk)


def _s5_final(u, bre, bim, ar, ai, *, rb):
    T = u.shape[0]
    sp = _s5_specs(T, rb)

    def body(u_ref, bre_ref, bim_ref, ar_ref, ai_ref, fr_ref, fi_ref, xr, xi):
        @pl.when(pl.program_id(0) == 0)
        def _():
            fr_ref[...] = jnp.zeros_like(fr_ref)
            fi_ref[...] = jnp.zeros_like(fi_ref)

        _s5_bu(u_ref, bre_ref, bim_ref, xr, xi)
        _s5_scan(xr, xi, fr_ref, fi_ref, ar_ref, ai_ref, rb // S5_SEG, False)

    return pl.pallas_call(
        body, name="s5_final", grid=(sp["nblk"],),
        in_specs=[sp["tok"], sp["bmat"], sp["bmat"], sp["avec"], sp["avec"]], out_specs=[sp["seg"], sp["seg"]],
        out_shape=[jax.ShapeDtypeStruct((S5_SEG, S5_N), f32)] * 2,
        scratch_shapes=[pltpu.VMEM((rb, S5_N), f32)] * 2, compiler_params=_cp("arbitrary"),
    )(u, bre, bim, ar, ai)


def _s5_fwd(u, bre, bim, ar, ai, fr, fi, cre, cim, dsk, wg, bg, *, rb):
    T = u.shape[0]
    sp = _s5_specs(T, rb)
    seg_len = T // S5_SEG

    def body(u_ref, bre_ref, bim_ref, ar_ref, ai_ref, fr_ref, fi_ref, cre_ref, cim_ref, d_ref, wg_ref, bg_ref,
             o_ref, s0r_ref, s0i_ref, xr, xi, sr, si):
        @pl.when(pl.program_id(0) == 0)
        def _():
            i_r, i_i = _s5_seg_carry(fr_ref[...], fi_ref[...], ar_ref[...], ai_ref[...], seg_len, False)
            sr[...] = i_r
            si[...] = i_i

        s0r_ref[0] = sr[...]
        s0i_ref[0] = si[...]
        _s5_bu(u_ref, bre_ref, bim_ref, xr, xi)
        _s5_scan(xr, xi, sr, si, ar_ref, ai_ref, rb // S5_SEG, True)
        y = _s5_y(xr, xi, u_ref, cre_ref, cim_ref, d_ref)
        z, _ = _gelu_parts(y)
        v = _bdot(z, wg_ref[...]) + bg_ref[...]
        o_ref[...] = (z * jax.nn.sigmoid(v)).astype(bf16)

    wspec = pl.BlockSpec((4 * LANES, 4 * LANES), lambda i: (0, 0))
    return pl.pallas_call(
        body, name="s5_fwd", grid=(sp["nblk"],),
        in_specs=[sp["tok"], sp["bmat"], sp["bmat"], sp["avec"], sp["avec"], sp["seg"], sp["seg"], sp["cmat"], sp["cmat"],
                  sp["cvec"], wspec, sp["cvec"]],
        out_specs=[sp["tok"], sp["s0"], sp["s0"]],
        out_shape=[jax.ShapeDtypeStruct((T, 4 * LANES), bf16)] + [jax.ShapeDtypeStruct((sp["nblk"], S5_SEG, S5_N), f32)] * 2,
        scratch_shapes=[pltpu.VMEM((rb, S5_N), f32)] * 2 + [pltpu.VMEM((S5_SEG, S5_N), f32)] * 2,
        compiler_params=_cp("arbitrary"),
    )(u, bre, bim, ar, ai, fr, fi, cre, cim, dsk, wg, bg)


def _s5_bwd_a(u, bre, bim, ar, ai, s0r, s0i, cre, cim, cret, cimt, dsk, wg, bg, dout, *, rb):
    T = u.shape[0]
    sp = _s5_specs(T, rb, rev=True)

    def body(u_ref, bre_ref, bim_ref, ar_ref, ai_ref, s0r_ref, s0i_ref, cre_ref, cim_ref, cret_ref, cimt_ref,
             d_ref, wg_ref, bg_ref, do_ref, dy_ref, glr_ref, gli_ref, dcre_ref, dcim_ref, dd_ref, dwg_ref, dbg_ref,
             xr, xi, dr, di, sr, si):
        @pl.when(pl.program_id(0) == 0)
        def _():
            for r in (glr_ref, gli_ref, dcre_ref, dcim_ref, dd_ref, dwg_ref, dbg_ref):
                r[...] = jnp.zeros_like(r)

        sr[...] = s0r_ref[0]
        si[...] = s0i_ref[0]
        _s5_bu(u_ref, bre_ref, bim_ref, xr, xi)
        _s5_scan(xr, xi, sr, si, ar_ref, ai_ref, rb // S5_SEG, True)
        uv = u_ref[...]
        y = _s5_y(xr, xi, u_ref, cre_ref, cim_ref, d_ref)
        z, gz = _gelu_parts(y)
        v = _bdot(z, wg_ref[...]) + bg_ref[...]
        sg = jax.nn.sigmoid(v)
        dov = do_ref[...].astype(f32)
        dv = dov * z * sg * (1.0 - sg)
        dz = dov * sg + _bdot(dv, wg_ref[...], NT)
        dy = dz * gz
        dy_ref[...] = dy
        dwg_ref[...] += _bdot(z, dv, TN)
        dbg_ref[...] += jnp.sum(dv, axis=0, keepdims=True)
        dd_ref[...] += jnp.sum(dy * uv, axis=0, keepdims=True)
        for k in range(S5_UB):
            cs = slice(k * S5_LC, (k + 1) * S5_LC)
            dyk = dy[:, k * LANES:(k + 1) * LANES]
            dcre_ref[k] += _bdot(xr[:, cs], dyk, TN)
            dcim_ref[k] -= _bdot(xi[:, cs], dyk, TN)
            dr[:, cs] = _bdot(dyk, cret_ref[k])
            di[:, cs] = -_bdot(dyk, cimt_ref[k])
        _s5_rscan(dr, di, None, None, None, None, glr_ref, gli_ref, None, None, ar_ref, ai_ref, rb // S5_SEG)

    wspec = pl.BlockSpec((4 * LANES, 4 * LANES), lambda i: (0, 0))
    return pl.pallas_call(
        body, name="s5_bwd_a", grid=(sp["nblk"],),
        in_specs=[sp["tok"], sp["bmat"], sp["bmat"], sp["avec"], sp["avec"], sp["s0"], sp["s0"], sp["cmat"], sp["cmat"],
                  sp["bmat"], sp["bmat"], sp["cvec"], wspec, sp["cvec"], sp["tok"]],
        out_specs=[sp["tok"], sp["seg"], sp["seg"], sp["cmat"], sp["cmat"], sp["cvec"], wspec, sp["cvec"]],
        out_shape=[jax.ShapeDtypeStruct((T, 4 * LANES), f32)] + [jax.ShapeDtypeStruct((S5_SEG, S5_N), f32)] * 2
        + [jax.ShapeDtypeStruct((S5_UB, S5_LC, LANES), f32)] * 2
        + [jax.ShapeDtypeStruct((1, 4 * LANES), f32), jax.ShapeDtypeStruct((4 * LANES, 4 * LANES), f32),
           jax.ShapeDtypeStruct((1, 4 * LANES), f32)],
        scratch_shapes=[pltpu.VMEM((rb, S5_N), f32)] * 4 + [pltpu.VMEM((S5_SEG, S5_N), f32)] * 2,
        compiler_params=_cp("arbitrary"),
    )(u, bre, bim, ar, ai, s0r, s0i, cre, cim, cret, cimt, dsk, wg, bg, dout)


def _s5_bwd_b(u, bre, bim, bret, bimt, ar, ai, s0r, s0i, glr, gli, cret, cimt, dsk, dy, *, rb):
    T = u.shape[0]
    sp = _s5_specs(T, rb, rev=True)
    seg_len = T // S5_SEG
    nblk = sp["nblk"]

    def body(u_ref, bre_ref, bim_ref, bret_ref, bimt_ref, ar_ref, ai_ref, s0r_ref, s0i_ref, glr_ref, gli_ref,
             cret_ref, cimt_ref, d_ref, dy_ref, du_ref, dbre_ref, dbim_ref, dar_ref, dai_ref,
             xr, xi, dr, di, sr, si, gr, gi, acc_r, acc_i):
        @pl.when(pl.program_id(0) == 0)
        def _():
            x_r, x_i = _s5_seg_carry(glr_ref[...], gli_ref[...], ar_ref[...], ai_ref[...], seg_len, True)
            gr[...] = x_r
            gi[...] = x_i
            acc_r[...] = jnp.zeros_like(acc_r)
            acc_i[...] = jnp.zeros_like(acc_i)
            dbre_ref[...] = jnp.zeros_like(dbre_ref)
            dbim_ref[...] = jnp.zeros_like(dbim_ref)

        sr[...] = s0r_ref[0]
        si[...] = s0i_ref[0]
        _s5_bu(u_ref, bre_ref, bim_ref, xr, xi)
        _s5_scan(xr, xi, sr, si, ar_ref, ai_ref, rb // S5_SEG, True)
        dy = dy_ref[...]
        for k in range(S5_UB):
            cs = slice(k * S5_LC, (k + 1) * S5_LC)
            dyk = dy[:, k * LANES:(k + 1) * LANES]
            dr[:, cs] = _bdot(dyk, cret_ref[k])
            di[:, cs] = -_bdot(dyk, cimt_ref[k])
        sr[...] = s0r_ref[0]
        si[...] = s0i_ref[0]
        _s5_rscan(dr, di, xr, xi, sr, si, gr, gi, acc_r, acc_i, ar_ref, ai_ref, rb // S5_SEG)
        dus = []
        for k in range(S5_UB):
            cs = slice(k * S5_LC, (k + 1) * S5_LC)
            uk = u_ref[:, k * LANES:(k + 1) * LANES]
            dbre_ref[k] += _bdot(uk, dr[:, cs], TN)
            dbim_ref[k] += _bdot(uk, di[:, cs], TN)
            dus.append(_bdot(dr[:, cs], bret_ref[k]) + _bdot(di[:, cs], bimt_ref[k]))
        du_ref[...] = (jnp.concatenate(dus, axis=1) + d_ref[...] * dy).astype(bf16)

        @pl.when(pl.program_id(0) == nblk - 1)
        def _():
            dar_ref[...] = jnp.sum(acc_r[...], axis=0, keepdims=True)
            dai_ref[...] = jnp.sum(acc_i[...], axis=0, keepdims=True)

    return pl.pallas_call(
        body, name="s5_bwd_b", grid=(nblk,),
        in_specs=[sp["tok"], sp["bmat"], sp["bmat"], sp["cmat"], sp["cmat"], sp["avec"], sp["avec"], sp["s0"], sp["s0"],
                  sp["seg"], sp["seg"], sp["bmat"], sp["bmat"], sp["cvec"], sp["tok"]],
        out_specs=[sp["tok"], sp["bmat"], sp["bmat"], sp["avec"], sp["avec"]],
        out_shape=[jax.ShapeDtypeStruct((T, 4 * LANES), bf16)] + [jax.ShapeDtypeStruct((S5_UB, LANES, S5_LC), f32)] * 2
        + [jax.ShapeDtypeStruct((1, S5_N), f32)] * 2,
        scratch_shapes=[pltpu.VMEM((rb, S5_N), f32)] * 4 + [pltpu.VMEM((S5_SEG, S5_N), f32)] * 6,
        compiler_params=_cp("arbitrary"),
    )(u, bre, bim, bret, bimt, ar, ai, s0r, s0i, glr, gli, cret, cimt, dsk, dy)


def _blockdiag(w, transpose=False):
    if transpose:
        w = jnp.swapaxes(w, 1, 2)
    g, a, b = w.shape
    eye = jnp.eye(8, dtype=w.dtype)
    return jnp.einsum("kgab,gj->kgajb", w.reshape(4, 8, a, b), eye).reshape(4, 8 * a, 8 * b)


def _blockdiag_t(m, a, b):
    eye = jnp.eye(8, dtype=m.dtype)
    return jnp.einsum("kgajb,gj->kgab", m.reshape(4, 8, a, 8, b), eye).reshape(32, a, b)


ROT = MLA_ROPE // 2


def _rope_tables(positions):
    freqs = ROPE_THETA ** (-jnp.arange(0, MLA_ROPE, 2, dtype=f32) / MLA_ROPE)
    ang = positions.astype(f32)[:, None] * freqs
    cos, sin, z = jnp.cos(ang), jnp.sin(ang), jnp.zeros_like(ang)
    return (jnp.concatenate([cos, cos, z, z], axis=1), jnp.concatenate([-sin, z, z, z], axis=1),
            jnp.concatenate([z, sin, z, z], axis=1))


def _rot(x, c, sa, sb):
    return x * c + pltpu.roll(x, LANES - ROT, 1) * sa + pltpu.roll(x, ROT, 1) * sb


def _rot_t(dy, c, sa, sb):
    return dy * c + pltpu.roll(dy * sa, ROT, 1) + pltpu.roll(dy * sb, LANES - ROT, 1)


def _rms(xv, g):
    return xv * lax.rsqrt(jnp.mean(xv * xv, axis=-1, keepdims=True) + EPS) * g


QW, KVW = MLA_Q_RANK, MLA_KV_RANK
ODD_PAD = QW + KVW + LANES


def _mla_prep_fwd(proj, qg, kvg, tabs, *, tm=512):
    T = proj.shape[0]
    tm = _tile(T, tm)

    def body(p_ref, qg_ref, kvg_ref, c_ref, sa_ref, sb_ref, cq_ref, ckv_ref, kr_ref):
        cq_ref[...] = _rms(p_ref[:, :QW], qg_ref[...]).astype(bf16)
        ckv_ref[...] = _rms(p_ref[:, QW:QW + KVW], kvg_ref[...]).astype(bf16)
        kr_ref[...] = _rot(p_ref[:, QW + KVW:], c_ref[...], sa_ref[...], sb_ref[...]).astype(bf16)

    row = lambda w: pl.BlockSpec((tm, w), lambda i: (i, 0))
    vec = lambda w: pl.BlockSpec((1, w), lambda i: (0, 0))
    return pl.pallas_call(
        body, name="mla_prep_fwd", grid=(T // tm,),
        in_specs=[row(ODD_PAD), vec(QW), vec(KVW), row(LANES), row(LANES), row(LANES)],
        out_specs=[row(QW), row(KVW), row(LANES)],
        out_shape=[jax.ShapeDtypeStruct((T, QW), bf16), jax.ShapeDtypeStruct((T, KVW), bf16),
                   jax.ShapeDtypeStruct((T, LANES), bf16)],
        compiler_params=_cp("parallel"),
    )(proj, qg, kvg, *tabs)


def _mla_prep_bwd(proj, qg, kvg, tabs, dcqn, dckvn, dkr_heads, *, tm=512):
    T = proj.shape[0]
    tm = _tile(T, tm)

    def body(p_ref, qg_ref, kvg_ref, c_ref, sa_ref, sb_ref, dcq_ref, dckv_ref, dkr_ref, dp_ref, dqg_ref, dkvg_ref):
        dcq, dqg = _rms_bwd_math(p_ref[:, :QW], qg_ref[...], dcq_ref[...])
        dckv, dkvg = _rms_bwd_math(p_ref[:, QW:QW + KVW], kvg_ref[...], dckv_ref[...])
        dk = dkr_ref[:, :LANES]
        for h in range(1, MLA_HEADS):
            dk = dk + dkr_ref[:, h * LANES:(h + 1) * LANES]
        dkr = _rot_t(dk, c_ref[...], sa_ref[...], sb_ref[...])
        dp_ref[...] = jnp.concatenate([dcq, dckv, dkr], axis=1).astype(bf16)

        @pl.when(pl.program_id(0) == 0)
        def _():
            dqg_ref[...] = dqg
            dkvg_ref[...] = dkvg

        @pl.when(pl.program_id(0) > 0)
        def _():
            dqg_ref[...] += dqg
            dkvg_ref[...] += dkvg

    row = lambda w: pl.BlockSpec((tm, w), lambda i: (i, 0))
    vec = lambda w: pl.BlockSpec((1, w), lambda i: (0, 0))
    return pl.pallas_call(
        body, name="mla_prep_bwd", grid=(T // tm,),
        in_specs=[row(ODD_PAD), vec(QW), vec(KVW), row(LANES), row(LANES), row(LANES), row(QW), row(KVW),
                  row(MLA_HEADS * LANES)],
        out_specs=[row(ODD_PAD), vec(QW), vec(KVW)],
        out_shape=[jax.ShapeDtypeStruct((T, ODD_PAD), bf16), jax.ShapeDtypeStruct((1, QW), f32),
                   jax.ShapeDtypeStruct((1, KVW), f32)],
        compiler_params=_cp("arbitrary"),
    )(proj, qg, kvg, *tabs, dcqn, dckvn, dkr_heads)


HQ = 2 * LANES
QK_SCALE = MLA_QK ** -0.5


def _q_post(q, tabs, *, transpose, name, tm=512):
    T = q.shape[0]
    tm = _tile(T, tm)

    def body(q_ref, c_ref, sa_ref, sb_ref, o_ref):
        qv = q_ref[...].astype(f32)
        rope = (_rot_t if transpose else _rot)(qv[:, LANES:], c_ref[...], sa_ref[...], sb_ref[...])
        o_ref[...] = (jnp.concatenate([qv[:, :LANES], rope], axis=1) * QK_SCALE).astype(bf16)

    tab = pl.BlockSpec((tm, LANES), lambda i, h: (i, 0))
    blk = pl.BlockSpec((tm, HQ), lambda i, h: (i, h))
    return pl.pallas_call(
        body, name=name, grid=(T // tm, MLA_HEADS), in_specs=[blk, tab, tab, tab], out_specs=blk,
        out_shape=jax.ShapeDtypeStruct(q.shape, bf16), compiler_params=_cp("parallel", "parallel"),
    )(q, *tabs)


def _causal_mask(i, j, tq, tk):
    r = lax.broadcasted_iota(jnp.int32, (tq, tk), 0) + i * tq
    c = lax.broadcasted_iota(jnp.int32, (tq, tk), 1) + j * tk
    return c <= r


def _flash_fwd(q, kv, kr, *, tb=512):
    T = q.shape[0]
    tb = _tile(T, tb)
    nb = T // tb
    H = MLA_HEADS

    def body(q_ref, kn_ref, v_ref, kr_ref, o_ref, lse_ref, m_s, l_s, acc):
        i, j = pl.program_id(1), pl.program_id(2)

        @pl.when(j == 0)
        def _():
            m_s[...] = jnp.full_like(m_s, -jnp.inf)
            l_s[...] = jnp.zeros_like(l_s)
            acc[...] = jnp.zeros_like(acc)

        @pl.when(j <= i)
        def _():
            k = jnp.concatenate([kn_ref[...], kr_ref[...]], axis=1)
            s = _dot(q_ref[...], k, NT)
            s = jnp.where(_causal_mask(i, j, tb, tb), s, -jnp.inf)
            m_new = jnp.maximum(m_s[...], jnp.max(s, axis=-1, keepdims=True))
            alpha = jnp.exp(m_s[...] - m_new)
            p = jnp.exp(s - m_new)
            l_s[...] = alpha * l_s[...] + jnp.sum(p, axis=-1, keepdims=True)
            acc[...] = alpha * acc[...] + _dot(p.astype(bf16), v_ref[...])
            m_s[...] = m_new

        @pl.when(j == i)
        def _():
            o_ref[...] = (acc[...] / l_s[...]).astype(bf16)
            lse_ref[0] = m_s[...] + jnp.log(l_s[...])

    kblk = lambda off: pl.BlockSpec((tb, LANES), lambda h, i, j: (jnp.minimum(j, i), 2 * h + off))
    return pl.pallas_call(
        body, name="flash_fwd", grid=(H, nb, nb),
        in_specs=[pl.BlockSpec((tb, HQ), lambda h, i, j: (i, h)), kblk(0), kblk(1),
                  pl.BlockSpec((tb, LANES), lambda h, i, j: (jnp.minimum(j, i), 0))],
        out_specs=[pl.BlockSpec((tb, LANES), lambda h, i, j: (i, h)), pl.BlockSpec((1, tb, 1), lambda h, i, j: (h, i, 0))],
        out_shape=[jax.ShapeDtypeStruct((T, H * LANES), bf16), jax.ShapeDtypeStruct((H, T, 1), f32)],
        scratch_shapes=[pltpu.VMEM((tb, 1), f32), pltpu.VMEM((tb, 1), f32), pltpu.VMEM((tb, LANES), f32)],
        compiler_params=_cp("parallel", "parallel", "arbitrary"),
    )(q, kv, kv, kr)


def _attn_delta(o, do, *, tm=512):
    T = o.shape[0]
    tm = _tile(T, tm)

    def body(o_ref, do_ref, d_ref):
        d_ref[0] = jnp.sum(o_ref[...].astype(f32) * do_ref[...], axis=-1, keepdims=True)

    blk = pl.BlockSpec((tm, LANES), lambda h, i: (i, h))
    return pl.pallas_call(
        body, name="attn_delta", grid=(MLA_HEADS, T // tm), in_specs=[blk, blk],
        out_specs=pl.BlockSpec((1, tm, 1), lambda h, i: (h, i, 0)),
        out_shape=jax.ShapeDtypeStruct((MLA_HEADS, T, 1), f32), compiler_params=_cp("parallel", "parallel"),
    )(o, do)


def _flash_p_ds(q_ref, kn_ref, v_ref, kr_ref, do_ref, lse_ref, dl_ref, i, j, tb):
    k = jnp.concatenate([kn_ref[...], kr_ref[...]], axis=1)
    s = _dot(q_ref[...], k, NT)
    p = jnp.where(_causal_mask(i, j, tb, tb), jnp.exp(s - lse_ref[0]), 0.0)
    dp = _bdot(do_ref[...], v_ref[...], NT)
    ds = p * (dp - dl_ref[0])
    return k, p, ds


def _flash_bwd_kv(q, kv, kr, do, lse, delta, *, tb=512):
    T = q.shape[0]
    tb = _tile(T, tb)
    nb = T // tb
    H = MLA_HEADS

    def body(q_ref, kn_ref, v_ref, kr_ref, do_ref, lse_ref, dl_ref, dkv_ref, dkr_ref, dk_acc, dv_acc):
        j, ii = pl.program_id(1), pl.program_id(2)
        i = jnp.maximum(ii, j)

        @pl.when(ii == 0)
        def _():
            dk_acc[...] = jnp.zeros_like(dk_acc)
            dv_acc[...] = jnp.zeros_like(dv_acc)

        @pl.when(ii >= j)
        def _():
            _, p, ds = _flash_p_ds(q_ref, kn_ref, v_ref, kr_ref, do_ref, lse_ref, dl_ref, i, j, tb)
            dv_acc[...] += _bdot(p, do_ref[...], TN)
            dk_acc[...] += _bdot(ds, q_ref[...], TN)

        @pl.when(ii == nb - 1)
        def _():
            dkv_ref[...] = jnp.concatenate([dk_acc[:, :LANES], dv_acc[...]], axis=1).astype(bf16)
            dkr_ref[...] = dk_acc[:, LANES:]

    qi = lambda h, j, i: jnp.maximum(i, j)
    kblk = lambda off: pl.BlockSpec((tb, LANES), lambda h, j, i: (j, 2 * h + off))
    vec = pl.BlockSpec((1, tb, 1), lambda h, j, i: (h, qi(h, j, i), 0))
    return pl.pallas_call(
        body, name="flash_bwd_kv", grid=(H, nb, nb),
        in_specs=[pl.BlockSpec((tb, HQ), lambda h, j, i: (qi(h, j, i), h)), kblk(0), kblk(1),
                  pl.BlockSpec((tb, LANES), lambda h, j, i: (j, 0)),
                  pl.BlockSpec((tb, LANES), lambda h, j, i: (qi(h, j, i), h)), vec, vec],
        out_specs=[pl.BlockSpec((tb, HQ), lambda h, j, i: (j, h)), pl.BlockSpec((tb, LANES), lambda h, j, i: (j, h))],
        out_shape=[jax.ShapeDtypeStruct((T, H * HQ), bf16), jax.ShapeDtypeStruct((T, H * LANES), f32)],
        scratch_shapes=[pltpu.VMEM((tb, HQ), f32), pltpu.VMEM((tb, LANES), f32)],
        compiler_params=_cp("parallel", "parallel", "arbitrary"),
    )(q, kv, kv, kr, do, lse, delta)


def _flash_bwd_q(q, kv, kr, do, lse, delta, *, tb=512):
    T = q.shape[0]
    tb = _tile(T, tb)
    nb = T // tb
    H = MLA_HEADS

    def body(q_ref, kn_ref, v_ref, kr_ref, do_ref, lse_ref, dl_ref, dq_ref, acc):
        i, j = pl.program_id(1), pl.program_id(2)

        @pl.when(j == 0)
        def _():
            acc[...] = jnp.zeros_like(acc)

        @pl.when(j <= i)
        def _():
            k, _, ds = _flash_p_ds(q_ref, kn_ref, v_ref, kr_ref, do_ref, lse_ref, dl_ref, i, j, tb)
            acc[...] += _bdot(ds, k)

        @pl.when(j == i)
        def _():
            dq_ref[...] = acc[...]

    kj = lambda h, i, j: jnp.minimum(j, i)
    kblk = lambda off: pl.BlockSpec((tb, LANES), lambda h, i, j: (kj(h, i, j), 2 * h + off))
    vec = pl.BlockSpec((1, tb, 1), lambda h, i, j: (h, i, 0))
    return pl.pallas_call(
        body, name="flash_bwd_q", grid=(H, nb, nb),
        in_specs=[pl.BlockSpec((tb, HQ), lambda h, i, j: (i, h)), kblk(0), kblk(1),
                  pl.BlockSpec((tb, LANES), lambda h, i, j: (kj(h, i, j), 0)),
                  pl.BlockSpec((tb, LANES), lambda h, i, j: (i, h)), vec, vec],
        out_specs=pl.BlockSpec((tb, HQ), lambda h, i, j: (i, h)),
        out_shape=jax.ShapeDtypeStruct((T, H * HQ), f32),
        scratch_shapes=[pltpu.VMEM((tb, HQ), f32)], compiler_params=_cp("parallel", "parallel", "arbitrary"),
    )(q, kv, kv, kr, do, lse, delta)


HBM_SPEC = pl.BlockSpec(memory_space=pltpu.HBM)
N_CHIPS = 4
N_DEV = 8

BIG = {"even_w_in": 1, "s5_w_glu": 0, "even_w_out": 0, "odd_w_in": 0, "mla_w_uq": 1, "mla_w_ukv": 1, "odd_w_out": 0,
       "ffn_w_in": 2, "ffn_w_out": 1}
LAYERED = ("ffn_w_in", "ffn_w_out")
GROUPS = {"even_in": ("even_w_in",), "even_rest": ("s5_w_glu", "even_w_out"), "ffn0": LAYERED,
          "odd": ("odd_w_in", "mla_w_uq", "mla_w_ukv", "odd_w_out"), "ffn1": LAYERED}
GROUP_LAYER = {"ffn0": 0, "ffn1": 1}


def _place():
    x, y, c = lax.axis_index("x"), lax.axis_index("y"), lax.axis_index("c")
    chips = [(1 - x, y), (x, 1 - y), (1 - x, 1 - y)]
    return x, y, c, chips


def _slab(ref, axis, k, size):
    start = pl.multiple_of(k * size, size if axis == 0 else LANES)
    idx = [slice(None)] * len(ref.shape)
    idx[axis] = pl.ds(start, size)
    return ref.at[tuple(idx)]


SEM_SPEC = pl.BlockSpec(memory_space=pltpu.SEMAPHORE)
ANY_SPEC = pl.BlockSpec(memory_space=pl.ANY)
EFFECT = pltpu.SideEffectType.DATAFLOW_SIDE_EFFECTING


def _hbm(a):
    return pltpu.with_memory_space_constraint(a, pltpu.HBM)


class _Gather:
    copies = 3

    def __init__(self, axis, size):
        self.axis, self.size = axis, size

    def view(self, land, kk):
        return land.at[kk] if self.axis is None else _slab(land, self.axis, kk, self.size)

    def own(self, land, place):
        return self.view(land, 2 * place[0] + place[1])

    def sends(self, src, land, place):
        x, y, c, chips = place
        return [(src, self.own(land, place), (*chip, c)) for chip in chips]

    def recvs(self, land, place):
        return [self.view(land, 2 * cx + cy) for cx, cy in place[3]]


class _Scatter:
    copies = 3

    def __init__(self, axis, size, layer=None):
        self.axis, self.size, self.layer = axis, size, layer

    def row(self, land, j):
        return land.at[j] if self.layer is None else land.at[j, self.layer]

    def sends(self, src, land, place):
        c, chips = place[2], place[3]
        return [(_slab(src, self.axis, 2 * cx + cy, self.size), self.row(land, j), (cx, cy, c))
                for j, (cx, cy) in enumerate(chips)]

    def recvs(self, land, place):
        return [self.row(land, j) for j in range(3)]


class _ToAll:
    copies = N_DEV - 1

    def own(self, land, place):
        x, y, c, _ = place
        return land.at[4 * x + 2 * y + c]

    def sends(self, src, land, place):
        x, y, c, _ = place
        flip = lambda v, bit: 1 - v if bit else v
        return [(src, self.own(land, place), (flip(x, m & 4), flip(y, m & 2), flip(c, m & 1))) for m in range(1, N_DEV)]

    def recvs(self, land, place):
        x, y, c, _ = place
        d = 4 * x + 2 * y + c
        return [land.at[d ^ m] for m in range(1, N_DEV)]


def _sem_base(routes):
    base = [0]
    for r in routes:
        base.append(base[-1] + r.copies)
    return base


def _push_start(name, items):
    n = len(items)
    base = _sem_base([it[0] for it in items])

    def body(*refs):
        srcs, lands, send, recv, token = refs[:n], refs[n:2 * n], refs[2 * n], refs[2 * n + 1], refs[-1]
        place = _place()
        for i, (route, _, _) in enumerate(items):
            for j, (s, d, dev) in enumerate(route.sends(srcs[i], lands[i], place)):
                pltpu.make_async_remote_copy(src_ref=s, dst_ref=d, send_sem=send.at[base[i] + j], recv_sem=recv.at[base[i] + j],
                                             device_id=dev, device_id_type=MESH).start()
        token[...] = jnp.zeros_like(token)

    arrays = [_hbm(it[1]) for it in items] + [_hbm(it[2]) for it in items]
    res = pl.pallas_call(
        body, name=name,
        out_shape=[pltpu.SemaphoreType.DMA((base[-1],)), pltpu.SemaphoreType.DMA((base[-1],))]
        + [pltpu.HBM(a.shape, a.dtype) for a in arrays] + [jax.ShapeDtypeStruct((SUBLANES, LANES), f32)],
        in_specs=[HBM_SPEC] * (2 * n), out_specs=[SEM_SPEC, SEM_SPEC] + [HBM_SPEC] * (2 * n) + [pl.BlockSpec(memory_space=pltpu.VMEM)],
        input_output_aliases={i: 2 + i for i in range(2 * n)},
        compiler_params=pltpu.CompilerParams(has_side_effects=EFFECT),
    )(*arrays)
    return (res[0], res[1]), list(res[2:2 + n]), list(res[2 + n:2 + 2 * n]), res[-1]


def _push_wait(name, groups, after):
    arrays, index = [], {}
    for routes, _, srcs, lands in groups:
        for a in list(srcs) + list(lands):
            if id(a) not in index:
                index[id(a)] = len(arrays)
                arrays.append(a)
    na, ng = len(arrays), len(groups)

    def body(*refs):
        arr, sems = refs[:na], refs[na:na + 2 * ng]
        place = _place()
        for g, (routes, _, srcs, lands) in enumerate(groups):
            send, recv = sems[2 * g], sems[2 * g + 1]
            base = _sem_base(routes)
            for i, route in enumerate(routes):
                src, land = arr[index[id(srcs[i])]], arr[index[id(lands[i])]]
                for j, ((s, d, dev), mine) in enumerate(zip(route.sends(src, land, place), route.recvs(land, place))):
                    cp = pltpu.make_async_remote_copy(src_ref=s, dst_ref=mine, send_sem=send.at[base[i] + j],
                                                      recv_sem=recv.at[base[i] + j], device_id=dev,
                                                      device_id_type=MESH)
                    cp.wait_send()
                    cp.wait_recv()

    sem_args = [s for g in groups for s in g[1]]
    res = pl.pallas_call(
        body, name=name, out_shape=[pltpu.HBM(a.shape, a.dtype) for a in arrays],
        in_specs=[HBM_SPEC] * na + [SEM_SPEC] * (2 * ng) + [ANY_SPEC] * len(after), out_specs=[HBM_SPEC] * na,
        input_output_aliases={i: i for i in range(na)},
        compiler_params=pltpu.CompilerParams(has_side_effects=EFFECT),
    )(*arrays, *sem_args, *after)
    return [[res[index[id(a)]] for a in g[3]] for g in groups]


def _place_own(items):
    n = len(items)

    def body(*refs):
        ins, outs, sem = refs[:n], refs[n:2 * n], refs[-1]
        place = _place()
        cps = [pltpu.make_async_copy(ins[i], items[i][0].own(outs[i], place), sem.at[i]) for i in range(n)]
        for cp in cps:
            cp.start()
        for cp in cps:
            cp.wait()

    return pl.pallas_call(
        body, name="place_own", in_specs=[HBM_SPEC] * n, out_specs=[HBM_SPEC] * n,
        out_shape=[jax.ShapeDtypeStruct(it[2], it[1].dtype) for it in items],
        scratch_shapes=[pltpu.SemaphoreType.DMA((n,))],
    )(*[it[1] for it in items])


def _swap_with_sibling(parts, tag):
    names = list(parts)

    def body(*refs):
        n = len(names)
        ins, outs, send, recv = refs[:n], refs[n:2 * n], refs[-2], refs[-1]
        x, y, c, _ = _place()
        cps = [pltpu.make_async_remote_copy(src_ref=ins[a], dst_ref=outs[a], send_sem=send.at[a], recv_sem=recv.at[a],
                                            device_id=(x, y, 1 - c), device_id_type=MESH) for a in range(n)]
        for cp in cps:
            cp.start()
        for cp in cps:
            cp.wait_recv()
        for cp in cps:
            cp.wait_send()

    res = pl.pallas_call(
        body, name=f"swap_with_sibling_{tag}", in_specs=[HBM_SPEC] * len(names), out_specs=[HBM_SPEC] * len(names),
        out_shape=[jax.ShapeDtypeStruct(parts[n].shape, parts[n].dtype) for n in names],
        scratch_shapes=[pltpu.SemaphoreType.DMA((len(names),)), pltpu.SemaphoreType.DMA((len(names),))],
    )(*[parts[n] for n in names])
    return dict(zip(names, res))


ELEMENTWISE_BLOCK_BYTES = 1 << 20


def _rows(r, c):
    for t in (512, 256, 128, 64, 32, 16, 8):
        if r % t == 0 and t * c * 4 <= ELEMENTWISE_BLOCK_BYTES:
            return t
    return r


def _sum4(owns, axis, recv, kidx, *, name):
    L = len(owns)
    R, C = recv.shape[2:]
    tm = _rows(R, C)
    nr = R // tm

    def body(k_ref, *refs):
        own_refs, r_ref, out_ref = refs[:L], refs[L], refs[L + 1]
        for li in range(L):
            @pl.when(pl.program_id(0) == li)
            def _(o_ref=own_refs[li]):
                out_ref[...] = ((o_ref[...] + r_ref[0, 0].astype(f32)) + r_ref[1, 0].astype(f32)) + r_ref[2, 0].astype(f32)

    own_map = (lambda l, i, k: (i, k[0])) if axis == 1 else (lambda l, i, k: (k[0] * nr + i, 0))
    return pl.pallas_call(
        body, name=name, out_shape=jax.ShapeDtypeStruct((L * R, C), f32),
        grid_spec=pltpu.PrefetchScalarGridSpec(
            num_scalar_prefetch=1, grid=(L, nr),
            in_specs=[pl.BlockSpec((tm, C), own_map)] * L + [pl.BlockSpec((3, 1, tm, C), lambda l, i, k: (0, l, i, 0))],
            out_specs=pl.BlockSpec((tm, C), lambda l, i, k: (l * nr + i, 0))),
        compiler_params=_cp("parallel", "parallel"),
    )(kidx, *owns, recv)


def _adamw(w, m, v, parts, *, name):
    R, C = w.shape
    tm = _rows(R, C)
    npart = len(parts)

    def body(*refs):
        w_ref, m_ref, v_ref = refs[:3]
        g_ref, d_ref, m2_ref, v2_ref = refs[3 + npart:]
        g = refs[3][...]
        for p_ref in refs[4:3 + npart]:
            g = g + p_ref[...]
        m2 = ADAM_B1 * m_ref[...] + (1.0 - ADAM_B1) * g
        v2 = ADAM_B2 * v_ref[...] + (1.0 - ADAM_B2) * (g * g)
        m_hat = m2 / (1.0 - ADAM_B1 ** ADAM_STEP)
        v_hat = v2 / (1.0 - ADAM_B2 ** ADAM_STEP)
        g_ref[...] = g
        d_ref[...] = -ADAM_LR * (m_hat / (jnp.sqrt(v_hat) + ADAM_EPS) + ADAM_WD * w_ref[...])
        m2_ref[...] = m2
        v2_ref[...] = v2

    blk = pl.BlockSpec((tm, C), lambda i: (i, 0))
    return pl.pallas_call(
        body, name=name, grid=(R // tm,),
        in_specs=[blk] * (3 + npart), out_specs=[blk] * 4,
        out_shape=[jax.ShapeDtypeStruct((R, C), f32)] * 4, compiler_params=_cp("parallel"),
    )(w, m, v, *parts)


def _pad_odd(w):
    return jnp.pad(w, ((0, 0), (0, ODD_PAD - w.shape[1])))


def _uq_cat(w):
    r = w.shape[0]
    return jnp.pad(w.reshape(r, MLA_HEADS, MLA_QK), ((0, 0), (0, 0), (0, HQ - MLA_QK))).reshape(r, MLA_HEADS * HQ)


def _uq_uncat(w):
    r = w.shape[0]
    return w.reshape(r, MLA_HEADS, HQ)[:, :, :MLA_QK].reshape(r, MLA_HEADS * MLA_QK)


def _to_segments(v):
    T, C = v.shape
    return v.reshape(S5_SEG, T // S5_SEG, C).transpose(1, 0, 2).reshape(T, C)


def _from_segments(v):
    T, C = v.shape
    return v.reshape(T // S5_SEG, S5_SEG, C).transpose(1, 0, 2).reshape(T, C)


def _s5_rb(T):
    return min(512, T)


def _ffn_fwd(h, g, w_in, cw, cb, w_out, tag):
    hn = _rms_fwd(h, g, name=f"ffn{tag}_norm")
    au = _mm(hn, w_in, name=f"ffn{tag}_in", tn=1408)
    z = _ffn_mid_fwd(au, cw, cb, name=f"ffn{tag}_mid")
    return _mm(z, w_out, res=h, name=f"ffn{tag}_out", tk=1408), (hn, au, z)


def _ffn_bwd(h, g, w_in, cw, cb, w_out, saved, dh, tag, dep=None):
    hn, au, z = saved
    dz = _mm(dh, w_out, tb=True, name=f"ffn{tag}_dz", tn=1408, dep=dep)
    dw_out = _mm(z, dh, ta=True, also_bf16=True, name=f"ffn{tag}_dwout", tm=1408)
    dau, dcw, dcb = _ffn_mid_bwd(au, cw, cb, dz, name=f"ffn{tag}_dmid")
    dhn = _mm(dau, w_in, tb=True, name=f"ffn{tag}_dhn", tk=1408)
    dw_in = _mm(hn, dau, ta=True, also_bf16=True, name=f"ffn{tag}_dwin", tn=1408)
    dh_in, dg = _rms_bwd(h, g, dhn, dh, name=f"ffn{tag}_dnorm")
    return dh_in, dg, dw_in, dcw, dcb, dw_out


def _local_step(x, positions, target, get_w, P, put_g):
    T = x.shape[0]
    rb = _s5_rb(T)
    row = lambda v: v.reshape(1, -1)
    g_mix, g_ffn = P["norm_mix_g"], P["norm_ffn_g"]
    lbl, hng = P["hgrn_lb_logits"], P["hgrn_norm_g"]
    dsk, bg = P["s5_d"], P["s5_b_glu"]
    qg, kvg = P["mla_q_norm_g"], P["mla_kv_norm_g"]
    cw, cb = P["ffn_conv_w"], P["ffn_conv_b"]

    col = lambda v: v.reshape(S5_N, 1)
    disc_in = (col(P["s5_a_re"]), col(P["s5_a_im"]), col(jnp.repeat(P["s5_log_dt"].reshape(S5_GROUPS), S5_STATE)),
               P["s5_b_re"].reshape(S5_N, S5_GROUP), P["s5_b_im"].reshape(S5_N, S5_GROUP))
    abr, abi, bbr, bbi = _s5_disc_fwd(*disc_in)
    ar, ai = abr.reshape(1, S5_N), abi.reshape(1, S5_N)
    bbr3, bbi3 = bbr.reshape(S5_GROUPS, S5_STATE, S5_GROUP), bbi.reshape(S5_GROUPS, S5_STATE, S5_GROUP)
    bre, bim = _blockdiag(bbr3, True).astype(bf16), _blockdiag(bbi3, True).astype(bf16)
    bret, bimt = _blockdiag(bbr3).astype(bf16), _blockdiag(bbi3).astype(bf16)
    c_re, c_im = P["s5_c_re"].reshape(S5_GROUPS, S5_GROUP, S5_STATE), P["s5_c_im"].reshape(S5_GROUPS, S5_GROUP, S5_STATE)
    cre, cim = _blockdiag(c_re, True).astype(bf16), _blockdiag(c_im, True).astype(bf16)
    cret, cimt = _blockdiag(c_re).astype(bf16), _blockdiag(c_im).astype(bf16)

    hn0 = _rms_fwd(x, g_mix[0:1], name="mix0_norm")
    We = get_w("even_in", hn0)
    proj_e = _mm(hn0, We["even_w_in"], name="even_in", tn=1280)
    Wr = get_w("even_rest", proj_e)
    ya, states = _hgrn_fwd(proj_e, lbl, hng)
    u_seg = _to_segments(proj_e[:, 4 * 512:])
    fr, fi = _s5_final(u_seg, bre, bim, ar, ai, rb=rb)
    yb_seg, s0r, s0i = _s5_fwd(u_seg, bre, bim, ar, ai, fr, fi, cre, cim, dsk, Wr["s5_w_glu"], bg, rb=rb)
    ycat = jnp.concatenate([ya, _from_segments(yb_seg)], axis=1)
    h1 = _mm(ycat, Wr["even_w_out"], res=x, name="even_out")
    Wf0 = get_w("ffn0", h1)
    h2, ffn0 = _ffn_fwd(h1, g_ffn[0:1], Wf0["ffn_w_in"], cw[0], cb[0:1], Wf0["ffn_w_out"], 0)

    tabs = _rope_tables(positions)
    hn2 = _rms_fwd(h2, g_mix[1:2], name="mix1_norm")
    Wo = get_w("odd", hn2)
    proj_o = _mm(hn2, Wo["odd_w_in"], name="odd_in")
    cqn, ckvn, kr = _mla_prep_fwd(proj_o, qg, kvg, tabs)
    q = _q_post(_mm(cqn, Wo["mla_w_uq"], name="mla_uq"), tabs, transpose=False, name="q_post")
    kvb = _mm(ckvn, Wo["mla_w_ukv"], out_dtype=bf16, name="mla_ukv")
    o, lse = _flash_fwd(q, kvb, kr)
    h3 = _mm(o, Wo["odd_w_out"], res=h2, name="odd_out")
    Wf1 = get_w("ffn1", h3)
    h4, ffn1 = _ffn_fwd(h3, g_ffn[1:2], Wf1["ffn_w_in"], cw[1], cb[1:2], Wf1["ffn_w_out"], 1)
    loss, dh4, dg_final = _loss_head(h4, row(P["final_norm_g"]), target)

    dh3, dg_ffn1, dw_fin1, dcw1, dcb1, dw_fout1 = _ffn_bwd(
        h3, g_ffn[1:2], Wf1["ffn_w_in"], cw[1], cb[1:2], Wf1["ffn_w_out"], ffn1, dh4, 1)
    sent = put_g("ffn1", {"ffn_w_in": dw_fin1, "ffn_w_out": dw_fout1})
    do = _mm(dh3, Wo["odd_w_out"], tb=True, name="odd_do", dep=sent)
    dw_oout = _mm(o, dh3, ta=True, also_bf16=True, name="odd_dwout")
    delta = _attn_delta(o, do)
    dkv, dkr_h = _flash_bwd_kv(q, kvb, kr, do, lse, delta)
    dq = _q_post(_flash_bwd_q(q, kvb, kr, do, lse, delta), tabs, transpose=True, name="dq_post")
    dw_uq = _mm(cqn, dq, ta=True, also_bf16=True, name="mla_dwuq")
    dcqn = _mm(dq, Wo["mla_w_uq"], tb=True, name="mla_dcq")
    dw_ukv = _mm(ckvn, dkv, ta=True, also_bf16=True, name="mla_dwukv")
    dckvn = _mm(dkv, Wo["mla_w_ukv"], tb=True, name="mla_dckv")
    dproj_o, dqg, dkvg = _mla_prep_bwd(proj_o, qg, kvg, tabs, dcqn, dckvn, dkr_h)
    dhn2 = _mm(dproj_o, Wo["odd_w_in"], tb=True, name="odd_dhn")
    dw_oin = _mm(hn2, dproj_o, ta=True, also_bf16=True, name="odd_dwin")
    sent = put_g("odd", {"odd_w_in": dw_oin, "mla_w_uq": dw_uq, "mla_w_ukv": dw_ukv, "odd_w_out": dw_oout})
    dh2, dg_mix1 = _rms_bwd(h2, g_mix[1:2], dhn2, dh3, name="mix1_dnorm")

    dh1, dg_ffn0, dw_fin0, dcw0, dcb0, dw_fout0 = _ffn_bwd(
        h1, g_ffn[0:1], Wf0["ffn_w_in"], cw[0], cb[0:1], Wf0["ffn_w_out"], ffn0, dh2, 0, dep=sent)
    sent = put_g("ffn0", {"ffn_w_in": dw_fin0, "ffn_w_out": dw_fout0})
    dycat = _mm(dh1, Wr["even_w_out"], tb=True, name="even_dy", dep=sent)
    dw_eout = _mm(ycat, dh1, ta=True, also_bf16=True, name="even_dwout")
    dq_h, df_h, di_h, dg_h, dlbl, dhng = _hgrn_bwd(proj_e, lbl, hng, states, dycat)
    dyb_seg = _to_segments(dycat[:, 512:])
    dy_s5, glr, gli, dcre, dcim, dd, dwg, dbg = _s5_bwd_a(
        u_seg, bre, bim, ar, ai, s0r, s0i, cre, cim, cret, cimt, dsk, Wr["s5_w_glu"], bg, dyb_seg, rb=rb)
    du_seg, dbre, dbim, dar, dai = _s5_bwd_b(
        u_seg, bre, bim, bret, bimt, ar, ai, s0r, s0i, glr, gli, cret, cimt, dsk, dy_s5, rb=rb)
    dproj_e = jnp.concatenate([dq_h, df_h, di_h, dg_h, _from_segments(du_seg)], axis=1)
    dhn0 = _mm(dproj_e, We["even_w_in"], tb=True, name="even_dhn", tk=1280)
    dw_ein = _mm(hn0, dproj_e, ta=True, also_bf16=True, name="even_dwin", tn=1280)
    dx, dg_mix0 = _rms_bwd(x, g_mix[0:1], dhn0, dh1, name="mix0_dnorm")

    unblk = lambda m, a, b: jnp.swapaxes(_blockdiag_t(m, a, b), 1, 2)
    dbbr = unblk(dbre, S5_GROUP, S5_STATE).reshape(S5_N, S5_GROUP)
    dbbi = unblk(dbim, S5_GROUP, S5_STATE).reshape(S5_N, S5_GROUP)
    d_ar, d_ai, d_ldt, d_br, d_bi = _s5_disc_bwd(*disc_in, (dar.reshape(S5_N, 1), dai.reshape(S5_N, 1), dbbr, dbbi))
    small = {
        "norm_mix_g": jnp.concatenate([dg_mix0, dg_mix1], axis=0),
        "norm_ffn_g": jnp.concatenate([dg_ffn0, dg_ffn1], axis=0),
        "final_norm_g": dg_final.reshape(-1),
        "hgrn_lb_logits": dlbl, "hgrn_norm_g": dhng,
        "s5_a_re": d_ar.reshape(1, S5_GROUPS, S5_STATE), "s5_a_im": d_ai.reshape(1, S5_GROUPS, S5_STATE),
        "s5_log_dt": d_ldt.reshape(S5_GROUPS, S5_STATE).sum(axis=1).reshape(1, S5_GROUPS),
        "s5_b_re": d_br.reshape(1, S5_GROUPS, S5_STATE, S5_GROUP), "s5_b_im": d_bi.reshape(1, S5_GROUPS, S5_STATE, S5_GROUP),
        "s5_c_re": unblk(dcre, S5_STATE, S5_GROUP).reshape(1, S5_GROUPS, S5_GROUP, S5_STATE),
        "s5_c_im": unblk(dcim, S5_STATE, S5_GROUP).reshape(1, S5_GROUPS, S5_GROUP, S5_STATE),
        "s5_d": dd, "s5_b_glu": dbg, "mla_q_norm_g": dqg, "mla_kv_norm_g": dkvg,
        "ffn_conv_w": jnp.stack([dcw0, dcw1]), "ffn_conv_b": jnp.concatenate([dcb0, dcb1], axis=0),
    }
    put_g("even", {"even_w_in": dw_ein, "s5_w_glu": (dwg, dwg.astype(bf16)), "even_w_out": dw_eout}, small)
    return loss, dx


WEIGHTS = ["norm_mix_g", "norm_ffn_g", "final_norm_g", "even_w_in", "hgrn_lb_logits", "hgrn_norm_g", "s5_a_re", "s5_a_im",
           "s5_log_dt", "s5_b_re", "s5_b_im", "s5_c_re", "s5_c_im", "s5_d", "s5_w_glu", "s5_b_glu", "even_w_out", "odd_w_in",
           "mla_q_norm_g", "mla_w_uq", "mla_kv_norm_g", "mla_w_ukv", "odd_w_out", "ffn_w_in", "ffn_conv_w", "ffn_conv_b",
           "ffn_w_out"]
SMALL_SHARDED = {"mla_q_norm_g": 1, "mla_kv_norm_g": 1, "ffn_conv_w": 2}
SMALL = [n for n in WEIGHTS if n not in BIG]


def _pack(arrays):
    flat = jnp.concatenate([a.reshape(-1) for a in arrays])
    n = flat.shape[0]
    tile = SUBLANES * LANES
    return jnp.pad(flat, (0, -n % tile)).reshape(-1, LANES)


def _unpack(block, shapes):
    flat, out, off = block.reshape(-1), [], 0
    for s in shapes:
        n = math.prod(s)
        out.append(flat[off:off + n].reshape(s))
        off += n
    return out


def kernel(x, positions, norm_mix_g, norm_ffn_g, final_norm_g, even_w_in, hgrn_lb_logits, hgrn_norm_g, s5_a_re, s5_a_im, s5_log_dt, s5_b_re, s5_b_im, s5_c_re, s5_c_im, s5_d, s5_w_glu, s5_b_glu, even_w_out, odd_w_in, mla_q_norm_g, mla_w_uq, mla_kv_norm_g, mla_w_ukv, odd_w_out, ffn_w_in, ffn_conv_w, ffn_conv_b, ffn_w_out, loss_target, m_norm_mix_g, m_norm_ffn_g, m_final_norm_g, m_even_w_in, m_hgrn_lb_logits, m_hgrn_norm_g, m_s5_a_re, m_s5_a_im, m_s5_log_dt, m_s5_b_re, m_s5_b_im, m_s5_c_re, m_s5_c_im, m_s5_d, m_s5_w_glu, m_s5_b_glu, m_even_w_out, m_odd_w_in, m_mla_q_norm_g, m_mla_w_uq, m_mla_kv_norm_g, m_mla_w_ukv, m_odd_w_out, m_ffn_w_in, m_ffn_conv_w, m_ffn_conv_b, m_ffn_w_out, v_norm_mix_g, v_norm_ffn_g, v_final_norm_g, v_even_w_in, v_hgrn_lb_logits, v_hgrn_norm_g, v_s5_a_re, v_s5_a_im, v_s5_log_dt, v_s5_b_re, v_s5_b_im, v_s5_c_re, v_s5_c_im, v_s5_d, v_s5_w_glu, v_s5_b_glu, v_even_w_out, v_odd_w_in, v_mla_q_norm_g, v_mla_w_uq, v_mla_kv_norm_g, v_mla_w_ukv, v_odd_w_out, v_ffn_w_in, v_ffn_conv_w, v_ffn_conv_b, v_ffn_w_out):
    args = dict(locals())
    w = {n: args[n] for n in WEIGHTS}
    m = {n: args["m_" + n] for n in WEIGHTS}
    v = {n: args["v_" + n] for n in WEIGHTS}
    k = 2 * lax.axis_index("x") + lax.axis_index("y")
    kidx = k.reshape(1).astype(jnp.int32)
    axis2d = lambda n: BIG[n] - (1 if n in LAYERED else 0)
    slab = lambda n: w[n].shape[1 + axis2d(n)]

    small_sh_shapes = [w[n].shape for n in SMALL_SHARDED]
    items = {}
    for group, names in GROUPS.items():
        layer = GROUP_LAYER.get(group, 0)
        items[group] = [(_Gather(axis2d(n), slab(n)), w[n][layer].astype(bf16)) for n in names]
    items["even_in"].append((_Gather(None, None), _pack([w[n] for n in SMALL_SHARDED])))
    flat = [it for g in items.values() for it in g]
    shapes = [(N_CHIPS,) + b.shape if r.axis is None else
              tuple(d * (N_CHIPS if a == r.axis else 1) for a, d in enumerate(b.shape)) for r, b in flat]
    placed = iter(_place_own([(r, b, s) for (r, b), s in zip(flat, shapes)]))
    gathers, tokens = {}, []
    for group in GROUPS:
        sems, srcs, lands, token = _push_start(f"gather_start_{group}", [(r, b, next(placed)) for r, b in items[group]])
        gathers[group] = ([r for r, _ in items[group]], sems, srcs, lands)
        tokens.append(token[0, 0])
    started = functools.reduce(jnp.add, tokens)

    def landed(group, after):
        return _push_wait(f"gather_wait_{group}", [gathers[group]], [after])[0]

    even = landed("even_in", (started + norm_mix_g[0, 0]).reshape(1))
    per_chip = [_unpack(even[-1][c], small_sh_shapes) for c in range(N_CHIPS)]
    P = {n: w[n] for n in SMALL if n not in SMALL_SHARDED}
    for i, (n, ax) in enumerate(SMALL_SHARDED.items()):
        P[n] = jnp.concatenate([per_chip[c][i] for c in range(N_CHIPS)], axis=ax)
    P["mla_q_norm_g"], P["mla_kv_norm_g"] = P["mla_q_norm_g"].reshape(1, -1), P["mla_kv_norm_g"].reshape(1, -1)
    fix_w = {"odd_w_in": _pad_odd, "mla_w_uq": _uq_cat}

    def get_w(group, after):
        full = even if group == "even_in" else landed(group, after)
        return {n: fix_w.get(n, lambda a: a)(a) for n, a in zip(GROUPS[group], full)}

    fix_g = {"odd_w_in": lambda g: g[:, :odd_w_in.shape[2]], "mla_w_uq": _uq_uncat}
    g32, scatters, land_now, small_shapes = {}, {}, {}, []

    def put_g(group, grads, small=None):
        layer = GROUP_LAYER.get(group)
        routes, srcs, names = [], [], list(grads)
        for n in names:
            f = fix_g.get(n, lambda g: g)
            g32.setdefault(n, {})[layer or 0] = f(grads[n][0])
            routes.append(_Scatter(axis2d(n), slab(n), layer if n in LAYERED else None))
            srcs.append(f(grads[n][1]))
            if n not in land_now:
                land_now[n] = lax.empty((3,) + w[n].shape[0 if n in LAYERED else 1:], bf16)
        if small is not None:
            full = [small[n].reshape(tuple(d * (N_CHIPS if a == SMALL_SHARDED.get(n, -1) else 1)
                                           for a, d in enumerate(w[n].shape))) for n in SMALL]
            small_shapes.extend(a.shape for a in full)
            vec = _pack(full)
            names.append("small")
            routes.append(_ToAll())
            srcs.append(vec)
            land_now["small"] = _place_own([(routes[-1], vec, (N_DEV,) + vec.shape)])[0]
        sems, srcs, lands, token = _push_start(f"scatter_start_{group}", [(r, s, land_now[n]) for r, s, n in zip(routes, srcs, names)])
        land_now.update(zip(names, lands))
        scatters[group] = (routes, sems, srcs, names)
        return token

    loss, dx = _local_step(x[0], positions[0], loss_target[0], get_w, P, put_g)
    loss = lax.psum(loss[0, 0], ("x", "y", "c"))

    out = {}

    def finish(tag, groups, after):
        waits = [(scatters[g][0], scatters[g][1], scatters[g][2], [land_now[n] for n in scatters[g][3]]) for g in groups]
        for g, lands in zip(groups, _push_wait(f"scatter_wait_{tag}", waits, after)):
            land_now.update(zip(scatters[g][3], lands))
        names = [n for n in dict.fromkeys(n for g in groups for n in scatters[g][3]) if n != "small"]
        part = {}
        for n in names:
            recv = land_now[n] if n in LAYERED else land_now[n][:, None]
            part[n] = _sum4([g32[n][l] for l in sorted(g32[n])], axis2d(n), recv, kidx, name=f"sum4_{n}")
        other = _swap_with_sibling(part, tag)
        done = []
        for n in names:
            C = part[n].shape[-1]
            res = _adamw(w[n].reshape(-1, C), m[n].reshape(-1, C), v[n].reshape(-1, C), [part[n], other[n]], name=f"adamw_{n}")
            out[n] = [r.reshape(w[n].shape) for r in res]
            done.append(res[0])
        return done

    done = finish("a", ["ffn1", "odd", "ffn0"], [dx])
    finish("b", ["even"], done)

    small_all = land_now["small"]
    per_dev = [_unpack(small_all[d], small_shapes) for d in range(N_DEV)]
    parts = []
    for d in range(N_DEV):
        mine = []
        for i, n in enumerate(SMALL):
            g = per_dev[d][i]
            if n in SMALL_SHARDED:
                ax = SMALL_SHARDED[n]
                g = lax.dynamic_slice_in_dim(g, k * w[n].shape[ax], w[n].shape[ax], axis=ax)
            mine.append(g)
        parts.append(_pack(mine))
    res = _adamw(_pack([w[n] for n in SMALL]), _pack([m[n] for n in SMALL]), _pack([v[n] for n in SMALL]), parts,
                 name="adamw_small")
    unpacked = [_unpack(r, [w[n].shape for n in SMALL]) for r in res]
    for i, n in enumerate(SMALL):
        out[n] = [u[i] for u in unpacked]

    return (loss, dx[None], *[out[n][0] for n in WEIGHTS], *[out[n][1] for n in WEIGHTS],
            *[out[n][2] for n in WEIGHTS], *[out[n][3] for n in WEIGHTS])
```

```python
import functools
import math

import jax
import jax.numpy as jnp
from jax import lax
from jax.experimental import pallas as pl
from jax.experimental.pallas import tpu as pltpu

f32, bf16 = jnp.float32, jnp.bfloat16
EPS = 1e-6
LANES = 128
SUBLANES = 8
VMEM_BYTES = 48 * 1024 * 1024
HGRN_CHUNK = 64
HGRN_HEADS = 4
S5_GROUPS, S5_STATE, S5_GROUP = 32, 64, 16
S5_N = S5_GROUPS * S5_STATE
S5_SEG = SUBLANES
MLA_HEADS, MLA_NOPE, MLA_ROPE, MLA_V = 8, 128, 64, 128
MLA_QK = MLA_NOPE + MLA_ROPE
MLA_Q_RANK, MLA_KV_RANK = 384, 256
ROPE_THETA = 10000.0
D_FF = 2816
ADAM_LR, ADAM_B1, ADAM_B2, ADAM_EPS, ADAM_WD, ADAM_STEP = 0.001, 0.9, 0.999, 1e-08, 0.01, 10
MESH = pl.DeviceIdType.MESH
HI = lax.Precision.HIGHEST


def _cp(*dims):
    return pltpu.CompilerParams(dimension_semantics=dims if dims else None, vmem_limit_bytes=VMEM_BYTES)


def _tile(n, t):
    if n <= t:
        return n
    c = (t // LANES) * LANES
    while c >= LANES:
        if n % c == 0:
            return c
        c -= LANES
    return n


def _dot(a, b, dn=None, precision=None):
    if dn is None:
        dn = (((a.ndim - 1,), (0,)), ((), ()))
    return lax.dot_general(a, b, dn, preferred_element_type=f32, precision=precision)


NT = (((1,), (1,)), ((), ()))
TN = (((0,), (0,)), ((), ()))


def _bdot(a, b, dn=None):
    return _dot(a.astype(bf16), b.astype(bf16), dn)


def _mm(a, b, *, name, ta=False, tb=False, out_dtype=f32, res=None, also_bf16=False, tm=1024, tn=1024, tk=1024, dep=None):
    M, K = (a.shape[1], a.shape[0]) if ta else a.shape
    N = b.shape[0] if tb else b.shape[1]
    tm, tn, tk = _tile(M, tm), _tile(N, tn), _tile(K, tk)
    nk = K // tk
    dn = (((0 if ta else 1,), (1 if tb else 0,)), ((), ()))

    def body(*refs):
        a_ref, b_ref = refs[0], refs[1]
        r_ref = refs[2] if res is not None else None
        nin = 2 + (res is not None) + (dep is not None)
        outs = refs[nin:-1]
        acc = refs[-1]
        k = pl.program_id(2)
        p = _bdot(a_ref[...], b_ref[...], dn)

        @pl.when(k == 0)
        def _():
            acc[...] = p

        @pl.when(k > 0)
        def _():
            acc[...] += p

        @pl.when(k == nk - 1)
        def _():
            r = acc[...]
            if r_ref is not None:
                r = r + r_ref[...]
            outs[0][...] = r.astype(out_dtype)
            if also_bf16:
                outs[1][...] = r.astype(bf16)

    a_spec = pl.BlockSpec((tk, tm), lambda i, j, k: (k, i)) if ta else pl.BlockSpec((tm, tk), lambda i, j, k: (i, k))
    b_spec = pl.BlockSpec((tn, tk), lambda i, j, k: (j, k)) if tb else pl.BlockSpec((tk, tn), lambda i, j, k: (k, j))
    o_spec = pl.BlockSpec((tm, tn), lambda i, j, k: (i, j))
    in_specs, args = [a_spec, b_spec], [a, b]
    if res is not None:
        in_specs.append(o_spec)
        args.append(res)
    if dep is not None:
        in_specs.append(pl.BlockSpec(memory_space=pl.ANY))
        args.append(dep)
    out_shape = [jax.ShapeDtypeStruct((M, N), out_dtype)]
    out_specs = [o_spec]
    if also_bf16:
        out_shape.append(jax.ShapeDtypeStruct((M, N), bf16))
        out_specs.append(o_spec)
    out = pl.pallas_call(
        body, name=name, grid=(M // tm, N // tn, nk), in_specs=in_specs, out_specs=out_specs, out_shape=out_shape,
        scratch_shapes=[pltpu.VMEM((tm, tn), f32)], compiler_params=_cp("parallel", "parallel", "arbitrary"),
    )(*args)
    return out if also_bf16 else out[0]


def _rms_fwd(x, g, *, name, col=0, width=None, tm=512):
    T = x.shape[0]
    width = x.shape[1] if width is None else width
    tm = _tile(T, tm)

    def body(x_ref, g_ref, o_ref):
        xv = x_ref[...]
        r = lax.rsqrt(jnp.mean(xv * xv, axis=-1, keepdims=True) + EPS)
        o_ref[...] = (xv * r * g_ref[...]).astype(bf16)

    return pl.pallas_call(
        body, name=name, grid=(T // tm,),
        in_specs=[pl.BlockSpec((tm, width), lambda i: (i, col)), pl.BlockSpec((1, width), lambda i: (0, 0))],
        out_specs=pl.BlockSpec((tm, width), lambda i: (i, 0)), out_shape=jax.ShapeDtypeStruct((T, width), bf16),
        compiler_params=_cp("parallel"),
    )(x, g)


def _rms_bwd_math(xv, g, dy):
    r = lax.rsqrt(jnp.mean(xv * xv, axis=-1, keepdims=True) + EPS)
    xh = xv * r
    dxh = dy * g
    dx = r * (dxh - xh * jnp.mean(dxh * xh, axis=-1, keepdims=True))
    dg = jnp.sum(dy * xh, axis=0, keepdims=True)
    return dx, dg


def _rms_bwd(x, g, dy, res=None, *, name, tm=512):
    T, D = x.shape
    tm = _tile(T, tm)

    def body(*refs):
        x_ref, g_ref, dy_ref = refs[:3]
        r_ref = refs[3] if res is not None else None
        dx_ref, dg_ref = refs[-2:]
        dx, dg = _rms_bwd_math(x_ref[...], g_ref[...], dy_ref[...].astype(f32))
        if r_ref is not None:
            dx = dx + r_ref[...]
        dx_ref[...] = dx

        @pl.when(pl.program_id(0) == 0)
        def _():
            dg_ref[...] = dg

        @pl.when(pl.program_id(0) > 0)
        def _():
            dg_ref[...] += dg

    row = pl.BlockSpec((tm, D), lambda i: (i, 0))
    vec = pl.BlockSpec((1, D), lambda i: (0, 0))
    in_specs, args = [row, vec, row], [x, g, dy]
    if res is not None:
        in_specs.append(row)
        args.append(res)
    return pl.pallas_call(
        body, name=name, grid=(T // tm,), in_specs=in_specs, out_specs=[row, vec],
        out_shape=[jax.ShapeDtypeStruct((T, D), f32), jax.ShapeDtypeStruct((1, D), f32)],
        compiler_params=_cp("arbitrary"),
    )(*args)


def _loss_head(h, g, target, *, tm=512):
    T, D = h.shape
    tm = _tile(T, tm)

    def body(h_ref, g_ref, t_ref, loss_ref, dh_ref, dg_ref):
        hv, gv = h_ref[...], g_ref[...]
        r = lax.rsqrt(jnp.mean(hv * hv, axis=-1, keepdims=True) + EPS)
        e = hv * r * gv - t_ref[...]
        part = 0.5 * jnp.sum(jnp.mean(e * e, axis=-1, keepdims=True), axis=0, keepdims=True)
        dx, dg = _rms_bwd_math(hv, gv, e * (1.0 / D))
        dh_ref[...] = dx

        @pl.when(pl.program_id(0) == 0)
        def _():
            loss_ref[...] = part
            dg_ref[...] = dg

        @pl.when(pl.program_id(0) > 0)
        def _():
            loss_ref[...] += part
            dg_ref[...] += dg

    row = pl.BlockSpec((tm, D), lambda i: (i, 0))
    vec = pl.BlockSpec((1, D), lambda i: (0, 0))
    return pl.pallas_call(
        body, name="loss_head", grid=(T // tm,), in_specs=[row, vec, row],
        out_specs=[pl.BlockSpec((1, 1), lambda i: (0, 0)), row, vec],
        out_shape=[jax.ShapeDtypeStruct((1, 1), f32), jax.ShapeDtypeStruct((T, D), f32), jax.ShapeDtypeStruct((1, D), f32)],
        compiler_params=_cp("arbitrary"),
    )(h, g, target)


def _shift_down(v, k):
    rows = lax.broadcasted_iota(jnp.int32, v.shape, 0)
    return jnp.where(rows < k, 0.0, pltpu.roll(v, k, 0))


def _shift_up(v, k):
    n = v.shape[0]
    rows = lax.broadcasted_iota(jnp.int32, v.shape, 0)
    return jnp.where(rows >= n - k, 0.0, pltpu.roll(v, n - k, 0))


def _ffn_mid_fwd(au, cw, cb, *, name):
    T = au.shape[0]
    F = au.shape[1] // 2
    nb = F // LANES

    def body(a_ref, u_ref, w_ref, b_ref, z_ref):
        a = a_ref[...]
        w = w_ref[...]
        ac = w[0:1] * _shift_down(a, 2) + w[1:2] * _shift_down(a, 1) + w[2:3] * a + b_ref[...]
        z_ref[...] = (ac * jax.nn.sigmoid(ac) * u_ref[...]).astype(bf16)

    return pl.pallas_call(
        body, name=name, grid=(nb,),
        in_specs=[pl.BlockSpec((T, LANES), lambda j: (0, j)), pl.BlockSpec((T, LANES), lambda j: (0, nb + j)),
                  pl.BlockSpec((3, LANES), lambda j: (0, j)), pl.BlockSpec((1, LANES), lambda j: (0, j))],
        out_specs=pl.BlockSpec((T, LANES), lambda j: (0, j)), out_shape=jax.ShapeDtypeStruct((T, F), bf16),
        compiler_params=_cp("parallel"),
    )(au, au, cw, cb)


def _ffn_mid_bwd(au, cw, cb, dz, *, name):
    T = au.shape[0]
    F = au.shape[1] // 2
    nb = F // LANES

    def body(a_ref, u_ref, w_ref, b_ref, dz_ref, da_ref, du_ref, dw_ref, db_ref):
        a = a_ref[...]
        w = w_ref[...]
        a2, a1 = _shift_down(a, 2), _shift_down(a, 1)
        ac = w[0:1] * a2 + w[1:2] * a1 + w[2:3] * a + b_ref[...]
        sg = jax.nn.sigmoid(ac)
        dz = dz_ref[...].astype(f32)
        du_ref[...] = (dz * ac * sg).astype(bf16)
        dac = dz * u_ref[...] * sg * (1.0 + ac * (1.0 - sg))
        da_ref[...] = (w[2:3] * dac + w[1:2] * _shift_up(dac, 1) + w[0:1] * _shift_up(dac, 2)).astype(bf16)
        rows = lax.broadcasted_iota(jnp.int32, (3, LANES), 0)
        s0 = jnp.sum(dac * a2, axis=0, keepdims=True)
        s1 = jnp.sum(dac * a1, axis=0, keepdims=True)
        s2 = jnp.sum(dac * a, axis=0, keepdims=True)
        dw_ref[...] = jnp.where(rows == 0, s0, jnp.where(rows == 1, s1, s2))
        db_ref[...] = jnp.sum(dac, axis=0, keepdims=True)

    col = lambda off: pl.BlockSpec((T, LANES), lambda j: (0, off + j))
    da, du, dw, db = pl.pallas_call(
        body, name=name, grid=(nb,),
        in_specs=[col(0), col(nb), pl.BlockSpec((3, LANES), lambda j: (0, j)), pl.BlockSpec((1, LANES), lambda j: (0, j)), col(0)],
        out_specs=[col(0), col(0), pl.BlockSpec((3, LANES), lambda j: (0, j)), pl.BlockSpec((1, LANES), lambda j: (0, j))],
        out_shape=[jax.ShapeDtypeStruct((T, F), bf16), jax.ShapeDtypeStruct((T, F), bf16),
                   jax.ShapeDtypeStruct((3, F), f32), jax.ShapeDtypeStruct((1, F), f32)],
        compiler_params=_cp("parallel"),
    )(au, au, cw, cb, dz)
    return jnp.concatenate([da, du], axis=1), dw, db


def _hgrn_lb(l):
    m = jnp.max(l, axis=0, keepdims=True)
    e = jnp.exp(l - m)
    return e[0:1] / jnp.sum(e, axis=0, keepdims=True)


def _hgrn_chunk(q, fx, lb):
    C = q.shape[0]
    sg = jax.nn.sigmoid(fx)
    F = lb + (1.0 - lb) * sg
    k = 1.0 - F
    logF = jnp.log(F)
    r = lax.broadcasted_iota(jnp.int32, (C, C), 0)
    c = lax.broadcasted_iota(jnp.int32, (C, C), 1)
    tril = (r >= c)
    b = _dot(tril.astype(f32), logF, precision=HI)
    bl = jnp.sum(logF, axis=0, keepdims=True)
    eb = jnp.exp(b)
    enb = jnp.exp(-b)
    elb = jnp.exp(bl - b)
    return dict(sg=sg, F=F, k=k, b=b, bl=bl, eb=eb, enb=enb, elb=elb, qd=q * eb, kd=k * enb, kl=k * elb, tril=tril)


def _hgrn_fwd(proj, lbl, ng, *, rb=512):
    T = proj.shape[0]
    rb = min(rb, T)
    cpb = rb // HGRN_CHUNK
    nblk = T // rb
    H = HGRN_HEADS

    def body(q_ref, f_ref, i_ref, g_ref, lbl_ref, ng_ref, y_ref, st_ref, S):
        @pl.when(pl.program_id(1) == 0)
        def _():
            S[...] = jnp.zeros_like(S)

        lb = _hgrn_lb(lbl_ref[...])
        ngv = ng_ref[...]
        for c in range(cpb):
            sl = pl.ds(c * HGRN_CHUNK, HGRN_CHUNK)
            v, gx = i_ref[sl, :], g_ref[sl, :]
            ch = _hgrn_chunk(q_ref[sl, :], f_ref[sl, :], lb)
            att = jnp.where(ch["tril"], _bdot(ch["qd"], ch["kd"], NT), 0.0)
            St = S[...]
            st_ref[0, c] = St
            o = _bdot(att, v) + _bdot(ch["qd"], St, NT)
            S[...] = St * jnp.exp(ch["bl"]) + _bdot(v, ch["kl"], TN)
            r = lax.rsqrt(jnp.mean(o * o, axis=-1, keepdims=True) + EPS)
            y_ref[sl, :] = (o * r * ngv * (gx * jax.nn.sigmoid(gx))).astype(bf16)

    col = lambda off: pl.BlockSpec((rb, LANES), lambda h, n: (n, off + h))
    return pl.pallas_call(
        body, name="hgrn_fwd", grid=(H, nblk),
        in_specs=[col(0), col(H), col(2 * H), col(3 * H), pl.BlockSpec((2, LANES), lambda h, n: (0, h)),
                  pl.BlockSpec((1, LANES), lambda h, n: (0, h))],
        out_specs=[pl.BlockSpec((rb, LANES), lambda h, n: (n, h)),
                   pl.BlockSpec((1, cpb, LANES, LANES), lambda h, n: (h, n, 0, 0))],
        out_shape=[jax.ShapeDtypeStruct((T, H * LANES), bf16),
                   jax.ShapeDtypeStruct((H, T // HGRN_CHUNK, LANES, LANES), f32)],
        scratch_shapes=[pltpu.VMEM((LANES, LANES), f32)], compiler_params=_cp("parallel", "arbitrary"),
    )(proj, proj, proj, proj, lbl, ng)


def _hgrn_bwd(proj, lbl, ng, states, dy, *, rb=512):
    T = proj.shape[0]
    rb = min(rb, T)
    cpb = rb // HGRN_CHUNK
    nblk = T // rb
    H = HGRN_HEADS
    C = HGRN_CHUNK

    def body(q_ref, f_ref, i_ref, g_ref, lbl_ref, ng_ref, st_ref, dy_ref,
             dq_ref, df_ref, di_ref, dg_ref, dl_ref, dng_ref, dS, dlb_acc, dng_acc):
        n = pl.program_id(1)

        @pl.when(n == 0)
        def _():
            dS[...] = jnp.zeros_like(dS)
            dlb_acc[...] = jnp.zeros_like(dlb_acc)
            dng_acc[...] = jnp.zeros_like(dng_acc)

        lb = _hgrn_lb(lbl_ref[...])
        ngv = ng_ref[...]
        r_i = lax.broadcasted_iota(jnp.int32, (C, C), 0)
        c_i = lax.broadcasted_iota(jnp.int32, (C, C), 1)
        triu = (c_i >= r_i).astype(f32)
        for c in reversed(range(cpb)):
            sl = pl.ds(c * C, C)
            q, v, gx = q_ref[sl, :], i_ref[sl, :], g_ref[sl, :]
            ch = _hgrn_chunk(q, f_ref[sl, :], lb)
            qd, kd, kl = ch["qd"], ch["kd"], ch["kl"]
            att = jnp.where(ch["tril"], _bdot(qd, kd, NT), 0.0)
            St = st_ref[0, c]
            o = _bdot(att, v) + _bdot(qd, St, NT)
            r = lax.rsqrt(jnp.mean(o * o, axis=-1, keepdims=True) + EPS)
            on = o * r
            sgg = jax.nn.sigmoid(gx)
            gate = gx * sgg
            dyv = dy_ref[sl, :].astype(f32)
            dg_ref[sl, :] = (dyv * on * ngv * sgg * (1.0 + gx * (1.0 - sgg))).astype(bf16)
            dng_acc[...] += jnp.sum(dyv * on * gate, axis=0, keepdims=True)
            don = dyv * ngv * gate
            do = r * (don - on * jnp.mean(don * on, axis=-1, keepdims=True))
            dSt = dS[...]
            dA = jnp.where(ch["tril"], _bdot(do, v, NT), 0.0)
            dv = _bdot(att, do, TN) + _bdot(kl, dSt, NT)
            dqd = _bdot(dA, kd) + _bdot(do, St)
            dkd = _bdot(dA, qd, TN)
            dkl = _bdot(v, dSt)
            dec = jnp.exp(ch["bl"])
            ddec = jnp.sum(St * dSt, axis=0, keepdims=True)
            dS[...] = _bdot(do, qd, TN) + dSt * dec
            dB = dqd * qd - dkd * kd - dkl * kl
            dbl = jnp.sum(dkl * kl, axis=0, keepdims=True) + ddec * dec
            dk = dkd * ch["enb"] + dkl * ch["elb"]
            dlogF = _dot(triu, dB, precision=HI) + dbl
            dF = dlogF / ch["F"] - dk
            sg = ch["sg"]
            dq_ref[sl, :] = (dqd * ch["eb"]).astype(bf16)
            di_ref[sl, :] = dv.astype(bf16)
            df_ref[sl, :] = (dF * (1.0 - lb) * sg * (1.0 - sg)).astype(bf16)
            dlb_acc[...] += jnp.sum(dF * (1.0 - sg), axis=0, keepdims=True)

        @pl.when(n == nblk - 1)
        def _():
            dl0 = dlb_acc[...] * lb * (1.0 - lb)
            rows = lax.broadcasted_iota(jnp.int32, (2, LANES), 0)
            dl_ref[...] = jnp.where(rows == 0, dl0, -dl0)
            dng_ref[...] = dng_acc[...]

    col = lambda off: pl.BlockSpec((rb, LANES), lambda h, n: (nblk - 1 - n, off + h))
    vec = lambda rows: pl.BlockSpec((rows, LANES), lambda h, n: (0, h))
    outc = pl.BlockSpec((rb, LANES), lambda h, n: (nblk - 1 - n, h))
    tok = jax.ShapeDtypeStruct((T, H * LANES), bf16)
    return pl.pallas_call(
        body, name="hgrn_bwd", grid=(H, nblk),
        in_specs=[col(0), col(H), col(2 * H), col(3 * H), vec(2), vec(1),
                  pl.BlockSpec((1, cpb, LANES, LANES), lambda h, n: (h, nblk - 1 - n, 0, 0)), col(0)],
        out_specs=[outc, outc, outc, outc, vec(2), vec(1)],
        out_shape=[tok, tok, tok, tok, jax.ShapeDtypeStruct((2, H * LANES), f32), jax.ShapeDtypeStruct((1, H * LANES), f32)],
        scratch_shapes=[pltpu.VMEM((LANES, LANES), f32), pltpu.VMEM((1, LANES), f32), pltpu.VMEM((1, LANES), f32)],
        compiler_params=_cp("parallel", "arbitrary"),
    )(proj, proj, proj, proj, lbl, ng, states, dy)


def _s5_disc_math(ar, ai, ldt, br, bi):
    dt = jnp.exp(ldt)
    mag = jnp.exp(ar * dt)
    abr, abi = mag * jnp.cos(ai * dt), mag * jnp.sin(ai * dt)
    den = ar * ar + ai * ai
    xr, xi = abr - 1.0, abi
    cr = (xr * ar + xi * ai) / den
    ci = (xi * ar - xr * ai) / den
    return abr, abi, cr * br - ci * bi, cr * bi + ci * br


def _s5_disc_fwd(ar, ai, ldt, br, bi):
    def body(ar_ref, ai_ref, ldt_ref, br_ref, bi_ref, o0, o1, o2, o3):
        outs = _s5_disc_math(ar_ref[...], ai_ref[...], ldt_ref[...], br_ref[...], bi_ref[...])
        for o, v in zip((o0, o1, o2, o3), outs):
            o[...] = v

    return pl.pallas_call(
        body, name="s5_disc_fwd",
        out_shape=[jax.ShapeDtypeStruct(ar.shape, f32)] * 2 + [jax.ShapeDtypeStruct(br.shape, f32)] * 2,
    )(ar, ai, ldt, br, bi)


def _s5_disc_bwd(ar, ai, ldt, br, bi, cts):
    def body(ar_ref, ai_ref, ldt_ref, br_ref, bi_ref, c0, c1, c2, c3, o0, o1, o2, o3, o4):
        _, vjp = jax.vjp(_s5_disc_math, ar_ref[...], ai_ref[...], ldt_ref[...], br_ref[...], bi_ref[...])
        for o, v in zip((o0, o1, o2, o3, o4), vjp((c0[...], c1[...], c2[...], c3[...]))):
            o[...] = v

    return pl.pallas_call(
        body, name="s5_disc_bwd",
        out_shape=[jax.ShapeDtypeStruct(ar.shape, f32)] * 3 + [jax.ShapeDtypeStruct(br.shape, f32)] * 2,
    )(ar, ai, ldt, br, bi, *cts)


S5_LC = 512
S5_NLC = S5_N // S5_LC
S5_UB = 4


def _cmul(ar, ai, xr, xi):
    return ar * xr - ai * xi, ar * xi + ai * xr


def _cpow(ar, ai, n):
    rr, ri = None, None
    br, bi = ar, ai
    while n:
        if n & 1:
            rr, ri = (br, bi) if rr is None else _cmul(rr, ri, br, bi)
        n >>= 1
        if n:
            br, bi = _cmul(br, bi, br, bi)
    return rr, ri


def _s5_bu(u_ref, bre_ref, bim_ref, xr, xi):
    for k in range(S5_UB):
        uk = u_ref[:, k * LANES:(k + 1) * LANES].astype(bf16)
        xr[:, k * S5_LC:(k + 1) * S5_LC] = _dot(uk, bre_ref[k])
        xi[:, k * S5_LC:(k + 1) * S5_LC] = _dot(uk, bim_ref[k])


def _s5_scan(xr, xi, sr, si, ar_ref, ai_ref, nsteps, store):
    for c in range(S5_NLC):
        cs = slice(c * S5_LC, (c + 1) * S5_LC)
        a_r = jnp.broadcast_to(ar_ref[:, cs], (S5_SEG, S5_LC))
        a_i = jnp.broadcast_to(ai_ref[:, cs], (S5_SEG, S5_LC))

        def step(j, carry, cs=cs, a_r=a_r, a_i=a_i):
            pr, pi = carry
            rows = pl.ds(pl.multiple_of(j * S5_SEG, S5_SEG), S5_SEG)
            nr = a_r * pr - a_i * pi + xr[rows, cs]
            ni = a_r * pi + a_i * pr + xi[rows, cs]
            if store:
                xr[rows, cs] = nr
                xi[rows, cs] = ni
            return nr, ni

        fr, fi = lax.fori_loop(0, nsteps, step, (sr[:, cs], si[:, cs]))
        sr[:, cs] = fr
        si[:, cs] = fi


def _s5_rscan(dr, di, xr, xi, s0r, s0i, gr, gi, acc_r, acc_i, ar_ref, ai_ref, nsteps):
    for c in range(S5_NLC):
        cs = slice(c * S5_LC, (c + 1) * S5_LC)
        a_r = jnp.broadcast_to(ar_ref[:, cs], (S5_SEG, S5_LC))
        a_i = jnp.broadcast_to(ai_ref[:, cs], (S5_SEG, S5_LC))

        def step(jj, carry, cs=cs, a_r=a_r, a_i=a_i):
            pr, pi, cr, ci = carry
            j = nsteps - 1 - jj
            rows = pl.ds(pl.multiple_of(j * S5_SEG, S5_SEG), S5_SEG)
            nr = dr[rows, cs] + a_r * pr + a_i * pi
            ni = di[rows, cs] + a_r * pi - a_i * pr
            dr[rows, cs] = nr
            di[rows, cs] = ni
            if acc_r is not None:
                prev = pl.ds(pl.multiple_of(jnp.maximum(j - 1, 0) * S5_SEG, S5_SEG), S5_SEG)
                first = j == 0
                pr_s = jnp.where(first, s0r[:, cs], xr[prev, cs])
                pi_s = jnp.where(first, s0i[:, cs], xi[prev, cs])
                cr = cr + nr * pr_s + ni * pi_s
                ci = ci - nr * pi_s + ni * pr_s
            return nr, ni, cr, ci

        z = jnp.zeros((S5_SEG, S5_LC), f32)
        init = (gr[:, cs], gi[:, cs], z, z)
        fr, fi, cr, ci = lax.fori_loop(0, nsteps, step, init)
        gr[:, cs] = fr
        gi[:, cs] = fi
        if acc_r is not None:
            acc_r[:, cs] += cr
            acc_i[:, cs] += ci


def _s5_seg_carry(fr, fi, ar, ai, seg_len, reverse):
    pr, pi = _cpow(ar, ai if not reverse else -ai, seg_len)
    rows = lax.broadcasted_iota(jnp.int32, fr.shape, 0)
    cr, ci = jnp.zeros_like(fr), jnp.zeros_like(fi)
    sh = (S5_SEG - 1) if reverse else 1
    fr_s, fi_s = pltpu.roll(fr, sh, 0), pltpu.roll(fi, sh, 0)
    order = range(S5_SEG - 2, -1, -1) if reverse else range(1, S5_SEG)
    for r in order:
        c_r, c_i = pltpu.roll(cr, sh, 0), pltpu.roll(ci, sh, 0)
        m_r, m_i = _cmul(pr, pi, c_r, c_i)
        cr = jnp.where(rows == r, m_r + fr_s, cr)
        ci = jnp.where(rows == r, m_i + fi_s, ci)
    return cr, ci


def _gelu_parts(y):
    c0 = math.sqrt(2.0 / math.pi)
    t = jnp.tanh(c0 * (y + 0.044715 * y * y * y))
    z = 0.5 * y * (1.0 + t)
    dz = 0.5 * (1.0 + t) + 0.5 * y * (1.0 - t * t) * c0 * (1.0 + 3.0 * 0.044715 * y * y)
    return z, dz


def _s5_y(xr, xi, u_ref, cre_ref, cim_ref, d_ref):
    ys = []
    for k in range(S5_UB):
        cs = slice(k * S5_LC, (k + 1) * S5_LC)
        ys.append(_bdot(xr[:, cs], cre_ref[k]) - _bdot(xi[:, cs], cim_ref[k]))
    return jnp.concatenate(ys, axis=1) + d_ref[...] * u_ref[...]


def _s5_specs(T, rb, rev=False):
    nblk = T // rb
    blk = (lambda i: (nblk - 1 - i, 0)) if rev else (lambda i: (i, 0))
    tok = pl.BlockSpec((rb, 4 * LANES), blk)
    bmat = pl.BlockSpec((S5_UB, LANES, S5_LC), lambda i: (0, 0, 0))
    cmat = pl.BlockSpec((S5_UB, S5_LC, LANES), lambda i: (0, 0, 0))
    avec = pl.BlockSpec((1, S5_N), lambda i: (0, 0))
    seg = pl.BlockSpec((S5_SEG, S5_N), lambda i: (0, 0))
    cvec = pl.BlockSpec((1, 4 * LANES), lambda i: (0, 0))
    s0 = pl.BlockSpec((1, S5_SEG, S5_N), (lambda i: (nblk - 1 - i, 0, 0)) if rev else (lambda i: (i, 0, 0)))
    return dict(tok=tok, bmat=bmat, cmat=cmat, avec=avec, seg=seg, cvec=cvec, s0=s0, nblk=nblk)


def _s5_final(u, bre, bim, ar, ai, *, rb):
    T = u.shape[0]
    sp = _s5_specs(T, rb)

    def body(u_ref, bre_ref, bim_ref, ar_ref, ai_ref, fr_ref, fi_ref, xr, xi):
        @pl.when(pl.program_id(0) == 0)
        def _():
            fr_ref[...] = jnp.zeros_like(fr_ref)
            fi_ref[...] = jnp.zeros_like(fi_ref)

        _s5_bu(u_ref, bre_ref, bim_ref, xr, xi)
        _s5_scan(xr, xi, fr_ref, fi_ref, ar_ref, ai_ref, rb // S5_SEG, False)

    return pl.pallas_call(
        body, name="s5_final", grid=(sp["nblk"],),
        in_specs=[sp["tok"], sp["bmat"], sp["bmat"], sp["avec"], sp["avec"]], out_specs=[sp["seg"], sp["seg"]],
        out_shape=[jax.ShapeDtypeStruct((S5_SEG, S5_N), f32)] * 2,
        scratch_shapes=[pltpu.VMEM((rb, S5_N), f32)] * 2, compiler_params=_cp("arbitrary"),
    )(u, bre, bim, ar, ai)


def _s5_fwd(u, bre, bim, ar, ai, fr, fi, cre, cim, dsk, wg, bg, *, rb):
    T = u.shape[0]
    sp = _s5_specs(T, rb)
    seg_len = T // S5_SEG

    def body(u_ref, bre_ref, bim_ref, ar_ref, ai_ref, fr_ref, fi_ref, cre_ref, cim_ref, d_ref, wg_ref, bg_ref,
             o_ref, s0r_ref, s0i_ref, xr, xi, sr, si):
        @pl.when(pl.program_id(0) == 0)
        def _():
            i_r, i_i = _s5_seg_carry(fr_ref[...], fi_ref[...], ar_ref[...], ai_ref[...], seg_len, False)
            sr[...] = i_r
            si[...] = i_i

        s0r_ref[0] = sr[...]
        s0i_ref[0] = si[...]
        _s5_bu(u_ref, bre_ref, bim_ref, xr, xi)
        _s5_scan(xr, xi, sr, si, ar_ref, ai_ref, rb // S5_SEG, True)
        y = _s5_y(xr, xi, u_ref, cre_ref, cim_ref, d_ref)
        z, _ = _gelu_parts(y)
        v = _bdot(z, wg_ref[...]) + bg_ref[...]
        o_ref[...] = (z * jax.nn.sigmoid(v)).astype(bf16)

    wspec = pl.BlockSpec((4 * LANES, 4 * LANES), lambda i: (0, 0))
    return pl.pallas_call(
        body, name="s5_fwd", grid=(sp["nblk"],),
        in_specs=[sp["tok"], sp["bmat"], sp["bmat"], sp["avec"], sp["avec"], sp["seg"], sp["seg"], sp["cmat"], sp["cmat"],
                  sp["cvec"], wspec, sp["cvec"]],
        out_specs=[sp["tok"], sp["s0"], sp["s0"]],
        out_shape=[jax.ShapeDtypeStruct((T, 4 * LANES), bf16)] + [jax.ShapeDtypeStruct((sp["nblk"], S5_SEG, S5_N), f32)] * 2,
        scratch_shapes=[pltpu.VMEM((rb, S5_N), f32)] * 2 + [pltpu.VMEM((S5_SEG, S5_N), f32)] * 2,
        compiler_params=_cp("arbitrary"),
    )(u, bre, bim, ar, ai, fr, fi, cre, cim, dsk, wg, bg)


def _s5_bwd_a(u, bre, bim, ar, ai, s0r, s0i, cre, cim, cret, cimt, dsk, wg, bg, dout, *, rb):
    T = u.shape[0]
    sp = _s5_specs(T, rb, rev=True)

    def body(u_ref, bre_ref, bim_ref, ar_ref, ai_ref, s0r_ref, s0i_ref, cre_ref, cim_ref, cret_ref, cimt_ref,
             d_ref, wg_ref, bg_ref, do_ref, dy_ref, glr_ref, gli_ref, dcre_ref, dcim_ref, dd_ref, dwg_ref, dbg_ref,
             xr, xi, dr, di, sr, si):
        @pl.when(pl.program_id(0) == 0)
        def _():
            for r in (glr_ref, gli_ref, dcre_ref, dcim_ref, dd_ref, dwg_ref, dbg_ref):
                r[...] = jnp.zeros_like(r)

        sr[...] = s0r_ref[0]
        si[...] = s0i_ref[0]
        _s5_bu(u_ref, bre_ref, bim_ref, xr, xi)
        _s5_scan(xr, xi, sr, si, ar_ref, ai_ref, rb // S5_SEG, True)
        uv = u_ref[...]
        y = _s5_y(xr, xi, u_ref, cre_ref, cim_ref, d_ref)
        z, gz = _gelu_parts(y)
        v = _bdot(z, wg_ref[...]) + bg_ref[...]
        sg = jax.nn.sigmoid(v)
        dov = do_ref[...].astype(f32)
        dv = dov * z * sg * (1.0 - sg)
        dz = dov * sg + _bdot(dv, wg_ref[...], NT)
        dy = dz * gz
        dy_ref[...] = dy
        dwg_ref[...] += _bdot(z, dv, TN)
        dbg_ref[...] += jnp.sum(dv, axis=0, keepdims=True)
        dd_ref[...] += jnp.sum(dy * uv, axis=0, keepdims=True)
        for k in range(S5_UB):
            cs = slice(k * S5_LC, (k + 1) * S5_LC)
            dyk = dy[:, k * LANES:(k + 1) * LANES]
            dcre_ref[k] += _bdot(xr[:, cs], dyk, TN)
            dcim_ref[k] -= _bdot(xi[:, cs], dyk, TN)
            dr[:, cs] = _bdot(dyk, cret_ref[k])
            di[:, cs] = -_bdot(dyk, cimt_ref[k])
        _s5_rscan(dr, di, None, None, None, None, glr_ref, gli_ref, None, None, ar_ref, ai_ref, rb // S5_SEG)

    wspec = pl.BlockSpec((4 * LANES, 4 * LANES), lambda i: (0, 0))
    return pl.pallas_call(
        body, name="s5_bwd_a", grid=(sp["nblk"],),
        in_specs=[sp["tok"], sp["bmat"], sp["bmat"], sp["avec"], sp["avec"], sp["s0"], sp["s0"], sp["cmat"], sp["cmat"],
                  sp["bmat"], sp["bmat"], sp["cvec"], wspec, sp["cvec"], sp["tok"]],
        out_specs=[sp["tok"], sp["seg"], sp["seg"], sp["cmat"], sp["cmat"], sp["cvec"], wspec, sp["cvec"]],
        out_shape=[jax.ShapeDtypeStruct((T, 4 * LANES), f32)] + [jax.ShapeDtypeStruct((S5_SEG, S5_N), f32)] * 2
        + [jax.ShapeDtypeStruct((S5_UB, S5_LC, LANES), f32)] * 2
        + [jax.ShapeDtypeStruct((1, 4 * LANES), f32), jax.ShapeDtypeStruct((4 * LANES, 4 * LANES), f32),
           jax.ShapeDtypeStruct((1, 4 * LANES), f32)],
        scratch_shapes=[pltpu.VMEM((rb, S5_N), f32)] * 4 + [pltpu.VMEM((S5_SEG, S5_N), f32)] * 2,
        compiler_params=_cp("arbitrary"),
    )(u, bre, bim, ar, ai, s0r, s0i, cre, cim, cret, cimt, dsk, wg, bg, dout)


def _s5_bwd_b(u, bre, bim, bret, bimt, ar, ai, s0r, s0i, glr, gli, cret, cimt, dsk, dy, *, rb):
    T = u.shape[0]
    sp = _s5_specs(T, rb, rev=True)
    seg_len = T // S5_SEG
    nblk = sp["nblk"]

    def body(u_ref, bre_ref, bim_ref, bret_ref, bimt_ref, ar_ref, ai_ref, s0r_ref, s0i_ref, glr_ref, gli_ref,
             cret_ref, cimt_ref, d_ref, dy_ref, du_ref, dbre_ref, dbim_ref, dar_ref, dai_ref,
             xr, xi, dr, di, sr, si, gr, gi, acc_r, acc_i):
        @pl.when(pl.program_id(0) == 0)
        def _():
            x_r, x_i = _s5_seg_carry(glr_ref[...], gli_ref[...], ar_ref[...], ai_ref[...], seg_len, True)
            gr[...] = x_r
            gi[...] = x_i
            acc_r[...] = jnp.zeros_like(acc_r)
            acc_i[...] = jnp.zeros_like(acc_i)
            dbre_ref[...] = jnp.zeros_like(dbre_ref)
            dbim_ref[...] = jnp.zeros_like(dbim_ref)

        sr[...] = s0r_ref[0]
        si[...] = s0i_ref[0]
        _s5_bu(u_ref, bre_ref, bim_ref, xr, xi)
        _s5_scan(xr, xi, sr, si, ar_ref, ai_ref, rb // S5_SEG, True)
        dy = dy_ref[...]
        for k in range(S5_UB):
            cs = slice(k * S5_LC, (k + 1) * S5_LC)
            dyk = dy[:, k * LANES:(k + 1) * LANES]
            dr[:, cs] = _bdot(dyk, cret_ref[k])
            di[:, cs] = -_bdot(dyk, cimt_ref[k])
        sr[...] = s0r_ref[0]
        si[...] = s0i_ref[0]
        _s5_rscan(dr, di, xr, xi, sr, si, gr, gi, acc_r, acc_i, ar_ref, ai_ref, rb // S5_SEG)
        dus = []
        for k in range(S5_UB):
            cs = slice(k * S5_LC, (k + 1) * S5_LC)
            uk = u_ref[:, k * LANES:(k + 1) * LANES]
            dbre_ref[k] += _bdot(uk, dr[:, cs], TN)
            dbim_ref[k] += _bdot(uk, di[:, cs], TN)
            dus.append(_bdot(dr[:, cs], bret_ref[k]) + _bdot(di[:, cs], bimt_ref[k]))
        du_ref[...] = (jnp.concatenate(dus, axis=1) + d_ref[...] * dy).astype(bf16)

        @pl.when(pl.program_id(0) == nblk - 1)
        def _():
            dar_ref[...] = jnp.sum(acc_r[...], axis=0, keepdims=True)
            dai_ref[...] = jnp.sum(acc_i[...], axis=0, keepdims=True)

    return pl.pallas_call(
        body, name="s5_bwd_b", grid=(nblk,),
        in_specs=[sp["tok"], sp["bmat"], sp["bmat"], sp["cmat"], sp["cmat"], sp["avec"], sp["avec"], sp["s0"], sp["s0"],
                  sp["seg"], sp["seg"], sp["bmat"], sp["bmat"], sp["cvec"], sp["tok"]],
        out_specs=[sp["tok"], sp["bmat"], sp["bmat"], sp["avec"], sp["avec"]],
        out_shape=[jax.ShapeDtypeStruct((T, 4 * LANES), bf16)] + [jax.ShapeDtypeStruct((S5_UB, LANES, S5_LC), f32)] * 2
        + [jax.ShapeDtypeStruct((1, S5_N), f32)] * 2,
        scratch_shapes=[pltpu.VMEM((rb, S5_N), f32)] * 4 + [pltpu.VMEM((S5_SEG, S5_N), f32)] * 6,
        compiler_params=_cp("arbitrary"),
    )(u, bre, bim, bret, bimt, ar, ai, s0r, s0i, glr, gli, cret, cimt, dsk, dy)


def _blockdiag(w, transpose=False):
    if transpose:
        w = jnp.swapaxes(w, 1, 2)
    g, a, b = w.shape
    eye = jnp.eye(8, dtype=w.dtype)
    return jnp.einsum("kgab,gj->kgajb", w.reshape(4, 8, a, b), eye).reshape(4, 8 * a, 8 * b)


def _blockdiag_t(m, a, b):
    eye = jnp.eye(8, dtype=m.dtype)
    return jnp.einsum("kgajb,gj->kgab", m.reshape(4, 8, a, 8, b), eye).reshape(32, a, b)


ROT = MLA_ROPE // 2


def _rope_tables(positions):
    freqs = ROPE_THETA ** (-jnp.arange(0, MLA_ROPE, 2, dtype=f32) / MLA_ROPE)
    ang = positions.astype(f32)[:, None] * freqs
    cos, sin, z = jnp.cos(ang), jnp.sin(ang), jnp.zeros_like(ang)
    return (jnp.concatenate([cos, cos, z, z], axis=1), jnp.concatenate([-sin, z, z, z], axis=1),
            jnp.concatenate([z, sin, z, z], axis=1))


def _rot(x, c, sa, sb):
    return x * c + pltpu.roll(x, LANES - ROT, 1) * sa + pltpu.roll(x, ROT, 1) * sb


def _rot_t(dy, c, sa, sb):
    return dy * c + pltpu.roll(dy * sa, ROT, 1) + pltpu.roll(dy * sb, LANES - ROT, 1)


def _rms(xv, g):
    return xv * lax.rsqrt(jnp.mean(xv * xv, axis=-1, keepdims=True) + EPS) * g


QW, KVW = MLA_Q_RANK, MLA_KV_RANK
ODD_PAD = QW + KVW + LANES


def _mla_prep_fwd(proj, qg, kvg, tabs, *, tm=512):
    T = proj.shape[0]
    tm = _tile(T, tm)

    def body(p_ref, qg_ref, kvg_ref, c_ref, sa_ref, sb_ref, cq_ref, ckv_ref, kr_ref):
        cq_ref[...] = _rms(p_ref[:, :QW], qg_ref[...]).astype(bf16)
        ckv_ref[...] = _rms(p_ref[:, QW:QW + KVW], kvg_ref[...]).astype(bf16)
        kr_ref[...] = _rot(p_ref[:, QW + KVW:], c_ref[...], sa_ref[...], sb_ref[...]).astype(bf16)

    row = lambda w: pl.BlockSpec((tm, w), lambda i: (i, 0))
    vec = lambda w: pl.BlockSpec((1, w), lambda i: (0, 0))
    return pl.pallas_call(
        body, name="mla_prep_fwd", grid=(T // tm,),
        in_specs=[row(ODD_PAD), vec(QW), vec(KVW), row(LANES), row(LANES), row(LANES)],
        out_specs=[row(QW), row(KVW), row(LANES)],
        out_shape=[jax.ShapeDtypeStruct((T, QW), bf16), jax.ShapeDtypeStruct((T, KVW), bf16),
                   jax.ShapeDtypeStruct((T, LANES), bf16)],
        compiler_params=_cp("parallel"),
    )(proj, qg, kvg, *tabs)


def _mla_prep_bwd(proj, qg, kvg, tabs, dcqn, dckvn, dkr_heads, *, tm=512):
    T = proj.shape[0]
    tm = _tile(T, tm)

    def body(p_ref, qg_ref, kvg_ref, c_ref, sa_ref, sb_ref, dcq_ref, dckv_ref, dkr_ref, dp_ref, dqg_ref, dkvg_ref):
        dcq, dqg = _rms_bwd_math(p_ref[:, :QW], qg_ref[...], dcq_ref[...])
        dckv, dkvg = _rms_bwd_math(p_ref[:, QW:QW + KVW], kvg_ref[...], dckv_ref[...])
        dk = dkr_ref[:, :LANES]
        for h in range(1, MLA_HEADS):
            dk = dk + dkr_ref[:, h * LANES:(h + 1) * LANES]
        dkr = _rot_t(dk, c_ref[...], sa_ref[...], sb_ref[...])
        dp_ref[...] = jnp.concatenate([dcq, dckv, dkr], axis=1).astype(bf16)

        @pl.when(pl.program_id(0) == 0)
        def _():
            dqg_ref[...] = dqg
            dkvg_ref[...] = dkvg

        @pl.when(pl.program_id(0) > 0)
        def _():
            dqg_ref[...] += dqg
            dkvg_ref[...] += dkvg

    row = lambda w: pl.BlockSpec((tm, w), lambda i: (i, 0))
    vec = lambda w: pl.BlockSpec((1, w), lambda i: (0, 0))
    return pl.pallas_call(
        body, name="mla_prep_bwd", grid=(T // tm,),
        in_specs=[row(ODD_PAD), vec(QW), vec(KVW), row(LANES), row(LANES), row(LANES), row(QW), row(KVW),
                  row(MLA_HEADS * LANES)],
        out_specs=[row(ODD_PAD), vec(QW), vec(KVW)],
        out_shape=[jax.ShapeDtypeStruct((T, ODD_PAD), bf16), jax.ShapeDtypeStruct((1, QW), f32),
                   jax.ShapeDtypeStruct((1, KVW), f32)],
        compiler_params=_cp("arbitrary"),
    )(proj, qg, kvg, *tabs, dcqn, dckvn, dkr_heads)


HQ = 2 * LANES
QK_SCALE = MLA_QK ** -0.5


def _q_post(q, tabs, *, transpose, name, tm=512):
    T = q.shape[0]
    tm = _tile(T, tm)

    def body(q_ref, c_ref, sa_ref, sb_ref, o_ref):
        qv = q_ref[...].astype(f32)
        rope = (_rot_t if transpose else _rot)(qv[:, LANES:], c_ref[...], sa_ref[...], sb_ref[...])
        o_ref[...] = (jnp.concatenate([qv[:, :LANES], rope], axis=1) * QK_SCALE).astype(bf16)

    tab = pl.BlockSpec((tm, LANES), lambda i, h: (i, 0))
    blk = pl.BlockSpec((tm, HQ), lambda i, h: (i, h))
    return pl.pallas_call(
        body, name=name, grid=(T // tm, MLA_HEADS), in_specs=[blk, tab, tab, tab], out_specs=blk,
        out_shape=jax.ShapeDtypeStruct(q.shape, bf16), compiler_params=_cp("parallel", "parallel"),
    )(q, *tabs)


def _causal_mask(i, j, tq, tk):
    r = lax.broadcasted_iota(jnp.int32, (tq, tk), 0) + i * tq
    c = lax.broadcasted_iota(jnp.int32, (tq, tk), 1) + j * tk
    return c <= r


def _flash_fwd(q, kv, kr, *, tb=512):
    T = q.shape[0]
    tb = _tile(T, tb)
    nb = T // tb
    H = MLA_HEADS

    def body(q_ref, kn_ref, v_ref, kr_ref, o_ref, lse_ref, m_s, l_s, acc):
        i, j = pl.program_id(1), pl.program_id(2)

        @pl.when(j == 0)
        def _():
            m_s[...] = jnp.full_like(m_s, -jnp.inf)
            l_s[...] = jnp.zeros_like(l_s)
            acc[...] = jnp.zeros_like(acc)

        @pl.when(j <= i)
        def _():
            k = jnp.concatenate([kn_ref[...], kr_ref[...]], axis=1)
            s = _dot(q_ref[...], k, NT)
            s = jnp.where(_causal_mask(i, j, tb, tb), s, -jnp.inf)
            m_new = jnp.maximum(m_s[...], jnp.max(s, axis=-1, keepdims=True))
            alpha = jnp.exp(m_s[...] - m_new)
            p = jnp.exp(s - m_new)
            l_s[...] = alpha * l_s[...] + jnp.sum(p, axis=-1, keepdims=True)
            acc[...] = alpha * acc[...] + _dot(p.astype(bf16), v_ref[...])
            m_s[...] = m_new

        @pl.when(j == i)
        def _():
            o_ref[...] = (acc[...] / l_s[...]).astype(bf16)
            lse_ref[0] = m_s[...] + jnp.log(l_s[...])

    kblk = lambda off: pl.BlockSpec((tb, LANES), lambda h, i, j: (jnp.minimum(j, i), 2 * h + off))
    return pl.pallas_call(
        body, name="flash_fwd", grid=(H, nb, nb),
        in_specs=[pl.BlockSpec((tb, HQ), lambda h, i, j: (i, h)), kblk(0), kblk(1),
                  pl.BlockSpec((tb, LANES), lambda h, i, j: (jnp.minimum(j, i), 0))],
        out_specs=[pl.BlockSpec((tb, LANES), lambda h, i, j: (i, h)), pl.BlockSpec((1, tb, 1), lambda h, i, j: (h, i, 0))],
        out_shape=[jax.ShapeDtypeStruct((T, H * LANES), bf16), jax.ShapeDtypeStruct((H, T, 1), f32)],
        scratch_shapes=[pltpu.VMEM((tb, 1), f32), pltpu.VMEM((tb, 1), f32), pltpu.VMEM((tb, LANES), f32)],
        compiler_params=_cp("parallel", "parallel", "arbitrary"),
    )(q, kv, kv, kr)


def _attn_delta(o, do, *, tm=512):
    T = o.shape[0]
    tm = _tile(T, tm)

    def body(o_ref, do_ref, d_ref):
        d_ref[0] = jnp.sum(o_ref[...].astype(f32) * do_ref[...], axis=-1, keepdims=True)

    blk = pl.BlockSpec((tm, LANES), lambda h, i: (i, h))
    return pl.pallas_call(
        body, name="attn_delta", grid=(MLA_HEADS, T // tm), in_specs=[blk, blk],
        out_specs=pl.BlockSpec((1, tm, 1), lambda h, i: (h, i, 0)),
        out_shape=jax.ShapeDtypeStruct((MLA_HEADS, T, 1), f32), compiler_params=_cp("parallel", "parallel"),
    )(o, do)


def _flash_p_ds(q_ref, kn_ref, v_ref, kr_ref, do_ref, lse_ref, dl_ref, i, j, tb):
    k = jnp.concatenate([kn_ref[...], kr_ref[...]], axis=1)
    s = _dot(q_ref[...], k, NT)
    p = jnp.where(_causal_mask(i, j, tb, tb), jnp.exp(s - lse_ref[0]), 0.0)
    dp = _bdot(do_ref[...], v_ref[...], NT)
    ds = p * (dp - dl_ref[0])
    return k, p, ds


def _flash_bwd_kv(q, kv, kr, do, lse, delta, *, tb=512):
    T = q.shape[0]
    tb = _tile(T, tb)
    nb = T // tb
    H = MLA_HEADS

    def body(q_ref, kn_ref, v_ref, kr_ref, do_ref, lse_ref, dl_ref, dkv_ref, dkr_ref, dk_acc, dv_acc):
        j, ii = pl.program_id(1), pl.program_id(2)
        i = jnp.maximum(ii, j)

        @pl.when(ii == 0)
        def _():
            dk_acc[...] = jnp.zeros_like(dk_acc)
            dv_acc[...] = jnp.zeros_like(dv_acc)

        @pl.when(ii >= j)
        def _():
            _, p, ds = _flash_p_ds(q_ref, kn_ref, v_ref, kr_ref, do_ref, lse_ref, dl_ref, i, j, tb)
            dv_acc[...] += _bdot(p, do_ref[...], TN)
            dk_acc[...] += _bdot(ds, q_ref[...], TN)

        @pl.when(ii == nb - 1)
        def _():
            dkv_ref[...] = jnp.concatenate([dk_acc[:, :LANES], dv_acc[...]], axis=1).astype(bf16)
            dkr_ref[...] = dk_acc[:, LANES:]

    qi = lambda h, j, i: jnp.maximum(i, j)
    kblk = lambda off: pl.BlockSpec((tb, LANES), lambda h, j, i: (j, 2 * h + off))
    vec = pl.BlockSpec((1, tb, 1), lambda h, j, i: (h, qi(h, j, i), 0))
    return pl.pallas_call(
        body, name="flash_bwd_kv", grid=(H, nb, nb),
        in_specs=[pl.BlockSpec((tb, HQ), lambda h, j, i: (qi(h, j, i), h)), kblk(0), kblk(1),
                  pl.BlockSpec((tb, LANES), lambda h, j, i: (j, 0)),
                  pl.BlockSpec((tb, LANES), lambda h, j, i: (qi(h, j, i), h)), vec, vec],
        out_specs=[pl.BlockSpec((tb, HQ), lambda h, j, i: (j, h)), pl.BlockSpec((tb, LANES), lambda h, j, i: (j, h))],
        out_shape=[jax.ShapeDtypeStruct((T, H * HQ), bf16), jax.ShapeDtypeStruct((T, H * LANES), f32)],
        scratch_shapes=[pltpu.VMEM((tb, HQ), f32), pltpu.VMEM((tb, LANES), f32)],
        compiler_params=_cp("parallel", "parallel", "arbitrary"),
    )(q, kv, kv, kr, do, lse, delta)


def _flash_bwd_q(q, kv, kr, do, lse, delta, *, tb=512):
    T = q.shape[0]
    tb = _tile(T, tb)
    nb = T // tb
    H = MLA_HEADS

    def body(q_ref, kn_ref, v_ref, kr_ref, do_ref, lse_ref, dl_ref, dq_ref, acc):
        i, j = pl.program_id(1), pl.program_id(2)

        @pl.when(j == 0)
        def _():
            acc[...] = jnp.zeros_like(acc)

        @pl.when(j <= i)
        def _():
            k, _, ds = _flash_p_ds(q_ref, kn_ref, v_ref, kr_ref, do_ref, lse_ref, dl_ref, i, j, tb)
            acc[...] += _bdot(ds, k)

        @pl.when(j == i)
        def _():
            dq_ref[...] = acc[...]

    kj = lambda h, i, j: jnp.minimum(j, i)
    kblk = lambda off: pl.BlockSpec((tb, LANES), lambda h, i, j: (kj(h, i, j), 2 * h + off))
    vec = pl.BlockSpec((1, tb, 1), lambda h, i, j: (h, i, 0))
    return pl.pallas_call(
        body, name="flash_bwd_q", grid=(H, nb, nb),
        in_specs=[pl.BlockSpec((tb, HQ), lambda h, i, j: (i, h)), kblk(0), kblk(1),
                  pl.BlockSpec((tb, LANES), lambda h, i, j: (kj(h, i, j), 0)),
                  pl.BlockSpec((tb, LANES), lambda h, i, j: (i, h)), vec, vec],
        out_specs=pl.BlockSpec((tb, HQ), lambda h, i, j: (i, h)),
        out_shape=jax.ShapeDtypeStruct((T, H * HQ), f32),
        scratch_shapes=[pltpu.VMEM((tb, HQ), f32)], compiler_params=_cp("parallel", "parallel", "arbitrary"),
    )(q, kv, kv, kr, do, lse, delta)


HBM_SPEC = pl.BlockSpec(memory_space=pltpu.HBM)
N_CHIPS = 4
N_DEV = 8

BIG = {"even_w_in": 1, "s5_w_glu": 0, "even_w_out": 0, "odd_w_in": 0, "mla_w_uq": 1, "mla_w_ukv": 1, "odd_w_out": 0,
       "ffn_w_in": 2, "ffn_w_out": 1}
LAYERED = ("ffn_w_in", "ffn_w_out")
GROUPS = {"even_in": ("even_w_in",), "even_rest": ("s5_w_glu", "even_w_out"), "ffn0": LAYERED,
          "odd": ("odd_w_in", "mla_w_uq", "mla_w_ukv", "odd_w_out"), "ffn1": LAYERED}
GROUP_LAYER = {"ffn0": 0, "ffn1": 1}


def _place():
    x, y, c = lax.axis_index("x"), lax.axis_index("y"), lax.axis_index("c")
    chips = [(1 - x, y), (x, 1 - y), (1 - x, 1 - y)]
    return x, y, c, chips


def _slab(ref, axis, k, size):
    start = pl.multiple_of(k * size, size if axis == 0 else LANES)
    idx = [slice(None)] * len(ref.shape)
    idx[axis] = pl.ds(start, size)
    return ref.at[tuple(idx)]


SEM_SPEC = pl.BlockSpec(memory_space=pltpu.SEMAPHORE)
ANY_SPEC = pl.BlockSpec(memory_space=pl.ANY)
EFFECT = pltpu.SideEffectType.DATAFLOW_SIDE_EFFECTING


def _hbm(a):
    return pltpu.with_memory_space_constraint(a, pltpu.HBM)


class _Gather:
    copies = 3

    def __init__(self, axis, size):
        self.axis, self.size = axis, size

    def view(self, land, kk):
        return _slab(land, self.axis, kk, self.size)

    def own(self, land, place):
        return self.view(land, 2 * place[0] + place[1])

    def sends(self, src, land, place):
        x, y, c, chips = place
        return [(self.own(land, place) if src is None else src, self.own(land, place), (*chip, c)) for chip in chips]

    def recvs(self, land, place):
        return [self.view(land, 2 * cx + cy) for cx, cy in place[3]]


class _Scatter:
    copies = 3

    def __init__(self, axis, size, layer=None):
        self.axis, self.size, self.layer = axis, size, layer

    def row(self, land, j):
        return land.at[j] if self.layer is None else land.at[j, self.layer]

    def sends(self, src, land, place):
        c, chips = place[2], place[3]
        return [(_slab(src, self.axis, 2 * cx + cy, self.size), self.row(land, j), (cx, cy, c))
                for j, (cx, cy) in enumerate(chips)]

    def recvs(self, land, place):
        return [self.row(land, j) for j in range(3)]


class _ToAll:
    copies = N_DEV - 1

    def __init__(self, size):
        self.size = size

    def sends(self, src, land, place):
        x, y, c, _ = place
        flip = lambda v, bit: 1 - v if bit else v
        own = _slab(land, 0, 4 * x + 2 * y + c, self.size)
        return [(own, own, (flip(x, m & 4), flip(y, m & 2), flip(c, m & 1))) for m in range(1, N_DEV)]

    def recvs(self, land, place):
        x, y, c, _ = place
        d = 4 * x + 2 * y + c
        return [_slab(land, 0, d ^ m, self.size) for m in range(1, N_DEV)]


def _unique(arrays):
    out, index = [], {}
    for a in arrays:
        if a is not None and id(a) not in index:
            index[id(a)] = len(out)
            out.append(a)
    return out, index


def _sem_base(routes):
    base = [0]
    for r in routes:
        base.append(base[-1] + r.copies)
    return base


def _push_start(name, items):
    n = len(items)
    base = _sem_base([it[0] for it in items])
    arrays, index = _unique([it[1] for it in items] + [it[2] for it in items])
    na = len(arrays)

    def body(*refs):
        arr, send, recv, token = refs[:na], refs[na], refs[na + 1], refs[-1]
        place = _place()
        for i, (route, src, land) in enumerate(items):
            s_ref = None if src is None else arr[index[id(src)]]
            for j, (s, d, dev) in enumerate(route.sends(s_ref, arr[index[id(land)]], place)):
                pltpu.make_async_remote_copy(src_ref=s, dst_ref=d, send_sem=send.at[base[i] + j], recv_sem=recv.at[base[i] + j],
                                             device_id=dev, device_id_type=MESH).start()
        token[...] = jnp.zeros_like(token)

    res = pl.pallas_call(
        body, name=name,
        out_shape=[pltpu.SemaphoreType.DMA((base[-1],)), pltpu.SemaphoreType.DMA((base[-1],))]
        + [pltpu.HBM(a.shape, a.dtype) for a in arrays] + [jax.ShapeDtypeStruct((SUBLANES, LANES), f32)],
        in_specs=[HBM_SPEC] * na, out_specs=[SEM_SPEC, SEM_SPEC] + [HBM_SPEC] * na + [pl.BlockSpec(memory_space=pltpu.VMEM)],
        input_output_aliases={i: 2 + i for i in range(na)},
        compiler_params=pltpu.CompilerParams(has_side_effects=EFFECT),
    )(*[_hbm(a) for a in arrays])
    thru = lambda a: None if a is None else res[2 + index[id(a)]]
    return (res[0], res[1]), [thru(it[1]) for it in items], [thru(it[2]) for it in items], res[-1]


def _push_wait(name, groups, after):
    arrays, index = _unique([a for _, _, srcs, lands in groups for a in list(srcs) + list(lands)])
    na, ng = len(arrays), len(groups)

    def body(*refs):
        arr, sems = refs[:na], refs[na:na + 2 * ng]
        place = _place()
        for g, (routes, _, srcs, lands) in enumerate(groups):
            send, recv = sems[2 * g], sems[2 * g + 1]
            base = _sem_base(routes)
            for i, route in enumerate(routes):
                src, land = None if srcs[i] is None else arr[index[id(srcs[i])]], arr[index[id(lands[i])]]
                for j, ((s, d, dev), mine) in enumerate(zip(route.sends(src, land, place), route.recvs(land, place))):
                    cp = pltpu.make_async_remote_copy(src_ref=s, dst_ref=mine, send_sem=send.at[base[i] + j],
                                                      recv_sem=recv.at[base[i] + j], device_id=dev,
                                                      device_id_type=MESH)
                    cp.wait_send()
                    cp.wait_recv()

    sem_args = [s for g in groups for s in g[1]]
    res = pl.pallas_call(
        body, name=name, out_shape=[pltpu.HBM(a.shape, a.dtype) for a in arrays],
        in_specs=[HBM_SPEC] * na + [SEM_SPEC] * (2 * ng) + [ANY_SPEC] * len(after), out_specs=[HBM_SPEC] * na,
        input_output_aliases={i: i for i in range(na)},
        compiler_params=pltpu.CompilerParams(has_side_effects=EFFECT),
    )(*arrays, *sem_args, *after)
    return [[res[index[id(a)]] for a in g[3]] for g in groups]


def _place_slab(block, axis, slabs, idx, dtype, *, name):
    R, C = block.shape
    tm = _rows(R, C)
    nr = R // tm
    out_map = (lambda i, k: (i, k[0])) if axis == 1 else (lambda i, k: (k[0] * nr + i, 0))

    def body(k_ref, x_ref, o_ref):
        o_ref[...] = x_ref[...].astype(dtype)

    full = (R, C * slabs) if axis == 1 else (R * slabs, C)
    return pl.pallas_call(
        body, name=name, out_shape=jax.ShapeDtypeStruct(full, dtype),
        grid_spec=pltpu.PrefetchScalarGridSpec(
            num_scalar_prefetch=1, grid=(nr,), in_specs=[pl.BlockSpec((tm, C), lambda i, k: (i, 0))],
            out_specs=pl.BlockSpec((tm, C), out_map)),
        compiler_params=_cp("parallel"),
    )(idx, block)


def _swap_with_sibling(parts, tag):
    names = list(parts)

    def body(*refs):
        n = len(names)
        ins, outs, send, recv = refs[:n], refs[n:2 * n], refs[-2], refs[-1]
        x, y, c, _ = _place()
        cps = [pltpu.make_async_remote_copy(src_ref=ins[a], dst_ref=outs[a], send_sem=send.at[a], recv_sem=recv.at[a],
                                            device_id=(x, y, 1 - c), device_id_type=MESH) for a in range(n)]
        for cp in cps:
            cp.start()
        for cp in cps:
            cp.wait_recv()
        for cp in cps:
            cp.wait_send()

    res = pl.pallas_call(
        body, name=f"swap_with_sibling_{tag}", in_specs=[HBM_SPEC] * len(names), out_specs=[HBM_SPEC] * len(names),
        out_shape=[jax.ShapeDtypeStruct(parts[n].shape, parts[n].dtype) for n in names],
        scratch_shapes=[pltpu.SemaphoreType.DMA((len(names),)), pltpu.SemaphoreType.DMA((len(names),))],
    )(*[parts[n] for n in names])
    return dict(zip(names, res))


ELEMENTWISE_BLOCK_BYTES = 1 << 20


def _rows(r, c):
    for t in (512, 256, 128, 64, 32, 16, 8):
        if r % t == 0 and t * c * 4 <= ELEMENTWISE_BLOCK_BYTES:
            return t
    return r


def _sum4(owns, axis, recv, kidx, *, name):
    L = len(owns)
    R, C = recv.shape[2:]
    tm = _rows(R, C)
    nr = R // tm

    def body(k_ref, *refs):
        own_refs, r_ref, out_ref = refs[:L], refs[L], refs[L + 1]
        for li in range(L):
            @pl.when(pl.program_id(0) == li)
            def _(o_ref=own_refs[li]):
                out_ref[...] = ((o_ref[...] + r_ref[0, 0].astype(f32)) + r_ref[1, 0].astype(f32)) + r_ref[2, 0].astype(f32)

    own_map = (lambda l, i, k: (i, k[0])) if axis == 1 else (lambda l, i, k: (k[0] * nr + i, 0))
    return pl.pallas_call(
        body, name=name, out_shape=jax.ShapeDtypeStruct((L * R, C), f32),
        grid_spec=pltpu.PrefetchScalarGridSpec(
            num_scalar_prefetch=1, grid=(L, nr),
            in_specs=[pl.BlockSpec((tm, C), own_map)] * L + [pl.BlockSpec((3, 1, tm, C), lambda l, i, k: (0, l, i, 0))],
            out_specs=pl.BlockSpec((tm, C), lambda l, i, k: (l * nr + i, 0))),
        compiler_params=_cp("parallel", "parallel"),
    )(kidx, *owns, recv)


def _adamw(w, m, v, parts, *, name):
    R, C = w.shape
    tm = _rows(R, C)
    npart = len(parts)

    def body(*refs):
        w_ref, m_ref, v_ref = refs[:3]
        g_ref, d_ref, m2_ref, v2_ref = refs[3 + npart:]
        g = refs[3][...]
        for p_ref in refs[4:3 + npart]:
            g = g + p_ref[...]
        g_ref[...] = g
        d_ref[...], m2_ref[...], v2_ref[...] = _adam_math(w_ref[...], m_ref[...], v_ref[...], g)

    blk = pl.BlockSpec((tm, C), lambda i: (i, 0))
    return pl.pallas_call(
        body, name=name, grid=(R // tm,),
        in_specs=[blk] * (3 + npart), out_specs=[blk] * 4,
        out_shape=[jax.ShapeDtypeStruct((R, C), f32)] * 4, compiler_params=_cp("parallel"),
    )(w, m, v, *parts)


def _adam_math(w, m, v, g):
    m2 = ADAM_B1 * m + (1.0 - ADAM_B1) * g
    v2 = ADAM_B2 * v + (1.0 - ADAM_B2) * (g * g)
    m_hat = m2 / (1.0 - ADAM_B1 ** ADAM_STEP)
    v_hat = v2 / (1.0 - ADAM_B2 ** ADAM_STEP)
    return -ADAM_LR * (m_hat / (jnp.sqrt(v_hat) + ADAM_EPS) + ADAM_WD * w), m2, v2


def _adamw_small(landed, w, m, v, kidx, ra, rb):
    rs = ra + N_CHIPS * rb

    def body(k_ref, l_ref, w_ref, m_ref, v_ref, g_ref, d_ref, m2_ref, v2_ref):
        mine = pl.multiple_of(ra + k_ref[0] * rb, SUBLANES)
        for lo, n, off in ((0, ra, 0), (ra, rb, mine)):
            g = l_ref[pl.ds(off, n), :]
            for d in range(1, N_DEV):
                g = g + l_ref[pl.ds(d * rs + off, n), :]
            rows = pl.ds(lo, n)
            delta, m2, v2 = _adam_math(w_ref[rows, :], m_ref[rows, :], v_ref[rows, :], g)
            g_ref[rows, :] = g
            d_ref[rows, :] = delta
            m2_ref[rows, :] = m2
            v2_ref[rows, :] = v2

    vmem = pl.BlockSpec(memory_space=pltpu.VMEM)
    return pl.pallas_call(
        body, name="adamw_small", out_shape=[jax.ShapeDtypeStruct(w.shape, f32)] * 4,
        grid_spec=pltpu.PrefetchScalarGridSpec(num_scalar_prefetch=1, grid=(), in_specs=[vmem] * 4, out_specs=[vmem] * 4),
        compiler_params=_cp(),
    )(kidx, landed, w, m, v)


def _pad_odd(w):
    return jnp.pad(w, ((0, 0), (0, ODD_PAD - w.shape[1])))


def _uq_cat(w):
    r = w.shape[0]
    return jnp.pad(w.reshape(r, MLA_HEADS, MLA_QK), ((0, 0), (0, 0), (0, HQ - MLA_QK))).reshape(r, MLA_HEADS * HQ)


def _uq_uncat(w):
    r = w.shape[0]
    return w.reshape(r, MLA_HEADS, HQ)[:, :, :MLA_QK].reshape(r, MLA_HEADS * MLA_QK)


def _to_segments(v):
    T, C = v.shape
    return v.reshape(S5_SEG, T // S5_SEG, C).transpose(1, 0, 2).reshape(T, C)


def _from_segments(v):
    T, C = v.shape
    return v.reshape(T // S5_SEG, S5_SEG, C).transpose(1, 0, 2).reshape(T, C)


def _s5_rb(T):
    return min(512, T)


def _ffn_fwd(h, g, w_in, cw, cb, w_out, tag):
    hn = _rms_fwd(h, g, name=f"ffn{tag}_norm")
    au = _mm(hn, w_in, name=f"ffn{tag}_in", tn=1408)
    z = _ffn_mid_fwd(au, cw, cb, name=f"ffn{tag}_mid")
    return _mm(z, w_out, res=h, name=f"ffn{tag}_out", tk=1408), (hn, au, z)


def _ffn_bwd(h, g, w_in, cw, cb, w_out, saved, dh, tag, dep=None):
    hn, au, z = saved
    dz = _mm(dh, w_out, tb=True, name=f"ffn{tag}_dz", tn=1408, dep=dep)
    dw_out = _mm(z, dh, ta=True, also_bf16=True, name=f"ffn{tag}_dwout", tm=1408)
    dau, dcw, dcb = _ffn_mid_bwd(au, cw, cb, dz, name=f"ffn{tag}_dmid")
    dhn = _mm(dau, w_in, tb=True, name=f"ffn{tag}_dhn", tk=1408)
    dw_in = _mm(hn, dau, ta=True, also_bf16=True, name=f"ffn{tag}_dwin", tn=1408)
    dh_in, dg = _rms_bwd(h, g, dhn, dh, name=f"ffn{tag}_dnorm")
    return dh_in, dg, dw_in, dcw, dcb, dw_out


def _local_step(x, positions, target, get_w, P, put_g):
    T = x.shape[0]
    rb = _s5_rb(T)
    row = lambda v: v.reshape(1, -1)
    g_mix, g_ffn = P["norm_mix_g"], P["norm_ffn_g"]
    lbl, hng = P["hgrn_lb_logits"], P["hgrn_norm_g"]
    dsk, bg = P["s5_d"], P["s5_b_glu"]
    qg, kvg = P["mla_q_norm_g"], P["mla_kv_norm_g"]
    cw, cb = P["ffn_conv_w"], P["ffn_conv_b"]

    col = lambda v: v.reshape(S5_N, 1)
    disc_in = (col(P["s5_a_re"]), col(P["s5_a_im"]), col(jnp.repeat(P["s5_log_dt"].reshape(S5_GROUPS), S5_STATE)),
               P["s5_b_re"].reshape(S5_N, S5_GROUP), P["s5_b_im"].reshape(S5_N, S5_GROUP))
    abr, abi, bbr, bbi = _s5_disc_fwd(*disc_in)
    ar, ai = abr.reshape(1, S5_N), abi.reshape(1, S5_N)
    bbr3, bbi3 = bbr.reshape(S5_GROUPS, S5_STATE, S5_GROUP), bbi.reshape(S5_GROUPS, S5_STATE, S5_GROUP)
    bre, bim = _blockdiag(bbr3, True).astype(bf16), _blockdiag(bbi3, True).astype(bf16)
    bret, bimt = _blockdiag(bbr3).astype(bf16), _blockdiag(bbi3).astype(bf16)
    c_re, c_im = P["s5_c_re"].reshape(S5_GROUPS, S5_GROUP, S5_STATE), P["s5_c_im"].reshape(S5_GROUPS, S5_GROUP, S5_STATE)
    cre, cim = _blockdiag(c_re, True).astype(bf16), _blockdiag(c_im, True).astype(bf16)
    cret, cimt = _blockdiag(c_re).astype(bf16), _blockdiag(c_im).astype(bf16)

    hn0 = _rms_fwd(x, g_mix[0:1], name="mix0_norm")
    We = get_w("even_in", hn0)
    proj_e = _mm(hn0, We["even_w_in"], name="even_in", tn=1280)
    Wr = get_w("even_rest", proj_e)
    ya, states = _hgrn_fwd(proj_e, lbl, hng)
    u_seg = _to_segments(proj_e[:, 4 * 512:])
    fr, fi = _s5_final(u_seg, bre, bim, ar, ai, rb=rb)
    yb_seg, s0r, s0i = _s5_fwd(u_seg, bre, bim, ar, ai, fr, fi, cre, cim, dsk, Wr["s5_w_glu"], bg, rb=rb)
    ycat = jnp.concatenate([ya, _from_segments(yb_seg)], axis=1)
    h1 = _mm(ycat, Wr["even_w_out"], res=x, name="even_out")
    Wf0 = get_w("ffn0", h1)
    h2, ffn0 = _ffn_fwd(h1, g_ffn[0:1], Wf0["ffn_w_in"], cw[0], cb[0:1], Wf0["ffn_w_out"], 0)

    tabs = _rope_tables(positions)
    hn2 = _rms_fwd(h2, g_mix[1:2], name="mix1_norm")
    Wo = get_w("odd", hn2)
    proj_o = _mm(hn2, Wo["odd_w_in"], name="odd_in")
    cqn, ckvn, kr = _mla_prep_fwd(proj_o, qg, kvg, tabs)
    q = _q_post(_mm(cqn, Wo["mla_w_uq"], name="mla_uq"), tabs, transpose=False, name="q_post")
    kvb = _mm(ckvn, Wo["mla_w_ukv"], out_dtype=bf16, name="mla_ukv")
    o, lse = _flash_fwd(q, kvb, kr)
    h3 = _mm(o, Wo["odd_w_out"], res=h2, name="odd_out")
    Wf1 = get_w("ffn1", h3)
    h4, ffn1 = _ffn_fwd(h3, g_ffn[1:2], Wf1["ffn_w_in"], cw[1], cb[1:2], Wf1["ffn_w_out"], 1)
    loss, dh4, dg_final = _loss_head(h4, row(P["final_norm_g"]), target)

    dh3, dg_ffn1, dw_fin1, dcw1, dcb1, dw_fout1 = _ffn_bwd(
        h3, g_ffn[1:2], Wf1["ffn_w_in"], cw[1], cb[1:2], Wf1["ffn_w_out"], ffn1, dh4, 1)
    sent = put_g("ffn1", {"ffn_w_in": dw_fin1, "ffn_w_out": dw_fout1})
    do = _mm(dh3, Wo["odd_w_out"], tb=True, name="odd_do", dep=sent)
    dw_oout = _mm(o, dh3, ta=True, also_bf16=True, name="odd_dwout")
    delta = _attn_delta(o, do)
    dkv, dkr_h = _flash_bwd_kv(q, kvb, kr, do, lse, delta)
    dq = _q_post(_flash_bwd_q(q, kvb, kr, do, lse, delta), tabs, transpose=True, name="dq_post")
    dw_uq = _mm(cqn, dq, ta=True, also_bf16=True, name="mla_dwuq")
    dcqn = _mm(dq, Wo["mla_w_uq"], tb=True, name="mla_dcq")
    dw_ukv = _mm(ckvn, dkv, ta=True, also_bf16=True, name="mla_dwukv")
    dckvn = _mm(dkv, Wo["mla_w_ukv"], tb=True, name="mla_dckv")
    dproj_o, dqg, dkvg = _mla_prep_bwd(proj_o, qg, kvg, tabs, dcqn, dckvn, dkr_h)
    dhn2 = _mm(dproj_o, Wo["odd_w_in"], tb=True, name="odd_dhn")
    dw_oin = _mm(hn2, dproj_o, ta=True, also_bf16=True, name="odd_dwin")
    sent = put_g("odd", {"odd_w_in": dw_oin, "mla_w_uq": dw_uq, "mla_w_ukv": dw_ukv, "odd_w_out": dw_oout})
    dh2, dg_mix1 = _rms_bwd(h2, g_mix[1:2], dhn2, dh3, name="mix1_dnorm")

    dh1, dg_ffn0, dw_fin0, dcw0, dcb0, dw_fout0 = _ffn_bwd(
        h1, g_ffn[0:1], Wf0["ffn_w_in"], cw[0], cb[0:1], Wf0["ffn_w_out"], ffn0, dh2, 0, dep=sent)
    sent = put_g("ffn0", {"ffn_w_in": dw_fin0, "ffn_w_out": dw_fout0})
    dycat = _mm(dh1, Wr["even_w_out"], tb=True, name="even_dy", dep=sent)
    dw_eout = _mm(ycat, dh1, ta=True, also_bf16=True, name="even_dwout")
    dq_h, df_h, di_h, dg_h, dlbl, dhng = _hgrn_bwd(proj_e, lbl, hng, states, dycat)
    dyb_seg = _to_segments(dycat[:, 512:])
    dy_s5, glr, gli, dcre, dcim, dd, dwg, dbg = _s5_bwd_a(
        u_seg, bre, bim, ar, ai, s0r, s0i, cre, cim, cret, cimt, dsk, Wr["s5_w_glu"], bg, dyb_seg, rb=rb)
    du_seg, dbre, dbim, dar, dai = _s5_bwd_b(
        u_seg, bre, bim, bret, bimt, ar, ai, s0r, s0i, glr, gli, cret, cimt, dsk, dy_s5, rb=rb)
    dproj_e = jnp.concatenate([dq_h, df_h, di_h, dg_h, _from_segments(du_seg)], axis=1)
    dhn0 = _mm(dproj_e, We["even_w_in"], tb=True, name="even_dhn", tk=1280)
    dw_ein = _mm(hn0, dproj_e, ta=True, also_bf16=True, name="even_dwin", tn=1280)
    dx, dg_mix0 = _rms_bwd(x, g_mix[0:1], dhn0, dh1, name="mix0_dnorm")

    unblk = lambda m, a, b: jnp.swapaxes(_blockdiag_t(m, a, b), 1, 2)
    dbbr = unblk(dbre, S5_GROUP, S5_STATE).reshape(S5_N, S5_GROUP)
    dbbi = unblk(dbim, S5_GROUP, S5_STATE).reshape(S5_N, S5_GROUP)
    d_ar, d_ai, d_ldt, d_br, d_bi = _s5_disc_bwd(*disc_in, (dar.reshape(S5_N, 1), dai.reshape(S5_N, 1), dbbr, dbbi))
    small = {
        "norm_mix_g": jnp.concatenate([dg_mix0, dg_mix1], axis=0),
        "norm_ffn_g": jnp.concatenate([dg_ffn0, dg_ffn1], axis=0),
        "final_norm_g": dg_final.reshape(-1),
        "hgrn_lb_logits": dlbl, "hgrn_norm_g": dhng,
        "s5_a_re": d_ar.reshape(1, S5_GROUPS, S5_STATE), "s5_a_im": d_ai.reshape(1, S5_GROUPS, S5_STATE),
        "s5_log_dt": d_ldt.reshape(S5_GROUPS, S5_STATE).sum(axis=1).reshape(1, S5_GROUPS),
        "s5_b_re": d_br.reshape(1, S5_GROUPS, S5_STATE, S5_GROUP), "s5_b_im": d_bi.reshape(1, S5_GROUPS, S5_STATE, S5_GROUP),
        "s5_c_re": unblk(dcre, S5_STATE, S5_GROUP).reshape(1, S5_GROUPS, S5_GROUP, S5_STATE),
        "s5_c_im": unblk(dcim, S5_STATE, S5_GROUP).reshape(1, S5_GROUPS, S5_GROUP, S5_STATE),
        "s5_d": dd, "s5_b_glu": dbg, "mla_q_norm_g": dqg, "mla_kv_norm_g": dkvg,
        "ffn_conv_w": jnp.stack([dcw0, dcw1]), "ffn_conv_b": jnp.concatenate([dcb0, dcb1], axis=0),
    }
    put_g("even", {"even_w_in": dw_ein, "s5_w_glu": (dwg, dwg.astype(bf16)), "even_w_out": dw_eout}, small)
    return loss, dx


WEIGHTS = ["norm_mix_g", "norm_ffn_g", "final_norm_g", "even_w_in", "hgrn_lb_logits", "hgrn_norm_g", "s5_a_re", "s5_a_im",
           "s5_log_dt", "s5_b_re", "s5_b_im", "s5_c_re", "s5_c_im", "s5_d", "s5_w_glu", "s5_b_glu", "even_w_out", "odd_w_in",
           "mla_q_norm_g", "mla_w_uq", "mla_kv_norm_g", "mla_w_ukv", "odd_w_out", "ffn_w_in", "ffn_conv_w", "ffn_conv_b",
           "ffn_w_out"]
SMALL_SHARDED = {"mla_q_norm_g": 1, "mla_kv_norm_g": 1, "ffn_conv_w": 2}
SMALL = [n for n in WEIGHTS if n not in BIG]
SMALL_REP = [n for n in SMALL if n not in SMALL_SHARDED]


def _pack_rows(shapes):
    n = sum(math.prod(s) for s in shapes)
    return -(-n // (SUBLANES * LANES)) * SUBLANES


def _pack(arrays, rows):
    flat = jnp.concatenate([a.reshape(-1) for a in arrays])
    return jnp.pad(flat, (0, rows * LANES - flat.shape[0])).reshape(rows, LANES)


def _unpack(block, shapes):
    flat, out, off = block.reshape(-1), [], 0
    for s in shapes:
        n = math.prod(s)
        out.append(flat[off:off + n].reshape(s))
        off += n
    return out


def kernel(x, positions, norm_mix_g, norm_ffn_g, final_norm_g, even_w_in, hgrn_lb_logits, hgrn_norm_g, s5_a_re, s5_a_im, s5_log_dt, s5_b_re, s5_b_im, s5_c_re, s5_c_im, s5_d, s5_w_glu, s5_b_glu, even_w_out, odd_w_in, mla_q_norm_g, mla_w_uq, mla_kv_norm_g, mla_w_ukv, odd_w_out, ffn_w_in, ffn_conv_w, ffn_conv_b, ffn_w_out, loss_target, m_norm_mix_g, m_norm_ffn_g, m_final_norm_g, m_even_w_in, m_hgrn_lb_logits, m_hgrn_norm_g, m_s5_a_re, m_s5_a_im, m_s5_log_dt, m_s5_b_re, m_s5_b_im, m_s5_c_re, m_s5_c_im, m_s5_d, m_s5_w_glu, m_s5_b_glu, m_even_w_out, m_odd_w_in, m_mla_q_norm_g, m_mla_w_uq, m_mla_kv_norm_g, m_mla_w_ukv, m_odd_w_out, m_ffn_w_in, m_ffn_conv_w, m_ffn_conv_b, m_ffn_w_out, v_norm_mix_g, v_norm_ffn_g, v_final_norm_g, v_even_w_in, v_hgrn_lb_logits, v_hgrn_norm_g, v_s5_a_re, v_s5_a_im, v_s5_log_dt, v_s5_b_re, v_s5_b_im, v_s5_c_re, v_s5_c_im, v_s5_d, v_s5_w_glu, v_s5_b_glu, v_even_w_out, v_odd_w_in, v_mla_q_norm_g, v_mla_w_uq, v_mla_kv_norm_g, v_mla_w_ukv, v_odd_w_out, v_ffn_w_in, v_ffn_conv_w, v_ffn_conv_b, v_ffn_w_out):
    args = dict(locals())
    w = {n: args[n] for n in WEIGHTS}
    m = {n: args["m_" + n] for n in WEIGHTS}
    v = {n: args["v_" + n] for n in WEIGHTS}
    k = 2 * lax.axis_index("x") + lax.axis_index("y")
    kidx = k.reshape(1).astype(jnp.int32)
    axis2d = lambda n: BIG[n] - (1 if n in LAYERED else 0)
    slab = lambda n: w[n].shape[1 + axis2d(n)]

    small_sh_shapes = [w[n].shape for n in SMALL_SHARDED]
    rb = _pack_rows(small_sh_shapes)
    items = {}
    for group, names in GROUPS.items():
        layer = GROUP_LAYER.get(group, 0)
        items[group] = [(_Gather(axis2d(n), slab(n)), None,
                         _place_slab(w[n][layer], axis2d(n), N_CHIPS, kidx, bf16, name=f"place_{n}_{layer}")) for n in names]
    items["even_in"].append((_Gather(0, rb), None,
                             _place_slab(_pack([w[n] for n in SMALL_SHARDED], rb), 0, N_CHIPS, kidx, f32, name="place_small")))
    gathers, tokens = {}, []
    for group in GROUPS:
        sems, srcs, lands, token = _push_start(f"gather_start_{group}", items[group])
        gathers[group] = ([it[0] for it in items[group]], sems, srcs, lands)
        tokens.append(token[0, 0])
    started = functools.reduce(jnp.add, tokens)

    def landed(group, after):
        return _push_wait(f"gather_wait_{group}", [gathers[group]], [after])[0]

    even = landed("even_in", (started + norm_mix_g[0, 0]).reshape(1))
    per_chip = [_unpack(even[-1][c * rb:(c + 1) * rb], small_sh_shapes) for c in range(N_CHIPS)]
    P = {n: w[n] for n in SMALL_REP}
    for i, (n, ax) in enumerate(SMALL_SHARDED.items()):
        P[n] = jnp.concatenate([per_chip[c][i] for c in range(N_CHIPS)], axis=ax)
    P["mla_q_norm_g"], P["mla_kv_norm_g"] = P["mla_q_norm_g"].reshape(1, -1), P["mla_kv_norm_g"].reshape(1, -1)
    fix_w = {"odd_w_in": _pad_odd, "mla_w_uq": _uq_cat}

    def get_w(group, after):
        full = even if group == "even_in" else landed(group, after)
        return {n: fix_w.get(n, lambda a: a)(a) for n, a in zip(GROUPS[group], full)}

    fix_g = {"odd_w_in": lambda g: g[:, :odd_w_in.shape[2]], "mla_w_uq": _uq_uncat}
    g32, scatters, land_now = {}, {}, {}
    ra = _pack_rows([w[n].shape for n in SMALL_REP])
    rs = ra + N_CHIPS * rb
    didx = (2 * kidx + lax.axis_index("c")).astype(jnp.int32)

    def put_g(group, grads, small=None):
        layer = GROUP_LAYER.get(group)
        routes, srcs, names = [], [], list(grads)
        for n in names:
            f = fix_g.get(n, lambda g: g)
            g32.setdefault(n, {})[layer or 0] = f(grads[n][0])
            routes.append(_Scatter(axis2d(n), slab(n), layer if n in LAYERED else None))
            srcs.append(f(grads[n][1]))
            if n not in land_now:
                land_now[n] = lax.empty((3,) + w[n].shape[0 if n in LAYERED else 1:], bf16)
        if small is not None:
            blocks = [_pack([small[n] for n in SMALL_REP], ra)]
            for chip in range(N_CHIPS):
                sl = lambda n, ax: lax.slice_in_dim(small[n].reshape(w[n].shape[:ax] + (-1,) + w[n].shape[ax + 1:]),
                                                    chip * w[n].shape[ax], (chip + 1) * w[n].shape[ax], axis=ax)
                blocks.append(_pack([sl(n, ax) for n, ax in SMALL_SHARDED.items()], rb))
            names.append("small")
            routes.append(_ToAll(rs))
            srcs.append(None)
            land_now["small"] = _place_slab(jnp.concatenate(blocks), 0, N_DEV, didx, f32, name="place_small_grads")
        sems, srcs, lands, token = _push_start(f"scatter_start_{group}", [(r, s, land_now[n]) for r, s, n in zip(routes, srcs, names)])
        land_now.update(zip(names, lands))
        scatters[group] = (routes, sems, srcs, names)
        sent.append(token)
        return token

    sent = []
    loss, dx = _local_step(x[0], positions[0], loss_target[0], get_w, P, put_g)
    sent_last = sent[-1]
    loss = lax.psum(loss[0, 0], ("x", "y", "c"))

    out = {}

    def finish(tag, groups, after):
        waits = [(scatters[g][0], scatters[g][1], scatters[g][2], [land_now[n] for n in scatters[g][3]]) for g in groups]
        for g, lands in zip(groups, _push_wait(f"scatter_wait_{tag}", waits, after)):
            land_now.update(zip(scatters[g][3], lands))
        names = [n for n in dict.fromkeys(n for g in groups for n in scatters[g][3]) if n != "small"]
        part = {}
        for n in names:
            recv = land_now[n] if n in LAYERED else land_now[n][:, None]
            part[n] = _sum4([g32[n][l] for l in sorted(g32[n])], axis2d(n), recv, kidx, name=f"sum4_{n}")
        other = _swap_with_sibling(part, tag)
        done = []
        for n in names:
            C = part[n].shape[-1]
            res = _adamw(w[n].reshape(-1, C), m[n].reshape(-1, C), v[n].reshape(-1, C), [part[n], other[n]], name=f"adamw_{n}")
            out[n] = [r.reshape(w[n].shape) for r in res]
            done.append(res[0])
        return done

    done = finish("a", ["ffn1", "odd", "ffn0"], [dx, sent_last])
    finish("b", ["even"], done)

    order = SMALL_REP + list(SMALL_SHARDED)
    packed = lambda src: jnp.concatenate([_pack([src[n] for n in SMALL_REP], ra), _pack([src[n] for n in SMALL_SHARDED], rb)])
    res = _adamw_small(land_now["small"], packed(w), packed(m), packed(v), kidx, ra, rb)
    for r in res:
        parts = _unpack(r[:ra], [w[n].shape for n in SMALL_REP]) + _unpack(r[ra:], small_sh_shapes)
        for n, a in zip(order, parts):
            out.setdefault(n, []).append(a)

    return (loss, dx[None], *[out[n][0] for n in WEIGHTS], *[out[n][1] for n in WEIGHTS],
            *[out[n][2] for n in WEIGHTS], *[out[n][3] for n in WEIGHTS])
```

```python
import functools
import math

import jax
import jax.numpy as jnp
from jax import lax
from jax.experimental import pallas as pl
from jax.experimental.pallas import tpu as pltpu

f32, bf16 = jnp.float32, jnp.bfloat16
EPS = 1e-6
LANES = 128
SUBLANES = 8
VMEM_BYTES = 48 * 1024 * 1024
HGRN_CHUNK = 64
HGRN_HEADS = 4
S5_GROUPS, S5_STATE, S5_GROUP = 32, 64, 16
S5_N = S5_GROUPS * S5_STATE
S5_SEG = SUBLANES
MLA_HEADS, MLA_NOPE, MLA_ROPE, MLA_V = 8, 128, 64, 128
MLA_QK = MLA_NOPE + MLA_ROPE
MLA_Q_RANK, MLA_KV_RANK = 384, 256
ROPE_THETA = 10000.0
D_FF = 2816
ADAM_LR, ADAM_B1, ADAM_B2, ADAM_EPS, ADAM_WD, ADAM_STEP = 0.001, 0.9, 0.999, 1e-08, 0.01, 10
MESH = pl.DeviceIdType.MESH
HI = lax.Precision.HIGHEST


def _cp(*dims):
    return pltpu.CompilerParams(dimension_semantics=dims if dims else None, vmem_limit_bytes=VMEM_BYTES)


def _tile(n, t):
    if n <= t:
        return n
    c = (t // LANES) * LANES
    while c >= LANES:
        if n % c == 0:
            return c
        c -= LANES
    return n


def _dot(a, b, dn=None, precision=None):
    if dn is None:
        dn = (((a.ndim - 1,), (0,)), ((), ()))
    return lax.dot_general(a, b, dn, preferred_element_type=f32, precision=precision)


NT = (((1,), (1,)), ((), ()))
TN = (((0,), (0,)), ((), ()))


def _bdot(a, b, dn=None):
    return _dot(a.astype(bf16), b.astype(bf16), dn)


def _mm(a, b, *, name, ta=False, tb=False, out_dtype=f32, res=None, also_bf16=False, tm=1024, tn=1024, tk=1024, dep=None):
    halves = lambda s: (s[1], 2 * s[2]) if len(s) == 3 else s
    M, K = (a.shape[1], a.shape[0]) if ta else halves(a.shape)
    N = b.shape[0] if tb else halves(b.shape)[1]
    tm, tn, tk = _tile(M, tm), _tile(N, tn), _tile(K, tk)
    if a.ndim == 3:
        tk = _tile(K // 2, tk)
    if b.ndim == 3:
        tn = _tile(N // 2, tn)
    nk = K // tk
    dn = (((0 if ta else 1,), (1 if tb else 0,)), ((), ()))

    def body(*refs):
        a_ref, b_ref = refs[0], refs[1]
        r_ref = refs[2] if res is not None else None
        nin = 2 + (res is not None) + (dep is not None)
        outs = refs[nin:-1]
        acc = refs[-1]
        k = pl.program_id(2)
        p = _bdot(a_ref[...], b_ref[...], dn)

        @pl.when(k == 0)
        def _():
            acc[...] = p

        @pl.when(k > 0)
        def _():
            acc[...] += p

        @pl.when(k == nk - 1)
        def _():
            r = acc[...]
            if r_ref is not None:
                r = r + r_ref[...]
            outs[0][...] = r.astype(out_dtype)
            if also_bf16:
                outs[1][...] = r.astype(bf16)

    a_spec = pl.BlockSpec((tk, tm), lambda i, j, k: (k, i)) if ta else pl.BlockSpec((tm, tk), lambda i, j, k: (i, k))
    b_spec = pl.BlockSpec((tn, tk), lambda i, j, k: (j, k)) if tb else pl.BlockSpec((tk, tn), lambda i, j, k: (k, j))
    if a.ndim == 3:
        kh = K // 2 // tk
        a_spec = pl.BlockSpec((None, tm, tk), lambda i, j, k: (k // kh, i, k % kh))
    if b.ndim == 3:
        nh = N // 2 // tn
        b_spec = pl.BlockSpec((None, tk, tn), lambda i, j, k: (j // nh, k, j % nh))
    o_spec = pl.BlockSpec((tm, tn), lambda i, j, k: (i, j))
    in_specs, args = [a_spec, b_spec], [a, b]
    if res is not None:
        in_specs.append(o_spec)
        args.append(res)
    if dep is not None:
        in_specs.append(pl.BlockSpec(memory_space=pl.ANY))
        args.append(dep)
    out_shape = [jax.ShapeDtypeStruct((M, N), out_dtype)]
    out_specs = [o_spec]
    if also_bf16:
        out_shape.append(jax.ShapeDtypeStruct((M, N), bf16))
        out_specs.append(o_spec)
    out = pl.pallas_call(
        body, name=name, grid=(M // tm, N // tn, nk), in_specs=in_specs, out_specs=out_specs, out_shape=out_shape,
        scratch_shapes=[pltpu.VMEM((tm, tn), f32)], compiler_params=_cp("parallel", "parallel", "arbitrary"),
    )(*args)
    return out if also_bf16 else out[0]


def _rms_fwd(x, g, *, name, col=0, width=None, tm=512):
    T = x.shape[0]
    width = x.shape[1] if width is None else width
    tm = _tile(T, tm)

    def body(x_ref, g_ref, o_ref):
        xv = x_ref[...]
        r = lax.rsqrt(jnp.mean(xv * xv, axis=-1, keepdims=True) + EPS)
        o_ref[...] = (xv * r * g_ref[...]).astype(bf16)

    return pl.pallas_call(
        body, name=name, grid=(T // tm,),
        in_specs=[pl.BlockSpec((tm, width), lambda i: (i, col)), pl.BlockSpec((1, width), lambda i: (0, 0))],
        out_specs=pl.BlockSpec((tm, width), lambda i: (i, 0)), out_shape=jax.ShapeDtypeStruct((T, width), bf16),
        compiler_params=_cp("parallel"),
    )(x, g)


def _rms_bwd_math(xv, g, dy):
    r = lax.rsqrt(jnp.mean(xv * xv, axis=-1, keepdims=True) + EPS)
    xh = xv * r
    dxh = dy * g
    dx = r * (dxh - xh * jnp.mean(dxh * xh, axis=-1, keepdims=True))
    dg = jnp.sum(dy * xh, axis=0, keepdims=True)
    return dx, dg


def _rms_bwd(x, g, dy, res=None, *, name, tm=512):
    T, D = x.shape
    tm = _tile(T, tm)

    def body(*refs):
        x_ref, g_ref, dy_ref = refs[:3]
        r_ref = refs[3] if res is not None else None
        dx_ref, dg_ref = refs[-2:]
        dx, dg = _rms_bwd_math(x_ref[...], g_ref[...], dy_ref[...].astype(f32))
        if r_ref is not None:
            dx = dx + r_ref[...]
        dx_ref[...] = dx

        @pl.when(pl.program_id(0) == 0)
        def _():
            dg_ref[...] = dg

        @pl.when(pl.program_id(0) > 0)
        def _():
            dg_ref[...] += dg

    row = pl.BlockSpec((tm, D), lambda i: (i, 0))
    vec = pl.BlockSpec((1, D), lambda i: (0, 0))
    in_specs, args = [row, vec, row], [x, g, dy]
    if res is not None:
        in_specs.append(row)
        args.append(res)
    return pl.pallas_call(
        body, name=name, grid=(T // tm,), in_specs=in_specs, out_specs=[row, vec],
        out_shape=[jax.ShapeDtypeStruct((T, D), f32), jax.ShapeDtypeStruct((1, D), f32)],
        compiler_params=_cp("arbitrary"),
    )(*args)


def _loss_head(h, g, target, *, tm=512):
    T, D = h.shape
    tm = _tile(T, tm)

    def body(h_ref, g_ref, t_ref, loss_ref, dh_ref, dg_ref):
        hv, gv = h_ref[...], g_ref[...]
        r = lax.rsqrt(jnp.mean(hv * hv, axis=-1, keepdims=True) + EPS)
        e = hv * r * gv - t_ref[...]
        part = 0.5 * jnp.sum(jnp.mean(e * e, axis=-1, keepdims=True), axis=0, keepdims=True)
        dx, dg = _rms_bwd_math(hv, gv, e * (1.0 / D))
        dh_ref[...] = dx

        @pl.when(pl.program_id(0) == 0)
        def _():
            loss_ref[...] = part
            dg_ref[...] = dg

        @pl.when(pl.program_id(0) > 0)
        def _():
            loss_ref[...] += part
            dg_ref[...] += dg

    row = pl.BlockSpec((tm, D), lambda i: (i, 0))
    vec = pl.BlockSpec((1, D), lambda i: (0, 0))
    return pl.pallas_call(
        body, name="loss_head", grid=(T // tm,), in_specs=[row, vec, row],
        out_specs=[pl.BlockSpec((1, 1), lambda i: (0, 0)), row, vec],
        out_shape=[jax.ShapeDtypeStruct((1, 1), f32), jax.ShapeDtypeStruct((T, D), f32), jax.ShapeDtypeStruct((1, D), f32)],
        compiler_params=_cp("arbitrary"),
    )(h, g, target)


def _shift_down(v, k):
    rows = lax.broadcasted_iota(jnp.int32, v.shape, 0)
    return jnp.where(rows < k, 0.0, pltpu.roll(v, k, 0))


def _shift_up(v, k):
    n = v.shape[0]
    rows = lax.broadcasted_iota(jnp.int32, v.shape, 0)
    return jnp.where(rows >= n - k, 0.0, pltpu.roll(v, n - k, 0))


def _ffn_mid_fwd(au, cw, cb, *, name):
    T = au.shape[0]
    F = au.shape[1] // 2
    nb = F // LANES

    def body(a_ref, u_ref, w_ref, b_ref, z_ref):
        a = a_ref[...]
        w = w_ref[...]
        ac = w[0:1] * _shift_down(a, 2) + w[1:2] * _shift_down(a, 1) + w[2:3] * a + b_ref[...]
        z_ref[...] = (ac * jax.nn.sigmoid(ac) * u_ref[...]).astype(bf16)

    return pl.pallas_call(
        body, name=name, grid=(nb,),
        in_specs=[pl.BlockSpec((T, LANES), lambda j: (0, j)), pl.BlockSpec((T, LANES), lambda j: (0, nb + j)),
                  pl.BlockSpec((3, LANES), lambda j: (0, j)), pl.BlockSpec((1, LANES), lambda j: (0, j))],
        out_specs=pl.BlockSpec((T, LANES), lambda j: (0, j)), out_shape=jax.ShapeDtypeStruct((T, F), bf16),
        compiler_params=_cp("parallel"),
    )(au, au, cw, cb)


def _ffn_mid_bwd(au, cw, cb, dz, *, name):
    T = au.shape[0]
    F = au.shape[1] // 2
    nb = F // LANES

    def body(a_ref, u_ref, w_ref, b_ref, dz_ref, dau_ref, dw_ref, db_ref):
        a = a_ref[...]
        w = w_ref[...]
        a2, a1 = _shift_down(a, 2), _shift_down(a, 1)
        ac = w[0:1] * a2 + w[1:2] * a1 + w[2:3] * a + b_ref[...]
        sg = jax.nn.sigmoid(ac)
        dz = dz_ref[...].astype(f32)
        dau_ref[1] = (dz * ac * sg).astype(bf16)
        dac = dz * u_ref[...] * sg * (1.0 + ac * (1.0 - sg))
        dau_ref[0] = (w[2:3] * dac + w[1:2] * _shift_up(dac, 1) + w[0:1] * _shift_up(dac, 2)).astype(bf16)
        rows = lax.broadcasted_iota(jnp.int32, (3, LANES), 0)
        s0 = jnp.sum(dac * a2, axis=0, keepdims=True)
        s1 = jnp.sum(dac * a1, axis=0, keepdims=True)
        s2 = jnp.sum(dac * a, axis=0, keepdims=True)
        dw_ref[...] = jnp.where(rows == 0, s0, jnp.where(rows == 1, s1, s2))
        db_ref[...] = jnp.sum(dac, axis=0, keepdims=True)

    col = lambda off: pl.BlockSpec((T, LANES), lambda j: (0, off + j))
    return pl.pallas_call(
        body, name=name, grid=(nb,),
        in_specs=[col(0), col(nb), pl.BlockSpec((3, LANES), lambda j: (0, j)), pl.BlockSpec((1, LANES), lambda j: (0, j)), col(0)],
        out_specs=[pl.BlockSpec((2, T, LANES), lambda j: (0, 0, j)), pl.BlockSpec((3, LANES), lambda j: (0, j)),
                   pl.BlockSpec((1, LANES), lambda j: (0, j))],
        out_shape=[jax.ShapeDtypeStruct((2, T, F), bf16), jax.ShapeDtypeStruct((3, F), f32), jax.ShapeDtypeStruct((1, F), f32)],
        compiler_params=_cp("parallel"),
    )(au, au, cw, cb, dz)


def _hgrn_lb(l):
    m = jnp.max(l, axis=0, keepdims=True)
    e = jnp.exp(l - m)
    return e[0:1] / jnp.sum(e, axis=0, keepdims=True)


def _hgrn_chunk(q, fx, lb):
    C = q.shape[0]
    sg = jax.nn.sigmoid(fx)
    F = lb + (1.0 - lb) * sg
    k = 1.0 - F
    logF = jnp.log(F)
    r = lax.broadcasted_iota(jnp.int32, (C, C), 0)
    c = lax.broadcasted_iota(jnp.int32, (C, C), 1)
    tril = (r >= c)
    b = _dot(tril.astype(f32), logF, precision=HI)
    bl = jnp.sum(logF, axis=0, keepdims=True)
    eb = jnp.exp(b)
    enb = jnp.exp(-b)
    elb = jnp.exp(bl - b)
    return dict(sg=sg, F=F, k=k, b=b, bl=bl, eb=eb, enb=enb, elb=elb, qd=q * eb, kd=k * enb, kl=k * elb, tril=tril)


def _hgrn_fwd(proj, lbl, ng, *, rb=512):
    T = proj.shape[0]
    rb = min(rb, T)
    cpb = rb // HGRN_CHUNK
    nblk = T // rb
    H = HGRN_HEADS

    def body(q_ref, f_ref, i_ref, g_ref, lbl_ref, ng_ref, y_ref, st_ref, S):
        @pl.when(pl.program_id(1) == 0)
        def _():
            S[...] = jnp.zeros_like(S)

        lb = _hgrn_lb(lbl_ref[...])
        ngv = ng_ref[...]
        for c in range(cpb):
            sl = pl.ds(c * HGRN_CHUNK, HGRN_CHUNK)
            v, gx = i_ref[sl, :], g_ref[sl, :]
            ch = _hgrn_chunk(q_ref[sl, :], f_ref[sl, :], lb)
            att = jnp.where(ch["tril"], _bdot(ch["qd"], ch["kd"], NT), 0.0)
            St = S[...]
            st_ref[0, c] = St
            o = _bdot(att, v) + _bdot(ch["qd"], St, NT)
            S[...] = St * jnp.exp(ch["bl"]) + _bdot(v, ch["kl"], TN)
            r = lax.rsqrt(jnp.mean(o * o, axis=-1, keepdims=True) + EPS)
            y_ref[sl, :] = (o * r * ngv * (gx * jax.nn.sigmoid(gx))).astype(bf16)

    col = lambda off: pl.BlockSpec((rb, LANES), lambda h, n: (n, off + h))
    return pl.pallas_call(
        body, name="hgrn_fwd", grid=(H, nblk),
        in_specs=[col(0), col(H), col(2 * H), col(3 * H), pl.BlockSpec((2, LANES), lambda h, n: (0, h)),
                  pl.BlockSpec((1, LANES), lambda h, n: (0, h))],
        out_specs=[pl.BlockSpec((rb, LANES), lambda h, n: (n, h)),
                   pl.BlockSpec((1, cpb, LANES, LANES), lambda h, n: (h, n, 0, 0))],
        out_shape=[jax.ShapeDtypeStruct((T, H * LANES), bf16),
                   jax.ShapeDtypeStruct((H, T // HGRN_CHUNK, LANES, LANES), f32)],
        scratch_shapes=[pltpu.VMEM((LANES, LANES), f32)], compiler_params=_cp("parallel", "arbitrary"),
    )(proj, proj, proj, proj, lbl, ng)


def _hgrn_bwd(proj, lbl, ng, states, dy, *, rb=512):
    T = proj.shape[0]
    rb = min(rb, T)
    cpb = rb // HGRN_CHUNK
    nblk = T // rb
    H = HGRN_HEADS
    C = HGRN_CHUNK

    def body(q_ref, f_ref, i_ref, g_ref, lbl_ref, ng_ref, st_ref, dy_ref,
             dq_ref, df_ref, di_ref, dg_ref, dl_ref, dng_ref, dS, dlb_acc, dng_acc):
        n = pl.program_id(1)

        @pl.when(n == 0)
        def _():
            dS[...] = jnp.zeros_like(dS)
            dlb_acc[...] = jnp.zeros_like(dlb_acc)
            dng_acc[...] = jnp.zeros_like(dng_acc)

        lb = _hgrn_lb(lbl_ref[...])
        ngv = ng_ref[...]
        r_i = lax.broadcasted_iota(jnp.int32, (C, C), 0)
        c_i = lax.broadcasted_iota(jnp.int32, (C, C), 1)
        triu = (c_i >= r_i).astype(f32)
        for c in reversed(range(cpb)):
            sl = pl.ds(c * C, C)
            q, v, gx = q_ref[sl, :], i_ref[sl, :], g_ref[sl, :]
            ch = _hgrn_chunk(q, f_ref[sl, :], lb)
            qd, kd, kl = ch["qd"], ch["kd"], ch["kl"]
            att = jnp.where(ch["tril"], _bdot(qd, kd, NT), 0.0)
            St = st_ref[0, c]
            o = _bdot(att, v) + _bdot(qd, St, NT)
            r = lax.rsqrt(jnp.mean(o * o, axis=-1, keepdims=True) + EPS)
            on = o * r
            sgg = jax.nn.sigmoid(gx)
            gate = gx * sgg
            dyv = dy_ref[sl, :].astype(f32)
            dg_ref[sl, :] = (dyv * on * ngv * sgg * (1.0 + gx * (1.0 - sgg))).astype(bf16)
            dng_acc[...] += jnp.sum(dyv * on * gate, axis=0, keepdims=True)
            don = dyv * ngv * gate
            do = r * (don - on * jnp.mean(don * on, axis=-1, keepdims=True))
            dSt = dS[...]
            dA = jnp.where(ch["tril"], _bdot(do, v, NT), 0.0)
            dv = _bdot(att, do, TN) + _bdot(kl, dSt, NT)
            dqd = _bdot(dA, kd) + _bdot(do, St)
            dkd = _bdot(dA, qd, TN)
            dkl = _bdot(v, dSt)
            dec = jnp.exp(ch["bl"])
            ddec = jnp.sum(St * dSt, axis=0, keepdims=True)
            dS[...] = _bdot(do, qd, TN) + dSt * dec
            dB = dqd * qd - dkd * kd - dkl * kl
            dbl = jnp.sum(dkl * kl, axis=0, keepdims=True) + ddec * dec
            dk = dkd * ch["enb"] + dkl * ch["elb"]
            dlogF = _dot(triu, dB, precision=HI) + dbl
            dF = dlogF / ch["F"] - dk
            sg = ch["sg"]
            dq_ref[sl, :] = (dqd * ch["eb"]).astype(bf16)
            di_ref[sl, :] = dv.astype(bf16)
            df_ref[sl, :] = (dF * (1.0 - lb) * sg * (1.0 - sg)).astype(bf16)
            dlb_acc[...] += jnp.sum(dF * (1.0 - sg), axis=0, keepdims=True)

        @pl.when(n == nblk - 1)
        def _():
            dl0 = dlb_acc[...] * lb * (1.0 - lb)
            rows = lax.broadcasted_iota(jnp.int32, (2, LANES), 0)
            dl_ref[...] = jnp.where(rows == 0, dl0, -dl0)
            dng_ref[...] = dng_acc[...]

    col = lambda off: pl.BlockSpec((rb, LANES), lambda h, n: (nblk - 1 - n, off + h))
    vec = lambda rows: pl.BlockSpec((rows, LANES), lambda h, n: (0, h))
    outc = pl.BlockSpec((rb, LANES), lambda h, n: (nblk - 1 - n, h))
    tok = jax.ShapeDtypeStruct((T, H * LANES), bf16)
    return pl.pallas_call(
        body, name="hgrn_bwd", grid=(H, nblk),
        in_specs=[col(0), col(H), col(2 * H), col(3 * H), vec(2), vec(1),
                  pl.BlockSpec((1, cpb, LANES, LANES), lambda h, n: (h, nblk - 1 - n, 0, 0)), col(0)],
        out_specs=[outc, outc, outc, outc, vec(2), vec(1)],
        out_shape=[tok, tok, tok, tok, jax.ShapeDtypeStruct((2, H * LANES), f32), jax.ShapeDtypeStruct((1, H * LANES), f32)],
        scratch_shapes=[pltpu.VMEM((LANES, LANES), f32), pltpu.VMEM((1, LANES), f32), pltpu.VMEM((1, LANES), f32)],
        compiler_params=_cp("parallel", "arbitrary"),
    )(proj, proj, proj, proj, lbl, ng, states, dy)


def _s5_disc_math(ar, ai, ldt, br, bi):
    dt = jnp.exp(ldt)
    mag = jnp.exp(ar * dt)
    abr, abi = mag * jnp.cos(ai * dt), mag * jnp.sin(ai * dt)
    den = ar * ar + ai * ai
    xr, xi = abr - 1.0, abi
    cr = (xr * ar + xi * ai) / den
    ci = (xi * ar - xr * ai) / den
    return abr, abi, cr * br - ci * bi, cr * bi + ci * br


def _s5_disc_fwd(ar, ai, ldt, br, bi):
    def body(ar_ref, ai_ref, ldt_ref, br_ref, bi_ref, o0, o1, o2, o3):
        outs = _s5_disc_math(ar_ref[...], ai_ref[...], ldt_ref[...], br_ref[...], bi_ref[...])
        for o, v in zip((o0, o1, o2, o3), outs):
            o[...] = v

    return pl.pallas_call(
        body, name="s5_disc_fwd",
        out_shape=[jax.ShapeDtypeStruct(ar.shape, f32)] * 2 + [jax.ShapeDtypeStruct(br.shape, f32)] * 2,
    )(ar, ai, ldt, br, bi)


def _s5_disc_bwd(ar, ai, ldt, br, bi, cts):
    def body(ar_ref, ai_ref, ldt_ref, br_ref, bi_ref, c0, c1, c2, c3, o0, o1, o2, o3, o4):
        _, vjp = jax.vjp(_s5_disc_math, ar_ref[...], ai_ref[...], ldt_ref[...], br_ref[...], bi_ref[...])
        for o, v in zip((o0, o1, o2, o3, o4), vjp((c0[...], c1[...], c2[...], c3[...]))):
            o[...] = v

    return pl.pallas_call(
        body, name="s5_disc_bwd",
        out_shape=[jax.ShapeDtypeStruct(ar.shape, f32)] * 3 + [jax.ShapeDtypeStruct(br.shape, f32)] * 2,
    )(ar, ai, ldt, br, bi, *cts)


S5_LC = 512
S5_NLC = S5_N // S5_LC
S5_UB = 4


def _cmul(ar, ai, xr, xi):
    return ar * xr - ai * xi, ar * xi + ai * xr


def _cpow(ar, ai, n):
    rr, ri = None, None
    br, bi = ar, ai
    while n:
        if n & 1:
            rr, ri = (br, bi) if rr is None else _cmul(rr, ri, br, bi)
        n >>= 1
        if n:
            br, bi = _cmul(br, bi, br, bi)
    return rr, ri


def _s5_bu(u_ref, bre_ref, bim_ref, xr, xi):
    for k in range(S5_UB):
        uk = u_ref[:, k * LANES:(k + 1) * LANES].astype(bf16)
        xr[:, k * S5_LC:(k + 1) * S5_LC] = _dot(uk, bre_ref[k])
        xi[:, k * S5_LC:(k + 1) * S5_LC] = _dot(uk, bim_ref[k])


def _s5_scan(xr, xi, sr, si, ar_ref, ai_ref, nsteps, store):
    for c in range(S5_NLC):
        cs = slice(c * S5_LC, (c + 1) * S5_LC)
        a_r = jnp.broadcast_to(ar_ref[:, cs], (S5_SEG, S5_LC))
        a_i = jnp.broadcast_to(ai_ref[:, cs], (S5_SEG, S5_LC))

        def step(j, carry, cs=cs, a_r=a_r, a_i=a_i):
            pr, pi = carry
            rows = pl.ds(pl.multiple_of(j * S5_SEG, S5_SEG), S5_SEG)
            nr = a_r * pr - a_i * pi + xr[rows, cs]
            ni = a_r * pi + a_i * pr + xi[rows, cs]
            if store:
                xr[rows, cs] = nr
                xi[rows, cs] = ni
            return nr, ni

        fr, fi = lax.fori_loop(0, nsteps, step, (sr[:, cs], si[:, cs]))
        sr[:, cs] = fr
        si[:, cs] = fi


def _s5_rscan(dr, di, xr, xi, s0r, s0i, gr, gi, acc_r, acc_i, ar_ref, ai_ref, nsteps):
    for c in range(S5_NLC):
        cs = slice(c * S5_LC, (c + 1) * S5_LC)
        a_r = jnp.broadcast_to(ar_ref[:, cs], (S5_SEG, S5_LC))
        a_i = jnp.broadcast_to(ai_ref[:, cs], (S5_SEG, S5_LC))

        def step(jj, carry, cs=cs, a_r=a_r, a_i=a_i):
            pr, pi, cr, ci = carry
            j = nsteps - 1 - jj
            rows = pl.ds(pl.multiple_of(j * S5_SEG, S5_SEG), S5_SEG)
            nr = dr[rows, cs] + a_r * pr + a_i * pi
            ni = di[rows, cs] + a_r * pi - a_i * pr
            dr[rows, cs] = nr
            di[rows, cs] = ni
            if acc_r is not None:
                prev = pl.ds(pl.multiple_of(jnp.maximum(j - 1, 0) * S5_SEG, S5_SEG), S5_SEG)
                first = j == 0
                pr_s = jnp.where(first, s0r[:, cs], xr[prev, cs])
                pi_s = jnp.where(first, s0i[:, cs], xi[prev, cs])
                cr = cr + nr * pr_s + ni * pi_s
                ci = ci - nr * pi_s + ni * pr_s
            return nr, ni, cr, ci

        z = jnp.zeros((S5_SEG, S5_LC), f32)
        init = (gr[:, cs], gi[:, cs], z, z)
        fr, fi, cr, ci = lax.fori_loop(0, nsteps, step, init)
        gr[:, cs] = fr
        gi[:, cs] = fi
        if acc_r is not None:
            acc_r[:, cs] += cr
            acc_i[:, cs] += ci


def _s5_seg_carry(fr, fi, ar, ai, seg_len, reverse):
    pr, pi = _cpow(ar, ai if not reverse else -ai, seg_len)
    rows = lax.broadcasted_iota(jnp.int32, fr.shape, 0)
    cr, ci = jnp.zeros_like(fr), jnp.zeros_like(fi)
    sh = (S5_SEG - 1) if reverse else 1
    fr_s, fi_s = pltpu.roll(fr, sh, 0), pltpu.roll(fi, sh, 0)
    order = range(S5_SEG - 2, -1, -1) if reverse else range(1, S5_SEG)
    for r in order:
        c_r, c_i = pltpu.roll(cr, sh, 0), pltpu.roll(ci, sh, 0)
        m_r, m_i = _cmul(pr, pi, c_r, c_i)
        cr = jnp.where(rows == r, m_r + fr_s, cr)
        ci = jnp.where(rows == r, m_i + fi_s, ci)
    return cr, ci


def _gelu_parts(y):
    c0 = math.sqrt(2.0 / math.pi)
    t = jnp.tanh(c0 * (y + 0.044715 * y * y * y))
    z = 0.5 * y * (1.0 + t)
    dz = 0.5 * (1.0 + t) + 0.5 * y * (1.0 - t * t) * c0 * (1.0 + 3.0 * 0.044715 * y * y)
    return z, dz


def _s5_y(xr, xi, u_ref, cre_ref, cim_ref, d_ref):
    ys = []
    for k in range(S5_UB):
        cs = slice(k * S5_LC, (k + 1) * S5_LC)
        ys.append(_bdot(xr[:, cs], cre_ref[k]) - _bdot(xi[:, cs], cim_ref[k]))
    return jnp.concatenate(ys, axis=1) + d_ref[...] * u_ref[...]


def _s5_specs(T, rb, rev=False):
    nblk = T // rb
    blk = (lambda i: (nblk - 1 - i, 0)) if rev else (lambda i: (i, 0))
    tok = pl.BlockSpec((rb, 4 * LANES), blk)
    bmat = pl.BlockSpec((S5_UB, LANES, S5_LC), lambda i: (0, 0, 0))
    cmat = pl.BlockSpec((S5_UB, S5_LC, LANES), lambda i: (0, 0, 0))
    avec = pl.BlockSpec((1, S5_N), lambda i: (0, 0))
    seg = pl.BlockSpec((S5_SEG, S5_N), lambda i: (0, 0))
    cvec = pl.BlockSpec((1, 4 * LANES), lambda i: (0, 0))
    s0 = pl.BlockSpec((1, S5_SEG, S5_N), (lambda i: (nblk - 1 - i, 0, 0)) if rev else (lambda i: (i, 0, 0)))
    return dict(tok=tok, bmat=bmat, cmat=cmat, avec=avec, seg=seg, cvec=cvec, s0=s0, nblk=nblk)


def _s5_final(u, bre, bim, ar, ai, *, rb):
    T = u.shape[0]
    sp = _s5_specs(T, rb)

    def body(u_ref, bre_ref, bim_ref, ar_ref, ai_ref, fr_ref, fi_ref, xr, xi):
        @pl.when(pl.program_id(0) == 0)
        def _():
            fr_ref[...] = jnp.zeros_like(fr_ref)
            fi_ref[...] = jnp.zeros_like(fi_ref)

        _s5_bu(u_ref, bre_ref, bim_ref, xr, xi)
        _s5_scan(xr, xi, fr_ref, fi_ref, ar_ref, ai_ref, rb // S5_SEG, False)

    return pl.pallas_call(
        body, name="s5_final", grid=(sp["nblk"],),
        in_specs=[sp["tok"], sp["bmat"], sp["bmat"], sp["avec"], sp["avec"]], out_specs=[sp["seg"], sp["seg"]],
        out_shape=[jax.ShapeDtypeStruct((S5_SEG, S5_N), f32)] * 2,
        scratch_shapes=[pltpu.VMEM((rb, S5_N), f32)] * 2, compiler_params=_cp("arbitrary"),
    )(u, bre, bim, ar, ai)


def _s5_fwd(u, bre, bim, ar, ai, fr, fi, cre, cim, dsk, wg, bg, *, rb):
    T = u.shape[0]
    sp = _s5_specs(T, rb)
    seg_len = T // S5_SEG

    def body(u_ref, bre_ref, bim_ref, ar_ref, ai_ref, fr_ref, fi_ref, cre_ref, cim_ref, d_ref, wg_ref, bg_ref,
             o_ref, s0r_ref, s0i_ref, xr, xi, sr, si):
        @pl.when(pl.program_id(0) == 0)
        def _():
            i_r, i_i = _s5_seg_carry(fr_ref[...], fi_ref[...], ar_ref[...], ai_ref[...], seg_len, False)
            sr[...] = i_r
            si[...] = i_i

        s0r_ref[0] = sr[...]
        s0i_ref[0] = si[...]
        _s5_bu(u_ref, bre_ref, bim_ref, xr, xi)
        _s5_scan(xr, xi, sr, si, ar_ref, ai_ref, rb // S5_SEG, True)
        y = _s5_y(xr, xi, u_ref, cre_ref, cim_ref, d_ref)
        z, _ = _gelu_parts(y)
        v = _bdot(z, wg_ref[...]) + bg_ref[...]
        o_ref[...] = (z * jax.nn.sigmoid(v)).astype(bf16)

    wspec = pl.BlockSpec((4 * LANES, 4 * LANES), lambda i: (0, 0))
    return pl.pallas_call(
        body, name="s5_fwd", grid=(sp["nblk"],),
        in_specs=[sp["tok"], sp["bmat"], sp["bmat"], sp["avec"], sp["avec"], sp["seg"], sp["seg"], sp["cmat"], sp["cmat"],
                  sp["cvec"], wspec, sp["cvec"]],
        out_specs=[sp["tok"], sp["s0"], sp["s0"]],
        out_shape=[jax.ShapeDtypeStruct((T, 4 * LANES), bf16)] + [jax.ShapeDtypeStruct((sp["nblk"], S5_SEG, S5_N), f32)] * 2,
        scratch_shapes=[pltpu.VMEM((rb, S5_N), f32)] * 2 + [pltpu.VMEM((S5_SEG, S5_N), f32)] * 2,
        compiler_params=_cp("arbitrary"),
    )(u, bre, bim, ar, ai, fr, fi, cre, cim, dsk, wg, bg)


def _s5_bwd_a(u, bre, bim, ar, ai, s0r, s0i, cre, cim, cret, cimt, dsk, wg, bg, dout, *, rb):
    T = u.shape[0]
    sp = _s5_specs(T, rb, rev=True)

    def body(u_ref, bre_ref, bim_ref, ar_ref, ai_ref, s0r_ref, s0i_ref, cre_ref, cim_ref, cret_ref, cimt_ref,
             d_ref, wg_ref, bg_ref, do_ref, dy_ref, glr_ref, gli_ref, dcre_ref, dcim_ref, dd_ref, dwg_ref, dbg_ref,
             xr, xi, dr, di, sr, si):
        @pl.when(pl.program_id(0) == 0)
        def _():
            for r in (glr_ref, gli_ref, dcre_ref, dcim_ref, dd_ref, dwg_ref, dbg_ref):
                r[...] = jnp.zeros_like(r)

        sr[...] = s0r_ref[0]
        si[...] = s0i_ref[0]
        _s5_bu(u_ref, bre_ref, bim_ref, xr, xi)
        _s5_scan(xr, xi, sr, si, ar_ref, ai_ref, rb // S5_SEG, True)
        uv = u_ref[...]
        y = _s5_y(xr, xi, u_ref, cre_ref, cim_ref, d_ref)
        z, gz = _gelu_parts(y)
        v = _bdot(z, wg_ref[...]) + bg_ref[...]
        sg = jax.nn.sigmoid(v)
        dov = do_ref[...].astype(f32)
        dv = dov * z * sg * (1.0 - sg)
        dz = dov * sg + _bdot(dv, wg_ref[...], NT)
        dy = dz * gz
        dy_ref[...] = dy
        dwg_ref[...] += _bdot(z, dv, TN)
        dbg_ref[...] += jnp.sum(dv, axis=0, keepdims=True)
        dd_ref[...] += jnp.sum(dy * uv, axis=0, keepdims=True)
        for k in range(S5_UB):
            cs = slice(k * S5_LC, (k + 1) * S5_LC)
            dyk = dy[:, k * LANES:(k + 1) * LANES]
            dcre_ref[k] += _bdot(xr[:, cs], dyk, TN)
            dcim_ref[k] -= _bdot(xi[:, cs], dyk, TN)
            dr[:, cs] = _bdot(dyk, cret_ref[k])
            di[:, cs] = -_bdot(dyk, cimt_ref[k])
        _s5_rscan(dr, di, None, None, None, None, glr_ref, gli_ref, None, None, ar_ref, ai_ref, rb // S5_SEG)

    wspec = pl.BlockSpec((4 * LANES, 4 * LANES), lambda i: (0, 0))
    return pl.pallas_call(
        body, name="s5_bwd_a", grid=(sp["nblk"],),
        in_specs=[sp["tok"], sp["bmat"], sp["bmat"], sp["avec"], sp["avec"], sp["s0"], sp["s0"], sp["cmat"], sp["cmat"],
                  sp["bmat"], sp["bmat"], sp["cvec"], wspec, sp["cvec"], sp["tok"]],
        out_specs=[sp["tok"], sp["seg"], sp["seg"], sp["cmat"], sp["cmat"], sp["cvec"], wspec, sp["cvec"]],
        out_shape=[jax.ShapeDtypeStruct((T, 4 * LANES), f32)] + [jax.ShapeDtypeStruct((S5_SEG, S5_N), f32)] * 2
        + [jax.ShapeDtypeStruct((S5_UB, S5_LC, LANES), f32)] * 2
        + [jax.ShapeDtypeStruct((1, 4 * LANES), f32), jax.ShapeDtypeStruct((4 * LANES, 4 * LANES), f32),
           jax.ShapeDtypeStruct((1, 4 * LANES), f32)],
        scratch_shapes=[pltpu.VMEM((rb, S5_N), f32)] * 4 + [pltpu.VMEM((S5_SEG, S5_N), f32)] * 2,
        compiler_params=_cp("arbitrary"),
    )(u, bre, bim, ar, ai, s0r, s0i, cre, cim, cret, cimt, dsk, wg, bg, dout)


def _s5_bwd_b(u, bre, bim, bret, bimt, ar, ai, s0r, s0i, glr, gli, cret, cimt, dsk, dy, *, rb):
    T = u.shape[0]
    sp = _s5_specs(T, rb, rev=True)
    seg_len = T // S5_SEG
    nblk = sp["nblk"]

    def body(u_ref, bre_ref, bim_ref, bret_ref, bimt_ref, ar_ref, ai_ref, s0r_ref, s0i_ref, glr_ref, gli_ref,
             cret_ref, cimt_ref, d_ref, dy_ref, du_ref, dbre_ref, dbim_ref, dar_ref, dai_ref,
             xr, xi, dr, di, sr, si, gr, gi, acc_r, acc_i):
        @pl.when(pl.program_id(0) == 0)
        def _():
            x_r, x_i = _s5_seg_carry(glr_ref[...], gli_ref[...], ar_ref[...], ai_ref[...], seg_len, True)
            gr[...] = x_r
            gi[...] = x_i
            acc_r[...] = jnp.zeros_like(acc_r)
            acc_i[...] = jnp.zeros_like(acc_i)
            dbre_ref[...] = jnp.zeros_like(dbre_ref)
            dbim_ref[...] = jnp.zeros_like(dbim_ref)

        sr[...] = s0r_ref[0]
        si[...] = s0i_ref[0]
        _s5_bu(u_ref, bre_ref, bim_ref, xr, xi)
        _s5_scan(xr, xi, sr, si, ar_ref, ai_ref, rb // S5_SEG, True)
        dy = dy_ref[...]
        for k in range(S5_UB):
            cs = slice(k * S5_LC, (k + 1) * S5_LC)
            dyk = dy[:, k * LANES:(k + 1) * LANES]
            dr[:, cs] = _bdot(dyk, cret_ref[k])
            di[:, cs] = -_bdot(dyk, cimt_ref[k])
        sr[...] = s0r_ref[0]
        si[...] = s0i_ref[0]
        _s5_rscan(dr, di, xr, xi, sr, si, gr, gi, acc_r, acc_i, ar_ref, ai_ref, rb // S5_SEG)
        dus = []
        for k in range(S5_UB):
            cs = slice(k * S5_LC, (k + 1) * S5_LC)
            uk = u_ref[:, k * LANES:(k + 1) * LANES]
            dbre_ref[k] += _bdot(uk, dr[:, cs], TN)
            dbim_ref[k] += _bdot(uk, di[:, cs], TN)
            dus.append(_bdot(dr[:, cs], bret_ref[k]) + _bdot(di[:, cs], bimt_ref[k]))
        du_ref[...] = (jnp.concatenate(dus, axis=1) + d_ref[...] * dy).astype(bf16)

        @pl.when(pl.program_id(0) == nblk - 1)
        def _():
            dar_ref[...] = jnp.sum(acc_r[...], axis=0, keepdims=True)
            dai_ref[...] = jnp.sum(acc_i[...], axis=0, keepdims=True)

    return pl.pallas_call(
        body, name="s5_bwd_b", grid=(nblk,),
        in_specs=[sp["tok"], sp["bmat"], sp["bmat"], sp["cmat"], sp["cmat"], sp["avec"], sp["avec"], sp["s0"], sp["s0"],
                  sp["seg"], sp["seg"], sp["bmat"], sp["bmat"], sp["cvec"], sp["tok"]],
        out_specs=[sp["tok"], sp["bmat"], sp["bmat"], sp["avec"], sp["avec"]],
        out_shape=[jax.ShapeDtypeStruct((T, 4 * LANES), bf16)] + [jax.ShapeDtypeStruct((S5_UB, LANES, S5_LC), f32)] * 2
        + [jax.ShapeDtypeStruct((1, S5_N), f32)] * 2,
        scratch_shapes=[pltpu.VMEM((rb, S5_N), f32)] * 4 + [pltpu.VMEM((S5_SEG, S5_N), f32)] * 6,
        compiler_params=_cp("arbitrary"),
    )(u, bre, bim, bret, bimt, ar, ai, s0r, s0i, glr, gli, cret, cimt, dsk, dy)


def _blockdiag(w, transpose=False):
    if transpose:
        w = jnp.swapaxes(w, 1, 2)
    g, a, b = w.shape
    eye = jnp.eye(8, dtype=w.dtype)
    return jnp.einsum("kgab,gj->kgajb", w.reshape(4, 8, a, b), eye).reshape(4, 8 * a, 8 * b)


def _blockdiag_t(m, a, b):
    eye = jnp.eye(8, dtype=m.dtype)
    return jnp.einsum("kgajb,gj->kgab", m.reshape(4, 8, a, 8, b), eye).reshape(32, a, b)


ROT = MLA_ROPE // 2


def _rope_tables(positions):
    freqs = ROPE_THETA ** (-jnp.arange(0, MLA_ROPE, 2, dtype=f32) / MLA_ROPE)
    ang = positions.astype(f32)[:, None] * freqs
    cos, sin, z = jnp.cos(ang), jnp.sin(ang), jnp.zeros_like(ang)
    return (jnp.concatenate([cos, cos, z, z], axis=1), jnp.concatenate([-sin, z, z, z], axis=1),
            jnp.concatenate([z, sin, z, z], axis=1))


def _rot(x, c, sa, sb):
    return x * c + pltpu.roll(x, LANES - ROT, 1) * sa + pltpu.roll(x, ROT, 1) * sb


def _rot_t(dy, c, sa, sb):
    return dy * c + pltpu.roll(dy * sa, ROT, 1) + pltpu.roll(dy * sb, LANES - ROT, 1)


def _rms(xv, g):
    return xv * lax.rsqrt(jnp.mean(xv * xv, axis=-1, keepdims=True) + EPS) * g


QW, KVW = MLA_Q_RANK, MLA_KV_RANK
ODD_PAD = QW + KVW + LANES


def _mla_prep_fwd(proj, qg, kvg, tabs, *, tm=512):
    T = proj.shape[0]
    tm = _tile(T, tm)

    def body(p_ref, qg_ref, kvg_ref, c_ref, sa_ref, sb_ref, cq_ref, ckv_ref, kr_ref):
        cq_ref[...] = _rms(p_ref[:, :QW], qg_ref[...]).astype(bf16)
        ckv_ref[...] = _rms(p_ref[:, QW:QW + KVW], kvg_ref[...]).astype(bf16)
        kr_ref[...] = _rot(p_ref[:, QW + KVW:], c_ref[...], sa_ref[...], sb_ref[...]).astype(bf16)

    row = lambda w: pl.BlockSpec((tm, w), lambda i: (i, 0))
    vec = lambda w: pl.BlockSpec((1, w), lambda i: (0, 0))
    return pl.pallas_call(
        body, name="mla_prep_fwd", grid=(T // tm,),
        in_specs=[row(ODD_PAD), vec(QW), vec(KVW), row(LANES), row(LANES), row(LANES)],
        out_specs=[row(QW), row(KVW), row(LANES)],
        out_shape=[jax.ShapeDtypeStruct((T, QW), bf16), jax.ShapeDtypeStruct((T, KVW), bf16),
                   jax.ShapeDtypeStruct((T, LANES), bf16)],
        compiler_params=_cp("parallel"),
    )(proj, qg, kvg, *tabs)


def _mla_prep_bwd(proj, qg, kvg, tabs, dcqn, dckvn, dkr_heads, *, tm=512):
    T = proj.shape[0]
    tm = _tile(T, tm)

    def body(p_ref, qg_ref, kvg_ref, c_ref, sa_ref, sb_ref, dcq_ref, dckv_ref, dkr_ref, dp_ref, dqg_ref, dkvg_ref):
        dcq, dqg = _rms_bwd_math(p_ref[:, :QW], qg_ref[...], dcq_ref[...])
        dckv, dkvg = _rms_bwd_math(p_ref[:, QW:QW + KVW], kvg_ref[...], dckv_ref[...])
        dk = dkr_ref[:, :LANES]
        for h in range(1, MLA_HEADS):
            dk = dk + dkr_ref[:, h * LANES:(h + 1) * LANES]
        dkr = _rot_t(dk, c_ref[...], sa_ref[...], sb_ref[...])
        dp_ref[...] = jnp.concatenate([dcq, dckv, dkr], axis=1).astype(bf16)

        @pl.when(pl.program_id(0) == 0)
        def _():
            dqg_ref[...] = dqg
            dkvg_ref[...] = dkvg

        @pl.when(pl.program_id(0) > 0)
        def _():
            dqg_ref[...] += dqg
            dkvg_ref[...] += dkvg

    row = lambda w: pl.BlockSpec((tm, w), lambda i: (i, 0))
    vec = lambda w: pl.BlockSpec((1, w), lambda i: (0, 0))
    return pl.pallas_call(
        body, name="mla_prep_bwd", grid=(T // tm,),
        in_specs=[row(ODD_PAD), vec(QW), vec(KVW), row(LANES), row(LANES), row(LANES), row(QW), row(KVW),
                  row(MLA_HEADS * LANES)],
        out_specs=[row(ODD_PAD), vec(QW), vec(KVW)],
        out_shape=[jax.ShapeDtypeStruct((T, ODD_PAD), bf16), jax.ShapeDtypeStruct((1, QW), f32),
                   jax.ShapeDtypeStruct((1, KVW), f32)],
        compiler_params=_cp("arbitrary"),
    )(proj, qg, kvg, *tabs, dcqn, dckvn, dkr_heads)


HQ = 2 * LANES
QK_SCALE = MLA_QK ** -0.5


def _q_post(q, tabs, *, transpose, name, tm=512):
    T = q.shape[0]
    tm = _tile(T, tm)

    def body(q_ref, c_ref, sa_ref, sb_ref, o_ref):
        qv = q_ref[...].astype(f32)
        rope = (_rot_t if transpose else _rot)(qv[:, LANES:], c_ref[...], sa_ref[...], sb_ref[...])
        o_ref[...] = (jnp.concatenate([qv[:, :LANES], rope], axis=1) * QK_SCALE).astype(bf16)

    tab = pl.BlockSpec((tm, LANES), lambda i, h: (i, 0))
    blk = pl.BlockSpec((tm, HQ), lambda i, h: (i, h))
    return pl.pallas_call(
        body, name=name, grid=(T // tm, MLA_HEADS), in_specs=[blk, tab, tab, tab], out_specs=blk,
        out_shape=jax.ShapeDtypeStruct(q.shape, bf16), compiler_params=_cp("parallel", "parallel"),
    )(q, *tabs)


def _causal_mask(i, j, tq, tk):
    r = lax.broadcasted_iota(jnp.int32, (tq, tk), 0) + i * tq
    c = lax.broadcasted_iota(jnp.int32, (tq, tk), 1) + j * tk
    return c <= r


def _flash_fwd(q, kv, kr, *, tq=1024, tk=512):
    T = q.shape[0]
    tq = _tile(T, tq)
    tk = _tile(tq, tk)
    per = tq // tk
    H = MLA_HEADS

    def body(q_ref, kn_ref, v_ref, kr_ref, o_ref, lse_ref, m_s, l_s, acc):
        i, j = pl.program_id(1), pl.program_id(2)
        last = (i + 1) * per - 1

        @pl.when(j == 0)
        def _():
            m_s[...] = jnp.full_like(m_s, -jnp.inf)
            l_s[...] = jnp.zeros_like(l_s)
            acc[...] = jnp.zeros_like(acc)

        def step(masked):
            k = jnp.concatenate([kn_ref[...], kr_ref[...]], axis=1)
            s = _dot(q_ref[...], k, NT)
            if masked:
                s = jnp.where(_causal_mask(i, j, tq, tk), s, -jnp.inf)
            m_new = jnp.maximum(m_s[...], jnp.max(s, axis=-1, keepdims=True))
            alpha = jnp.exp(m_s[...] - m_new)
            p = jnp.exp(s - m_new)
            l_s[...] = alpha * l_s[...] + jnp.sum(p, axis=-1, keepdims=True)
            acc[...] = alpha * acc[...] + _dot(p.astype(bf16), v_ref[...])
            m_s[...] = m_new

        pl.when(j < i * per)(functools.partial(step, False))
        pl.when((j >= i * per) & (j <= last))(functools.partial(step, True))

        @pl.when(j == last)
        def _():
            o_ref[...] = (acc[...] / l_s[...]).astype(bf16)
            lse_ref[0] = m_s[...] + jnp.log(l_s[...])

    kj = lambda i, j: jnp.minimum(j, (i + 1) * per - 1)
    kblk = lambda off: pl.BlockSpec((tk, LANES), lambda h, i, j: (kj(i, j), 2 * h + off))
    return pl.pallas_call(
        body, name="flash_fwd", grid=(H, T // tq, T // tk),
        in_specs=[pl.BlockSpec((tq, HQ), lambda h, i, j: (i, h)), kblk(0), kblk(1),
                  pl.BlockSpec((tk, LANES), lambda h, i, j: (kj(i, j), 0))],
        out_specs=[pl.BlockSpec((tq, LANES), lambda h, i, j: (i, h)), pl.BlockSpec((1, tq, 1), lambda h, i, j: (h, i, 0))],
        out_shape=[jax.ShapeDtypeStruct((T, H * LANES), bf16), jax.ShapeDtypeStruct((H, T, 1), f32)],
        scratch_shapes=[pltpu.VMEM((tq, 1), f32), pltpu.VMEM((tq, 1), f32), pltpu.VMEM((tq, LANES), f32)],
        compiler_params=_cp("parallel", "parallel", "arbitrary"),
    )(q, kv, kv, kr)


def _attn_delta(o, do, *, tm=512):
    T = o.shape[0]
    tm = _tile(T, tm)

    def body(o_ref, do_ref, d_ref):
        d_ref[0] = jnp.sum(o_ref[...].astype(f32) * do_ref[...], axis=-1, keepdims=True)

    blk = pl.BlockSpec((tm, LANES), lambda h, i: (i, h))
    return pl.pallas_call(
        body, name="attn_delta", grid=(MLA_HEADS, T // tm), in_specs=[blk, blk],
        out_specs=pl.BlockSpec((1, tm, 1), lambda h, i: (h, i, 0)),
        out_shape=jax.ShapeDtypeStruct((MLA_HEADS, T, 1), f32), compiler_params=_cp("parallel", "parallel"),
    )(o, do)


def _flash_bwd(q, kv, kr, do, lse, delta, *, tb=512):
    T = q.shape[0]
    tb = _tile(T, tb)
    nb = T // tb
    H = MLA_HEADS

    def body(q_ref, kn_ref, v_ref, kr_ref, do_ref, lse_ref, dl_ref, dkv_ref, dkr_ref, dq_ref, dk_acc, dv_acc):
        j, ii = pl.program_id(1), pl.program_id(2)
        i = jnp.maximum(ii, j)

        @pl.when((j == 0) & (ii == 0))
        def _():
            dq_ref[...] = jnp.zeros_like(dq_ref)

        @pl.when(ii == 0)
        def _():
            dk_acc[...] = jnp.zeros_like(dk_acc)
            dv_acc[...] = jnp.zeros_like(dv_acc)

        def step(masked):
            k = jnp.concatenate([kn_ref[...], kr_ref[...]], axis=1)
            p = jnp.exp(_dot(q_ref[...], k, NT) - lse_ref[0])
            if masked:
                p = jnp.where(_causal_mask(i, j, tb, tb), p, 0.0)
            ds = p * (_bdot(do_ref[...], v_ref[...], NT) - dl_ref[0])
            dv_acc[...] += _bdot(p, do_ref[...], TN)
            dk_acc[...] += _bdot(ds, q_ref[...], TN)
            rows = pl.ds(pl.multiple_of(i * tb, tb), tb)
            dq_ref[rows, :] += _bdot(ds, k)

        pl.when(ii > j)(functools.partial(step, False))
        pl.when(ii == j)(functools.partial(step, True))

        @pl.when(ii == nb - 1)
        def _():
            dkv_ref[...] = jnp.concatenate([dk_acc[:, :LANES], dv_acc[...]], axis=1).astype(bf16)
            dkr_ref[...] = dk_acc[:, LANES:]

    qi = lambda h, j, i: jnp.maximum(i, j)
    kblk = lambda off: pl.BlockSpec((tb, LANES), lambda h, j, i: (j, 2 * h + off))
    vec = pl.BlockSpec((1, tb, 1), lambda h, j, i: (h, qi(h, j, i), 0))
    return pl.pallas_call(
        body, name="flash_bwd", grid=(H, nb, nb),
        in_specs=[pl.BlockSpec((tb, HQ), lambda h, j, i: (qi(h, j, i), h)), kblk(0), kblk(1),
                  pl.BlockSpec((tb, LANES), lambda h, j, i: (j, 0)),
                  pl.BlockSpec((tb, LANES), lambda h, j, i: (qi(h, j, i), h)), vec, vec],
        out_specs=[pl.BlockSpec((tb, HQ), lambda h, j, i: (j, h)), pl.BlockSpec((tb, LANES), lambda h, j, i: (j, h)),
                   pl.BlockSpec((T, HQ), lambda h, j, i: (0, h))],
        out_shape=[jax.ShapeDtypeStruct((T, H * HQ), bf16), jax.ShapeDtypeStruct((T, H * LANES), f32),
                   jax.ShapeDtypeStruct((T, H * HQ), f32)],
        scratch_shapes=[pltpu.VMEM((tb, HQ), f32), pltpu.VMEM((tb, LANES), f32)],
        compiler_params=_cp("parallel", "arbitrary", "arbitrary"),
    )(q, kv, kv, kr, do, lse, delta)


HBM_SPEC = pl.BlockSpec(memory_space=pltpu.HBM)
N_CHIPS = 4
N_DEV = 8

BIG = {"even_w_in": 1, "s5_w_glu": 0, "even_w_out": 0, "odd_w_in": 0, "mla_w_uq": 1, "mla_w_ukv": 1, "odd_w_out": 0,
       "ffn_w_in": 2, "ffn_w_out": 1}
LAYERED = ("ffn_w_in", "ffn_w_out")
GROUPS = {"even_in": ("even_w_in",), "even_rest": ("s5_w_glu", "even_w_out"), "ffn0": LAYERED,
          "odd": ("odd_w_in", "mla_w_uq", "mla_w_ukv", "odd_w_out"), "ffn1": LAYERED}
GROUP_LAYER = {"ffn0": 0, "ffn1": 1}


def _place():
    x, y, c = lax.axis_index("x"), lax.axis_index("y"), lax.axis_index("c")
    chips = [(1 - x, y), (x, 1 - y), (1 - x, 1 - y)]
    return x, y, c, chips


def _slab(ref, axis, k, size):
    start = pl.multiple_of(k * size, size if axis == 0 else LANES)
    idx = [slice(None)] * len(ref.shape)
    idx[axis] = pl.ds(start, size)
    return ref.at[tuple(idx)]


SEM_SPEC = pl.BlockSpec(memory_space=pltpu.SEMAPHORE)
ANY_SPEC = pl.BlockSpec(memory_space=pl.ANY)
EFFECT = pltpu.SideEffectType.DATAFLOW_SIDE_EFFECTING


def _hbm(a):
    return pltpu.with_memory_space_constraint(a, pltpu.HBM)


class _Gather:
    copies = 3

    def __init__(self, axis, size):
        self.axis, self.size = axis, size

    def view(self, land, kk):
        return _slab(land, self.axis, kk, self.size)

    def own(self, land, place):
        return self.view(land, 2 * place[0] + place[1])

    def sends(self, src, land, place):
        x, y, c, chips = place
        return [(self.own(land, place) if src is None else src, self.own(land, place), (*chip, c)) for chip in chips]

    def recvs(self, land, place):
        return [self.view(land, 2 * cx + cy) for cx, cy in place[3]]


class _Scatter:
    copies = 3

    def __init__(self, axis, size, layer=None):
        self.axis, self.size, self.layer = axis, size, layer

    def row(self, land, j):
        return land.at[j] if self.layer is None else land.at[j, self.layer]

    def sends(self, src, land, place):
        c, chips = place[2], place[3]
        return [(_slab(src, self.axis, 2 * cx + cy, self.size), self.row(land, j), (cx, cy, c))
                for j, (cx, cy) in enumerate(chips)]

    def recvs(self, land, place):
        return [self.row(land, j) for j in range(3)]


class _ToAll:
    copies = N_DEV - 1

    def __init__(self, size):
        self.size = size

    def sends(self, src, land, place):
        x, y, c, _ = place
        flip = lambda v, bit: 1 - v if bit else v
        own = _slab(land, 0, 4 * x + 2 * y + c, self.size)
        return [(own, own, (flip(x, m & 4), flip(y, m & 2), flip(c, m & 1))) for m in range(1, N_DEV)]

    def recvs(self, land, place):
        x, y, c, _ = place
        d = 4 * x + 2 * y + c
        return [_slab(land, 0, d ^ m, self.size) for m in range(1, N_DEV)]


def _unique(arrays):
    out, index = [], {}
    for a in arrays:
        if a is not None and id(a) not in index:
            index[id(a)] = len(out)
            out.append(a)
    return out, index


def _sem_base(routes):
    base = [0]
    for r in routes:
        base.append(base[-1] + r.copies)
    return base


def _push_start(name, items):
    n = len(items)
    base = _sem_base([it[0] for it in items])
    arrays, index = _unique([it[1] for it in items] + [it[2] for it in items])
    na = len(arrays)

    def body(*refs):
        arr, send, recv, token = refs[:na], refs[na], refs[na + 1], refs[-1]
        place = _place()
        for i, (route, src, land) in enumerate(items):
            s_ref = None if src is None else arr[index[id(src)]]
            for j, (s, d, dev) in enumerate(route.sends(s_ref, arr[index[id(land)]], place)):
                pltpu.make_async_remote_copy(src_ref=s, dst_ref=d, send_sem=send.at[base[i] + j], recv_sem=recv.at[base[i] + j],
                                             device_id=dev, device_id_type=MESH).start()
        token[...] = jnp.zeros_like(token)

    res = pl.pallas_call(
        body, name=name,
        out_shape=[pltpu.SemaphoreType.DMA((base[-1],)), pltpu.SemaphoreType.DMA((base[-1],))]
        + [pltpu.HBM(a.shape, a.dtype) for a in arrays] + [jax.ShapeDtypeStruct((SUBLANES, LANES), f32)],
        in_specs=[HBM_SPEC] * na, out_specs=[SEM_SPEC, SEM_SPEC] + [HBM_SPEC] * na + [pl.BlockSpec(memory_space=pltpu.VMEM)],
        input_output_aliases={i: 2 + i for i in range(na)},
        compiler_params=pltpu.CompilerParams(has_side_effects=EFFECT),
    )(*[_hbm(a) for a in arrays])
    thru = lambda a: None if a is None else res[2 + index[id(a)]]
    return (res[0], res[1]), [thru(it[1]) for it in items], [thru(it[2]) for it in items], res[-1]


def _push_wait(name, groups, after):
    arrays, index = _unique([a for _, _, srcs, lands in groups for a in list(srcs) + list(lands)])
    na, ng = len(arrays), len(groups)

    def body(*refs):
        arr, sems = refs[:na], refs[na:na + 2 * ng]
        place = _place()
        for g, (routes, _, srcs, lands) in enumerate(groups):
            send, recv = sems[2 * g], sems[2 * g + 1]
            base = _sem_base(routes)
            for i, route in enumerate(routes):
                src, land = None if srcs[i] is None else arr[index[id(srcs[i])]], arr[index[id(lands[i])]]
                for j, ((s, d, dev), mine) in enumerate(zip(route.sends(src, land, place), route.recvs(land, place))):
                    cp = pltpu.make_async_remote_copy(src_ref=s, dst_ref=mine, send_sem=send.at[base[i] + j],
                                                      recv_sem=recv.at[base[i] + j], device_id=dev,
                                                      device_id_type=MESH)
                    cp.wait_send()
                    cp.wait_recv()

    sem_args = [s for g in groups for s in g[1]]
    res = pl.pallas_call(
        body, name=name, out_shape=[pltpu.HBM(a.shape, a.dtype) for a in arrays],
        in_specs=[HBM_SPEC] * na + [SEM_SPEC] * (2 * ng) + [ANY_SPEC] * len(after), out_specs=[HBM_SPEC] * na,
        input_output_aliases={i: i for i in range(na)},
        compiler_params=pltpu.CompilerParams(has_side_effects=EFFECT),
    )(*arrays, *sem_args, *after)
    return [[res[index[id(a)]] for a in g[3]] for g in groups]


def _place_slab(block, axis, slabs, idx, dtype, *, name):
    R, C = block.shape
    tm = _rows(R, C)
    nr = R // tm
    out_map = (lambda i, k: (i, k[0])) if axis == 1 else (lambda i, k: (k[0] * nr + i, 0))

    def body(k_ref, x_ref, o_ref):
        o_ref[...] = x_ref[...].astype(dtype)

    full = (R, C * slabs) if axis == 1 else (R * slabs, C)
    return pl.pallas_call(
        body, name=name, out_shape=jax.ShapeDtypeStruct(full, dtype),
        grid_spec=pltpu.PrefetchScalarGridSpec(
            num_scalar_prefetch=1, grid=(nr,), in_specs=[pl.BlockSpec((tm, C), lambda i, k: (i, 0))],
            out_specs=pl.BlockSpec((tm, C), out_map)),
        compiler_params=_cp("parallel"),
    )(idx, block)


def _swap_with_sibling(parts, tag):
    names = list(parts)

    def body(*refs):
        n = len(names)
        ins, outs, send, recv = refs[:n], refs[n:2 * n], refs[-2], refs[-1]
        x, y, c, _ = _place()
        cps = [pltpu.make_async_remote_copy(src_ref=ins[a], dst_ref=outs[a], send_sem=send.at[a], recv_sem=recv.at[a],
                                            device_id=(x, y, 1 - c), device_id_type=MESH) for a in range(n)]
        for cp in cps:
            cp.start()
        for cp in cps:
            cp.wait_recv()
        for cp in cps:
            cp.wait_send()

    res = pl.pallas_call(
        body, name=f"swap_with_sibling_{tag}", in_specs=[HBM_SPEC] * len(names), out_specs=[HBM_SPEC] * len(names),
        out_shape=[jax.ShapeDtypeStruct(parts[n].shape, parts[n].dtype) for n in names],
        scratch_shapes=[pltpu.SemaphoreType.DMA((len(names),)), pltpu.SemaphoreType.DMA((len(names),))],
    )(*[parts[n] for n in names])
    return dict(zip(names, res))


ELEMENTWISE_BLOCK_BYTES = 1 << 20


def _rows(r, c):
    for t in (512, 256, 128, 64, 32, 16, 8):
        if r % t == 0 and t * c * 4 <= ELEMENTWISE_BLOCK_BYTES:
            return t
    return r


def _sum4(owns, axis, recv, kidx, *, name):
    L = len(owns)
    R, C = recv.shape[2:]
    tm = _rows(R, C)
    nr = R // tm

    def body(k_ref, *refs):
        own_refs, r_ref, out_ref = refs[:L], refs[L], refs[L + 1]
        for li in range(L):
            @pl.when(pl.program_id(0) == li)
            def _(o_ref=own_refs[li]):
                out_ref[...] = ((o_ref[...] + r_ref[0, 0].astype(f32)) + r_ref[1, 0].astype(f32)) + r_ref[2, 0].astype(f32)

    own_map = (lambda l, i, k: (i, k[0])) if axis == 1 else (lambda l, i, k: (k[0] * nr + i, 0))
    return pl.pallas_call(
        body, name=name, out_shape=jax.ShapeDtypeStruct((L * R, C), f32),
        grid_spec=pltpu.PrefetchScalarGridSpec(
            num_scalar_prefetch=1, grid=(L, nr),
            in_specs=[pl.BlockSpec((tm, C), own_map)] * L + [pl.BlockSpec((3, 1, tm, C), lambda l, i, k: (0, l, i, 0))],
            out_specs=pl.BlockSpec((tm, C), lambda l, i, k: (l * nr + i, 0))),
        compiler_params=_cp("parallel", "parallel"),
    )(kidx, *owns, recv)


def _adamw(w, m, v, parts, *, name):
    R, C = w.shape
    tm = _rows(R, C)
    npart = len(parts)

    def body(*refs):
        w_ref, m_ref, v_ref = refs[:3]
        g_ref, d_ref, m2_ref, v2_ref = refs[3 + npart:]
        g = refs[3][...]
        for p_ref in refs[4:3 + npart]:
            g = g + p_ref[...]
        g_ref[...] = g
        d_ref[...], m2_ref[...], v2_ref[...] = _adam_math(w_ref[...], m_ref[...], v_ref[...], g)

    blk = pl.BlockSpec((tm, C), lambda i: (i, 0))
    return pl.pallas_call(
        body, name=name, grid=(R // tm,),
        in_specs=[blk] * (3 + npart), out_specs=[blk] * 4,
        out_shape=[jax.ShapeDtypeStruct((R, C), f32)] * 4, compiler_params=_cp("parallel"),
    )(w, m, v, *parts)


def _adam_math(w, m, v, g):
    m2 = ADAM_B1 * m + (1.0 - ADAM_B1) * g
    v2 = ADAM_B2 * v + (1.0 - ADAM_B2) * (g * g)
    m_hat = m2 / (1.0 - ADAM_B1 ** ADAM_STEP)
    v_hat = v2 / (1.0 - ADAM_B2 ** ADAM_STEP)
    return -ADAM_LR * (m_hat / (jnp.sqrt(v_hat) + ADAM_EPS) + ADAM_WD * w), m2, v2


def _adamw_small(landed, w, m, v, kidx, ra, rb):
    rs = ra + N_CHIPS * rb

    def body(k_ref, l_ref, w_ref, m_ref, v_ref, g_ref, d_ref, m2_ref, v2_ref):
        mine = pl.multiple_of(ra + k_ref[0] * rb, SUBLANES)
        for lo, n, off in ((0, ra, 0), (ra, rb, mine)):
            g = l_ref[pl.ds(off, n), :]
            for d in range(1, N_DEV):
                g = g + l_ref[pl.ds(d * rs + off, n), :]
            rows = pl.ds(lo, n)
            delta, m2, v2 = _adam_math(w_ref[rows, :], m_ref[rows, :], v_ref[rows, :], g)
            g_ref[rows, :] = g
            d_ref[rows, :] = delta
            m2_ref[rows, :] = m2
            v2_ref[rows, :] = v2

    vmem = pl.BlockSpec(memory_space=pltpu.VMEM)
    return pl.pallas_call(
        body, name="adamw_small", out_shape=[jax.ShapeDtypeStruct(w.shape, f32)] * 4,
        grid_spec=pltpu.PrefetchScalarGridSpec(num_scalar_prefetch=1, grid=(), in_specs=[vmem] * 4, out_specs=[vmem] * 4),
        compiler_params=_cp(),
    )(kidx, landed, w, m, v)


def _pad_odd(w):
    return jnp.pad(w, ((0, 0), (0, ODD_PAD - w.shape[1])))


def _uq_cat(w):
    r = w.shape[0]
    return jnp.pad(w.reshape(r, MLA_HEADS, MLA_QK), ((0, 0), (0, 0), (0, HQ - MLA_QK))).reshape(r, MLA_HEADS * HQ)


def _uq_uncat(w):
    r = w.shape[0]
    return w.reshape(r, MLA_HEADS, HQ)[:, :, :MLA_QK].reshape(r, MLA_HEADS * MLA_QK)


def _to_segments(v):
    T, C = v.shape
    return v.reshape(S5_SEG, T // S5_SEG, C).transpose(1, 0, 2).reshape(T, C)


def _from_segments(v):
    T, C = v.shape
    return v.reshape(T // S5_SEG, S5_SEG, C).transpose(1, 0, 2).reshape(T, C)


def _s5_rb(T):
    return min(512, T)


def _ffn_fwd(h, g, w_in, cw, cb, w_out, tag):
    hn = _rms_fwd(h, g, name=f"ffn{tag}_norm")
    au = _mm(hn, w_in, name=f"ffn{tag}_in", tn=1408)
    z = _ffn_mid_fwd(au, cw, cb, name=f"ffn{tag}_mid")
    return _mm(z, w_out, res=h, name=f"ffn{tag}_out", tk=1408), (hn, au, z)


def _ffn_bwd(h, g, w_in, cw, cb, w_out, saved, dh, tag, dep=None):
    hn, au, z = saved
    dz = _mm(dh, w_out, tb=True, name=f"ffn{tag}_dz", tn=1408, dep=dep)
    dw_out = _mm(z, dh, ta=True, also_bf16=True, name=f"ffn{tag}_dwout", tm=1408)
    dau, dcw, dcb = _ffn_mid_bwd(au, cw, cb, dz, name=f"ffn{tag}_dmid")
    dhn = _mm(dau, w_in, tb=True, name=f"ffn{tag}_dhn", tk=1408)
    dw_in = _mm(hn, dau, ta=True, also_bf16=True, name=f"ffn{tag}_dwin", tn=1408)
    dh_in, dg = _rms_bwd(h, g, dhn, dh, name=f"ffn{tag}_dnorm")
    return dh_in, dg, dw_in, dcw, dcb, dw_out


def _local_step(x, positions, target, get_w, P, put_g):
    T = x.shape[0]
    rb = _s5_rb(T)
    row = lambda v: v.reshape(1, -1)
    g_mix, g_ffn = P["norm_mix_g"], P["norm_ffn_g"]
    lbl, hng = P["hgrn_lb_logits"], P["hgrn_norm_g"]
    dsk, bg = P["s5_d"], P["s5_b_glu"]
    qg, kvg = P["mla_q_norm_g"], P["mla_kv_norm_g"]
    cw, cb = P["ffn_conv_w"], P["ffn_conv_b"]

    col = lambda v: v.reshape(S5_N, 1)
    disc_in = (col(P["s5_a_re"]), col(P["s5_a_im"]), col(jnp.repeat(P["s5_log_dt"].reshape(S5_GROUPS), S5_STATE)),
               P["s5_b_re"].reshape(S5_N, S5_GROUP), P["s5_b_im"].reshape(S5_N, S5_GROUP))
    abr, abi, bbr, bbi = _s5_disc_fwd(*disc_in)
    ar, ai = abr.reshape(1, S5_N), abi.reshape(1, S5_N)
    bbr3, bbi3 = bbr.reshape(S5_GROUPS, S5_STATE, S5_GROUP), bbi.reshape(S5_GROUPS, S5_STATE, S5_GROUP)
    bre, bim = _blockdiag(bbr3, True).astype(bf16), _blockdiag(bbi3, True).astype(bf16)
    bret, bimt = _blockdiag(bbr3).astype(bf16), _blockdiag(bbi3).astype(bf16)
    c_re, c_im = P["s5_c_re"].reshape(S5_GROUPS, S5_GROUP, S5_STATE), P["s5_c_im"].reshape(S5_GROUPS, S5_GROUP, S5_STATE)
    cre, cim = _blockdiag(c_re, True).astype(bf16), _blockdiag(c_im, True).astype(bf16)
    cret, cimt = _blockdiag(c_re).astype(bf16), _blockdiag(c_im).astype(bf16)

    hn0 = _rms_fwd(x, g_mix[0:1], name="mix0_norm")
    We = get_w("even_in", hn0)
    proj_e = _mm(hn0, We["even_w_in"], name="even_in", tn=1280)
    Wr = get_w("even_rest", proj_e)
    ya, states = _hgrn_fwd(proj_e, lbl, hng)
    u_seg = _to_segments(proj_e[:, 4 * 512:])
    fr, fi = _s5_final(u_seg, bre, bim, ar, ai, rb=rb)
    yb_seg, s0r, s0i = _s5_fwd(u_seg, bre, bim, ar, ai, fr, fi, cre, cim, dsk, Wr["s5_w_glu"], bg, rb=rb)
    ycat = jnp.concatenate([ya, _from_segments(yb_seg)], axis=1)
    h1 = _mm(ycat, Wr["even_w_out"], res=x, name="even_out")
    Wf0 = get_w("ffn0", h1)
    h2, ffn0 = _ffn_fwd(h1, g_ffn[0:1], Wf0["ffn_w_in"], cw[0], cb[0:1], Wf0["ffn_w_out"], 0)

    tabs = _rope_tables(positions)
    hn2 = _rms_fwd(h2, g_mix[1:2], name="mix1_norm")
    Wo = get_w("odd", hn2)
    proj_o = _mm(hn2, Wo["odd_w_in"], name="odd_in")
    cqn, ckvn, kr = _mla_prep_fwd(proj_o, qg, kvg, tabs)
    q = _q_post(_mm(cqn, Wo["mla_w_uq"], name="mla_uq"), tabs, transpose=False, name="q_post")
    kvb = _mm(ckvn, Wo["mla_w_ukv"], out_dtype=bf16, name="mla_ukv")
    o, lse = _flash_fwd(q, kvb, kr)
    h3 = _mm(o, Wo["odd_w_out"], res=h2, name="odd_out")
    Wf1 = get_w("ffn1", h3)
    h4, ffn1 = _ffn_fwd(h3, g_ffn[1:2], Wf1["ffn_w_in"], cw[1], cb[1:2], Wf1["ffn_w_out"], 1)
    loss, dh4, dg_final = _loss_head(h4, row(P["final_norm_g"]), target)

    dh3, dg_ffn1, dw_fin1, dcw1, dcb1, dw_fout1 = _ffn_bwd(
        h3, g_ffn[1:2], Wf1["ffn_w_in"], cw[1], cb[1:2], Wf1["ffn_w_out"], ffn1, dh4, 1)
    sent = put_g("ffn1", {"ffn_w_in": dw_fin1, "ffn_w_out": dw_fout1})
    do = _mm(dh3, Wo["odd_w_out"], tb=True, name="odd_do", dep=sent)
    dw_oout = _mm(o, dh3, ta=True, also_bf16=True, name="odd_dwout")
    delta = _attn_delta(o, do)
    dkv, dkr_h, dq = _flash_bwd(q, kvb, kr, do, lse, delta)
    dq = _q_post(dq, tabs, transpose=True, name="dq_post")
    dw_uq = _mm(cqn, dq, ta=True, also_bf16=True, name="mla_dwuq")
    dcqn = _mm(dq, Wo["mla_w_uq"], tb=True, name="mla_dcq")
    dw_ukv = _mm(ckvn, dkv, ta=True, also_bf16=True, name="mla_dwukv")
    dckvn = _mm(dkv, Wo["mla_w_ukv"], tb=True, name="mla_dckv")
    dproj_o, dqg, dkvg = _mla_prep_bwd(proj_o, qg, kvg, tabs, dcqn, dckvn, dkr_h)
    dhn2 = _mm(dproj_o, Wo["odd_w_in"], tb=True, name="odd_dhn")
    dw_oin = _mm(hn2, dproj_o, ta=True, also_bf16=True, name="odd_dwin")
    sent = put_g("odd", {"odd_w_in": dw_oin, "mla_w_uq": dw_uq, "mla_w_ukv": dw_ukv, "odd_w_out": dw_oout})
    dh2, dg_mix1 = _rms_bwd(h2, g_mix[1:2], dhn2, dh3, name="mix1_dnorm")

    dh1, dg_ffn0, dw_fin0, dcw0, dcb0, dw_fout0 = _ffn_bwd(
        h1, g_ffn[0:1], Wf0["ffn_w_in"], cw[0], cb[0:1], Wf0["ffn_w_out"], ffn0, dh2, 0, dep=sent)
    sent = put_g("ffn0", {"ffn_w_in": dw_fin0, "ffn_w_out": dw_fout0})
    dycat = _mm(dh1, Wr["even_w_out"], tb=True, name="even_dy", dep=sent)
    dw_eout = _mm(ycat, dh1, ta=True, also_bf16=True, name="even_dwout")
    dq_h, df_h, di_h, dg_h, dlbl, dhng = _hgrn_bwd(proj_e, lbl, hng, states, dycat)
    dyb_seg = _to_segments(dycat[:, 512:])
    dy_s5, glr, gli, dcre, dcim, dd, dwg, dbg = _s5_bwd_a(
        u_seg, bre, bim, ar, ai, s0r, s0i, cre, cim, cret, cimt, dsk, Wr["s5_w_glu"], bg, dyb_seg, rb=rb)
    du_seg, dbre, dbim, dar, dai = _s5_bwd_b(
        u_seg, bre, bim, bret, bimt, ar, ai, s0r, s0i, glr, gli, cret, cimt, dsk, dy_s5, rb=rb)
    dproj_e = jnp.concatenate([dq_h, df_h, di_h, dg_h, _from_segments(du_seg)], axis=1)
    dhn0 = _mm(dproj_e, We["even_w_in"], tb=True, name="even_dhn", tk=1280)
    dw_ein = _mm(hn0, dproj_e, ta=True, also_bf16=True, name="even_dwin", tn=1280)
    dx, dg_mix0 = _rms_bwd(x, g_mix[0:1], dhn0, dh1, name="mix0_dnorm")

    unblk = lambda m, a, b: jnp.swapaxes(_blockdiag_t(m, a, b), 1, 2)
    dbbr = unblk(dbre, S5_GROUP, S5_STATE).reshape(S5_N, S5_GROUP)
    dbbi = unblk(dbim, S5_GROUP, S5_STATE).reshape(S5_N, S5_GROUP)
    d_ar, d_ai, d_ldt, d_br, d_bi = _s5_disc_bwd(*disc_in, (dar.reshape(S5_N, 1), dai.reshape(S5_N, 1), dbbr, dbbi))
    small = {
        "norm_mix_g": jnp.concatenate([dg_mix0, dg_mix1], axis=0),
        "norm_ffn_g": jnp.concatenate([dg_ffn0, dg_ffn1], axis=0),
        "final_norm_g": dg_final.reshape(-1),
        "hgrn_lb_logits": dlbl, "hgrn_norm_g": dhng,
        "s5_a_re": d_ar.reshape(1, S5_GROUPS, S5_STATE), "s5_a_im": d_ai.reshape(1, S5_GROUPS, S5_STATE),
        "s5_log_dt": d_ldt.reshape(S5_GROUPS, S5_STATE).sum(axis=1).reshape(1, S5_GROUPS),
        "s5_b_re": d_br.reshape(1, S5_GROUPS, S5_STATE, S5_GROUP), "s5_b_im": d_bi.reshape(1, S5_GROUPS, S5_STATE, S5_GROUP),
        "s5_c_re": unblk(dcre, S5_STATE, S5_GROUP).reshape(1, S5_GROUPS, S5_GROUP, S5_STATE),
        "s5_c_im": unblk(dcim, S5_STATE, S5_GROUP).reshape(1, S5_GROUPS, S5_GROUP, S5_STATE),
        "s5_d": dd, "s5_b_glu": dbg, "mla_q_norm_g": dqg, "mla_kv_norm_g": dkvg,
        "ffn_conv_w": jnp.stack([dcw0, dcw1]), "ffn_conv_b": jnp.concatenate([dcb0, dcb1], axis=0),
    }
    put_g("even", {"even_w_in": dw_ein, "s5_w_glu": (dwg, dwg.astype(bf16)), "even_w_out": dw_eout}, small)
    return loss, dx


WEIGHTS = ["norm_mix_g", "norm_ffn_g", "final_norm_g", "even_w_in", "hgrn_lb_logits", "hgrn_norm_g", "s5_a_re", "s5_a_im",
           "s5_log_dt", "s5_b_re", "s5_b_im", "s5_c_re", "s5_c_im", "s5_d", "s5_w_glu", "s5_b_glu", "even_w_out", "odd_w_in",
           "mla_q_norm_g", "mla_w_uq", "mla_kv_norm_g", "mla_w_ukv", "odd_w_out", "ffn_w_in", "ffn_conv_w", "ffn_conv_b",
           "ffn_w_out"]
SMALL_SHARDED = {"mla_q_norm_g": 1, "mla_kv_norm_g": 1, "ffn_conv_w": 2}
SMALL = [n for n in WEIGHTS if n not in BIG]
SMALL_REP = [n for n in SMALL if n not in SMALL_SHARDED]


def _pack_rows(shapes):
    n = sum(math.prod(s) for s in shapes)
    return -(-n // (SUBLANES * LANES)) * SUBLANES


def _pack(arrays, rows):
    flat = jnp.concatenate([a.reshape(-1) for a in arrays])
    return jnp.pad(flat, (0, rows * LANES - flat.shape[0])).reshape(rows, LANES)


def _unpack(block, shapes):
    flat, out, off = block.reshape(-1), [], 0
    for s in shapes:
        n = math.prod(s)
        out.append(flat[off:off + n].reshape(s))
        off += n
    return out


def kernel(x, positions, norm_mix_g, norm_ffn_g, final_norm_g, even_w_in, hgrn_lb_logits, hgrn_norm_g, s5_a_re, s5_a_im, s5_log_dt, s5_b_re, s5_b_im, s5_c_re, s5_c_im, s5_d, s5_w_glu, s5_b_glu, even_w_out, odd_w_in, mla_q_norm_g, mla_w_uq, mla_kv_norm_g, mla_w_ukv, odd_w_out, ffn_w_in, ffn_conv_w, ffn_conv_b, ffn_w_out, loss_target, m_norm_mix_g, m_norm_ffn_g, m_final_norm_g, m_even_w_in, m_hgrn_lb_logits, m_hgrn_norm_g, m_s5_a_re, m_s5_a_im, m_s5_log_dt, m_s5_b_re, m_s5_b_im, m_s5_c_re, m_s5_c_im, m_s5_d, m_s5_w_glu, m_s5_b_glu, m_even_w_out, m_odd_w_in, m_mla_q_norm_g, m_mla_w_uq, m_mla_kv_norm_g, m_mla_w_ukv, m_odd_w_out, m_ffn_w_in, m_ffn_conv_w, m_ffn_conv_b, m_ffn_w_out, v_norm_mix_g, v_norm_ffn_g, v_final_norm_g, v_even_w_in, v_hgrn_lb_logits, v_hgrn_norm_g, v_s5_a_re, v_s5_a_im, v_s5_log_dt, v_s5_b_re, v_s5_b_im, v_s5_c_re, v_s5_c_im, v_s5_d, v_s5_w_glu, v_s5_b_glu, v_even_w_out, v_odd_w_in, v_mla_q_norm_g, v_mla_w_uq, v_mla_kv_norm_g, v_mla_w_ukv, v_odd_w_out, v_ffn_w_in, v_ffn_conv_w, v_ffn_conv_b, v_ffn_w_out):
    args = dict(locals())
    w = {n: args[n] for n in WEIGHTS}
    m = {n: args["m_" + n] for n in WEIGHTS}
    v = {n: args["v_" + n] for n in WEIGHTS}
    k = 2 * lax.axis_index("x") + lax.axis_index("y")
    kidx = k.reshape(1).astype(jnp.int32)
    axis2d = lambda n: BIG[n] - (1 if n in LAYERED else 0)
    slab = lambda n: w[n].shape[1 + axis2d(n)]

    small_sh_shapes = [w[n].shape for n in SMALL_SHARDED]
    rb = _pack_rows(small_sh_shapes)
    items = {}
    for group, names in GROUPS.items():
        layer = GROUP_LAYER.get(group, 0)
        items[group] = [(_Gather(axis2d(n), slab(n)), None,
                         _place_slab(w[n][layer], axis2d(n), N_CHIPS, kidx, bf16, name=f"place_{n}_{layer}")) for n in names]
    items["even_in"].append((_Gather(0, rb), None,
                             _place_slab(_pack([w[n] for n in SMALL_SHARDED], rb), 0, N_CHIPS, kidx, f32, name="place_small")))
    gathers, tokens = {}, []
    for group in GROUPS:
        sems, srcs, lands, token = _push_start(f"gather_start_{group}", items[group])
        gathers[group] = ([it[0] for it in items[group]], sems, srcs, lands)
        tokens.append(token[0, 0])
    started = functools.reduce(jnp.add, tokens)

    def landed(group, after):
        return _push_wait(f"gather_wait_{group}", [gathers[group]], [after])[0]

    even = landed("even_in", (started + norm_mix_g[0, 0]).reshape(1))
    per_chip = [_unpack(even[-1][c * rb:(c + 1) * rb], small_sh_shapes) for c in range(N_CHIPS)]
    P = {n: w[n] for n in SMALL_REP}
    for i, (n, ax) in enumerate(SMALL_SHARDED.items()):
        P[n] = jnp.concatenate([per_chip[c][i] for c in range(N_CHIPS)], axis=ax)
    P["mla_q_norm_g"], P["mla_kv_norm_g"] = P["mla_q_norm_g"].reshape(1, -1), P["mla_kv_norm_g"].reshape(1, -1)
    fix_w = {"odd_w_in": _pad_odd, "mla_w_uq": _uq_cat}

    def get_w(group, after):
        full = even if group == "even_in" else landed(group, after)
        return {n: fix_w.get(n, lambda a: a)(a) for n, a in zip(GROUPS[group], full)}

    fix_g = {"odd_w_in": lambda g: g[:, :odd_w_in.shape[2]], "mla_w_uq": _uq_uncat}
    g32, scatters, land_now = {}, {}, {}
    ra = _pack_rows([w[n].shape for n in SMALL_REP])
    rs = ra + N_CHIPS * rb
    didx = (2 * kidx + lax.axis_index("c")).astype(jnp.int32)

    def put_g(group, grads, small=None):
        layer = GROUP_LAYER.get(group)
        routes, srcs, names = [], [], list(grads)
        for n in names:
            f = fix_g.get(n, lambda g: g)
            g32.setdefault(n, {})[layer or 0] = f(grads[n][0])
            routes.append(_Scatter(axis2d(n), slab(n), layer if n in LAYERED else None))
            srcs.append(f(grads[n][1]))
            if n not in land_now:
                land_now[n] = lax.empty((3,) + w[n].shape[0 if n in LAYERED else 1:], bf16)
        if small is not None:
            blocks = [_pack([small[n] for n in SMALL_REP], ra)]
            for chip in range(N_CHIPS):
                sl = lambda n, ax: lax.slice_in_dim(small[n].reshape(w[n].shape[:ax] + (-1,) + w[n].shape[ax + 1:]),
                                                    chip * w[n].shape[ax], (chip + 1) * w[n].shape[ax], axis=ax)
                blocks.append(_pack([sl(n, ax) for n, ax in SMALL_SHARDED.items()], rb))
            names.append("small")
            routes.append(_ToAll(rs))
            srcs.append(None)
            land_now["small"] = _place_slab(jnp.concatenate(blocks), 0, N_DEV, didx, f32, name="place_small_grads")
        sems, srcs, lands, token = _push_start(f"scatter_start_{group}", [(r, s, land_now[n]) for r, s, n in zip(routes, srcs, names)])
        land_now.update(zip(names, lands))
        scatters[group] = (routes, sems, srcs, names)
        sent.append(token)
        return token

    sent = []
    loss, dx = _local_step(x[0], positions[0], loss_target[0], get_w, P, put_g)
    sent_last = sent[-1]
    loss = lax.psum(loss[0, 0], ("x", "y", "c"))

    out = {}

    def finish(tag, groups, after):
        waits = [(scatters[g][0], scatters[g][1], scatters[g][2], [land_now[n] for n in scatters[g][3]]) for g in groups]
        for g, lands in zip(groups, _push_wait(f"scatter_wait_{tag}", waits, after)):
            land_now.update(zip(scatters[g][3], lands))
        names = [n for n in dict.fromkeys(n for g in groups for n in scatters[g][3]) if n != "small"]
        part = {}
        for n in names:
            recv = land_now[n] if n in LAYERED else land_now[n][:, None]
            part[n] = _sum4([g32[n][l] for l in sorted(g32[n])], axis2d(n), recv, kidx, name=f"sum4_{n}")
        other = _swap_with_sibling(part, tag)
        done = []
        for n in names:
            C = part[n].shape[-1]
            res = _adamw(w[n].reshape(-1, C), m[n].reshape(-1, C), v[n].reshape(-1, C), [part[n], other[n]], name=f"adamw_{n}")
            out[n] = [r.reshape(w[n].shape) for r in res]
            done.append(res[0])
        return done

    done = finish("a", ["ffn1", "odd", "ffn0"], [dx, sent_last])
    finish("b", ["even"], done)

    order = SMALL_REP + list(SMALL_SHARDED)
    packed = lambda src: jnp.concatenate([_pack([src[n] for n in SMALL_REP], ra), _pack([src[n] for n in SMALL_SHARDED], rb)])
    res = _adamw_small(land_now["small"], packed(w), packed(m), packed(v), kidx, ra, rb)
    for r in res:
        parts = _unpack(r[:ra], [w[n].shape for n in SMALL_REP]) + _unpack(r[ra:], small_sh_shapes)
        for n, a in zip(order, parts):
            out.setdefault(n, []).append(a)

    return (loss, dx[None], *[out[n][0] for n in WEIGHTS], *[out[n][1] for n in WEIGHTS],
            *[out[n][2] for n in WEIGHTS], *[out[n][3] for n in WEIGHTS])
```

```python
import functools
import math

import jax
import jax.numpy as jnp
from jax import lax
from jax.experimental import pallas as pl
from jax.experimental.pallas import tpu as pltpu

f32, bf16 = jnp.float32, jnp.bfloat16
EPS = 1e-6
LANES = 128
SUBLANES = 8
VMEM_BYTES = 48 * 1024 * 1024
HGRN_CHUNK = 64
HGRN_HEADS = 4
S5_GROUPS, S5_STATE, S5_GROUP = 32, 64, 16
S5_N = S5_GROUPS * S5_STATE
S5_SEG = SUBLANES
MLA_HEADS, MLA_NOPE, MLA_ROPE, MLA_V = 8, 128, 64, 128
MLA_QK = MLA_NOPE + MLA_ROPE
MLA_Q_RANK, MLA_KV_RANK = 384, 256
ROPE_THETA = 10000.0
D_FF = 2816
ADAM_LR, ADAM_B1, ADAM_B2, ADAM_EPS, ADAM_WD, ADAM_STEP = 0.001, 0.9, 0.999, 1e-08, 0.01, 10
MESH = pl.DeviceIdType.MESH
HI = lax.Precision.HIGHEST


def _cp(*dims):
    return pltpu.CompilerParams(dimension_semantics=dims if dims else None, vmem_limit_bytes=VMEM_BYTES)


def _tile(n, t):
    if n <= t:
        return n
    c = (t // LANES) * LANES
    while c >= LANES:
        if n % c == 0:
            return c
        c -= LANES
    return n


def _dot(a, b, dn=None, precision=None):
    if dn is None:
        dn = (((a.ndim - 1,), (0,)), ((), ()))
    return lax.dot_general(a, b, dn, preferred_element_type=f32, precision=precision)


NT = (((1,), (1,)), ((), ()))
TN = (((0,), (0,)), ((), ()))


def _bdot(a, b, dn=None):
    return _dot(a.astype(bf16), b.astype(bf16), dn)


def _mm(a, b, *, name, ta=False, tb=False, out_dtype=f32, res=None, also_bf16=False, tm=1024, tn=1024, tk=1024, dep=None):
    halves = lambda s: (s[1], 2 * s[2]) if len(s) == 3 else s
    M, K = (a.shape[1], a.shape[0]) if ta else halves(a.shape)
    N = b.shape[0] if tb else halves(b.shape)[1]
    tm, tn, tk = _tile(M, tm), _tile(N, tn), _tile(K, tk)
    if a.ndim == 3:
        tk = _tile(K // 2, tk)
    if b.ndim == 3:
        tn = _tile(N // 2, tn)
    nk = K // tk
    dn = (((0 if ta else 1,), (1 if tb else 0,)), ((), ()))

    def body(*refs):
        a_ref, b_ref = refs[0], refs[1]
        r_ref = refs[2] if res is not None else None
        nin = 2 + (res is not None) + (dep is not None)
        outs = refs[nin:-1]
        acc = refs[-1]
        k = pl.program_id(2)
        p = _bdot(a_ref[...], b_ref[...], dn)

        @pl.when(k == 0)
        def _():
            acc[...] = p

        @pl.when(k > 0)
        def _():
            acc[...] += p

        @pl.when(k == nk - 1)
        def _():
            r = acc[...]
            if r_ref is not None:
                r = r + r_ref[...]
            outs[0][...] = r.astype(out_dtype)
            if also_bf16:
                outs[1][...] = r.astype(bf16)

    a_spec = pl.BlockSpec((tk, tm), lambda i, j, k: (k, i)) if ta else pl.BlockSpec((tm, tk), lambda i, j, k: (i, k))
    b_spec = pl.BlockSpec((tn, tk), lambda i, j, k: (j, k)) if tb else pl.BlockSpec((tk, tn), lambda i, j, k: (k, j))
    if a.ndim == 3:
        kh = K // 2 // tk
        a_spec = pl.BlockSpec((None, tm, tk), lambda i, j, k: (k // kh, i, k % kh))
    if b.ndim == 3:
        nh = N // 2 // tn
        b_spec = pl.BlockSpec((None, tk, tn), lambda i, j, k: (j // nh, k, j % nh))
    o_spec = pl.BlockSpec((tm, tn), lambda i, j, k: (i, j))
    in_specs, args = [a_spec, b_spec], [a, b]
    if res is not None:
        in_specs.append(o_spec)
        args.append(res)
    if dep is not None:
        in_specs.append(pl.BlockSpec(memory_space=pl.ANY))
        args.append(dep)
    out_shape = [jax.ShapeDtypeStruct((M, N), out_dtype)]
    out_specs = [o_spec]
    if also_bf16:
        out_shape.append(jax.ShapeDtypeStruct((M, N), bf16))
        out_specs.append(o_spec)
    out = pl.pallas_call(
        body, name=name, grid=(M // tm, N // tn, nk), in_specs=in_specs, out_specs=out_specs, out_shape=out_shape,
        scratch_shapes=[pltpu.VMEM((tm, tn), f32)], compiler_params=_cp("parallel", "parallel", "arbitrary"),
    )(*args)
    return out if also_bf16 else out[0]


def _rms_fwd(x, g, *, name, col=0, width=None, tm=512):
    T = x.shape[0]
    width = x.shape[1] if width is None else width
    tm = _tile(T, tm)

    def body(x_ref, g_ref, o_ref):
        xv = x_ref[...]
        r = lax.rsqrt(jnp.mean(xv * xv, axis=-1, keepdims=True) + EPS)
        o_ref[...] = (xv * r * g_ref[...]).astype(bf16)

    return pl.pallas_call(
        body, name=name, grid=(T // tm,),
        in_specs=[pl.BlockSpec((tm, width), lambda i: (i, col)), pl.BlockSpec((1, width), lambda i: (0, 0))],
        out_specs=pl.BlockSpec((tm, width), lambda i: (i, 0)), out_shape=jax.ShapeDtypeStruct((T, width), bf16),
        compiler_params=_cp("parallel"),
    )(x, g)


def _rms_bwd_math(xv, g, dy):
    r = lax.rsqrt(jnp.mean(xv * xv, axis=-1, keepdims=True) + EPS)
    xh = xv * r
    dxh = dy * g
    dx = r * (dxh - xh * jnp.mean(dxh * xh, axis=-1, keepdims=True))
    dg = jnp.sum(dy * xh, axis=0, keepdims=True)
    return dx, dg


def _rms_bwd(x, g, dy, res=None, *, name, tm=512):
    T, D = x.shape
    tm = _tile(T, tm)

    def body(*refs):
        x_ref, g_ref, dy_ref = refs[:3]
        r_ref = refs[3] if res is not None else None
        dx_ref, dg_ref = refs[-2:]
        dx, dg = _rms_bwd_math(x_ref[...], g_ref[...], dy_ref[...].astype(f32))
        if r_ref is not None:
            dx = dx + r_ref[...]
        dx_ref[...] = dx

        @pl.when(pl.program_id(0) == 0)
        def _():
            dg_ref[...] = dg

        @pl.when(pl.program_id(0) > 0)
        def _():
            dg_ref[...] += dg

    row = pl.BlockSpec((tm, D), lambda i: (i, 0))
    vec = pl.BlockSpec((1, D), lambda i: (0, 0))
    in_specs, args = [row, vec, row], [x, g, dy]
    if res is not None:
        in_specs.append(row)
        args.append(res)
    return pl.pallas_call(
        body, name=name, grid=(T // tm,), in_specs=in_specs, out_specs=[row, vec],
        out_shape=[jax.ShapeDtypeStruct((T, D), f32), jax.ShapeDtypeStruct((1, D), f32)],
        compiler_params=_cp("arbitrary"),
    )(*args)


def _loss_head(h, g, target, *, tm=512):
    T, D = h.shape
    tm = _tile(T, tm)

    def body(h_ref, g_ref, t_ref, loss_ref, dh_ref, dg_ref):
        hv, gv = h_ref[...], g_ref[...]
        r = lax.rsqrt(jnp.mean(hv * hv, axis=-1, keepdims=True) + EPS)
        e = hv * r * gv - t_ref[...]
        part = 0.5 * jnp.sum(jnp.mean(e * e, axis=-1, keepdims=True), axis=0, keepdims=True)
        dx, dg = _rms_bwd_math(hv, gv, e * (1.0 / D))
        dh_ref[...] = dx

        @pl.when(pl.program_id(0) == 0)
        def _():
            loss_ref[...] = part
            dg_ref[...] = dg

        @pl.when(pl.program_id(0) > 0)
        def _():
            loss_ref[...] += part
            dg_ref[...] += dg

    row = pl.BlockSpec((tm, D), lambda i: (i, 0))
    vec = pl.BlockSpec((1, D), lambda i: (0, 0))
    return pl.pallas_call(
        body, name="loss_head", grid=(T // tm,), in_specs=[row, vec, row],
        out_specs=[pl.BlockSpec((1, 1), lambda i: (0, 0)), row, vec],
        out_shape=[jax.ShapeDtypeStruct((1, 1), f32), jax.ShapeDtypeStruct((T, D), f32), jax.ShapeDtypeStruct((1, D), f32)],
        compiler_params=_cp("arbitrary"),
    )(h, g, target)


FFN_ROWS = 256
HALO = SUBLANES


def _conv_taps(a_ref, c, rc):
    if isinstance(c, int) and c == 0:
        ext = jnp.concatenate([jnp.zeros((HALO, LANES), f32), a_ref[pl.ds(0, rc), :]], axis=0)
    else:
        ext = a_ref[pl.ds(pl.multiple_of(c * rc - HALO, HALO), rc + HALO), :]
    return ext[HALO:], pltpu.roll(ext, 1, 0)[HALO:], pltpu.roll(ext, 2, 0)[HALO:]


def _chunk_rows(c, rc):
    return pl.ds(c * rc, rc) if isinstance(c, int) else pl.ds(pl.multiple_of(c * rc, rc), rc)


def _ffn_mid_fwd(au, cw, cb, *, name):
    T = au.shape[0]
    F = au.shape[1] // 2
    nb = F // LANES
    rc = min(FFN_ROWS, T)
    nc = T // rc

    def body(a_ref, u_ref, w_ref, b_ref, z_ref):
        w, b = w_ref[...], b_ref[...]

        def chunk(c):
            a, a1, a2 = _conv_taps(a_ref, c, rc)
            rows = _chunk_rows(c, rc)
            ac = w[0:1] * a2 + w[1:2] * a1 + w[2:3] * a + b
            z_ref[rows, :] = (ac * jax.nn.sigmoid(ac) * u_ref[rows, :]).astype(bf16)

        chunk(0)
        lax.fori_loop(1, nc, lambda c, _: chunk(c), None)

    return pl.pallas_call(
        body, name=name, grid=(nb,),
        in_specs=[pl.BlockSpec((T, LANES), lambda j: (0, j)), pl.BlockSpec((T, LANES), lambda j: (0, nb + j)),
                  pl.BlockSpec((3, LANES), lambda j: (0, j)), pl.BlockSpec((1, LANES), lambda j: (0, j))],
        out_specs=pl.BlockSpec((T, LANES), lambda j: (0, j)), out_shape=jax.ShapeDtypeStruct((T, F), bf16),
        compiler_params=_cp("parallel"),
    )(au, au, cw, cb)


def _ffn_mid_bwd(au, cw, cb, dz, *, name):
    T = au.shape[0]
    F = au.shape[1] // 2
    nb = F // LANES

    rc = min(FFN_ROWS, T)
    nc = T // rc

    def body(a_ref, u_ref, w_ref, b_ref, dz_ref, dau_ref, dw_ref, db_ref):
        w, b = w_ref[...], b_ref[...]

        def chunk(c, carry):
            nxt, s0, s1, s2, sb = carry
            a, a1, a2 = _conv_taps(a_ref, c, rc)
            rows = _chunk_rows(c, rc)
            ac = w[0:1] * a2 + w[1:2] * a1 + w[2:3] * a + b
            sg = jax.nn.sigmoid(ac)
            dz = dz_ref[rows, :].astype(f32)
            dau_ref[1, rows, :] = (dz * ac * sg).astype(bf16)
            dac = dz * u_ref[rows, :] * sg * (1.0 + ac * (1.0 - sg))
            ext = jnp.concatenate([dac, nxt], axis=0)
            d1, d2 = pltpu.roll(ext, rc + HALO - 1, 0)[:rc], pltpu.roll(ext, rc + HALO - 2, 0)[:rc]
            dau_ref[0, rows, :] = (w[2:3] * dac + w[1:2] * d1 + w[0:1] * d2).astype(bf16)
            tot = lambda v: jnp.sum(v, axis=0, keepdims=True)
            return dac[:HALO], s0 + tot(dac * a2), s1 + tot(dac * a1), s2 + tot(dac * a), sb + tot(dac)

        z = jnp.zeros((1, LANES), f32)
        carry = (jnp.zeros((HALO, LANES), f32), z, z, z, z)
        carry = lax.fori_loop(0, nc - 1, lambda k, cr: chunk(nc - 1 - k, cr), carry)
        _, s0, s1, s2, sb = chunk(0, carry)
        rows = lax.broadcasted_iota(jnp.int32, (3, LANES), 0)
        dw_ref[...] = jnp.where(rows == 0, s0, jnp.where(rows == 1, s1, s2))
        db_ref[...] = sb

    col = lambda off: pl.BlockSpec((T, LANES), lambda j: (0, off + j))
    return pl.pallas_call(
        body, name=name, grid=(nb,),
        in_specs=[col(0), col(nb), pl.BlockSpec((3, LANES), lambda j: (0, j)), pl.BlockSpec((1, LANES), lambda j: (0, j)), col(0)],
        out_specs=[pl.BlockSpec((2, T, LANES), lambda j: (0, 0, j)), pl.BlockSpec((3, LANES), lambda j: (0, j)),
                   pl.BlockSpec((1, LANES), lambda j: (0, j))],
        out_shape=[jax.ShapeDtypeStruct((2, T, F), bf16), jax.ShapeDtypeStruct((3, F), f32), jax.ShapeDtypeStruct((1, F), f32)],
        compiler_params=_cp("parallel"),
    )(au, au, cw, cb, dz)


BNN = (((2,), (1,)), ((0,), (0,)))
BNT = (((2,), (2,)), ((0,), (0,)))
BTN = (((1,), (1,)), ((0,), (0,)))


def _heads(x):
    return jnp.stack([x[:, h * LANES:(h + 1) * LANES] for h in range(HGRN_HEADS)])


def _put_heads(ref, rows, x, dtype):
    for h in range(HGRN_HEADS):
        ref[rows, h * LANES:(h + 1) * LANES] = x[h].astype(dtype)


def _hgrn_lb(l):
    m = jnp.max(l, axis=0, keepdims=True)
    e = jnp.exp(l - m)
    return e[0:1] / jnp.sum(e, axis=0, keepdims=True)


def _hgrn_chunk(q, fx, lb):
    H, C = q.shape[0], q.shape[1]
    sg = jax.nn.sigmoid(fx)
    F = lb + (1.0 - lb) * sg
    k = 1.0 - F
    logF = jnp.log(F)
    r = lax.broadcasted_iota(jnp.int32, (H, C, C), 1)
    c = lax.broadcasted_iota(jnp.int32, (H, C, C), 2)
    tril = (r >= c)
    b = _dot(tril.astype(f32), logF, BNN, precision=HI)
    bl = jnp.sum(logF, axis=1, keepdims=True)
    eb = jnp.exp(b)
    enb = jnp.exp(-b)
    elb = jnp.exp(bl - b)
    return dict(sg=sg, F=F, k=k, b=b, bl=bl, eb=eb, enb=enb, elb=elb, qd=q * eb, kd=k * enb, kl=k * elb, tril=tril)


def _hgrn_fwd(proj, lbl, ng, *, rb=512):
    T = proj.shape[0]
    rb = min(rb, T)
    cpb = rb // HGRN_CHUNK
    nblk = T // rb
    H = HGRN_HEADS

    def body(q_ref, f_ref, i_ref, g_ref, lbl_ref, ng_ref, y_ref, st_ref, S):
        @pl.when(pl.program_id(0) == 0)
        def _():
            S[...] = jnp.zeros_like(S)

        lb = _heads(_hgrn_lb(lbl_ref[...]))
        ngv = _heads(ng_ref[...])
        for c in range(cpb):
            sl = pl.ds(c * HGRN_CHUNK, HGRN_CHUNK)
            v, gx = _heads(i_ref[sl, :]), _heads(g_ref[sl, :])
            ch = _hgrn_chunk(_heads(q_ref[sl, :]), _heads(f_ref[sl, :]), lb)
            att = jnp.where(ch["tril"], _bdot(ch["qd"], ch["kd"], BNT), 0.0)
            St = S[...]
            st_ref[:, c] = St
            o = _bdot(att, v, BNN) + _bdot(ch["qd"], St, BNT)
            S[...] = St * jnp.exp(ch["bl"]) + _bdot(v, ch["kl"], BTN)
            r = lax.rsqrt(jnp.mean(o * o, axis=-1, keepdims=True) + EPS)
            _put_heads(y_ref, sl, o * r * ngv * (gx * jax.nn.sigmoid(gx)), bf16)

    col = lambda off: pl.BlockSpec((rb, H * LANES), lambda n: (n, off))
    return pl.pallas_call(
        body, name="hgrn_fwd", grid=(nblk,),
        in_specs=[col(0), col(1), col(2), col(3), pl.BlockSpec((2, H * LANES), lambda n: (0, 0)),
                  pl.BlockSpec((1, H * LANES), lambda n: (0, 0))],
        out_specs=[pl.BlockSpec((rb, H * LANES), lambda n: (n, 0)),
                   pl.BlockSpec((H, cpb, LANES, LANES), lambda n: (0, n, 0, 0))],
        out_shape=[jax.ShapeDtypeStruct((T, H * LANES), bf16),
                   jax.ShapeDtypeStruct((H, T // HGRN_CHUNK, LANES, LANES), f32)],
        scratch_shapes=[pltpu.VMEM((H, LANES, LANES), f32)], compiler_params=_cp("arbitrary"),
    )(proj, proj, proj, proj, lbl, ng)


def _hgrn_bwd(proj, lbl, ng, states, dy, *, rb=512):
    T = proj.shape[0]
    rb = min(rb, T)
    cpb = rb // HGRN_CHUNK
    nblk = T // rb
    H = HGRN_HEADS
    C = HGRN_CHUNK

    def body(q_ref, f_ref, i_ref, g_ref, lbl_ref, ng_ref, st_ref, dy_ref,
             dq_ref, df_ref, di_ref, dg_ref, dl_ref, dng_ref, dS, dlb_acc, dng_acc):
        n = pl.program_id(0)

        @pl.when(n == 0)
        def _():
            dS[...] = jnp.zeros_like(dS)
            dlb_acc[...] = jnp.zeros_like(dlb_acc)
            dng_acc[...] = jnp.zeros_like(dng_acc)

        lb_row = _hgrn_lb(lbl_ref[...])
        lb = _heads(lb_row)
        ngv = _heads(ng_ref[...])
        r_i = lax.broadcasted_iota(jnp.int32, (H, C, C), 1)
        c_i = lax.broadcasted_iota(jnp.int32, (H, C, C), 2)
        triu = (c_i >= r_i).astype(f32)
        rows_sum = lambda x: jnp.sum(x, axis=1, keepdims=True)
        for c in reversed(range(cpb)):
            sl = pl.ds(c * C, C)
            q, v, gx = _heads(q_ref[sl, :]), _heads(i_ref[sl, :]), _heads(g_ref[sl, :])
            ch = _hgrn_chunk(q, _heads(f_ref[sl, :]), lb)
            qd, kd, kl = ch["qd"], ch["kd"], ch["kl"]
            att = jnp.where(ch["tril"], _bdot(qd, kd, BNT), 0.0)
            St = st_ref[:, c]
            o = _bdot(att, v, BNN) + _bdot(qd, St, BNT)
            r = lax.rsqrt(jnp.mean(o * o, axis=-1, keepdims=True) + EPS)
            on = o * r
            sgg = jax.nn.sigmoid(gx)
            gate = gx * sgg
            dyv = _heads(dy_ref[sl, :].astype(f32))
            _put_heads(dg_ref, sl, dyv * on * ngv * sgg * (1.0 + gx * (1.0 - sgg)), bf16)
            dng_acc[...] += rows_sum(dyv * on * gate)
            don = dyv * ngv * gate
            do = r * (don - on * jnp.mean(don * on, axis=-1, keepdims=True))
            dSt = dS[...]
            dA = jnp.where(ch["tril"], _bdot(do, v, BNT), 0.0)
            dv = _bdot(att, do, BTN) + _bdot(kl, dSt, BNT)
            dqd = _bdot(dA, kd, BNN) + _bdot(do, St, BNN)
            dkd = _bdot(dA, qd, BTN)
            dkl = _bdot(v, dSt, BNN)
            dec = jnp.exp(ch["bl"])
            ddec = rows_sum(St * dSt)
            dS[...] = _bdot(do, qd, BTN) + dSt * dec
            dB = dqd * qd - dkd * kd - dkl * kl
            dbl = rows_sum(dkl * kl) + ddec * dec
            dk = dkd * ch["enb"] + dkl * ch["elb"]
            dlogF = _dot(triu, dB, BNN, precision=HI) + dbl
            dF = dlogF / ch["F"] - dk
            sg = ch["sg"]
            _put_heads(dq_ref, sl, dqd * ch["eb"], bf16)
            _put_heads(di_ref, sl, dv, bf16)
            _put_heads(df_ref, sl, dF * (1.0 - lb) * sg * (1.0 - sg), bf16)
            dlb_acc[...] += rows_sum(dF * (1.0 - sg))

        @pl.when(n == nblk - 1)
        def _():
            rows = lax.broadcasted_iota(jnp.int32, (2, LANES), 0)
            for h in range(H):
                hs = pl.ds(h * LANES, LANES)
                lbh = lb_row[:, h * LANES:(h + 1) * LANES]
                dl0 = dlb_acc[h] * lbh * (1.0 - lbh)
                dl_ref[:, hs] = jnp.where(rows == 0, dl0, -dl0)
                dng_ref[:, hs] = dng_acc[h]

    col = lambda off: pl.BlockSpec((rb, H * LANES), lambda n: (nblk - 1 - n, off))
    vec = lambda rows: pl.BlockSpec((rows, H * LANES), lambda n: (0, 0))
    tok = jax.ShapeDtypeStruct((T, H * LANES), bf16)
    return pl.pallas_call(
        body, name="hgrn_bwd", grid=(nblk,),
        in_specs=[col(0), col(1), col(2), col(3), vec(2), vec(1),
                  pl.BlockSpec((H, cpb, LANES, LANES), lambda n: (0, nblk - 1 - n, 0, 0)), col(0)],
        out_specs=[col(0), col(0), col(0), col(0), vec(2), vec(1)],
        out_shape=[tok, tok, tok, tok, jax.ShapeDtypeStruct((2, H * LANES), f32), jax.ShapeDtypeStruct((1, H * LANES), f32)],
        scratch_shapes=[pltpu.VMEM((H, LANES, LANES), f32), pltpu.VMEM((H, 1, LANES), f32), pltpu.VMEM((H, 1, LANES), f32)],
        compiler_params=_cp("arbitrary"),
    )(proj, proj, proj, proj, lbl, ng, states, dy)


def _s5_disc_math(ar, ai, ldt, br, bi):
    dt = jnp.exp(ldt)
    mag = jnp.exp(ar * dt)
    abr, abi = mag * jnp.cos(ai * dt), mag * jnp.sin(ai * dt)
    den = ar * ar + ai * ai
    xr, xi = abr - 1.0, abi
    cr = (xr * ar + xi * ai) / den
    ci = (xi * ar - xr * ai) / den
    return abr, abi, cr * br - ci * bi, cr * bi + ci * br


def _s5_disc_fwd(ar, ai, ldt, br, bi):
    def body(ar_ref, ai_ref, ldt_ref, br_ref, bi_ref, o0, o1, o2, o3):
        outs = _s5_disc_math(ar_ref[...], ai_ref[...], ldt_ref[...], br_ref[...], bi_ref[...])
        for o, v in zip((o0, o1, o2, o3), outs):
            o[...] = v

    return pl.pallas_call(
        body, name="s5_disc_fwd",
        out_shape=[jax.ShapeDtypeStruct(ar.shape, f32)] * 2 + [jax.ShapeDtypeStruct(br.shape, f32)] * 2,
    )(ar, ai, ldt, br, bi)


def _s5_disc_bwd(ar, ai, ldt, br, bi, cts):
    def body(ar_ref, ai_ref, ldt_ref, br_ref, bi_ref, c0, c1, c2, c3, o0, o1, o2, o3, o4):
        _, vjp = jax.vjp(_s5_disc_math, ar_ref[...], ai_ref[...], ldt_ref[...], br_ref[...], bi_ref[...])
        for o, v in zip((o0, o1, o2, o3, o4), vjp((c0[...], c1[...], c2[...], c3[...]))):
            o[...] = v

    return pl.pallas_call(
        body, name="s5_disc_bwd",
        out_shape=[jax.ShapeDtypeStruct(ar.shape, f32)] * 3 + [jax.ShapeDtypeStruct(br.shape, f32)] * 2,
    )(ar, ai, ldt, br, bi, *cts)


S5_LC = 512
S5_NLC = S5_N // S5_LC
S5_UB = 4


def _cmul(ar, ai, xr, xi):
    return ar * xr - ai * xi, ar * xi + ai * xr


def _cpow(ar, ai, n):
    rr, ri = None, None
    br, bi = ar, ai
    while n:
        if n & 1:
            rr, ri = (br, bi) if rr is None else _cmul(rr, ri, br, bi)
        n >>= 1
        if n:
            br, bi = _cmul(br, bi, br, bi)
    return rr, ri


def _s5_bu(u_ref, bre_ref, bim_ref, xr, xi):
    for k in range(S5_UB):
        uk = u_ref[:, k * LANES:(k + 1) * LANES].astype(bf16)
        xr[:, k * S5_LC:(k + 1) * S5_LC] = _dot(uk, bre_ref[k])
        xi[:, k * S5_LC:(k + 1) * S5_LC] = _dot(uk, bim_ref[k])


def _s5_scan(xr, xi, sr, si, ar_ref, ai_ref, nsteps, store):
    for c in range(S5_NLC):
        cs = slice(c * S5_LC, (c + 1) * S5_LC)
        a_r = jnp.broadcast_to(ar_ref[:, cs], (S5_SEG, S5_LC))
        a_i = jnp.broadcast_to(ai_ref[:, cs], (S5_SEG, S5_LC))

        def step(j, carry, cs=cs, a_r=a_r, a_i=a_i):
            pr, pi = carry
            rows = pl.ds(pl.multiple_of(j * S5_SEG, S5_SEG), S5_SEG)
            nr = a_r * pr - a_i * pi + xr[rows, cs]
            ni = a_r * pi + a_i * pr + xi[rows, cs]
            if store:
                xr[rows, cs] = nr
                xi[rows, cs] = ni
            return nr, ni

        fr, fi = lax.fori_loop(0, nsteps, step, (sr[:, cs], si[:, cs]))
        sr[:, cs] = fr
        si[:, cs] = fi


def _s5_rscan(dr, di, xr, xi, s0r, s0i, gr, gi, acc_r, acc_i, ar_ref, ai_ref, nsteps):
    for c in range(S5_NLC):
        cs = slice(c * S5_LC, (c + 1) * S5_LC)
        a_r = jnp.broadcast_to(ar_ref[:, cs], (S5_SEG, S5_LC))
        a_i = jnp.broadcast_to(ai_ref[:, cs], (S5_SEG, S5_LC))

        def step(jj, carry, cs=cs, a_r=a_r, a_i=a_i):
            pr, pi, cr, ci = carry
            j = nsteps - 1 - jj
            rows = pl.ds(pl.multiple_of(j * S5_SEG, S5_SEG), S5_SEG)
            nr = dr[rows, cs] + a_r * pr + a_i * pi
            ni = di[rows, cs] + a_r * pi - a_i * pr
            dr[rows, cs] = nr
            di[rows, cs] = ni
            if acc_r is not None:
                prev = pl.ds(pl.multiple_of(jnp.maximum(j - 1, 0) * S5_SEG, S5_SEG), S5_SEG)
                first = j == 0
                pr_s = jnp.where(first, s0r[:, cs], xr[prev, cs])
                pi_s = jnp.where(first, s0i[:, cs], xi[prev, cs])
                cr = cr + nr * pr_s + ni * pi_s
                ci = ci - nr * pi_s + ni * pr_s
            return nr, ni, cr, ci

        z = jnp.zeros((S5_SEG, S5_LC), f32)
        init = (gr[:, cs], gi[:, cs], z, z)
        fr, fi, cr, ci = lax.fori_loop(0, nsteps, step, init)
        gr[:, cs] = fr
        gi[:, cs] = fi
        if acc_r is not None:
            acc_r[:, cs] += cr
            acc_i[:, cs] += ci


def _s5_seg_carry(fr, fi, ar, ai, seg_len, reverse):
    pr, pi = _cpow(ar, ai if not reverse else -ai, seg_len)
    rows = lax.broadcasted_iota(jnp.int32, fr.shape, 0)
    cr, ci = jnp.zeros_like(fr), jnp.zeros_like(fi)
    sh = (S5_SEG - 1) if reverse else 1
    fr_s, fi_s = pltpu.roll(fr, sh, 0), pltpu.roll(fi, sh, 0)
    order = range(S5_SEG - 2, -1, -1) if reverse else range(1, S5_SEG)
    for r in order:
        c_r, c_i = pltpu.roll(cr, sh, 0), pltpu.roll(ci, sh, 0)
        m_r, m_i = _cmul(pr, pi, c_r, c_i)
        cr = jnp.where(rows == r, m_r + fr_s, cr)
        ci = jnp.where(rows == r, m_i + fi_s, ci)
    return cr, ci


def _gelu_parts(y):
    c0 = math.sqrt(2.0 / math.pi)
    t = jnp.tanh(c0 * (y + 0.044715 * y * y * y))
    z = 0.5 * y * (1.0 + t)
    dz = 0.5 * (1.0 + t) + 0.5 * y * (1.0 - t * t) * c0 * (1.0 + 3.0 * 0.044715 * y * y)
    return z, dz


def _s5_y(xr, xi, u_ref, cre_ref, cim_ref, d_ref):
    ys = []
    for k in range(S5_UB):
        cs = slice(k * S5_LC, (k + 1) * S5_LC)
        ys.append(_bdot(xr[:, cs], cre_ref[k]) - _bdot(xi[:, cs], cim_ref[k]))
    return jnp.concatenate(ys, axis=1) + d_ref[...] * u_ref[...]


def _s5_specs(T, rb, rev=False):
    nblk = T // rb
    blk = (lambda i: (nblk - 1 - i, 0)) if rev else (lambda i: (i, 0))
    tok = pl.BlockSpec((rb, 4 * LANES), blk)
    bmat = pl.BlockSpec((S5_UB, LANES, S5_LC), lambda i: (0, 0, 0))
    cmat = pl.BlockSpec((S5_UB, S5_LC, LANES), lambda i: (0, 0, 0))
    avec = pl.BlockSpec((1, S5_N), lambda i: (0, 0))
    seg = pl.BlockSpec((S5_SEG, S5_N), lambda i: (0, 0))
    cvec = pl.BlockSpec((1, 4 * LANES), lambda i: (0, 0))
    s0 = pl.BlockSpec((1, S5_SEG, S5_N), (lambda i: (nblk - 1 - i, 0, 0)) if rev else (lambda i: (i, 0, 0)))
    return dict(tok=tok, bmat=bmat, cmat=cmat, avec=avec, seg=seg, cvec=cvec, s0=s0, nblk=nblk)


def _s5_final(u, bre, bim, ar, ai, *, rb):
    T = u.shape[0]
    sp = _s5_specs(T, rb)

    def body(u_ref, bre_ref, bim_ref, ar_ref, ai_ref, fr_ref, fi_ref, xr, xi):
        @pl.when(pl.program_id(0) == 0)
        def _():
            fr_ref[...] = jnp.zeros_like(fr_ref)
            fi_ref[...] = jnp.zeros_like(fi_ref)

        _s5_bu(u_ref, bre_ref, bim_ref, xr, xi)
        _s5_scan(xr, xi, fr_ref, fi_ref, ar_ref, ai_ref, rb // S5_SEG, False)

    return pl.pallas_call(
        body, name="s5_final", grid=(sp["nblk"],),
        in_specs=[sp["tok"], sp["bmat"], sp["bmat"], sp["avec"], sp["avec"]], out_specs=[sp["seg"], sp["seg"]],
        out_shape=[jax.ShapeDtypeStruct((S5_SEG, S5_N), f32)] * 2,
        scratch_shapes=[pltpu.VMEM((rb, S5_N), f32)] * 2, compiler_params=_cp("arbitrary"),
    )(u, bre, bim, ar, ai)


def _s5_fwd(u, bre, bim, ar, ai, fr, fi, cre, cim, dsk, wg, bg, *, rb):
    T = u.shape[0]
    sp = _s5_specs(T, rb)
    seg_len = T // S5_SEG

    def body(u_ref, bre_ref, bim_ref, ar_ref, ai_ref, fr_ref, fi_ref, cre_ref, cim_ref, d_ref, wg_ref, bg_ref,
             o_ref, s0r_ref, s0i_ref, xr, xi, sr, si):
        @pl.when(pl.program_id(0) == 0)
        def _():
            i_r, i_i = _s5_seg_carry(fr_ref[...], fi_ref[...], ar_ref[...], ai_ref[...], seg_len, False)
            sr[...] = i_r
            si[...] = i_i

        s0r_ref[0] = sr[...]
        s0i_ref[0] = si[...]
        _s5_bu(u_ref, bre_ref, bim_ref, xr, xi)
        _s5_scan(xr, xi, sr, si, ar_ref, ai_ref, rb // S5_SEG, True)
        y = _s5_y(xr, xi, u_ref, cre_ref, cim_ref, d_ref)
        z, _ = _gelu_parts(y)
        v = _bdot(z, wg_ref[...]) + bg_ref[...]
        o_ref[...] = (z * jax.nn.sigmoid(v)).astype(bf16)

    wspec = pl.BlockSpec((4 * LANES, 4 * LANES), lambda i: (0, 0))
    return pl.pallas_call(
        body, name="s5_fwd", grid=(sp["nblk"],),
        in_specs=[sp["tok"], sp["bmat"], sp["bmat"], sp["avec"], sp["avec"], sp["seg"], sp["seg"], sp["cmat"], sp["cmat"],
                  sp["cvec"], wspec, sp["cvec"]],
        out_specs=[sp["tok"], sp["s0"], sp["s0"]],
        out_shape=[jax.ShapeDtypeStruct((T, 4 * LANES), bf16)] + [jax.ShapeDtypeStruct((sp["nblk"], S5_SEG, S5_N), f32)] * 2,
        scratch_shapes=[pltpu.VMEM((rb, S5_N), f32)] * 2 + [pltpu.VMEM((S5_SEG, S5_N), f32)] * 2,
        compiler_params=_cp("arbitrary"),
    )(u, bre, bim, ar, ai, fr, fi, cre, cim, dsk, wg, bg)


def _s5_bwd_a(u, bre, bim, ar, ai, s0r, s0i, cre, cim, cret, cimt, dsk, wg, bg, dout, *, rb):
    T = u.shape[0]
    sp = _s5_specs(T, rb, rev=True)

    def body(u_ref, bre_ref, bim_ref, ar_ref, ai_ref, s0r_ref, s0i_ref, cre_ref, cim_ref, cret_ref, cimt_ref,
             d_ref, wg_ref, bg_ref, do_ref, dy_ref, glr_ref, gli_ref, dcre_ref, dcim_ref, dd_ref, dwg_ref, dbg_ref,
             xr, xi, dr, di, sr, si):
        @pl.when(pl.program_id(0) == 0)
        def _():
            for r in (glr_ref, gli_ref, dcre_ref, dcim_ref, dd_ref, dwg_ref, dbg_ref):
                r[...] = jnp.zeros_like(r)

        sr[...] = s0r_ref[0]
        si[...] = s0i_ref[0]
        _s5_bu(u_ref, bre_ref, bim_ref, xr, xi)
        _s5_scan(xr, xi, sr, si, ar_ref, ai_ref, rb // S5_SEG, True)
        uv = u_ref[...]
        y = _s5_y(xr, xi, u_ref, cre_ref, cim_ref, d_ref)
        z, gz = _gelu_parts(y)
        v = _bdot(z, wg_ref[...]) + bg_ref[...]
        sg = jax.nn.sigmoid(v)
        dov = do_ref[...].astype(f32)
        dv = dov * z * sg * (1.0 - sg)
        dz = dov * sg + _bdot(dv, wg_ref[...], NT)
        dy = dz * gz
        dy_ref[...] = dy
        dwg_ref[...] += _bdot(z, dv, TN)
        dbg_ref[...] += jnp.sum(dv, axis=0, keepdims=True)
        dd_ref[...] += jnp.sum(dy * uv, axis=0, keepdims=True)
        for k in range(S5_UB):
            cs = slice(k * S5_LC, (k + 1) * S5_LC)
            dyk = dy[:, k * LANES:(k + 1) * LANES]
            dcre_ref[k] += _bdot(xr[:, cs], dyk, TN)
            dcim_ref[k] -= _bdot(xi[:, cs], dyk, TN)
            dr[:, cs] = _bdot(dyk, cret_ref[k])
            di[:, cs] = -_bdot(dyk, cimt_ref[k])
        _s5_rscan(dr, di, None, None, None, None, glr_ref, gli_ref, None, None, ar_ref, ai_ref, rb // S5_SEG)

    wspec = pl.BlockSpec((4 * LANES, 4 * LANES), lambda i: (0, 0))
    return pl.pallas_call(
        body, name="s5_bwd_a", grid=(sp["nblk"],),
        in_specs=[sp["tok"], sp["bmat"], sp["bmat"], sp["avec"], sp["avec"], sp["s0"], sp["s0"], sp["cmat"], sp["cmat"],
                  sp["bmat"], sp["bmat"], sp["cvec"], wspec, sp["cvec"], sp["tok"]],
        out_specs=[sp["tok"], sp["seg"], sp["seg"], sp["cmat"], sp["cmat"], sp["cvec"], wspec, sp["cvec"]],
        out_shape=[jax.ShapeDtypeStruct((T, 4 * LANES), f32)] + [jax.ShapeDtypeStruct((S5_SEG, S5_N), f32)] * 2
        + [jax.ShapeDtypeStruct((S5_UB, S5_LC, LANES), f32)] * 2
        + [jax.ShapeDtypeStruct((1, 4 * LANES), f32), jax.ShapeDtypeStruct((4 * LANES, 4 * LANES), f32),
           jax.ShapeDtypeStruct((1, 4 * LANES), f32)],
        scratch_shapes=[pltpu.VMEM((rb, S5_N), f32)] * 4 + [pltpu.VMEM((S5_SEG, S5_N), f32)] * 2,
        compiler_params=_cp("arbitrary"),
    )(u, bre, bim, ar, ai, s0r, s0i, cre, cim, cret, cimt, dsk, wg, bg, dout)


def _s5_bwd_b(u, bre, bim, bret, bimt, ar, ai, s0r, s0i, glr, gli, cret, cimt, dsk, dy, *, rb):
    T = u.shape[0]
    sp = _s5_specs(T, rb, rev=True)
    seg_len = T // S5_SEG
    nblk = sp["nblk"]

    def body(u_ref, bre_ref, bim_ref, bret_ref, bimt_ref, ar_ref, ai_ref, s0r_ref, s0i_ref, glr_ref, gli_ref,
             cret_ref, cimt_ref, d_ref, dy_ref, du_ref, dbre_ref, dbim_ref, dar_ref, dai_ref,
             xr, xi, dr, di, sr, si, gr, gi, acc_r, acc_i):
        @pl.when(pl.program_id(0) == 0)
        def _():
            x_r, x_i = _s5_seg_carry(glr_ref[...], gli_ref[...], ar_ref[...], ai_ref[...], seg_len, True)
            gr[...] = x_r
            gi[...] = x_i
            acc_r[...] = jnp.zeros_like(acc_r)
            acc_i[...] = jnp.zeros_like(acc_i)
            dbre_ref[...] = jnp.zeros_like(dbre_ref)
            dbim_ref[...] = jnp.zeros_like(dbim_ref)

        sr[...] = s0r_ref[0]
        si[...] = s0i_ref[0]
        _s5_bu(u_ref, bre_ref, bim_ref, xr, xi)
        _s5_scan(xr, xi, sr, si, ar_ref, ai_ref, rb // S5_SEG, True)
        dy = dy_ref[...]
        for k in range(S5_UB):
            cs = slice(k * S5_LC, (k + 1) * S5_LC)
            dyk = dy[:, k * LANES:(k + 1) * LANES]
            dr[:, cs] = _bdot(dyk, cret_ref[k])
            di[:, cs] = -_bdot(dyk, cimt_ref[k])
        sr[...] = s0r_ref[0]
        si[...] = s0i_ref[0]
        _s5_rscan(dr, di, xr, xi, sr, si, gr, gi, acc_r, acc_i, ar_ref, ai_ref, rb // S5_SEG)
        dus = []
        for k in range(S5_UB):
            cs = slice(k * S5_LC, (k + 1) * S5_LC)
            uk = u_ref[:, k * LANES:(k + 1) * LANES]
            dbre_ref[k] += _bdot(uk, dr[:, cs], TN)
            dbim_ref[k] += _bdot(uk, di[:, cs], TN)
            dus.append(_bdot(dr[:, cs], bret_ref[k]) + _bdot(di[:, cs], bimt_ref[k]))
        du_ref[...] = (jnp.concatenate(dus, axis=1) + d_ref[...] * dy).astype(bf16)

        @pl.when(pl.program_id(0) == nblk - 1)
        def _():
            dar_ref[...] = jnp.sum(acc_r[...], axis=0, keepdims=True)
            dai_ref[...] = jnp.sum(acc_i[...], axis=0, keepdims=True)

    return pl.pallas_call(
        body, name="s5_bwd_b", grid=(nblk,),
        in_specs=[sp["tok"], sp["bmat"], sp["bmat"], sp["cmat"], sp["cmat"], sp["avec"], sp["avec"], sp["s0"], sp["s0"],
                  sp["seg"], sp["seg"], sp["bmat"], sp["bmat"], sp["cvec"], sp["tok"]],
        out_specs=[sp["tok"], sp["bmat"], sp["bmat"], sp["avec"], sp["avec"]],
        out_shape=[jax.ShapeDtypeStruct((T, 4 * LANES), bf16)] + [jax.ShapeDtypeStruct((S5_UB, LANES, S5_LC), f32)] * 2
        + [jax.ShapeDtypeStruct((1, S5_N), f32)] * 2,
        scratch_shapes=[pltpu.VMEM((rb, S5_N), f32)] * 4 + [pltpu.VMEM((S5_SEG, S5_N), f32)] * 6,
        compiler_params=_cp("arbitrary"),
    )(u, bre, bim, bret, bimt, ar, ai, s0r, s0i, glr, gli, cret, cimt, dsk, dy)


def _blockdiag(w, transpose=False):
    if transpose:
        w = jnp.swapaxes(w, 1, 2)
    g, a, b = w.shape
    eye = jnp.eye(8, dtype=w.dtype)
    return jnp.einsum("kgab,gj->kgajb", w.reshape(4, 8, a, b), eye).reshape(4, 8 * a, 8 * b)


def _blockdiag_t(m, a, b):
    eye = jnp.eye(8, dtype=m.dtype)
    return jnp.einsum("kgajb,gj->kgab", m.reshape(4, 8, a, 8, b), eye).reshape(32, a, b)


ROT = MLA_ROPE // 2


def _rope_tables(positions):
    freqs = ROPE_THETA ** (-jnp.arange(0, MLA_ROPE, 2, dtype=f32) / MLA_ROPE)
    ang = positions.astype(f32)[:, None] * freqs
    cos, sin, z = jnp.cos(ang), jnp.sin(ang), jnp.zeros_like(ang)
    return (jnp.concatenate([cos, cos, z, z], axis=1), jnp.concatenate([-sin, z, z, z], axis=1),
            jnp.concatenate([z, sin, z, z], axis=1))


def _rot(x, c, sa, sb):
    return x * c + pltpu.roll(x, LANES - ROT, 1) * sa + pltpu.roll(x, ROT, 1) * sb


def _rot_t(dy, c, sa, sb):
    return dy * c + pltpu.roll(dy * sa, ROT, 1) + pltpu.roll(dy * sb, LANES - ROT, 1)


def _rms(xv, g):
    return xv * lax.rsqrt(jnp.mean(xv * xv, axis=-1, keepdims=True) + EPS) * g


QW, KVW = MLA_Q_RANK, MLA_KV_RANK
ODD_PAD = QW + KVW + LANES


def _mla_prep_fwd(proj, qg, kvg, tabs, *, tm=512):
    T = proj.shape[0]
    tm = _tile(T, tm)

    def body(p_ref, qg_ref, kvg_ref, c_ref, sa_ref, sb_ref, cq_ref, ckv_ref, kr_ref):
        cq_ref[...] = _rms(p_ref[:, :QW], qg_ref[...]).astype(bf16)
        ckv_ref[...] = _rms(p_ref[:, QW:QW + KVW], kvg_ref[...]).astype(bf16)
        kr_ref[...] = _rot(p_ref[:, QW + KVW:], c_ref[...], sa_ref[...], sb_ref[...]).astype(bf16)

    row = lambda w: pl.BlockSpec((tm, w), lambda i: (i, 0))
    vec = lambda w: pl.BlockSpec((1, w), lambda i: (0, 0))
    return pl.pallas_call(
        body, name="mla_prep_fwd", grid=(T // tm,),
        in_specs=[row(ODD_PAD), vec(QW), vec(KVW), row(LANES), row(LANES), row(LANES)],
        out_specs=[row(QW), row(KVW), row(LANES)],
        out_shape=[jax.ShapeDtypeStruct((T, QW), bf16), jax.ShapeDtypeStruct((T, KVW), bf16),
                   jax.ShapeDtypeStruct((T, LANES), bf16)],
        compiler_params=_cp("parallel"),
    )(proj, qg, kvg, *tabs)


def _mla_prep_bwd(proj, qg, kvg, tabs, dcqn, dckvn, dkr_heads, *, tm=512):
    T = proj.shape[0]
    tm = _tile(T, tm)

    def body(p_ref, qg_ref, kvg_ref, c_ref, sa_ref, sb_ref, dcq_ref, dckv_ref, dkr_ref, dp_ref, dqg_ref, dkvg_ref):
        dcq, dqg = _rms_bwd_math(p_ref[:, :QW], qg_ref[...], dcq_ref[...])
        dckv, dkvg = _rms_bwd_math(p_ref[:, QW:QW + KVW], kvg_ref[...], dckv_ref[...])
        dk = dkr_ref[:, :LANES]
        for h in range(1, MLA_HEADS):
            dk = dk + dkr_ref[:, h * LANES:(h + 1) * LANES]
        dkr = _rot_t(dk, c_ref[...], sa_ref[...], sb_ref[...])
        dp_ref[...] = jnp.concatenate([dcq, dckv, dkr], axis=1).astype(bf16)

        @pl.when(pl.program_id(0) == 0)
        def _():
            dqg_ref[...] = dqg
            dkvg_ref[...] = dkvg

        @pl.when(pl.program_id(0) > 0)
        def _():
            dqg_ref[...] += dqg
            dkvg_ref[...] += dkvg

    row = lambda w: pl.BlockSpec((tm, w), lambda i: (i, 0))
    vec = lambda w: pl.BlockSpec((1, w), lambda i: (0, 0))
    return pl.pallas_call(
        body, name="mla_prep_bwd", grid=(T // tm,),
        in_specs=[row(ODD_PAD), vec(QW), vec(KVW), row(LANES), row(LANES), row(LANES), row(QW), row(KVW),
                  row(MLA_HEADS * LANES)],
        out_specs=[row(ODD_PAD), vec(QW), vec(KVW)],
        out_shape=[jax.ShapeDtypeStruct((T, ODD_PAD), bf16), jax.ShapeDtypeStruct((1, QW), f32),
                   jax.ShapeDtypeStruct((1, KVW), f32)],
        compiler_params=_cp("arbitrary"),
    )(proj, qg, kvg, *tabs, dcqn, dckvn, dkr_heads)


HQ = 2 * LANES
QK_SCALE = MLA_QK ** -0.5


def _q_post(q, tabs, *, transpose, name, tm=512):
    T = q.shape[0]
    tm = _tile(T, tm)

    def body(q_ref, c_ref, sa_ref, sb_ref, o_ref):
        c, sa, sb = c_ref[...], sa_ref[...], sb_ref[...]
        for h in range(MLA_HEADS):
            nope, rope = pl.ds(h * HQ, LANES), pl.ds(h * HQ + LANES, LANES)
            o_ref[:, nope] = (q_ref[:, nope].astype(f32) * QK_SCALE).astype(bf16)
            o_ref[:, rope] = ((_rot_t if transpose else _rot)(q_ref[:, rope].astype(f32), c, sa, sb) * QK_SCALE).astype(bf16)

    tab = pl.BlockSpec((tm, LANES), lambda i: (i, 0))
    blk = pl.BlockSpec((tm, MLA_HEADS * HQ), lambda i: (i, 0))
    return pl.pallas_call(
        body, name=name, grid=(T // tm,), in_specs=[blk, tab, tab, tab], out_specs=blk,
        out_shape=jax.ShapeDtypeStruct(q.shape, bf16), compiler_params=_cp("parallel"),
    )(q, *tabs)


def _causal_mask(i, j, tq, tk):
    r = lax.broadcasted_iota(jnp.int32, (tq, tk), 0) + i * tq
    c = lax.broadcasted_iota(jnp.int32, (tq, tk), 1) + j * tk
    return c <= r


def _flash_fwd(q, kv, kr, *, tq=1024, tk=512):
    T = q.shape[0]
    tq = _tile(T, tq)
    tk = _tile(tq, tk)
    per = tq // tk
    H = MLA_HEADS

    def body(q_ref, kn_ref, v_ref, kr_ref, o_ref, lse_ref, m_s, l_s, acc):
        i, j = pl.program_id(1), pl.program_id(2)
        last = (i + 1) * per - 1

        @pl.when(j == 0)
        def _():
            m_s[...] = jnp.full_like(m_s, -jnp.inf)
            l_s[...] = jnp.zeros_like(l_s)
            acc[...] = jnp.zeros_like(acc)

        def step(masked):
            k = jnp.concatenate([kn_ref[...], kr_ref[...]], axis=1)
            s = _dot(q_ref[...], k, NT)
            if masked:
                s = jnp.where(_causal_mask(i, j, tq, tk), s, -jnp.inf)
            m_new = jnp.maximum(m_s[...], jnp.max(s, axis=-1, keepdims=True))
            alpha = jnp.exp(m_s[...] - m_new)
            p = jnp.exp(s - m_new)
            l_s[...] = alpha * l_s[...] + jnp.sum(p, axis=-1, keepdims=True)
            acc[...] = alpha * acc[...] + _dot(p.astype(bf16), v_ref[...])
            m_s[...] = m_new

        pl.when(j < i * per)(functools.partial(step, False))
        pl.when((j >= i * per) & (j <= last))(functools.partial(step, True))

        @pl.when(j == last)
        def _():
            o_ref[...] = (acc[...] / l_s[...]).astype(bf16)
            lse_ref[0] = m_s[...] + jnp.log(l_s[...])

    kj = lambda i, j: jnp.minimum(j, (i + 1) * per - 1)
    kblk = lambda off: pl.BlockSpec((tk, LANES), lambda h, i, j: (kj(i, j), 2 * h + off))
    return pl.pallas_call(
        body, name="flash_fwd", grid=(H, T // tq, T // tk),
        in_specs=[pl.BlockSpec((tq, HQ), lambda h, i, j: (i, h)), kblk(0), kblk(1),
                  pl.BlockSpec((tk, LANES), lambda h, i, j: (kj(i, j), 0))],
        out_specs=[pl.BlockSpec((tq, LANES), lambda h, i, j: (i, h)), pl.BlockSpec((1, tq, 1), lambda h, i, j: (h, i, 0))],
        out_shape=[jax.ShapeDtypeStruct((T, H * LANES), bf16), jax.ShapeDtypeStruct((H, T, 1), f32)],
        scratch_shapes=[pltpu.VMEM((tq, 1), f32), pltpu.VMEM((tq, 1), f32), pltpu.VMEM((tq, LANES), f32)],
        compiler_params=_cp("parallel", "parallel", "arbitrary"),
    )(q, kv, kv, kr)


def _flash_bwd(q, kv, kr, o, do, lse, *, tb=512):
    T = q.shape[0]
    tb = _tile(T, tb)
    nb = T // tb
    H = MLA_HEADS

    def body(q_ref, kn_ref, v_ref, kr_ref, o_ref, do_ref, lse_ref, dkv_ref, dkr_ref, dq_ref, dk_acc, dv_acc):
        j, ii = pl.program_id(1), pl.program_id(2)
        i = jnp.maximum(ii, j)

        @pl.when((j == 0) & (ii == 0))
        def _():
            dq_ref[...] = jnp.zeros_like(dq_ref)

        @pl.when(ii == 0)
        def _():
            dk_acc[...] = jnp.zeros_like(dk_acc)
            dv_acc[...] = jnp.zeros_like(dv_acc)

        def step(masked):
            k = jnp.concatenate([kn_ref[...], kr_ref[...]], axis=1)
            p = jnp.exp(_dot(q_ref[...], k, NT) - lse_ref[0])
            if masked:
                p = jnp.where(_causal_mask(i, j, tb, tb), p, 0.0)
            delta = jnp.sum(o_ref[...].astype(f32) * do_ref[...], axis=-1, keepdims=True)
            ds = p * (_bdot(do_ref[...], v_ref[...], NT) - delta)
            dv_acc[...] += _bdot(p, do_ref[...], TN)
            dk_acc[...] += _bdot(ds, q_ref[...], TN)
            rows = pl.ds(pl.multiple_of(i * tb, tb), tb)
            dq_ref[rows, :] += _bdot(ds, k)

        pl.when(ii > j)(functools.partial(step, False))
        pl.when(ii == j)(functools.partial(step, True))

        @pl.when(ii == nb - 1)
        def _():
            dkv_ref[...] = jnp.concatenate([dk_acc[:, :LANES], dv_acc[...]], axis=1).astype(bf16)
            dkr_ref[...] = dk_acc[:, LANES:]

    qi = lambda h, j, i: jnp.maximum(i, j)
    kblk = lambda off: pl.BlockSpec((tb, LANES), lambda h, j, i: (j, 2 * h + off))
    vec = pl.BlockSpec((1, tb, 1), lambda h, j, i: (h, qi(h, j, i), 0))
    qblk = pl.BlockSpec((tb, LANES), lambda h, j, i: (qi(h, j, i), h))
    return pl.pallas_call(
        body, name="flash_bwd", grid=(H, nb, nb),
        in_specs=[pl.BlockSpec((tb, HQ), lambda h, j, i: (qi(h, j, i), h)), kblk(0), kblk(1),
                  pl.BlockSpec((tb, LANES), lambda h, j, i: (j, 0)), qblk, qblk, vec],
        out_specs=[pl.BlockSpec((tb, HQ), lambda h, j, i: (j, h)), pl.BlockSpec((tb, LANES), lambda h, j, i: (j, h)),
                   pl.BlockSpec((T, HQ), lambda h, j, i: (0, h))],
        out_shape=[jax.ShapeDtypeStruct((T, H * HQ), bf16), jax.ShapeDtypeStruct((T, H * LANES), f32),
                   jax.ShapeDtypeStruct((T, H * HQ), f32)],
        scratch_shapes=[pltpu.VMEM((tb, HQ), f32), pltpu.VMEM((tb, LANES), f32)],
        compiler_params=_cp("parallel", "arbitrary", "arbitrary"),
    )(q, kv, kv, kr, o, do, lse)


HBM_SPEC = pl.BlockSpec(memory_space=pltpu.HBM)
N_CHIPS = 4
N_DEV = 8

BIG = {"even_w_in": 1, "s5_w_glu": 0, "even_w_out": 0, "odd_w_in": 0, "mla_w_uq": 1, "mla_w_ukv": 1, "odd_w_out": 0,
       "ffn_w_in": 2, "ffn_w_out": 1}
LAYERED = ("ffn_w_in", "ffn_w_out")
GROUPS = {"even_in": ("even_w_in",), "even_rest": ("s5_w_glu", "even_w_out"), "ffn0": LAYERED,
          "odd": ("odd_w_in", "mla_w_uq", "mla_w_ukv", "odd_w_out"), "ffn1": LAYERED}
GROUP_LAYER = {"ffn0": 0, "ffn1": 1}


def _place():
    x, y, c = lax.axis_index("x"), lax.axis_index("y"), lax.axis_index("c")
    chips = [(1 - x, y), (x, 1 - y), (1 - x, 1 - y)]
    return x, y, c, chips


def _slab(ref, axis, k, size):
    start = pl.multiple_of(k * size, size if axis == 0 else LANES)
    idx = [slice(None)] * len(ref.shape)
    idx[axis] = pl.ds(start, size)
    return ref.at[tuple(idx)]


SEM_SPEC = pl.BlockSpec(memory_space=pltpu.SEMAPHORE)
ANY_SPEC = pl.BlockSpec(memory_space=pl.ANY)
EFFECT = pltpu.SideEffectType.DATAFLOW_SIDE_EFFECTING


def _hbm(a):
    return pltpu.with_memory_space_constraint(a, pltpu.HBM)


class _Gather:
    copies = 3

    def __init__(self, axis, size):
        self.axis, self.size = axis, size

    def view(self, land, kk):
        return _slab(land, self.axis, kk, self.size)

    def own(self, land, place):
        return self.view(land, 2 * place[0] + place[1])

    def sends(self, src, land, place):
        x, y, c, chips = place
        return [(self.own(land, place) if src is None else src, self.own(land, place), (*chip, c)) for chip in chips]

    def recvs(self, land, place):
        return [self.view(land, 2 * cx + cy) for cx, cy in place[3]]


class _Scatter:
    copies = 3

    def __init__(self, axis, size, layer=None):
        self.axis, self.size, self.layer = axis, size, layer

    def row(self, land, j):
        return land.at[j] if self.layer is None else land.at[j, self.layer]

    def sends(self, src, land, place):
        c, chips = place[2], place[3]
        return [(_slab(src, self.axis, 2 * cx + cy, self.size), self.row(land, j), (cx, cy, c))
                for j, (cx, cy) in enumerate(chips)]

    def recvs(self, land, place):
        return [self.row(land, j) for j in range(3)]


class _ToAll:
    copies = N_DEV - 1

    def __init__(self, size):
        self.size = size

    def sends(self, src, land, place):
        x, y, c, _ = place
        flip = lambda v, bit: 1 - v if bit else v
        own = _slab(land, 0, 4 * x + 2 * y + c, self.size)
        return [(own, own, (flip(x, m & 4), flip(y, m & 2), flip(c, m & 1))) for m in range(1, N_DEV)]

    def recvs(self, land, place):
        x, y, c, _ = place
        d = 4 * x + 2 * y + c
        return [_slab(land, 0, d ^ m, self.size) for m in range(1, N_DEV)]


def _unique(arrays):
    out, index = [], {}
    for a in arrays:
        if a is not None and id(a) not in index:
            index[id(a)] = len(out)
            out.append(a)
    return out, index


def _sem_base(routes):
    base = [0]
    for r in routes:
        base.append(base[-1] + r.copies)
    return base


def _push_start(name, items):
    n = len(items)
    base = _sem_base([it[0] for it in items])
    arrays, index = _unique([it[1] for it in items] + [it[2] for it in items])
    na = len(arrays)

    def body(*refs):
        arr, send, recv, token = refs[:na], refs[na], refs[na + 1], refs[-1]
        place = _place()
        for i, (route, src, land) in enumerate(items):
            s_ref = None if src is None else arr[index[id(src)]]
            for j, (s, d, dev) in enumerate(route.sends(s_ref, arr[index[id(land)]], place)):
                pltpu.make_async_remote_copy(src_ref=s, dst_ref=d, send_sem=send.at[base[i] + j], recv_sem=recv.at[base[i] + j],
                                             device_id=dev, device_id_type=MESH).start()
        token[...] = jnp.zeros_like(token)

    res = pl.pallas_call(
        body, name=name,
        out_shape=[pltpu.SemaphoreType.DMA((base[-1],)), pltpu.SemaphoreType.DMA((base[-1],))]
        + [pltpu.HBM(a.shape, a.dtype) for a in arrays] + [jax.ShapeDtypeStruct((SUBLANES, LANES), f32)],
        in_specs=[HBM_SPEC] * na, out_specs=[SEM_SPEC, SEM_SPEC] + [HBM_SPEC] * na + [pl.BlockSpec(memory_space=pltpu.VMEM)],
        input_output_aliases={i: 2 + i for i in range(na)},
        compiler_params=pltpu.CompilerParams(has_side_effects=EFFECT),
    )(*[_hbm(a) for a in arrays])
    thru = lambda a: None if a is None else res[2 + index[id(a)]]
    return (res[0], res[1]), [thru(it[1]) for it in items], [thru(it[2]) for it in items], res[-1]


def _push_wait(name, groups, after):
    arrays, index = _unique([a for _, _, srcs, lands in groups for a in list(srcs) + list(lands)])
    na, ng = len(arrays), len(groups)

    def body(*refs):
        arr, sems = refs[:na], refs[na:na + 2 * ng]
        place = _place()
        for g, (routes, _, srcs, lands) in enumerate(groups):
            send, recv = sems[2 * g], sems[2 * g + 1]
            base = _sem_base(routes)
            for i, route in enumerate(routes):
                src, land = None if srcs[i] is None else arr[index[id(srcs[i])]], arr[index[id(lands[i])]]
                for j, ((s, d, dev), mine) in enumerate(zip(route.sends(src, land, place), route.recvs(land, place))):
                    cp = pltpu.make_async_remote_copy(src_ref=s, dst_ref=mine, send_sem=send.at[base[i] + j],
                                                      recv_sem=recv.at[base[i] + j], device_id=dev,
                                                      device_id_type=MESH)
                    cp.wait_send()
                    cp.wait_recv()

    sem_args = [s for g in groups for s in g[1]]
    res = pl.pallas_call(
        body, name=name, out_shape=[pltpu.HBM(a.shape, a.dtype) for a in arrays],
        in_specs=[HBM_SPEC] * na + [SEM_SPEC] * (2 * ng) + [ANY_SPEC] * len(after), out_specs=[HBM_SPEC] * na,
        input_output_aliases={i: i for i in range(na)},
        compiler_params=pltpu.CompilerParams(has_side_effects=EFFECT),
    )(*arrays, *sem_args, *after)
    return [[res[index[id(a)]] for a in g[3]] for g in groups]


def _place_slab(block, axis, slabs, idx, dtype, *, name):
    R, C = block.shape
    tm = _rows(R, C)
    nr = R // tm
    out_map = (lambda i, k: (i, k[0])) if axis == 1 else (lambda i, k: (k[0] * nr + i, 0))

    def body(k_ref, x_ref, o_ref):
        o_ref[...] = x_ref[...].astype(dtype)

    full = (R, C * slabs) if axis == 1 else (R * slabs, C)
    return pl.pallas_call(
        body, name=name, out_shape=jax.ShapeDtypeStruct(full, dtype),
        grid_spec=pltpu.PrefetchScalarGridSpec(
            num_scalar_prefetch=1, grid=(nr,), in_specs=[pl.BlockSpec((tm, C), lambda i, k: (i, 0))],
            out_specs=pl.BlockSpec((tm, C), out_map)),
        compiler_params=_cp("parallel"),
    )(idx, block)


def _swap_with_sibling(parts, tag):
    names = list(parts)

    def body(*refs):
        n = len(names)
        ins, outs, send, recv = refs[:n], refs[n:2 * n], refs[-2], refs[-1]
        x, y, c, _ = _place()
        cps = [pltpu.make_async_remote_copy(src_ref=ins[a], dst_ref=outs[a], send_sem=send.at[a], recv_sem=recv.at[a],
                                            device_id=(x, y, 1 - c), device_id_type=MESH) for a in range(n)]
        for cp in cps:
            cp.start()
        for cp in cps:
            cp.wait_recv()
        for cp in cps:
            cp.wait_send()

    res = pl.pallas_call(
        body, name=f"swap_with_sibling_{tag}", in_specs=[HBM_SPEC] * len(names), out_specs=[HBM_SPEC] * len(names),
        out_shape=[jax.ShapeDtypeStruct(parts[n].shape, parts[n].dtype) for n in names],
        scratch_shapes=[pltpu.SemaphoreType.DMA((len(names),)), pltpu.SemaphoreType.DMA((len(names),))],
    )(*[parts[n] for n in names])
    return dict(zip(names, res))


ELEMENTWISE_BLOCK_BYTES = 1 << 20


def _rows(r, c):
    for t in (512, 256, 128, 64, 32, 16, 8):
        if r % t == 0 and t * c * 4 <= ELEMENTWISE_BLOCK_BYTES:
            return t
    return r


def _sum4(owns, axis, recv, kidx, *, name):
    L = len(owns)
    R, C = recv.shape[2:]
    tm = _rows(R, C)
    nr = R // tm

    def body(k_ref, *refs):
        own_refs, r_ref, out_ref = refs[:L], refs[L], refs[L + 1]
        for li in range(L):
            @pl.when(pl.program_id(0) == li)
            def _(o_ref=own_refs[li]):
                out_ref[...] = ((o_ref[...] + r_ref[0, 0].astype(f32)) + r_ref[1, 0].astype(f32)) + r_ref[2, 0].astype(f32)

    own_map = (lambda l, i, k: (i, k[0])) if axis == 1 else (lambda l, i, k: (k[0] * nr + i, 0))
    return pl.pallas_call(
        body, name=name, out_shape=jax.ShapeDtypeStruct((L * R, C), f32),
        grid_spec=pltpu.PrefetchScalarGridSpec(
            num_scalar_prefetch=1, grid=(L, nr),
            in_specs=[pl.BlockSpec((tm, C), own_map)] * L + [pl.BlockSpec((3, 1, tm, C), lambda l, i, k: (0, l, i, 0))],
            out_specs=pl.BlockSpec((tm, C), lambda l, i, k: (l * nr + i, 0))),
        compiler_params=_cp("parallel", "parallel"),
    )(kidx, *owns, recv)


def _adamw(w, m, v, parts, *, name):
    R, C = w.shape
    tm = _rows(R, C)
    npart = len(parts)

    def body(*refs):
        w_ref, m_ref, v_ref = refs[:3]
        g_ref, d_ref, m2_ref, v2_ref = refs[3 + npart:]
        g = refs[3][...]
        for p_ref in refs[4:3 + npart]:
            g = g + p_ref[...]
        g_ref[...] = g
        d_ref[...], m2_ref[...], v2_ref[...] = _adam_math(w_ref[...], m_ref[...], v_ref[...], g)

    blk = pl.BlockSpec((tm, C), lambda i: (i, 0))
    return pl.pallas_call(
        body, name=name, grid=(R // tm,),
        in_specs=[blk] * (3 + npart), out_specs=[blk] * 4,
        out_shape=[jax.ShapeDtypeStruct((R, C), f32)] * 4, compiler_params=_cp("parallel"),
    )(w, m, v, *parts)


def _adam_math(w, m, v, g):
    m2 = ADAM_B1 * m + (1.0 - ADAM_B1) * g
    v2 = ADAM_B2 * v + (1.0 - ADAM_B2) * (g * g)
    m_hat = m2 / (1.0 - ADAM_B1 ** ADAM_STEP)
    v_hat = v2 / (1.0 - ADAM_B2 ** ADAM_STEP)
    return -ADAM_LR * (m_hat / (jnp.sqrt(v_hat) + ADAM_EPS) + ADAM_WD * w), m2, v2


def _adamw_small(landed, w, m, v, kidx, ra, rb):
    rs = ra + N_CHIPS * rb

    def body(k_ref, l_ref, w_ref, m_ref, v_ref, g_ref, d_ref, m2_ref, v2_ref):
        mine = pl.multiple_of(ra + k_ref[0] * rb, SUBLANES)
        for lo, n, off in ((0, ra, 0), (ra, rb, mine)):
            g = l_ref[pl.ds(off, n), :]
            for d in range(1, N_DEV):
                g = g + l_ref[pl.ds(d * rs + off, n), :]
            rows = pl.ds(lo, n)
            delta, m2, v2 = _adam_math(w_ref[rows, :], m_ref[rows, :], v_ref[rows, :], g)
            g_ref[rows, :] = g
            d_ref[rows, :] = delta
            m2_ref[rows, :] = m2
            v2_ref[rows, :] = v2

    vmem = pl.BlockSpec(memory_space=pltpu.VMEM)
    return pl.pallas_call(
        body, name="adamw_small", out_shape=[jax.ShapeDtypeStruct(w.shape, f32)] * 4,
        grid_spec=pltpu.PrefetchScalarGridSpec(num_scalar_prefetch=1, grid=(), in_specs=[vmem] * 4, out_specs=[vmem] * 4),
        compiler_params=_cp(),
    )(kidx, landed, w, m, v)


def _pad_odd(w):
    return jnp.pad(w, ((0, 0), (0, ODD_PAD - w.shape[1])))


def _uq_cat(w):
    r = w.shape[0]
    return jnp.pad(w.reshape(r, MLA_HEADS, MLA_QK), ((0, 0), (0, 0), (0, HQ - MLA_QK))).reshape(r, MLA_HEADS * HQ)


def _uq_uncat(w):
    r = w.shape[0]
    return w.reshape(r, MLA_HEADS, HQ)[:, :, :MLA_QK].reshape(r, MLA_HEADS * MLA_QK)


def _to_segments(v):
    T, C = v.shape
    return v.reshape(S5_SEG, T // S5_SEG, C).transpose(1, 0, 2).reshape(T, C)


def _from_segments(v):
    T, C = v.shape
    return v.reshape(T // S5_SEG, S5_SEG, C).transpose(1, 0, 2).reshape(T, C)


def _s5_rb(T):
    return min(512, T)


def _ffn_fwd(h, g, w_in, cw, cb, w_out, tag):
    hn = _rms_fwd(h, g, name=f"ffn{tag}_norm")
    au = _mm(hn, w_in, name=f"ffn{tag}_in", tn=1408)
    z = _ffn_mid_fwd(au, cw, cb, name=f"ffn{tag}_mid")
    return _mm(z, w_out, res=h, name=f"ffn{tag}_out", tk=1408), (hn, au, z)


def _ffn_bwd(h, g, w_in, cw, cb, w_out, saved, dh, tag, dep=None):
    hn, au, z = saved
    dz = _mm(dh, w_out, tb=True, name=f"ffn{tag}_dz", tn=1408, dep=dep)
    dw_out = _mm(z, dh, ta=True, also_bf16=True, name=f"ffn{tag}_dwout", tm=1408)
    dau, dcw, dcb = _ffn_mid_bwd(au, cw, cb, dz, name=f"ffn{tag}_dmid")
    dhn = _mm(dau, w_in, tb=True, name=f"ffn{tag}_dhn", tk=1408)
    dw_in = _mm(hn, dau, ta=True, also_bf16=True, name=f"ffn{tag}_dwin", tn=1408)
    dh_in, dg = _rms_bwd(h, g, dhn, dh, name=f"ffn{tag}_dnorm")
    return dh_in, dg, dw_in, dcw, dcb, dw_out


def _local_step(x, positions, target, get_w, P, put_g):
    T = x.shape[0]
    rb = _s5_rb(T)
    row = lambda v: v.reshape(1, -1)
    g_mix, g_ffn = P["norm_mix_g"], P["norm_ffn_g"]
    lbl, hng = P["hgrn_lb_logits"], P["hgrn_norm_g"]
    dsk, bg = P["s5_d"], P["s5_b_glu"]
    qg, kvg = P["mla_q_norm_g"], P["mla_kv_norm_g"]
    cw, cb = P["ffn_conv_w"], P["ffn_conv_b"]

    col = lambda v: v.reshape(S5_N, 1)
    disc_in = (col(P["s5_a_re"]), col(P["s5_a_im"]), col(jnp.repeat(P["s5_log_dt"].reshape(S5_GROUPS), S5_STATE)),
               P["s5_b_re"].reshape(S5_N, S5_GROUP), P["s5_b_im"].reshape(S5_N, S5_GROUP))
    abr, abi, bbr, bbi = _s5_disc_fwd(*disc_in)
    ar, ai = abr.reshape(1, S5_N), abi.reshape(1, S5_N)
    bbr3, bbi3 = bbr.reshape(S5_GROUPS, S5_STATE, S5_GROUP), bbi.reshape(S5_GROUPS, S5_STATE, S5_GROUP)
    bre, bim = _blockdiag(bbr3, True).astype(bf16), _blockdiag(bbi3, True).astype(bf16)
    bret, bimt = _blockdiag(bbr3).astype(bf16), _blockdiag(bbi3).astype(bf16)
    c_re, c_im = P["s5_c_re"].reshape(S5_GROUPS, S5_GROUP, S5_STATE), P["s5_c_im"].reshape(S5_GROUPS, S5_GROUP, S5_STATE)
    cre, cim = _blockdiag(c_re, True).astype(bf16), _blockdiag(c_im, True).astype(bf16)
    cret, cimt = _blockdiag(c_re).astype(bf16), _blockdiag(c_im).astype(bf16)

    hn0 = _rms_fwd(x, g_mix[0:1], name="mix0_norm")
    We = get_w("even_in", hn0)
    proj_e = _mm(hn0, We["even_w_in"], name="even_in", tn=1280)
    Wr = get_w("even_rest", proj_e)
    ya, states = _hgrn_fwd(proj_e, lbl, hng)
    u_seg = _to_segments(proj_e[:, 4 * 512:])
    fr, fi = _s5_final(u_seg, bre, bim, ar, ai, rb=rb)
    yb_seg, s0r, s0i = _s5_fwd(u_seg, bre, bim, ar, ai, fr, fi, cre, cim, dsk, Wr["s5_w_glu"], bg, rb=rb)
    ycat = jnp.concatenate([ya, _from_segments(yb_seg)], axis=1)
    h1 = _mm(ycat, Wr["even_w_out"], res=x, name="even_out")
    Wf0 = get_w("ffn0", h1)
    h2, ffn0 = _ffn_fwd(h1, g_ffn[0:1], Wf0["ffn_w_in"], cw[0], cb[0:1], Wf0["ffn_w_out"], 0)

    tabs = _rope_tables(positions)
    hn2 = _rms_fwd(h2, g_mix[1:2], name="mix1_norm")
    Wo = get_w("odd", hn2)
    proj_o = _mm(hn2, Wo["odd_w_in"], name="odd_in")
    cqn, ckvn, kr = _mla_prep_fwd(proj_o, qg, kvg, tabs)
    q = _q_post(_mm(cqn, Wo["mla_w_uq"], name="mla_uq"), tabs, transpose=False, name="q_post")
    kvb = _mm(ckvn, Wo["mla_w_ukv"], out_dtype=bf16, name="mla_ukv")
    o, lse = _flash_fwd(q, kvb, kr)
    h3 = _mm(o, Wo["odd_w_out"], res=h2, name="odd_out")
    Wf1 = get_w("ffn1", h3)
    h4, ffn1 = _ffn_fwd(h3, g_ffn[1:2], Wf1["ffn_w_in"], cw[1], cb[1:2], Wf1["ffn_w_out"], 1)
    loss, dh4, dg_final = _loss_head(h4, row(P["final_norm_g"]), target)

    dh3, dg_ffn1, dw_fin1, dcw1, dcb1, dw_fout1 = _ffn_bwd(
        h3, g_ffn[1:2], Wf1["ffn_w_in"], cw[1], cb[1:2], Wf1["ffn_w_out"], ffn1, dh4, 1)
    sent = put_g("ffn1", {"ffn_w_in": dw_fin1, "ffn_w_out": dw_fout1})
    do = _mm(dh3, Wo["odd_w_out"], tb=True, name="odd_do", dep=sent)
    dw_oout = _mm(o, dh3, ta=True, also_bf16=True, name="odd_dwout")
    dkv, dkr_h, dq = _flash_bwd(q, kvb, kr, o, do, lse)
    dq = _q_post(dq, tabs, transpose=True, name="dq_post")
    dw_uq = _mm(cqn, dq, ta=True, also_bf16=True, name="mla_dwuq")
    dcqn = _mm(dq, Wo["mla_w_uq"], tb=True, name="mla_dcq")
    dw_ukv = _mm(ckvn, dkv, ta=True, also_bf16=True, name="mla_dwukv")
    dckvn = _mm(dkv, Wo["mla_w_ukv"], tb=True, name="mla_dckv")
    dproj_o, dqg, dkvg = _mla_prep_bwd(proj_o, qg, kvg, tabs, dcqn, dckvn, dkr_h)
    dhn2 = _mm(dproj_o, Wo["odd_w_in"], tb=True, name="odd_dhn")
    dw_oin = _mm(hn2, dproj_o, ta=True, also_bf16=True, name="odd_dwin")
    sent = put_g("odd", {"odd_w_in": dw_oin, "mla_w_uq": dw_uq, "mla_w_ukv": dw_ukv, "odd_w_out": dw_oout})
    dh2, dg_mix1 = _rms_bwd(h2, g_mix[1:2], dhn2, dh3, name="mix1_dnorm")

    dh1, dg_ffn0, dw_fin0, dcw0, dcb0, dw_fout0 = _ffn_bwd(
        h1, g_ffn[0:1], Wf0["ffn_w_in"], cw[0], cb[0:1], Wf0["ffn_w_out"], ffn0, dh2, 0, dep=sent)
    sent = put_g("ffn0", {"ffn_w_in": dw_fin0, "ffn_w_out": dw_fout0})
    dycat = _mm(dh1, Wr["even_w_out"], tb=True, name="even_dy", dep=sent)
    dw_eout = _mm(ycat, dh1, ta=True, also_bf16=True, name="even_dwout")
    dq_h, df_h, di_h, dg_h, dlbl, dhng = _hgrn_bwd(proj_e, lbl, hng, states, dycat)
    dyb_seg = _to_segments(dycat[:, 512:])
    dy_s5, glr, gli, dcre, dcim, dd, dwg, dbg = _s5_bwd_a(
        u_seg, bre, bim, ar, ai, s0r, s0i, cre, cim, cret, cimt, dsk, Wr["s5_w_glu"], bg, dyb_seg, rb=rb)
    du_seg, dbre, dbim, dar, dai = _s5_bwd_b(
        u_seg, bre, bim, bret, bimt, ar, ai, s0r, s0i, glr, gli, cret, cimt, dsk, dy_s5, rb=rb)
    dproj_e = jnp.concatenate([dq_h, df_h, di_h, dg_h, _from_segments(du_seg)], axis=1)
    dhn0 = _mm(dproj_e, We["even_w_in"], tb=True, name="even_dhn", tk=1280)
    dw_ein = _mm(hn0, dproj_e, ta=True, also_bf16=True, name="even_dwin", tn=1280)
    dx, dg_mix0 = _rms_bwd(x, g_mix[0:1], dhn0, dh1, name="mix0_dnorm")

    unblk = lambda m, a, b: jnp.swapaxes(_blockdiag_t(m, a, b), 1, 2)
    dbbr = unblk(dbre, S5_GROUP, S5_STATE).reshape(S5_N, S5_GROUP)
    dbbi = unblk(dbim, S5_GROUP, S5_STATE).reshape(S5_N, S5_GROUP)
    d_ar, d_ai, d_ldt, d_br, d_bi = _s5_disc_bwd(*disc_in, (dar.reshape(S5_N, 1), dai.reshape(S5_N, 1), dbbr, dbbi))
    small = {
        "norm_mix_g": jnp.concatenate([dg_mix0, dg_mix1], axis=0),
        "norm_ffn_g": jnp.concatenate([dg_ffn0, dg_ffn1], axis=0),
        "final_norm_g": dg_final.reshape(-1),
        "hgrn_lb_logits": dlbl, "hgrn_norm_g": dhng,
        "s5_a_re": d_ar.reshape(1, S5_GROUPS, S5_STATE), "s5_a_im": d_ai.reshape(1, S5_GROUPS, S5_STATE),
        "s5_log_dt": d_ldt.reshape(S5_GROUPS, S5_STATE).sum(axis=1).reshape(1, S5_GROUPS),
        "s5_b_re": d_br.reshape(1, S5_GROUPS, S5_STATE, S5_GROUP), "s5_b_im": d_bi.reshape(1, S5_GROUPS, S5_STATE, S5_GROUP),
        "s5_c_re": unblk(dcre, S5_STATE, S5_GROUP).reshape(1, S5_GROUPS, S5_GROUP, S5_STATE),
        "s5_c_im": unblk(dcim, S5_STATE, S5_GROUP).reshape(1, S5_GROUPS, S5_GROUP, S5_STATE),
        "s5_d": dd, "s5_b_glu": dbg, "mla_q_norm_g": dqg, "mla_kv_norm_g": dkvg,
        "ffn_conv_w": jnp.stack([dcw0, dcw1]), "ffn_conv_b": jnp.concatenate([dcb0, dcb1], axis=0),
    }
    put_g("even", {"even_w_in": dw_ein, "s5_w_glu": (dwg, dwg.astype(bf16)), "even_w_out": dw_eout}, small)
    return loss, dx


WEIGHTS = ["norm_mix_g", "norm_ffn_g", "final_norm_g", "even_w_in", "hgrn_lb_logits", "hgrn_norm_g", "s5_a_re", "s5_a_im",
           "s5_log_dt", "s5_b_re", "s5_b_im", "s5_c_re", "s5_c_im", "s5_d", "s5_w_glu", "s5_b_glu", "even_w_out", "odd_w_in",
           "mla_q_norm_g", "mla_w_uq", "mla_kv_norm_g", "mla_w_ukv", "odd_w_out", "ffn_w_in", "ffn_conv_w", "ffn_conv_b",
           "ffn_w_out"]
SMALL_SHARDED = {"mla_q_norm_g": 1, "mla_kv_norm_g": 1, "ffn_conv_w": 2}
SMALL = [n for n in WEIGHTS if n not in BIG]
SMALL_REP = [n for n in SMALL if n not in SMALL_SHARDED]


def _pack_rows(shapes):
    n = sum(math.prod(s) for s in shapes)
    return -(-n // (SUBLANES * LANES)) * SUBLANES


def _pack(arrays, rows):
    flat = jnp.concatenate([a.reshape(-1) for a in arrays])
    return jnp.pad(flat, (0, rows * LANES - flat.shape[0])).reshape(rows, LANES)


def _unpack(block, shapes):
    flat, out, off = block.reshape(-1), [], 0
    for s in shapes:
        n = math.prod(s)
        out.append(flat[off:off + n].reshape(s))
        off += n
    return out


def kernel(x, positions, norm_mix_g, norm_ffn_g, final_norm_g, even_w_in, hgrn_lb_logits, hgrn_norm_g, s5_a_re, s5_a_im, s5_log_dt, s5_b_re, s5_b_im, s5_c_re, s5_c_im, s5_d, s5_w_glu, s5_b_glu, even_w_out, odd_w_in, mla_q_norm_g, mla_w_uq, mla_kv_norm_g, mla_w_ukv, odd_w_out, ffn_w_in, ffn_conv_w, ffn_conv_b, ffn_w_out, loss_target, m_norm_mix_g, m_norm_ffn_g, m_final_norm_g, m_even_w_in, m_hgrn_lb_logits, m_hgrn_norm_g, m_s5_a_re, m_s5_a_im, m_s5_log_dt, m_s5_b_re, m_s5_b_im, m_s5_c_re, m_s5_c_im, m_s5_d, m_s5_w_glu, m_s5_b_glu, m_even_w_out, m_odd_w_in, m_mla_q_norm_g, m_mla_w_uq, m_mla_kv_norm_g, m_mla_w_ukv, m_odd_w_out, m_ffn_w_in, m_ffn_conv_w, m_ffn_conv_b, m_ffn_w_out, v_norm_mix_g, v_norm_ffn_g, v_final_norm_g, v_even_w_in, v_hgrn_lb_logits, v_hgrn_norm_g, v_s5_a_re, v_s5_a_im, v_s5_log_dt, v_s5_b_re, v_s5_b_im, v_s5_c_re, v_s5_c_im, v_s5_d, v_s5_w_glu, v_s5_b_glu, v_even_w_out, v_odd_w_in, v_mla_q_norm_g, v_mla_w_uq, v_mla_kv_norm_g, v_mla_w_ukv, v_odd_w_out, v_ffn_w_in, v_ffn_conv_w, v_ffn_conv_b, v_ffn_w_out):
    args = dict(locals())
    w = {n: args[n] for n in WEIGHTS}
    m = {n: args["m_" + n] for n in WEIGHTS}
    v = {n: args["v_" + n] for n in WEIGHTS}
    k = 2 * lax.axis_index("x") + lax.axis_index("y")
    kidx = k.reshape(1).astype(jnp.int32)
    axis2d = lambda n: BIG[n] - (1 if n in LAYERED else 0)
    slab = lambda n: w[n].shape[1 + axis2d(n)]

    small_sh_shapes = [w[n].shape for n in SMALL_SHARDED]
    rb = _pack_rows(small_sh_shapes)
    items = {}
    for group, names in GROUPS.items():
        layer = GROUP_LAYER.get(group, 0)
        items[group] = [(_Gather(axis2d(n), slab(n)), None,
                         _place_slab(w[n][layer], axis2d(n), N_CHIPS, kidx, bf16, name=f"place_{n}_{layer}")) for n in names]
    items["even_in"].append((_Gather(0, rb), None,
                             _place_slab(_pack([w[n] for n in SMALL_SHARDED], rb), 0, N_CHIPS, kidx, f32, name="place_small")))
    gathers, tokens = {}, []
    for group in GROUPS:
        sems, srcs, lands, token = _push_start(f"gather_start_{group}", items[group])
        gathers[group] = ([it[0] for it in items[group]], sems, srcs, lands)
        tokens.append(token[0, 0])
    started = functools.reduce(jnp.add, tokens)

    def landed(group, after):
        return _push_wait(f"gather_wait_{group}", [gathers[group]], [after])[0]

    even = landed("even_in", (started + norm_mix_g[0, 0]).reshape(1))
    per_chip = [_unpack(even[-1][c * rb:(c + 1) * rb], small_sh_shapes) for c in range(N_CHIPS)]
    P = {n: w[n] for n in SMALL_REP}
    for i, (n, ax) in enumerate(SMALL_SHARDED.items()):
        P[n] = jnp.concatenate([per_chip[c][i] for c in range(N_CHIPS)], axis=ax)
    P["mla_q_norm_g"], P["mla_kv_norm_g"] = P["mla_q_norm_g"].reshape(1, -1), P["mla_kv_norm_g"].reshape(1, -1)
    fix_w = {"odd_w_in": _pad_odd, "mla_w_uq": _uq_cat}

    def get_w(group, after):
        full = even if group == "even_in" else landed(group, after)
        return {n: fix_w.get(n, lambda a: a)(a) for n, a in zip(GROUPS[group], full)}

    fix_g = {"odd_w_in": lambda g: g[:, :odd_w_in.shape[2]], "mla_w_uq": _uq_uncat}
    g32, scatters, land_now = {}, {}, {}
    ra = _pack_rows([w[n].shape for n in SMALL_REP])
    rs = ra + N_CHIPS * rb
    didx = (2 * kidx + lax.axis_index("c")).astype(jnp.int32)

    def put_g(group, grads, small=None):
        layer = GROUP_LAYER.get(group)
        routes, srcs, names = [], [], list(grads)
        for n in names:
            f = fix_g.get(n, lambda g: g)
            g32.setdefault(n, {})[layer or 0] = f(grads[n][0])
            routes.append(_Scatter(axis2d(n), slab(n), layer if n in LAYERED else None))
            srcs.append(f(grads[n][1]))
            if n not in land_now:
                land_now[n] = lax.empty((3,) + w[n].shape[0 if n in LAYERED else 1:], bf16)
        if small is not None:
            blocks = [_pack([small[n] for n in SMALL_REP], ra)]
            for chip in range(N_CHIPS):
                sl = lambda n, ax: lax.slice_in_dim(small[n].reshape(w[n].shape[:ax] + (-1,) + w[n].shape[ax + 1:]),
                                                    chip * w[n].shape[ax], (chip + 1) * w[n].shape[ax], axis=ax)
                blocks.append(_pack([sl(n, ax) for n, ax in SMALL_SHARDED.items()], rb))
            names.append("small")
            routes.append(_ToAll(rs))
            srcs.append(None)
            land_now["small"] = _place_slab(jnp.concatenate(blocks), 0, N_DEV, didx, f32, name="place_small_grads")
        sems, srcs, lands, token = _push_start(f"scatter_start_{group}", [(r, s, land_now[n]) for r, s, n in zip(routes, srcs, names)])
        land_now.update(zip(names, lands))
        scatters[group] = (routes, sems, srcs, names)
        sent.append(token)
        return token

    sent = []
    loss, dx = _local_step(x[0], positions[0], loss_target[0], get_w, P, put_g)
    sent_last = sent[-1]
    loss = lax.psum(loss[0, 0], ("x", "y", "c"))

    out = {}

    def finish(tag, groups, after):
        waits = [(scatters[g][0], scatters[g][1], scatters[g][2], [land_now[n] for n in scatters[g][3]]) for g in groups]
        for g, lands in zip(groups, _push_wait(f"scatter_wait_{tag}", waits, after)):
            land_now.update(zip(scatters[g][3], lands))
        names = [n for n in dict.fromkeys(n for g in groups for n in scatters[g][3]) if n != "small"]
        part = {}
        for n in names:
            recv = land_now[n] if n in LAYERED else land_now[n][:, None]
            part[n] = _sum4([g32[n][l] for l in sorted(g32[n])], axis2d(n), recv, kidx, name=f"sum4_{n}")
        other = _swap_with_sibling(part, tag)
        done = []
        for n in names:
            C = part[n].shape[-1]
            res = _adamw(w[n].reshape(-1, C), m[n].reshape(-1, C), v[n].reshape(-1, C), [part[n], other[n]], name=f"adamw_{n}")
            out[n] = [r.reshape(w[n].shape) for r in res]
            done.append(res[0])
        return done

    done = finish("a", ["ffn1", "odd", "ffn0"], [dx, sent_last])
    finish("b", ["even"], done)

    order = SMALL_REP + list(SMALL_SHARDED)
    packed = lambda src: jnp.concatenate([_pack([src[n] for n in SMALL_REP], ra), _pack([src[n] for n in SMALL_SHARDED], rb)])
    res = _adamw_small(land_now["small"], packed(w), packed(m), packed(v), kidx, ra, rb)
    for r in res:
        parts = _unpack(r[:ra], [w[n].shape for n in SMALL_REP]) + _unpack(r[ra:], small_sh_shapes)
        for n, a in zip(order, parts):
            out.setdefault(n, []).append(a)

    return (loss, dx[None], *[out[n][0] for n in WEIGHTS], *[out[n][1] for n in WEIGHTS],
            *[out[n][2] for n in WEIGHTS], *[out[n][3] for n in WEIGHTS])
```

```python
import functools
import math

import jax
import jax.numpy as jnp
from jax import lax
from jax.experimental import pallas as pl
from jax.experimental.pallas import tpu as pltpu

f32, bf16 = jnp.float32, jnp.bfloat16
EPS = 1e-6
LANES = 128
SUBLANES = 8
VMEM_BYTES = 48 * 1024 * 1024
HGRN_CHUNK = 64
HGRN_HEADS = 4
S5_GROUPS, S5_STATE, S5_GROUP = 32, 64, 16
S5_N = S5_GROUPS * S5_STATE
S5_SEG = SUBLANES
MLA_HEADS, MLA_NOPE, MLA_ROPE, MLA_V = 8, 128, 64, 128
MLA_QK = MLA_NOPE + MLA_ROPE
MLA_Q_RANK, MLA_KV_RANK = 384, 256
ROPE_THETA = 10000.0
D_FF = 2816
ADAM_LR, ADAM_B1, ADAM_B2, ADAM_EPS, ADAM_WD, ADAM_STEP = 0.001, 0.9, 0.999, 1e-08, 0.01, 10
MESH = pl.DeviceIdType.MESH
HI = lax.Precision.HIGHEST


def _cp(*dims):
    return pltpu.CompilerParams(dimension_semantics=dims if dims else None, vmem_limit_bytes=VMEM_BYTES)


def _tile(n, t):
    if n <= t:
        return n
    c = (t // LANES) * LANES
    while c >= LANES:
        if n % c == 0:
            return c
        c -= LANES
    return n


def _dot(a, b, dn=None, precision=None):
    if dn is None:
        dn = (((a.ndim - 1,), (0,)), ((), ()))
    return lax.dot_general(a, b, dn, preferred_element_type=f32, precision=precision)


NT = (((1,), (1,)), ((), ()))
TN = (((0,), (0,)), ((), ()))


def _bdot(a, b, dn=None):
    return _dot(a.astype(bf16), b.astype(bf16), dn)


def _mm(a, b, *, name, ta=False, tb=False, out_dtype=f32, res=None, also_bf16=False, tm=1024, tn=1024, tk=1024, dep=None):
    halves = lambda s: (s[1], 2 * s[2]) if len(s) == 3 else s
    M, K = (a.shape[1], a.shape[0]) if ta else halves(a.shape)
    N = b.shape[0] if tb else halves(b.shape)[1]
    tm, tn, tk = _tile(M, tm), _tile(N, tn), _tile(K, tk)
    if a.ndim == 3:
        tk = _tile(K // 2, tk)
    if b.ndim == 3:
        tn = _tile(N // 2, tn)
    nk = K // tk
    dn = (((0 if ta else 1,), (1 if tb else 0,)), ((), ()))

    def body(*refs):
        a_ref, b_ref = refs[0], refs[1]
        r_ref = refs[2] if res is not None else None
        nin = 2 + (res is not None) + (dep is not None)
        outs = refs[nin:-1]
        acc = refs[-1]
        k = pl.program_id(2)
        p = _bdot(a_ref[...], b_ref[...], dn)

        @pl.when(k == 0)
        def _():
            acc[...] = p

        @pl.when(k > 0)
        def _():
            acc[...] += p

        @pl.when(k == nk - 1)
        def _():
            r = acc[...]
            if r_ref is not None:
                r = r + r_ref[...]
            outs[0][...] = r.astype(out_dtype)
            if also_bf16:
                outs[1][...] = r.astype(bf16)

    a_spec = pl.BlockSpec((tk, tm), lambda i, j, k: (k, i)) if ta else pl.BlockSpec((tm, tk), lambda i, j, k: (i, k))
    b_spec = pl.BlockSpec((tn, tk), lambda i, j, k: (j, k)) if tb else pl.BlockSpec((tk, tn), lambda i, j, k: (k, j))
    if a.ndim == 3:
        kh = K // 2 // tk
        a_spec = pl.BlockSpec((None, tm, tk), lambda i, j, k: (k // kh, i, k % kh))
    if b.ndim == 3:
        nh = N // 2 // tn
        b_spec = pl.BlockSpec((None, tk, tn), lambda i, j, k: (j // nh, k, j % nh))
    o_spec = pl.BlockSpec((tm, tn), lambda i, j, k: (i, j))
    in_specs, args = [a_spec, b_spec], [a, b]
    if res is not None:
        in_specs.append(o_spec)
        args.append(res)
    if dep is not None:
        in_specs.append(pl.BlockSpec(memory_space=pl.ANY))
        args.append(dep)
    out_shape = [jax.ShapeDtypeStruct((M, N), out_dtype)]
    out_specs = [o_spec]
    if also_bf16:
        out_shape.append(jax.ShapeDtypeStruct((M, N), bf16))
        out_specs.append(o_spec)
    out = pl.pallas_call(
        body, name=name, grid=(M // tm, N // tn, nk), in_specs=in_specs, out_specs=out_specs, out_shape=out_shape,
        scratch_shapes=[pltpu.VMEM((tm, tn), f32)], compiler_params=_cp("parallel", "parallel", "arbitrary"),
    )(*args)
    return out if also_bf16 else out[0]


def _rms_fwd(x, g, *, name, col=0, width=None, tm=512):
    T = x.shape[0]
    width = x.shape[1] if width is None else width
    tm = _tile(T, tm)

    def body(x_ref, g_ref, o_ref):
        xv = x_ref[...]
        r = lax.rsqrt(jnp.mean(xv * xv, axis=-1, keepdims=True) + EPS)
        o_ref[...] = (xv * r * g_ref[...]).astype(bf16)

    return pl.pallas_call(
        body, name=name, grid=(T // tm,),
        in_specs=[pl.BlockSpec((tm, width), lambda i: (i, col)), pl.BlockSpec((1, width), lambda i: (0, 0))],
        out_specs=pl.BlockSpec((tm, width), lambda i: (i, 0)), out_shape=jax.ShapeDtypeStruct((T, width), bf16),
        compiler_params=_cp("parallel"),
    )(x, g)


def _rms_bwd_math(xv, g, dy):
    r = lax.rsqrt(jnp.mean(xv * xv, axis=-1, keepdims=True) + EPS)
    xh = xv * r
    dxh = dy * g
    dx = r * (dxh - xh * jnp.mean(dxh * xh, axis=-1, keepdims=True))
    dg = jnp.sum(dy * xh, axis=0, keepdims=True)
    return dx, dg


def _rms_bwd(x, g, dy, res=None, *, name, tm=512):
    T, D = x.shape
    tm = _tile(T, tm)

    def body(*refs):
        x_ref, g_ref, dy_ref = refs[:3]
        r_ref = refs[3] if res is not None else None
        dx_ref, dg_ref = refs[-2:]
        dx, dg = _rms_bwd_math(x_ref[...], g_ref[...], dy_ref[...].astype(f32))
        if r_ref is not None:
            dx = dx + r_ref[...]
        dx_ref[...] = dx

        @pl.when(pl.program_id(0) == 0)
        def _():
            dg_ref[...] = dg

        @pl.when(pl.program_id(0) > 0)
        def _():
            dg_ref[...] += dg

    row = pl.BlockSpec((tm, D), lambda i: (i, 0))
    vec = pl.BlockSpec((1, D), lambda i: (0, 0))
    in_specs, args = [row, vec, row], [x, g, dy]
    if res is not None:
        in_specs.append(row)
        args.append(res)
    return pl.pallas_call(
        body, name=name, grid=(T // tm,), in_specs=in_specs, out_specs=[row, vec],
        out_shape=[jax.ShapeDtypeStruct((T, D), f32), jax.ShapeDtypeStruct((1, D), f32)],
        compiler_params=_cp("arbitrary"),
    )(*args)


def _loss_head(h, g, target, *, tm=512):
    T, D = h.shape
    tm = _tile(T, tm)

    def body(h_ref, g_ref, t_ref, loss_ref, dh_ref, dg_ref):
        hv, gv = h_ref[...], g_ref[...]
        r = lax.rsqrt(jnp.mean(hv * hv, axis=-1, keepdims=True) + EPS)
        e = hv * r * gv - t_ref[...]
        part = 0.5 * jnp.sum(jnp.mean(e * e, axis=-1, keepdims=True), axis=0, keepdims=True)
        dx, dg = _rms_bwd_math(hv, gv, e * (1.0 / D))
        dh_ref[...] = dx

        @pl.when(pl.program_id(0) == 0)
        def _():
            loss_ref[...] = part
            dg_ref[...] = dg

        @pl.when(pl.program_id(0) > 0)
        def _():
            loss_ref[...] += part
            dg_ref[...] += dg

    row = pl.BlockSpec((tm, D), lambda i: (i, 0))
    vec = pl.BlockSpec((1, D), lambda i: (0, 0))
    return pl.pallas_call(
        body, name="loss_head", grid=(T // tm,), in_specs=[row, vec, row],
        out_specs=[pl.BlockSpec((1, 1), lambda i: (0, 0)), row, vec],
        out_shape=[jax.ShapeDtypeStruct((1, 1), f32), jax.ShapeDtypeStruct((T, D), f32), jax.ShapeDtypeStruct((1, D), f32)],
        compiler_params=_cp("arbitrary"),
    )(h, g, target)


FFN_W = 2 * LANES
FFN_ROWS = 128
HALO = SUBLANES


def _conv_taps(a_ref, c, rc):
    if isinstance(c, int) and c == 0:
        ext = jnp.concatenate([jnp.zeros((HALO, FFN_W), f32), a_ref[pl.ds(0, rc), :]], axis=0)
    else:
        ext = a_ref[pl.ds(pl.multiple_of(c * rc - HALO, HALO), rc + HALO), :]
    return ext[HALO:], pltpu.roll(ext, 1, 0)[HALO:], pltpu.roll(ext, 2, 0)[HALO:]


def _chunk_rows(c, rc):
    return pl.ds(c * rc, rc) if isinstance(c, int) else pl.ds(pl.multiple_of(c * rc, rc), rc)


def _ffn_mid_fwd(au, cw, cb, *, name):
    T = au.shape[0]
    F = au.shape[1] // 2
    nb = F // FFN_W
    rc = min(FFN_ROWS, T)
    nc = T // rc

    def body(a_ref, u_ref, w_ref, b_ref, z_ref):
        w, b = w_ref[...], b_ref[...]

        def chunk(c):
            a, a1, a2 = _conv_taps(a_ref, c, rc)
            rows = _chunk_rows(c, rc)
            ac = w[0:1] * a2 + w[1:2] * a1 + w[2:3] * a + b
            z_ref[rows, :] = (ac * jax.nn.sigmoid(ac) * u_ref[rows, :]).astype(bf16)

        chunk(0)
        lax.fori_loop(1, nc, lambda c, _: chunk(c), None)

    return pl.pallas_call(
        body, name=name, grid=(nb,),
        in_specs=[pl.BlockSpec((T, FFN_W), lambda j: (0, j)), pl.BlockSpec((T, FFN_W), lambda j: (0, nb + j)),
                  pl.BlockSpec((3, FFN_W), lambda j: (0, j)), pl.BlockSpec((1, FFN_W), lambda j: (0, j))],
        out_specs=pl.BlockSpec((T, FFN_W), lambda j: (0, j)), out_shape=jax.ShapeDtypeStruct((T, F), bf16),
        compiler_params=_cp("parallel"),
    )(au, au, cw, cb)


def _ffn_mid_bwd(au, cw, cb, dz, *, name):
    T = au.shape[0]
    F = au.shape[1] // 2
    nb = F // FFN_W
    rc = min(FFN_ROWS, T)
    nc = T // rc

    def body(a_ref, u_ref, w_ref, b_ref, dz_ref, dau_ref, dw_ref, db_ref):
        w, b = w_ref[...], b_ref[...]

        def chunk(c, carry):
            nxt, s0, s1, s2, sb = carry
            a, a1, a2 = _conv_taps(a_ref, c, rc)
            rows = _chunk_rows(c, rc)
            ac = w[0:1] * a2 + w[1:2] * a1 + w[2:3] * a + b
            sg = jax.nn.sigmoid(ac)
            dz = dz_ref[rows, :].astype(f32)
            dau_ref[1, rows, :] = (dz * ac * sg).astype(bf16)
            dac = dz * u_ref[rows, :] * sg * (1.0 + ac * (1.0 - sg))
            ext = jnp.concatenate([dac, nxt], axis=0)
            d1, d2 = pltpu.roll(ext, rc + HALO - 1, 0)[:rc], pltpu.roll(ext, rc + HALO - 2, 0)[:rc]
            dau_ref[0, rows, :] = (w[2:3] * dac + w[1:2] * d1 + w[0:1] * d2).astype(bf16)
            tot = lambda v: jnp.sum(v, axis=0, keepdims=True)
            return dac[:HALO], s0 + tot(dac * a2), s1 + tot(dac * a1), s2 + tot(dac * a), sb + tot(dac)

        z = jnp.zeros((1, FFN_W), f32)
        carry = (jnp.zeros((HALO, FFN_W), f32), z, z, z, z)
        carry = lax.fori_loop(0, nc - 1, lambda k, cr: chunk(nc - 1 - k, cr), carry)
        _, s0, s1, s2, sb = chunk(0, carry)
        rows = lax.broadcasted_iota(jnp.int32, (3, FFN_W), 0)
        dw_ref[...] = jnp.where(rows == 0, s0, jnp.where(rows == 1, s1, s2))
        db_ref[...] = sb

    col = lambda off: pl.BlockSpec((T, FFN_W), lambda j: (0, off + j))
    return pl.pallas_call(
        body, name=name, grid=(nb,),
        in_specs=[col(0), col(nb), pl.BlockSpec((3, FFN_W), lambda j: (0, j)), pl.BlockSpec((1, FFN_W), lambda j: (0, j)), col(0)],
        out_specs=[pl.BlockSpec((2, T, FFN_W), lambda j: (0, 0, j)), pl.BlockSpec((3, FFN_W), lambda j: (0, j)),
                   pl.BlockSpec((1, FFN_W), lambda j: (0, j))],
        out_shape=[jax.ShapeDtypeStruct((2, T, F), bf16), jax.ShapeDtypeStruct((3, F), f32), jax.ShapeDtypeStruct((1, F), f32)],
        compiler_params=_cp("parallel"),
    )(au, au, cw, cb, dz)


BNN = (((2,), (1,)), ((0,), (0,)))
BNT = (((2,), (2,)), ((0,), (0,)))
BTN = (((1,), (1,)), ((0,), (0,)))


def _heads(x):
    return jnp.stack([x[:, h * LANES:(h + 1) * LANES] for h in range(HGRN_HEADS)])


def _put_heads(ref, rows, x, dtype):
    for h in range(HGRN_HEADS):
        ref[rows, h * LANES:(h + 1) * LANES] = x[h].astype(dtype)


def _hgrn_lb(l):
    m = jnp.max(l, axis=0, keepdims=True)
    e = jnp.exp(l - m)
    return e[0:1] / jnp.sum(e, axis=0, keepdims=True)


def _hgrn_chunk(q, fx, lb):
    H, C = q.shape[0], q.shape[1]
    sg = jax.nn.sigmoid(fx)
    F = lb + (1.0 - lb) * sg
    k = 1.0 - F
    logF = jnp.log(F)
    r = lax.broadcasted_iota(jnp.int32, (H, C, C), 1)
    c = lax.broadcasted_iota(jnp.int32, (H, C, C), 2)
    tril = (r >= c)
    b = _dot(tril.astype(f32), logF, BNN, precision=HI)
    bl = jnp.sum(logF, axis=1, keepdims=True)
    eb = jnp.exp(b)
    enb = jnp.exp(-b)
    elb = jnp.exp(bl - b)
    return dict(sg=sg, F=F, k=k, b=b, bl=bl, eb=eb, enb=enb, elb=elb, qd=q * eb, kd=k * enb, kl=k * elb, tril=tril)


def _hgrn_fwd(proj, lbl, ng, *, rb=512):
    T = proj.shape[0]
    rb = min(rb, T)
    cpb = rb // HGRN_CHUNK
    nblk = T // rb
    H = HGRN_HEADS

    def body(q_ref, f_ref, i_ref, g_ref, lbl_ref, ng_ref, y_ref, st_ref, S):
        @pl.when(pl.program_id(0) == 0)
        def _():
            S[...] = jnp.zeros_like(S)

        lb = _heads(_hgrn_lb(lbl_ref[...]))
        ngv = _heads(ng_ref[...])
        for c in range(cpb):
            sl = pl.ds(c * HGRN_CHUNK, HGRN_CHUNK)
            v, gx = _heads(i_ref[sl, :]), _heads(g_ref[sl, :])
            ch = _hgrn_chunk(_heads(q_ref[sl, :]), _heads(f_ref[sl, :]), lb)
            att = jnp.where(ch["tril"], _bdot(ch["qd"], ch["kd"], BNT), 0.0)
            St = S[...]
            st_ref[:, c] = St
            o = _bdot(att, v, BNN) + _bdot(ch["qd"], St, BNT)
            S[...] = St * jnp.exp(ch["bl"]) + _bdot(v, ch["kl"], BTN)
            r = lax.rsqrt(jnp.mean(o * o, axis=-1, keepdims=True) + EPS)
            _put_heads(y_ref, sl, o * r * ngv * (gx * jax.nn.sigmoid(gx)), bf16)

    col = lambda off: pl.BlockSpec((rb, H * LANES), lambda n: (n, off))
    return pl.pallas_call(
        body, name="hgrn_fwd", grid=(nblk,),
        in_specs=[col(0), col(1), col(2), col(3), pl.BlockSpec((2, H * LANES), lambda n: (0, 0)),
                  pl.BlockSpec((1, H * LANES), lambda n: (0, 0))],
        out_specs=[pl.BlockSpec((rb, H * LANES), lambda n: (n, 0)),
                   pl.BlockSpec((H, cpb, LANES, LANES), lambda n: (0, n, 0, 0))],
        out_shape=[jax.ShapeDtypeStruct((T, H * LANES), bf16),
                   jax.ShapeDtypeStruct((H, T // HGRN_CHUNK, LANES, LANES), f32)],
        scratch_shapes=[pltpu.VMEM((H, LANES, LANES), f32)], compiler_params=_cp("arbitrary"),
    )(proj, proj, proj, proj, lbl, ng)


def _hgrn_bwd(proj, lbl, ng, states, dy, *, rb=512):
    T = proj.shape[0]
    rb = min(rb, T)
    cpb = rb // HGRN_CHUNK
    nblk = T // rb
    H = HGRN_HEADS
    C = HGRN_CHUNK

    def body(q_ref, f_ref, i_ref, g_ref, lbl_ref, ng_ref, st_ref, dy_ref,
             dq_ref, df_ref, di_ref, dg_ref, dl_ref, dng_ref, dS, dlb_acc, dng_acc):
        n = pl.program_id(0)

        @pl.when(n == 0)
        def _():
            dS[...] = jnp.zeros_like(dS)
            dlb_acc[...] = jnp.zeros_like(dlb_acc)
            dng_acc[...] = jnp.zeros_like(dng_acc)

        lb_row = _hgrn_lb(lbl_ref[...])
        lb = _heads(lb_row)
        ngv = _heads(ng_ref[...])
        r_i = lax.broadcasted_iota(jnp.int32, (H, C, C), 1)
        c_i = lax.broadcasted_iota(jnp.int32, (H, C, C), 2)
        triu = (c_i >= r_i).astype(f32)
        rows_sum = lambda x: jnp.sum(x, axis=1, keepdims=True)
        for c in reversed(range(cpb)):
            sl = pl.ds(c * C, C)
            q, v, gx = _heads(q_ref[sl, :]), _heads(i_ref[sl, :]), _heads(g_ref[sl, :])
            ch = _hgrn_chunk(q, _heads(f_ref[sl, :]), lb)
            qd, kd, kl = ch["qd"], ch["kd"], ch["kl"]
            att = jnp.where(ch["tril"], _bdot(qd, kd, BNT), 0.0)
            St = st_ref[:, c]
            o = _bdot(att, v, BNN) + _bdot(qd, St, BNT)
            r = lax.rsqrt(jnp.mean(o * o, axis=-1, keepdims=True) + EPS)
            on = o * r
            sgg = jax.nn.sigmoid(gx)
            gate = gx * sgg
            dyv = _heads(dy_ref[sl, :].astype(f32))
            _put_heads(dg_ref, sl, dyv * on * ngv * sgg * (1.0 + gx * (1.0 - sgg)), bf16)
            dng_acc[...] += rows_sum(dyv * on * gate)
            don = dyv * ngv * gate
            do = r * (don - on * jnp.mean(don * on, axis=-1, keepdims=True))
            dSt = dS[...]
            dA = jnp.where(ch["tril"], _bdot(do, v, BNT), 0.0)
            dv = _bdot(att, do, BTN) + _bdot(kl, dSt, BNT)
            dqd = _bdot(dA, kd, BNN) + _bdot(do, St, BNN)
            dkd = _bdot(dA, qd, BTN)
            dkl = _bdot(v, dSt, BNN)
            dec = jnp.exp(ch["bl"])
            ddec = rows_sum(St * dSt)
            dS[...] = _bdot(do, qd, BTN) + dSt * dec
            dB = dqd * qd - dkd * kd - dkl * kl
            dbl = rows_sum(dkl * kl) + ddec * dec
            dk = dkd * ch["enb"] + dkl * ch["elb"]
            dlogF = _dot(triu, dB, BNN, precision=HI) + dbl
            dF = dlogF / ch["F"] - dk
            sg = ch["sg"]
            _put_heads(dq_ref, sl, dqd * ch["eb"], bf16)
            _put_heads(di_ref, sl, dv, bf16)
            _put_heads(df_ref, sl, dF * (1.0 - lb) * sg * (1.0 - sg), bf16)
            dlb_acc[...] += rows_sum(dF * (1.0 - sg))

        @pl.when(n == nblk - 1)
        def _():
            rows = lax.broadcasted_iota(jnp.int32, (2, LANES), 0)
            for h in range(H):
                hs = pl.ds(h * LANES, LANES)
                lbh = lb_row[:, h * LANES:(h + 1) * LANES]
                dl0 = dlb_acc[h] * lbh * (1.0 - lbh)
                dl_ref[:, hs] = jnp.where(rows == 0, dl0, -dl0)
                dng_ref[:, hs] = dng_acc[h]

    col = lambda off: pl.BlockSpec((rb, H * LANES), lambda n: (nblk - 1 - n, off))
    vec = lambda rows: pl.BlockSpec((rows, H * LANES), lambda n: (0, 0))
    tok = jax.ShapeDtypeStruct((T, H * LANES), bf16)
    return pl.pallas_call(
        body, name="hgrn_bwd", grid=(nblk,),
        in_specs=[col(0), col(1), col(2), col(3), vec(2), vec(1),
                  pl.BlockSpec((H, cpb, LANES, LANES), lambda n: (0, nblk - 1 - n, 0, 0)), col(0)],
        out_specs=[col(0), col(0), col(0), col(0), vec(2), vec(1)],
        out_shape=[tok, tok, tok, tok, jax.ShapeDtypeStruct((2, H * LANES), f32), jax.ShapeDtypeStruct((1, H * LANES), f32)],
        scratch_shapes=[pltpu.VMEM((H, LANES, LANES), f32), pltpu.VMEM((H, 1, LANES), f32), pltpu.VMEM((H, 1, LANES), f32)],
        compiler_params=_cp("arbitrary"),
    )(proj, proj, proj, proj, lbl, ng, states, dy)


def _s5_disc_math(ar, ai, ldt, br, bi):
    dt = jnp.exp(ldt)
    mag = jnp.exp(ar * dt)
    abr, abi = mag * jnp.cos(ai * dt), mag * jnp.sin(ai * dt)
    den = ar * ar + ai * ai
    xr, xi = abr - 1.0, abi
    cr = (xr * ar + xi * ai) / den
    ci = (xi * ar - xr * ai) / den
    return abr, abi, cr * br - ci * bi, cr * bi + ci * br


def _s5_disc_fwd(ar, ai, ldt, br, bi):
    def body(ar_ref, ai_ref, ldt_ref, br_ref, bi_ref, o0, o1, o2, o3):
        outs = _s5_disc_math(ar_ref[...], ai_ref[...], ldt_ref[...], br_ref[...], bi_ref[...])
        for o, v in zip((o0, o1, o2, o3), outs):
            o[...] = v

    return pl.pallas_call(
        body, name="s5_disc_fwd",
        out_shape=[jax.ShapeDtypeStruct(ar.shape, f32)] * 2 + [jax.ShapeDtypeStruct(br.shape, f32)] * 2,
    )(ar, ai, ldt, br, bi)


def _s5_disc_bwd(ar, ai, ldt, br, bi, cts):
    def body(ar_ref, ai_ref, ldt_ref, br_ref, bi_ref, c0, c1, c2, c3, o0, o1, o2, o3, o4):
        _, vjp = jax.vjp(_s5_disc_math, ar_ref[...], ai_ref[...], ldt_ref[...], br_ref[...], bi_ref[...])
        for o, v in zip((o0, o1, o2, o3, o4), vjp((c0[...], c1[...], c2[...], c3[...]))):
            o[...] = v

    return pl.pallas_call(
        body, name="s5_disc_bwd",
        out_shape=[jax.ShapeDtypeStruct(ar.shape, f32)] * 3 + [jax.ShapeDtypeStruct(br.shape, f32)] * 2,
    )(ar, ai, ldt, br, bi, *cts)


S5_LC = 512
S5_NLC = S5_N // S5_LC
S5_UB = 4
S5_UNROLL = 4


def _cmul(ar, ai, xr, xi):
    return ar * xr - ai * xi, ar * xi + ai * xr


def _cpow(ar, ai, n):
    rr, ri = None, None
    br, bi = ar, ai
    while n:
        if n & 1:
            rr, ri = (br, bi) if rr is None else _cmul(rr, ri, br, bi)
        n >>= 1
        if n:
            br, bi = _cmul(br, bi, br, bi)
    return rr, ri


def _s5_bu(u_ref, bre_ref, bim_ref, xr, xi):
    for k in range(S5_UB):
        uk = u_ref[:, k * LANES:(k + 1) * LANES].astype(bf16)
        xr[:, k * S5_LC:(k + 1) * S5_LC] = _dot(uk, bre_ref[k])
        xi[:, k * S5_LC:(k + 1) * S5_LC] = _dot(uk, bim_ref[k])


def _s5_scan(xr, xi, sr, si, ar_ref, ai_ref, nsteps, store):
    for c in range(S5_NLC):
        cs = slice(c * S5_LC, (c + 1) * S5_LC)
        a_r = jnp.broadcast_to(ar_ref[:, cs], (S5_SEG, S5_LC))
        a_i = jnp.broadcast_to(ai_ref[:, cs], (S5_SEG, S5_LC))

        def step(j, carry, cs=cs, a_r=a_r, a_i=a_i):
            pr, pi = carry
            rows = pl.ds(pl.multiple_of(j * S5_SEG, S5_SEG), S5_SEG)
            nr = a_r * pr - a_i * pi + xr[rows, cs]
            ni = a_r * pi + a_i * pr + xi[rows, cs]
            if store:
                xr[rows, cs] = nr
                xi[rows, cs] = ni
            return nr, ni

        fr, fi = lax.fori_loop(0, nsteps, step, (sr[:, cs], si[:, cs]), unroll=S5_UNROLL)
        sr[:, cs] = fr
        si[:, cs] = fi


def _s5_rscan(dr, di, xr, xi, s0r, s0i, gr, gi, acc_r, acc_i, ar_ref, ai_ref, nsteps):
    for c in range(S5_NLC):
        cs = slice(c * S5_LC, (c + 1) * S5_LC)
        a_r = jnp.broadcast_to(ar_ref[:, cs], (S5_SEG, S5_LC))
        a_i = jnp.broadcast_to(ai_ref[:, cs], (S5_SEG, S5_LC))

        def step(jj, carry, cs=cs, a_r=a_r, a_i=a_i):
            pr, pi, cr, ci = carry
            j = nsteps - 1 - jj
            rows = pl.ds(pl.multiple_of(j * S5_SEG, S5_SEG), S5_SEG)
            nr = dr[rows, cs] + a_r * pr + a_i * pi
            ni = di[rows, cs] + a_r * pi - a_i * pr
            dr[rows, cs] = nr
            di[rows, cs] = ni
            if acc_r is not None:
                prev = pl.ds(pl.multiple_of(jnp.maximum(j - 1, 0) * S5_SEG, S5_SEG), S5_SEG)
                first = j == 0
                pr_s = jnp.where(first, s0r[:, cs], xr[prev, cs])
                pi_s = jnp.where(first, s0i[:, cs], xi[prev, cs])
                cr = cr + nr * pr_s + ni * pi_s
                ci = ci - nr * pi_s + ni * pr_s
            return nr, ni, cr, ci

        z = jnp.zeros((S5_SEG, S5_LC), f32)
        init = (gr[:, cs], gi[:, cs], z, z)
        fr, fi, cr, ci = lax.fori_loop(0, nsteps, step, init, unroll=S5_UNROLL)
        gr[:, cs] = fr
        gi[:, cs] = fi
        if acc_r is not None:
            acc_r[:, cs] += cr
            acc_i[:, cs] += ci


def _s5_seg_carry(fr, fi, ar, ai, seg_len, reverse):
    pr, pi = _cpow(ar, ai if not reverse else -ai, seg_len)
    rows = lax.broadcasted_iota(jnp.int32, fr.shape, 0)
    cr, ci = jnp.zeros_like(fr), jnp.zeros_like(fi)
    sh = (S5_SEG - 1) if reverse else 1
    fr_s, fi_s = pltpu.roll(fr, sh, 0), pltpu.roll(fi, sh, 0)
    order = range(S5_SEG - 2, -1, -1) if reverse else range(1, S5_SEG)
    for r in order:
        c_r, c_i = pltpu.roll(cr, sh, 0), pltpu.roll(ci, sh, 0)
        m_r, m_i = _cmul(pr, pi, c_r, c_i)
        cr = jnp.where(rows == r, m_r + fr_s, cr)
        ci = jnp.where(rows == r, m_i + fi_s, ci)
    return cr, ci


def _gelu_parts(y):
    c0 = math.sqrt(2.0 / math.pi)
    t = jnp.tanh(c0 * (y + 0.044715 * y * y * y))
    z = 0.5 * y * (1.0 + t)
    dz = 0.5 * (1.0 + t) + 0.5 * y * (1.0 - t * t) * c0 * (1.0 + 3.0 * 0.044715 * y * y)
    return z, dz


def _s5_y(xr, xi, u_ref, cre_ref, cim_ref, d_ref):
    ys = []
    for k in range(S5_UB):
        cs = slice(k * S5_LC, (k + 1) * S5_LC)
        ys.append(_bdot(xr[:, cs], cre_ref[k]) - _bdot(xi[:, cs], cim_ref[k]))
    return jnp.concatenate(ys, axis=1) + d_ref[...] * u_ref[...]


def _s5_specs(T, rb, rev=False):
    nblk = T // rb
    blk = (lambda i: (nblk - 1 - i, 0)) if rev else (lambda i: (i, 0))
    tok = pl.BlockSpec((rb, 4 * LANES), blk)
    bmat = pl.BlockSpec((S5_UB, LANES, S5_LC), lambda i: (0, 0, 0))
    cmat = pl.BlockSpec((S5_UB, S5_LC, LANES), lambda i: (0, 0, 0))
    avec = pl.BlockSpec((1, S5_N), lambda i: (0, 0))
    seg = pl.BlockSpec((S5_SEG, S5_N), lambda i: (0, 0))
    cvec = pl.BlockSpec((1, 4 * LANES), lambda i: (0, 0))
    s0 = pl.BlockSpec((1, S5_SEG, S5_N), (lambda i: (nblk - 1 - i, 0, 0)) if rev else (lambda i: (i, 0, 0)))
    return dict(tok=tok, bmat=bmat, cmat=cmat, avec=avec, seg=seg, cvec=cvec, s0=s0, nblk=nblk)


def _s5_final(u, bre, bim, ar, ai, *, rb):
    T = u.shape[0]
    sp = _s5_specs(T, rb)

    def body(u_ref, bre_ref, bim_ref, ar_ref, ai_ref, fr_ref, fi_ref, xr, xi):
        @pl.when(pl.program_id(0) == 0)
        def _():
            fr_ref[...] = jnp.zeros_like(fr_ref)
            fi_ref[...] = jnp.zeros_like(fi_ref)

        _s5_bu(u_ref, bre_ref, bim_ref, xr, xi)
        _s5_scan(xr, xi, fr_ref, fi_ref, ar_ref, ai_ref, rb // S5_SEG, False)

    return pl.pallas_call(
        body, name="s5_final", grid=(sp["nblk"],),
        in_specs=[sp["tok"], sp["bmat"], sp["bmat"], sp["avec"], sp["avec"]], out_specs=[sp["seg"], sp["seg"]],
        out_shape=[jax.ShapeDtypeStruct((S5_SEG, S5_N), f32)] * 2,
        scratch_shapes=[pltpu.VMEM((rb, S5_N), f32)] * 2, compiler_params=_cp("arbitrary"),
    )(u, bre, bim, ar, ai)


def _s5_fwd(u, bre, bim, ar, ai, fr, fi, cre, cim, dsk, wg, bg, *, rb):
    T = u.shape[0]
    sp = _s5_specs(T, rb)
    seg_len = T // S5_SEG

    def body(u_ref, bre_ref, bim_ref, ar_ref, ai_ref, fr_ref, fi_ref, cre_ref, cim_ref, d_ref, wg_ref, bg_ref,
             o_ref, s0r_ref, s0i_ref, xr, xi, sr, si):
        @pl.when(pl.program_id(0) == 0)
        def _():
            i_r, i_i = _s5_seg_carry(fr_ref[...], fi_ref[...], ar_ref[...], ai_ref[...], seg_len, False)
            sr[...] = i_r
            si[...] = i_i

        s0r_ref[0] = sr[...]
        s0i_ref[0] = si[...]
        _s5_bu(u_ref, bre_ref, bim_ref, xr, xi)
        _s5_scan(xr, xi, sr, si, ar_ref, ai_ref, rb // S5_SEG, True)
        y = _s5_y(xr, xi, u_ref, cre_ref, cim_ref, d_ref)
        z, _ = _gelu_parts(y)
        v = _bdot(z, wg_ref[...]) + bg_ref[...]
        o_ref[...] = (z * jax.nn.sigmoid(v)).astype(bf16)

    wspec = pl.BlockSpec((4 * LANES, 4 * LANES), lambda i: (0, 0))
    return pl.pallas_call(
        body, name="s5_fwd", grid=(sp["nblk"],),
        in_specs=[sp["tok"], sp["bmat"], sp["bmat"], sp["avec"], sp["avec"], sp["seg"], sp["seg"], sp["cmat"], sp["cmat"],
                  sp["cvec"], wspec, sp["cvec"]],
        out_specs=[sp["tok"], sp["s0"], sp["s0"]],
        out_shape=[jax.ShapeDtypeStruct((T, 4 * LANES), bf16)] + [jax.ShapeDtypeStruct((sp["nblk"], S5_SEG, S5_N), f32)] * 2,
        scratch_shapes=[pltpu.VMEM((rb, S5_N), f32)] * 2 + [pltpu.VMEM((S5_SEG, S5_N), f32)] * 2,
        compiler_params=_cp("arbitrary"),
    )(u, bre, bim, ar, ai, fr, fi, cre, cim, dsk, wg, bg)


def _s5_bwd_a(u, bre, bim, ar, ai, s0r, s0i, cre, cim, cret, cimt, dsk, wg, bg, dout, *, rb):
    T = u.shape[0]
    sp = _s5_specs(T, rb, rev=True)

    def body(u_ref, bre_ref, bim_ref, ar_ref, ai_ref, s0r_ref, s0i_ref, cre_ref, cim_ref, cret_ref, cimt_ref,
             d_ref, wg_ref, bg_ref, do_ref, dy_ref, glr_ref, gli_ref, dcre_ref, dcim_ref, dd_ref, dwg_ref, dbg_ref,
             xr, xi, dr, di, sr, si):
        @pl.when(pl.program_id(0) == 0)
        def _():
            for r in (glr_ref, gli_ref, dcre_ref, dcim_ref, dd_ref, dwg_ref, dbg_ref):
                r[...] = jnp.zeros_like(r)

        sr[...] = s0r_ref[0]
        si[...] = s0i_ref[0]
        _s5_bu(u_ref, bre_ref, bim_ref, xr, xi)
        _s5_scan(xr, xi, sr, si, ar_ref, ai_ref, rb // S5_SEG, True)
        uv = u_ref[...]
        y = _s5_y(xr, xi, u_ref, cre_ref, cim_ref, d_ref)
        z, gz = _gelu_parts(y)
        v = _bdot(z, wg_ref[...]) + bg_ref[...]
        sg = jax.nn.sigmoid(v)
        dov = do_ref[...].astype(f32)
        dv = dov * z * sg * (1.0 - sg)
        dz = dov * sg + _bdot(dv, wg_ref[...], NT)
        dy = dz * gz
        dy_ref[...] = dy
        dwg_ref[...] += _bdot(z, dv, TN)
        dbg_ref[...] += jnp.sum(dv, axis=0, keepdims=True)
        dd_ref[...] += jnp.sum(dy * uv, axis=0, keepdims=True)
        for k in range(S5_UB):
            cs = slice(k * S5_LC, (k + 1) * S5_LC)
            dyk = dy[:, k * LANES:(k + 1) * LANES]
            dcre_ref[k] += _bdot(xr[:, cs], dyk, TN)
            dcim_ref[k] -= _bdot(xi[:, cs], dyk, TN)
            dr[:, cs] = _bdot(dyk, cret_ref[k])
            di[:, cs] = -_bdot(dyk, cimt_ref[k])
        _s5_rscan(dr, di, None, None, None, None, glr_ref, gli_ref, None, None, ar_ref, ai_ref, rb // S5_SEG)

    wspec = pl.BlockSpec((4 * LANES, 4 * LANES), lambda i: (0, 0))
    return pl.pallas_call(
        body, name="s5_bwd_a", grid=(sp["nblk"],),
        in_specs=[sp["tok"], sp["bmat"], sp["bmat"], sp["avec"], sp["avec"], sp["s0"], sp["s0"], sp["cmat"], sp["cmat"],
                  sp["bmat"], sp["bmat"], sp["cvec"], wspec, sp["cvec"], sp["tok"]],
        out_specs=[sp["tok"], sp["seg"], sp["seg"], sp["cmat"], sp["cmat"], sp["cvec"], wspec, sp["cvec"]],
        out_shape=[jax.ShapeDtypeStruct((T, 4 * LANES), f32)] + [jax.ShapeDtypeStruct((S5_SEG, S5_N), f32)] * 2
        + [jax.ShapeDtypeStruct((S5_UB, S5_LC, LANES), f32)] * 2
        + [jax.ShapeDtypeStruct((1, 4 * LANES), f32), jax.ShapeDtypeStruct((4 * LANES, 4 * LANES), f32),
           jax.ShapeDtypeStruct((1, 4 * LANES), f32)],
        scratch_shapes=[pltpu.VMEM((rb, S5_N), f32)] * 4 + [pltpu.VMEM((S5_SEG, S5_N), f32)] * 2,
        compiler_params=_cp("arbitrary"),
    )(u, bre, bim, ar, ai, s0r, s0i, cre, cim, cret, cimt, dsk, wg, bg, dout)


def _s5_bwd_b(u, bre, bim, bret, bimt, ar, ai, s0r, s0i, glr, gli, cret, cimt, dsk, dy, *, rb):
    T = u.shape[0]
    sp = _s5_specs(T, rb, rev=True)
    seg_len = T // S5_SEG
    nblk = sp["nblk"]

    def body(u_ref, bre_ref, bim_ref, bret_ref, bimt_ref, ar_ref, ai_ref, s0r_ref, s0i_ref, glr_ref, gli_ref,
             cret_ref, cimt_ref, d_ref, dy_ref, du_ref, dbre_ref, dbim_ref, dar_ref, dai_ref,
             xr, xi, dr, di, sr, si, gr, gi, acc_r, acc_i):
        @pl.when(pl.program_id(0) == 0)
        def _():
            x_r, x_i = _s5_seg_carry(glr_ref[...], gli_ref[...], ar_ref[...], ai_ref[...], seg_len, True)
            gr[...] = x_r
            gi[...] = x_i
            acc_r[...] = jnp.zeros_like(acc_r)
            acc_i[...] = jnp.zeros_like(acc_i)
            dbre_ref[...] = jnp.zeros_like(dbre_ref)
            dbim_ref[...] = jnp.zeros_like(dbim_ref)

        sr[...] = s0r_ref[0]
        si[...] = s0i_ref[0]
        _s5_bu(u_ref, bre_ref, bim_ref, xr, xi)
        _s5_scan(xr, xi, sr, si, ar_ref, ai_ref, rb // S5_SEG, True)
        dy = dy_ref[...]
        for k in range(S5_UB):
            cs = slice(k * S5_LC, (k + 1) * S5_LC)
            dyk = dy[:, k * LANES:(k + 1) * LANES]
            dr[:, cs] = _bdot(dyk, cret_ref[k])
            di[:, cs] = -_bdot(dyk, cimt_ref[k])
        sr[...] = s0r_ref[0]
        si[...] = s0i_ref[0]
        _s5_rscan(dr, di, xr, xi, sr, si, gr, gi, acc_r, acc_i, ar_ref, ai_ref, rb // S5_SEG)
        dus = []
        for k in range(S5_UB):
            cs = slice(k * S5_LC, (k + 1) * S5_LC)
            uk = u_ref[:, k * LANES:(k + 1) * LANES]
            dbre_ref[k] += _bdot(uk, dr[:, cs], TN)
            dbim_ref[k] += _bdot(uk, di[:, cs], TN)
            dus.append(_bdot(dr[:, cs], bret_ref[k]) + _bdot(di[:, cs], bimt_ref[k]))
        du_ref[...] = (jnp.concatenate(dus, axis=1) + d_ref[...] * dy).astype(bf16)

        @pl.when(pl.program_id(0) == nblk - 1)
        def _():
            dar_ref[...] = jnp.sum(acc_r[...], axis=0, keepdims=True)
            dai_ref[...] = jnp.sum(acc_i[...], axis=0, keepdims=True)

    return pl.pallas_call(
        body, name="s5_bwd_b", grid=(nblk,),
        in_specs=[sp["tok"], sp["bmat"], sp["bmat"], sp["cmat"], sp["cmat"], sp["avec"], sp["avec"], sp["s0"], sp["s0"],
                  sp["seg"], sp["seg"], sp["bmat"], sp["bmat"], sp["cvec"], sp["tok"]],
        out_specs=[sp["tok"], sp["bmat"], sp["bmat"], sp["avec"], sp["avec"]],
        out_shape=[jax.ShapeDtypeStruct((T, 4 * LANES), bf16)] + [jax.ShapeDtypeStruct((S5_UB, LANES, S5_LC), f32)] * 2
        + [jax.ShapeDtypeStruct((1, S5_N), f32)] * 2,
        scratch_shapes=[pltpu.VMEM((rb, S5_N), f32)] * 4 + [pltpu.VMEM((S5_SEG, S5_N), f32)] * 6,
        compiler_params=_cp("arbitrary"),
    )(u, bre, bim, bret, bimt, ar, ai, s0r, s0i, glr, gli, cret, cimt, dsk, dy)


def _blockdiag(w, transpose=False):
    if transpose:
        w = jnp.swapaxes(w, 1, 2)
    g, a, b = w.shape
    eye = jnp.eye(8, dtype=w.dtype)
    return jnp.einsum("kgab,gj->kgajb", w.reshape(4, 8, a, b), eye).reshape(4, 8 * a, 8 * b)


def _blockdiag_t(m, a, b):
    eye = jnp.eye(8, dtype=m.dtype)
    return jnp.einsum("kgajb,gj->kgab", m.reshape(4, 8, a, 8, b), eye).reshape(32, a, b)


ROT = MLA_ROPE // 2


def _rope_tables(positions):
    freqs = ROPE_THETA ** (-jnp.arange(0, MLA_ROPE, 2, dtype=f32) / MLA_ROPE)
    ang = positions.astype(f32)[:, None] * freqs
    cos, sin, z = jnp.cos(ang), jnp.sin(ang), jnp.zeros_like(ang)
    return (jnp.concatenate([cos, cos, z, z], axis=1), jnp.concatenate([-sin, z, z, z], axis=1),
            jnp.concatenate([z, sin, z, z], axis=1))


def _rot(x, c, sa, sb):
    return x * c + pltpu.roll(x, LANES - ROT, 1) * sa + pltpu.roll(x, ROT, 1) * sb


def _rot_t(dy, c, sa, sb):
    return dy * c + pltpu.roll(dy * sa, ROT, 1) + pltpu.roll(dy * sb, LANES - ROT, 1)


def _rms(xv, g):
    return xv * lax.rsqrt(jnp.mean(xv * xv, axis=-1, keepdims=True) + EPS) * g


QW, KVW = MLA_Q_RANK, MLA_KV_RANK
ODD_PAD = QW + KVW + LANES


def _mla_prep_fwd(proj, qg, kvg, tabs, *, tm=512):
    T = proj.shape[0]
    tm = _tile(T, tm)

    def body(p_ref, qg_ref, kvg_ref, c_ref, sa_ref, sb_ref, cq_ref, ckv_ref, kr_ref):
        cq_ref[...] = _rms(p_ref[:, :QW], qg_ref[...]).astype(bf16)
        ckv_ref[...] = _rms(p_ref[:, QW:QW + KVW], kvg_ref[...]).astype(bf16)
        kr_ref[...] = _rot(p_ref[:, QW + KVW:], c_ref[...], sa_ref[...], sb_ref[...]).astype(bf16)

    row = lambda w: pl.BlockSpec((tm, w), lambda i: (i, 0))
    vec = lambda w: pl.BlockSpec((1, w), lambda i: (0, 0))
    return pl.pallas_call(
        body, name="mla_prep_fwd", grid=(T // tm,),
        in_specs=[row(ODD_PAD), vec(QW), vec(KVW), row(LANES), row(LANES), row(LANES)],
        out_specs=[row(QW), row(KVW), row(LANES)],
        out_shape=[jax.ShapeDtypeStruct((T, QW), bf16), jax.ShapeDtypeStruct((T, KVW), bf16),
                   jax.ShapeDtypeStruct((T, LANES), bf16)],
        compiler_params=_cp("parallel"),
    )(proj, qg, kvg, *tabs)


def _mla_prep_bwd(proj, qg, kvg, tabs, dcqn, dckvn, dkr_heads, *, tm=512):
    T = proj.shape[0]
    tm = _tile(T, tm)

    def body(p_ref, qg_ref, kvg_ref, c_ref, sa_ref, sb_ref, dcq_ref, dckv_ref, dkr_ref, dp_ref, dqg_ref, dkvg_ref):
        dcq, dqg = _rms_bwd_math(p_ref[:, :QW], qg_ref[...], dcq_ref[...])
        dckv, dkvg = _rms_bwd_math(p_ref[:, QW:QW + KVW], kvg_ref[...], dckv_ref[...])
        dk = dkr_ref[:, :LANES]
        for h in range(1, MLA_HEADS):
            dk = dk + dkr_ref[:, h * LANES:(h + 1) * LANES]
        dkr = _rot_t(dk, c_ref[...], sa_ref[...], sb_ref[...])
        dp_ref[...] = jnp.concatenate([dcq, dckv, dkr], axis=1).astype(bf16)

        @pl.when(pl.program_id(0) == 0)
        def _():
            dqg_ref[...] = dqg
            dkvg_ref[...] = dkvg

        @pl.when(pl.program_id(0) > 0)
        def _():
            dqg_ref[...] += dqg
            dkvg_ref[...] += dkvg

    row = lambda w: pl.BlockSpec((tm, w), lambda i: (i, 0))
    vec = lambda w: pl.BlockSpec((1, w), lambda i: (0, 0))
    return pl.pallas_call(
        body, name="mla_prep_bwd", grid=(T // tm,),
        in_specs=[row(ODD_PAD), vec(QW), vec(KVW), row(LANES), row(LANES), row(LANES), row(QW), row(KVW),
                  row(MLA_HEADS * LANES)],
        out_specs=[row(ODD_PAD), vec(QW), vec(KVW)],
        out_shape=[jax.ShapeDtypeStruct((T, ODD_PAD), bf16), jax.ShapeDtypeStruct((1, QW), f32),
                   jax.ShapeDtypeStruct((1, KVW), f32)],
        compiler_params=_cp("arbitrary"),
    )(proj, qg, kvg, *tabs, dcqn, dckvn, dkr_heads)


HQ = 2 * LANES
QK_SCALE = MLA_QK ** -0.5


def _q_post(q, tabs, *, transpose, name, tm=512):
    T = q.shape[0]
    tm = _tile(T, tm)

    def body(q_ref, c_ref, sa_ref, sb_ref, o_ref):
        c, sa, sb = c_ref[...], sa_ref[...], sb_ref[...]
        for h in range(MLA_HEADS):
            nope, rope = pl.ds(h * HQ, LANES), pl.ds(h * HQ + LANES, LANES)
            o_ref[:, nope] = (q_ref[:, nope].astype(f32) * QK_SCALE).astype(bf16)
            o_ref[:, rope] = ((_rot_t if transpose else _rot)(q_ref[:, rope].astype(f32), c, sa, sb) * QK_SCALE).astype(bf16)

    tab = pl.BlockSpec((tm, LANES), lambda i: (i, 0))
    blk = pl.BlockSpec((tm, MLA_HEADS * HQ), lambda i: (i, 0))
    return pl.pallas_call(
        body, name=name, grid=(T // tm,), in_specs=[blk, tab, tab, tab], out_specs=blk,
        out_shape=jax.ShapeDtypeStruct(q.shape, bf16), compiler_params=_cp("parallel"),
    )(q, *tabs)


def _causal_mask(i, j, tq, tk):
    r = lax.broadcasted_iota(jnp.int32, (tq, tk), 0) + i * tq
    c = lax.broadcasted_iota(jnp.int32, (tq, tk), 1) + j * tk
    return c <= r


def _flash_fwd(q, kv, kr, *, tq=1024, tk=512):
    T = q.shape[0]
    tq = _tile(T, tq)
    tk = _tile(tq, tk)
    per = tq // tk
    H = MLA_HEADS

    def body(q_ref, kn_ref, v_ref, kr_ref, o_ref, lse_ref, m_s, acc):
        i, j = pl.program_id(1), pl.program_id(2)
        last = (i + 1) * per - 1

        @pl.when(j == 0)
        def _():
            m_s[...] = jnp.full_like(m_s, -jnp.inf)
            acc[...] = jnp.zeros_like(acc)

        def step(masked):
            k = jnp.concatenate([kn_ref[...], kr_ref[...]], axis=1)
            s = _dot(q_ref[...], k, NT)
            if masked:
                s = jnp.where(_causal_mask(i, j, tq, tk), s, -jnp.inf)
            m_new = jnp.maximum(m_s[...], jnp.max(s, axis=-1, keepdims=True))
            alpha = jnp.exp(m_s[...] - m_new)
            p = jnp.exp((s - m_new).astype(bf16))
            v1 = jnp.concatenate([v_ref[...], jnp.ones((tk, LANES), bf16)], axis=1)
            acc[...] = alpha * acc[...] + _dot(p, v1)
            m_s[...] = m_new

        pl.when(j < i * per)(functools.partial(step, False))
        pl.when((j >= i * per) & (j <= last))(functools.partial(step, True))

        @pl.when(j == last)
        def _():
            l = acc[:, LANES:]
            o_ref[...] = (acc[:, :LANES] / l).astype(bf16)
            lse_ref[0] = m_s[...] + jnp.log(jnp.max(l, axis=-1, keepdims=True))

    kj = lambda i, j: jnp.minimum(j, (i + 1) * per - 1)
    kblk = lambda off: pl.BlockSpec((tk, LANES), lambda h, i, j: (kj(i, j), 2 * h + off))
    return pl.pallas_call(
        body, name="flash_fwd", grid=(H, T // tq, T // tk),
        in_specs=[pl.BlockSpec((tq, HQ), lambda h, i, j: (i, h)), kblk(0), kblk(1),
                  pl.BlockSpec((tk, LANES), lambda h, i, j: (kj(i, j), 0))],
        out_specs=[pl.BlockSpec((tq, LANES), lambda h, i, j: (i, h)), pl.BlockSpec((1, tq, 1), lambda h, i, j: (h, i, 0))],
        out_shape=[jax.ShapeDtypeStruct((T, H * LANES), bf16), jax.ShapeDtypeStruct((H, T, 1), f32)],
        scratch_shapes=[pltpu.VMEM((tq, 1), f32), pltpu.VMEM((tq, 2 * LANES), f32)],
        compiler_params=_cp("parallel", "parallel", "arbitrary"),
    )(q, kv, kv, kr)


def _flash_bwd(q, kv, kr, o, do, lse, *, tb=512):
    T = q.shape[0]
    tb = _tile(T, tb)
    nb = T // tb
    H = MLA_HEADS

    def body(q_ref, kn_ref, v_ref, kr_ref, o_ref, do_ref, lse_ref, dkv_ref, dkr_ref, dq_ref, dk_acc, dv_acc):
        j, ii = pl.program_id(1), pl.program_id(2)
        i = jnp.maximum(ii, j)

        @pl.when((j == 0) & (ii == 0))
        def _():
            dq_ref[...] = jnp.zeros_like(dq_ref)

        @pl.when(ii == 0)
        def _():
            dk_acc[...] = jnp.zeros_like(dk_acc)
            dv_acc[...] = jnp.zeros_like(dv_acc)

        def step(masked):
            k = jnp.concatenate([kn_ref[...], kr_ref[...]], axis=1)
            p = jnp.exp((_dot(q_ref[...], k, NT) - lse_ref[0]).astype(bf16))
            if masked:
                p = jnp.where(_causal_mask(i, j, tb, tb), p, jnp.zeros_like(p))
            delta = jnp.sum(o_ref[...].astype(f32) * do_ref[...], axis=-1, keepdims=True)
            ds = p * (_bdot(do_ref[...], v_ref[...], NT) - delta).astype(bf16)
            dv_acc[...] += _bdot(p, do_ref[...], TN)
            dk_acc[...] += _bdot(ds, q_ref[...], TN)
            rows = pl.ds(pl.multiple_of(i * tb, tb), tb)
            dq_ref[rows, :] += _bdot(ds, k)

        pl.when(ii > j)(functools.partial(step, False))
        pl.when(ii == j)(functools.partial(step, True))

        @pl.when(ii == nb - 1)
        def _():
            dkv_ref[...] = jnp.concatenate([dk_acc[:, :LANES], dv_acc[...]], axis=1).astype(bf16)
            dkr_ref[...] = dk_acc[:, LANES:]

    qi = lambda h, j, i: jnp.maximum(i, j)
    kblk = lambda off: pl.BlockSpec((tb, LANES), lambda h, j, i: (j, 2 * h + off))
    vec = pl.BlockSpec((1, tb, 1), lambda h, j, i: (h, qi(h, j, i), 0))
    qblk = pl.BlockSpec((tb, LANES), lambda h, j, i: (qi(h, j, i), h))
    return pl.pallas_call(
        body, name="flash_bwd", grid=(H, nb, nb),
        in_specs=[pl.BlockSpec((tb, HQ), lambda h, j, i: (qi(h, j, i), h)), kblk(0), kblk(1),
                  pl.BlockSpec((tb, LANES), lambda h, j, i: (j, 0)), qblk, qblk, vec],
        out_specs=[pl.BlockSpec((tb, HQ), lambda h, j, i: (j, h)), pl.BlockSpec((tb, LANES), lambda h, j, i: (j, h)),
                   pl.BlockSpec((T, HQ), lambda h, j, i: (0, h))],
        out_shape=[jax.ShapeDtypeStruct((T, H * HQ), bf16), jax.ShapeDtypeStruct((T, H * LANES), f32),
                   jax.ShapeDtypeStruct((T, H * HQ), f32)],
        scratch_shapes=[pltpu.VMEM((tb, HQ), f32), pltpu.VMEM((tb, LANES), f32)],
        compiler_params=_cp("parallel", "arbitrary", "arbitrary"),
    )(q, kv, kv, kr, o, do, lse)


HBM_SPEC = pl.BlockSpec(memory_space=pltpu.HBM)
N_CHIPS = 4
N_DEV = 8

BIG = {"even_w_in": 1, "s5_w_glu": 0, "even_w_out": 0, "odd_w_in": 0, "mla_w_uq": 1, "mla_w_ukv": 1, "odd_w_out": 0,
       "ffn_w_in": 2, "ffn_w_out": 1}
LAYERED = ("ffn_w_in", "ffn_w_out")
GROUPS = {"even_in": ("even_w_in",), "even_rest": ("s5_w_glu", "even_w_out"), "ffn0": LAYERED,
          "odd": ("odd_w_in", "mla_w_uq", "mla_w_ukv", "odd_w_out"), "ffn1": LAYERED}
GROUP_LAYER = {"ffn0": 0, "ffn1": 1}


def _place():
    x, y, c = lax.axis_index("x"), lax.axis_index("y"), lax.axis_index("c")
    chips = [(1 - x, y), (x, 1 - y), (1 - x, 1 - y)]
    return x, y, c, chips


def _slab(ref, axis, k, size):
    start = pl.multiple_of(k * size, size if axis == 0 else LANES)
    idx = [slice(None)] * len(ref.shape)
    idx[axis] = pl.ds(start, size)
    return ref.at[tuple(idx)]


SEM_SPEC = pl.BlockSpec(memory_space=pltpu.SEMAPHORE)
ANY_SPEC = pl.BlockSpec(memory_space=pl.ANY)
EFFECT = pltpu.SideEffectType.DATAFLOW_SIDE_EFFECTING


def _hbm(a):
    return pltpu.with_memory_space_constraint(a, pltpu.HBM)


class _Gather:
    copies = 3

    def __init__(self, axis, size):
        self.axis, self.size = axis, size

    def view(self, land, kk):
        return _slab(land, self.axis, kk, self.size)

    def own(self, land, place):
        return self.view(land, 2 * place[0] + place[1])

    def sends(self, src, land, place):
        x, y, c, chips = place
        return [(self.own(land, place) if src is None else src, self.own(land, place), (*chip, c)) for chip in chips]

    def recvs(self, land, place):
        return [self.view(land, 2 * cx + cy) for cx, cy in place[3]]


class _Scatter:
    copies = 3

    def __init__(self, axis, size, layer=None):
        self.axis, self.size, self.layer = axis, size, layer

    def row(self, land, j):
        return land.at[j] if self.layer is None else land.at[j, self.layer]

    def sends(self, src, land, place):
        c, chips = place[2], place[3]
        return [(_slab(src, self.axis, 2 * cx + cy, self.size), self.row(land, j), (cx, cy, c))
                for j, (cx, cy) in enumerate(chips)]

    def recvs(self, land, place):
        return [self.row(land, j) for j in range(3)]


class _ToAll:
    copies = N_DEV - 1

    def __init__(self, size):
        self.size = size

    def sends(self, src, land, place):
        x, y, c, _ = place
        flip = lambda v, bit: 1 - v if bit else v
        own = _slab(land, 0, 4 * x + 2 * y + c, self.size)
        return [(own, own, (flip(x, m & 4), flip(y, m & 2), flip(c, m & 1))) for m in range(1, N_DEV)]

    def recvs(self, land, place):
        x, y, c, _ = place
        d = 4 * x + 2 * y + c
        return [_slab(land, 0, d ^ m, self.size) for m in range(1, N_DEV)]


def _unique(arrays):
    out, index = [], {}
    for a in arrays:
        if a is not None and id(a) not in index:
            index[id(a)] = len(out)
            out.append(a)
    return out, index


def _sem_base(routes):
    base = [0]
    for r in routes:
        base.append(base[-1] + r.copies)
    return base


def _push_start(name, items):
    n = len(items)
    base = _sem_base([it[0] for it in items])
    arrays, index = _unique([it[1] for it in items] + [it[2] for it in items])
    na = len(arrays)

    def body(*refs):
        arr, send, recv, token = refs[:na], refs[na], refs[na + 1], refs[-1]
        place = _place()
        for i, (route, src, land) in enumerate(items):
            s_ref = None if src is None else arr[index[id(src)]]
            for j, (s, d, dev) in enumerate(route.sends(s_ref, arr[index[id(land)]], place)):
                pltpu.make_async_remote_copy(src_ref=s, dst_ref=d, send_sem=send.at[base[i] + j], recv_sem=recv.at[base[i] + j],
                                             device_id=dev, device_id_type=MESH).start()
        token[...] = jnp.zeros_like(token)

    res = pl.pallas_call(
        body, name=name,
        out_shape=[pltpu.SemaphoreType.DMA((base[-1],)), pltpu.SemaphoreType.DMA((base[-1],))]
        + [pltpu.HBM(a.shape, a.dtype) for a in arrays] + [jax.ShapeDtypeStruct((SUBLANES, LANES), f32)],
        in_specs=[HBM_SPEC] * na, out_specs=[SEM_SPEC, SEM_SPEC] + [HBM_SPEC] * na + [pl.BlockSpec(memory_space=pltpu.VMEM)],
        input_output_aliases={i: 2 + i for i in range(na)},
        compiler_params=pltpu.CompilerParams(has_side_effects=EFFECT),
    )(*[_hbm(a) for a in arrays])
    thru = lambda a: None if a is None else res[2 + index[id(a)]]
    return (res[0], res[1]), [thru(it[1]) for it in items], [thru(it[2]) for it in items], res[-1]


def _push_wait(name, groups, after):
    arrays, index = _unique([a for _, _, srcs, lands in groups for a in list(srcs) + list(lands)])
    na, ng = len(arrays), len(groups)

    def body(*refs):
        arr, sems = refs[:na], refs[na:na + 2 * ng]
        place = _place()
        for g, (routes, _, srcs, lands) in enumerate(groups):
            send, recv = sems[2 * g], sems[2 * g + 1]
            base = _sem_base(routes)
            for i, route in enumerate(routes):
                src, land = None if srcs[i] is None else arr[index[id(srcs[i])]], arr[index[id(lands[i])]]
                for j, ((s, d, dev), mine) in enumerate(zip(route.sends(src, land, place), route.recvs(land, place))):
                    cp = pltpu.make_async_remote_copy(src_ref=s, dst_ref=mine, send_sem=send.at[base[i] + j],
                                                      recv_sem=recv.at[base[i] + j], device_id=dev,
                                                      device_id_type=MESH)
                    cp.wait_send()
                    cp.wait_recv()

    sem_args = [s for g in groups for s in g[1]]
    res = pl.pallas_call(
        body, name=name, out_shape=[pltpu.HBM(a.shape, a.dtype) for a in arrays],
        in_specs=[HBM_SPEC] * na + [SEM_SPEC] * (2 * ng) + [ANY_SPEC] * len(after), out_specs=[HBM_SPEC] * na,
        input_output_aliases={i: i for i in range(na)},
        compiler_params=pltpu.CompilerParams(has_side_effects=EFFECT),
    )(*arrays, *sem_args, *after)
    return [[res[index[id(a)]] for a in g[3]] for g in groups]


def _place_slab(block, axis, slabs, idx, dtype, *, name):
    R, C = block.shape
    tm = _rows(R, C)
    nr = R // tm
    out_map = (lambda i, k: (i, k[0])) if axis == 1 else (lambda i, k: (k[0] * nr + i, 0))

    def body(k_ref, x_ref, o_ref):
        o_ref[...] = x_ref[...].astype(dtype)

    full = (R, C * slabs) if axis == 1 else (R * slabs, C)
    return pl.pallas_call(
        body, name=name, out_shape=jax.ShapeDtypeStruct(full, dtype),
        grid_spec=pltpu.PrefetchScalarGridSpec(
            num_scalar_prefetch=1, grid=(nr,), in_specs=[pl.BlockSpec((tm, C), lambda i, k: (i, 0))],
            out_specs=pl.BlockSpec((tm, C), out_map)),
        compiler_params=_cp("parallel"),
    )(idx, block)


def _swap_with_sibling(parts, tag):
    names = list(parts)

    def body(*refs):
        n = len(names)
        ins, outs, send, recv = refs[:n], refs[n:2 * n], refs[-2], refs[-1]
        x, y, c, _ = _place()
        cps = [pltpu.make_async_remote_copy(src_ref=ins[a], dst_ref=outs[a], send_sem=send.at[a], recv_sem=recv.at[a],
                                            device_id=(x, y, 1 - c), device_id_type=MESH) for a in range(n)]
        for cp in cps:
            cp.start()
        for cp in cps:
            cp.wait_recv()
        for cp in cps:
            cp.wait_send()

    res = pl.pallas_call(
        body, name=f"swap_with_sibling_{tag}", in_specs=[HBM_SPEC] * len(names), out_specs=[HBM_SPEC] * len(names),
        out_shape=[jax.ShapeDtypeStruct(parts[n].shape, parts[n].dtype) for n in names],
        scratch_shapes=[pltpu.SemaphoreType.DMA((len(names),)), pltpu.SemaphoreType.DMA((len(names),))],
    )(*[parts[n] for n in names])
    return dict(zip(names, res))


ELEMENTWISE_BLOCK_BYTES = 1 << 20


def _rows(r, c):
    for t in (512, 256, 128, 64, 32, 16, 8):
        if r % t == 0 and t * c * 4 <= ELEMENTWISE_BLOCK_BYTES:
            return t
    return r


def _sum4(owns, axis, recv, kidx, *, name):
    L = len(owns)
    R, C = recv.shape[2:]
    tm = _rows(R, C)
    nr = R // tm

    def body(k_ref, *refs):
        own_refs, r_ref, out_ref = refs[:L], refs[L], refs[L + 1]
        for li in range(L):
            @pl.when(pl.program_id(0) == li)
            def _(o_ref=own_refs[li]):
                out_ref[...] = ((o_ref[...] + r_ref[0, 0].astype(f32)) + r_ref[1, 0].astype(f32)) + r_ref[2, 0].astype(f32)

    own_map = (lambda l, i, k: (i, k[0])) if axis == 1 else (lambda l, i, k: (k[0] * nr + i, 0))
    return pl.pallas_call(
        body, name=name, out_shape=jax.ShapeDtypeStruct((L * R, C), f32),
        grid_spec=pltpu.PrefetchScalarGridSpec(
            num_scalar_prefetch=1, grid=(L, nr),
            in_specs=[pl.BlockSpec((tm, C), own_map)] * L + [pl.BlockSpec((3, 1, tm, C), lambda l, i, k: (0, l, i, 0))],
            out_specs=pl.BlockSpec((tm, C), lambda l, i, k: (l * nr + i, 0))),
        compiler_params=_cp("parallel", "parallel"),
    )(kidx, *owns, recv)


def _adamw(w, m, v, parts, *, name):
    R, C = w.shape
    tm = _rows(R, C)
    npart = len(parts)

    def body(*refs):
        w_ref, m_ref, v_ref = refs[:3]
        g_ref, d_ref, m2_ref, v2_ref = refs[3 + npart:]
        g = refs[3][...]
        for p_ref in refs[4:3 + npart]:
            g = g + p_ref[...]
        g_ref[...] = g
        d_ref[...], m2_ref[...], v2_ref[...] = _adam_math(w_ref[...], m_ref[...], v_ref[...], g)

    blk = pl.BlockSpec((tm, C), lambda i: (i, 0))
    return pl.pallas_call(
        body, name=name, grid=(R // tm,),
        in_specs=[blk] * (3 + npart), out_specs=[blk] * 4,
        out_shape=[jax.ShapeDtypeStruct((R, C), f32)] * 4, compiler_params=_cp("parallel"),
    )(w, m, v, *parts)


def _adam_math(w, m, v, g):
    m2 = ADAM_B1 * m + (1.0 - ADAM_B1) * g
    v2 = ADAM_B2 * v + (1.0 - ADAM_B2) * (g * g)
    m_hat = m2 / (1.0 - ADAM_B1 ** ADAM_STEP)
    v_hat = v2 / (1.0 - ADAM_B2 ** ADAM_STEP)
    return -ADAM_LR * (m_hat / (jnp.sqrt(v_hat) + ADAM_EPS) + ADAM_WD * w), m2, v2


def _adamw_small(landed, w, m, v, kidx, ra, rb):
    rs = ra + N_CHIPS * rb

    def body(k_ref, l_ref, w_ref, m_ref, v_ref, g_ref, d_ref, m2_ref, v2_ref):
        mine = pl.multiple_of(ra + k_ref[0] * rb, SUBLANES)
        for lo, n, off in ((0, ra, 0), (ra, rb, mine)):
            g = l_ref[pl.ds(off, n), :]
            for d in range(1, N_DEV):
                g = g + l_ref[pl.ds(d * rs + off, n), :]
            rows = pl.ds(lo, n)
            delta, m2, v2 = _adam_math(w_ref[rows, :], m_ref[rows, :], v_ref[rows, :], g)
            g_ref[rows, :] = g
            d_ref[rows, :] = delta
            m2_ref[rows, :] = m2
            v2_ref[rows, :] = v2

    vmem = pl.BlockSpec(memory_space=pltpu.VMEM)
    return pl.pallas_call(
        body, name="adamw_small", out_shape=[jax.ShapeDtypeStruct(w.shape, f32)] * 4,
        grid_spec=pltpu.PrefetchScalarGridSpec(num_scalar_prefetch=1, grid=(), in_specs=[vmem] * 4, out_specs=[vmem] * 4),
        compiler_params=_cp(),
    )(kidx, landed, w, m, v)


def _pad_odd(w):
    return jnp.pad(w, ((0, 0), (0, ODD_PAD - w.shape[1])))


def _uq_cat(w):
    r = w.shape[0]
    return jnp.pad(w.reshape(r, MLA_HEADS, MLA_QK), ((0, 0), (0, 0), (0, HQ - MLA_QK))).reshape(r, MLA_HEADS * HQ)


def _uq_uncat(w):
    r = w.shape[0]
    return w.reshape(r, MLA_HEADS, HQ)[:, :, :MLA_QK].reshape(r, MLA_HEADS * MLA_QK)


def _to_segments(v):
    T, C = v.shape
    return v.reshape(S5_SEG, T // S5_SEG, C).transpose(1, 0, 2).reshape(T, C)


def _from_segments(v):
    T, C = v.shape
    return v.reshape(T // S5_SEG, S5_SEG, C).transpose(1, 0, 2).reshape(T, C)


def _s5_rb(T):
    return min(512, T)


def _ffn_fwd(h, g, w_in, cw, cb, w_out, tag):
    hn = _rms_fwd(h, g, name=f"ffn{tag}_norm")
    au = _mm(hn, w_in, name=f"ffn{tag}_in", tn=1408)
    z = _ffn_mid_fwd(au, cw, cb, name=f"ffn{tag}_mid")
    return _mm(z, w_out, res=h, name=f"ffn{tag}_out", tk=1408), (hn, au, z)


def _ffn_bwd(h, g, w_in, cw, cb, w_out, saved, dh, tag, dep=None):
    hn, au, z = saved
    dz = _mm(dh, w_out, tb=True, out_dtype=bf16, name=f"ffn{tag}_dz", tn=1408, dep=dep)
    dw_out = _mm(z, dh, ta=True, also_bf16=True, name=f"ffn{tag}_dwout", tm=1408)
    dau, dcw, dcb = _ffn_mid_bwd(au, cw, cb, dz, name=f"ffn{tag}_dmid")
    dhn = _mm(dau, w_in, tb=True, name=f"ffn{tag}_dhn", tk=1408)
    dw_in = _mm(hn, dau, ta=True, also_bf16=True, name=f"ffn{tag}_dwin", tn=1408)
    dh_in, dg = _rms_bwd(h, g, dhn, dh, name=f"ffn{tag}_dnorm")
    return dh_in, dg, dw_in, dcw, dcb, dw_out


def _local_step(x, positions, target, get_w, P, put_g):
    T = x.shape[0]
    rb = _s5_rb(T)
    row = lambda v: v.reshape(1, -1)
    g_mix, g_ffn = P["norm_mix_g"], P["norm_ffn_g"]
    lbl, hng = P["hgrn_lb_logits"], P["hgrn_norm_g"]
    dsk, bg = P["s5_d"], P["s5_b_glu"]
    qg, kvg = P["mla_q_norm_g"], P["mla_kv_norm_g"]
    cw, cb = P["ffn_conv_w"], P["ffn_conv_b"]

    col = lambda v: v.reshape(S5_N, 1)
    disc_in = (col(P["s5_a_re"]), col(P["s5_a_im"]), col(jnp.repeat(P["s5_log_dt"].reshape(S5_GROUPS), S5_STATE)),
               P["s5_b_re"].reshape(S5_N, S5_GROUP), P["s5_b_im"].reshape(S5_N, S5_GROUP))
    abr, abi, bbr, bbi = _s5_disc_fwd(*disc_in)
    ar, ai = abr.reshape(1, S5_N), abi.reshape(1, S5_N)
    bbr3, bbi3 = bbr.reshape(S5_GROUPS, S5_STATE, S5_GROUP), bbi.reshape(S5_GROUPS, S5_STATE, S5_GROUP)
    bre, bim = _blockdiag(bbr3, True).astype(bf16), _blockdiag(bbi3, True).astype(bf16)
    bret, bimt = _blockdiag(bbr3).astype(bf16), _blockdiag(bbi3).astype(bf16)
    c_re, c_im = P["s5_c_re"].reshape(S5_GROUPS, S5_GROUP, S5_STATE), P["s5_c_im"].reshape(S5_GROUPS, S5_GROUP, S5_STATE)
    cre, cim = _blockdiag(c_re, True).astype(bf16), _blockdiag(c_im, True).astype(bf16)
    cret, cimt = _blockdiag(c_re).astype(bf16), _blockdiag(c_im).astype(bf16)

    hn0 = _rms_fwd(x, g_mix[0:1], name="mix0_norm")
    We = get_w("even_in", hn0)
    proj_e = _mm(hn0, We["even_w_in"], name="even_in", tn=1280)
    Wr = get_w("even_rest", proj_e)
    ya, states = _hgrn_fwd(proj_e, lbl, hng)
    u_seg = _to_segments(proj_e[:, 4 * 512:])
    fr, fi = _s5_final(u_seg, bre, bim, ar, ai, rb=rb)
    yb_seg, s0r, s0i = _s5_fwd(u_seg, bre, bim, ar, ai, fr, fi, cre, cim, dsk, Wr["s5_w_glu"], bg, rb=rb)
    ycat = jnp.concatenate([ya, _from_segments(yb_seg)], axis=1)
    h1 = _mm(ycat, Wr["even_w_out"], res=x, name="even_out")
    Wf0 = get_w("ffn0", h1)
    h2, ffn0 = _ffn_fwd(h1, g_ffn[0:1], Wf0["ffn_w_in"], cw[0], cb[0:1], Wf0["ffn_w_out"], 0)

    tabs = _rope_tables(positions)
    hn2 = _rms_fwd(h2, g_mix[1:2], name="mix1_norm")
    Wo = get_w("odd", hn2)
    proj_o = _mm(hn2, Wo["odd_w_in"], name="odd_in")
    cqn, ckvn, kr = _mla_prep_fwd(proj_o, qg, kvg, tabs)
    q = _q_post(_mm(cqn, Wo["mla_w_uq"], name="mla_uq"), tabs, transpose=False, name="q_post")
    kvb = _mm(ckvn, Wo["mla_w_ukv"], out_dtype=bf16, name="mla_ukv")
    o, lse = _flash_fwd(q, kvb, kr)
    h3 = _mm(o, Wo["odd_w_out"], res=h2, name="odd_out")
    Wf1 = get_w("ffn1", h3)
    h4, ffn1 = _ffn_fwd(h3, g_ffn[1:2], Wf1["ffn_w_in"], cw[1], cb[1:2], Wf1["ffn_w_out"], 1)
    loss, dh4, dg_final = _loss_head(h4, row(P["final_norm_g"]), target)

    dh3, dg_ffn1, dw_fin1, dcw1, dcb1, dw_fout1 = _ffn_bwd(
        h3, g_ffn[1:2], Wf1["ffn_w_in"], cw[1], cb[1:2], Wf1["ffn_w_out"], ffn1, dh4, 1)
    sent = put_g("ffn1", {"ffn_w_in": dw_fin1, "ffn_w_out": dw_fout1})
    do = _mm(dh3, Wo["odd_w_out"], tb=True, name="odd_do", dep=sent)
    dw_oout = _mm(o, dh3, ta=True, also_bf16=True, name="odd_dwout")
    dkv, dkr_h, dq = _flash_bwd(q, kvb, kr, o, do, lse)
    dq = _q_post(dq, tabs, transpose=True, name="dq_post")
    dw_uq = _mm(cqn, dq, ta=True, also_bf16=True, name="mla_dwuq")
    dcqn = _mm(dq, Wo["mla_w_uq"], tb=True, name="mla_dcq")
    dw_ukv = _mm(ckvn, dkv, ta=True, also_bf16=True, name="mla_dwukv")
    dckvn = _mm(dkv, Wo["mla_w_ukv"], tb=True, name="mla_dckv")
    dproj_o, dqg, dkvg = _mla_prep_bwd(proj_o, qg, kvg, tabs, dcqn, dckvn, dkr_h)
    dhn2 = _mm(dproj_o, Wo["odd_w_in"], tb=True, name="odd_dhn")
    dw_oin = _mm(hn2, dproj_o, ta=True, also_bf16=True, name="odd_dwin")
    sent = put_g("odd", {"odd_w_in": dw_oin, "mla_w_uq": dw_uq, "mla_w_ukv": dw_ukv, "odd_w_out": dw_oout})
    dh2, dg_mix1 = _rms_bwd(h2, g_mix[1:2], dhn2, dh3, name="mix1_dnorm")

    dh1, dg_ffn0, dw_fin0, dcw0, dcb0, dw_fout0 = _ffn_bwd(
        h1, g_ffn[0:1], Wf0["ffn_w_in"], cw[0], cb[0:1], Wf0["ffn_w_out"], ffn0, dh2, 0, dep=sent)
    sent = put_g("ffn0", {"ffn_w_in": dw_fin0, "ffn_w_out": dw_fout0})
    dycat = _mm(dh1, Wr["even_w_out"], tb=True, name="even_dy", dep=sent)
    dw_eout = _mm(ycat, dh1, ta=True, also_bf16=True, name="even_dwout")
    dq_h, df_h, di_h, dg_h, dlbl, dhng = _hgrn_bwd(proj_e, lbl, hng, states, dycat)
    dyb_seg = _to_segments(dycat[:, 512:])
    dy_s5, glr, gli, dcre, dcim, dd, dwg, dbg = _s5_bwd_a(
        u_seg, bre, bim, ar, ai, s0r, s0i, cre, cim, cret, cimt, dsk, Wr["s5_w_glu"], bg, dyb_seg, rb=rb)
    du_seg, dbre, dbim, dar, dai = _s5_bwd_b(
        u_seg, bre, bim, bret, bimt, ar, ai, s0r, s0i, glr, gli, cret, cimt, dsk, dy_s5, rb=rb)
    dproj_e = jnp.concatenate([dq_h, df_h, di_h, dg_h, _from_segments(du_seg)], axis=1)
    dhn0 = _mm(dproj_e, We["even_w_in"], tb=True, name="even_dhn", tk=1280)
    dw_ein = _mm(hn0, dproj_e, ta=True, also_bf16=True, name="even_dwin", tn=1280)
    dx, dg_mix0 = _rms_bwd(x, g_mix[0:1], dhn0, dh1, name="mix0_dnorm")

    unblk = lambda m, a, b: jnp.swapaxes(_blockdiag_t(m, a, b), 1, 2)
    dbbr = unblk(dbre, S5_GROUP, S5_STATE).reshape(S5_N, S5_GROUP)
    dbbi = unblk(dbim, S5_GROUP, S5_STATE).reshape(S5_N, S5_GROUP)
    d_ar, d_ai, d_ldt, d_br, d_bi = _s5_disc_bwd(*disc_in, (dar.reshape(S5_N, 1), dai.reshape(S5_N, 1), dbbr, dbbi))
    small = {
        "norm_mix_g": jnp.concatenate([dg_mix0, dg_mix1], axis=0),
        "norm_ffn_g": jnp.concatenate([dg_ffn0, dg_ffn1], axis=0),
        "final_norm_g": dg_final.reshape(-1),
        "hgrn_lb_logits": dlbl, "hgrn_norm_g": dhng,
        "s5_a_re": d_ar.reshape(1, S5_GROUPS, S5_STATE), "s5_a_im": d_ai.reshape(1, S5_GROUPS, S5_STATE),
        "s5_log_dt": d_ldt.reshape(S5_GROUPS, S5_STATE).sum(axis=1).reshape(1, S5_GROUPS),
        "s5_b_re": d_br.reshape(1, S5_GROUPS, S5_STATE, S5_GROUP), "s5_b_im": d_bi.reshape(1, S5_GROUPS, S5_STATE, S5_GROUP),
        "s5_c_re": unblk(dcre, S5_STATE, S5_GROUP).reshape(1, S5_GROUPS, S5_GROUP, S5_STATE),
        "s5_c_im": unblk(dcim, S5_STATE, S5_GROUP).reshape(1, S5_GROUPS, S5_GROUP, S5_STATE),
        "s5_d": dd, "s5_b_glu": dbg, "mla_q_norm_g": dqg, "mla_kv_norm_g": dkvg,
        "ffn_conv_w": jnp.stack([dcw0, dcw1]), "ffn_conv_b": jnp.concatenate([dcb0, dcb1], axis=0),
    }
    put_g("even", {"even_w_in": dw_ein, "s5_w_glu": (dwg, dwg.astype(bf16)), "even_w_out": dw_eout}, small)
    return loss, dx


WEIGHTS = ["norm_mix_g", "norm_ffn_g", "final_norm_g", "even_w_in", "hgrn_lb_logits", "hgrn_norm_g", "s5_a_re", "s5_a_im",
           "s5_log_dt", "s5_b_re", "s5_b_im", "s5_c_re", "s5_c_im", "s5_d", "s5_w_glu", "s5_b_glu", "even_w_out", "odd_w_in",
           "mla_q_norm_g", "mla_w_uq", "mla_kv_norm_g", "mla_w_ukv", "odd_w_out", "ffn_w_in", "ffn_conv_w", "ffn_conv_b",
           "ffn_w_out"]
SMALL_SHARDED = {"mla_q_norm_g": 1, "mla_kv_norm_g": 1, "ffn_conv_w": 2}
SMALL = [n for n in WEIGHTS if n not in BIG]
SMALL_REP = [n for n in SMALL if n not in SMALL_SHARDED]


def _pack_rows(shapes):
    n = sum(math.prod(s) for s in shapes)
    return -(-n // (SUBLANES * LANES)) * SUBLANES


def _pack(arrays, rows):
    flat = jnp.concatenate([a.reshape(-1) for a in arrays])
    return jnp.pad(flat, (0, rows * LANES - flat.shape[0])).reshape(rows, LANES)


def _unpack(block, shapes):
    flat, out, off = block.reshape(-1), [], 0
    for s in shapes:
        n = math.prod(s)
        out.append(flat[off:off + n].reshape(s))
        off += n
    return out


def kernel(x, positions, norm_mix_g, norm_ffn_g, final_norm_g, even_w_in, hgrn_lb_logits, hgrn_norm_g, s5_a_re, s5_a_im, s5_log_dt, s5_b_re, s5_b_im, s5_c_re, s5_c_im, s5_d, s5_w_glu, s5_b_glu, even_w_out, odd_w_in, mla_q_norm_g, mla_w_uq, mla_kv_norm_g, mla_w_ukv, odd_w_out, ffn_w_in, ffn_conv_w, ffn_conv_b, ffn_w_out, loss_target, m_norm_mix_g, m_norm_ffn_g, m_final_norm_g, m_even_w_in, m_hgrn_lb_logits, m_hgrn_norm_g, m_s5_a_re, m_s5_a_im, m_s5_log_dt, m_s5_b_re, m_s5_b_im, m_s5_c_re, m_s5_c_im, m_s5_d, m_s5_w_glu, m_s5_b_glu, m_even_w_out, m_odd_w_in, m_mla_q_norm_g, m_mla_w_uq, m_mla_kv_norm_g, m_mla_w_ukv, m_odd_w_out, m_ffn_w_in, m_ffn_conv_w, m_ffn_conv_b, m_ffn_w_out, v_norm_mix_g, v_norm_ffn_g, v_final_norm_g, v_even_w_in, v_hgrn_lb_logits, v_hgrn_norm_g, v_s5_a_re, v_s5_a_im, v_s5_log_dt, v_s5_b_re, v_s5_b_im, v_s5_c_re, v_s5_c_im, v_s5_d, v_s5_w_glu, v_s5_b_glu, v_even_w_out, v_odd_w_in, v_mla_q_norm_g, v_mla_w_uq, v_mla_kv_norm_g, v_mla_w_ukv, v_odd_w_out, v_ffn_w_in, v_ffn_conv_w, v_ffn_conv_b, v_ffn_w_out):
    args = dict(locals())
    w = {n: args[n] for n in WEIGHTS}
    m = {n: args["m_" + n] for n in WEIGHTS}
    v = {n: args["v_" + n] for n in WEIGHTS}
    k = 2 * lax.axis_index("x") + lax.axis_index("y")
    kidx = k.reshape(1).astype(jnp.int32)
    axis2d = lambda n: BIG[n] - (1 if n in LAYERED else 0)
    slab = lambda n: w[n].shape[1 + axis2d(n)]

    small_sh_shapes = [w[n].shape for n in SMALL_SHARDED]
    rb = _pack_rows(small_sh_shapes)
    items = {}
    for group, names in GROUPS.items():
        layer = GROUP_LAYER.get(group, 0)
        items[group] = [(_Gather(axis2d(n), slab(n)), None,
                         _place_slab(w[n][layer], axis2d(n), N_CHIPS, kidx, bf16, name=f"place_{n}_{layer}")) for n in names]
    items["even_in"].append((_Gather(0, rb), None,
                             _place_slab(_pack([w[n] for n in SMALL_SHARDED], rb), 0, N_CHIPS, kidx, f32, name="place_small")))
    gathers, tokens = {}, []
    for group in GROUPS:
        sems, srcs, lands, token = _push_start(f"gather_start_{group}", items[group])
        gathers[group] = ([it[0] for it in items[group]], sems, srcs, lands)
        tokens.append(token[0, 0])
    started = functools.reduce(jnp.add, tokens)

    def landed(group, after):
        return _push_wait(f"gather_wait_{group}", [gathers[group]], [after])[0]

    even = landed("even_in", (started + norm_mix_g[0, 0]).reshape(1))
    per_chip = [_unpack(even[-1][c * rb:(c + 1) * rb], small_sh_shapes) for c in range(N_CHIPS)]
    P = {n: w[n] for n in SMALL_REP}
    for i, (n, ax) in enumerate(SMALL_SHARDED.items()):
        P[n] = jnp.concatenate([per_chip[c][i] for c in range(N_CHIPS)], axis=ax)
    P["mla_q_norm_g"], P["mla_kv_norm_g"] = P["mla_q_norm_g"].reshape(1, -1), P["mla_kv_norm_g"].reshape(1, -1)
    fix_w = {"odd_w_in": _pad_odd, "mla_w_uq": _uq_cat}

    def get_w(group, after):
        full = even if group == "even_in" else landed(group, after)
        return {n: fix_w.get(n, lambda a: a)(a) for n, a in zip(GROUPS[group], full)}

    fix_g = {"odd_w_in": lambda g: g[:, :odd_w_in.shape[2]], "mla_w_uq": _uq_uncat}
    g32, scatters, land_now = {}, {}, {}
    ra = _pack_rows([w[n].shape for n in SMALL_REP])
    rs = ra + N_CHIPS * rb
    didx = (2 * kidx + lax.axis_index("c")).astype(jnp.int32)

    def put_g(group, grads, small=None):
        layer = GROUP_LAYER.get(group)
        routes, srcs, names = [], [], list(grads)
        for n in names:
            f = fix_g.get(n, lambda g: g)
            g32.setdefault(n, {})[layer or 0] = f(grads[n][0])
            routes.append(_Scatter(axis2d(n), slab(n), layer if n in LAYERED else None))
            srcs.append(f(grads[n][1]))
            if n not in land_now:
                land_now[n] = lax.empty((3,) + w[n].shape[0 if n in LAYERED else 1:], bf16)
        if small is not None:
            blocks = [_pack([small[n] for n in SMALL_REP], ra)]
            for chip in range(N_CHIPS):
                sl = lambda n, ax: lax.slice_in_dim(small[n].reshape(w[n].shape[:ax] + (-1,) + w[n].shape[ax + 1:]),
                                                    chip * w[n].shape[ax], (chip + 1) * w[n].shape[ax], axis=ax)
                blocks.append(_pack([sl(n, ax) for n, ax in SMALL_SHARDED.items()], rb))
            names.append("small")
            routes.append(_ToAll(rs))
            srcs.append(None)
            land_now["small"] = _place_slab(jnp.concatenate(blocks), 0, N_DEV, didx, f32, name="place_small_grads")
        sems, srcs, lands, token = _push_start(f"scatter_start_{group}", [(r, s, land_now[n]) for r, s, n in zip(routes, srcs, names)])
        land_now.update(zip(names, lands))
        scatters[group] = (routes, sems, srcs, names)
        sent.append(token)
        return token

    sent = []
    loss, dx = _local_step(x[0], positions[0], loss_target[0], get_w, P, put_g)
    sent_last = sent[-1]
    loss = lax.psum(loss[0, 0], ("x", "y", "c"))

    out = {}

    def finish(tag, groups, after):
        waits = [(scatters[g][0], scatters[g][1], scatters[g][2], [land_now[n] for n in scatters[g][3]]) for g in groups]
        for g, lands in zip(groups, _push_wait(f"scatter_wait_{tag}", waits, after)):
            land_now.update(zip(scatters[g][3], lands))
        names = [n for n in dict.fromkeys(n for g in groups for n in scatters[g][3]) if n != "small"]
        part = {}
        for n in names:
            recv = land_now[n] if n in LAYERED else land_now[n][:, None]
            part[n] = _sum4([g32[n][l] for l in sorted(g32[n])], axis2d(n), recv, kidx, name=f"sum4_{n}")
        other = _swap_with_sibling(part, tag)
        done = []
        for n in names:
            C = part[n].shape[-1]
            res = _adamw(w[n].reshape(-1, C), m[n].reshape(-1, C), v[n].reshape(-1, C), [part[n], other[n]], name=f"adamw_{n}")
            out[n] = [r.reshape(w[n].shape) for r in res]
            done.append(res[0])
        return done

    done = finish("a", ["ffn1", "odd", "ffn0"], [dx, sent_last])
    finish("b", ["even"], done)

    order = SMALL_REP + list(SMALL_SHARDED)
    packed = lambda src: jnp.concatenate([_pack([src[n] for n in SMALL_REP], ra), _pack([src[n] for n in SMALL_SHARDED], rb)])
    res = _adamw_small(land_now["small"], packed(w), packed(m), packed(v), kidx, ra, rb)
    for r in res:
        parts = _unpack(r[:ra], [w[n].shape for n in SMALL_REP]) + _unpack(r[ra:], small_sh_shapes)
        for n, a in zip(order, parts):
            out.setdefault(n, []).append(a)

    return (loss, dx[None], *[out[n][0] for n in WEIGHTS], *[out[n][1] for n in WEIGHTS],
            *[out[n][2] for n in WEIGHTS], *[out[n][3] for n in WEIGHTS])
```

```python
import functools
import math

import jax
import jax.numpy as jnp
from jax import lax
from jax.experimental import pallas as pl
from jax.experimental.pallas import tpu as pltpu

f32, bf16 = jnp.float32, jnp.bfloat16
EPS = 1e-6
LANES = 128
SUBLANES = 8
VMEM_BYTES = 48 * 1024 * 1024
HGRN_CHUNK = 64
HGRN_HEADS = 4
S5_GROUPS, S5_STATE, S5_GROUP = 32, 64, 16
S5_N = S5_GROUPS * S5_STATE
S5_SEG = SUBLANES
MLA_HEADS, MLA_NOPE, MLA_ROPE, MLA_V = 8, 128, 64, 128
MLA_QK = MLA_NOPE + MLA_ROPE
MLA_Q_RANK, MLA_KV_RANK = 384, 256
ROPE_THETA = 10000.0
D_FF = 2816
ADAM_LR, ADAM_B1, ADAM_B2, ADAM_EPS, ADAM_WD, ADAM_STEP = 0.001, 0.9, 0.999, 1e-08, 0.01, 10
MESH = pl.DeviceIdType.MESH
HI = lax.Precision.HIGHEST


def _cp(*dims):
    return pltpu.CompilerParams(dimension_semantics=dims if dims else None, vmem_limit_bytes=VMEM_BYTES)


def _tile(n, t):
    if n <= t:
        return n
    c = (t // LANES) * LANES
    while c >= LANES:
        if n % c == 0:
            return c
        c -= LANES
    return n


def _dot(a, b, dn=None, precision=None):
    if dn is None:
        dn = (((a.ndim - 1,), (0,)), ((), ()))
    return lax.dot_general(a, b, dn, preferred_element_type=f32, precision=precision)


NT = (((1,), (1,)), ((), ()))
TN = (((0,), (0,)), ((), ()))


def _bdot(a, b, dn=None):
    return _dot(a.astype(bf16), b.astype(bf16), dn)


def _mm(a, b, *, name, ta=False, tb=False, out_dtype=f32, res=None, also_bf16=False, tm=1024, tn=1024, tk=1024, dep=None):
    halves = lambda s: (s[1], 2 * s[2]) if len(s) == 3 else s
    M, K = (a.shape[1], a.shape[0]) if ta else halves(a.shape)
    N = b.shape[0] if tb else halves(b.shape)[1]
    tm, tn, tk = _tile(M, tm), _tile(N, tn), _tile(K, tk)
    if a.ndim == 3:
        tk = _tile(K // 2, tk)
    if b.ndim == 3:
        tn = _tile(N // 2, tn)
    nk = K // tk
    dn = (((0 if ta else 1,), (1 if tb else 0,)), ((), ()))

    def body(*refs):
        a_ref, b_ref = refs[0], refs[1]
        r_ref = refs[2] if res is not None else None
        nin = 2 + (res is not None) + (dep is not None)
        outs = refs[nin:-1]
        acc = refs[-1]
        k = pl.program_id(2)
        p = _bdot(a_ref[...], b_ref[...], dn)

        @pl.when(k == 0)
        def _():
            acc[...] = p

        @pl.when(k > 0)
        def _():
            acc[...] += p

        @pl.when(k == nk - 1)
        def _():
            r = acc[...]
            if r_ref is not None:
                r = r + r_ref[...]
            outs[0][...] = r.astype(out_dtype)
            if also_bf16:
                outs[1][...] = r.astype(bf16)

    a_spec = pl.BlockSpec((tk, tm), lambda i, j, k: (k, i)) if ta else pl.BlockSpec((tm, tk), lambda i, j, k: (i, k))
    b_spec = pl.BlockSpec((tn, tk), lambda i, j, k: (j, k)) if tb else pl.BlockSpec((tk, tn), lambda i, j, k: (k, j))
    if a.ndim == 3:
        kh = K // 2 // tk
        a_spec = pl.BlockSpec((None, tm, tk), lambda i, j, k: (k // kh, i, k % kh))
    if b.ndim == 3:
        nh = N // 2 // tn
        b_spec = pl.BlockSpec((None, tk, tn), lambda i, j, k: (j // nh, k, j % nh))
    o_spec = pl.BlockSpec((tm, tn), lambda i, j, k: (i, j))
    in_specs, args = [a_spec, b_spec], [a, b]
    if res is not None:
        in_specs.append(o_spec)
        args.append(res)
    if dep is not None:
        in_specs.append(pl.BlockSpec(memory_space=pl.ANY))
        args.append(dep)
    out_shape = [jax.ShapeDtypeStruct((M, N), out_dtype)]
    out_specs = [o_spec]
    if also_bf16:
        out_shape.append(jax.ShapeDtypeStruct((M, N), bf16))
        out_specs.append(o_spec)
    out = pl.pallas_call(
        body, name=name, grid=(M // tm, N // tn, nk), in_specs=in_specs, out_specs=out_specs, out_shape=out_shape,
        scratch_shapes=[pltpu.VMEM((tm, tn), f32)], compiler_params=_cp("parallel", "parallel", "arbitrary"),
    )(*args)
    return out if also_bf16 else out[0]


def _rms_fwd(x, g, *, name, col=0, width=None, tm=512):
    T = x.shape[0]
    width = x.shape[1] if width is None else width
    tm = _tile(T, tm)

    def body(x_ref, g_ref, o_ref):
        xv = x_ref[...]
        r = lax.rsqrt(jnp.mean(xv * xv, axis=-1, keepdims=True) + EPS)
        o_ref[...] = (xv * r * g_ref[...]).astype(bf16)

    return pl.pallas_call(
        body, name=name, grid=(T // tm,),
        in_specs=[pl.BlockSpec((tm, width), lambda i: (i, col)), pl.BlockSpec((1, width), lambda i: (0, 0))],
        out_specs=pl.BlockSpec((tm, width), lambda i: (i, 0)), out_shape=jax.ShapeDtypeStruct((T, width), bf16),
        compiler_params=_cp("parallel"),
    )(x, g)


def _rms_bwd_math(xv, g, dy):
    r = lax.rsqrt(jnp.mean(xv * xv, axis=-1, keepdims=True) + EPS)
    xh = xv * r
    dxh = dy * g
    dx = r * (dxh - xh * jnp.mean(dxh * xh, axis=-1, keepdims=True))
    dg = jnp.sum(dy * xh, axis=0, keepdims=True)
    return dx, dg


def _rms_bwd(x, g, dy, res=None, *, name, tm=512):
    T, D = x.shape
    tm = _tile(T, tm)

    def body(*refs):
        x_ref, g_ref, dy_ref = refs[:3]
        r_ref = refs[3] if res is not None else None
        dx_ref, dg_ref = refs[-2:]
        dx, dg = _rms_bwd_math(x_ref[...], g_ref[...], dy_ref[...].astype(f32))
        if r_ref is not None:
            dx = dx + r_ref[...]
        dx_ref[...] = dx

        @pl.when(pl.program_id(0) == 0)
        def _():
            dg_ref[...] = dg

        @pl.when(pl.program_id(0) > 0)
        def _():
            dg_ref[...] += dg

    row = pl.BlockSpec((tm, D), lambda i: (i, 0))
    vec = pl.BlockSpec((1, D), lambda i: (0, 0))
    in_specs, args = [row, vec, row], [x, g, dy]
    if res is not None:
        in_specs.append(row)
        args.append(res)
    return pl.pallas_call(
        body, name=name, grid=(T // tm,), in_specs=in_specs, out_specs=[row, vec],
        out_shape=[jax.ShapeDtypeStruct((T, D), f32), jax.ShapeDtypeStruct((1, D), f32)],
        compiler_params=_cp("arbitrary"),
    )(*args)


def _loss_head(h, g, target, *, tm=512):
    T, D = h.shape
    tm = _tile(T, tm)

    def body(h_ref, g_ref, t_ref, loss_ref, dh_ref, dg_ref):
        hv, gv = h_ref[...], g_ref[...]
        r = lax.rsqrt(jnp.mean(hv * hv, axis=-1, keepdims=True) + EPS)
        e = hv * r * gv - t_ref[...]
        part = 0.5 * jnp.sum(jnp.mean(e * e, axis=-1, keepdims=True), axis=0, keepdims=True)
        dx, dg = _rms_bwd_math(hv, gv, e * (1.0 / D))
        dh_ref[...] = dx

        @pl.when(pl.program_id(0) == 0)
        def _():
            loss_ref[...] = part
            dg_ref[...] = dg

        @pl.when(pl.program_id(0) > 0)
        def _():
            loss_ref[...] += part
            dg_ref[...] += dg

    row = pl.BlockSpec((tm, D), lambda i: (i, 0))
    vec = pl.BlockSpec((1, D), lambda i: (0, 0))
    return pl.pallas_call(
        body, name="loss_head", grid=(T // tm,), in_specs=[row, vec, row],
        out_specs=[pl.BlockSpec((1, 1), lambda i: (0, 0)), row, vec],
        out_shape=[jax.ShapeDtypeStruct((1, 1), f32), jax.ShapeDtypeStruct((T, D), f32), jax.ShapeDtypeStruct((1, D), f32)],
        compiler_params=_cp("arbitrary"),
    )(h, g, target)


FFN_W = 2 * LANES
FFN_ROWS = 128
HALO = 2 * SUBLANES


def _conv_taps(a_ref, c, rc):
    if isinstance(c, int) and c == 0:
        ext = jnp.concatenate([jnp.zeros((HALO, FFN_W), f32), a_ref[pl.ds(0, rc), :].astype(f32)], axis=0)
    else:
        ext = a_ref[pl.ds(pl.multiple_of(c * rc - HALO, HALO), rc + HALO), :].astype(f32)
    return ext[HALO:], pltpu.roll(ext, 1, 0)[HALO:], pltpu.roll(ext, 2, 0)[HALO:]


def _chunk_rows(c, rc):
    return pl.ds(c * rc, rc) if isinstance(c, int) else pl.ds(pl.multiple_of(c * rc, rc), rc)


def _ffn_mid_fwd(au, cw, cb, *, name):
    T = au.shape[0]
    F = au.shape[1] // 2
    nb = F // FFN_W
    rc = min(FFN_ROWS, T)
    nc = T // rc

    def body(a_ref, u_ref, w_ref, b_ref, z_ref):
        w, b = w_ref[...], b_ref[...]

        def chunk(c):
            a, a1, a2 = _conv_taps(a_ref, c, rc)
            rows = _chunk_rows(c, rc)
            ac = w[0:1] * a2 + w[1:2] * a1 + w[2:3] * a + b
            z_ref[rows, :] = (ac * jax.nn.sigmoid(ac) * u_ref[rows, :].astype(f32)).astype(bf16)

        chunk(0)
        lax.fori_loop(1, nc, lambda c, _: chunk(c), None)

    return pl.pallas_call(
        body, name=name, grid=(nb,),
        in_specs=[pl.BlockSpec((T, FFN_W), lambda j: (0, j)), pl.BlockSpec((T, FFN_W), lambda j: (0, nb + j)),
                  pl.BlockSpec((3, FFN_W), lambda j: (0, j)), pl.BlockSpec((1, FFN_W), lambda j: (0, j))],
        out_specs=pl.BlockSpec((T, FFN_W), lambda j: (0, j)), out_shape=jax.ShapeDtypeStruct((T, F), bf16),
        compiler_params=_cp("parallel"),
    )(au, au, cw, cb)


def _ffn_mid_bwd(au, cw, cb, dz, *, name):
    T = au.shape[0]
    F = au.shape[1] // 2
    nb = F // FFN_W
    rc = min(FFN_ROWS, T)
    nc = T // rc

    def body(a_ref, u_ref, w_ref, b_ref, dz_ref, dau_ref, dw_ref, db_ref):
        w, b = w_ref[...], b_ref[...]

        def chunk(c, carry):
            nxt, s0, s1, s2, sb = carry
            a, a1, a2 = _conv_taps(a_ref, c, rc)
            rows = _chunk_rows(c, rc)
            ac = w[0:1] * a2 + w[1:2] * a1 + w[2:3] * a + b
            sg = jax.nn.sigmoid(ac)
            dz = dz_ref[rows, :].astype(f32)
            dau_ref[1, rows, :] = (dz * ac * sg).astype(bf16)
            dac = dz * u_ref[rows, :].astype(f32) * sg * (1.0 + ac * (1.0 - sg))
            ext = jnp.concatenate([dac, nxt], axis=0)
            d1, d2 = pltpu.roll(ext, rc + HALO - 1, 0)[:rc], pltpu.roll(ext, rc + HALO - 2, 0)[:rc]
            dau_ref[0, rows, :] = (w[2:3] * dac + w[1:2] * d1 + w[0:1] * d2).astype(bf16)
            tot = lambda v: jnp.sum(v, axis=0, keepdims=True)
            return dac[:HALO], s0 + tot(dac * a2), s1 + tot(dac * a1), s2 + tot(dac * a), sb + tot(dac)

        z = jnp.zeros((1, FFN_W), f32)
        carry = (jnp.zeros((HALO, FFN_W), f32), z, z, z, z)
        carry = lax.fori_loop(0, nc - 1, lambda k, cr: chunk(nc - 1 - k, cr), carry)
        _, s0, s1, s2, sb = chunk(0, carry)
        rows = lax.broadcasted_iota(jnp.int32, (3, FFN_W), 0)
        dw_ref[...] = jnp.where(rows == 0, s0, jnp.where(rows == 1, s1, s2))
        db_ref[...] = sb

    col = lambda off: pl.BlockSpec((T, FFN_W), lambda j: (0, off + j))
    return pl.pallas_call(
        body, name=name, grid=(nb,),
        in_specs=[col(0), col(nb), pl.BlockSpec((3, FFN_W), lambda j: (0, j)), pl.BlockSpec((1, FFN_W), lambda j: (0, j)), col(0)],
        out_specs=[pl.BlockSpec((2, T, FFN_W), lambda j: (0, 0, j)), pl.BlockSpec((3, FFN_W), lambda j: (0, j)),
                   pl.BlockSpec((1, FFN_W), lambda j: (0, j))],
        out_shape=[jax.ShapeDtypeStruct((2, T, F), bf16), jax.ShapeDtypeStruct((3, F), f32), jax.ShapeDtypeStruct((1, F), f32)],
        compiler_params=_cp("parallel"),
    )(au, au, cw, cb, dz)


BNN = (((2,), (1,)), ((0,), (0,)))
BNT = (((2,), (2,)), ((0,), (0,)))
BTN = (((1,), (1,)), ((0,), (0,)))


def _heads(x):
    return jnp.stack([x[:, h * LANES:(h + 1) * LANES] for h in range(HGRN_HEADS)])


def _put_heads(ref, rows, x, dtype):
    for h in range(HGRN_HEADS):
        ref[rows, h * LANES:(h + 1) * LANES] = x[h].astype(dtype)


def _hgrn_lb(l):
    m = jnp.max(l, axis=0, keepdims=True)
    e = jnp.exp(l - m)
    return e[0:1] / jnp.sum(e, axis=0, keepdims=True)


def _hgrn_chunk(q, fx, lb):
    H, C = q.shape[0], q.shape[1]
    sg = jax.nn.sigmoid(fx)
    F = lb + (1.0 - lb) * sg
    k = 1.0 - F
    logF = jnp.log(F)
    r = lax.broadcasted_iota(jnp.int32, (H, C, C), 1)
    c = lax.broadcasted_iota(jnp.int32, (H, C, C), 2)
    tril = (r >= c)
    b = _dot(tril.astype(f32), logF, BNN, precision=HI)
    bl = jnp.sum(logF, axis=1, keepdims=True)
    eb = jnp.exp(b)
    enb = jnp.exp(-b)
    elb = jnp.exp(bl - b)
    return dict(sg=sg, F=F, k=k, b=b, bl=bl, eb=eb, enb=enb, elb=elb, qd=q * eb, kd=k * enb, kl=k * elb, tril=tril)


def _hgrn_fwd(proj, lbl, ng, *, rb=512):
    T = proj.shape[0]
    rb = min(rb, T)
    cpb = rb // HGRN_CHUNK
    nblk = T // rb
    H = HGRN_HEADS

    def body(q_ref, f_ref, i_ref, g_ref, lbl_ref, ng_ref, y_ref, st_ref, S):
        @pl.when(pl.program_id(0) == 0)
        def _():
            S[...] = jnp.zeros_like(S)

        lb = _heads(_hgrn_lb(lbl_ref[...]))
        ngv = _heads(ng_ref[...])
        for c in range(cpb):
            sl = pl.ds(c * HGRN_CHUNK, HGRN_CHUNK)
            v, gx = _heads(i_ref[sl, :]), _heads(g_ref[sl, :])
            ch = _hgrn_chunk(_heads(q_ref[sl, :]), _heads(f_ref[sl, :]), lb)
            att = jnp.where(ch["tril"], _bdot(ch["qd"], ch["kd"], BNT), 0.0)
            St = S[...]
            st_ref[:, c] = St
            o = _bdot(att, v, BNN) + _bdot(ch["qd"], St, BNT)
            S[...] = St * jnp.exp(ch["bl"]) + _bdot(v, ch["kl"], BTN)
            r = lax.rsqrt(jnp.mean(o * o, axis=-1, keepdims=True) + EPS)
            _put_heads(y_ref, sl, o * r * ngv * (gx * jax.nn.sigmoid(gx)), bf16)

    col = lambda off: pl.BlockSpec((rb, H * LANES), lambda n: (n, off))
    return pl.pallas_call(
        body, name="hgrn_fwd", grid=(nblk,),
        in_specs=[col(0), col(1), col(2), col(3), pl.BlockSpec((2, H * LANES), lambda n: (0, 0)),
                  pl.BlockSpec((1, H * LANES), lambda n: (0, 0))],
        out_specs=[pl.BlockSpec((rb, H * LANES), lambda n: (n, 0)),
                   pl.BlockSpec((H, cpb, LANES, LANES), lambda n: (0, n, 0, 0))],
        out_shape=[jax.ShapeDtypeStruct((T, H * LANES), bf16),
                   jax.ShapeDtypeStruct((H, T // HGRN_CHUNK, LANES, LANES), f32)],
        scratch_shapes=[pltpu.VMEM((H, LANES, LANES), f32)], compiler_params=_cp("arbitrary"),
    )(proj, proj, proj, proj, lbl, ng)


def _hgrn_bwd(proj, lbl, ng, states, dy, *, rb=512):
    T = proj.shape[0]
    rb = min(rb, T)
    cpb = rb // HGRN_CHUNK
    nblk = T // rb
    H = HGRN_HEADS
    C = HGRN_CHUNK

    def body(q_ref, f_ref, i_ref, g_ref, lbl_ref, ng_ref, st_ref, dy_ref,
             dq_ref, df_ref, di_ref, dg_ref, dl_ref, dng_ref, dS, dlb_acc, dng_acc):
        n = pl.program_id(0)

        @pl.when(n == 0)
        def _():
            dS[...] = jnp.zeros_like(dS)
            dlb_acc[...] = jnp.zeros_like(dlb_acc)
            dng_acc[...] = jnp.zeros_like(dng_acc)

        lb_row = _hgrn_lb(lbl_ref[...])
        lb = _heads(lb_row)
        ngv = _heads(ng_ref[...])
        r_i = lax.broadcasted_iota(jnp.int32, (H, C, C), 1)
        c_i = lax.broadcasted_iota(jnp.int32, (H, C, C), 2)
        triu = (c_i >= r_i).astype(f32)
        rows_sum = lambda x: jnp.sum(x, axis=1, keepdims=True)
        for c in reversed(range(cpb)):
            sl = pl.ds(c * C, C)
            q, v, gx = _heads(q_ref[sl, :]), _heads(i_ref[sl, :]), _heads(g_ref[sl, :])
            ch = _hgrn_chunk(q, _heads(f_ref[sl, :]), lb)
            qd, kd, kl = ch["qd"], ch["kd"], ch["kl"]
            att = jnp.where(ch["tril"], _bdot(qd, kd, BNT), 0.0)
            St = st_ref[:, c]
            o = _bdot(att, v, BNN) + _bdot(qd, St, BNT)
            r = lax.rsqrt(jnp.mean(o * o, axis=-1, keepdims=True) + EPS)
            on = o * r
            sgg = jax.nn.sigmoid(gx)
            gate = gx * sgg
            dyv = _heads(dy_ref[sl, :].astype(f32))
            _put_heads(dg_ref, sl, dyv * on * ngv * sgg * (1.0 + gx * (1.0 - sgg)), bf16)
            dng_acc[...] += rows_sum(dyv * on * gate)
            don = dyv * ngv * gate
            do = r * (don - on * jnp.mean(don * on, axis=-1, keepdims=True))
            dSt = dS[...]
            dA = jnp.where(ch["tril"], _bdot(do, v, BNT), 0.0)
            dv = _bdot(att, do, BTN) + _bdot(kl, dSt, BNT)
            dqd = _bdot(dA, kd, BNN) + _bdot(do, St, BNN)
            dkd = _bdot(dA, qd, BTN)
            dkl = _bdot(v, dSt, BNN)
            dec = jnp.exp(ch["bl"])
            ddec = rows_sum(St * dSt)
            dS[...] = _bdot(do, qd, BTN) + dSt * dec
            dB = dqd * qd - dkd * kd - dkl * kl
            dbl = rows_sum(dkl * kl) + ddec * dec
            dk = dkd * ch["enb"] + dkl * ch["elb"]
            dlogF = _dot(triu, dB, BNN, precision=HI) + dbl
            dF = dlogF / ch["F"] - dk
            sg = ch["sg"]
            _put_heads(dq_ref, sl, dqd * ch["eb"], bf16)
            _put_heads(di_ref, sl, dv, bf16)
            _put_heads(df_ref, sl, dF * (1.0 - lb) * sg * (1.0 - sg), bf16)
            dlb_acc[...] += rows_sum(dF * (1.0 - sg))

        @pl.when(n == nblk - 1)
        def _():
            rows = lax.broadcasted_iota(jnp.int32, (2, LANES), 0)
            for h in range(H):
                hs = pl.ds(h * LANES, LANES)
                lbh = lb_row[:, h * LANES:(h + 1) * LANES]
                dl0 = dlb_acc[h] * lbh * (1.0 - lbh)
                dl_ref[:, hs] = jnp.where(rows == 0, dl0, -dl0)
                dng_ref[:, hs] = dng_acc[h]

    col = lambda off: pl.BlockSpec((rb, H * LANES), lambda n: (nblk - 1 - n, off))
    vec = lambda rows: pl.BlockSpec((rows, H * LANES), lambda n: (0, 0))
    tok = jax.ShapeDtypeStruct((T, H * LANES), bf16)
    return pl.pallas_call(
        body, name="hgrn_bwd", grid=(nblk,),
        in_specs=[col(0), col(1), col(2), col(3), vec(2), vec(1),
                  pl.BlockSpec((H, cpb, LANES, LANES), lambda n: (0, nblk - 1 - n, 0, 0)), col(0)],
        out_specs=[col(0), col(0), col(0), col(0), vec(2), vec(1)],
        out_shape=[tok, tok, tok, tok, jax.ShapeDtypeStruct((2, H * LANES), f32), jax.ShapeDtypeStruct((1, H * LANES), f32)],
        scratch_shapes=[pltpu.VMEM((H, LANES, LANES), f32), pltpu.VMEM((H, 1, LANES), f32), pltpu.VMEM((H, 1, LANES), f32)],
        compiler_params=_cp("arbitrary"),
    )(proj, proj, proj, proj, lbl, ng, states, dy)


def _s5_disc_math(ar, ai, ldt, br, bi):
    dt = jnp.exp(ldt)
    mag = jnp.exp(ar * dt)
    abr, abi = mag * jnp.cos(ai * dt), mag * jnp.sin(ai * dt)
    den = ar * ar + ai * ai
    xr, xi = abr - 1.0, abi
    cr = (xr * ar + xi * ai) / den
    ci = (xi * ar - xr * ai) / den
    return abr, abi, cr * br - ci * bi, cr * bi + ci * br


def _s5_disc_fwd(ar, ai, ldt, br, bi):
    def body(ar_ref, ai_ref, ldt_ref, br_ref, bi_ref, o0, o1, o2, o3):
        outs = _s5_disc_math(ar_ref[...], ai_ref[...], ldt_ref[...], br_ref[...], bi_ref[...])
        for o, v in zip((o0, o1, o2, o3), outs):
            o[...] = v

    return pl.pallas_call(
        body, name="s5_disc_fwd",
        out_shape=[jax.ShapeDtypeStruct(ar.shape, f32)] * 2 + [jax.ShapeDtypeStruct(br.shape, f32)] * 2,
    )(ar, ai, ldt, br, bi)


def _s5_disc_bwd(ar, ai, ldt, br, bi, cts):
    def body(ar_ref, ai_ref, ldt_ref, br_ref, bi_ref, c0, c1, c2, c3, o0, o1, o2, o3, o4):
        _, vjp = jax.vjp(_s5_disc_math, ar_ref[...], ai_ref[...], ldt_ref[...], br_ref[...], bi_ref[...])
        for o, v in zip((o0, o1, o2, o3, o4), vjp((c0[...], c1[...], c2[...], c3[...]))):
            o[...] = v

    return pl.pallas_call(
        body, name="s5_disc_bwd",
        out_shape=[jax.ShapeDtypeStruct(ar.shape, f32)] * 3 + [jax.ShapeDtypeStruct(br.shape, f32)] * 2,
    )(ar, ai, ldt, br, bi, *cts)


S5_LC = 512
S5_NLC = S5_N // S5_LC
S5_UB = 4
S5_UNROLL = 4


def _cmul(ar, ai, xr, xi):
    return ar * xr - ai * xi, ar * xi + ai * xr


def _cpow(ar, ai, n):
    rr, ri = None, None
    br, bi = ar, ai
    while n:
        if n & 1:
            rr, ri = (br, bi) if rr is None else _cmul(rr, ri, br, bi)
        n >>= 1
        if n:
            br, bi = _cmul(br, bi, br, bi)
    return rr, ri


def _s5_bu(u_ref, bre_ref, bim_ref, xr, xi):
    for k in range(S5_UB):
        uk = u_ref[:, k * LANES:(k + 1) * LANES].astype(bf16)
        xr[:, k * S5_LC:(k + 1) * S5_LC] = _dot(uk, bre_ref[k])
        xi[:, k * S5_LC:(k + 1) * S5_LC] = _dot(uk, bim_ref[k])


def _s5_scan(xr, xi, sr, si, ar_ref, ai_ref, nsteps, store):
    for c in range(S5_NLC):
        cs = slice(c * S5_LC, (c + 1) * S5_LC)
        a_r = jnp.broadcast_to(ar_ref[:, cs], (S5_SEG, S5_LC))
        a_i = jnp.broadcast_to(ai_ref[:, cs], (S5_SEG, S5_LC))

        def step(j, carry, cs=cs, a_r=a_r, a_i=a_i):
            pr, pi = carry
            rows = pl.ds(pl.multiple_of(j * S5_SEG, S5_SEG), S5_SEG)
            nr = a_r * pr - a_i * pi + xr[rows, cs]
            ni = a_r * pi + a_i * pr + xi[rows, cs]
            if store:
                xr[rows, cs] = nr
                xi[rows, cs] = ni
            return nr, ni

        fr, fi = lax.fori_loop(0, nsteps, step, (sr[:, cs], si[:, cs]), unroll=S5_UNROLL)
        sr[:, cs] = fr
        si[:, cs] = fi


def _s5_rscan(dr, di, xr, xi, s0r, s0i, gr, gi, acc_r, acc_i, ar_ref, ai_ref, nsteps):
    for c in range(S5_NLC):
        cs = slice(c * S5_LC, (c + 1) * S5_LC)
        a_r = jnp.broadcast_to(ar_ref[:, cs], (S5_SEG, S5_LC))
        a_i = jnp.broadcast_to(ai_ref[:, cs], (S5_SEG, S5_LC))

        def step(jj, carry, cs=cs, a_r=a_r, a_i=a_i):
            pr, pi, cr, ci = carry
            j = nsteps - 1 - jj
            rows = pl.ds(pl.multiple_of(j * S5_SEG, S5_SEG), S5_SEG)
            nr = dr[rows, cs] + a_r * pr + a_i * pi
            ni = di[rows, cs] + a_r * pi - a_i * pr
            dr[rows, cs] = nr
            di[rows, cs] = ni
            if acc_r is not None:
                prev = pl.ds(pl.multiple_of(jnp.maximum(j - 1, 0) * S5_SEG, S5_SEG), S5_SEG)
                first = j == 0
                pr_s = jnp.where(first, s0r[:, cs], xr[prev, cs])
                pi_s = jnp.where(first, s0i[:, cs], xi[prev, cs])
                cr = cr + nr * pr_s + ni * pi_s
                ci = ci - nr * pi_s + ni * pr_s
            return nr, ni, cr, ci

        z = jnp.zeros((S5_SEG, S5_LC), f32)
        init = (gr[:, cs], gi[:, cs], z, z)
        fr, fi, cr, ci = lax.fori_loop(0, nsteps, step, init, unroll=S5_UNROLL)
        gr[:, cs] = fr
        gi[:, cs] = fi
        if acc_r is not None:
            acc_r[:, cs] += cr
            acc_i[:, cs] += ci


def _s5_seg_carry(fr, fi, ar, ai, seg_len, reverse):
    pr, pi = _cpow(ar, ai if not reverse else -ai, seg_len)
    rows = lax.broadcasted_iota(jnp.int32, fr.shape, 0)
    cr, ci = jnp.zeros_like(fr), jnp.zeros_like(fi)
    sh = (S5_SEG - 1) if reverse else 1
    fr_s, fi_s = pltpu.roll(fr, sh, 0), pltpu.roll(fi, sh, 0)
    order = range(S5_SEG - 2, -1, -1) if reverse else range(1, S5_SEG)
    for r in order:
        c_r, c_i = pltpu.roll(cr, sh, 0), pltpu.roll(ci, sh, 0)
        m_r, m_i = _cmul(pr, pi, c_r, c_i)
        cr = jnp.where(rows == r, m_r + fr_s, cr)
        ci = jnp.where(rows == r, m_i + fi_s, ci)
    return cr, ci


def _gelu_parts(y):
    c0 = math.sqrt(2.0 / math.pi)
    t = jnp.tanh(c0 * (y + 0.044715 * y * y * y))
    z = 0.5 * y * (1.0 + t)
    dz = 0.5 * (1.0 + t) + 0.5 * y * (1.0 - t * t) * c0 * (1.0 + 3.0 * 0.044715 * y * y)
    return z, dz


def _s5_y(xr, xi, u_ref, cre_ref, cim_ref, d_ref):
    ys = []
    for k in range(S5_UB):
        cs = slice(k * S5_LC, (k + 1) * S5_LC)
        ys.append(_bdot(xr[:, cs], cre_ref[k]) - _bdot(xi[:, cs], cim_ref[k]))
    return jnp.concatenate(ys, axis=1) + d_ref[...] * u_ref[...]


def _s5_specs(T, rb, rev=False):
    nblk = T // rb
    blk = (lambda i: (nblk - 1 - i, 0)) if rev else (lambda i: (i, 0))
    tok = pl.BlockSpec((rb, 4 * LANES), blk)
    bmat = pl.BlockSpec((S5_UB, LANES, S5_LC), lambda i: (0, 0, 0))
    cmat = pl.BlockSpec((S5_UB, S5_LC, LANES), lambda i: (0, 0, 0))
    avec = pl.BlockSpec((1, S5_N), lambda i: (0, 0))
    seg = pl.BlockSpec((S5_SEG, S5_N), lambda i: (0, 0))
    cvec = pl.BlockSpec((1, 4 * LANES), lambda i: (0, 0))
    s0 = pl.BlockSpec((1, S5_SEG, S5_N), (lambda i: (nblk - 1 - i, 0, 0)) if rev else (lambda i: (i, 0, 0)))
    return dict(tok=tok, bmat=bmat, cmat=cmat, avec=avec, seg=seg, cvec=cvec, s0=s0, nblk=nblk)


def _s5_final(u, bre, bim, ar, ai, *, rb):
    T = u.shape[0]
    sp = _s5_specs(T, rb)

    def body(u_ref, bre_ref, bim_ref, ar_ref, ai_ref, fr_ref, fi_ref, xr, xi):
        @pl.when(pl.program_id(0) == 0)
        def _():
            fr_ref[...] = jnp.zeros_like(fr_ref)
            fi_ref[...] = jnp.zeros_like(fi_ref)

        _s5_bu(u_ref, bre_ref, bim_ref, xr, xi)
        _s5_scan(xr, xi, fr_ref, fi_ref, ar_ref, ai_ref, rb // S5_SEG, False)

    return pl.pallas_call(
        body, name="s5_final", grid=(sp["nblk"],),
        in_specs=[sp["tok"], sp["bmat"], sp["bmat"], sp["avec"], sp["avec"]], out_specs=[sp["seg"], sp["seg"]],
        out_shape=[jax.ShapeDtypeStruct((S5_SEG, S5_N), f32)] * 2,
        scratch_shapes=[pltpu.VMEM((rb, S5_N), f32)] * 2, compiler_params=_cp("arbitrary"),
    )(u, bre, bim, ar, ai)


def _s5_fwd(u, bre, bim, ar, ai, fr, fi, cre, cim, dsk, wg, bg, *, rb):
    T = u.shape[0]
    sp = _s5_specs(T, rb)
    seg_len = T // S5_SEG

    def body(u_ref, bre_ref, bim_ref, ar_ref, ai_ref, fr_ref, fi_ref, cre_ref, cim_ref, d_ref, wg_ref, bg_ref,
             o_ref, s0r_ref, s0i_ref, xr, xi, sr, si):
        @pl.when(pl.program_id(0) == 0)
        def _():
            i_r, i_i = _s5_seg_carry(fr_ref[...], fi_ref[...], ar_ref[...], ai_ref[...], seg_len, False)
            sr[...] = i_r
            si[...] = i_i

        s0r_ref[0] = sr[...]
        s0i_ref[0] = si[...]
        _s5_bu(u_ref, bre_ref, bim_ref, xr, xi)
        _s5_scan(xr, xi, sr, si, ar_ref, ai_ref, rb // S5_SEG, True)
        y = _s5_y(xr, xi, u_ref, cre_ref, cim_ref, d_ref)
        z, _ = _gelu_parts(y)
        v = _bdot(z, wg_ref[...]) + bg_ref[...]
        o_ref[...] = (z * jax.nn.sigmoid(v)).astype(bf16)

    wspec = pl.BlockSpec((4 * LANES, 4 * LANES), lambda i: (0, 0))
    return pl.pallas_call(
        body, name="s5_fwd", grid=(sp["nblk"],),
        in_specs=[sp["tok"], sp["bmat"], sp["bmat"], sp["avec"], sp["avec"], sp["seg"], sp["seg"], sp["cmat"], sp["cmat"],
                  sp["cvec"], wspec, sp["cvec"]],
        out_specs=[sp["tok"], sp["s0"], sp["s0"]],
        out_shape=[jax.ShapeDtypeStruct((T, 4 * LANES), bf16)] + [jax.ShapeDtypeStruct((sp["nblk"], S5_SEG, S5_N), f32)] * 2,
        scratch_shapes=[pltpu.VMEM((rb, S5_N), f32)] * 2 + [pltpu.VMEM((S5_SEG, S5_N), f32)] * 2,
        compiler_params=_cp("arbitrary"),
    )(u, bre, bim, ar, ai, fr, fi, cre, cim, dsk, wg, bg)


def _s5_bwd_a(u, bre, bim, ar, ai, s0r, s0i, cre, cim, cret, cimt, dsk, wg, bg, dout, *, rb):
    T = u.shape[0]
    sp = _s5_specs(T, rb, rev=True)

    def body(u_ref, bre_ref, bim_ref, ar_ref, ai_ref, s0r_ref, s0i_ref, cre_ref, cim_ref, cret_ref, cimt_ref,
             d_ref, wg_ref, bg_ref, do_ref, dy_ref, glr_ref, gli_ref, dcre_ref, dcim_ref, dd_ref, dwg_ref, dbg_ref,
             xr, xi, dr, di, sr, si):
        @pl.when(pl.program_id(0) == 0)
        def _():
            for r in (glr_ref, gli_ref, dcre_ref, dcim_ref, dd_ref, dwg_ref, dbg_ref):
                r[...] = jnp.zeros_like(r)

        sr[...] = s0r_ref[0]
        si[...] = s0i_ref[0]
        _s5_bu(u_ref, bre_ref, bim_ref, xr, xi)
        _s5_scan(xr, xi, sr, si, ar_ref, ai_ref, rb // S5_SEG, True)
        uv = u_ref[...]
        y = _s5_y(xr, xi, u_ref, cre_ref, cim_ref, d_ref)
        z, gz = _gelu_parts(y)
        v = _bdot(z, wg_ref[...]) + bg_ref[...]
        sg = jax.nn.sigmoid(v)
        dov = do_ref[...].astype(f32)
        dv = dov * z * sg * (1.0 - sg)
        dz = dov * sg + _bdot(dv, wg_ref[...], NT)
        dy = dz * gz
        dy_ref[...] = dy
        dwg_ref[...] += _bdot(z, dv, TN)
        dbg_ref[...] += jnp.sum(dv, axis=0, keepdims=True)
        dd_ref[...] += jnp.sum(dy * uv, axis=0, keepdims=True)
        for k in range(S5_UB):
            cs = slice(k * S5_LC, (k + 1) * S5_LC)
            dyk = dy[:, k * LANES:(k + 1) * LANES]
            dcre_ref[k] += _bdot(xr[:, cs], dyk, TN)
            dcim_ref[k] -= _bdot(xi[:, cs], dyk, TN)
            dr[:, cs] = _bdot(dyk, cret_ref[k])
            di[:, cs] = -_bdot(dyk, cimt_ref[k])
        _s5_rscan(dr, di, None, None, None, None, glr_ref, gli_ref, None, None, ar_ref, ai_ref, rb // S5_SEG)

    wspec = pl.BlockSpec((4 * LANES, 4 * LANES), lambda i: (0, 0))
    return pl.pallas_call(
        body, name="s5_bwd_a", grid=(sp["nblk"],),
        in_specs=[sp["tok"], sp["bmat"], sp["bmat"], sp["avec"], sp["avec"], sp["s0"], sp["s0"], sp["cmat"], sp["cmat"],
                  sp["bmat"], sp["bmat"], sp["cvec"], wspec, sp["cvec"], sp["tok"]],
        out_specs=[sp["tok"], sp["seg"], sp["seg"], sp["cmat"], sp["cmat"], sp["cvec"], wspec, sp["cvec"]],
        out_shape=[jax.ShapeDtypeStruct((T, 4 * LANES), f32)] + [jax.ShapeDtypeStruct((S5_SEG, S5_N), f32)] * 2
        + [jax.ShapeDtypeStruct((S5_UB, S5_LC, LANES), f32)] * 2
        + [jax.ShapeDtypeStruct((1, 4 * LANES), f32), jax.ShapeDtypeStruct((4 * LANES, 4 * LANES), f32),
           jax.ShapeDtypeStruct((1, 4 * LANES), f32)],
        scratch_shapes=[pltpu.VMEM((rb, S5_N), f32)] * 4 + [pltpu.VMEM((S5_SEG, S5_N), f32)] * 2,
        compiler_params=_cp("arbitrary"),
    )(u, bre, bim, ar, ai, s0r, s0i, cre, cim, cret, cimt, dsk, wg, bg, dout)


def _s5_bwd_b(u, bre, bim, bret, bimt, ar, ai, s0r, s0i, glr, gli, cret, cimt, dsk, dy, *, rb):
    T = u.shape[0]
    sp = _s5_specs(T, rb, rev=True)
    seg_len = T // S5_SEG
    nblk = sp["nblk"]

    def body(u_ref, bre_ref, bim_ref, bret_ref, bimt_ref, ar_ref, ai_ref, s0r_ref, s0i_ref, glr_ref, gli_ref,
             cret_ref, cimt_ref, d_ref, dy_ref, du_ref, dbre_ref, dbim_ref, dar_ref, dai_ref,
             xr, xi, dr, di, sr, si, gr, gi, acc_r, acc_i):
        @pl.when(pl.program_id(0) == 0)
        def _():
            x_r, x_i = _s5_seg_carry(glr_ref[...], gli_ref[...], ar_ref[...], ai_ref[...], seg_len, True)
            gr[...] = x_r
            gi[...] = x_i
            acc_r[...] = jnp.zeros_like(acc_r)
            acc_i[...] = jnp.zeros_like(acc_i)
            dbre_ref[...] = jnp.zeros_like(dbre_ref)
            dbim_ref[...] = jnp.zeros_like(dbim_ref)

        sr[...] = s0r_ref[0]
        si[...] = s0i_ref[0]
        _s5_bu(u_ref, bre_ref, bim_ref, xr, xi)
        _s5_scan(xr, xi, sr, si, ar_ref, ai_ref, rb // S5_SEG, True)
        dy = dy_ref[...]
        for k in range(S5_UB):
            cs = slice(k * S5_LC, (k + 1) * S5_LC)
            dyk = dy[:, k * LANES:(k + 1) * LANES]
            dr[:, cs] = _bdot(dyk, cret_ref[k])
            di[:, cs] = -_bdot(dyk, cimt_ref[k])
        sr[...] = s0r_ref[0]
        si[...] = s0i_ref[0]
        _s5_rscan(dr, di, xr, xi, sr, si, gr, gi, acc_r, acc_i, ar_ref, ai_ref, rb // S5_SEG)
        dus = []
        for k in range(S5_UB):
            cs = slice(k * S5_LC, (k + 1) * S5_LC)
            uk = u_ref[:, k * LANES:(k + 1) * LANES]
            dbre_ref[k] += _bdot(uk, dr[:, cs], TN)
            dbim_ref[k] += _bdot(uk, di[:, cs], TN)
            dus.append(_bdot(dr[:, cs], bret_ref[k]) + _bdot(di[:, cs], bimt_ref[k]))
        du_ref[...] = (jnp.concatenate(dus, axis=1) + d_ref[...] * dy).astype(bf16)

        @pl.when(pl.program_id(0) == nblk - 1)
        def _():
            dar_ref[...] = jnp.sum(acc_r[...], axis=0, keepdims=True)
            dai_ref[...] = jnp.sum(acc_i[...], axis=0, keepdims=True)

    return pl.pallas_call(
        body, name="s5_bwd_b", grid=(nblk,),
        in_specs=[sp["tok"], sp["bmat"], sp["bmat"], sp["cmat"], sp["cmat"], sp["avec"], sp["avec"], sp["s0"], sp["s0"],
                  sp["seg"], sp["seg"], sp["bmat"], sp["bmat"], sp["cvec"], sp["tok"]],
        out_specs=[sp["tok"], sp["bmat"], sp["bmat"], sp["avec"], sp["avec"]],
        out_shape=[jax.ShapeDtypeStruct((T, 4 * LANES), bf16)] + [jax.ShapeDtypeStruct((S5_UB, LANES, S5_LC), f32)] * 2
        + [jax.ShapeDtypeStruct((1, S5_N), f32)] * 2,
        scratch_shapes=[pltpu.VMEM((rb, S5_N), f32)] * 4 + [pltpu.VMEM((S5_SEG, S5_N), f32)] * 6,
        compiler_params=_cp("arbitrary"),
    )(u, bre, bim, bret, bimt, ar, ai, s0r, s0i, glr, gli, cret, cimt, dsk, dy)


def _blockdiag(w, transpose=False):
    if transpose:
        w = jnp.swapaxes(w, 1, 2)
    g, a, b = w.shape
    eye = jnp.eye(8, dtype=w.dtype)
    return jnp.einsum("kgab,gj->kgajb", w.reshape(4, 8, a, b), eye).reshape(4, 8 * a, 8 * b)


def _blockdiag_t(m, a, b):
    eye = jnp.eye(8, dtype=m.dtype)
    return jnp.einsum("kgajb,gj->kgab", m.reshape(4, 8, a, 8, b), eye).reshape(32, a, b)


ROT = MLA_ROPE // 2


def _rope_tables(positions):
    freqs = ROPE_THETA ** (-jnp.arange(0, MLA_ROPE, 2, dtype=f32) / MLA_ROPE)
    ang = positions.astype(f32)[:, None] * freqs
    cos, sin, z = jnp.cos(ang), jnp.sin(ang), jnp.zeros_like(ang)
    return (jnp.concatenate([cos, cos, z, z], axis=1), jnp.concatenate([-sin, z, z, z], axis=1),
            jnp.concatenate([z, sin, z, z], axis=1))


def _rot(x, c, sa, sb):
    return x * c + pltpu.roll(x, LANES - ROT, 1) * sa + pltpu.roll(x, ROT, 1) * sb


def _rot_t(dy, c, sa, sb):
    return dy * c + pltpu.roll(dy * sa, ROT, 1) + pltpu.roll(dy * sb, LANES - ROT, 1)


def _rms(xv, g):
    return xv * lax.rsqrt(jnp.mean(xv * xv, axis=-1, keepdims=True) + EPS) * g


QW, KVW = MLA_Q_RANK, MLA_KV_RANK
ODD_PAD = QW + KVW + LANES


def _mla_prep_fwd(proj, qg, kvg, tabs, *, tm=512):
    T = proj.shape[0]
    tm = _tile(T, tm)

    def body(p_ref, qg_ref, kvg_ref, c_ref, sa_ref, sb_ref, cq_ref, ckv_ref, kr_ref):
        cq_ref[...] = _rms(p_ref[:, :QW], qg_ref[...]).astype(bf16)
        ckv_ref[...] = _rms(p_ref[:, QW:QW + KVW], kvg_ref[...]).astype(bf16)
        kr_ref[...] = _rot(p_ref[:, QW + KVW:], c_ref[...], sa_ref[...], sb_ref[...]).astype(bf16)

    row = lambda w: pl.BlockSpec((tm, w), lambda i: (i, 0))
    vec = lambda w: pl.BlockSpec((1, w), lambda i: (0, 0))
    return pl.pallas_call(
        body, name="mla_prep_fwd", grid=(T // tm,),
        in_specs=[row(ODD_PAD), vec(QW), vec(KVW), row(LANES), row(LANES), row(LANES)],
        out_specs=[row(QW), row(KVW), row(LANES)],
        out_shape=[jax.ShapeDtypeStruct((T, QW), bf16), jax.ShapeDtypeStruct((T, KVW), bf16),
                   jax.ShapeDtypeStruct((T, LANES), bf16)],
        compiler_params=_cp("parallel"),
    )(proj, qg, kvg, *tabs)


def _mla_prep_bwd(proj, qg, kvg, tabs, dcqn, dckvn, dkr_heads, *, tm=512):
    T = proj.shape[0]
    tm = _tile(T, tm)

    def body(p_ref, qg_ref, kvg_ref, c_ref, sa_ref, sb_ref, dcq_ref, dckv_ref, dkr_ref, dp_ref, dqg_ref, dkvg_ref):
        dcq, dqg = _rms_bwd_math(p_ref[:, :QW], qg_ref[...], dcq_ref[...])
        dckv, dkvg = _rms_bwd_math(p_ref[:, QW:QW + KVW], kvg_ref[...], dckv_ref[...])
        dk = dkr_ref[:, :LANES]
        for h in range(1, MLA_HEADS):
            dk = dk + dkr_ref[:, h * LANES:(h + 1) * LANES]
        dkr = _rot_t(dk, c_ref[...], sa_ref[...], sb_ref[...])
        dp_ref[...] = jnp.concatenate([dcq, dckv, dkr], axis=1).astype(bf16)

        @pl.when(pl.program_id(0) == 0)
        def _():
            dqg_ref[...] = dqg
            dkvg_ref[...] = dkvg

        @pl.when(pl.program_id(0) > 0)
        def _():
            dqg_ref[...] += dqg
            dkvg_ref[...] += dkvg

    row = lambda w: pl.BlockSpec((tm, w), lambda i: (i, 0))
    vec = lambda w: pl.BlockSpec((1, w), lambda i: (0, 0))
    return pl.pallas_call(
        body, name="mla_prep_bwd", grid=(T // tm,),
        in_specs=[row(ODD_PAD), vec(QW), vec(KVW), row(LANES), row(LANES), row(LANES), row(QW), row(KVW),
                  row(MLA_HEADS * LANES)],
        out_specs=[row(ODD_PAD), vec(QW), vec(KVW)],
        out_shape=[jax.ShapeDtypeStruct((T, ODD_PAD), bf16), jax.ShapeDtypeStruct((1, QW), f32),
                   jax.ShapeDtypeStruct((1, KVW), f32)],
        compiler_params=_cp("arbitrary"),
    )(proj, qg, kvg, *tabs, dcqn, dckvn, dkr_heads)


HQ = 2 * LANES
QK_SCALE = MLA_QK ** -0.5


def _q_post(q, tabs, *, transpose, name, tm=512):
    T = q.shape[0]
    tm = _tile(T, tm)

    def body(q_ref, c_ref, sa_ref, sb_ref, o_ref):
        c, sa, sb = c_ref[...], sa_ref[...], sb_ref[...]
        for h in range(MLA_HEADS):
            nope, rope = pl.ds(h * HQ, LANES), pl.ds(h * HQ + LANES, LANES)
            o_ref[:, nope] = (q_ref[:, nope].astype(f32) * QK_SCALE).astype(bf16)
            o_ref[:, rope] = ((_rot_t if transpose else _rot)(q_ref[:, rope].astype(f32), c, sa, sb) * QK_SCALE).astype(bf16)

    tab = pl.BlockSpec((tm, LANES), lambda i: (i, 0))
    blk = pl.BlockSpec((tm, MLA_HEADS * HQ), lambda i: (i, 0))
    return pl.pallas_call(
        body, name=name, grid=(T // tm,), in_specs=[blk, tab, tab, tab], out_specs=blk,
        out_shape=jax.ShapeDtypeStruct(q.shape, bf16), compiler_params=_cp("parallel"),
    )(q, *tabs)


def _causal_mask(i, j, tq, tk):
    r = lax.broadcasted_iota(jnp.int32, (tq, tk), 0) + i * tq
    c = lax.broadcasted_iota(jnp.int32, (tq, tk), 1) + j * tk
    return c <= r


def _flash_fwd(q, kv, kr, *, tq=1024, tk=1024):
    T = q.shape[0]
    tq = _tile(T, tq)
    tk = _tile(tq, tk)
    per = tq // tk
    H = MLA_HEADS

    def body(q_ref, kn_ref, v_ref, kr_ref, o_ref, lse_ref, m_s, acc):
        i, j = pl.program_id(1), pl.program_id(2)
        last = (i + 1) * per - 1

        @pl.when(j == 0)
        def _():
            m_s[...] = jnp.full_like(m_s, -jnp.inf)
            acc[...] = jnp.zeros_like(acc)

        def step(masked):
            k = jnp.concatenate([kn_ref[...], kr_ref[...]], axis=1)
            s = _dot(q_ref[...], k, NT)
            if masked:
                s = jnp.where(_causal_mask(i, j, tq, tk), s, -jnp.inf)
            m_new = jnp.maximum(m_s[...], jnp.max(s, axis=-1, keepdims=True))
            alpha = jnp.exp(m_s[...] - m_new)
            p = jnp.exp((s - m_new).astype(bf16))
            v1 = jnp.concatenate([v_ref[...], jnp.ones((tk, LANES), bf16)], axis=1)
            acc[...] = alpha * acc[...] + _dot(p, v1)
            m_s[...] = m_new

        pl.when(j < i * per)(functools.partial(step, False))
        pl.when((j >= i * per) & (j <= last))(functools.partial(step, True))

        @pl.when(j == last)
        def _():
            l = acc[:, LANES:]
            o_ref[...] = (acc[:, :LANES] / l).astype(bf16)
            lse_ref[0] = m_s[...] + jnp.log(jnp.max(l, axis=-1, keepdims=True))

    kj = lambda i, j: jnp.minimum(j, (i + 1) * per - 1)
    kblk = lambda off: pl.BlockSpec((tk, LANES), lambda h, i, j: (kj(i, j), 2 * h + off))
    return pl.pallas_call(
        body, name="flash_fwd", grid=(H, T // tq, T // tk),
        in_specs=[pl.BlockSpec((tq, HQ), lambda h, i, j: (i, h)), kblk(0), kblk(1),
                  pl.BlockSpec((tk, LANES), lambda h, i, j: (kj(i, j), 0))],
        out_specs=[pl.BlockSpec((tq, LANES), lambda h, i, j: (i, h)), pl.BlockSpec((1, tq, 1), lambda h, i, j: (h, i, 0))],
        out_shape=[jax.ShapeDtypeStruct((T, H * LANES), bf16), jax.ShapeDtypeStruct((H, T, 1), f32)],
        scratch_shapes=[pltpu.VMEM((tq, 1), f32), pltpu.VMEM((tq, 2 * LANES), f32)],
        compiler_params=_cp("parallel", "parallel", "arbitrary"),
    )(q, kv, kv, kr)


def _flash_bwd(q, kv, kr, o, do, lse, *, tb=512):
    T = q.shape[0]
    tb = _tile(T, tb)
    nb = T // tb
    H = MLA_HEADS

    def body(q_ref, kn_ref, v_ref, kr_ref, o_ref, do_ref, lse_ref, dkv_ref, dkr_ref, dq_ref, dk_acc, dv_acc):
        j, ii = pl.program_id(1), pl.program_id(2)
        i = jnp.maximum(ii, j)

        @pl.when((j == 0) & (ii == 0))
        def _():
            dq_ref[...] = jnp.zeros_like(dq_ref)

        @pl.when(ii == 0)
        def _():
            dk_acc[...] = jnp.zeros_like(dk_acc)
            dv_acc[...] = jnp.zeros_like(dv_acc)

        def step(masked):
            k = jnp.concatenate([kn_ref[...], kr_ref[...]], axis=1)
            p = jnp.exp((_dot(q_ref[...], k, NT) - lse_ref[0]).astype(bf16))
            if masked:
                p = jnp.where(_causal_mask(i, j, tb, tb), p, jnp.zeros_like(p))
            delta = jnp.sum(o_ref[...].astype(f32) * do_ref[...], axis=-1, keepdims=True)
            ds = p * (_bdot(do_ref[...], v_ref[...], NT) - delta).astype(bf16)
            dv_acc[...] += _bdot(p, do_ref[...], TN)
            dk_acc[...] += _bdot(ds, q_ref[...], TN)
            rows = pl.ds(pl.multiple_of(i * tb, tb), tb)
            dq_ref[rows, :] += _bdot(ds, k)

        pl.when(ii > j)(functools.partial(step, False))
        pl.when(ii == j)(functools.partial(step, True))

        @pl.when(ii == nb - 1)
        def _():
            dkv_ref[...] = jnp.concatenate([dk_acc[:, :LANES], dv_acc[...]], axis=1).astype(bf16)
            dkr_ref[...] = dk_acc[:, LANES:]

    qi = lambda h, j, i: jnp.maximum(i, j)
    kblk = lambda off: pl.BlockSpec((tb, LANES), lambda h, j, i: (j, 2 * h + off))
    vec = pl.BlockSpec((1, tb, 1), lambda h, j, i: (h, qi(h, j, i), 0))
    qblk = pl.BlockSpec((tb, LANES), lambda h, j, i: (qi(h, j, i), h))
    return pl.pallas_call(
        body, name="flash_bwd", grid=(H, nb, nb),
        in_specs=[pl.BlockSpec((tb, HQ), lambda h, j, i: (qi(h, j, i), h)), kblk(0), kblk(1),
                  pl.BlockSpec((tb, LANES), lambda h, j, i: (j, 0)), qblk, qblk, vec],
        out_specs=[pl.BlockSpec((tb, HQ), lambda h, j, i: (j, h)), pl.BlockSpec((tb, LANES), lambda h, j, i: (j, h)),
                   pl.BlockSpec((T, HQ), lambda h, j, i: (0, h))],
        out_shape=[jax.ShapeDtypeStruct((T, H * HQ), bf16), jax.ShapeDtypeStruct((T, H * LANES), f32),
                   jax.ShapeDtypeStruct((T, H * HQ), f32)],
        scratch_shapes=[pltpu.VMEM((tb, HQ), f32), pltpu.VMEM((tb, LANES), f32)],
        compiler_params=_cp("parallel", "arbitrary", "arbitrary"),
    )(q, kv, kv, kr, o, do, lse)


HBM_SPEC = pl.BlockSpec(memory_space=pltpu.HBM)
N_CHIPS = 4
N_DEV = 8

BIG = {"even_w_in": 1, "s5_w_glu": 0, "even_w_out": 0, "odd_w_in": 0, "mla_w_uq": 1, "mla_w_ukv": 1, "odd_w_out": 0,
       "ffn_w_in": 2, "ffn_w_out": 1}
LAYERED = ("ffn_w_in", "ffn_w_out")
GROUPS = {"even_in": ("even_w_in",), "even_rest": ("s5_w_glu", "even_w_out"), "ffn0": LAYERED,
          "odd": ("odd_w_in", "mla_w_uq", "mla_w_ukv", "odd_w_out"), "ffn1": LAYERED}
GROUP_LAYER = {"ffn0": 0, "ffn1": 1}


def _place():
    x, y, c = lax.axis_index("x"), lax.axis_index("y"), lax.axis_index("c")
    chips = [(1 - x, y), (x, 1 - y), (1 - x, 1 - y)]
    return x, y, c, chips


def _slab(ref, axis, k, size):
    start = pl.multiple_of(k * size, size if axis == 0 else LANES)
    idx = [slice(None)] * len(ref.shape)
    idx[axis] = pl.ds(start, size)
    return ref.at[tuple(idx)]


SEM_SPEC = pl.BlockSpec(memory_space=pltpu.SEMAPHORE)
ANY_SPEC = pl.BlockSpec(memory_space=pl.ANY)
EFFECT = pltpu.SideEffectType.DATAFLOW_SIDE_EFFECTING


def _hbm(a):
    return pltpu.with_memory_space_constraint(a, pltpu.HBM)


class _Gather:
    copies = 3

    def __init__(self, axis, size):
        self.axis, self.size = axis, size

    def view(self, land, kk):
        return _slab(land, self.axis, kk, self.size)

    def own(self, land, place):
        return self.view(land, 2 * place[0] + place[1])

    def sends(self, src, land, place):
        x, y, c, chips = place
        return [(self.own(land, place) if src is None else src, self.own(land, place), (*chip, c)) for chip in chips]

    def recvs(self, land, place):
        return [self.view(land, 2 * cx + cy) for cx, cy in place[3]]


class _Scatter:
    copies = 3

    def __init__(self, axis, size, layer=None):
        self.axis, self.size, self.layer = axis, size, layer

    def row(self, land, j):
        return land.at[j] if self.layer is None else land.at[j, self.layer]

    def sends(self, src, land, place):
        c, chips = place[2], place[3]
        return [(_slab(src, self.axis, 2 * cx + cy, self.size), self.row(land, j), (cx, cy, c))
                for j, (cx, cy) in enumerate(chips)]

    def recvs(self, land, place):
        return [self.row(land, j) for j in range(3)]


class _ToAll:
    copies = N_DEV - 1

    def __init__(self, size):
        self.size = size

    def sends(self, src, land, place):
        x, y, c, _ = place
        flip = lambda v, bit: 1 - v if bit else v
        own = _slab(land, 0, 4 * x + 2 * y + c, self.size)
        return [(own, own, (flip(x, m & 4), flip(y, m & 2), flip(c, m & 1))) for m in range(1, N_DEV)]

    def recvs(self, land, place):
        x, y, c, _ = place
        d = 4 * x + 2 * y + c
        return [_slab(land, 0, d ^ m, self.size) for m in range(1, N_DEV)]


def _unique(arrays):
    out, index = [], {}
    for a in arrays:
        if a is not None and id(a) not in index:
            index[id(a)] = len(out)
            out.append(a)
    return out, index


def _sem_base(routes):
    base = [0]
    for r in routes:
        base.append(base[-1] + r.copies)
    return base


def _push_start(name, items):
    n = len(items)
    base = _sem_base([it[0] for it in items])
    arrays, index = _unique([it[1] for it in items] + [it[2] for it in items])
    na = len(arrays)

    def body(*refs):
        arr, send, recv, token = refs[:na], refs[na], refs[na + 1], refs[-1]
        place = _place()
        for i, (route, src, land) in enumerate(items):
            s_ref = None if src is None else arr[index[id(src)]]
            for j, (s, d, dev) in enumerate(route.sends(s_ref, arr[index[id(land)]], place)):
                pltpu.make_async_remote_copy(src_ref=s, dst_ref=d, send_sem=send.at[base[i] + j], recv_sem=recv.at[base[i] + j],
                                             device_id=dev, device_id_type=MESH).start()
        token[...] = jnp.zeros_like(token)

    res = pl.pallas_call(
        body, name=name,
        out_shape=[pltpu.SemaphoreType.DMA((base[-1],)), pltpu.SemaphoreType.DMA((base[-1],))]
        + [pltpu.HBM(a.shape, a.dtype) for a in arrays] + [jax.ShapeDtypeStruct((SUBLANES, LANES), f32)],
        in_specs=[HBM_SPEC] * na, out_specs=[SEM_SPEC, SEM_SPEC] + [HBM_SPEC] * na + [pl.BlockSpec(memory_space=pltpu.VMEM)],
        input_output_aliases={i: 2 + i for i in range(na)},
        compiler_params=pltpu.CompilerParams(has_side_effects=EFFECT),
    )(*[_hbm(a) for a in arrays])
    thru = lambda a: None if a is None else res[2 + index[id(a)]]
    return (res[0], res[1]), [thru(it[1]) for it in items], [thru(it[2]) for it in items], res[-1]


def _push_wait(name, groups, after):
    arrays, index = _unique([a for _, _, srcs, lands in groups for a in list(srcs) + list(lands)])
    na, ng = len(arrays), len(groups)

    def body(*refs):
        arr, sems = refs[:na], refs[na:na + 2 * ng]
        place = _place()
        for g, (routes, _, srcs, lands) in enumerate(groups):
            send, recv = sems[2 * g], sems[2 * g + 1]
            base = _sem_base(routes)
            for i, route in enumerate(routes):
                src, land = None if srcs[i] is None else arr[index[id(srcs[i])]], arr[index[id(lands[i])]]
                for j, ((s, d, dev), mine) in enumerate(zip(route.sends(src, land, place), route.recvs(land, place))):
                    cp = pltpu.make_async_remote_copy(src_ref=s, dst_ref=mine, send_sem=send.at[base[i] + j],
                                                      recv_sem=recv.at[base[i] + j], device_id=dev,
                                                      device_id_type=MESH)
                    cp.wait_send()
                    cp.wait_recv()

    sem_args = [s for g in groups for s in g[1]]
    res = pl.pallas_call(
        body, name=name, out_shape=[pltpu.HBM(a.shape, a.dtype) for a in arrays],
        in_specs=[HBM_SPEC] * na + [SEM_SPEC] * (2 * ng) + [ANY_SPEC] * len(after), out_specs=[HBM_SPEC] * na,
        input_output_aliases={i: i for i in range(na)},
        compiler_params=pltpu.CompilerParams(has_side_effects=EFFECT),
    )(*arrays, *sem_args, *after)
    return [[res[index[id(a)]] for a in g[3]] for g in groups]


def _place_slab(block, axis, slabs, idx, dtype, *, name):
    R, C = block.shape
    tm = _rows(R, C)
    nr = R // tm
    out_map = (lambda i, k: (i, k[0])) if axis == 1 else (lambda i, k: (k[0] * nr + i, 0))

    def body(k_ref, x_ref, o_ref):
        o_ref[...] = x_ref[...].astype(dtype)

    full = (R, C * slabs) if axis == 1 else (R * slabs, C)
    return pl.pallas_call(
        body, name=name, out_shape=jax.ShapeDtypeStruct(full, dtype),
        grid_spec=pltpu.PrefetchScalarGridSpec(
            num_scalar_prefetch=1, grid=(nr,), in_specs=[pl.BlockSpec((tm, C), lambda i, k: (i, 0))],
            out_specs=pl.BlockSpec((tm, C), out_map)),
        compiler_params=_cp("parallel"),
    )(idx, block)


def _swap_with_sibling(parts, tag):
    names = list(parts)

    def body(*refs):
        n = len(names)
        ins, outs, send, recv = refs[:n], refs[n:2 * n], refs[-2], refs[-1]
        x, y, c, _ = _place()
        cps = [pltpu.make_async_remote_copy(src_ref=ins[a], dst_ref=outs[a], send_sem=send.at[a], recv_sem=recv.at[a],
                                            device_id=(x, y, 1 - c), device_id_type=MESH) for a in range(n)]
        for cp in cps:
            cp.start()
        for cp in cps:
            cp.wait_recv()
        for cp in cps:
            cp.wait_send()

    res = pl.pallas_call(
        body, name=f"swap_with_sibling_{tag}", in_specs=[HBM_SPEC] * len(names), out_specs=[HBM_SPEC] * len(names),
        out_shape=[jax.ShapeDtypeStruct(parts[n].shape, parts[n].dtype) for n in names],
        scratch_shapes=[pltpu.SemaphoreType.DMA((len(names),)), pltpu.SemaphoreType.DMA((len(names),))],
    )(*[parts[n] for n in names])
    return dict(zip(names, res))


ELEMENTWISE_BLOCK_BYTES = 1 << 20


def _rows(r, c):
    for t in (512, 256, 128, 64, 32, 16, 8):
        if r % t == 0 and t * c * 4 <= ELEMENTWISE_BLOCK_BYTES:
            return t
    return r


def _sum4(owns, axis, recv, kidx, *, name):
    L = len(owns)
    R, C = recv.shape[2:]
    tm = _rows(R, C)
    nr = R // tm

    def body(k_ref, *refs):
        own_refs, r_ref, out_ref = refs[:L], refs[L], refs[L + 1]
        for li in range(L):
            @pl.when(pl.program_id(0) == li)
            def _(o_ref=own_refs[li]):
                out_ref[...] = ((o_ref[...] + r_ref[0, 0].astype(f32)) + r_ref[1, 0].astype(f32)) + r_ref[2, 0].astype(f32)

    own_map = (lambda l, i, k: (i, k[0])) if axis == 1 else (lambda l, i, k: (k[0] * nr + i, 0))
    return pl.pallas_call(
        body, name=name, out_shape=jax.ShapeDtypeStruct((L * R, C), f32),
        grid_spec=pltpu.PrefetchScalarGridSpec(
            num_scalar_prefetch=1, grid=(L, nr),
            in_specs=[pl.BlockSpec((tm, C), own_map)] * L + [pl.BlockSpec((3, 1, tm, C), lambda l, i, k: (0, l, i, 0))],
            out_specs=pl.BlockSpec((tm, C), lambda l, i, k: (l * nr + i, 0))),
        compiler_params=_cp("parallel", "parallel"),
    )(kidx, *owns, recv)


def _adamw(w, m, v, parts, *, name):
    R, C = w.shape
    tm = _rows(R, C)
    npart = len(parts)

    def body(*refs):
        w_ref, m_ref, v_ref = refs[:3]
        g_ref, d_ref, m2_ref, v2_ref = refs[3 + npart:]
        g = refs[3][...]
        for p_ref in refs[4:3 + npart]:
            g = g + p_ref[...]
        g_ref[...] = g
        d_ref[...], m2_ref[...], v2_ref[...] = _adam_math(w_ref[...], m_ref[...], v_ref[...], g)

    blk = pl.BlockSpec((tm, C), lambda i: (i, 0))
    return pl.pallas_call(
        body, name=name, grid=(R // tm,),
        in_specs=[blk] * (3 + npart), out_specs=[blk] * 4,
        out_shape=[jax.ShapeDtypeStruct((R, C), f32)] * 4, compiler_params=_cp("parallel"),
    )(w, m, v, *parts)


def _adam_math(w, m, v, g):
    m2 = ADAM_B1 * m + (1.0 - ADAM_B1) * g
    v2 = ADAM_B2 * v + (1.0 - ADAM_B2) * (g * g)
    m_hat = m2 / (1.0 - ADAM_B1 ** ADAM_STEP)
    v_hat = v2 / (1.0 - ADAM_B2 ** ADAM_STEP)
    return -ADAM_LR * (m_hat / (jnp.sqrt(v_hat) + ADAM_EPS) + ADAM_WD * w), m2, v2


def _adamw_small(landed, w, m, v, kidx, ra, rb):
    rs = ra + N_CHIPS * rb

    def body(k_ref, l_ref, w_ref, m_ref, v_ref, g_ref, d_ref, m2_ref, v2_ref):
        mine = pl.multiple_of(ra + k_ref[0] * rb, SUBLANES)
        for lo, n, off in ((0, ra, 0), (ra, rb, mine)):
            g = l_ref[pl.ds(off, n), :]
            for d in range(1, N_DEV):
                g = g + l_ref[pl.ds(d * rs + off, n), :]
            rows = pl.ds(lo, n)
            delta, m2, v2 = _adam_math(w_ref[rows, :], m_ref[rows, :], v_ref[rows, :], g)
            g_ref[rows, :] = g
            d_ref[rows, :] = delta
            m2_ref[rows, :] = m2
            v2_ref[rows, :] = v2

    vmem = pl.BlockSpec(memory_space=pltpu.VMEM)
    return pl.pallas_call(
        body, name="adamw_small", out_shape=[jax.ShapeDtypeStruct(w.shape, f32)] * 4,
        grid_spec=pltpu.PrefetchScalarGridSpec(num_scalar_prefetch=1, grid=(), in_specs=[vmem] * 4, out_specs=[vmem] * 4),
        compiler_params=_cp(),
    )(kidx, landed, w, m, v)


def _pad_odd(w):
    return jnp.pad(w, ((0, 0), (0, ODD_PAD - w.shape[1])))


def _uq_cat(w):
    r = w.shape[0]
    return jnp.pad(w.reshape(r, MLA_HEADS, MLA_QK), ((0, 0), (0, 0), (0, HQ - MLA_QK))).reshape(r, MLA_HEADS * HQ)


def _uq_uncat(w):
    r = w.shape[0]
    return w.reshape(r, MLA_HEADS, HQ)[:, :, :MLA_QK].reshape(r, MLA_HEADS * MLA_QK)


def _to_segments(v):
    T, C = v.shape
    return v.reshape(S5_SEG, T // S5_SEG, C).transpose(1, 0, 2).reshape(T, C)


def _from_segments(v):
    T, C = v.shape
    return v.reshape(T // S5_SEG, S5_SEG, C).transpose(1, 0, 2).reshape(T, C)


def _s5_rb(T):
    return min(512, T)


def _ffn_fwd(h, g, w_in, cw, cb, w_out, tag):
    hn = _rms_fwd(h, g, name=f"ffn{tag}_norm")
    au = _mm(hn, w_in, out_dtype=bf16, name=f"ffn{tag}_in", tn=1408)
    z = _ffn_mid_fwd(au, cw, cb, name=f"ffn{tag}_mid")
    return _mm(z, w_out, res=h, name=f"ffn{tag}_out", tk=1408), (hn, au, z)


def _ffn_bwd(h, g, w_in, cw, cb, w_out, saved, dh, tag, dep=None):
    hn, au, z = saved
    dz = _mm(dh, w_out, tb=True, out_dtype=bf16, name=f"ffn{tag}_dz", tn=1408, dep=dep)
    dw_out = _mm(z, dh, ta=True, also_bf16=True, name=f"ffn{tag}_dwout", tm=1408)
    dau, dcw, dcb = _ffn_mid_bwd(au, cw, cb, dz, name=f"ffn{tag}_dmid")
    dhn = _mm(dau, w_in, tb=True, out_dtype=bf16, name=f"ffn{tag}_dhn", tk=1408)
    dw_in = _mm(hn, dau, ta=True, also_bf16=True, name=f"ffn{tag}_dwin", tn=1408)
    dh_in, dg = _rms_bwd(h, g, dhn, dh, name=f"ffn{tag}_dnorm")
    return dh_in, dg, dw_in, dcw, dcb, dw_out


def _local_step(x, positions, target, get_w, P, put_g):
    T = x.shape[0]
    rb = _s5_rb(T)
    row = lambda v: v.reshape(1, -1)
    g_mix, g_ffn = P["norm_mix_g"], P["norm_ffn_g"]
    lbl, hng = P["hgrn_lb_logits"], P["hgrn_norm_g"]
    dsk, bg = P["s5_d"], P["s5_b_glu"]
    qg, kvg = P["mla_q_norm_g"], P["mla_kv_norm_g"]
    cw, cb = P["ffn_conv_w"], P["ffn_conv_b"]

    col = lambda v: v.reshape(S5_N, 1)
    disc_in = (col(P["s5_a_re"]), col(P["s5_a_im"]), col(jnp.repeat(P["s5_log_dt"].reshape(S5_GROUPS), S5_STATE)),
               P["s5_b_re"].reshape(S5_N, S5_GROUP), P["s5_b_im"].reshape(S5_N, S5_GROUP))
    abr, abi, bbr, bbi = _s5_disc_fwd(*disc_in)
    ar, ai = abr.reshape(1, S5_N), abi.reshape(1, S5_N)
    bbr3, bbi3 = bbr.reshape(S5_GROUPS, S5_STATE, S5_GROUP), bbi.reshape(S5_GROUPS, S5_STATE, S5_GROUP)
    bre, bim = _blockdiag(bbr3, True).astype(bf16), _blockdiag(bbi3, True).astype(bf16)
    bret, bimt = _blockdiag(bbr3).astype(bf16), _blockdiag(bbi3).astype(bf16)
    c_re, c_im = P["s5_c_re"].reshape(S5_GROUPS, S5_GROUP, S5_STATE), P["s5_c_im"].reshape(S5_GROUPS, S5_GROUP, S5_STATE)
    cre, cim = _blockdiag(c_re, True).astype(bf16), _blockdiag(c_im, True).astype(bf16)
    cret, cimt = _blockdiag(c_re).astype(bf16), _blockdiag(c_im).astype(bf16)

    hn0 = _rms_fwd(x, g_mix[0:1], name="mix0_norm")
    We = get_w("even_in", hn0)
    proj_e = _mm(hn0, We["even_w_in"], name="even_in", tn=1280)
    Wr = get_w("even_rest", proj_e)
    ya, states = _hgrn_fwd(proj_e, lbl, hng)
    u_seg = _to_segments(proj_e[:, 4 * 512:])
    fr, fi = _s5_final(u_seg, bre, bim, ar, ai, rb=rb)
    yb_seg, s0r, s0i = _s5_fwd(u_seg, bre, bim, ar, ai, fr, fi, cre, cim, dsk, Wr["s5_w_glu"], bg, rb=rb)
    ycat = jnp.concatenate([ya, _from_segments(yb_seg)], axis=1)
    h1 = _mm(ycat, Wr["even_w_out"], res=x, name="even_out")
    Wf0 = get_w("ffn0", h1)
    h2, ffn0 = _ffn_fwd(h1, g_ffn[0:1], Wf0["ffn_w_in"], cw[0], cb[0:1], Wf0["ffn_w_out"], 0)

    tabs = _rope_tables(positions)
    hn2 = _rms_fwd(h2, g_mix[1:2], name="mix1_norm")
    Wo = get_w("odd", hn2)
    proj_o = _mm(hn2, Wo["odd_w_in"], name="odd_in")
    cqn, ckvn, kr = _mla_prep_fwd(proj_o, qg, kvg, tabs)
    q = _q_post(_mm(cqn, Wo["mla_w_uq"], name="mla_uq"), tabs, transpose=False, name="q_post")
    kvb = _mm(ckvn, Wo["mla_w_ukv"], out_dtype=bf16, name="mla_ukv")
    o, lse = _flash_fwd(q, kvb, kr)
    h3 = _mm(o, Wo["odd_w_out"], res=h2, name="odd_out")
    Wf1 = get_w("ffn1", h3)
    h4, ffn1 = _ffn_fwd(h3, g_ffn[1:2], Wf1["ffn_w_in"], cw[1], cb[1:2], Wf1["ffn_w_out"], 1)
    loss, dh4, dg_final = _loss_head(h4, row(P["final_norm_g"]), target)

    dh3, dg_ffn1, dw_fin1, dcw1, dcb1, dw_fout1 = _ffn_bwd(
        h3, g_ffn[1:2], Wf1["ffn_w_in"], cw[1], cb[1:2], Wf1["ffn_w_out"], ffn1, dh4, 1)
    sent = put_g("ffn1", {"ffn_w_in": dw_fin1, "ffn_w_out": dw_fout1})
    do = _mm(dh3, Wo["odd_w_out"], tb=True, out_dtype=bf16, name="odd_do", dep=sent)
    dw_oout = _mm(o, dh3, ta=True, also_bf16=True, name="odd_dwout")
    dkv, dkr_h, dq = _flash_bwd(q, kvb, kr, o, do, lse)
    dq = _q_post(dq, tabs, transpose=True, name="dq_post")
    dw_uq = _mm(cqn, dq, ta=True, also_bf16=True, name="mla_dwuq")
    dcqn = _mm(dq, Wo["mla_w_uq"], tb=True, name="mla_dcq")
    dw_ukv = _mm(ckvn, dkv, ta=True, also_bf16=True, name="mla_dwukv")
    dckvn = _mm(dkv, Wo["mla_w_ukv"], tb=True, name="mla_dckv")
    dproj_o, dqg, dkvg = _mla_prep_bwd(proj_o, qg, kvg, tabs, dcqn, dckvn, dkr_h)
    dhn2 = _mm(dproj_o, Wo["odd_w_in"], tb=True, out_dtype=bf16, name="odd_dhn")
    dw_oin = _mm(hn2, dproj_o, ta=True, also_bf16=True, name="odd_dwin")
    sent = put_g("odd", {"odd_w_in": dw_oin, "mla_w_uq": dw_uq, "mla_w_ukv": dw_ukv, "odd_w_out": dw_oout})
    dh2, dg_mix1 = _rms_bwd(h2, g_mix[1:2], dhn2, dh3, name="mix1_dnorm")

    dh1, dg_ffn0, dw_fin0, dcw0, dcb0, dw_fout0 = _ffn_bwd(
        h1, g_ffn[0:1], Wf0["ffn_w_in"], cw[0], cb[0:1], Wf0["ffn_w_out"], ffn0, dh2, 0, dep=sent)
    sent = put_g("ffn0", {"ffn_w_in": dw_fin0, "ffn_w_out": dw_fout0})
    dycat = _mm(dh1, Wr["even_w_out"], tb=True, name="even_dy", dep=sent)
    dw_eout = _mm(ycat, dh1, ta=True, also_bf16=True, name="even_dwout")
    dq_h, df_h, di_h, dg_h, dlbl, dhng = _hgrn_bwd(proj_e, lbl, hng, states, dycat)
    dyb_seg = _to_segments(dycat[:, 512:])
    dy_s5, glr, gli, dcre, dcim, dd, dwg, dbg = _s5_bwd_a(
        u_seg, bre, bim, ar, ai, s0r, s0i, cre, cim, cret, cimt, dsk, Wr["s5_w_glu"], bg, dyb_seg, rb=rb)
    du_seg, dbre, dbim, dar, dai = _s5_bwd_b(
        u_seg, bre, bim, bret, bimt, ar, ai, s0r, s0i, glr, gli, cret, cimt, dsk, dy_s5, rb=rb)
    dproj_e = jnp.concatenate([dq_h, df_h, di_h, dg_h, _from_segments(du_seg)], axis=1)
    dhn0 = _mm(dproj_e, We["even_w_in"], tb=True, out_dtype=bf16, name="even_dhn", tk=1280)
    dw_ein = _mm(hn0, dproj_e, ta=True, also_bf16=True, name="even_dwin", tn=1280)
    dx, dg_mix0 = _rms_bwd(x, g_mix[0:1], dhn0, dh1, name="mix0_dnorm")

    unblk = lambda m, a, b: jnp.swapaxes(_blockdiag_t(m, a, b), 1, 2)
    dbbr = unblk(dbre, S5_GROUP, S5_STATE).reshape(S5_N, S5_GROUP)
    dbbi = unblk(dbim, S5_GROUP, S5_STATE).reshape(S5_N, S5_GROUP)
    d_ar, d_ai, d_ldt, d_br, d_bi = _s5_disc_bwd(*disc_in, (dar.reshape(S5_N, 1), dai.reshape(S5_N, 1), dbbr, dbbi))
    small = {
        "norm_mix_g": jnp.concatenate([dg_mix0, dg_mix1], axis=0),
        "norm_ffn_g": jnp.concatenate([dg_ffn0, dg_ffn1], axis=0),
        "final_norm_g": dg_final.reshape(-1),
        "hgrn_lb_logits": dlbl, "hgrn_norm_g": dhng,
        "s5_a_re": d_ar.reshape(1, S5_GROUPS, S5_STATE), "s5_a_im": d_ai.reshape(1, S5_GROUPS, S5_STATE),
        "s5_log_dt": d_ldt.reshape(S5_GROUPS, S5_STATE).sum(axis=1).reshape(1, S5_GROUPS),
        "s5_b_re": d_br.reshape(1, S5_GROUPS, S5_STATE, S5_GROUP), "s5_b_im": d_bi.reshape(1, S5_GROUPS, S5_STATE, S5_GROUP),
        "s5_c_re": unblk(dcre, S5_STATE, S5_GROUP).reshape(1, S5_GROUPS, S5_GROUP, S5_STATE),
        "s5_c_im": unblk(dcim, S5_STATE, S5_GROUP).reshape(1, S5_GROUPS, S5_GROUP, S5_STATE),
        "s5_d": dd, "s5_b_glu": dbg, "mla_q_norm_g": dqg, "mla_kv_norm_g": dkvg,
        "ffn_conv_w": jnp.stack([dcw0, dcw1]), "ffn_conv_b": jnp.concatenate([dcb0, dcb1], axis=0),
    }
    put_g("even", {"even_w_in": dw_ein, "s5_w_glu": (dwg, dwg.astype(bf16)), "even_w_out": dw_eout}, small)
    return loss, dx


WEIGHTS = ["norm_mix_g", "norm_ffn_g", "final_norm_g", "even_w_in", "hgrn_lb_logits", "hgrn_norm_g", "s5_a_re", "s5_a_im",
           "s5_log_dt", "s5_b_re", "s5_b_im", "s5_c_re", "s5_c_im", "s5_d", "s5_w_glu", "s5_b_glu", "even_w_out", "odd_w_in",
           "mla_q_norm_g", "mla_w_uq", "mla_kv_norm_g", "mla_w_ukv", "odd_w_out", "ffn_w_in", "ffn_conv_w", "ffn_conv_b",
           "ffn_w_out"]
SMALL_SHARDED = {"mla_q_norm_g": 1, "mla_kv_norm_g": 1, "ffn_conv_w": 2}
SMALL = [n for n in WEIGHTS if n not in BIG]
SMALL_REP = [n for n in SMALL if n not in SMALL_SHARDED]


def _pack_rows(shapes):
    n = sum(math.prod(s) for s in shapes)
    return -(-n // (SUBLANES * LANES)) * SUBLANES


def _pack(arrays, rows):
    flat = jnp.concatenate([a.reshape(-1) for a in arrays])
    return jnp.pad(flat, (0, rows * LANES - flat.shape[0])).reshape(rows, LANES)


def _unpack(block, shapes):
    flat, out, off = block.reshape(-1), [], 0
    for s in shapes:
        n = math.prod(s)
        out.append(flat[off:off + n].reshape(s))
        off += n
    return out


def kernel(x, positions, norm_mix_g, norm_ffn_g, final_norm_g, even_w_in, hgrn_lb_logits, hgrn_norm_g, s5_a_re, s5_a_im, s5_log_dt, s5_b_re, s5_b_im, s5_c_re, s5_c_im, s5_d, s5_w_glu, s5_b_glu, even_w_out, odd_w_in, mla_q_norm_g, mla_w_uq, mla_kv_norm_g, mla_w_ukv, odd_w_out, ffn_w_in, ffn_conv_w, ffn_conv_b, ffn_w_out, loss_target, m_norm_mix_g, m_norm_ffn_g, m_final_norm_g, m_even_w_in, m_hgrn_lb_logits, m_hgrn_norm_g, m_s5_a_re, m_s5_a_im, m_s5_log_dt, m_s5_b_re, m_s5_b_im, m_s5_c_re, m_s5_c_im, m_s5_d, m_s5_w_glu, m_s5_b_glu, m_even_w_out, m_odd_w_in, m_mla_q_norm_g, m_mla_w_uq, m_mla_kv_norm_g, m_mla_w_ukv, m_odd_w_out, m_ffn_w_in, m_ffn_conv_w, m_ffn_conv_b, m_ffn_w_out, v_norm_mix_g, v_norm_ffn_g, v_final_norm_g, v_even_w_in, v_hgrn_lb_logits, v_hgrn_norm_g, v_s5_a_re, v_s5_a_im, v_s5_log_dt, v_s5_b_re, v_s5_b_im, v_s5_c_re, v_s5_c_im, v_s5_d, v_s5_w_glu, v_s5_b_glu, v_even_w_out, v_odd_w_in, v_mla_q_norm_g, v_mla_w_uq, v_mla_kv_norm_g, v_mla_w_ukv, v_odd_w_out, v_ffn_w_in, v_ffn_conv_w, v_ffn_conv_b, v_ffn_w_out):
    args = dict(locals())
    w = {n: args[n] for n in WEIGHTS}
    m = {n: args["m_" + n] for n in WEIGHTS}
    v = {n: args["v_" + n] for n in WEIGHTS}
    k = 2 * lax.axis_index("x") + lax.axis_index("y")
    kidx = k.reshape(1).astype(jnp.int32)
    axis2d = lambda n: BIG[n] - (1 if n in LAYERED else 0)
    slab = lambda n: w[n].shape[1 + axis2d(n)]

    small_sh_shapes = [w[n].shape for n in SMALL_SHARDED]
    rb = _pack_rows(small_sh_shapes)
    items = {}
    for group, names in GROUPS.items():
        layer = GROUP_LAYER.get(group, 0)
        items[group] = [(_Gather(axis2d(n), slab(n)), None,
                         _place_slab(w[n][layer], axis2d(n), N_CHIPS, kidx, bf16, name=f"place_{n}_{layer}")) for n in names]
    items["even_in"].append((_Gather(0, rb), None,
                             _place_slab(_pack([w[n] for n in SMALL_SHARDED], rb), 0, N_CHIPS, kidx, f32, name="place_small")))
    gathers, tokens = {}, []
    for group in GROUPS:
        sems, srcs, lands, token = _push_start(f"gather_start_{group}", items[group])
        gathers[group] = ([it[0] for it in items[group]], sems, srcs, lands)
        tokens.append(token[0, 0])
    started = functools.reduce(jnp.add, tokens)

    def landed(group, after):
        return _push_wait(f"gather_wait_{group}", [gathers[group]], [after])[0]

    even = landed("even_in", (started + norm_mix_g[0, 0]).reshape(1))
    per_chip = [_unpack(even[-1][c * rb:(c + 1) * rb], small_sh_shapes) for c in range(N_CHIPS)]
    P = {n: w[n] for n in SMALL_REP}
    for i, (n, ax) in enumerate(SMALL_SHARDED.items()):
        P[n] = jnp.concatenate([per_chip[c][i] for c in range(N_CHIPS)], axis=ax)
    P["mla_q_norm_g"], P["mla_kv_norm_g"] = P["mla_q_norm_g"].reshape(1, -1), P["mla_kv_norm_g"].reshape(1, -1)
    fix_w = {"odd_w_in": _pad_odd, "mla_w_uq": _uq_cat}

    def get_w(group, after):
        full = even if group == "even_in" else landed(group, after)
        return {n: fix_w.get(n, lambda a: a)(a) for n, a in zip(GROUPS[group], full)}

    fix_g = {"odd_w_in": lambda g: g[:, :odd_w_in.shape[2]], "mla_w_uq": _uq_uncat}
    g32, scatters, land_now = {}, {}, {}
    ra = _pack_rows([w[n].shape for n in SMALL_REP])
    rs = ra + N_CHIPS * rb
    didx = (2 * kidx + lax.axis_index("c")).astype(jnp.int32)

    def put_g(group, grads, small=None):
        layer = GROUP_LAYER.get(group)
        routes, srcs, names = [], [], list(grads)
        for n in names:
            f = fix_g.get(n, lambda g: g)
            g32.setdefault(n, {})[layer or 0] = f(grads[n][0])
            routes.append(_Scatter(axis2d(n), slab(n), layer if n in LAYERED else None))
            srcs.append(f(grads[n][1]))
            if n not in land_now:
                land_now[n] = lax.empty((3,) + w[n].shape[0 if n in LAYERED else 1:], bf16)
        if small is not None:
            blocks = [_pack([small[n] for n in SMALL_REP], ra)]
            for chip in range(N_CHIPS):
                sl = lambda n, ax: lax.slice_in_dim(small[n].reshape(w[n].shape[:ax] + (-1,) + w[n].shape[ax + 1:]),
                                                    chip * w[n].shape[ax], (chip + 1) * w[n].shape[ax], axis=ax)
                blocks.append(_pack([sl(n, ax) for n, ax in SMALL_SHARDED.items()], rb))
            names.append("small")
            routes.append(_ToAll(rs))
            srcs.append(None)
            land_now["small"] = _place_slab(jnp.concatenate(blocks), 0, N_DEV, didx, f32, name="place_small_grads")
        sems, srcs, lands, token = _push_start(f"scatter_start_{group}", [(r, s, land_now[n]) for r, s, n in zip(routes, srcs, names)])
        land_now.update(zip(names, lands))
        scatters[group] = (routes, sems, srcs, names)
        sent.append(token)
        return token

    sent = []
    loss, dx = _local_step(x[0], positions[0], loss_target[0], get_w, P, put_g)
    sent_last = sent[-1]
    loss = lax.psum(loss[0, 0], ("x", "y", "c"))

    out = {}

    def finish(tag, groups, after):
        waits = [(scatters[g][0], scatters[g][1], scatters[g][2], [land_now[n] for n in scatters[g][3]]) for g in groups]
        for g, lands in zip(groups, _push_wait(f"scatter_wait_{tag}", waits, after)):
            land_now.update(zip(scatters[g][3], lands))
        names = [n for n in dict.fromkeys(n for g in groups for n in scatters[g][3]) if n != "small"]
        part = {}
        for n in names:
            recv = land_now[n] if n in LAYERED else land_now[n][:, None]
            part[n] = _sum4([g32[n][l] for l in sorted(g32[n])], axis2d(n), recv, kidx, name=f"sum4_{n}")
        other = _swap_with_sibling(part, tag)
        done = []
        for n in names:
            C = part[n].shape[-1]
            res = _adamw(w[n].reshape(-1, C), m[n].reshape(-1, C), v[n].reshape(-1, C), [part[n], other[n]], name=f"adamw_{n}")
            out[n] = [r.reshape(w[n].shape) for r in res]
            done.append(res[0])
        return done

    done = finish("a", ["ffn1", "odd", "ffn0"], [dx, sent_last])
    finish("b", ["even"], done)

    order = SMALL_REP + list(SMALL_SHARDED)
    packed = lambda src: jnp.concatenate([_pack([src[n] for n in SMALL_REP], ra), _pack([src[n] for n in SMALL_SHARDED], rb)])
    res = _adamw_small(land_now["small"], packed(w), packed(m), packed(v), kidx, ra, rb)
    for r in res:
        parts = _unpack(r[:ra], [w[n].shape for n in SMALL_REP]) + _unpack(r[ra:], small_sh_shapes)
        for n, a in zip(order, parts):
            out.setdefault(n, []).append(a)

    return (loss, dx[None], *[out[n][0] for n in WEIGHTS], *[out[n][1] for n in WEIGHTS],
            *[out[n][2] for n in WEIGHTS], *[out[n][3] for n in WEIGHTS])
```

```python
import functools
import math

import jax
import jax.numpy as jnp
from jax import lax
from jax.experimental import pallas as pl
from jax.experimental.pallas import tpu as pltpu

f32, bf16 = jnp.float32, jnp.bfloat16
EPS = 1e-6
LANES = 128
SUBLANES = 8
VMEM_BYTES = 48 * 1024 * 1024
HGRN_CHUNK = 64
HGRN_HEADS = 4
S5_GROUPS, S5_STATE, S5_GROUP = 32, 64, 16
S5_N = S5_GROUPS * S5_STATE
S5_SEG = SUBLANES
MLA_HEADS, MLA_NOPE, MLA_ROPE, MLA_V = 8, 128, 64, 128
MLA_QK = MLA_NOPE + MLA_ROPE
MLA_Q_RANK, MLA_KV_RANK = 384, 256
ROPE_THETA = 10000.0
D_FF = 2816
ADAM_LR, ADAM_B1, ADAM_B2, ADAM_EPS, ADAM_WD, ADAM_STEP = 0.001, 0.9, 0.999, 1e-08, 0.01, 10
MESH = pl.DeviceIdType.MESH
HI = lax.Precision.HIGHEST


def _cp(*dims):
    return pltpu.CompilerParams(dimension_semantics=dims if dims else None, vmem_limit_bytes=VMEM_BYTES)


def _tile(n, t):
    if n <= t:
        return n
    c = (t // LANES) * LANES
    while c >= LANES:
        if n % c == 0:
            return c
        c -= LANES
    return n


def _dot(a, b, dn=None, precision=None):
    if dn is None:
        dn = (((a.ndim - 1,), (0,)), ((), ()))
    return lax.dot_general(a, b, dn, preferred_element_type=f32, precision=precision)


NT = (((1,), (1,)), ((), ()))
TN = (((0,), (0,)), ((), ()))


def _bdot(a, b, dn=None):
    return _dot(a.astype(bf16), b.astype(bf16), dn)


def _mm(a, b, *, name, ta=False, tb=False, out_dtype=f32, res=None, also_bf16=False, tm=1024, tn=1024, tk=1024, dep=None,
        norm_g=None, norm_bwd=None):
    halves = lambda s: (s[1], 2 * s[2]) if len(s) == 3 else s
    M, K = (a.shape[1], a.shape[0]) if ta else halves(a.shape)
    N = b.shape[0] if tb else halves(b.shape)[1]
    rows = norm_g is not None or norm_bwd is not None
    if rows:
        tm, tn = min(tm, 512), N
    tm, tn, tk = _tile(M, tm), _tile(N, tn), _tile(K, tk)
    if a.ndim == 3:
        tk = _tile(K // 2, tk)
    if b.ndim == 3:
        tn = _tile(N // 2, tn)
    nk = K // tk
    dn = (((0 if ta else 1,), (1 if tb else 0,)), ((), ()))
    extra = [] if norm_bwd is None else list(norm_bwd)
    if norm_g is not None:
        extra.append(norm_g)

    def body(*refs):
        a_ref, b_ref = refs[0], refs[1]
        r_ref = refs[2] if res is not None else None
        nin = 2 + (res is not None) + (dep is not None) + len(extra)
        ex = refs[nin - len(extra):nin]
        outs = refs[nin:-1]
        acc = refs[-1]
        k = pl.program_id(2)
        p = _bdot(a_ref[...], b_ref[...], dn)

        @pl.when(k == 0)
        def _():
            acc[...] = p

        @pl.when(k > 0)
        def _():
            acc[...] += p

        @pl.when(k == nk - 1)
        def _():
            r = acc[...]
            if norm_bwd is not None:
                r, dg = _rms_bwd_math(ex[0][...], ex[1][...], r)

                @pl.when(pl.program_id(0) == 0)
                def _():
                    outs[1][...] = dg

                @pl.when(pl.program_id(0) > 0)
                def _():
                    outs[1][...] += dg

            if r_ref is not None:
                r = r + r_ref[...]
            outs[0][...] = r.astype(out_dtype)
            if also_bf16:
                outs[1][...] = r.astype(bf16)
            if norm_g is not None:
                outs[1][...] = _rms(r, ex[-1][...]).astype(bf16)

    a_spec = pl.BlockSpec((tk, tm), lambda i, j, k: (k, i)) if ta else pl.BlockSpec((tm, tk), lambda i, j, k: (i, k))
    b_spec = pl.BlockSpec((tn, tk), lambda i, j, k: (j, k)) if tb else pl.BlockSpec((tk, tn), lambda i, j, k: (k, j))
    if a.ndim == 3:
        kh = K // 2 // tk
        a_spec = pl.BlockSpec((None, tm, tk), lambda i, j, k: (k // kh, i, k % kh))
    if b.ndim == 3:
        nh = N // 2 // tn
        b_spec = pl.BlockSpec((None, tk, tn), lambda i, j, k: (j // nh, k, j % nh))
    o_spec = pl.BlockSpec((tm, tn), lambda i, j, k: (i, j))
    in_specs, args = [a_spec, b_spec], [a, b]
    if res is not None:
        in_specs.append(o_spec)
        args.append(res)
    if dep is not None:
        in_specs.append(pl.BlockSpec(memory_space=pl.ANY))
        args.append(dep)
    vec = pl.BlockSpec((1, tn), lambda i, j, k: (0, j))
    if norm_bwd is not None:
        in_specs += [o_spec, vec]
    if norm_g is not None:
        in_specs.append(vec)
    args += extra
    out_shape = [jax.ShapeDtypeStruct((M, N), out_dtype)]
    out_specs = [o_spec]
    if also_bf16 or norm_g is not None:
        out_shape.append(jax.ShapeDtypeStruct((M, N), bf16))
        out_specs.append(o_spec)
    if norm_bwd is not None:
        out_shape.append(jax.ShapeDtypeStruct((1, N), f32))
        out_specs.append(vec)
    dims = ("arbitrary" if norm_bwd is not None else "parallel", "parallel", "arbitrary")
    out = pl.pallas_call(
        body, name=name, grid=(M // tm, N // tn, nk), in_specs=in_specs, out_specs=out_specs, out_shape=out_shape,
        scratch_shapes=[pltpu.VMEM((tm, tn), f32)], compiler_params=_cp(*dims),
    )(*args)
    return out if len(out) > 1 else out[0]


def _rms_fwd(x, g, *, name, col=0, width=None, tm=512):
    T = x.shape[0]
    width = x.shape[1] if width is None else width
    tm = _tile(T, tm)

    def body(x_ref, g_ref, o_ref):
        xv = x_ref[...]
        r = lax.rsqrt(jnp.mean(xv * xv, axis=-1, keepdims=True) + EPS)
        o_ref[...] = (xv * r * g_ref[...]).astype(bf16)

    return pl.pallas_call(
        body, name=name, grid=(T // tm,),
        in_specs=[pl.BlockSpec((tm, width), lambda i: (i, col)), pl.BlockSpec((1, width), lambda i: (0, 0))],
        out_specs=pl.BlockSpec((tm, width), lambda i: (i, 0)), out_shape=jax.ShapeDtypeStruct((T, width), bf16),
        compiler_params=_cp("parallel"),
    )(x, g)


def _rms_bwd_math(xv, g, dy):
    r = lax.rsqrt(jnp.mean(xv * xv, axis=-1, keepdims=True) + EPS)
    xh = xv * r
    dxh = dy * g
    dx = r * (dxh - xh * jnp.mean(dxh * xh, axis=-1, keepdims=True))
    dg = jnp.sum(dy * xh, axis=0, keepdims=True)
    return dx, dg


def _loss_head(h, g, target, *, tm=512):
    T, D = h.shape
    tm = _tile(T, tm)

    def body(h_ref, g_ref, t_ref, loss_ref, dh_ref, dg_ref):
        hv, gv = h_ref[...], g_ref[...]
        r = lax.rsqrt(jnp.mean(hv * hv, axis=-1, keepdims=True) + EPS)
        e = hv * r * gv - t_ref[...]
        part = 0.5 * jnp.sum(jnp.mean(e * e, axis=-1, keepdims=True), axis=0, keepdims=True)
        dx, dg = _rms_bwd_math(hv, gv, e * (1.0 / D))
        dh_ref[...] = dx

        @pl.when(pl.program_id(0) == 0)
        def _():
            loss_ref[...] = part
            dg_ref[...] = dg

        @pl.when(pl.program_id(0) > 0)
        def _():
            loss_ref[...] += part
            dg_ref[...] += dg

    row = pl.BlockSpec((tm, D), lambda i: (i, 0))
    vec = pl.BlockSpec((1, D), lambda i: (0, 0))
    return pl.pallas_call(
        body, name="loss_head", grid=(T // tm,), in_specs=[row, vec, row],
        out_specs=[pl.BlockSpec((1, 1), lambda i: (0, 0)), row, vec],
        out_shape=[jax.ShapeDtypeStruct((1, 1), f32), jax.ShapeDtypeStruct((T, D), f32), jax.ShapeDtypeStruct((1, D), f32)],
        compiler_params=_cp("arbitrary"),
    )(h, g, target)


FFN_W = 2 * LANES
FFN_ROWS = 128
HALO = 2 * SUBLANES


def _conv_taps(a_ref, c, rc):
    if isinstance(c, int) and c == 0:
        ext = jnp.concatenate([jnp.zeros((HALO, FFN_W), f32), a_ref[pl.ds(0, rc), :].astype(f32)], axis=0)
    else:
        ext = a_ref[pl.ds(pl.multiple_of(c * rc - HALO, HALO), rc + HALO), :].astype(f32)
    return ext[HALO:], pltpu.roll(ext, 1, 0)[HALO:], pltpu.roll(ext, 2, 0)[HALO:]


def _chunk_rows(c, rc):
    return pl.ds(c * rc, rc) if isinstance(c, int) else pl.ds(pl.multiple_of(c * rc, rc), rc)


def _ffn_mid_fwd(au, cw, cb, *, name):
    T = au.shape[0]
    F = au.shape[1] // 2
    nb = F // FFN_W
    rc = min(FFN_ROWS, T)
    nc = T // rc

    def body(a_ref, u_ref, w_ref, b_ref, z_ref):
        w, b = w_ref[...], b_ref[...]

        def chunk(c):
            a, a1, a2 = _conv_taps(a_ref, c, rc)
            rows = _chunk_rows(c, rc)
            ac = w[0:1] * a2 + w[1:2] * a1 + w[2:3] * a + b
            z_ref[rows, :] = (ac * jax.nn.sigmoid(ac) * u_ref[rows, :].astype(f32)).astype(bf16)

        chunk(0)
        lax.fori_loop(1, nc, lambda c, _: chunk(c), None)

    return pl.pallas_call(
        body, name=name, grid=(nb,),
        in_specs=[pl.BlockSpec((T, FFN_W), lambda j: (0, j)), pl.BlockSpec((T, FFN_W), lambda j: (0, nb + j)),
                  pl.BlockSpec((3, FFN_W), lambda j: (0, j)), pl.BlockSpec((1, FFN_W), lambda j: (0, j))],
        out_specs=pl.BlockSpec((T, FFN_W), lambda j: (0, j)), out_shape=jax.ShapeDtypeStruct((T, F), bf16),
        compiler_params=_cp("parallel"),
    )(au, au, cw, cb)


def _ffn_mid_bwd(au, cw, cb, dz, *, name):
    T = au.shape[0]
    F = au.shape[1] // 2
    nb = F // FFN_W
    rc = min(FFN_ROWS, T)
    nc = T // rc

    def body(a_ref, u_ref, w_ref, b_ref, dz_ref, dau_ref, dw_ref, db_ref):
        w, b = w_ref[...], b_ref[...]

        def chunk(c, carry):
            nxt, s0, s1, s2, sb = carry
            a, a1, a2 = _conv_taps(a_ref, c, rc)
            rows = _chunk_rows(c, rc)
            ac = w[0:1] * a2 + w[1:2] * a1 + w[2:3] * a + b
            sg = jax.nn.sigmoid(ac)
            dz = dz_ref[rows, :].astype(f32)
            dau_ref[1, rows, :] = (dz * ac * sg).astype(bf16)
            dac = dz * u_ref[rows, :].astype(f32) * sg * (1.0 + ac * (1.0 - sg))
            ext = jnp.concatenate([dac, nxt], axis=0)
            d1, d2 = pltpu.roll(ext, rc + HALO - 1, 0)[:rc], pltpu.roll(ext, rc + HALO - 2, 0)[:rc]
            dau_ref[0, rows, :] = (w[2:3] * dac + w[1:2] * d1 + w[0:1] * d2).astype(bf16)
            tot = lambda v: jnp.sum(v, axis=0, keepdims=True)
            return dac[:HALO], s0 + tot(dac * a2), s1 + tot(dac * a1), s2 + tot(dac * a), sb + tot(dac)

        z = jnp.zeros((1, FFN_W), f32)
        carry = (jnp.zeros((HALO, FFN_W), f32), z, z, z, z)
        carry = lax.fori_loop(0, nc - 1, lambda k, cr: chunk(nc - 1 - k, cr), carry)
        _, s0, s1, s2, sb = chunk(0, carry)
        rows = lax.broadcasted_iota(jnp.int32, (3, FFN_W), 0)
        dw_ref[...] = jnp.where(rows == 0, s0, jnp.where(rows == 1, s1, s2))
        db_ref[...] = sb

    col = lambda off: pl.BlockSpec((T, FFN_W), lambda j: (0, off + j))
    return pl.pallas_call(
        body, name=name, grid=(nb,),
        in_specs=[col(0), col(nb), pl.BlockSpec((3, FFN_W), lambda j: (0, j)), pl.BlockSpec((1, FFN_W), lambda j: (0, j)), col(0)],
        out_specs=[pl.BlockSpec((2, T, FFN_W), lambda j: (0, 0, j)), pl.BlockSpec((3, FFN_W), lambda j: (0, j)),
                   pl.BlockSpec((1, FFN_W), lambda j: (0, j))],
        out_shape=[jax.ShapeDtypeStruct((2, T, F), bf16), jax.ShapeDtypeStruct((3, F), f32), jax.ShapeDtypeStruct((1, F), f32)],
        compiler_params=_cp("parallel"),
    )(au, au, cw, cb, dz)


BNN = (((2,), (1,)), ((0,), (0,)))
BNT = (((2,), (2,)), ((0,), (0,)))
BTN = (((1,), (1,)), ((0,), (0,)))


def _heads(x):
    return jnp.stack([x[:, h * LANES:(h + 1) * LANES] for h in range(HGRN_HEADS)])


def _put_heads(ref, rows, x, dtype):
    for h in range(HGRN_HEADS):
        ref[rows, h * LANES:(h + 1) * LANES] = x[h].astype(dtype)


def _hgrn_lb(l):
    m = jnp.max(l, axis=0, keepdims=True)
    e = jnp.exp(l - m)
    return e[0:1] / jnp.sum(e, axis=0, keepdims=True)


def _hgrn_chunk(q, fx, lb):
    H, C = q.shape[0], q.shape[1]
    sg = jax.nn.sigmoid(fx)
    F = lb + (1.0 - lb) * sg
    k = 1.0 - F
    logF = jnp.log(F)
    r = lax.broadcasted_iota(jnp.int32, (H, C, C), 1)
    c = lax.broadcasted_iota(jnp.int32, (H, C, C), 2)
    tril = (r >= c)
    b = _dot(tril.astype(f32), logF, BNN, precision=HI)
    bl = jnp.sum(logF, axis=1, keepdims=True)
    eb = jnp.exp(b)
    enb = jnp.exp(-b)
    elb = jnp.exp(bl - b)
    return dict(sg=sg, F=F, k=k, b=b, bl=bl, eb=eb, enb=enb, elb=elb, qd=q * eb, kd=k * enb, kl=k * elb, tril=tril)


def _hgrn_fwd(proj, lbl, ng, *, rb=512):
    T = proj.shape[0]
    rb = min(rb, T)
    cpb = rb // HGRN_CHUNK
    nblk = T // rb
    H = HGRN_HEADS

    def body(q_ref, f_ref, i_ref, g_ref, lbl_ref, ng_ref, y_ref, st_ref, S):
        @pl.when(pl.program_id(0) == 0)
        def _():
            S[...] = jnp.zeros_like(S)

        lb = _heads(_hgrn_lb(lbl_ref[...]))
        ngv = _heads(ng_ref[...])
        for c in range(cpb):
            sl = pl.ds(c * HGRN_CHUNK, HGRN_CHUNK)
            v, gx = _heads(i_ref[sl, :]), _heads(g_ref[sl, :])
            ch = _hgrn_chunk(_heads(q_ref[sl, :]), _heads(f_ref[sl, :]), lb)
            att = jnp.where(ch["tril"], _bdot(ch["qd"], ch["kd"], BNT), 0.0)
            St = S[...]
            st_ref[:, c] = St
            o = _bdot(att, v, BNN) + _bdot(ch["qd"], St, BNT)
            S[...] = St * jnp.exp(ch["bl"]) + _bdot(v, ch["kl"], BTN)
            r = lax.rsqrt(jnp.mean(o * o, axis=-1, keepdims=True) + EPS)
            _put_heads(y_ref, sl, o * r * ngv * (gx * jax.nn.sigmoid(gx)), bf16)

    col = lambda off: pl.BlockSpec((rb, H * LANES), lambda n: (n, off))
    return pl.pallas_call(
        body, name="hgrn_fwd", grid=(nblk,),
        in_specs=[col(0), col(1), col(2), col(3), pl.BlockSpec((2, H * LANES), lambda n: (0, 0)),
                  pl.BlockSpec((1, H * LANES), lambda n: (0, 0))],
        out_specs=[pl.BlockSpec((rb, H * LANES), lambda n: (n, 0)),
                   pl.BlockSpec((H, cpb, LANES, LANES), lambda n: (0, n, 0, 0))],
        out_shape=[jax.ShapeDtypeStruct((T, H * LANES), bf16),
                   jax.ShapeDtypeStruct((H, T // HGRN_CHUNK, LANES, LANES), f32)],
        scratch_shapes=[pltpu.VMEM((H, LANES, LANES), f32)], compiler_params=_cp("arbitrary"),
    )(proj, proj, proj, proj, lbl, ng)


def _hgrn_bwd(proj, lbl, ng, states, dy, *, rb=512):
    T = proj.shape[0]
    rb = min(rb, T)
    cpb = rb // HGRN_CHUNK
    nblk = T // rb
    H = HGRN_HEADS
    C = HGRN_CHUNK

    def body(q_ref, f_ref, i_ref, g_ref, lbl_ref, ng_ref, st_ref, dy_ref,
             dq_ref, df_ref, di_ref, dg_ref, dl_ref, dng_ref, dS, dlb_acc, dng_acc):
        n = pl.program_id(0)

        @pl.when(n == 0)
        def _():
            dS[...] = jnp.zeros_like(dS)
            dlb_acc[...] = jnp.zeros_like(dlb_acc)
            dng_acc[...] = jnp.zeros_like(dng_acc)

        lb_row = _hgrn_lb(lbl_ref[...])
        lb = _heads(lb_row)
        ngv = _heads(ng_ref[...])
        r_i = lax.broadcasted_iota(jnp.int32, (H, C, C), 1)
        c_i = lax.broadcasted_iota(jnp.int32, (H, C, C), 2)
        triu = (c_i >= r_i).astype(f32)
        rows_sum = lambda x: jnp.sum(x, axis=1, keepdims=True)
        for c in reversed(range(cpb)):
            sl = pl.ds(c * C, C)
            q, v, gx = _heads(q_ref[sl, :]), _heads(i_ref[sl, :]), _heads(g_ref[sl, :])
            ch = _hgrn_chunk(q, _heads(f_ref[sl, :]), lb)
            qd, kd, kl = ch["qd"], ch["kd"], ch["kl"]
            att = jnp.where(ch["tril"], _bdot(qd, kd, BNT), 0.0)
            St = st_ref[:, c]
            o = _bdot(att, v, BNN) + _bdot(qd, St, BNT)
            r = lax.rsqrt(jnp.mean(o * o, axis=-1, keepdims=True) + EPS)
            on = o * r
            sgg = jax.nn.sigmoid(gx)
            gate = gx * sgg
            dyv = _heads(dy_ref[sl, :].astype(f32))
            _put_heads(dg_ref, sl, dyv * on * ngv * sgg * (1.0 + gx * (1.0 - sgg)), bf16)
            dng_acc[...] += rows_sum(dyv * on * gate)
            don = dyv * ngv * gate
            do = r * (don - on * jnp.mean(don * on, axis=-1, keepdims=True))
            dSt = dS[...]
            dA = jnp.where(ch["tril"], _bdot(do, v, BNT), 0.0)
            dv = _bdot(att, do, BTN) + _bdot(kl, dSt, BNT)
            dqd = _bdot(dA, kd, BNN) + _bdot(do, St, BNN)
            dkd = _bdot(dA, qd, BTN)
            dkl = _bdot(v, dSt, BNN)
            dec = jnp.exp(ch["bl"])
            ddec = rows_sum(St * dSt)
            dS[...] = _bdot(do, qd, BTN) + dSt * dec
            dB = dqd * qd - dkd * kd - dkl * kl
            dbl = rows_sum(dkl * kl) + ddec * dec
            dk = dkd * ch["enb"] + dkl * ch["elb"]
            dlogF = _dot(triu, dB, BNN, precision=HI) + dbl
            dF = dlogF / ch["F"] - dk
            sg = ch["sg"]
            _put_heads(dq_ref, sl, dqd * ch["eb"], bf16)
            _put_heads(di_ref, sl, dv, bf16)
            _put_heads(df_ref, sl, dF * (1.0 - lb) * sg * (1.0 - sg), bf16)
            dlb_acc[...] += rows_sum(dF * (1.0 - sg))

        @pl.when(n == nblk - 1)
        def _():
            rows = lax.broadcasted_iota(jnp.int32, (2, LANES), 0)
            for h in range(H):
                hs = pl.ds(h * LANES, LANES)
                lbh = lb_row[:, h * LANES:(h + 1) * LANES]
                dl0 = dlb_acc[h] * lbh * (1.0 - lbh)
                dl_ref[:, hs] = jnp.where(rows == 0, dl0, -dl0)
                dng_ref[:, hs] = dng_acc[h]

    col = lambda off: pl.BlockSpec((rb, H * LANES), lambda n: (nblk - 1 - n, off))
    vec = lambda rows: pl.BlockSpec((rows, H * LANES), lambda n: (0, 0))
    tok = jax.ShapeDtypeStruct((T, H * LANES), bf16)
    return pl.pallas_call(
        body, name="hgrn_bwd", grid=(nblk,),
        in_specs=[col(0), col(1), col(2), col(3), vec(2), vec(1),
                  pl.BlockSpec((H, cpb, LANES, LANES), lambda n: (0, nblk - 1 - n, 0, 0)), col(0)],
        out_specs=[col(0), col(0), col(0), col(0), vec(2), vec(1)],
        out_shape=[tok, tok, tok, tok, jax.ShapeDtypeStruct((2, H * LANES), f32), jax.ShapeDtypeStruct((1, H * LANES), f32)],
        scratch_shapes=[pltpu.VMEM((H, LANES, LANES), f32), pltpu.VMEM((H, 1, LANES), f32), pltpu.VMEM((H, 1, LANES), f32)],
        compiler_params=_cp("arbitrary"),
    )(proj, proj, proj, proj, lbl, ng, states, dy)


def _s5_disc_math(ar, ai, ldt, br, bi):
    dt = jnp.exp(ldt)
    mag = jnp.exp(ar * dt)
    abr, abi = mag * jnp.cos(ai * dt), mag * jnp.sin(ai * dt)
    den = ar * ar + ai * ai
    xr, xi = abr - 1.0, abi
    cr = (xr * ar + xi * ai) / den
    ci = (xi * ar - xr * ai) / den
    return abr, abi, cr * br - ci * bi, cr * bi + ci * br


def _s5_disc_fwd(ar, ai, ldt, br, bi):
    def body(ar_ref, ai_ref, ldt_ref, br_ref, bi_ref, o0, o1, o2, o3):
        outs = _s5_disc_math(ar_ref[...], ai_ref[...], ldt_ref[...], br_ref[...], bi_ref[...])
        for o, v in zip((o0, o1, o2, o3), outs):
            o[...] = v

    return pl.pallas_call(
        body, name="s5_disc_fwd",
        out_shape=[jax.ShapeDtypeStruct(ar.shape, f32)] * 2 + [jax.ShapeDtypeStruct(br.shape, f32)] * 2,
    )(ar, ai, ldt, br, bi)


def _s5_disc_bwd(ar, ai, ldt, br, bi, cts):
    def body(ar_ref, ai_ref, ldt_ref, br_ref, bi_ref, c0, c1, c2, c3, o0, o1, o2, o3, o4):
        _, vjp = jax.vjp(_s5_disc_math, ar_ref[...], ai_ref[...], ldt_ref[...], br_ref[...], bi_ref[...])
        for o, v in zip((o0, o1, o2, o3, o4), vjp((c0[...], c1[...], c2[...], c3[...]))):
            o[...] = v

    return pl.pallas_call(
        body, name="s5_disc_bwd",
        out_shape=[jax.ShapeDtypeStruct(ar.shape, f32)] * 3 + [jax.ShapeDtypeStruct(br.shape, f32)] * 2,
    )(ar, ai, ldt, br, bi, *cts)


S5_LC = 512
S5_NLC = S5_N // S5_LC
S5_UB = 4
S5_UNROLL = 4


def _cmul(ar, ai, xr, xi):
    return ar * xr - ai * xi, ar * xi + ai * xr


def _cpow(ar, ai, n):
    rr, ri = None, None
    br, bi = ar, ai
    while n:
        if n & 1:
            rr, ri = (br, bi) if rr is None else _cmul(rr, ri, br, bi)
        n >>= 1
        if n:
            br, bi = _cmul(br, bi, br, bi)
    return rr, ri


def _s5_bu(u_ref, bre_ref, bim_ref, xr, xi):
    for k in range(S5_UB):
        uk = u_ref[:, k * LANES:(k + 1) * LANES].astype(bf16)
        xr[:, k * S5_LC:(k + 1) * S5_LC] = _dot(uk, bre_ref[k])
        xi[:, k * S5_LC:(k + 1) * S5_LC] = _dot(uk, bim_ref[k])


def _s5_scan(xr, xi, sr, si, ar_ref, ai_ref, nsteps, store):
    for c in range(S5_NLC):
        cs = slice(c * S5_LC, (c + 1) * S5_LC)
        a_r = jnp.broadcast_to(ar_ref[:, cs], (S5_SEG, S5_LC))
        a_i = jnp.broadcast_to(ai_ref[:, cs], (S5_SEG, S5_LC))

        def step(j, carry, cs=cs, a_r=a_r, a_i=a_i):
            pr, pi = carry
            rows = pl.ds(pl.multiple_of(j * S5_SEG, S5_SEG), S5_SEG)
            nr = a_r * pr - a_i * pi + xr[rows, cs]
            ni = a_r * pi + a_i * pr + xi[rows, cs]
            if store:
                xr[rows, cs] = nr
                xi[rows, cs] = ni
            return nr, ni

        fr, fi = lax.fori_loop(0, nsteps, step, (sr[:, cs], si[:, cs]), unroll=S5_UNROLL)
        sr[:, cs] = fr
        si[:, cs] = fi


def _s5_rscan(dr, di, xr, xi, s0r, s0i, gr, gi, acc_r, acc_i, ar_ref, ai_ref, nsteps):
    for c in range(S5_NLC):
        cs = slice(c * S5_LC, (c + 1) * S5_LC)
        a_r = jnp.broadcast_to(ar_ref[:, cs], (S5_SEG, S5_LC))
        a_i = jnp.broadcast_to(ai_ref[:, cs], (S5_SEG, S5_LC))

        def step(jj, carry, cs=cs, a_r=a_r, a_i=a_i):
            pr, pi, cr, ci = carry
            j = nsteps - 1 - jj
            rows = pl.ds(pl.multiple_of(j * S5_SEG, S5_SEG), S5_SEG)
            nr = dr[rows, cs] + a_r * pr + a_i * pi
            ni = di[rows, cs] + a_r * pi - a_i * pr
            dr[rows, cs] = nr
            di[rows, cs] = ni
            if acc_r is not None:
                prev = pl.ds(pl.multiple_of(jnp.maximum(j - 1, 0) * S5_SEG, S5_SEG), S5_SEG)
                first = j == 0
                pr_s = jnp.where(first, s0r[:, cs], xr[prev, cs])
                pi_s = jnp.where(first, s0i[:, cs], xi[prev, cs])
                cr = cr + nr * pr_s + ni * pi_s
                ci = ci - nr * pi_s + ni * pr_s
            return nr, ni, cr, ci

        z = jnp.zeros((S5_SEG, S5_LC), f32)
        init = (gr[:, cs], gi[:, cs], z, z)
        fr, fi, cr, ci = lax.fori_loop(0, nsteps, step, init, unroll=S5_UNROLL)
        gr[:, cs] = fr
        gi[:, cs] = fi
        if acc_r is not None:
            acc_r[:, cs] += cr
            acc_i[:, cs] += ci


def _s5_seg_carry(fr, fi, ar, ai, seg_len, reverse):
    pr, pi = _cpow(ar, ai if not reverse else -ai, seg_len)
    rows = lax.broadcasted_iota(jnp.int32, fr.shape, 0)
    cr, ci = jnp.zeros_like(fr), jnp.zeros_like(fi)
    sh = (S5_SEG - 1) if reverse else 1
    fr_s, fi_s = pltpu.roll(fr, sh, 0), pltpu.roll(fi, sh, 0)
    order = range(S5_SEG - 2, -1, -1) if reverse else range(1, S5_SEG)
    for r in order:
        c_r, c_i = pltpu.roll(cr, sh, 0), pltpu.roll(ci, sh, 0)
        m_r, m_i = _cmul(pr, pi, c_r, c_i)
        cr = jnp.where(rows == r, m_r + fr_s, cr)
        ci = jnp.where(rows == r, m_i + fi_s, ci)
    return cr, ci


def _gelu_parts(y):
    c0 = math.sqrt(2.0 / math.pi)
    t = jnp.tanh(c0 * (y + 0.044715 * y * y * y))
    z = 0.5 * y * (1.0 + t)
    dz = 0.5 * (1.0 + t) + 0.5 * y * (1.0 - t * t) * c0 * (1.0 + 3.0 * 0.044715 * y * y)
    return z, dz


def _s5_y(xr, xi, u_ref, cre_ref, cim_ref, d_ref):
    ys = []
    for k in range(S5_UB):
        cs = slice(k * S5_LC, (k + 1) * S5_LC)
        ys.append(_bdot(xr[:, cs], cre_ref[k]) - _bdot(xi[:, cs], cim_ref[k]))
    return jnp.concatenate(ys, axis=1) + d_ref[...] * u_ref[...]


def _s5_specs(T, rb, rev=False):
    nblk = T // rb
    blk = (lambda i: (nblk - 1 - i, 0)) if rev else (lambda i: (i, 0))
    tok = pl.BlockSpec((rb, 4 * LANES), blk)
    bmat = pl.BlockSpec((S5_UB, LANES, S5_LC), lambda i: (0, 0, 0))
    cmat = pl.BlockSpec((S5_UB, S5_LC, LANES), lambda i: (0, 0, 0))
    avec = pl.BlockSpec((1, S5_N), lambda i: (0, 0))
    seg = pl.BlockSpec((S5_SEG, S5_N), lambda i: (0, 0))
    cvec = pl.BlockSpec((1, 4 * LANES), lambda i: (0, 0))
    s0 = pl.BlockSpec((1, S5_SEG, S5_N), (lambda i: (nblk - 1 - i, 0, 0)) if rev else (lambda i: (i, 0, 0)))
    return dict(tok=tok, bmat=bmat, cmat=cmat, avec=avec, seg=seg, cvec=cvec, s0=s0, nblk=nblk)


def _s5_final(u, bre, bim, ar, ai, *, rb):
    T = u.shape[0]
    sp = _s5_specs(T, rb)

    def body(u_ref, bre_ref, bim_ref, ar_ref, ai_ref, fr_ref, fi_ref, xr, xi):
        @pl.when(pl.program_id(0) == 0)
        def _():
            fr_ref[...] = jnp.zeros_like(fr_ref)
            fi_ref[...] = jnp.zeros_like(fi_ref)

        _s5_bu(u_ref, bre_ref, bim_ref, xr, xi)
        _s5_scan(xr, xi, fr_ref, fi_ref, ar_ref, ai_ref, rb // S5_SEG, False)

    return pl.pallas_call(
        body, name="s5_final", grid=(sp["nblk"],),
        in_specs=[sp["tok"], sp["bmat"], sp["bmat"], sp["avec"], sp["avec"]], out_specs=[sp["seg"], sp["seg"]],
        out_shape=[jax.ShapeDtypeStruct((S5_SEG, S5_N), f32)] * 2,
        scratch_shapes=[pltpu.VMEM((rb, S5_N), f32)] * 2, compiler_params=_cp("arbitrary"),
    )(u, bre, bim, ar, ai)


def _s5_fwd(u, bre, bim, ar, ai, fr, fi, cre, cim, dsk, wg, bg, *, rb):
    T = u.shape[0]
    sp = _s5_specs(T, rb)
    seg_len = T // S5_SEG

    def body(u_ref, bre_ref, bim_ref, ar_ref, ai_ref, fr_ref, fi_ref, cre_ref, cim_ref, d_ref, wg_ref, bg_ref,
             o_ref, s0r_ref, s0i_ref, xr, xi, sr, si):
        @pl.when(pl.program_id(0) == 0)
        def _():
            i_r, i_i = _s5_seg_carry(fr_ref[...], fi_ref[...], ar_ref[...], ai_ref[...], seg_len, False)
            sr[...] = i_r
            si[...] = i_i

        s0r_ref[0] = sr[...]
        s0i_ref[0] = si[...]
        _s5_bu(u_ref, bre_ref, bim_ref, xr, xi)
        _s5_scan(xr, xi, sr, si, ar_ref, ai_ref, rb // S5_SEG, True)
        y = _s5_y(xr, xi, u_ref, cre_ref, cim_ref, d_ref)
        z, _ = _gelu_parts(y)
        v = _bdot(z, wg_ref[...]) + bg_ref[...]
        o_ref[...] = (z * jax.nn.sigmoid(v)).astype(bf16)

    wspec = pl.BlockSpec((4 * LANES, 4 * LANES), lambda i: (0, 0))
    return pl.pallas_call(
        body, name="s5_fwd", grid=(sp["nblk"],),
        in_specs=[sp["tok"], sp["bmat"], sp["bmat"], sp["avec"], sp["avec"], sp["seg"], sp["seg"], sp["cmat"], sp["cmat"],
                  sp["cvec"], wspec, sp["cvec"]],
        out_specs=[sp["tok"], sp["s0"], sp["s0"]],
        out_shape=[jax.ShapeDtypeStruct((T, 4 * LANES), bf16)] + [jax.ShapeDtypeStruct((sp["nblk"], S5_SEG, S5_N), f32)] * 2,
        scratch_shapes=[pltpu.VMEM((rb, S5_N), f32)] * 2 + [pltpu.VMEM((S5_SEG, S5_N), f32)] * 2,
        compiler_params=_cp("arbitrary"),
    )(u, bre, bim, ar, ai, fr, fi, cre, cim, dsk, wg, bg)


def _s5_bwd_a(u, bre, bim, ar, ai, s0r, s0i, cre, cim, cret, cimt, dsk, wg, bg, dout, *, rb):
    T = u.shape[0]
    sp = _s5_specs(T, rb, rev=True)

    def body(u_ref, bre_ref, bim_ref, ar_ref, ai_ref, s0r_ref, s0i_ref, cre_ref, cim_ref, cret_ref, cimt_ref,
             d_ref, wg_ref, bg_ref, do_ref, dy_ref, glr_ref, gli_ref, dcre_ref, dcim_ref, dd_ref, dwg_ref, dbg_ref,
             xr, xi, dr, di, sr, si):
        @pl.when(pl.program_id(0) == 0)
        def _():
            for r in (glr_ref, gli_ref, dcre_ref, dcim_ref, dd_ref, dwg_ref, dbg_ref):
                r[...] = jnp.zeros_like(r)

        sr[...] = s0r_ref[0]
        si[...] = s0i_ref[0]
        _s5_bu(u_ref, bre_ref, bim_ref, xr, xi)
        _s5_scan(xr, xi, sr, si, ar_ref, ai_ref, rb // S5_SEG, True)
        uv = u_ref[...]
        y = _s5_y(xr, xi, u_ref, cre_ref, cim_ref, d_ref)
        z, gz = _gelu_parts(y)
        v = _bdot(z, wg_ref[...]) + bg_ref[...]
        sg = jax.nn.sigmoid(v)
        dov = do_ref[...].astype(f32)
        dv = dov * z * sg * (1.0 - sg)
        dz = dov * sg + _bdot(dv, wg_ref[...], NT)
        dy = dz * gz
        dy_ref[...] = dy
        dwg_ref[...] += _bdot(z, dv, TN)
        dbg_ref[...] += jnp.sum(dv, axis=0, keepdims=True)
        dd_ref[...] += jnp.sum(dy * uv, axis=0, keepdims=True)
        for k in range(S5_UB):
            cs = slice(k * S5_LC, (k + 1) * S5_LC)
            dyk = dy[:, k * LANES:(k + 1) * LANES]
            dcre_ref[k] += _bdot(xr[:, cs], dyk, TN)
            dcim_ref[k] -= _bdot(xi[:, cs], dyk, TN)
            dr[:, cs] = _bdot(dyk, cret_ref[k])
            di[:, cs] = -_bdot(dyk, cimt_ref[k])
        _s5_rscan(dr, di, None, None, None, None, glr_ref, gli_ref, None, None, ar_ref, ai_ref, rb // S5_SEG)

    wspec = pl.BlockSpec((4 * LANES, 4 * LANES), lambda i: (0, 0))
    return pl.pallas_call(
        body, name="s5_bwd_a", grid=(sp["nblk"],),
        in_specs=[sp["tok"], sp["bmat"], sp["bmat"], sp["avec"], sp["avec"], sp["s0"], sp["s0"], sp["cmat"], sp["cmat"],
                  sp["bmat"], sp["bmat"], sp["cvec"], wspec, sp["cvec"], sp["tok"]],
        out_specs=[sp["tok"], sp["seg"], sp["seg"], sp["cmat"], sp["cmat"], sp["cvec"], wspec, sp["cvec"]],
        out_shape=[jax.ShapeDtypeStruct((T, 4 * LANES), f32)] + [jax.ShapeDtypeStruct((S5_SEG, S5_N), f32)] * 2
        + [jax.ShapeDtypeStruct((S5_UB, S5_LC, LANES), f32)] * 2
        + [jax.ShapeDtypeStruct((1, 4 * LANES), f32), jax.ShapeDtypeStruct((4 * LANES, 4 * LANES), f32),
           jax.ShapeDtypeStruct((1, 4 * LANES), f32)],
        scratch_shapes=[pltpu.VMEM((rb, S5_N), f32)] * 4 + [pltpu.VMEM((S5_SEG, S5_N), f32)] * 2,
        compiler_params=_cp("arbitrary"),
    )(u, bre, bim, ar, ai, s0r, s0i, cre, cim, cret, cimt, dsk, wg, bg, dout)


def _s5_bwd_b(u, bre, bim, bret, bimt, ar, ai, s0r, s0i, glr, gli, cret, cimt, dsk, dy, *, rb):
    T = u.shape[0]
    sp = _s5_specs(T, rb, rev=True)
    seg_len = T // S5_SEG
    nblk = sp["nblk"]

    def body(u_ref, bre_ref, bim_ref, bret_ref, bimt_ref, ar_ref, ai_ref, s0r_ref, s0i_ref, glr_ref, gli_ref,
             cret_ref, cimt_ref, d_ref, dy_ref, du_ref, dbre_ref, dbim_ref, dar_ref, dai_ref,
             xr, xi, dr, di, sr, si, gr, gi, acc_r, acc_i):
        @pl.when(pl.program_id(0) == 0)
        def _():
            x_r, x_i = _s5_seg_carry(glr_ref[...], gli_ref[...], ar_ref[...], ai_ref[...], seg_len, True)
            gr[...] = x_r
            gi[...] = x_i
            acc_r[...] = jnp.zeros_like(acc_r)
            acc_i[...] = jnp.zeros_like(acc_i)
            dbre_ref[...] = jnp.zeros_like(dbre_ref)
            dbim_ref[...] = jnp.zeros_like(dbim_ref)

        sr[...] = s0r_ref[0]
        si[...] = s0i_ref[0]
        _s5_bu(u_ref, bre_ref, bim_ref, xr, xi)
        _s5_scan(xr, xi, sr, si, ar_ref, ai_ref, rb // S5_SEG, True)
        dy = dy_ref[...]
        for k in range(S5_UB):
            cs = slice(k * S5_LC, (k + 1) * S5_LC)
            dyk = dy[:, k * LANES:(k + 1) * LANES]
            dr[:, cs] = _bdot(dyk, cret_ref[k])
            di[:, cs] = -_bdot(dyk, cimt_ref[k])
        sr[...] = s0r_ref[0]
        si[...] = s0i_ref[0]
        _s5_rscan(dr, di, xr, xi, sr, si, gr, gi, acc_r, acc_i, ar_ref, ai_ref, rb // S5_SEG)
        dus = []
        for k in range(S5_UB):
            cs = slice(k * S5_LC, (k + 1) * S5_LC)
            uk = u_ref[:, k * LANES:(k + 1) * LANES]
            dbre_ref[k] += _bdot(uk, dr[:, cs], TN)
            dbim_ref[k] += _bdot(uk, di[:, cs], TN)
            dus.append(_bdot(dr[:, cs], bret_ref[k]) + _bdot(di[:, cs], bimt_ref[k]))
        du_ref[...] = (jnp.concatenate(dus, axis=1) + d_ref[...] * dy).astype(bf16)

        @pl.when(pl.program_id(0) == nblk - 1)
        def _():
            dar_ref[...] = jnp.sum(acc_r[...], axis=0, keepdims=True)
            dai_ref[...] = jnp.sum(acc_i[...], axis=0, keepdims=True)

    return pl.pallas_call(
        body, name="s5_bwd_b", grid=(nblk,),
        in_specs=[sp["tok"], sp["bmat"], sp["bmat"], sp["cmat"], sp["cmat"], sp["avec"], sp["avec"], sp["s0"], sp["s0"],
                  sp["seg"], sp["seg"], sp["bmat"], sp["bmat"], sp["cvec"], sp["tok"]],
        out_specs=[sp["tok"], sp["bmat"], sp["bmat"], sp["avec"], sp["avec"]],
        out_shape=[jax.ShapeDtypeStruct((T, 4 * LANES), bf16)] + [jax.ShapeDtypeStruct((S5_UB, LANES, S5_LC), f32)] * 2
        + [jax.ShapeDtypeStruct((1, S5_N), f32)] * 2,
        scratch_shapes=[pltpu.VMEM((rb, S5_N), f32)] * 4 + [pltpu.VMEM((S5_SEG, S5_N), f32)] * 6,
        compiler_params=_cp("arbitrary"),
    )(u, bre, bim, bret, bimt, ar, ai, s0r, s0i, glr, gli, cret, cimt, dsk, dy)


def _blockdiag(w, transpose=False):
    if transpose:
        w = jnp.swapaxes(w, 1, 2)
    g, a, b = w.shape
    eye = jnp.eye(8, dtype=w.dtype)
    return jnp.einsum("kgab,gj->kgajb", w.reshape(4, 8, a, b), eye).reshape(4, 8 * a, 8 * b)


def _blockdiag_t(m, a, b):
    eye = jnp.eye(8, dtype=m.dtype)
    return jnp.einsum("kgajb,gj->kgab", m.reshape(4, 8, a, 8, b), eye).reshape(32, a, b)


ROT = MLA_ROPE // 2


def _rope_tables(positions):
    freqs = ROPE_THETA ** (-jnp.arange(0, MLA_ROPE, 2, dtype=f32) / MLA_ROPE)
    ang = positions.astype(f32)[:, None] * freqs
    cos, sin, z = jnp.cos(ang), jnp.sin(ang), jnp.zeros_like(ang)
    return (jnp.concatenate([cos, cos, z, z], axis=1), jnp.concatenate([-sin, z, z, z], axis=1),
            jnp.concatenate([z, sin, z, z], axis=1))


def _rot(x, c, sa, sb):
    return x * c + pltpu.roll(x, LANES - ROT, 1) * sa + pltpu.roll(x, ROT, 1) * sb


def _rot_t(dy, c, sa, sb):
    return dy * c + pltpu.roll(dy * sa, ROT, 1) + pltpu.roll(dy * sb, LANES - ROT, 1)


def _rms(xv, g):
    return xv * lax.rsqrt(jnp.mean(xv * xv, axis=-1, keepdims=True) + EPS) * g


QW, KVW = MLA_Q_RANK, MLA_KV_RANK
ODD_PAD = QW + KVW + LANES


def _mla_prep_fwd(proj, qg, kvg, tabs, *, tm=512):
    T = proj.shape[0]
    tm = _tile(T, tm)

    def body(p_ref, qg_ref, kvg_ref, c_ref, sa_ref, sb_ref, cq_ref, ckv_ref, kr_ref):
        cq_ref[...] = _rms(p_ref[:, :QW], qg_ref[...]).astype(bf16)
        ckv_ref[...] = _rms(p_ref[:, QW:QW + KVW], kvg_ref[...]).astype(bf16)
        kr_ref[...] = _rot(p_ref[:, QW + KVW:], c_ref[...], sa_ref[...], sb_ref[...]).astype(bf16)

    row = lambda w: pl.BlockSpec((tm, w), lambda i: (i, 0))
    vec = lambda w: pl.BlockSpec((1, w), lambda i: (0, 0))
    return pl.pallas_call(
        body, name="mla_prep_fwd", grid=(T // tm,),
        in_specs=[row(ODD_PAD), vec(QW), vec(KVW), row(LANES), row(LANES), row(LANES)],
        out_specs=[row(QW), row(KVW), row(LANES)],
        out_shape=[jax.ShapeDtypeStruct((T, QW), bf16), jax.ShapeDtypeStruct((T, KVW), bf16),
                   jax.ShapeDtypeStruct((T, LANES), bf16)],
        compiler_params=_cp("parallel"),
    )(proj, qg, kvg, *tabs)


def _mla_prep_bwd(proj, qg, kvg, tabs, dcqn, dckvn, dkr_heads, *, tm=512):
    T = proj.shape[0]
    tm = _tile(T, tm)

    def body(p_ref, qg_ref, kvg_ref, c_ref, sa_ref, sb_ref, dcq_ref, dckv_ref, dkr_ref, dp_ref, dqg_ref, dkvg_ref):
        dcq, dqg = _rms_bwd_math(p_ref[:, :QW], qg_ref[...], dcq_ref[...])
        dckv, dkvg = _rms_bwd_math(p_ref[:, QW:QW + KVW], kvg_ref[...], dckv_ref[...])
        dk = dkr_ref[:, :LANES]
        for h in range(1, MLA_HEADS):
            dk = dk + dkr_ref[:, h * LANES:(h + 1) * LANES]
        dkr = _rot_t(dk, c_ref[...], sa_ref[...], sb_ref[...])
        dp_ref[...] = jnp.concatenate([dcq, dckv, dkr], axis=1).astype(bf16)

        @pl.when(pl.program_id(0) == 0)
        def _():
            dqg_ref[...] = dqg
            dkvg_ref[...] = dkvg

        @pl.when(pl.program_id(0) > 0)
        def _():
            dqg_ref[...] += dqg
            dkvg_ref[...] += dkvg

    row = lambda w: pl.BlockSpec((tm, w), lambda i: (i, 0))
    vec = lambda w: pl.BlockSpec((1, w), lambda i: (0, 0))
    return pl.pallas_call(
        body, name="mla_prep_bwd", grid=(T // tm,),
        in_specs=[row(ODD_PAD), vec(QW), vec(KVW), row(LANES), row(LANES), row(LANES), row(QW), row(KVW),
                  row(MLA_HEADS * LANES)],
        out_specs=[row(ODD_PAD), vec(QW), vec(KVW)],
        out_shape=[jax.ShapeDtypeStruct((T, ODD_PAD), bf16), jax.ShapeDtypeStruct((1, QW), f32),
                   jax.ShapeDtypeStruct((1, KVW), f32)],
        compiler_params=_cp("arbitrary"),
    )(proj, qg, kvg, *tabs, dcqn, dckvn, dkr_heads)


HQ = 2 * LANES
QK_SCALE = MLA_QK ** -0.5


def _q_post(q, tabs, *, transpose, name, tm=512):
    T = q.shape[0]
    tm = _tile(T, tm)

    def body(q_ref, c_ref, sa_ref, sb_ref, o_ref):
        c, sa, sb = c_ref[...], sa_ref[...], sb_ref[...]
        for h in range(MLA_HEADS):
            nope, rope = pl.ds(h * HQ, LANES), pl.ds(h * HQ + LANES, LANES)
            o_ref[:, nope] = (q_ref[:, nope].astype(f32) * QK_SCALE).astype(bf16)
            o_ref[:, rope] = ((_rot_t if transpose else _rot)(q_ref[:, rope].astype(f32), c, sa, sb) * QK_SCALE).astype(bf16)

    tab = pl.BlockSpec((tm, LANES), lambda i: (i, 0))
    blk = pl.BlockSpec((tm, MLA_HEADS * HQ), lambda i: (i, 0))
    return pl.pallas_call(
        body, name=name, grid=(T // tm,), in_specs=[blk, tab, tab, tab], out_specs=blk,
        out_shape=jax.ShapeDtypeStruct(q.shape, bf16), compiler_params=_cp("parallel"),
    )(q, *tabs)


def _causal_mask(i, j, tq, tk):
    r = lax.broadcasted_iota(jnp.int32, (tq, tk), 0) + i * tq
    c = lax.broadcasted_iota(jnp.int32, (tq, tk), 1) + j * tk
    return c <= r


def _flash_fwd(q, kv, kr, *, tq=1024, tk=1024):
    T = q.shape[0]
    tq = _tile(T, tq)
    tk = _tile(tq, tk)
    per = tq // tk
    H = MLA_HEADS

    def body(q_ref, kn_ref, v_ref, kr_ref, o_ref, lse_ref, m_s, acc):
        i, j = pl.program_id(1), pl.program_id(2)
        last = (i + 1) * per - 1

        @pl.when(j == 0)
        def _():
            m_s[...] = jnp.full_like(m_s, -jnp.inf)
            acc[...] = jnp.zeros_like(acc)

        def step(masked):
            k = jnp.concatenate([kn_ref[...], kr_ref[...]], axis=1)
            s = _dot(q_ref[...], k, NT)
            if masked:
                s = jnp.where(_causal_mask(i, j, tq, tk), s, -jnp.inf)
            m_new = jnp.maximum(m_s[...], jnp.max(s, axis=-1, keepdims=True))
            alpha = jnp.exp(m_s[...] - m_new)
            p = jnp.exp((s - m_new).astype(bf16))
            v1 = jnp.concatenate([v_ref[...], jnp.ones((tk, LANES), bf16)], axis=1)
            acc[...] = alpha * acc[...] + _dot(p, v1)
            m_s[...] = m_new

        pl.when(j < i * per)(functools.partial(step, False))
        pl.when((j >= i * per) & (j <= last))(functools.partial(step, True))

        @pl.when(j == last)
        def _():
            l = acc[:, LANES:]
            o_ref[...] = (acc[:, :LANES] / l).astype(bf16)
            lse_ref[0] = m_s[...] + jnp.log(jnp.max(l, axis=-1, keepdims=True))

    kj = lambda i, j: jnp.minimum(j, (i + 1) * per - 1)
    kblk = lambda off: pl.BlockSpec((tk, LANES), lambda h, i, j: (kj(i, j), 2 * h + off))
    return pl.pallas_call(
        body, name="flash_fwd", grid=(H, T // tq, T // tk),
        in_specs=[pl.BlockSpec((tq, HQ), lambda h, i, j: (i, h)), kblk(0), kblk(1),
                  pl.BlockSpec((tk, LANES), lambda h, i, j: (kj(i, j), 0))],
        out_specs=[pl.BlockSpec((tq, LANES), lambda h, i, j: (i, h)), pl.BlockSpec((1, tq, 1), lambda h, i, j: (h, i, 0))],
        out_shape=[jax.ShapeDtypeStruct((T, H * LANES), bf16), jax.ShapeDtypeStruct((H, T, 1), f32)],
        scratch_shapes=[pltpu.VMEM((tq, 1), f32), pltpu.VMEM((tq, 2 * LANES), f32)],
        compiler_params=_cp("parallel", "parallel", "arbitrary"),
    )(q, kv, kv, kr)


def _flash_bwd(q, kv, kr, o, do, lse, *, tb=1024):
    T = q.shape[0]
    tb = _tile(T, tb)
    nb = T // tb
    H = MLA_HEADS

    def body(q_ref, kn_ref, v_ref, kr_ref, o_ref, do_ref, lse_ref, dkv_ref, dkr_ref, dq_ref, dk_acc, dv_acc):
        j, ii = pl.program_id(1), pl.program_id(2)
        i = jnp.maximum(ii, j)

        @pl.when((j == 0) & (ii == 0))
        def _():
            dq_ref[...] = jnp.zeros_like(dq_ref)

        @pl.when(ii == 0)
        def _():
            dk_acc[...] = jnp.zeros_like(dk_acc)
            dv_acc[...] = jnp.zeros_like(dv_acc)

        def step(masked):
            k = jnp.concatenate([kn_ref[...], kr_ref[...]], axis=1)
            p = jnp.exp((_dot(q_ref[...], k, NT) - lse_ref[0]).astype(bf16))
            if masked:
                p = jnp.where(_causal_mask(i, j, tb, tb), p, jnp.zeros_like(p))
            delta = jnp.sum(o_ref[...].astype(f32) * do_ref[...], axis=-1, keepdims=True)
            ds = p * (_bdot(do_ref[...], v_ref[...], NT) - delta).astype(bf16)
            dv_acc[...] += _bdot(p, do_ref[...], TN)
            dk_acc[...] += _bdot(ds, q_ref[...], TN)
            rows = pl.ds(pl.multiple_of(i * tb, tb), tb)
            dq_ref[rows, :] += _bdot(ds, k)

        pl.when(ii > j)(functools.partial(step, False))
        pl.when(ii == j)(functools.partial(step, True))

        @pl.when(ii == nb - 1)
        def _():
            dkv_ref[...] = jnp.concatenate([dk_acc[:, :LANES], dv_acc[...]], axis=1).astype(bf16)
            dkr_ref[...] = dk_acc[:, LANES:]

    qi = lambda h, j, i: jnp.maximum(i, j)
    kblk = lambda off: pl.BlockSpec((tb, LANES), lambda h, j, i: (j, 2 * h + off))
    vec = pl.BlockSpec((1, tb, 1), lambda h, j, i: (h, qi(h, j, i), 0))
    qblk = pl.BlockSpec((tb, LANES), lambda h, j, i: (qi(h, j, i), h))
    return pl.pallas_call(
        body, name="flash_bwd", grid=(H, nb, nb),
        in_specs=[pl.BlockSpec((tb, HQ), lambda h, j, i: (qi(h, j, i), h)), kblk(0), kblk(1),
                  pl.BlockSpec((tb, LANES), lambda h, j, i: (j, 0)), qblk, qblk, vec],
        out_specs=[pl.BlockSpec((tb, HQ), lambda h, j, i: (j, h)), pl.BlockSpec((tb, LANES), lambda h, j, i: (j, h)),
                   pl.BlockSpec((T, HQ), lambda h, j, i: (0, h))],
        out_shape=[jax.ShapeDtypeStruct((T, H * HQ), bf16), jax.ShapeDtypeStruct((T, H * LANES), f32),
                   jax.ShapeDtypeStruct((T, H * HQ), f32)],
        scratch_shapes=[pltpu.VMEM((tb, HQ), f32), pltpu.VMEM((tb, LANES), f32)],
        compiler_params=_cp("parallel", "arbitrary", "arbitrary"),
    )(q, kv, kv, kr, o, do, lse)


HBM_SPEC = pl.BlockSpec(memory_space=pltpu.HBM)
N_CHIPS = 4
N_DEV = 8

BIG = {"even_w_in": 1, "s5_w_glu": 0, "even_w_out": 0, "odd_w_in": 0, "mla_w_uq": 1, "mla_w_ukv": 1, "odd_w_out": 0,
       "ffn_w_in": 2, "ffn_w_out": 1}
LAYERED = ("ffn_w_in", "ffn_w_out")
GROUPS = {"even_in": ("even_w_in",), "even_rest": ("s5_w_glu", "even_w_out"), "ffn0": LAYERED,
          "odd": ("odd_w_in", "mla_w_uq", "mla_w_ukv", "odd_w_out"), "ffn1": LAYERED}
GROUP_LAYER = {"ffn0": 0, "ffn1": 1}


def _place():
    x, y, c = lax.axis_index("x"), lax.axis_index("y"), lax.axis_index("c")
    chips = [(1 - x, y), (x, 1 - y), (1 - x, 1 - y)]
    return x, y, c, chips


def _slab(ref, axis, k, size):
    start = pl.multiple_of(k * size, size if axis == 0 else LANES)
    idx = [slice(None)] * len(ref.shape)
    idx[axis] = pl.ds(start, size)
    return ref.at[tuple(idx)]


SEM_SPEC = pl.BlockSpec(memory_space=pltpu.SEMAPHORE)
ANY_SPEC = pl.BlockSpec(memory_space=pl.ANY)
EFFECT = pltpu.SideEffectType.DATAFLOW_SIDE_EFFECTING


def _hbm(a):
    return pltpu.with_memory_space_constraint(a, pltpu.HBM)


class _Gather:
    copies = 3

    def __init__(self, axis, size):
        self.axis, self.size = axis, size

    def view(self, land, kk):
        return _slab(land, self.axis, kk, self.size)

    def own(self, land, place):
        return self.view(land, 2 * place[0] + place[1])

    def sends(self, src, land, place):
        x, y, c, chips = place
        return [(self.own(land, place) if src is None else src, self.own(land, place), (*chip, c)) for chip in chips]

    def recvs(self, land, place):
        return [self.view(land, 2 * cx + cy) for cx, cy in place[3]]


class _Scatter:
    copies = 3

    def __init__(self, axis, size, layer=None):
        self.axis, self.size, self.layer = axis, size, layer

    def row(self, land, j):
        return land.at[j] if self.layer is None else land.at[j, self.layer]

    def sends(self, src, land, place):
        c, chips = place[2], place[3]
        return [(_slab(src, self.axis, 2 * cx + cy, self.size), self.row(land, j), (cx, cy, c))
                for j, (cx, cy) in enumerate(chips)]

    def recvs(self, land, place):
        return [self.row(land, j) for j in range(3)]


class _ToAll:
    copies = N_DEV - 1

    def __init__(self, size):
        self.size = size

    def sends(self, src, land, place):
        x, y, c, _ = place
        flip = lambda v, bit: 1 - v if bit else v
        own = _slab(land, 0, 4 * x + 2 * y + c, self.size)
        return [(own, own, (flip(x, m & 4), flip(y, m & 2), flip(c, m & 1))) for m in range(1, N_DEV)]

    def recvs(self, land, place):
        x, y, c, _ = place
        d = 4 * x + 2 * y + c
        return [_slab(land, 0, d ^ m, self.size) for m in range(1, N_DEV)]


def _unique(arrays):
    out, index = [], {}
    for a in arrays:
        if a is not None and id(a) not in index:
            index[id(a)] = len(out)
            out.append(a)
    return out, index


def _sem_base(routes):
    base = [0]
    for r in routes:
        base.append(base[-1] + r.copies)
    return base


def _push_start(name, items):
    n = len(items)
    base = _sem_base([it[0] for it in items])
    arrays, index = _unique([it[1] for it in items] + [it[2] for it in items])
    na = len(arrays)

    def body(*refs):
        arr, send, recv, token = refs[:na], refs[na], refs[na + 1], refs[-1]
        place = _place()
        for i, (route, src, land) in enumerate(items):
            s_ref = None if src is None else arr[index[id(src)]]
            for j, (s, d, dev) in enumerate(route.sends(s_ref, arr[index[id(land)]], place)):
                pltpu.make_async_remote_copy(src_ref=s, dst_ref=d, send_sem=send.at[base[i] + j], recv_sem=recv.at[base[i] + j],
                                             device_id=dev, device_id_type=MESH).start()
        token[...] = jnp.zeros_like(token)

    res = pl.pallas_call(
        body, name=name,
        out_shape=[pltpu.SemaphoreType.DMA((base[-1],)), pltpu.SemaphoreType.DMA((base[-1],))]
        + [pltpu.HBM(a.shape, a.dtype) for a in arrays] + [jax.ShapeDtypeStruct((SUBLANES, LANES), f32)],
        in_specs=[HBM_SPEC] * na, out_specs=[SEM_SPEC, SEM_SPEC] + [HBM_SPEC] * na + [pl.BlockSpec(memory_space=pltpu.VMEM)],
        input_output_aliases={i: 2 + i for i in range(na)},
        compiler_params=pltpu.CompilerParams(has_side_effects=EFFECT),
    )(*[_hbm(a) for a in arrays])
    thru = lambda a: None if a is None else res[2 + index[id(a)]]
    return (res[0], res[1]), [thru(it[1]) for it in items], [thru(it[2]) for it in items], res[-1]


def _push_wait(name, groups, after):
    arrays, index = _unique([a for _, _, srcs, lands in groups for a in list(srcs) + list(lands)])
    na, ng = len(arrays), len(groups)

    def body(*refs):
        arr, sems = refs[:na], refs[na:na + 2 * ng]
        place = _place()
        for g, (routes, _, srcs, lands) in enumerate(groups):
            send, recv = sems[2 * g], sems[2 * g + 1]
            base = _sem_base(routes)
            for i, route in enumerate(routes):
                src, land = None if srcs[i] is None else arr[index[id(srcs[i])]], arr[index[id(lands[i])]]
                for j, ((s, d, dev), mine) in enumerate(zip(route.sends(src, land, place), route.recvs(land, place))):
                    cp = pltpu.make_async_remote_copy(src_ref=s, dst_ref=mine, send_sem=send.at[base[i] + j],
                                                      recv_sem=recv.at[base[i] + j], device_id=dev,
                                                      device_id_type=MESH)
                    cp.wait_send()
                    cp.wait_recv()

    sem_args = [s for g in groups for s in g[1]]
    res = pl.pallas_call(
        body, name=name, out_shape=[pltpu.HBM(a.shape, a.dtype) for a in arrays],
        in_specs=[HBM_SPEC] * na + [SEM_SPEC] * (2 * ng) + [ANY_SPEC] * len(after), out_specs=[HBM_SPEC] * na,
        input_output_aliases={i: i for i in range(na)},
        compiler_params=pltpu.CompilerParams(has_side_effects=EFFECT),
    )(*arrays, *sem_args, *after)
    return [[res[index[id(a)]] for a in g[3]] for g in groups]


def _place_slab(block, axis, slabs, idx, dtype, *, name):
    R, C = block.shape
    tm = _rows(R, C)
    nr = R // tm
    out_map = (lambda i, k: (i, k[0])) if axis == 1 else (lambda i, k: (k[0] * nr + i, 0))

    def body(k_ref, x_ref, o_ref):
        o_ref[...] = x_ref[...].astype(dtype)

    full = (R, C * slabs) if axis == 1 else (R * slabs, C)
    return pl.pallas_call(
        body, name=name, out_shape=jax.ShapeDtypeStruct(full, dtype),
        grid_spec=pltpu.PrefetchScalarGridSpec(
            num_scalar_prefetch=1, grid=(nr,), in_specs=[pl.BlockSpec((tm, C), lambda i, k: (i, 0))],
            out_specs=pl.BlockSpec((tm, C), out_map)),
        compiler_params=_cp("parallel"),
    )(idx, block)


def _swap_with_sibling(parts, tag):
    names = list(parts)

    def body(*refs):
        n = len(names)
        ins, outs, send, recv = refs[:n], refs[n:2 * n], refs[-2], refs[-1]
        x, y, c, _ = _place()
        cps = [pltpu.make_async_remote_copy(src_ref=ins[a], dst_ref=outs[a], send_sem=send.at[a], recv_sem=recv.at[a],
                                            device_id=(x, y, 1 - c), device_id_type=MESH) for a in range(n)]
        for cp in cps:
            cp.start()
        for cp in cps:
            cp.wait_recv()
        for cp in cps:
            cp.wait_send()

    res = pl.pallas_call(
        body, name=f"swap_with_sibling_{tag}", in_specs=[HBM_SPEC] * len(names), out_specs=[HBM_SPEC] * len(names),
        out_shape=[jax.ShapeDtypeStruct(parts[n].shape, parts[n].dtype) for n in names],
        scratch_shapes=[pltpu.SemaphoreType.DMA((len(names),)), pltpu.SemaphoreType.DMA((len(names),))],
    )(*[parts[n] for n in names])
    return dict(zip(names, res))


ELEMENTWISE_BLOCK_BYTES = 1 << 20


def _rows(r, c):
    for t in (512, 256, 128, 64, 32, 16, 8):
        if r % t == 0 and t * c * 4 <= ELEMENTWISE_BLOCK_BYTES:
            return t
    return r


def _sum4(owns, axis, recv, kidx, *, name):
    L = len(owns)
    R, C = recv.shape[2:]
    tm = _rows(R, C)
    nr = R // tm

    def body(k_ref, *refs):
        own_refs, r_ref, out_ref = refs[:L], refs[L], refs[L + 1]
        for li in range(L):
            @pl.when(pl.program_id(0) == li)
            def _(o_ref=own_refs[li]):
                out_ref[...] = ((o_ref[...] + r_ref[0, 0].astype(f32)) + r_ref[1, 0].astype(f32)) + r_ref[2, 0].astype(f32)

    own_map = (lambda l, i, k: (i, k[0])) if axis == 1 else (lambda l, i, k: (k[0] * nr + i, 0))
    return pl.pallas_call(
        body, name=name, out_shape=jax.ShapeDtypeStruct((L * R, C), f32),
        grid_spec=pltpu.PrefetchScalarGridSpec(
            num_scalar_prefetch=1, grid=(L, nr),
            in_specs=[pl.BlockSpec((tm, C), own_map)] * L + [pl.BlockSpec((3, 1, tm, C), lambda l, i, k: (0, l, i, 0))],
            out_specs=pl.BlockSpec((tm, C), lambda l, i, k: (l * nr + i, 0))),
        compiler_params=_cp("parallel", "parallel"),
    )(kidx, *owns, recv)


def _adamw(w, m, v, parts, *, name):
    R, C = w.shape
    tm = _rows(R, C)
    npart = len(parts)

    def body(*refs):
        w_ref, m_ref, v_ref = refs[:3]
        g_ref, d_ref, m2_ref, v2_ref = refs[3 + npart:]
        g = refs[3][...]
        for p_ref in refs[4:3 + npart]:
            g = g + p_ref[...]
        g_ref[...] = g
        d_ref[...], m2_ref[...], v2_ref[...] = _adam_math(w_ref[...], m_ref[...], v_ref[...], g)

    blk = pl.BlockSpec((tm, C), lambda i: (i, 0))
    return pl.pallas_call(
        body, name=name, grid=(R // tm,),
        in_specs=[blk] * (3 + npart), out_specs=[blk] * 4,
        out_shape=[jax.ShapeDtypeStruct((R, C), f32)] * 4, compiler_params=_cp("parallel"),
    )(w, m, v, *parts)


def _adam_math(w, m, v, g):
    m2 = ADAM_B1 * m + (1.0 - ADAM_B1) * g
    v2 = ADAM_B2 * v + (1.0 - ADAM_B2) * (g * g)
    m_hat = m2 / (1.0 - ADAM_B1 ** ADAM_STEP)
    v_hat = v2 / (1.0 - ADAM_B2 ** ADAM_STEP)
    return -ADAM_LR * (m_hat / (jnp.sqrt(v_hat) + ADAM_EPS) + ADAM_WD * w), m2, v2


def _adamw_small(landed, w, m, v, kidx, ra, rb):
    rs = ra + N_CHIPS * rb

    def body(k_ref, l_ref, w_ref, m_ref, v_ref, g_ref, d_ref, m2_ref, v2_ref):
        mine = pl.multiple_of(ra + k_ref[0] * rb, SUBLANES)
        for lo, n, off in ((0, ra, 0), (ra, rb, mine)):
            g = l_ref[pl.ds(off, n), :]
            for d in range(1, N_DEV):
                g = g + l_ref[pl.ds(d * rs + off, n), :]
            rows = pl.ds(lo, n)
            delta, m2, v2 = _adam_math(w_ref[rows, :], m_ref[rows, :], v_ref[rows, :], g)
            g_ref[rows, :] = g
            d_ref[rows, :] = delta
            m2_ref[rows, :] = m2
            v2_ref[rows, :] = v2

    vmem = pl.BlockSpec(memory_space=pltpu.VMEM)
    return pl.pallas_call(
        body, name="adamw_small", out_shape=[jax.ShapeDtypeStruct(w.shape, f32)] * 4,
        grid_spec=pltpu.PrefetchScalarGridSpec(num_scalar_prefetch=1, grid=(), in_specs=[vmem] * 4, out_specs=[vmem] * 4),
        compiler_params=_cp(),
    )(kidx, landed, w, m, v)


def _pad_odd(w):
    return jnp.pad(w, ((0, 0), (0, ODD_PAD - w.shape[1])))


def _uq_cat(w):
    r = w.shape[0]
    return jnp.pad(w.reshape(r, MLA_HEADS, MLA_QK), ((0, 0), (0, 0), (0, HQ - MLA_QK))).reshape(r, MLA_HEADS * HQ)


def _uq_uncat(w):
    r = w.shape[0]
    return w.reshape(r, MLA_HEADS, HQ)[:, :, :MLA_QK].reshape(r, MLA_HEADS * MLA_QK)


def _to_segments(v):
    T, C = v.shape
    return v.reshape(S5_SEG, T // S5_SEG, C).transpose(1, 0, 2).reshape(T, C)


def _from_segments(v):
    T, C = v.shape
    return v.reshape(T // S5_SEG, S5_SEG, C).transpose(1, 0, 2).reshape(T, C)


def _s5_rb(T):
    return min(512, T)


def _ffn_fwd(h, hn, w_in, cw, cb, w_out, tag, next_g=None):
    au = _mm(hn, w_in, out_dtype=bf16, name=f"ffn{tag}_in", tn=1408)
    z = _ffn_mid_fwd(au, cw, cb, name=f"ffn{tag}_mid")
    return _mm(z, w_out, res=h, norm_g=next_g, name=f"ffn{tag}_out", tk=1408), (hn, au, z)


def _ffn_bwd(h, g, w_in, cw, cb, w_out, saved, dh, tag, dep=None):
    hn, au, z = saved
    dz = _mm(dh, w_out, tb=True, out_dtype=bf16, name=f"ffn{tag}_dz", tn=1408, dep=dep)
    dw_out = _mm(z, dh, ta=True, also_bf16=True, name=f"ffn{tag}_dwout", tm=1408)
    dau, dcw, dcb = _ffn_mid_bwd(au, cw, cb, dz, name=f"ffn{tag}_dmid")
    dh_in, dg = _mm(dau, w_in, tb=True, res=dh, norm_bwd=(h, g), name=f"ffn{tag}_dhn", tk=1408)
    dw_in = _mm(hn, dau, ta=True, also_bf16=True, name=f"ffn{tag}_dwin", tn=1408)
    return dh_in, dg, dw_in, dcw, dcb, dw_out


def _local_step(x, positions, target, get_w, P, put_g):
    T = x.shape[0]
    rb = _s5_rb(T)
    row = lambda v: v.reshape(1, -1)
    g_mix, g_ffn = P["norm_mix_g"], P["norm_ffn_g"]
    lbl, hng = P["hgrn_lb_logits"], P["hgrn_norm_g"]
    dsk, bg = P["s5_d"], P["s5_b_glu"]
    qg, kvg = P["mla_q_norm_g"], P["mla_kv_norm_g"]
    cw, cb = P["ffn_conv_w"], P["ffn_conv_b"]

    col = lambda v: v.reshape(S5_N, 1)
    disc_in = (col(P["s5_a_re"]), col(P["s5_a_im"]), col(jnp.repeat(P["s5_log_dt"].reshape(S5_GROUPS), S5_STATE)),
               P["s5_b_re"].reshape(S5_N, S5_GROUP), P["s5_b_im"].reshape(S5_N, S5_GROUP))
    abr, abi, bbr, bbi = _s5_disc_fwd(*disc_in)
    ar, ai = abr.reshape(1, S5_N), abi.reshape(1, S5_N)
    bbr3, bbi3 = bbr.reshape(S5_GROUPS, S5_STATE, S5_GROUP), bbi.reshape(S5_GROUPS, S5_STATE, S5_GROUP)
    bre, bim = _blockdiag(bbr3, True).astype(bf16), _blockdiag(bbi3, True).astype(bf16)
    bret, bimt = _blockdiag(bbr3).astype(bf16), _blockdiag(bbi3).astype(bf16)
    c_re, c_im = P["s5_c_re"].reshape(S5_GROUPS, S5_GROUP, S5_STATE), P["s5_c_im"].reshape(S5_GROUPS, S5_GROUP, S5_STATE)
    cre, cim = _blockdiag(c_re, True).astype(bf16), _blockdiag(c_im, True).astype(bf16)
    cret, cimt = _blockdiag(c_re).astype(bf16), _blockdiag(c_im).astype(bf16)

    hn0 = _rms_fwd(x, g_mix[0:1], name="mix0_norm")
    We = get_w("even_in", hn0)
    proj_e = _mm(hn0, We["even_w_in"], name="even_in", tn=1280)
    Wr = get_w("even_rest", proj_e)
    ya, states = _hgrn_fwd(proj_e, lbl, hng)
    u_seg = _to_segments(proj_e[:, 4 * 512:])
    fr, fi = _s5_final(u_seg, bre, bim, ar, ai, rb=rb)
    yb_seg, s0r, s0i = _s5_fwd(u_seg, bre, bim, ar, ai, fr, fi, cre, cim, dsk, Wr["s5_w_glu"], bg, rb=rb)
    ycat = jnp.concatenate([ya, _from_segments(yb_seg)], axis=1)
    h1, hnf0 = _mm(ycat, Wr["even_w_out"], res=x, norm_g=g_ffn[0:1], name="even_out")
    Wf0 = get_w("ffn0", h1)
    (h2, hn2), ffn0 = _ffn_fwd(h1, hnf0, Wf0["ffn_w_in"], cw[0], cb[0:1], Wf0["ffn_w_out"], 0, next_g=g_mix[1:2])

    tabs = _rope_tables(positions)
    Wo = get_w("odd", hn2)
    proj_o = _mm(hn2, Wo["odd_w_in"], name="odd_in")
    cqn, ckvn, kr = _mla_prep_fwd(proj_o, qg, kvg, tabs)
    q = _q_post(_mm(cqn, Wo["mla_w_uq"], name="mla_uq"), tabs, transpose=False, name="q_post")
    kvb = _mm(ckvn, Wo["mla_w_ukv"], out_dtype=bf16, name="mla_ukv")
    o, lse = _flash_fwd(q, kvb, kr)
    h3, hnf1 = _mm(o, Wo["odd_w_out"], res=h2, norm_g=g_ffn[1:2], name="odd_out")
    Wf1 = get_w("ffn1", h3)
    h4, ffn1 = _ffn_fwd(h3, hnf1, Wf1["ffn_w_in"], cw[1], cb[1:2], Wf1["ffn_w_out"], 1)
    loss, dh4, dg_final = _loss_head(h4, row(P["final_norm_g"]), target)

    dh3, dg_ffn1, dw_fin1, dcw1, dcb1, dw_fout1 = _ffn_bwd(
        h3, g_ffn[1:2], Wf1["ffn_w_in"], cw[1], cb[1:2], Wf1["ffn_w_out"], ffn1, dh4, 1)
    sent = put_g("ffn1", {"ffn_w_in": dw_fin1, "ffn_w_out": dw_fout1})
    do = _mm(dh3, Wo["odd_w_out"], tb=True, out_dtype=bf16, name="odd_do", dep=sent)
    dw_oout = _mm(o, dh3, ta=True, also_bf16=True, name="odd_dwout")
    dkv, dkr_h, dq = _flash_bwd(q, kvb, kr, o, do, lse)
    dq = _q_post(dq, tabs, transpose=True, name="dq_post")
    dw_uq = _mm(cqn, dq, ta=True, also_bf16=True, name="mla_dwuq")
    dcqn = _mm(dq, Wo["mla_w_uq"], tb=True, name="mla_dcq")
    dw_ukv = _mm(ckvn, dkv, ta=True, also_bf16=True, name="mla_dwukv")
    dckvn = _mm(dkv, Wo["mla_w_ukv"], tb=True, name="mla_dckv")
    dproj_o, dqg, dkvg = _mla_prep_bwd(proj_o, qg, kvg, tabs, dcqn, dckvn, dkr_h)
    dw_oin = _mm(hn2, dproj_o, ta=True, also_bf16=True, name="odd_dwin")
    sent = put_g("odd", {"odd_w_in": dw_oin, "mla_w_uq": dw_uq, "mla_w_ukv": dw_ukv, "odd_w_out": dw_oout})
    dh2, dg_mix1 = _mm(dproj_o, Wo["odd_w_in"], tb=True, res=dh3, norm_bwd=(h2, g_mix[1:2]), name="odd_dhn")

    dh1, dg_ffn0, dw_fin0, dcw0, dcb0, dw_fout0 = _ffn_bwd(
        h1, g_ffn[0:1], Wf0["ffn_w_in"], cw[0], cb[0:1], Wf0["ffn_w_out"], ffn0, dh2, 0, dep=sent)
    sent = put_g("ffn0", {"ffn_w_in": dw_fin0, "ffn_w_out": dw_fout0})
    dycat = _mm(dh1, Wr["even_w_out"], tb=True, name="even_dy", dep=sent)
    dw_eout = _mm(ycat, dh1, ta=True, also_bf16=True, name="even_dwout")
    dq_h, df_h, di_h, dg_h, dlbl, dhng = _hgrn_bwd(proj_e, lbl, hng, states, dycat)
    dyb_seg = _to_segments(dycat[:, 512:])
    dy_s5, glr, gli, dcre, dcim, dd, dwg, dbg = _s5_bwd_a(
        u_seg, bre, bim, ar, ai, s0r, s0i, cre, cim, cret, cimt, dsk, Wr["s5_w_glu"], bg, dyb_seg, rb=rb)
    du_seg, dbre, dbim, dar, dai = _s5_bwd_b(
        u_seg, bre, bim, bret, bimt, ar, ai, s0r, s0i, glr, gli, cret, cimt, dsk, dy_s5, rb=rb)
    dproj_e = jnp.concatenate([dq_h, df_h, di_h, dg_h, _from_segments(du_seg)], axis=1)
    dx, dg_mix0 = _mm(dproj_e, We["even_w_in"], tb=True, res=dh1, norm_bwd=(x, g_mix[0:1]), name="even_dhn", tk=1280)
    dw_ein = _mm(hn0, dproj_e, ta=True, also_bf16=True, name="even_dwin", tn=1280)

    unblk = lambda m, a, b: jnp.swapaxes(_blockdiag_t(m, a, b), 1, 2)
    dbbr = unblk(dbre, S5_GROUP, S5_STATE).reshape(S5_N, S5_GROUP)
    dbbi = unblk(dbim, S5_GROUP, S5_STATE).reshape(S5_N, S5_GROUP)
    d_ar, d_ai, d_ldt, d_br, d_bi = _s5_disc_bwd(*disc_in, (dar.reshape(S5_N, 1), dai.reshape(S5_N, 1), dbbr, dbbi))
    small = {
        "norm_mix_g": jnp.concatenate([dg_mix0, dg_mix1], axis=0),
        "norm_ffn_g": jnp.concatenate([dg_ffn0, dg_ffn1], axis=0),
        "final_norm_g": dg_final.reshape(-1),
        "hgrn_lb_logits": dlbl, "hgrn_norm_g": dhng,
        "s5_a_re": d_ar.reshape(1, S5_GROUPS, S5_STATE), "s5_a_im": d_ai.reshape(1, S5_GROUPS, S5_STATE),
        "s5_log_dt": d_ldt.reshape(S5_GROUPS, S5_STATE).sum(axis=1).reshape(1, S5_GROUPS),
        "s5_b_re": d_br.reshape(1, S5_GROUPS, S5_STATE, S5_GROUP), "s5_b_im": d_bi.reshape(1, S5_GROUPS, S5_STATE, S5_GROUP),
        "s5_c_re": unblk(dcre, S5_STATE, S5_GROUP).reshape(1, S5_GROUPS, S5_GROUP, S5_STATE),
        "s5_c_im": unblk(dcim, S5_STATE, S5_GROUP).reshape(1, S5_GROUPS, S5_GROUP, S5_STATE),
        "s5_d": dd, "s5_b_glu": dbg, "mla_q_norm_g": dqg, "mla_kv_norm_g": dkvg,
        "ffn_conv_w": jnp.stack([dcw0, dcw1]), "ffn_conv_b": jnp.concatenate([dcb0, dcb1], axis=0),
    }
    put_g("even", {"even_w_in": dw_ein, "s5_w_glu": (dwg, dwg.astype(bf16)), "even_w_out": dw_eout}, small)
    return loss, dx


WEIGHTS = ["norm_mix_g", "norm_ffn_g", "final_norm_g", "even_w_in", "hgrn_lb_logits", "hgrn_norm_g", "s5_a_re", "s5_a_im",
           "s5_log_dt", "s5_b_re", "s5_b_im", "s5_c_re", "s5_c_im", "s5_d", "s5_w_glu", "s5_b_glu", "even_w_out", "odd_w_in",
           "mla_q_norm_g", "mla_w_uq", "mla_kv_norm_g", "mla_w_ukv", "odd_w_out", "ffn_w_in", "ffn_conv_w", "ffn_conv_b",
           "ffn_w_out"]
SMALL_SHARDED = {"mla_q_norm_g": 1, "mla_kv_norm_g": 1, "ffn_conv_w": 2}
SMALL = [n for n in WEIGHTS if n not in BIG]
SMALL_REP = [n for n in SMALL if n not in SMALL_SHARDED]


def _pack_rows(shapes):
    n = sum(math.prod(s) for s in shapes)
    return -(-n // (SUBLANES * LANES)) * SUBLANES


def _pack(arrays, rows):
    flat = jnp.concatenate([a.reshape(-1) for a in arrays])
    return jnp.pad(flat, (0, rows * LANES - flat.shape[0])).reshape(rows, LANES)


def _unpack(block, shapes):
    flat, out, off = block.reshape(-1), [], 0
    for s in shapes:
        n = math.prod(s)
        out.append(flat[off:off + n].reshape(s))
        off += n
    return out


def kernel(x, positions, norm_mix_g, norm_ffn_g, final_norm_g, even_w_in, hgrn_lb_logits, hgrn_norm_g, s5_a_re, s5_a_im, s5_log_dt, s5_b_re, s5_b_im, s5_c_re, s5_c_im, s5_d, s5_w_glu, s5_b_glu, even_w_out, odd_w_in, mla_q_norm_g, mla_w_uq, mla_kv_norm_g, mla_w_ukv, odd_w_out, ffn_w_in, ffn_conv_w, ffn_conv_b, ffn_w_out, loss_target, m_norm_mix_g, m_norm_ffn_g, m_final_norm_g, m_even_w_in, m_hgrn_lb_logits, m_hgrn_norm_g, m_s5_a_re, m_s5_a_im, m_s5_log_dt, m_s5_b_re, m_s5_b_im, m_s5_c_re, m_s5_c_im, m_s5_d, m_s5_w_glu, m_s5_b_glu, m_even_w_out, m_odd_w_in, m_mla_q_norm_g, m_mla_w_uq, m_mla_kv_norm_g, m_mla_w_ukv, m_odd_w_out, m_ffn_w_in, m_ffn_conv_w, m_ffn_conv_b, m_ffn_w_out, v_norm_mix_g, v_norm_ffn_g, v_final_norm_g, v_even_w_in, v_hgrn_lb_logits, v_hgrn_norm_g, v_s5_a_re, v_s5_a_im, v_s5_log_dt, v_s5_b_re, v_s5_b_im, v_s5_c_re, v_s5_c_im, v_s5_d, v_s5_w_glu, v_s5_b_glu, v_even_w_out, v_odd_w_in, v_mla_q_norm_g, v_mla_w_uq, v_mla_kv_norm_g, v_mla_w_ukv, v_odd_w_out, v_ffn_w_in, v_ffn_conv_w, v_ffn_conv_b, v_ffn_w_out):
    args = dict(locals())
    w = {n: args[n] for n in WEIGHTS}
    m = {n: args["m_" + n] for n in WEIGHTS}
    v = {n: args["v_" + n] for n in WEIGHTS}
    k = 2 * lax.axis_index("x") + lax.axis_index("y")
    kidx = k.reshape(1).astype(jnp.int32)
    axis2d = lambda n: BIG[n] - (1 if n in LAYERED else 0)
    slab = lambda n: w[n].shape[1 + axis2d(n)]

    small_sh_shapes = [w[n].shape for n in SMALL_SHARDED]
    rb = _pack_rows(small_sh_shapes)
    items = {}
    for group, names in GROUPS.items():
        layer = GROUP_LAYER.get(group, 0)
        items[group] = [(_Gather(axis2d(n), slab(n)), None,
                         _place_slab(w[n][layer], axis2d(n), N_CHIPS, kidx, bf16, name=f"place_{n}_{layer}")) for n in names]
    items["even_in"].append((_Gather(0, rb), None,
                             _place_slab(_pack([w[n] for n in SMALL_SHARDED], rb), 0, N_CHIPS, kidx, f32, name="place_small")))
    gathers, tokens = {}, []
    for group in GROUPS:
        sems, srcs, lands, token = _push_start(f"gather_start_{group}", items[group])
        gathers[group] = ([it[0] for it in items[group]], sems, srcs, lands)
        tokens.append(token[0, 0])
    started = functools.reduce(jnp.add, tokens)

    def landed(group, after):
        return _push_wait(f"gather_wait_{group}", [gathers[group]], [after])[0]

    even = landed("even_in", (started + norm_mix_g[0, 0]).reshape(1))
    per_chip = [_unpack(even[-1][c * rb:(c + 1) * rb], small_sh_shapes) for c in range(N_CHIPS)]
    P = {n: w[n] for n in SMALL_REP}
    for i, (n, ax) in enumerate(SMALL_SHARDED.items()):
        P[n] = jnp.concatenate([per_chip[c][i] for c in range(N_CHIPS)], axis=ax)
    P["mla_q_norm_g"], P["mla_kv_norm_g"] = P["mla_q_norm_g"].reshape(1, -1), P["mla_kv_norm_g"].reshape(1, -1)
    fix_w = {"odd_w_in": _pad_odd, "mla_w_uq": _uq_cat}

    def get_w(group, after):
        full = even if group == "even_in" else landed(group, after)
        return {n: fix_w.get(n, lambda a: a)(a) for n, a in zip(GROUPS[group], full)}

    fix_g = {"odd_w_in": lambda g: g[:, :odd_w_in.shape[2]], "mla_w_uq": _uq_uncat}
    g32, scatters, land_now = {}, {}, {}
    ra = _pack_rows([w[n].shape for n in SMALL_REP])
    rs = ra + N_CHIPS * rb
    didx = (2 * kidx + lax.axis_index("c")).astype(jnp.int32)

    def put_g(group, grads, small=None):
        layer = GROUP_LAYER.get(group)
        routes, srcs, names = [], [], list(grads)
        for n in names:
            f = fix_g.get(n, lambda g: g)
            g32.setdefault(n, {})[layer or 0] = f(grads[n][0])
            routes.append(_Scatter(axis2d(n), slab(n), layer if n in LAYERED else None))
            srcs.append(f(grads[n][1]))
            if n not in land_now:
                land_now[n] = lax.empty((3,) + w[n].shape[0 if n in LAYERED else 1:], bf16)
        if small is not None:
            blocks = [_pack([small[n] for n in SMALL_REP], ra)]
            for chip in range(N_CHIPS):
                sl = lambda n, ax: lax.slice_in_dim(small[n].reshape(w[n].shape[:ax] + (-1,) + w[n].shape[ax + 1:]),
                                                    chip * w[n].shape[ax], (chip + 1) * w[n].shape[ax], axis=ax)
                blocks.append(_pack([sl(n, ax) for n, ax in SMALL_SHARDED.items()], rb))
            names.append("small")
            routes.append(_ToAll(rs))
            srcs.append(None)
            land_now["small"] = _place_slab(jnp.concatenate(blocks), 0, N_DEV, didx, f32, name="place_small_grads")
        sems, srcs, lands, token = _push_start(f"scatter_start_{group}", [(r, s, land_now[n]) for r, s, n in zip(routes, srcs, names)])
        land_now.update(zip(names, lands))
        scatters[group] = (routes, sems, srcs, names)
        sent.append(token)
        return token

    sent = []
    loss, dx = _local_step(x[0], positions[0], loss_target[0], get_w, P, put_g)
    sent_last = sent[-1]
    loss = lax.psum(loss[0, 0], ("x", "y", "c"))

    out = {}

    def finish(tag, groups, after):
        waits = [(scatters[g][0], scatters[g][1], scatters[g][2], [land_now[n] for n in scatters[g][3]]) for g in groups]
        for g, lands in zip(groups, _push_wait(f"scatter_wait_{tag}", waits, after)):
            land_now.update(zip(scatters[g][3], lands))
        names = [n for n in dict.fromkeys(n for g in groups for n in scatters[g][3]) if n != "small"]
        part = {}
        for n in names:
            recv = land_now[n] if n in LAYERED else land_now[n][:, None]
            part[n] = _sum4([g32[n][l] for l in sorted(g32[n])], axis2d(n), recv, kidx, name=f"sum4_{n}")
        other = _swap_with_sibling(part, tag)
        done = []
        for n in names:
            C = part[n].shape[-1]
            res = _adamw(w[n].reshape(-1, C), m[n].reshape(-1, C), v[n].reshape(-1, C), [part[n], other[n]], name=f"adamw_{n}")
            out[n] = [r.reshape(w[n].shape) for r in res]
            done.append(res[0])
        return done

    done = finish("a", ["ffn1", "odd", "ffn0"], [dx, sent_last])
    finish("b", ["even"], done)

    order = SMALL_REP + list(SMALL_SHARDED)
    packed = lambda src: jnp.concatenate([_pack([src[n] for n in SMALL_REP], ra), _pack([src[n] for n in SMALL_SHARDED], rb)])
    res = _adamw_small(land_now["small"], packed(w), packed(m), packed(v), kidx, ra, rb)
    for r in res:
        parts = _unpack(r[:ra], [w[n].shape for n in SMALL_REP]) + _unpack(r[ra:], small_sh_shapes)
        for n, a in zip(order, parts):
            out.setdefault(n, []).append(a)

    return (loss, dx[None], *[out[n][0] for n in WEIGHTS], *[out[n][1] for n in WEIGHTS],
            *[out[n][2] for n in WEIGHTS], *[out[n][3] for n in WEIGHTS])
```

```python
import functools
import math

import jax
import jax.numpy as jnp
from jax import lax
from jax.experimental import pallas as pl
from jax.experimental.pallas import tpu as pltpu

f32, bf16 = jnp.float32, jnp.bfloat16
EPS = 1e-6
LANES = 128
SUBLANES = 8
VMEM_BYTES = 48 * 1024 * 1024
HGRN_CHUNK = 64
HGRN_HEADS = 4
S5_GROUPS, S5_STATE, S5_GROUP = 32, 64, 16
S5_N = S5_GROUPS * S5_STATE
S5_SEG = SUBLANES
MLA_HEADS, MLA_NOPE, MLA_ROPE, MLA_V = 8, 128, 64, 128
MLA_QK = MLA_NOPE + MLA_ROPE
MLA_Q_RANK, MLA_KV_RANK = 384, 256
ROPE_THETA = 10000.0
D_FF = 2816
ADAM_LR, ADAM_B1, ADAM_B2, ADAM_EPS, ADAM_WD, ADAM_STEP = 0.001, 0.9, 0.999, 1e-08, 0.01, 10
MESH = pl.DeviceIdType.MESH
HI = lax.Precision.HIGHEST


def _cp(*dims):
    return pltpu.CompilerParams(dimension_semantics=dims if dims else None, vmem_limit_bytes=VMEM_BYTES)


def _tile(n, t):
    if n <= t:
        return n
    c = (t // LANES) * LANES
    while c >= LANES:
        if n % c == 0:
            return c
        c -= LANES
    return n


def _dot(a, b, dn=None, precision=None):
    if dn is None:
        dn = (((a.ndim - 1,), (0,)), ((), ()))
    return lax.dot_general(a, b, dn, preferred_element_type=f32, precision=precision)


NT = (((1,), (1,)), ((), ()))
TN = (((0,), (0,)), ((), ()))


def _bdot(a, b, dn=None):
    return _dot(a.astype(bf16), b.astype(bf16), dn)


def _mm(a, b, *, name, ta=False, tb=False, out_dtype=f32, res=None, also_bf16=False, tm=1024, tn=1024, tk=1024, dep=None,
        norm_g=None, norm_bwd=None):
    halves = lambda s: (s[1], 2 * s[2]) if len(s) == 3 else s
    M, K = (a.shape[1], a.shape[0]) if ta else halves(a.shape)
    N = b.shape[0] if tb else halves(b.shape)[1]
    rows = norm_g is not None or norm_bwd is not None
    if rows:
        tm, tn = min(tm, 512), N
    tm, tn, tk = _tile(M, tm), _tile(N, tn), _tile(K, tk)
    if a.ndim == 3:
        tk = _tile(K // 2, tk)
    if b.ndim == 3:
        tn = _tile(N // 2, tn)
    nk = K // tk
    dn = (((0 if ta else 1,), (1 if tb else 0,)), ((), ()))
    extra = [] if norm_bwd is None else list(norm_bwd)
    if norm_g is not None:
        extra.append(norm_g)

    def body(*refs):
        a_ref, b_ref = refs[0], refs[1]
        r_ref = refs[2] if res is not None else None
        nin = 2 + (res is not None) + (dep is not None) + len(extra)
        ex = refs[nin - len(extra):nin]
        outs = refs[nin:-1] if nk > 1 else refs[nin:]
        acc = refs[-1] if nk > 1 else None
        k = pl.program_id(2)
        p = _bdot(a_ref[...], b_ref[...], dn)

        if nk > 1:
            @pl.when(k == 0)
            def _():
                acc[...] = p

            @pl.when((k > 0) & (k < nk - 1))
            def _():
                acc[...] += p

        @pl.when(k == nk - 1)
        def _():
            r = acc[...] + p if nk > 1 else p
            if norm_bwd is not None:
                r, dg = _rms_bwd_math(ex[0][...], ex[1][...], r)

                @pl.when(pl.program_id(0) == 0)
                def _():
                    outs[1][...] = dg

                @pl.when(pl.program_id(0) > 0)
                def _():
                    outs[1][...] += dg

            if r_ref is not None:
                r = r + r_ref[...]
            outs[0][...] = r.astype(out_dtype)
            if also_bf16:
                outs[1][...] = r.astype(bf16)
            if norm_g is not None:
                outs[1][...] = _rms(r, ex[-1][...]).astype(bf16)

    a_spec = pl.BlockSpec((tk, tm), lambda i, j, k: (k, i)) if ta else pl.BlockSpec((tm, tk), lambda i, j, k: (i, k))
    b_spec = pl.BlockSpec((tn, tk), lambda i, j, k: (j, k)) if tb else pl.BlockSpec((tk, tn), lambda i, j, k: (k, j))
    if a.ndim == 3:
        kh = K // 2 // tk
        a_spec = pl.BlockSpec((None, tm, tk), lambda i, j, k: (k // kh, i, k % kh))
    if b.ndim == 3:
        nh = N // 2 // tn
        b_spec = pl.BlockSpec((None, tk, tn), lambda i, j, k: (j // nh, k, j % nh))
    o_spec = pl.BlockSpec((tm, tn), lambda i, j, k: (i, j))
    in_specs, args = [a_spec, b_spec], [a, b]
    if res is not None:
        in_specs.append(o_spec)
        args.append(res)
    if dep is not None:
        in_specs.append(pl.BlockSpec(memory_space=pl.ANY))
        args.append(dep)
    vec = pl.BlockSpec((1, tn), lambda i, j, k: (0, j))
    if norm_bwd is not None:
        in_specs += [o_spec, vec]
    if norm_g is not None:
        in_specs.append(vec)
    args += extra
    out_shape = [jax.ShapeDtypeStruct((M, N), out_dtype)]
    out_specs = [o_spec]
    if also_bf16 or norm_g is not None:
        out_shape.append(jax.ShapeDtypeStruct((M, N), bf16))
        out_specs.append(o_spec)
    if norm_bwd is not None:
        out_shape.append(jax.ShapeDtypeStruct((1, N), f32))
        out_specs.append(vec)
    dims = ("arbitrary" if norm_bwd is not None else "parallel", "parallel", "arbitrary")
    out = pl.pallas_call(
        body, name=name, grid=(M // tm, N // tn, nk), in_specs=in_specs, out_specs=out_specs, out_shape=out_shape,
        scratch_shapes=[pltpu.VMEM((tm, tn), f32)] if nk > 1 else [], compiler_params=_cp(*dims),
    )(*args)
    return out if len(out) > 1 else out[0]


def _rms_fwd(x, g, *, name, col=0, width=None, tm=512):
    T = x.shape[0]
    width = x.shape[1] if width is None else width
    tm = _tile(T, tm)

    def body(x_ref, g_ref, o_ref):
        xv = x_ref[...]
        r = lax.rsqrt(jnp.mean(xv * xv, axis=-1, keepdims=True) + EPS)
        o_ref[...] = (xv * r * g_ref[...]).astype(bf16)

    return pl.pallas_call(
        body, name=name, grid=(T // tm,),
        in_specs=[pl.BlockSpec((tm, width), lambda i: (i, col)), pl.BlockSpec((1, width), lambda i: (0, 0))],
        out_specs=pl.BlockSpec((tm, width), lambda i: (i, 0)), out_shape=jax.ShapeDtypeStruct((T, width), bf16),
        compiler_params=_cp("parallel"),
    )(x, g)


def _rms_bwd_math(xv, g, dy):
    r = lax.rsqrt(jnp.mean(xv * xv, axis=-1, keepdims=True) + EPS)
    xh = xv * r
    dxh = dy * g
    dx = r * (dxh - xh * jnp.mean(dxh * xh, axis=-1, keepdims=True))
    dg = jnp.sum(dy * xh, axis=0, keepdims=True)
    return dx, dg


def _loss_head(h, g, target, *, tm=512):
    T, D = h.shape
    tm = _tile(T, tm)

    def body(h_ref, g_ref, t_ref, loss_ref, dh_ref, dg_ref):
        hv, gv = h_ref[...], g_ref[...]
        r = lax.rsqrt(jnp.mean(hv * hv, axis=-1, keepdims=True) + EPS)
        e = hv * r * gv - t_ref[...]
        part = 0.5 * jnp.sum(jnp.mean(e * e, axis=-1, keepdims=True), axis=0, keepdims=True)
        dx, dg = _rms_bwd_math(hv, gv, e * (1.0 / D))
        dh_ref[...] = dx

        @pl.when(pl.program_id(0) == 0)
        def _():
            loss_ref[...] = part
            dg_ref[...] = dg

        @pl.when(pl.program_id(0) > 0)
        def _():
            loss_ref[...] += part
            dg_ref[...] += dg

    row = pl.BlockSpec((tm, D), lambda i: (i, 0))
    vec = pl.BlockSpec((1, D), lambda i: (0, 0))
    return pl.pallas_call(
        body, name="loss_head", grid=(T // tm,), in_specs=[row, vec, row],
        out_specs=[pl.BlockSpec((1, 1), lambda i: (0, 0)), row, vec],
        out_shape=[jax.ShapeDtypeStruct((1, 1), f32), jax.ShapeDtypeStruct((T, D), f32), jax.ShapeDtypeStruct((1, D), f32)],
        compiler_params=_cp("arbitrary"),
    )(h, g, target)


FFN_W = 2 * LANES
FFN_ROWS = 128
HALO = 2 * SUBLANES


def _conv_taps(a_ref, c, rc):
    if isinstance(c, int) and c == 0:
        ext = jnp.concatenate([jnp.zeros((HALO, FFN_W), f32), a_ref[pl.ds(0, rc), :].astype(f32)], axis=0)
    else:
        ext = a_ref[pl.ds(pl.multiple_of(c * rc - HALO, HALO), rc + HALO), :].astype(f32)
    return ext[HALO:], pltpu.roll(ext, 1, 0)[HALO:], pltpu.roll(ext, 2, 0)[HALO:]


def _chunk_rows(c, rc):
    return pl.ds(c * rc, rc) if isinstance(c, int) else pl.ds(pl.multiple_of(c * rc, rc), rc)


def _ffn_mid_fwd(au, cw, cb, *, name):
    T = au.shape[0]
    F = au.shape[1] // 2
    nb = F // FFN_W
    rc = min(FFN_ROWS, T)
    nc = T // rc

    def body(a_ref, u_ref, w_ref, b_ref, z_ref):
        w, b = w_ref[...], b_ref[...]

        def chunk(c):
            a, a1, a2 = _conv_taps(a_ref, c, rc)
            rows = _chunk_rows(c, rc)
            ac = (w[0:1] * a2 + w[1:2] * a1 + w[2:3] * a + b).astype(bf16)
            z_ref[rows, :] = ac * jax.nn.sigmoid(ac) * u_ref[rows, :]

        chunk(0)
        lax.fori_loop(1, nc, lambda c, _: chunk(c), None)

    return pl.pallas_call(
        body, name=name, grid=(nb,),
        in_specs=[pl.BlockSpec((T, FFN_W), lambda j: (0, j)), pl.BlockSpec((T, FFN_W), lambda j: (0, nb + j)),
                  pl.BlockSpec((3, FFN_W), lambda j: (0, j)), pl.BlockSpec((1, FFN_W), lambda j: (0, j))],
        out_specs=pl.BlockSpec((T, FFN_W), lambda j: (0, j)), out_shape=jax.ShapeDtypeStruct((T, F), bf16),
        compiler_params=_cp("parallel"),
    )(au, au, cw, cb)


def _ffn_mid_bwd(au, cw, cb, dz, *, name):
    T = au.shape[0]
    F = au.shape[1] // 2
    nb = F // FFN_W
    rc = min(FFN_ROWS, T)
    nc = T // rc

    def body(a_ref, u_ref, w_ref, b_ref, dz_ref, dau_ref, dw_ref, db_ref):
        w, b = w_ref[...], b_ref[...]

        def chunk(c, carry):
            nxt, s0, s1, s2, sb = carry
            a, a1, a2 = _conv_taps(a_ref, c, rc)
            rows = _chunk_rows(c, rc)
            ac = (w[0:1] * a2 + w[1:2] * a1 + w[2:3] * a + b).astype(bf16)
            sg = jax.nn.sigmoid(ac)
            dz = dz_ref[rows, :]
            dau_ref[1, rows, :] = dz * ac * sg
            dac = (dz * u_ref[rows, :] * sg * (1.0 + ac * (1.0 - sg))).astype(f32)
            ext = jnp.concatenate([dac, nxt], axis=0)
            d1, d2 = pltpu.roll(ext, rc + HALO - 1, 0)[:rc], pltpu.roll(ext, rc + HALO - 2, 0)[:rc]
            dau_ref[0, rows, :] = (w[2:3] * dac + w[1:2] * d1 + w[0:1] * d2).astype(bf16)
            tot = lambda v: jnp.sum(v, axis=0, keepdims=True)
            return dac[:HALO], s0 + tot(dac * a2), s1 + tot(dac * a1), s2 + tot(dac * a), sb + tot(dac)

        z = jnp.zeros((1, FFN_W), f32)
        carry = (jnp.zeros((HALO, FFN_W), f32), z, z, z, z)
        carry = lax.fori_loop(0, nc - 1, lambda k, cr: chunk(nc - 1 - k, cr), carry)
        _, s0, s1, s2, sb = chunk(0, carry)
        rows = lax.broadcasted_iota(jnp.int32, (3, FFN_W), 0)
        dw_ref[...] = jnp.where(rows == 0, s0, jnp.where(rows == 1, s1, s2))
        db_ref[...] = sb

    col = lambda off: pl.BlockSpec((T, FFN_W), lambda j: (0, off + j))
    return pl.pallas_call(
        body, name=name, grid=(nb,),
        in_specs=[col(0), col(nb), pl.BlockSpec((3, FFN_W), lambda j: (0, j)), pl.BlockSpec((1, FFN_W), lambda j: (0, j)), col(0)],
        out_specs=[pl.BlockSpec((2, T, FFN_W), lambda j: (0, 0, j)), pl.BlockSpec((3, FFN_W), lambda j: (0, j)),
                   pl.BlockSpec((1, FFN_W), lambda j: (0, j))],
        out_shape=[jax.ShapeDtypeStruct((2, T, F), bf16), jax.ShapeDtypeStruct((3, F), f32), jax.ShapeDtypeStruct((1, F), f32)],
        compiler_params=_cp("parallel"),
    )(au, au, cw, cb, dz)


BNN = (((2,), (1,)), ((0,), (0,)))
BNT = (((2,), (2,)), ((0,), (0,)))
BTN = (((1,), (1,)), ((0,), (0,)))


def _heads(x):
    return jnp.stack([x[:, h * LANES:(h + 1) * LANES] for h in range(HGRN_HEADS)])


def _put_heads(ref, rows, x, dtype):
    for h in range(HGRN_HEADS):
        ref[rows, h * LANES:(h + 1) * LANES] = x[h].astype(dtype)


def _hgrn_lb(l):
    m = jnp.max(l, axis=0, keepdims=True)
    e = jnp.exp(l - m)
    return e[0:1] / jnp.sum(e, axis=0, keepdims=True)


def _hgrn_chunk(q, fx, lb):
    H, C = q.shape[0], q.shape[1]
    sg = jax.nn.sigmoid(fx)
    F = lb + (1.0 - lb) * sg
    k = 1.0 - F
    logF = jnp.log(F)
    r = lax.broadcasted_iota(jnp.int32, (H, C, C), 1)
    c = lax.broadcasted_iota(jnp.int32, (H, C, C), 2)
    tril = (r >= c)
    b = _dot(tril.astype(f32), logF, BNN, precision=HI)
    bl = jnp.sum(logF, axis=1, keepdims=True)
    eb = jnp.exp(b)
    enb = jnp.exp(-b)
    elb = jnp.exp(bl - b)
    return dict(sg=sg, F=F, k=k, b=b, bl=bl, eb=eb, enb=enb, elb=elb, qd=q * eb, kd=k * enb, kl=k * elb, tril=tril)


def _hgrn_fwd(proj, lbl, ng, *, rb=512):
    T = proj.shape[0]
    rb = min(rb, T)
    cpb = rb // HGRN_CHUNK
    nblk = T // rb
    H = HGRN_HEADS

    def body(q_ref, f_ref, i_ref, g_ref, lbl_ref, ng_ref, y_ref, st_ref, S):
        @pl.when(pl.program_id(0) == 0)
        def _():
            S[...] = jnp.zeros_like(S)

        lb = _heads(_hgrn_lb(lbl_ref[...]))
        ngv = _heads(ng_ref[...])
        for c in range(cpb):
            sl = pl.ds(c * HGRN_CHUNK, HGRN_CHUNK)
            v, gx = _heads(i_ref[sl, :]), _heads(g_ref[sl, :])
            ch = _hgrn_chunk(_heads(q_ref[sl, :]), _heads(f_ref[sl, :]), lb)
            att = jnp.where(ch["tril"], _bdot(ch["qd"], ch["kd"], BNT), 0.0)
            St = S[...]
            st_ref[:, c] = St
            o = _bdot(att, v, BNN) + _bdot(ch["qd"], St, BNT)
            S[...] = St * jnp.exp(ch["bl"]) + _bdot(v, ch["kl"], BTN)
            r = lax.rsqrt(jnp.mean(o * o, axis=-1, keepdims=True) + EPS)
            _put_heads(y_ref, sl, o * r * ngv * (gx * jax.nn.sigmoid(gx)), bf16)

    col = lambda off: pl.BlockSpec((rb, H * LANES), lambda n: (n, off))
    return pl.pallas_call(
        body, name="hgrn_fwd", grid=(nblk,),
        in_specs=[col(0), col(1), col(2), col(3), pl.BlockSpec((2, H * LANES), lambda n: (0, 0)),
                  pl.BlockSpec((1, H * LANES), lambda n: (0, 0))],
        out_specs=[pl.BlockSpec((rb, H * LANES), lambda n: (n, 0)),
                   pl.BlockSpec((H, cpb, LANES, LANES), lambda n: (0, n, 0, 0))],
        out_shape=[jax.ShapeDtypeStruct((T, H * LANES), bf16),
                   jax.ShapeDtypeStruct((H, T // HGRN_CHUNK, LANES, LANES), f32)],
        scratch_shapes=[pltpu.VMEM((H, LANES, LANES), f32)], compiler_params=_cp("arbitrary"),
    )(proj, proj, proj, proj, lbl, ng)


def _hgrn_bwd(proj, lbl, ng, states, dy, *, rb=512):
    T = proj.shape[0]
    rb = min(rb, T)
    cpb = rb // HGRN_CHUNK
    nblk = T // rb
    H = HGRN_HEADS
    C = HGRN_CHUNK

    def body(q_ref, f_ref, i_ref, g_ref, lbl_ref, ng_ref, st_ref, dy_ref,
             dq_ref, df_ref, di_ref, dg_ref, dl_ref, dng_ref, dS, dlb_acc, dng_acc):
        n = pl.program_id(0)

        @pl.when(n == 0)
        def _():
            dS[...] = jnp.zeros_like(dS)
            dlb_acc[...] = jnp.zeros_like(dlb_acc)
            dng_acc[...] = jnp.zeros_like(dng_acc)

        lb_row = _hgrn_lb(lbl_ref[...])
        lb = _heads(lb_row)
        ngv = _heads(ng_ref[...])
        r_i = lax.broadcasted_iota(jnp.int32, (H, C, C), 1)
        c_i = lax.broadcasted_iota(jnp.int32, (H, C, C), 2)
        triu = (c_i >= r_i).astype(f32)
        rows_sum = lambda x: jnp.sum(x, axis=1, keepdims=True)
        for c in reversed(range(cpb)):
            sl = pl.ds(c * C, C)
            q, v, gx = _heads(q_ref[sl, :]), _heads(i_ref[sl, :]), _heads(g_ref[sl, :])
            ch = _hgrn_chunk(q, _heads(f_ref[sl, :]), lb)
            qd, kd, kl = ch["qd"], ch["kd"], ch["kl"]
            att = jnp.where(ch["tril"], _bdot(qd, kd, BNT), 0.0)
            St = st_ref[:, c]
            o = _bdot(att, v, BNN) + _bdot(qd, St, BNT)
            r = lax.rsqrt(jnp.mean(o * o, axis=-1, keepdims=True) + EPS)
            on = o * r
            sgg = jax.nn.sigmoid(gx)
            gate = gx * sgg
            dyv = _heads(dy_ref[sl, :].astype(f32))
            _put_heads(dg_ref, sl, dyv * on * ngv * sgg * (1.0 + gx * (1.0 - sgg)), bf16)
            dng_acc[...] += rows_sum(dyv * on * gate)
            don = dyv * ngv * gate
            do = r * (don - on * jnp.mean(don * on, axis=-1, keepdims=True))
            dSt = dS[...]
            dA = jnp.where(ch["tril"], _bdot(do, v, BNT), 0.0)
            dv = _bdot(att, do, BTN) + _bdot(kl, dSt, BNT)
            dqd = _bdot(dA, kd, BNN) + _bdot(do, St, BNN)
            dkd = _bdot(dA, qd, BTN)
            dkl = _bdot(v, dSt, BNN)
            dec = jnp.exp(ch["bl"])
            ddec = rows_sum(St * dSt)
            dS[...] = _bdot(do, qd, BTN) + dSt * dec
            dB = dqd * qd - dkd * kd - dkl * kl
            dbl = rows_sum(dkl * kl) + ddec * dec
            dk = dkd * ch["enb"] + dkl * ch["elb"]
            dlogF = _dot(triu, dB, BNN, precision=HI) + dbl
            dF = dlogF / ch["F"] - dk
            sg = ch["sg"]
            _put_heads(dq_ref, sl, dqd * ch["eb"], bf16)
            _put_heads(di_ref, sl, dv, bf16)
            _put_heads(df_ref, sl, dF * (1.0 - lb) * sg * (1.0 - sg), bf16)
            dlb_acc[...] += rows_sum(dF * (1.0 - sg))

        @pl.when(n == nblk - 1)
        def _():
            rows = lax.broadcasted_iota(jnp.int32, (2, LANES), 0)
            for h in range(H):
                hs = pl.ds(h * LANES, LANES)
                lbh = lb_row[:, h * LANES:(h + 1) * LANES]
                dl0 = dlb_acc[h] * lbh * (1.0 - lbh)
                dl_ref[:, hs] = jnp.where(rows == 0, dl0, -dl0)
                dng_ref[:, hs] = dng_acc[h]

    col = lambda off: pl.BlockSpec((rb, H * LANES), lambda n: (nblk - 1 - n, off))
    vec = lambda rows: pl.BlockSpec((rows, H * LANES), lambda n: (0, 0))
    tok = jax.ShapeDtypeStruct((T, H * LANES), bf16)
    return pl.pallas_call(
        body, name="hgrn_bwd", grid=(nblk,),
        in_specs=[col(0), col(1), col(2), col(3), vec(2), vec(1),
                  pl.BlockSpec((H, cpb, LANES, LANES), lambda n: (0, nblk - 1 - n, 0, 0)), col(0)],
        out_specs=[col(0), col(0), col(0), col(0), vec(2), vec(1)],
        out_shape=[tok, tok, tok, tok, jax.ShapeDtypeStruct((2, H * LANES), f32), jax.ShapeDtypeStruct((1, H * LANES), f32)],
        scratch_shapes=[pltpu.VMEM((H, LANES, LANES), f32), pltpu.VMEM((H, 1, LANES), f32), pltpu.VMEM((H, 1, LANES), f32)],
        compiler_params=_cp("arbitrary"),
    )(proj, proj, proj, proj, lbl, ng, states, dy)


def _s5_disc_math(ar, ai, ldt, br, bi):
    dt = jnp.exp(ldt)
    mag = jnp.exp(ar * dt)
    abr, abi = mag * jnp.cos(ai * dt), mag * jnp.sin(ai * dt)
    den = ar * ar + ai * ai
    xr, xi = abr - 1.0, abi
    cr = (xr * ar + xi * ai) / den
    ci = (xi * ar - xr * ai) / den
    return abr, abi, cr * br - ci * bi, cr * bi + ci * br


def _s5_disc_fwd(ar, ai, ldt, br, bi):
    def body(ar_ref, ai_ref, ldt_ref, br_ref, bi_ref, o0, o1, o2, o3):
        outs = _s5_disc_math(ar_ref[...], ai_ref[...], ldt_ref[...], br_ref[...], bi_ref[...])
        for o, v in zip((o0, o1, o2, o3), outs):
            o[...] = v

    return pl.pallas_call(
        body, name="s5_disc_fwd",
        out_shape=[jax.ShapeDtypeStruct(ar.shape, f32)] * 2 + [jax.ShapeDtypeStruct(br.shape, f32)] * 2,
    )(ar, ai, ldt, br, bi)


def _s5_disc_bwd(ar, ai, ldt, br, bi, cts):
    def body(ar_ref, ai_ref, ldt_ref, br_ref, bi_ref, c0, c1, c2, c3, o0, o1, o2, o3, o4):
        _, vjp = jax.vjp(_s5_disc_math, ar_ref[...], ai_ref[...], ldt_ref[...], br_ref[...], bi_ref[...])
        for o, v in zip((o0, o1, o2, o3, o4), vjp((c0[...], c1[...], c2[...], c3[...]))):
            o[...] = v

    return pl.pallas_call(
        body, name="s5_disc_bwd",
        out_shape=[jax.ShapeDtypeStruct(ar.shape, f32)] * 3 + [jax.ShapeDtypeStruct(br.shape, f32)] * 2,
    )(ar, ai, ldt, br, bi, *cts)


S5_LC = 512
S5_NLC = S5_N // S5_LC
S5_UB = 4
S5_UNROLL = 4


def _cmul(ar, ai, xr, xi):
    return ar * xr - ai * xi, ar * xi + ai * xr


def _cpow(ar, ai, n):
    rr, ri = None, None
    br, bi = ar, ai
    while n:
        if n & 1:
            rr, ri = (br, bi) if rr is None else _cmul(rr, ri, br, bi)
        n >>= 1
        if n:
            br, bi = _cmul(br, bi, br, bi)
    return rr, ri


def _s5_bu(u_ref, bre_ref, bim_ref, xr, xi):
    for k in range(S5_UB):
        uk = u_ref[:, k * LANES:(k + 1) * LANES].astype(bf16)
        xr[:, k * S5_LC:(k + 1) * S5_LC] = _dot(uk, bre_ref[k])
        xi[:, k * S5_LC:(k + 1) * S5_LC] = _dot(uk, bim_ref[k])


def _s5_scan(xr, xi, sr, si, ar_ref, ai_ref, nsteps, store):
    for c in range(S5_NLC):
        cs = slice(c * S5_LC, (c + 1) * S5_LC)
        a_r = jnp.broadcast_to(ar_ref[:, cs], (S5_SEG, S5_LC))
        a_i = jnp.broadcast_to(ai_ref[:, cs], (S5_SEG, S5_LC))

        def step(j, carry, cs=cs, a_r=a_r, a_i=a_i):
            pr, pi = carry
            rows = pl.ds(pl.multiple_of(j * S5_SEG, S5_SEG), S5_SEG)
            nr = a_r * pr - a_i * pi + xr[rows, cs]
            ni = a_r * pi + a_i * pr + xi[rows, cs]
            if store:
                xr[rows, cs] = nr
                xi[rows, cs] = ni
            return nr, ni

        fr, fi = lax.fori_loop(0, nsteps, step, (sr[:, cs], si[:, cs]), unroll=S5_UNROLL)
        sr[:, cs] = fr
        si[:, cs] = fi


def _s5_rscan(dr, di, xr, xi, s0r, s0i, gr, gi, acc_r, acc_i, ar_ref, ai_ref, nsteps):
    for c in range(S5_NLC):
        cs = slice(c * S5_LC, (c + 1) * S5_LC)
        a_r = jnp.broadcast_to(ar_ref[:, cs], (S5_SEG, S5_LC))
        a_i = jnp.broadcast_to(ai_ref[:, cs], (S5_SEG, S5_LC))

        def step(jj, carry, cs=cs, a_r=a_r, a_i=a_i):
            pr, pi, cr, ci = carry
            j = nsteps - 1 - jj
            rows = pl.ds(pl.multiple_of(j * S5_SEG, S5_SEG), S5_SEG)
            nr = dr[rows, cs] + a_r * pr + a_i * pi
            ni = di[rows, cs] + a_r * pi - a_i * pr
            dr[rows, cs] = nr
            di[rows, cs] = ni
            if acc_r is not None:
                prev = pl.ds(pl.multiple_of(jnp.maximum(j - 1, 0) * S5_SEG, S5_SEG), S5_SEG)
                first = j == 0
                pr_s = jnp.where(first, s0r[:, cs], xr[prev, cs])
                pi_s = jnp.where(first, s0i[:, cs], xi[prev, cs])
                cr = cr + nr * pr_s + ni * pi_s
                ci = ci - nr * pi_s + ni * pr_s
            return nr, ni, cr, ci

        z = jnp.zeros((S5_SEG, S5_LC), f32)
        init = (gr[:, cs], gi[:, cs], z, z)
        fr, fi, cr, ci = lax.fori_loop(0, nsteps, step, init, unroll=S5_UNROLL)
        gr[:, cs] = fr
        gi[:, cs] = fi
        if acc_r is not None:
            acc_r[:, cs] += cr
            acc_i[:, cs] += ci


def _s5_seg_carry(fr, fi, ar, ai, seg_len, reverse):
    pr, pi = _cpow(ar, ai if not reverse else -ai, seg_len)
    rows = lax.broadcasted_iota(jnp.int32, fr.shape, 0)
    cr, ci = jnp.zeros_like(fr), jnp.zeros_like(fi)
    sh = (S5_SEG - 1) if reverse else 1
    fr_s, fi_s = pltpu.roll(fr, sh, 0), pltpu.roll(fi, sh, 0)
    order = range(S5_SEG - 2, -1, -1) if reverse else range(1, S5_SEG)
    for r in order:
        c_r, c_i = pltpu.roll(cr, sh, 0), pltpu.roll(ci, sh, 0)
        m_r, m_i = _cmul(pr, pi, c_r, c_i)
        cr = jnp.where(rows == r, m_r + fr_s, cr)
        ci = jnp.where(rows == r, m_i + fi_s, ci)
    return cr, ci


def _gelu_parts(y):
    c0 = math.sqrt(2.0 / math.pi)
    t = jnp.tanh(c0 * (y + 0.044715 * y * y * y))
    z = 0.5 * y * (1.0 + t)
    dz = 0.5 * (1.0 + t) + 0.5 * y * (1.0 - t * t) * c0 * (1.0 + 3.0 * 0.044715 * y * y)
    return z, dz


def _s5_y(xr, xi, u_ref, cre_ref, cim_ref, d_ref):
    ys = []
    for k in range(S5_UB):
        cs = slice(k * S5_LC, (k + 1) * S5_LC)
        ys.append(_bdot(xr[:, cs], cre_ref[k]) - _bdot(xi[:, cs], cim_ref[k]))
    return jnp.concatenate(ys, axis=1) + d_ref[...] * u_ref[...]


def _s5_specs(T, rb, rev=False):
    nblk = T // rb
    blk = (lambda i: (nblk - 1 - i, 0)) if rev else (lambda i: (i, 0))
    tok = pl.BlockSpec((rb, 4 * LANES), blk)
    bmat = pl.BlockSpec((S5_UB, LANES, S5_LC), lambda i: (0, 0, 0))
    cmat = pl.BlockSpec((S5_UB, S5_LC, LANES), lambda i: (0, 0, 0))
    avec = pl.BlockSpec((1, S5_N), lambda i: (0, 0))
    seg = pl.BlockSpec((S5_SEG, S5_N), lambda i: (0, 0))
    cvec = pl.BlockSpec((1, 4 * LANES), lambda i: (0, 0))
    s0 = pl.BlockSpec((1, S5_SEG, S5_N), (lambda i: (nblk - 1 - i, 0, 0)) if rev else (lambda i: (i, 0, 0)))
    return dict(tok=tok, bmat=bmat, cmat=cmat, avec=avec, seg=seg, cvec=cvec, s0=s0, nblk=nblk)


def _s5_final(u, bre, bim, ar, ai, *, rb):
    T = u.shape[0]
    sp = _s5_specs(T, rb)

    def body(u_ref, bre_ref, bim_ref, ar_ref, ai_ref, fr_ref, fi_ref, xr, xi):
        @pl.when(pl.program_id(0) == 0)
        def _():
            fr_ref[...] = jnp.zeros_like(fr_ref)
            fi_ref[...] = jnp.zeros_like(fi_ref)

        _s5_bu(u_ref, bre_ref, bim_ref, xr, xi)
        _s5_scan(xr, xi, fr_ref, fi_ref, ar_ref, ai_ref, rb // S5_SEG, False)

    return pl.pallas_call(
        body, name="s5_final", grid=(sp["nblk"],),
        in_specs=[sp["tok"], sp["bmat"], sp["bmat"], sp["avec"], sp["avec"]], out_specs=[sp["seg"], sp["seg"]],
        out_shape=[jax.ShapeDtypeStruct((S5_SEG, S5_N), f32)] * 2,
        scratch_shapes=[pltpu.VMEM((rb, S5_N), f32)] * 2, compiler_params=_cp("arbitrary"),
    )(u, bre, bim, ar, ai)


def _s5_fwd(u, bre, bim, ar, ai, fr, fi, cre, cim, dsk, wg, bg, *, rb):
    T = u.shape[0]
    sp = _s5_specs(T, rb)
    seg_len = T // S5_SEG

    def body(u_ref, bre_ref, bim_ref, ar_ref, ai_ref, fr_ref, fi_ref, cre_ref, cim_ref, d_ref, wg_ref, bg_ref,
             o_ref, s0r_ref, s0i_ref, xr, xi, sr, si):
        @pl.when(pl.program_id(0) == 0)
        def _():
            i_r, i_i = _s5_seg_carry(fr_ref[...], fi_ref[...], ar_ref[...], ai_ref[...], seg_len, False)
            sr[...] = i_r
            si[...] = i_i

        s0r_ref[0] = sr[...]
        s0i_ref[0] = si[...]
        _s5_bu(u_ref, bre_ref, bim_ref, xr, xi)
        _s5_scan(xr, xi, sr, si, ar_ref, ai_ref, rb // S5_SEG, True)
        y = _s5_y(xr, xi, u_ref, cre_ref, cim_ref, d_ref)
        z, _ = _gelu_parts(y)
        v = _bdot(z, wg_ref[...]) + bg_ref[...]
        o_ref[...] = (z * jax.nn.sigmoid(v)).astype(bf16)

    wspec = pl.BlockSpec((4 * LANES, 4 * LANES), lambda i: (0, 0))
    return pl.pallas_call(
        body, name="s5_fwd", grid=(sp["nblk"],),
        in_specs=[sp["tok"], sp["bmat"], sp["bmat"], sp["avec"], sp["avec"], sp["seg"], sp["seg"], sp["cmat"], sp["cmat"],
                  sp["cvec"], wspec, sp["cvec"]],
        out_specs=[sp["tok"], sp["s0"], sp["s0"]],
        out_shape=[jax.ShapeDtypeStruct((T, 4 * LANES), bf16)] + [jax.ShapeDtypeStruct((sp["nblk"], S5_SEG, S5_N), f32)] * 2,
        scratch_shapes=[pltpu.VMEM((rb, S5_N), f32)] * 2 + [pltpu.VMEM((S5_SEG, S5_N), f32)] * 2,
        compiler_params=_cp("arbitrary"),
    )(u, bre, bim, ar, ai, fr, fi, cre, cim, dsk, wg, bg)


def _s5_bwd_a(u, bre, bim, ar, ai, s0r, s0i, cre, cim, cret, cimt, dsk, wg, bg, dout, *, rb):
    T = u.shape[0]
    sp = _s5_specs(T, rb, rev=True)

    def body(u_ref, bre_ref, bim_ref, ar_ref, ai_ref, s0r_ref, s0i_ref, cre_ref, cim_ref, cret_ref, cimt_ref,
             d_ref, wg_ref, bg_ref, do_ref, dy_ref, glr_ref, gli_ref, dcre_ref, dcim_ref, dd_ref, dwg_ref, dbg_ref,
             xr, xi, dr, di, sr, si):
        @pl.when(pl.program_id(0) == 0)
        def _():
            for r in (glr_ref, gli_ref, dcre_ref, dcim_ref, dd_ref, dwg_ref, dbg_ref):
                r[...] = jnp.zeros_like(r)

        sr[...] = s0r_ref[0]
        si[...] = s0i_ref[0]
        _s5_bu(u_ref, bre_ref, bim_ref, xr, xi)
        _s5_scan(xr, xi, sr, si, ar_ref, ai_ref, rb // S5_SEG, True)
        uv = u_ref[...]
        y = _s5_y(xr, xi, u_ref, cre_ref, cim_ref, d_ref)
        z, gz = _gelu_parts(y)
        v = _bdot(z, wg_ref[...]) + bg_ref[...]
        sg = jax.nn.sigmoid(v)
        dov = do_ref[...].astype(f32)
        dv = dov * z * sg * (1.0 - sg)
        dz = dov * sg + _bdot(dv, wg_ref[...], NT)
        dy = dz * gz
        dy_ref[...] = dy
        dwg_ref[...] += _bdot(z, dv, TN)
        dbg_ref[...] += jnp.sum(dv, axis=0, keepdims=True)
        dd_ref[...] += jnp.sum(dy * uv, axis=0, keepdims=True)
        for k in range(S5_UB):
            cs = slice(k * S5_LC, (k + 1) * S5_LC)
            dyk = dy[:, k * LANES:(k + 1) * LANES]
            dcre_ref[k] += _bdot(xr[:, cs], dyk, TN)
            dcim_ref[k] -= _bdot(xi[:, cs], dyk, TN)
            dr[:, cs] = _bdot(dyk, cret_ref[k])
            di[:, cs] = -_bdot(dyk, cimt_ref[k])
        _s5_rscan(dr, di, None, None, None, None, glr_ref, gli_ref, None, None, ar_ref, ai_ref, rb // S5_SEG)

    wspec = pl.BlockSpec((4 * LANES, 4 * LANES), lambda i: (0, 0))
    return pl.pallas_call(
        body, name="s5_bwd_a", grid=(sp["nblk"],),
        in_specs=[sp["tok"], sp["bmat"], sp["bmat"], sp["avec"], sp["avec"], sp["s0"], sp["s0"], sp["cmat"], sp["cmat"],
                  sp["bmat"], sp["bmat"], sp["cvec"], wspec, sp["cvec"], sp["tok"]],
        out_specs=[sp["tok"], sp["seg"], sp["seg"], sp["cmat"], sp["cmat"], sp["cvec"], wspec, sp["cvec"]],
        out_shape=[jax.ShapeDtypeStruct((T, 4 * LANES), f32)] + [jax.ShapeDtypeStruct((S5_SEG, S5_N), f32)] * 2
        + [jax.ShapeDtypeStruct((S5_UB, S5_LC, LANES), f32)] * 2
        + [jax.ShapeDtypeStruct((1, 4 * LANES), f32), jax.ShapeDtypeStruct((4 * LANES, 4 * LANES), f32),
           jax.ShapeDtypeStruct((1, 4 * LANES), f32)],
        scratch_shapes=[pltpu.VMEM((rb, S5_N), f32)] * 4 + [pltpu.VMEM((S5_SEG, S5_N), f32)] * 2,
        compiler_params=_cp("arbitrary"),
    )(u, bre, bim, ar, ai, s0r, s0i, cre, cim, cret, cimt, dsk, wg, bg, dout)


def _s5_bwd_b(u, bre, bim, bret, bimt, ar, ai, s0r, s0i, glr, gli, cret, cimt, dsk, dy, *, rb):
    T = u.shape[0]
    sp = _s5_specs(T, rb, rev=True)
    seg_len = T // S5_SEG
    nblk = sp["nblk"]

    def body(u_ref, bre_ref, bim_ref, bret_ref, bimt_ref, ar_ref, ai_ref, s0r_ref, s0i_ref, glr_ref, gli_ref,
             cret_ref, cimt_ref, d_ref, dy_ref, du_ref, dbre_ref, dbim_ref, dar_ref, dai_ref,
             xr, xi, dr, di, sr, si, gr, gi, acc_r, acc_i):
        @pl.when(pl.program_id(0) == 0)
        def _():
            x_r, x_i = _s5_seg_carry(glr_ref[...], gli_ref[...], ar_ref[...], ai_ref[...], seg_len, True)
            gr[...] = x_r
            gi[...] = x_i
            acc_r[...] = jnp.zeros_like(acc_r)
            acc_i[...] = jnp.zeros_like(acc_i)
            dbre_ref[...] = jnp.zeros_like(dbre_ref)
            dbim_ref[...] = jnp.zeros_like(dbim_ref)

        sr[...] = s0r_ref[0]
        si[...] = s0i_ref[0]
        _s5_bu(u_ref, bre_ref, bim_ref, xr, xi)
        _s5_scan(xr, xi, sr, si, ar_ref, ai_ref, rb // S5_SEG, True)
        dy = dy_ref[...]
        for k in range(S5_UB):
            cs = slice(k * S5_LC, (k + 1) * S5_LC)
            dyk = dy[:, k * LANES:(k + 1) * LANES]
            dr[:, cs] = _bdot(dyk, cret_ref[k])
            di[:, cs] = -_bdot(dyk, cimt_ref[k])
        sr[...] = s0r_ref[0]
        si[...] = s0i_ref[0]
        _s5_rscan(dr, di, xr, xi, sr, si, gr, gi, acc_r, acc_i, ar_ref, ai_ref, rb // S5_SEG)
        dus = []
        for k in range(S5_UB):
            cs = slice(k * S5_LC, (k + 1) * S5_LC)
            uk = u_ref[:, k * LANES:(k + 1) * LANES]
            dbre_ref[k] += _bdot(uk, dr[:, cs], TN)
            dbim_ref[k] += _bdot(uk, di[:, cs], TN)
            dus.append(_bdot(dr[:, cs], bret_ref[k]) + _bdot(di[:, cs], bimt_ref[k]))
        du_ref[...] = (jnp.concatenate(dus, axis=1) + d_ref[...] * dy).astype(bf16)

        @pl.when(pl.program_id(0) == nblk - 1)
        def _():
            dar_ref[...] = jnp.sum(acc_r[...], axis=0, keepdims=True)
            dai_ref[...] = jnp.sum(acc_i[...], axis=0, keepdims=True)

    return pl.pallas_call(
        body, name="s5_bwd_b", grid=(nblk,),
        in_specs=[sp["tok"], sp["bmat"], sp["bmat"], sp["cmat"], sp["cmat"], sp["avec"], sp["avec"], sp["s0"], sp["s0"],
                  sp["seg"], sp["seg"], sp["bmat"], sp["bmat"], sp["cvec"], sp["tok"]],
        out_specs=[sp["tok"], sp["bmat"], sp["bmat"], sp["avec"], sp["avec"]],
        out_shape=[jax.ShapeDtypeStruct((T, 4 * LANES), bf16)] + [jax.ShapeDtypeStruct((S5_UB, LANES, S5_LC), f32)] * 2
        + [jax.ShapeDtypeStruct((1, S5_N), f32)] * 2,
        scratch_shapes=[pltpu.VMEM((rb, S5_N), f32)] * 4 + [pltpu.VMEM((S5_SEG, S5_N), f32)] * 6,
        compiler_params=_cp("arbitrary"),
    )(u, bre, bim, bret, bimt, ar, ai, s0r, s0i, glr, gli, cret, cimt, dsk, dy)


def _blockdiag(w, transpose=False):
    if transpose:
        w = jnp.swapaxes(w, 1, 2)
    g, a, b = w.shape
    eye = jnp.eye(8, dtype=w.dtype)
    return jnp.einsum("kgab,gj->kgajb", w.reshape(4, 8, a, b), eye).reshape(4, 8 * a, 8 * b)


def _blockdiag_t(m, a, b):
    eye = jnp.eye(8, dtype=m.dtype)
    return jnp.einsum("kgajb,gj->kgab", m.reshape(4, 8, a, 8, b), eye).reshape(32, a, b)


ROT = MLA_ROPE // 2


def _rope_tables(positions):
    freqs = ROPE_THETA ** (-jnp.arange(0, MLA_ROPE, 2, dtype=f32) / MLA_ROPE)
    ang = positions.astype(f32)[:, None] * freqs
    cos, sin, z = jnp.cos(ang), jnp.sin(ang), jnp.zeros_like(ang)
    return (jnp.concatenate([cos, cos, z, z], axis=1), jnp.concatenate([-sin, z, z, z], axis=1),
            jnp.concatenate([z, sin, z, z], axis=1))


def _rot(x, c, sa, sb):
    return x * c + pltpu.roll(x, LANES - ROT, 1) * sa + pltpu.roll(x, ROT, 1) * sb


def _rot_t(dy, c, sa, sb):
    return dy * c + pltpu.roll(dy * sa, ROT, 1) + pltpu.roll(dy * sb, LANES - ROT, 1)


def _rms(xv, g):
    return xv * lax.rsqrt(jnp.mean(xv * xv, axis=-1, keepdims=True) + EPS) * g


QW, KVW = MLA_Q_RANK, MLA_KV_RANK
ODD_PAD = QW + KVW + LANES


def _mla_prep_fwd(proj, qg, kvg, tabs, *, tm=512):
    T = proj.shape[0]
    tm = _tile(T, tm)

    def body(p_ref, qg_ref, kvg_ref, c_ref, sa_ref, sb_ref, cq_ref, ckv_ref, kr_ref):
        cq_ref[...] = _rms(p_ref[:, :QW], qg_ref[...]).astype(bf16)
        ckv_ref[...] = _rms(p_ref[:, QW:QW + KVW], kvg_ref[...]).astype(bf16)
        kr_ref[...] = _rot(p_ref[:, QW + KVW:], c_ref[...], sa_ref[...], sb_ref[...]).astype(bf16)

    row = lambda w: pl.BlockSpec((tm, w), lambda i: (i, 0))
    vec = lambda w: pl.BlockSpec((1, w), lambda i: (0, 0))
    return pl.pallas_call(
        body, name="mla_prep_fwd", grid=(T // tm,),
        in_specs=[row(ODD_PAD), vec(QW), vec(KVW), row(LANES), row(LANES), row(LANES)],
        out_specs=[row(QW), row(KVW), row(LANES)],
        out_shape=[jax.ShapeDtypeStruct((T, QW), bf16), jax.ShapeDtypeStruct((T, KVW), bf16),
                   jax.ShapeDtypeStruct((T, LANES), bf16)],
        compiler_params=_cp("parallel"),
    )(proj, qg, kvg, *tabs)


def _mla_prep_bwd(proj, qg, kvg, tabs, dcqn, dckvn, dkr_heads, *, tm=512):
    T = proj.shape[0]
    tm = _tile(T, tm)

    def body(p_ref, qg_ref, kvg_ref, c_ref, sa_ref, sb_ref, dcq_ref, dckv_ref, dkr_ref, dp_ref, dqg_ref, dkvg_ref):
        dcq, dqg = _rms_bwd_math(p_ref[:, :QW], qg_ref[...], dcq_ref[...])
        dckv, dkvg = _rms_bwd_math(p_ref[:, QW:QW + KVW], kvg_ref[...], dckv_ref[...])
        dk = dkr_ref[:, :LANES]
        for h in range(1, MLA_HEADS):
            dk = dk + dkr_ref[:, h * LANES:(h + 1) * LANES]
        dkr = _rot_t(dk, c_ref[...], sa_ref[...], sb_ref[...])
        dp_ref[...] = jnp.concatenate([dcq, dckv, dkr], axis=1).astype(bf16)

        @pl.when(pl.program_id(0) == 0)
        def _():
            dqg_ref[...] = dqg
            dkvg_ref[...] = dkvg

        @pl.when(pl.program_id(0) > 0)
        def _():
            dqg_ref[...] += dqg
            dkvg_ref[...] += dkvg

    row = lambda w: pl.BlockSpec((tm, w), lambda i: (i, 0))
    vec = lambda w: pl.BlockSpec((1, w), lambda i: (0, 0))
    return pl.pallas_call(
        body, name="mla_prep_bwd", grid=(T // tm,),
        in_specs=[row(ODD_PAD), vec(QW), vec(KVW), row(LANES), row(LANES), row(LANES), row(QW), row(KVW),
                  row(MLA_HEADS * LANES)],
        out_specs=[row(ODD_PAD), vec(QW), vec(KVW)],
        out_shape=[jax.ShapeDtypeStruct((T, ODD_PAD), bf16), jax.ShapeDtypeStruct((1, QW), f32),
                   jax.ShapeDtypeStruct((1, KVW), f32)],
        compiler_params=_cp("arbitrary"),
    )(proj, qg, kvg, *tabs, dcqn, dckvn, dkr_heads)


HQ = 2 * LANES
QK_SCALE = MLA_QK ** -0.5


def _q_post(q, tabs, *, transpose, name, tm=512):
    T = q.shape[0]
    tm = _tile(T, tm)

    def body(q_ref, c_ref, sa_ref, sb_ref, o_ref):
        c, sa, sb = c_ref[...], sa_ref[...], sb_ref[...]
        for h in range(MLA_HEADS):
            nope, rope = pl.ds(h * HQ, LANES), pl.ds(h * HQ + LANES, LANES)
            o_ref[:, nope] = (q_ref[:, nope].astype(f32) * QK_SCALE).astype(bf16)
            o_ref[:, rope] = ((_rot_t if transpose else _rot)(q_ref[:, rope].astype(f32), c, sa, sb) * QK_SCALE).astype(bf16)

    tab = pl.BlockSpec((tm, LANES), lambda i: (i, 0))
    blk = pl.BlockSpec((tm, MLA_HEADS * HQ), lambda i: (i, 0))
    return pl.pallas_call(
        body, name=name, grid=(T // tm,), in_specs=[blk, tab, tab, tab], out_specs=blk,
        out_shape=jax.ShapeDtypeStruct(q.shape, bf16), compiler_params=_cp("parallel"),
    )(q, *tabs)


def _causal_mask(i, j, tq, tk):
    r = lax.broadcasted_iota(jnp.int32, (tq, tk), 0) + i * tq
    c = lax.broadcasted_iota(jnp.int32, (tq, tk), 1) + j * tk
    return c <= r


def _flash_fwd(q, kv, kr, *, tq=1024, tk=1024):
    T = q.shape[0]
    tq = _tile(T, tq)
    tk = _tile(tq, tk)
    per = tq // tk
    H = MLA_HEADS

    def body(q_ref, kn_ref, v_ref, kr_ref, o_ref, lse_ref, m_s, acc):
        i, j = pl.program_id(1), pl.program_id(2)
        last = (i + 1) * per - 1

        @pl.when(j == 0)
        def _():
            m_s[...] = jnp.full_like(m_s, -jnp.inf)
            acc[...] = jnp.zeros_like(acc)

        def step(masked):
            k = jnp.concatenate([kn_ref[...], kr_ref[...]], axis=1)
            s = _dot(q_ref[...], k, NT)
            if masked:
                s = jnp.where(_causal_mask(i, j, tq, tk), s, -jnp.inf)
            m_new = jnp.maximum(m_s[...], jnp.max(s, axis=-1, keepdims=True))
            alpha = jnp.exp(m_s[...] - m_new)
            p = jnp.exp((s - m_new).astype(bf16))
            v1 = jnp.concatenate([v_ref[...], jnp.ones((tk, LANES), bf16)], axis=1)
            acc[...] = alpha * acc[...] + _dot(p, v1)
            m_s[...] = m_new

        pl.when(j < i * per)(functools.partial(step, False))
        pl.when((j >= i * per) & (j <= last))(functools.partial(step, True))

        @pl.when(j == last)
        def _():
            l = acc[:, LANES:]
            o_ref[...] = (acc[:, :LANES] / l).astype(bf16)
            lse_ref[0] = m_s[...] + jnp.log(jnp.max(l, axis=-1, keepdims=True))

    kj = lambda i, j: jnp.minimum(j, (i + 1) * per - 1)
    kblk = lambda off: pl.BlockSpec((tk, LANES), lambda h, i, j: (kj(i, j), 2 * h + off))
    return pl.pallas_call(
        body, name="flash_fwd", grid=(H, T // tq, T // tk),
        in_specs=[pl.BlockSpec((tq, HQ), lambda h, i, j: (i, h)), kblk(0), kblk(1),
                  pl.BlockSpec((tk, LANES), lambda h, i, j: (kj(i, j), 0))],
        out_specs=[pl.BlockSpec((tq, LANES), lambda h, i, j: (i, h)), pl.BlockSpec((1, tq, 1), lambda h, i, j: (h, i, 0))],
        out_shape=[jax.ShapeDtypeStruct((T, H * LANES), bf16), jax.ShapeDtypeStruct((H, T, 1), f32)],
        scratch_shapes=[pltpu.VMEM((tq, 1), f32), pltpu.VMEM((tq, 2 * LANES), f32)],
        compiler_params=_cp("parallel", "parallel", "arbitrary"),
    )(q, kv, kv, kr)


def _flash_bwd(q, kv, kr, o, do, lse, *, tb=1024):
    T = q.shape[0]
    tb = _tile(T, tb)
    nb = T // tb
    H = MLA_HEADS

    def body(q_ref, kn_ref, v_ref, kr_ref, o_ref, do_ref, lse_ref, dkv_ref, dkr_ref, dq_ref, dk_acc, dv_acc):
        j, ii = pl.program_id(1), pl.program_id(2)
        i = jnp.maximum(ii, j)

        @pl.when((j == 0) & (ii == 0))
        def _():
            dq_ref[...] = jnp.zeros_like(dq_ref)

        @pl.when(ii == 0)
        def _():
            dk_acc[...] = jnp.zeros_like(dk_acc)
            dv_acc[...] = jnp.zeros_like(dv_acc)

        def step(masked):
            k = jnp.concatenate([kn_ref[...], kr_ref[...]], axis=1)
            p = jnp.exp((_dot(q_ref[...], k, NT) - lse_ref[0]).astype(bf16))
            if masked:
                p = jnp.where(_causal_mask(i, j, tb, tb), p, jnp.zeros_like(p))
            delta = jnp.sum(o_ref[...].astype(f32) * do_ref[...], axis=-1, keepdims=True)
            ds = p * (_bdot(do_ref[...], v_ref[...], NT) - delta).astype(bf16)
            dv_acc[...] += _bdot(p, do_ref[...], TN)
            dk_acc[...] += _bdot(ds, q_ref[...], TN)
            rows = pl.ds(pl.multiple_of(i * tb, tb), tb)
            dq_ref[rows, :] += _bdot(ds, k)

        pl.when(ii > j)(functools.partial(step, False))
        pl.when(ii == j)(functools.partial(step, True))

        @pl.when(ii == nb - 1)
        def _():
            dkv_ref[...] = jnp.concatenate([dk_acc[:, :LANES], dv_acc[...]], axis=1).astype(bf16)
            dkr_ref[...] = dk_acc[:, LANES:]

    qi = lambda h, j, i: jnp.maximum(i, j)
    kblk = lambda off: pl.BlockSpec((tb, LANES), lambda h, j, i: (j, 2 * h + off))
    vec = pl.BlockSpec((1, tb, 1), lambda h, j, i: (h, qi(h, j, i), 0))
    qblk = pl.BlockSpec((tb, LANES), lambda h, j, i: (qi(h, j, i), h))
    return pl.pallas_call(
        body, name="flash_bwd", grid=(H, nb, nb),
        in_specs=[pl.BlockSpec((tb, HQ), lambda h, j, i: (qi(h, j, i), h)), kblk(0), kblk(1),
                  pl.BlockSpec((tb, LANES), lambda h, j, i: (j, 0)), qblk, qblk, vec],
        out_specs=[pl.BlockSpec((tb, HQ), lambda h, j, i: (j, h)), pl.BlockSpec((tb, LANES), lambda h, j, i: (j, h)),
                   pl.BlockSpec((T, HQ), lambda h, j, i: (0, h))],
        out_shape=[jax.ShapeDtypeStruct((T, H * HQ), bf16), jax.ShapeDtypeStruct((T, H * LANES), f32),
                   jax.ShapeDtypeStruct((T, H * HQ), f32)],
        scratch_shapes=[pltpu.VMEM((tb, HQ), f32), pltpu.VMEM((tb, LANES), f32)],
        compiler_params=_cp("parallel", "arbitrary", "arbitrary"),
    )(q, kv, kv, kr, o, do, lse)


HBM_SPEC = pl.BlockSpec(memory_space=pltpu.HBM)
N_CHIPS = 4
N_DEV = 8

BIG = {"even_w_in": 1, "s5_w_glu": 0, "even_w_out": 0, "odd_w_in": 0, "mla_w_uq": 1, "mla_w_ukv": 1, "odd_w_out": 0,
       "ffn_w_in": 2, "ffn_w_out": 1}
LAYERED = ("ffn_w_in", "ffn_w_out")
GROUPS = {"even_in": ("even_w_in",), "even_rest": ("s5_w_glu", "even_w_out"), "ffn0": LAYERED,
          "odd": ("odd_w_in", "mla_w_uq", "mla_w_ukv", "odd_w_out"), "ffn1": LAYERED}
GROUP_LAYER = {"ffn0": 0, "ffn1": 1}


def _place():
    x, y, c = lax.axis_index("x"), lax.axis_index("y"), lax.axis_index("c")
    chips = [(1 - x, y), (x, 1 - y), (1 - x, 1 - y)]
    return x, y, c, chips


def _slab(ref, axis, k, size):
    start = pl.multiple_of(k * size, size if axis == 0 else LANES)
    idx = [slice(None)] * len(ref.shape)
    idx[axis] = pl.ds(start, size)
    return ref.at[tuple(idx)]


SEM_SPEC = pl.BlockSpec(memory_space=pltpu.SEMAPHORE)
ANY_SPEC = pl.BlockSpec(memory_space=pl.ANY)
EFFECT = pltpu.SideEffectType.DATAFLOW_SIDE_EFFECTING


def _hbm(a):
    return pltpu.with_memory_space_constraint(a, pltpu.HBM)


class _Gather:
    copies = 3

    def __init__(self, axis, size):
        self.axis, self.size = axis, size

    def view(self, land, kk):
        return _slab(land, self.axis, kk, self.size)

    def own(self, land, place):
        return self.view(land, 2 * place[0] + place[1])

    def sends(self, src, land, place):
        x, y, c, chips = place
        return [(self.own(land, place) if src is None else src, self.own(land, place), (*chip, c)) for chip in chips]

    def recvs(self, land, place):
        return [self.view(land, 2 * cx + cy) for cx, cy in place[3]]


class _Scatter:
    copies = 3

    def __init__(self, axis, size, layer=None):
        self.axis, self.size, self.layer = axis, size, layer

    def row(self, land, j):
        return land.at[j] if self.layer is None else land.at[j, self.layer]

    def sends(self, src, land, place):
        c, chips = place[2], place[3]
        return [(_slab(src, self.axis, 2 * cx + cy, self.size), self.row(land, j), (cx, cy, c))
                for j, (cx, cy) in enumerate(chips)]

    def recvs(self, land, place):
        return [self.row(land, j) for j in range(3)]


class _ToAll:
    copies = N_DEV - 1

    def __init__(self, size):
        self.size = size

    def sends(self, src, land, place):
        x, y, c, _ = place
        flip = lambda v, bit: 1 - v if bit else v
        own = _slab(land, 0, 4 * x + 2 * y + c, self.size)
        return [(own, own, (flip(x, m & 4), flip(y, m & 2), flip(c, m & 1))) for m in range(1, N_DEV)]

    def recvs(self, land, place):
        x, y, c, _ = place
        d = 4 * x + 2 * y + c
        return [_slab(land, 0, d ^ m, self.size) for m in range(1, N_DEV)]


def _unique(arrays):
    out, index = [], {}
    for a in arrays:
        if a is not None and id(a) not in index:
            index[id(a)] = len(out)
            out.append(a)
    return out, index


def _sem_base(routes):
    base = [0]
    for r in routes:
        base.append(base[-1] + r.copies)
    return base


def _push_start(name, items):
    n = len(items)
    base = _sem_base([it[0] for it in items])
    arrays, index = _unique([it[1] for it in items] + [it[2] for it in items])
    na = len(arrays)

    def body(*refs):
        arr, send, recv, token = refs[:na], refs[na], refs[na + 1], refs[-1]
        place = _place()
        for i, (route, src, land) in enumerate(items):
            s_ref = None if src is None else arr[index[id(src)]]
            for j, (s, d, dev) in enumerate(route.sends(s_ref, arr[index[id(land)]], place)):
                pltpu.make_async_remote_copy(src_ref=s, dst_ref=d, send_sem=send.at[base[i] + j], recv_sem=recv.at[base[i] + j],
                                             device_id=dev, device_id_type=MESH).start()
        token[...] = jnp.zeros_like(token)

    res = pl.pallas_call(
        body, name=name,
        out_shape=[pltpu.SemaphoreType.DMA((base[-1],)), pltpu.SemaphoreType.DMA((base[-1],))]
        + [pltpu.HBM(a.shape, a.dtype) for a in arrays] + [jax.ShapeDtypeStruct((SUBLANES, LANES), f32)],
        in_specs=[HBM_SPEC] * na, out_specs=[SEM_SPEC, SEM_SPEC] + [HBM_SPEC] * na + [pl.BlockSpec(memory_space=pltpu.VMEM)],
        input_output_aliases={i: 2 + i for i in range(na)},
        compiler_params=pltpu.CompilerParams(has_side_effects=EFFECT),
    )(*[_hbm(a) for a in arrays])
    thru = lambda a: None if a is None else res[2 + index[id(a)]]
    return (res[0], res[1]), [thru(it[1]) for it in items], [thru(it[2]) for it in items], res[-1]


def _push_wait(name, groups, after):
    arrays, index = _unique([a for _, _, srcs, lands in groups for a in list(srcs) + list(lands)])
    na, ng = len(arrays), len(groups)

    def body(*refs):
        arr, sems = refs[:na], refs[na:na + 2 * ng]
        place = _place()
        for g, (routes, _, srcs, lands) in enumerate(groups):
            send, recv = sems[2 * g], sems[2 * g + 1]
            base = _sem_base(routes)
            for i, route in enumerate(routes):
                src, land = None if srcs[i] is None else arr[index[id(srcs[i])]], arr[index[id(lands[i])]]
                for j, ((s, d, dev), mine) in enumerate(zip(route.sends(src, land, place), route.recvs(land, place))):
                    cp = pltpu.make_async_remote_copy(src_ref=s, dst_ref=mine, send_sem=send.at[base[i] + j],
                                                      recv_sem=recv.at[base[i] + j], device_id=dev,
                                                      device_id_type=MESH)
                    cp.wait_send()
                    cp.wait_recv()

    sem_args = [s for g in groups for s in g[1]]
    res = pl.pallas_call(
        body, name=name, out_shape=[pltpu.HBM(a.shape, a.dtype) for a in arrays],
        in_specs=[HBM_SPEC] * na + [SEM_SPEC] * (2 * ng) + [ANY_SPEC] * len(after), out_specs=[HBM_SPEC] * na,
        input_output_aliases={i: i for i in range(na)},
        compiler_params=pltpu.CompilerParams(has_side_effects=EFFECT),
    )(*arrays, *sem_args, *after)
    return [[res[index[id(a)]] for a in g[3]] for g in groups]


def _place_slab(block, axis, slabs, idx, dtype, *, name):
    R, C = block.shape
    tm = _rows(R, C)
    nr = R // tm
    out_map = (lambda i, k: (i, k[0])) if axis == 1 else (lambda i, k: (k[0] * nr + i, 0))

    def body(k_ref, x_ref, o_ref):
        o_ref[...] = x_ref[...].astype(dtype)

    full = (R, C * slabs) if axis == 1 else (R * slabs, C)
    return pl.pallas_call(
        body, name=name, out_shape=jax.ShapeDtypeStruct(full, dtype),
        grid_spec=pltpu.PrefetchScalarGridSpec(
            num_scalar_prefetch=1, grid=(nr,), in_specs=[pl.BlockSpec((tm, C), lambda i, k: (i, 0))],
            out_specs=pl.BlockSpec((tm, C), out_map)),
        compiler_params=_cp("parallel"),
    )(idx, block)


def _swap_with_sibling(parts, tag):
    names = list(parts)

    def body(*refs):
        n = len(names)
        ins, outs, send, recv = refs[:n], refs[n:2 * n], refs[-2], refs[-1]
        x, y, c, _ = _place()
        cps = [pltpu.make_async_remote_copy(src_ref=ins[a], dst_ref=outs[a], send_sem=send.at[a], recv_sem=recv.at[a],
                                            device_id=(x, y, 1 - c), device_id_type=MESH) for a in range(n)]
        for cp in cps:
            cp.start()
        for cp in cps:
            cp.wait_recv()
        for cp in cps:
            cp.wait_send()

    res = pl.pallas_call(
        body, name=f"swap_with_sibling_{tag}", in_specs=[HBM_SPEC] * len(names), out_specs=[HBM_SPEC] * len(names),
        out_shape=[jax.ShapeDtypeStruct(parts[n].shape, parts[n].dtype) for n in names],
        scratch_shapes=[pltpu.SemaphoreType.DMA((len(names),)), pltpu.SemaphoreType.DMA((len(names),))],
    )(*[parts[n] for n in names])
    return dict(zip(names, res))


ELEMENTWISE_BLOCK_BYTES = 1 << 20


def _rows(r, c):
    for t in (512, 256, 128, 64, 32, 16, 8):
        if r % t == 0 and t * c * 4 <= ELEMENTWISE_BLOCK_BYTES:
            return t
    return r


def _sum4(owns, axis, recv, kidx, *, name):
    L = len(owns)
    R, C = recv.shape[2:]
    tm = _rows(R, C)
    nr = R // tm

    def body(k_ref, *refs):
        own_refs, r_ref, out_ref = refs[:L], refs[L], refs[L + 1]
        for li in range(L):
            @pl.when(pl.program_id(0) == li)
            def _(o_ref=own_refs[li]):
                out_ref[...] = ((o_ref[...] + r_ref[0, 0].astype(f32)) + r_ref[1, 0].astype(f32)) + r_ref[2, 0].astype(f32)

    own_map = (lambda l, i, k: (i, k[0])) if axis == 1 else (lambda l, i, k: (k[0] * nr + i, 0))
    return pl.pallas_call(
        body, name=name, out_shape=jax.ShapeDtypeStruct((L * R, C), f32),
        grid_spec=pltpu.PrefetchScalarGridSpec(
            num_scalar_prefetch=1, grid=(L, nr),
            in_specs=[pl.BlockSpec((tm, C), own_map)] * L + [pl.BlockSpec((3, 1, tm, C), lambda l, i, k: (0, l, i, 0))],
            out_specs=pl.BlockSpec((tm, C), lambda l, i, k: (l * nr + i, 0))),
        compiler_params=_cp("parallel", "parallel"),
    )(kidx, *owns, recv)


def _adamw(w, m, v, parts, *, name):
    R, C = w.shape
    tm = _rows(R, C)
    npart = len(parts)

    def body(*refs):
        w_ref, m_ref, v_ref = refs[:3]
        g_ref, d_ref, m2_ref, v2_ref = refs[3 + npart:]
        g = refs[3][...]
        for p_ref in refs[4:3 + npart]:
            g = g + p_ref[...]
        g_ref[...] = g
        d_ref[...], m2_ref[...], v2_ref[...] = _adam_math(w_ref[...], m_ref[...], v_ref[...], g)

    blk = pl.BlockSpec((tm, C), lambda i: (i, 0))
    return pl.pallas_call(
        body, name=name, grid=(R // tm,),
        in_specs=[blk] * (3 + npart), out_specs=[blk] * 4,
        out_shape=[jax.ShapeDtypeStruct((R, C), f32)] * 4, compiler_params=_cp("parallel"),
    )(w, m, v, *parts)


def _adam_math(w, m, v, g):
    m2 = ADAM_B1 * m + (1.0 - ADAM_B1) * g
    v2 = ADAM_B2 * v + (1.0 - ADAM_B2) * (g * g)
    m_hat = m2 / (1.0 - ADAM_B1 ** ADAM_STEP)
    v_hat = v2 / (1.0 - ADAM_B2 ** ADAM_STEP)
    return -ADAM_LR * (m_hat / (jnp.sqrt(v_hat) + ADAM_EPS) + ADAM_WD * w), m2, v2


def _adamw_small(landed, w, m, v, kidx, ra, rb):
    rs = ra + N_CHIPS * rb

    def body(k_ref, l_ref, w_ref, m_ref, v_ref, g_ref, d_ref, m2_ref, v2_ref):
        mine = pl.multiple_of(ra + k_ref[0] * rb, SUBLANES)
        for lo, n, off in ((0, ra, 0), (ra, rb, mine)):
            g = l_ref[pl.ds(off, n), :]
            for d in range(1, N_DEV):
                g = g + l_ref[pl.ds(d * rs + off, n), :]
            rows = pl.ds(lo, n)
            delta, m2, v2 = _adam_math(w_ref[rows, :], m_ref[rows, :], v_ref[rows, :], g)
            g_ref[rows, :] = g
            d_ref[rows, :] = delta
            m2_ref[rows, :] = m2
            v2_ref[rows, :] = v2

    vmem = pl.BlockSpec(memory_space=pltpu.VMEM)
    return pl.pallas_call(
        body, name="adamw_small", out_shape=[jax.ShapeDtypeStruct(w.shape, f32)] * 4,
        grid_spec=pltpu.PrefetchScalarGridSpec(num_scalar_prefetch=1, grid=(), in_specs=[vmem] * 4, out_specs=[vmem] * 4),
        compiler_params=_cp(),
    )(kidx, landed, w, m, v)


def _pad_odd(w):
    return jnp.pad(w, ((0, 0), (0, ODD_PAD - w.shape[1])))


def _uq_cat(w):
    r = w.shape[0]
    return jnp.pad(w.reshape(r, MLA_HEADS, MLA_QK), ((0, 0), (0, 0), (0, HQ - MLA_QK))).reshape(r, MLA_HEADS * HQ)


def _uq_uncat(w):
    r = w.shape[0]
    return w.reshape(r, MLA_HEADS, HQ)[:, :, :MLA_QK].reshape(r, MLA_HEADS * MLA_QK)


def _to_segments(v):
    T, C = v.shape
    return v.reshape(S5_SEG, T // S5_SEG, C).transpose(1, 0, 2).reshape(T, C)


def _from_segments(v):
    T, C = v.shape
    return v.reshape(T // S5_SEG, S5_SEG, C).transpose(1, 0, 2).reshape(T, C)


def _s5_rb(T):
    return min(512, T)


def _ffn_fwd(h, hn, w_in, cw, cb, w_out, tag, next_g=None):
    au = _mm(hn, w_in, out_dtype=bf16, name=f"ffn{tag}_in", tn=1408)
    z = _ffn_mid_fwd(au, cw, cb, name=f"ffn{tag}_mid")
    return _mm(z, w_out, res=h, norm_g=next_g, name=f"ffn{tag}_out", tk=1408), (hn, au, z)


def _ffn_bwd(h, g, w_in, cw, cb, w_out, saved, dh, tag, dep=None):
    hn, au, z = saved
    dz = _mm(dh, w_out, tb=True, out_dtype=bf16, name=f"ffn{tag}_dz", tn=1408, dep=dep)
    dw_out = _mm(z, dh, ta=True, also_bf16=True, name=f"ffn{tag}_dwout", tm=1408)
    dau, dcw, dcb = _ffn_mid_bwd(au, cw, cb, dz, name=f"ffn{tag}_dmid")
    dh_in, dg = _mm(dau, w_in, tb=True, res=dh, norm_bwd=(h, g), name=f"ffn{tag}_dhn", tk=1408)
    dw_in = _mm(hn, dau, ta=True, also_bf16=True, name=f"ffn{tag}_dwin", tn=1408)
    return dh_in, dg, dw_in, dcw, dcb, dw_out


def _local_step(x, positions, target, get_w, P, put_g):
    T = x.shape[0]
    rb = _s5_rb(T)
    row = lambda v: v.reshape(1, -1)
    g_mix, g_ffn = P["norm_mix_g"], P["norm_ffn_g"]
    lbl, hng = P["hgrn_lb_logits"], P["hgrn_norm_g"]
    dsk, bg = P["s5_d"], P["s5_b_glu"]
    qg, kvg = P["mla_q_norm_g"], P["mla_kv_norm_g"]
    cw, cb = P["ffn_conv_w"], P["ffn_conv_b"]

    col = lambda v: v.reshape(S5_N, 1)
    disc_in = (col(P["s5_a_re"]), col(P["s5_a_im"]), col(jnp.repeat(P["s5_log_dt"].reshape(S5_GROUPS), S5_STATE)),
               P["s5_b_re"].reshape(S5_N, S5_GROUP), P["s5_b_im"].reshape(S5_N, S5_GROUP))
    abr, abi, bbr, bbi = _s5_disc_fwd(*disc_in)
    ar, ai = abr.reshape(1, S5_N), abi.reshape(1, S5_N)
    bbr3, bbi3 = bbr.reshape(S5_GROUPS, S5_STATE, S5_GROUP), bbi.reshape(S5_GROUPS, S5_STATE, S5_GROUP)
    bre, bim = _blockdiag(bbr3, True).astype(bf16), _blockdiag(bbi3, True).astype(bf16)
    bret, bimt = _blockdiag(bbr3).astype(bf16), _blockdiag(bbi3).astype(bf16)
    c_re, c_im = P["s5_c_re"].reshape(S5_GROUPS, S5_GROUP, S5_STATE), P["s5_c_im"].reshape(S5_GROUPS, S5_GROUP, S5_STATE)
    cre, cim = _blockdiag(c_re, True).astype(bf16), _blockdiag(c_im, True).astype(bf16)
    cret, cimt = _blockdiag(c_re).astype(bf16), _blockdiag(c_im).astype(bf16)

    hn0 = _rms_fwd(x, g_mix[0:1], name="mix0_norm")
    We = get_w("even_in", hn0)
    proj_e = _mm(hn0, We["even_w_in"], name="even_in", tn=1280)
    Wr = get_w("even_rest", proj_e)
    ya, states = _hgrn_fwd(proj_e, lbl, hng)
    u_seg = _to_segments(proj_e[:, 4 * 512:])
    fr, fi = _s5_final(u_seg, bre, bim, ar, ai, rb=rb)
    yb_seg, s0r, s0i = _s5_fwd(u_seg, bre, bim, ar, ai, fr, fi, cre, cim, dsk, Wr["s5_w_glu"], bg, rb=rb)
    ycat = jnp.concatenate([ya, _from_segments(yb_seg)], axis=1)
    h1, hnf0 = _mm(ycat, Wr["even_w_out"], res=x, norm_g=g_ffn[0:1], name="even_out")
    Wf0 = get_w("ffn0", h1)
    (h2, hn2), ffn0 = _ffn_fwd(h1, hnf0, Wf0["ffn_w_in"], cw[0], cb[0:1], Wf0["ffn_w_out"], 0, next_g=g_mix[1:2])

    tabs = _rope_tables(positions)
    Wo = get_w("odd", hn2)
    proj_o = _mm(hn2, Wo["odd_w_in"], name="odd_in")
    cqn, ckvn, kr = _mla_prep_fwd(proj_o, qg, kvg, tabs)
    q = _q_post(_mm(cqn, Wo["mla_w_uq"], name="mla_uq"), tabs, transpose=False, name="q_post")
    kvb = _mm(ckvn, Wo["mla_w_ukv"], out_dtype=bf16, name="mla_ukv")
    o, lse = _flash_fwd(q, kvb, kr)
    h3, hnf1 = _mm(o, Wo["odd_w_out"], res=h2, norm_g=g_ffn[1:2], name="odd_out")
    Wf1 = get_w("ffn1", h3)
    h4, ffn1 = _ffn_fwd(h3, hnf1, Wf1["ffn_w_in"], cw[1], cb[1:2], Wf1["ffn_w_out"], 1)
    loss, dh4, dg_final = _loss_head(h4, row(P["final_norm_g"]), target)

    dh3, dg_ffn1, dw_fin1, dcw1, dcb1, dw_fout1 = _ffn_bwd(
        h3, g_ffn[1:2], Wf1["ffn_w_in"], cw[1], cb[1:2], Wf1["ffn_w_out"], ffn1, dh4, 1)
    sent = put_g("ffn1", {"ffn_w_in": dw_fin1, "ffn_w_out": dw_fout1})
    do = _mm(dh3, Wo["odd_w_out"], tb=True, out_dtype=bf16, name="odd_do", dep=sent)
    dw_oout = _mm(o, dh3, ta=True, also_bf16=True, name="odd_dwout")
    dkv, dkr_h, dq = _flash_bwd(q, kvb, kr, o, do, lse)
    dq = _q_post(dq, tabs, transpose=True, name="dq_post")
    dw_uq = _mm(cqn, dq, ta=True, also_bf16=True, name="mla_dwuq")
    dcqn = _mm(dq, Wo["mla_w_uq"], tb=True, name="mla_dcq")
    dw_ukv = _mm(ckvn, dkv, ta=True, also_bf16=True, name="mla_dwukv")
    dckvn = _mm(dkv, Wo["mla_w_ukv"], tb=True, name="mla_dckv")
    dproj_o, dqg, dkvg = _mla_prep_bwd(proj_o, qg, kvg, tabs, dcqn, dckvn, dkr_h)
    dw_oin = _mm(hn2, dproj_o, ta=True, also_bf16=True, name="odd_dwin")
    sent = put_g("odd", {"odd_w_in": dw_oin, "mla_w_uq": dw_uq, "mla_w_ukv": dw_ukv, "odd_w_out": dw_oout})
    dh2, dg_mix1 = _mm(dproj_o, Wo["odd_w_in"], tb=True, res=dh3, norm_bwd=(h2, g_mix[1:2]), name="odd_dhn")

    dh1, dg_ffn0, dw_fin0, dcw0, dcb0, dw_fout0 = _ffn_bwd(
        h1, g_ffn[0:1], Wf0["ffn_w_in"], cw[0], cb[0:1], Wf0["ffn_w_out"], ffn0, dh2, 0, dep=sent)
    sent = put_g("ffn0", {"ffn_w_in": dw_fin0, "ffn_w_out": dw_fout0})
    dycat = _mm(dh1, Wr["even_w_out"], tb=True, name="even_dy", dep=sent)
    dw_eout = _mm(ycat, dh1, ta=True, also_bf16=True, name="even_dwout")
    dq_h, df_h, di_h, dg_h, dlbl, dhng = _hgrn_bwd(proj_e, lbl, hng, states, dycat)
    dyb_seg = _to_segments(dycat[:, 512:])
    dy_s5, glr, gli, dcre, dcim, dd, dwg, dbg = _s5_bwd_a(
        u_seg, bre, bim, ar, ai, s0r, s0i, cre, cim, cret, cimt, dsk, Wr["s5_w_glu"], bg, dyb_seg, rb=rb)
    du_seg, dbre, dbim, dar, dai = _s5_bwd_b(
        u_seg, bre, bim, bret, bimt, ar, ai, s0r, s0i, glr, gli, cret, cimt, dsk, dy_s5, rb=rb)
    dproj_e = jnp.concatenate([dq_h, df_h, di_h, dg_h, _from_segments(du_seg)], axis=1)
    dx, dg_mix0 = _mm(dproj_e, We["even_w_in"], tb=True, res=dh1, norm_bwd=(x, g_mix[0:1]), name="even_dhn", tk=1280)
    dw_ein = _mm(hn0, dproj_e, ta=True, also_bf16=True, name="even_dwin", tn=1280)

    unblk = lambda m, a, b: jnp.swapaxes(_blockdiag_t(m, a, b), 1, 2)
    dbbr = unblk(dbre, S5_GROUP, S5_STATE).reshape(S5_N, S5_GROUP)
    dbbi = unblk(dbim, S5_GROUP, S5_STATE).reshape(S5_N, S5_GROUP)
    d_ar, d_ai, d_ldt, d_br, d_bi = _s5_disc_bwd(*disc_in, (dar.reshape(S5_N, 1), dai.reshape(S5_N, 1), dbbr, dbbi))
    small = {
        "norm_mix_g": jnp.concatenate([dg_mix0, dg_mix1], axis=0),
        "norm_ffn_g": jnp.concatenate([dg_ffn0, dg_ffn1], axis=0),
        "final_norm_g": dg_final.reshape(-1),
        "hgrn_lb_logits": dlbl, "hgrn_norm_g": dhng,
        "s5_a_re": d_ar.reshape(1, S5_GROUPS, S5_STATE), "s5_a_im": d_ai.reshape(1, S5_GROUPS, S5_STATE),
        "s5_log_dt": d_ldt.reshape(S5_GROUPS, S5_STATE).sum(axis=1).reshape(1, S5_GROUPS),
        "s5_b_re": d_br.reshape(1, S5_GROUPS, S5_STATE, S5_GROUP), "s5_b_im": d_bi.reshape(1, S5_GROUPS, S5_STATE, S5_GROUP),
        "s5_c_re": unblk(dcre, S5_STATE, S5_GROUP).reshape(1, S5_GROUPS, S5_GROUP, S5_STATE),
        "s5_c_im": unblk(dcim, S5_STATE, S5_GROUP).reshape(1, S5_GROUPS, S5_GROUP, S5_STATE),
        "s5_d": dd, "s5_b_glu": dbg, "mla_q_norm_g": dqg, "mla_kv_norm_g": dkvg,
        "ffn_conv_w": jnp.stack([dcw0, dcw1]), "ffn_conv_b": jnp.concatenate([dcb0, dcb1], axis=0),
    }
    put_g("even", {"even_w_in": dw_ein, "s5_w_glu": (dwg, dwg.astype(bf16)), "even_w_out": dw_eout}, small)
    return loss, dx


WEIGHTS = ["norm_mix_g", "norm_ffn_g", "final_norm_g", "even_w_in", "hgrn_lb_logits", "hgrn_norm_g", "s5_a_re", "s5_a_im",
           "s5_log_dt", "s5_b_re", "s5_b_im", "s5_c_re", "s5_c_im", "s5_d", "s5_w_glu", "s5_b_glu", "even_w_out", "odd_w_in",
           "mla_q_norm_g", "mla_w_uq", "mla_kv_norm_g", "mla_w_ukv", "odd_w_out", "ffn_w_in", "ffn_conv_w", "ffn_conv_b",
           "ffn_w_out"]
SMALL_SHARDED = {"mla_q_norm_g": 1, "mla_kv_norm_g": 1, "ffn_conv_w": 2}
SMALL = [n for n in WEIGHTS if n not in BIG]
SMALL_REP = [n for n in SMALL if n not in SMALL_SHARDED]


def _pack_rows(shapes):
    n = sum(math.prod(s) for s in shapes)
    return -(-n // (SUBLANES * LANES)) * SUBLANES


def _pack(arrays, rows):
    flat = jnp.concatenate([a.reshape(-1) for a in arrays])
    return jnp.pad(flat, (0, rows * LANES - flat.shape[0])).reshape(rows, LANES)


def _unpack(block, shapes):
    flat, out, off = block.reshape(-1), [], 0
    for s in shapes:
        n = math.prod(s)
        out.append(flat[off:off + n].reshape(s))
        off += n
    return out


def kernel(x, positions, norm_mix_g, norm_ffn_g, final_norm_g, even_w_in, hgrn_lb_logits, hgrn_norm_g, s5_a_re, s5_a_im, s5_log_dt, s5_b_re, s5_b_im, s5_c_re, s5_c_im, s5_d, s5_w_glu, s5_b_glu, even_w_out, odd_w_in, mla_q_norm_g, mla_w_uq, mla_kv_norm_g, mla_w_ukv, odd_w_out, ffn_w_in, ffn_conv_w, ffn_conv_b, ffn_w_out, loss_target, m_norm_mix_g, m_norm_ffn_g, m_final_norm_g, m_even_w_in, m_hgrn_lb_logits, m_hgrn_norm_g, m_s5_a_re, m_s5_a_im, m_s5_log_dt, m_s5_b_re, m_s5_b_im, m_s5_c_re, m_s5_c_im, m_s5_d, m_s5_w_glu, m_s5_b_glu, m_even_w_out, m_odd_w_in, m_mla_q_norm_g, m_mla_w_uq, m_mla_kv_norm_g, m_mla_w_ukv, m_odd_w_out, m_ffn_w_in, m_ffn_conv_w, m_ffn_conv_b, m_ffn_w_out, v_norm_mix_g, v_norm_ffn_g, v_final_norm_g, v_even_w_in, v_hgrn_lb_logits, v_hgrn_norm_g, v_s5_a_re, v_s5_a_im, v_s5_log_dt, v_s5_b_re, v_s5_b_im, v_s5_c_re, v_s5_c_im, v_s5_d, v_s5_w_glu, v_s5_b_glu, v_even_w_out, v_odd_w_in, v_mla_q_norm_g, v_mla_w_uq, v_mla_kv_norm_g, v_mla_w_ukv, v_odd_w_out, v_ffn_w_in, v_ffn_conv_w, v_ffn_conv_b, v_ffn_w_out):
    args = dict(locals())
    w = {n: args[n] for n in WEIGHTS}
    m = {n: args["m_" + n] for n in WEIGHTS}
    v = {n: args["v_" + n] for n in WEIGHTS}
    k = 2 * lax.axis_index("x") + lax.axis_index("y")
    kidx = k.reshape(1).astype(jnp.int32)
    axis2d = lambda n: BIG[n] - (1 if n in LAYERED else 0)
    slab = lambda n: w[n].shape[1 + axis2d(n)]

    small_sh_shapes = [w[n].shape for n in SMALL_SHARDED]
    rb = _pack_rows(small_sh_shapes)
    items = {}
    for group, names in GROUPS.items():
        layer = GROUP_LAYER.get(group, 0)
        items[group] = [(_Gather(axis2d(n), slab(n)), None,
                         _place_slab(w[n][layer], axis2d(n), N_CHIPS, kidx, bf16, name=f"place_{n}_{layer}")) for n in names]
    items["even_in"].append((_Gather(0, rb), None,
                             _place_slab(_pack([w[n] for n in SMALL_SHARDED], rb), 0, N_CHIPS, kidx, f32, name="place_small")))
    gathers, tokens = {}, []
    for group in GROUPS:
        sems, srcs, lands, token = _push_start(f"gather_start_{group}", items[group])
        gathers[group] = ([it[0] for it in items[group]], sems, srcs, lands)
        tokens.append(token[0, 0])
    started = functools.reduce(jnp.add, tokens)

    def landed(group, after):
        return _push_wait(f"gather_wait_{group}", [gathers[group]], [after])[0]

    even = landed("even_in", (started + norm_mix_g[0, 0]).reshape(1))
    per_chip = [_unpack(even[-1][c * rb:(c + 1) * rb], small_sh_shapes) for c in range(N_CHIPS)]
    P = {n: w[n] for n in SMALL_REP}
    for i, (n, ax) in enumerate(SMALL_SHARDED.items()):
        P[n] = jnp.concatenate([per_chip[c][i] for c in range(N_CHIPS)], axis=ax)
    P["mla_q_norm_g"], P["mla_kv_norm_g"] = P["mla_q_norm_g"].reshape(1, -1), P["mla_kv_norm_g"].reshape(1, -1)
    fix_w = {"odd_w_in": _pad_odd, "mla_w_uq": _uq_cat}

    def get_w(group, after):
        full = even if group == "even_in" else landed(group, after)
        return {n: fix_w.get(n, lambda a: a)(a) for n, a in zip(GROUPS[group], full)}

    fix_g = {"odd_w_in": lambda g: g[:, :odd_w_in.shape[2]], "mla_w_uq": _uq_uncat}
    g32, scatters, land_now = {}, {}, {}
    ra = _pack_rows([w[n].shape for n in SMALL_REP])
    rs = ra + N_CHIPS * rb
    didx = (2 * kidx + lax.axis_index("c")).astype(jnp.int32)

    def put_g(group, grads, small=None):
        layer = GROUP_LAYER.get(group)
        routes, srcs, names = [], [], list(grads)
        for n in names:
            f = fix_g.get(n, lambda g: g)
            g32.setdefault(n, {})[layer or 0] = f(grads[n][0])
            routes.append(_Scatter(axis2d(n), slab(n), layer if n in LAYERED else None))
            srcs.append(f(grads[n][1]))
            if n not in land_now:
                land_now[n] = lax.empty((3,) + w[n].shape[0 if n in LAYERED else 1:], bf16)
        if small is not None:
            blocks = [_pack([small[n] for n in SMALL_REP], ra)]
            for chip in range(N_CHIPS):
                sl = lambda n, ax: lax.slice_in_dim(small[n].reshape(w[n].shape[:ax] + (-1,) + w[n].shape[ax + 1:]),
                                                    chip * w[n].shape[ax], (chip + 1) * w[n].shape[ax], axis=ax)
                blocks.append(_pack([sl(n, ax) for n, ax in SMALL_SHARDED.items()], rb))
            names.append("small")
            routes.append(_ToAll(rs))
            srcs.append(None)
            land_now["small"] = _place_slab(jnp.concatenate(blocks), 0, N_DEV, didx, f32, name="place_small_grads")
        sems, srcs, lands, token = _push_start(f"scatter_start_{group}", [(r, s, land_now[n]) for r, s, n in zip(routes, srcs, names)])
        land_now.update(zip(names, lands))
        scatters[group] = (routes, sems, srcs, names)
        sent.append(token)
        return token

    sent = []
    loss, dx = _local_step(x[0], positions[0], loss_target[0], get_w, P, put_g)
    sent_last = sent[-1]
    loss = lax.psum(loss[0, 0], ("x", "y", "c"))

    out = {}

    def finish(tag, groups, after):
        waits = [(scatters[g][0], scatters[g][1], scatters[g][2], [land_now[n] for n in scatters[g][3]]) for g in groups]
        for g, lands in zip(groups, _push_wait(f"scatter_wait_{tag}", waits, after)):
            land_now.update(zip(scatters[g][3], lands))
        names = [n for n in dict.fromkeys(n for g in groups for n in scatters[g][3]) if n != "small"]
        part = {}
        for n in names:
            recv = land_now[n] if n in LAYERED else land_now[n][:, None]
            part[n] = _sum4([g32[n][l] for l in sorted(g32[n])], axis2d(n), recv, kidx, name=f"sum4_{n}")
        other = _swap_with_sibling(part, tag)
        done = []
        for n in names:
            C = part[n].shape[-1]
            res = _adamw(w[n].reshape(-1, C), m[n].reshape(-1, C), v[n].reshape(-1, C), [part[n], other[n]], name=f"adamw_{n}")
            out[n] = [r.reshape(w[n].shape) for r in res]
            done.append(res[0])
        return done

    done = finish("a", ["ffn1", "odd", "ffn0"], [dx, sent_last])
    finish("b", ["even"], done)

    order = SMALL_REP + list(SMALL_SHARDED)
    packed = lambda src: jnp.concatenate([_pack([src[n] for n in SMALL_REP], ra), _pack([src[n] for n in SMALL_SHARDED], rb)])
    res = _adamw_small(land_now["small"], packed(w), packed(m), packed(v), kidx, ra, rb)
    for r in res:
        parts = _unpack(r[:ra], [w[n].shape for n in SMALL_REP]) + _unpack(r[ra:], small_sh_shapes)
        for n, a in zip(order, parts):
            out.setdefault(n, []).append(a)

    return (loss, dx[None], *[out[n][0] for n in WEIGHTS], *[out[n][1] for n in WEIGHTS],
            *[out[n][2] for n in WEIGHTS], *[out[n][3] for n in WEIGHTS])
```

```python
import functools
import math

import jax
import jax.numpy as jnp
from jax import lax
from jax.experimental import pallas as pl
from jax.experimental.pallas import tpu as pltpu

f32, bf16 = jnp.float32, jnp.bfloat16
EPS = 1e-6
LANES = 128
SUBLANES = 8
VMEM_BYTES = 48 * 1024 * 1024
HGRN_CHUNK = 64
HGRN_HEADS = 4
S5_GROUPS, S5_STATE, S5_GROUP = 32, 64, 16
S5_N = S5_GROUPS * S5_STATE
S5_SEG = SUBLANES
MLA_HEADS, MLA_NOPE, MLA_ROPE, MLA_V = 8, 128, 64, 128
MLA_QK = MLA_NOPE + MLA_ROPE
MLA_Q_RANK, MLA_KV_RANK = 384, 256
ROPE_THETA = 10000.0
D_FF = 2816
ADAM_LR, ADAM_B1, ADAM_B2, ADAM_EPS, ADAM_WD, ADAM_STEP = 0.001, 0.9, 0.999, 1e-08, 0.01, 10
MESH = pl.DeviceIdType.MESH
HI = lax.Precision.HIGHEST


def _cp(*dims):
    return pltpu.CompilerParams(dimension_semantics=dims if dims else None, vmem_limit_bytes=VMEM_BYTES)


def _tile(n, t):
    if n <= t:
        return n
    c = (t // LANES) * LANES
    while c >= LANES:
        if n % c == 0:
            return c
        c -= LANES
    return n


def _dot(a, b, dn=None, precision=None):
    if dn is None:
        dn = (((a.ndim - 1,), (0,)), ((), ()))
    return lax.dot_general(a, b, dn, preferred_element_type=f32, precision=precision)


NT = (((1,), (1,)), ((), ()))
TN = (((0,), (0,)), ((), ()))


def _bdot(a, b, dn=None):
    return _dot(a.astype(bf16), b.astype(bf16), dn)


def _mm(a, b, *, name, ta=False, tb=False, out_dtype=f32, res=None, also_bf16=False, tm=1024, tn=1024, tk=1024, dep=None,
        norm_g=None, norm_bwd=None):
    halves = lambda s: (s[1], 2 * s[2]) if len(s) == 3 else s
    M, K = (a.shape[1], a.shape[0]) if ta else halves(a.shape)
    N = b.shape[0] if tb else halves(b.shape)[1]
    rows = norm_g is not None or norm_bwd is not None
    if rows:
        tm, tn = min(tm, 512), N
    tm, tn, tk = _tile(M, tm), _tile(N, tn), _tile(K, tk)
    if a.ndim == 3:
        tk = _tile(K // 2, tk)
    if b.ndim == 3:
        tn = _tile(N // 2, tn)
    nk = K // tk
    dn = (((0 if ta else 1,), (1 if tb else 0,)), ((), ()))
    extra = [] if norm_bwd is None else list(norm_bwd)
    if norm_g is not None:
        extra.append(norm_g)

    def body(*refs):
        a_ref, b_ref = refs[0], refs[1]
        r_ref = refs[2] if res is not None else None
        nin = 2 + (res is not None) + (dep is not None) + len(extra)
        ex = refs[nin - len(extra):nin]
        outs = refs[nin:-1] if nk > 1 else refs[nin:]
        acc = refs[-1] if nk > 1 else None
        k = pl.program_id(2)
        p = _bdot(a_ref[...], b_ref[...], dn)

        if nk > 1:
            @pl.when(k == 0)
            def _():
                acc[...] = p

            @pl.when((k > 0) & (k < nk - 1))
            def _():
                acc[...] += p

        @pl.when(k == nk - 1)
        def _():
            r = acc[...] + p if nk > 1 else p
            if norm_bwd is not None:
                r, dg = _rms_bwd_math(ex[0][...], ex[1][...], r)

                @pl.when(pl.program_id(0) == 0)
                def _():
                    outs[1][...] = dg

                @pl.when(pl.program_id(0) > 0)
                def _():
                    outs[1][...] += dg

            if r_ref is not None:
                r = r + r_ref[...]
            outs[0][...] = r.astype(out_dtype)
            if also_bf16:
                outs[1][...] = r.astype(bf16)
            if norm_g is not None:
                outs[1][...] = _rms(r, ex[-1][...]).astype(bf16)

    a_spec = pl.BlockSpec((tk, tm), lambda i, j, k: (k, i)) if ta else pl.BlockSpec((tm, tk), lambda i, j, k: (i, k))
    b_spec = pl.BlockSpec((tn, tk), lambda i, j, k: (j, k)) if tb else pl.BlockSpec((tk, tn), lambda i, j, k: (k, j))
    if a.ndim == 3:
        kh = K // 2 // tk
        a_spec = pl.BlockSpec((None, tm, tk), lambda i, j, k: (k // kh, i, k % kh))
    if b.ndim == 3:
        nh = N // 2 // tn
        b_spec = pl.BlockSpec((None, tk, tn), lambda i, j, k: (j // nh, k, j % nh))
    o_spec = pl.BlockSpec((tm, tn), lambda i, j, k: (i, j))
    in_specs, args = [a_spec, b_spec], [a, b]
    if res is not None:
        in_specs.append(o_spec)
        args.append(res)
    if dep is not None:
        in_specs.append(pl.BlockSpec(memory_space=pl.ANY))
        args.append(dep)
    vec = pl.BlockSpec((1, tn), lambda i, j, k: (0, j))
    if norm_bwd is not None:
        in_specs += [o_spec, vec]
    if norm_g is not None:
        in_specs.append(vec)
    args += extra
    out_shape = [jax.ShapeDtypeStruct((M, N), out_dtype)]
    out_specs = [o_spec]
    if also_bf16 or norm_g is not None:
        out_shape.append(jax.ShapeDtypeStruct((M, N), bf16))
        out_specs.append(o_spec)
    if norm_bwd is not None:
        out_shape.append(jax.ShapeDtypeStruct((1, N), f32))
        out_specs.append(vec)
    dims = ("arbitrary" if norm_bwd is not None else "parallel", "parallel", "arbitrary")
    out = pl.pallas_call(
        body, name=name, grid=(M // tm, N // tn, nk), in_specs=in_specs, out_specs=out_specs, out_shape=out_shape,
        scratch_shapes=[pltpu.VMEM((tm, tn), f32)] if nk > 1 else [], compiler_params=_cp(*dims),
    )(*args)
    return out if len(out) > 1 else out[0]


def _rms_fwd(x, g, *, name, col=0, width=None, tm=512):
    T = x.shape[0]
    width = x.shape[1] if width is None else width
    tm = _tile(T, tm)

    def body(x_ref, g_ref, o_ref):
        xv = x_ref[...]
        r = lax.rsqrt(jnp.mean(xv * xv, axis=-1, keepdims=True) + EPS)
        o_ref[...] = (xv * r * g_ref[...]).astype(bf16)

    return pl.pallas_call(
        body, name=name, grid=(T // tm,),
        in_specs=[pl.BlockSpec((tm, width), lambda i: (i, col)), pl.BlockSpec((1, width), lambda i: (0, 0))],
        out_specs=pl.BlockSpec((tm, width), lambda i: (i, 0)), out_shape=jax.ShapeDtypeStruct((T, width), bf16),
        compiler_params=_cp("parallel"),
    )(x, g)


def _rms_bwd_math(xv, g, dy):
    r = lax.rsqrt(jnp.mean(xv * xv, axis=-1, keepdims=True) + EPS)
    xh = xv * r
    dxh = dy * g
    dx = r * (dxh - xh * jnp.mean(dxh * xh, axis=-1, keepdims=True))
    dg = jnp.sum(dy * xh, axis=0, keepdims=True)
    return dx, dg


def _loss_head(h, g, target, *, tm=512):
    T, D = h.shape
    tm = _tile(T, tm)

    def body(h_ref, g_ref, t_ref, loss_ref, dh_ref, dg_ref):
        hv, gv = h_ref[...], g_ref[...]
        r = lax.rsqrt(jnp.mean(hv * hv, axis=-1, keepdims=True) + EPS)
        e = hv * r * gv - t_ref[...]
        part = 0.5 * jnp.sum(jnp.mean(e * e, axis=-1, keepdims=True), axis=0, keepdims=True)
        dx, dg = _rms_bwd_math(hv, gv, e * (1.0 / D))
        dh_ref[...] = dx

        @pl.when(pl.program_id(0) == 0)
        def _():
            loss_ref[...] = part
            dg_ref[...] = dg

        @pl.when(pl.program_id(0) > 0)
        def _():
            loss_ref[...] += part
            dg_ref[...] += dg

    row = pl.BlockSpec((tm, D), lambda i: (i, 0))
    vec = pl.BlockSpec((1, D), lambda i: (0, 0))
    return pl.pallas_call(
        body, name="loss_head", grid=(T // tm,), in_specs=[row, vec, row],
        out_specs=[pl.BlockSpec((1, 1), lambda i: (0, 0)), row, vec],
        out_shape=[jax.ShapeDtypeStruct((1, 1), f32), jax.ShapeDtypeStruct((T, D), f32), jax.ShapeDtypeStruct((1, D), f32)],
        compiler_params=_cp("arbitrary"),
    )(h, g, target)


FFN_W = 2 * LANES
FFN_ROWS = 128
HALO = 2 * SUBLANES


def _conv_taps(a_ref, c, rc):
    if isinstance(c, int) and c == 0:
        ext = jnp.concatenate([jnp.zeros((HALO, FFN_W), f32), a_ref[pl.ds(0, rc), :].astype(f32)], axis=0)
    else:
        ext = a_ref[pl.ds(pl.multiple_of(c * rc - HALO, HALO), rc + HALO), :].astype(f32)
    return ext[HALO:], pltpu.roll(ext, 1, 0)[HALO:], pltpu.roll(ext, 2, 0)[HALO:]


def _chunk_rows(c, rc):
    return pl.ds(c * rc, rc) if isinstance(c, int) else pl.ds(pl.multiple_of(c * rc, rc), rc)


def _ffn_mid_fwd(au, cw, cb, *, name):
    T = au.shape[0]
    F = au.shape[1] // 2
    nb = F // FFN_W
    rc = min(FFN_ROWS, T)
    nc = T // rc

    def body(a_ref, u_ref, w_ref, b_ref, z_ref):
        w, b = w_ref[...], b_ref[...]

        def chunk(c):
            a, a1, a2 = _conv_taps(a_ref, c, rc)
            rows = _chunk_rows(c, rc)
            ac = (w[0:1] * a2 + w[1:2] * a1 + w[2:3] * a + b).astype(bf16)
            z_ref[rows, :] = ac * jax.nn.sigmoid(ac) * u_ref[rows, :]

        chunk(0)
        lax.fori_loop(1, nc, lambda c, _: chunk(c), None)

    return pl.pallas_call(
        body, name=name, grid=(nb,),
        in_specs=[pl.BlockSpec((T, FFN_W), lambda j: (0, j)), pl.BlockSpec((T, FFN_W), lambda j: (0, nb + j)),
                  pl.BlockSpec((3, FFN_W), lambda j: (0, j)), pl.BlockSpec((1, FFN_W), lambda j: (0, j))],
        out_specs=pl.BlockSpec((T, FFN_W), lambda j: (0, j)), out_shape=jax.ShapeDtypeStruct((T, F), bf16),
        compiler_params=_cp("parallel"),
    )(au, au, cw, cb)


def _ffn_mid_bwd(au, cw, cb, dz, *, name):
    T = au.shape[0]
    F = au.shape[1] // 2
    nb = F // FFN_W
    rc = min(FFN_ROWS, T)
    nc = T // rc

    def body(a_ref, u_ref, w_ref, b_ref, dz_ref, dau_ref, dw_ref, db_ref):
        w, b = w_ref[...], b_ref[...]

        def chunk(c, carry):
            nxt, s0, s1, s2, sb = carry
            a, a1, a2 = _conv_taps(a_ref, c, rc)
            rows = _chunk_rows(c, rc)
            ac = (w[0:1] * a2 + w[1:2] * a1 + w[2:3] * a + b).astype(bf16)
            sg = jax.nn.sigmoid(ac)
            dz = dz_ref[rows, :]
            dau_ref[1, rows, :] = dz * ac * sg
            dac = (dz * u_ref[rows, :] * sg * (1.0 + ac * (1.0 - sg))).astype(f32)
            ext = jnp.concatenate([dac, nxt], axis=0)
            d1, d2 = pltpu.roll(ext, rc + HALO - 1, 0)[:rc], pltpu.roll(ext, rc + HALO - 2, 0)[:rc]
            dau_ref[0, rows, :] = (w[2:3] * dac + w[1:2] * d1 + w[0:1] * d2).astype(bf16)
            tot = lambda v: jnp.sum(v, axis=0, keepdims=True)
            return dac[:HALO], s0 + tot(dac * a2), s1 + tot(dac * a1), s2 + tot(dac * a), sb + tot(dac)

        z = jnp.zeros((1, FFN_W), f32)
        carry = (jnp.zeros((HALO, FFN_W), f32), z, z, z, z)
        carry = lax.fori_loop(0, nc - 1, lambda k, cr: chunk(nc - 1 - k, cr), carry)
        _, s0, s1, s2, sb = chunk(0, carry)
        rows = lax.broadcasted_iota(jnp.int32, (3, FFN_W), 0)
        dw_ref[...] = jnp.where(rows == 0, s0, jnp.where(rows == 1, s1, s2))
        db_ref[...] = sb

    col = lambda off: pl.BlockSpec((T, FFN_W), lambda j: (0, off + j))
    return pl.pallas_call(
        body, name=name, grid=(nb,),
        in_specs=[col(0), col(nb), pl.BlockSpec((3, FFN_W), lambda j: (0, j)), pl.BlockSpec((1, FFN_W), lambda j: (0, j)), col(0)],
        out_specs=[pl.BlockSpec((2, T, FFN_W), lambda j: (0, 0, j)), pl.BlockSpec((3, FFN_W), lambda j: (0, j)),
                   pl.BlockSpec((1, FFN_W), lambda j: (0, j))],
        out_shape=[jax.ShapeDtypeStruct((2, T, F), bf16), jax.ShapeDtypeStruct((3, F), f32), jax.ShapeDtypeStruct((1, F), f32)],
        compiler_params=_cp("parallel"),
    )(au, au, cw, cb, dz)


BNN = (((2,), (1,)), ((0,), (0,)))
BNT = (((2,), (2,)), ((0,), (0,)))
BTN = (((1,), (1,)), ((0,), (0,)))


def _heads(x):
    return jnp.stack([x[:, h * LANES:(h + 1) * LANES] for h in range(HGRN_HEADS)])


def _put_heads(ref, rows, x, dtype):
    for h in range(HGRN_HEADS):
        ref[rows, h * LANES:(h + 1) * LANES] = x[h].astype(dtype)


def _hgrn_lb(l):
    m = jnp.max(l, axis=0, keepdims=True)
    e = jnp.exp(l - m)
    return e[0:1] / jnp.sum(e, axis=0, keepdims=True)


def _hgrn_chunk(q, fx, lb):
    H, C = q.shape[0], q.shape[1]
    sg = jax.nn.sigmoid(fx)
    F = lb + (1.0 - lb) * sg
    k = 1.0 - F
    logF = jnp.log(F)
    r = lax.broadcasted_iota(jnp.int32, (H, C, C), 1)
    c = lax.broadcasted_iota(jnp.int32, (H, C, C), 2)
    tril = (r >= c)
    b = _dot(tril.astype(f32), logF, BNN, precision=HI)
    bl = jnp.sum(logF, axis=1, keepdims=True)
    eb = jnp.exp(b)
    enb = jnp.exp(-b)
    elb = jnp.exp(bl - b)
    return dict(sg=sg, F=F, k=k, b=b, bl=bl, eb=eb, enb=enb, elb=elb, qd=q * eb, kd=k * enb, kl=k * elb, tril=tril)


def _hgrn_fwd(proj, lbl, ng, *, rb=512):
    T = proj.shape[0]
    rb = min(rb, T)
    cpb = rb // HGRN_CHUNK
    nblk = T // rb
    H = HGRN_HEADS

    def body(q_ref, f_ref, i_ref, g_ref, lbl_ref, ng_ref, y_ref, st_ref, S):
        @pl.when(pl.program_id(0) == 0)
        def _():
            S[...] = jnp.zeros_like(S)

        lb = _heads(_hgrn_lb(lbl_ref[...]))
        ngv = _heads(ng_ref[...])
        for c in range(cpb):
            sl = pl.ds(c * HGRN_CHUNK, HGRN_CHUNK)
            v, gx = _heads(i_ref[sl, :]), _heads(g_ref[sl, :])
            ch = _hgrn_chunk(_heads(q_ref[sl, :]), _heads(f_ref[sl, :]), lb)
            att = jnp.where(ch["tril"], _bdot(ch["qd"], ch["kd"], BNT), 0.0)
            St = S[...]
            st_ref[:, c] = St
            o = _bdot(att, v, BNN) + _bdot(ch["qd"], St, BNT)
            S[...] = St * jnp.exp(ch["bl"]) + _bdot(v, ch["kl"], BTN)
            r = lax.rsqrt(jnp.mean(o * o, axis=-1, keepdims=True) + EPS)
            _put_heads(y_ref, sl, o * r * ngv * (gx * jax.nn.sigmoid(gx)), bf16)

    col = lambda off: pl.BlockSpec((rb, H * LANES), lambda n: (n, off))
    return pl.pallas_call(
        body, name="hgrn_fwd", grid=(nblk,),
        in_specs=[col(0), col(1), col(2), col(3), pl.BlockSpec((2, H * LANES), lambda n: (0, 0)),
                  pl.BlockSpec((1, H * LANES), lambda n: (0, 0))],
        out_specs=[pl.BlockSpec((rb, H * LANES), lambda n: (n, 0)),
                   pl.BlockSpec((H, cpb, LANES, LANES), lambda n: (0, n, 0, 0))],
        out_shape=[jax.ShapeDtypeStruct((T, H * LANES), bf16),
                   jax.ShapeDtypeStruct((H, T // HGRN_CHUNK, LANES, LANES), f32)],
        scratch_shapes=[pltpu.VMEM((H, LANES, LANES), f32)], compiler_params=_cp("arbitrary"),
    )(proj, proj, proj, proj, lbl, ng)


def _hgrn_bwd(proj, lbl, ng, states, dy, *, rb=512):
    T = proj.shape[0]
    rb = min(rb, T)
    cpb = rb // HGRN_CHUNK
    nblk = T // rb
    H = HGRN_HEADS
    C = HGRN_CHUNK

    def body(q_ref, f_ref, i_ref, g_ref, lbl_ref, ng_ref, st_ref, dy_ref,
             dq_ref, df_ref, di_ref, dg_ref, dl_ref, dng_ref, dS, dlb_acc, dng_acc):
        n = pl.program_id(0)

        @pl.when(n == 0)
        def _():
            dS[...] = jnp.zeros_like(dS)
            dlb_acc[...] = jnp.zeros_like(dlb_acc)
            dng_acc[...] = jnp.zeros_like(dng_acc)

        lb_row = _hgrn_lb(lbl_ref[...])
        lb = _heads(lb_row)
        ngv = _heads(ng_ref[...])
        r_i = lax.broadcasted_iota(jnp.int32, (H, C, C), 1)
        c_i = lax.broadcasted_iota(jnp.int32, (H, C, C), 2)
        triu = (c_i >= r_i).astype(f32)
        rows_sum = lambda x: jnp.sum(x, axis=1, keepdims=True)
        for c in reversed(range(cpb)):
            sl = pl.ds(c * C, C)
            q, v, gx = _heads(q_ref[sl, :]), _heads(i_ref[sl, :]), _heads(g_ref[sl, :])
            ch = _hgrn_chunk(q, _heads(f_ref[sl, :]), lb)
            qd, kd, kl = ch["qd"], ch["kd"], ch["kl"]
            att = jnp.where(ch["tril"], _bdot(qd, kd, BNT), 0.0)
            St = st_ref[:, c]
            o = _bdot(att, v, BNN) + _bdot(qd, St, BNT)
            r = lax.rsqrt(jnp.mean(o * o, axis=-1, keepdims=True) + EPS)
            on = o * r
            sgg = jax.nn.sigmoid(gx)
            gate = gx * sgg
            dyv = _heads(dy_ref[sl, :].astype(f32))
            _put_heads(dg_ref, sl, dyv * on * ngv * sgg * (1.0 + gx * (1.0 - sgg)), bf16)
            dng_acc[...] += rows_sum(dyv * on * gate)
            don = dyv * ngv * gate
            do = r * (don - on * jnp.mean(don * on, axis=-1, keepdims=True))
            dSt = dS[...]
            dA = jnp.where(ch["tril"], _bdot(do, v, BNT), 0.0)
            dv = _bdot(att, do, BTN) + _bdot(kl, dSt, BNT)
            dqd = _bdot(dA, kd, BNN) + _bdot(do, St, BNN)
            dkd = _bdot(dA, qd, BTN)
            dkl = _bdot(v, dSt, BNN)
            dec = jnp.exp(ch["bl"])
            ddec = rows_sum(St * dSt)
            dS[...] = _bdot(do, qd, BTN) + dSt * dec
            dB = dqd * qd - dkd * kd - dkl * kl
            dbl = rows_sum(dkl * kl) + ddec * dec
            dk = dkd * ch["enb"] + dkl * ch["elb"]
            dlogF = _dot(triu, dB, BNN, precision=HI) + dbl
            dF = dlogF / ch["F"] - dk
            sg = ch["sg"]
            _put_heads(dq_ref, sl, dqd * ch["eb"], bf16)
            _put_heads(di_ref, sl, dv, bf16)
            _put_heads(df_ref, sl, dF * (1.0 - lb) * sg * (1.0 - sg), bf16)
            dlb_acc[...] += rows_sum(dF * (1.0 - sg))

        @pl.when(n == nblk - 1)
        def _():
            rows = lax.broadcasted_iota(jnp.int32, (2, LANES), 0)
            for h in range(H):
                hs = pl.ds(h * LANES, LANES)
                lbh = lb_row[:, h * LANES:(h + 1) * LANES]
                dl0 = dlb_acc[h] * lbh * (1.0 - lbh)
                dl_ref[:, hs] = jnp.where(rows == 0, dl0, -dl0)
                dng_ref[:, hs] = dng_acc[h]

    col = lambda off: pl.BlockSpec((rb, H * LANES), lambda n: (nblk - 1 - n, off))
    vec = lambda rows: pl.BlockSpec((rows, H * LANES), lambda n: (0, 0))
    tok = jax.ShapeDtypeStruct((T, H * LANES), bf16)
    return pl.pallas_call(
        body, name="hgrn_bwd", grid=(nblk,),
        in_specs=[col(0), col(1), col(2), col(3), vec(2), vec(1),
                  pl.BlockSpec((H, cpb, LANES, LANES), lambda n: (0, nblk - 1 - n, 0, 0)), col(0)],
        out_specs=[col(0), col(0), col(0), col(0), vec(2), vec(1)],
        out_shape=[tok, tok, tok, tok, jax.ShapeDtypeStruct((2, H * LANES), f32), jax.ShapeDtypeStruct((1, H * LANES), f32)],
        scratch_shapes=[pltpu.VMEM((H, LANES, LANES), f32), pltpu.VMEM((H, 1, LANES), f32), pltpu.VMEM((H, 1, LANES), f32)],
        compiler_params=_cp("arbitrary"),
    )(proj, proj, proj, proj, lbl, ng, states, dy)


def _s5_disc_math(ar, ai, ldt, br, bi):
    dt = jnp.exp(ldt)
    mag = jnp.exp(ar * dt)
    abr, abi = mag * jnp.cos(ai * dt), mag * jnp.sin(ai * dt)
    den = ar * ar + ai * ai
    xr, xi = abr - 1.0, abi
    cr = (xr * ar + xi * ai) / den
    ci = (xi * ar - xr * ai) / den
    return abr, abi, cr * br - ci * bi, cr * bi + ci * br


def _s5_disc_fwd(ar, ai, ldt, br, bi):
    def body(ar_ref, ai_ref, ldt_ref, br_ref, bi_ref, o0, o1, o2, o3):
        outs = _s5_disc_math(ar_ref[...], ai_ref[...], ldt_ref[...], br_ref[...], bi_ref[...])
        for o, v in zip((o0, o1, o2, o3), outs):
            o[...] = v

    return pl.pallas_call(
        body, name="s5_disc_fwd",
        out_shape=[jax.ShapeDtypeStruct(ar.shape, f32)] * 2 + [jax.ShapeDtypeStruct(br.shape, f32)] * 2,
    )(ar, ai, ldt, br, bi)


def _s5_disc_bwd(ar, ai, ldt, br, bi, cts):
    def body(ar_ref, ai_ref, ldt_ref, br_ref, bi_ref, c0, c1, c2, c3, o0, o1, o2, o3, o4):
        _, vjp = jax.vjp(_s5_disc_math, ar_ref[...], ai_ref[...], ldt_ref[...], br_ref[...], bi_ref[...])
        for o, v in zip((o0, o1, o2, o3, o4), vjp((c0[...], c1[...], c2[...], c3[...]))):
            o[...] = v

    return pl.pallas_call(
        body, name="s5_disc_bwd",
        out_shape=[jax.ShapeDtypeStruct(ar.shape, f32)] * 3 + [jax.ShapeDtypeStruct(br.shape, f32)] * 2,
    )(ar, ai, ldt, br, bi, *cts)


S5_LC = 512
S5_NLC = S5_N // S5_LC
S5_UB = 4
S5_UNROLL = 4


def _cmul(ar, ai, xr, xi):
    return ar * xr - ai * xi, ar * xi + ai * xr


def _cpow(ar, ai, n):
    rr, ri = None, None
    br, bi = ar, ai
    while n:
        if n & 1:
            rr, ri = (br, bi) if rr is None else _cmul(rr, ri, br, bi)
        n >>= 1
        if n:
            br, bi = _cmul(br, bi, br, bi)
    return rr, ri


def _s5_bu(u_ref, bre_ref, bim_ref, xr, xi):
    for k in range(S5_UB):
        uk = u_ref[:, k * LANES:(k + 1) * LANES].astype(bf16)
        xr[:, k * S5_LC:(k + 1) * S5_LC] = _dot(uk, bre_ref[k])
        xi[:, k * S5_LC:(k + 1) * S5_LC] = _dot(uk, bim_ref[k])


def _s5_scan(xr, xi, sr, si, ar_ref, ai_ref, nsteps, store):
    for c in range(S5_NLC):
        cs = slice(c * S5_LC, (c + 1) * S5_LC)
        a_r = jnp.broadcast_to(ar_ref[:, cs], (S5_SEG, S5_LC))
        a_i = jnp.broadcast_to(ai_ref[:, cs], (S5_SEG, S5_LC))

        def step(j, carry, cs=cs, a_r=a_r, a_i=a_i):
            pr, pi = carry
            rows = pl.ds(pl.multiple_of(j * S5_SEG, S5_SEG), S5_SEG)
            nr = a_r * pr - a_i * pi + xr[rows, cs]
            ni = a_r * pi + a_i * pr + xi[rows, cs]
            if store:
                xr[rows, cs] = nr
                xi[rows, cs] = ni
            return nr, ni

        fr, fi = lax.fori_loop(0, nsteps, step, (sr[:, cs], si[:, cs]), unroll=S5_UNROLL)
        sr[:, cs] = fr
        si[:, cs] = fi


def _s5_rscan(dr, di, xr, xi, s0r, s0i, gr, gi, acc_r, acc_i, ar_ref, ai_ref, nsteps):
    for c in range(S5_NLC):
        cs = slice(c * S5_LC, (c + 1) * S5_LC)
        a_r = jnp.broadcast_to(ar_ref[:, cs], (S5_SEG, S5_LC))
        a_i = jnp.broadcast_to(ai_ref[:, cs], (S5_SEG, S5_LC))

        def step(jj, carry, cs=cs, a_r=a_r, a_i=a_i):
            pr, pi, cr, ci = carry
            j = nsteps - 1 - jj
            rows = pl.ds(pl.multiple_of(j * S5_SEG, S5_SEG), S5_SEG)
            nr = dr[rows, cs] + a_r * pr + a_i * pi
            ni = di[rows, cs] + a_r * pi - a_i * pr
            dr[rows, cs] = nr
            di[rows, cs] = ni
            if acc_r is not None:
                prev = pl.ds(pl.multiple_of(jnp.maximum(j - 1, 0) * S5_SEG, S5_SEG), S5_SEG)
                first = j == 0
                pr_s = jnp.where(first, s0r[:, cs], xr[prev, cs])
                pi_s = jnp.where(first, s0i[:, cs], xi[prev, cs])
                cr = cr + nr * pr_s + ni * pi_s
                ci = ci - nr * pi_s + ni * pr_s
            return nr, ni, cr, ci

        z = jnp.zeros((S5_SEG, S5_LC), f32)
        init = (gr[:, cs], gi[:, cs], z, z)
        fr, fi, cr, ci = lax.fori_loop(0, nsteps, step, init, unroll=S5_UNROLL)
        gr[:, cs] = fr
        gi[:, cs] = fi
        if acc_r is not None:
            acc_r[:, cs] += cr
            acc_i[:, cs] += ci


def _s5_seg_carry(fr, fi, ar, ai, seg_len, reverse):
    pr, pi = _cpow(ar, ai if not reverse else -ai, seg_len)
    rows = lax.broadcasted_iota(jnp.int32, fr.shape, 0)
    cr, ci = jnp.zeros_like(fr), jnp.zeros_like(fi)
    sh = (S5_SEG - 1) if reverse else 1
    fr_s, fi_s = pltpu.roll(fr, sh, 0), pltpu.roll(fi, sh, 0)
    order = range(S5_SEG - 2, -1, -1) if reverse else range(1, S5_SEG)
    for r in order:
        c_r, c_i = pltpu.roll(cr, sh, 0), pltpu.roll(ci, sh, 0)
        m_r, m_i = _cmul(pr, pi, c_r, c_i)
        cr = jnp.where(rows == r, m_r + fr_s, cr)
        ci = jnp.where(rows == r, m_i + fi_s, ci)
    return cr, ci


def _gelu_parts(y):
    c0 = math.sqrt(2.0 / math.pi)
    t = jnp.tanh(c0 * (y + 0.044715 * y * y * y))
    z = 0.5 * y * (1.0 + t)
    dz = 0.5 * (1.0 + t) + 0.5 * y * (1.0 - t * t) * c0 * (1.0 + 3.0 * 0.044715 * y * y)
    return z, dz


def _s5_y(xr, xi, u_ref, cre_ref, cim_ref, d_ref):
    ys = []
    for k in range(S5_UB):
        cs = slice(k * S5_LC, (k + 1) * S5_LC)
        ys.append(_bdot(xr[:, cs], cre_ref[k]) - _bdot(xi[:, cs], cim_ref[k]))
    return jnp.concatenate(ys, axis=1) + d_ref[...] * u_ref[...]


def _s5_specs(T, rb, rev=False):
    nblk = T // rb
    blk = (lambda i: (nblk - 1 - i, 0)) if rev else (lambda i: (i, 0))
    tok = pl.BlockSpec((rb, 4 * LANES), blk)
    bmat = pl.BlockSpec((S5_UB, LANES, S5_LC), lambda i: (0, 0, 0))
    cmat = pl.BlockSpec((S5_UB, S5_LC, LANES), lambda i: (0, 0, 0))
    avec = pl.BlockSpec((1, S5_N), lambda i: (0, 0))
    seg = pl.BlockSpec((S5_SEG, S5_N), lambda i: (0, 0))
    cvec = pl.BlockSpec((1, 4 * LANES), lambda i: (0, 0))
    s0 = pl.BlockSpec((1, S5_SEG, S5_N), (lambda i: (nblk - 1 - i, 0, 0)) if rev else (lambda i: (i, 0, 0)))
    return dict(tok=tok, bmat=bmat, cmat=cmat, avec=avec, seg=seg, cvec=cvec, s0=s0, nblk=nblk)


def _s5_final(u, bre, bim, ar, ai, *, rb):
    T = u.shape[0]
    sp = _s5_specs(T, rb)

    def body(u_ref, bre_ref, bim_ref, ar_ref, ai_ref, fr_ref, fi_ref, xr, xi):
        @pl.when(pl.program_id(0) == 0)
        def _():
            fr_ref[...] = jnp.zeros_like(fr_ref)
            fi_ref[...] = jnp.zeros_like(fi_ref)

        _s5_bu(u_ref, bre_ref, bim_ref, xr, xi)
        _s5_scan(xr, xi, fr_ref, fi_ref, ar_ref, ai_ref, rb // S5_SEG, False)

    return pl.pallas_call(
        body, name="s5_final", grid=(sp["nblk"],),
        in_specs=[sp["tok"], sp["bmat"], sp["bmat"], sp["avec"], sp["avec"]], out_specs=[sp["seg"], sp["seg"]],
        out_shape=[jax.ShapeDtypeStruct((S5_SEG, S5_N), f32)] * 2,
        scratch_shapes=[pltpu.VMEM((rb, S5_N), f32)] * 2, compiler_params=_cp("arbitrary"),
    )(u, bre, bim, ar, ai)


def _s5_fwd(u, bre, bim, ar, ai, fr, fi, cre, cim, dsk, wg, bg, *, rb):
    T = u.shape[0]
    sp = _s5_specs(T, rb)
    seg_len = T // S5_SEG

    def body(u_ref, bre_ref, bim_ref, ar_ref, ai_ref, fr_ref, fi_ref, cre_ref, cim_ref, d_ref, wg_ref, bg_ref,
             o_ref, s0r_ref, s0i_ref, xr, xi, sr, si):
        @pl.when(pl.program_id(0) == 0)
        def _():
            i_r, i_i = _s5_seg_carry(fr_ref[...], fi_ref[...], ar_ref[...], ai_ref[...], seg_len, False)
            sr[...] = i_r
            si[...] = i_i

        s0r_ref[0] = sr[...]
        s0i_ref[0] = si[...]
        _s5_bu(u_ref, bre_ref, bim_ref, xr, xi)
        _s5_scan(xr, xi, sr, si, ar_ref, ai_ref, rb // S5_SEG, True)
        y = _s5_y(xr, xi, u_ref, cre_ref, cim_ref, d_ref)
        z, _ = _gelu_parts(y)
        v = _bdot(z, wg_ref[...]) + bg_ref[...]
        o_ref[...] = (z * jax.nn.sigmoid(v)).astype(bf16)

    wspec = pl.BlockSpec((4 * LANES, 4 * LANES), lambda i: (0, 0))
    return pl.pallas_call(
        body, name="s5_fwd", grid=(sp["nblk"],),
        in_specs=[sp["tok"], sp["bmat"], sp["bmat"], sp["avec"], sp["avec"], sp["seg"], sp["seg"], sp["cmat"], sp["cmat"],
                  sp["cvec"], wspec, sp["cvec"]],
        out_specs=[sp["tok"], sp["s0"], sp["s0"]],
        out_shape=[jax.ShapeDtypeStruct((T, 4 * LANES), bf16)] + [jax.ShapeDtypeStruct((sp["nblk"], S5_SEG, S5_N), f32)] * 2,
        scratch_shapes=[pltpu.VMEM((rb, S5_N), f32)] * 2 + [pltpu.VMEM((S5_SEG, S5_N), f32)] * 2,
        compiler_params=_cp("arbitrary"),
    )(u, bre, bim, ar, ai, fr, fi, cre, cim, dsk, wg, bg)


def _s5_bwd_a(u, bre, bim, ar, ai, s0r, s0i, cre, cim, cret, cimt, dsk, wg, bg, dout, *, rb):
    T = u.shape[0]
    sp = _s5_specs(T, rb, rev=True)

    def body(u_ref, bre_ref, bim_ref, ar_ref, ai_ref, s0r_ref, s0i_ref, cre_ref, cim_ref, cret_ref, cimt_ref,
             d_ref, wg_ref, bg_ref, do_ref, dy_ref, glr_ref, gli_ref, dcre_ref, dcim_ref, dd_ref, dwg_ref, dbg_ref,
             xr, xi, dr, di, sr, si):
        @pl.when(pl.program_id(0) == 0)
        def _():
            for r in (glr_ref, gli_ref, dcre_ref, dcim_ref, dd_ref, dwg_ref, dbg_ref):
                r[...] = jnp.zeros_like(r)

        sr[...] = s0r_ref[0]
        si[...] = s0i_ref[0]
        _s5_bu(u_ref, bre_ref, bim_ref, xr, xi)
        _s5_scan(xr, xi, sr, si, ar_ref, ai_ref, rb // S5_SEG, True)
        uv = u_ref[...]
        y = _s5_y(xr, xi, u_ref, cre_ref, cim_ref, d_ref)
        z, gz = _gelu_parts(y)
        v = _bdot(z, wg_ref[...]) + bg_ref[...]
        sg = jax.nn.sigmoid(v)
        dov = do_ref[...].astype(f32)
        dv = dov * z * sg * (1.0 - sg)
        dz = dov * sg + _bdot(dv, wg_ref[...], NT)
        dy = dz * gz
        dy_ref[...] = dy
        dwg_ref[...] += _bdot(z, dv, TN)
        dbg_ref[...] += jnp.sum(dv, axis=0, keepdims=True)
        dd_ref[...] += jnp.sum(dy * uv, axis=0, keepdims=True)
        for k in range(S5_UB):
            cs = slice(k * S5_LC, (k + 1) * S5_LC)
            dyk = dy[:, k * LANES:(k + 1) * LANES]
            dcre_ref[k] += _bdot(xr[:, cs], dyk, TN)
            dcim_ref[k] -= _bdot(xi[:, cs], dyk, TN)
            dr[:, cs] = _bdot(dyk, cret_ref[k])
            di[:, cs] = -_bdot(dyk, cimt_ref[k])
        _s5_rscan(dr, di, None, None, None, None, glr_ref, gli_ref, None, None, ar_ref, ai_ref, rb // S5_SEG)

    wspec = pl.BlockSpec((4 * LANES, 4 * LANES), lambda i: (0, 0))
    return pl.pallas_call(
        body, name="s5_bwd_a", grid=(sp["nblk"],),
        in_specs=[sp["tok"], sp["bmat"], sp["bmat"], sp["avec"], sp["avec"], sp["s0"], sp["s0"], sp["cmat"], sp["cmat"],
                  sp["bmat"], sp["bmat"], sp["cvec"], wspec, sp["cvec"], sp["tok"]],
        out_specs=[sp["tok"], sp["seg"], sp["seg"], sp["cmat"], sp["cmat"], sp["cvec"], wspec, sp["cvec"]],
        out_shape=[jax.ShapeDtypeStruct((T, 4 * LANES), f32)] + [jax.ShapeDtypeStruct((S5_SEG, S5_N), f32)] * 2
        + [jax.ShapeDtypeStruct((S5_UB, S5_LC, LANES), f32)] * 2
        + [jax.ShapeDtypeStruct((1, 4 * LANES), f32), jax.ShapeDtypeStruct((4 * LANES, 4 * LANES), f32),
           jax.ShapeDtypeStruct((1, 4 * LANES), f32)],
        scratch_shapes=[pltpu.VMEM((rb, S5_N), f32)] * 4 + [pltpu.VMEM((S5_SEG, S5_N), f32)] * 2,
        compiler_params=_cp("arbitrary"),
    )(u, bre, bim, ar, ai, s0r, s0i, cre, cim, cret, cimt, dsk, wg, bg, dout)


def _s5_bwd_b(u, bre, bim, bret, bimt, ar, ai, s0r, s0i, glr, gli, cret, cimt, dsk, dy, *, rb):
    T = u.shape[0]
    sp = _s5_specs(T, rb, rev=True)
    seg_len = T // S5_SEG
    nblk = sp["nblk"]

    def body(u_ref, bre_ref, bim_ref, bret_ref, bimt_ref, ar_ref, ai_ref, s0r_ref, s0i_ref, glr_ref, gli_ref,
             cret_ref, cimt_ref, d_ref, dy_ref, du_ref, dbre_ref, dbim_ref, dar_ref, dai_ref,
             xr, xi, dr, di, sr, si, gr, gi, acc_r, acc_i):
        @pl.when(pl.program_id(0) == 0)
        def _():
            x_r, x_i = _s5_seg_carry(glr_ref[...], gli_ref[...], ar_ref[...], ai_ref[...], seg_len, True)
            gr[...] = x_r
            gi[...] = x_i
            acc_r[...] = jnp.zeros_like(acc_r)
            acc_i[...] = jnp.zeros_like(acc_i)
            dbre_ref[...] = jnp.zeros_like(dbre_ref)
            dbim_ref[...] = jnp.zeros_like(dbim_ref)

        sr[...] = s0r_ref[0]
        si[...] = s0i_ref[0]
        _s5_bu(u_ref, bre_ref, bim_ref, xr, xi)
        _s5_scan(xr, xi, sr, si, ar_ref, ai_ref, rb // S5_SEG, True)
        dy = dy_ref[...]
        for k in range(S5_UB):
            cs = slice(k * S5_LC, (k + 1) * S5_LC)
            dyk = dy[:, k * LANES:(k + 1) * LANES]
            dr[:, cs] = _bdot(dyk, cret_ref[k])
            di[:, cs] = -_bdot(dyk, cimt_ref[k])
        sr[...] = s0r_ref[0]
        si[...] = s0i_ref[0]
        _s5_rscan(dr, di, xr, xi, sr, si, gr, gi, acc_r, acc_i, ar_ref, ai_ref, rb // S5_SEG)
        dus = []
        for k in range(S5_UB):
            cs = slice(k * S5_LC, (k + 1) * S5_LC)
            uk = u_ref[:, k * LANES:(k + 1) * LANES]
            dbre_ref[k] += _bdot(uk, dr[:, cs], TN)
            dbim_ref[k] += _bdot(uk, di[:, cs], TN)
            dus.append(_bdot(dr[:, cs], bret_ref[k]) + _bdot(di[:, cs], bimt_ref[k]))
        du_ref[...] = (jnp.concatenate(dus, axis=1) + d_ref[...] * dy).astype(bf16)

        @pl.when(pl.program_id(0) == nblk - 1)
        def _():
            dar_ref[...] = jnp.sum(acc_r[...], axis=0, keepdims=True)
            dai_ref[...] = jnp.sum(acc_i[...], axis=0, keepdims=True)

    return pl.pallas_call(
        body, name="s5_bwd_b", grid=(nblk,),
        in_specs=[sp["tok"], sp["bmat"], sp["bmat"], sp["cmat"], sp["cmat"], sp["avec"], sp["avec"], sp["s0"], sp["s0"],
                  sp["seg"], sp["seg"], sp["bmat"], sp["bmat"], sp["cvec"], sp["tok"]],
        out_specs=[sp["tok"], sp["bmat"], sp["bmat"], sp["avec"], sp["avec"]],
        out_shape=[jax.ShapeDtypeStruct((T, 4 * LANES), bf16)] + [jax.ShapeDtypeStruct((S5_UB, LANES, S5_LC), f32)] * 2
        + [jax.ShapeDtypeStruct((1, S5_N), f32)] * 2,
        scratch_shapes=[pltpu.VMEM((rb, S5_N), f32)] * 4 + [pltpu.VMEM((S5_SEG, S5_N), f32)] * 6,
        compiler_params=_cp("arbitrary"),
    )(u, bre, bim, bret, bimt, ar, ai, s0r, s0i, glr, gli, cret, cimt, dsk, dy)


def _blockdiag(w, transpose=False):
    if transpose:
        w = jnp.swapaxes(w, 1, 2)
    g, a, b = w.shape
    eye = jnp.eye(8, dtype=w.dtype)
    return jnp.einsum("kgab,gj->kgajb", w.reshape(4, 8, a, b), eye).reshape(4, 8 * a, 8 * b)


def _blockdiag_t(m, a, b):
    eye = jnp.eye(8, dtype=m.dtype)
    return jnp.einsum("kgajb,gj->kgab", m.reshape(4, 8, a, 8, b), eye).reshape(32, a, b)


ROT = MLA_ROPE // 2


def _rope_tables(positions):
    freqs = ROPE_THETA ** (-jnp.arange(0, MLA_ROPE, 2, dtype=f32) / MLA_ROPE)
    ang = positions.astype(f32)[:, None] * freqs
    cos, sin, z = jnp.cos(ang), jnp.sin(ang), jnp.zeros_like(ang)
    return (jnp.concatenate([cos, cos, z, z], axis=1), jnp.concatenate([-sin, z, z, z], axis=1),
            jnp.concatenate([z, sin, z, z], axis=1))


def _rot(x, c, sa, sb):
    return x * c + pltpu.roll(x, LANES - ROT, 1) * sa + pltpu.roll(x, ROT, 1) * sb


def _rot_t(dy, c, sa, sb):
    return dy * c + pltpu.roll(dy * sa, ROT, 1) + pltpu.roll(dy * sb, LANES - ROT, 1)


def _rms(xv, g):
    return xv * lax.rsqrt(jnp.mean(xv * xv, axis=-1, keepdims=True) + EPS) * g


QW, KVW = MLA_Q_RANK, MLA_KV_RANK
ODD_PAD = QW + KVW + LANES


def _mla_prep_fwd(proj, qg, kvg, tabs, *, tm=512):
    T = proj.shape[0]
    tm = _tile(T, tm)

    def body(p_ref, qg_ref, kvg_ref, c_ref, sa_ref, sb_ref, cq_ref, ckv_ref, kr_ref):
        cq_ref[...] = _rms(p_ref[:, :QW], qg_ref[...]).astype(bf16)
        ckv_ref[...] = _rms(p_ref[:, QW:QW + KVW], kvg_ref[...]).astype(bf16)
        kr_ref[...] = _rot(p_ref[:, QW + KVW:], c_ref[...], sa_ref[...], sb_ref[...]).astype(bf16)

    row = lambda w: pl.BlockSpec((tm, w), lambda i: (i, 0))
    vec = lambda w: pl.BlockSpec((1, w), lambda i: (0, 0))
    return pl.pallas_call(
        body, name="mla_prep_fwd", grid=(T // tm,),
        in_specs=[row(ODD_PAD), vec(QW), vec(KVW), row(LANES), row(LANES), row(LANES)],
        out_specs=[row(QW), row(KVW), row(LANES)],
        out_shape=[jax.ShapeDtypeStruct((T, QW), bf16), jax.ShapeDtypeStruct((T, KVW), bf16),
                   jax.ShapeDtypeStruct((T, LANES), bf16)],
        compiler_params=_cp("parallel"),
    )(proj, qg, kvg, *tabs)


def _mla_prep_bwd(proj, qg, kvg, tabs, dcqn, dckvn, dkr_heads, *, tm=512):
    T = proj.shape[0]
    tm = _tile(T, tm)

    def body(p_ref, qg_ref, kvg_ref, c_ref, sa_ref, sb_ref, dcq_ref, dckv_ref, dkr_ref, dp_ref, dqg_ref, dkvg_ref):
        dcq, dqg = _rms_bwd_math(p_ref[:, :QW], qg_ref[...], dcq_ref[...])
        dckv, dkvg = _rms_bwd_math(p_ref[:, QW:QW + KVW], kvg_ref[...], dckv_ref[...])
        dk = dkr_ref[:, :LANES]
        for h in range(1, MLA_HEADS):
            dk = dk + dkr_ref[:, h * LANES:(h + 1) * LANES]
        dkr = _rot_t(dk, c_ref[...], sa_ref[...], sb_ref[...])
        dp_ref[...] = jnp.concatenate([dcq, dckv, dkr], axis=1).astype(bf16)

        @pl.when(pl.program_id(0) == 0)
        def _():
            dqg_ref[...] = dqg
            dkvg_ref[...] = dkvg

        @pl.when(pl.program_id(0) > 0)
        def _():
            dqg_ref[...] += dqg
            dkvg_ref[...] += dkvg

    row = lambda w: pl.BlockSpec((tm, w), lambda i: (i, 0))
    vec = lambda w: pl.BlockSpec((1, w), lambda i: (0, 0))
    return pl.pallas_call(
        body, name="mla_prep_bwd", grid=(T // tm,),
        in_specs=[row(ODD_PAD), vec(QW), vec(KVW), row(LANES), row(LANES), row(LANES), row(QW), row(KVW),
                  row(MLA_HEADS * LANES)],
        out_specs=[row(ODD_PAD), vec(QW), vec(KVW)],
        out_shape=[jax.ShapeDtypeStruct((T, ODD_PAD), bf16), jax.ShapeDtypeStruct((1, QW), f32),
                   jax.ShapeDtypeStruct((1, KVW), f32)],
        compiler_params=_cp("arbitrary"),
    )(proj, qg, kvg, *tabs, dcqn, dckvn, dkr_heads)


HQ = 2 * LANES
QK_SCALE = MLA_QK ** -0.5


def _q_post(q, tabs, *, transpose, name, tm=512):
    T = q.shape[0]
    tm = _tile(T, tm)

    def body(q_ref, c_ref, sa_ref, sb_ref, o_ref):
        c, sa, sb = c_ref[...], sa_ref[...], sb_ref[...]
        for h in range(MLA_HEADS):
            nope, rope = pl.ds(h * HQ, LANES), pl.ds(h * HQ + LANES, LANES)
            o_ref[:, nope] = (q_ref[:, nope].astype(f32) * QK_SCALE).astype(bf16)
            o_ref[:, rope] = ((_rot_t if transpose else _rot)(q_ref[:, rope].astype(f32), c, sa, sb) * QK_SCALE).astype(bf16)

    tab = pl.BlockSpec((tm, LANES), lambda i: (i, 0))
    blk = pl.BlockSpec((tm, MLA_HEADS * HQ), lambda i: (i, 0))
    return pl.pallas_call(
        body, name=name, grid=(T // tm,), in_specs=[blk, tab, tab, tab], out_specs=blk,
        out_shape=jax.ShapeDtypeStruct(q.shape, bf16), compiler_params=_cp("parallel"),
    )(q, *tabs)


def _causal_mask(i, j, tq, tk):
    r = lax.broadcasted_iota(jnp.int32, (tq, tk), 0) + i * tq
    c = lax.broadcasted_iota(jnp.int32, (tq, tk), 1) + j * tk
    return c <= r


FLASH_PARTS = 4
FLASH_BWD_PARTS = 1


def _flash_fwd(q, kv, kr, *, tq=1024, tk=1024):
    T = q.shape[0]
    tq = _tile(T, tq)
    tk = _tile(tq, tk)
    per = tq // tk
    H = MLA_HEADS

    def body(q_ref, kn_ref, v_ref, kr_ref, o_ref, lse_ref, m_s, acc):
        i, j = pl.program_id(1), pl.program_id(2)
        last = (i + 1) * per - 1

        @pl.when(j == 0)
        def _():
            m_s[...] = jnp.full_like(m_s, -jnp.inf)
            acc[...] = jnp.zeros_like(acc)

        def step(masked):
            k = jnp.concatenate([kn_ref[...], kr_ref[...]], axis=1)
            v1 = jnp.concatenate([v_ref[...], jnp.ones((tk, LANES), bf16)], axis=1)
            mask = _causal_mask(i, j, tq, tk) if masked else None
            for part in range(FLASH_PARTS):
                rows = pl.ds(part * (tq // FLASH_PARTS), tq // FLASH_PARTS)
                s = _dot(q_ref[rows, :], k, NT)
                if masked:
                    s = jnp.where(mask[part * (tq // FLASH_PARTS):(part + 1) * (tq // FLASH_PARTS)], s, -jnp.inf)
                m_new = jnp.maximum(m_s[rows, :], jnp.max(s, axis=-1, keepdims=True))
                alpha = jnp.exp(m_s[rows, :] - m_new)
                p = jnp.exp((s - m_new).astype(bf16))
                acc[rows, :] = alpha * acc[rows, :] + _dot(p, v1)
                m_s[rows, :] = m_new

        pl.when(j < i * per)(functools.partial(step, False))
        pl.when((j >= i * per) & (j <= last))(functools.partial(step, True))

        @pl.when(j == last)
        def _():
            l = acc[:, LANES:]
            o_ref[...] = (acc[:, :LANES] / l).astype(bf16)
            lse_ref[0] = m_s[...] + jnp.log(jnp.max(l, axis=-1, keepdims=True))

    kj = lambda i, j: jnp.minimum(j, (i + 1) * per - 1)
    kblk = lambda off: pl.BlockSpec((tk, LANES), lambda h, i, j: (kj(i, j), 2 * h + off))
    return pl.pallas_call(
        body, name="flash_fwd", grid=(H, T // tq, T // tk),
        in_specs=[pl.BlockSpec((tq, HQ), lambda h, i, j: (i, h)), kblk(0), kblk(1),
                  pl.BlockSpec((tk, LANES), lambda h, i, j: (kj(i, j), 0))],
        out_specs=[pl.BlockSpec((tq, LANES), lambda h, i, j: (i, h)), pl.BlockSpec((1, tq, 1), lambda h, i, j: (h, i, 0))],
        out_shape=[jax.ShapeDtypeStruct((T, H * LANES), bf16), jax.ShapeDtypeStruct((H, T, 1), f32)],
        scratch_shapes=[pltpu.VMEM((tq, 1), f32), pltpu.VMEM((tq, 2 * LANES), f32)],
        compiler_params=_cp("parallel", "parallel", "arbitrary"),
    )(q, kv, kv, kr)


def _flash_bwd(q, kv, kr, o, do, lse, *, tb=1024):
    T = q.shape[0]
    tb = _tile(T, tb)
    nb = T // tb
    H = MLA_HEADS

    def body(q_ref, kn_ref, v_ref, kr_ref, o_ref, do_ref, lse_ref, dkv_ref, dkr_ref, dq_ref, dk_acc, dv_acc):
        j, ii = pl.program_id(1), pl.program_id(2)
        i = jnp.maximum(ii, j)

        @pl.when((j == 0) & (ii == 0))
        def _():
            dq_ref[...] = jnp.zeros_like(dq_ref)

        @pl.when(ii == 0)
        def _():
            dk_acc[...] = jnp.zeros_like(dk_acc)
            dv_acc[...] = jnp.zeros_like(dv_acc)

        def step(masked):
            k = jnp.concatenate([kn_ref[...], kr_ref[...]], axis=1)
            v = v_ref[...]
            mask = _causal_mask(i, j, tb, tb) if masked else None
            tp = tb // FLASH_BWD_PARTS
            for part in range(FLASH_BWD_PARTS):
                rows = pl.ds(part * tp, tp)
                qp, dop = q_ref[rows, :], do_ref[rows, :]
                p = jnp.exp((_dot(qp, k, NT) - lse_ref[0, rows, :]).astype(bf16))
                if masked:
                    p = jnp.where(mask[part * tp:(part + 1) * tp], p, jnp.zeros_like(p))
                delta = jnp.sum(o_ref[rows, :].astype(f32) * dop, axis=-1, keepdims=True)
                ds = p * (_bdot(dop, v, NT) - delta).astype(bf16)
                dv_acc[...] += _bdot(p, dop, TN)
                dk_acc[...] += _bdot(ds, qp, TN)
                dq_rows = pl.ds(pl.multiple_of(i * tb, tb) + part * tp, tp)
                dq_ref[dq_rows, :] += _bdot(ds, k)

        pl.when(ii > j)(functools.partial(step, False))
        pl.when(ii == j)(functools.partial(step, True))

        @pl.when(ii == nb - 1)
        def _():
            dkv_ref[...] = jnp.concatenate([dk_acc[:, :LANES], dv_acc[...]], axis=1).astype(bf16)
            dkr_ref[...] = dk_acc[:, LANES:]

    qi = lambda h, j, i: jnp.maximum(i, j)
    kblk = lambda off: pl.BlockSpec((tb, LANES), lambda h, j, i: (j, 2 * h + off))
    vec = pl.BlockSpec((1, tb, 1), lambda h, j, i: (h, qi(h, j, i), 0))
    qblk = pl.BlockSpec((tb, LANES), lambda h, j, i: (qi(h, j, i), h))
    return pl.pallas_call(
        body, name="flash_bwd", grid=(H, nb, nb),
        in_specs=[pl.BlockSpec((tb, HQ), lambda h, j, i: (qi(h, j, i), h)), kblk(0), kblk(1),
                  pl.BlockSpec((tb, LANES), lambda h, j, i: (j, 0)), qblk, qblk, vec],
        out_specs=[pl.BlockSpec((tb, HQ), lambda h, j, i: (j, h)), pl.BlockSpec((tb, LANES), lambda h, j, i: (j, h)),
                   pl.BlockSpec((T, HQ), lambda h, j, i: (0, h))],
        out_shape=[jax.ShapeDtypeStruct((T, H * HQ), bf16), jax.ShapeDtypeStruct((T, H * LANES), f32),
                   jax.ShapeDtypeStruct((T, H * HQ), f32)],
        scratch_shapes=[pltpu.VMEM((tb, HQ), f32), pltpu.VMEM((tb, LANES), f32)],
        compiler_params=_cp("parallel", "arbitrary", "arbitrary"),
    )(q, kv, kv, kr, o, do, lse)


HBM_SPEC = pl.BlockSpec(memory_space=pltpu.HBM)
N_CHIPS = 4
N_DEV = 8

BIG = {"even_w_in": 1, "s5_w_glu": 0, "even_w_out": 0, "odd_w_in": 0, "mla_w_uq": 1, "mla_w_ukv": 1, "odd_w_out": 0,
       "ffn_w_in": 2, "ffn_w_out": 1}
LAYERED = ("ffn_w_in", "ffn_w_out")
GROUPS = {"even_in": ("even_w_in",), "even_rest": ("s5_w_glu", "even_w_out"), "ffn0": LAYERED,
          "odd": ("odd_w_in", "mla_w_uq", "mla_w_ukv", "odd_w_out"), "ffn1": LAYERED}
GROUP_LAYER = {"ffn0": 0, "ffn1": 1}


def _place():
    x, y, c = lax.axis_index("x"), lax.axis_index("y"), lax.axis_index("c")
    chips = [(1 - x, y), (x, 1 - y), (1 - x, 1 - y)]
    return x, y, c, chips


def _slab(ref, axis, k, size):
    start = pl.multiple_of(k * size, size if axis == 0 else LANES)
    idx = [slice(None)] * len(ref.shape)
    idx[axis] = pl.ds(start, size)
    return ref.at[tuple(idx)]


SEM_SPEC = pl.BlockSpec(memory_space=pltpu.SEMAPHORE)
ANY_SPEC = pl.BlockSpec(memory_space=pl.ANY)
EFFECT = pltpu.SideEffectType.DATAFLOW_SIDE_EFFECTING


def _hbm(a):
    return pltpu.with_memory_space_constraint(a, pltpu.HBM)


class _Gather:
    copies = 3

    def __init__(self, axis, size):
        self.axis, self.size = axis, size

    def view(self, land, kk):
        return _slab(land, self.axis, kk, self.size)

    def own(self, land, place):
        return self.view(land, 2 * place[0] + place[1])

    def sends(self, src, land, place):
        x, y, c, chips = place
        return [(self.own(land, place) if src is None else src, self.own(land, place), (*chip, c)) for chip in chips]

    def recvs(self, land, place):
        return [self.view(land, 2 * cx + cy) for cx, cy in place[3]]


class _Scatter:
    copies = 3

    def __init__(self, axis, size, layer=None):
        self.axis, self.size, self.layer = axis, size, layer

    def row(self, land, j):
        return land.at[j] if self.layer is None else land.at[j, self.layer]

    def sends(self, src, land, place):
        c, chips = place[2], place[3]
        return [(_slab(src, self.axis, 2 * cx + cy, self.size), self.row(land, j), (cx, cy, c))
                for j, (cx, cy) in enumerate(chips)]

    def recvs(self, land, place):
        return [self.row(land, j) for j in range(3)]


class _ToAll:
    copies = N_DEV - 1

    def __init__(self, size):
        self.size = size

    def sends(self, src, land, place):
        x, y, c, _ = place
        flip = lambda v, bit: 1 - v if bit else v
        own = _slab(land, 0, 4 * x + 2 * y + c, self.size)
        return [(own, own, (flip(x, m & 4), flip(y, m & 2), flip(c, m & 1))) for m in range(1, N_DEV)]

    def recvs(self, land, place):
        x, y, c, _ = place
        d = 4 * x + 2 * y + c
        return [_slab(land, 0, d ^ m, self.size) for m in range(1, N_DEV)]


def _unique(arrays):
    out, index = [], {}
    for a in arrays:
        if a is not None and id(a) not in index:
            index[id(a)] = len(out)
            out.append(a)
    return out, index


def _sem_base(routes):
    base = [0]
    for r in routes:
        base.append(base[-1] + r.copies)
    return base


def _push_start(name, items):
    n = len(items)
    base = _sem_base([it[0] for it in items])
    arrays, index = _unique([it[1] for it in items] + [it[2] for it in items])
    na = len(arrays)

    def body(*refs):
        arr, send, recv, token = refs[:na], refs[na], refs[na + 1], refs[-1]
        place = _place()
        for i, (route, src, land) in enumerate(items):
            s_ref = None if src is None else arr[index[id(src)]]
            for j, (s, d, dev) in enumerate(route.sends(s_ref, arr[index[id(land)]], place)):
                pltpu.make_async_remote_copy(src_ref=s, dst_ref=d, send_sem=send.at[base[i] + j], recv_sem=recv.at[base[i] + j],
                                             device_id=dev, device_id_type=MESH).start()
        token[...] = jnp.zeros_like(token)

    res = pl.pallas_call(
        body, name=name,
        out_shape=[pltpu.SemaphoreType.DMA((base[-1],)), pltpu.SemaphoreType.DMA((base[-1],))]
        + [pltpu.HBM(a.shape, a.dtype) for a in arrays] + [jax.ShapeDtypeStruct((SUBLANES, LANES), f32)],
        in_specs=[HBM_SPEC] * na, out_specs=[SEM_SPEC, SEM_SPEC] + [HBM_SPEC] * na + [pl.BlockSpec(memory_space=pltpu.VMEM)],
        input_output_aliases={i: 2 + i for i in range(na)},
        compiler_params=pltpu.CompilerParams(has_side_effects=EFFECT),
    )(*[_hbm(a) for a in arrays])
    thru = lambda a: None if a is None else res[2 + index[id(a)]]
    return (res[0], res[1]), [thru(it[1]) for it in items], [thru(it[2]) for it in items], res[-1]


def _push_wait(name, groups, after):
    arrays, index = _unique([a for _, _, srcs, lands in groups for a in list(srcs) + list(lands)])
    na, ng = len(arrays), len(groups)

    def body(*refs):
        arr, sems = refs[:na], refs[na:na + 2 * ng]
        place = _place()
        for g, (routes, _, srcs, lands) in enumerate(groups):
            send, recv = sems[2 * g], sems[2 * g + 1]
            base = _sem_base(routes)
            for i, route in enumerate(routes):
                src, land = None if srcs[i] is None else arr[index[id(srcs[i])]], arr[index[id(lands[i])]]
                for j, ((s, d, dev), mine) in enumerate(zip(route.sends(src, land, place), route.recvs(land, place))):
                    cp = pltpu.make_async_remote_copy(src_ref=s, dst_ref=mine, send_sem=send.at[base[i] + j],
                                                      recv_sem=recv.at[base[i] + j], device_id=dev,
                                                      device_id_type=MESH)
                    cp.wait_send()
                    cp.wait_recv()

    sem_args = [s for g in groups for s in g[1]]
    res = pl.pallas_call(
        body, name=name, out_shape=[pltpu.HBM(a.shape, a.dtype) for a in arrays],
        in_specs=[HBM_SPEC] * na + [SEM_SPEC] * (2 * ng) + [ANY_SPEC] * len(after), out_specs=[HBM_SPEC] * na,
        input_output_aliases={i: i for i in range(na)},
        compiler_params=pltpu.CompilerParams(has_side_effects=EFFECT),
    )(*arrays, *sem_args, *after)
    return [[res[index[id(a)]] for a in g[3]] for g in groups]


def _place_slab(block, axis, slabs, idx, dtype, *, name):
    R, C = block.shape
    tm = _rows(R, C)
    nr = R // tm
    out_map = (lambda i, k: (i, k[0])) if axis == 1 else (lambda i, k: (k[0] * nr + i, 0))

    def body(k_ref, x_ref, o_ref):
        o_ref[...] = x_ref[...].astype(dtype)

    full = (R, C * slabs) if axis == 1 else (R * slabs, C)
    return pl.pallas_call(
        body, name=name, out_shape=jax.ShapeDtypeStruct(full, dtype),
        grid_spec=pltpu.PrefetchScalarGridSpec(
            num_scalar_prefetch=1, grid=(nr,), in_specs=[pl.BlockSpec((tm, C), lambda i, k: (i, 0))],
            out_specs=pl.BlockSpec((tm, C), out_map)),
        compiler_params=_cp("parallel"),
    )(idx, block)


def _swap_with_sibling(parts, tag):
    names = list(parts)

    def body(*refs):
        n = len(names)
        ins, outs, send, recv = refs[:n], refs[n:2 * n], refs[-2], refs[-1]
        x, y, c, _ = _place()
        cps = [pltpu.make_async_remote_copy(src_ref=ins[a], dst_ref=outs[a], send_sem=send.at[a], recv_sem=recv.at[a],
                                            device_id=(x, y, 1 - c), device_id_type=MESH) for a in range(n)]
        for cp in cps:
            cp.start()
        for cp in cps:
            cp.wait_recv()
        for cp in cps:
            cp.wait_send()

    res = pl.pallas_call(
        body, name=f"swap_with_sibling_{tag}", in_specs=[HBM_SPEC] * len(names), out_specs=[HBM_SPEC] * len(names),
        out_shape=[jax.ShapeDtypeStruct(parts[n].shape, parts[n].dtype) for n in names],
        scratch_shapes=[pltpu.SemaphoreType.DMA((len(names),)), pltpu.SemaphoreType.DMA((len(names),))],
    )(*[parts[n] for n in names])
    return dict(zip(names, res))


ELEMENTWISE_BLOCK_BYTES = 1 << 20


def _rows(r, c):
    for t in (512, 256, 128, 64, 32, 16, 8):
        if r % t == 0 and t * c * 4 <= ELEMENTWISE_BLOCK_BYTES:
            return t
    return r


def _sum4(owns, axis, recv, kidx, *, name):
    L = len(owns)
    R, C = recv.shape[2:]
    tm = _rows(R, C)
    nr = R // tm

    def body(k_ref, *refs):
        own_refs, r_ref, out_ref = refs[:L], refs[L], refs[L + 1]
        for li in range(L):
            @pl.when(pl.program_id(0) == li)
            def _(o_ref=own_refs[li]):
                out_ref[...] = ((o_ref[...] + r_ref[0, 0].astype(f32)) + r_ref[1, 0].astype(f32)) + r_ref[2, 0].astype(f32)

    own_map = (lambda l, i, k: (i, k[0])) if axis == 1 else (lambda l, i, k: (k[0] * nr + i, 0))
    return pl.pallas_call(
        body, name=name, out_shape=jax.ShapeDtypeStruct((L * R, C), f32),
        grid_spec=pltpu.PrefetchScalarGridSpec(
            num_scalar_prefetch=1, grid=(L, nr),
            in_specs=[pl.BlockSpec((tm, C), own_map)] * L + [pl.BlockSpec((3, 1, tm, C), lambda l, i, k: (0, l, i, 0))],
            out_specs=pl.BlockSpec((tm, C), lambda l, i, k: (l * nr + i, 0))),
        compiler_params=_cp("parallel", "parallel"),
    )(kidx, *owns, recv)


def _adamw(w, m, v, parts, *, name):
    R, C = w.shape
    tm = _rows(R, C)
    npart = len(parts)

    def body(*refs):
        w_ref, m_ref, v_ref = refs[:3]
        g_ref, d_ref, m2_ref, v2_ref = refs[3 + npart:]
        g = refs[3][...]
        for p_ref in refs[4:3 + npart]:
            g = g + p_ref[...]
        g_ref[...] = g
        d_ref[...], m2_ref[...], v2_ref[...] = _adam_math(w_ref[...], m_ref[...], v_ref[...], g)

    blk = pl.BlockSpec((tm, C), lambda i: (i, 0))
    return pl.pallas_call(
        body, name=name, grid=(R // tm,),
        in_specs=[blk] * (3 + npart), out_specs=[blk] * 4,
        out_shape=[jax.ShapeDtypeStruct((R, C), f32)] * 4, compiler_params=_cp("parallel"),
    )(w, m, v, *parts)


def _adam_math(w, m, v, g):
    m2 = ADAM_B1 * m + (1.0 - ADAM_B1) * g
    v2 = ADAM_B2 * v + (1.0 - ADAM_B2) * (g * g)
    m_hat = m2 / (1.0 - ADAM_B1 ** ADAM_STEP)
    v_hat = v2 / (1.0 - ADAM_B2 ** ADAM_STEP)
    return -ADAM_LR * (m_hat / (jnp.sqrt(v_hat) + ADAM_EPS) + ADAM_WD * w), m2, v2


def _adamw_small(landed, w, m, v, kidx, ra, rb):
    rs = ra + N_CHIPS * rb

    def body(k_ref, l_ref, w_ref, m_ref, v_ref, g_ref, d_ref, m2_ref, v2_ref):
        mine = pl.multiple_of(ra + k_ref[0] * rb, SUBLANES)
        for lo, n, off in ((0, ra, 0), (ra, rb, mine)):
            g = l_ref[pl.ds(off, n), :]
            for d in range(1, N_DEV):
                g = g + l_ref[pl.ds(d * rs + off, n), :]
            rows = pl.ds(lo, n)
            delta, m2, v2 = _adam_math(w_ref[rows, :], m_ref[rows, :], v_ref[rows, :], g)
            g_ref[rows, :] = g
            d_ref[rows, :] = delta
            m2_ref[rows, :] = m2
            v2_ref[rows, :] = v2

    vmem = pl.BlockSpec(memory_space=pltpu.VMEM)
    return pl.pallas_call(
        body, name="adamw_small", out_shape=[jax.ShapeDtypeStruct(w.shape, f32)] * 4,
        grid_spec=pltpu.PrefetchScalarGridSpec(num_scalar_prefetch=1, grid=(), in_specs=[vmem] * 4, out_specs=[vmem] * 4),
        compiler_params=_cp(),
    )(kidx, landed, w, m, v)


def _pad_odd(w):
    return jnp.pad(w, ((0, 0), (0, ODD_PAD - w.shape[1])))


def _uq_cat(w):
    r = w.shape[0]
    return jnp.pad(w.reshape(r, MLA_HEADS, MLA_QK), ((0, 0), (0, 0), (0, HQ - MLA_QK))).reshape(r, MLA_HEADS * HQ)


def _uq_uncat(w):
    r = w.shape[0]
    return w.reshape(r, MLA_HEADS, HQ)[:, :, :MLA_QK].reshape(r, MLA_HEADS * MLA_QK)


def _to_segments(v):
    T, C = v.shape
    return v.reshape(S5_SEG, T // S5_SEG, C).transpose(1, 0, 2).reshape(T, C)


def _from_segments(v):
    T, C = v.shape
    return v.reshape(T // S5_SEG, S5_SEG, C).transpose(1, 0, 2).reshape(T, C)


def _s5_rb(T):
    return min(512, T)


def _ffn_fwd(h, hn, w_in, cw, cb, w_out, tag, next_g=None):
    au = _mm(hn, w_in, out_dtype=bf16, name=f"ffn{tag}_in", tn=1408)
    z = _ffn_mid_fwd(au, cw, cb, name=f"ffn{tag}_mid")
    return _mm(z, w_out, res=h, norm_g=next_g, name=f"ffn{tag}_out", tk=1408), (hn, au, z)


def _ffn_bwd(h, g, w_in, cw, cb, w_out, saved, dh, tag, dep=None):
    hn, au, z = saved
    dz = _mm(dh, w_out, tb=True, out_dtype=bf16, name=f"ffn{tag}_dz", tn=1408, dep=dep)
    dw_out = _mm(z, dh, ta=True, also_bf16=True, name=f"ffn{tag}_dwout", tm=1408)
    dau, dcw, dcb = _ffn_mid_bwd(au, cw, cb, dz, name=f"ffn{tag}_dmid")
    dh_in, dg = _mm(dau, w_in, tb=True, res=dh, norm_bwd=(h, g), name=f"ffn{tag}_dhn", tk=1408)
    dw_in = _mm(hn, dau, ta=True, also_bf16=True, name=f"ffn{tag}_dwin", tn=1408)
    return dh_in, dg, dw_in, dcw, dcb, dw_out


def _local_step(x, positions, target, get_w, P, put_g):
    T = x.shape[0]
    rb = _s5_rb(T)
    row = lambda v: v.reshape(1, -1)
    g_mix, g_ffn = P["norm_mix_g"], P["norm_ffn_g"]
    lbl, hng = P["hgrn_lb_logits"], P["hgrn_norm_g"]
    dsk, bg = P["s5_d"], P["s5_b_glu"]
    qg, kvg = P["mla_q_norm_g"], P["mla_kv_norm_g"]
    cw, cb = P["ffn_conv_w"], P["ffn_conv_b"]

    col = lambda v: v.reshape(S5_N, 1)
    disc_in = (col(P["s5_a_re"]), col(P["s5_a_im"]), col(jnp.repeat(P["s5_log_dt"].reshape(S5_GROUPS), S5_STATE)),
               P["s5_b_re"].reshape(S5_N, S5_GROUP), P["s5_b_im"].reshape(S5_N, S5_GROUP))
    abr, abi, bbr, bbi = _s5_disc_fwd(*disc_in)
    ar, ai = abr.reshape(1, S5_N), abi.reshape(1, S5_N)
    bbr3, bbi3 = bbr.reshape(S5_GROUPS, S5_STATE, S5_GROUP), bbi.reshape(S5_GROUPS, S5_STATE, S5_GROUP)
    bre, bim = _blockdiag(bbr3, True).astype(bf16), _blockdiag(bbi3, True).astype(bf16)
    bret, bimt = _blockdiag(bbr3).astype(bf16), _blockdiag(bbi3).astype(bf16)
    c_re, c_im = P["s5_c_re"].reshape(S5_GROUPS, S5_GROUP, S5_STATE), P["s5_c_im"].reshape(S5_GROUPS, S5_GROUP, S5_STATE)
    cre, cim = _blockdiag(c_re, True).astype(bf16), _blockdiag(c_im, True).astype(bf16)
    cret, cimt = _blockdiag(c_re).astype(bf16), _blockdiag(c_im).astype(bf16)

    hn0 = _rms_fwd(x, g_mix[0:1], name="mix0_norm")
    We = get_w("even_in", hn0)
    proj_e = _mm(hn0, We["even_w_in"], name="even_in", tn=1280)
    Wr = get_w("even_rest", proj_e)
    ya, states = _hgrn_fwd(proj_e, lbl, hng)
    u_seg = _to_segments(proj_e[:, 4 * 512:])
    fr, fi = _s5_final(u_seg, bre, bim, ar, ai, rb=rb)
    yb_seg, s0r, s0i = _s5_fwd(u_seg, bre, bim, ar, ai, fr, fi, cre, cim, dsk, Wr["s5_w_glu"], bg, rb=rb)
    ycat = jnp.concatenate([ya, _from_segments(yb_seg)], axis=1)
    h1, hnf0 = _mm(ycat, Wr["even_w_out"], res=x, norm_g=g_ffn[0:1], name="even_out")
    Wf0 = get_w("ffn0", h1)
    (h2, hn2), ffn0 = _ffn_fwd(h1, hnf0, Wf0["ffn_w_in"], cw[0], cb[0:1], Wf0["ffn_w_out"], 0, next_g=g_mix[1:2])

    tabs = _rope_tables(positions)
    Wo = get_w("odd", hn2)
    proj_o = _mm(hn2, Wo["odd_w_in"], name="odd_in")
    cqn, ckvn, kr = _mla_prep_fwd(proj_o, qg, kvg, tabs)
    q = _q_post(_mm(cqn, Wo["mla_w_uq"], name="mla_uq"), tabs, transpose=False, name="q_post")
    kvb = _mm(ckvn, Wo["mla_w_ukv"], out_dtype=bf16, name="mla_ukv")
    o, lse = _flash_fwd(q, kvb, kr)
    h3, hnf1 = _mm(o, Wo["odd_w_out"], res=h2, norm_g=g_ffn[1:2], name="odd_out")
    Wf1 = get_w("ffn1", h3)
    h4, ffn1 = _ffn_fwd(h3, hnf1, Wf1["ffn_w_in"], cw[1], cb[1:2], Wf1["ffn_w_out"], 1)
    loss, dh4, dg_final = _loss_head(h4, row(P["final_norm_g"]), target)

    dh3, dg_ffn1, dw_fin1, dcw1, dcb1, dw_fout1 = _ffn_bwd(
        h3, g_ffn[1:2], Wf1["ffn_w_in"], cw[1], cb[1:2], Wf1["ffn_w_out"], ffn1, dh4, 1)
    sent = put_g("ffn1", {"ffn_w_in": dw_fin1, "ffn_w_out": dw_fout1})
    do = _mm(dh3, Wo["odd_w_out"], tb=True, out_dtype=bf16, name="odd_do", dep=sent)
    dw_oout = _mm(o, dh3, ta=True, also_bf16=True, name="odd_dwout")
    dkv, dkr_h, dq = _flash_bwd(q, kvb, kr, o, do, lse)
    dq = _q_post(dq, tabs, transpose=True, name="dq_post")
    dw_uq = _mm(cqn, dq, ta=True, also_bf16=True, name="mla_dwuq")
    dcqn = _mm(dq, Wo["mla_w_uq"], tb=True, name="mla_dcq")
    dw_ukv = _mm(ckvn, dkv, ta=True, also_bf16=True, name="mla_dwukv")
    dckvn = _mm(dkv, Wo["mla_w_ukv"], tb=True, name="mla_dckv")
    dproj_o, dqg, dkvg = _mla_prep_bwd(proj_o, qg, kvg, tabs, dcqn, dckvn, dkr_h)
    dw_oin = _mm(hn2, dproj_o, ta=True, also_bf16=True, name="odd_dwin")
    sent = put_g("odd", {"odd_w_in": dw_oin, "mla_w_uq": dw_uq, "mla_w_ukv": dw_ukv, "odd_w_out": dw_oout})
    dh2, dg_mix1 = _mm(dproj_o, Wo["odd_w_in"], tb=True, res=dh3, norm_bwd=(h2, g_mix[1:2]), name="odd_dhn")

    dh1, dg_ffn0, dw_fin0, dcw0, dcb0, dw_fout0 = _ffn_bwd(
        h1, g_ffn[0:1], Wf0["ffn_w_in"], cw[0], cb[0:1], Wf0["ffn_w_out"], ffn0, dh2, 0, dep=sent)
    sent = put_g("ffn0", {"ffn_w_in": dw_fin0, "ffn_w_out": dw_fout0})
    dycat = _mm(dh1, Wr["even_w_out"], tb=True, name="even_dy", dep=sent)
    dw_eout = _mm(ycat, dh1, ta=True, also_bf16=True, name="even_dwout")
    dq_h, df_h, di_h, dg_h, dlbl, dhng = _hgrn_bwd(proj_e, lbl, hng, states, dycat)
    dyb_seg = _to_segments(dycat[:, 512:])
    dy_s5, glr, gli, dcre, dcim, dd, dwg, dbg = _s5_bwd_a(
        u_seg, bre, bim, ar, ai, s0r, s0i, cre, cim, cret, cimt, dsk, Wr["s5_w_glu"], bg, dyb_seg, rb=rb)
    du_seg, dbre, dbim, dar, dai = _s5_bwd_b(
        u_seg, bre, bim, bret, bimt, ar, ai, s0r, s0i, glr, gli, cret, cimt, dsk, dy_s5, rb=rb)
    dproj_e = jnp.concatenate([dq_h, df_h, di_h, dg_h, _from_segments(du_seg)], axis=1)
    dx, dg_mix0 = _mm(dproj_e, We["even_w_in"], tb=True, res=dh1, norm_bwd=(x, g_mix[0:1]), name="even_dhn", tk=1280)
    dw_ein = _mm(hn0, dproj_e, ta=True, also_bf16=True, name="even_dwin", tn=1280)

    unblk = lambda m, a, b: jnp.swapaxes(_blockdiag_t(m, a, b), 1, 2)
    dbbr = unblk(dbre, S5_GROUP, S5_STATE).reshape(S5_N, S5_GROUP)
    dbbi = unblk(dbim, S5_GROUP, S5_STATE).reshape(S5_N, S5_GROUP)
    d_ar, d_ai, d_ldt, d_br, d_bi = _s5_disc_bwd(*disc_in, (dar.reshape(S5_N, 1), dai.reshape(S5_N, 1), dbbr, dbbi))
    small = {
        "norm_mix_g": jnp.concatenate([dg_mix0, dg_mix1], axis=0),
        "norm_ffn_g": jnp.concatenate([dg_ffn0, dg_ffn1], axis=0),
        "final_norm_g": dg_final.reshape(-1),
        "hgrn_lb_logits": dlbl, "hgrn_norm_g": dhng,
        "s5_a_re": d_ar.reshape(1, S5_GROUPS, S5_STATE), "s5_a_im": d_ai.reshape(1, S5_GROUPS, S5_STATE),
        "s5_log_dt": d_ldt.reshape(S5_GROUPS, S5_STATE).sum(axis=1).reshape(1, S5_GROUPS),
        "s5_b_re": d_br.reshape(1, S5_GROUPS, S5_STATE, S5_GROUP), "s5_b_im": d_bi.reshape(1, S5_GROUPS, S5_STATE, S5_GROUP),
        "s5_c_re": unblk(dcre, S5_STATE, S5_GROUP).reshape(1, S5_GROUPS, S5_GROUP, S5_STATE),
        "s5_c_im": unblk(dcim, S5_STATE, S5_GROUP).reshape(1, S5_GROUPS, S5_GROUP, S5_STATE),
        "s5_d": dd, "s5_b_glu": dbg, "mla_q_norm_g": dqg, "mla_kv_norm_g": dkvg,
        "ffn_conv_w": jnp.stack([dcw0, dcw1]), "ffn_conv_b": jnp.concatenate([dcb0, dcb1], axis=0),
    }
    put_g("even", {"even_w_in": dw_ein, "s5_w_glu": (dwg, dwg.astype(bf16)), "even_w_out": dw_eout}, small)
    return loss, dx


WEIGHTS = ["norm_mix_g", "norm_ffn_g", "final_norm_g", "even_w_in", "hgrn_lb_logits", "hgrn_norm_g", "s5_a_re", "s5_a_im",
           "s5_log_dt", "s5_b_re", "s5_b_im", "s5_c_re", "s5_c_im", "s5_d", "s5_w_glu", "s5_b_glu", "even_w_out", "odd_w_in",
           "mla_q_norm_g", "mla_w_uq", "mla_kv_norm_g", "mla_w_ukv", "odd_w_out", "ffn_w_in", "ffn_conv_w", "ffn_conv_b",
           "ffn_w_out"]
SMALL_SHARDED = {"mla_q_norm_g": 1, "mla_kv_norm_g": 1, "ffn_conv_w": 2}
SMALL = [n for n in WEIGHTS if n not in BIG]
SMALL_REP = [n for n in SMALL if n not in SMALL_SHARDED]


def _pack_rows(shapes):
    n = sum(math.prod(s) for s in shapes)
    return -(-n // (SUBLANES * LANES)) * SUBLANES


def _pack(arrays, rows):
    flat = jnp.concatenate([a.reshape(-1) for a in arrays])
    return jnp.pad(flat, (0, rows * LANES - flat.shape[0])).reshape(rows, LANES)


def _unpack(block, shapes):
    flat, out, off = block.reshape(-1), [], 0
    for s in shapes:
        n = math.prod(s)
        out.append(flat[off:off + n].reshape(s))
        off += n
    return out


def kernel(x, positions, norm_mix_g, norm_ffn_g, final_norm_g, even_w_in, hgrn_lb_logits, hgrn_norm_g, s5_a_re, s5_a_im, s5_log_dt, s5_b_re, s5_b_im, s5_c_re, s5_c_im, s5_d, s5_w_glu, s5_b_glu, even_w_out, odd_w_in, mla_q_norm_g, mla_w_uq, mla_kv_norm_g, mla_w_ukv, odd_w_out, ffn_w_in, ffn_conv_w, ffn_conv_b, ffn_w_out, loss_target, m_norm_mix_g, m_norm_ffn_g, m_final_norm_g, m_even_w_in, m_hgrn_lb_logits, m_hgrn_norm_g, m_s5_a_re, m_s5_a_im, m_s5_log_dt, m_s5_b_re, m_s5_b_im, m_s5_c_re, m_s5_c_im, m_s5_d, m_s5_w_glu, m_s5_b_glu, m_even_w_out, m_odd_w_in, m_mla_q_norm_g, m_mla_w_uq, m_mla_kv_norm_g, m_mla_w_ukv, m_odd_w_out, m_ffn_w_in, m_ffn_conv_w, m_ffn_conv_b, m_ffn_w_out, v_norm_mix_g, v_norm_ffn_g, v_final_norm_g, v_even_w_in, v_hgrn_lb_logits, v_hgrn_norm_g, v_s5_a_re, v_s5_a_im, v_s5_log_dt, v_s5_b_re, v_s5_b_im, v_s5_c_re, v_s5_c_im, v_s5_d, v_s5_w_glu, v_s5_b_glu, v_even_w_out, v_odd_w_in, v_mla_q_norm_g, v_mla_w_uq, v_mla_kv_norm_g, v_mla_w_ukv, v_odd_w_out, v_ffn_w_in, v_ffn_conv_w, v_ffn_conv_b, v_ffn_w_out):
    args = dict(locals())
    w = {n: args[n] for n in WEIGHTS}
    m = {n: args["m_" + n] for n in WEIGHTS}
    v = {n: args["v_" + n] for n in WEIGHTS}
    k = 2 * lax.axis_index("x") + lax.axis_index("y")
    kidx = k.reshape(1).astype(jnp.int32)
    axis2d = lambda n: BIG[n] - (1 if n in LAYERED else 0)
    slab = lambda n: w[n].shape[1 + axis2d(n)]

    small_sh_shapes = [w[n].shape for n in SMALL_SHARDED]
    rb = _pack_rows(small_sh_shapes)
    items = {}
    for group, names in GROUPS.items():
        layer = GROUP_LAYER.get(group, 0)
        items[group] = [(_Gather(axis2d(n), slab(n)), None,
                         _place_slab(w[n][layer], axis2d(n), N_CHIPS, kidx, bf16, name=f"place_{n}_{layer}")) for n in names]
    items["even_in"].append((_Gather(0, rb), None,
                             _place_slab(_pack([w[n] for n in SMALL_SHARDED], rb), 0, N_CHIPS, kidx, f32, name="place_small")))
    gathers, tokens = {}, []
    for group in GROUPS:
        sems, srcs, lands, token = _push_start(f"gather_start_{group}", items[group])
        gathers[group] = ([it[0] for it in items[group]], sems, srcs, lands)
        tokens.append(token[0, 0])
    started = functools.reduce(jnp.add, tokens)

    def landed(group, after):
        return _push_wait(f"gather_wait_{group}", [gathers[group]], [after])[0]

    even = landed("even_in", (started + norm_mix_g[0, 0]).reshape(1))
    per_chip = [_unpack(even[-1][c * rb:(c + 1) * rb], small_sh_shapes) for c in range(N_CHIPS)]
    P = {n: w[n] for n in SMALL_REP}
    for i, (n, ax) in enumerate(SMALL_SHARDED.items()):
        P[n] = jnp.concatenate([per_chip[c][i] for c in range(N_CHIPS)], axis=ax)
    P["mla_q_norm_g"], P["mla_kv_norm_g"] = P["mla_q_norm_g"].reshape(1, -1), P["mla_kv_norm_g"].reshape(1, -1)
    fix_w = {"odd_w_in": _pad_odd, "mla_w_uq": _uq_cat}

    def get_w(group, after):
        full = even if group == "even_in" else landed(group, after)
        return {n: fix_w.get(n, lambda a: a)(a) for n, a in zip(GROUPS[group], full)}

    fix_g = {"odd_w_in": lambda g: g[:, :odd_w_in.shape[2]], "mla_w_uq": _uq_uncat}
    g32, scatters, land_now = {}, {}, {}
    ra = _pack_rows([w[n].shape for n in SMALL_REP])
    rs = ra + N_CHIPS * rb
    didx = (2 * kidx + lax.axis_index("c")).astype(jnp.int32)

    def put_g(group, grads, small=None):
        layer = GROUP_LAYER.get(group)
        routes, srcs, names = [], [], list(grads)
        for n in names:
            f = fix_g.get(n, lambda g: g)
            g32.setdefault(n, {})[layer or 0] = f(grads[n][0])
            routes.append(_Scatter(axis2d(n), slab(n), layer if n in LAYERED else None))
            srcs.append(f(grads[n][1]))
            if n not in land_now:
                land_now[n] = lax.empty((3,) + w[n].shape[0 if n in LAYERED else 1:], bf16)
        if small is not None:
            blocks = [_pack([small[n] for n in SMALL_REP], ra)]
            for chip in range(N_CHIPS):
                sl = lambda n, ax: lax.slice_in_dim(small[n].reshape(w[n].shape[:ax] + (-1,) + w[n].shape[ax + 1:]),
                                                    chip * w[n].shape[ax], (chip + 1) * w[n].shape[ax], axis=ax)
                blocks.append(_pack([sl(n, ax) for n, ax in SMALL_SHARDED.items()], rb))
            names.append("small")
            routes.append(_ToAll(rs))
            srcs.append(None)
            land_now["small"] = _place_slab(jnp.concatenate(blocks), 0, N_DEV, didx, f32, name="place_small_grads")
        sems, srcs, lands, token = _push_start(f"scatter_start_{group}", [(r, s, land_now[n]) for r, s, n in zip(routes, srcs, names)])
        land_now.update(zip(names, lands))
        scatters[group] = (routes, sems, srcs, names)
        sent.append(token)
        return token

    sent = []
    loss, dx = _local_step(x[0], positions[0], loss_target[0], get_w, P, put_g)
    sent_last = sent[-1]
    loss = lax.psum(loss[0, 0], ("x", "y", "c"))

    out = {}

    def finish(tag, groups, after):
        waits = [(scatters[g][0], scatters[g][1], scatters[g][2], [land_now[n] for n in scatters[g][3]]) for g in groups]
        for g, lands in zip(groups, _push_wait(f"scatter_wait_{tag}", waits, after)):
            land_now.update(zip(scatters[g][3], lands))
        names = [n for n in dict.fromkeys(n for g in groups for n in scatters[g][3]) if n != "small"]
        part = {}
        for n in names:
            recv = land_now[n] if n in LAYERED else land_now[n][:, None]
            part[n] = _sum4([g32[n][l] for l in sorted(g32[n])], axis2d(n), recv, kidx, name=f"sum4_{n}")
        other = _swap_with_sibling(part, tag)
        done = []
        for n in names:
            C = part[n].shape[-1]
            res = _adamw(w[n].reshape(-1, C), m[n].reshape(-1, C), v[n].reshape(-1, C), [part[n], other[n]], name=f"adamw_{n}")
            out[n] = [r.reshape(w[n].shape) for r in res]
            done.append(res[0])
        return done

    done = finish("a", ["ffn1", "odd", "ffn0"], [dx, sent_last])
    finish("b", ["even"], done)

    order = SMALL_REP + list(SMALL_SHARDED)
    packed = lambda src: jnp.concatenate([_pack([src[n] for n in SMALL_REP], ra), _pack([src[n] for n in SMALL_SHARDED], rb)])
    res = _adamw_small(land_now["small"], packed(w), packed(m), packed(v), kidx, ra, rb)
    for r in res:
        parts = _unpack(r[:ra], [w[n].shape for n in SMALL_REP]) + _unpack(r[ra:], small_sh_shapes)
        for n, a in zip(order, parts):
            out.setdefault(n, []).append(a)

    return (loss, dx[None], *[out[n][0] for n in WEIGHTS], *[out[n][1] for n in WEIGHTS],
            *[out[n][2] for n in WEIGHTS], *[out[n][3] for n in WEIGHTS])
```

```python
import functools
import math

import jax
import jax.numpy as jnp
from jax import lax
from jax.experimental import pallas as pl
from jax.experimental.pallas import tpu as pltpu

f32, bf16 = jnp.float32, jnp.bfloat16
EPS = 1e-6
LANES = 128
SUBLANES = 8
VMEM_BYTES = 48 * 1024 * 1024
HGRN_CHUNK = 64
HGRN_HEADS = 4
S5_GROUPS, S5_STATE, S5_GROUP = 32, 64, 16
S5_N = S5_GROUPS * S5_STATE
S5_SEG = SUBLANES
MLA_HEADS, MLA_NOPE, MLA_ROPE, MLA_V = 8, 128, 64, 128
MLA_QK = MLA_NOPE + MLA_ROPE
MLA_Q_RANK, MLA_KV_RANK = 384, 256
ROPE_THETA = 10000.0
D_FF = 2816
ADAM_LR, ADAM_B1, ADAM_B2, ADAM_EPS, ADAM_WD, ADAM_STEP = 0.001, 0.9, 0.999, 1e-08, 0.01, 10
MESH = pl.DeviceIdType.MESH
HI = lax.Precision.HIGHEST


def _cp(*dims):
    return pltpu.CompilerParams(dimension_semantics=dims if dims else None, vmem_limit_bytes=VMEM_BYTES)


def _tile(n, t):
    if n <= t:
        return n
    c = (t // LANES) * LANES
    while c >= LANES:
        if n % c == 0:
            return c
        c -= LANES
    return n


def _dot(a, b, dn=None, precision=None):
    if dn is None:
        dn = (((a.ndim - 1,), (0,)), ((), ()))
    return lax.dot_general(a, b, dn, preferred_element_type=f32, precision=precision)


NT = (((1,), (1,)), ((), ()))
TN = (((0,), (0,)), ((), ()))


def _bdot(a, b, dn=None):
    return _dot(a.astype(bf16), b.astype(bf16), dn)


MM_PARTS = 2


def _mm(a, b, *, name, ta=False, tb=False, out_dtype=f32, res=None, also_bf16=False, tm=1024, tn=1024, tk=1024, dep=None,
        norm_g=None, norm_bwd=None):
    halves = lambda s: (s[1], 2 * s[2]) if len(s) == 3 else s
    M, K = (a.shape[1], a.shape[0]) if ta else halves(a.shape)
    N = b.shape[0] if tb else halves(b.shape)[1]
    rows = norm_g is not None or norm_bwd is not None
    if rows:
        tm, tn = min(tm, 512), N
    tm, tn, tk = _tile(M, tm), _tile(N, tn), _tile(K, tk)
    if a.ndim == 3:
        tk = _tile(K // 2, tk)
    if b.ndim == 3:
        tn = _tile(N // 2, tn)
    nk = K // tk
    parts = MM_PARTS if tm % (MM_PARTS * LANES) == 0 else 1
    dn = (((0 if ta else 1,), (1 if tb else 0,)), ((), ()))
    extra = [] if norm_bwd is None else list(norm_bwd)
    if norm_g is not None:
        extra.append(norm_g)

    def body(*refs):
        a_ref, b_ref = refs[0], refs[1]
        r_ref = refs[2] if res is not None else None
        nin = 2 + (res is not None) + (dep is not None) + len(extra)
        ex = refs[nin - len(extra):nin]
        outs = refs[nin:-1] if nk > 1 else refs[nin:]
        acc = refs[-1] if nk > 1 else None
        k = pl.program_id(2)
        b_blk = b_ref[...]
        if nk > 1:
            @pl.when(k == 0)
            def _():
                acc[...] = jnp.zeros_like(acc)

        groups = []
        for part in range(parts):
            rows = pl.ds(part * (tm // parts), tm // parts)
            p = _bdot(a_ref[:, rows] if ta else a_ref[rows, :], b_blk, dn)
            if nk > 1:
                acc[rows, :] += p
            groups.append((rows, p))

        def epilogue():
            for part, (rows, p) in enumerate(groups):
                r = acc[rows, :] if nk > 1 else p
                if norm_bwd is not None:
                    r, dg = _rms_bwd_math(ex[0][rows, :], ex[1][...], r)
                    if part == 0:
                        @pl.when(pl.program_id(0) == 0)
                        def _(dg=dg):
                            outs[1][...] = dg

                    @pl.when((pl.program_id(0) > 0) | (part > 0))
                    def _(dg=dg):
                        outs[1][...] += dg
                if r_ref is not None:
                    r = r + r_ref[rows, :]
                outs[0][rows, :] = r.astype(out_dtype)
                if also_bf16:
                    outs[1][rows, :] = r.astype(bf16)
                if norm_g is not None:
                    outs[1][rows, :] = _rms(r, ex[-1][...]).astype(bf16)

        if nk > 1:
            pl.when(k == nk - 1)(epilogue)
        else:
            epilogue()

    a_spec = pl.BlockSpec((tk, tm), lambda i, j, k: (k, i)) if ta else pl.BlockSpec((tm, tk), lambda i, j, k: (i, k))
    b_spec = pl.BlockSpec((tn, tk), lambda i, j, k: (j, k)) if tb else pl.BlockSpec((tk, tn), lambda i, j, k: (k, j))
    if a.ndim == 3:
        kh = K // 2 // tk
        a_spec = pl.BlockSpec((None, tm, tk), lambda i, j, k: (k // kh, i, k % kh))
    if b.ndim == 3:
        nh = N // 2 // tn
        b_spec = pl.BlockSpec((None, tk, tn), lambda i, j, k: (j // nh, k, j % nh))
    o_spec = pl.BlockSpec((tm, tn), lambda i, j, k: (i, j))
    in_specs, args = [a_spec, b_spec], [a, b]
    if res is not None:
        in_specs.append(o_spec)
        args.append(res)
    if dep is not None:
        in_specs.append(pl.BlockSpec(memory_space=pl.ANY))
        args.append(dep)
    vec = pl.BlockSpec((1, tn), lambda i, j, k: (0, j))
    if norm_bwd is not None:
        in_specs += [o_spec, vec]
    if norm_g is not None:
        in_specs.append(vec)
    args += extra
    out_shape = [jax.ShapeDtypeStruct((M, N), out_dtype)]
    out_specs = [o_spec]
    if also_bf16 or norm_g is not None:
        out_shape.append(jax.ShapeDtypeStruct((M, N), bf16))
        out_specs.append(o_spec)
    if norm_bwd is not None:
        out_shape.append(jax.ShapeDtypeStruct((1, N), f32))
        out_specs.append(vec)
    dims = ("arbitrary" if norm_bwd is not None else "parallel", "parallel", "arbitrary")
    out = pl.pallas_call(
        body, name=name, grid=(M // tm, N // tn, nk), in_specs=in_specs, out_specs=out_specs, out_shape=out_shape,
        scratch_shapes=[pltpu.VMEM((tm, tn), f32)] if nk > 1 else [], compiler_params=_cp(*dims),
    )(*args)
    return out if len(out) > 1 else out[0]


def _rms_fwd(x, g, *, name, col=0, width=None, tm=512):
    T = x.shape[0]
    width = x.shape[1] if width is None else width
    tm = _tile(T, tm)

    def body(x_ref, g_ref, o_ref):
        xv = x_ref[...]
        r = lax.rsqrt(jnp.mean(xv * xv, axis=-1, keepdims=True) + EPS)
        o_ref[...] = (xv * r * g_ref[...]).astype(bf16)

    return pl.pallas_call(
        body, name=name, grid=(T // tm,),
        in_specs=[pl.BlockSpec((tm, width), lambda i: (i, col)), pl.BlockSpec((1, width), lambda i: (0, 0))],
        out_specs=pl.BlockSpec((tm, width), lambda i: (i, 0)), out_shape=jax.ShapeDtypeStruct((T, width), bf16),
        compiler_params=_cp("parallel"),
    )(x, g)


def _rms_bwd_math(xv, g, dy):
    r = lax.rsqrt(jnp.mean(xv * xv, axis=-1, keepdims=True) + EPS)
    xh = xv * r
    dxh = dy * g
    dx = r * (dxh - xh * jnp.mean(dxh * xh, axis=-1, keepdims=True))
    dg = jnp.sum(dy * xh, axis=0, keepdims=True)
    return dx, dg


def _loss_head(h, g, target, *, tm=512):
    T, D = h.shape
    tm = _tile(T, tm)

    def body(h_ref, g_ref, t_ref, loss_ref, dh_ref, dg_ref):
        hv, gv = h_ref[...], g_ref[...]
        r = lax.rsqrt(jnp.mean(hv * hv, axis=-1, keepdims=True) + EPS)
        e = hv * r * gv - t_ref[...]
        part = 0.5 * jnp.sum(jnp.mean(e * e, axis=-1, keepdims=True), axis=0, keepdims=True)
        dx, dg = _rms_bwd_math(hv, gv, e * (1.0 / D))
        dh_ref[...] = dx

        @pl.when(pl.program_id(0) == 0)
        def _():
            loss_ref[...] = part
            dg_ref[...] = dg

        @pl.when(pl.program_id(0) > 0)
        def _():
            loss_ref[...] += part
            dg_ref[...] += dg

    row = pl.BlockSpec((tm, D), lambda i: (i, 0))
    vec = pl.BlockSpec((1, D), lambda i: (0, 0))
    return pl.pallas_call(
        body, name="loss_head", grid=(T // tm,), in_specs=[row, vec, row],
        out_specs=[pl.BlockSpec((1, 1), lambda i: (0, 0)), row, vec],
        out_shape=[jax.ShapeDtypeStruct((1, 1), f32), jax.ShapeDtypeStruct((T, D), f32), jax.ShapeDtypeStruct((1, D), f32)],
        compiler_params=_cp("arbitrary"),
    )(h, g, target)


FFN_W = 2 * LANES
FFN_ROWS = 128
HALO = 2 * SUBLANES


def _conv_taps(a_ref, c, rc):
    if isinstance(c, int) and c == 0:
        ext = jnp.concatenate([jnp.zeros((HALO, FFN_W), f32), a_ref[pl.ds(0, rc), :].astype(f32)], axis=0)
    else:
        ext = a_ref[pl.ds(pl.multiple_of(c * rc - HALO, HALO), rc + HALO), :].astype(f32)
    return ext[HALO:], pltpu.roll(ext, 1, 0)[HALO:], pltpu.roll(ext, 2, 0)[HALO:]


def _chunk_rows(c, rc):
    return pl.ds(c * rc, rc) if isinstance(c, int) else pl.ds(pl.multiple_of(c * rc, rc), rc)


def _ffn_mid_fwd(au, cw, cb, *, name):
    T = au.shape[0]
    F = au.shape[1] // 2
    nb = F // FFN_W
    rc = min(FFN_ROWS, T)
    nc = T // rc

    def body(a_ref, u_ref, w_ref, b_ref, z_ref):
        w, b = w_ref[...], b_ref[...]

        def chunk(c):
            a, a1, a2 = _conv_taps(a_ref, c, rc)
            rows = _chunk_rows(c, rc)
            ac = (w[0:1] * a2 + w[1:2] * a1 + w[2:3] * a + b).astype(bf16)
            z_ref[rows, :] = ac * jax.nn.sigmoid(ac) * u_ref[rows, :]

        chunk(0)
        lax.fori_loop(1, nc, lambda c, _: chunk(c), None)

    return pl.pallas_call(
        body, name=name, grid=(nb,),
        in_specs=[pl.BlockSpec((T, FFN_W), lambda j: (0, j)), pl.BlockSpec((T, FFN_W), lambda j: (0, nb + j)),
                  pl.BlockSpec((3, FFN_W), lambda j: (0, j)), pl.BlockSpec((1, FFN_W), lambda j: (0, j))],
        out_specs=pl.BlockSpec((T, FFN_W), lambda j: (0, j)), out_shape=jax.ShapeDtypeStruct((T, F), bf16),
        compiler_params=_cp("parallel"),
    )(au, au, cw, cb)


def _ffn_mid_bwd(au, cw, cb, dz, *, name):
    T = au.shape[0]
    F = au.shape[1] // 2
    nb = F // FFN_W
    rc = min(FFN_ROWS, T)
    nc = T // rc

    def body(a_ref, u_ref, w_ref, b_ref, dz_ref, dau_ref, dw_ref, db_ref):
        w, b = w_ref[...], b_ref[...]

        def chunk(c, carry):
            nxt, s0, s1, s2, sb = carry
            a, a1, a2 = _conv_taps(a_ref, c, rc)
            rows = _chunk_rows(c, rc)
            ac = (w[0:1] * a2 + w[1:2] * a1 + w[2:3] * a + b).astype(bf16)
            sg = jax.nn.sigmoid(ac)
            dz = dz_ref[rows, :]
            dau_ref[1, rows, :] = dz * ac * sg
            dac = (dz * u_ref[rows, :] * sg * (1.0 + ac * (1.0 - sg))).astype(f32)
            ext = jnp.concatenate([dac, nxt], axis=0)
            d1, d2 = pltpu.roll(ext, rc + HALO - 1, 0)[:rc], pltpu.roll(ext, rc + HALO - 2, 0)[:rc]
            dau_ref[0, rows, :] = (w[2:3] * dac + w[1:2] * d1 + w[0:1] * d2).astype(bf16)
            tot = lambda v: jnp.sum(v, axis=0, keepdims=True)
            return dac[:HALO], s0 + tot(dac * a2), s1 + tot(dac * a1), s2 + tot(dac * a), sb + tot(dac)

        z = jnp.zeros((1, FFN_W), f32)
        carry = (jnp.zeros((HALO, FFN_W), f32), z, z, z, z)
        carry = lax.fori_loop(0, nc - 1, lambda k, cr: chunk(nc - 1 - k, cr), carry)
        _, s0, s1, s2, sb = chunk(0, carry)
        rows = lax.broadcasted_iota(jnp.int32, (3, FFN_W), 0)
        dw_ref[...] = jnp.where(rows == 0, s0, jnp.where(rows == 1, s1, s2))
        db_ref[...] = sb

    col = lambda off: pl.BlockSpec((T, FFN_W), lambda j: (0, off + j))
    return pl.pallas_call(
        body, name=name, grid=(nb,),
        in_specs=[col(0), col(nb), pl.BlockSpec((3, FFN_W), lambda j: (0, j)), pl.BlockSpec((1, FFN_W), lambda j: (0, j)), col(0)],
        out_specs=[pl.BlockSpec((2, T, FFN_W), lambda j: (0, 0, j)), pl.BlockSpec((3, FFN_W), lambda j: (0, j)),
                   pl.BlockSpec((1, FFN_W), lambda j: (0, j))],
        out_shape=[jax.ShapeDtypeStruct((2, T, F), bf16), jax.ShapeDtypeStruct((3, F), f32), jax.ShapeDtypeStruct((1, F), f32)],
        compiler_params=_cp("parallel"),
    )(au, au, cw, cb, dz)


BNN = (((2,), (1,)), ((0,), (0,)))
BNT = (((2,), (2,)), ((0,), (0,)))
BTN = (((1,), (1,)), ((0,), (0,)))


def _heads(x):
    return jnp.stack([x[:, h * LANES:(h + 1) * LANES] for h in range(HGRN_HEADS)])


def _put_heads(ref, rows, x, dtype):
    for h in range(HGRN_HEADS):
        ref[rows, h * LANES:(h + 1) * LANES] = x[h].astype(dtype)


def _hgrn_lb(l):
    m = jnp.max(l, axis=0, keepdims=True)
    e = jnp.exp(l - m)
    return e[0:1] / jnp.sum(e, axis=0, keepdims=True)


def _hgrn_chunk(q, fx, lb):
    H, C = q.shape[0], q.shape[1]
    sg = jax.nn.sigmoid(fx)
    F = lb + (1.0 - lb) * sg
    k = 1.0 - F
    logF = jnp.log(F)
    r = lax.broadcasted_iota(jnp.int32, (H, C, C), 1)
    c = lax.broadcasted_iota(jnp.int32, (H, C, C), 2)
    tril = (r >= c)
    b = _dot(tril.astype(f32), logF, BNN, precision=HI)
    bl = jnp.sum(logF, axis=1, keepdims=True)
    eb = jnp.exp(b)
    enb = jnp.exp(-b)
    elb = jnp.exp(bl - b)
    return dict(sg=sg, F=F, k=k, b=b, bl=bl, eb=eb, enb=enb, elb=elb, qd=q * eb, kd=k * enb, kl=k * elb, tril=tril)


def _hgrn_fwd(proj, lbl, ng, *, rb=512):
    T = proj.shape[0]
    rb = min(rb, T)
    cpb = rb // HGRN_CHUNK
    nblk = T // rb
    H = HGRN_HEADS

    def body(q_ref, f_ref, i_ref, g_ref, lbl_ref, ng_ref, y_ref, st_ref, S):
        @pl.when(pl.program_id(0) == 0)
        def _():
            S[...] = jnp.zeros_like(S)

        lb = _heads(_hgrn_lb(lbl_ref[...]))
        ngv = _heads(ng_ref[...])
        for c in range(cpb):
            sl = pl.ds(c * HGRN_CHUNK, HGRN_CHUNK)
            v, gx = _heads(i_ref[sl, :]), _heads(g_ref[sl, :])
            ch = _hgrn_chunk(_heads(q_ref[sl, :]), _heads(f_ref[sl, :]), lb)
            att = jnp.where(ch["tril"], _bdot(ch["qd"], ch["kd"], BNT), 0.0)
            St = S[...]
            st_ref[:, c] = St
            o = _bdot(att, v, BNN) + _bdot(ch["qd"], St, BNT)
            S[...] = St * jnp.exp(ch["bl"]) + _bdot(v, ch["kl"], BTN)
            r = lax.rsqrt(jnp.mean(o * o, axis=-1, keepdims=True) + EPS)
            _put_heads(y_ref, sl, o * r * ngv * (gx * jax.nn.sigmoid(gx)), bf16)

    col = lambda off: pl.BlockSpec((rb, H * LANES), lambda n: (n, off))
    return pl.pallas_call(
        body, name="hgrn_fwd", grid=(nblk,),
        in_specs=[col(0), col(1), col(2), col(3), pl.BlockSpec((2, H * LANES), lambda n: (0, 0)),
                  pl.BlockSpec((1, H * LANES), lambda n: (0, 0))],
        out_specs=[pl.BlockSpec((rb, H * LANES), lambda n: (n, 0)),
                   pl.BlockSpec((H, cpb, LANES, LANES), lambda n: (0, n, 0, 0))],
        out_shape=[jax.ShapeDtypeStruct((T, H * LANES), bf16),
                   jax.ShapeDtypeStruct((H, T // HGRN_CHUNK, LANES, LANES), f32)],
        scratch_shapes=[pltpu.VMEM((H, LANES, LANES), f32)], compiler_params=_cp("arbitrary"),
    )(proj, proj, proj, proj, lbl, ng)


def _hgrn_bwd(proj, lbl, ng, states, dy, *, rb=512):
    T = proj.shape[0]
    rb = min(rb, T)
    cpb = rb // HGRN_CHUNK
    nblk = T // rb
    H = HGRN_HEADS
    C = HGRN_CHUNK

    def body(q_ref, f_ref, i_ref, g_ref, lbl_ref, ng_ref, st_ref, dy_ref,
             dq_ref, df_ref, di_ref, dg_ref, dl_ref, dng_ref, dS, dlb_acc, dng_acc):
        n = pl.program_id(0)

        @pl.when(n == 0)
        def _():
            dS[...] = jnp.zeros_like(dS)
            dlb_acc[...] = jnp.zeros_like(dlb_acc)
            dng_acc[...] = jnp.zeros_like(dng_acc)

        lb_row = _hgrn_lb(lbl_ref[...])
        lb = _heads(lb_row)
        ngv = _heads(ng_ref[...])
        r_i = lax.broadcasted_iota(jnp.int32, (H, C, C), 1)
        c_i = lax.broadcasted_iota(jnp.int32, (H, C, C), 2)
        triu = (c_i >= r_i).astype(f32)
        rows_sum = lambda x: jnp.sum(x, axis=1, keepdims=True)
        for c in reversed(range(cpb)):
            sl = pl.ds(c * C, C)
            q, v, gx = _heads(q_ref[sl, :]), _heads(i_ref[sl, :]), _heads(g_ref[sl, :])
            ch = _hgrn_chunk(q, _heads(f_ref[sl, :]), lb)
            qd, kd, kl = ch["qd"], ch["kd"], ch["kl"]
            att = jnp.where(ch["tril"], _bdot(qd, kd, BNT), 0.0)
            St = st_ref[:, c]
            o = _bdot(att, v, BNN) + _bdot(qd, St, BNT)
            r = lax.rsqrt(jnp.mean(o * o, axis=-1, keepdims=True) + EPS)
            on = o * r
            sgg = jax.nn.sigmoid(gx)
            gate = gx * sgg
            dyv = _heads(dy_ref[sl, :].astype(f32))
            _put_heads(dg_ref, sl, dyv * on * ngv * sgg * (1.0 + gx * (1.0 - sgg)), bf16)
            dng_acc[...] += rows_sum(dyv * on * gate)
            don = dyv * ngv * gate
            do = r * (don - on * jnp.mean(don * on, axis=-1, keepdims=True))
            dSt = dS[...]
            dA = jnp.where(ch["tril"], _bdot(do, v, BNT), 0.0)
            dv = _bdot(att, do, BTN) + _bdot(kl, dSt, BNT)
            dqd = _bdot(dA, kd, BNN) + _bdot(do, St, BNN)
            dkd = _bdot(dA, qd, BTN)
            dkl = _bdot(v, dSt, BNN)
            dec = jnp.exp(ch["bl"])
            ddec = rows_sum(St * dSt)
            dS[...] = _bdot(do, qd, BTN) + dSt * dec
            dB = dqd * qd - dkd * kd - dkl * kl
            dbl = rows_sum(dkl * kl) + ddec * dec
            dk = dkd * ch["enb"] + dkl * ch["elb"]
            dlogF = _dot(triu, dB, BNN, precision=HI) + dbl
            dF = dlogF / ch["F"] - dk
            sg = ch["sg"]
            _put_heads(dq_ref, sl, dqd * ch["eb"], bf16)
            _put_heads(di_ref, sl, dv, bf16)
            _put_heads(df_ref, sl, dF * (1.0 - lb) * sg * (1.0 - sg), bf16)
            dlb_acc[...] += rows_sum(dF * (1.0 - sg))

        @pl.when(n == nblk - 1)
        def _():
            rows = lax.broadcasted_iota(jnp.int32, (2, LANES), 0)
            for h in range(H):
                hs = pl.ds(h * LANES, LANES)
                lbh = lb_row[:, h * LANES:(h + 1) * LANES]
                dl0 = dlb_acc[h] * lbh * (1.0 - lbh)
                dl_ref[:, hs] = jnp.where(rows == 0, dl0, -dl0)
                dng_ref[:, hs] = dng_acc[h]

    col = lambda off: pl.BlockSpec((rb, H * LANES), lambda n: (nblk - 1 - n, off))
    vec = lambda rows: pl.BlockSpec((rows, H * LANES), lambda n: (0, 0))
    tok = jax.ShapeDtypeStruct((T, H * LANES), bf16)
    return pl.pallas_call(
        body, name="hgrn_bwd", grid=(nblk,),
        in_specs=[col(0), col(1), col(2), col(3), vec(2), vec(1),
                  pl.BlockSpec((H, cpb, LANES, LANES), lambda n: (0, nblk - 1 - n, 0, 0)), col(0)],
        out_specs=[col(0), col(0), col(0), col(0), vec(2), vec(1)],
        out_shape=[tok, tok, tok, tok, jax.ShapeDtypeStruct((2, H * LANES), f32), jax.ShapeDtypeStruct((1, H * LANES), f32)],
        scratch_shapes=[pltpu.VMEM((H, LANES, LANES), f32), pltpu.VMEM((H, 1, LANES), f32), pltpu.VMEM((H, 1, LANES), f32)],
        compiler_params=_cp("arbitrary"),
    )(proj, proj, proj, proj, lbl, ng, states, dy)


def _s5_disc_math(ar, ai, ldt, br, bi):
    dt = jnp.exp(ldt)
    mag = jnp.exp(ar * dt)
    abr, abi = mag * jnp.cos(ai * dt), mag * jnp.sin(ai * dt)
    den = ar * ar + ai * ai
    xr, xi = abr - 1.0, abi
    cr = (xr * ar + xi * ai) / den
    ci = (xi * ar - xr * ai) / den
    return abr, abi, cr * br - ci * bi, cr * bi + ci * br


def _s5_disc_fwd(ar, ai, ldt, br, bi):
    def body(ar_ref, ai_ref, ldt_ref, br_ref, bi_ref, o0, o1, o2, o3):
        outs = _s5_disc_math(ar_ref[...], ai_ref[...], ldt_ref[...], br_ref[...], bi_ref[...])
        for o, v in zip((o0, o1, o2, o3), outs):
            o[...] = v

    return pl.pallas_call(
        body, name="s5_disc_fwd",
        out_shape=[jax.ShapeDtypeStruct(ar.shape, f32)] * 2 + [jax.ShapeDtypeStruct(br.shape, f32)] * 2,
    )(ar, ai, ldt, br, bi)


def _s5_disc_bwd(ar, ai, ldt, br, bi, cts):
    def body(ar_ref, ai_ref, ldt_ref, br_ref, bi_ref, c0, c1, c2, c3, o0, o1, o2, o3, o4):
        _, vjp = jax.vjp(_s5_disc_math, ar_ref[...], ai_ref[...], ldt_ref[...], br_ref[...], bi_ref[...])
        for o, v in zip((o0, o1, o2, o3, o4), vjp((c0[...], c1[...], c2[...], c3[...]))):
            o[...] = v

    return pl.pallas_call(
        body, name="s5_disc_bwd",
        out_shape=[jax.ShapeDtypeStruct(ar.shape, f32)] * 3 + [jax.ShapeDtypeStruct(br.shape, f32)] * 2,
    )(ar, ai, ldt, br, bi, *cts)


S5_LC = 512
S5_NLC = S5_N // S5_LC
S5_UB = 4
S5_UNROLL = 4


def _cmul(ar, ai, xr, xi):
    return ar * xr - ai * xi, ar * xi + ai * xr


def _cpow(ar, ai, n):
    rr, ri = None, None
    br, bi = ar, ai
    while n:
        if n & 1:
            rr, ri = (br, bi) if rr is None else _cmul(rr, ri, br, bi)
        n >>= 1
        if n:
            br, bi = _cmul(br, bi, br, bi)
    return rr, ri


def _s5_bu(u_ref, bre_ref, bim_ref, xr, xi):
    for k in range(S5_UB):
        uk = u_ref[:, k * LANES:(k + 1) * LANES].astype(bf16)
        xr[:, k * S5_LC:(k + 1) * S5_LC] = _dot(uk, bre_ref[k])
        xi[:, k * S5_LC:(k + 1) * S5_LC] = _dot(uk, bim_ref[k])


def _s5_scan(xr, xi, sr, si, ar_ref, ai_ref, nsteps, store):
    for c in range(S5_NLC):
        cs = slice(c * S5_LC, (c + 1) * S5_LC)
        a_r = jnp.broadcast_to(ar_ref[:, cs], (S5_SEG, S5_LC))
        a_i = jnp.broadcast_to(ai_ref[:, cs], (S5_SEG, S5_LC))

        def step(j, carry, cs=cs, a_r=a_r, a_i=a_i):
            pr, pi = carry
            rows = pl.ds(pl.multiple_of(j * S5_SEG, S5_SEG), S5_SEG)
            nr = a_r * pr - a_i * pi + xr[rows, cs]
            ni = a_r * pi + a_i * pr + xi[rows, cs]
            if store:
                xr[rows, cs] = nr
                xi[rows, cs] = ni
            return nr, ni

        fr, fi = lax.fori_loop(0, nsteps, step, (sr[:, cs], si[:, cs]), unroll=S5_UNROLL)
        sr[:, cs] = fr
        si[:, cs] = fi


def _s5_rscan(dr, di, xr, xi, s0r, s0i, gr, gi, acc_r, acc_i, ar_ref, ai_ref, nsteps):
    for c in range(S5_NLC):
        cs = slice(c * S5_LC, (c + 1) * S5_LC)
        a_r = jnp.broadcast_to(ar_ref[:, cs], (S5_SEG, S5_LC))
        a_i = jnp.broadcast_to(ai_ref[:, cs], (S5_SEG, S5_LC))

        def step(jj, carry, cs=cs, a_r=a_r, a_i=a_i):
            pr, pi, cr, ci = carry
            j = nsteps - 1 - jj
            rows = pl.ds(pl.multiple_of(j * S5_SEG, S5_SEG), S5_SEG)
            nr = dr[rows, cs] + a_r * pr + a_i * pi
            ni = di[rows, cs] + a_r * pi - a_i * pr
            dr[rows, cs] = nr
            di[rows, cs] = ni
            if acc_r is not None:
                prev = pl.ds(pl.multiple_of(jnp.maximum(j - 1, 0) * S5_SEG, S5_SEG), S5_SEG)
                first = j == 0
                pr_s = jnp.where(first, s0r[:, cs], xr[prev, cs])
                pi_s = jnp.where(first, s0i[:, cs], xi[prev, cs])
                cr = cr + nr * pr_s + ni * pi_s
                ci = ci - nr * pi_s + ni * pr_s
            return nr, ni, cr, ci

        z = jnp.zeros((S5_SEG, S5_LC), f32)
        init = (gr[:, cs], gi[:, cs], z, z)
        fr, fi, cr, ci = lax.fori_loop(0, nsteps, step, init, unroll=S5_UNROLL)
        gr[:, cs] = fr
        gi[:, cs] = fi
        if acc_r is not None:
            acc_r[:, cs] += cr
            acc_i[:, cs] += ci


def _s5_seg_carry(fr, fi, ar, ai, seg_len, reverse):
    pr, pi = _cpow(ar, ai if not reverse else -ai, seg_len)
    rows = lax.broadcasted_iota(jnp.int32, fr.shape, 0)
    cr, ci = jnp.zeros_like(fr), jnp.zeros_like(fi)
    sh = (S5_SEG - 1) if reverse else 1
    fr_s, fi_s = pltpu.roll(fr, sh, 0), pltpu.roll(fi, sh, 0)
    order = range(S5_SEG - 2, -1, -1) if reverse else range(1, S5_SEG)
    for r in order:
        c_r, c_i = pltpu.roll(cr, sh, 0), pltpu.roll(ci, sh, 0)
        m_r, m_i = _cmul(pr, pi, c_r, c_i)
        cr = jnp.where(rows == r, m_r + fr_s, cr)
        ci = jnp.where(rows == r, m_i + fi_s, ci)
    return cr, ci


def _gelu_parts(y):
    c0 = math.sqrt(2.0 / math.pi)
    t = jnp.tanh(c0 * (y + 0.044715 * y * y * y))
    z = 0.5 * y * (1.0 + t)
    dz = 0.5 * (1.0 + t) + 0.5 * y * (1.0 - t * t) * c0 * (1.0 + 3.0 * 0.044715 * y * y)
    return z, dz


def _s5_y(xr, xi, u_ref, cre_ref, cim_ref, d_ref):
    ys = []
    for k in range(S5_UB):
        cs = slice(k * S5_LC, (k + 1) * S5_LC)
        ys.append(_bdot(xr[:, cs], cre_ref[k]) - _bdot(xi[:, cs], cim_ref[k]))
    return jnp.concatenate(ys, axis=1) + d_ref[...] * u_ref[...]


def _s5_specs(T, rb, rev=False):
    nblk = T // rb
    blk = (lambda i: (nblk - 1 - i, 0)) if rev else (lambda i: (i, 0))
    tok = pl.BlockSpec((rb, 4 * LANES), blk)
    bmat = pl.BlockSpec((S5_UB, LANES, S5_LC), lambda i: (0, 0, 0))
    cmat = pl.BlockSpec((S5_UB, S5_LC, LANES), lambda i: (0, 0, 0))
    avec = pl.BlockSpec((1, S5_N), lambda i: (0, 0))
    seg = pl.BlockSpec((S5_SEG, S5_N), lambda i: (0, 0))
    cvec = pl.BlockSpec((1, 4 * LANES), lambda i: (0, 0))
    s0 = pl.BlockSpec((1, S5_SEG, S5_N), (lambda i: (nblk - 1 - i, 0, 0)) if rev else (lambda i: (i, 0, 0)))
    return dict(tok=tok, bmat=bmat, cmat=cmat, avec=avec, seg=seg, cvec=cvec, s0=s0, nblk=nblk)


def _s5_final(u, bre, bim, ar, ai, *, rb):
    T = u.shape[0]
    sp = _s5_specs(T, rb)

    def body(u_ref, bre_ref, bim_ref, ar_ref, ai_ref, fr_ref, fi_ref, xr, xi):
        @pl.when(pl.program_id(0) == 0)
        def _():
            fr_ref[...] = jnp.zeros_like(fr_ref)
            fi_ref[...] = jnp.zeros_like(fi_ref)

        _s5_bu(u_ref, bre_ref, bim_ref, xr, xi)
        _s5_scan(xr, xi, fr_ref, fi_ref, ar_ref, ai_ref, rb // S5_SEG, False)

    return pl.pallas_call(
        body, name="s5_final", grid=(sp["nblk"],),
        in_specs=[sp["tok"], sp["bmat"], sp["bmat"], sp["avec"], sp["avec"]], out_specs=[sp["seg"], sp["seg"]],
        out_shape=[jax.ShapeDtypeStruct((S5_SEG, S5_N), f32)] * 2,
        scratch_shapes=[pltpu.VMEM((rb, S5_N), f32)] * 2, compiler_params=_cp("arbitrary"),
    )(u, bre, bim, ar, ai)


def _s5_fwd(u, bre, bim, ar, ai, fr, fi, cre, cim, dsk, wg, bg, *, rb):
    T = u.shape[0]
    sp = _s5_specs(T, rb)
    seg_len = T // S5_SEG

    def body(u_ref, bre_ref, bim_ref, ar_ref, ai_ref, fr_ref, fi_ref, cre_ref, cim_ref, d_ref, wg_ref, bg_ref,
             o_ref, s0r_ref, s0i_ref, xr, xi, sr, si):
        @pl.when(pl.program_id(0) == 0)
        def _():
            i_r, i_i = _s5_seg_carry(fr_ref[...], fi_ref[...], ar_ref[...], ai_ref[...], seg_len, False)
            sr[...] = i_r
            si[...] = i_i

        s0r_ref[0] = sr[...]
        s0i_ref[0] = si[...]
        _s5_bu(u_ref, bre_ref, bim_ref, xr, xi)
        _s5_scan(xr, xi, sr, si, ar_ref, ai_ref, rb // S5_SEG, True)
        y = _s5_y(xr, xi, u_ref, cre_ref, cim_ref, d_ref)
        z, _ = _gelu_parts(y)
        v = _bdot(z, wg_ref[...]) + bg_ref[...]
        o_ref[...] = (z * jax.nn.sigmoid(v)).astype(bf16)

    wspec = pl.BlockSpec((4 * LANES, 4 * LANES), lambda i: (0, 0))
    return pl.pallas_call(
        body, name="s5_fwd", grid=(sp["nblk"],),
        in_specs=[sp["tok"], sp["bmat"], sp["bmat"], sp["avec"], sp["avec"], sp["seg"], sp["seg"], sp["cmat"], sp["cmat"],
                  sp["cvec"], wspec, sp["cvec"]],
        out_specs=[sp["tok"], sp["s0"], sp["s0"]],
        out_shape=[jax.ShapeDtypeStruct((T, 4 * LANES), bf16)] + [jax.ShapeDtypeStruct((sp["nblk"], S5_SEG, S5_N), f32)] * 2,
        scratch_shapes=[pltpu.VMEM((rb, S5_N), f32)] * 2 + [pltpu.VMEM((S5_SEG, S5_N), f32)] * 2,
        compiler_params=_cp("arbitrary"),
    )(u, bre, bim, ar, ai, fr, fi, cre, cim, dsk, wg, bg)


def _s5_bwd_a(u, bre, bim, ar, ai, s0r, s0i, cre, cim, cret, cimt, dsk, wg, bg, dout, *, rb):
    T = u.shape[0]
    sp = _s5_specs(T, rb, rev=True)

    def body(u_ref, bre_ref, bim_ref, ar_ref, ai_ref, s0r_ref, s0i_ref, cre_ref, cim_ref, cret_ref, cimt_ref,
             d_ref, wg_ref, bg_ref, do_ref, dy_ref, glr_ref, gli_ref, dcre_ref, dcim_ref, dd_ref, dwg_ref, dbg_ref,
             xr, xi, dr, di, sr, si):
        @pl.when(pl.program_id(0) == 0)
        def _():
            for r in (glr_ref, gli_ref, dcre_ref, dcim_ref, dd_ref, dwg_ref, dbg_ref):
                r[...] = jnp.zeros_like(r)

        sr[...] = s0r_ref[0]
        si[...] = s0i_ref[0]
        _s5_bu(u_ref, bre_ref, bim_ref, xr, xi)
        _s5_scan(xr, xi, sr, si, ar_ref, ai_ref, rb // S5_SEG, True)
        uv = u_ref[...]
        y = _s5_y(xr, xi, u_ref, cre_ref, cim_ref, d_ref)
        z, gz = _gelu_parts(y)
        v = _bdot(z, wg_ref[...]) + bg_ref[...]
        sg = jax.nn.sigmoid(v)
        dov = do_ref[...].astype(f32)
        dv = dov * z * sg * (1.0 - sg)
        dz = dov * sg + _bdot(dv, wg_ref[...], NT)
        dy = dz * gz
        dy_ref[...] = dy
        dwg_ref[...] += _bdot(z, dv, TN)
        dbg_ref[...] += jnp.sum(dv, axis=0, keepdims=True)
        dd_ref[...] += jnp.sum(dy * uv, axis=0, keepdims=True)
        for k in range(S5_UB):
            cs = slice(k * S5_LC, (k + 1) * S5_LC)
            dyk = dy[:, k * LANES:(k + 1) * LANES]
            dcre_ref[k] += _bdot(xr[:, cs], dyk, TN)
            dcim_ref[k] -= _bdot(xi[:, cs], dyk, TN)
            dr[:, cs] = _bdot(dyk, cret_ref[k])
            di[:, cs] = -_bdot(dyk, cimt_ref[k])
        _s5_rscan(dr, di, None, None, None, None, glr_ref, gli_ref, None, None, ar_ref, ai_ref, rb // S5_SEG)

    wspec = pl.BlockSpec((4 * LANES, 4 * LANES), lambda i: (0, 0))
    return pl.pallas_call(
        body, name="s5_bwd_a", grid=(sp["nblk"],),
        in_specs=[sp["tok"], sp["bmat"], sp["bmat"], sp["avec"], sp["avec"], sp["s0"], sp["s0"], sp["cmat"], sp["cmat"],
                  sp["bmat"], sp["bmat"], sp["cvec"], wspec, sp["cvec"], sp["tok"]],
        out_specs=[sp["tok"], sp["seg"], sp["seg"], sp["cmat"], sp["cmat"], sp["cvec"], wspec, sp["cvec"]],
        out_shape=[jax.ShapeDtypeStruct((T, 4 * LANES), f32)] + [jax.ShapeDtypeStruct((S5_SEG, S5_N), f32)] * 2
        + [jax.ShapeDtypeStruct((S5_UB, S5_LC, LANES), f32)] * 2
        + [jax.ShapeDtypeStruct((1, 4 * LANES), f32), jax.ShapeDtypeStruct((4 * LANES, 4 * LANES), f32),
           jax.ShapeDtypeStruct((1, 4 * LANES), f32)],
        scratch_shapes=[pltpu.VMEM((rb, S5_N), f32)] * 4 + [pltpu.VMEM((S5_SEG, S5_N), f32)] * 2,
        compiler_params=_cp("arbitrary"),
    )(u, bre, bim, ar, ai, s0r, s0i, cre, cim, cret, cimt, dsk, wg, bg, dout)


def _s5_bwd_b(u, bre, bim, bret, bimt, ar, ai, s0r, s0i, glr, gli, cret, cimt, dsk, dy, *, rb):
    T = u.shape[0]
    sp = _s5_specs(T, rb, rev=True)
    seg_len = T // S5_SEG
    nblk = sp["nblk"]

    def body(u_ref, bre_ref, bim_ref, bret_ref, bimt_ref, ar_ref, ai_ref, s0r_ref, s0i_ref, glr_ref, gli_ref,
             cret_ref, cimt_ref, d_ref, dy_ref, du_ref, dbre_ref, dbim_ref, dar_ref, dai_ref,
             xr, xi, dr, di, sr, si, gr, gi, acc_r, acc_i):
        @pl.when(pl.program_id(0) == 0)
        def _():
            x_r, x_i = _s5_seg_carry(glr_ref[...], gli_ref[...], ar_ref[...], ai_ref[...], seg_len, True)
            gr[...] = x_r
            gi[...] = x_i
            acc_r[...] = jnp.zeros_like(acc_r)
            acc_i[...] = jnp.zeros_like(acc_i)
            dbre_ref[...] = jnp.zeros_like(dbre_ref)
            dbim_ref[...] = jnp.zeros_like(dbim_ref)

        sr[...] = s0r_ref[0]
        si[...] = s0i_ref[0]
        _s5_bu(u_ref, bre_ref, bim_ref, xr, xi)
        _s5_scan(xr, xi, sr, si, ar_ref, ai_ref, rb // S5_SEG, True)
        dy = dy_ref[...]
        for k in range(S5_UB):
            cs = slice(k * S5_LC, (k + 1) * S5_LC)
            dyk = dy[:, k * LANES:(k + 1) * LANES]
            dr[:, cs] = _bdot(dyk, cret_ref[k])
            di[:, cs] = -_bdot(dyk, cimt_ref[k])
        sr[...] = s0r_ref[0]
        si[...] = s0i_ref[0]
        _s5_rscan(dr, di, xr, xi, sr, si, gr, gi, acc_r, acc_i, ar_ref, ai_ref, rb // S5_SEG)
        dus = []
        for k in range(S5_UB):
            cs = slice(k * S5_LC, (k + 1) * S5_LC)
            uk = u_ref[:, k * LANES:(k + 1) * LANES]
            dbre_ref[k] += _bdot(uk, dr[:, cs], TN)
            dbim_ref[k] += _bdot(uk, di[:, cs], TN)
            dus.append(_bdot(dr[:, cs], bret_ref[k]) + _bdot(di[:, cs], bimt_ref[k]))
        du_ref[...] = (jnp.concatenate(dus, axis=1) + d_ref[...] * dy).astype(bf16)

        @pl.when(pl.program_id(0) == nblk - 1)
        def _():
            dar_ref[...] = jnp.sum(acc_r[...], axis=0, keepdims=True)
            dai_ref[...] = jnp.sum(acc_i[...], axis=0, keepdims=True)

    return pl.pallas_call(
        body, name="s5_bwd_b", grid=(nblk,),
        in_specs=[sp["tok"], sp["bmat"], sp["bmat"], sp["cmat"], sp["cmat"], sp["avec"], sp["avec"], sp["s0"], sp["s0"],
                  sp["seg"], sp["seg"], sp["bmat"], sp["bmat"], sp["cvec"], sp["tok"]],
        out_specs=[sp["tok"], sp["bmat"], sp["bmat"], sp["avec"], sp["avec"]],
        out_shape=[jax.ShapeDtypeStruct((T, 4 * LANES), bf16)] + [jax.ShapeDtypeStruct((S5_UB, LANES, S5_LC), f32)] * 2
        + [jax.ShapeDtypeStruct((1, S5_N), f32)] * 2,
        scratch_shapes=[pltpu.VMEM((rb, S5_N), f32)] * 4 + [pltpu.VMEM((S5_SEG, S5_N), f32)] * 6,
        compiler_params=_cp("arbitrary"),
    )(u, bre, bim, bret, bimt, ar, ai, s0r, s0i, glr, gli, cret, cimt, dsk, dy)


def _blockdiag(w, transpose=False):
    if transpose:
        w = jnp.swapaxes(w, 1, 2)
    g, a, b = w.shape
    eye = jnp.eye(8, dtype=w.dtype)
    return jnp.einsum("kgab,gj->kgajb", w.reshape(4, 8, a, b), eye).reshape(4, 8 * a, 8 * b)


def _blockdiag_t(m, a, b):
    eye = jnp.eye(8, dtype=m.dtype)
    return jnp.einsum("kgajb,gj->kgab", m.reshape(4, 8, a, 8, b), eye).reshape(32, a, b)


ROT = MLA_ROPE // 2


def _rope_tables(positions):
    freqs = ROPE_THETA ** (-jnp.arange(0, MLA_ROPE, 2, dtype=f32) / MLA_ROPE)
    ang = positions.astype(f32)[:, None] * freqs
    cos, sin, z = jnp.cos(ang), jnp.sin(ang), jnp.zeros_like(ang)
    return (jnp.concatenate([cos, cos, z, z], axis=1), jnp.concatenate([-sin, z, z, z], axis=1),
            jnp.concatenate([z, sin, z, z], axis=1))


def _rot(x, c, sa, sb):
    return x * c + pltpu.roll(x, LANES - ROT, 1) * sa + pltpu.roll(x, ROT, 1) * sb


def _rot_t(dy, c, sa, sb):
    return dy * c + pltpu.roll(dy * sa, ROT, 1) + pltpu.roll(dy * sb, LANES - ROT, 1)


def _rms(xv, g):
    return xv * lax.rsqrt(jnp.mean(xv * xv, axis=-1, keepdims=True) + EPS) * g


QW, KVW = MLA_Q_RANK, MLA_KV_RANK
ODD_PAD = QW + KVW + LANES


def _mla_prep_fwd(proj, qg, kvg, tabs, *, tm=512):
    T = proj.shape[0]
    tm = _tile(T, tm)

    def body(p_ref, qg_ref, kvg_ref, c_ref, sa_ref, sb_ref, cq_ref, ckv_ref, kr_ref):
        cq_ref[...] = _rms(p_ref[:, :QW], qg_ref[...]).astype(bf16)
        ckv_ref[...] = _rms(p_ref[:, QW:QW + KVW], kvg_ref[...]).astype(bf16)
        kr_ref[...] = _rot(p_ref[:, QW + KVW:], c_ref[...], sa_ref[...], sb_ref[...]).astype(bf16)

    row = lambda w: pl.BlockSpec((tm, w), lambda i: (i, 0))
    vec = lambda w: pl.BlockSpec((1, w), lambda i: (0, 0))
    return pl.pallas_call(
        body, name="mla_prep_fwd", grid=(T // tm,),
        in_specs=[row(ODD_PAD), vec(QW), vec(KVW), row(LANES), row(LANES), row(LANES)],
        out_specs=[row(QW), row(KVW), row(LANES)],
        out_shape=[jax.ShapeDtypeStruct((T, QW), bf16), jax.ShapeDtypeStruct((T, KVW), bf16),
                   jax.ShapeDtypeStruct((T, LANES), bf16)],
        compiler_params=_cp("parallel"),
    )(proj, qg, kvg, *tabs)


def _mla_prep_bwd(proj, qg, kvg, tabs, dcqn, dckvn, dkr_heads, *, tm=512):
    T = proj.shape[0]
    tm = _tile(T, tm)

    def body(p_ref, qg_ref, kvg_ref, c_ref, sa_ref, sb_ref, dcq_ref, dckv_ref, dkr_ref, dp_ref, dqg_ref, dkvg_ref):
        dcq, dqg = _rms_bwd_math(p_ref[:, :QW], qg_ref[...], dcq_ref[...])
        dckv, dkvg = _rms_bwd_math(p_ref[:, QW:QW + KVW], kvg_ref[...], dckv_ref[...])
        dk = dkr_ref[:, :LANES]
        for h in range(1, MLA_HEADS):
            dk = dk + dkr_ref[:, h * LANES:(h + 1) * LANES]
        dkr = _rot_t(dk, c_ref[...], sa_ref[...], sb_ref[...])
        dp_ref[...] = jnp.concatenate([dcq, dckv, dkr], axis=1).astype(bf16)

        @pl.when(pl.program_id(0) == 0)
        def _():
            dqg_ref[...] = dqg
            dkvg_ref[...] = dkvg

        @pl.when(pl.program_id(0) > 0)
        def _():
            dqg_ref[...] += dqg
            dkvg_ref[...] += dkvg

    row = lambda w: pl.BlockSpec((tm, w), lambda i: (i, 0))
    vec = lambda w: pl.BlockSpec((1, w), lambda i: (0, 0))
    return pl.pallas_call(
        body, name="mla_prep_bwd", grid=(T // tm,),
        in_specs=[row(ODD_PAD), vec(QW), vec(KVW), row(LANES), row(LANES), row(LANES), row(QW), row(KVW),
                  row(MLA_HEADS * LANES)],
        out_specs=[row(ODD_PAD), vec(QW), vec(KVW)],
        out_shape=[jax.ShapeDtypeStruct((T, ODD_PAD), bf16), jax.ShapeDtypeStruct((1, QW), f32),
                   jax.ShapeDtypeStruct((1, KVW), f32)],
        compiler_params=_cp("arbitrary"),
    )(proj, qg, kvg, *tabs, dcqn, dckvn, dkr_heads)


HQ = 2 * LANES
QK_SCALE = MLA_QK ** -0.5


def _q_post(q, tabs, *, transpose, name, tm=512):
    T = q.shape[0]
    tm = _tile(T, tm)

    def body(q_ref, c_ref, sa_ref, sb_ref, o_ref):
        c, sa, sb = c_ref[...], sa_ref[...], sb_ref[...]
        for h in range(MLA_HEADS):
            nope, rope = pl.ds(h * HQ, LANES), pl.ds(h * HQ + LANES, LANES)
            o_ref[:, nope] = (q_ref[:, nope].astype(f32) * QK_SCALE).astype(bf16)
            o_ref[:, rope] = ((_rot_t if transpose else _rot)(q_ref[:, rope].astype(f32), c, sa, sb) * QK_SCALE).astype(bf16)

    tab = pl.BlockSpec((tm, LANES), lambda i: (i, 0))
    blk = pl.BlockSpec((tm, MLA_HEADS * HQ), lambda i: (i, 0))
    return pl.pallas_call(
        body, name=name, grid=(T // tm,), in_specs=[blk, tab, tab, tab], out_specs=blk,
        out_shape=jax.ShapeDtypeStruct(q.shape, bf16), compiler_params=_cp("parallel"),
    )(q, *tabs)


def _causal_mask(i, j, tq, tk):
    r = lax.broadcasted_iota(jnp.int32, (tq, tk), 0) + i * tq
    c = lax.broadcasted_iota(jnp.int32, (tq, tk), 1) + j * tk
    return c <= r


FLASH_PARTS = 4
FLASH_BWD_PARTS = 1


def _flash_fwd(q, kv, kr, *, tq=1024, tk=1024):
    T = q.shape[0]
    tq = _tile(T, tq)
    tk = _tile(tq, tk)
    per = tq // tk
    H = MLA_HEADS

    def body(q_ref, kn_ref, v_ref, kr_ref, o_ref, lse_ref, m_s, acc):
        i, j = pl.program_id(1), pl.program_id(2)
        last = (i + 1) * per - 1

        @pl.when(j == 0)
        def _():
            m_s[...] = jnp.full_like(m_s, -jnp.inf)
            acc[...] = jnp.zeros_like(acc)

        def step(masked):
            k = jnp.concatenate([kn_ref[...], kr_ref[...]], axis=1)
            v1 = jnp.concatenate([v_ref[...], jnp.ones((tk, LANES), bf16)], axis=1)
            mask = _causal_mask(i, j, tq, tk) if masked else None
            for part in range(FLASH_PARTS):
                rows = pl.ds(part * (tq // FLASH_PARTS), tq // FLASH_PARTS)
                s = _dot(q_ref[rows, :], k, NT)
                if masked:
                    s = jnp.where(mask[part * (tq // FLASH_PARTS):(part + 1) * (tq // FLASH_PARTS)], s, -jnp.inf)
                m_new = jnp.maximum(m_s[rows, :], jnp.max(s, axis=-1, keepdims=True))
                alpha = jnp.exp(m_s[rows, :] - m_new)
                p = jnp.exp((s - m_new).astype(bf16))
                acc[rows, :] = alpha * acc[rows, :] + _dot(p, v1)
                m_s[rows, :] = m_new

        pl.when(j < i * per)(functools.partial(step, False))
        pl.when((j >= i * per) & (j <= last))(functools.partial(step, True))

        @pl.when(j == last)
        def _():
            l = acc[:, LANES:]
            o_ref[...] = (acc[:, :LANES] / l).astype(bf16)
            lse_ref[0] = m_s[...] + jnp.log(jnp.max(l, axis=-1, keepdims=True))

    kj = lambda i, j: jnp.minimum(j, (i + 1) * per - 1)
    kblk = lambda off: pl.BlockSpec((tk, LANES), lambda h, i, j: (kj(i, j), 2 * h + off))
    return pl.pallas_call(
        body, name="flash_fwd", grid=(H, T // tq, T // tk),
        in_specs=[pl.BlockSpec((tq, HQ), lambda h, i, j: (i, h)), kblk(0), kblk(1),
                  pl.BlockSpec((tk, LANES), lambda h, i, j: (kj(i, j), 0))],
        out_specs=[pl.BlockSpec((tq, LANES), lambda h, i, j: (i, h)), pl.BlockSpec((1, tq, 1), lambda h, i, j: (h, i, 0))],
        out_shape=[jax.ShapeDtypeStruct((T, H * LANES), bf16), jax.ShapeDtypeStruct((H, T, 1), f32)],
        scratch_shapes=[pltpu.VMEM((tq, 1), f32), pltpu.VMEM((tq, 2 * LANES), f32)],
        compiler_params=_cp("parallel", "parallel", "arbitrary"),
    )(q, kv, kv, kr)


def _flash_bwd(q, kv, kr, o, do, lse, *, tb=1024):
    T = q.shape[0]
    tb = _tile(T, tb)
    nb = T // tb
    H = MLA_HEADS

    def body(q_ref, kn_ref, v_ref, kr_ref, o_ref, do_ref, lse_ref, dkv_ref, dkr_ref, dq_ref, dk_acc, dv_acc, ds_s):
        j, ii = pl.program_id(1), pl.program_id(2)
        i = jnp.maximum(ii, j)

        @pl.when((j == 0) & (ii == 0))
        def _():
            dq_ref[...] = jnp.zeros_like(dq_ref)

        @pl.when(ii == 0)
        def _():
            dk_acc[...] = jnp.zeros_like(dk_acc)
            dv_acc[...] = jnp.zeros_like(dv_acc)

        def step(masked):
            qv, dov, lse = q_ref[...], do_ref[...], lse_ref[0]
            delta = jnp.sum(o_ref[...].astype(f32) * dov, axis=-1, keepdims=True)
            mask = _causal_mask(i, j, tb, tb) if masked else None
            tp = tb // FLASH_BWD_PARTS
            for part in range(FLASH_BWD_PARTS):
                keys = pl.ds(part * tp, tp)
                kp = jnp.concatenate([kn_ref[keys, :], kr_ref[keys, :]], axis=1)
                p = jnp.exp((_dot(qv, kp, NT) - lse).astype(bf16))
                if masked:
                    p = jnp.where(mask[:, part * tp:(part + 1) * tp], p, jnp.zeros_like(p))
                ds = p * (_bdot(dov, v_ref[keys, :], NT) - delta).astype(bf16)
                dv_acc[keys, :] += _bdot(p, dov, TN)
                dk_acc[keys, :] += _bdot(ds, qv, TN)
                ds_s[:, keys] = ds
            k = jnp.concatenate([kn_ref[...], kr_ref[...]], axis=1)
            dq_ref[pl.ds(pl.multiple_of(i * tb, tb), tb), :] += _bdot(ds_s[...], k)

        pl.when(ii > j)(functools.partial(step, False))
        pl.when(ii == j)(functools.partial(step, True))

        @pl.when(ii == nb - 1)
        def _():
            dkv_ref[...] = jnp.concatenate([dk_acc[:, :LANES], dv_acc[...]], axis=1).astype(bf16)
            dkr_ref[...] = dk_acc[:, LANES:]

    qi = lambda h, j, i: jnp.maximum(i, j)
    kblk = lambda off: pl.BlockSpec((tb, LANES), lambda h, j, i: (j, 2 * h + off))
    vec = pl.BlockSpec((1, tb, 1), lambda h, j, i: (h, qi(h, j, i), 0))
    qblk = pl.BlockSpec((tb, LANES), lambda h, j, i: (qi(h, j, i), h))
    return pl.pallas_call(
        body, name="flash_bwd", grid=(H, nb, nb),
        in_specs=[pl.BlockSpec((tb, HQ), lambda h, j, i: (qi(h, j, i), h)), kblk(0), kblk(1),
                  pl.BlockSpec((tb, LANES), lambda h, j, i: (j, 0)), qblk, qblk, vec],
        out_specs=[pl.BlockSpec((tb, HQ), lambda h, j, i: (j, h)), pl.BlockSpec((tb, LANES), lambda h, j, i: (j, h)),
                   pl.BlockSpec((T, HQ), lambda h, j, i: (0, h))],
        out_shape=[jax.ShapeDtypeStruct((T, H * HQ), bf16), jax.ShapeDtypeStruct((T, H * LANES), f32),
                   jax.ShapeDtypeStruct((T, H * HQ), f32)],
        scratch_shapes=[pltpu.VMEM((tb, HQ), f32), pltpu.VMEM((tb, LANES), f32), pltpu.VMEM((tb, tb), bf16)],
        compiler_params=_cp("parallel", "arbitrary", "arbitrary"),
    )(q, kv, kv, kr, o, do, lse)


HBM_SPEC = pl.BlockSpec(memory_space=pltpu.HBM)
N_CHIPS = 4
N_DEV = 8

BIG = {"even_w_in": 1, "s5_w_glu": 0, "even_w_out": 0, "odd_w_in": 0, "mla_w_uq": 1, "mla_w_ukv": 1, "odd_w_out": 0,
       "ffn_w_in": 2, "ffn_w_out": 1}
LAYERED = ("ffn_w_in", "ffn_w_out")
GROUPS = {"even_in": ("even_w_in",), "even_rest": ("s5_w_glu", "even_w_out"), "ffn0": LAYERED,
          "odd": ("odd_w_in", "mla_w_uq", "mla_w_ukv", "odd_w_out"), "ffn1": LAYERED}
GROUP_LAYER = {"ffn0": 0, "ffn1": 1}


def _place():
    x, y, c = lax.axis_index("x"), lax.axis_index("y"), lax.axis_index("c")
    chips = [(1 - x, y), (x, 1 - y), (1 - x, 1 - y)]
    return x, y, c, chips


def _slab(ref, axis, k, size):
    start = pl.multiple_of(k * size, size if axis == 0 else LANES)
    idx = [slice(None)] * len(ref.shape)
    idx[axis] = pl.ds(start, size)
    return ref.at[tuple(idx)]


SEM_SPEC = pl.BlockSpec(memory_space=pltpu.SEMAPHORE)
ANY_SPEC = pl.BlockSpec(memory_space=pl.ANY)
EFFECT = pltpu.SideEffectType.DATAFLOW_SIDE_EFFECTING


def _hbm(a):
    return pltpu.with_memory_space_constraint(a, pltpu.HBM)


class _Gather:
    copies = 3

    def __init__(self, axis, size):
        self.axis, self.size = axis, size

    def view(self, land, kk):
        return _slab(land, self.axis, kk, self.size)

    def own(self, land, place):
        return self.view(land, 2 * place[0] + place[1])

    def sends(self, src, land, place):
        x, y, c, chips = place
        return [(self.own(land, place) if src is None else src, self.own(land, place), (*chip, c)) for chip in chips]

    def recvs(self, land, place):
        return [self.view(land, 2 * cx + cy) for cx, cy in place[3]]


class _Scatter:
    copies = 3

    def __init__(self, axis, size, layer=None):
        self.axis, self.size, self.layer = axis, size, layer

    def row(self, land, j):
        return land.at[j] if self.layer is None else land.at[j, self.layer]

    def sends(self, src, land, place):
        c, chips = place[2], place[3]
        return [(_slab(src, self.axis, 2 * cx + cy, self.size), self.row(land, j), (cx, cy, c))
                for j, (cx, cy) in enumerate(chips)]

    def recvs(self, land, place):
        return [self.row(land, j) for j in range(3)]


class _ToAll:
    copies = N_DEV - 1

    def __init__(self, size):
        self.size = size

    def sends(self, src, land, place):
        x, y, c, _ = place
        flip = lambda v, bit: 1 - v if bit else v
        own = _slab(land, 0, 4 * x + 2 * y + c, self.size)
        return [(own, own, (flip(x, m & 4), flip(y, m & 2), flip(c, m & 1))) for m in range(1, N_DEV)]

    def recvs(self, land, place):
        x, y, c, _ = place
        d = 4 * x + 2 * y + c
        return [_slab(land, 0, d ^ m, self.size) for m in range(1, N_DEV)]


def _unique(arrays):
    out, index = [], {}
    for a in arrays:
        if a is not None and id(a) not in index:
            index[id(a)] = len(out)
            out.append(a)
    return out, index


def _sem_base(routes):
    base = [0]
    for r in routes:
        base.append(base[-1] + r.copies)
    return base


def _push_start(name, items):
    n = len(items)
    base = _sem_base([it[0] for it in items])
    arrays, index = _unique([it[1] for it in items] + [it[2] for it in items])
    na = len(arrays)

    def body(*refs):
        arr, send, recv, token = refs[:na], refs[na], refs[na + 1], refs[-1]
        place = _place()
        for i, (route, src, land) in enumerate(items):
            s_ref = None if src is None else arr[index[id(src)]]
            for j, (s, d, dev) in enumerate(route.sends(s_ref, arr[index[id(land)]], place)):
                pltpu.make_async_remote_copy(src_ref=s, dst_ref=d, send_sem=send.at[base[i] + j], recv_sem=recv.at[base[i] + j],
                                             device_id=dev, device_id_type=MESH).start()
        token[...] = jnp.zeros_like(token)

    res = pl.pallas_call(
        body, name=name,
        out_shape=[pltpu.SemaphoreType.DMA((base[-1],)), pltpu.SemaphoreType.DMA((base[-1],))]
        + [pltpu.HBM(a.shape, a.dtype) for a in arrays] + [jax.ShapeDtypeStruct((SUBLANES, LANES), f32)],
        in_specs=[HBM_SPEC] * na, out_specs=[SEM_SPEC, SEM_SPEC] + [HBM_SPEC] * na + [pl.BlockSpec(memory_space=pltpu.VMEM)],
        input_output_aliases={i: 2 + i for i in range(na)},
        compiler_params=pltpu.CompilerParams(has_side_effects=EFFECT),
    )(*[_hbm(a) for a in arrays])
    thru = lambda a: None if a is None else res[2 + index[id(a)]]
    return (res[0], res[1]), [thru(it[1]) for it in items], [thru(it[2]) for it in items], res[-1]


def _push_wait(name, groups, after):
    arrays, index = _unique([a for _, _, srcs, lands in groups for a in list(srcs) + list(lands)])
    na, ng = len(arrays), len(groups)

    def body(*refs):
        arr, sems = refs[:na], refs[na:na + 2 * ng]
        place = _place()
        for g, (routes, _, srcs, lands) in enumerate(groups):
            send, recv = sems[2 * g], sems[2 * g + 1]
            base = _sem_base(routes)
            for i, route in enumerate(routes):
                src, land = None if srcs[i] is None else arr[index[id(srcs[i])]], arr[index[id(lands[i])]]
                for j, ((s, d, dev), mine) in enumerate(zip(route.sends(src, land, place), route.recvs(land, place))):
                    cp = pltpu.make_async_remote_copy(src_ref=s, dst_ref=mine, send_sem=send.at[base[i] + j],
                                                      recv_sem=recv.at[base[i] + j], device_id=dev,
                                                      device_id_type=MESH)
                    cp.wait_send()
                    cp.wait_recv()

    sem_args = [s for g in groups for s in g[1]]
    res = pl.pallas_call(
        body, name=name, out_shape=[pltpu.HBM(a.shape, a.dtype) for a in arrays],
        in_specs=[HBM_SPEC] * na + [SEM_SPEC] * (2 * ng) + [ANY_SPEC] * len(after), out_specs=[HBM_SPEC] * na,
        input_output_aliases={i: i for i in range(na)},
        compiler_params=pltpu.CompilerParams(has_side_effects=EFFECT),
    )(*arrays, *sem_args, *after)
    return [[res[index[id(a)]] for a in g[3]] for g in groups]


def _place_slab(block, axis, slabs, idx, dtype, *, name):
    R, C = block.shape
    tm = _rows(R, C)
    nr = R // tm
    out_map = (lambda i, k: (i, k[0])) if axis == 1 else (lambda i, k: (k[0] * nr + i, 0))

    def body(k_ref, x_ref, o_ref):
        o_ref[...] = x_ref[...].astype(dtype)

    full = (R, C * slabs) if axis == 1 else (R * slabs, C)
    return pl.pallas_call(
        body, name=name, out_shape=jax.ShapeDtypeStruct(full, dtype),
        grid_spec=pltpu.PrefetchScalarGridSpec(
            num_scalar_prefetch=1, grid=(nr,), in_specs=[pl.BlockSpec((tm, C), lambda i, k: (i, 0))],
            out_specs=pl.BlockSpec((tm, C), out_map)),
        compiler_params=_cp("parallel"),
    )(idx, block)


def _swap_with_sibling(parts, tag):
    names = list(parts)

    def body(*refs):
        n = len(names)
        ins, outs, send, recv = refs[:n], refs[n:2 * n], refs[-2], refs[-1]
        x, y, c, _ = _place()
        cps = [pltpu.make_async_remote_copy(src_ref=ins[a], dst_ref=outs[a], send_sem=send.at[a], recv_sem=recv.at[a],
                                            device_id=(x, y, 1 - c), device_id_type=MESH) for a in range(n)]
        for cp in cps:
            cp.start()
        for cp in cps:
            cp.wait_recv()
        for cp in cps:
            cp.wait_send()

    res = pl.pallas_call(
        body, name=f"swap_with_sibling_{tag}", in_specs=[HBM_SPEC] * len(names), out_specs=[HBM_SPEC] * len(names),
        out_shape=[jax.ShapeDtypeStruct(parts[n].shape, parts[n].dtype) for n in names],
        scratch_shapes=[pltpu.SemaphoreType.DMA((len(names),)), pltpu.SemaphoreType.DMA((len(names),))],
    )(*[parts[n] for n in names])
    return dict(zip(names, res))


ELEMENTWISE_BLOCK_BYTES = 1 << 20


def _rows(r, c):
    for t in (512, 256, 128, 64, 32, 16, 8):
        if r % t == 0 and t * c * 4 <= ELEMENTWISE_BLOCK_BYTES:
            return t
    return r


def _sum4(owns, axis, recv, kidx, *, name):
    L = len(owns)
    R, C = recv.shape[2:]
    tm = _rows(R, C)
    nr = R // tm

    def body(k_ref, *refs):
        own_refs, r_ref, out_ref = refs[:L], refs[L], refs[L + 1]
        for li in range(L):
            @pl.when(pl.program_id(0) == li)
            def _(o_ref=own_refs[li]):
                out_ref[...] = ((o_ref[...] + r_ref[0, 0].astype(f32)) + r_ref[1, 0].astype(f32)) + r_ref[2, 0].astype(f32)

    own_map = (lambda l, i, k: (i, k[0])) if axis == 1 else (lambda l, i, k: (k[0] * nr + i, 0))
    return pl.pallas_call(
        body, name=name, out_shape=jax.ShapeDtypeStruct((L * R, C), f32),
        grid_spec=pltpu.PrefetchScalarGridSpec(
            num_scalar_prefetch=1, grid=(L, nr),
            in_specs=[pl.BlockSpec((tm, C), own_map)] * L + [pl.BlockSpec((3, 1, tm, C), lambda l, i, k: (0, l, i, 0))],
            out_specs=pl.BlockSpec((tm, C), lambda l, i, k: (l * nr + i, 0))),
        compiler_params=_cp("parallel", "parallel"),
    )(kidx, *owns, recv)


def _adamw(w, m, v, parts, *, name):
    R, C = w.shape
    tm = _rows(R, C)
    npart = len(parts)

    def body(*refs):
        w_ref, m_ref, v_ref = refs[:3]
        g_ref, d_ref, m2_ref, v2_ref = refs[3 + npart:]
        g = refs[3][...]
        for p_ref in refs[4:3 + npart]:
            g = g + p_ref[...]
        g_ref[...] = g
        d_ref[...], m2_ref[...], v2_ref[...] = _adam_math(w_ref[...], m_ref[...], v_ref[...], g)

    blk = pl.BlockSpec((tm, C), lambda i: (i, 0))
    return pl.pallas_call(
        body, name=name, grid=(R // tm,),
        in_specs=[blk] * (3 + npart), out_specs=[blk] * 4,
        out_shape=[jax.ShapeDtypeStruct((R, C), f32)] * 4, compiler_params=_cp("parallel"),
    )(w, m, v, *parts)


def _adam_math(w, m, v, g):
    m2 = ADAM_B1 * m + (1.0 - ADAM_B1) * g
    v2 = ADAM_B2 * v + (1.0 - ADAM_B2) * (g * g)
    m_hat = m2 / (1.0 - ADAM_B1 ** ADAM_STEP)
    v_hat = v2 / (1.0 - ADAM_B2 ** ADAM_STEP)
    return -ADAM_LR * (m_hat / (jnp.sqrt(v_hat) + ADAM_EPS) + ADAM_WD * w), m2, v2


def _adamw_small(landed, w, m, v, kidx, ra, rb):
    rs = ra + N_CHIPS * rb

    def body(k_ref, l_ref, w_ref, m_ref, v_ref, g_ref, d_ref, m2_ref, v2_ref):
        mine = pl.multiple_of(ra + k_ref[0] * rb, SUBLANES)
        for lo, n, off in ((0, ra, 0), (ra, rb, mine)):
            g = l_ref[pl.ds(off, n), :]
            for d in range(1, N_DEV):
                g = g + l_ref[pl.ds(d * rs + off, n), :]
            rows = pl.ds(lo, n)
            delta, m2, v2 = _adam_math(w_ref[rows, :], m_ref[rows, :], v_ref[rows, :], g)
            g_ref[rows, :] = g
            d_ref[rows, :] = delta
            m2_ref[rows, :] = m2
            v2_ref[rows, :] = v2

    vmem = pl.BlockSpec(memory_space=pltpu.VMEM)
    return pl.pallas_call(
        body, name="adamw_small", out_shape=[jax.ShapeDtypeStruct(w.shape, f32)] * 4,
        grid_spec=pltpu.PrefetchScalarGridSpec(num_scalar_prefetch=1, grid=(), in_specs=[vmem] * 4, out_specs=[vmem] * 4),
        compiler_params=_cp(),
    )(kidx, landed, w, m, v)


def _pad_odd(w):
    return jnp.pad(w, ((0, 0), (0, ODD_PAD - w.shape[1])))


def _uq_cat(w):
    r = w.shape[0]
    return jnp.pad(w.reshape(r, MLA_HEADS, MLA_QK), ((0, 0), (0, 0), (0, HQ - MLA_QK))).reshape(r, MLA_HEADS * HQ)


def _uq_uncat(w):
    r = w.shape[0]
    return w.reshape(r, MLA_HEADS, HQ)[:, :, :MLA_QK].reshape(r, MLA_HEADS * MLA_QK)


def _to_segments(v):
    T, C = v.shape
    return v.reshape(S5_SEG, T // S5_SEG, C).transpose(1, 0, 2).reshape(T, C)


def _from_segments(v):
    T, C = v.shape
    return v.reshape(T // S5_SEG, S5_SEG, C).transpose(1, 0, 2).reshape(T, C)


def _s5_rb(T):
    return min(512, T)


def _ffn_fwd(h, hn, w_in, cw, cb, w_out, tag, next_g=None):
    au = _mm(hn, w_in, out_dtype=bf16, name=f"ffn{tag}_in", tn=1408)
    z = _ffn_mid_fwd(au, cw, cb, name=f"ffn{tag}_mid")
    return _mm(z, w_out, res=h, norm_g=next_g, name=f"ffn{tag}_out", tk=1408), (hn, au, z)


def _ffn_bwd(h, g, w_in, cw, cb, w_out, saved, dh, tag, dep=None):
    hn, au, z = saved
    dz = _mm(dh, w_out, tb=True, out_dtype=bf16, name=f"ffn{tag}_dz", tn=1408, dep=dep)
    dw_out = _mm(z, dh, ta=True, also_bf16=True, name=f"ffn{tag}_dwout", tm=1408)
    dau, dcw, dcb = _ffn_mid_bwd(au, cw, cb, dz, name=f"ffn{tag}_dmid")
    dh_in, dg = _mm(dau, w_in, tb=True, res=dh, norm_bwd=(h, g), name=f"ffn{tag}_dhn", tk=1408)
    dw_in = _mm(hn, dau, ta=True, also_bf16=True, name=f"ffn{tag}_dwin", tn=1408)
    return dh_in, dg, dw_in, dcw, dcb, dw_out


def _local_step(x, positions, target, get_w, P, put_g):
    T = x.shape[0]
    rb = _s5_rb(T)
    row = lambda v: v.reshape(1, -1)
    g_mix, g_ffn = P["norm_mix_g"], P["norm_ffn_g"]
    lbl, hng = P["hgrn_lb_logits"], P["hgrn_norm_g"]
    dsk, bg = P["s5_d"], P["s5_b_glu"]
    qg, kvg = P["mla_q_norm_g"], P["mla_kv_norm_g"]
    cw, cb = P["ffn_conv_w"], P["ffn_conv_b"]

    col = lambda v: v.reshape(S5_N, 1)
    disc_in = (col(P["s5_a_re"]), col(P["s5_a_im"]), col(jnp.repeat(P["s5_log_dt"].reshape(S5_GROUPS), S5_STATE)),
               P["s5_b_re"].reshape(S5_N, S5_GROUP), P["s5_b_im"].reshape(S5_N, S5_GROUP))
    abr, abi, bbr, bbi = _s5_disc_fwd(*disc_in)
    ar, ai = abr.reshape(1, S5_N), abi.reshape(1, S5_N)
    bbr3, bbi3 = bbr.reshape(S5_GROUPS, S5_STATE, S5_GROUP), bbi.reshape(S5_GROUPS, S5_STATE, S5_GROUP)
    bre, bim = _blockdiag(bbr3, True).astype(bf16), _blockdiag(bbi3, True).astype(bf16)
    bret, bimt = _blockdiag(bbr3).astype(bf16), _blockdiag(bbi3).astype(bf16)
    c_re, c_im = P["s5_c_re"].reshape(S5_GROUPS, S5_GROUP, S5_STATE), P["s5_c_im"].reshape(S5_GROUPS, S5_GROUP, S5_STATE)
    cre, cim = _blockdiag(c_re, True).astype(bf16), _blockdiag(c_im, True).astype(bf16)
    cret, cimt = _blockdiag(c_re).astype(bf16), _blockdiag(c_im).astype(bf16)

    hn0 = _rms_fwd(x, g_mix[0:1], name="mix0_norm")
    We = get_w("even_in", hn0)
    proj_e = _mm(hn0, We["even_w_in"], name="even_in", tn=1280)
    Wr = get_w("even_rest", proj_e)
    ya, states = _hgrn_fwd(proj_e, lbl, hng)
    u_seg = _to_segments(proj_e[:, 4 * 512:])
    fr, fi = _s5_final(u_seg, bre, bim, ar, ai, rb=rb)
    yb_seg, s0r, s0i = _s5_fwd(u_seg, bre, bim, ar, ai, fr, fi, cre, cim, dsk, Wr["s5_w_glu"], bg, rb=rb)
    ycat = jnp.concatenate([ya, _from_segments(yb_seg)], axis=1)
    h1, hnf0 = _mm(ycat, Wr["even_w_out"], res=x, norm_g=g_ffn[0:1], name="even_out")
    Wf0 = get_w("ffn0", h1)
    (h2, hn2), ffn0 = _ffn_fwd(h1, hnf0, Wf0["ffn_w_in"], cw[0], cb[0:1], Wf0["ffn_w_out"], 0, next_g=g_mix[1:2])

    tabs = _rope_tables(positions)
    Wo = get_w("odd", hn2)
    proj_o = _mm(hn2, Wo["odd_w_in"], name="odd_in")
    cqn, ckvn, kr = _mla_prep_fwd(proj_o, qg, kvg, tabs)
    q = _q_post(_mm(cqn, Wo["mla_w_uq"], name="mla_uq"), tabs, transpose=False, name="q_post")
    kvb = _mm(ckvn, Wo["mla_w_ukv"], out_dtype=bf16, name="mla_ukv")
    o, lse = _flash_fwd(q, kvb, kr)
    h3, hnf1 = _mm(o, Wo["odd_w_out"], res=h2, norm_g=g_ffn[1:2], name="odd_out")
    Wf1 = get_w("ffn1", h3)
    h4, ffn1 = _ffn_fwd(h3, hnf1, Wf1["ffn_w_in"], cw[1], cb[1:2], Wf1["ffn_w_out"], 1)
    loss, dh4, dg_final = _loss_head(h4, row(P["final_norm_g"]), target)

    dh3, dg_ffn1, dw_fin1, dcw1, dcb1, dw_fout1 = _ffn_bwd(
        h3, g_ffn[1:2], Wf1["ffn_w_in"], cw[1], cb[1:2], Wf1["ffn_w_out"], ffn1, dh4, 1)
    sent = put_g("ffn1", {"ffn_w_in": dw_fin1, "ffn_w_out": dw_fout1})
    do = _mm(dh3, Wo["odd_w_out"], tb=True, out_dtype=bf16, name="odd_do", dep=sent)
    dw_oout = _mm(o, dh3, ta=True, also_bf16=True, name="odd_dwout")
    dkv, dkr_h, dq = _flash_bwd(q, kvb, kr, o, do, lse)
    dq = _q_post(dq, tabs, transpose=True, name="dq_post")
    dw_uq = _mm(cqn, dq, ta=True, also_bf16=True, name="mla_dwuq")
    dcqn = _mm(dq, Wo["mla_w_uq"], tb=True, name="mla_dcq")
    dw_ukv = _mm(ckvn, dkv, ta=True, also_bf16=True, name="mla_dwukv")
    dckvn = _mm(dkv, Wo["mla_w_ukv"], tb=True, name="mla_dckv")
    dproj_o, dqg, dkvg = _mla_prep_bwd(proj_o, qg, kvg, tabs, dcqn, dckvn, dkr_h)
    dw_oin = _mm(hn2, dproj_o, ta=True, also_bf16=True, name="odd_dwin")
    sent = put_g("odd", {"odd_w_in": dw_oin, "mla_w_uq": dw_uq, "mla_w_ukv": dw_ukv, "odd_w_out": dw_oout})
    dh2, dg_mix1 = _mm(dproj_o, Wo["odd_w_in"], tb=True, res=dh3, norm_bwd=(h2, g_mix[1:2]), name="odd_dhn")

    dh1, dg_ffn0, dw_fin0, dcw0, dcb0, dw_fout0 = _ffn_bwd(
        h1, g_ffn[0:1], Wf0["ffn_w_in"], cw[0], cb[0:1], Wf0["ffn_w_out"], ffn0, dh2, 0, dep=sent)
    sent = put_g("ffn0", {"ffn_w_in": dw_fin0, "ffn_w_out": dw_fout0})
    dycat = _mm(dh1, Wr["even_w_out"], tb=True, name="even_dy", dep=sent)
    dw_eout = _mm(ycat, dh1, ta=True, also_bf16=True, name="even_dwout")
    dq_h, df_h, di_h, dg_h, dlbl, dhng = _hgrn_bwd(proj_e, lbl, hng, states, dycat)
    dyb_seg = _to_segments(dycat[:, 512:])
    dy_s5, glr, gli, dcre, dcim, dd, dwg, dbg = _s5_bwd_a(
        u_seg, bre, bim, ar, ai, s0r, s0i, cre, cim, cret, cimt, dsk, Wr["s5_w_glu"], bg, dyb_seg, rb=rb)
    du_seg, dbre, dbim, dar, dai = _s5_bwd_b(
        u_seg, bre, bim, bret, bimt, ar, ai, s0r, s0i, glr, gli, cret, cimt, dsk, dy_s5, rb=rb)
    dproj_e = jnp.concatenate([dq_h, df_h, di_h, dg_h, _from_segments(du_seg)], axis=1)
    dx, dg_mix0 = _mm(dproj_e, We["even_w_in"], tb=True, res=dh1, norm_bwd=(x, g_mix[0:1]), name="even_dhn", tk=1280)
    dw_ein = _mm(hn0, dproj_e, ta=True, also_bf16=True, name="even_dwin", tn=1280)

    unblk = lambda m, a, b: jnp.swapaxes(_blockdiag_t(m, a, b), 1, 2)
    dbbr = unblk(dbre, S5_GROUP, S5_STATE).reshape(S5_N, S5_GROUP)
    dbbi = unblk(dbim, S5_GROUP, S5_STATE).reshape(S5_N, S5_GROUP)
    d_ar, d_ai, d_ldt, d_br, d_bi = _s5_disc_bwd(*disc_in, (dar.reshape(S5_N, 1), dai.reshape(S5_N, 1), dbbr, dbbi))
    small = {
        "norm_mix_g": jnp.concatenate([dg_mix0, dg_mix1], axis=0),
        "norm_ffn_g": jnp.concatenate([dg_ffn0, dg_ffn1], axis=0),
        "final_norm_g": dg_final.reshape(-1),
        "hgrn_lb_logits": dlbl, "hgrn_norm_g": dhng,
        "s5_a_re": d_ar.reshape(1, S5_GROUPS, S5_STATE), "s5_a_im": d_ai.reshape(1, S5_GROUPS, S5_STATE),
        "s5_log_dt": d_ldt.reshape(S5_GROUPS, S5_STATE).sum(axis=1).reshape(1, S5_GROUPS),
        "s5_b_re": d_br.reshape(1, S5_GROUPS, S5_STATE, S5_GROUP), "s5_b_im": d_bi.reshape(1, S5_GROUPS, S5_STATE, S5_GROUP),
        "s5_c_re": unblk(dcre, S5_STATE, S5_GROUP).reshape(1, S5_GROUPS, S5_GROUP, S5_STATE),
        "s5_c_im": unblk(dcim, S5_STATE, S5_GROUP).reshape(1, S5_GROUPS, S5_GROUP, S5_STATE),
        "s5_d": dd, "s5_b_glu": dbg, "mla_q_norm_g": dqg, "mla_kv_norm_g": dkvg,
        "ffn_conv_w": jnp.stack([dcw0, dcw1]), "ffn_conv_b": jnp.concatenate([dcb0, dcb1], axis=0),
    }
    put_g("even", {"even_w_in": dw_ein, "s5_w_glu": (dwg, dwg.astype(bf16)), "even_w_out": dw_eout}, small)
    return loss, dx


WEIGHTS = ["norm_mix_g", "norm_ffn_g", "final_norm_g", "even_w_in", "hgrn_lb_logits", "hgrn_norm_g", "s5_a_re", "s5_a_im",
           "s5_log_dt", "s5_b_re", "s5_b_im", "s5_c_re", "s5_c_im", "s5_d", "s5_w_glu", "s5_b_glu", "even_w_out", "odd_w_in",
           "mla_q_norm_g", "mla_w_uq", "mla_kv_norm_g", "mla_w_ukv", "odd_w_out", "ffn_w_in", "ffn_conv_w", "ffn_conv_b",
           "ffn_w_out"]
SMALL_SHARDED = {"mla_q_norm_g": 1, "mla_kv_norm_g": 1, "ffn_conv_w": 2}
SMALL = [n for n in WEIGHTS if n not in BIG]
SMALL_REP = [n for n in SMALL if n not in SMALL_SHARDED]


def _pack_rows(shapes):
    n = sum(math.prod(s) for s in shapes)
    return -(-n // (SUBLANES * LANES)) * SUBLANES


def _pack(arrays, rows):
    flat = jnp.concatenate([a.reshape(-1) for a in arrays])
    return jnp.pad(flat, (0, rows * LANES - flat.shape[0])).reshape(rows, LANES)


def _unpack(block, shapes):
    flat, out, off = block.reshape(-1), [], 0
    for s in shapes:
        n = math.prod(s)
        out.append(flat[off:off + n].reshape(s))
        off += n
    return out


def kernel(x, positions, norm_mix_g, norm_ffn_g, final_norm_g, even_w_in, hgrn_lb_logits, hgrn_norm_g, s5_a_re, s5_a_im, s5_log_dt, s5_b_re, s5_b_im, s5_c_re, s5_c_im, s5_d, s5_w_glu, s5_b_glu, even_w_out, odd_w_in, mla_q_norm_g, mla_w_uq, mla_kv_norm_g, mla_w_ukv, odd_w_out, ffn_w_in, ffn_conv_w, ffn_conv_b, ffn_w_out, loss_target, m_norm_mix_g, m_norm_ffn_g, m_final_norm_g, m_even_w_in, m_hgrn_lb_logits, m_hgrn_norm_g, m_s5_a_re, m_s5_a_im, m_s5_log_dt, m_s5_b_re, m_s5_b_im, m_s5_c_re, m_s5_c_im, m_s5_d, m_s5_w_glu, m_s5_b_glu, m_even_w_out, m_odd_w_in, m_mla_q_norm_g, m_mla_w_uq, m_mla_kv_norm_g, m_mla_w_ukv, m_odd_w_out, m_ffn_w_in, m_ffn_conv_w, m_ffn_conv_b, m_ffn_w_out, v_norm_mix_g, v_norm_ffn_g, v_final_norm_g, v_even_w_in, v_hgrn_lb_logits, v_hgrn_norm_g, v_s5_a_re, v_s5_a_im, v_s5_log_dt, v_s5_b_re, v_s5_b_im, v_s5_c_re, v_s5_c_im, v_s5_d, v_s5_w_glu, v_s5_b_glu, v_even_w_out, v_odd_w_in, v_mla_q_norm_g, v_mla_w_uq, v_mla_kv_norm_g, v_mla_w_ukv, v_odd_w_out, v_ffn_w_in, v_ffn_conv_w, v_ffn_conv_b, v_ffn_w_out):
    args = dict(locals())
    w = {n: args[n] for n in WEIGHTS}
    m = {n: args["m_" + n] for n in WEIGHTS}
    v = {n: args["v_" + n] for n in WEIGHTS}
    k = 2 * lax.axis_index("x") + lax.axis_index("y")
    kidx = k.reshape(1).astype(jnp.int32)
    axis2d = lambda n: BIG[n] - (1 if n in LAYERED else 0)
    slab = lambda n: w[n].shape[1 + axis2d(n)]

    small_sh_shapes = [w[n].shape for n in SMALL_SHARDED]
    rb = _pack_rows(small_sh_shapes)
    items = {}
    for group, names in GROUPS.items():
        layer = GROUP_LAYER.get(group, 0)
        items[group] = [(_Gather(axis2d(n), slab(n)), None,
                         _place_slab(w[n][layer], axis2d(n), N_CHIPS, kidx, bf16, name=f"place_{n}_{layer}")) for n in names]
    items["even_in"].append((_Gather(0, rb), None,
                             _place_slab(_pack([w[n] for n in SMALL_SHARDED], rb), 0, N_CHIPS, kidx, f32, name="place_small")))
    gathers, tokens = {}, []
    for group in GROUPS:
        sems, srcs, lands, token = _push_start(f"gather_start_{group}", items[group])
        gathers[group] = ([it[0] for it in items[group]], sems, srcs, lands)
        tokens.append(token[0, 0])
    started = functools.reduce(jnp.add, tokens)

    def landed(group, after):
        return _push_wait(f"gather_wait_{group}", [gathers[group]], [after])[0]

    even = landed("even_in", (started + norm_mix_g[0, 0]).reshape(1))
    per_chip = [_unpack(even[-1][c * rb:(c + 1) * rb], small_sh_shapes) for c in range(N_CHIPS)]
    P = {n: w[n] for n in SMALL_REP}
    for i, (n, ax) in enumerate(SMALL_SHARDED.items()):
        P[n] = jnp.concatenate([per_chip[c][i] for c in range(N_CHIPS)], axis=ax)
    P["mla_q_norm_g"], P["mla_kv_norm_g"] = P["mla_q_norm_g"].reshape(1, -1), P["mla_kv_norm_g"].reshape(1, -1)
    fix_w = {"odd_w_in": _pad_odd, "mla_w_uq": _uq_cat}

    def get_w(group, after):
        full = even if group == "even_in" else landed(group, after)
        return {n: fix_w.get(n, lambda a: a)(a) for n, a in zip(GROUPS[group], full)}

    fix_g = {"odd_w_in": lambda g: g[:, :odd_w_in.shape[2]], "mla_w_uq": _uq_uncat}
    g32, scatters, land_now = {}, {}, {}
    ra = _pack_rows([w[n].shape for n in SMALL_REP])
    rs = ra + N_CHIPS * rb
    didx = (2 * kidx + lax.axis_index("c")).astype(jnp.int32)

    def put_g(group, grads, small=None):
        layer = GROUP_LAYER.get(group)
        routes, srcs, names = [], [], list(grads)
        for n in names:
            f = fix_g.get(n, lambda g: g)
            g32.setdefault(n, {})[layer or 0] = f(grads[n][0])
            routes.append(_Scatter(axis2d(n), slab(n), layer if n in LAYERED else None))
            srcs.append(f(grads[n][1]))
            if n not in land_now:
                land_now[n] = lax.empty((3,) + w[n].shape[0 if n in LAYERED else 1:], bf16)
        if small is not None:
            blocks = [_pack([small[n] for n in SMALL_REP], ra)]
            for chip in range(N_CHIPS):
                sl = lambda n, ax: lax.slice_in_dim(small[n].reshape(w[n].shape[:ax] + (-1,) + w[n].shape[ax + 1:]),
                                                    chip * w[n].shape[ax], (chip + 1) * w[n].shape[ax], axis=ax)
                blocks.append(_pack([sl(n, ax) for n, ax in SMALL_SHARDED.items()], rb))
            names.append("small")
            routes.append(_ToAll(rs))
            srcs.append(None)
            land_now["small"] = _place_slab(jnp.concatenate(blocks), 0, N_DEV, didx, f32, name="place_small_grads")
        sems, srcs, lands, token = _push_start(f"scatter_start_{group}", [(r, s, land_now[n]) for r, s, n in zip(routes, srcs, names)])
        land_now.update(zip(names, lands))
        scatters[group] = (routes, sems, srcs, names)
        sent.append(token)
        return token

    sent = []
    loss, dx = _local_step(x[0], positions[0], loss_target[0], get_w, P, put_g)
    sent_last = sent[-1]
    loss = lax.psum(loss[0, 0], ("x", "y", "c"))

    out = {}

    def finish(tag, groups, after):
        waits = [(scatters[g][0], scatters[g][1], scatters[g][2], [land_now[n] for n in scatters[g][3]]) for g in groups]
        for g, lands in zip(groups, _push_wait(f"scatter_wait_{tag}", waits, after)):
            land_now.update(zip(scatters[g][3], lands))
        names = [n for n in dict.fromkeys(n for g in groups for n in scatters[g][3]) if n != "small"]
        part = {}
        for n in names:
            recv = land_now[n] if n in LAYERED else land_now[n][:, None]
            part[n] = _sum4([g32[n][l] for l in sorted(g32[n])], axis2d(n), recv, kidx, name=f"sum4_{n}")
        other = _swap_with_sibling(part, tag)
        done = []
        for n in names:
            C = part[n].shape[-1]
            res = _adamw(w[n].reshape(-1, C), m[n].reshape(-1, C), v[n].reshape(-1, C), [part[n], other[n]], name=f"adamw_{n}")
            out[n] = [r.reshape(w[n].shape) for r in res]
            done.append(res[0])
        return done

    done = finish("a", ["ffn1", "odd", "ffn0"], [dx, sent_last])
    finish("b", ["even"], done)

    order = SMALL_REP + list(SMALL_SHARDED)
    packed = lambda src: jnp.concatenate([_pack([src[n] for n in SMALL_REP], ra), _pack([src[n] for n in SMALL_SHARDED], rb)])
    res = _adamw_small(land_now["small"], packed(w), packed(m), packed(v), kidx, ra, rb)
    for r in res:
        parts = _unpack(r[:ra], [w[n].shape for n in SMALL_REP]) + _unpack(r[ra:], small_sh_shapes)
        for n, a in zip(order, parts):
            out.setdefault(n, []).append(a)

    return (loss, dx[None], *[out[n][0] for n in WEIGHTS], *[out[n][1] for n in WEIGHTS],
            *[out[n][2] for n in WEIGHTS], *[out[n][3] for n in WEIGHTS])
```

```python
import functools
import math

import jax
import jax.numpy as jnp
from jax import lax
from jax.experimental import pallas as pl
from jax.experimental.pallas import tpu as pltpu

f32, bf16 = jnp.float32, jnp.bfloat16
EPS = 1e-6
LANES = 128
SUBLANES = 8
VMEM_BYTES = 48 * 1024 * 1024
HGRN_CHUNK = 64
HGRN_HEADS = 4
S5_GROUPS, S5_STATE, S5_GROUP = 32, 64, 16
S5_N = S5_GROUPS * S5_STATE
S5_SEG = SUBLANES
MLA_HEADS, MLA_NOPE, MLA_ROPE, MLA_V = 8, 128, 64, 128
MLA_QK = MLA_NOPE + MLA_ROPE
MLA_Q_RANK, MLA_KV_RANK = 384, 256
ROPE_THETA = 10000.0
D_FF = 2816
ADAM_LR, ADAM_B1, ADAM_B2, ADAM_EPS, ADAM_WD, ADAM_STEP = 0.001, 0.9, 0.999, 1e-08, 0.01, 10
MESH = pl.DeviceIdType.MESH
HI = lax.Precision.HIGHEST


def _cp(*dims):
    return pltpu.CompilerParams(dimension_semantics=dims if dims else None, vmem_limit_bytes=VMEM_BYTES)


def _tile(n, t):
    if n <= t:
        return n
    c = (t // LANES) * LANES
    while c >= LANES:
        if n % c == 0:
            return c
        c -= LANES
    return n


def _dot(a, b, dn=None, precision=None):
    if dn is None:
        dn = (((a.ndim - 1,), (0,)), ((), ()))
    return lax.dot_general(a, b, dn, preferred_element_type=f32, precision=precision)


NT = (((1,), (1,)), ((), ()))
TN = (((0,), (0,)), ((), ()))


def _bdot(a, b, dn=None):
    return _dot(a.astype(bf16), b.astype(bf16), dn)


MM_PARTS = 2
ROWS_MODE_BIG = 4 * 1024 * 1024


def _mm(a, b, *, name, ta=False, tb=False, out_dtype=f32, res=None, also_bf16=False, tm=1024, tn=1024, tk=1024, dep=None,
        norm_g=None, norm_bwd=None):
    halves = lambda s: (s[1], 2 * s[2]) if len(s) == 3 else s
    M, K = (a.shape[1], a.shape[0]) if ta else halves(a.shape)
    N = b.shape[0] if tb else halves(b.shape)[1]
    rows = norm_g is not None or norm_bwd is not None
    if rows:
        tm, tn, tk = (256 if K * N > ROWS_MODE_BIG else 512), N, K
    tm, tn, tk = _tile(M, tm), _tile(N, tn), _tile(K, tk)
    both = rows and a.ndim == 3 and tb
    if a.ndim == 3 and not both:
        tk = _tile(K // 2, tk)
    if b.ndim == 3:
        tn = _tile(N // 2, tn)
    nk = K // tk
    parts = MM_PARTS if tm % (MM_PARTS * LANES) == 0 else 1
    dn = (((0 if ta else 1,), (1 if tb else 0,)), ((), ()))
    extra = [] if norm_bwd is None else list(norm_bwd)
    if norm_g is not None:
        extra.append(norm_g)

    def body(*refs):
        a_ref, b_ref = refs[0], refs[1]
        r_ref = refs[2] if res is not None else None
        nin = 2 + (res is not None) + (dep is not None) + len(extra)
        ex = refs[nin - len(extra):nin]
        outs = refs[nin:-1] if nk > 1 else refs[nin:]
        acc = refs[-1] if nk > 1 else None
        k = pl.program_id(2)
        b_blk = b_ref[...]
        if nk > 1:
            @pl.when(k == 0)
            def _():
                acc[...] = jnp.zeros_like(acc)

        groups = []
        for part in range(parts):
            rows = pl.ds(part * (tm // parts), tm // parts)
            if both:
                p = _bdot(a_ref[0, rows, :], b_blk[:, :K // 2], dn) + _bdot(a_ref[1, rows, :], b_blk[:, K // 2:], dn)
            else:
                p = _bdot(a_ref[:, rows] if ta else a_ref[rows, :], b_blk, dn)
            if nk > 1:
                acc[rows, :] += p
            groups.append((rows, p))

        def epilogue():
            for part, (rows, p) in enumerate(groups):
                r = acc[rows, :] if nk > 1 else p
                if norm_bwd is not None:
                    r, dg = _rms_bwd_math(ex[0][rows, :], ex[1][...], r)
                    if part == 0:
                        @pl.when(pl.program_id(0) == 0)
                        def _(dg=dg):
                            outs[1][...] = dg

                    @pl.when((pl.program_id(0) > 0) | (part > 0))
                    def _(dg=dg):
                        outs[1][...] += dg
                if r_ref is not None:
                    r = r + r_ref[rows, :]
                outs[0][rows, :] = r.astype(out_dtype)
                if also_bf16:
                    outs[1][rows, :] = r.astype(bf16)
                if norm_g is not None:
                    outs[1][rows, :] = _rms(r, ex[-1][...]).astype(bf16)

        if nk > 1:
            pl.when(k == nk - 1)(epilogue)
        else:
            epilogue()

    a_spec = pl.BlockSpec((tk, tm), lambda i, j, k: (k, i)) if ta else pl.BlockSpec((tm, tk), lambda i, j, k: (i, k))
    b_spec = pl.BlockSpec((tn, tk), lambda i, j, k: (j, k)) if tb else pl.BlockSpec((tk, tn), lambda i, j, k: (k, j))
    if both:
        a_spec = pl.BlockSpec((2, tm, K // 2), lambda i, j, k: (0, i, 0))
    elif a.ndim == 3:
        kh = K // 2 // tk
        a_spec = pl.BlockSpec((None, tm, tk), lambda i, j, k: (k // kh, i, k % kh))
    if b.ndim == 3:
        nh = N // 2 // tn
        b_spec = pl.BlockSpec((None, tk, tn), lambda i, j, k: (j // nh, k, j % nh))
    o_spec = pl.BlockSpec((tm, tn), lambda i, j, k: (i, j))
    in_specs, args = [a_spec, b_spec], [a, b]
    if res is not None:
        in_specs.append(o_spec)
        args.append(res)
    if dep is not None:
        in_specs.append(pl.BlockSpec(memory_space=pl.ANY))
        args.append(dep)
    vec = pl.BlockSpec((1, tn), lambda i, j, k: (0, j))
    if norm_bwd is not None:
        in_specs += [o_spec, vec]
    if norm_g is not None:
        in_specs.append(vec)
    args += extra
    out_shape = [jax.ShapeDtypeStruct((M, N), out_dtype)]
    out_specs = [o_spec]
    if also_bf16 or norm_g is not None:
        out_shape.append(jax.ShapeDtypeStruct((M, N), bf16))
        out_specs.append(o_spec)
    if norm_bwd is not None:
        out_shape.append(jax.ShapeDtypeStruct((1, N), f32))
        out_specs.append(vec)
    dims = ("arbitrary" if norm_bwd is not None else "parallel", "parallel", "arbitrary")
    out = pl.pallas_call(
        body, name=name, grid=(M // tm, N // tn, nk), in_specs=in_specs, out_specs=out_specs, out_shape=out_shape,
        scratch_shapes=[pltpu.VMEM((tm, tn), f32)] if nk > 1 else [], compiler_params=_cp(*dims),
    )(*args)
    return out if len(out) > 1 else out[0]


def _rms_fwd(x, g, *, name, col=0, width=None, tm=512):
    T = x.shape[0]
    width = x.shape[1] if width is None else width
    tm = _tile(T, tm)

    def body(x_ref, g_ref, o_ref):
        xv = x_ref[...]
        r = lax.rsqrt(jnp.mean(xv * xv, axis=-1, keepdims=True) + EPS)
        o_ref[...] = (xv * r * g_ref[...]).astype(bf16)

    return pl.pallas_call(
        body, name=name, grid=(T // tm,),
        in_specs=[pl.BlockSpec((tm, width), lambda i: (i, col)), pl.BlockSpec((1, width), lambda i: (0, 0))],
        out_specs=pl.BlockSpec((tm, width), lambda i: (i, 0)), out_shape=jax.ShapeDtypeStruct((T, width), bf16),
        compiler_params=_cp("parallel"),
    )(x, g)


def _rms_bwd_math(xv, g, dy):
    r = lax.rsqrt(jnp.mean(xv * xv, axis=-1, keepdims=True) + EPS)
    xh = xv * r
    dxh = dy * g
    dx = r * (dxh - xh * jnp.mean(dxh * xh, axis=-1, keepdims=True))
    dg = jnp.sum(dy * xh, axis=0, keepdims=True)
    return dx, dg


def _loss_head(h, g, target, *, tm=512):
    T, D = h.shape
    tm = _tile(T, tm)

    def body(h_ref, g_ref, t_ref, loss_ref, dh_ref, dg_ref):
        hv, gv = h_ref[...], g_ref[...]
        r = lax.rsqrt(jnp.mean(hv * hv, axis=-1, keepdims=True) + EPS)
        e = hv * r * gv - t_ref[...]
        part = 0.5 * jnp.sum(jnp.mean(e * e, axis=-1, keepdims=True), axis=0, keepdims=True)
        dx, dg = _rms_bwd_math(hv, gv, e * (1.0 / D))
        dh_ref[...] = dx

        @pl.when(pl.program_id(0) == 0)
        def _():
            loss_ref[...] = part
            dg_ref[...] = dg

        @pl.when(pl.program_id(0) > 0)
        def _():
            loss_ref[...] += part
            dg_ref[...] += dg

    row = pl.BlockSpec((tm, D), lambda i: (i, 0))
    vec = pl.BlockSpec((1, D), lambda i: (0, 0))
    return pl.pallas_call(
        body, name="loss_head", grid=(T // tm,), in_specs=[row, vec, row],
        out_specs=[pl.BlockSpec((1, 1), lambda i: (0, 0)), row, vec],
        out_shape=[jax.ShapeDtypeStruct((1, 1), f32), jax.ShapeDtypeStruct((T, D), f32), jax.ShapeDtypeStruct((1, D), f32)],
        compiler_params=_cp("arbitrary"),
    )(h, g, target)


FFN_W = 2 * LANES
FFN_ROWS = 128
HALO = 2 * SUBLANES


def _conv_taps(a_ref, c, rc):
    if isinstance(c, int) and c == 0:
        ext = jnp.concatenate([jnp.zeros((HALO, FFN_W), f32), a_ref[pl.ds(0, rc), :].astype(f32)], axis=0)
    else:
        ext = a_ref[pl.ds(pl.multiple_of(c * rc - HALO, HALO), rc + HALO), :].astype(f32)
    return ext[HALO:], pltpu.roll(ext, 1, 0)[HALO:], pltpu.roll(ext, 2, 0)[HALO:]


def _chunk_rows(c, rc):
    return pl.ds(c * rc, rc) if isinstance(c, int) else pl.ds(pl.multiple_of(c * rc, rc), rc)


def _ffn_mid_fwd(au, cw, cb, *, name):
    T = au.shape[0]
    F = au.shape[1] // 2
    nb = F // FFN_W
    rc = min(FFN_ROWS, T)
    nc = T // rc

    def body(a_ref, u_ref, w_ref, b_ref, z_ref):
        w, b = w_ref[...], b_ref[...]

        def chunk(c):
            a, a1, a2 = _conv_taps(a_ref, c, rc)
            rows = _chunk_rows(c, rc)
            ac = (w[0:1] * a2 + w[1:2] * a1 + w[2:3] * a + b).astype(bf16)
            z_ref[rows, :] = ac * jax.nn.sigmoid(ac) * u_ref[rows, :]

        chunk(0)
        lax.fori_loop(1, nc, lambda c, _: chunk(c), None)

    return pl.pallas_call(
        body, name=name, grid=(nb,),
        in_specs=[pl.BlockSpec((T, FFN_W), lambda j: (0, j)), pl.BlockSpec((T, FFN_W), lambda j: (0, nb + j)),
                  pl.BlockSpec((3, FFN_W), lambda j: (0, j)), pl.BlockSpec((1, FFN_W), lambda j: (0, j))],
        out_specs=pl.BlockSpec((T, FFN_W), lambda j: (0, j)), out_shape=jax.ShapeDtypeStruct((T, F), bf16),
        compiler_params=_cp("parallel"),
    )(au, au, cw, cb)


def _ffn_mid_bwd(au, cw, cb, dz, *, name):
    T = au.shape[0]
    F = au.shape[1] // 2
    nb = F // FFN_W
    rc = min(FFN_ROWS, T)
    nc = T // rc

    def body(a_ref, u_ref, w_ref, b_ref, dz_ref, dau_ref, dw_ref, db_ref):
        w, b = w_ref[...], b_ref[...]

        def chunk(c, carry):
            nxt, s0, s1, s2, sb = carry
            a, a1, a2 = _conv_taps(a_ref, c, rc)
            rows = _chunk_rows(c, rc)
            ac = (w[0:1] * a2 + w[1:2] * a1 + w[2:3] * a + b).astype(bf16)
            sg = jax.nn.sigmoid(ac)
            dz = dz_ref[rows, :]
            dau_ref[1, rows, :] = dz * ac * sg
            dac = (dz * u_ref[rows, :] * sg * (1.0 + ac * (1.0 - sg))).astype(f32)
            ext = jnp.concatenate([dac, nxt], axis=0)
            d1, d2 = pltpu.roll(ext, rc + HALO - 1, 0)[:rc], pltpu.roll(ext, rc + HALO - 2, 0)[:rc]
            dau_ref[0, rows, :] = (w[2:3] * dac + w[1:2] * d1 + w[0:1] * d2).astype(bf16)
            tot = lambda v: jnp.sum(v, axis=0, keepdims=True)
            return dac[:HALO], s0 + tot(dac * a2), s1 + tot(dac * a1), s2 + tot(dac * a), sb + tot(dac)

        z = jnp.zeros((1, FFN_W), f32)
        carry = (jnp.zeros((HALO, FFN_W), f32), z, z, z, z)
        carry = lax.fori_loop(0, nc - 1, lambda k, cr: chunk(nc - 1 - k, cr), carry)
        _, s0, s1, s2, sb = chunk(0, carry)
        rows = lax.broadcasted_iota(jnp.int32, (3, FFN_W), 0)
        dw_ref[...] = jnp.where(rows == 0, s0, jnp.where(rows == 1, s1, s2))
        db_ref[...] = sb

    col = lambda off: pl.BlockSpec((T, FFN_W), lambda j: (0, off + j))
    return pl.pallas_call(
        body, name=name, grid=(nb,),
        in_specs=[col(0), col(nb), pl.BlockSpec((3, FFN_W), lambda j: (0, j)), pl.BlockSpec((1, FFN_W), lambda j: (0, j)), col(0)],
        out_specs=[pl.BlockSpec((2, T, FFN_W), lambda j: (0, 0, j)), pl.BlockSpec((3, FFN_W), lambda j: (0, j)),
                   pl.BlockSpec((1, FFN_W), lambda j: (0, j))],
        out_shape=[jax.ShapeDtypeStruct((2, T, F), bf16), jax.ShapeDtypeStruct((3, F), f32), jax.ShapeDtypeStruct((1, F), f32)],
        compiler_params=_cp("parallel"),
    )(au, au, cw, cb, dz)


BNN = (((2,), (1,)), ((0,), (0,)))
BNT = (((2,), (2,)), ((0,), (0,)))
BTN = (((1,), (1,)), ((0,), (0,)))


def _heads(x):
    return jnp.stack([x[:, h * LANES:(h + 1) * LANES] for h in range(HGRN_HEADS)])


def _put_heads(ref, rows, x, dtype):
    for h in range(HGRN_HEADS):
        ref[rows, h * LANES:(h + 1) * LANES] = x[h].astype(dtype)


def _hgrn_lb(l):
    m = jnp.max(l, axis=0, keepdims=True)
    e = jnp.exp(l - m)
    return e[0:1] / jnp.sum(e, axis=0, keepdims=True)


def _hgrn_chunk(q, fx, lb):
    H, C = q.shape[0], q.shape[1]
    sg = jax.nn.sigmoid(fx)
    F = lb + (1.0 - lb) * sg
    k = 1.0 - F
    logF = jnp.log(F)
    r = lax.broadcasted_iota(jnp.int32, (H, C, C), 1)
    c = lax.broadcasted_iota(jnp.int32, (H, C, C), 2)
    tril = (r >= c)
    b = _dot(tril.astype(f32), logF, BNN, precision=HI)
    bl = jnp.sum(logF, axis=1, keepdims=True)
    eb = jnp.exp(b)
    enb = jnp.exp(-b)
    elb = jnp.exp(bl - b)
    return dict(sg=sg, F=F, k=k, b=b, bl=bl, eb=eb, enb=enb, elb=elb, qd=q * eb, kd=k * enb, kl=k * elb, tril=tril)


def _hgrn_fwd(proj, lbl, ng, *, rb=512):
    T = proj.shape[0]
    rb = min(rb, T)
    cpb = rb // HGRN_CHUNK
    nblk = T // rb
    H = HGRN_HEADS

    def body(q_ref, f_ref, i_ref, g_ref, lbl_ref, ng_ref, y_ref, st_ref, S):
        @pl.when(pl.program_id(0) == 0)
        def _():
            S[...] = jnp.zeros_like(S)

        lb = _heads(_hgrn_lb(lbl_ref[...]))
        ngv = _heads(ng_ref[...])
        for c in range(cpb):
            sl = pl.ds(c * HGRN_CHUNK, HGRN_CHUNK)
            v, gx = _heads(i_ref[sl, :]), _heads(g_ref[sl, :])
            ch = _hgrn_chunk(_heads(q_ref[sl, :]), _heads(f_ref[sl, :]), lb)
            att = jnp.where(ch["tril"], _bdot(ch["qd"], ch["kd"], BNT), 0.0)
            St = S[...]
            st_ref[:, c] = St
            o = _bdot(att, v, BNN) + _bdot(ch["qd"], St, BNT)
            S[...] = St * jnp.exp(ch["bl"]) + _bdot(v, ch["kl"], BTN)
            r = lax.rsqrt(jnp.mean(o * o, axis=-1, keepdims=True) + EPS)
            _put_heads(y_ref, sl, o * r * ngv * (gx * jax.nn.sigmoid(gx)), bf16)

    col = lambda off: pl.BlockSpec((rb, H * LANES), lambda n: (n, off))
    return pl.pallas_call(
        body, name="hgrn_fwd", grid=(nblk,),
        in_specs=[col(0), col(1), col(2), col(3), pl.BlockSpec((2, H * LANES), lambda n: (0, 0)),
                  pl.BlockSpec((1, H * LANES), lambda n: (0, 0))],
        out_specs=[pl.BlockSpec((rb, H * LANES), lambda n: (n, 0)),
                   pl.BlockSpec((H, cpb, LANES, LANES), lambda n: (0, n, 0, 0))],
        out_shape=[jax.ShapeDtypeStruct((T, H * LANES), bf16),
                   jax.ShapeDtypeStruct((H, T // HGRN_CHUNK, LANES, LANES), f32)],
        scratch_shapes=[pltpu.VMEM((H, LANES, LANES), f32)], compiler_params=_cp("arbitrary"),
    )(proj, proj, proj, proj, lbl, ng)


def _hgrn_bwd(proj, lbl, ng, states, dy, *, rb=512):
    T = proj.shape[0]
    rb = min(rb, T)
    cpb = rb // HGRN_CHUNK
    nblk = T // rb
    H = HGRN_HEADS
    C = HGRN_CHUNK

    def body(q_ref, f_ref, i_ref, g_ref, lbl_ref, ng_ref, st_ref, dy_ref,
             dq_ref, df_ref, di_ref, dg_ref, dl_ref, dng_ref, dS, dlb_acc, dng_acc):
        n = pl.program_id(0)

        @pl.when(n == 0)
        def _():
            dS[...] = jnp.zeros_like(dS)
            dlb_acc[...] = jnp.zeros_like(dlb_acc)
            dng_acc[...] = jnp.zeros_like(dng_acc)

        lb_row = _hgrn_lb(lbl_ref[...])
        lb = _heads(lb_row)
        ngv = _heads(ng_ref[...])
        r_i = lax.broadcasted_iota(jnp.int32, (H, C, C), 1)
        c_i = lax.broadcasted_iota(jnp.int32, (H, C, C), 2)
        triu = (c_i >= r_i).astype(f32)
        rows_sum = lambda x: jnp.sum(x, axis=1, keepdims=True)
        for c in reversed(range(cpb)):
            sl = pl.ds(c * C, C)
            q, v, gx = _heads(q_ref[sl, :]), _heads(i_ref[sl, :]), _heads(g_ref[sl, :])
            ch = _hgrn_chunk(q, _heads(f_ref[sl, :]), lb)
            qd, kd, kl = ch["qd"], ch["kd"], ch["kl"]
            att = jnp.where(ch["tril"], _bdot(qd, kd, BNT), 0.0)
            St = st_ref[:, c]
            o = _bdot(att, v, BNN) + _bdot(qd, St, BNT)
            r = lax.rsqrt(jnp.mean(o * o, axis=-1, keepdims=True) + EPS)
            on = o * r
            sgg = jax.nn.sigmoid(gx)
            gate = gx * sgg
            dyv = _heads(dy_ref[sl, :].astype(f32))
            _put_heads(dg_ref, sl, dyv * on * ngv * sgg * (1.0 + gx * (1.0 - sgg)), bf16)
            dng_acc[...] += rows_sum(dyv * on * gate)
            don = dyv * ngv * gate
            do = r * (don - on * jnp.mean(don * on, axis=-1, keepdims=True))
            dSt = dS[...]
            dA = jnp.where(ch["tril"], _bdot(do, v, BNT), 0.0)
            dv = _bdot(att, do, BTN) + _bdot(kl, dSt, BNT)
            dqd = _bdot(dA, kd, BNN) + _bdot(do, St, BNN)
            dkd = _bdot(dA, qd, BTN)
            dkl = _bdot(v, dSt, BNN)
            dec = jnp.exp(ch["bl"])
            ddec = rows_sum(St * dSt)
            dS[...] = _bdot(do, qd, BTN) + dSt * dec
            dB = dqd * qd - dkd * kd - dkl * kl
            dbl = rows_sum(dkl * kl) + ddec * dec
            dk = dkd * ch["enb"] + dkl * ch["elb"]
            dlogF = _dot(triu, dB, BNN, precision=HI) + dbl
            dF = dlogF / ch["F"] - dk
            sg = ch["sg"]
            _put_heads(dq_ref, sl, dqd * ch["eb"], bf16)
            _put_heads(di_ref, sl, dv, bf16)
            _put_heads(df_ref, sl, dF * (1.0 - lb) * sg * (1.0 - sg), bf16)
            dlb_acc[...] += rows_sum(dF * (1.0 - sg))

        @pl.when(n == nblk - 1)
        def _():
            rows = lax.broadcasted_iota(jnp.int32, (2, LANES), 0)
            for h in range(H):
                hs = pl.ds(h * LANES, LANES)
                lbh = lb_row[:, h * LANES:(h + 1) * LANES]
                dl0 = dlb_acc[h] * lbh * (1.0 - lbh)
                dl_ref[:, hs] = jnp.where(rows == 0, dl0, -dl0)
                dng_ref[:, hs] = dng_acc[h]

    col = lambda off: pl.BlockSpec((rb, H * LANES), lambda n: (nblk - 1 - n, off))
    vec = lambda rows: pl.BlockSpec((rows, H * LANES), lambda n: (0, 0))
    tok = jax.ShapeDtypeStruct((T, H * LANES), bf16)
    return pl.pallas_call(
        body, name="hgrn_bwd", grid=(nblk,),
        in_specs=[col(0), col(1), col(2), col(3), vec(2), vec(1),
                  pl.BlockSpec((H, cpb, LANES, LANES), lambda n: (0, nblk - 1 - n, 0, 0)), col(0)],
        out_specs=[col(0), col(0), col(0), col(0), vec(2), vec(1)],
        out_shape=[tok, tok, tok, tok, jax.ShapeDtypeStruct((2, H * LANES), f32), jax.ShapeDtypeStruct((1, H * LANES), f32)],
        scratch_shapes=[pltpu.VMEM((H, LANES, LANES), f32), pltpu.VMEM((H, 1, LANES), f32), pltpu.VMEM((H, 1, LANES), f32)],
        compiler_params=_cp("arbitrary"),
    )(proj, proj, proj, proj, lbl, ng, states, dy)


def _s5_disc_math(ar, ai, ldt, br, bi):
    dt = jnp.exp(ldt)
    mag = jnp.exp(ar * dt)
    abr, abi = mag * jnp.cos(ai * dt), mag * jnp.sin(ai * dt)
    den = ar * ar + ai * ai
    xr, xi = abr - 1.0, abi
    cr = (xr * ar + xi * ai) / den
    ci = (xi * ar - xr * ai) / den
    return abr, abi, cr * br - ci * bi, cr * bi + ci * br


def _s5_disc_fwd(ar, ai, ldt, br, bi):
    def body(ar_ref, ai_ref, ldt_ref, br_ref, bi_ref, o0, o1, o2, o3):
        outs = _s5_disc_math(ar_ref[...], ai_ref[...], ldt_ref[...], br_ref[...], bi_ref[...])
        for o, v in zip((o0, o1, o2, o3), outs):
            o[...] = v

    return pl.pallas_call(
        body, name="s5_disc_fwd",
        out_shape=[jax.ShapeDtypeStruct(ar.shape, f32)] * 2 + [jax.ShapeDtypeStruct(br.shape, f32)] * 2,
    )(ar, ai, ldt, br, bi)


def _s5_disc_bwd(ar, ai, ldt, br, bi, cts):
    def body(ar_ref, ai_ref, ldt_ref, br_ref, bi_ref, c0, c1, c2, c3, o0, o1, o2, o3, o4):
        _, vjp = jax.vjp(_s5_disc_math, ar_ref[...], ai_ref[...], ldt_ref[...], br_ref[...], bi_ref[...])
        for o, v in zip((o0, o1, o2, o3, o4), vjp((c0[...], c1[...], c2[...], c3[...]))):
            o[...] = v

    return pl.pallas_call(
        body, name="s5_disc_bwd",
        out_shape=[jax.ShapeDtypeStruct(ar.shape, f32)] * 3 + [jax.ShapeDtypeStruct(br.shape, f32)] * 2,
    )(ar, ai, ldt, br, bi, *cts)


S5_LC = 512
S5_NLC = S5_N // S5_LC
S5_UB = 4
S5_UNROLL = 4
S5_TOGETHER = 2


def _cmul(ar, ai, xr, xi):
    return ar * xr - ai * xi, ar * xi + ai * xr


def _cpow(ar, ai, n):
    rr, ri = None, None
    br, bi = ar, ai
    while n:
        if n & 1:
            rr, ri = (br, bi) if rr is None else _cmul(rr, ri, br, bi)
        n >>= 1
        if n:
            br, bi = _cmul(br, bi, br, bi)
    return rr, ri


def _s5_bu(u_ref, bre_ref, bim_ref, xr, xi):
    for k in range(S5_UB):
        uk = u_ref[:, k * LANES:(k + 1) * LANES].astype(bf16)
        xr[:, k * S5_LC:(k + 1) * S5_LC] = _dot(uk, bre_ref[k])
        xi[:, k * S5_LC:(k + 1) * S5_LC] = _dot(uk, bim_ref[k])


def _s5_scan(xr, xi, sr, si, ar_ref, ai_ref, nsteps, store):
    for c0 in range(0, S5_NLC, S5_TOGETHER):
        css = [slice(c * S5_LC, (c + 1) * S5_LC) for c in range(c0, c0 + S5_TOGETHER)]
        a = [(jnp.broadcast_to(ar_ref[:, cs], (S5_SEG, S5_LC)), jnp.broadcast_to(ai_ref[:, cs], (S5_SEG, S5_LC))) for cs in css]

        def step(j, carry, css=css, a=a):
            rows = pl.ds(pl.multiple_of(j * S5_SEG, S5_SEG), S5_SEG)
            out = []
            for u, cs in enumerate(css):
                (a_r, a_i), pr, pi = a[u], carry[2 * u], carry[2 * u + 1]
                nr = a_r * pr - a_i * pi + xr[rows, cs]
                ni = a_r * pi + a_i * pr + xi[rows, cs]
                if store:
                    xr[rows, cs] = nr
                    xi[rows, cs] = ni
                out += [nr, ni]
            return tuple(out)

        init = tuple(v for cs in css for v in (sr[:, cs], si[:, cs]))
        fin = lax.fori_loop(0, nsteps, step, init)
        for u, cs in enumerate(css):
            sr[:, cs] = fin[2 * u]
            si[:, cs] = fin[2 * u + 1]


def _s5_rscan(dr, di, xr, xi, s0r, s0i, gr, gi, acc_r, acc_i, ar_ref, ai_ref, nsteps):
    for c in range(S5_NLC):
        cs = slice(c * S5_LC, (c + 1) * S5_LC)
        a_r = jnp.broadcast_to(ar_ref[:, cs], (S5_SEG, S5_LC))
        a_i = jnp.broadcast_to(ai_ref[:, cs], (S5_SEG, S5_LC))

        def step(jj, carry, cs=cs, a_r=a_r, a_i=a_i):
            pr, pi, cr, ci = carry
            j = nsteps - 1 - jj
            rows = pl.ds(pl.multiple_of(j * S5_SEG, S5_SEG), S5_SEG)
            nr = dr[rows, cs] + a_r * pr + a_i * pi
            ni = di[rows, cs] + a_r * pi - a_i * pr
            dr[rows, cs] = nr
            di[rows, cs] = ni
            if acc_r is not None:
                prev = pl.ds(pl.multiple_of(jnp.maximum(j - 1, 0) * S5_SEG, S5_SEG), S5_SEG)
                first = j == 0
                pr_s = jnp.where(first, s0r[:, cs], xr[prev, cs])
                pi_s = jnp.where(first, s0i[:, cs], xi[prev, cs])
                cr = cr + nr * pr_s + ni * pi_s
                ci = ci - nr * pi_s + ni * pr_s
            return nr, ni, cr, ci

        z = jnp.zeros((S5_SEG, S5_LC), f32)
        init = (gr[:, cs], gi[:, cs], z, z)
        fr, fi, cr, ci = lax.fori_loop(0, nsteps, step, init, unroll=S5_UNROLL)
        gr[:, cs] = fr
        gi[:, cs] = fi
        if acc_r is not None:
            acc_r[:, cs] += cr
            acc_i[:, cs] += ci


def _s5_seg_carry(fr, fi, ar, ai, seg_len, reverse):
    pr, pi = _cpow(ar, ai if not reverse else -ai, seg_len)
    rows = lax.broadcasted_iota(jnp.int32, fr.shape, 0)
    cr, ci = jnp.zeros_like(fr), jnp.zeros_like(fi)
    sh = (S5_SEG - 1) if reverse else 1
    fr_s, fi_s = pltpu.roll(fr, sh, 0), pltpu.roll(fi, sh, 0)
    order = range(S5_SEG - 2, -1, -1) if reverse else range(1, S5_SEG)
    for r in order:
        c_r, c_i = pltpu.roll(cr, sh, 0), pltpu.roll(ci, sh, 0)
        m_r, m_i = _cmul(pr, pi, c_r, c_i)
        cr = jnp.where(rows == r, m_r + fr_s, cr)
        ci = jnp.where(rows == r, m_i + fi_s, ci)
    return cr, ci


def _gelu_parts(y):
    c0 = math.sqrt(2.0 / math.pi)
    t = jnp.tanh(c0 * (y + 0.044715 * y * y * y))
    z = 0.5 * y * (1.0 + t)
    dz = 0.5 * (1.0 + t) + 0.5 * y * (1.0 - t * t) * c0 * (1.0 + 3.0 * 0.044715 * y * y)
    return z, dz


def _s5_y(xr, xi, u_ref, cre_ref, cim_ref, d_ref):
    ys = []
    for k in range(S5_UB):
        cs = slice(k * S5_LC, (k + 1) * S5_LC)
        ys.append(_bdot(xr[:, cs], cre_ref[k]) - _bdot(xi[:, cs], cim_ref[k]))
    return jnp.concatenate(ys, axis=1) + d_ref[...] * u_ref[...]


def _s5_specs(T, rb, rev=False):
    nblk = T // rb
    blk = (lambda i: (nblk - 1 - i, 0)) if rev else (lambda i: (i, 0))
    tok = pl.BlockSpec((rb, 4 * LANES), blk)
    bmat = pl.BlockSpec((S5_UB, LANES, S5_LC), lambda i: (0, 0, 0))
    cmat = pl.BlockSpec((S5_UB, S5_LC, LANES), lambda i: (0, 0, 0))
    avec = pl.BlockSpec((1, S5_N), lambda i: (0, 0))
    seg = pl.BlockSpec((S5_SEG, S5_N), lambda i: (0, 0))
    cvec = pl.BlockSpec((1, 4 * LANES), lambda i: (0, 0))
    s0 = pl.BlockSpec((1, S5_SEG, S5_N), (lambda i: (nblk - 1 - i, 0, 0)) if rev else (lambda i: (i, 0, 0)))
    return dict(tok=tok, bmat=bmat, cmat=cmat, avec=avec, seg=seg, cvec=cvec, s0=s0, nblk=nblk)


def _s5_final(u, bre, bim, ar, ai, *, rb):
    T = u.shape[0]
    sp = _s5_specs(T, rb)

    def body(u_ref, bre_ref, bim_ref, ar_ref, ai_ref, fr_ref, fi_ref, xr, xi):
        @pl.when(pl.program_id(0) == 0)
        def _():
            fr_ref[...] = jnp.zeros_like(fr_ref)
            fi_ref[...] = jnp.zeros_like(fi_ref)

        _s5_bu(u_ref, bre_ref, bim_ref, xr, xi)
        _s5_scan(xr, xi, fr_ref, fi_ref, ar_ref, ai_ref, rb // S5_SEG, False)

    return pl.pallas_call(
        body, name="s5_final", grid=(sp["nblk"],),
        in_specs=[sp["tok"], sp["bmat"], sp["bmat"], sp["avec"], sp["avec"]], out_specs=[sp["seg"], sp["seg"]],
        out_shape=[jax.ShapeDtypeStruct((S5_SEG, S5_N), f32)] * 2,
        scratch_shapes=[pltpu.VMEM((rb, S5_N), f32)] * 2, compiler_params=_cp("arbitrary"),
    )(u, bre, bim, ar, ai)


def _s5_fwd(u, bre, bim, ar, ai, fr, fi, cre, cim, dsk, wg, bg, *, rb):
    T = u.shape[0]
    sp = _s5_specs(T, rb)
    seg_len = T // S5_SEG

    def body(u_ref, bre_ref, bim_ref, ar_ref, ai_ref, fr_ref, fi_ref, cre_ref, cim_ref, d_ref, wg_ref, bg_ref,
             o_ref, s0r_ref, s0i_ref, xr, xi, sr, si):
        @pl.when(pl.program_id(0) == 0)
        def _():
            i_r, i_i = _s5_seg_carry(fr_ref[...], fi_ref[...], ar_ref[...], ai_ref[...], seg_len, False)
            sr[...] = i_r
            si[...] = i_i

        s0r_ref[0] = sr[...]
        s0i_ref[0] = si[...]
        _s5_bu(u_ref, bre_ref, bim_ref, xr, xi)
        _s5_scan(xr, xi, sr, si, ar_ref, ai_ref, rb // S5_SEG, True)
        y = _s5_y(xr, xi, u_ref, cre_ref, cim_ref, d_ref)
        z, _ = _gelu_parts(y)
        v = _bdot(z, wg_ref[...]) + bg_ref[...]
        o_ref[...] = (z * jax.nn.sigmoid(v)).astype(bf16)

    wspec = pl.BlockSpec((4 * LANES, 4 * LANES), lambda i: (0, 0))
    return pl.pallas_call(
        body, name="s5_fwd", grid=(sp["nblk"],),
        in_specs=[sp["tok"], sp["bmat"], sp["bmat"], sp["avec"], sp["avec"], sp["seg"], sp["seg"], sp["cmat"], sp["cmat"],
                  sp["cvec"], wspec, sp["cvec"]],
        out_specs=[sp["tok"], sp["s0"], sp["s0"]],
        out_shape=[jax.ShapeDtypeStruct((T, 4 * LANES), bf16)] + [jax.ShapeDtypeStruct((sp["nblk"], S5_SEG, S5_N), f32)] * 2,
        scratch_shapes=[pltpu.VMEM((rb, S5_N), f32)] * 2 + [pltpu.VMEM((S5_SEG, S5_N), f32)] * 2,
        compiler_params=_cp("arbitrary"),
    )(u, bre, bim, ar, ai, fr, fi, cre, cim, dsk, wg, bg)


def _s5_bwd_a(u, bre, bim, ar, ai, s0r, s0i, cre, cim, cret, cimt, dsk, wg, bg, dout, *, rb):
    T = u.shape[0]
    sp = _s5_specs(T, rb, rev=True)

    def body(u_ref, bre_ref, bim_ref, ar_ref, ai_ref, s0r_ref, s0i_ref, cre_ref, cim_ref, cret_ref, cimt_ref,
             d_ref, wg_ref, bg_ref, do_ref, dy_ref, glr_ref, gli_ref, dcre_ref, dcim_ref, dd_ref, dwg_ref, dbg_ref,
             xr, xi, dr, di, sr, si):
        @pl.when(pl.program_id(0) == 0)
        def _():
            for r in (glr_ref, gli_ref, dcre_ref, dcim_ref, dd_ref, dwg_ref, dbg_ref):
                r[...] = jnp.zeros_like(r)

        sr[...] = s0r_ref[0]
        si[...] = s0i_ref[0]
        _s5_bu(u_ref, bre_ref, bim_ref, xr, xi)
        _s5_scan(xr, xi, sr, si, ar_ref, ai_ref, rb // S5_SEG, True)
        uv = u_ref[...]
        y = _s5_y(xr, xi, u_ref, cre_ref, cim_ref, d_ref)
        z, gz = _gelu_parts(y)
        v = _bdot(z, wg_ref[...]) + bg_ref[...]
        sg = jax.nn.sigmoid(v)
        dov = do_ref[...].astype(f32)
        dv = dov * z * sg * (1.0 - sg)
        dz = dov * sg + _bdot(dv, wg_ref[...], NT)
        dy = dz * gz
        dy_ref[...] = dy
        dwg_ref[...] += _bdot(z, dv, TN)
        dbg_ref[...] += jnp.sum(dv, axis=0, keepdims=True)
        dd_ref[...] += jnp.sum(dy * uv, axis=0, keepdims=True)
        for k in range(S5_UB):
            cs = slice(k * S5_LC, (k + 1) * S5_LC)
            dyk = dy[:, k * LANES:(k + 1) * LANES]
            dcre_ref[k] += _bdot(xr[:, cs], dyk, TN)
            dcim_ref[k] -= _bdot(xi[:, cs], dyk, TN)
            dr[:, cs] = _bdot(dyk, cret_ref[k])
            di[:, cs] = -_bdot(dyk, cimt_ref[k])
        _s5_rscan(dr, di, None, None, None, None, glr_ref, gli_ref, None, None, ar_ref, ai_ref, rb // S5_SEG)

    wspec = pl.BlockSpec((4 * LANES, 4 * LANES), lambda i: (0, 0))
    return pl.pallas_call(
        body, name="s5_bwd_a", grid=(sp["nblk"],),
        in_specs=[sp["tok"], sp["bmat"], sp["bmat"], sp["avec"], sp["avec"], sp["s0"], sp["s0"], sp["cmat"], sp["cmat"],
                  sp["bmat"], sp["bmat"], sp["cvec"], wspec, sp["cvec"], sp["tok"]],
        out_specs=[sp["tok"], sp["seg"], sp["seg"], sp["cmat"], sp["cmat"], sp["cvec"], wspec, sp["cvec"]],
        out_shape=[jax.ShapeDtypeStruct((T, 4 * LANES), f32)] + [jax.ShapeDtypeStruct((S5_SEG, S5_N), f32)] * 2
        + [jax.ShapeDtypeStruct((S5_UB, S5_LC, LANES), f32)] * 2
        + [jax.ShapeDtypeStruct((1, 4 * LANES), f32), jax.ShapeDtypeStruct((4 * LANES, 4 * LANES), f32),
           jax.ShapeDtypeStruct((1, 4 * LANES), f32)],
        scratch_shapes=[pltpu.VMEM((rb, S5_N), f32)] * 4 + [pltpu.VMEM((S5_SEG, S5_N), f32)] * 2,
        compiler_params=_cp("arbitrary"),
    )(u, bre, bim, ar, ai, s0r, s0i, cre, cim, cret, cimt, dsk, wg, bg, dout)


def _s5_bwd_b(u, bre, bim, bret, bimt, ar, ai, s0r, s0i, glr, gli, cret, cimt, dsk, dy, *, rb):
    T = u.shape[0]
    sp = _s5_specs(T, rb, rev=True)
    seg_len = T // S5_SEG
    nblk = sp["nblk"]

    def body(u_ref, bre_ref, bim_ref, bret_ref, bimt_ref, ar_ref, ai_ref, s0r_ref, s0i_ref, glr_ref, gli_ref,
             cret_ref, cimt_ref, d_ref, dy_ref, du_ref, dbre_ref, dbim_ref, dar_ref, dai_ref,
             xr, xi, dr, di, sr, si, gr, gi, acc_r, acc_i):
        @pl.when(pl.program_id(0) == 0)
        def _():
            x_r, x_i = _s5_seg_carry(glr_ref[...], gli_ref[...], ar_ref[...], ai_ref[...], seg_len, True)
            gr[...] = x_r
            gi[...] = x_i
            acc_r[...] = jnp.zeros_like(acc_r)
            acc_i[...] = jnp.zeros_like(acc_i)
            dbre_ref[...] = jnp.zeros_like(dbre_ref)
            dbim_ref[...] = jnp.zeros_like(dbim_ref)

        sr[...] = s0r_ref[0]
        si[...] = s0i_ref[0]
        _s5_bu(u_ref, bre_ref, bim_ref, xr, xi)
        _s5_scan(xr, xi, sr, si, ar_ref, ai_ref, rb // S5_SEG, True)
        dy = dy_ref[...]
        for k in range(S5_UB):
            cs = slice(k * S5_LC, (k + 1) * S5_LC)
            dyk = dy[:, k * LANES:(k + 1) * LANES]
            dr[:, cs] = _bdot(dyk, cret_ref[k])
            di[:, cs] = -_bdot(dyk, cimt_ref[k])
        sr[...] = s0r_ref[0]
        si[...] = s0i_ref[0]
        _s5_rscan(dr, di, xr, xi, sr, si, gr, gi, acc_r, acc_i, ar_ref, ai_ref, rb // S5_SEG)
        dus = []
        for k in range(S5_UB):
            cs = slice(k * S5_LC, (k + 1) * S5_LC)
            uk = u_ref[:, k * LANES:(k + 1) * LANES]
            dbre_ref[k] += _bdot(uk, dr[:, cs], TN)
            dbim_ref[k] += _bdot(uk, di[:, cs], TN)
            dus.append(_bdot(dr[:, cs], bret_ref[k]) + _bdot(di[:, cs], bimt_ref[k]))
        du_ref[...] = (jnp.concatenate(dus, axis=1) + d_ref[...] * dy).astype(bf16)

        @pl.when(pl.program_id(0) == nblk - 1)
        def _():
            dar_ref[...] = jnp.sum(acc_r[...], axis=0, keepdims=True)
            dai_ref[...] = jnp.sum(acc_i[...], axis=0, keepdims=True)

    return pl.pallas_call(
        body, name="s5_bwd_b", grid=(nblk,),
        in_specs=[sp["tok"], sp["bmat"], sp["bmat"], sp["cmat"], sp["cmat"], sp["avec"], sp["avec"], sp["s0"], sp["s0"],
                  sp["seg"], sp["seg"], sp["bmat"], sp["bmat"], sp["cvec"], sp["tok"]],
        out_specs=[sp["tok"], sp["bmat"], sp["bmat"], sp["avec"], sp["avec"]],
        out_shape=[jax.ShapeDtypeStruct((T, 4 * LANES), bf16)] + [jax.ShapeDtypeStruct((S5_UB, LANES, S5_LC), f32)] * 2
        + [jax.ShapeDtypeStruct((1, S5_N), f32)] * 2,
        scratch_shapes=[pltpu.VMEM((rb, S5_N), f32)] * 4 + [pltpu.VMEM((S5_SEG, S5_N), f32)] * 6,
        compiler_params=_cp("arbitrary"),
    )(u, bre, bim, bret, bimt, ar, ai, s0r, s0i, glr, gli, cret, cimt, dsk, dy)


def _blockdiag(w, transpose=False):
    if transpose:
        w = jnp.swapaxes(w, 1, 2)
    g, a, b = w.shape
    eye = jnp.eye(8, dtype=w.dtype)
    return jnp.einsum("kgab,gj->kgajb", w.reshape(4, 8, a, b), eye).reshape(4, 8 * a, 8 * b)


def _blockdiag_t(m, a, b):
    eye = jnp.eye(8, dtype=m.dtype)
    return jnp.einsum("kgajb,gj->kgab", m.reshape(4, 8, a, 8, b), eye).reshape(32, a, b)


ROT = MLA_ROPE // 2


def _rope_tables(positions):
    freqs = ROPE_THETA ** (-jnp.arange(0, MLA_ROPE, 2, dtype=f32) / MLA_ROPE)
    ang = positions.astype(f32)[:, None] * freqs
    cos, sin, z = jnp.cos(ang), jnp.sin(ang), jnp.zeros_like(ang)
    return (jnp.concatenate([cos, cos, z, z], axis=1), jnp.concatenate([-sin, z, z, z], axis=1),
            jnp.concatenate([z, sin, z, z], axis=1))


def _rot(x, c, sa, sb):
    return x * c + pltpu.roll(x, LANES - ROT, 1) * sa + pltpu.roll(x, ROT, 1) * sb


def _rot_t(dy, c, sa, sb):
    return dy * c + pltpu.roll(dy * sa, ROT, 1) + pltpu.roll(dy * sb, LANES - ROT, 1)


def _rms(xv, g):
    return xv * lax.rsqrt(jnp.mean(xv * xv, axis=-1, keepdims=True) + EPS) * g


QW, KVW = MLA_Q_RANK, MLA_KV_RANK
ODD_PAD = QW + KVW + LANES


def _mla_prep_fwd(proj, qg, kvg, tabs, *, tm=512):
    T = proj.shape[0]
    tm = _tile(T, tm)

    def body(p_ref, qg_ref, kvg_ref, c_ref, sa_ref, sb_ref, cq_ref, ckv_ref, kr_ref):
        cq_ref[...] = _rms(p_ref[:, :QW], qg_ref[...]).astype(bf16)
        ckv_ref[...] = _rms(p_ref[:, QW:QW + KVW], kvg_ref[...]).astype(bf16)
        kr_ref[...] = _rot(p_ref[:, QW + KVW:], c_ref[...], sa_ref[...], sb_ref[...]).astype(bf16)

    row = lambda w: pl.BlockSpec((tm, w), lambda i: (i, 0))
    vec = lambda w: pl.BlockSpec((1, w), lambda i: (0, 0))
    return pl.pallas_call(
        body, name="mla_prep_fwd", grid=(T // tm,),
        in_specs=[row(ODD_PAD), vec(QW), vec(KVW), row(LANES), row(LANES), row(LANES)],
        out_specs=[row(QW), row(KVW), row(LANES)],
        out_shape=[jax.ShapeDtypeStruct((T, QW), bf16), jax.ShapeDtypeStruct((T, KVW), bf16),
                   jax.ShapeDtypeStruct((T, LANES), bf16)],
        compiler_params=_cp("parallel"),
    )(proj, qg, kvg, *tabs)


def _mla_prep_bwd(proj, qg, kvg, tabs, dcqn, dckvn, dkr_heads, *, tm=512):
    T = proj.shape[0]
    tm = _tile(T, tm)

    def body(p_ref, qg_ref, kvg_ref, c_ref, sa_ref, sb_ref, dcq_ref, dckv_ref, dkr_ref, dp_ref, dqg_ref, dkvg_ref):
        dcq, dqg = _rms_bwd_math(p_ref[:, :QW], qg_ref[...], dcq_ref[...])
        dckv, dkvg = _rms_bwd_math(p_ref[:, QW:QW + KVW], kvg_ref[...], dckv_ref[...])
        dk = dkr_ref[:, :LANES]
        for h in range(1, MLA_HEADS):
            dk = dk + dkr_ref[:, h * LANES:(h + 1) * LANES]
        dkr = _rot_t(dk, c_ref[...], sa_ref[...], sb_ref[...])
        dp_ref[...] = jnp.concatenate([dcq, dckv, dkr], axis=1).astype(bf16)

        @pl.when(pl.program_id(0) == 0)
        def _():
            dqg_ref[...] = dqg
            dkvg_ref[...] = dkvg

        @pl.when(pl.program_id(0) > 0)
        def _():
            dqg_ref[...] += dqg
            dkvg_ref[...] += dkvg

    row = lambda w: pl.BlockSpec((tm, w), lambda i: (i, 0))
    vec = lambda w: pl.BlockSpec((1, w), lambda i: (0, 0))
    return pl.pallas_call(
        body, name="mla_prep_bwd", grid=(T // tm,),
        in_specs=[row(ODD_PAD), vec(QW), vec(KVW), row(LANES), row(LANES), row(LANES), row(QW), row(KVW),
                  row(MLA_HEADS * LANES)],
        out_specs=[row(ODD_PAD), vec(QW), vec(KVW)],
        out_shape=[jax.ShapeDtypeStruct((T, ODD_PAD), bf16), jax.ShapeDtypeStruct((1, QW), f32),
                   jax.ShapeDtypeStruct((1, KVW), f32)],
        compiler_params=_cp("arbitrary"),
    )(proj, qg, kvg, *tabs, dcqn, dckvn, dkr_heads)


HQ = 2 * LANES
QK_SCALE = MLA_QK ** -0.5


def _q_post(q, tabs, *, transpose, name, tm=512):
    T = q.shape[0]
    tm = _tile(T, tm)

    def body(q_ref, c_ref, sa_ref, sb_ref, o_ref):
        c, sa, sb = c_ref[...], sa_ref[...], sb_ref[...]
        for h in range(MLA_HEADS):
            nope, rope = pl.ds(h * HQ, LANES), pl.ds(h * HQ + LANES, LANES)
            o_ref[:, nope] = (q_ref[:, nope].astype(f32) * QK_SCALE).astype(bf16)
            o_ref[:, rope] = ((_rot_t if transpose else _rot)(q_ref[:, rope].astype(f32), c, sa, sb) * QK_SCALE).astype(bf16)

    tab = pl.BlockSpec((tm, LANES), lambda i: (i, 0))
    blk = pl.BlockSpec((tm, MLA_HEADS * HQ), lambda i: (i, 0))
    return pl.pallas_call(
        body, name=name, grid=(T // tm,), in_specs=[blk, tab, tab, tab], out_specs=blk,
        out_shape=jax.ShapeDtypeStruct(q.shape, bf16), compiler_params=_cp("parallel"),
    )(q, *tabs)


def _causal_mask(i, j, tq, tk):
    r = lax.broadcasted_iota(jnp.int32, (tq, tk), 0) + i * tq
    c = lax.broadcasted_iota(jnp.int32, (tq, tk), 1) + j * tk
    return c <= r


FLASH_PARTS = 4


def _flash_fwd(q, kv, kr, *, tq=1024, tk=1024):
    T = q.shape[0]
    tq = _tile(T, tq)
    tk = _tile(tq, tk)
    per = tq // tk
    H = MLA_HEADS

    def body(q_ref, kn_ref, v_ref, kr_ref, o_ref, lse_ref, m_s, acc):
        i, j = pl.program_id(1), pl.program_id(2)
        last = (i + 1) * per - 1

        @pl.when(j == 0)
        def _():
            m_s[...] = jnp.full_like(m_s, -jnp.inf)
            acc[...] = jnp.zeros_like(acc)

        def step(masked):
            k = jnp.concatenate([kn_ref[...], kr_ref[...]], axis=1)
            v1 = jnp.concatenate([v_ref[...], jnp.ones((tk, LANES), bf16)], axis=1)
            mask = _causal_mask(i, j, tq, tk) if masked else None
            for part in range(FLASH_PARTS):
                rows = pl.ds(part * (tq // FLASH_PARTS), tq // FLASH_PARTS)
                s = _dot(q_ref[rows, :], k, NT)
                if masked:
                    s = jnp.where(mask[part * (tq // FLASH_PARTS):(part + 1) * (tq // FLASH_PARTS)], s, -jnp.inf)
                m_new = jnp.maximum(m_s[rows, :], jnp.max(s, axis=-1, keepdims=True))
                alpha = jnp.exp(m_s[rows, :] - m_new)
                p = jnp.exp((s - m_new).astype(bf16))
                acc[rows, :] = alpha * acc[rows, :] + _dot(p, v1)
                m_s[rows, :] = m_new

        pl.when(j < i * per)(functools.partial(step, False))
        pl.when((j >= i * per) & (j <= last))(functools.partial(step, True))

        @pl.when(j == last)
        def _():
            l = acc[:, LANES:]
            o_ref[...] = (acc[:, :LANES] / l).astype(bf16)
            lse_ref[0] = m_s[...] + jnp.log(jnp.max(l, axis=-1, keepdims=True))

    kj = lambda i, j: jnp.minimum(j, (i + 1) * per - 1)
    kblk = lambda off: pl.BlockSpec((tk, LANES), lambda h, i, j: (kj(i, j), 2 * h + off))
    return pl.pallas_call(
        body, name="flash_fwd", grid=(H, T // tq, T // tk),
        in_specs=[pl.BlockSpec((tq, HQ), lambda h, i, j: (i, h)), kblk(0), kblk(1),
                  pl.BlockSpec((tk, LANES), lambda h, i, j: (kj(i, j), 0))],
        out_specs=[pl.BlockSpec((tq, LANES), lambda h, i, j: (i, h)), pl.BlockSpec((1, tq, 1), lambda h, i, j: (h, i, 0))],
        out_shape=[jax.ShapeDtypeStruct((T, H * LANES), bf16), jax.ShapeDtypeStruct((H, T, 1), f32)],
        scratch_shapes=[pltpu.VMEM((tq, 1), f32), pltpu.VMEM((tq, 2 * LANES), f32)],
        compiler_params=_cp("parallel", "parallel", "arbitrary"),
    )(q, kv, kv, kr)


def _flash_bwd(q, kv, kr, o, do, lse, *, tb=1024):
    T = q.shape[0]
    tb = _tile(T, tb)
    nb = T // tb
    H = MLA_HEADS

    def body(q_ref, kn_ref, v_ref, kr_ref, o_ref, do_ref, lse_ref, dkv_ref, dkr_ref, dq_ref, dk_acc, dv_acc):
        j, ii = pl.program_id(1), pl.program_id(2)
        i = jnp.maximum(ii, j)

        @pl.when((j == 0) & (ii == 0))
        def _():
            dq_ref[...] = jnp.zeros_like(dq_ref)

        @pl.when(ii == 0)
        def _():
            dk_acc[...] = jnp.zeros_like(dk_acc)
            dv_acc[...] = jnp.zeros_like(dv_acc)

        def step(masked):
            k = jnp.concatenate([kn_ref[...], kr_ref[...]], axis=1)
            p = jnp.exp((_dot(q_ref[...], k, NT) - lse_ref[0]).astype(bf16))
            if masked:
                p = jnp.where(_causal_mask(i, j, tb, tb), p, jnp.zeros_like(p))
            delta = jnp.sum(o_ref[...].astype(f32) * do_ref[...], axis=-1, keepdims=True)
            ds = p * (_bdot(do_ref[...], v_ref[...], NT) - delta).astype(bf16)
            dv_acc[...] += _bdot(p, do_ref[...], TN)
            dk_acc[...] += _bdot(ds, q_ref[...], TN)
            dq_ref[pl.ds(pl.multiple_of(i * tb, tb), tb), :] += _bdot(ds, k)

        pl.when(ii > j)(functools.partial(step, False))
        pl.when(ii == j)(functools.partial(step, True))

        @pl.when(ii == nb - 1)
        def _():
            dkv_ref[...] = jnp.concatenate([dk_acc[:, :LANES], dv_acc[...]], axis=1).astype(bf16)
            dkr_ref[...] = dk_acc[:, LANES:]

    qi = lambda h, j, i: jnp.maximum(i, j)
    kblk = lambda off: pl.BlockSpec((tb, LANES), lambda h, j, i: (j, 2 * h + off))
    vec = pl.BlockSpec((1, tb, 1), lambda h, j, i: (h, qi(h, j, i), 0))
    qblk = pl.BlockSpec((tb, LANES), lambda h, j, i: (qi(h, j, i), h))
    return pl.pallas_call(
        body, name="flash_bwd", grid=(H, nb, nb),
        in_specs=[pl.BlockSpec((tb, HQ), lambda h, j, i: (qi(h, j, i), h)), kblk(0), kblk(1),
                  pl.BlockSpec((tb, LANES), lambda h, j, i: (j, 0)), qblk, qblk, vec],
        out_specs=[pl.BlockSpec((tb, HQ), lambda h, j, i: (j, h)), pl.BlockSpec((tb, LANES), lambda h, j, i: (j, h)),
                   pl.BlockSpec((T, HQ), lambda h, j, i: (0, h))],
        out_shape=[jax.ShapeDtypeStruct((T, H * HQ), bf16), jax.ShapeDtypeStruct((T, H * LANES), f32),
                   jax.ShapeDtypeStruct((T, H * HQ), f32)],
        scratch_shapes=[pltpu.VMEM((tb, HQ), f32), pltpu.VMEM((tb, LANES), f32)],
        compiler_params=_cp("parallel", "arbitrary", "arbitrary"),
    )(q, kv, kv, kr, o, do, lse)


HBM_SPEC = pl.BlockSpec(memory_space=pltpu.HBM)
N_CHIPS = 4
N_DEV = 8

BIG = {"even_w_in": 1, "s5_w_glu": 0, "even_w_out": 0, "odd_w_in": 0, "mla_w_uq": 1, "mla_w_ukv": 1, "odd_w_out": 0,
       "ffn_w_in": 2, "ffn_w_out": 1}
LAYERED = ("ffn_w_in", "ffn_w_out")
GROUPS = {"even_in": ("even_w_in",), "even_rest": ("s5_w_glu", "even_w_out"), "ffn0": LAYERED,
          "odd": ("odd_w_in", "mla_w_uq", "mla_w_ukv", "odd_w_out"), "ffn1": LAYERED}
GROUP_LAYER = {"ffn0": 0, "ffn1": 1}


def _place():
    x, y, c = lax.axis_index("x"), lax.axis_index("y"), lax.axis_index("c")
    chips = [(1 - x, y), (x, 1 - y), (1 - x, 1 - y)]
    return x, y, c, chips


def _slab(ref, axis, k, size):
    start = pl.multiple_of(k * size, size if axis == 0 else LANES)
    idx = [slice(None)] * len(ref.shape)
    idx[axis] = pl.ds(start, size)
    return ref.at[tuple(idx)]


SEM_SPEC = pl.BlockSpec(memory_space=pltpu.SEMAPHORE)
ANY_SPEC = pl.BlockSpec(memory_space=pl.ANY)
EFFECT = pltpu.SideEffectType.DATAFLOW_SIDE_EFFECTING


def _hbm(a):
    return pltpu.with_memory_space_constraint(a, pltpu.HBM)


class _Gather:
    copies = 3

    def __init__(self, axis, size):
        self.axis, self.size = axis, size

    def view(self, land, kk):
        return _slab(land, self.axis, kk, self.size)

    def own(self, land, place):
        return self.view(land, 2 * place[0] + place[1])

    def sends(self, src, land, place):
        x, y, c, chips = place
        return [(self.own(land, place) if src is None else src, self.own(land, place), (*chip, c)) for chip in chips]

    def recvs(self, land, place):
        return [self.view(land, 2 * cx + cy) for cx, cy in place[3]]


class _Scatter:
    copies = 3

    def __init__(self, axis, size, layer=None):
        self.axis, self.size, self.layer = axis, size, layer

    def row(self, land, j):
        return land.at[j] if self.layer is None else land.at[j, self.layer]

    def sends(self, src, land, place):
        c, chips = place[2], place[3]
        return [(_slab(src, self.axis, 2 * cx + cy, self.size), self.row(land, j), (cx, cy, c))
                for j, (cx, cy) in enumerate(chips)]

    def recvs(self, land, place):
        return [self.row(land, j) for j in range(3)]


class _ToAll:
    copies = N_DEV - 1

    def __init__(self, size):
        self.size = size

    def sends(self, src, land, place):
        x, y, c, _ = place
        flip = lambda v, bit: 1 - v if bit else v
        own = _slab(land, 0, 4 * x + 2 * y + c, self.size)
        return [(own, own, (flip(x, m & 4), flip(y, m & 2), flip(c, m & 1))) for m in range(1, N_DEV)]

    def recvs(self, land, place):
        x, y, c, _ = place
        d = 4 * x + 2 * y + c
        return [_slab(land, 0, d ^ m, self.size) for m in range(1, N_DEV)]


def _unique(arrays):
    out, index = [], {}
    for a in arrays:
        if a is not None and id(a) not in index:
            index[id(a)] = len(out)
            out.append(a)
    return out, index


def _sem_base(routes):
    base = [0]
    for r in routes:
        base.append(base[-1] + r.copies)
    return base


def _push_start(name, items):
    n = len(items)
    base = _sem_base([it[0] for it in items])
    arrays, index = _unique([it[1] for it in items] + [it[2] for it in items])
    na = len(arrays)

    def body(*refs):
        arr, send, recv, token = refs[:na], refs[na], refs[na + 1], refs[-1]
        place = _place()
        for i, (route, src, land) in enumerate(items):
            s_ref = None if src is None else arr[index[id(src)]]
            for j, (s, d, dev) in enumerate(route.sends(s_ref, arr[index[id(land)]], place)):
                pltpu.make_async_remote_copy(src_ref=s, dst_ref=d, send_sem=send.at[base[i] + j], recv_sem=recv.at[base[i] + j],
                                             device_id=dev, device_id_type=MESH).start()
        token[...] = jnp.zeros_like(token)

    res = pl.pallas_call(
        body, name=name,
        out_shape=[pltpu.SemaphoreType.DMA((base[-1],)), pltpu.SemaphoreType.DMA((base[-1],))]
        + [pltpu.HBM(a.shape, a.dtype) for a in arrays] + [jax.ShapeDtypeStruct((SUBLANES, LANES), f32)],
        in_specs=[HBM_SPEC] * na, out_specs=[SEM_SPEC, SEM_SPEC] + [HBM_SPEC] * na + [pl.BlockSpec(memory_space=pltpu.VMEM)],
        input_output_aliases={i: 2 + i for i in range(na)},
        compiler_params=pltpu.CompilerParams(has_side_effects=EFFECT),
    )(*[_hbm(a) for a in arrays])
    thru = lambda a: None if a is None else res[2 + index[id(a)]]
    return (res[0], res[1]), [thru(it[1]) for it in items], [thru(it[2]) for it in items], res[-1]


def _push_wait(name, groups, after):
    arrays, index = _unique([a for _, _, srcs, lands in groups for a in list(srcs) + list(lands)])
    na, ng = len(arrays), len(groups)

    def body(*refs):
        arr, sems = refs[:na], refs[na:na + 2 * ng]
        place = _place()
        for g, (routes, _, srcs, lands) in enumerate(groups):
            send, recv = sems[2 * g], sems[2 * g + 1]
            base = _sem_base(routes)
            for i, route in enumerate(routes):
                src, land = None if srcs[i] is None else arr[index[id(srcs[i])]], arr[index[id(lands[i])]]
                for j, ((s, d, dev), mine) in enumerate(zip(route.sends(src, land, place), route.recvs(land, place))):
                    cp = pltpu.make_async_remote_copy(src_ref=s, dst_ref=mine, send_sem=send.at[base[i] + j],
                                                      recv_sem=recv.at[base[i] + j], device_id=dev,
                                                      device_id_type=MESH)
                    cp.wait_send()
                    cp.wait_recv()

    sem_args = [s for g in groups for s in g[1]]
    res = pl.pallas_call(
        body, name=name, out_shape=[pltpu.HBM(a.shape, a.dtype) for a in arrays],
        in_specs=[HBM_SPEC] * na + [SEM_SPEC] * (2 * ng) + [ANY_SPEC] * len(after), out_specs=[HBM_SPEC] * na,
        input_output_aliases={i: i for i in range(na)},
        compiler_params=pltpu.CompilerParams(has_side_effects=EFFECT),
    )(*arrays, *sem_args, *after)
    return [[res[index[id(a)]] for a in g[3]] for g in groups]


def _place_slab(block, axis, slabs, idx, dtype, *, name):
    R, C = block.shape
    tm = _rows(R, C)
    nr = R // tm
    out_map = (lambda i, k: (i, k[0])) if axis == 1 else (lambda i, k: (k[0] * nr + i, 0))

    def body(k_ref, x_ref, o_ref):
        o_ref[...] = x_ref[...].astype(dtype)

    full = (R, C * slabs) if axis == 1 else (R * slabs, C)
    return pl.pallas_call(
        body, name=name, out_shape=jax.ShapeDtypeStruct(full, dtype),
        grid_spec=pltpu.PrefetchScalarGridSpec(
            num_scalar_prefetch=1, grid=(nr,), in_specs=[pl.BlockSpec((tm, C), lambda i, k: (i, 0))],
            out_specs=pl.BlockSpec((tm, C), out_map)),
        compiler_params=_cp("parallel"),
    )(idx, block)


def _swap_with_sibling(parts, tag):
    names = list(parts)

    def body(*refs):
        n = len(names)
        ins, outs, send, recv = refs[:n], refs[n:2 * n], refs[-2], refs[-1]
        x, y, c, _ = _place()
        cps = [pltpu.make_async_remote_copy(src_ref=ins[a], dst_ref=outs[a], send_sem=send.at[a], recv_sem=recv.at[a],
                                            device_id=(x, y, 1 - c), device_id_type=MESH) for a in range(n)]
        for cp in cps:
            cp.start()
        for cp in cps:
            cp.wait_recv()
        for cp in cps:
            cp.wait_send()

    res = pl.pallas_call(
        body, name=f"swap_with_sibling_{tag}", in_specs=[HBM_SPEC] * len(names), out_specs=[HBM_SPEC] * len(names),
        out_shape=[jax.ShapeDtypeStruct(parts[n].shape, parts[n].dtype) for n in names],
        scratch_shapes=[pltpu.SemaphoreType.DMA((len(names),)), pltpu.SemaphoreType.DMA((len(names),))],
    )(*[parts[n] for n in names])
    return dict(zip(names, res))


ELEMENTWISE_BLOCK_BYTES = 1 << 20


def _rows(r, c):
    for t in (512, 256, 128, 64, 32, 16, 8):
        if r % t == 0 and t * c * 4 <= ELEMENTWISE_BLOCK_BYTES:
            return t
    return r


def _sum4(owns, axis, recv, kidx, *, name):
    L = len(owns)
    R, C = recv.shape[2:]
    tm = _rows(R, C)
    nr = R // tm

    def body(k_ref, *refs):
        own_refs, r_ref, out_ref = refs[:L], refs[L], refs[L + 1]
        for li in range(L):
            @pl.when(pl.program_id(0) == li)
            def _(o_ref=own_refs[li]):
                out_ref[...] = ((o_ref[...] + r_ref[0, 0].astype(f32)) + r_ref[1, 0].astype(f32)) + r_ref[2, 0].astype(f32)

    own_map = (lambda l, i, k: (i, k[0])) if axis == 1 else (lambda l, i, k: (k[0] * nr + i, 0))
    return pl.pallas_call(
        body, name=name, out_shape=jax.ShapeDtypeStruct((L * R, C), f32),
        grid_spec=pltpu.PrefetchScalarGridSpec(
            num_scalar_prefetch=1, grid=(L, nr),
            in_specs=[pl.BlockSpec((tm, C), own_map)] * L + [pl.BlockSpec((3, 1, tm, C), lambda l, i, k: (0, l, i, 0))],
            out_specs=pl.BlockSpec((tm, C), lambda l, i, k: (l * nr + i, 0))),
        compiler_params=_cp("parallel", "parallel"),
    )(kidx, *owns, recv)


def _adamw(w, m, v, parts, *, name):
    R, C = w.shape
    tm = _rows(R, C)
    npart = len(parts)

    def body(*refs):
        w_ref, m_ref, v_ref = refs[:3]
        g_ref, d_ref, m2_ref, v2_ref = refs[3 + npart:]
        g = refs[3][...]
        for p_ref in refs[4:3 + npart]:
            g = g + p_ref[...]
        g_ref[...] = g
        d_ref[...], m2_ref[...], v2_ref[...] = _adam_math(w_ref[...], m_ref[...], v_ref[...], g)

    blk = pl.BlockSpec((tm, C), lambda i: (i, 0))
    return pl.pallas_call(
        body, name=name, grid=(R // tm,),
        in_specs=[blk] * (3 + npart), out_specs=[blk] * 4,
        out_shape=[jax.ShapeDtypeStruct((R, C), f32)] * 4, compiler_params=_cp("parallel"),
    )(w, m, v, *parts)


def _adam_math(w, m, v, g):
    m2 = ADAM_B1 * m + (1.0 - ADAM_B1) * g
    v2 = ADAM_B2 * v + (1.0 - ADAM_B2) * (g * g)
    m_hat = m2 / (1.0 - ADAM_B1 ** ADAM_STEP)
    v_hat = v2 / (1.0 - ADAM_B2 ** ADAM_STEP)
    return -ADAM_LR * (m_hat / (jnp.sqrt(v_hat) + ADAM_EPS) + ADAM_WD * w), m2, v2


def _adamw_small(landed, w, m, v, kidx, ra, rb):
    rs = ra + N_CHIPS * rb

    def body(k_ref, l_ref, w_ref, m_ref, v_ref, g_ref, d_ref, m2_ref, v2_ref):
        mine = pl.multiple_of(ra + k_ref[0] * rb, SUBLANES)
        for lo, n, off in ((0, ra, 0), (ra, rb, mine)):
            g = l_ref[pl.ds(off, n), :]
            for d in range(1, N_DEV):
                g = g + l_ref[pl.ds(d * rs + off, n), :]
            rows = pl.ds(lo, n)
            delta, m2, v2 = _adam_math(w_ref[rows, :], m_ref[rows, :], v_ref[rows, :], g)
            g_ref[rows, :] = g
            d_ref[rows, :] = delta
            m2_ref[rows, :] = m2
            v2_ref[rows, :] = v2

    vmem = pl.BlockSpec(memory_space=pltpu.VMEM)
    return pl.pallas_call(
        body, name="adamw_small", out_shape=[jax.ShapeDtypeStruct(w.shape, f32)] * 4,
        grid_spec=pltpu.PrefetchScalarGridSpec(num_scalar_prefetch=1, grid=(), in_specs=[vmem] * 4, out_specs=[vmem] * 4),
        compiler_params=_cp(),
    )(kidx, landed, w, m, v)


def _pad_odd(w):
    return jnp.pad(w, ((0, 0), (0, ODD_PAD - w.shape[1])))


def _uq_cat(w):
    r = w.shape[0]
    return jnp.pad(w.reshape(r, MLA_HEADS, MLA_QK), ((0, 0), (0, 0), (0, HQ - MLA_QK))).reshape(r, MLA_HEADS * HQ)


def _uq_uncat(w):
    r = w.shape[0]
    return w.reshape(r, MLA_HEADS, HQ)[:, :, :MLA_QK].reshape(r, MLA_HEADS * MLA_QK)


def _to_segments(v):
    T, C = v.shape
    return v.reshape(S5_SEG, T // S5_SEG, C).transpose(1, 0, 2).reshape(T, C)


def _from_segments(v):
    T, C = v.shape
    return v.reshape(T // S5_SEG, S5_SEG, C).transpose(1, 0, 2).reshape(T, C)


def _s5_rb(T):
    return min(512, T)


def _ffn_fwd(h, hn, w_in, cw, cb, w_out, tag, next_g=None):
    au = _mm(hn, w_in, out_dtype=bf16, name=f"ffn{tag}_in", tn=1408)
    z = _ffn_mid_fwd(au, cw, cb, name=f"ffn{tag}_mid")
    return _mm(z, w_out, res=h, norm_g=next_g, name=f"ffn{tag}_out", tm=512, tk=D_FF), (hn, au, z)


def _ffn_bwd(h, g, w_in, cw, cb, w_out, saved, dh, tag, dep=None):
    hn, au, z = saved
    dz = _mm(dh, w_out, tb=True, out_dtype=bf16, name=f"ffn{tag}_dz", tn=1408, dep=dep)
    dw_out = _mm(z, dh, ta=True, also_bf16=True, name=f"ffn{tag}_dwout", tm=1408)
    dau, dcw, dcb = _ffn_mid_bwd(au, cw, cb, dz, name=f"ffn{tag}_dmid")
    dh_in, dg = _mm(dau, w_in, tb=True, res=dh, norm_bwd=(h, g), name=f"ffn{tag}_dhn", tk=1408)
    dw_in = _mm(hn, dau, ta=True, also_bf16=True, name=f"ffn{tag}_dwin", tn=1408)
    return dh_in, dg, dw_in, dcw, dcb, dw_out


def _local_step(x, positions, target, get_w, P, put_g):
    T = x.shape[0]
    rb = _s5_rb(T)
    row = lambda v: v.reshape(1, -1)
    g_mix, g_ffn = P["norm_mix_g"], P["norm_ffn_g"]
    lbl, hng = P["hgrn_lb_logits"], P["hgrn_norm_g"]
    dsk, bg = P["s5_d"], P["s5_b_glu"]
    qg, kvg = P["mla_q_norm_g"], P["mla_kv_norm_g"]
    cw, cb = P["ffn_conv_w"], P["ffn_conv_b"]

    col = lambda v: v.reshape(S5_N, 1)
    disc_in = (col(P["s5_a_re"]), col(P["s5_a_im"]), col(jnp.repeat(P["s5_log_dt"].reshape(S5_GROUPS), S5_STATE)),
               P["s5_b_re"].reshape(S5_N, S5_GROUP), P["s5_b_im"].reshape(S5_N, S5_GROUP))
    abr, abi, bbr, bbi = _s5_disc_fwd(*disc_in)
    ar, ai = abr.reshape(1, S5_N), abi.reshape(1, S5_N)
    bbr3, bbi3 = bbr.reshape(S5_GROUPS, S5_STATE, S5_GROUP), bbi.reshape(S5_GROUPS, S5_STATE, S5_GROUP)
    bre, bim = _blockdiag(bbr3, True).astype(bf16), _blockdiag(bbi3, True).astype(bf16)
    bret, bimt = _blockdiag(bbr3).astype(bf16), _blockdiag(bbi3).astype(bf16)
    c_re, c_im = P["s5_c_re"].reshape(S5_GROUPS, S5_GROUP, S5_STATE), P["s5_c_im"].reshape(S5_GROUPS, S5_GROUP, S5_STATE)
    cre, cim = _blockdiag(c_re, True).astype(bf16), _blockdiag(c_im, True).astype(bf16)
    cret, cimt = _blockdiag(c_re).astype(bf16), _blockdiag(c_im).astype(bf16)

    hn0 = _rms_fwd(x, g_mix[0:1], name="mix0_norm")
    We = get_w("even_in", hn0)
    proj_e = _mm(hn0, We["even_w_in"], name="even_in", tn=1280)
    Wr = get_w("even_rest", proj_e)
    ya, states = _hgrn_fwd(proj_e, lbl, hng)
    u_seg = _to_segments(proj_e[:, 4 * 512:])
    fr, fi = _s5_final(u_seg, bre, bim, ar, ai, rb=rb)
    yb_seg, s0r, s0i = _s5_fwd(u_seg, bre, bim, ar, ai, fr, fi, cre, cim, dsk, Wr["s5_w_glu"], bg, rb=rb)
    ycat = jnp.concatenate([ya, _from_segments(yb_seg)], axis=1)
    h1, hnf0 = _mm(ycat, Wr["even_w_out"], res=x, norm_g=g_ffn[0:1], name="even_out")
    Wf0 = get_w("ffn0", h1)
    (h2, hn2), ffn0 = _ffn_fwd(h1, hnf0, Wf0["ffn_w_in"], cw[0], cb[0:1], Wf0["ffn_w_out"], 0, next_g=g_mix[1:2])

    tabs = _rope_tables(positions)
    Wo = get_w("odd", hn2)
    proj_o = _mm(hn2, Wo["odd_w_in"], name="odd_in")
    cqn, ckvn, kr = _mla_prep_fwd(proj_o, qg, kvg, tabs)
    q = _q_post(_mm(cqn, Wo["mla_w_uq"], name="mla_uq"), tabs, transpose=False, name="q_post")
    kvb = _mm(ckvn, Wo["mla_w_ukv"], out_dtype=bf16, name="mla_ukv")
    o, lse = _flash_fwd(q, kvb, kr)
    h3, hnf1 = _mm(o, Wo["odd_w_out"], res=h2, norm_g=g_ffn[1:2], name="odd_out")
    Wf1 = get_w("ffn1", h3)
    h4, ffn1 = _ffn_fwd(h3, hnf1, Wf1["ffn_w_in"], cw[1], cb[1:2], Wf1["ffn_w_out"], 1)
    loss, dh4, dg_final = _loss_head(h4, row(P["final_norm_g"]), target)

    dh3, dg_ffn1, dw_fin1, dcw1, dcb1, dw_fout1 = _ffn_bwd(
        h3, g_ffn[1:2], Wf1["ffn_w_in"], cw[1], cb[1:2], Wf1["ffn_w_out"], ffn1, dh4, 1)
    sent = put_g("ffn1", {"ffn_w_in": dw_fin1, "ffn_w_out": dw_fout1})
    do = _mm(dh3, Wo["odd_w_out"], tb=True, out_dtype=bf16, name="odd_do", dep=sent)
    dw_oout = _mm(o, dh3, ta=True, also_bf16=True, name="odd_dwout")
    dkv, dkr_h, dq = _flash_bwd(q, kvb, kr, o, do, lse)
    dq = _q_post(dq, tabs, transpose=True, name="dq_post")
    dw_uq = _mm(cqn, dq, ta=True, also_bf16=True, name="mla_dwuq")
    dcqn = _mm(dq, Wo["mla_w_uq"], tb=True, name="mla_dcq")
    dw_ukv = _mm(ckvn, dkv, ta=True, also_bf16=True, name="mla_dwukv")
    dckvn = _mm(dkv, Wo["mla_w_ukv"], tb=True, name="mla_dckv")
    dproj_o, dqg, dkvg = _mla_prep_bwd(proj_o, qg, kvg, tabs, dcqn, dckvn, dkr_h)
    dw_oin = _mm(hn2, dproj_o, ta=True, also_bf16=True, name="odd_dwin")
    sent = put_g("odd", {"odd_w_in": dw_oin, "mla_w_uq": dw_uq, "mla_w_ukv": dw_ukv, "odd_w_out": dw_oout})
    dh2, dg_mix1 = _mm(dproj_o, Wo["odd_w_in"], tb=True, res=dh3, norm_bwd=(h2, g_mix[1:2]), name="odd_dhn")

    dh1, dg_ffn0, dw_fin0, dcw0, dcb0, dw_fout0 = _ffn_bwd(
        h1, g_ffn[0:1], Wf0["ffn_w_in"], cw[0], cb[0:1], Wf0["ffn_w_out"], ffn0, dh2, 0, dep=sent)
    sent = put_g("ffn0", {"ffn_w_in": dw_fin0, "ffn_w_out": dw_fout0})
    dycat = _mm(dh1, Wr["even_w_out"], tb=True, name="even_dy", dep=sent)
    dw_eout = _mm(ycat, dh1, ta=True, also_bf16=True, name="even_dwout")
    dq_h, df_h, di_h, dg_h, dlbl, dhng = _hgrn_bwd(proj_e, lbl, hng, states, dycat)
    dyb_seg = _to_segments(dycat[:, 512:])
    dy_s5, glr, gli, dcre, dcim, dd, dwg, dbg = _s5_bwd_a(
        u_seg, bre, bim, ar, ai, s0r, s0i, cre, cim, cret, cimt, dsk, Wr["s5_w_glu"], bg, dyb_seg, rb=rb)
    du_seg, dbre, dbim, dar, dai = _s5_bwd_b(
        u_seg, bre, bim, bret, bimt, ar, ai, s0r, s0i, glr, gli, cret, cimt, dsk, dy_s5, rb=rb)
    dproj_e = jnp.concatenate([dq_h, df_h, di_h, dg_h, _from_segments(du_seg)], axis=1)
    dx, dg_mix0 = _mm(dproj_e, We["even_w_in"], tb=True, res=dh1, norm_bwd=(x, g_mix[0:1]), name="even_dhn", tk=1280)
    dw_ein = _mm(hn0, dproj_e, ta=True, also_bf16=True, name="even_dwin", tn=1280)

    unblk = lambda m, a, b: jnp.swapaxes(_blockdiag_t(m, a, b), 1, 2)
    dbbr = unblk(dbre, S5_GROUP, S5_STATE).reshape(S5_N, S5_GROUP)
    dbbi = unblk(dbim, S5_GROUP, S5_STATE).reshape(S5_N, S5_GROUP)
    d_ar, d_ai, d_ldt, d_br, d_bi = _s5_disc_bwd(*disc_in, (dar.reshape(S5_N, 1), dai.reshape(S5_N, 1), dbbr, dbbi))
    small = {
        "norm_mix_g": jnp.concatenate([dg_mix0, dg_mix1], axis=0),
        "norm_ffn_g": jnp.concatenate([dg_ffn0, dg_ffn1], axis=0),
        "final_norm_g": dg_final.reshape(-1),
        "hgrn_lb_logits": dlbl, "hgrn_norm_g": dhng,
        "s5_a_re": d_ar.reshape(1, S5_GROUPS, S5_STATE), "s5_a_im": d_ai.reshape(1, S5_GROUPS, S5_STATE),
        "s5_log_dt": d_ldt.reshape(S5_GROUPS, S5_STATE).sum(axis=1).reshape(1, S5_GROUPS),
        "s5_b_re": d_br.reshape(1, S5_GROUPS, S5_STATE, S5_GROUP), "s5_b_im": d_bi.reshape(1, S5_GROUPS, S5_STATE, S5_GROUP),
        "s5_c_re": unblk(dcre, S5_STATE, S5_GROUP).reshape(1, S5_GROUPS, S5_GROUP, S5_STATE),
        "s5_c_im": unblk(dcim, S5_STATE, S5_GROUP).reshape(1, S5_GROUPS, S5_GROUP, S5_STATE),
        "s5_d": dd, "s5_b_glu": dbg, "mla_q_norm_g": dqg, "mla_kv_norm_g": dkvg,
        "ffn_conv_w": jnp.stack([dcw0, dcw1]), "ffn_conv_b": jnp.concatenate([dcb0, dcb1], axis=0),
    }
    put_g("even", {"even_w_in": dw_ein, "s5_w_glu": (dwg, dwg.astype(bf16)), "even_w_out": dw_eout}, small)
    return loss, dx


WEIGHTS = ["norm_mix_g", "norm_ffn_g", "final_norm_g", "even_w_in", "hgrn_lb_logits", "hgrn_norm_g", "s5_a_re", "s5_a_im",
           "s5_log_dt", "s5_b_re", "s5_b_im", "s5_c_re", "s5_c_im", "s5_d", "s5_w_glu", "s5_b_glu", "even_w_out", "odd_w_in",
           "mla_q_norm_g", "mla_w_uq", "mla_kv_norm_g", "mla_w_ukv", "odd_w_out", "ffn_w_in", "ffn_conv_w", "ffn_conv_b",
           "ffn_w_out"]
SMALL_SHARDED = {"mla_q_norm_g": 1, "mla_kv_norm_g": 1, "ffn_conv_w": 2}
SMALL = [n for n in WEIGHTS if n not in BIG]
SMALL_REP = [n for n in SMALL if n not in SMALL_SHARDED]


def _pack_rows(shapes):
    n = sum(math.prod(s) for s in shapes)
    return -(-n // (SUBLANES * LANES)) * SUBLANES


def _pack(arrays, rows):
    flat = jnp.concatenate([a.reshape(-1) for a in arrays])
    return jnp.pad(flat, (0, rows * LANES - flat.shape[0])).reshape(rows, LANES)


def _unpack(block, shapes):
    flat, out, off = block.reshape(-1), [], 0
    for s in shapes:
        n = math.prod(s)
        out.append(flat[off:off + n].reshape(s))
        off += n
    return out


def kernel(x, positions, norm_mix_g, norm_ffn_g, final_norm_g, even_w_in, hgrn_lb_logits, hgrn_norm_g, s5_a_re, s5_a_im, s5_log_dt, s5_b_re, s5_b_im, s5_c_re, s5_c_im, s5_d, s5_w_glu, s5_b_glu, even_w_out, odd_w_in, mla_q_norm_g, mla_w_uq, mla_kv_norm_g, mla_w_ukv, odd_w_out, ffn_w_in, ffn_conv_w, ffn_conv_b, ffn_w_out, loss_target, m_norm_mix_g, m_norm_ffn_g, m_final_norm_g, m_even_w_in, m_hgrn_lb_logits, m_hgrn_norm_g, m_s5_a_re, m_s5_a_im, m_s5_log_dt, m_s5_b_re, m_s5_b_im, m_s5_c_re, m_s5_c_im, m_s5_d, m_s5_w_glu, m_s5_b_glu, m_even_w_out, m_odd_w_in, m_mla_q_norm_g, m_mla_w_uq, m_mla_kv_norm_g, m_mla_w_ukv, m_odd_w_out, m_ffn_w_in, m_ffn_conv_w, m_ffn_conv_b, m_ffn_w_out, v_norm_mix_g, v_norm_ffn_g, v_final_norm_g, v_even_w_in, v_hgrn_lb_logits, v_hgrn_norm_g, v_s5_a_re, v_s5_a_im, v_s5_log_dt, v_s5_b_re, v_s5_b_im, v_s5_c_re, v_s5_c_im, v_s5_d, v_s5_w_glu, v_s5_b_glu, v_even_w_out, v_odd_w_in, v_mla_q_norm_g, v_mla_w_uq, v_mla_kv_norm_g, v_mla_w_ukv, v_odd_w_out, v_ffn_w_in, v_ffn_conv_w, v_ffn_conv_b, v_ffn_w_out):
    args = dict(locals())
    w = {n: args[n] for n in WEIGHTS}
    m = {n: args["m_" + n] for n in WEIGHTS}
    v = {n: args["v_" + n] for n in WEIGHTS}
    k = 2 * lax.axis_index("x") + lax.axis_index("y")
    kidx = k.reshape(1).astype(jnp.int32)
    axis2d = lambda n: BIG[n] - (1 if n in LAYERED else 0)
    slab = lambda n: w[n].shape[1 + axis2d(n)]

    small_sh_shapes = [w[n].shape for n in SMALL_SHARDED]
    rb = _pack_rows(small_sh_shapes)
    items = {}
    for group, names in GROUPS.items():
        layer = GROUP_LAYER.get(group, 0)
        items[group] = [(_Gather(axis2d(n), slab(n)), None,
                         _place_slab(w[n][layer], axis2d(n), N_CHIPS, kidx, bf16, name=f"place_{n}_{layer}")) for n in names]
    items["even_in"].append((_Gather(0, rb), None,
                             _place_slab(_pack([w[n] for n in SMALL_SHARDED], rb), 0, N_CHIPS, kidx, f32, name="place_small")))
    gathers, tokens = {}, []
    for group in GROUPS:
        sems, srcs, lands, token = _push_start(f"gather_start_{group}", items[group])
        gathers[group] = ([it[0] for it in items[group]], sems, srcs, lands)
        tokens.append(token[0, 0])
    started = functools.reduce(jnp.add, tokens)

    def landed(group, after):
        return _push_wait(f"gather_wait_{group}", [gathers[group]], [after])[0]

    even = landed("even_in", (started + norm_mix_g[0, 0]).reshape(1))
    per_chip = [_unpack(even[-1][c * rb:(c + 1) * rb], small_sh_shapes) for c in range(N_CHIPS)]
    P = {n: w[n] for n in SMALL_REP}
    for i, (n, ax) in enumerate(SMALL_SHARDED.items()):
        P[n] = jnp.concatenate([per_chip[c][i] for c in range(N_CHIPS)], axis=ax)
    P["mla_q_norm_g"], P["mla_kv_norm_g"] = P["mla_q_norm_g"].reshape(1, -1), P["mla_kv_norm_g"].reshape(1, -1)
    fix_w = {"odd_w_in": _pad_odd, "mla_w_uq": _uq_cat}

    def get_w(group, after):
        full = even if group == "even_in" else landed(group, after)
        return {n: fix_w.get(n, lambda a: a)(a) for n, a in zip(GROUPS[group], full)}

    fix_g = {"odd_w_in": lambda g: g[:, :odd_w_in.shape[2]], "mla_w_uq": _uq_uncat}
    g32, scatters, land_now = {}, {}, {}
    ra = _pack_rows([w[n].shape for n in SMALL_REP])
    rs = ra + N_CHIPS * rb
    didx = (2 * kidx + lax.axis_index("c")).astype(jnp.int32)

    def put_g(group, grads, small=None):
        layer = GROUP_LAYER.get(group)
        routes, srcs, names = [], [], list(grads)
        for n in names:
            f = fix_g.get(n, lambda g: g)
            g32.setdefault(n, {})[layer or 0] = f(grads[n][0])
            routes.append(_Scatter(axis2d(n), slab(n), layer if n in LAYERED else None))
            srcs.append(f(grads[n][1]))
            if n not in land_now:
                land_now[n] = lax.empty((3,) + w[n].shape[0 if n in LAYERED else 1:], bf16)
        if small is not None:
            blocks = [_pack([small[n] for n in SMALL_REP], ra)]
            for chip in range(N_CHIPS):
                sl = lambda n, ax: lax.slice_in_dim(small[n].reshape(w[n].shape[:ax] + (-1,) + w[n].shape[ax + 1:]),
                                                    chip * w[n].shape[ax], (chip + 1) * w[n].shape[ax], axis=ax)
                blocks.append(_pack([sl(n, ax) for n, ax in SMALL_SHARDED.items()], rb))
            names.append("small")
            routes.append(_ToAll(rs))
            srcs.append(None)
            land_now["small"] = _place_slab(jnp.concatenate(blocks), 0, N_DEV, didx, f32, name="place_small_grads")
        sems, srcs, lands, token = _push_start(f"scatter_start_{group}", [(r, s, land_now[n]) for r, s, n in zip(routes, srcs, names)])
        land_now.update(zip(names, lands))
        scatters[group] = (routes, sems, srcs, names)
        sent.append(token)
        return token

    sent = []
    loss, dx = _local_step(x[0], positions[0], loss_target[0], get_w, P, put_g)
    sent_last = sent[-1]
    loss = lax.psum(loss[0, 0], ("x", "y", "c"))

    out = {}

    def finish(tag, groups, after):
        waits = [(scatters[g][0], scatters[g][1], scatters[g][2], [land_now[n] for n in scatters[g][3]]) for g in groups]
        for g, lands in zip(groups, _push_wait(f"scatter_wait_{tag}", waits, after)):
            land_now.update(zip(scatters[g][3], lands))
        names = [n for n in dict.fromkeys(n for g in groups for n in scatters[g][3]) if n != "small"]
        part = {}
        for n in names:
            recv = land_now[n] if n in LAYERED else land_now[n][:, None]
            part[n] = _sum4([g32[n][l] for l in sorted(g32[n])], axis2d(n), recv, kidx, name=f"sum4_{n}")
        other = _swap_with_sibling(part, tag)
        done = []
        for n in names:
            C = part[n].shape[-1]
            res = _adamw(w[n].reshape(-1, C), m[n].reshape(-1, C), v[n].reshape(-1, C), [part[n], other[n]], name=f"adamw_{n}")
            out[n] = [r.reshape(w[n].shape) for r in res]
            done.append(res[0])
        return done

    done = finish("a", ["ffn1", "odd", "ffn0"], [dx, sent_last])
    finish("b", ["even"], done)

    order = SMALL_REP + list(SMALL_SHARDED)
    packed = lambda src: jnp.concatenate([_pack([src[n] for n in SMALL_REP], ra), _pack([src[n] for n in SMALL_SHARDED], rb)])
    res = _adamw_small(land_now["small"], packed(w), packed(m), packed(v), kidx, ra, rb)
    for r in res:
        parts = _unpack(r[:ra], [w[n].shape for n in SMALL_REP]) + _unpack(r[ra:], small_sh_shapes)
        for n, a in zip(order, parts):
            out.setdefault(n, []).append(a)

    return (loss, dx[None], *[out[n][0] for n in WEIGHTS], *[out[n][1] for n in WEIGHTS],
            *[out[n][2] for n in WEIGHTS], *[out[n][3] for n in WEIGHTS])
```

```python
import functools
import math

import jax
import jax.numpy as jnp
from jax import lax
from jax.experimental import pallas as pl
from jax.experimental.pallas import tpu as pltpu

f32, bf16 = jnp.float32, jnp.bfloat16
EPS = 1e-6
LANES = 128
SUBLANES = 8
VMEM_BYTES = 48 * 1024 * 1024
HGRN_CHUNK = 64
HGRN_HEADS = 4
S5_GROUPS, S5_STATE, S5_GROUP = 32, 64, 16
S5_N = S5_GROUPS * S5_STATE
S5_SEG = SUBLANES
MLA_HEADS, MLA_NOPE, MLA_ROPE, MLA_V = 8, 128, 64, 128
MLA_QK = MLA_NOPE + MLA_ROPE
MLA_Q_RANK, MLA_KV_RANK = 384, 256
ROPE_THETA = 10000.0
D_FF = 2816
ADAM_LR, ADAM_B1, ADAM_B2, ADAM_EPS, ADAM_WD, ADAM_STEP = 0.001, 0.9, 0.999, 1e-08, 0.01, 10
MESH = pl.DeviceIdType.MESH
HI = lax.Precision.HIGHEST


def _cp(*dims):
    return pltpu.CompilerParams(dimension_semantics=dims if dims else None, vmem_limit_bytes=VMEM_BYTES)


def _tile(n, t):
    if n <= t:
        return n
    c = (t // LANES) * LANES
    while c >= LANES:
        if n % c == 0:
            return c
        c -= LANES
    return n


def _dot(a, b, dn=None, precision=None):
    if dn is None:
        dn = (((a.ndim - 1,), (0,)), ((), ()))
    return lax.dot_general(a, b, dn, preferred_element_type=f32, precision=precision)


NT = (((1,), (1,)), ((), ()))
TN = (((0,), (0,)), ((), ()))


def _bdot(a, b, dn=None):
    return _dot(a.astype(bf16), b.astype(bf16), dn)


MM_PARTS = 2


def _mm(a, b, *, name, ta=False, tb=False, out_dtype=f32, res=None, also_bf16=False, tm=1024, tn=1024, tk=1024, dep=None,
        norm_g=None, norm_bwd=None):
    halves = lambda s: (s[1], 2 * s[2]) if len(s) == 3 else s
    M, K = (a.shape[1], a.shape[0]) if ta else halves(a.shape)
    N = b.shape[0] if tb else halves(b.shape)[1]
    rows = norm_g is not None or norm_bwd is not None
    if rows:
        tm, tn, tk = 512, N, K
    tm, tn, tk = _tile(M, tm), _tile(N, tn), _tile(K, tk)
    both = rows and a.ndim == 3 and tb
    if a.ndim == 3 and not both:
        tk = _tile(K // 2, tk)
    if b.ndim == 3:
        tn = _tile(N // 2, tn)
    nk = K // tk
    parts = MM_PARTS if tm % (MM_PARTS * LANES) == 0 else 1
    dn = (((0 if ta else 1,), (1 if tb else 0,)), ((), ()))
    extra = [] if norm_bwd is None else list(norm_bwd)
    if norm_g is not None:
        extra.append(norm_g)

    def body(*refs):
        a_ref, b_ref = refs[0], refs[1]
        r_ref = refs[2] if res is not None else None
        nin = 2 + (res is not None) + (dep is not None) + len(extra)
        ex = refs[nin - len(extra):nin]
        outs = refs[nin:-1] if nk > 1 else refs[nin:]
        acc = refs[-1] if nk > 1 else None
        k = pl.program_id(2)
        b_blk = b_ref[...]
        if nk > 1:
            @pl.when(k == 0)
            def _():
                acc[...] = jnp.zeros_like(acc)

        groups = []
        for part in range(parts):
            rows = pl.ds(part * (tm // parts), tm // parts)
            if both:
                p = _bdot(a_ref[0, rows, :], b_blk[:, :K // 2], dn) + _bdot(a_ref[1, rows, :], b_blk[:, K // 2:], dn)
            else:
                p = _bdot(a_ref[:, rows] if ta else a_ref[rows, :], b_blk, dn)
            if nk > 1:
                acc[rows, :] += p
            groups.append((rows, p))

        def epilogue():
            for part, (rows, p) in enumerate(groups):
                r = acc[rows, :] if nk > 1 else p
                if norm_bwd is not None:
                    r, dg = _rms_bwd_math(ex[0][rows, :], ex[1][...], r)
                    if part == 0:
                        @pl.when(pl.program_id(0) == 0)
                        def _(dg=dg):
                            outs[1][...] = dg

                    @pl.when((pl.program_id(0) > 0) | (part > 0))
                    def _(dg=dg):
                        outs[1][...] += dg
                if r_ref is not None:
                    r = r + r_ref[rows, :]
                outs[0][rows, :] = r.astype(out_dtype)
                if also_bf16:
                    outs[1][rows, :] = r.astype(bf16)
                if norm_g is not None:
                    outs[1][rows, :] = _rms(r, ex[-1][...]).astype(bf16)

        if nk > 1:
            pl.when(k == nk - 1)(epilogue)
        else:
            epilogue()

    a_spec = pl.BlockSpec((tk, tm), lambda i, j, k: (k, i)) if ta else pl.BlockSpec((tm, tk), lambda i, j, k: (i, k))
    b_spec = pl.BlockSpec((tn, tk), lambda i, j, k: (j, k)) if tb else pl.BlockSpec((tk, tn), lambda i, j, k: (k, j))
    if rows:
        b_spec = pl.BlockSpec((tn, tk) if tb else (tk, tn), lambda i, j, k: (0, 0), pipeline_mode=pl.Buffered(1))
    if both:
        a_spec = pl.BlockSpec((2, tm, K // 2), lambda i, j, k: (0, i, 0))
    elif a.ndim == 3:
        kh = K // 2 // tk
        a_spec = pl.BlockSpec((None, tm, tk), lambda i, j, k: (k // kh, i, k % kh))
    if b.ndim == 3:
        nh = N // 2 // tn
        b_spec = pl.BlockSpec((None, tk, tn), lambda i, j, k: (j // nh, k, j % nh))
    o_spec = pl.BlockSpec((tm, tn), lambda i, j, k: (i, j))
    in_specs, args = [a_spec, b_spec], [a, b]
    if res is not None:
        in_specs.append(o_spec)
        args.append(res)
    if dep is not None:
        in_specs.append(pl.BlockSpec(memory_space=pl.ANY))
        args.append(dep)
    vec = pl.BlockSpec((1, tn), lambda i, j, k: (0, j))
    if norm_bwd is not None:
        in_specs += [o_spec, vec]
    if norm_g is not None:
        in_specs.append(vec)
    args += extra
    out_shape = [jax.ShapeDtypeStruct((M, N), out_dtype)]
    out_specs = [o_spec]
    if also_bf16 or norm_g is not None:
        out_shape.append(jax.ShapeDtypeStruct((M, N), bf16))
        out_specs.append(o_spec)
    if norm_bwd is not None:
        out_shape.append(jax.ShapeDtypeStruct((1, N), f32))
        out_specs.append(vec)
    dims = ("arbitrary" if norm_bwd is not None else "parallel", "parallel", "arbitrary")
    out = pl.pallas_call(
        body, name=name, grid=(M // tm, N // tn, nk), in_specs=in_specs, out_specs=out_specs, out_shape=out_shape,
        scratch_shapes=[pltpu.VMEM((tm, tn), f32)] if nk > 1 else [], compiler_params=_cp(*dims),
    )(*args)
    return out if len(out) > 1 else out[0]


def _rms_fwd(x, g, *, name, col=0, width=None, tm=512):
    T = x.shape[0]
    width = x.shape[1] if width is None else width
    tm = _tile(T, tm)

    def body(x_ref, g_ref, o_ref):
        xv = x_ref[...]
        r = lax.rsqrt(jnp.mean(xv * xv, axis=-1, keepdims=True) + EPS)
        o_ref[...] = (xv * r * g_ref[...]).astype(bf16)

    return pl.pallas_call(
        body, name=name, grid=(T // tm,),
        in_specs=[pl.BlockSpec((tm, width), lambda i: (i, col)), pl.BlockSpec((1, width), lambda i: (0, 0))],
        out_specs=pl.BlockSpec((tm, width), lambda i: (i, 0)), out_shape=jax.ShapeDtypeStruct((T, width), bf16),
        compiler_params=_cp("parallel"),
    )(x, g)


def _rms_bwd_math(xv, g, dy):
    r = lax.rsqrt(jnp.mean(xv * xv, axis=-1, keepdims=True) + EPS)
    xh = xv * r
    dxh = dy * g
    dx = r * (dxh - xh * jnp.mean(dxh * xh, axis=-1, keepdims=True))
    dg = jnp.sum(dy * xh, axis=0, keepdims=True)
    return dx, dg


def _loss_head(h, g, target, *, tm=512):
    T, D = h.shape
    tm = _tile(T, tm)

    def body(h_ref, g_ref, t_ref, loss_ref, dh_ref, dg_ref):
        hv, gv = h_ref[...], g_ref[...]
        r = lax.rsqrt(jnp.mean(hv * hv, axis=-1, keepdims=True) + EPS)
        e = hv * r * gv - t_ref[...]
        part = 0.5 * jnp.sum(jnp.mean(e * e, axis=-1, keepdims=True), axis=0, keepdims=True)
        dx, dg = _rms_bwd_math(hv, gv, e * (1.0 / D))
        dh_ref[...] = dx

        @pl.when(pl.program_id(0) == 0)
        def _():
            loss_ref[...] = part
            dg_ref[...] = dg

        @pl.when(pl.program_id(0) > 0)
        def _():
            loss_ref[...] += part
            dg_ref[...] += dg

    row = pl.BlockSpec((tm, D), lambda i: (i, 0))
    vec = pl.BlockSpec((1, D), lambda i: (0, 0))
    return pl.pallas_call(
        body, name="loss_head", grid=(T // tm,), in_specs=[row, vec, row],
        out_specs=[pl.BlockSpec((1, 1), lambda i: (0, 0)), row, vec],
        out_shape=[jax.ShapeDtypeStruct((1, 1), f32), jax.ShapeDtypeStruct((T, D), f32), jax.ShapeDtypeStruct((1, D), f32)],
        compiler_params=_cp("arbitrary"),
    )(h, g, target)


FFN_W = 2 * LANES
FFN_ROWS = 128
HALO = 2 * SUBLANES


def _conv_taps(a_ref, c, rc):
    if isinstance(c, int) and c == 0:
        ext = jnp.concatenate([jnp.zeros((HALO, FFN_W), f32), a_ref[pl.ds(0, rc), :].astype(f32)], axis=0)
    else:
        ext = a_ref[pl.ds(pl.multiple_of(c * rc - HALO, HALO), rc + HALO), :].astype(f32)
    return ext[HALO:], pltpu.roll(ext, 1, 0)[HALO:], pltpu.roll(ext, 2, 0)[HALO:]


def _chunk_rows(c, rc):
    return pl.ds(c * rc, rc) if isinstance(c, int) else pl.ds(pl.multiple_of(c * rc, rc), rc)


def _ffn_mid_fwd(au, cw, cb, *, name):
    T = au.shape[0]
    F = au.shape[1] // 2
    nb = F // FFN_W
    rc = min(FFN_ROWS, T)
    nc = T // rc

    def body(a_ref, u_ref, w_ref, b_ref, z_ref):
        w, b = w_ref[...], b_ref[...]

        def chunk(c):
            a, a1, a2 = _conv_taps(a_ref, c, rc)
            rows = _chunk_rows(c, rc)
            ac = (w[0:1] * a2 + w[1:2] * a1 + w[2:3] * a + b).astype(bf16)
            z_ref[rows, :] = ac * jax.nn.sigmoid(ac) * u_ref[rows, :]

        chunk(0)
        lax.fori_loop(1, nc, lambda c, _: chunk(c), None)

    return pl.pallas_call(
        body, name=name, grid=(nb,),
        in_specs=[pl.BlockSpec((T, FFN_W), lambda j: (0, j)), pl.BlockSpec((T, FFN_W), lambda j: (0, nb + j)),
                  pl.BlockSpec((3, FFN_W), lambda j: (0, j)), pl.BlockSpec((1, FFN_W), lambda j: (0, j))],
        out_specs=pl.BlockSpec((T, FFN_W), lambda j: (0, j)), out_shape=jax.ShapeDtypeStruct((T, F), bf16),
        compiler_params=_cp("parallel"),
    )(au, au, cw, cb)


def _ffn_mid_bwd(au, cw, cb, dz, *, name):
    T = au.shape[0]
    F = au.shape[1] // 2
    nb = F // FFN_W
    rc = min(FFN_ROWS, T)
    nc = T // rc

    def body(a_ref, u_ref, w_ref, b_ref, dz_ref, dau_ref, dw_ref, db_ref):
        w, b = w_ref[...], b_ref[...]

        def chunk(c, carry):
            nxt, s0, s1, s2, sb = carry
            a, a1, a2 = _conv_taps(a_ref, c, rc)
            rows = _chunk_rows(c, rc)
            ac = (w[0:1] * a2 + w[1:2] * a1 + w[2:3] * a + b).astype(bf16)
            sg = jax.nn.sigmoid(ac)
            dz = dz_ref[rows, :]
            dau_ref[1, rows, :] = dz * ac * sg
            dac = (dz * u_ref[rows, :] * sg * (1.0 + ac * (1.0 - sg))).astype(f32)
            ext = jnp.concatenate([dac, nxt], axis=0)
            d1, d2 = pltpu.roll(ext, rc + HALO - 1, 0)[:rc], pltpu.roll(ext, rc + HALO - 2, 0)[:rc]
            dau_ref[0, rows, :] = (w[2:3] * dac + w[1:2] * d1 + w[0:1] * d2).astype(bf16)
            tot = lambda v: jnp.sum(v, axis=0, keepdims=True)
            return dac[:HALO], s0 + tot(dac * a2), s1 + tot(dac * a1), s2 + tot(dac * a), sb + tot(dac)

        z = jnp.zeros((1, FFN_W), f32)
        carry = (jnp.zeros((HALO, FFN_W), f32), z, z, z, z)
        carry = lax.fori_loop(0, nc - 1, lambda k, cr: chunk(nc - 1 - k, cr), carry)
        _, s0, s1, s2, sb = chunk(0, carry)
        rows = lax.broadcasted_iota(jnp.int32, (3, FFN_W), 0)
        dw_ref[...] = jnp.where(rows == 0, s0, jnp.where(rows == 1, s1, s2))
        db_ref[...] = sb

    col = lambda off: pl.BlockSpec((T, FFN_W), lambda j: (0, off + j))
    return pl.pallas_call(
        body, name=name, grid=(nb,),
        in_specs=[col(0), col(nb), pl.BlockSpec((3, FFN_W), lambda j: (0, j)), pl.BlockSpec((1, FFN_W), lambda j: (0, j)), col(0)],
        out_specs=[pl.BlockSpec((2, T, FFN_W), lambda j: (0, 0, j)), pl.BlockSpec((3, FFN_W), lambda j: (0, j)),
                   pl.BlockSpec((1, FFN_W), lambda j: (0, j))],
        out_shape=[jax.ShapeDtypeStruct((2, T, F), bf16), jax.ShapeDtypeStruct((3, F), f32), jax.ShapeDtypeStruct((1, F), f32)],
        compiler_params=_cp("parallel"),
    )(au, au, cw, cb, dz)


BNN = (((2,), (1,)), ((0,), (0,)))
BNT = (((2,), (2,)), ((0,), (0,)))
BTN = (((1,), (1,)), ((0,), (0,)))


def _heads(x):
    return jnp.stack([x[:, h * LANES:(h + 1) * LANES] for h in range(HGRN_HEADS)])


def _put_heads(ref, rows, x, dtype):
    for h in range(HGRN_HEADS):
        ref[rows, h * LANES:(h + 1) * LANES] = x[h].astype(dtype)


def _hgrn_lb(l):
    m = jnp.max(l, axis=0, keepdims=True)
    e = jnp.exp(l - m)
    return e[0:1] / jnp.sum(e, axis=0, keepdims=True)


def _hgrn_chunk(q, fx, lb):
    H, C = q.shape[0], q.shape[1]
    sg = jax.nn.sigmoid(fx)
    F = lb + (1.0 - lb) * sg
    k = 1.0 - F
    logF = jnp.log(F)
    r = lax.broadcasted_iota(jnp.int32, (H, C, C), 1)
    c = lax.broadcasted_iota(jnp.int32, (H, C, C), 2)
    tril = (r >= c)
    b = _dot(tril.astype(f32), logF, BNN, precision=HI)
    bl = jnp.sum(logF, axis=1, keepdims=True)
    eb = jnp.exp(b)
    enb = jnp.exp(-b)
    elb = jnp.exp(bl - b)
    return dict(sg=sg, F=F, k=k, b=b, bl=bl, eb=eb, enb=enb, elb=elb, qd=q * eb, kd=k * enb, kl=k * elb, tril=tril)


def _hgrn_fwd(proj, lbl, ng, *, rb=512):
    T = proj.shape[0]
    rb = min(rb, T)
    cpb = rb // HGRN_CHUNK
    nblk = T // rb
    H = HGRN_HEADS

    def body(q_ref, f_ref, i_ref, g_ref, lbl_ref, ng_ref, y_ref, st_ref, S):
        @pl.when(pl.program_id(0) == 0)
        def _():
            S[...] = jnp.zeros_like(S)

        lb = _heads(_hgrn_lb(lbl_ref[...]))
        ngv = _heads(ng_ref[...])
        for c in range(cpb):
            sl = pl.ds(c * HGRN_CHUNK, HGRN_CHUNK)
            v, gx = _heads(i_ref[sl, :]), _heads(g_ref[sl, :])
            ch = _hgrn_chunk(_heads(q_ref[sl, :]), _heads(f_ref[sl, :]), lb)
            att = jnp.where(ch["tril"], _bdot(ch["qd"], ch["kd"], BNT), 0.0)
            St = S[...]
            st_ref[:, c] = St
            o = _bdot(att, v, BNN) + _bdot(ch["qd"], St, BNT)
            S[...] = St * jnp.exp(ch["bl"]) + _bdot(v, ch["kl"], BTN)
            r = lax.rsqrt(jnp.mean(o * o, axis=-1, keepdims=True) + EPS)
            _put_heads(y_ref, sl, o * r * ngv * (gx * jax.nn.sigmoid(gx)), bf16)

    col = lambda off: pl.BlockSpec((rb, H * LANES), lambda n: (n, off))
    return pl.pallas_call(
        body, name="hgrn_fwd", grid=(nblk,),
        in_specs=[col(0), col(1), col(2), col(3), pl.BlockSpec((2, H * LANES), lambda n: (0, 0)),
                  pl.BlockSpec((1, H * LANES), lambda n: (0, 0))],
        out_specs=[pl.BlockSpec((rb, H * LANES), lambda n: (n, 0)),
                   pl.BlockSpec((H, cpb, LANES, LANES), lambda n: (0, n, 0, 0))],
        out_shape=[jax.ShapeDtypeStruct((T, H * LANES), bf16),
                   jax.ShapeDtypeStruct((H, T // HGRN_CHUNK, LANES, LANES), f32)],
        scratch_shapes=[pltpu.VMEM((H, LANES, LANES), f32)], compiler_params=_cp("arbitrary"),
    )(proj, proj, proj, proj, lbl, ng)


def _hgrn_bwd(proj, lbl, ng, states, dy, *, rb=512):
    T = proj.shape[0]
    rb = min(rb, T)
    cpb = rb // HGRN_CHUNK
    nblk = T // rb
    H = HGRN_HEADS
    C = HGRN_CHUNK

    def body(q_ref, f_ref, i_ref, g_ref, lbl_ref, ng_ref, st_ref, dy_ref,
             dq_ref, df_ref, di_ref, dg_ref, dl_ref, dng_ref, dS, dlb_acc, dng_acc):
        n = pl.program_id(0)

        @pl.when(n == 0)
        def _():
            dS[...] = jnp.zeros_like(dS)
            dlb_acc[...] = jnp.zeros_like(dlb_acc)
            dng_acc[...] = jnp.zeros_like(dng_acc)

        lb_row = _hgrn_lb(lbl_ref[...])
        lb = _heads(lb_row)
        ngv = _heads(ng_ref[...])
        r_i = lax.broadcasted_iota(jnp.int32, (H, C, C), 1)
        c_i = lax.broadcasted_iota(jnp.int32, (H, C, C), 2)
        triu = (c_i >= r_i).astype(f32)
        rows_sum = lambda x: jnp.sum(x, axis=1, keepdims=True)
        for c in reversed(range(cpb)):
            sl = pl.ds(c * C, C)
            q, v, gx = _heads(q_ref[sl, :]), _heads(i_ref[sl, :]), _heads(g_ref[sl, :])
            ch = _hgrn_chunk(q, _heads(f_ref[sl, :]), lb)
            qd, kd, kl = ch["qd"], ch["kd"], ch["kl"]
            att = jnp.where(ch["tril"], _bdot(qd, kd, BNT), 0.0)
            St = st_ref[:, c]
            o = _bdot(att, v, BNN) + _bdot(qd, St, BNT)
            r = lax.rsqrt(jnp.mean(o * o, axis=-1, keepdims=True) + EPS)
            on = o * r
            sgg = jax.nn.sigmoid(gx)
            gate = gx * sgg
            dyv = _heads(dy_ref[sl, :].astype(f32))
            _put_heads(dg_ref, sl, dyv * on * ngv * sgg * (1.0 + gx * (1.0 - sgg)), bf16)
            dng_acc[...] += rows_sum(dyv * on * gate)
            don = dyv * ngv * gate
            do = r * (don - on * jnp.mean(don * on, axis=-1, keepdims=True))
            dSt = dS[...]
            dA = jnp.where(ch["tril"], _bdot(do, v, BNT), 0.0)
            dv = _bdot(att, do, BTN) + _bdot(kl, dSt, BNT)
            dqd = _bdot(dA, kd, BNN) + _bdot(do, St, BNN)
            dkd = _bdot(dA, qd, BTN)
            dkl = _bdot(v, dSt, BNN)
            dec = jnp.exp(ch["bl"])
            ddec = rows_sum(St * dSt)
            dS[...] = _bdot(do, qd, BTN) + dSt * dec
            dB = dqd * qd - dkd * kd - dkl * kl
            dbl = rows_sum(dkl * kl) + ddec * dec
            dk = dkd * ch["enb"] + dkl * ch["elb"]
            dlogF = _dot(triu, dB, BNN, precision=HI) + dbl
            dF = dlogF / ch["F"] - dk
            sg = ch["sg"]
            _put_heads(dq_ref, sl, dqd * ch["eb"], bf16)
            _put_heads(di_ref, sl, dv, bf16)
            _put_heads(df_ref, sl, dF * (1.0 - lb) * sg * (1.0 - sg), bf16)
            dlb_acc[...] += rows_sum(dF * (1.0 - sg))

        @pl.when(n == nblk - 1)
        def _():
            rows = lax.broadcasted_iota(jnp.int32, (2, LANES), 0)
            for h in range(H):
                hs = pl.ds(h * LANES, LANES)
                lbh = lb_row[:, h * LANES:(h + 1) * LANES]
                dl0 = dlb_acc[h] * lbh * (1.0 - lbh)
                dl_ref[:, hs] = jnp.where(rows == 0, dl0, -dl0)
                dng_ref[:, hs] = dng_acc[h]

    col = lambda off: pl.BlockSpec((rb, H * LANES), lambda n: (nblk - 1 - n, off))
    vec = lambda rows: pl.BlockSpec((rows, H * LANES), lambda n: (0, 0))
    tok = jax.ShapeDtypeStruct((T, H * LANES), bf16)
    return pl.pallas_call(
        body, name="hgrn_bwd", grid=(nblk,),
        in_specs=[col(0), col(1), col(2), col(3), vec(2), vec(1),
                  pl.BlockSpec((H, cpb, LANES, LANES), lambda n: (0, nblk - 1 - n, 0, 0)), col(0)],
        out_specs=[col(0), col(0), col(0), col(0), vec(2), vec(1)],
        out_shape=[tok, tok, tok, tok, jax.ShapeDtypeStruct((2, H * LANES), f32), jax.ShapeDtypeStruct((1, H * LANES), f32)],
        scratch_shapes=[pltpu.VMEM((H, LANES, LANES), f32), pltpu.VMEM((H, 1, LANES), f32), pltpu.VMEM((H, 1, LANES), f32)],
        compiler_params=_cp("arbitrary"),
    )(proj, proj, proj, proj, lbl, ng, states, dy)


def _s5_disc_math(ar, ai, ldt, br, bi):
    dt = jnp.exp(ldt)
    mag = jnp.exp(ar * dt)
    abr, abi = mag * jnp.cos(ai * dt), mag * jnp.sin(ai * dt)
    den = ar * ar + ai * ai
    xr, xi = abr - 1.0, abi
    cr = (xr * ar + xi * ai) / den
    ci = (xi * ar - xr * ai) / den
    return abr, abi, cr * br - ci * bi, cr * bi + ci * br


def _s5_disc_fwd(ar, ai, ldt, br, bi):
    def body(ar_ref, ai_ref, ldt_ref, br_ref, bi_ref, o0, o1, o2, o3):
        outs = _s5_disc_math(ar_ref[...], ai_ref[...], ldt_ref[...], br_ref[...], bi_ref[...])
        for o, v in zip((o0, o1, o2, o3), outs):
            o[...] = v

    return pl.pallas_call(
        body, name="s5_disc_fwd",
        out_shape=[jax.ShapeDtypeStruct(ar.shape, f32)] * 2 + [jax.ShapeDtypeStruct(br.shape, f32)] * 2,
    )(ar, ai, ldt, br, bi)


def _s5_disc_bwd(ar, ai, ldt, br, bi, cts):
    def body(ar_ref, ai_ref, ldt_ref, br_ref, bi_ref, c0, c1, c2, c3, o0, o1, o2, o3, o4):
        _, vjp = jax.vjp(_s5_disc_math, ar_ref[...], ai_ref[...], ldt_ref[...], br_ref[...], bi_ref[...])
        for o, v in zip((o0, o1, o2, o3, o4), vjp((c0[...], c1[...], c2[...], c3[...]))):
            o[...] = v

    return pl.pallas_call(
        body, name="s5_disc_bwd",
        out_shape=[jax.ShapeDtypeStruct(ar.shape, f32)] * 3 + [jax.ShapeDtypeStruct(br.shape, f32)] * 2,
    )(ar, ai, ldt, br, bi, *cts)


S5_LC = 512
S5_NLC = S5_N // S5_LC
S5_UB = 4
S5_UNROLL = 4
S5_TOGETHER = 2


def _cmul(ar, ai, xr, xi):
    return ar * xr - ai * xi, ar * xi + ai * xr


def _cpow(ar, ai, n):
    rr, ri = None, None
    br, bi = ar, ai
    while n:
        if n & 1:
            rr, ri = (br, bi) if rr is None else _cmul(rr, ri, br, bi)
        n >>= 1
        if n:
            br, bi = _cmul(br, bi, br, bi)
    return rr, ri


def _s5_bu(u_ref, bre_ref, bim_ref, xr, xi):
    for k in range(S5_UB):
        uk = u_ref[:, k * LANES:(k + 1) * LANES].astype(bf16)
        xr[:, k * S5_LC:(k + 1) * S5_LC] = _dot(uk, bre_ref[k])
        xi[:, k * S5_LC:(k + 1) * S5_LC] = _dot(uk, bim_ref[k])


def _s5_scan(xr, xi, sr, si, ar_ref, ai_ref, nsteps, store):
    for c0 in range(0, S5_NLC, S5_TOGETHER):
        css = [slice(c * S5_LC, (c + 1) * S5_LC) for c in range(c0, c0 + S5_TOGETHER)]
        a = [(jnp.broadcast_to(ar_ref[:, cs], (S5_SEG, S5_LC)), jnp.broadcast_to(ai_ref[:, cs], (S5_SEG, S5_LC))) for cs in css]

        def step(j, carry, css=css, a=a):
            rows = pl.ds(pl.multiple_of(j * S5_SEG, S5_SEG), S5_SEG)
            out = []
            for u, cs in enumerate(css):
                (a_r, a_i), pr, pi = a[u], carry[2 * u], carry[2 * u + 1]
                nr = a_r * pr - a_i * pi + xr[rows, cs]
                ni = a_r * pi + a_i * pr + xi[rows, cs]
                if store:
                    xr[rows, cs] = nr
                    xi[rows, cs] = ni
                out += [nr, ni]
            return tuple(out)

        init = tuple(v for cs in css for v in (sr[:, cs], si[:, cs]))
        fin = lax.fori_loop(0, nsteps, step, init)
        for u, cs in enumerate(css):
            sr[:, cs] = fin[2 * u]
            si[:, cs] = fin[2 * u + 1]


def _s5_rscan(dr, di, xr, xi, s0r, s0i, gr, gi, acc_r, acc_i, ar_ref, ai_ref, nsteps):
    for c in range(S5_NLC):
        cs = slice(c * S5_LC, (c + 1) * S5_LC)
        a_r = jnp.broadcast_to(ar_ref[:, cs], (S5_SEG, S5_LC))
        a_i = jnp.broadcast_to(ai_ref[:, cs], (S5_SEG, S5_LC))

        def step(jj, carry, cs=cs, a_r=a_r, a_i=a_i):
            pr, pi, cr, ci = carry
            j = nsteps - 1 - jj
            rows = pl.ds(pl.multiple_of(j * S5_SEG, S5_SEG), S5_SEG)
            nr = dr[rows, cs] + a_r * pr + a_i * pi
            ni = di[rows, cs] + a_r * pi - a_i * pr
            dr[rows, cs] = nr
            di[rows, cs] = ni
            if acc_r is not None:
                prev = pl.ds(pl.multiple_of(jnp.maximum(j - 1, 0) * S5_SEG, S5_SEG), S5_SEG)
                first = j == 0
                pr_s = jnp.where(first, s0r[:, cs], xr[prev, cs])
                pi_s = jnp.where(first, s0i[:, cs], xi[prev, cs])
                cr = cr + nr * pr_s + ni * pi_s
                ci = ci - nr * pi_s + ni * pr_s
            return nr, ni, cr, ci

        z = jnp.zeros((S5_SEG, S5_LC), f32)
        init = (gr[:, cs], gi[:, cs], z, z)
        fr, fi, cr, ci = lax.fori_loop(0, nsteps, step, init, unroll=S5_UNROLL)
        gr[:, cs] = fr
        gi[:, cs] = fi
        if acc_r is not None:
            acc_r[:, cs] += cr
            acc_i[:, cs] += ci


def _s5_seg_carry(fr, fi, ar, ai, seg_len, reverse):
    pr, pi = _cpow(ar, ai if not reverse else -ai, seg_len)
    rows = lax.broadcasted_iota(jnp.int32, fr.shape, 0)
    cr, ci = jnp.zeros_like(fr), jnp.zeros_like(fi)
    sh = (S5_SEG - 1) if reverse else 1
    fr_s, fi_s = pltpu.roll(fr, sh, 0), pltpu.roll(fi, sh, 0)
    order = range(S5_SEG - 2, -1, -1) if reverse else range(1, S5_SEG)
    for r in order:
        c_r, c_i = pltpu.roll(cr, sh, 0), pltpu.roll(ci, sh, 0)
        m_r, m_i = _cmul(pr, pi, c_r, c_i)
        cr = jnp.where(rows == r, m_r + fr_s, cr)
        ci = jnp.where(rows == r, m_i + fi_s, ci)
    return cr, ci


def _gelu_parts(y):
    c0 = math.sqrt(2.0 / math.pi)
    t = jnp.tanh(c0 * (y + 0.044715 * y * y * y))
    z = 0.5 * y * (1.0 + t)
    dz = 0.5 * (1.0 + t) + 0.5 * y * (1.0 - t * t) * c0 * (1.0 + 3.0 * 0.044715 * y * y)
    return z, dz


def _s5_y(xr, xi, u_ref, cre_ref, cim_ref, d_ref):
    ys = []
    for k in range(S5_UB):
        cs = slice(k * S5_LC, (k + 1) * S5_LC)
        ys.append(_bdot(xr[:, cs], cre_ref[k]) - _bdot(xi[:, cs], cim_ref[k]))
    return jnp.concatenate(ys, axis=1) + d_ref[...] * u_ref[...]


def _s5_specs(T, rb, rev=False):
    nblk = T // rb
    blk = (lambda i: (nblk - 1 - i, 0)) if rev else (lambda i: (i, 0))
    tok = pl.BlockSpec((rb, 4 * LANES), blk)
    bmat = pl.BlockSpec((S5_UB, LANES, S5_LC), lambda i: (0, 0, 0))
    cmat = pl.BlockSpec((S5_UB, S5_LC, LANES), lambda i: (0, 0, 0))
    avec = pl.BlockSpec((1, S5_N), lambda i: (0, 0))
    seg = pl.BlockSpec((S5_SEG, S5_N), lambda i: (0, 0))
    cvec = pl.BlockSpec((1, 4 * LANES), lambda i: (0, 0))
    s0 = pl.BlockSpec((1, S5_SEG, S5_N), (lambda i: (nblk - 1 - i, 0, 0)) if rev else (lambda i: (i, 0, 0)))
    return dict(tok=tok, bmat=bmat, cmat=cmat, avec=avec, seg=seg, cvec=cvec, s0=s0, nblk=nblk)


def _s5_final(u, bre, bim, ar, ai, *, rb):
    T = u.shape[0]
    sp = _s5_specs(T, rb)

    def body(u_ref, bre_ref, bim_ref, ar_ref, ai_ref, fr_ref, fi_ref, xr, xi):
        @pl.when(pl.program_id(0) == 0)
        def _():
            fr_ref[...] = jnp.zeros_like(fr_ref)
            fi_ref[...] = jnp.zeros_like(fi_ref)

        _s5_bu(u_ref, bre_ref, bim_ref, xr, xi)
        _s5_scan(xr, xi, fr_ref, fi_ref, ar_ref, ai_ref, rb // S5_SEG, False)

    return pl.pallas_call(
        body, name="s5_final", grid=(sp["nblk"],),
        in_specs=[sp["tok"], sp["bmat"], sp["bmat"], sp["avec"], sp["avec"]], out_specs=[sp["seg"], sp["seg"]],
        out_shape=[jax.ShapeDtypeStruct((S5_SEG, S5_N), f32)] * 2,
        scratch_shapes=[pltpu.VMEM((rb, S5_N), f32)] * 2, compiler_params=_cp("arbitrary"),
    )(u, bre, bim, ar, ai)


def _s5_fwd(u, bre, bim, ar, ai, fr, fi, cre, cim, dsk, wg, bg, *, rb):
    T = u.shape[0]
    sp = _s5_specs(T, rb)
    seg_len = T // S5_SEG

    def body(u_ref, bre_ref, bim_ref, ar_ref, ai_ref, fr_ref, fi_ref, cre_ref, cim_ref, d_ref, wg_ref, bg_ref,
             o_ref, s0r_ref, s0i_ref, xr, xi, sr, si):
        @pl.when(pl.program_id(0) == 0)
        def _():
            i_r, i_i = _s5_seg_carry(fr_ref[...], fi_ref[...], ar_ref[...], ai_ref[...], seg_len, False)
            sr[...] = i_r
            si[...] = i_i

        s0r_ref[0] = sr[...]
        s0i_ref[0] = si[...]
        _s5_bu(u_ref, bre_ref, bim_ref, xr, xi)
        _s5_scan(xr, xi, sr, si, ar_ref, ai_ref, rb // S5_SEG, True)
        y = _s5_y(xr, xi, u_ref, cre_ref, cim_ref, d_ref)
        z, _ = _gelu_parts(y)
        v = _bdot(z, wg_ref[...]) + bg_ref[...]
        o_ref[...] = (z * jax.nn.sigmoid(v)).astype(bf16)

    wspec = pl.BlockSpec((4 * LANES, 4 * LANES), lambda i: (0, 0))
    return pl.pallas_call(
        body, name="s5_fwd", grid=(sp["nblk"],),
        in_specs=[sp["tok"], sp["bmat"], sp["bmat"], sp["avec"], sp["avec"], sp["seg"], sp["seg"], sp["cmat"], sp["cmat"],
                  sp["cvec"], wspec, sp["cvec"]],
        out_specs=[sp["tok"], sp["s0"], sp["s0"]],
        out_shape=[jax.ShapeDtypeStruct((T, 4 * LANES), bf16)] + [jax.ShapeDtypeStruct((sp["nblk"], S5_SEG, S5_N), f32)] * 2,
        scratch_shapes=[pltpu.VMEM((rb, S5_N), f32)] * 2 + [pltpu.VMEM((S5_SEG, S5_N), f32)] * 2,
        compiler_params=_cp("arbitrary"),
    )(u, bre, bim, ar, ai, fr, fi, cre, cim, dsk, wg, bg)


def _s5_bwd_a(u, bre, bim, ar, ai, s0r, s0i, cre, cim, cret, cimt, dsk, wg, bg, dout, *, rb):
    T = u.shape[0]
    sp = _s5_specs(T, rb, rev=True)

    def body(u_ref, bre_ref, bim_ref, ar_ref, ai_ref, s0r_ref, s0i_ref, cre_ref, cim_ref, cret_ref, cimt_ref,
             d_ref, wg_ref, bg_ref, do_ref, dy_ref, glr_ref, gli_ref, dcre_ref, dcim_ref, dd_ref, dwg_ref, dbg_ref,
             xr, xi, dr, di, sr, si):
        @pl.when(pl.program_id(0) == 0)
        def _():
            for r in (glr_ref, gli_ref, dcre_ref, dcim_ref, dd_ref, dwg_ref, dbg_ref):
                r[...] = jnp.zeros_like(r)

        sr[...] = s0r_ref[0]
        si[...] = s0i_ref[0]
        _s5_bu(u_ref, bre_ref, bim_ref, xr, xi)
        _s5_scan(xr, xi, sr, si, ar_ref, ai_ref, rb // S5_SEG, True)
        uv = u_ref[...]
        y = _s5_y(xr, xi, u_ref, cre_ref, cim_ref, d_ref)
        z, gz = _gelu_parts(y)
        v = _bdot(z, wg_ref[...]) + bg_ref[...]
        sg = jax.nn.sigmoid(v)
        dov = do_ref[...].astype(f32)
        dv = dov * z * sg * (1.0 - sg)
        dz = dov * sg + _bdot(dv, wg_ref[...], NT)
        dy = dz * gz
        dy_ref[...] = dy
        dwg_ref[...] += _bdot(z, dv, TN)
        dbg_ref[...] += jnp.sum(dv, axis=0, keepdims=True)
        dd_ref[...] += jnp.sum(dy * uv, axis=0, keepdims=True)
        for k in range(S5_UB):
            cs = slice(k * S5_LC, (k + 1) * S5_LC)
            dyk = dy[:, k * LANES:(k + 1) * LANES]
            dcre_ref[k] += _bdot(xr[:, cs], dyk, TN)
            dcim_ref[k] -= _bdot(xi[:, cs], dyk, TN)
            dr[:, cs] = _bdot(dyk, cret_ref[k])
            di[:, cs] = -_bdot(dyk, cimt_ref[k])
        _s5_rscan(dr, di, None, None, None, None, glr_ref, gli_ref, None, None, ar_ref, ai_ref, rb // S5_SEG)

    wspec = pl.BlockSpec((4 * LANES, 4 * LANES), lambda i: (0, 0))
    return pl.pallas_call(
        body, name="s5_bwd_a", grid=(sp["nblk"],),
        in_specs=[sp["tok"], sp["bmat"], sp["bmat"], sp["avec"], sp["avec"], sp["s0"], sp["s0"], sp["cmat"], sp["cmat"],
                  sp["bmat"], sp["bmat"], sp["cvec"], wspec, sp["cvec"], sp["tok"]],
        out_specs=[sp["tok"], sp["seg"], sp["seg"], sp["cmat"], sp["cmat"], sp["cvec"], wspec, sp["cvec"]],
        out_shape=[jax.ShapeDtypeStruct((T, 4 * LANES), f32)] + [jax.ShapeDtypeStruct((S5_SEG, S5_N), f32)] * 2
        + [jax.ShapeDtypeStruct((S5_UB, S5_LC, LANES), f32)] * 2
        + [jax.ShapeDtypeStruct((1, 4 * LANES), f32), jax.ShapeDtypeStruct((4 * LANES, 4 * LANES), f32),
           jax.ShapeDtypeStruct((1, 4 * LANES), f32)],
        scratch_shapes=[pltpu.VMEM((rb, S5_N), f32)] * 4 + [pltpu.VMEM((S5_SEG, S5_N), f32)] * 2,
        compiler_params=_cp("arbitrary"),
    )(u, bre, bim, ar, ai, s0r, s0i, cre, cim, cret, cimt, dsk, wg, bg, dout)


def _s5_bwd_b(u, bre, bim, bret, bimt, ar, ai, s0r, s0i, glr, gli, cret, cimt, dsk, dy, *, rb):
    T = u.shape[0]
    sp = _s5_specs(T, rb, rev=True)
    seg_len = T // S5_SEG
    nblk = sp["nblk"]

    def body(u_ref, bre_ref, bim_ref, bret_ref, bimt_ref, ar_ref, ai_ref, s0r_ref, s0i_ref, glr_ref, gli_ref,
             cret_ref, cimt_ref, d_ref, dy_ref, du_ref, dbre_ref, dbim_ref, dar_ref, dai_ref,
             xr, xi, dr, di, sr, si, gr, gi, acc_r, acc_i):
        @pl.when(pl.program_id(0) == 0)
        def _():
            x_r, x_i = _s5_seg_carry(glr_ref[...], gli_ref[...], ar_ref[...], ai_ref[...], seg_len, True)
            gr[...] = x_r
            gi[...] = x_i
            acc_r[...] = jnp.zeros_like(acc_r)
            acc_i[...] = jnp.zeros_like(acc_i)
            dbre_ref[...] = jnp.zeros_like(dbre_ref)
            dbim_ref[...] = jnp.zeros_like(dbim_ref)

        sr[...] = s0r_ref[0]
        si[...] = s0i_ref[0]
        _s5_bu(u_ref, bre_ref, bim_ref, xr, xi)
        _s5_scan(xr, xi, sr, si, ar_ref, ai_ref, rb // S5_SEG, True)
        dy = dy_ref[...]
        for k in range(S5_UB):
            cs = slice(k * S5_LC, (k + 1) * S5_LC)
            dyk = dy[:, k * LANES:(k + 1) * LANES]
            dr[:, cs] = _bdot(dyk, cret_ref[k])
            di[:, cs] = -_bdot(dyk, cimt_ref[k])
        sr[...] = s0r_ref[0]
        si[...] = s0i_ref[0]
        _s5_rscan(dr, di, xr, xi, sr, si, gr, gi, acc_r, acc_i, ar_ref, ai_ref, rb // S5_SEG)
        dus = []
        for k in range(S5_UB):
            cs = slice(k * S5_LC, (k + 1) * S5_LC)
            uk = u_ref[:, k * LANES:(k + 1) * LANES]
            dbre_ref[k] += _bdot(uk, dr[:, cs], TN)
            dbim_ref[k] += _bdot(uk, di[:, cs], TN)
            dus.append(_bdot(dr[:, cs], bret_ref[k]) + _bdot(di[:, cs], bimt_ref[k]))
        du_ref[...] = (jnp.concatenate(dus, axis=1) + d_ref[...] * dy).astype(bf16)

        @pl.when(pl.program_id(0) == nblk - 1)
        def _():
            dar_ref[...] = jnp.sum(acc_r[...], axis=0, keepdims=True)
            dai_ref[...] = jnp.sum(acc_i[...], axis=0, keepdims=True)

    return pl.pallas_call(
        body, name="s5_bwd_b", grid=(nblk,),
        in_specs=[sp["tok"], sp["bmat"], sp["bmat"], sp["cmat"], sp["cmat"], sp["avec"], sp["avec"], sp["s0"], sp["s0"],
                  sp["seg"], sp["seg"], sp["bmat"], sp["bmat"], sp["cvec"], sp["tok"]],
        out_specs=[sp["tok"], sp["bmat"], sp["bmat"], sp["avec"], sp["avec"]],
        out_shape=[jax.ShapeDtypeStruct((T, 4 * LANES), bf16)] + [jax.ShapeDtypeStruct((S5_UB, LANES, S5_LC), f32)] * 2
        + [jax.ShapeDtypeStruct((1, S5_N), f32)] * 2,
        scratch_shapes=[pltpu.VMEM((rb, S5_N), f32)] * 4 + [pltpu.VMEM((S5_SEG, S5_N), f32)] * 6,
        compiler_params=_cp("arbitrary"),
    )(u, bre, bim, bret, bimt, ar, ai, s0r, s0i, glr, gli, cret, cimt, dsk, dy)


def _blockdiag(w, transpose=False):
    if transpose:
        w = jnp.swapaxes(w, 1, 2)
    g, a, b = w.shape
    eye = jnp.eye(8, dtype=w.dtype)
    return jnp.einsum("kgab,gj->kgajb", w.reshape(4, 8, a, b), eye).reshape(4, 8 * a, 8 * b)


def _blockdiag_t(m, a, b):
    eye = jnp.eye(8, dtype=m.dtype)
    return jnp.einsum("kgajb,gj->kgab", m.reshape(4, 8, a, 8, b), eye).reshape(32, a, b)


ROT = MLA_ROPE // 2


def _rope_tables(positions):
    freqs = ROPE_THETA ** (-jnp.arange(0, MLA_ROPE, 2, dtype=f32) / MLA_ROPE)
    ang = positions.astype(f32)[:, None] * freqs
    cos, sin, z = jnp.cos(ang), jnp.sin(ang), jnp.zeros_like(ang)
    return (jnp.concatenate([cos, cos, z, z], axis=1), jnp.concatenate([-sin, z, z, z], axis=1),
            jnp.concatenate([z, sin, z, z], axis=1))


def _rot(x, c, sa, sb):
    return x * c + pltpu.roll(x, LANES - ROT, 1) * sa + pltpu.roll(x, ROT, 1) * sb


def _rot_t(dy, c, sa, sb):
    return dy * c + pltpu.roll(dy * sa, ROT, 1) + pltpu.roll(dy * sb, LANES - ROT, 1)


def _rms(xv, g):
    return xv * lax.rsqrt(jnp.mean(xv * xv, axis=-1, keepdims=True) + EPS) * g


QW, KVW = MLA_Q_RANK, MLA_KV_RANK
ODD_PAD = QW + KVW + LANES


def _mla_prep_fwd(proj, qg, kvg, tabs, *, tm=512):
    T = proj.shape[0]
    tm = _tile(T, tm)

    def body(p_ref, qg_ref, kvg_ref, c_ref, sa_ref, sb_ref, cq_ref, ckv_ref, kr_ref):
        cq_ref[...] = _rms(p_ref[:, :QW], qg_ref[...]).astype(bf16)
        ckv_ref[...] = _rms(p_ref[:, QW:QW + KVW], kvg_ref[...]).astype(bf16)
        kr_ref[...] = _rot(p_ref[:, QW + KVW:], c_ref[...], sa_ref[...], sb_ref[...]).astype(bf16)

    row = lambda w: pl.BlockSpec((tm, w), lambda i: (i, 0))
    vec = lambda w: pl.BlockSpec((1, w), lambda i: (0, 0))
    return pl.pallas_call(
        body, name="mla_prep_fwd", grid=(T // tm,),
        in_specs=[row(ODD_PAD), vec(QW), vec(KVW), row(LANES), row(LANES), row(LANES)],
        out_specs=[row(QW), row(KVW), row(LANES)],
        out_shape=[jax.ShapeDtypeStruct((T, QW), bf16), jax.ShapeDtypeStruct((T, KVW), bf16),
                   jax.ShapeDtypeStruct((T, LANES), bf16)],
        compiler_params=_cp("parallel"),
    )(proj, qg, kvg, *tabs)


def _mla_prep_bwd(proj, qg, kvg, tabs, dcqn, dckvn, dkr_heads, *, tm=512):
    T = proj.shape[0]
    tm = _tile(T, tm)

    def body(p_ref, qg_ref, kvg_ref, c_ref, sa_ref, sb_ref, dcq_ref, dckv_ref, dkr_ref, dp_ref, dqg_ref, dkvg_ref):
        dcq, dqg = _rms_bwd_math(p_ref[:, :QW], qg_ref[...], dcq_ref[...])
        dckv, dkvg = _rms_bwd_math(p_ref[:, QW:QW + KVW], kvg_ref[...], dckv_ref[...])
        dk = dkr_ref[:, :LANES]
        for h in range(1, MLA_HEADS):
            dk = dk + dkr_ref[:, h * LANES:(h + 1) * LANES]
        dkr = _rot_t(dk, c_ref[...], sa_ref[...], sb_ref[...])
        dp_ref[...] = jnp.concatenate([dcq, dckv, dkr], axis=1).astype(bf16)

        @pl.when(pl.program_id(0) == 0)
        def _():
            dqg_ref[...] = dqg
            dkvg_ref[...] = dkvg

        @pl.when(pl.program_id(0) > 0)
        def _():
            dqg_ref[...] += dqg
            dkvg_ref[...] += dkvg

    row = lambda w: pl.BlockSpec((tm, w), lambda i: (i, 0))
    vec = lambda w: pl.BlockSpec((1, w), lambda i: (0, 0))
    return pl.pallas_call(
        body, name="mla_prep_bwd", grid=(T // tm,),
        in_specs=[row(ODD_PAD), vec(QW), vec(KVW), row(LANES), row(LANES), row(LANES), row(QW), row(KVW),
                  row(MLA_HEADS * LANES)],
        out_specs=[row(ODD_PAD), vec(QW), vec(KVW)],
        out_shape=[jax.ShapeDtypeStruct((T, ODD_PAD), bf16), jax.ShapeDtypeStruct((1, QW), f32),
                   jax.ShapeDtypeStruct((1, KVW), f32)],
        compiler_params=_cp("arbitrary"),
    )(proj, qg, kvg, *tabs, dcqn, dckvn, dkr_heads)


HQ = 2 * LANES
QK_SCALE = MLA_QK ** -0.5


def _q_post(q, tabs, *, transpose, name, tm=512):
    T = q.shape[0]
    tm = _tile(T, tm)

    def body(q_ref, c_ref, sa_ref, sb_ref, o_ref):
        c, sa, sb = c_ref[...], sa_ref[...], sb_ref[...]
        for h in range(MLA_HEADS):
            nope, rope = pl.ds(h * HQ, LANES), pl.ds(h * HQ + LANES, LANES)
            o_ref[:, nope] = (q_ref[:, nope].astype(f32) * QK_SCALE).astype(bf16)
            o_ref[:, rope] = ((_rot_t if transpose else _rot)(q_ref[:, rope].astype(f32), c, sa, sb) * QK_SCALE).astype(bf16)

    tab = pl.BlockSpec((tm, LANES), lambda i: (i, 0))
    blk = pl.BlockSpec((tm, MLA_HEADS * HQ), lambda i: (i, 0))
    return pl.pallas_call(
        body, name=name, grid=(T // tm,), in_specs=[blk, tab, tab, tab], out_specs=blk,
        out_shape=jax.ShapeDtypeStruct(q.shape, bf16), compiler_params=_cp("parallel"),
    )(q, *tabs)


def _causal_mask(i, j, tq, tk):
    r = lax.broadcasted_iota(jnp.int32, (tq, tk), 0) + i * tq
    c = lax.broadcasted_iota(jnp.int32, (tq, tk), 1) + j * tk
    return c <= r


FLASH_PARTS = 4


def _flash_fwd(q, kv, kr, *, tq=1024, tk=1024):
    T = q.shape[0]
    tq = _tile(T, tq)
    tk = _tile(tq, tk)
    per = tq // tk
    H = MLA_HEADS

    def body(q_ref, kn_ref, v_ref, kr_ref, o_ref, lse_ref, m_s, acc):
        i, j = pl.program_id(1), pl.program_id(2)
        last = (i + 1) * per - 1

        @pl.when(j == 0)
        def _():
            m_s[...] = jnp.full_like(m_s, -jnp.inf)
            acc[...] = jnp.zeros_like(acc)

        def step(masked):
            k = jnp.concatenate([kn_ref[...], kr_ref[...]], axis=1)
            v1 = jnp.concatenate([v_ref[...], jnp.ones((tk, LANES), bf16)], axis=1)
            mask = _causal_mask(i, j, tq, tk) if masked else None
            for part in range(FLASH_PARTS):
                rows = pl.ds(part * (tq // FLASH_PARTS), tq // FLASH_PARTS)
                s = _dot(q_ref[rows, :], k, NT)
                if masked:
                    s = jnp.where(mask[part * (tq // FLASH_PARTS):(part + 1) * (tq // FLASH_PARTS)], s, -jnp.inf)
                m_new = jnp.maximum(m_s[rows, :], jnp.max(s, axis=-1, keepdims=True))
                alpha = jnp.exp(m_s[rows, :] - m_new)
                p = jnp.exp((s - m_new).astype(bf16))
                acc[rows, :] = alpha * acc[rows, :] + _dot(p, v1)
                m_s[rows, :] = m_new

        pl.when(j < i * per)(functools.partial(step, False))
        pl.when((j >= i * per) & (j <= last))(functools.partial(step, True))

        @pl.when(j == last)
        def _():
            l = acc[:, LANES:]
            o_ref[...] = (acc[:, :LANES] / l).astype(bf16)
            lse_ref[0] = m_s[...] + jnp.log(jnp.max(l, axis=-1, keepdims=True))

    kj = lambda i, j: jnp.minimum(j, (i + 1) * per - 1)
    kblk = lambda off: pl.BlockSpec((tk, LANES), lambda h, i, j: (kj(i, j), 2 * h + off))
    return pl.pallas_call(
        body, name="flash_fwd", grid=(H, T // tq, T // tk),
        in_specs=[pl.BlockSpec((tq, HQ), lambda h, i, j: (i, h)), kblk(0), kblk(1),
                  pl.BlockSpec((tk, LANES), lambda h, i, j: (kj(i, j), 0))],
        out_specs=[pl.BlockSpec((tq, LANES), lambda h, i, j: (i, h)), pl.BlockSpec((1, tq, 1), lambda h, i, j: (h, i, 0))],
        out_shape=[jax.ShapeDtypeStruct((T, H * LANES), bf16), jax.ShapeDtypeStruct((H, T, 1), f32)],
        scratch_shapes=[pltpu.VMEM((tq, 1), f32), pltpu.VMEM((tq, 2 * LANES), f32)],
        compiler_params=_cp("parallel", "parallel", "arbitrary"),
    )(q, kv, kv, kr)


def _flash_bwd(q, kv, kr, o, do, lse, *, tb=1024):
    T = q.shape[0]
    tb = _tile(T, tb)
    nb = T // tb
    H = MLA_HEADS

    def body(q_ref, kn_ref, v_ref, kr_ref, o_ref, do_ref, lse_ref, dkv_ref, dkr_ref, dq_ref, dk_acc, dv_acc):
        j, ii = pl.program_id(1), pl.program_id(2)
        i = jnp.maximum(ii, j)

        @pl.when((j == 0) & (ii == 0))
        def _():
            dq_ref[...] = jnp.zeros_like(dq_ref)

        @pl.when(ii == 0)
        def _():
            dk_acc[...] = jnp.zeros_like(dk_acc)
            dv_acc[...] = jnp.zeros_like(dv_acc)

        def step(masked):
            k = jnp.concatenate([kn_ref[...], kr_ref[...]], axis=1)
            p = jnp.exp((_dot(q_ref[...], k, NT) - lse_ref[0]).astype(bf16))
            if masked:
                p = jnp.where(_causal_mask(i, j, tb, tb), p, jnp.zeros_like(p))
            delta = jnp.sum(o_ref[...].astype(f32) * do_ref[...], axis=-1, keepdims=True)
            ds = p * (_bdot(do_ref[...], v_ref[...], NT) - delta).astype(bf16)
            dv_acc[...] += _bdot(p, do_ref[...], TN)
            dk_acc[...] += _bdot(ds, q_ref[...], TN)
            dq_ref[pl.ds(pl.multiple_of(i * tb, tb), tb), :] += _bdot(ds, k)

        pl.when(ii > j)(functools.partial(step, False))
        pl.when(ii == j)(functools.partial(step, True))

        @pl.when(ii == nb - 1)
        def _():
            dkv_ref[...] = jnp.concatenate([dk_acc[:, :LANES], dv_acc[...]], axis=1).astype(bf16)
            dkr_ref[...] = dk_acc[:, LANES:]

    qi = lambda h, j, i: jnp.maximum(i, j)
    kblk = lambda off: pl.BlockSpec((tb, LANES), lambda h, j, i: (j, 2 * h + off))
    vec = pl.BlockSpec((1, tb, 1), lambda h, j, i: (h, qi(h, j, i), 0))
    qblk = pl.BlockSpec((tb, LANES), lambda h, j, i: (qi(h, j, i), h))
    return pl.pallas_call(
        body, name="flash_bwd", grid=(H, nb, nb),
        in_specs=[pl.BlockSpec((tb, HQ), lambda h, j, i: (qi(h, j, i), h)), kblk(0), kblk(1),
                  pl.BlockSpec((tb, LANES), lambda h, j, i: (j, 0)), qblk, qblk, vec],
        out_specs=[pl.BlockSpec((tb, HQ), lambda h, j, i: (j, h)), pl.BlockSpec((tb, LANES), lambda h, j, i: (j, h)),
                   pl.BlockSpec((T, HQ), lambda h, j, i: (0, h))],
        out_shape=[jax.ShapeDtypeStruct((T, H * HQ), bf16), jax.ShapeDtypeStruct((T, H * LANES), f32),
                   jax.ShapeDtypeStruct((T, H * HQ), f32)],
        scratch_shapes=[pltpu.VMEM((tb, HQ), f32), pltpu.VMEM((tb, LANES), f32)],
        compiler_params=_cp("parallel", "arbitrary", "arbitrary"),
    )(q, kv, kv, kr, o, do, lse)


HBM_SPEC = pl.BlockSpec(memory_space=pltpu.HBM)
N_CHIPS = 4
N_DEV = 8

BIG = {"even_w_in": 1, "s5_w_glu": 0, "even_w_out": 0, "odd_w_in": 0, "mla_w_uq": 1, "mla_w_ukv": 1, "odd_w_out": 0,
       "ffn_w_in": 2, "ffn_w_out": 1}
LAYERED = ("ffn_w_in", "ffn_w_out")
GROUPS = {"even_in": ("even_w_in",), "even_rest": ("s5_w_glu", "even_w_out"), "ffn0": LAYERED,
          "odd": ("odd_w_in", "mla_w_uq", "mla_w_ukv", "odd_w_out"), "ffn1": LAYERED}
GROUP_LAYER = {"ffn0": 0, "ffn1": 1}


def _place():
    x, y, c = lax.axis_index("x"), lax.axis_index("y"), lax.axis_index("c")
    chips = [(1 - x, y), (x, 1 - y), (1 - x, 1 - y)]
    return x, y, c, chips


def _slab(ref, axis, k, size):
    start = pl.multiple_of(k * size, size if axis == 0 else LANES)
    idx = [slice(None)] * len(ref.shape)
    idx[axis] = pl.ds(start, size)
    return ref.at[tuple(idx)]


SEM_SPEC = pl.BlockSpec(memory_space=pltpu.SEMAPHORE)
ANY_SPEC = pl.BlockSpec(memory_space=pl.ANY)
EFFECT = pltpu.SideEffectType.DATAFLOW_SIDE_EFFECTING


def _hbm(a):
    return pltpu.with_memory_space_constraint(a, pltpu.HBM)


class _Gather:
    copies = 3

    def __init__(self, axis, size):
        self.axis, self.size = axis, size

    def view(self, land, kk):
        return _slab(land, self.axis, kk, self.size)

    def own(self, land, place):
        return self.view(land, 2 * place[0] + place[1])

    def sends(self, src, land, place):
        x, y, c, chips = place
        return [(self.own(land, place) if src is None else src, self.own(land, place), (*chip, c)) for chip in chips]

    def recvs(self, land, place):
        return [self.view(land, 2 * cx + cy) for cx, cy in place[3]]


class _Scatter:
    copies = 3

    def __init__(self, axis, size, layer=None):
        self.axis, self.size, self.layer = axis, size, layer

    def row(self, land, j):
        return land.at[j] if self.layer is None else land.at[j, self.layer]

    def sends(self, src, land, place):
        c, chips = place[2], place[3]
        return [(_slab(src, self.axis, 2 * cx + cy, self.size), self.row(land, j), (cx, cy, c))
                for j, (cx, cy) in enumerate(chips)]

    def recvs(self, land, place):
        return [self.row(land, j) for j in range(3)]


class _ToAll:
    copies = N_DEV - 1

    def __init__(self, size):
        self.size = size

    def sends(self, src, land, place):
        x, y, c, _ = place
        flip = lambda v, bit: 1 - v if bit else v
        own = _slab(land, 0, 4 * x + 2 * y + c, self.size)
        return [(own, own, (flip(x, m & 4), flip(y, m & 2), flip(c, m & 1))) for m in range(1, N_DEV)]

    def recvs(self, land, place):
        x, y, c, _ = place
        d = 4 * x + 2 * y + c
        return [_slab(land, 0, d ^ m, self.size) for m in range(1, N_DEV)]


def _unique(arrays):
    out, index = [], {}
    for a in arrays:
        if a is not None and id(a) not in index:
            index[id(a)] = len(out)
            out.append(a)
    return out, index


def _sem_base(routes):
    base = [0]
    for r in routes:
        base.append(base[-1] + r.copies)
    return base


def _push_start(name, items):
    n = len(items)
    base = _sem_base([it[0] for it in items])
    arrays, index = _unique([it[1] for it in items] + [it[2] for it in items])
    na = len(arrays)

    def body(*refs):
        arr, send, recv, token = refs[:na], refs[na], refs[na + 1], refs[-1]
        place = _place()
        for i, (route, src, land) in enumerate(items):
            s_ref = None if src is None else arr[index[id(src)]]
            for j, (s, d, dev) in enumerate(route.sends(s_ref, arr[index[id(land)]], place)):
                pltpu.make_async_remote_copy(src_ref=s, dst_ref=d, send_sem=send.at[base[i] + j], recv_sem=recv.at[base[i] + j],
                                             device_id=dev, device_id_type=MESH).start()
        token[...] = jnp.zeros_like(token)

    res = pl.pallas_call(
        body, name=name,
        out_shape=[pltpu.SemaphoreType.DMA((base[-1],)), pltpu.SemaphoreType.DMA((base[-1],))]
        + [pltpu.HBM(a.shape, a.dtype) for a in arrays] + [jax.ShapeDtypeStruct((SUBLANES, LANES), f32)],
        in_specs=[HBM_SPEC] * na, out_specs=[SEM_SPEC, SEM_SPEC] + [HBM_SPEC] * na + [pl.BlockSpec(memory_space=pltpu.VMEM)],
        input_output_aliases={i: 2 + i for i in range(na)},
        compiler_params=pltpu.CompilerParams(has_side_effects=EFFECT),
    )(*[_hbm(a) for a in arrays])
    thru = lambda a: None if a is None else res[2 + index[id(a)]]
    return (res[0], res[1]), [thru(it[1]) for it in items], [thru(it[2]) for it in items], res[-1]


def _push_wait(name, groups, after):
    arrays, index = _unique([a for _, _, srcs, lands in groups for a in list(srcs) + list(lands)])
    na, ng = len(arrays), len(groups)

    def body(*refs):
        arr, sems = refs[:na], refs[na:na + 2 * ng]
        place = _place()
        for g, (routes, _, srcs, lands) in enumerate(groups):
            send, recv = sems[2 * g], sems[2 * g + 1]
            base = _sem_base(routes)
            for i, route in enumerate(routes):
                src, land = None if srcs[i] is None else arr[index[id(srcs[i])]], arr[index[id(lands[i])]]
                for j, ((s, d, dev), mine) in enumerate(zip(route.sends(src, land, place), route.recvs(land, place))):
                    cp = pltpu.make_async_remote_copy(src_ref=s, dst_ref=mine, send_sem=send.at[base[i] + j],
                                                      recv_sem=recv.at[base[i] + j], device_id=dev,
                                                      device_id_type=MESH)
                    cp.wait_send()
                    cp.wait_recv()

    sem_args = [s for g in groups for s in g[1]]
    res = pl.pallas_call(
        body, name=name, out_shape=[pltpu.HBM(a.shape, a.dtype) for a in arrays],
        in_specs=[HBM_SPEC] * na + [SEM_SPEC] * (2 * ng) + [ANY_SPEC] * len(after), out_specs=[HBM_SPEC] * na,
        input_output_aliases={i: i for i in range(na)},
        compiler_params=pltpu.CompilerParams(has_side_effects=EFFECT),
    )(*arrays, *sem_args, *after)
    return [[res[index[id(a)]] for a in g[3]] for g in groups]


def _place_slab(block, axis, slabs, idx, dtype, *, name):
    R, C = block.shape
    tm = _rows(R, C)
    nr = R // tm
    out_map = (lambda i, k: (i, k[0])) if axis == 1 else (lambda i, k: (k[0] * nr + i, 0))

    def body(k_ref, x_ref, o_ref):
        o_ref[...] = x_ref[...].astype(dtype)

    full = (R, C * slabs) if axis == 1 else (R * slabs, C)
    return pl.pallas_call(
        body, name=name, out_shape=jax.ShapeDtypeStruct(full, dtype),
        grid_spec=pltpu.PrefetchScalarGridSpec(
            num_scalar_prefetch=1, grid=(nr,), in_specs=[pl.BlockSpec((tm, C), lambda i, k: (i, 0))],
            out_specs=pl.BlockSpec((tm, C), out_map)),
        compiler_params=_cp("parallel"),
    )(idx, block)


def _swap_with_sibling(parts, tag):
    names = list(parts)

    def body(*refs):
        n = len(names)
        ins, outs, send, recv = refs[:n], refs[n:2 * n], refs[-2], refs[-1]
        x, y, c, _ = _place()
        cps = [pltpu.make_async_remote_copy(src_ref=ins[a], dst_ref=outs[a], send_sem=send.at[a], recv_sem=recv.at[a],
                                            device_id=(x, y, 1 - c), device_id_type=MESH) for a in range(n)]
        for cp in cps:
            cp.start()
        for cp in cps:
            cp.wait_recv()
        for cp in cps:
            cp.wait_send()

    res = pl.pallas_call(
        body, name=f"swap_with_sibling_{tag}", in_specs=[HBM_SPEC] * len(names), out_specs=[HBM_SPEC] * len(names),
        out_shape=[jax.ShapeDtypeStruct(parts[n].shape, parts[n].dtype) for n in names],
        scratch_shapes=[pltpu.SemaphoreType.DMA((len(names),)), pltpu.SemaphoreType.DMA((len(names),))],
    )(*[parts[n] for n in names])
    return dict(zip(names, res))


ELEMENTWISE_BLOCK_BYTES = 1 << 20


def _rows(r, c):
    for t in (512, 256, 128, 64, 32, 16, 8):
        if r % t == 0 and t * c * 4 <= ELEMENTWISE_BLOCK_BYTES:
            return t
    return r


def _sum4(owns, axis, recv, kidx, *, name):
    L = len(owns)
    R, C = recv.shape[2:]
    tm = _rows(R, C)
    nr = R // tm

    def body(k_ref, *refs):
        own_refs, r_ref, out_ref = refs[:L], refs[L], refs[L + 1]
        for li in range(L):
            @pl.when(pl.program_id(0) == li)
            def _(o_ref=own_refs[li]):
                out_ref[...] = ((o_ref[...] + r_ref[0, 0].astype(f32)) + r_ref[1, 0].astype(f32)) + r_ref[2, 0].astype(f32)

    own_map = (lambda l, i, k: (i, k[0])) if axis == 1 else (lambda l, i, k: (k[0] * nr + i, 0))
    return pl.pallas_call(
        body, name=name, out_shape=jax.ShapeDtypeStruct((L * R, C), f32),
        grid_spec=pltpu.PrefetchScalarGridSpec(
            num_scalar_prefetch=1, grid=(L, nr),
            in_specs=[pl.BlockSpec((tm, C), own_map)] * L + [pl.BlockSpec((3, 1, tm, C), lambda l, i, k: (0, l, i, 0))],
            out_specs=pl.BlockSpec((tm, C), lambda l, i, k: (l * nr + i, 0))),
        compiler_params=_cp("parallel", "parallel"),
    )(kidx, *owns, recv)


def _adamw(w, m, v, parts, *, name):
    R, C = w.shape
    tm = _rows(R, C)
    npart = len(parts)

    def body(*refs):
        w_ref, m_ref, v_ref = refs[:3]
        g_ref, d_ref, m2_ref, v2_ref = refs[3 + npart:]
        g = refs[3][...]
        for p_ref in refs[4:3 + npart]:
            g = g + p_ref[...]
        g_ref[...] = g
        d_ref[...], m2_ref[...], v2_ref[...] = _adam_math(w_ref[...], m_ref[...], v_ref[...], g)

    blk = pl.BlockSpec((tm, C), lambda i: (i, 0))
    return pl.pallas_call(
        body, name=name, grid=(R // tm,),
        in_specs=[blk] * (3 + npart), out_specs=[blk] * 4,
        out_shape=[jax.ShapeDtypeStruct((R, C), f32)] * 4, compiler_params=_cp("parallel"),
    )(w, m, v, *parts)


def _adam_math(w, m, v, g):
    m2 = ADAM_B1 * m + (1.0 - ADAM_B1) * g
    v2 = ADAM_B2 * v + (1.0 - ADAM_B2) * (g * g)
    m_hat = m2 / (1.0 - ADAM_B1 ** ADAM_STEP)
    v_hat = v2 / (1.0 - ADAM_B2 ** ADAM_STEP)
    return -ADAM_LR * (m_hat / (jnp.sqrt(v_hat) + ADAM_EPS) + ADAM_WD * w), m2, v2


def _adamw_small(landed, w, m, v, kidx, ra, rb):
    rs = ra + N_CHIPS * rb

    def body(k_ref, l_ref, w_ref, m_ref, v_ref, g_ref, d_ref, m2_ref, v2_ref):
        mine = pl.multiple_of(ra + k_ref[0] * rb, SUBLANES)
        for lo, n, off in ((0, ra, 0), (ra, rb, mine)):
            g = l_ref[pl.ds(off, n), :]
            for d in range(1, N_DEV):
                g = g + l_ref[pl.ds(d * rs + off, n), :]
            rows = pl.ds(lo, n)
            delta, m2, v2 = _adam_math(w_ref[rows, :], m_ref[rows, :], v_ref[rows, :], g)
            g_ref[rows, :] = g
            d_ref[rows, :] = delta
            m2_ref[rows, :] = m2
            v2_ref[rows, :] = v2

    vmem = pl.BlockSpec(memory_space=pltpu.VMEM)
    return pl.pallas_call(
        body, name="adamw_small", out_shape=[jax.ShapeDtypeStruct(w.shape, f32)] * 4,
        grid_spec=pltpu.PrefetchScalarGridSpec(num_scalar_prefetch=1, grid=(), in_specs=[vmem] * 4, out_specs=[vmem] * 4),
        compiler_params=_cp(),
    )(kidx, landed, w, m, v)


def _pad_odd(w):
    return jnp.pad(w, ((0, 0), (0, ODD_PAD - w.shape[1])))


def _uq_cat(w):
    r = w.shape[0]
    return jnp.pad(w.reshape(r, MLA_HEADS, MLA_QK), ((0, 0), (0, 0), (0, HQ - MLA_QK))).reshape(r, MLA_HEADS * HQ)


def _uq_uncat(w):
    r = w.shape[0]
    return w.reshape(r, MLA_HEADS, HQ)[:, :, :MLA_QK].reshape(r, MLA_HEADS * MLA_QK)


def _to_segments(v):
    T, C = v.shape
    return v.reshape(S5_SEG, T // S5_SEG, C).transpose(1, 0, 2).reshape(T, C)


def _from_segments(v):
    T, C = v.shape
    return v.reshape(T // S5_SEG, S5_SEG, C).transpose(1, 0, 2).reshape(T, C)


def _s5_rb(T):
    return min(512, T)


def _ffn_fwd(h, hn, w_in, cw, cb, w_out, tag, next_g=None):
    au = _mm(hn, w_in, out_dtype=bf16, name=f"ffn{tag}_in", tn=1408)
    z = _ffn_mid_fwd(au, cw, cb, name=f"ffn{tag}_mid")
    return _mm(z, w_out, res=h, norm_g=next_g, name=f"ffn{tag}_out", tm=512, tk=D_FF), (hn, au, z)


def _ffn_bwd(h, g, w_in, cw, cb, w_out, saved, dh, tag, dep=None):
    hn, au, z = saved
    dz = _mm(dh, w_out, tb=True, out_dtype=bf16, name=f"ffn{tag}_dz", tn=1408, dep=dep)
    dw_out = _mm(z, dh, ta=True, also_bf16=True, name=f"ffn{tag}_dwout", tm=1408)
    dau, dcw, dcb = _ffn_mid_bwd(au, cw, cb, dz, name=f"ffn{tag}_dmid")
    dh_in, dg = _mm(dau, w_in, tb=True, res=dh, norm_bwd=(h, g), name=f"ffn{tag}_dhn", tk=1408)
    dw_in = _mm(hn, dau, ta=True, also_bf16=True, name=f"ffn{tag}_dwin", tn=1408)
    return dh_in, dg, dw_in, dcw, dcb, dw_out


def _local_step(x, positions, target, get_w, P, put_g):
    T = x.shape[0]
    rb = _s5_rb(T)
    row = lambda v: v.reshape(1, -1)
    g_mix, g_ffn = P["norm_mix_g"], P["norm_ffn_g"]
    lbl, hng = P["hgrn_lb_logits"], P["hgrn_norm_g"]
    dsk, bg = P["s5_d"], P["s5_b_glu"]
    qg, kvg = P["mla_q_norm_g"], P["mla_kv_norm_g"]
    cw, cb = P["ffn_conv_w"], P["ffn_conv_b"]

    col = lambda v: v.reshape(S5_N, 1)
    disc_in = (col(P["s5_a_re"]), col(P["s5_a_im"]), col(jnp.repeat(P["s5_log_dt"].reshape(S5_GROUPS), S5_STATE)),
               P["s5_b_re"].reshape(S5_N, S5_GROUP), P["s5_b_im"].reshape(S5_N, S5_GROUP))
    abr, abi, bbr, bbi = _s5_disc_fwd(*disc_in)
    ar, ai = abr.reshape(1, S5_N), abi.reshape(1, S5_N)
    bbr3, bbi3 = bbr.reshape(S5_GROUPS, S5_STATE, S5_GROUP), bbi.reshape(S5_GROUPS, S5_STATE, S5_GROUP)
    bre, bim = _blockdiag(bbr3, True).astype(bf16), _blockdiag(bbi3, True).astype(bf16)
    bret, bimt = _blockdiag(bbr3).astype(bf16), _blockdiag(bbi3).astype(bf16)
    c_re, c_im = P["s5_c_re"].reshape(S5_GROUPS, S5_GROUP, S5_STATE), P["s5_c_im"].reshape(S5_GROUPS, S5_GROUP, S5_STATE)
    cre, cim = _blockdiag(c_re, True).astype(bf16), _blockdiag(c_im, True).astype(bf16)
    cret, cimt = _blockdiag(c_re).astype(bf16), _blockdiag(c_im).astype(bf16)

    hn0 = _rms_fwd(x, g_mix[0:1], name="mix0_norm")
    We = get_w("even_in", hn0)
    proj_e = _mm(hn0, We["even_w_in"], name="even_in", tn=1280)
    Wr = get_w("even_rest", proj_e)
    ya, states = _hgrn_fwd(proj_e, lbl, hng)
    u_seg = _to_segments(proj_e[:, 4 * 512:])
    fr, fi = _s5_final(u_seg, bre, bim, ar, ai, rb=rb)
    yb_seg, s0r, s0i = _s5_fwd(u_seg, bre, bim, ar, ai, fr, fi, cre, cim, dsk, Wr["s5_w_glu"], bg, rb=rb)
    ycat = jnp.concatenate([ya, _from_segments(yb_seg)], axis=1)
    h1, hnf0 = _mm(ycat, Wr["even_w_out"], res=x, norm_g=g_ffn[0:1], name="even_out")
    Wf0 = get_w("ffn0", h1)
    (h2, hn2), ffn0 = _ffn_fwd(h1, hnf0, Wf0["ffn_w_in"], cw[0], cb[0:1], Wf0["ffn_w_out"], 0, next_g=g_mix[1:2])

    tabs = _rope_tables(positions)
    Wo = get_w("odd", hn2)
    proj_o = _mm(hn2, Wo["odd_w_in"], name="odd_in")
    cqn, ckvn, kr = _mla_prep_fwd(proj_o, qg, kvg, tabs)
    q = _q_post(_mm(cqn, Wo["mla_w_uq"], name="mla_uq"), tabs, transpose=False, name="q_post")
    kvb = _mm(ckvn, Wo["mla_w_ukv"], out_dtype=bf16, name="mla_ukv")
    o, lse = _flash_fwd(q, kvb, kr)
    h3, hnf1 = _mm(o, Wo["odd_w_out"], res=h2, norm_g=g_ffn[1:2], name="odd_out")
    Wf1 = get_w("ffn1", h3)
    h4, ffn1 = _ffn_fwd(h3, hnf1, Wf1["ffn_w_in"], cw[1], cb[1:2], Wf1["ffn_w_out"], 1)
    loss, dh4, dg_final = _loss_head(h4, row(P["final_norm_g"]), target)

    dh3, dg_ffn1, dw_fin1, dcw1, dcb1, dw_fout1 = _ffn_bwd(
        h3, g_ffn[1:2], Wf1["ffn_w_in"], cw[1], cb[1:2], Wf1["ffn_w_out"], ffn1, dh4, 1)
    sent = put_g("ffn1", {"ffn_w_in": dw_fin1, "ffn_w_out": dw_fout1})
    do = _mm(dh3, Wo["odd_w_out"], tb=True, out_dtype=bf16, name="odd_do", dep=sent)
    dw_oout = _mm(o, dh3, ta=True, also_bf16=True, name="odd_dwout")
    dkv, dkr_h, dq = _flash_bwd(q, kvb, kr, o, do, lse)
    dq = _q_post(dq, tabs, transpose=True, name="dq_post")
    dw_uq = _mm(cqn, dq, ta=True, also_bf16=True, name="mla_dwuq")
    dcqn = _mm(dq, Wo["mla_w_uq"], tb=True, name="mla_dcq")
    dw_ukv = _mm(ckvn, dkv, ta=True, also_bf16=True, name="mla_dwukv")
    dckvn = _mm(dkv, Wo["mla_w_ukv"], tb=True, name="mla_dckv")
    dproj_o, dqg, dkvg = _mla_prep_bwd(proj_o, qg, kvg, tabs, dcqn, dckvn, dkr_h)
    dw_oin = _mm(hn2, dproj_o, ta=True, also_bf16=True, name="odd_dwin")
    sent = put_g("odd", {"odd_w_in": dw_oin, "mla_w_uq": dw_uq, "mla_w_ukv": dw_ukv, "odd_w_out": dw_oout})
    dh2, dg_mix1 = _mm(dproj_o, Wo["odd_w_in"], tb=True, res=dh3, norm_bwd=(h2, g_mix[1:2]), name="odd_dhn")

    dh1, dg_ffn0, dw_fin0, dcw0, dcb0, dw_fout0 = _ffn_bwd(
        h1, g_ffn[0:1], Wf0["ffn_w_in"], cw[0], cb[0:1], Wf0["ffn_w_out"], ffn0, dh2, 0, dep=sent)
    sent = put_g("ffn0", {"ffn_w_in": dw_fin0, "ffn_w_out": dw_fout0})
    dycat = _mm(dh1, Wr["even_w_out"], tb=True, name="even_dy", dep=sent)
    dw_eout = _mm(ycat, dh1, ta=True, also_bf16=True, name="even_dwout")
    dq_h, df_h, di_h, dg_h, dlbl, dhng = _hgrn_bwd(proj_e, lbl, hng, states, dycat)
    dyb_seg = _to_segments(dycat[:, 512:])
    dy_s5, glr, gli, dcre, dcim, dd, dwg, dbg = _s5_bwd_a(
        u_seg, bre, bim, ar, ai, s0r, s0i, cre, cim, cret, cimt, dsk, Wr["s5_w_glu"], bg, dyb_seg, rb=rb)
    du_seg, dbre, dbim, dar, dai = _s5_bwd_b(
        u_seg, bre, bim, bret, bimt, ar, ai, s0r, s0i, glr, gli, cret, cimt, dsk, dy_s5, rb=rb)
    dproj_e = jnp.concatenate([dq_h, df_h, di_h, dg_h, _from_segments(du_seg)], axis=1)
    dx, dg_mix0 = _mm(dproj_e, We["even_w_in"], tb=True, res=dh1, norm_bwd=(x, g_mix[0:1]), name="even_dhn", tk=1280)
    dw_ein = _mm(hn0, dproj_e, ta=True, also_bf16=True, name="even_dwin", tn=1280)

    unblk = lambda m, a, b: jnp.swapaxes(_blockdiag_t(m, a, b), 1, 2)
    dbbr = unblk(dbre, S5_GROUP, S5_STATE).reshape(S5_N, S5_GROUP)
    dbbi = unblk(dbim, S5_GROUP, S5_STATE).reshape(S5_N, S5_GROUP)
    d_ar, d_ai, d_ldt, d_br, d_bi = _s5_disc_bwd(*disc_in, (dar.reshape(S5_N, 1), dai.reshape(S5_N, 1), dbbr, dbbi))
    small = {
        "norm_mix_g": jnp.concatenate([dg_mix0, dg_mix1], axis=0),
        "norm_ffn_g": jnp.concatenate([dg_ffn0, dg_ffn1], axis=0),
        "final_norm_g": dg_final.reshape(-1),
        "hgrn_lb_logits": dlbl, "hgrn_norm_g": dhng,
        "s5_a_re": d_ar.reshape(1, S5_GROUPS, S5_STATE), "s5_a_im": d_ai.reshape(1, S5_GROUPS, S5_STATE),
        "s5_log_dt": d_ldt.reshape(S5_GROUPS, S5_STATE).sum(axis=1).reshape(1, S5_GROUPS),
        "s5_b_re": d_br.reshape(1, S5_GROUPS, S5_STATE, S5_GROUP), "s5_b_im": d_bi.reshape(1, S5_GROUPS, S5_STATE, S5_GROUP),
        "s5_c_re": unblk(dcre, S5_STATE, S5_GROUP).reshape(1, S5_GROUPS, S5_GROUP, S5_STATE),
        "s5_c_im": unblk(dcim, S5_STATE, S5_GROUP).reshape(1, S5_GROUPS, S5_GROUP, S5_STATE),
        "s5_d": dd, "s5_b_glu": dbg, "mla_q_norm_g": dqg, "mla_kv_norm_g": dkvg,
        "ffn_conv_w": jnp.stack([dcw0, dcw1]), "ffn_conv_b": jnp.concatenate([dcb0, dcb1], axis=0),
    }
    put_g("even", {"even_w_in": dw_ein, "s5_w_glu": (dwg, dwg.astype(bf16)), "even_w_out": dw_eout}, small)
    return loss, dx


WEIGHTS = ["norm_mix_g", "norm_ffn_g", "final_norm_g", "even_w_in", "hgrn_lb_logits", "hgrn_norm_g", "s5_a_re", "s5_a_im",
           "s5_log_dt", "s5_b_re", "s5_b_im", "s5_c_re", "s5_c_im", "s5_d", "s5_w_glu", "s5_b_glu", "even_w_out", "odd_w_in",
           "mla_q_norm_g", "mla_w_uq", "mla_kv_norm_g", "mla_w_ukv", "odd_w_out", "ffn_w_in", "ffn_conv_w", "ffn_conv_b",
           "ffn_w_out"]
SMALL_SHARDED = {"mla_q_norm_g": 1, "mla_kv_norm_g": 1, "ffn_conv_w": 2}
SMALL = [n for n in WEIGHTS if n not in BIG]
SMALL_REP = [n for n in SMALL if n not in SMALL_SHARDED]


def _pack_rows(shapes):
    n = sum(math.prod(s) for s in shapes)
    return -(-n // (SUBLANES * LANES)) * SUBLANES


def _pack(arrays, rows):
    flat = jnp.concatenate([a.reshape(-1) for a in arrays])
    return jnp.pad(flat, (0, rows * LANES - flat.shape[0])).reshape(rows, LANES)


def _unpack(block, shapes):
    flat, out, off = block.reshape(-1), [], 0
    for s in shapes:
        n = math.prod(s)
        out.append(flat[off:off + n].reshape(s))
        off += n
    return out


def kernel(x, positions, norm_mix_g, norm_ffn_g, final_norm_g, even_w_in, hgrn_lb_logits, hgrn_norm_g, s5_a_re, s5_a_im, s5_log_dt, s5_b_re, s5_b_im, s5_c_re, s5_c_im, s5_d, s5_w_glu, s5_b_glu, even_w_out, odd_w_in, mla_q_norm_g, mla_w_uq, mla_kv_norm_g, mla_w_ukv, odd_w_out, ffn_w_in, ffn_conv_w, ffn_conv_b, ffn_w_out, loss_target, m_norm_mix_g, m_norm_ffn_g, m_final_norm_g, m_even_w_in, m_hgrn_lb_logits, m_hgrn_norm_g, m_s5_a_re, m_s5_a_im, m_s5_log_dt, m_s5_b_re, m_s5_b_im, m_s5_c_re, m_s5_c_im, m_s5_d, m_s5_w_glu, m_s5_b_glu, m_even_w_out, m_odd_w_in, m_mla_q_norm_g, m_mla_w_uq, m_mla_kv_norm_g, m_mla_w_ukv, m_odd_w_out, m_ffn_w_in, m_ffn_conv_w, m_ffn_conv_b, m_ffn_w_out, v_norm_mix_g, v_norm_ffn_g, v_final_norm_g, v_even_w_in, v_hgrn_lb_logits, v_hgrn_norm_g, v_s5_a_re, v_s5_a_im, v_s5_log_dt, v_s5_b_re, v_s5_b_im, v_s5_c_re, v_s5_c_im, v_s5_d, v_s5_w_glu, v_s5_b_glu, v_even_w_out, v_odd_w_in, v_mla_q_norm_g, v_mla_w_uq, v_mla_kv_norm_g, v_mla_w_ukv, v_odd_w_out, v_ffn_w_in, v_ffn_conv_w, v_ffn_conv_b, v_ffn_w_out):
    args = dict(locals())
    w = {n: args[n] for n in WEIGHTS}
    m = {n: args["m_" + n] for n in WEIGHTS}
    v = {n: args["v_" + n] for n in WEIGHTS}
    k = 2 * lax.axis_index("x") + lax.axis_index("y")
    kidx = k.reshape(1).astype(jnp.int32)
    axis2d = lambda n: BIG[n] - (1 if n in LAYERED else 0)
    slab = lambda n: w[n].shape[1 + axis2d(n)]

    small_sh_shapes = [w[n].shape for n in SMALL_SHARDED]
    rb = _pack_rows(small_sh_shapes)
    items = {}
    for group, names in GROUPS.items():
        layer = GROUP_LAYER.get(group, 0)
        items[group] = [(_Gather(axis2d(n), slab(n)), None,
                         _place_slab(w[n][layer], axis2d(n), N_CHIPS, kidx, bf16, name=f"place_{n}_{layer}")) for n in names]
    items["even_in"].append((_Gather(0, rb), None,
                             _place_slab(_pack([w[n] for n in SMALL_SHARDED], rb), 0, N_CHIPS, kidx, f32, name="place_small")))
    gathers, tokens = {}, []
    for group in GROUPS:
        sems, srcs, lands, token = _push_start(f"gather_start_{group}", items[group])
        gathers[group] = ([it[0] for it in items[group]], sems, srcs, lands)
        tokens.append(token[0, 0])
    started = functools.reduce(jnp.add, tokens)

    def landed(group, after):
        return _push_wait(f"gather_wait_{group}", [gathers[group]], [after])[0]

    even = landed("even_in", (started + norm_mix_g[0, 0]).reshape(1))
    per_chip = [_unpack(even[-1][c * rb:(c + 1) * rb], small_sh_shapes) for c in range(N_CHIPS)]
    P = {n: w[n] for n in SMALL_REP}
    for i, (n, ax) in enumerate(SMALL_SHARDED.items()):
        P[n] = jnp.concatenate([per_chip[c][i] for c in range(N_CHIPS)], axis=ax)
    P["mla_q_norm_g"], P["mla_kv_norm_g"] = P["mla_q_norm_g"].reshape(1, -1), P["mla_kv_norm_g"].reshape(1, -1)
    fix_w = {"odd_w_in": _pad_odd, "mla_w_uq": _uq_cat}

    def get_w(group, after):
        full = even if group == "even_in" else landed(group, after)
        return {n: fix_w.get(n, lambda a: a)(a) for n, a in zip(GROUPS[group], full)}

    fix_g = {"odd_w_in": lambda g: g[:, :odd_w_in.shape[2]], "mla_w_uq": _uq_uncat}
    g32, scatters, land_now = {}, {}, {}
    ra = _pack_rows([w[n].shape for n in SMALL_REP])
    rs = ra + N_CHIPS * rb
    didx = (2 * kidx + lax.axis_index("c")).astype(jnp.int32)

    def put_g(group, grads, small=None):
        layer = GROUP_LAYER.get(group)
        routes, srcs, names = [], [], list(grads)
        for n in names:
            f = fix_g.get(n, lambda g: g)
            g32.setdefault(n, {})[layer or 0] = f(grads[n][0])
            routes.append(_Scatter(axis2d(n), slab(n), layer if n in LAYERED else None))
            srcs.append(f(grads[n][1]))
            if n not in land_now:
                land_now[n] = lax.empty((3,) + w[n].shape[0 if n in LAYERED else 1:], bf16)
        if small is not None:
            blocks = [_pack([small[n] for n in SMALL_REP], ra)]
            for chip in range(N_CHIPS):
                sl = lambda n, ax: lax.slice_in_dim(small[n].reshape(w[n].shape[:ax] + (-1,) + w[n].shape[ax + 1:]),
                                                    chip * w[n].shape[ax], (chip + 1) * w[n].shape[ax], axis=ax)
                blocks.append(_pack([sl(n, ax) for n, ax in SMALL_SHARDED.items()], rb))
            names.append("small")
            routes.append(_ToAll(rs))
            srcs.append(None)
            land_now["small"] = _place_slab(jnp.concatenate(blocks), 0, N_DEV, didx, f32, name="place_small_grads")
        sems, srcs, lands, token = _push_start(f"scatter_start_{group}", [(r, s, land_now[n]) for r, s, n in zip(routes, srcs, names)])
        land_now.update(zip(names, lands))
        scatters[group] = (routes, sems, srcs, names)
        sent.append(token)
        return token

    sent = []
    loss, dx = _local_step(x[0], positions[0], loss_target[0], get_w, P, put_g)
    sent_last = sent[-1]
    loss = lax.psum(loss[0, 0], ("x", "y", "c"))

    out = {}

    def finish(tag, groups, after):
        waits = [(scatters[g][0], scatters[g][1], scatters[g][2], [land_now[n] for n in scatters[g][3]]) for g in groups]
        for g, lands in zip(groups, _push_wait(f"scatter_wait_{tag}", waits, after)):
            land_now.update(zip(scatters[g][3], lands))
        names = [n for n in dict.fromkeys(n for g in groups for n in scatters[g][3]) if n != "small"]
        part = {}
        for n in names:
            recv = land_now[n] if n in LAYERED else land_now[n][:, None]
            part[n] = _sum4([g32[n][l] for l in sorted(g32[n])], axis2d(n), recv, kidx, name=f"sum4_{n}")
        other = _swap_with_sibling(part, tag)
        done = []
        for n in names:
            C = part[n].shape[-1]
            res = _adamw(w[n].reshape(-1, C), m[n].reshape(-1, C), v[n].reshape(-1, C), [part[n], other[n]], name=f"adamw_{n}")
            out[n] = [r.reshape(w[n].shape) for r in res]
            done.append(res[0])
        return done

    done = finish("a", ["ffn1", "odd", "ffn0"], [dx, sent_last])
    finish("b", ["even"], done)

    order = SMALL_REP + list(SMALL_SHARDED)
    packed = lambda src: jnp.concatenate([_pack([src[n] for n in SMALL_REP], ra), _pack([src[n] for n in SMALL_SHARDED], rb)])
    res = _adamw_small(land_now["small"], packed(w), packed(m), packed(v), kidx, ra, rb)
    for r in res:
        parts = _unpack(r[:ra], [w[n].shape for n in SMALL_REP]) + _unpack(r[ra:], small_sh_shapes)
        for n, a in zip(order, parts):
            out.setdefault(n, []).append(a)

    return (loss, dx[None], *[out[n][0] for n in WEIGHTS], *[out[n][1] for n in WEIGHTS],
            *[out[n][2] for n in WEIGHTS], *[out[n][3] for n in WEIGHTS])
```

```python
import functools
import math

import jax
import jax.numpy as jnp
from jax import lax
from jax.experimental import pallas as pl
from jax.experimental.pallas import tpu as pltpu

f32, bf16 = jnp.float32, jnp.bfloat16
EPS = 1e-6
LANES = 128
SUBLANES = 8
VMEM_BYTES = 48 * 1024 * 1024
HGRN_CHUNK = 64
HGRN_HEADS = 4
S5_GROUPS, S5_STATE, S5_GROUP = 32, 64, 16
S5_N = S5_GROUPS * S5_STATE
S5_SEG = SUBLANES
MLA_HEADS, MLA_NOPE, MLA_ROPE, MLA_V = 8, 128, 64, 128
MLA_QK = MLA_NOPE + MLA_ROPE
MLA_Q_RANK, MLA_KV_RANK = 384, 256
ROPE_THETA = 10000.0
D_FF = 2816
ADAM_LR, ADAM_B1, ADAM_B2, ADAM_EPS, ADAM_WD, ADAM_STEP = 0.001, 0.9, 0.999, 1e-08, 0.01, 10
MESH = pl.DeviceIdType.MESH
HI = lax.Precision.HIGHEST


def _cp(*dims):
    return pltpu.CompilerParams(dimension_semantics=dims if dims else None, vmem_limit_bytes=VMEM_BYTES)


def _tile(n, t):
    if n <= t:
        return n
    c = (t // LANES) * LANES
    while c >= LANES:
        if n % c == 0:
            return c
        c -= LANES
    return n


def _dot(a, b, dn=None, precision=None):
    if dn is None:
        dn = (((a.ndim - 1,), (0,)), ((), ()))
    return lax.dot_general(a, b, dn, preferred_element_type=f32, precision=precision)


NT = (((1,), (1,)), ((), ()))
TN = (((0,), (0,)), ((), ()))


def _bdot(a, b, dn=None):
    return _dot(a.astype(bf16), b.astype(bf16), dn)


MM_PARTS = 2


def _mm(a, b, *, name, ta=False, tb=False, out_dtype=f32, res=None, also_bf16=False, tm=1024, tn=1024, tk=1024, dep=None,
        norm_g=None, norm_bwd=None):
    halves = lambda s: (s[1], 2 * s[2]) if len(s) == 3 else s
    M, K = (a.shape[1], a.shape[0]) if ta else halves(a.shape)
    N = b.shape[0] if tb else halves(b.shape)[1]
    rows = norm_g is not None or norm_bwd is not None
    if rows:
        tm, tn, tk = 512, N, K
    tm, tn, tk = _tile(M, tm), _tile(N, tn), _tile(K, tk)
    both = rows and a.ndim == 3 and tb
    if a.ndim == 3 and not both:
        tk = _tile(K // 2, tk)
    if b.ndim == 3:
        tn = _tile(N // 2, tn)
    nk = K // tk
    parts = MM_PARTS if tm % (MM_PARTS * LANES) == 0 else 1
    dn = (((0 if ta else 1,), (1 if tb else 0,)), ((), ()))
    extra = [] if norm_bwd is None else list(norm_bwd)
    if norm_g is not None:
        extra.append(norm_g)

    def body(*refs):
        a_ref, b_ref = refs[0], refs[1]
        r_ref = refs[2] if res is not None else None
        nin = 2 + (res is not None) + (dep is not None) + len(extra)
        ex = refs[nin - len(extra):nin]
        outs = refs[nin:-1] if nk > 1 else refs[nin:]
        acc = refs[-1] if nk > 1 else None
        k = pl.program_id(2)
        b_blk = b_ref[...]
        if nk > 1:
            @pl.when(k == 0)
            def _():
                acc[...] = jnp.zeros_like(acc)

        groups = []
        for part in range(parts):
            rows = pl.ds(part * (tm // parts), tm // parts)
            if both:
                p = _bdot(a_ref[0, rows, :], b_blk[:, :K // 2], dn) + _bdot(a_ref[1, rows, :], b_blk[:, K // 2:], dn)
            else:
                p = _bdot(a_ref[:, rows] if ta else a_ref[rows, :], b_blk, dn)
            if nk > 1:
                acc[rows, :] += p
            groups.append((rows, p))

        def epilogue():
            for part, (rows, p) in enumerate(groups):
                r = acc[rows, :] if nk > 1 else p
                if norm_bwd is not None:
                    r, dg = _rms_bwd_math(ex[0][rows, :], ex[1][...], r)
                    if part == 0:
                        @pl.when(pl.program_id(0) == 0)
                        def _(dg=dg):
                            outs[1][...] = dg

                    @pl.when((pl.program_id(0) > 0) | (part > 0))
                    def _(dg=dg):
                        outs[1][...] += dg
                if r_ref is not None:
                    r = r + r_ref[rows, :]
                outs[0][rows, :] = r.astype(out_dtype)
                if also_bf16:
                    outs[1][rows, :] = r.astype(bf16)
                if norm_g is not None:
                    outs[1][rows, :] = _rms(r, ex[-1][...]).astype(bf16)

        if nk > 1:
            pl.when(k == nk - 1)(epilogue)
        else:
            epilogue()

    a_spec = pl.BlockSpec((tk, tm), lambda i, j, k: (k, i)) if ta else pl.BlockSpec((tm, tk), lambda i, j, k: (i, k))
    b_spec = pl.BlockSpec((tn, tk), lambda i, j, k: (j, k)) if tb else pl.BlockSpec((tk, tn), lambda i, j, k: (k, j))
    if rows:
        b_spec = pl.BlockSpec((tn, tk) if tb else (tk, tn), lambda i, j, k: (0, 0), pipeline_mode=pl.Buffered(1))
    if both:
        a_spec = pl.BlockSpec((2, tm, K // 2), lambda i, j, k: (0, i, 0))
    elif a.ndim == 3:
        kh = K // 2 // tk
        a_spec = pl.BlockSpec((None, tm, tk), lambda i, j, k: (k // kh, i, k % kh))
    if b.ndim == 3:
        nh = N // 2 // tn
        b_spec = pl.BlockSpec((None, tk, tn), lambda i, j, k: (j // nh, k, j % nh))
    o_spec = pl.BlockSpec((tm, tn), lambda i, j, k: (i, j))
    in_specs, args = [a_spec, b_spec], [a, b]
    if res is not None:
        in_specs.append(o_spec)
        args.append(res)
    if dep is not None:
        in_specs.append(pl.BlockSpec(memory_space=pl.ANY))
        args.append(dep)
    vec = pl.BlockSpec((1, tn), lambda i, j, k: (0, j))
    if norm_bwd is not None:
        in_specs += [o_spec, vec]
    if norm_g is not None:
        in_specs.append(vec)
    args += extra
    out_shape = [jax.ShapeDtypeStruct((M, N), out_dtype)]
    out_specs = [o_spec]
    if also_bf16 or norm_g is not None:
        out_shape.append(jax.ShapeDtypeStruct((M, N), bf16))
        out_specs.append(o_spec)
    if norm_bwd is not None:
        out_shape.append(jax.ShapeDtypeStruct((1, N), f32))
        out_specs.append(vec)
    dims = ("arbitrary" if norm_bwd is not None else "parallel", "parallel", "arbitrary")
    out = pl.pallas_call(
        body, name=name, grid=(M // tm, N // tn, nk), in_specs=in_specs, out_specs=out_specs, out_shape=out_shape,
        scratch_shapes=[pltpu.VMEM((tm, tn), f32)] if nk > 1 else [], compiler_params=_cp(*dims),
    )(*args)
    return out if len(out) > 1 else out[0]


def _rms_fwd(x, g, *, name, col=0, width=None, tm=512):
    T = x.shape[0]
    width = x.shape[1] if width is None else width
    tm = _tile(T, tm)

    def body(x_ref, g_ref, o_ref):
        xv = x_ref[...]
        r = lax.rsqrt(jnp.mean(xv * xv, axis=-1, keepdims=True) + EPS)
        o_ref[...] = (xv * r * g_ref[...]).astype(bf16)

    return pl.pallas_call(
        body, name=name, grid=(T // tm,),
        in_specs=[pl.BlockSpec((tm, width), lambda i: (i, col)), pl.BlockSpec((1, width), lambda i: (0, 0))],
        out_specs=pl.BlockSpec((tm, width), lambda i: (i, 0)), out_shape=jax.ShapeDtypeStruct((T, width), bf16),
        compiler_params=_cp("parallel"),
    )(x, g)


def _rms_bwd_math(xv, g, dy):
    r = lax.rsqrt(jnp.mean(xv * xv, axis=-1, keepdims=True) + EPS)
    xh = xv * r
    dxh = dy * g
    dx = r * (dxh - xh * jnp.mean(dxh * xh, axis=-1, keepdims=True))
    dg = jnp.sum(dy * xh, axis=0, keepdims=True)
    return dx, dg


def _loss_head(h, g, target, *, tm=512):
    T, D = h.shape
    tm = _tile(T, tm)

    def body(h_ref, g_ref, t_ref, loss_ref, dh_ref, dg_ref):
        hv, gv = h_ref[...], g_ref[...]
        r = lax.rsqrt(jnp.mean(hv * hv, axis=-1, keepdims=True) + EPS)
        e = hv * r * gv - t_ref[...]
        part = 0.5 * jnp.sum(jnp.mean(e * e, axis=-1, keepdims=True), axis=0, keepdims=True)
        dx, dg = _rms_bwd_math(hv, gv, e * (1.0 / D))
        dh_ref[...] = dx

        @pl.when(pl.program_id(0) == 0)
        def _():
            loss_ref[...] = part
            dg_ref[...] = dg

        @pl.when(pl.program_id(0) > 0)
        def _():
            loss_ref[...] += part
            dg_ref[...] += dg

    row = pl.BlockSpec((tm, D), lambda i: (i, 0))
    vec = pl.BlockSpec((1, D), lambda i: (0, 0))
    return pl.pallas_call(
        body, name="loss_head", grid=(T // tm,), in_specs=[row, vec, row],
        out_specs=[pl.BlockSpec((1, 1), lambda i: (0, 0)), row, vec],
        out_shape=[jax.ShapeDtypeStruct((1, 1), f32), jax.ShapeDtypeStruct((T, D), f32), jax.ShapeDtypeStruct((1, D), f32)],
        compiler_params=_cp("arbitrary"),
    )(h, g, target)


FFN_W = 2 * LANES
FFN_ROWS = 128
HALO = 2 * SUBLANES


def _conv_taps(a_ref, c, rc):
    if isinstance(c, int) and c == 0:
        ext = jnp.concatenate([jnp.zeros((HALO, FFN_W), f32), a_ref[pl.ds(0, rc), :].astype(f32)], axis=0)
    else:
        ext = a_ref[pl.ds(pl.multiple_of(c * rc - HALO, HALO), rc + HALO), :].astype(f32)
    return ext[HALO:], pltpu.roll(ext, 1, 0)[HALO:], pltpu.roll(ext, 2, 0)[HALO:]


def _chunk_rows(c, rc):
    return pl.ds(c * rc, rc) if isinstance(c, int) else pl.ds(pl.multiple_of(c * rc, rc), rc)


def _ffn_mid_fwd(au, cw, cb, *, name):
    T = au.shape[0]
    F = au.shape[1] // 2
    nb = F // FFN_W
    rc = min(FFN_ROWS, T)
    nc = T // rc

    def body(a_ref, u_ref, w_ref, b_ref, z_ref):
        w, b = w_ref[...], b_ref[...]

        def chunk(c):
            a, a1, a2 = _conv_taps(a_ref, c, rc)
            rows = _chunk_rows(c, rc)
            ac = (w[0:1] * a2 + w[1:2] * a1 + w[2:3] * a + b).astype(bf16)
            z_ref[rows, :] = ac * jax.nn.sigmoid(ac) * u_ref[rows, :]

        chunk(0)
        lax.fori_loop(1, nc, lambda c, _: chunk(c), None)

    return pl.pallas_call(
        body, name=name, grid=(nb,),
        in_specs=[pl.BlockSpec((T, FFN_W), lambda j: (0, j)), pl.BlockSpec((T, FFN_W), lambda j: (0, nb + j)),
                  pl.BlockSpec((3, FFN_W), lambda j: (0, j)), pl.BlockSpec((1, FFN_W), lambda j: (0, j))],
        out_specs=pl.BlockSpec((T, FFN_W), lambda j: (0, j)), out_shape=jax.ShapeDtypeStruct((T, F), bf16),
        compiler_params=_cp("parallel"),
    )(au, au, cw, cb)


def _ffn_mid_bwd(au, cw, cb, dz, *, name):
    T = au.shape[0]
    F = au.shape[1] // 2
    nb = F // FFN_W
    rc = min(FFN_ROWS, T)
    nc = T // rc

    def body(a_ref, u_ref, w_ref, b_ref, dz_ref, dau_ref, dw_ref, db_ref):
        w, b = w_ref[...], b_ref[...]

        def chunk(c, carry):
            nxt, s0, s1, s2, sb = carry
            a, a1, a2 = _conv_taps(a_ref, c, rc)
            rows = _chunk_rows(c, rc)
            ac = (w[0:1] * a2 + w[1:2] * a1 + w[2:3] * a + b).astype(bf16)
            sg = jax.nn.sigmoid(ac)
            dz = dz_ref[rows, :]
            dau_ref[1, rows, :] = dz * ac * sg
            dac = (dz * u_ref[rows, :] * sg * (1.0 + ac * (1.0 - sg))).astype(f32)
            ext = jnp.concatenate([dac, nxt], axis=0)
            d1, d2 = pltpu.roll(ext, rc + HALO - 1, 0)[:rc], pltpu.roll(ext, rc + HALO - 2, 0)[:rc]
            dau_ref[0, rows, :] = (w[2:3] * dac + w[1:2] * d1 + w[0:1] * d2).astype(bf16)
            tot = lambda v: jnp.sum(v, axis=0, keepdims=True)
            return dac[:HALO], s0 + tot(dac * a2), s1 + tot(dac * a1), s2 + tot(dac * a), sb + tot(dac)

        z = jnp.zeros((1, FFN_W), f32)
        carry = (jnp.zeros((HALO, FFN_W), f32), z, z, z, z)
        carry = lax.fori_loop(0, nc - 1, lambda k, cr: chunk(nc - 1 - k, cr), carry)
        _, s0, s1, s2, sb = chunk(0, carry)
        rows = lax.broadcasted_iota(jnp.int32, (3, FFN_W), 0)
        dw_ref[...] = jnp.where(rows == 0, s0, jnp.where(rows == 1, s1, s2))
        db_ref[...] = sb

    col = lambda off: pl.BlockSpec((T, FFN_W), lambda j: (0, off + j))
    return pl.pallas_call(
        body, name=name, grid=(nb,),
        in_specs=[col(0), col(nb), pl.BlockSpec((3, FFN_W), lambda j: (0, j)), pl.BlockSpec((1, FFN_W), lambda j: (0, j)), col(0)],
        out_specs=[pl.BlockSpec((2, T, FFN_W), lambda j: (0, 0, j)), pl.BlockSpec((3, FFN_W), lambda j: (0, j)),
                   pl.BlockSpec((1, FFN_W), lambda j: (0, j))],
        out_shape=[jax.ShapeDtypeStruct((2, T, F), bf16), jax.ShapeDtypeStruct((3, F), f32), jax.ShapeDtypeStruct((1, F), f32)],
        compiler_params=_cp("parallel"),
    )(au, au, cw, cb, dz)


BNN = (((2,), (1,)), ((0,), (0,)))
BNT = (((2,), (2,)), ((0,), (0,)))
BTN = (((1,), (1,)), ((0,), (0,)))


def _heads(x):
    return jnp.stack([x[:, h * LANES:(h + 1) * LANES] for h in range(HGRN_HEADS)])


def _put_heads(ref, rows, x, dtype):
    for h in range(HGRN_HEADS):
        ref[rows, h * LANES:(h + 1) * LANES] = x[h].astype(dtype)


def _hgrn_lb(l):
    m = jnp.max(l, axis=0, keepdims=True)
    e = jnp.exp(l - m)
    return e[0:1] / jnp.sum(e, axis=0, keepdims=True)


def _hgrn_chunk(q, fx, lb):
    H, C = q.shape[0], q.shape[1]
    sg = jax.nn.sigmoid(fx)
    F = lb + (1.0 - lb) * sg
    k = 1.0 - F
    logF = jnp.log(F)
    r = lax.broadcasted_iota(jnp.int32, (H, C, C), 1)
    c = lax.broadcasted_iota(jnp.int32, (H, C, C), 2)
    tril = (r >= c)
    b = _dot(tril.astype(f32), logF, BNN, precision=HI)
    bl = jnp.sum(logF, axis=1, keepdims=True)
    eb = jnp.exp(b)
    enb = jnp.exp(-b)
    elb = jnp.exp(bl - b)
    return dict(sg=sg, F=F, k=k, b=b, bl=bl, eb=eb, enb=enb, elb=elb, qd=q * eb, kd=k * enb, kl=k * elb, tril=tril)


def _hgrn_fwd(proj, lbl, ng, *, rb=512):
    T = proj.shape[0]
    rb = min(rb, T)
    cpb = rb // HGRN_CHUNK
    nblk = T // rb
    H = HGRN_HEADS

    def body(q_ref, f_ref, i_ref, g_ref, lbl_ref, ng_ref, y_ref, st_ref, S):
        @pl.when(pl.program_id(0) == 0)
        def _():
            S[...] = jnp.zeros_like(S)

        lb = _heads(_hgrn_lb(lbl_ref[...]))
        ngv = _heads(ng_ref[...])
        for c in range(cpb):
            sl = pl.ds(c * HGRN_CHUNK, HGRN_CHUNK)
            v, gx = _heads(i_ref[sl, :]), _heads(g_ref[sl, :])
            ch = _hgrn_chunk(_heads(q_ref[sl, :]), _heads(f_ref[sl, :]), lb)
            att = jnp.where(ch["tril"], _bdot(ch["qd"], ch["kd"], BNT), 0.0)
            St = S[...]
            st_ref[:, c] = St
            o = _bdot(att, v, BNN) + _bdot(ch["qd"], St, BNT)
            S[...] = St * jnp.exp(ch["bl"]) + _bdot(v, ch["kl"], BTN)
            r = lax.rsqrt(jnp.mean(o * o, axis=-1, keepdims=True) + EPS)
            _put_heads(y_ref, sl, o * r * ngv * (gx * jax.nn.sigmoid(gx)), bf16)

    col = lambda off: pl.BlockSpec((rb, H * LANES), lambda n: (n, off))
    return pl.pallas_call(
        body, name="hgrn_fwd", grid=(nblk,),
        in_specs=[col(0), col(1), col(2), col(3), pl.BlockSpec((2, H * LANES), lambda n: (0, 0)),
                  pl.BlockSpec((1, H * LANES), lambda n: (0, 0))],
        out_specs=[pl.BlockSpec((rb, H * LANES), lambda n: (n, 0)),
                   pl.BlockSpec((H, cpb, LANES, LANES), lambda n: (0, n, 0, 0))],
        out_shape=[jax.ShapeDtypeStruct((T, H * LANES), bf16),
                   jax.ShapeDtypeStruct((H, T // HGRN_CHUNK, LANES, LANES), f32)],
        scratch_shapes=[pltpu.VMEM((H, LANES, LANES), f32)], compiler_params=_cp("arbitrary"),
    )(proj, proj, proj, proj, lbl, ng)


def _hgrn_bwd(proj, lbl, ng, states, dy, *, rb=512):
    T = proj.shape[0]
    rb = min(rb, T)
    cpb = rb // HGRN_CHUNK
    nblk = T // rb
    H = HGRN_HEADS
    C = HGRN_CHUNK

    def body(q_ref, f_ref, i_ref, g_ref, lbl_ref, ng_ref, st_ref, dy_ref,
             dq_ref, df_ref, di_ref, dg_ref, dl_ref, dng_ref, dS, dlb_acc, dng_acc):
        n = pl.program_id(0)

        @pl.when(n == 0)
        def _():
            dS[...] = jnp.zeros_like(dS)
            dlb_acc[...] = jnp.zeros_like(dlb_acc)
            dng_acc[...] = jnp.zeros_like(dng_acc)

        lb_row = _hgrn_lb(lbl_ref[...])
        lb = _heads(lb_row)
        ngv = _heads(ng_ref[...])
        r_i = lax.broadcasted_iota(jnp.int32, (H, C, C), 1)
        c_i = lax.broadcasted_iota(jnp.int32, (H, C, C), 2)
        triu = (c_i >= r_i).astype(f32)
        rows_sum = lambda x: jnp.sum(x, axis=1, keepdims=True)
        for c in reversed(range(cpb)):
            sl = pl.ds(c * C, C)
            q, v, gx = _heads(q_ref[sl, :]), _heads(i_ref[sl, :]), _heads(g_ref[sl, :])
            ch = _hgrn_chunk(q, _heads(f_ref[sl, :]), lb)
            qd, kd, kl = ch["qd"], ch["kd"], ch["kl"]
            att = jnp.where(ch["tril"], _bdot(qd, kd, BNT), 0.0)
            St = st_ref[:, c]
            o = _bdot(att, v, BNN) + _bdot(qd, St, BNT)
            r = lax.rsqrt(jnp.mean(o * o, axis=-1, keepdims=True) + EPS)
            on = o * r
            sgg = jax.nn.sigmoid(gx)
            gate = gx * sgg
            dyv = _heads(dy_ref[sl, :].astype(f32))
            _put_heads(dg_ref, sl, dyv * on * ngv * sgg * (1.0 + gx * (1.0 - sgg)), bf16)
            dng_acc[...] += rows_sum(dyv * on * gate)
            don = dyv * ngv * gate
            do = r * (don - on * jnp.mean(don * on, axis=-1, keepdims=True))
            dSt = dS[...]
            dA = jnp.where(ch["tril"], _bdot(do, v, BNT), 0.0)
            dv = _bdot(att, do, BTN) + _bdot(kl, dSt, BNT)
            dqd = _bdot(dA, kd, BNN) + _bdot(do, St, BNN)
            dkd = _bdot(dA, qd, BTN)
            dkl = _bdot(v, dSt, BNN)
            dec = jnp.exp(ch["bl"])
            ddec = rows_sum(St * dSt)
            dS[...] = _bdot(do, qd, BTN) + dSt * dec
            dB = dqd * qd - dkd * kd - dkl * kl
            dbl = rows_sum(dkl * kl) + ddec * dec
            dk = dkd * ch["enb"] + dkl * ch["elb"]
            dlogF = _dot(triu, dB, BNN, precision=HI) + dbl
            dF = dlogF / ch["F"] - dk
            sg = ch["sg"]
            _put_heads(dq_ref, sl, dqd * ch["eb"], bf16)
            _put_heads(di_ref, sl, dv, bf16)
            _put_heads(df_ref, sl, dF * (1.0 - lb) * sg * (1.0 - sg), bf16)
            dlb_acc[...] += rows_sum(dF * (1.0 - sg))

        @pl.when(n == nblk - 1)
        def _():
            rows = lax.broadcasted_iota(jnp.int32, (2, LANES), 0)
            for h in range(H):
                hs = pl.ds(h * LANES, LANES)
                lbh = lb_row[:, h * LANES:(h + 1) * LANES]
                dl0 = dlb_acc[h] * lbh * (1.0 - lbh)
                dl_ref[:, hs] = jnp.where(rows == 0, dl0, -dl0)
                dng_ref[:, hs] = dng_acc[h]

    col = lambda off: pl.BlockSpec((rb, H * LANES), lambda n: (nblk - 1 - n, off))
    vec = lambda rows: pl.BlockSpec((rows, H * LANES), lambda n: (0, 0))
    tok = jax.ShapeDtypeStruct((T, H * LANES), bf16)
    return pl.pallas_call(
        body, name="hgrn_bwd", grid=(nblk,),
        in_specs=[col(0), col(1), col(2), col(3), vec(2), vec(1),
                  pl.BlockSpec((H, cpb, LANES, LANES), lambda n: (0, nblk - 1 - n, 0, 0)), col(0)],
        out_specs=[col(0), col(0), col(0), col(0), vec(2), vec(1)],
        out_shape=[tok, tok, tok, tok, jax.ShapeDtypeStruct((2, H * LANES), f32), jax.ShapeDtypeStruct((1, H * LANES), f32)],
        scratch_shapes=[pltpu.VMEM((H, LANES, LANES), f32), pltpu.VMEM((H, 1, LANES), f32), pltpu.VMEM((H, 1, LANES), f32)],
        compiler_params=_cp("arbitrary"),
    )(proj, proj, proj, proj, lbl, ng, states, dy)


def _s5_disc_math(ar, ai, ldt, br, bi):
    dt = jnp.exp(ldt)
    mag = jnp.exp(ar * dt)
    abr, abi = mag * jnp.cos(ai * dt), mag * jnp.sin(ai * dt)
    den = ar * ar + ai * ai
    xr, xi = abr - 1.0, abi
    cr = (xr * ar + xi * ai) / den
    ci = (xi * ar - xr * ai) / den
    return abr, abi, cr * br - ci * bi, cr * bi + ci * br


def _s5_disc_fwd(ar, ai, ldt, br, bi):
    def body(ar_ref, ai_ref, ldt_ref, br_ref, bi_ref, o0, o1, o2, o3):
        outs = _s5_disc_math(ar_ref[...], ai_ref[...], ldt_ref[...], br_ref[...], bi_ref[...])
        for o, v in zip((o0, o1, o2, o3), outs):
            o[...] = v

    return pl.pallas_call(
        body, name="s5_disc_fwd",
        out_shape=[jax.ShapeDtypeStruct(ar.shape, f32)] * 2 + [jax.ShapeDtypeStruct(br.shape, f32)] * 2,
    )(ar, ai, ldt, br, bi)


def _s5_disc_bwd(ar, ai, ldt, br, bi, cts):
    def body(ar_ref, ai_ref, ldt_ref, br_ref, bi_ref, c0, c1, c2, c3, o0, o1, o2, o3, o4):
        _, vjp = jax.vjp(_s5_disc_math, ar_ref[...], ai_ref[...], ldt_ref[...], br_ref[...], bi_ref[...])
        for o, v in zip((o0, o1, o2, o3, o4), vjp((c0[...], c1[...], c2[...], c3[...]))):
            o[...] = v

    return pl.pallas_call(
        body, name="s5_disc_bwd",
        out_shape=[jax.ShapeDtypeStruct(ar.shape, f32)] * 3 + [jax.ShapeDtypeStruct(br.shape, f32)] * 2,
    )(ar, ai, ldt, br, bi, *cts)


S5_LC = 512
S5_NLC = S5_N // S5_LC
S5_UB = 4
S5_UNROLL = 4
S5_TOGETHER = 2


def _cmul(ar, ai, xr, xi):
    return ar * xr - ai * xi, ar * xi + ai * xr


def _cpow(ar, ai, n):
    rr, ri = None, None
    br, bi = ar, ai
    while n:
        if n & 1:
            rr, ri = (br, bi) if rr is None else _cmul(rr, ri, br, bi)
        n >>= 1
        if n:
            br, bi = _cmul(br, bi, br, bi)
    return rr, ri


def _s5_bu(u_ref, bre_ref, bim_ref, xr, xi):
    for k in range(S5_UB):
        uk = u_ref[:, k * LANES:(k + 1) * LANES].astype(bf16)
        xr[:, k * S5_LC:(k + 1) * S5_LC] = _dot(uk, bre_ref[k])
        xi[:, k * S5_LC:(k + 1) * S5_LC] = _dot(uk, bim_ref[k])


def _s5_scan(xr, xi, sr, si, ar_ref, ai_ref, nsteps, store):
    for c0 in range(0, S5_NLC, S5_TOGETHER):
        css = [slice(c * S5_LC, (c + 1) * S5_LC) for c in range(c0, c0 + S5_TOGETHER)]
        a = [(jnp.broadcast_to(ar_ref[:, cs], (S5_SEG, S5_LC)), jnp.broadcast_to(ai_ref[:, cs], (S5_SEG, S5_LC))) for cs in css]

        def step(j, carry, css=css, a=a):
            rows = pl.ds(pl.multiple_of(j * S5_SEG, S5_SEG), S5_SEG)
            out = []
            for u, cs in enumerate(css):
                (a_r, a_i), pr, pi = a[u], carry[2 * u], carry[2 * u + 1]
                nr = a_r * pr - a_i * pi + xr[rows, cs]
                ni = a_r * pi + a_i * pr + xi[rows, cs]
                if store:
                    xr[rows, cs] = nr
                    xi[rows, cs] = ni
                out += [nr, ni]
            return tuple(out)

        init = tuple(v for cs in css for v in (sr[:, cs], si[:, cs]))
        fin = lax.fori_loop(0, nsteps, step, init)
        for u, cs in enumerate(css):
            sr[:, cs] = fin[2 * u]
            si[:, cs] = fin[2 * u + 1]


def _s5_rscan(dr, di, xr, xi, s0r, s0i, gr, gi, acc_r, acc_i, ar_ref, ai_ref, nsteps):
    for c in range(S5_NLC):
        cs = slice(c * S5_LC, (c + 1) * S5_LC)
        a_r = jnp.broadcast_to(ar_ref[:, cs], (S5_SEG, S5_LC))
        a_i = jnp.broadcast_to(ai_ref[:, cs], (S5_SEG, S5_LC))

        def step(jj, carry, cs=cs, a_r=a_r, a_i=a_i):
            pr, pi, cr, ci = carry
            j = nsteps - 1 - jj
            rows = pl.ds(pl.multiple_of(j * S5_SEG, S5_SEG), S5_SEG)
            nr = dr[rows, cs] + a_r * pr + a_i * pi
            ni = di[rows, cs] + a_r * pi - a_i * pr
            dr[rows, cs] = nr
            di[rows, cs] = ni
            if acc_r is not None:
                prev = pl.ds(pl.multiple_of(jnp.maximum(j - 1, 0) * S5_SEG, S5_SEG), S5_SEG)
                first = j == 0
                pr_s = jnp.where(first, s0r[:, cs], xr[prev, cs])
                pi_s = jnp.where(first, s0i[:, cs], xi[prev, cs])
                cr = cr + nr * pr_s + ni * pi_s
                ci = ci - nr * pi_s + ni * pr_s
            return nr, ni, cr, ci

        z = jnp.zeros((S5_SEG, S5_LC), f32)
        init = (gr[:, cs], gi[:, cs], z, z)
        fr, fi, cr, ci = lax.fori_loop(0, nsteps, step, init, unroll=S5_UNROLL)
        gr[:, cs] = fr
        gi[:, cs] = fi
        if acc_r is not None:
            acc_r[:, cs] += cr
            acc_i[:, cs] += ci


def _s5_seg_carry(fr, fi, ar, ai, seg_len, reverse):
    pr, pi = _cpow(ar, ai if not reverse else -ai, seg_len)
    rows = lax.broadcasted_iota(jnp.int32, fr.shape, 0)
    cr, ci = jnp.zeros_like(fr), jnp.zeros_like(fi)
    sh = (S5_SEG - 1) if reverse else 1
    fr_s, fi_s = pltpu.roll(fr, sh, 0), pltpu.roll(fi, sh, 0)
    order = range(S5_SEG - 2, -1, -1) if reverse else range(1, S5_SEG)
    for r in order:
        c_r, c_i = pltpu.roll(cr, sh, 0), pltpu.roll(ci, sh, 0)
        m_r, m_i = _cmul(pr, pi, c_r, c_i)
        cr = jnp.where(rows == r, m_r + fr_s, cr)
        ci = jnp.where(rows == r, m_i + fi_s, ci)
    return cr, ci


def _gelu_parts(y):
    c0 = math.sqrt(2.0 / math.pi)
    t = jnp.tanh(c0 * (y + 0.044715 * y * y * y))
    z = 0.5 * y * (1.0 + t)
    dz = 0.5 * (1.0 + t) + 0.5 * y * (1.0 - t * t) * c0 * (1.0 + 3.0 * 0.044715 * y * y)
    return z, dz


def _s5_y(xr, xi, u_ref, cre_ref, cim_ref, d_ref):
    ys = []
    for k in range(S5_UB):
        cs = slice(k * S5_LC, (k + 1) * S5_LC)
        ys.append(_bdot(xr[:, cs], cre_ref[k]) - _bdot(xi[:, cs], cim_ref[k]))
    return jnp.concatenate(ys, axis=1) + d_ref[...] * u_ref[...]


def _s5_specs(T, rb, rev=False):
    nblk = T // rb
    blk = (lambda i: (nblk - 1 - i, 0)) if rev else (lambda i: (i, 0))
    tok = pl.BlockSpec((rb, 4 * LANES), blk)
    bmat = pl.BlockSpec((S5_UB, LANES, S5_LC), lambda i: (0, 0, 0))
    cmat = pl.BlockSpec((S5_UB, S5_LC, LANES), lambda i: (0, 0, 0))
    avec = pl.BlockSpec((1, S5_N), lambda i: (0, 0))
    seg = pl.BlockSpec((S5_SEG, S5_N), lambda i: (0, 0))
    cvec = pl.BlockSpec((1, 4 * LANES), lambda i: (0, 0))
    s0 = pl.BlockSpec((1, S5_SEG, S5_N), (lambda i: (nblk - 1 - i, 0, 0)) if rev else (lambda i: (i, 0, 0)))
    return dict(tok=tok, bmat=bmat, cmat=cmat, avec=avec, seg=seg, cvec=cvec, s0=s0, nblk=nblk)


def _s5_final(u, bre, bim, ar, ai, *, rb):
    T = u.shape[0]
    sp = _s5_specs(T, rb)

    def body(u_ref, bre_ref, bim_ref, ar_ref, ai_ref, fr_ref, fi_ref, xr, xi):
        @pl.when(pl.program_id(0) == 0)
        def _():
            fr_ref[...] = jnp.zeros_like(fr_ref)
            fi_ref[...] = jnp.zeros_like(fi_ref)

        _s5_bu(u_ref, bre_ref, bim_ref, xr, xi)
        _s5_scan(xr, xi, fr_ref, fi_ref, ar_ref, ai_ref, rb // S5_SEG, False)

    return pl.pallas_call(
        body, name="s5_final", grid=(sp["nblk"],),
        in_specs=[sp["tok"], sp["bmat"], sp["bmat"], sp["avec"], sp["avec"]], out_specs=[sp["seg"], sp["seg"]],
        out_shape=[jax.ShapeDtypeStruct((S5_SEG, S5_N), f32)] * 2,
        scratch_shapes=[pltpu.VMEM((rb, S5_N), f32)] * 2, compiler_params=_cp("arbitrary"),
    )(u, bre, bim, ar, ai)


def _s5_fwd(u, bre, bim, ar, ai, fr, fi, cre, cim, dsk, wg, bg, *, rb):
    T = u.shape[0]
    sp = _s5_specs(T, rb)
    seg_len = T // S5_SEG

    def body(u_ref, bre_ref, bim_ref, ar_ref, ai_ref, fr_ref, fi_ref, cre_ref, cim_ref, d_ref, wg_ref, bg_ref,
             o_ref, s0r_ref, s0i_ref, xr, xi, sr, si):
        @pl.when(pl.program_id(0) == 0)
        def _():
            i_r, i_i = _s5_seg_carry(fr_ref[...], fi_ref[...], ar_ref[...], ai_ref[...], seg_len, False)
            sr[...] = i_r
            si[...] = i_i

        s0r_ref[0] = sr[...]
        s0i_ref[0] = si[...]
        _s5_bu(u_ref, bre_ref, bim_ref, xr, xi)
        _s5_scan(xr, xi, sr, si, ar_ref, ai_ref, rb // S5_SEG, True)
        y = _s5_y(xr, xi, u_ref, cre_ref, cim_ref, d_ref)
        z, _ = _gelu_parts(y)
        v = _bdot(z, wg_ref[...]) + bg_ref[...]
        o_ref[...] = (z * jax.nn.sigmoid(v)).astype(bf16)

    wspec = pl.BlockSpec((4 * LANES, 4 * LANES), lambda i: (0, 0))
    return pl.pallas_call(
        body, name="s5_fwd", grid=(sp["nblk"],),
        in_specs=[sp["tok"], sp["bmat"], sp["bmat"], sp["avec"], sp["avec"], sp["seg"], sp["seg"], sp["cmat"], sp["cmat"],
                  sp["cvec"], wspec, sp["cvec"]],
        out_specs=[sp["tok"], sp["s0"], sp["s0"]],
        out_shape=[jax.ShapeDtypeStruct((T, 4 * LANES), bf16)] + [jax.ShapeDtypeStruct((sp["nblk"], S5_SEG, S5_N), f32)] * 2,
        scratch_shapes=[pltpu.VMEM((rb, S5_N), f32)] * 2 + [pltpu.VMEM((S5_SEG, S5_N), f32)] * 2,
        compiler_params=_cp("arbitrary"),
    )(u, bre, bim, ar, ai, fr, fi, cre, cim, dsk, wg, bg)


def _s5_bwd_a(u, bre, bim, ar, ai, s0r, s0i, cre, cim, cret, cimt, dsk, wg, bg, dout, *, rb):
    T = u.shape[0]
    sp = _s5_specs(T, rb, rev=True)

    def body(u_ref, bre_ref, bim_ref, ar_ref, ai_ref, s0r_ref, s0i_ref, cre_ref, cim_ref, cret_ref, cimt_ref,
             d_ref, wg_ref, bg_ref, do_ref, dy_ref, glr_ref, gli_ref, dcre_ref, dcim_ref, dd_ref, dwg_ref, dbg_ref,
             xr, xi, dr, di, sr, si):
        @pl.when(pl.program_id(0) == 0)
        def _():
            for r in (glr_ref, gli_ref, dcre_ref, dcim_ref, dd_ref, dwg_ref, dbg_ref):
                r[...] = jnp.zeros_like(r)

        sr[...] = s0r_ref[0]
        si[...] = s0i_ref[0]
        _s5_bu(u_ref, bre_ref, bim_ref, xr, xi)
        _s5_scan(xr, xi, sr, si, ar_ref, ai_ref, rb // S5_SEG, True)
        uv = u_ref[...]
        y = _s5_y(xr, xi, u_ref, cre_ref, cim_ref, d_ref)
        z, gz = _gelu_parts(y)
        v = _bdot(z, wg_ref[...]) + bg_ref[...]
        sg = jax.nn.sigmoid(v)
        dov = do_ref[...].astype(f32)
        dv = dov * z * sg * (1.0 - sg)
        dz = dov * sg + _bdot(dv, wg_ref[...], NT)
        dy = dz * gz
        dy_ref[...] = dy
        dwg_ref[...] += _bdot(z, dv, TN)
        dbg_ref[...] += jnp.sum(dv, axis=0, keepdims=True)
        dd_ref[...] += jnp.sum(dy * uv, axis=0, keepdims=True)
        for k in range(S5_UB):
            cs = slice(k * S5_LC, (k + 1) * S5_LC)
            dyk = dy[:, k * LANES:(k + 1) * LANES]
            dcre_ref[k] += _bdot(xr[:, cs], dyk, TN)
            dcim_ref[k] -= _bdot(xi[:, cs], dyk, TN)
            dr[:, cs] = _bdot(dyk, cret_ref[k])
            di[:, cs] = -_bdot(dyk, cimt_ref[k])
        _s5_rscan(dr, di, None, None, None, None, glr_ref, gli_ref, None, None, ar_ref, ai_ref, rb // S5_SEG)

    wspec = pl.BlockSpec((4 * LANES, 4 * LANES), lambda i: (0, 0))
    return pl.pallas_call(
        body, name="s5_bwd_a", grid=(sp["nblk"],),
        in_specs=[sp["tok"], sp["bmat"], sp["bmat"], sp["avec"], sp["avec"], sp["s0"], sp["s0"], sp["cmat"], sp["cmat"],
                  sp["bmat"], sp["bmat"], sp["cvec"], wspec, sp["cvec"], sp["tok"]],
        out_specs=[sp["tok"], sp["seg"], sp["seg"], sp["cmat"], sp["cmat"], sp["cvec"], wspec, sp["cvec"]],
        out_shape=[jax.ShapeDtypeStruct((T, 4 * LANES), f32)] + [jax.ShapeDtypeStruct((S5_SEG, S5_N), f32)] * 2
        + [jax.ShapeDtypeStruct((S5_UB, S5_LC, LANES), f32)] * 2
        + [jax.ShapeDtypeStruct((1, 4 * LANES), f32), jax.ShapeDtypeStruct((4 * LANES, 4 * LANES), f32),
           jax.ShapeDtypeStruct((1, 4 * LANES), f32)],
        scratch_shapes=[pltpu.VMEM((rb, S5_N), f32)] * 4 + [pltpu.VMEM((S5_SEG, S5_N), f32)] * 2,
        compiler_params=_cp("arbitrary"),
    )(u, bre, bim, ar, ai, s0r, s0i, cre, cim, cret, cimt, dsk, wg, bg, dout)


def _s5_bwd_b(u, bre, bim, bret, bimt, ar, ai, s0r, s0i, glr, gli, cret, cimt, dsk, dy, *, rb):
    T = u.shape[0]
    sp = _s5_specs(T, rb, rev=True)
    seg_len = T // S5_SEG
    nblk = sp["nblk"]

    def body(u_ref, bre_ref, bim_ref, bret_ref, bimt_ref, ar_ref, ai_ref, s0r_ref, s0i_ref, glr_ref, gli_ref,
             cret_ref, cimt_ref, d_ref, dy_ref, du_ref, dbre_ref, dbim_ref, dar_ref, dai_ref,
             xr, xi, dr, di, sr, si, gr, gi, acc_r, acc_i):
        @pl.when(pl.program_id(0) == 0)
        def _():
            x_r, x_i = _s5_seg_carry(glr_ref[...], gli_ref[...], ar_ref[...], ai_ref[...], seg_len, True)
            gr[...] = x_r
            gi[...] = x_i
            acc_r[...] = jnp.zeros_like(acc_r)
            acc_i[...] = jnp.zeros_like(acc_i)
            dbre_ref[...] = jnp.zeros_like(dbre_ref)
            dbim_ref[...] = jnp.zeros_like(dbim_ref)

        sr[...] = s0r_ref[0]
        si[...] = s0i_ref[0]
        _s5_bu(u_ref, bre_ref, bim_ref, xr, xi)
        _s5_scan(xr, xi, sr, si, ar_ref, ai_ref, rb // S5_SEG, True)
        dy = dy_ref[...]
        for k in range(S5_UB):
            cs = slice(k * S5_LC, (k + 1) * S5_LC)
            dyk = dy[:, k * LANES:(k + 1) * LANES]
            dr[:, cs] = _bdot(dyk, cret_ref[k])
            di[:, cs] = -_bdot(dyk, cimt_ref[k])
        sr[...] = s0r_ref[0]
        si[...] = s0i_ref[0]
        _s5_rscan(dr, di, xr, xi, sr, si, gr, gi, acc_r, acc_i, ar_ref, ai_ref, rb // S5_SEG)
        dus = []
        for k in range(S5_UB):
            cs = slice(k * S5_LC, (k + 1) * S5_LC)
            uk = u_ref[:, k * LANES:(k + 1) * LANES]
            dbre_ref[k] += _bdot(uk, dr[:, cs], TN)
            dbim_ref[k] += _bdot(uk, di[:, cs], TN)
            dus.append(_bdot(dr[:, cs], bret_ref[k]) + _bdot(di[:, cs], bimt_ref[k]))
        du_ref[...] = (jnp.concatenate(dus, axis=1) + d_ref[...] * dy).astype(bf16)

        @pl.when(pl.program_id(0) == nblk - 1)
        def _():
            dar_ref[...] = jnp.sum(acc_r[...], axis=0, keepdims=True)
            dai_ref[...] = jnp.sum(acc_i[...], axis=0, keepdims=True)

    return pl.pallas_call(
        body, name="s5_bwd_b", grid=(nblk,),
        in_specs=[sp["tok"], sp["bmat"], sp["bmat"], sp["cmat"], sp["cmat"], sp["avec"], sp["avec"], sp["s0"], sp["s0"],
                  sp["seg"], sp["seg"], sp["bmat"], sp["bmat"], sp["cvec"], sp["tok"]],
        out_specs=[sp["tok"], sp["bmat"], sp["bmat"], sp["avec"], sp["avec"]],
        out_shape=[jax.ShapeDtypeStruct((T, 4 * LANES), bf16)] + [jax.ShapeDtypeStruct((S5_UB, LANES, S5_LC), f32)] * 2
        + [jax.ShapeDtypeStruct((1, S5_N), f32)] * 2,
        scratch_shapes=[pltpu.VMEM((rb, S5_N), f32)] * 4 + [pltpu.VMEM((S5_SEG, S5_N), f32)] * 6,
        compiler_params=_cp("arbitrary"),
    )(u, bre, bim, bret, bimt, ar, ai, s0r, s0i, glr, gli, cret, cimt, dsk, dy)


def _blockdiag(w, transpose=False):
    if transpose:
        w = jnp.swapaxes(w, 1, 2)
    g, a, b = w.shape
    eye = jnp.eye(8, dtype=w.dtype)
    return jnp.einsum("kgab,gj->kgajb", w.reshape(4, 8, a, b), eye).reshape(4, 8 * a, 8 * b)


def _blockdiag_t(m, a, b):
    eye = jnp.eye(8, dtype=m.dtype)
    return jnp.einsum("kgajb,gj->kgab", m.reshape(4, 8, a, 8, b), eye).reshape(32, a, b)


ROT = MLA_ROPE // 2


def _rope_tables(positions):
    freqs = ROPE_THETA ** (-jnp.arange(0, MLA_ROPE, 2, dtype=f32) / MLA_ROPE)
    ang = positions.astype(f32)[:, None] * freqs
    cos, sin, z = jnp.cos(ang), jnp.sin(ang), jnp.zeros_like(ang)
    return (jnp.concatenate([cos, cos, z, z], axis=1), jnp.concatenate([-sin, z, z, z], axis=1),
            jnp.concatenate([z, sin, z, z], axis=1))


def _rot(x, c, sa, sb):
    return x * c + pltpu.roll(x, LANES - ROT, 1) * sa + pltpu.roll(x, ROT, 1) * sb


def _rot_t(dy, c, sa, sb):
    return dy * c + pltpu.roll(dy * sa, ROT, 1) + pltpu.roll(dy * sb, LANES - ROT, 1)


def _rms(xv, g):
    return xv * lax.rsqrt(jnp.mean(xv * xv, axis=-1, keepdims=True) + EPS) * g


QW, KVW = MLA_Q_RANK, MLA_KV_RANK
ODD_PAD = QW + KVW + LANES


def _mla_prep_fwd(proj, qg, kvg, tabs, *, tm=512):
    T = proj.shape[0]
    tm = _tile(T, tm)

    def body(p_ref, qg_ref, kvg_ref, c_ref, sa_ref, sb_ref, cq_ref, ckv_ref, kr_ref):
        cq_ref[...] = _rms(p_ref[:, :QW], qg_ref[...]).astype(bf16)
        ckv_ref[...] = _rms(p_ref[:, QW:QW + KVW], kvg_ref[...]).astype(bf16)
        kr_ref[...] = _rot(p_ref[:, QW + KVW:], c_ref[...], sa_ref[...], sb_ref[...]).astype(bf16)

    row = lambda w: pl.BlockSpec((tm, w), lambda i: (i, 0))
    vec = lambda w: pl.BlockSpec((1, w), lambda i: (0, 0))
    return pl.pallas_call(
        body, name="mla_prep_fwd", grid=(T // tm,),
        in_specs=[row(ODD_PAD), vec(QW), vec(KVW), row(LANES), row(LANES), row(LANES)],
        out_specs=[row(QW), row(KVW), row(LANES)],
        out_shape=[jax.ShapeDtypeStruct((T, QW), bf16), jax.ShapeDtypeStruct((T, KVW), bf16),
                   jax.ShapeDtypeStruct((T, LANES), bf16)],
        compiler_params=_cp("parallel"),
    )(proj, qg, kvg, *tabs)


def _mla_prep_bwd(proj, qg, kvg, tabs, dcqn, dckvn, dkr_heads, *, tm=512):
    T = proj.shape[0]
    tm = _tile(T, tm)

    def body(p_ref, qg_ref, kvg_ref, c_ref, sa_ref, sb_ref, dcq_ref, dckv_ref, dkr_ref, dp_ref, dqg_ref, dkvg_ref):
        dcq, dqg = _rms_bwd_math(p_ref[:, :QW], qg_ref[...], dcq_ref[...])
        dckv, dkvg = _rms_bwd_math(p_ref[:, QW:QW + KVW], kvg_ref[...], dckv_ref[...])
        dk = dkr_ref[:, :LANES]
        for h in range(1, MLA_HEADS):
            dk = dk + dkr_ref[:, h * LANES:(h + 1) * LANES]
        dkr = _rot_t(dk, c_ref[...], sa_ref[...], sb_ref[...])
        dp_ref[...] = jnp.concatenate([dcq, dckv, dkr], axis=1).astype(bf16)

        @pl.when(pl.program_id(0) == 0)
        def _():
            dqg_ref[...] = dqg
            dkvg_ref[...] = dkvg

        @pl.when(pl.program_id(0) > 0)
        def _():
            dqg_ref[...] += dqg
            dkvg_ref[...] += dkvg

    row = lambda w: pl.BlockSpec((tm, w), lambda i: (i, 0))
    vec = lambda w: pl.BlockSpec((1, w), lambda i: (0, 0))
    return pl.pallas_call(
        body, name="mla_prep_bwd", grid=(T // tm,),
        in_specs=[row(ODD_PAD), vec(QW), vec(KVW), row(LANES), row(LANES), row(LANES), row(QW), row(KVW),
                  row(MLA_HEADS * LANES)],
        out_specs=[row(ODD_PAD), vec(QW), vec(KVW)],
        out_shape=[jax.ShapeDtypeStruct((T, ODD_PAD), bf16), jax.ShapeDtypeStruct((1, QW), f32),
                   jax.ShapeDtypeStruct((1, KVW), f32)],
        compiler_params=_cp("arbitrary"),
    )(proj, qg, kvg, *tabs, dcqn, dckvn, dkr_heads)


HQ = 2 * LANES
QK_SCALE = MLA_QK ** -0.5


def _q_post(q, tabs, *, transpose, name, tm=512):
    T = q.shape[0]
    tm = _tile(T, tm)

    def body(q_ref, c_ref, sa_ref, sb_ref, o_ref):
        c, sa, sb = c_ref[...], sa_ref[...], sb_ref[...]
        for h in range(MLA_HEADS):
            nope, rope = pl.ds(h * HQ, LANES), pl.ds(h * HQ + LANES, LANES)
            o_ref[:, nope] = (q_ref[:, nope].astype(f32) * QK_SCALE).astype(bf16)
            o_ref[:, rope] = ((_rot_t if transpose else _rot)(q_ref[:, rope].astype(f32), c, sa, sb) * QK_SCALE).astype(bf16)

    tab = pl.BlockSpec((tm, LANES), lambda i: (i, 0))
    blk = pl.BlockSpec((tm, MLA_HEADS * HQ), lambda i: (i, 0))
    return pl.pallas_call(
        body, name=name, grid=(T // tm,), in_specs=[blk, tab, tab, tab], out_specs=blk,
        out_shape=jax.ShapeDtypeStruct(q.shape, bf16), compiler_params=_cp("parallel"),
    )(q, *tabs)


def _causal_mask(i, j, tq, tk):
    r = lax.broadcasted_iota(jnp.int32, (tq, tk), 0) + i * tq
    c = lax.broadcasted_iota(jnp.int32, (tq, tk), 1) + j * tk
    return c <= r


FLASH_PARTS = 4


def _flash_fwd(q, kv, kr, *, tq=1024, tk=1024):
    T = q.shape[0]
    tq = _tile(T, tq)
    tk = _tile(tq, tk)
    per = tq // tk
    H = MLA_HEADS

    def body(q_ref, kn_ref, v_ref, kr_ref, o_ref, lse_ref, m_s, acc):
        i, j = pl.program_id(1), pl.program_id(2)
        last = (i + 1) * per - 1

        @pl.when(j == 0)
        def _():
            m_s[...] = jnp.full_like(m_s, -jnp.inf)
            acc[...] = jnp.zeros_like(acc)

        def step(masked):
            k = jnp.concatenate([kn_ref[...], kr_ref[...]], axis=1)
            v1 = jnp.concatenate([v_ref[...], jnp.ones((tk, LANES), bf16)], axis=1)
            mask = _causal_mask(i, j, tq, tk) if masked else None
            for part in range(FLASH_PARTS):
                rows = pl.ds(part * (tq // FLASH_PARTS), tq // FLASH_PARTS)
                s = _dot(q_ref[rows, :], k, NT)
                if masked:
                    s = jnp.where(mask[part * (tq // FLASH_PARTS):(part + 1) * (tq // FLASH_PARTS)], s, -jnp.inf)
                m_new = jnp.maximum(m_s[rows, :], jnp.max(s, axis=-1, keepdims=True))
                alpha = jnp.exp(m_s[rows, :] - m_new)
                p = jnp.exp((s - m_new).astype(bf16))
                acc[rows, :] = alpha * acc[rows, :] + _dot(p, v1)
                m_s[rows, :] = m_new

        pl.when(j < i * per)(functools.partial(step, False))
        pl.when((j >= i * per) & (j <= last))(functools.partial(step, True))

        @pl.when(j == last)
        def _():
            l = acc[:, LANES:]
            o_ref[...] = (acc[:, :LANES] / l).astype(bf16)
            lse_ref[0] = m_s[...] + jnp.log(jnp.max(l, axis=-1, keepdims=True))

    kj = lambda i, j: jnp.minimum(j, (i + 1) * per - 1)
    kblk = lambda off: pl.BlockSpec((tk, LANES), lambda h, i, j: (kj(i, j), 2 * h + off))
    return pl.pallas_call(
        body, name="flash_fwd", grid=(H, T // tq, T // tk),
        in_specs=[pl.BlockSpec((tq, HQ), lambda h, i, j: (i, h)), kblk(0), kblk(1),
                  pl.BlockSpec((tk, LANES), lambda h, i, j: (kj(i, j), 0))],
        out_specs=[pl.BlockSpec((tq, LANES), lambda h, i, j: (i, h)), pl.BlockSpec((1, tq, 1), lambda h, i, j: (h, i, 0))],
        out_shape=[jax.ShapeDtypeStruct((T, H * LANES), bf16), jax.ShapeDtypeStruct((H, T, 1), f32)],
        scratch_shapes=[pltpu.VMEM((tq, 1), f32), pltpu.VMEM((tq, 2 * LANES), f32)],
        compiler_params=_cp("parallel", "parallel", "arbitrary"),
    )(q, kv, kv, kr)


def _flash_bwd(q, kv, kr, o, do, lse, *, tb=1024):
    T = q.shape[0]
    tb = _tile(T, tb)
    nb = T // tb
    H = MLA_HEADS

    def body(q_ref, kn_ref, v_ref, kr_ref, o_ref, do_ref, lse_ref, dkv_ref, dkr_ref, dq_ref, dk_acc, dv_acc):
        j, ii = pl.program_id(1), pl.program_id(2)
        i = jnp.maximum(ii, j)

        @pl.when((j == 0) & (ii == 0))
        def _():
            dq_ref[...] = jnp.zeros_like(dq_ref)

        @pl.when(ii == 0)
        def _():
            dk_acc[...] = jnp.zeros_like(dk_acc)
            dv_acc[...] = jnp.zeros_like(dv_acc)

        def step(masked):
            k = jnp.concatenate([kn_ref[...], kr_ref[...]], axis=1)
            p = jnp.exp((_dot(q_ref[...], k, NT) - lse_ref[0]).astype(bf16))
            if masked:
                p = jnp.where(_causal_mask(i, j, tb, tb), p, jnp.zeros_like(p))
            delta = jnp.sum(o_ref[...].astype(f32) * do_ref[...], axis=-1, keepdims=True)
            ds = p * (_bdot(do_ref[...], v_ref[...], NT) - delta).astype(bf16)
            dv_acc[...] += _bdot(p, do_ref[...], TN)
            dk_acc[...] += _bdot(ds, q_ref[...], TN)
            dq_ref[pl.ds(pl.multiple_of(i * tb, tb), tb), :] += _bdot(ds, k)

        pl.when(ii > j)(functools.partial(step, False))
        pl.when(ii == j)(functools.partial(step, True))

        @pl.when(ii == nb - 1)
        def _():
            dkv_ref[...] = jnp.concatenate([dk_acc[:, :LANES], dv_acc[...]], axis=1).astype(bf16)
            dkr_ref[...] = dk_acc[:, LANES:]

    qi = lambda h, j, i: jnp.maximum(i, j)
    kblk = lambda off: pl.BlockSpec((tb, LANES), lambda h, j, i: (j, 2 * h + off))
    vec = pl.BlockSpec((1, tb, 1), lambda h, j, i: (h, qi(h, j, i), 0))
    qblk = pl.BlockSpec((tb, LANES), lambda h, j, i: (qi(h, j, i), h))
    return pl.pallas_call(
        body, name="flash_bwd", grid=(H, nb, nb),
        in_specs=[pl.BlockSpec((tb, HQ), lambda h, j, i: (qi(h, j, i), h)), kblk(0), kblk(1),
                  pl.BlockSpec((tb, LANES), lambda h, j, i: (j, 0)), qblk, qblk, vec],
        out_specs=[pl.BlockSpec((tb, HQ), lambda h, j, i: (j, h)), pl.BlockSpec((tb, LANES), lambda h, j, i: (j, h)),
                   pl.BlockSpec((T, HQ), lambda h, j, i: (0, h))],
        out_shape=[jax.ShapeDtypeStruct((T, H * HQ), bf16), jax.ShapeDtypeStruct((T, H * LANES), f32),
                   jax.ShapeDtypeStruct((T, H * HQ), f32)],
        scratch_shapes=[pltpu.VMEM((tb, HQ), f32), pltpu.VMEM((tb, LANES), f32)],
        compiler_params=_cp("parallel", "arbitrary", "arbitrary"),
    )(q, kv, kv, kr, o, do, lse)


HBM_SPEC = pl.BlockSpec(memory_space=pltpu.HBM)
N_CHIPS = 4
N_DEV = 8

BIG = {"even_w_in": 1, "s5_w_glu": 0, "even_w_out": 0, "odd_w_in": 0, "mla_w_uq": 1, "mla_w_ukv": 1, "odd_w_out": 0,
       "ffn_w_in": 2, "ffn_w_out": 1}
LAYERED = ("ffn_w_in", "ffn_w_out")
GROUPS = {"even_in": ("even_w_in",), "even_rest": ("s5_w_glu", "even_w_out"), "ffn0": LAYERED,
          "odd": ("odd_w_in", "mla_w_uq", "mla_w_ukv", "odd_w_out"), "ffn1": LAYERED}
GROUP_LAYER = {"ffn0": 0, "ffn1": 1}


def _place():
    x, y, c = lax.axis_index("x"), lax.axis_index("y"), lax.axis_index("c")
    chips = [(1 - x, y), (x, 1 - y), (1 - x, 1 - y)]
    return x, y, c, chips


def _slab(ref, axis, k, size):
    start = pl.multiple_of(k * size, size if axis == 0 else LANES)
    idx = [slice(None)] * len(ref.shape)
    idx[axis] = pl.ds(start, size)
    return ref.at[tuple(idx)]


SEM_SPEC = pl.BlockSpec(memory_space=pltpu.SEMAPHORE)
ANY_SPEC = pl.BlockSpec(memory_space=pl.ANY)
EFFECT = pltpu.SideEffectType.DATAFLOW_SIDE_EFFECTING


def _hbm(a):
    return pltpu.with_memory_space_constraint(a, pltpu.HBM)


class _Gather:
    copies = 3

    def __init__(self, axis, size):
        self.axis, self.size = axis, size

    def view(self, land, kk):
        return _slab(land, self.axis, kk, self.size)

    def own(self, land, place):
        return self.view(land, 2 * place[0] + place[1])

    def sends(self, src, land, place):
        x, y, c, chips = place
        return [(self.own(land, place) if src is None else src, self.own(land, place), (*chip, c)) for chip in chips]

    def recvs(self, land, place):
        return [self.view(land, 2 * cx + cy) for cx, cy in place[3]]


class _Scatter:
    copies = 3

    def __init__(self, axis, size, layer=None):
        self.axis, self.size, self.layer = axis, size, layer

    def row(self, land, j):
        return land.at[j] if self.layer is None else land.at[j, self.layer]

    def sends(self, src, land, place):
        c, chips = place[2], place[3]
        return [(_slab(src, self.axis, 2 * cx + cy, self.size), self.row(land, j), (cx, cy, c))
                for j, (cx, cy) in enumerate(chips)]

    def recvs(self, land, place):
        return [self.row(land, j) for j in range(3)]


class _Sibling:
    copies = 1

    def sends(self, src, land, place):
        x, y, c, _ = place
        return [(src, land, (x, y, 1 - c))]

    def recvs(self, land, place):
        return [land]


class _ToAll:
    copies = N_DEV - 1

    def __init__(self, size):
        self.size = size

    def sends(self, src, land, place):
        x, y, c, _ = place
        flip = lambda v, bit: 1 - v if bit else v
        own = _slab(land, 0, 4 * x + 2 * y + c, self.size)
        return [(own, own, (flip(x, m & 4), flip(y, m & 2), flip(c, m & 1))) for m in range(1, N_DEV)]

    def recvs(self, land, place):
        x, y, c, _ = place
        d = 4 * x + 2 * y + c
        return [_slab(land, 0, d ^ m, self.size) for m in range(1, N_DEV)]


def _unique(arrays):
    out, index = [], {}
    for a in arrays:
        if a is not None and id(a) not in index:
            index[id(a)] = len(out)
            out.append(a)
    return out, index


def _sem_base(routes):
    base = [0]
    for r in routes:
        base.append(base[-1] + r.copies)
    return base


def _push_start(name, items):
    n = len(items)
    base = _sem_base([it[0] for it in items])
    arrays, index = _unique([it[1] for it in items] + [it[2] for it in items])
    na = len(arrays)

    def body(*refs):
        arr, send, recv, token = refs[:na], refs[na], refs[na + 1], refs[-1]
        place = _place()
        for i, (route, src, land) in enumerate(items):
            s_ref = None if src is None else arr[index[id(src)]]
            for j, (s, d, dev) in enumerate(route.sends(s_ref, arr[index[id(land)]], place)):
                pltpu.make_async_remote_copy(src_ref=s, dst_ref=d, send_sem=send.at[base[i] + j], recv_sem=recv.at[base[i] + j],
                                             device_id=dev, device_id_type=MESH).start()
        token[...] = jnp.zeros_like(token)

    res = pl.pallas_call(
        body, name=name,
        out_shape=[pltpu.SemaphoreType.DMA((base[-1],)), pltpu.SemaphoreType.DMA((base[-1],))]
        + [pltpu.HBM(a.shape, a.dtype) for a in arrays] + [jax.ShapeDtypeStruct((SUBLANES, LANES), f32)],
        in_specs=[HBM_SPEC] * na, out_specs=[SEM_SPEC, SEM_SPEC] + [HBM_SPEC] * na + [pl.BlockSpec(memory_space=pltpu.VMEM)],
        input_output_aliases={i: 2 + i for i in range(na)},
        compiler_params=pltpu.CompilerParams(has_side_effects=EFFECT),
    )(*[_hbm(a) for a in arrays])
    thru = lambda a: None if a is None else res[2 + index[id(a)]]
    return (res[0], res[1]), [thru(it[1]) for it in items], [thru(it[2]) for it in items], res[-1]


def _push_wait(name, groups, after, with_srcs=False):
    arrays, index = _unique([a for _, _, srcs, lands in groups for a in list(srcs) + list(lands)])
    na, ng = len(arrays), len(groups)

    def body(*refs):
        arr, sems = refs[:na], refs[na:na + 2 * ng]
        place = _place()
        for g, (routes, _, srcs, lands) in enumerate(groups):
            send, recv = sems[2 * g], sems[2 * g + 1]
            base = _sem_base(routes)
            for i, route in enumerate(routes):
                src, land = None if srcs[i] is None else arr[index[id(srcs[i])]], arr[index[id(lands[i])]]
                for j, ((s, d, dev), mine) in enumerate(zip(route.sends(src, land, place), route.recvs(land, place))):
                    cp = pltpu.make_async_remote_copy(src_ref=s, dst_ref=mine, send_sem=send.at[base[i] + j],
                                                      recv_sem=recv.at[base[i] + j], device_id=dev,
                                                      device_id_type=MESH)
                    cp.wait_send()
                    cp.wait_recv()

    sem_args = [s for g in groups for s in g[1]]
    res = pl.pallas_call(
        body, name=name, out_shape=[pltpu.HBM(a.shape, a.dtype) for a in arrays],
        in_specs=[HBM_SPEC] * na + [SEM_SPEC] * (2 * ng) + [ANY_SPEC] * len(after), out_specs=[HBM_SPEC] * na,
        input_output_aliases={i: i for i in range(na)},
        compiler_params=pltpu.CompilerParams(has_side_effects=EFFECT),
    )(*arrays, *sem_args, *after)
    if with_srcs:
        return [([res[index[id(a)]] for a in g[2]], [res[index[id(a)]] for a in g[3]]) for g in groups]
    return [[res[index[id(a)]] for a in g[3]] for g in groups]


def _place_slab(block, axis, slabs, idx, dtype, *, name):
    R, C = block.shape
    tm = _rows(R, C)
    nr = R // tm
    out_map = (lambda i, k: (i, k[0])) if axis == 1 else (lambda i, k: (k[0] * nr + i, 0))

    def body(k_ref, x_ref, o_ref):
        o_ref[...] = x_ref[...].astype(dtype)

    full = (R, C * slabs) if axis == 1 else (R * slabs, C)
    return pl.pallas_call(
        body, name=name, out_shape=jax.ShapeDtypeStruct(full, dtype),
        grid_spec=pltpu.PrefetchScalarGridSpec(
            num_scalar_prefetch=1, grid=(nr,), in_specs=[pl.BlockSpec((tm, C), lambda i, k: (i, 0))],
            out_specs=pl.BlockSpec((tm, C), out_map)),
        compiler_params=_cp("parallel"),
    )(idx, block)


ELEMENTWISE_BLOCK_BYTES = 1 << 20


def _rows(r, c):
    for t in (512, 256, 128, 64, 32, 16, 8):
        if r % t == 0 and t * c * 4 <= ELEMENTWISE_BLOCK_BYTES:
            return t
    return r


def _sum4(owns, axis, recv, kidx, *, name):
    L = len(owns)
    R, C = recv.shape[2:]
    tm = _rows(R, C)
    nr = R // tm

    def body(k_ref, *refs):
        own_refs, r_ref, out_ref = refs[:L], refs[L], refs[L + 1]
        for li in range(L):
            @pl.when(pl.program_id(0) == li)
            def _(o_ref=own_refs[li]):
                out_ref[...] = ((o_ref[...] + r_ref[0, 0].astype(f32)) + r_ref[1, 0].astype(f32)) + r_ref[2, 0].astype(f32)

    own_map = (lambda l, i, k: (i, k[0])) if axis == 1 else (lambda l, i, k: (k[0] * nr + i, 0))
    return pl.pallas_call(
        body, name=name, out_shape=jax.ShapeDtypeStruct((L * R, C), f32),
        grid_spec=pltpu.PrefetchScalarGridSpec(
            num_scalar_prefetch=1, grid=(L, nr),
            in_specs=[pl.BlockSpec((tm, C), own_map)] * L + [pl.BlockSpec((3, 1, tm, C), lambda l, i, k: (0, l, i, 0))],
            out_specs=pl.BlockSpec((tm, C), lambda l, i, k: (l * nr + i, 0))),
        compiler_params=_cp("parallel", "parallel"),
    )(kidx, *owns, recv)


def _adamw(w, m, v, parts, *, name):
    R, C = w.shape
    tm = _rows(R, C)
    npart = len(parts)

    def body(*refs):
        w_ref, m_ref, v_ref = refs[:3]
        g_ref, d_ref, m2_ref, v2_ref = refs[3 + npart:]
        g = refs[3][...]
        for p_ref in refs[4:3 + npart]:
            g = g + p_ref[...]
        g_ref[...] = g
        d_ref[...], m2_ref[...], v2_ref[...] = _adam_math(w_ref[...], m_ref[...], v_ref[...], g)

    blk = pl.BlockSpec((tm, C), lambda i: (i, 0))
    return pl.pallas_call(
        body, name=name, grid=(R // tm,),
        in_specs=[blk] * (3 + npart), out_specs=[blk] * 4,
        out_shape=[jax.ShapeDtypeStruct((R, C), f32)] * 4, compiler_params=_cp("parallel"),
    )(w, m, v, *parts)


def _adam_math(w, m, v, g):
    m2 = ADAM_B1 * m + (1.0 - ADAM_B1) * g
    v2 = ADAM_B2 * v + (1.0 - ADAM_B2) * (g * g)
    m_hat = m2 / (1.0 - ADAM_B1 ** ADAM_STEP)
    v_hat = v2 / (1.0 - ADAM_B2 ** ADAM_STEP)
    return -ADAM_LR * (m_hat / (jnp.sqrt(v_hat) + ADAM_EPS) + ADAM_WD * w), m2, v2


def _adamw_small(landed, w, m, v, kidx, ra, rb):
    rs = ra + N_CHIPS * rb

    def body(k_ref, l_ref, w_ref, m_ref, v_ref, g_ref, d_ref, m2_ref, v2_ref):
        mine = pl.multiple_of(ra + k_ref[0] * rb, SUBLANES)
        for lo, n, off in ((0, ra, 0), (ra, rb, mine)):
            g = l_ref[pl.ds(off, n), :]
            for d in range(1, N_DEV):
                g = g + l_ref[pl.ds(d * rs + off, n), :]
            rows = pl.ds(lo, n)
            delta, m2, v2 = _adam_math(w_ref[rows, :], m_ref[rows, :], v_ref[rows, :], g)
            g_ref[rows, :] = g
            d_ref[rows, :] = delta
            m2_ref[rows, :] = m2
            v2_ref[rows, :] = v2

    vmem = pl.BlockSpec(memory_space=pltpu.VMEM)
    return pl.pallas_call(
        body, name="adamw_small", out_shape=[jax.ShapeDtypeStruct(w.shape, f32)] * 4,
        grid_spec=pltpu.PrefetchScalarGridSpec(num_scalar_prefetch=1, grid=(), in_specs=[vmem] * 4, out_specs=[vmem] * 4),
        compiler_params=_cp(),
    )(kidx, landed, w, m, v)


def _pad_odd(w):
    return jnp.pad(w, ((0, 0), (0, ODD_PAD - w.shape[1])))


def _uq_cat(w):
    r = w.shape[0]
    return jnp.pad(w.reshape(r, MLA_HEADS, MLA_QK), ((0, 0), (0, 0), (0, HQ - MLA_QK))).reshape(r, MLA_HEADS * HQ)


def _uq_uncat(w):
    r = w.shape[0]
    return w.reshape(r, MLA_HEADS, HQ)[:, :, :MLA_QK].reshape(r, MLA_HEADS * MLA_QK)


def _to_segments(v):
    T, C = v.shape
    return v.reshape(S5_SEG, T // S5_SEG, C).transpose(1, 0, 2).reshape(T, C)


def _from_segments(v):
    T, C = v.shape
    return v.reshape(T // S5_SEG, S5_SEG, C).transpose(1, 0, 2).reshape(T, C)


def _s5_rb(T):
    return min(512, T)


def _ffn_fwd(h, hn, w_in, cw, cb, w_out, tag, next_g=None):
    au = _mm(hn, w_in, out_dtype=bf16, name=f"ffn{tag}_in", tn=1408)
    z = _ffn_mid_fwd(au, cw, cb, name=f"ffn{tag}_mid")
    return _mm(z, w_out, res=h, norm_g=next_g, name=f"ffn{tag}_out", tm=512, tk=D_FF), (hn, au, z)


def _ffn_bwd(h, g, w_in, cw, cb, w_out, saved, dh, tag, dep=None):
    hn, au, z = saved
    dz = _mm(dh, w_out, tb=True, out_dtype=bf16, name=f"ffn{tag}_dz", tn=1408, dep=dep)
    dw_out = _mm(z, dh, ta=True, also_bf16=True, name=f"ffn{tag}_dwout", tm=1408)
    dau, dcw, dcb = _ffn_mid_bwd(au, cw, cb, dz, name=f"ffn{tag}_dmid")
    dh_in, dg = _mm(dau, w_in, tb=True, res=dh, norm_bwd=(h, g), name=f"ffn{tag}_dhn", tk=1408)
    dw_in = _mm(hn, dau, ta=True, also_bf16=True, name=f"ffn{tag}_dwin", tn=1408)
    return dh_in, dg, dw_in, dcw, dcb, dw_out


def _local_step(x, positions, target, get_w, P, put_g):
    T = x.shape[0]
    rb = _s5_rb(T)
    row = lambda v: v.reshape(1, -1)
    g_mix, g_ffn = P["norm_mix_g"], P["norm_ffn_g"]
    lbl, hng = P["hgrn_lb_logits"], P["hgrn_norm_g"]
    dsk, bg = P["s5_d"], P["s5_b_glu"]
    qg, kvg = P["mla_q_norm_g"], P["mla_kv_norm_g"]
    cw, cb = P["ffn_conv_w"], P["ffn_conv_b"]

    col = lambda v: v.reshape(S5_N, 1)
    disc_in = (col(P["s5_a_re"]), col(P["s5_a_im"]), col(jnp.repeat(P["s5_log_dt"].reshape(S5_GROUPS), S5_STATE)),
               P["s5_b_re"].reshape(S5_N, S5_GROUP), P["s5_b_im"].reshape(S5_N, S5_GROUP))
    abr, abi, bbr, bbi = _s5_disc_fwd(*disc_in)
    ar, ai = abr.reshape(1, S5_N), abi.reshape(1, S5_N)
    bbr3, bbi3 = bbr.reshape(S5_GROUPS, S5_STATE, S5_GROUP), bbi.reshape(S5_GROUPS, S5_STATE, S5_GROUP)
    bre, bim = _blockdiag(bbr3, True).astype(bf16), _blockdiag(bbi3, True).astype(bf16)
    bret, bimt = _blockdiag(bbr3).astype(bf16), _blockdiag(bbi3).astype(bf16)
    c_re, c_im = P["s5_c_re"].reshape(S5_GROUPS, S5_GROUP, S5_STATE), P["s5_c_im"].reshape(S5_GROUPS, S5_GROUP, S5_STATE)
    cre, cim = _blockdiag(c_re, True).astype(bf16), _blockdiag(c_im, True).astype(bf16)
    cret, cimt = _blockdiag(c_re).astype(bf16), _blockdiag(c_im).astype(bf16)

    hn0 = _rms_fwd(x, g_mix[0:1], name="mix0_norm")
    We = get_w("even_in", hn0)
    proj_e = _mm(hn0, We["even_w_in"], name="even_in", tn=1280)
    Wr = get_w("even_rest", proj_e)
    ya, states = _hgrn_fwd(proj_e, lbl, hng)
    u_seg = _to_segments(proj_e[:, 4 * 512:])
    fr, fi = _s5_final(u_seg, bre, bim, ar, ai, rb=rb)
    yb_seg, s0r, s0i = _s5_fwd(u_seg, bre, bim, ar, ai, fr, fi, cre, cim, dsk, Wr["s5_w_glu"], bg, rb=rb)
    ycat = jnp.concatenate([ya, _from_segments(yb_seg)], axis=1)
    h1, hnf0 = _mm(ycat, Wr["even_w_out"], res=x, norm_g=g_ffn[0:1], name="even_out")
    Wf0 = get_w("ffn0", h1)
    (h2, hn2), ffn0 = _ffn_fwd(h1, hnf0, Wf0["ffn_w_in"], cw[0], cb[0:1], Wf0["ffn_w_out"], 0, next_g=g_mix[1:2])

    tabs = _rope_tables(positions)
    Wo = get_w("odd", hn2)
    proj_o = _mm(hn2, Wo["odd_w_in"], name="odd_in")
    cqn, ckvn, kr = _mla_prep_fwd(proj_o, qg, kvg, tabs)
    q = _q_post(_mm(cqn, Wo["mla_w_uq"], name="mla_uq"), tabs, transpose=False, name="q_post")
    kvb = _mm(ckvn, Wo["mla_w_ukv"], out_dtype=bf16, name="mla_ukv")
    o, lse = _flash_fwd(q, kvb, kr)
    h3, hnf1 = _mm(o, Wo["odd_w_out"], res=h2, norm_g=g_ffn[1:2], name="odd_out")
    Wf1 = get_w("ffn1", h3)
    h4, ffn1 = _ffn_fwd(h3, hnf1, Wf1["ffn_w_in"], cw[1], cb[1:2], Wf1["ffn_w_out"], 1)
    loss, dh4, dg_final = _loss_head(h4, row(P["final_norm_g"]), target)

    dh3, dg_ffn1, dw_fin1, dcw1, dcb1, dw_fout1 = _ffn_bwd(
        h3, g_ffn[1:2], Wf1["ffn_w_in"], cw[1], cb[1:2], Wf1["ffn_w_out"], ffn1, dh4, 1)
    sent = put_g("ffn1", {"ffn_w_in": dw_fin1, "ffn_w_out": dw_fout1})
    do = _mm(dh3, Wo["odd_w_out"], tb=True, out_dtype=bf16, name="odd_do", dep=sent)
    dw_oout = _mm(o, dh3, ta=True, also_bf16=True, name="odd_dwout")
    dkv, dkr_h, dq = _flash_bwd(q, kvb, kr, o, do, lse)
    dq = _q_post(dq, tabs, transpose=True, name="dq_post")
    dw_uq = _mm(cqn, dq, ta=True, also_bf16=True, name="mla_dwuq")
    dcqn = _mm(dq, Wo["mla_w_uq"], tb=True, name="mla_dcq")
    dw_ukv = _mm(ckvn, dkv, ta=True, also_bf16=True, name="mla_dwukv")
    dckvn = _mm(dkv, Wo["mla_w_ukv"], tb=True, name="mla_dckv")
    dproj_o, dqg, dkvg = _mla_prep_bwd(proj_o, qg, kvg, tabs, dcqn, dckvn, dkr_h)
    dw_oin = _mm(hn2, dproj_o, ta=True, also_bf16=True, name="odd_dwin")
    sent = put_g("odd", {"odd_w_in": dw_oin, "mla_w_uq": dw_uq, "mla_w_ukv": dw_ukv, "odd_w_out": dw_oout})
    dh2, dg_mix1 = _mm(dproj_o, Wo["odd_w_in"], tb=True, res=dh3, norm_bwd=(h2, g_mix[1:2]), name="odd_dhn")

    dh1, dg_ffn0, dw_fin0, dcw0, dcb0, dw_fout0 = _ffn_bwd(
        h1, g_ffn[0:1], Wf0["ffn_w_in"], cw[0], cb[0:1], Wf0["ffn_w_out"], ffn0, dh2, 0, dep=sent)
    sent = put_g("ffn0", {"ffn_w_in": dw_fin0, "ffn_w_out": dw_fout0})
    dycat = _mm(dh1, Wr["even_w_out"], tb=True, name="even_dy", dep=sent)
    dw_eout = _mm(ycat, dh1, ta=True, also_bf16=True, name="even_dwout")
    dq_h, df_h, di_h, dg_h, dlbl, dhng = _hgrn_bwd(proj_e, lbl, hng, states, dycat)
    dyb_seg = _to_segments(dycat[:, 512:])
    dy_s5, glr, gli, dcre, dcim, dd, dwg, dbg = _s5_bwd_a(
        u_seg, bre, bim, ar, ai, s0r, s0i, cre, cim, cret, cimt, dsk, Wr["s5_w_glu"], bg, dyb_seg, rb=rb)
    du_seg, dbre, dbim, dar, dai = _s5_bwd_b(
        u_seg, bre, bim, bret, bimt, ar, ai, s0r, s0i, glr, gli, cret, cimt, dsk, dy_s5, rb=rb)
    dproj_e = jnp.concatenate([dq_h, df_h, di_h, dg_h, _from_segments(du_seg)], axis=1)
    dx, dg_mix0 = _mm(dproj_e, We["even_w_in"], tb=True, res=dh1, norm_bwd=(x, g_mix[0:1]), name="even_dhn", tk=1280)
    dw_ein = _mm(hn0, dproj_e, ta=True, also_bf16=True, name="even_dwin", tn=1280)

    unblk = lambda m, a, b: jnp.swapaxes(_blockdiag_t(m, a, b), 1, 2)
    dbbr = unblk(dbre, S5_GROUP, S5_STATE).reshape(S5_N, S5_GROUP)
    dbbi = unblk(dbim, S5_GROUP, S5_STATE).reshape(S5_N, S5_GROUP)
    d_ar, d_ai, d_ldt, d_br, d_bi = _s5_disc_bwd(*disc_in, (dar.reshape(S5_N, 1), dai.reshape(S5_N, 1), dbbr, dbbi))
    small = {
        "norm_mix_g": jnp.concatenate([dg_mix0, dg_mix1], axis=0),
        "norm_ffn_g": jnp.concatenate([dg_ffn0, dg_ffn1], axis=0),
        "final_norm_g": dg_final.reshape(-1),
        "hgrn_lb_logits": dlbl, "hgrn_norm_g": dhng,
        "s5_a_re": d_ar.reshape(1, S5_GROUPS, S5_STATE), "s5_a_im": d_ai.reshape(1, S5_GROUPS, S5_STATE),
        "s5_log_dt": d_ldt.reshape(S5_GROUPS, S5_STATE).sum(axis=1).reshape(1, S5_GROUPS),
        "s5_b_re": d_br.reshape(1, S5_GROUPS, S5_STATE, S5_GROUP), "s5_b_im": d_bi.reshape(1, S5_GROUPS, S5_STATE, S5_GROUP),
        "s5_c_re": unblk(dcre, S5_STATE, S5_GROUP).reshape(1, S5_GROUPS, S5_GROUP, S5_STATE),
        "s5_c_im": unblk(dcim, S5_STATE, S5_GROUP).reshape(1, S5_GROUPS, S5_GROUP, S5_STATE),
        "s5_d": dd, "s5_b_glu": dbg, "mla_q_norm_g": dqg, "mla_kv_norm_g": dkvg,
        "ffn_conv_w": jnp.stack([dcw0, dcw1]), "ffn_conv_b": jnp.concatenate([dcb0, dcb1], axis=0),
    }
    put_g("even", {"even_w_in": dw_ein, "s5_w_glu": (dwg, dwg.astype(bf16)), "even_w_out": dw_eout}, small)
    return loss, dx


WEIGHTS = ["norm_mix_g", "norm_ffn_g", "final_norm_g", "even_w_in", "hgrn_lb_logits", "hgrn_norm_g", "s5_a_re", "s5_a_im",
           "s5_log_dt", "s5_b_re", "s5_b_im", "s5_c_re", "s5_c_im", "s5_d", "s5_w_glu", "s5_b_glu", "even_w_out", "odd_w_in",
           "mla_q_norm_g", "mla_w_uq", "mla_kv_norm_g", "mla_w_ukv", "odd_w_out", "ffn_w_in", "ffn_conv_w", "ffn_conv_b",
           "ffn_w_out"]
SMALL_SHARDED = {"mla_q_norm_g": 1, "mla_kv_norm_g": 1, "ffn_conv_w": 2}
SMALL = [n for n in WEIGHTS if n not in BIG]
SMALL_REP = [n for n in SMALL if n not in SMALL_SHARDED]


def _pack_rows(shapes):
    n = sum(math.prod(s) for s in shapes)
    return -(-n // (SUBLANES * LANES)) * SUBLANES


def _pack(arrays, rows):
    flat = jnp.concatenate([a.reshape(-1) for a in arrays])
    return jnp.pad(flat, (0, rows * LANES - flat.shape[0])).reshape(rows, LANES)


def _unpack(block, shapes):
    flat, out, off = block.reshape(-1), [], 0
    for s in shapes:
        n = math.prod(s)
        out.append(flat[off:off + n].reshape(s))
        off += n
    return out


def kernel(x, positions, norm_mix_g, norm_ffn_g, final_norm_g, even_w_in, hgrn_lb_logits, hgrn_norm_g, s5_a_re, s5_a_im, s5_log_dt, s5_b_re, s5_b_im, s5_c_re, s5_c_im, s5_d, s5_w_glu, s5_b_glu, even_w_out, odd_w_in, mla_q_norm_g, mla_w_uq, mla_kv_norm_g, mla_w_ukv, odd_w_out, ffn_w_in, ffn_conv_w, ffn_conv_b, ffn_w_out, loss_target, m_norm_mix_g, m_norm_ffn_g, m_final_norm_g, m_even_w_in, m_hgrn_lb_logits, m_hgrn_norm_g, m_s5_a_re, m_s5_a_im, m_s5_log_dt, m_s5_b_re, m_s5_b_im, m_s5_c_re, m_s5_c_im, m_s5_d, m_s5_w_glu, m_s5_b_glu, m_even_w_out, m_odd_w_in, m_mla_q_norm_g, m_mla_w_uq, m_mla_kv_norm_g, m_mla_w_ukv, m_odd_w_out, m_ffn_w_in, m_ffn_conv_w, m_ffn_conv_b, m_ffn_w_out, v_norm_mix_g, v_norm_ffn_g, v_final_norm_g, v_even_w_in, v_hgrn_lb_logits, v_hgrn_norm_g, v_s5_a_re, v_s5_a_im, v_s5_log_dt, v_s5_b_re, v_s5_b_im, v_s5_c_re, v_s5_c_im, v_s5_d, v_s5_w_glu, v_s5_b_glu, v_even_w_out, v_odd_w_in, v_mla_q_norm_g, v_mla_w_uq, v_mla_kv_norm_g, v_mla_w_ukv, v_odd_w_out, v_ffn_w_in, v_ffn_conv_w, v_ffn_conv_b, v_ffn_w_out):
    args = dict(locals())
    w = {n: args[n] for n in WEIGHTS}
    m = {n: args["m_" + n] for n in WEIGHTS}
    v = {n: args["v_" + n] for n in WEIGHTS}
    k = 2 * lax.axis_index("x") + lax.axis_index("y")
    kidx = k.reshape(1).astype(jnp.int32)
    axis2d = lambda n: BIG[n] - (1 if n in LAYERED else 0)
    slab = lambda n: w[n].shape[1 + axis2d(n)]

    small_sh_shapes = [w[n].shape for n in SMALL_SHARDED]
    rb = _pack_rows(small_sh_shapes)
    items = {}
    for group, names in GROUPS.items():
        layer = GROUP_LAYER.get(group, 0)
        items[group] = [(_Gather(axis2d(n), slab(n)), None,
                         _place_slab(w[n][layer], axis2d(n), N_CHIPS, kidx, bf16, name=f"place_{n}_{layer}")) for n in names]
    items["even_in"].append((_Gather(0, rb), None,
                             _place_slab(_pack([w[n] for n in SMALL_SHARDED], rb), 0, N_CHIPS, kidx, f32, name="place_small")))
    gathers, tokens = {}, []
    for group in GROUPS:
        sems, srcs, lands, token = _push_start(f"gather_start_{group}", items[group])
        gathers[group] = ([it[0] for it in items[group]], sems, srcs, lands)
        tokens.append(token[0, 0])
    started = functools.reduce(jnp.add, tokens)

    def landed(group, after):
        return _push_wait(f"gather_wait_{group}", [gathers[group]], [after])[0]

    even = landed("even_in", (started + norm_mix_g[0, 0]).reshape(1))
    per_chip = [_unpack(even[-1][c * rb:(c + 1) * rb], small_sh_shapes) for c in range(N_CHIPS)]
    P = {n: w[n] for n in SMALL_REP}
    for i, (n, ax) in enumerate(SMALL_SHARDED.items()):
        P[n] = jnp.concatenate([per_chip[c][i] for c in range(N_CHIPS)], axis=ax)
    P["mla_q_norm_g"], P["mla_kv_norm_g"] = P["mla_q_norm_g"].reshape(1, -1), P["mla_kv_norm_g"].reshape(1, -1)
    fix_w = {"odd_w_in": _pad_odd, "mla_w_uq": _uq_cat}

    def get_w(group, after):
        full = even if group == "even_in" else landed(group, after)
        return {n: fix_w.get(n, lambda a: a)(a) for n, a in zip(GROUPS[group], full)}

    fix_g = {"odd_w_in": lambda g: g[:, :odd_w_in.shape[2]], "mla_w_uq": _uq_uncat}
    g32, scatters, land_now = {}, {}, {}
    ra = _pack_rows([w[n].shape for n in SMALL_REP])
    rs = ra + N_CHIPS * rb
    didx = (2 * kidx + lax.axis_index("c")).astype(jnp.int32)

    def put_g(group, grads, small=None):
        layer = GROUP_LAYER.get(group)
        routes, srcs, names = [], [], list(grads)
        for n in names:
            f = fix_g.get(n, lambda g: g)
            g32.setdefault(n, {})[layer or 0] = f(grads[n][0])
            routes.append(_Scatter(axis2d(n), slab(n), layer if n in LAYERED else None))
            srcs.append(f(grads[n][1]))
            if n not in land_now:
                land_now[n] = lax.empty((3,) + w[n].shape[0 if n in LAYERED else 1:], bf16)
        if small is not None:
            blocks = [_pack([small[n] for n in SMALL_REP], ra)]
            for chip in range(N_CHIPS):
                sl = lambda n, ax: lax.slice_in_dim(small[n].reshape(w[n].shape[:ax] + (-1,) + w[n].shape[ax + 1:]),
                                                    chip * w[n].shape[ax], (chip + 1) * w[n].shape[ax], axis=ax)
                blocks.append(_pack([sl(n, ax) for n, ax in SMALL_SHARDED.items()], rb))
            names.append("small")
            routes.append(_ToAll(rs))
            srcs.append(None)
            land_now["small"] = _place_slab(jnp.concatenate(blocks), 0, N_DEV, didx, f32, name="place_small_grads")
        sems, srcs, lands, token = _push_start(f"scatter_start_{group}", [(r, s, land_now[n]) for r, s, n in zip(routes, srcs, names)])
        land_now.update(zip(names, lands))
        scatters[group] = (routes, sems, srcs, names)
        sent.append(token)
        return token

    sent = []
    loss, dx = _local_step(x[0], positions[0], loss_target[0], get_w, P, put_g)
    sent_last = sent[-1]
    loss = lax.psum(loss[0, 0], ("x", "y", "c"))

    out = {}

    def arrive(tag, groups, after):
        waits = [(scatters[g][0], scatters[g][1], scatters[g][2], [land_now[n] for n in scatters[g][3]]) for g in groups]
        for g, lands in zip(groups, _push_wait(f"scatter_wait_{tag}", waits, after)):
            land_now.update(zip(scatters[g][3], lands))
        names = [n for n in dict.fromkeys(n for g in groups for n in scatters[g][3]) if n != "small"]
        part = {}
        for n in names:
            recv = land_now[n] if n in LAYERED else land_now[n][:, None]
            part[n] = _sum4([g32[n][l] for l in sorted(g32[n])], axis2d(n), recv, kidx, name=f"sum4_{n}")
        items = [(_Sibling(), part[n], lax.empty(part[n].shape, f32)) for n in names]
        sems, srcs, lands, token = _push_start(f"swap_start_{tag}", items)
        return (names, part, ([it[0] for it in items], sems, srcs, lands)), token

    def update(tag, arrived, after):
        names, _, push = arrived
        mine, theirs = _push_wait(f"swap_wait_{tag}", [push], after, with_srcs=True)[0]
        part, other = dict(zip(names, mine)), dict(zip(names, theirs))
        done = []
        for n in names:
            C = part[n].shape[-1]
            res = _adamw(w[n].reshape(-1, C), m[n].reshape(-1, C), v[n].reshape(-1, C), [part[n], other[n]], name=f"adamw_{n}")
            out[n] = [r.reshape(w[n].shape) for r in res]
            done.append(res[0])
        return done

    first, crossing_a = arrive("a", ["ffn1", "odd", "ffn0"], [dx, sent_last])
    last, crossing_b = arrive("b", ["even"], [crossing_a])

    order = SMALL_REP + list(SMALL_SHARDED)
    packed = lambda src: jnp.concatenate([_pack([src[n] for n in SMALL_REP], ra), _pack([src[n] for n in SMALL_SHARDED], rb)])
    res = _adamw_small(land_now["small"], packed(w), packed(m), packed(v), kidx, ra, rb)
    update("b", last, update("a", first, [res[0], crossing_b]))
    for r in res:
        parts = _unpack(r[:ra], [w[n].shape for n in SMALL_REP]) + _unpack(r[ra:], small_sh_shapes)
        for n, a in zip(order, parts):
            out.setdefault(n, []).append(a)

    return (loss, dx[None], *[out[n][0] for n in WEIGHTS], *[out[n][1] for n in WEIGHTS],
            *[out[n][2] for n in WEIGHTS], *[out[n][3] for n in WEIGHTS])
```

```python
import functools
import math

import jax
import jax.numpy as jnp
from jax import lax
from jax.experimental import pallas as pl
from jax.experimental.pallas import tpu as pltpu

f32, bf16 = jnp.float32, jnp.bfloat16
EPS = 1e-6
LANES = 128
SUBLANES = 8
VMEM_BYTES = 48 * 1024 * 1024
HGRN_CHUNK = 64
HGRN_HEADS = 4
S5_GROUPS, S5_STATE, S5_GROUP = 32, 64, 16
S5_N = S5_GROUPS * S5_STATE
S5_SEG = SUBLANES
MLA_HEADS, MLA_NOPE, MLA_ROPE, MLA_V = 8, 128, 64, 128
MLA_QK = MLA_NOPE + MLA_ROPE
MLA_Q_RANK, MLA_KV_RANK = 384, 256
ROPE_THETA = 10000.0
D_FF = 2816
ADAM_LR, ADAM_B1, ADAM_B2, ADAM_EPS, ADAM_WD, ADAM_STEP = 0.001, 0.9, 0.999, 1e-08, 0.01, 10
MESH = pl.DeviceIdType.MESH
HI = lax.Precision.HIGHEST


def _cp(*dims):
    return pltpu.CompilerParams(dimension_semantics=dims if dims else None, vmem_limit_bytes=VMEM_BYTES)


def _tile(n, t):
    if n <= t:
        return n
    c = (t // LANES) * LANES
    while c >= LANES:
        if n % c == 0:
            return c
        c -= LANES
    return n


def _dot(a, b, dn=None, precision=None):
    if dn is None:
        dn = (((a.ndim - 1,), (0,)), ((), ()))
    return lax.dot_general(a, b, dn, preferred_element_type=f32, precision=precision)


NT = (((1,), (1,)), ((), ()))
TN = (((0,), (0,)), ((), ()))


def _bdot(a, b, dn=None):
    return _dot(a.astype(bf16), b.astype(bf16), dn)


MM_PARTS = 2


def _mm(a, b, *, name, ta=False, tb=False, out_dtype=f32, res=None, also_bf16=False, tm=1024, tn=1024, tk=1024, dep=None,
        norm_g=None, norm_bwd=None):
    halves = lambda s: (s[1], 2 * s[2]) if len(s) == 3 else s
    M, K = (a.shape[1], a.shape[0]) if ta else halves(a.shape)
    N = b.shape[0] if tb else halves(b.shape)[1]
    rows = norm_g is not None or norm_bwd is not None
    if rows:
        tm, tn, tk = 512, N, K
    tm, tn, tk = _tile(M, tm), _tile(N, tn), _tile(K, tk)
    both = rows and a.ndim == 3 and tb
    if a.ndim == 3 and not both:
        tk = _tile(K // 2, tk)
    if b.ndim == 3:
        tn = _tile(N // 2, tn)
    nk = K // tk
    parts = MM_PARTS if tm % (MM_PARTS * LANES) == 0 else 1
    dn = (((0 if ta else 1,), (1 if tb else 0,)), ((), ()))
    extra = [] if norm_bwd is None else list(norm_bwd)
    if norm_g is not None:
        extra.append(norm_g)

    def body(*refs):
        a_ref, b_ref = refs[0], refs[1]
        r_ref = refs[2] if res is not None else None
        nin = 2 + (res is not None) + (dep is not None) + len(extra)
        ex = refs[nin - len(extra):nin]
        outs = refs[nin:-1] if nk > 1 else refs[nin:]
        acc = refs[-1] if nk > 1 else None
        k = pl.program_id(2)
        b_blk = b_ref[...]
        if nk > 1:
            @pl.when(k == 0)
            def _():
                acc[...] = jnp.zeros_like(acc)

        groups = []
        for part in range(parts):
            rows = pl.ds(part * (tm // parts), tm // parts)
            if both:
                p = _bdot(a_ref[0, rows, :], b_blk[:, :K // 2], dn) + _bdot(a_ref[1, rows, :], b_blk[:, K // 2:], dn)
            else:
                p = _bdot(a_ref[:, rows] if ta else a_ref[rows, :], b_blk, dn)
            if nk > 1:
                acc[rows, :] += p
            groups.append((rows, p))

        def epilogue():
            for part, (rows, p) in enumerate(groups):
                r = acc[rows, :] if nk > 1 else p
                if norm_bwd is not None:
                    r, dg = _rms_bwd_math(ex[0][rows, :], ex[1][...], r)
                    if part == 0:
                        @pl.when(pl.program_id(0) == 0)
                        def _(dg=dg):
                            outs[1][...] = dg

                    @pl.when((pl.program_id(0) > 0) | (part > 0))
                    def _(dg=dg):
                        outs[1][...] += dg
                if r_ref is not None:
                    r = r + r_ref[rows, :]
                outs[0][rows, :] = r.astype(out_dtype)
                if also_bf16:
                    outs[1][rows, :] = r.astype(bf16)
                if norm_g is not None:
                    outs[1][rows, :] = _rms(r, ex[-1][...]).astype(bf16)

        if nk > 1:
            pl.when(k == nk - 1)(epilogue)
        else:
            epilogue()

    a_spec = pl.BlockSpec((tk, tm), lambda i, j, k: (k, i)) if ta else pl.BlockSpec((tm, tk), lambda i, j, k: (i, k))
    b_spec = pl.BlockSpec((tn, tk), lambda i, j, k: (j, k)) if tb else pl.BlockSpec((tk, tn), lambda i, j, k: (k, j))
    if rows:
        b_spec = pl.BlockSpec((tn, tk) if tb else (tk, tn), lambda i, j, k: (0, 0), pipeline_mode=pl.Buffered(1))
    if both:
        a_spec = pl.BlockSpec((2, tm, K // 2), lambda i, j, k: (0, i, 0))
    elif a.ndim == 3:
        kh = K // 2 // tk
        a_spec = pl.BlockSpec((None, tm, tk), lambda i, j, k: (k // kh, i, k % kh))
    if b.ndim == 3:
        nh = N // 2 // tn
        b_spec = pl.BlockSpec((None, tk, tn), lambda i, j, k: (j // nh, k, j % nh))
    o_spec = pl.BlockSpec((tm, tn), lambda i, j, k: (i, j))
    in_specs, args = [a_spec, b_spec], [a, b]
    if res is not None:
        in_specs.append(o_spec)
        args.append(res)
    if dep is not None:
        in_specs.append(pl.BlockSpec(memory_space=pl.ANY))
        args.append(dep)
    vec = pl.BlockSpec((1, tn), lambda i, j, k: (0, j))
    if norm_bwd is not None:
        in_specs += [o_spec, vec]
    if norm_g is not None:
        in_specs.append(vec)
    args += extra
    out_shape = [jax.ShapeDtypeStruct((M, N), out_dtype)]
    out_specs = [o_spec]
    if also_bf16 or norm_g is not None:
        out_shape.append(jax.ShapeDtypeStruct((M, N), bf16))
        out_specs.append(o_spec)
    if norm_bwd is not None:
        out_shape.append(jax.ShapeDtypeStruct((1, N), f32))
        out_specs.append(vec)
    dims = ("arbitrary" if norm_bwd is not None else "parallel", "parallel", "arbitrary")
    out = pl.pallas_call(
        body, name=name, grid=(M // tm, N // tn, nk), in_specs=in_specs, out_specs=out_specs, out_shape=out_shape,
        scratch_shapes=[pltpu.VMEM((tm, tn), f32)] if nk > 1 else [], compiler_params=_cp(*dims),
    )(*args)
    return out if len(out) > 1 else out[0]


def _rms_fwd(x, g, *, name, col=0, width=None, tm=512):
    T = x.shape[0]
    width = x.shape[1] if width is None else width
    tm = _tile(T, tm)

    def body(x_ref, g_ref, o_ref):
        xv = x_ref[...]
        r = lax.rsqrt(jnp.mean(xv * xv, axis=-1, keepdims=True) + EPS)
        o_ref[...] = (xv * r * g_ref[...]).astype(bf16)

    return pl.pallas_call(
        body, name=name, grid=(T // tm,),
        in_specs=[pl.BlockSpec((tm, width), lambda i: (i, col)), pl.BlockSpec((1, width), lambda i: (0, 0))],
        out_specs=pl.BlockSpec((tm, width), lambda i: (i, 0)), out_shape=jax.ShapeDtypeStruct((T, width), bf16),
        compiler_params=_cp("parallel"),
    )(x, g)


def _rms_bwd_math(xv, g, dy):
    r = lax.rsqrt(jnp.mean(xv * xv, axis=-1, keepdims=True) + EPS)
    xh = xv * r
    dxh = dy * g
    dx = r * (dxh - xh * jnp.mean(dxh * xh, axis=-1, keepdims=True))
    dg = jnp.sum(dy * xh, axis=0, keepdims=True)
    return dx, dg


def _loss_head(h, g, target, *, tm=512):
    T, D = h.shape
    tm = _tile(T, tm)

    def body(h_ref, g_ref, t_ref, loss_ref, dh_ref, dg_ref):
        hv, gv = h_ref[...], g_ref[...]
        r = lax.rsqrt(jnp.mean(hv * hv, axis=-1, keepdims=True) + EPS)
        e = hv * r * gv - t_ref[...]
        part = 0.5 * jnp.sum(jnp.mean(e * e, axis=-1, keepdims=True), axis=0, keepdims=True)
        dx, dg = _rms_bwd_math(hv, gv, e * (1.0 / D))
        dh_ref[...] = dx

        @pl.when(pl.program_id(0) == 0)
        def _():
            loss_ref[...] = part
            dg_ref[...] = dg

        @pl.when(pl.program_id(0) > 0)
        def _():
            loss_ref[...] += part
            dg_ref[...] += dg

    row = pl.BlockSpec((tm, D), lambda i: (i, 0))
    vec = pl.BlockSpec((1, D), lambda i: (0, 0))
    return pl.pallas_call(
        body, name="loss_head", grid=(T // tm,), in_specs=[row, vec, row],
        out_specs=[pl.BlockSpec((1, 1), lambda i: (0, 0)), row, vec],
        out_shape=[jax.ShapeDtypeStruct((1, 1), f32), jax.ShapeDtypeStruct((T, D), f32), jax.ShapeDtypeStruct((1, D), f32)],
        compiler_params=_cp("arbitrary"),
    )(h, g, target)


FFN_W = 2 * LANES
FFN_ROWS = 128
HALO = 2 * SUBLANES


def _conv_taps(a_ref, c, rc):
    if isinstance(c, int) and c == 0:
        ext = jnp.concatenate([jnp.zeros((HALO, FFN_W), f32), a_ref[pl.ds(0, rc), :].astype(f32)], axis=0)
    else:
        ext = a_ref[pl.ds(pl.multiple_of(c * rc - HALO, HALO), rc + HALO), :].astype(f32)
    return ext[HALO:], pltpu.roll(ext, 1, 0)[HALO:], pltpu.roll(ext, 2, 0)[HALO:]


def _chunk_rows(c, rc):
    return pl.ds(c * rc, rc) if isinstance(c, int) else pl.ds(pl.multiple_of(c * rc, rc), rc)


def _ffn_mid_fwd(au, cw, cb, *, name):
    T = au.shape[0]
    F = au.shape[1] // 2
    nb = F // FFN_W
    rc = min(FFN_ROWS, T)
    nc = T // rc

    def body(a_ref, u_ref, w_ref, b_ref, z_ref):
        w, b = w_ref[...], b_ref[...]

        def chunk(c):
            a, a1, a2 = _conv_taps(a_ref, c, rc)
            rows = _chunk_rows(c, rc)
            ac = (w[0:1] * a2 + w[1:2] * a1 + w[2:3] * a + b).astype(bf16)
            z_ref[rows, :] = ac * jax.nn.sigmoid(ac) * u_ref[rows, :]

        chunk(0)
        lax.fori_loop(1, nc, lambda c, _: chunk(c), None)

    return pl.pallas_call(
        body, name=name, grid=(nb,),
        in_specs=[pl.BlockSpec((T, FFN_W), lambda j: (0, j)), pl.BlockSpec((T, FFN_W), lambda j: (0, nb + j)),
                  pl.BlockSpec((3, FFN_W), lambda j: (0, j)), pl.BlockSpec((1, FFN_W), lambda j: (0, j))],
        out_specs=pl.BlockSpec((T, FFN_W), lambda j: (0, j)), out_shape=jax.ShapeDtypeStruct((T, F), bf16),
        compiler_params=_cp("parallel"),
    )(au, au, cw, cb)


def _ffn_mid_bwd(au, cw, cb, dz, *, name):
    T = au.shape[0]
    F = au.shape[1] // 2
    nb = F // FFN_W
    rc = min(FFN_ROWS, T)
    nc = T // rc

    def body(a_ref, u_ref, w_ref, b_ref, dz_ref, dau_ref, dw_ref, db_ref):
        w, b = w_ref[...], b_ref[...]

        def chunk(c, carry):
            nxt, s0, s1, s2, sb = carry
            a, a1, a2 = _conv_taps(a_ref, c, rc)
            rows = _chunk_rows(c, rc)
            ac = (w[0:1] * a2 + w[1:2] * a1 + w[2:3] * a + b).astype(bf16)
            sg = jax.nn.sigmoid(ac)
            dz = dz_ref[rows, :]
            dau_ref[1, rows, :] = dz * ac * sg
            dac = (dz * u_ref[rows, :] * sg * (1.0 + ac * (1.0 - sg))).astype(f32)
            ext = jnp.concatenate([dac, nxt], axis=0)
            d1, d2 = pltpu.roll(ext, rc + HALO - 1, 0)[:rc], pltpu.roll(ext, rc + HALO - 2, 0)[:rc]
            dau_ref[0, rows, :] = (w[2:3] * dac + w[1:2] * d1 + w[0:1] * d2).astype(bf16)
            tot = lambda v: jnp.sum(v, axis=0, keepdims=True)
            return dac[:HALO], s0 + tot(dac * a2), s1 + tot(dac * a1), s2 + tot(dac * a), sb + tot(dac)

        z = jnp.zeros((1, FFN_W), f32)
        carry = (jnp.zeros((HALO, FFN_W), f32), z, z, z, z)
        carry = lax.fori_loop(0, nc - 1, lambda k, cr: chunk(nc - 1 - k, cr), carry)
        _, s0, s1, s2, sb = chunk(0, carry)
        rows = lax.broadcasted_iota(jnp.int32, (3, FFN_W), 0)
        dw_ref[...] = jnp.where(rows == 0, s0, jnp.where(rows == 1, s1, s2))
        db_ref[...] = sb

    col = lambda off: pl.BlockSpec((T, FFN_W), lambda j: (0, off + j))
    return pl.pallas_call(
        body, name=name, grid=(nb,),
        in_specs=[col(0), col(nb), pl.BlockSpec((3, FFN_W), lambda j: (0, j)), pl.BlockSpec((1, FFN_W), lambda j: (0, j)), col(0)],
        out_specs=[pl.BlockSpec((2, T, FFN_W), lambda j: (0, 0, j)), pl.BlockSpec((3, FFN_W), lambda j: (0, j)),
                   pl.BlockSpec((1, FFN_W), lambda j: (0, j))],
        out_shape=[jax.ShapeDtypeStruct((2, T, F), bf16), jax.ShapeDtypeStruct((3, F), f32), jax.ShapeDtypeStruct((1, F), f32)],
        compiler_params=_cp("parallel"),
    )(au, au, cw, cb, dz)


BNN = (((2,), (1,)), ((0,), (0,)))
BNT = (((2,), (2,)), ((0,), (0,)))
BTN = (((1,), (1,)), ((0,), (0,)))


def _heads(x):
    return jnp.stack([x[:, h * LANES:(h + 1) * LANES] for h in range(HGRN_HEADS)])


def _put_heads(ref, rows, x, dtype):
    for h in range(HGRN_HEADS):
        ref[rows, h * LANES:(h + 1) * LANES] = x[h].astype(dtype)


def _hgrn_lb(l):
    m = jnp.max(l, axis=0, keepdims=True)
    e = jnp.exp(l - m)
    return e[0:1] / jnp.sum(e, axis=0, keepdims=True)


def _hgrn_chunk(q, fx, lb):
    H, C = q.shape[0], q.shape[1]
    sg = jax.nn.sigmoid(fx)
    F = lb + (1.0 - lb) * sg
    k = 1.0 - F
    logF = jnp.log(F)
    r = lax.broadcasted_iota(jnp.int32, (H, C, C), 1)
    c = lax.broadcasted_iota(jnp.int32, (H, C, C), 2)
    tril = (r >= c)
    b = _dot(tril.astype(f32), logF, BNN, precision=HI)
    bl = jnp.sum(logF, axis=1, keepdims=True)
    eb = jnp.exp(b)
    enb = jnp.exp(-b)
    elb = jnp.exp(bl - b)
    return dict(sg=sg, F=F, k=k, b=b, bl=bl, eb=eb, enb=enb, elb=elb, qd=q * eb, kd=k * enb, kl=k * elb, tril=tril)


def _hgrn_fwd(proj, lbl, ng, *, rb=512):
    T = proj.shape[0]
    rb = min(rb, T)
    cpb = rb // HGRN_CHUNK
    nblk = T // rb
    H = HGRN_HEADS

    def body(q_ref, f_ref, i_ref, g_ref, lbl_ref, ng_ref, y_ref, st_ref, S):
        @pl.when(pl.program_id(0) == 0)
        def _():
            S[...] = jnp.zeros_like(S)

        lb = _heads(_hgrn_lb(lbl_ref[...]))
        ngv = _heads(ng_ref[...])
        for c in range(cpb):
            sl = pl.ds(c * HGRN_CHUNK, HGRN_CHUNK)
            v, gx = _heads(i_ref[sl, :]), _heads(g_ref[sl, :])
            ch = _hgrn_chunk(_heads(q_ref[sl, :]), _heads(f_ref[sl, :]), lb)
            att = jnp.where(ch["tril"], _bdot(ch["qd"], ch["kd"], BNT), 0.0)
            St = S[...]
            st_ref[:, c] = St
            o = _bdot(att, v, BNN) + _bdot(ch["qd"], St, BNT)
            S[...] = St * jnp.exp(ch["bl"]) + _bdot(v, ch["kl"], BTN)
            r = lax.rsqrt(jnp.mean(o * o, axis=-1, keepdims=True) + EPS)
            _put_heads(y_ref, sl, o * r * ngv * (gx * jax.nn.sigmoid(gx)), bf16)

    col = lambda off: pl.BlockSpec((rb, H * LANES), lambda n: (n, off))
    return pl.pallas_call(
        body, name="hgrn_fwd", grid=(nblk,),
        in_specs=[col(0), col(1), col(2), col(3), pl.BlockSpec((2, H * LANES), lambda n: (0, 0)),
                  pl.BlockSpec((1, H * LANES), lambda n: (0, 0))],
        out_specs=[pl.BlockSpec((rb, H * LANES), lambda n: (n, 0)),
                   pl.BlockSpec((H, cpb, LANES, LANES), lambda n: (0, n, 0, 0))],
        out_shape=[jax.ShapeDtypeStruct((T, H * LANES), bf16),
                   jax.ShapeDtypeStruct((H, T // HGRN_CHUNK, LANES, LANES), f32)],
        scratch_shapes=[pltpu.VMEM((H, LANES, LANES), f32)], compiler_params=_cp("arbitrary"),
    )(proj, proj, proj, proj, lbl, ng)


def _hgrn_bwd(proj, lbl, ng, states, dy, *, rb=512):
    T = proj.shape[0]
    rb = min(rb, T)
    cpb = rb // HGRN_CHUNK
    nblk = T // rb
    H = HGRN_HEADS
    C = HGRN_CHUNK

    def body(q_ref, f_ref, i_ref, g_ref, lbl_ref, ng_ref, st_ref, dy_ref,
             dq_ref, df_ref, di_ref, dg_ref, dl_ref, dng_ref, dS, dlb_acc, dng_acc):
        n = pl.program_id(0)

        @pl.when(n == 0)
        def _():
            dS[...] = jnp.zeros_like(dS)
            dlb_acc[...] = jnp.zeros_like(dlb_acc)
            dng_acc[...] = jnp.zeros_like(dng_acc)

        lb_row = _hgrn_lb(lbl_ref[...])
        lb = _heads(lb_row)
        ngv = _heads(ng_ref[...])
        r_i = lax.broadcasted_iota(jnp.int32, (H, C, C), 1)
        c_i = lax.broadcasted_iota(jnp.int32, (H, C, C), 2)
        triu = (c_i >= r_i).astype(f32)
        rows_sum = lambda x: jnp.sum(x, axis=1, keepdims=True)
        for c in reversed(range(cpb)):
            sl = pl.ds(c * C, C)
            q, v, gx = _heads(q_ref[sl, :]), _heads(i_ref[sl, :]), _heads(g_ref[sl, :])
            ch = _hgrn_chunk(q, _heads(f_ref[sl, :]), lb)
            qd, kd, kl = ch["qd"], ch["kd"], ch["kl"]
            att = jnp.where(ch["tril"], _bdot(qd, kd, BNT), 0.0)
            St = st_ref[:, c]
            o = _bdot(att, v, BNN) + _bdot(qd, St, BNT)
            r = lax.rsqrt(jnp.mean(o * o, axis=-1, keepdims=True) + EPS)
            on = o * r
            sgg = jax.nn.sigmoid(gx)
            gate = gx * sgg
            dyv = _heads(dy_ref[sl, :].astype(f32))
            _put_heads(dg_ref, sl, dyv * on * ngv * sgg * (1.0 + gx * (1.0 - sgg)), bf16)
            dng_acc[...] += rows_sum(dyv * on * gate)
            don = dyv * ngv * gate
            do = r * (don - on * jnp.mean(don * on, axis=-1, keepdims=True))
            dSt = dS[...]
            dA = jnp.where(ch["tril"], _bdot(do, v, BNT), 0.0)
            dv = _bdot(att, do, BTN) + _bdot(kl, dSt, BNT)
            dqd = _bdot(dA, kd, BNN) + _bdot(do, St, BNN)
            dkd = _bdot(dA, qd, BTN)
            dkl = _bdot(v, dSt, BNN)
            dec = jnp.exp(ch["bl"])
            ddec = rows_sum(St * dSt)
            dS[...] = _bdot(do, qd, BTN) + dSt * dec
            dB = dqd * qd - dkd * kd - dkl * kl
            dbl = rows_sum(dkl * kl) + ddec * dec
            dk = dkd * ch["enb"] + dkl * ch["elb"]
            dlogF = _dot(triu, dB, BNN, precision=HI) + dbl
            dF = dlogF / ch["F"] - dk
            sg = ch["sg"]
            _put_heads(dq_ref, sl, dqd * ch["eb"], bf16)
            _put_heads(di_ref, sl, dv, bf16)
            _put_heads(df_ref, sl, dF * (1.0 - lb) * sg * (1.0 - sg), bf16)
            dlb_acc[...] += rows_sum(dF * (1.0 - sg))

        @pl.when(n == nblk - 1)
        def _():
            rows = lax.broadcasted_iota(jnp.int32, (2, LANES), 0)
            for h in range(H):
                hs = pl.ds(h * LANES, LANES)
                lbh = lb_row[:, h * LANES:(h + 1) * LANES]
                dl0 = dlb_acc[h] * lbh * (1.0 - lbh)
                dl_ref[:, hs] = jnp.where(rows == 0, dl0, -dl0)
                dng_ref[:, hs] = dng_acc[h]

    col = lambda off: pl.BlockSpec((rb, H * LANES), lambda n: (nblk - 1 - n, off))
    vec = lambda rows: pl.BlockSpec((rows, H * LANES), lambda n: (0, 0))
    tok = jax.ShapeDtypeStruct((T, H * LANES), bf16)
    return pl.pallas_call(
        body, name="hgrn_bwd", grid=(nblk,),
        in_specs=[col(0), col(1), col(2), col(3), vec(2), vec(1),
                  pl.BlockSpec((H, cpb, LANES, LANES), lambda n: (0, nblk - 1 - n, 0, 0)), col(0)],
        out_specs=[col(0), col(0), col(0), col(0), vec(2), vec(1)],
        out_shape=[tok, tok, tok, tok, jax.ShapeDtypeStruct((2, H * LANES), f32), jax.ShapeDtypeStruct((1, H * LANES), f32)],
        scratch_shapes=[pltpu.VMEM((H, LANES, LANES), f32), pltpu.VMEM((H, 1, LANES), f32), pltpu.VMEM((H, 1, LANES), f32)],
        compiler_params=_cp("arbitrary"),
    )(proj, proj, proj, proj, lbl, ng, states, dy)


def _s5_disc_math(ar, ai, ldt, br, bi):
    dt = jnp.exp(ldt)
    mag = jnp.exp(ar * dt)
    abr, abi = mag * jnp.cos(ai * dt), mag * jnp.sin(ai * dt)
    den = ar * ar + ai * ai
    xr, xi = abr - 1.0, abi
    cr = (xr * ar + xi * ai) / den
    ci = (xi * ar - xr * ai) / den
    return abr, abi, cr * br - ci * bi, cr * bi + ci * br


def _s5_disc_fwd(ar, ai, ldt, br, bi):
    def body(ar_ref, ai_ref, ldt_ref, br_ref, bi_ref, o0, o1, o2, o3):
        outs = _s5_disc_math(ar_ref[...], ai_ref[...], ldt_ref[...], br_ref[...], bi_ref[...])
        for o, v in zip((o0, o1, o2, o3), outs):
            o[...] = v

    return pl.pallas_call(
        body, name="s5_disc_fwd",
        out_shape=[jax.ShapeDtypeStruct(ar.shape, f32)] * 2 + [jax.ShapeDtypeStruct(br.shape, f32)] * 2,
    )(ar, ai, ldt, br, bi)


def _s5_disc_bwd(ar, ai, ldt, br, bi, cts):
    def body(ar_ref, ai_ref, ldt_ref, br_ref, bi_ref, c0, c1, c2, c3, o0, o1, o2, o3, o4):
        _, vjp = jax.vjp(_s5_disc_math, ar_ref[...], ai_ref[...], ldt_ref[...], br_ref[...], bi_ref[...])
        for o, v in zip((o0, o1, o2, o3, o4), vjp((c0[...], c1[...], c2[...], c3[...]))):
            o[...] = v

    return pl.pallas_call(
        body, name="s5_disc_bwd",
        out_shape=[jax.ShapeDtypeStruct(ar.shape, f32)] * 3 + [jax.ShapeDtypeStruct(br.shape, f32)] * 2,
    )(ar, ai, ldt, br, bi, *cts)


S5_LC = 512
S5_NLC = S5_N // S5_LC
S5_UB = 4
S5_UNROLL = 4
S5_TOGETHER = 2


def _cmul(ar, ai, xr, xi):
    return ar * xr - ai * xi, ar * xi + ai * xr


def _cpow(ar, ai, n):
    rr, ri = None, None
    br, bi = ar, ai
    while n:
        if n & 1:
            rr, ri = (br, bi) if rr is None else _cmul(rr, ri, br, bi)
        n >>= 1
        if n:
            br, bi = _cmul(br, bi, br, bi)
    return rr, ri


def _s5_bu(u_ref, bre_ref, bim_ref, xr, xi):
    for k in range(S5_UB):
        uk = u_ref[:, k * LANES:(k + 1) * LANES].astype(bf16)
        xr[:, k * S5_LC:(k + 1) * S5_LC] = _dot(uk, bre_ref[k])
        xi[:, k * S5_LC:(k + 1) * S5_LC] = _dot(uk, bim_ref[k])


def _s5_scan(xr, xi, sr, si, ar_ref, ai_ref, nsteps, store):
    for c0 in range(0, S5_NLC, S5_TOGETHER):
        css = [slice(c * S5_LC, (c + 1) * S5_LC) for c in range(c0, c0 + S5_TOGETHER)]
        a = [(jnp.broadcast_to(ar_ref[:, cs], (S5_SEG, S5_LC)), jnp.broadcast_to(ai_ref[:, cs], (S5_SEG, S5_LC))) for cs in css]

        def step(j, carry, css=css, a=a):
            rows = pl.ds(pl.multiple_of(j * S5_SEG, S5_SEG), S5_SEG)
            out = []
            for u, cs in enumerate(css):
                (a_r, a_i), pr, pi = a[u], carry[2 * u], carry[2 * u + 1]
                nr = a_r * pr - a_i * pi + xr[rows, cs]
                ni = a_r * pi + a_i * pr + xi[rows, cs]
                if store:
                    xr[rows, cs] = nr
                    xi[rows, cs] = ni
                out += [nr, ni]
            return tuple(out)

        init = tuple(v for cs in css for v in (sr[:, cs], si[:, cs]))
        fin = lax.fori_loop(0, nsteps, step, init)
        for u, cs in enumerate(css):
            sr[:, cs] = fin[2 * u]
            si[:, cs] = fin[2 * u + 1]


def _s5_rscan(dr, di, xr, xi, s0r, s0i, gr, gi, acc_r, acc_i, ar_ref, ai_ref, nsteps):
    for c in range(S5_NLC):
        cs = slice(c * S5_LC, (c + 1) * S5_LC)
        a_r = jnp.broadcast_to(ar_ref[:, cs], (S5_SEG, S5_LC))
        a_i = jnp.broadcast_to(ai_ref[:, cs], (S5_SEG, S5_LC))

        def step(jj, carry, cs=cs, a_r=a_r, a_i=a_i):
            pr, pi, cr, ci = carry
            j = nsteps - 1 - jj
            rows = pl.ds(pl.multiple_of(j * S5_SEG, S5_SEG), S5_SEG)
            nr = dr[rows, cs] + a_r * pr + a_i * pi
            ni = di[rows, cs] + a_r * pi - a_i * pr
            dr[rows, cs] = nr
            di[rows, cs] = ni
            if acc_r is not None:
                prev = pl.ds(pl.multiple_of(jnp.maximum(j - 1, 0) * S5_SEG, S5_SEG), S5_SEG)
                first = j == 0
                pr_s = jnp.where(first, s0r[:, cs], xr[prev, cs])
                pi_s = jnp.where(first, s0i[:, cs], xi[prev, cs])
                cr = cr + nr * pr_s + ni * pi_s
                ci = ci - nr * pi_s + ni * pr_s
            return nr, ni, cr, ci

        z = jnp.zeros((S5_SEG, S5_LC), f32)
        init = (gr[:, cs], gi[:, cs], z, z)
        fr, fi, cr, ci = lax.fori_loop(0, nsteps, step, init, unroll=S5_UNROLL)
        gr[:, cs] = fr
        gi[:, cs] = fi
        if acc_r is not None:
            acc_r[:, cs] += cr
            acc_i[:, cs] += ci


def _s5_seg_carry(fr, fi, ar, ai, seg_len, reverse):
    pr, pi = _cpow(ar, ai if not reverse else -ai, seg_len)
    rows = lax.broadcasted_iota(jnp.int32, fr.shape, 0)
    cr, ci = jnp.zeros_like(fr), jnp.zeros_like(fi)
    sh = (S5_SEG - 1) if reverse else 1
    fr_s, fi_s = pltpu.roll(fr, sh, 0), pltpu.roll(fi, sh, 0)
    order = range(S5_SEG - 2, -1, -1) if reverse else range(1, S5_SEG)
    for r in order:
        c_r, c_i = pltpu.roll(cr, sh, 0), pltpu.roll(ci, sh, 0)
        m_r, m_i = _cmul(pr, pi, c_r, c_i)
        cr = jnp.where(rows == r, m_r + fr_s, cr)
        ci = jnp.where(rows == r, m_i + fi_s, ci)
    return cr, ci


def _gelu_parts(y):
    c0 = math.sqrt(2.0 / math.pi)
    t = jnp.tanh(c0 * (y + 0.044715 * y * y * y))
    z = 0.5 * y * (1.0 + t)
    dz = 0.5 * (1.0 + t) + 0.5 * y * (1.0 - t * t) * c0 * (1.0 + 3.0 * 0.044715 * y * y)
    return z, dz


def _s5_y(xr, xi, u_ref, cre_ref, cim_ref, d_ref):
    ys = []
    for k in range(S5_UB):
        cs = slice(k * S5_LC, (k + 1) * S5_LC)
        ys.append(_bdot(xr[:, cs], cre_ref[k]) - _bdot(xi[:, cs], cim_ref[k]))
    return jnp.concatenate(ys, axis=1) + d_ref[...] * u_ref[...]


def _s5_specs(T, rb, rev=False):
    nblk = T // rb
    blk = (lambda i: (nblk - 1 - i, 0)) if rev else (lambda i: (i, 0))
    tok = pl.BlockSpec((rb, 4 * LANES), blk)
    bmat = pl.BlockSpec((S5_UB, LANES, S5_LC), lambda i: (0, 0, 0))
    cmat = pl.BlockSpec((S5_UB, S5_LC, LANES), lambda i: (0, 0, 0))
    avec = pl.BlockSpec((1, S5_N), lambda i: (0, 0))
    seg = pl.BlockSpec((S5_SEG, S5_N), lambda i: (0, 0))
    cvec = pl.BlockSpec((1, 4 * LANES), lambda i: (0, 0))
    s0 = pl.BlockSpec((1, S5_SEG, S5_N), (lambda i: (nblk - 1 - i, 0, 0)) if rev else (lambda i: (i, 0, 0)))
    return dict(tok=tok, bmat=bmat, cmat=cmat, avec=avec, seg=seg, cvec=cvec, s0=s0, nblk=nblk)


def _s5_final(u, bre, bim, ar, ai, *, rb):
    T = u.shape[0]
    sp = _s5_specs(T, rb)

    def body(u_ref, bre_ref, bim_ref, ar_ref, ai_ref, fr_ref, fi_ref, xr, xi):
        @pl.when(pl.program_id(0) == 0)
        def _():
            fr_ref[...] = jnp.zeros_like(fr_ref)
            fi_ref[...] = jnp.zeros_like(fi_ref)

        _s5_bu(u_ref, bre_ref, bim_ref, xr, xi)
        _s5_scan(xr, xi, fr_ref, fi_ref, ar_ref, ai_ref, rb // S5_SEG, False)

    return pl.pallas_call(
        body, name="s5_final", grid=(sp["nblk"],),
        in_specs=[sp["tok"], sp["bmat"], sp["bmat"], sp["avec"], sp["avec"]], out_specs=[sp["seg"], sp["seg"]],
        out_shape=[jax.ShapeDtypeStruct((S5_SEG, S5_N), f32)] * 2,
        scratch_shapes=[pltpu.VMEM((rb, S5_N), f32)] * 2, compiler_params=_cp("arbitrary"),
    )(u, bre, bim, ar, ai)


def _s5_fwd(u, bre, bim, ar, ai, fr, fi, cre, cim, dsk, wg, bg, *, rb):
    T = u.shape[0]
    sp = _s5_specs(T, rb)
    seg_len = T // S5_SEG

    def body(u_ref, bre_ref, bim_ref, ar_ref, ai_ref, fr_ref, fi_ref, cre_ref, cim_ref, d_ref, wg_ref, bg_ref,
             o_ref, s0r_ref, s0i_ref, xr, xi, sr, si):
        @pl.when(pl.program_id(0) == 0)
        def _():
            i_r, i_i = _s5_seg_carry(fr_ref[...], fi_ref[...], ar_ref[...], ai_ref[...], seg_len, False)
            sr[...] = i_r
            si[...] = i_i

        s0r_ref[0] = sr[...]
        s0i_ref[0] = si[...]
        _s5_bu(u_ref, bre_ref, bim_ref, xr, xi)
        _s5_scan(xr, xi, sr, si, ar_ref, ai_ref, rb // S5_SEG, True)
        y = _s5_y(xr, xi, u_ref, cre_ref, cim_ref, d_ref)
        z, _ = _gelu_parts(y)
        v = _bdot(z, wg_ref[...]) + bg_ref[...]
        o_ref[...] = (z * jax.nn.sigmoid(v)).astype(bf16)

    wspec = pl.BlockSpec((4 * LANES, 4 * LANES), lambda i: (0, 0))
    return pl.pallas_call(
        body, name="s5_fwd", grid=(sp["nblk"],),
        in_specs=[sp["tok"], sp["bmat"], sp["bmat"], sp["avec"], sp["avec"], sp["seg"], sp["seg"], sp["cmat"], sp["cmat"],
                  sp["cvec"], wspec, sp["cvec"]],
        out_specs=[sp["tok"], sp["s0"], sp["s0"]],
        out_shape=[jax.ShapeDtypeStruct((T, 4 * LANES), bf16)] + [jax.ShapeDtypeStruct((sp["nblk"], S5_SEG, S5_N), f32)] * 2,
        scratch_shapes=[pltpu.VMEM((rb, S5_N), f32)] * 2 + [pltpu.VMEM((S5_SEG, S5_N), f32)] * 2,
        compiler_params=_cp("arbitrary"),
    )(u, bre, bim, ar, ai, fr, fi, cre, cim, dsk, wg, bg)


def _s5_bwd_a(u, bre, bim, ar, ai, s0r, s0i, cre, cim, cret, cimt, dsk, wg, bg, dout, *, rb):
    T = u.shape[0]
    sp = _s5_specs(T, rb, rev=True)

    def body(u_ref, bre_ref, bim_ref, ar_ref, ai_ref, s0r_ref, s0i_ref, cre_ref, cim_ref, cret_ref, cimt_ref,
             d_ref, wg_ref, bg_ref, do_ref, dy_ref, glr_ref, gli_ref, dcre_ref, dcim_ref, dd_ref, dwg_ref, dbg_ref,
             xr, xi, dr, di, sr, si):
        @pl.when(pl.program_id(0) == 0)
        def _():
            for r in (glr_ref, gli_ref, dcre_ref, dcim_ref, dd_ref, dwg_ref, dbg_ref):
                r[...] = jnp.zeros_like(r)

        sr[...] = s0r_ref[0]
        si[...] = s0i_ref[0]
        _s5_bu(u_ref, bre_ref, bim_ref, xr, xi)
        _s5_scan(xr, xi, sr, si, ar_ref, ai_ref, rb // S5_SEG, True)
        uv = u_ref[...]
        y = _s5_y(xr, xi, u_ref, cre_ref, cim_ref, d_ref)
        z, gz = _gelu_parts(y)
        v = _bdot(z, wg_ref[...]) + bg_ref[...]
        sg = jax.nn.sigmoid(v)
        dov = do_ref[...].astype(f32)
        dv = dov * z * sg * (1.0 - sg)
        dz = dov * sg + _bdot(dv, wg_ref[...], NT)
        dy = dz * gz
        dy_ref[...] = dy
        dwg_ref[...] += _bdot(z, dv, TN)
        dbg_ref[...] += jnp.sum(dv, axis=0, keepdims=True)
        dd_ref[...] += jnp.sum(dy * uv, axis=0, keepdims=True)
        for k in range(S5_UB):
            cs = slice(k * S5_LC, (k + 1) * S5_LC)
            dyk = dy[:, k * LANES:(k + 1) * LANES]
            dcre_ref[k] += _bdot(xr[:, cs], dyk, TN)
            dcim_ref[k] -= _bdot(xi[:, cs], dyk, TN)
            dr[:, cs] = _bdot(dyk, cret_ref[k])
            di[:, cs] = -_bdot(dyk, cimt_ref[k])
        _s5_rscan(dr, di, None, None, None, None, glr_ref, gli_ref, None, None, ar_ref, ai_ref, rb // S5_SEG)

    wspec = pl.BlockSpec((4 * LANES, 4 * LANES), lambda i: (0, 0))
    return pl.pallas_call(
        body, name="s5_bwd_a", grid=(sp["nblk"],),
        in_specs=[sp["tok"], sp["bmat"], sp["bmat"], sp["avec"], sp["avec"], sp["s0"], sp["s0"], sp["cmat"], sp["cmat"],
                  sp["bmat"], sp["bmat"], sp["cvec"], wspec, sp["cvec"], sp["tok"]],
        out_specs=[sp["tok"], sp["seg"], sp["seg"], sp["cmat"], sp["cmat"], sp["cvec"], wspec, sp["cvec"]],
        out_shape=[jax.ShapeDtypeStruct((T, 4 * LANES), f32)] + [jax.ShapeDtypeStruct((S5_SEG, S5_N), f32)] * 2
        + [jax.ShapeDtypeStruct((S5_UB, S5_LC, LANES), f32)] * 2
        + [jax.ShapeDtypeStruct((1, 4 * LANES), f32), jax.ShapeDtypeStruct((4 * LANES, 4 * LANES), f32),
           jax.ShapeDtypeStruct((1, 4 * LANES), f32)],
        scratch_shapes=[pltpu.VMEM((rb, S5_N), f32)] * 4 + [pltpu.VMEM((S5_SEG, S5_N), f32)] * 2,
        compiler_params=_cp("arbitrary"),
    )(u, bre, bim, ar, ai, s0r, s0i, cre, cim, cret, cimt, dsk, wg, bg, dout)


def _s5_bwd_b(u, bre, bim, bret, bimt, ar, ai, s0r, s0i, glr, gli, cret, cimt, dsk, dy, *, rb):
    T = u.shape[0]
    sp = _s5_specs(T, rb, rev=True)
    seg_len = T // S5_SEG
    nblk = sp["nblk"]

    def body(u_ref, bre_ref, bim_ref, bret_ref, bimt_ref, ar_ref, ai_ref, s0r_ref, s0i_ref, glr_ref, gli_ref,
             cret_ref, cimt_ref, d_ref, dy_ref, du_ref, dbre_ref, dbim_ref, dar_ref, dai_ref,
             xr, xi, dr, di, sr, si, gr, gi, acc_r, acc_i):
        @pl.when(pl.program_id(0) == 0)
        def _():
            x_r, x_i = _s5_seg_carry(glr_ref[...], gli_ref[...], ar_ref[...], ai_ref[...], seg_len, True)
            gr[...] = x_r
            gi[...] = x_i
            acc_r[...] = jnp.zeros_like(acc_r)
            acc_i[...] = jnp.zeros_like(acc_i)
            dbre_ref[...] = jnp.zeros_like(dbre_ref)
            dbim_ref[...] = jnp.zeros_like(dbim_ref)

        sr[...] = s0r_ref[0]
        si[...] = s0i_ref[0]
        _s5_bu(u_ref, bre_ref, bim_ref, xr, xi)
        _s5_scan(xr, xi, sr, si, ar_ref, ai_ref, rb // S5_SEG, True)
        dy = dy_ref[...]
        for k in range(S5_UB):
            cs = slice(k * S5_LC, (k + 1) * S5_LC)
            dyk = dy[:, k * LANES:(k + 1) * LANES]
            dr[:, cs] = _bdot(dyk, cret_ref[k])
            di[:, cs] = -_bdot(dyk, cimt_ref[k])
        sr[...] = s0r_ref[0]
        si[...] = s0i_ref[0]
        _s5_rscan(dr, di, xr, xi, sr, si, gr, gi, acc_r, acc_i, ar_ref, ai_ref, rb // S5_SEG)
        dus = []
        for k in range(S5_UB):
            cs = slice(k * S5_LC, (k + 1) * S5_LC)
            uk = u_ref[:, k * LANES:(k + 1) * LANES]
            dbre_ref[k] += _bdot(uk, dr[:, cs], TN)
            dbim_ref[k] += _bdot(uk, di[:, cs], TN)
            dus.append(_bdot(dr[:, cs], bret_ref[k]) + _bdot(di[:, cs], bimt_ref[k]))
        du_ref[...] = (jnp.concatenate(dus, axis=1) + d_ref[...] * dy).astype(bf16)

        @pl.when(pl.program_id(0) == nblk - 1)
        def _():
            dar_ref[...] = jnp.sum(acc_r[...], axis=0, keepdims=True)
            dai_ref[...] = jnp.sum(acc_i[...], axis=0, keepdims=True)

    return pl.pallas_call(
        body, name="s5_bwd_b", grid=(nblk,),
        in_specs=[sp["tok"], sp["bmat"], sp["bmat"], sp["cmat"], sp["cmat"], sp["avec"], sp["avec"], sp["s0"], sp["s0"],
                  sp["seg"], sp["seg"], sp["bmat"], sp["bmat"], sp["cvec"], sp["tok"]],
        out_specs=[sp["tok"], sp["bmat"], sp["bmat"], sp["avec"], sp["avec"]],
        out_shape=[jax.ShapeDtypeStruct((T, 4 * LANES), bf16)] + [jax.ShapeDtypeStruct((S5_UB, LANES, S5_LC), f32)] * 2
        + [jax.ShapeDtypeStruct((1, S5_N), f32)] * 2,
        scratch_shapes=[pltpu.VMEM((rb, S5_N), f32)] * 4 + [pltpu.VMEM((S5_SEG, S5_N), f32)] * 6,
        compiler_params=_cp("arbitrary"),
    )(u, bre, bim, bret, bimt, ar, ai, s0r, s0i, glr, gli, cret, cimt, dsk, dy)


def _blockdiag(w, transpose=False):
    if transpose:
        w = jnp.swapaxes(w, 1, 2)
    g, a, b = w.shape
    eye = jnp.eye(8, dtype=w.dtype)
    return jnp.einsum("kgab,gj->kgajb", w.reshape(4, 8, a, b), eye).reshape(4, 8 * a, 8 * b)


def _blockdiag_t(m, a, b):
    eye = jnp.eye(8, dtype=m.dtype)
    return jnp.einsum("kgajb,gj->kgab", m.reshape(4, 8, a, 8, b), eye).reshape(32, a, b)


ROT = MLA_ROPE // 2


def _rope_tables(positions):
    freqs = ROPE_THETA ** (-jnp.arange(0, MLA_ROPE, 2, dtype=f32) / MLA_ROPE)
    ang = positions.astype(f32)[:, None] * freqs
    cos, sin, z = jnp.cos(ang), jnp.sin(ang), jnp.zeros_like(ang)
    return (jnp.concatenate([cos, cos, z, z], axis=1), jnp.concatenate([-sin, z, z, z], axis=1),
            jnp.concatenate([z, sin, z, z], axis=1))


def _rot(x, c, sa, sb):
    return x * c + pltpu.roll(x, LANES - ROT, 1) * sa + pltpu.roll(x, ROT, 1) * sb


def _rot_t(dy, c, sa, sb):
    return dy * c + pltpu.roll(dy * sa, ROT, 1) + pltpu.roll(dy * sb, LANES - ROT, 1)


def _rms(xv, g):
    return xv * lax.rsqrt(jnp.mean(xv * xv, axis=-1, keepdims=True) + EPS) * g


QW, KVW = MLA_Q_RANK, MLA_KV_RANK
ODD_PAD = QW + KVW + LANES


def _mla_prep_fwd(proj, qg, kvg, tabs, *, tm=512):
    T = proj.shape[0]
    tm = _tile(T, tm)

    def body(p_ref, qg_ref, kvg_ref, c_ref, sa_ref, sb_ref, cq_ref, ckv_ref, kr_ref):
        cq_ref[...] = _rms(p_ref[:, :QW], qg_ref[...]).astype(bf16)
        ckv_ref[...] = _rms(p_ref[:, QW:QW + KVW], kvg_ref[...]).astype(bf16)
        kr_ref[...] = _rot(p_ref[:, QW + KVW:], c_ref[...], sa_ref[...], sb_ref[...]).astype(bf16)

    row = lambda w: pl.BlockSpec((tm, w), lambda i: (i, 0))
    vec = lambda w: pl.BlockSpec((1, w), lambda i: (0, 0))
    return pl.pallas_call(
        body, name="mla_prep_fwd", grid=(T // tm,),
        in_specs=[row(ODD_PAD), vec(QW), vec(KVW), row(LANES), row(LANES), row(LANES)],
        out_specs=[row(QW), row(KVW), row(LANES)],
        out_shape=[jax.ShapeDtypeStruct((T, QW), bf16), jax.ShapeDtypeStruct((T, KVW), bf16),
                   jax.ShapeDtypeStruct((T, LANES), bf16)],
        compiler_params=_cp("parallel"),
    )(proj, qg, kvg, *tabs)


def _mla_prep_bwd(proj, qg, kvg, tabs, dcqn, dckvn, dkr_heads, *, tm=512):
    T = proj.shape[0]
    tm = _tile(T, tm)

    def body(p_ref, qg_ref, kvg_ref, c_ref, sa_ref, sb_ref, dcq_ref, dckv_ref, dkr_ref, dp_ref, dqg_ref, dkvg_ref):
        dcq, dqg = _rms_bwd_math(p_ref[:, :QW], qg_ref[...], dcq_ref[...])
        dckv, dkvg = _rms_bwd_math(p_ref[:, QW:QW + KVW], kvg_ref[...], dckv_ref[...])
        dk = dkr_ref[:, :LANES]
        for h in range(1, MLA_HEADS):
            dk = dk + dkr_ref[:, h * LANES:(h + 1) * LANES]
        dkr = _rot_t(dk, c_ref[...], sa_ref[...], sb_ref[...])
        dp_ref[...] = jnp.concatenate([dcq, dckv, dkr], axis=1).astype(bf16)

        @pl.when(pl.program_id(0) == 0)
        def _():
            dqg_ref[...] = dqg
            dkvg_ref[...] = dkvg

        @pl.when(pl.program_id(0) > 0)
        def _():
            dqg_ref[...] += dqg
            dkvg_ref[...] += dkvg

    row = lambda w: pl.BlockSpec((tm, w), lambda i: (i, 0))
    vec = lambda w: pl.BlockSpec((1, w), lambda i: (0, 0))
    return pl.pallas_call(
        body, name="mla_prep_bwd", grid=(T // tm,),
        in_specs=[row(ODD_PAD), vec(QW), vec(KVW), row(LANES), row(LANES), row(LANES), row(QW), row(KVW),
                  row(MLA_HEADS * LANES)],
        out_specs=[row(ODD_PAD), vec(QW), vec(KVW)],
        out_shape=[jax.ShapeDtypeStruct((T, ODD_PAD), bf16), jax.ShapeDtypeStruct((1, QW), f32),
                   jax.ShapeDtypeStruct((1, KVW), f32)],
        compiler_params=_cp("arbitrary"),
    )(proj, qg, kvg, *tabs, dcqn, dckvn, dkr_heads)


HQ = 2 * LANES
QK_SCALE = MLA_QK ** -0.5


def _q_post(q, tabs, *, transpose, name, tm=512):
    T = q.shape[0]
    tm = _tile(T, tm)

    def body(q_ref, c_ref, sa_ref, sb_ref, o_ref):
        c, sa, sb = c_ref[...], sa_ref[...], sb_ref[...]
        for h in range(MLA_HEADS):
            nope, rope = pl.ds(h * HQ, LANES), pl.ds(h * HQ + LANES, LANES)
            o_ref[:, nope] = (q_ref[:, nope].astype(f32) * QK_SCALE).astype(bf16)
            o_ref[:, rope] = ((_rot_t if transpose else _rot)(q_ref[:, rope].astype(f32), c, sa, sb) * QK_SCALE).astype(bf16)

    tab = pl.BlockSpec((tm, LANES), lambda i: (i, 0))
    blk = pl.BlockSpec((tm, MLA_HEADS * HQ), lambda i: (i, 0))
    return pl.pallas_call(
        body, name=name, grid=(T // tm,), in_specs=[blk, tab, tab, tab], out_specs=blk,
        out_shape=jax.ShapeDtypeStruct(q.shape, bf16), compiler_params=_cp("parallel"),
    )(q, *tabs)


def _causal_mask(i, j, tq, tk):
    r = lax.broadcasted_iota(jnp.int32, (tq, tk), 0) + i * tq
    c = lax.broadcasted_iota(jnp.int32, (tq, tk), 1) + j * tk
    return c <= r


FLASH_PARTS = 4


def _flash_fwd(q, kv, kr, *, tq=1024, tk=1024):
    T = q.shape[0]
    tq = _tile(T, tq)
    tk = _tile(tq, tk)
    per = tq // tk
    H = MLA_HEADS

    def body(q_ref, kn_ref, v_ref, kr_ref, o_ref, lse_ref, m_s, acc):
        i, j = pl.program_id(1), pl.program_id(2)
        last = (i + 1) * per - 1

        @pl.when(j == 0)
        def _():
            m_s[...] = jnp.full_like(m_s, -jnp.inf)
            acc[...] = jnp.zeros_like(acc)

        def step(masked):
            k = jnp.concatenate([kn_ref[...], kr_ref[...]], axis=1)
            v1 = jnp.concatenate([v_ref[...], jnp.ones((tk, LANES), bf16)], axis=1)
            mask = _causal_mask(i, j, tq, tk) if masked else None
            for part in range(FLASH_PARTS):
                rows = pl.ds(part * (tq // FLASH_PARTS), tq // FLASH_PARTS)
                s = _dot(q_ref[rows, :], k, NT)
                if masked:
                    s = jnp.where(mask[part * (tq // FLASH_PARTS):(part + 1) * (tq // FLASH_PARTS)], s, -jnp.inf)
                m_new = jnp.maximum(m_s[rows, :], jnp.max(s, axis=-1, keepdims=True))
                alpha = jnp.exp(m_s[rows, :] - m_new)
                p = jnp.exp((s - m_new).astype(bf16))
                acc[rows, :] = alpha * acc[rows, :] + _dot(p, v1)
                m_s[rows, :] = m_new

        pl.when(j < i * per)(functools.partial(step, False))
        pl.when((j >= i * per) & (j <= last))(functools.partial(step, True))

        @pl.when(j == last)
        def _():
            l = acc[:, LANES:]
            o_ref[...] = (acc[:, :LANES] / l).astype(bf16)
            lse_ref[0] = m_s[...] + jnp.log(jnp.max(l, axis=-1, keepdims=True))

    kj = lambda i, j: jnp.minimum(j, (i + 1) * per - 1)
    kblk = lambda off: pl.BlockSpec((tk, LANES), lambda h, i, j: (kj(i, j), 2 * h + off))
    return pl.pallas_call(
        body, name="flash_fwd", grid=(H, T // tq, T // tk),
        in_specs=[pl.BlockSpec((tq, HQ), lambda h, i, j: (i, h)), kblk(0), kblk(1),
                  pl.BlockSpec((tk, LANES), lambda h, i, j: (kj(i, j), 0))],
        out_specs=[pl.BlockSpec((tq, LANES), lambda h, i, j: (i, h)), pl.BlockSpec((1, tq, 1), lambda h, i, j: (h, i, 0))],
        out_shape=[jax.ShapeDtypeStruct((T, H * LANES), bf16), jax.ShapeDtypeStruct((H, T, 1), f32)],
        scratch_shapes=[pltpu.VMEM((tq, 1), f32), pltpu.VMEM((tq, 2 * LANES), f32)],
        compiler_params=_cp("parallel", "parallel", "arbitrary"),
    )(q, kv, kv, kr)


def _flash_bwd(q, kv, kr, o, do, lse, *, tb=1024):
    T = q.shape[0]
    tb = _tile(T, tb)
    nb = T // tb
    H = MLA_HEADS

    def body(q_ref, kn_ref, v_ref, kr_ref, o_ref, do_ref, lse_ref, dkv_ref, dkr_ref, dq_ref, dk_acc, dv_acc):
        j, ii = pl.program_id(1), pl.program_id(2)
        i = jnp.maximum(ii, j)

        @pl.when((j == 0) & (ii == 0))
        def _():
            dq_ref[...] = jnp.zeros_like(dq_ref)

        @pl.when(ii == 0)
        def _():
            dk_acc[...] = jnp.zeros_like(dk_acc)
            dv_acc[...] = jnp.zeros_like(dv_acc)

        def step(masked):
            k = jnp.concatenate([kn_ref[...], kr_ref[...]], axis=1)
            p = jnp.exp((_dot(q_ref[...], k, NT) - lse_ref[0]).astype(bf16))
            if masked:
                p = jnp.where(_causal_mask(i, j, tb, tb), p, jnp.zeros_like(p))
            delta = jnp.sum(o_ref[...].astype(f32) * do_ref[...], axis=-1, keepdims=True)
            ds = p * (_bdot(do_ref[...], v_ref[...], NT) - delta).astype(bf16)
            dv_acc[...] += _bdot(p, do_ref[...], TN)
            dk_acc[...] += _bdot(ds, q_ref[...], TN)
            dq_ref[pl.ds(pl.multiple_of(i * tb, tb), tb), :] += _bdot(ds, k)

        pl.when(ii > j)(functools.partial(step, False))
        pl.when(ii == j)(functools.partial(step, True))

        @pl.when(ii == nb - 1)
        def _():
            dkv_ref[...] = jnp.concatenate([dk_acc[:, :LANES], dv_acc[...]], axis=1).astype(bf16)
            dkr_ref[...] = dk_acc[:, LANES:]

    qi = lambda h, j, i: jnp.maximum(i, j)
    kblk = lambda off: pl.BlockSpec((tb, LANES), lambda h, j, i: (j, 2 * h + off))
    vec = pl.BlockSpec((1, tb, 1), lambda h, j, i: (h, qi(h, j, i), 0))
    qblk = pl.BlockSpec((tb, LANES), lambda h, j, i: (qi(h, j, i), h))
    return pl.pallas_call(
        body, name="flash_bwd", grid=(H, nb, nb),
        in_specs=[pl.BlockSpec((tb, HQ), lambda h, j, i: (qi(h, j, i), h)), kblk(0), kblk(1),
                  pl.BlockSpec((tb, LANES), lambda h, j, i: (j, 0)), qblk, qblk, vec],
        out_specs=[pl.BlockSpec((tb, HQ), lambda h, j, i: (j, h)), pl.BlockSpec((tb, LANES), lambda h, j, i: (j, h)),
                   pl.BlockSpec((T, HQ), lambda h, j, i: (0, h))],
        out_shape=[jax.ShapeDtypeStruct((T, H * HQ), bf16), jax.ShapeDtypeStruct((T, H * LANES), f32),
                   jax.ShapeDtypeStruct((T, H * HQ), f32)],
        scratch_shapes=[pltpu.VMEM((tb, HQ), f32), pltpu.VMEM((tb, LANES), f32)],
        compiler_params=_cp("parallel", "arbitrary", "arbitrary"),
    )(q, kv, kv, kr, o, do, lse)


HBM_SPEC = pl.BlockSpec(memory_space=pltpu.HBM)
N_CHIPS = 4
N_DEV = 8

BIG = {"even_w_in": 1, "s5_w_glu": 0, "even_w_out": 0, "odd_w_in": 0, "mla_w_uq": 1, "mla_w_ukv": 1, "odd_w_out": 0,
       "ffn_w_in": 2, "ffn_w_out": 1}
LAYERED = ("ffn_w_in", "ffn_w_out")
GROUPS = {"even_in": ("even_w_in",), "even_rest": ("s5_w_glu", "even_w_out"), "ffn0": LAYERED,
          "odd": ("odd_w_in", "mla_w_uq", "mla_w_ukv", "odd_w_out"), "ffn1": LAYERED}
GROUP_LAYER = {"ffn0": 0, "ffn1": 1}


def _place():
    x, y, c = lax.axis_index("x"), lax.axis_index("y"), lax.axis_index("c")
    chips = [(1 - x, y), (x, 1 - y), (1 - x, 1 - y)]
    return x, y, c, chips


def _slab(ref, axis, k, size):
    start = pl.multiple_of(k * size, size if axis == 0 else LANES)
    idx = [slice(None)] * len(ref.shape)
    idx[axis] = pl.ds(start, size)
    return ref.at[tuple(idx)]


SEM_SPEC = pl.BlockSpec(memory_space=pltpu.SEMAPHORE)
ANY_SPEC = pl.BlockSpec(memory_space=pl.ANY)
EFFECT = pltpu.SideEffectType.DATAFLOW_SIDE_EFFECTING


def _hbm(a):
    return pltpu.with_memory_space_constraint(a, pltpu.HBM)


class _Gather:
    copies = 3

    def __init__(self, axis, size):
        self.axis, self.size = axis, size

    def view(self, land, kk):
        return _slab(land, self.axis, kk, self.size)

    def own(self, land, place):
        return self.view(land, 2 * place[0] + place[1])

    def sends(self, src, land, place):
        x, y, c, chips = place
        return [(self.own(land, place) if src is None else src, self.own(land, place), (*chip, c)) for chip in chips]

    def recvs(self, land, place):
        return [self.view(land, 2 * cx + cy) for cx, cy in place[3]]


class _Scatter:
    copies = 3

    def __init__(self, axis, size, layer=None):
        self.axis, self.size, self.layer = axis, size, layer

    def row(self, land, j):
        return land.at[j] if self.layer is None else land.at[j, self.layer]

    def sends(self, src, land, place):
        c, chips = place[2], place[3]
        return [(_slab(src, self.axis, 2 * cx + cy, self.size), self.row(land, j), (cx, cy, c))
                for j, (cx, cy) in enumerate(chips)]

    def recvs(self, land, place):
        return [self.row(land, j) for j in range(3)]


class _Sibling:
    copies = 1

    def sends(self, src, land, place):
        x, y, c, _ = place
        return [(src, land, (x, y, 1 - c))]

    def recvs(self, land, place):
        return [land]


class _ToAll:
    copies = N_DEV - 1

    def __init__(self, size):
        self.size = size

    def sends(self, src, land, place):
        x, y, c, _ = place
        flip = lambda v, bit: 1 - v if bit else v
        own = _slab(land, 0, 4 * x + 2 * y + c, self.size)
        return [(own, own, (flip(x, m & 4), flip(y, m & 2), flip(c, m & 1))) for m in range(1, N_DEV)]

    def recvs(self, land, place):
        x, y, c, _ = place
        d = 4 * x + 2 * y + c
        return [_slab(land, 0, d ^ m, self.size) for m in range(1, N_DEV)]


def _unique(arrays):
    out, index = [], {}
    for a in arrays:
        if a is not None and id(a) not in index:
            index[id(a)] = len(out)
            out.append(a)
    return out, index


def _sem_base(routes):
    base = [0]
    for r in routes:
        base.append(base[-1] + r.copies)
    return base


def _push_start(name, items):
    n = len(items)
    base = _sem_base([it[0] for it in items])
    arrays, index = _unique([it[1] for it in items] + [it[2] for it in items])
    na = len(arrays)

    def body(*refs):
        arr, send, recv, token = refs[:na], refs[na], refs[na + 1], refs[-1]
        place = _place()
        for i, (route, src, land) in enumerate(items):
            s_ref = None if src is None else arr[index[id(src)]]
            for j, (s, d, dev) in enumerate(route.sends(s_ref, arr[index[id(land)]], place)):
                pltpu.make_async_remote_copy(src_ref=s, dst_ref=d, send_sem=send.at[base[i] + j], recv_sem=recv.at[base[i] + j],
                                             device_id=dev, device_id_type=MESH).start()
        token[...] = jnp.zeros_like(token)

    res = pl.pallas_call(
        body, name=name,
        out_shape=[pltpu.SemaphoreType.DMA((base[-1],)), pltpu.SemaphoreType.DMA((base[-1],))]
        + [pltpu.HBM(a.shape, a.dtype) for a in arrays] + [jax.ShapeDtypeStruct((SUBLANES, LANES), f32)],
        in_specs=[HBM_SPEC] * na, out_specs=[SEM_SPEC, SEM_SPEC] + [HBM_SPEC] * na + [pl.BlockSpec(memory_space=pltpu.VMEM)],
        input_output_aliases={i: 2 + i for i in range(na)},
        compiler_params=pltpu.CompilerParams(has_side_effects=EFFECT),
    )(*[_hbm(a) for a in arrays])
    thru = lambda a: None if a is None else res[2 + index[id(a)]]
    return (res[0], res[1]), [thru(it[1]) for it in items], [thru(it[2]) for it in items], res[-1]


def _push_wait(name, groups, after, with_srcs=False):
    arrays, index = _unique([a for _, _, srcs, lands in groups for a in list(srcs) + list(lands)])
    na, ng = len(arrays), len(groups)

    def body(*refs):
        arr, sems = refs[:na], refs[na:na + 2 * ng]
        place = _place()
        for g, (routes, _, srcs, lands) in enumerate(groups):
            send, recv = sems[2 * g], sems[2 * g + 1]
            base = _sem_base(routes)
            for i, route in enumerate(routes):
                src, land = None if srcs[i] is None else arr[index[id(srcs[i])]], arr[index[id(lands[i])]]
                for j, ((s, d, dev), mine) in enumerate(zip(route.sends(src, land, place), route.recvs(land, place))):
                    cp = pltpu.make_async_remote_copy(src_ref=s, dst_ref=mine, send_sem=send.at[base[i] + j],
                                                      recv_sem=recv.at[base[i] + j], device_id=dev,
                                                      device_id_type=MESH)
                    cp.wait_send()
                    cp.wait_recv()

    sem_args = [s for g in groups for s in g[1]]
    res = pl.pallas_call(
        body, name=name, out_shape=[pltpu.HBM(a.shape, a.dtype) for a in arrays],
        in_specs=[HBM_SPEC] * na + [SEM_SPEC] * (2 * ng) + [ANY_SPEC] * len(after), out_specs=[HBM_SPEC] * na,
        input_output_aliases={i: i for i in range(na)},
        compiler_params=pltpu.CompilerParams(has_side_effects=EFFECT),
    )(*arrays, *sem_args, *after)
    if with_srcs:
        return [([res[index[id(a)]] for a in g[2]], [res[index[id(a)]] for a in g[3]]) for g in groups]
    return [[res[index[id(a)]] for a in g[3]] for g in groups]


def _place_slab(block, axis, slabs, idx, dtype, *, name):
    R, C = block.shape
    tm = _rows(R, C)
    nr = R // tm
    out_map = (lambda i, k: (i, k[0])) if axis == 1 else (lambda i, k: (k[0] * nr + i, 0))

    def body(k_ref, x_ref, o_ref):
        o_ref[...] = x_ref[...].astype(dtype)

    full = (R, C * slabs) if axis == 1 else (R * slabs, C)
    return pl.pallas_call(
        body, name=name, out_shape=jax.ShapeDtypeStruct(full, dtype),
        grid_spec=pltpu.PrefetchScalarGridSpec(
            num_scalar_prefetch=1, grid=(nr,), in_specs=[pl.BlockSpec((tm, C), lambda i, k: (i, 0))],
            out_specs=pl.BlockSpec((tm, C), out_map)),
        compiler_params=_cp("parallel"),
    )(idx, block)


ELEMENTWISE_BLOCK_BYTES = 1 << 20


def _rows(r, c):
    for t in (512, 256, 128, 64, 32, 16, 8):
        if r % t == 0 and t * c * 4 <= ELEMENTWISE_BLOCK_BYTES:
            return t
    return r


def _sum4(owns, axis, recv, kidx, *, name, dep=None):
    L = len(owns)
    R, C = recv.shape[2:]
    tm = _rows(R, C)
    nr = R // tm
    deps = [] if dep is None else [dep]

    def body(k_ref, *refs):
        own_refs, r_ref, out_ref = refs[:L], refs[L], refs[-1]
        for li in range(L):
            @pl.when(pl.program_id(0) == li)
            def _(o_ref=own_refs[li]):
                out_ref[...] = ((o_ref[...] + r_ref[0, 0].astype(f32)) + r_ref[1, 0].astype(f32)) + r_ref[2, 0].astype(f32)

    own_map = (lambda l, i, k: (i, k[0])) if axis == 1 else (lambda l, i, k: (k[0] * nr + i, 0))
    return pl.pallas_call(
        body, name=name, out_shape=jax.ShapeDtypeStruct((L * R, C), f32),
        grid_spec=pltpu.PrefetchScalarGridSpec(
            num_scalar_prefetch=1, grid=(L, nr),
            in_specs=[pl.BlockSpec((tm, C), own_map)] * L + [pl.BlockSpec((3, 1, tm, C), lambda l, i, k: (0, l, i, 0))]
            + [pl.BlockSpec(memory_space=pl.ANY)] * len(deps),
            out_specs=pl.BlockSpec((tm, C), lambda l, i, k: (l * nr + i, 0))),
        compiler_params=_cp("parallel", "parallel"),
    )(kidx, *owns, recv, *deps)


def _adamw(w, m, v, parts, *, name):
    R, C = w.shape
    tm = _rows(R, C)
    npart = len(parts)

    def body(*refs):
        w_ref, m_ref, v_ref = refs[:3]
        g_ref, d_ref, m2_ref, v2_ref = refs[3 + npart:]
        g = refs[3][...]
        for p_ref in refs[4:3 + npart]:
            g = g + p_ref[...]
        g_ref[...] = g
        d_ref[...], m2_ref[...], v2_ref[...] = _adam_math(w_ref[...], m_ref[...], v_ref[...], g)

    blk = pl.BlockSpec((tm, C), lambda i: (i, 0))
    return pl.pallas_call(
        body, name=name, grid=(R // tm,),
        in_specs=[blk] * (3 + npart), out_specs=[blk] * 4,
        out_shape=[jax.ShapeDtypeStruct((R, C), f32)] * 4, compiler_params=_cp("parallel"),
    )(w, m, v, *parts)


def _adam_math(w, m, v, g):
    m2 = ADAM_B1 * m + (1.0 - ADAM_B1) * g
    v2 = ADAM_B2 * v + (1.0 - ADAM_B2) * (g * g)
    m_hat = m2 / (1.0 - ADAM_B1 ** ADAM_STEP)
    v_hat = v2 / (1.0 - ADAM_B2 ** ADAM_STEP)
    return -ADAM_LR * (m_hat / (jnp.sqrt(v_hat) + ADAM_EPS) + ADAM_WD * w), m2, v2


def _adamw_small(landed, w, m, v, kidx, ra, rb):
    rs = ra + N_CHIPS * rb

    def body(k_ref, l_ref, w_ref, m_ref, v_ref, g_ref, d_ref, m2_ref, v2_ref):
        mine = pl.multiple_of(ra + k_ref[0] * rb, SUBLANES)
        for lo, n, off in ((0, ra, 0), (ra, rb, mine)):
            g = l_ref[pl.ds(off, n), :]
            for d in range(1, N_DEV):
                g = g + l_ref[pl.ds(d * rs + off, n), :]
            rows = pl.ds(lo, n)
            delta, m2, v2 = _adam_math(w_ref[rows, :], m_ref[rows, :], v_ref[rows, :], g)
            g_ref[rows, :] = g
            d_ref[rows, :] = delta
            m2_ref[rows, :] = m2
            v2_ref[rows, :] = v2

    vmem = pl.BlockSpec(memory_space=pltpu.VMEM)
    return pl.pallas_call(
        body, name="adamw_small", out_shape=[jax.ShapeDtypeStruct(w.shape, f32)] * 4,
        grid_spec=pltpu.PrefetchScalarGridSpec(num_scalar_prefetch=1, grid=(), in_specs=[vmem] * 4, out_specs=[vmem] * 4),
        compiler_params=_cp(),
    )(kidx, landed, w, m, v)


def _pad_odd(w):
    return jnp.pad(w, ((0, 0), (0, ODD_PAD - w.shape[1])))


def _uq_cat(w):
    r = w.shape[0]
    return jnp.pad(w.reshape(r, MLA_HEADS, MLA_QK), ((0, 0), (0, 0), (0, HQ - MLA_QK))).reshape(r, MLA_HEADS * HQ)


def _uq_uncat(w):
    r = w.shape[0]
    return w.reshape(r, MLA_HEADS, HQ)[:, :, :MLA_QK].reshape(r, MLA_HEADS * MLA_QK)


def _to_segments(v):
    T, C = v.shape
    return v.reshape(S5_SEG, T // S5_SEG, C).transpose(1, 0, 2).reshape(T, C)


def _from_segments(v):
    T, C = v.shape
    return v.reshape(T // S5_SEG, S5_SEG, C).transpose(1, 0, 2).reshape(T, C)


def _s5_rb(T):
    return min(512, T)


def _ffn_fwd(h, hn, w_in, cw, cb, w_out, tag, next_g=None):
    au = _mm(hn, w_in, out_dtype=bf16, name=f"ffn{tag}_in", tn=1408)
    z = _ffn_mid_fwd(au, cw, cb, name=f"ffn{tag}_mid")
    return _mm(z, w_out, res=h, norm_g=next_g, name=f"ffn{tag}_out", tm=512, tk=D_FF), (hn, au, z)


def _ffn_bwd(h, g, w_in, cw, cb, w_out, saved, dh, tag, dep=None):
    hn, au, z = saved
    dz = _mm(dh, w_out, tb=True, out_dtype=bf16, name=f"ffn{tag}_dz", tn=1408, dep=dep)
    dw_out = _mm(z, dh, ta=True, also_bf16=True, name=f"ffn{tag}_dwout", tm=1408)
    dau, dcw, dcb = _ffn_mid_bwd(au, cw, cb, dz, name=f"ffn{tag}_dmid")
    dh_in, dg = _mm(dau, w_in, tb=True, res=dh, norm_bwd=(h, g), name=f"ffn{tag}_dhn", tk=1408)
    dw_in = _mm(hn, dau, ta=True, also_bf16=True, name=f"ffn{tag}_dwin", tn=1408)
    return dh_in, dg, dw_in, dcw, dcb, dw_out


def _local_step(x, positions, target, get_w, P, put_g):
    T = x.shape[0]
    rb = _s5_rb(T)
    row = lambda v: v.reshape(1, -1)
    g_mix, g_ffn = P["norm_mix_g"], P["norm_ffn_g"]
    lbl, hng = P["hgrn_lb_logits"], P["hgrn_norm_g"]
    dsk, bg = P["s5_d"], P["s5_b_glu"]
    qg, kvg = P["mla_q_norm_g"], P["mla_kv_norm_g"]
    cw, cb = P["ffn_conv_w"], P["ffn_conv_b"]

    col = lambda v: v.reshape(S5_N, 1)
    disc_in = (col(P["s5_a_re"]), col(P["s5_a_im"]), col(jnp.repeat(P["s5_log_dt"].reshape(S5_GROUPS), S5_STATE)),
               P["s5_b_re"].reshape(S5_N, S5_GROUP), P["s5_b_im"].reshape(S5_N, S5_GROUP))
    abr, abi, bbr, bbi = _s5_disc_fwd(*disc_in)
    ar, ai = abr.reshape(1, S5_N), abi.reshape(1, S5_N)
    bbr3, bbi3 = bbr.reshape(S5_GROUPS, S5_STATE, S5_GROUP), bbi.reshape(S5_GROUPS, S5_STATE, S5_GROUP)
    bre, bim = _blockdiag(bbr3, True).astype(bf16), _blockdiag(bbi3, True).astype(bf16)
    bret, bimt = _blockdiag(bbr3).astype(bf16), _blockdiag(bbi3).astype(bf16)
    c_re, c_im = P["s5_c_re"].reshape(S5_GROUPS, S5_GROUP, S5_STATE), P["s5_c_im"].reshape(S5_GROUPS, S5_GROUP, S5_STATE)
    cre, cim = _blockdiag(c_re, True).astype(bf16), _blockdiag(c_im, True).astype(bf16)
    cret, cimt = _blockdiag(c_re).astype(bf16), _blockdiag(c_im).astype(bf16)

    hn0 = _rms_fwd(x, g_mix[0:1], name="mix0_norm")
    We = get_w("even_in", hn0)
    proj_e = _mm(hn0, We["even_w_in"], name="even_in", tn=1280)
    Wr = get_w("even_rest", proj_e)
    ya, states = _hgrn_fwd(proj_e, lbl, hng)
    u_seg = _to_segments(proj_e[:, 4 * 512:])
    fr, fi = _s5_final(u_seg, bre, bim, ar, ai, rb=rb)
    yb_seg, s0r, s0i = _s5_fwd(u_seg, bre, bim, ar, ai, fr, fi, cre, cim, dsk, Wr["s5_w_glu"], bg, rb=rb)
    ycat = jnp.concatenate([ya, _from_segments(yb_seg)], axis=1)
    h1, hnf0 = _mm(ycat, Wr["even_w_out"], res=x, norm_g=g_ffn[0:1], name="even_out")
    Wf0 = get_w("ffn0", h1)
    (h2, hn2), ffn0 = _ffn_fwd(h1, hnf0, Wf0["ffn_w_in"], cw[0], cb[0:1], Wf0["ffn_w_out"], 0, next_g=g_mix[1:2])

    tabs = _rope_tables(positions)
    Wo = get_w("odd", hn2)
    proj_o = _mm(hn2, Wo["odd_w_in"], name="odd_in")
    cqn, ckvn, kr = _mla_prep_fwd(proj_o, qg, kvg, tabs)
    q = _q_post(_mm(cqn, Wo["mla_w_uq"], name="mla_uq"), tabs, transpose=False, name="q_post")
    kvb = _mm(ckvn, Wo["mla_w_ukv"], out_dtype=bf16, name="mla_ukv")
    o, lse = _flash_fwd(q, kvb, kr)
    h3, hnf1 = _mm(o, Wo["odd_w_out"], res=h2, norm_g=g_ffn[1:2], name="odd_out")
    Wf1 = get_w("ffn1", h3)
    h4, ffn1 = _ffn_fwd(h3, hnf1, Wf1["ffn_w_in"], cw[1], cb[1:2], Wf1["ffn_w_out"], 1)
    loss, dh4, dg_final = _loss_head(h4, row(P["final_norm_g"]), target)

    dh3, dg_ffn1, dw_fin1, dcw1, dcb1, dw_fout1 = _ffn_bwd(
        h3, g_ffn[1:2], Wf1["ffn_w_in"], cw[1], cb[1:2], Wf1["ffn_w_out"], ffn1, dh4, 1)
    sent = put_g("ffn1", {"ffn_w_in": dw_fin1, "ffn_w_out": dw_fout1})
    do = _mm(dh3, Wo["odd_w_out"], tb=True, out_dtype=bf16, name="odd_do", dep=sent)
    dw_oout = _mm(o, dh3, ta=True, also_bf16=True, name="odd_dwout")
    dkv, dkr_h, dq = _flash_bwd(q, kvb, kr, o, do, lse)
    dq = _q_post(dq, tabs, transpose=True, name="dq_post")
    dw_uq = _mm(cqn, dq, ta=True, also_bf16=True, name="mla_dwuq")
    dcqn = _mm(dq, Wo["mla_w_uq"], tb=True, name="mla_dcq")
    dw_ukv = _mm(ckvn, dkv, ta=True, also_bf16=True, name="mla_dwukv")
    dckvn = _mm(dkv, Wo["mla_w_ukv"], tb=True, name="mla_dckv")
    dproj_o, dqg, dkvg = _mla_prep_bwd(proj_o, qg, kvg, tabs, dcqn, dckvn, dkr_h)
    dw_oin = _mm(hn2, dproj_o, ta=True, also_bf16=True, name="odd_dwin")
    sent = put_g("odd", {"odd_w_in": dw_oin, "mla_w_uq": dw_uq, "mla_w_ukv": dw_ukv, "odd_w_out": dw_oout})
    dh2, dg_mix1 = _mm(dproj_o, Wo["odd_w_in"], tb=True, res=dh3, norm_bwd=(h2, g_mix[1:2]), name="odd_dhn")

    dh1, dg_ffn0, dw_fin0, dcw0, dcb0, dw_fout0 = _ffn_bwd(
        h1, g_ffn[0:1], Wf0["ffn_w_in"], cw[0], cb[0:1], Wf0["ffn_w_out"], ffn0, dh2, 0, dep=sent)
    sent = put_g("ffn0", {"ffn_w_in": dw_fin0, "ffn_w_out": dw_fout0})
    dycat = _mm(dh1, Wr["even_w_out"], tb=True, name="even_dy", dep=sent)
    dw_eout = _mm(ycat, dh1, ta=True, also_bf16=True, name="even_dwout")
    dq_h, df_h, di_h, dg_h, dlbl, dhng = _hgrn_bwd(proj_e, lbl, hng, states, dycat)
    dyb_seg = _to_segments(dycat[:, 512:])
    dy_s5, glr, gli, dcre, dcim, dd, dwg, dbg = _s5_bwd_a(
        u_seg, bre, bim, ar, ai, s0r, s0i, cre, cim, cret, cimt, dsk, Wr["s5_w_glu"], bg, dyb_seg, rb=rb)
    du_seg, dbre, dbim, dar, dai = _s5_bwd_b(
        u_seg, bre, bim, bret, bimt, ar, ai, s0r, s0i, glr, gli, cret, cimt, dsk, dy_s5, rb=rb)
    dproj_e = jnp.concatenate([dq_h, df_h, di_h, dg_h, _from_segments(du_seg)], axis=1)
    dx, dg_mix0 = _mm(dproj_e, We["even_w_in"], tb=True, res=dh1, norm_bwd=(x, g_mix[0:1]), name="even_dhn", tk=1280)
    dw_ein = _mm(hn0, dproj_e, ta=True, also_bf16=True, name="even_dwin", tn=1280)

    unblk = lambda m, a, b: jnp.swapaxes(_blockdiag_t(m, a, b), 1, 2)
    dbbr = unblk(dbre, S5_GROUP, S5_STATE).reshape(S5_N, S5_GROUP)
    dbbi = unblk(dbim, S5_GROUP, S5_STATE).reshape(S5_N, S5_GROUP)
    d_ar, d_ai, d_ldt, d_br, d_bi = _s5_disc_bwd(*disc_in, (dar.reshape(S5_N, 1), dai.reshape(S5_N, 1), dbbr, dbbi))
    small = {
        "norm_mix_g": jnp.concatenate([dg_mix0, dg_mix1], axis=0),
        "norm_ffn_g": jnp.concatenate([dg_ffn0, dg_ffn1], axis=0),
        "final_norm_g": dg_final.reshape(-1),
        "hgrn_lb_logits": dlbl, "hgrn_norm_g": dhng,
        "s5_a_re": d_ar.reshape(1, S5_GROUPS, S5_STATE), "s5_a_im": d_ai.reshape(1, S5_GROUPS, S5_STATE),
        "s5_log_dt": d_ldt.reshape(S5_GROUPS, S5_STATE).sum(axis=1).reshape(1, S5_GROUPS),
        "s5_b_re": d_br.reshape(1, S5_GROUPS, S5_STATE, S5_GROUP), "s5_b_im": d_bi.reshape(1, S5_GROUPS, S5_STATE, S5_GROUP),
        "s5_c_re": unblk(dcre, S5_STATE, S5_GROUP).reshape(1, S5_GROUPS, S5_GROUP, S5_STATE),
        "s5_c_im": unblk(dcim, S5_STATE, S5_GROUP).reshape(1, S5_GROUPS, S5_GROUP, S5_STATE),
        "s5_d": dd, "s5_b_glu": dbg, "mla_q_norm_g": dqg, "mla_kv_norm_g": dkvg,
        "ffn_conv_w": jnp.stack([dcw0, dcw1]), "ffn_conv_b": jnp.concatenate([dcb0, dcb1], axis=0),
    }
    put_g("even", {"even_w_in": dw_ein, "s5_w_glu": (dwg, dwg.astype(bf16)), "even_w_out": dw_eout}, small)
    return loss, dx


WEIGHTS = ["norm_mix_g", "norm_ffn_g", "final_norm_g", "even_w_in", "hgrn_lb_logits", "hgrn_norm_g", "s5_a_re", "s5_a_im",
           "s5_log_dt", "s5_b_re", "s5_b_im", "s5_c_re", "s5_c_im", "s5_d", "s5_w_glu", "s5_b_glu", "even_w_out", "odd_w_in",
           "mla_q_norm_g", "mla_w_uq", "mla_kv_norm_g", "mla_w_ukv", "odd_w_out", "ffn_w_in", "ffn_conv_w", "ffn_conv_b",
           "ffn_w_out"]
SMALL_SHARDED = {"mla_q_norm_g": 1, "mla_kv_norm_g": 1, "ffn_conv_w": 2}
SMALL = [n for n in WEIGHTS if n not in BIG]
SMALL_REP = [n for n in SMALL if n not in SMALL_SHARDED]


def _pack_rows(shapes):
    n = sum(math.prod(s) for s in shapes)
    return -(-n // (SUBLANES * LANES)) * SUBLANES


def _pack(arrays, rows):
    flat = jnp.concatenate([a.reshape(-1) for a in arrays])
    return jnp.pad(flat, (0, rows * LANES - flat.shape[0])).reshape(rows, LANES)


def _unpack(block, shapes):
    flat, out, off = block.reshape(-1), [], 0
    for s in shapes:
        n = math.prod(s)
        out.append(flat[off:off + n].reshape(s))
        off += n
    return out


def kernel(x, positions, norm_mix_g, norm_ffn_g, final_norm_g, even_w_in, hgrn_lb_logits, hgrn_norm_g, s5_a_re, s5_a_im, s5_log_dt, s5_b_re, s5_b_im, s5_c_re, s5_c_im, s5_d, s5_w_glu, s5_b_glu, even_w_out, odd_w_in, mla_q_norm_g, mla_w_uq, mla_kv_norm_g, mla_w_ukv, odd_w_out, ffn_w_in, ffn_conv_w, ffn_conv_b, ffn_w_out, loss_target, m_norm_mix_g, m_norm_ffn_g, m_final_norm_g, m_even_w_in, m_hgrn_lb_logits, m_hgrn_norm_g, m_s5_a_re, m_s5_a_im, m_s5_log_dt, m_s5_b_re, m_s5_b_im, m_s5_c_re, m_s5_c_im, m_s5_d, m_s5_w_glu, m_s5_b_glu, m_even_w_out, m_odd_w_in, m_mla_q_norm_g, m_mla_w_uq, m_mla_kv_norm_g, m_mla_w_ukv, m_odd_w_out, m_ffn_w_in, m_ffn_conv_w, m_ffn_conv_b, m_ffn_w_out, v_norm_mix_g, v_norm_ffn_g, v_final_norm_g, v_even_w_in, v_hgrn_lb_logits, v_hgrn_norm_g, v_s5_a_re, v_s5_a_im, v_s5_log_dt, v_s5_b_re, v_s5_b_im, v_s5_c_re, v_s5_c_im, v_s5_d, v_s5_w_glu, v_s5_b_glu, v_even_w_out, v_odd_w_in, v_mla_q_norm_g, v_mla_w_uq, v_mla_kv_norm_g, v_mla_w_ukv, v_odd_w_out, v_ffn_w_in, v_ffn_conv_w, v_ffn_conv_b, v_ffn_w_out):
    args = dict(locals())
    w = {n: args[n] for n in WEIGHTS}
    m = {n: args["m_" + n] for n in WEIGHTS}
    v = {n: args["v_" + n] for n in WEIGHTS}
    k = 2 * lax.axis_index("x") + lax.axis_index("y")
    kidx = k.reshape(1).astype(jnp.int32)
    axis2d = lambda n: BIG[n] - (1 if n in LAYERED else 0)
    slab = lambda n: w[n].shape[1 + axis2d(n)]

    small_sh_shapes = [w[n].shape for n in SMALL_SHARDED]
    rb = _pack_rows(small_sh_shapes)
    items = {}
    for group, names in GROUPS.items():
        layer = GROUP_LAYER.get(group, 0)
        items[group] = [(_Gather(axis2d(n), slab(n)), None,
                         _place_slab(w[n][layer], axis2d(n), N_CHIPS, kidx, bf16, name=f"place_{n}_{layer}")) for n in names]
    items["even_in"].append((_Gather(0, rb), None,
                             _place_slab(_pack([w[n] for n in SMALL_SHARDED], rb), 0, N_CHIPS, kidx, f32, name="place_small")))
    gathers, tokens = {}, []
    for group in GROUPS:
        sems, srcs, lands, token = _push_start(f"gather_start_{group}", items[group])
        gathers[group] = ([it[0] for it in items[group]], sems, srcs, lands)
        tokens.append(token[0, 0])
    started = functools.reduce(jnp.add, tokens)

    def landed(group, after):
        return _push_wait(f"gather_wait_{group}", [gathers[group]], [after])[0]

    even = landed("even_in", (started + norm_mix_g[0, 0]).reshape(1))
    per_chip = [_unpack(even[-1][c * rb:(c + 1) * rb], small_sh_shapes) for c in range(N_CHIPS)]
    P = {n: w[n] for n in SMALL_REP}
    for i, (n, ax) in enumerate(SMALL_SHARDED.items()):
        P[n] = jnp.concatenate([per_chip[c][i] for c in range(N_CHIPS)], axis=ax)
    P["mla_q_norm_g"], P["mla_kv_norm_g"] = P["mla_q_norm_g"].reshape(1, -1), P["mla_kv_norm_g"].reshape(1, -1)
    fix_w = {"odd_w_in": _pad_odd, "mla_w_uq": _uq_cat}

    def get_w(group, after):
        full = even if group == "even_in" else landed(group, after)
        return {n: fix_w.get(n, lambda a: a)(a) for n, a in zip(GROUPS[group], full)}

    fix_g = {"odd_w_in": lambda g: g[:, :odd_w_in.shape[2]], "mla_w_uq": _uq_uncat}
    g32, scatters, land_now = {}, {}, {}
    ra = _pack_rows([w[n].shape for n in SMALL_REP])
    rs = ra + N_CHIPS * rb
    didx = (2 * kidx + lax.axis_index("c")).astype(jnp.int32)

    def put_g(group, grads, small=None):
        layer = GROUP_LAYER.get(group)
        routes, srcs, names = [], [], list(grads)
        for n in names:
            f = fix_g.get(n, lambda g: g)
            g32.setdefault(n, {})[layer or 0] = f(grads[n][0])
            routes.append(_Scatter(axis2d(n), slab(n), layer if n in LAYERED else None))
            srcs.append(f(grads[n][1]))
            if n not in land_now:
                land_now[n] = lax.empty((3,) + w[n].shape[0 if n in LAYERED else 1:], bf16)
        if small is not None:
            blocks = [_pack([small[n] for n in SMALL_REP], ra)]
            for chip in range(N_CHIPS):
                sl = lambda n, ax: lax.slice_in_dim(small[n].reshape(w[n].shape[:ax] + (-1,) + w[n].shape[ax + 1:]),
                                                    chip * w[n].shape[ax], (chip + 1) * w[n].shape[ax], axis=ax)
                blocks.append(_pack([sl(n, ax) for n, ax in SMALL_SHARDED.items()], rb))
            names.append("small")
            routes.append(_ToAll(rs))
            srcs.append(None)
            land_now["small"] = _place_slab(jnp.concatenate(blocks), 0, N_DEV, didx, f32, name="place_small_grads")
        sems, srcs, lands, token = _push_start(f"scatter_start_{group}", [(r, s, land_now[n]) for r, s, n in zip(routes, srcs, names)])
        land_now.update(zip(names, lands))
        scatters[group] = (routes, sems, srcs, names)
        sent.append(token)
        return token

    sent = []
    loss, dx = _local_step(x[0], positions[0], loss_target[0], get_w, P, put_g)
    sent_last = sent[-1]
    loss = lax.psum(loss[0, 0], ("x", "y", "c"))

    out = {}

    def arrive(tag, groups, after):
        waits = [(scatters[g][0], scatters[g][1], scatters[g][2], [land_now[n] for n in scatters[g][3]]) for g in groups]
        for g, lands in zip(groups, _push_wait(f"scatter_wait_{tag}", waits, after)):
            land_now.update(zip(scatters[g][3], lands))

    def cross(tag, names, dep=None):
        part = {}
        for n in names:
            recv = land_now[n] if n in LAYERED else land_now[n][:, None]
            part[n] = _sum4([g32[n][l] for l in sorted(g32[n])], axis2d(n), recv, kidx, name=f"sum4_{n}", dep=dep)
        items = [(_Sibling(), part[n], lax.empty(part[n].shape, f32)) for n in names]
        sems, srcs, lands, token = _push_start(f"swap_start_{tag}", items)
        return (names, part, ([it[0] for it in items], sems, srcs, lands)), token

    def update(tag, arrived, after):
        names, _, push = arrived
        mine, theirs = _push_wait(f"swap_wait_{tag}", [push], after, with_srcs=True)[0]
        part, other = dict(zip(names, mine)), dict(zip(names, theirs))
        done = []
        for n in names:
            C = part[n].shape[-1]
            res = _adamw(w[n].reshape(-1, C), m[n].reshape(-1, C), v[n].reshape(-1, C), [part[n], other[n]], name=f"adamw_{n}")
            out[n] = [r.reshape(w[n].shape) for r in res]
            done.append(res[0])
        return done

    arrive("a", ["ffn1", "odd", "ffn0"], [dx, sent_last])
    a1, token_a1 = cross("a1", ["ffn_w_in"])
    a2, token_a2 = cross("a2", ["ffn_w_out"] + list(GROUPS["odd"]), dep=token_a1)
    done = update("a2", a2, update("a1", a1, [token_a2]))
    arrive("b", ["even"], done)
    b, token_b = cross("b", list(GROUPS["even_in"]) + list(GROUPS["even_rest"]))

    order = SMALL_REP + list(SMALL_SHARDED)
    packed = lambda src: jnp.concatenate([_pack([src[n] for n in SMALL_REP], ra), _pack([src[n] for n in SMALL_SHARDED], rb)])
    res = _adamw_small(land_now["small"], packed(w), packed(m), packed(v), kidx, ra, rb)
    update("b", b, [res[0], token_b])
    for r in res:
        parts = _unpack(r[:ra], [w[n].shape for n in SMALL_REP]) + _unpack(r[ra:], small_sh_shapes)
        for n, a in zip(order, parts):
            out.setdefault(n, []).append(a)

    return (loss, dx[None], *[out[n][0] for n in WEIGHTS], *[out[n][1] for n in WEIGHTS],
            *[out[n][2] for n in WEIGHTS], *[out[n][3] for n in WEIGHTS])
```

```python
import functools
import math

import jax
import jax.numpy as jnp
from jax import lax
from jax.experimental import pallas as pl
from jax.experimental.pallas import tpu as pltpu

f32, bf16 = jnp.float32, jnp.bfloat16
EPS = 1e-6
LANES = 128
SUBLANES = 8
VMEM_BYTES = 48 * 1024 * 1024
HGRN_CHUNK = 64
HGRN_HEADS = 4
S5_GROUPS, S5_STATE, S5_GROUP = 32, 64, 16
S5_N = S5_GROUPS * S5_STATE
S5_SEG = SUBLANES
MLA_HEADS, MLA_NOPE, MLA_ROPE, MLA_V = 8, 128, 64, 128
MLA_QK = MLA_NOPE + MLA_ROPE
MLA_Q_RANK, MLA_KV_RANK = 384, 256
ROPE_THETA = 10000.0
D_FF = 2816
ADAM_LR, ADAM_B1, ADAM_B2, ADAM_EPS, ADAM_WD, ADAM_STEP = 0.001, 0.9, 0.999, 1e-08, 0.01, 10
MESH = pl.DeviceIdType.MESH
HI = lax.Precision.HIGHEST


def _cp(*dims):
    return pltpu.CompilerParams(dimension_semantics=dims if dims else None, vmem_limit_bytes=VMEM_BYTES)


def _tile(n, t):
    if n <= t:
        return n
    c = (t // LANES) * LANES
    while c >= LANES:
        if n % c == 0:
            return c
        c -= LANES
    return n


def _dot(a, b, dn=None, precision=None):
    if dn is None:
        dn = (((a.ndim - 1,), (0,)), ((), ()))
    return lax.dot_general(a, b, dn, preferred_element_type=f32, precision=precision)


NT = (((1,), (1,)), ((), ()))
TN = (((0,), (0,)), ((), ()))


def _bdot(a, b, dn=None):
    return _dot(a.astype(bf16), b.astype(bf16), dn)


MM_PARTS = 2


def _mm(a, b, *, name, ta=False, tb=False, out_dtype=f32, res=None, also_bf16=False, tm=1024, tn=1024, tk=1024, dep=None,
        norm_g=None, norm_bwd=None):
    halves = lambda s: (s[1], 2 * s[2]) if len(s) == 3 else s
    M, K = (a.shape[1], a.shape[0]) if ta else halves(a.shape)
    N = b.shape[0] if tb else halves(b.shape)[1]
    rows = norm_g is not None or norm_bwd is not None
    if rows:
        tm, tn, tk = 512, N, K
    tm, tn, tk = _tile(M, tm), _tile(N, tn), _tile(K, tk)
    both = rows and a.ndim == 3 and tb
    if a.ndim == 3 and not both:
        tk = _tile(K // 2, tk)
    if b.ndim == 3:
        tn = _tile(N // 2, tn)
    nk = K // tk
    parts = MM_PARTS if tm % (MM_PARTS * LANES) == 0 else 1
    dn = (((0 if ta else 1,), (1 if tb else 0,)), ((), ()))
    extra = [] if norm_bwd is None else list(norm_bwd)
    if norm_g is not None:
        extra.append(norm_g)

    def body(*refs):
        a_ref, b_ref = refs[0], refs[1]
        r_ref = refs[2] if res is not None else None
        nin = 2 + (res is not None) + (dep is not None) + len(extra)
        ex = refs[nin - len(extra):nin]
        outs = refs[nin:-1] if nk > 1 else refs[nin:]
        acc = refs[-1] if nk > 1 else None
        k = pl.program_id(2)
        b_blk = b_ref[...]
        if nk > 1:
            @pl.when(k == 0)
            def _():
                acc[...] = jnp.zeros_like(acc)

        groups = []
        for part in range(parts):
            rows = pl.ds(part * (tm // parts), tm // parts)
            if both:
                p = _bdot(a_ref[0, rows, :], b_blk[:, :K // 2], dn) + _bdot(a_ref[1, rows, :], b_blk[:, K // 2:], dn)
            else:
                p = _bdot(a_ref[:, rows] if ta else a_ref[rows, :], b_blk, dn)
            if nk > 1:
                acc[rows, :] += p
            groups.append((rows, p))

        def epilogue():
            for part, (rows, p) in enumerate(groups):
                r = acc[rows, :] if nk > 1 else p
                if norm_bwd is not None:
                    r, dg = _rms_bwd_math(ex[0][rows, :], ex[1][...], r)
                    if part == 0:
                        @pl.when(pl.program_id(0) == 0)
                        def _(dg=dg):
                            outs[1][...] = dg

                    @pl.when((pl.program_id(0) > 0) | (part > 0))
                    def _(dg=dg):
                        outs[1][...] += dg
                if r_ref is not None:
                    r = r + r_ref[rows, :]
                outs[0][rows, :] = r.astype(out_dtype)
                if also_bf16:
                    outs[1][rows, :] = r.astype(bf16)
                if norm_g is not None:
                    outs[1][rows, :] = _rms(r, ex[-1][...]).astype(bf16)

        if nk > 1:
            pl.when(k == nk - 1)(epilogue)
        else:
            epilogue()

    a_spec = pl.BlockSpec((tk, tm), lambda i, j, k: (k, i)) if ta else pl.BlockSpec((tm, tk), lambda i, j, k: (i, k))
    b_spec = pl.BlockSpec((tn, tk), lambda i, j, k: (j, k)) if tb else pl.BlockSpec((tk, tn), lambda i, j, k: (k, j))
    if rows:
        b_spec = pl.BlockSpec((tn, tk) if tb else (tk, tn), lambda i, j, k: (0, 0), pipeline_mode=pl.Buffered(1))
    if both:
        a_spec = pl.BlockSpec((2, tm, K // 2), lambda i, j, k: (0, i, 0))
    elif a.ndim == 3:
        kh = K // 2 // tk
        a_spec = pl.BlockSpec((None, tm, tk), lambda i, j, k: (k // kh, i, k % kh))
    if b.ndim == 3:
        nh = N // 2 // tn
        b_spec = pl.BlockSpec((None, tk, tn), lambda i, j, k: (j // nh, k, j % nh))
    o_spec = pl.BlockSpec((tm, tn), lambda i, j, k: (i, j))
    in_specs, args = [a_spec, b_spec], [a, b]
    if res is not None:
        in_specs.append(o_spec)
        args.append(res)
    if dep is not None:
        in_specs.append(pl.BlockSpec(memory_space=pl.ANY))
        args.append(dep)
    vec = pl.BlockSpec((1, tn), lambda i, j, k: (0, j))
    if norm_bwd is not None:
        in_specs += [o_spec, vec]
    if norm_g is not None:
        in_specs.append(vec)
    args += extra
    out_shape = [jax.ShapeDtypeStruct((M, N), out_dtype)]
    out_specs = [o_spec]
    if also_bf16 or norm_g is not None:
        out_shape.append(jax.ShapeDtypeStruct((M, N), bf16))
        out_specs.append(o_spec)
    if norm_bwd is not None:
        out_shape.append(jax.ShapeDtypeStruct((1, N), f32))
        out_specs.append(vec)
    dims = ("arbitrary" if norm_bwd is not None else "parallel", "parallel", "arbitrary")
    out = pl.pallas_call(
        body, name=name, grid=(M // tm, N // tn, nk), in_specs=in_specs, out_specs=out_specs, out_shape=out_shape,
        scratch_shapes=[pltpu.VMEM((tm, tn), f32)] if nk > 1 else [], compiler_params=_cp(*dims),
    )(*args)
    return out if len(out) > 1 else out[0]


def _rms_fwd(x, g, *, name, tm=512):
    T, width = x.shape
    tm = _tile(T, tm)

    def body(x_ref, g_ref, o_ref):
        xv = x_ref[...]
        r = lax.rsqrt(jnp.mean(xv * xv, axis=-1, keepdims=True) + EPS)
        o_ref[...] = (xv * r * g_ref[...]).astype(bf16)

    return pl.pallas_call(
        body, name=name, grid=(T // tm,),
        in_specs=[pl.BlockSpec((tm, width), lambda i: (i, 0)), pl.BlockSpec((1, width), lambda i: (0, 0))],
        out_specs=pl.BlockSpec((tm, width), lambda i: (i, 0)), out_shape=jax.ShapeDtypeStruct((T, width), bf16),
        compiler_params=_cp("parallel"),
    )(x, g)


def _rms_bwd_math(xv, g, dy):
    r = lax.rsqrt(jnp.mean(xv * xv, axis=-1, keepdims=True) + EPS)
    xh = xv * r
    dxh = dy * g
    dx = r * (dxh - xh * jnp.mean(dxh * xh, axis=-1, keepdims=True))
    dg = jnp.sum(dy * xh, axis=0, keepdims=True)
    return dx, dg


def _loss_head(h, g, target, *, tm=512):
    T, D = h.shape
    tm = _tile(T, tm)

    def body(h_ref, g_ref, t_ref, loss_ref, dh_ref, dg_ref):
        hv, gv = h_ref[...], g_ref[...]
        r = lax.rsqrt(jnp.mean(hv * hv, axis=-1, keepdims=True) + EPS)
        e = hv * r * gv - t_ref[...]
        part = 0.5 * jnp.sum(jnp.mean(e * e, axis=-1, keepdims=True), axis=0, keepdims=True)
        dx, dg = _rms_bwd_math(hv, gv, e * (1.0 / D))
        dh_ref[...] = dx

        @pl.when(pl.program_id(0) == 0)
        def _():
            loss_ref[...] = part
            dg_ref[...] = dg

        @pl.when(pl.program_id(0) > 0)
        def _():
            loss_ref[...] += part
            dg_ref[...] += dg

    row = pl.BlockSpec((tm, D), lambda i: (i, 0))
    vec = pl.BlockSpec((1, D), lambda i: (0, 0))
    return pl.pallas_call(
        body, name="loss_head", grid=(T // tm,), in_specs=[row, vec, row],
        out_specs=[pl.BlockSpec((1, 1), lambda i: (0, 0)), row, vec],
        out_shape=[jax.ShapeDtypeStruct((1, 1), f32), jax.ShapeDtypeStruct((T, D), f32), jax.ShapeDtypeStruct((1, D), f32)],
        compiler_params=_cp("arbitrary"),
    )(h, g, target)


FFN_W = 2 * LANES
FFN_ROWS = 128
HALO = 2 * SUBLANES


def _conv_taps(a_ref, c, rc):
    if isinstance(c, int) and c == 0:
        ext = jnp.concatenate([jnp.zeros((HALO, FFN_W), f32), a_ref[pl.ds(0, rc), :].astype(f32)], axis=0)
    else:
        ext = a_ref[pl.ds(pl.multiple_of(c * rc - HALO, HALO), rc + HALO), :].astype(f32)
    return ext[HALO:], pltpu.roll(ext, 1, 0)[HALO:], pltpu.roll(ext, 2, 0)[HALO:]


def _chunk_rows(c, rc):
    return pl.ds(c * rc, rc) if isinstance(c, int) else pl.ds(pl.multiple_of(c * rc, rc), rc)


def _ffn_mid_fwd(au, cw, cb, *, name):
    T = au.shape[0]
    F = au.shape[1] // 2
    nb = F // FFN_W
    rc = min(FFN_ROWS, T)
    nc = T // rc

    def body(a_ref, u_ref, w_ref, b_ref, z_ref):
        w, b = w_ref[...], b_ref[...]

        def chunk(c):
            a, a1, a2 = _conv_taps(a_ref, c, rc)
            rows = _chunk_rows(c, rc)
            ac = (w[0:1] * a2 + w[1:2] * a1 + w[2:3] * a + b).astype(bf16)
            z_ref[rows, :] = ac * jax.nn.sigmoid(ac) * u_ref[rows, :]

        chunk(0)
        lax.fori_loop(1, nc, lambda c, _: chunk(c), None)

    return pl.pallas_call(
        body, name=name, grid=(nb,),
        in_specs=[pl.BlockSpec((T, FFN_W), lambda j: (0, j)), pl.BlockSpec((T, FFN_W), lambda j: (0, nb + j)),
                  pl.BlockSpec((3, FFN_W), lambda j: (0, j)), pl.BlockSpec((1, FFN_W), lambda j: (0, j))],
        out_specs=pl.BlockSpec((T, FFN_W), lambda j: (0, j)), out_shape=jax.ShapeDtypeStruct((T, F), bf16),
        compiler_params=_cp("parallel"),
    )(au, au, cw, cb)


def _ffn_mid_bwd(au, cw, cb, dz, *, name):
    T = au.shape[0]
    F = au.shape[1] // 2
    nb = F // FFN_W
    rc = min(FFN_ROWS, T)
    nc = T // rc

    def body(a_ref, u_ref, w_ref, b_ref, dz_ref, dau_ref, dw_ref, db_ref):
        w, b = w_ref[...], b_ref[...]

        def chunk(c, carry):
            nxt, s0, s1, s2, sb = carry
            a, a1, a2 = _conv_taps(a_ref, c, rc)
            rows = _chunk_rows(c, rc)
            ac = (w[0:1] * a2 + w[1:2] * a1 + w[2:3] * a + b).astype(bf16)
            sg = jax.nn.sigmoid(ac)
            dz = dz_ref[rows, :]
            dau_ref[1, rows, :] = dz * ac * sg
            dac = (dz * u_ref[rows, :] * sg * (1.0 + ac * (1.0 - sg))).astype(f32)
            ext = jnp.concatenate([dac, nxt], axis=0)
            d1, d2 = pltpu.roll(ext, rc + HALO - 1, 0)[:rc], pltpu.roll(ext, rc + HALO - 2, 0)[:rc]
            dau_ref[0, rows, :] = (w[2:3] * dac + w[1:2] * d1 + w[0:1] * d2).astype(bf16)
            tot = lambda v: jnp.sum(v, axis=0, keepdims=True)
            return dac[:HALO], s0 + tot(dac * a2), s1 + tot(dac * a1), s2 + tot(dac * a), sb + tot(dac)

        z = jnp.zeros((1, FFN_W), f32)
        carry = (jnp.zeros((HALO, FFN_W), f32), z, z, z, z)
        carry = lax.fori_loop(0, nc - 1, lambda k, cr: chunk(nc - 1 - k, cr), carry)
        _, s0, s1, s2, sb = chunk(0, carry)
        rows = lax.broadcasted_iota(jnp.int32, (3, FFN_W), 0)
        dw_ref[...] = jnp.where(rows == 0, s0, jnp.where(rows == 1, s1, s2))
        db_ref[...] = sb

    col = lambda off: pl.BlockSpec((T, FFN_W), lambda j: (0, off + j))
    return pl.pallas_call(
        body, name=name, grid=(nb,),
        in_specs=[col(0), col(nb), pl.BlockSpec((3, FFN_W), lambda j: (0, j)), pl.BlockSpec((1, FFN_W), lambda j: (0, j)), col(0)],
        out_specs=[pl.BlockSpec((2, T, FFN_W), lambda j: (0, 0, j)), pl.BlockSpec((3, FFN_W), lambda j: (0, j)),
                   pl.BlockSpec((1, FFN_W), lambda j: (0, j))],
        out_shape=[jax.ShapeDtypeStruct((2, T, F), bf16), jax.ShapeDtypeStruct((3, F), f32), jax.ShapeDtypeStruct((1, F), f32)],
        compiler_params=_cp("parallel"),
    )(au, au, cw, cb, dz)


BNN = (((2,), (1,)), ((0,), (0,)))
BNT = (((2,), (2,)), ((0,), (0,)))
BTN = (((1,), (1,)), ((0,), (0,)))


def _heads(x):
    return jnp.stack([x[:, h * LANES:(h + 1) * LANES] for h in range(HGRN_HEADS)])


def _put_heads(ref, rows, x, dtype):
    for h in range(HGRN_HEADS):
        ref[rows, h * LANES:(h + 1) * LANES] = x[h].astype(dtype)


def _hgrn_lb(l):
    m = jnp.max(l, axis=0, keepdims=True)
    e = jnp.exp(l - m)
    return e[0:1] / jnp.sum(e, axis=0, keepdims=True)


def _hgrn_chunk(q, fx, lb):
    H, C = q.shape[0], q.shape[1]
    sg = jax.nn.sigmoid(fx)
    F = lb + (1.0 - lb) * sg
    k = 1.0 - F
    logF = jnp.log(F)
    r = lax.broadcasted_iota(jnp.int32, (H, C, C), 1)
    c = lax.broadcasted_iota(jnp.int32, (H, C, C), 2)
    tril = (r >= c)
    b = _dot(tril.astype(f32), logF, BNN, precision=HI)
    bl = jnp.sum(logF, axis=1, keepdims=True)
    eb = jnp.exp(b)
    enb = jnp.exp(-b)
    elb = jnp.exp(bl - b)
    return dict(sg=sg, F=F, k=k, b=b, bl=bl, eb=eb, enb=enb, elb=elb, qd=q * eb, kd=k * enb, kl=k * elb, tril=tril)


def _hgrn_fwd(proj, lbl, ng, *, rb=512):
    T = proj.shape[0]
    rb = min(rb, T)
    cpb = rb // HGRN_CHUNK
    nblk = T // rb
    H = HGRN_HEADS

    def body(q_ref, f_ref, i_ref, g_ref, lbl_ref, ng_ref, y_ref, st_ref, S):
        @pl.when(pl.program_id(0) == 0)
        def _():
            S[...] = jnp.zeros_like(S)

        lb = _heads(_hgrn_lb(lbl_ref[...]))
        ngv = _heads(ng_ref[...])
        for c in range(cpb):
            sl = pl.ds(c * HGRN_CHUNK, HGRN_CHUNK)
            v, gx = _heads(i_ref[sl, :]), _heads(g_ref[sl, :])
            ch = _hgrn_chunk(_heads(q_ref[sl, :]), _heads(f_ref[sl, :]), lb)
            att = jnp.where(ch["tril"], _bdot(ch["qd"], ch["kd"], BNT), 0.0)
            St = S[...]
            st_ref[:, c] = St
            o = _bdot(att, v, BNN) + _bdot(ch["qd"], St, BNT)
            S[...] = St * jnp.exp(ch["bl"]) + _bdot(v, ch["kl"], BTN)
            r = lax.rsqrt(jnp.mean(o * o, axis=-1, keepdims=True) + EPS)
            _put_heads(y_ref, sl, o * r * ngv * (gx * jax.nn.sigmoid(gx)), bf16)

    col = lambda off: pl.BlockSpec((rb, H * LANES), lambda n: (n, off))
    return pl.pallas_call(
        body, name="hgrn_fwd", grid=(nblk,),
        in_specs=[col(0), col(1), col(2), col(3), pl.BlockSpec((2, H * LANES), lambda n: (0, 0)),
                  pl.BlockSpec((1, H * LANES), lambda n: (0, 0))],
        out_specs=[pl.BlockSpec((rb, H * LANES), lambda n: (n, 0)),
                   pl.BlockSpec((H, cpb, LANES, LANES), lambda n: (0, n, 0, 0))],
        out_shape=[jax.ShapeDtypeStruct((T, H * LANES), bf16),
                   jax.ShapeDtypeStruct((H, T // HGRN_CHUNK, LANES, LANES), f32)],
        scratch_shapes=[pltpu.VMEM((H, LANES, LANES), f32)], compiler_params=_cp("arbitrary"),
    )(proj, proj, proj, proj, lbl, ng)


def _hgrn_bwd(proj, lbl, ng, states, dy, *, rb=512):
    T = proj.shape[0]
    rb = min(rb, T)
    cpb = rb // HGRN_CHUNK
    nblk = T // rb
    H = HGRN_HEADS
    C = HGRN_CHUNK

    def body(q_ref, f_ref, i_ref, g_ref, lbl_ref, ng_ref, st_ref, dy_ref,
             dq_ref, df_ref, di_ref, dg_ref, dl_ref, dng_ref, dS, dlb_acc, dng_acc):
        n = pl.program_id(0)

        @pl.when(n == 0)
        def _():
            dS[...] = jnp.zeros_like(dS)
            dlb_acc[...] = jnp.zeros_like(dlb_acc)
            dng_acc[...] = jnp.zeros_like(dng_acc)

        lb_row = _hgrn_lb(lbl_ref[...])
        lb = _heads(lb_row)
        ngv = _heads(ng_ref[...])
        r_i = lax.broadcasted_iota(jnp.int32, (H, C, C), 1)
        c_i = lax.broadcasted_iota(jnp.int32, (H, C, C), 2)
        triu = (c_i >= r_i).astype(f32)
        rows_sum = lambda x: jnp.sum(x, axis=1, keepdims=True)
        for c in reversed(range(cpb)):
            sl = pl.ds(c * C, C)
            q, v, gx = _heads(q_ref[sl, :]), _heads(i_ref[sl, :]), _heads(g_ref[sl, :])
            ch = _hgrn_chunk(q, _heads(f_ref[sl, :]), lb)
            qd, kd, kl = ch["qd"], ch["kd"], ch["kl"]
            att = jnp.where(ch["tril"], _bdot(qd, kd, BNT), 0.0)
            St = st_ref[:, c]
            o = _bdot(att, v, BNN) + _bdot(qd, St, BNT)
            r = lax.rsqrt(jnp.mean(o * o, axis=-1, keepdims=True) + EPS)
            on = o * r
            sgg = jax.nn.sigmoid(gx)
            gate = gx * sgg
            dyv = _heads(dy_ref[sl, :].astype(f32))
            _put_heads(dg_ref, sl, dyv * on * ngv * sgg * (1.0 + gx * (1.0 - sgg)), bf16)
            dng_acc[...] += rows_sum(dyv * on * gate)
            don = dyv * ngv * gate
            do = r * (don - on * jnp.mean(don * on, axis=-1, keepdims=True))
            dSt = dS[...]
            dA = jnp.where(ch["tril"], _bdot(do, v, BNT), 0.0)
            dv = _bdot(att, do, BTN) + _bdot(kl, dSt, BNT)
            dqd = _bdot(dA, kd, BNN) + _bdot(do, St, BNN)
            dkd = _bdot(dA, qd, BTN)
            dkl = _bdot(v, dSt, BNN)
            dec = jnp.exp(ch["bl"])
            ddec = rows_sum(St * dSt)
            dS[...] = _bdot(do, qd, BTN) + dSt * dec
            dB = dqd * qd - dkd * kd - dkl * kl
            dbl = rows_sum(dkl * kl) + ddec * dec
            dk = dkd * ch["enb"] + dkl * ch["elb"]
            dlogF = _dot(triu, dB, BNN, precision=HI) + dbl
            dF = dlogF / ch["F"] - dk
            sg = ch["sg"]
            _put_heads(dq_ref, sl, dqd * ch["eb"], bf16)
            _put_heads(di_ref, sl, dv, bf16)
            _put_heads(df_ref, sl, dF * (1.0 - lb) * sg * (1.0 - sg), bf16)
            dlb_acc[...] += rows_sum(dF * (1.0 - sg))

        @pl.when(n == nblk - 1)
        def _():
            rows = lax.broadcasted_iota(jnp.int32, (2, LANES), 0)
            for h in range(H):
                hs = pl.ds(h * LANES, LANES)
                lbh = lb_row[:, h * LANES:(h + 1) * LANES]
                dl0 = dlb_acc[h] * lbh * (1.0 - lbh)
                dl_ref[:, hs] = jnp.where(rows == 0, dl0, -dl0)
                dng_ref[:, hs] = dng_acc[h]

    col = lambda off: pl.BlockSpec((rb, H * LANES), lambda n: (nblk - 1 - n, off))
    vec = lambda rows: pl.BlockSpec((rows, H * LANES), lambda n: (0, 0))
    tok = jax.ShapeDtypeStruct((T, H * LANES), bf16)
    return pl.pallas_call(
        body, name="hgrn_bwd", grid=(nblk,),
        in_specs=[col(0), col(1), col(2), col(3), vec(2), vec(1),
                  pl.BlockSpec((H, cpb, LANES, LANES), lambda n: (0, nblk - 1 - n, 0, 0)), col(0)],
        out_specs=[col(0), col(0), col(0), col(0), vec(2), vec(1)],
        out_shape=[tok, tok, tok, tok, jax.ShapeDtypeStruct((2, H * LANES), f32), jax.ShapeDtypeStruct((1, H * LANES), f32)],
        scratch_shapes=[pltpu.VMEM((H, LANES, LANES), f32), pltpu.VMEM((H, 1, LANES), f32), pltpu.VMEM((H, 1, LANES), f32)],
        compiler_params=_cp("arbitrary"),
    )(proj, proj, proj, proj, lbl, ng, states, dy)


def _s5_disc_math(ar, ai, ldt, br, bi):
    dt = jnp.exp(ldt)
    mag = jnp.exp(ar * dt)
    abr, abi = mag * jnp.cos(ai * dt), mag * jnp.sin(ai * dt)
    den = ar * ar + ai * ai
    xr, xi = abr - 1.0, abi
    cr = (xr * ar + xi * ai) / den
    ci = (xi * ar - xr * ai) / den
    return abr, abi, cr * br - ci * bi, cr * bi + ci * br


def _s5_disc_fwd(ar, ai, ldt, br, bi):
    def body(ar_ref, ai_ref, ldt_ref, br_ref, bi_ref, o0, o1, o2, o3):
        outs = _s5_disc_math(ar_ref[...], ai_ref[...], ldt_ref[...], br_ref[...], bi_ref[...])
        for o, v in zip((o0, o1, o2, o3), outs):
            o[...] = v

    return pl.pallas_call(
        body, name="s5_disc_fwd",
        out_shape=[jax.ShapeDtypeStruct(ar.shape, f32)] * 2 + [jax.ShapeDtypeStruct(br.shape, f32)] * 2,
    )(ar, ai, ldt, br, bi)


def _s5_disc_bwd(ar, ai, ldt, br, bi, cts):
    def body(ar_ref, ai_ref, ldt_ref, br_ref, bi_ref, c0, c1, c2, c3, o0, o1, o2, o3, o4):
        _, vjp = jax.vjp(_s5_disc_math, ar_ref[...], ai_ref[...], ldt_ref[...], br_ref[...], bi_ref[...])
        for o, v in zip((o0, o1, o2, o3, o4), vjp((c0[...], c1[...], c2[...], c3[...]))):
            o[...] = v

    return pl.pallas_call(
        body, name="s5_disc_bwd",
        out_shape=[jax.ShapeDtypeStruct(ar.shape, f32)] * 3 + [jax.ShapeDtypeStruct(br.shape, f32)] * 2,
    )(ar, ai, ldt, br, bi, *cts)


S5_LC = 512
S5_NLC = S5_N // S5_LC
S5_UB = 4
S5_UNROLL = 4
S5_TOGETHER = 2


def _cmul(ar, ai, xr, xi):
    return ar * xr - ai * xi, ar * xi + ai * xr


def _cpow(ar, ai, n):
    rr, ri = None, None
    br, bi = ar, ai
    while n:
        if n & 1:
            rr, ri = (br, bi) if rr is None else _cmul(rr, ri, br, bi)
        n >>= 1
        if n:
            br, bi = _cmul(br, bi, br, bi)
    return rr, ri


def _s5_bu(u_ref, bre_ref, bim_ref, xr, xi):
    for k in range(S5_UB):
        uk = u_ref[:, k * LANES:(k + 1) * LANES].astype(bf16)
        xr[:, k * S5_LC:(k + 1) * S5_LC] = _dot(uk, bre_ref[k])
        xi[:, k * S5_LC:(k + 1) * S5_LC] = _dot(uk, bim_ref[k])


def _s5_scan(xr, xi, sr, si, ar_ref, ai_ref, nsteps, store):
    for c0 in range(0, S5_NLC, S5_TOGETHER):
        css = [slice(c * S5_LC, (c + 1) * S5_LC) for c in range(c0, c0 + S5_TOGETHER)]
        a = [(jnp.broadcast_to(ar_ref[:, cs], (S5_SEG, S5_LC)), jnp.broadcast_to(ai_ref[:, cs], (S5_SEG, S5_LC))) for cs in css]

        def step(j, carry, css=css, a=a):
            rows = pl.ds(pl.multiple_of(j * S5_SEG, S5_SEG), S5_SEG)
            out = []
            for u, cs in enumerate(css):
                (a_r, a_i), pr, pi = a[u], carry[2 * u], carry[2 * u + 1]
                nr = a_r * pr - a_i * pi + xr[rows, cs]
                ni = a_r * pi + a_i * pr + xi[rows, cs]
                if store:
                    xr[rows, cs] = nr
                    xi[rows, cs] = ni
                out += [nr, ni]
            return tuple(out)

        init = tuple(v for cs in css for v in (sr[:, cs], si[:, cs]))
        fin = lax.fori_loop(0, nsteps, step, init)
        for u, cs in enumerate(css):
            sr[:, cs] = fin[2 * u]
            si[:, cs] = fin[2 * u + 1]


def _s5_rscan(dr, di, xr, xi, s0r, s0i, gr, gi, acc_r, acc_i, ar_ref, ai_ref, nsteps):
    for c in range(S5_NLC):
        cs = slice(c * S5_LC, (c + 1) * S5_LC)
        a_r = jnp.broadcast_to(ar_ref[:, cs], (S5_SEG, S5_LC))
        a_i = jnp.broadcast_to(ai_ref[:, cs], (S5_SEG, S5_LC))

        def update(j, carry, before, cs=cs, a_r=a_r, a_i=a_i):
            pr, pi, cr, ci = carry
            rows = pl.ds(j * S5_SEG if isinstance(j, int) else pl.multiple_of(j * S5_SEG, S5_SEG), S5_SEG)
            nr = dr[rows, cs] + a_r * pr + a_i * pi
            ni = di[rows, cs] + a_r * pi - a_i * pr
            dr[rows, cs] = nr
            di[rows, cs] = ni
            if before is not None:
                cr = cr + nr * before[0] + ni * before[1]
                ci = ci - nr * before[1] + ni * before[0]
            return nr, ni, cr, ci

        def step(jj, carry, cs=cs, update=update):
            j = nsteps - 1 - jj
            prev = pl.ds(pl.multiple_of((j - 1) * S5_SEG, S5_SEG), S5_SEG)
            return update(j, carry, None if acc_r is None else (xr[prev, cs], xi[prev, cs]))

        z = jnp.zeros((S5_SEG, S5_LC), f32)
        init = (gr[:, cs], gi[:, cs], z, z)
        carry = lax.fori_loop(0, nsteps - 1, step, init, unroll=S5_UNROLL)
        fr, fi, cr, ci = update(0, carry, None if acc_r is None else (s0r[:, cs], s0i[:, cs]))
        gr[:, cs] = fr
        gi[:, cs] = fi
        if acc_r is not None:
            acc_r[:, cs] += cr
            acc_i[:, cs] += ci


def _s5_seg_carry(fr, fi, ar, ai, seg_len, reverse):
    pr, pi = _cpow(ar, ai if not reverse else -ai, seg_len)
    rows = lax.broadcasted_iota(jnp.int32, fr.shape, 0)
    cr, ci = jnp.zeros_like(fr), jnp.zeros_like(fi)
    sh = (S5_SEG - 1) if reverse else 1
    fr_s, fi_s = pltpu.roll(fr, sh, 0), pltpu.roll(fi, sh, 0)
    order = range(S5_SEG - 2, -1, -1) if reverse else range(1, S5_SEG)
    for r in order:
        c_r, c_i = pltpu.roll(cr, sh, 0), pltpu.roll(ci, sh, 0)
        m_r, m_i = _cmul(pr, pi, c_r, c_i)
        cr = jnp.where(rows == r, m_r + fr_s, cr)
        ci = jnp.where(rows == r, m_i + fi_s, ci)
    return cr, ci


def _gelu_parts(y):
    c0 = math.sqrt(2.0 / math.pi)
    t = jnp.tanh(c0 * (y + 0.044715 * y * y * y))
    z = 0.5 * y * (1.0 + t)
    dz = 0.5 * (1.0 + t) + 0.5 * y * (1.0 - t * t) * c0 * (1.0 + 3.0 * 0.044715 * y * y)
    return z, dz


def _s5_y(xr, xi, u_ref, cre_ref, cim_ref, d_ref):
    ys = []
    for k in range(S5_UB):
        cs = slice(k * S5_LC, (k + 1) * S5_LC)
        ys.append(_bdot(xr[:, cs], cre_ref[k]) - _bdot(xi[:, cs], cim_ref[k]))
    return jnp.concatenate(ys, axis=1) + d_ref[...] * u_ref[...]


def _s5_specs(T, rb, rev=False):
    nblk = T // rb
    blk = (lambda i: (nblk - 1 - i, 0)) if rev else (lambda i: (i, 0))
    tok = pl.BlockSpec((rb, 4 * LANES), blk)
    bmat = pl.BlockSpec((S5_UB, LANES, S5_LC), lambda i: (0, 0, 0))
    cmat = pl.BlockSpec((S5_UB, S5_LC, LANES), lambda i: (0, 0, 0))
    avec = pl.BlockSpec((1, S5_N), lambda i: (0, 0))
    seg = pl.BlockSpec((S5_SEG, S5_N), lambda i: (0, 0))
    cvec = pl.BlockSpec((1, 4 * LANES), lambda i: (0, 0))
    s0 = pl.BlockSpec((1, S5_SEG, S5_N), (lambda i: (nblk - 1 - i, 0, 0)) if rev else (lambda i: (i, 0, 0)))
    return dict(tok=tok, bmat=bmat, cmat=cmat, avec=avec, seg=seg, cvec=cvec, s0=s0, nblk=nblk)


def _s5_final(u, bre, bim, ar, ai, *, rb):
    T = u.shape[0]
    sp = _s5_specs(T, rb)

    def body(u_ref, bre_ref, bim_ref, ar_ref, ai_ref, fr_ref, fi_ref, xr, xi):
        @pl.when(pl.program_id(0) == 0)
        def _():
            fr_ref[...] = jnp.zeros_like(fr_ref)
            fi_ref[...] = jnp.zeros_like(fi_ref)

        _s5_bu(u_ref, bre_ref, bim_ref, xr, xi)
        _s5_scan(xr, xi, fr_ref, fi_ref, ar_ref, ai_ref, rb // S5_SEG, False)

    return pl.pallas_call(
        body, name="s5_final", grid=(sp["nblk"],),
        in_specs=[sp["tok"], sp["bmat"], sp["bmat"], sp["avec"], sp["avec"]], out_specs=[sp["seg"], sp["seg"]],
        out_shape=[jax.ShapeDtypeStruct((S5_SEG, S5_N), f32)] * 2,
        scratch_shapes=[pltpu.VMEM((rb, S5_N), f32)] * 2, compiler_params=_cp("arbitrary"),
    )(u, bre, bim, ar, ai)


def _s5_fwd(u, bre, bim, ar, ai, fr, fi, cre, cim, dsk, wg, bg, *, rb):
    T = u.shape[0]
    sp = _s5_specs(T, rb)
    seg_len = T // S5_SEG

    def body(u_ref, bre_ref, bim_ref, ar_ref, ai_ref, fr_ref, fi_ref, cre_ref, cim_ref, d_ref, wg_ref, bg_ref,
             o_ref, s0r_ref, s0i_ref, xr, xi, sr, si):
        @pl.when(pl.program_id(0) == 0)
        def _():
            i_r, i_i = _s5_seg_carry(fr_ref[...], fi_ref[...], ar_ref[...], ai_ref[...], seg_len, False)
            sr[...] = i_r
            si[...] = i_i

        s0r_ref[0] = sr[...]
        s0i_ref[0] = si[...]
        _s5_bu(u_ref, bre_ref, bim_ref, xr, xi)
        _s5_scan(xr, xi, sr, si, ar_ref, ai_ref, rb // S5_SEG, True)
        y = _s5_y(xr, xi, u_ref, cre_ref, cim_ref, d_ref)
        z, _ = _gelu_parts(y)
        v = _bdot(z, wg_ref[...]) + bg_ref[...]
        o_ref[...] = (z * jax.nn.sigmoid(v)).astype(bf16)

    wspec = pl.BlockSpec((4 * LANES, 4 * LANES), lambda i: (0, 0))
    return pl.pallas_call(
        body, name="s5_fwd", grid=(sp["nblk"],),
        in_specs=[sp["tok"], sp["bmat"], sp["bmat"], sp["avec"], sp["avec"], sp["seg"], sp["seg"], sp["cmat"], sp["cmat"],
                  sp["cvec"], wspec, sp["cvec"]],
        out_specs=[sp["tok"], sp["s0"], sp["s0"]],
        out_shape=[jax.ShapeDtypeStruct((T, 4 * LANES), bf16)] + [jax.ShapeDtypeStruct((sp["nblk"], S5_SEG, S5_N), f32)] * 2,
        scratch_shapes=[pltpu.VMEM((rb, S5_N), f32)] * 2 + [pltpu.VMEM((S5_SEG, S5_N), f32)] * 2,
        compiler_params=_cp("arbitrary"),
    )(u, bre, bim, ar, ai, fr, fi, cre, cim, dsk, wg, bg)


def _s5_bwd_a(u, bre, bim, ar, ai, s0r, s0i, cre, cim, cret, cimt, dsk, wg, bg, dout, *, rb):
    T = u.shape[0]
    sp = _s5_specs(T, rb, rev=True)

    def body(u_ref, bre_ref, bim_ref, ar_ref, ai_ref, s0r_ref, s0i_ref, cre_ref, cim_ref, cret_ref, cimt_ref,
             d_ref, wg_ref, bg_ref, do_ref, dy_ref, glr_ref, gli_ref, dcre_ref, dcim_ref, dd_ref, dwg_ref, dbg_ref,
             xr, xi, dr, di, sr, si):
        @pl.when(pl.program_id(0) == 0)
        def _():
            for r in (glr_ref, gli_ref, dcre_ref, dcim_ref, dd_ref, dwg_ref, dbg_ref):
                r[...] = jnp.zeros_like(r)

        sr[...] = s0r_ref[0]
        si[...] = s0i_ref[0]
        _s5_bu(u_ref, bre_ref, bim_ref, xr, xi)
        _s5_scan(xr, xi, sr, si, ar_ref, ai_ref, rb // S5_SEG, True)
        uv = u_ref[...]
        y = _s5_y(xr, xi, u_ref, cre_ref, cim_ref, d_ref)
        z, gz = _gelu_parts(y)
        v = _bdot(z, wg_ref[...]) + bg_ref[...]
        sg = jax.nn.sigmoid(v)
        dov = do_ref[...].astype(f32)
        dv = dov * z * sg * (1.0 - sg)
        dz = dov * sg + _bdot(dv, wg_ref[...], NT)
        dy = dz * gz
        dy_ref[...] = dy
        dwg_ref[...] += _bdot(z, dv, TN)
        dbg_ref[...] += jnp.sum(dv, axis=0, keepdims=True)
        dd_ref[...] += jnp.sum(dy * uv, axis=0, keepdims=True)
        for k in range(S5_UB):
            cs = slice(k * S5_LC, (k + 1) * S5_LC)
            dyk = dy[:, k * LANES:(k + 1) * LANES]
            dcre_ref[k] += _bdot(xr[:, cs], dyk, TN)
            dcim_ref[k] -= _bdot(xi[:, cs], dyk, TN)
            dr[:, cs] = _bdot(dyk, cret_ref[k])
            di[:, cs] = -_bdot(dyk, cimt_ref[k])
        _s5_rscan(dr, di, None, None, None, None, glr_ref, gli_ref, None, None, ar_ref, ai_ref, rb // S5_SEG)

    wspec = pl.BlockSpec((4 * LANES, 4 * LANES), lambda i: (0, 0))
    return pl.pallas_call(
        body, name="s5_bwd_a", grid=(sp["nblk"],),
        in_specs=[sp["tok"], sp["bmat"], sp["bmat"], sp["avec"], sp["avec"], sp["s0"], sp["s0"], sp["cmat"], sp["cmat"],
                  sp["bmat"], sp["bmat"], sp["cvec"], wspec, sp["cvec"], sp["tok"]],
        out_specs=[sp["tok"], sp["seg"], sp["seg"], sp["cmat"], sp["cmat"], sp["cvec"], wspec, sp["cvec"]],
        out_shape=[jax.ShapeDtypeStruct((T, 4 * LANES), f32)] + [jax.ShapeDtypeStruct((S5_SEG, S5_N), f32)] * 2
        + [jax.ShapeDtypeStruct((S5_UB, S5_LC, LANES), f32)] * 2
        + [jax.ShapeDtypeStruct((1, 4 * LANES), f32), jax.ShapeDtypeStruct((4 * LANES, 4 * LANES), f32),
           jax.ShapeDtypeStruct((1, 4 * LANES), f32)],
        scratch_shapes=[pltpu.VMEM((rb, S5_N), f32)] * 4 + [pltpu.VMEM((S5_SEG, S5_N), f32)] * 2,
        compiler_params=_cp("arbitrary"),
    )(u, bre, bim, ar, ai, s0r, s0i, cre, cim, cret, cimt, dsk, wg, bg, dout)


def _s5_bwd_b(u, bre, bim, bret, bimt, ar, ai, s0r, s0i, glr, gli, cret, cimt, dsk, dy, *, rb):
    T = u.shape[0]
    sp = _s5_specs(T, rb, rev=True)
    seg_len = T // S5_SEG
    nblk = sp["nblk"]

    def body(u_ref, bre_ref, bim_ref, bret_ref, bimt_ref, ar_ref, ai_ref, s0r_ref, s0i_ref, glr_ref, gli_ref,
             cret_ref, cimt_ref, d_ref, dy_ref, du_ref, dbre_ref, dbim_ref, dar_ref, dai_ref,
             xr, xi, dr, di, sr, si, gr, gi, acc_r, acc_i):
        @pl.when(pl.program_id(0) == 0)
        def _():
            x_r, x_i = _s5_seg_carry(glr_ref[...], gli_ref[...], ar_ref[...], ai_ref[...], seg_len, True)
            gr[...] = x_r
            gi[...] = x_i
            acc_r[...] = jnp.zeros_like(acc_r)
            acc_i[...] = jnp.zeros_like(acc_i)
            dbre_ref[...] = jnp.zeros_like(dbre_ref)
            dbim_ref[...] = jnp.zeros_like(dbim_ref)

        sr[...] = s0r_ref[0]
        si[...] = s0i_ref[0]
        _s5_bu(u_ref, bre_ref, bim_ref, xr, xi)
        _s5_scan(xr, xi, sr, si, ar_ref, ai_ref, rb // S5_SEG, True)
        dy = dy_ref[...]
        for k in range(S5_UB):
            cs = slice(k * S5_LC, (k + 1) * S5_LC)
            dyk = dy[:, k * LANES:(k + 1) * LANES]
            dr[:, cs] = _bdot(dyk, cret_ref[k])
            di[:, cs] = -_bdot(dyk, cimt_ref[k])
        sr[...] = s0r_ref[0]
        si[...] = s0i_ref[0]
        _s5_rscan(dr, di, xr, xi, sr, si, gr, gi, acc_r, acc_i, ar_ref, ai_ref, rb // S5_SEG)
        dus = []
        for k in range(S5_UB):
            cs = slice(k * S5_LC, (k + 1) * S5_LC)
            uk = u_ref[:, k * LANES:(k + 1) * LANES]
            dbre_ref[k] += _bdot(uk, dr[:, cs], TN)
            dbim_ref[k] += _bdot(uk, di[:, cs], TN)
            dus.append(_bdot(dr[:, cs], bret_ref[k]) + _bdot(di[:, cs], bimt_ref[k]))
        du_ref[...] = (jnp.concatenate(dus, axis=1) + d_ref[...] * dy).astype(bf16)

        @pl.when(pl.program_id(0) == nblk - 1)
        def _():
            dar_ref[...] = jnp.sum(acc_r[...], axis=0, keepdims=True)
            dai_ref[...] = jnp.sum(acc_i[...], axis=0, keepdims=True)

    return pl.pallas_call(
        body, name="s5_bwd_b", grid=(nblk,),
        in_specs=[sp["tok"], sp["bmat"], sp["bmat"], sp["cmat"], sp["cmat"], sp["avec"], sp["avec"], sp["s0"], sp["s0"],
                  sp["seg"], sp["seg"], sp["bmat"], sp["bmat"], sp["cvec"], sp["tok"]],
        out_specs=[sp["tok"], sp["bmat"], sp["bmat"], sp["avec"], sp["avec"]],
        out_shape=[jax.ShapeDtypeStruct((T, 4 * LANES), bf16)] + [jax.ShapeDtypeStruct((S5_UB, LANES, S5_LC), f32)] * 2
        + [jax.ShapeDtypeStruct((1, S5_N), f32)] * 2,
        scratch_shapes=[pltpu.VMEM((rb, S5_N), f32)] * 4 + [pltpu.VMEM((S5_SEG, S5_N), f32)] * 6,
        compiler_params=_cp("arbitrary"),
    )(u, bre, bim, bret, bimt, ar, ai, s0r, s0i, glr, gli, cret, cimt, dsk, dy)


def _blockdiag(w, transpose=False):
    if transpose:
        w = jnp.swapaxes(w, 1, 2)
    g, a, b = w.shape
    eye = jnp.eye(8, dtype=w.dtype)
    return jnp.einsum("kgab,gj->kgajb", w.reshape(4, 8, a, b), eye).reshape(4, 8 * a, 8 * b)


def _blockdiag_t(m, a, b):
    eye = jnp.eye(8, dtype=m.dtype)
    return jnp.einsum("kgajb,gj->kgab", m.reshape(4, 8, a, 8, b), eye).reshape(32, a, b)


ROT = MLA_ROPE // 2


def _rope_tables(positions):
    freqs = ROPE_THETA ** (-jnp.arange(0, MLA_ROPE, 2, dtype=f32) / MLA_ROPE)
    ang = positions.astype(f32)[:, None] * freqs
    cos, sin, z = jnp.cos(ang), jnp.sin(ang), jnp.zeros_like(ang)
    return (jnp.concatenate([cos, cos, z, z], axis=1), jnp.concatenate([-sin, z, z, z], axis=1),
            jnp.concatenate([z, sin, z, z], axis=1))


def _rot(x, c, sa, sb):
    return x * c + pltpu.roll(x, LANES - ROT, 1) * sa + pltpu.roll(x, ROT, 1) * sb


def _rot_t(dy, c, sa, sb):
    return dy * c + pltpu.roll(dy * sa, ROT, 1) + pltpu.roll(dy * sb, LANES - ROT, 1)


def _rms(xv, g):
    return xv * lax.rsqrt(jnp.mean(xv * xv, axis=-1, keepdims=True) + EPS) * g


QW, KVW = MLA_Q_RANK, MLA_KV_RANK
ODD_PAD = QW + KVW + LANES


def _mla_prep_fwd(proj, qg, kvg, tabs, *, tm=512):
    T = proj.shape[0]
    tm = _tile(T, tm)

    def body(p_ref, qg_ref, kvg_ref, c_ref, sa_ref, sb_ref, cq_ref, ckv_ref, kr_ref):
        cq_ref[...] = _rms(p_ref[:, :QW], qg_ref[...]).astype(bf16)
        ckv_ref[...] = _rms(p_ref[:, QW:QW + KVW], kvg_ref[...]).astype(bf16)
        kr_ref[...] = _rot(p_ref[:, QW + KVW:], c_ref[...], sa_ref[...], sb_ref[...]).astype(bf16)

    row = lambda w: pl.BlockSpec((tm, w), lambda i: (i, 0))
    vec = lambda w: pl.BlockSpec((1, w), lambda i: (0, 0))
    return pl.pallas_call(
        body, name="mla_prep_fwd", grid=(T // tm,),
        in_specs=[row(ODD_PAD), vec(QW), vec(KVW), row(LANES), row(LANES), row(LANES)],
        out_specs=[row(QW), row(KVW), row(LANES)],
        out_shape=[jax.ShapeDtypeStruct((T, QW), bf16), jax.ShapeDtypeStruct((T, KVW), bf16),
                   jax.ShapeDtypeStruct((T, LANES), bf16)],
        compiler_params=_cp("parallel"),
    )(proj, qg, kvg, *tabs)


def _mla_prep_bwd(proj, qg, kvg, tabs, dcqn, dckvn, dkr_heads, *, tm=512):
    T = proj.shape[0]
    tm = _tile(T, tm)

    def body(p_ref, qg_ref, kvg_ref, c_ref, sa_ref, sb_ref, dcq_ref, dckv_ref, dkr_ref, dp_ref, dqg_ref, dkvg_ref):
        dcq, dqg = _rms_bwd_math(p_ref[:, :QW], qg_ref[...], dcq_ref[...])
        dckv, dkvg = _rms_bwd_math(p_ref[:, QW:QW + KVW], kvg_ref[...], dckv_ref[...])
        dk = dkr_ref[:, :LANES]
        for h in range(1, MLA_HEADS):
            dk = dk + dkr_ref[:, h * LANES:(h + 1) * LANES]
        dkr = _rot_t(dk, c_ref[...], sa_ref[...], sb_ref[...])
        dp_ref[...] = jnp.concatenate([dcq, dckv, dkr], axis=1).astype(bf16)

        @pl.when(pl.program_id(0) == 0)
        def _():
            dqg_ref[...] = dqg
            dkvg_ref[...] = dkvg

        @pl.when(pl.program_id(0) > 0)
        def _():
            dqg_ref[...] += dqg
            dkvg_ref[...] += dkvg

    row = lambda w: pl.BlockSpec((tm, w), lambda i: (i, 0))
    vec = lambda w: pl.BlockSpec((1, w), lambda i: (0, 0))
    return pl.pallas_call(
        body, name="mla_prep_bwd", grid=(T // tm,),
        in_specs=[row(ODD_PAD), vec(QW), vec(KVW), row(LANES), row(LANES), row(LANES), row(QW), row(KVW),
                  row(MLA_HEADS * LANES)],
        out_specs=[row(ODD_PAD), vec(QW), vec(KVW)],
        out_shape=[jax.ShapeDtypeStruct((T, ODD_PAD), bf16), jax.ShapeDtypeStruct((1, QW), f32),
                   jax.ShapeDtypeStruct((1, KVW), f32)],
        compiler_params=_cp("arbitrary"),
    )(proj, qg, kvg, *tabs, dcqn, dckvn, dkr_heads)


HQ = 2 * LANES
QK_SCALE = MLA_QK ** -0.5


def _q_post(q, tabs, *, transpose, name, tm=512):
    T = q.shape[0]
    tm = _tile(T, tm)

    def body(q_ref, c_ref, sa_ref, sb_ref, o_ref):
        c, sa, sb = c_ref[...], sa_ref[...], sb_ref[...]
        for h in range(MLA_HEADS):
            nope, rope = pl.ds(h * HQ, LANES), pl.ds(h * HQ + LANES, LANES)
            o_ref[:, nope] = (q_ref[:, nope].astype(f32) * QK_SCALE).astype(bf16)
            o_ref[:, rope] = ((_rot_t if transpose else _rot)(q_ref[:, rope].astype(f32), c, sa, sb) * QK_SCALE).astype(bf16)

    tab = pl.BlockSpec((tm, LANES), lambda i: (i, 0))
    blk = pl.BlockSpec((tm, MLA_HEADS * HQ), lambda i: (i, 0))
    return pl.pallas_call(
        body, name=name, grid=(T // tm,), in_specs=[blk, tab, tab, tab], out_specs=blk,
        out_shape=jax.ShapeDtypeStruct(q.shape, bf16), compiler_params=_cp("parallel"),
    )(q, *tabs)


def _causal_mask(i, j, tq, tk):
    r = lax.broadcasted_iota(jnp.int32, (tq, tk), 0) + i * tq
    c = lax.broadcasted_iota(jnp.int32, (tq, tk), 1) + j * tk
    return c <= r


FLASH_PARTS = 4


def _flash_fwd(q, kv, kr, *, tq=1024, tk=1024):
    T = q.shape[0]
    tq = _tile(T, tq)
    tk = _tile(tq, tk)
    per = tq // tk
    H = MLA_HEADS

    def body(q_ref, kn_ref, v_ref, kr_ref, o_ref, lse_ref, m_s, acc):
        i, j = pl.program_id(1), pl.program_id(2)
        last = (i + 1) * per - 1

        @pl.when(j == 0)
        def _():
            m_s[...] = jnp.full_like(m_s, -jnp.inf)
            acc[...] = jnp.zeros_like(acc)

        def step(masked):
            k = jnp.concatenate([kn_ref[...], kr_ref[...]], axis=1)
            v1 = jnp.concatenate([v_ref[...], jnp.ones((tk, LANES), bf16)], axis=1)
            mask = _causal_mask(i, j, tq, tk) if masked else None
            for part in range(FLASH_PARTS):
                rows = pl.ds(part * (tq // FLASH_PARTS), tq // FLASH_PARTS)
                s = _dot(q_ref[rows, :], k, NT)
                if masked:
                    s = jnp.where(mask[part * (tq // FLASH_PARTS):(part + 1) * (tq // FLASH_PARTS)], s, -jnp.inf)
                m_new = jnp.maximum(m_s[rows, :], jnp.max(s, axis=-1, keepdims=True))
                alpha = jnp.exp(m_s[rows, :] - m_new)
                p = jnp.exp((s - m_new).astype(bf16))
                acc[rows, :] = alpha * acc[rows, :] + _dot(p, v1)
                m_s[rows, :] = m_new

        pl.when(j < i * per)(functools.partial(step, False))
        pl.when((j >= i * per) & (j <= last))(functools.partial(step, True))

        @pl.when(j == last)
        def _():
            l = acc[:, LANES:]
            o_ref[...] = (acc[:, :LANES] / l).astype(bf16)
            lse_ref[0] = m_s[...] + jnp.log(jnp.max(l, axis=-1, keepdims=True))

    kj = lambda i, j: jnp.minimum(j, (i + 1) * per - 1)
    kblk = lambda off: pl.BlockSpec((tk, LANES), lambda h, i, j: (kj(i, j), 2 * h + off))
    return pl.pallas_call(
        body, name="flash_fwd", grid=(H, T // tq, T // tk),
        in_specs=[pl.BlockSpec((tq, HQ), lambda h, i, j: (i, h)), kblk(0), kblk(1),
                  pl.BlockSpec((tk, LANES), lambda h, i, j: (kj(i, j), 0))],
        out_specs=[pl.BlockSpec((tq, LANES), lambda h, i, j: (i, h)), pl.BlockSpec((1, tq, 1), lambda h, i, j: (h, i, 0))],
        out_shape=[jax.ShapeDtypeStruct((T, H * LANES), bf16), jax.ShapeDtypeStruct((H, T, 1), f32)],
        scratch_shapes=[pltpu.VMEM((tq, 1), f32), pltpu.VMEM((tq, 2 * LANES), f32)],
        compiler_params=_cp("parallel", "parallel", "arbitrary"),
    )(q, kv, kv, kr)


def _flash_bwd(q, kv, kr, o, do, lse, *, tb=1024):
    T = q.shape[0]
    tb = _tile(T, tb)
    nb = T // tb
    H = MLA_HEADS

    def body(q_ref, kn_ref, v_ref, kr_ref, o_ref, do_ref, lse_ref, dkv_ref, dkr_ref, dq_ref, dk_acc, dv_acc):
        j, ii = pl.program_id(1), pl.program_id(2)
        i = jnp.maximum(ii, j)

        @pl.when((j == 0) & (ii == 0))
        def _():
            dq_ref[...] = jnp.zeros_like(dq_ref)

        @pl.when(ii == 0)
        def _():
            dk_acc[...] = jnp.zeros_like(dk_acc)
            dv_acc[...] = jnp.zeros_like(dv_acc)

        def step(masked):
            k = jnp.concatenate([kn_ref[...], kr_ref[...]], axis=1)
            p = jnp.exp((_dot(q_ref[...], k, NT) - lse_ref[0]).astype(bf16))
            if masked:
                p = jnp.where(_causal_mask(i, j, tb, tb), p, jnp.zeros_like(p))
            delta = jnp.sum(o_ref[...].astype(f32) * do_ref[...], axis=-1, keepdims=True)
            ds = p * (_bdot(do_ref[...], v_ref[...], NT) - delta).astype(bf16)
            dv_acc[...] += _bdot(p, do_ref[...], TN)
            dk_acc[...] += _bdot(ds, q_ref[...], TN)
            dq_ref[pl.ds(pl.multiple_of(i * tb, tb), tb), :] += _bdot(ds, k)

        pl.when(ii > j)(functools.partial(step, False))
        pl.when(ii == j)(functools.partial(step, True))

        @pl.when(ii == nb - 1)
        def _():
            dkv_ref[...] = jnp.concatenate([dk_acc[:, :LANES], dv_acc[...]], axis=1).astype(bf16)
            dkr_ref[...] = dk_acc[:, LANES:]

    qi = lambda h, j, i: jnp.maximum(i, j)
    kblk = lambda off: pl.BlockSpec((tb, LANES), lambda h, j, i: (j, 2 * h + off))
    vec = pl.BlockSpec((1, tb, 1), lambda h, j, i: (h, qi(h, j, i), 0))
    qblk = pl.BlockSpec((tb, LANES), lambda h, j, i: (qi(h, j, i), h))
    return pl.pallas_call(
        body, name="flash_bwd", grid=(H, nb, nb),
        in_specs=[pl.BlockSpec((tb, HQ), lambda h, j, i: (qi(h, j, i), h)), kblk(0), kblk(1),
                  pl.BlockSpec((tb, LANES), lambda h, j, i: (j, 0)), qblk, qblk, vec],
        out_specs=[pl.BlockSpec((tb, HQ), lambda h, j, i: (j, h)), pl.BlockSpec((tb, LANES), lambda h, j, i: (j, h)),
                   pl.BlockSpec((T, HQ), lambda h, j, i: (0, h))],
        out_shape=[jax.ShapeDtypeStruct((T, H * HQ), bf16), jax.ShapeDtypeStruct((T, H * LANES), f32),
                   jax.ShapeDtypeStruct((T, H * HQ), f32)],
        scratch_shapes=[pltpu.VMEM((tb, HQ), f32), pltpu.VMEM((tb, LANES), f32)],
        compiler_params=_cp("parallel", "arbitrary", "arbitrary"),
    )(q, kv, kv, kr, o, do, lse)


HBM_SPEC = pl.BlockSpec(memory_space=pltpu.HBM)
N_CHIPS = 4
N_DEV = 8

BIG = {"even_w_in": 1, "s5_w_glu": 0, "even_w_out": 0, "odd_w_in": 0, "mla_w_uq": 1, "mla_w_ukv": 1, "odd_w_out": 0,
       "ffn_w_in": 2, "ffn_w_out": 1}
LAYERED = ("ffn_w_in", "ffn_w_out")
GROUPS = {"even_in": ("even_w_in",), "even_rest": ("s5_w_glu", "even_w_out"), "ffn0": LAYERED,
          "odd": ("odd_w_in", "mla_w_uq", "mla_w_ukv", "odd_w_out"), "ffn1": LAYERED}
GROUP_LAYER = {"ffn0": 0, "ffn1": 1}


def _place():
    x, y, c = lax.axis_index("x"), lax.axis_index("y"), lax.axis_index("c")
    chips = [(1 - x, y), (x, 1 - y), (1 - x, 1 - y)]
    return x, y, c, chips


def _slab(ref, axis, k, size):
    start = pl.multiple_of(k * size, size if axis == 0 else LANES)
    idx = [slice(None)] * len(ref.shape)
    idx[axis] = pl.ds(start, size)
    return ref.at[tuple(idx)]


SEM_SPEC = pl.BlockSpec(memory_space=pltpu.SEMAPHORE)
ANY_SPEC = pl.BlockSpec(memory_space=pl.ANY)
EFFECT = pltpu.SideEffectType.DATAFLOW_SIDE_EFFECTING


def _hbm(a):
    return pltpu.with_memory_space_constraint(a, pltpu.HBM)


class _Gather:
    copies = 3

    def __init__(self, axis, size):
        self.axis, self.size = axis, size

    def view(self, land, kk):
        return _slab(land, self.axis, kk, self.size)

    def own(self, land, place):
        return self.view(land, 2 * place[0] + place[1])

    def sends(self, src, land, place):
        x, y, c, chips = place
        return [(self.own(land, place) if src is None else src, self.own(land, place), (*chip, c)) for chip in chips]

    def recvs(self, land, place):
        return [self.view(land, 2 * cx + cy) for cx, cy in place[3]]


class _Scatter:
    copies = 3

    def __init__(self, axis, size, layer=None):
        self.axis, self.size, self.layer = axis, size, layer

    def row(self, land, j):
        return land.at[j] if self.layer is None else land.at[j, self.layer]

    def sends(self, src, land, place):
        c, chips = place[2], place[3]
        return [(_slab(src, self.axis, 2 * cx + cy, self.size), self.row(land, j), (cx, cy, c))
                for j, (cx, cy) in enumerate(chips)]

    def recvs(self, land, place):
        return [self.row(land, j) for j in range(3)]


class _Sibling:
    copies = 1

    def sends(self, src, land, place):
        x, y, c, _ = place
        return [(src, land, (x, y, 1 - c))]

    def recvs(self, land, place):
        return [land]


class _ToAll:
    copies = N_DEV - 1

    def __init__(self, size):
        self.size = size

    def sends(self, src, land, place):
        x, y, c, _ = place
        flip = lambda v, bit: 1 - v if bit else v
        own = _slab(land, 0, 4 * x + 2 * y + c, self.size)
        return [(own, own, (flip(x, m & 4), flip(y, m & 2), flip(c, m & 1))) for m in range(1, N_DEV)]

    def recvs(self, land, place):
        x, y, c, _ = place
        d = 4 * x + 2 * y + c
        return [_slab(land, 0, d ^ m, self.size) for m in range(1, N_DEV)]


def _unique(arrays):
    out, index = [], {}
    for a in arrays:
        if a is not None and id(a) not in index:
            index[id(a)] = len(out)
            out.append(a)
    return out, index


def _sem_base(routes):
    base = [0]
    for r in routes:
        base.append(base[-1] + r.copies)
    return base


def _push_start(name, items):
    n = len(items)
    base = _sem_base([it[0] for it in items])
    arrays, index = _unique([it[1] for it in items] + [it[2] for it in items])
    na = len(arrays)

    def body(*refs):
        arr, send, recv, token = refs[:na], refs[na], refs[na + 1], refs[-1]
        place = _place()
        for i, (route, src, land) in enumerate(items):
            s_ref = None if src is None else arr[index[id(src)]]
            for j, (s, d, dev) in enumerate(route.sends(s_ref, arr[index[id(land)]], place)):
                pltpu.make_async_remote_copy(src_ref=s, dst_ref=d, send_sem=send.at[base[i] + j], recv_sem=recv.at[base[i] + j],
                                             device_id=dev, device_id_type=MESH).start()
        token[...] = jnp.zeros_like(token)

    res = pl.pallas_call(
        body, name=name,
        out_shape=[pltpu.SemaphoreType.DMA((base[-1],)), pltpu.SemaphoreType.DMA((base[-1],))]
        + [pltpu.HBM(a.shape, a.dtype) for a in arrays] + [jax.ShapeDtypeStruct((SUBLANES, LANES), f32)],
        in_specs=[HBM_SPEC] * na, out_specs=[SEM_SPEC, SEM_SPEC] + [HBM_SPEC] * na + [pl.BlockSpec(memory_space=pltpu.VMEM)],
        input_output_aliases={i: 2 + i for i in range(na)},
        compiler_params=pltpu.CompilerParams(has_side_effects=EFFECT),
    )(*[_hbm(a) for a in arrays])
    thru = lambda a: None if a is None else res[2 + index[id(a)]]
    return (res[0], res[1]), [thru(it[1]) for it in items], [thru(it[2]) for it in items], res[-1]


def _push_wait(name, groups, after, with_srcs=False):
    arrays, index = _unique([a for _, _, srcs, lands in groups for a in list(srcs) + list(lands)])
    na, ng = len(arrays), len(groups)

    def body(*refs):
        arr, sems = refs[:na], refs[na:na + 2 * ng]
        place = _place()
        for g, (routes, _, srcs, lands) in enumerate(groups):
            send, recv = sems[2 * g], sems[2 * g + 1]
            base = _sem_base(routes)
            for i, route in enumerate(routes):
                src, land = None if srcs[i] is None else arr[index[id(srcs[i])]], arr[index[id(lands[i])]]
                for j, ((s, d, dev), mine) in enumerate(zip(route.sends(src, land, place), route.recvs(land, place))):
                    cp = pltpu.make_async_remote_copy(src_ref=s, dst_ref=mine, send_sem=send.at[base[i] + j],
                                                      recv_sem=recv.at[base[i] + j], device_id=dev,
                                                      device_id_type=MESH)
                    cp.wait_send()
                    cp.wait_recv()

    sem_args = [s for g in groups for s in g[1]]
    res = pl.pallas_call(
        body, name=name, out_shape=[pltpu.HBM(a.shape, a.dtype) for a in arrays],
        in_specs=[HBM_SPEC] * na + [SEM_SPEC] * (2 * ng) + [ANY_SPEC] * len(after), out_specs=[HBM_SPEC] * na,
        input_output_aliases={i: i for i in range(na)},
        compiler_params=pltpu.CompilerParams(has_side_effects=EFFECT),
    )(*arrays, *sem_args, *after)
    if with_srcs:
        return [([res[index[id(a)]] for a in g[2]], [res[index[id(a)]] for a in g[3]]) for g in groups]
    return [[res[index[id(a)]] for a in g[3]] for g in groups]


def _place_slab(block, axis, slabs, idx, dtype, *, name):
    R, C = block.shape
    tm = _rows(R, C)
    nr = R // tm
    out_map = (lambda i, k: (i, k[0])) if axis == 1 else (lambda i, k: (k[0] * nr + i, 0))

    def body(k_ref, x_ref, o_ref):
        o_ref[...] = x_ref[...].astype(dtype)

    full = (R, C * slabs) if axis == 1 else (R * slabs, C)
    return pl.pallas_call(
        body, name=name, out_shape=jax.ShapeDtypeStruct(full, dtype),
        grid_spec=pltpu.PrefetchScalarGridSpec(
            num_scalar_prefetch=1, grid=(nr,), in_specs=[pl.BlockSpec((tm, C), lambda i, k: (i, 0))],
            out_specs=pl.BlockSpec((tm, C), out_map)),
        compiler_params=_cp("parallel"),
    )(idx, block)


ELEMENTWISE_BLOCK_BYTES = 1 << 20


def _rows(r, c):
    for t in (512, 256, 128, 64, 32, 16, 8):
        if r % t == 0 and t * c * 4 <= ELEMENTWISE_BLOCK_BYTES:
            return t
    return r


def _sum4(owns, axis, recv, kidx, *, name, dep=None):
    L = len(owns)
    R, C = recv.shape[2:]
    tm = _rows(R, C)
    nr = R // tm
    deps = [] if dep is None else [dep]

    def body(k_ref, *refs):
        own_refs, r_ref, out_ref = refs[:L], refs[L], refs[-1]
        for li in range(L):
            @pl.when(pl.program_id(0) == li)
            def _(o_ref=own_refs[li]):
                out_ref[...] = ((o_ref[...] + r_ref[0, 0].astype(f32)) + r_ref[1, 0].astype(f32)) + r_ref[2, 0].astype(f32)

    own_map = (lambda l, i, k: (i, k[0])) if axis == 1 else (lambda l, i, k: (k[0] * nr + i, 0))
    return pl.pallas_call(
        body, name=name, out_shape=jax.ShapeDtypeStruct((L * R, C), f32),
        grid_spec=pltpu.PrefetchScalarGridSpec(
            num_scalar_prefetch=1, grid=(L, nr),
            in_specs=[pl.BlockSpec((tm, C), own_map)] * L + [pl.BlockSpec((3, 1, tm, C), lambda l, i, k: (0, l, i, 0))]
            + [pl.BlockSpec(memory_space=pl.ANY)] * len(deps),
            out_specs=pl.BlockSpec((tm, C), lambda l, i, k: (l * nr + i, 0))),
        compiler_params=_cp("parallel", "parallel"),
    )(kidx, *owns, recv, *deps)


def _adamw(w, m, v, parts, *, name):
    R, C = w.shape
    tm = _rows(R, C)
    npart = len(parts)

    def body(*refs):
        w_ref, m_ref, v_ref = refs[:3]
        g_ref, d_ref, m2_ref, v2_ref = refs[3 + npart:]
        g = refs[3][...]
        for p_ref in refs[4:3 + npart]:
            g = g + p_ref[...]
        g_ref[...] = g
        d_ref[...], m2_ref[...], v2_ref[...] = _adam_math(w_ref[...], m_ref[...], v_ref[...], g)

    blk = pl.BlockSpec((tm, C), lambda i: (i, 0))
    return pl.pallas_call(
        body, name=name, grid=(R // tm,),
        in_specs=[blk] * (3 + npart), out_specs=[blk] * 4,
        out_shape=[jax.ShapeDtypeStruct((R, C), f32)] * 4, compiler_params=_cp("parallel"),
    )(w, m, v, *parts)


def _adam_math(w, m, v, g):
    m2 = ADAM_B1 * m + (1.0 - ADAM_B1) * g
    v2 = ADAM_B2 * v + (1.0 - ADAM_B2) * (g * g)
    m_hat = m2 / (1.0 - ADAM_B1 ** ADAM_STEP)
    v_hat = v2 / (1.0 - ADAM_B2 ** ADAM_STEP)
    return -ADAM_LR * (m_hat / (jnp.sqrt(v_hat) + ADAM_EPS) + ADAM_WD * w), m2, v2


def _adamw_small(landed, w, m, v, kidx, ra, rb):
    rs = ra + N_CHIPS * rb

    def body(k_ref, l_ref, w_ref, m_ref, v_ref, g_ref, d_ref, m2_ref, v2_ref):
        mine = pl.multiple_of(ra + k_ref[0] * rb, SUBLANES)
        for lo, n, off in ((0, ra, 0), (ra, rb, mine)):
            g = l_ref[pl.ds(off, n), :]
            for d in range(1, N_DEV):
                g = g + l_ref[pl.ds(d * rs + off, n), :]
            rows = pl.ds(lo, n)
            delta, m2, v2 = _adam_math(w_ref[rows, :], m_ref[rows, :], v_ref[rows, :], g)
            g_ref[rows, :] = g
            d_ref[rows, :] = delta
            m2_ref[rows, :] = m2
            v2_ref[rows, :] = v2

    vmem = pl.BlockSpec(memory_space=pltpu.VMEM)
    return pl.pallas_call(
        body, name="adamw_small", out_shape=[jax.ShapeDtypeStruct(w.shape, f32)] * 4,
        grid_spec=pltpu.PrefetchScalarGridSpec(num_scalar_prefetch=1, grid=(), in_specs=[vmem] * 4, out_specs=[vmem] * 4),
        compiler_params=_cp(),
    )(kidx, landed, w, m, v)


def _pad_odd(w):
    return jnp.pad(w, ((0, 0), (0, ODD_PAD - w.shape[1])))


def _uq_cat(w):
    r = w.shape[0]
    return jnp.pad(w.reshape(r, MLA_HEADS, MLA_QK), ((0, 0), (0, 0), (0, HQ - MLA_QK))).reshape(r, MLA_HEADS * HQ)


def _uq_uncat(w):
    r = w.shape[0]
    return w.reshape(r, MLA_HEADS, HQ)[:, :, :MLA_QK].reshape(r, MLA_HEADS * MLA_QK)


def _to_segments(v):
    T, C = v.shape
    return v.reshape(S5_SEG, T // S5_SEG, C).transpose(1, 0, 2).reshape(T, C)


def _from_segments(v):
    T, C = v.shape
    return v.reshape(T // S5_SEG, S5_SEG, C).transpose(1, 0, 2).reshape(T, C)


def _s5_rb(T):
    return min(512, T)


def _ffn_fwd(h, hn, w_in, cw, cb, w_out, tag, next_g=None):
    au = _mm(hn, w_in, out_dtype=bf16, name=f"ffn{tag}_in", tn=1408)
    z = _ffn_mid_fwd(au, cw, cb, name=f"ffn{tag}_mid")
    return _mm(z, w_out, res=h, norm_g=next_g, name=f"ffn{tag}_out", tm=512, tk=D_FF), (hn, au, z)


def _ffn_bwd(h, g, w_in, cw, cb, w_out, saved, dh, tag, dep=None):
    hn, au, z = saved
    dz = _mm(dh, w_out, tb=True, out_dtype=bf16, name=f"ffn{tag}_dz", tn=1408, dep=dep)
    dw_out = _mm(z, dh, ta=True, also_bf16=True, name=f"ffn{tag}_dwout", tm=1408)
    dau, dcw, dcb = _ffn_mid_bwd(au, cw, cb, dz, name=f"ffn{tag}_dmid")
    dh_in, dg = _mm(dau, w_in, tb=True, res=dh, norm_bwd=(h, g), name=f"ffn{tag}_dhn", tk=1408)
    dw_in = _mm(hn, dau, ta=True, also_bf16=True, name=f"ffn{tag}_dwin", tn=1408)
    return dh_in, dg, dw_in, dcw, dcb, dw_out


def _local_step(x, positions, target, get_w, P, put_g):
    T = x.shape[0]
    rb = _s5_rb(T)
    row = lambda v: v.reshape(1, -1)
    g_mix, g_ffn = P["norm_mix_g"], P["norm_ffn_g"]
    lbl, hng = P["hgrn_lb_logits"], P["hgrn_norm_g"]
    dsk, bg = P["s5_d"], P["s5_b_glu"]
    qg, kvg = P["mla_q_norm_g"], P["mla_kv_norm_g"]
    cw, cb = P["ffn_conv_w"], P["ffn_conv_b"]

    col = lambda v: v.reshape(S5_N, 1)
    disc_in = (col(P["s5_a_re"]), col(P["s5_a_im"]), col(jnp.repeat(P["s5_log_dt"].reshape(S5_GROUPS), S5_STATE)),
               P["s5_b_re"].reshape(S5_N, S5_GROUP), P["s5_b_im"].reshape(S5_N, S5_GROUP))
    abr, abi, bbr, bbi = _s5_disc_fwd(*disc_in)
    ar, ai = abr.reshape(1, S5_N), abi.reshape(1, S5_N)
    bbr3, bbi3 = bbr.reshape(S5_GROUPS, S5_STATE, S5_GROUP), bbi.reshape(S5_GROUPS, S5_STATE, S5_GROUP)
    bre, bim = _blockdiag(bbr3, True).astype(bf16), _blockdiag(bbi3, True).astype(bf16)
    bret, bimt = _blockdiag(bbr3).astype(bf16), _blockdiag(bbi3).astype(bf16)
    c_re, c_im = P["s5_c_re"].reshape(S5_GROUPS, S5_GROUP, S5_STATE), P["s5_c_im"].reshape(S5_GROUPS, S5_GROUP, S5_STATE)
    cre, cim = _blockdiag(c_re, True).astype(bf16), _blockdiag(c_im, True).astype(bf16)
    cret, cimt = _blockdiag(c_re).astype(bf16), _blockdiag(c_im).astype(bf16)

    hn0 = _rms_fwd(x, g_mix[0:1], name="mix0_norm")
    We = get_w("even_in", hn0)
    proj_e = _mm(hn0, We["even_w_in"], name="even_in", tn=1280)
    Wr = get_w("even_rest", proj_e)
    ya, states = _hgrn_fwd(proj_e, lbl, hng)
    u_seg = _to_segments(proj_e[:, 4 * 512:])
    fr, fi = _s5_final(u_seg, bre, bim, ar, ai, rb=rb)
    yb_seg, s0r, s0i = _s5_fwd(u_seg, bre, bim, ar, ai, fr, fi, cre, cim, dsk, Wr["s5_w_glu"], bg, rb=rb)
    ycat = jnp.concatenate([ya, _from_segments(yb_seg)], axis=1)
    h1, hnf0 = _mm(ycat, Wr["even_w_out"], res=x, norm_g=g_ffn[0:1], name="even_out")
    Wf0 = get_w("ffn0", h1)
    (h2, hn2), ffn0 = _ffn_fwd(h1, hnf0, Wf0["ffn_w_in"], cw[0], cb[0:1], Wf0["ffn_w_out"], 0, next_g=g_mix[1:2])

    tabs = _rope_tables(positions)
    Wo = get_w("odd", hn2)
    proj_o = _mm(hn2, Wo["odd_w_in"], name="odd_in")
    cqn, ckvn, kr = _mla_prep_fwd(proj_o, qg, kvg, tabs)
    q = _q_post(_mm(cqn, Wo["mla_w_uq"], name="mla_uq"), tabs, transpose=False, name="q_post")
    kvb = _mm(ckvn, Wo["mla_w_ukv"], out_dtype=bf16, name="mla_ukv")
    o, lse = _flash_fwd(q, kvb, kr)
    h3, hnf1 = _mm(o, Wo["odd_w_out"], res=h2, norm_g=g_ffn[1:2], name="odd_out")
    Wf1 = get_w("ffn1", h3)
    h4, ffn1 = _ffn_fwd(h3, hnf1, Wf1["ffn_w_in"], cw[1], cb[1:2], Wf1["ffn_w_out"], 1)
    loss, dh4, dg_final = _loss_head(h4, row(P["final_norm_g"]), target)

    dh3, dg_ffn1, dw_fin1, dcw1, dcb1, dw_fout1 = _ffn_bwd(
        h3, g_ffn[1:2], Wf1["ffn_w_in"], cw[1], cb[1:2], Wf1["ffn_w_out"], ffn1, dh4, 1)
    sent = put_g("ffn1", {"ffn_w_in": dw_fin1, "ffn_w_out": dw_fout1})
    do = _mm(dh3, Wo["odd_w_out"], tb=True, out_dtype=bf16, name="odd_do", dep=sent)
    dw_oout = _mm(o, dh3, ta=True, also_bf16=True, name="odd_dwout")
    dkv, dkr_h, dq = _flash_bwd(q, kvb, kr, o, do, lse)
    dq = _q_post(dq, tabs, transpose=True, name="dq_post")
    dw_uq = _mm(cqn, dq, ta=True, also_bf16=True, name="mla_dwuq")
    dcqn = _mm(dq, Wo["mla_w_uq"], tb=True, name="mla_dcq", tk=MLA_HEADS * HQ)
    dw_ukv = _mm(ckvn, dkv, ta=True, also_bf16=True, name="mla_dwukv")
    dckvn = _mm(dkv, Wo["mla_w_ukv"], tb=True, name="mla_dckv")
    dproj_o, dqg, dkvg = _mla_prep_bwd(proj_o, qg, kvg, tabs, dcqn, dckvn, dkr_h)
    dw_oin = _mm(hn2, dproj_o, ta=True, also_bf16=True, name="odd_dwin")
    sent = put_g("odd", {"odd_w_in": dw_oin, "mla_w_uq": dw_uq, "mla_w_ukv": dw_ukv, "odd_w_out": dw_oout})
    dh2, dg_mix1 = _mm(dproj_o, Wo["odd_w_in"], tb=True, res=dh3, norm_bwd=(h2, g_mix[1:2]), name="odd_dhn")

    dh1, dg_ffn0, dw_fin0, dcw0, dcb0, dw_fout0 = _ffn_bwd(
        h1, g_ffn[0:1], Wf0["ffn_w_in"], cw[0], cb[0:1], Wf0["ffn_w_out"], ffn0, dh2, 0, dep=sent)
    sent = put_g("ffn0", {"ffn_w_in": dw_fin0, "ffn_w_out": dw_fout0})
    dycat = _mm(dh1, Wr["even_w_out"], tb=True, out_dtype=bf16, name="even_dy", dep=sent)
    dw_eout = _mm(ycat, dh1, ta=True, also_bf16=True, name="even_dwout")
    dq_h, df_h, di_h, dg_h, dlbl, dhng = _hgrn_bwd(proj_e, lbl, hng, states, dycat)
    dyb_seg = _to_segments(dycat[:, 512:])
    dy_s5, glr, gli, dcre, dcim, dd, dwg, dbg = _s5_bwd_a(
        u_seg, bre, bim, ar, ai, s0r, s0i, cre, cim, cret, cimt, dsk, Wr["s5_w_glu"], bg, dyb_seg, rb=rb)
    du_seg, dbre, dbim, dar, dai = _s5_bwd_b(
        u_seg, bre, bim, bret, bimt, ar, ai, s0r, s0i, glr, gli, cret, cimt, dsk, dy_s5, rb=rb)
    dproj_e = jnp.concatenate([dq_h, df_h, di_h, dg_h, _from_segments(du_seg)], axis=1)
    dx, dg_mix0 = _mm(dproj_e, We["even_w_in"], tb=True, res=dh1, norm_bwd=(x, g_mix[0:1]), name="even_dhn", tk=1280)
    dw_ein = _mm(hn0, dproj_e, ta=True, also_bf16=True, name="even_dwin", tn=1280)

    unblk = lambda m, a, b: jnp.swapaxes(_blockdiag_t(m, a, b), 1, 2)
    dbbr = unblk(dbre, S5_GROUP, S5_STATE).reshape(S5_N, S5_GROUP)
    dbbi = unblk(dbim, S5_GROUP, S5_STATE).reshape(S5_N, S5_GROUP)
    d_ar, d_ai, d_ldt, d_br, d_bi = _s5_disc_bwd(*disc_in, (dar.reshape(S5_N, 1), dai.reshape(S5_N, 1), dbbr, dbbi))
    small = {
        "norm_mix_g": jnp.concatenate([dg_mix0, dg_mix1], axis=0),
        "norm_ffn_g": jnp.concatenate([dg_ffn0, dg_ffn1], axis=0),
        "final_norm_g": dg_final.reshape(-1),
        "hgrn_lb_logits": dlbl, "hgrn_norm_g": dhng,
        "s5_a_re": d_ar.reshape(1, S5_GROUPS, S5_STATE), "s5_a_im": d_ai.reshape(1, S5_GROUPS, S5_STATE),
        "s5_log_dt": d_ldt.reshape(S5_GROUPS, S5_STATE).sum(axis=1).reshape(1, S5_GROUPS),
        "s5_b_re": d_br.reshape(1, S5_GROUPS, S5_STATE, S5_GROUP), "s5_b_im": d_bi.reshape(1, S5_GROUPS, S5_STATE, S5_GROUP),
        "s5_c_re": unblk(dcre, S5_STATE, S5_GROUP).reshape(1, S5_GROUPS, S5_GROUP, S5_STATE),
        "s5_c_im": unblk(dcim, S5_STATE, S5_GROUP).reshape(1, S5_GROUPS, S5_GROUP, S5_STATE),
        "s5_d": dd, "s5_b_glu": dbg, "mla_q_norm_g": dqg, "mla_kv_norm_g": dkvg,
        "ffn_conv_w": jnp.stack([dcw0, dcw1]), "ffn_conv_b": jnp.concatenate([dcb0, dcb1], axis=0),
    }
    put_g("even", {"even_w_in": dw_ein, "s5_w_glu": (dwg, dwg.astype(bf16)), "even_w_out": dw_eout}, small)
    return loss, dx


WEIGHTS = ["norm_mix_g", "norm_ffn_g", "final_norm_g", "even_w_in", "hgrn_lb_logits", "hgrn_norm_g", "s5_a_re", "s5_a_im",
           "s5_log_dt", "s5_b_re", "s5_b_im", "s5_c_re", "s5_c_im", "s5_d", "s5_w_glu", "s5_b_glu", "even_w_out", "odd_w_in",
           "mla_q_norm_g", "mla_w_uq", "mla_kv_norm_g", "mla_w_ukv", "odd_w_out", "ffn_w_in", "ffn_conv_w", "ffn_conv_b",
           "ffn_w_out"]
SMALL_SHARDED = {"mla_q_norm_g": 1, "mla_kv_norm_g": 1, "ffn_conv_w": 2}
SMALL = [n for n in WEIGHTS if n not in BIG]
SMALL_REP = [n for n in SMALL if n not in SMALL_SHARDED]


def _pack_rows(shapes):
    n = sum(math.prod(s) for s in shapes)
    return -(-n // (SUBLANES * LANES)) * SUBLANES


def _pack(arrays, rows):
    flat = jnp.concatenate([a.reshape(-1) for a in arrays])
    return jnp.pad(flat, (0, rows * LANES - flat.shape[0])).reshape(rows, LANES)


def _unpack(block, shapes):
    flat, out, off = block.reshape(-1), [], 0
    for s in shapes:
        n = math.prod(s)
        out.append(flat[off:off + n].reshape(s))
        off += n
    return out


def kernel(x, positions, norm_mix_g, norm_ffn_g, final_norm_g, even_w_in, hgrn_lb_logits, hgrn_norm_g, s5_a_re, s5_a_im, s5_log_dt, s5_b_re, s5_b_im, s5_c_re, s5_c_im, s5_d, s5_w_glu, s5_b_glu, even_w_out, odd_w_in, mla_q_norm_g, mla_w_uq, mla_kv_norm_g, mla_w_ukv, odd_w_out, ffn_w_in, ffn_conv_w, ffn_conv_b, ffn_w_out, loss_target, m_norm_mix_g, m_norm_ffn_g, m_final_norm_g, m_even_w_in, m_hgrn_lb_logits, m_hgrn_norm_g, m_s5_a_re, m_s5_a_im, m_s5_log_dt, m_s5_b_re, m_s5_b_im, m_s5_c_re, m_s5_c_im, m_s5_d, m_s5_w_glu, m_s5_b_glu, m_even_w_out, m_odd_w_in, m_mla_q_norm_g, m_mla_w_uq, m_mla_kv_norm_g, m_mla_w_ukv, m_odd_w_out, m_ffn_w_in, m_ffn_conv_w, m_ffn_conv_b, m_ffn_w_out, v_norm_mix_g, v_norm_ffn_g, v_final_norm_g, v_even_w_in, v_hgrn_lb_logits, v_hgrn_norm_g, v_s5_a_re, v_s5_a_im, v_s5_log_dt, v_s5_b_re, v_s5_b_im, v_s5_c_re, v_s5_c_im, v_s5_d, v_s5_w_glu, v_s5_b_glu, v_even_w_out, v_odd_w_in, v_mla_q_norm_g, v_mla_w_uq, v_mla_kv_norm_g, v_mla_w_ukv, v_odd_w_out, v_ffn_w_in, v_ffn_conv_w, v_ffn_conv_b, v_ffn_w_out):
    args = dict(locals())
    w = {n: args[n] for n in WEIGHTS}
    m = {n: args["m_" + n] for n in WEIGHTS}
    v = {n: args["v_" + n] for n in WEIGHTS}
    k = 2 * lax.axis_index("x") + lax.axis_index("y")
    kidx = k.reshape(1).astype(jnp.int32)
    axis2d = lambda n: BIG[n] - (1 if n in LAYERED else 0)
    slab = lambda n: w[n].shape[1 + axis2d(n)]

    small_sh_shapes = [w[n].shape for n in SMALL_SHARDED]
    rb = _pack_rows(small_sh_shapes)
    items = {}
    for group, names in GROUPS.items():
        layer = GROUP_LAYER.get(group, 0)
        items[group] = [(_Gather(axis2d(n), slab(n)), None,
                         _place_slab(w[n][layer], axis2d(n), N_CHIPS, kidx, bf16, name=f"place_{n}_{layer}")) for n in names]
    items["even_in"].append((_Gather(0, rb), None,
                             _place_slab(_pack([w[n] for n in SMALL_SHARDED], rb), 0, N_CHIPS, kidx, f32, name="place_small")))
    gathers, tokens = {}, []
    for group in GROUPS:
        sems, srcs, lands, token = _push_start(f"gather_start_{group}", items[group])
        gathers[group] = ([it[0] for it in items[group]], sems, srcs, lands)
        tokens.append(token[0, 0])
    started = functools.reduce(jnp.add, tokens)

    def landed(group, after):
        return _push_wait(f"gather_wait_{group}", [gathers[group]], [after])[0]

    even = landed("even_in", (started + norm_mix_g[0, 0]).reshape(1))
    per_chip = [_unpack(even[-1][c * rb:(c + 1) * rb], small_sh_shapes) for c in range(N_CHIPS)]
    P = {n: w[n] for n in SMALL_REP}
    for i, (n, ax) in enumerate(SMALL_SHARDED.items()):
        P[n] = jnp.concatenate([per_chip[c][i] for c in range(N_CHIPS)], axis=ax)
    P["mla_q_norm_g"], P["mla_kv_norm_g"] = P["mla_q_norm_g"].reshape(1, -1), P["mla_kv_norm_g"].reshape(1, -1)
    fix_w = {"odd_w_in": _pad_odd, "mla_w_uq": _uq_cat}

    def get_w(group, after):
        full = even if group == "even_in" else landed(group, after)
        return {n: fix_w.get(n, lambda a: a)(a) for n, a in zip(GROUPS[group], full)}

    fix_g = {"odd_w_in": lambda g: g[:, :odd_w_in.shape[2]], "mla_w_uq": _uq_uncat}
    g32, scatters, land_now = {}, {}, {}
    ra = _pack_rows([w[n].shape for n in SMALL_REP])
    rs = ra + N_CHIPS * rb
    didx = (2 * kidx + lax.axis_index("c")).astype(jnp.int32)

    def put_g(group, grads, small=None):
        layer = GROUP_LAYER.get(group)
        routes, srcs, names = [], [], list(grads)
        for n in names:
            f = fix_g.get(n, lambda g: g)
            g32.setdefault(n, {})[layer or 0] = f(grads[n][0])
            routes.append(_Scatter(axis2d(n), slab(n), layer if n in LAYERED else None))
            srcs.append(f(grads[n][1]))
            if n not in land_now:
                land_now[n] = lax.empty((3,) + w[n].shape[0 if n in LAYERED else 1:], bf16)
        if small is not None:
            blocks = [_pack([small[n] for n in SMALL_REP], ra)]
            for chip in range(N_CHIPS):
                sl = lambda n, ax: lax.slice_in_dim(small[n].reshape(w[n].shape[:ax] + (-1,) + w[n].shape[ax + 1:]),
                                                    chip * w[n].shape[ax], (chip + 1) * w[n].shape[ax], axis=ax)
                blocks.append(_pack([sl(n, ax) for n, ax in SMALL_SHARDED.items()], rb))
            names.append("small")
            routes.append(_ToAll(rs))
            srcs.append(None)
            land_now["small"] = _place_slab(jnp.concatenate(blocks), 0, N_DEV, didx, f32, name="place_small_grads")
        sems, srcs, lands, token = _push_start(f"scatter_start_{group}", [(r, s, land_now[n]) for r, s, n in zip(routes, srcs, names)])
        land_now.update(zip(names, lands))
        scatters[group] = (routes, sems, srcs, names)
        sent.append(token)
        return token

    sent = []
    loss, dx = _local_step(x[0], positions[0], loss_target[0], get_w, P, put_g)
    sent_last = sent[-1]
    loss = lax.psum(loss[0, 0], ("x", "y", "c"))

    out = {}

    def arrive(tag, groups, after):
        waits = [(scatters[g][0], scatters[g][1], scatters[g][2], [land_now[n] for n in scatters[g][3]]) for g in groups]
        for g, lands in zip(groups, _push_wait(f"scatter_wait_{tag}", waits, after)):
            land_now.update(zip(scatters[g][3], lands))

    def cross(tag, names, dep=None):
        part = {}
        for n in names:
            recv = land_now[n] if n in LAYERED else land_now[n][:, None]
            part[n] = _sum4([g32[n][l] for l in sorted(g32[n])], axis2d(n), recv, kidx, name=f"sum4_{n}", dep=dep)
        items = [(_Sibling(), part[n], lax.empty(part[n].shape, f32)) for n in names]
        sems, srcs, lands, token = _push_start(f"swap_start_{tag}", items)
        return (names, part, ([it[0] for it in items], sems, srcs, lands)), token

    def update(tag, arrived, after):
        names, _, push = arrived
        mine, theirs = _push_wait(f"swap_wait_{tag}", [push], after, with_srcs=True)[0]
        part, other = dict(zip(names, mine)), dict(zip(names, theirs))
        done = []
        for n in names:
            C = part[n].shape[-1]
            res = _adamw(w[n].reshape(-1, C), m[n].reshape(-1, C), v[n].reshape(-1, C), [part[n], other[n]], name=f"adamw_{n}")
            out[n] = [r.reshape(w[n].shape) for r in res]
            done.append(res[0])
        return done

    arrive("a", ["ffn1", "odd", "ffn0"], [dx, sent_last])
    a1, token_a1 = cross("a1", ["ffn_w_in"])
    a2, token_a2 = cross("a2", ["ffn_w_out"] + list(GROUPS["odd"]), dep=token_a1)
    done = update("a2", a2, update("a1", a1, [token_a2]))
    arrive("b", ["even"], done)
    b, token_b = cross("b", list(GROUPS["even_in"]) + list(GROUPS["even_rest"]))

    order = SMALL_REP + list(SMALL_SHARDED)
    packed = lambda src: jnp.concatenate([_pack([src[n] for n in SMALL_REP], ra), _pack([src[n] for n in SMALL_SHARDED], rb)])
    res = _adamw_small(land_now["small"], packed(w), packed(m), packed(v), kidx, ra, rb)
    update("b", b, [res[0], token_b])
    for r in res:
        parts = _unpack(r[:ra], [w[n].shape for n in SMALL_REP]) + _unpack(r[ra:], small_sh_shapes)
        for n, a in zip(order, parts):
            out.setdefault(n, []).append(a)

    return (loss, dx[None], *[out[n][0] for n in WEIGHTS], *[out[n][1] for n in WEIGHTS],
            *[out[n][2] for n in WEIGHTS], *[out[n][3] for n in WEIGHTS])
```

```python
import functools
import math

import jax
import jax.numpy as jnp
from jax import lax
from jax.experimental import pallas as pl
from jax.experimental.pallas import tpu as pltpu

f32, bf16 = jnp.float32, jnp.bfloat16
EPS = 1e-6
LANES = 128
SUBLANES = 8
VMEM_BYTES = 48 * 1024 * 1024
HGRN_CHUNK = 64
HGRN_HEADS = 4
S5_GROUPS, S5_STATE, S5_GROUP = 32, 64, 16
S5_N = S5_GROUPS * S5_STATE
S5_SEG = SUBLANES
MLA_HEADS, MLA_NOPE, MLA_ROPE, MLA_V = 8, 128, 64, 128
MLA_QK = MLA_NOPE + MLA_ROPE
MLA_Q_RANK, MLA_KV_RANK = 384, 256
ROPE_THETA = 10000.0
D_FF = 2816
ADAM_LR, ADAM_B1, ADAM_B2, ADAM_EPS, ADAM_WD, ADAM_STEP = 0.001, 0.9, 0.999, 1e-08, 0.01, 10
MESH = pl.DeviceIdType.MESH
HI = lax.Precision.HIGHEST


def _cp(*dims):
    return pltpu.CompilerParams(dimension_semantics=dims if dims else None, vmem_limit_bytes=VMEM_BYTES)


def _tile(n, t):
    if n <= t:
        return n
    c = (t // LANES) * LANES
    while c >= LANES:
        if n % c == 0:
            return c
        c -= LANES
    return n


def _dot(a, b, dn=None, precision=None):
    if dn is None:
        dn = (((a.ndim - 1,), (0,)), ((), ()))
    return lax.dot_general(a, b, dn, preferred_element_type=f32, precision=precision)


NT = (((1,), (1,)), ((), ()))
TN = (((0,), (0,)), ((), ()))


def _bdot(a, b, dn=None):
    return _dot(a.astype(bf16), b.astype(bf16), dn)


MM_PARTS = 2


def _mm(a, b, *, name, ta=False, tb=False, out_dtype=f32, res=None, also_bf16=False, tm=1024, tn=1024, tk=1024, dep=None,
        norm_g=None, norm_bwd=None, rope=None):
    halves = lambda s: (s[1], 2 * s[2]) if len(s) == 3 else s
    M, K = (a.shape[1], a.shape[0]) if ta else halves(a.shape)
    N = b.shape[0] if tb else halves(b.shape)[1]
    rows = norm_g is not None or norm_bwd is not None
    if rows:
        tm, tn, tk = 512, N, K
    tm, tn, tk = _tile(M, tm), _tile(N, tn), _tile(K, tk)
    both = rows and a.ndim == 3 and tb
    if a.ndim == 3 and not both:
        tk = _tile(K // 2, tk)
    if b.ndim == 3:
        tn = _tile(N // 2, tn)
    nk = K // tk
    parts = MM_PARTS if tm % (MM_PARTS * LANES) == 0 else 1
    dn = (((0 if ta else 1,), (1 if tb else 0,)), ((), ()))
    extra = [] if norm_bwd is None else list(norm_bwd)
    if norm_g is not None:
        extra.append(norm_g)
    if rope is not None:
        extra += list(rope)

    def body(*refs):
        a_ref, b_ref = refs[0], refs[1]
        r_ref = refs[2] if res is not None else None
        nin = 2 + (res is not None) + (dep is not None) + len(extra)
        ex = refs[nin - len(extra):nin]
        outs = refs[nin:-1] if nk > 1 else refs[nin:]
        acc = refs[-1] if nk > 1 else None
        k = pl.program_id(2)
        b_blk = b_ref[...]
        if nk > 1:
            @pl.when(k == 0)
            def _():
                acc[...] = jnp.zeros_like(acc)

        groups = []
        for part in range(parts):
            rows = pl.ds(part * (tm // parts), tm // parts)
            if both:
                p = _bdot(a_ref[0, rows, :], b_blk[:, :K // 2], dn) + _bdot(a_ref[1, rows, :], b_blk[:, K // 2:], dn)
            else:
                p = _bdot(a_ref[:, rows] if ta else a_ref[rows, :], b_blk, dn)
            if nk > 1:
                acc[rows, :] += p
            groups.append((rows, p))

        def epilogue():
            for part, (rows, p) in enumerate(groups):
                r = acc[rows, :] if nk > 1 else p
                if norm_bwd is not None:
                    r, dg = _rms_bwd_math(ex[0][rows, :], ex[1][...], r)
                    if part == 0:
                        @pl.when(pl.program_id(0) == 0)
                        def _(dg=dg):
                            outs[1][...] = dg

                    @pl.when((pl.program_id(0) > 0) | (part > 0))
                    def _(dg=dg):
                        outs[1][...] += dg
                if r_ref is not None:
                    r = r + r_ref[rows, :]
                if rope is not None:
                    c, sa, sb = (t[rows, :] for t in ex[-3:])
                    for h in range(tn // HQ):
                        lo = h * HQ
                        outs[0][rows, lo:lo + LANES] = (r[:, lo:lo + LANES] * QK_SCALE).astype(out_dtype)
                        outs[0][rows, lo + LANES:lo + HQ] = (_rot(r[:, lo + LANES:lo + HQ], c, sa, sb) * QK_SCALE).astype(out_dtype)
                    continue
                outs[0][rows, :] = r.astype(out_dtype)
                if also_bf16:
                    outs[1][rows, :] = r.astype(bf16)
                if norm_g is not None:
                    outs[1][rows, :] = _rms(r, ex[-1][...]).astype(bf16)

        if nk > 1:
            pl.when(k == nk - 1)(epilogue)
        else:
            epilogue()

    a_spec = pl.BlockSpec((tk, tm), lambda i, j, k: (k, i)) if ta else pl.BlockSpec((tm, tk), lambda i, j, k: (i, k))
    b_spec = pl.BlockSpec((tn, tk), lambda i, j, k: (j, k)) if tb else pl.BlockSpec((tk, tn), lambda i, j, k: (k, j))
    if rows:
        b_spec = pl.BlockSpec((tn, tk) if tb else (tk, tn), lambda i, j, k: (0, 0), pipeline_mode=pl.Buffered(1))
    if both:
        a_spec = pl.BlockSpec((2, tm, K // 2), lambda i, j, k: (0, i, 0))
    elif a.ndim == 3:
        kh = K // 2 // tk
        a_spec = pl.BlockSpec((None, tm, tk), lambda i, j, k: (k // kh, i, k % kh))
    if b.ndim == 3:
        nh = N // 2 // tn
        b_spec = pl.BlockSpec((None, tk, tn), lambda i, j, k: (j // nh, k, j % nh))
    o_spec = pl.BlockSpec((tm, tn), lambda i, j, k: (i, j))
    in_specs, args = [a_spec, b_spec], [a, b]
    if res is not None:
        in_specs.append(o_spec)
        args.append(res)
    if dep is not None:
        in_specs.append(pl.BlockSpec(memory_space=pl.ANY))
        args.append(dep)
    vec = pl.BlockSpec((1, tn), lambda i, j, k: (0, j))
    if norm_bwd is not None:
        in_specs += [o_spec, vec]
    if norm_g is not None:
        in_specs.append(vec)
    if rope is not None:
        in_specs += [pl.BlockSpec((tm, LANES), lambda i, j, k: (i, 0))] * 3
    args += extra
    out_shape = [jax.ShapeDtypeStruct((M, N), out_dtype)]
    out_specs = [o_spec]
    if also_bf16 or norm_g is not None:
        out_shape.append(jax.ShapeDtypeStruct((M, N), bf16))
        out_specs.append(o_spec)
    if norm_bwd is not None:
        out_shape.append(jax.ShapeDtypeStruct((1, N), f32))
        out_specs.append(vec)
    dims = ("arbitrary" if norm_bwd is not None else "parallel", "parallel", "arbitrary")
    out = pl.pallas_call(
        body, name=name, grid=(M // tm, N // tn, nk), in_specs=in_specs, out_specs=out_specs, out_shape=out_shape,
        scratch_shapes=[pltpu.VMEM((tm, tn), f32)] if nk > 1 else [], compiler_params=_cp(*dims),
    )(*args)
    return out if len(out) > 1 else out[0]


def _rms_fwd(x, g, *, name, tm=512):
    T, width = x.shape
    tm = _tile(T, tm)

    def body(x_ref, g_ref, o_ref):
        xv = x_ref[...]
        r = lax.rsqrt(jnp.mean(xv * xv, axis=-1, keepdims=True) + EPS)
        o_ref[...] = (xv * r * g_ref[...]).astype(bf16)

    return pl.pallas_call(
        body, name=name, grid=(T // tm,),
        in_specs=[pl.BlockSpec((tm, width), lambda i: (i, 0)), pl.BlockSpec((1, width), lambda i: (0, 0))],
        out_specs=pl.BlockSpec((tm, width), lambda i: (i, 0)), out_shape=jax.ShapeDtypeStruct((T, width), bf16),
        compiler_params=_cp("parallel"),
    )(x, g)


def _rms_bwd_math(xv, g, dy):
    r = lax.rsqrt(jnp.mean(xv * xv, axis=-1, keepdims=True) + EPS)
    xh = xv * r
    dxh = dy * g
    dx = r * (dxh - xh * jnp.mean(dxh * xh, axis=-1, keepdims=True))
    dg = jnp.sum(dy * xh, axis=0, keepdims=True)
    return dx, dg


def _loss_head(h, g, target, *, tm=512):
    T, D = h.shape
    tm = _tile(T, tm)

    def body(h_ref, g_ref, t_ref, loss_ref, dh_ref, dg_ref):
        hv, gv = h_ref[...], g_ref[...]
        r = lax.rsqrt(jnp.mean(hv * hv, axis=-1, keepdims=True) + EPS)
        e = hv * r * gv - t_ref[...]
        part = 0.5 * jnp.sum(jnp.mean(e * e, axis=-1, keepdims=True), axis=0, keepdims=True)
        dx, dg = _rms_bwd_math(hv, gv, e * (1.0 / D))
        dh_ref[...] = dx

        @pl.when(pl.program_id(0) == 0)
        def _():
            loss_ref[...] = part
            dg_ref[...] = dg

        @pl.when(pl.program_id(0) > 0)
        def _():
            loss_ref[...] += part
            dg_ref[...] += dg

    row = pl.BlockSpec((tm, D), lambda i: (i, 0))
    vec = pl.BlockSpec((1, D), lambda i: (0, 0))
    return pl.pallas_call(
        body, name="loss_head", grid=(T // tm,), in_specs=[row, vec, row],
        out_specs=[pl.BlockSpec((1, 1), lambda i: (0, 0)), row, vec],
        out_shape=[jax.ShapeDtypeStruct((1, 1), f32), jax.ShapeDtypeStruct((T, D), f32), jax.ShapeDtypeStruct((1, D), f32)],
        compiler_params=_cp("arbitrary"),
    )(h, g, target)


FFN_W = 2 * LANES
FFN_ROWS = 128
HALO = 2 * SUBLANES


def _conv_taps(a_ref, c, rc):
    if isinstance(c, int) and c == 0:
        ext = jnp.concatenate([jnp.zeros((HALO, FFN_W), f32), a_ref[pl.ds(0, rc), :].astype(f32)], axis=0)
    else:
        ext = a_ref[pl.ds(pl.multiple_of(c * rc - HALO, HALO), rc + HALO), :].astype(f32)
    return ext[HALO:], pltpu.roll(ext, 1, 0)[HALO:], pltpu.roll(ext, 2, 0)[HALO:]


def _chunk_rows(c, rc):
    return pl.ds(c * rc, rc) if isinstance(c, int) else pl.ds(pl.multiple_of(c * rc, rc), rc)


def _ffn_mid_fwd(au, cw, cb, *, name):
    T = au.shape[0]
    F = au.shape[1] // 2
    nb = F // FFN_W
    rc = min(FFN_ROWS, T)
    nc = T // rc

    def body(a_ref, u_ref, w_ref, b_ref, z_ref):
        w, b = w_ref[...], b_ref[...]

        def chunk(c):
            a, a1, a2 = _conv_taps(a_ref, c, rc)
            rows = _chunk_rows(c, rc)
            ac = (w[0:1] * a2 + w[1:2] * a1 + w[2:3] * a + b).astype(bf16)
            z_ref[rows, :] = ac * jax.nn.sigmoid(ac) * u_ref[rows, :]

        chunk(0)
        lax.fori_loop(1, nc, lambda c, _: chunk(c), None)

    return pl.pallas_call(
        body, name=name, grid=(nb,),
        in_specs=[pl.BlockSpec((T, FFN_W), lambda j: (0, j)), pl.BlockSpec((T, FFN_W), lambda j: (0, nb + j)),
                  pl.BlockSpec((3, FFN_W), lambda j: (0, j)), pl.BlockSpec((1, FFN_W), lambda j: (0, j))],
        out_specs=pl.BlockSpec((T, FFN_W), lambda j: (0, j)), out_shape=jax.ShapeDtypeStruct((T, F), bf16),
        compiler_params=_cp("parallel"),
    )(au, au, cw, cb)


def _ffn_mid_bwd(au, cw, cb, dz, *, name):
    T = au.shape[0]
    F = au.shape[1] // 2
    nb = F // FFN_W
    rc = min(FFN_ROWS, T)
    nc = T // rc

    def body(a_ref, u_ref, w_ref, b_ref, dz_ref, dau_ref, dw_ref, db_ref):
        w, b = w_ref[...], b_ref[...]

        def chunk(c, carry):
            nxt, s0, s1, s2, sb = carry
            a, a1, a2 = _conv_taps(a_ref, c, rc)
            rows = _chunk_rows(c, rc)
            ac = (w[0:1] * a2 + w[1:2] * a1 + w[2:3] * a + b).astype(bf16)
            sg = jax.nn.sigmoid(ac)
            dz = dz_ref[rows, :]
            dau_ref[1, rows, :] = dz * ac * sg
            dac = (dz * u_ref[rows, :] * sg * (1.0 + ac * (1.0 - sg))).astype(f32)
            ext = jnp.concatenate([dac, nxt], axis=0)
            d1, d2 = pltpu.roll(ext, rc + HALO - 1, 0)[:rc], pltpu.roll(ext, rc + HALO - 2, 0)[:rc]
            dau_ref[0, rows, :] = (w[2:3] * dac + w[1:2] * d1 + w[0:1] * d2).astype(bf16)
            tot = lambda v: jnp.sum(v, axis=0, keepdims=True)
            return dac[:HALO], s0 + tot(dac * a2), s1 + tot(dac * a1), s2 + tot(dac * a), sb + tot(dac)

        z = jnp.zeros((1, FFN_W), f32)
        carry = (jnp.zeros((HALO, FFN_W), f32), z, z, z, z)
        carry = lax.fori_loop(0, nc - 1, lambda k, cr: chunk(nc - 1 - k, cr), carry)
        _, s0, s1, s2, sb = chunk(0, carry)
        rows = lax.broadcasted_iota(jnp.int32, (3, FFN_W), 0)
        dw_ref[...] = jnp.where(rows == 0, s0, jnp.where(rows == 1, s1, s2))
        db_ref[...] = sb

    col = lambda off: pl.BlockSpec((T, FFN_W), lambda j: (0, off + j))
    return pl.pallas_call(
        body, name=name, grid=(nb,),
        in_specs=[col(0), col(nb), pl.BlockSpec((3, FFN_W), lambda j: (0, j)), pl.BlockSpec((1, FFN_W), lambda j: (0, j)), col(0)],
        out_specs=[pl.BlockSpec((2, T, FFN_W), lambda j: (0, 0, j)), pl.BlockSpec((3, FFN_W), lambda j: (0, j)),
                   pl.BlockSpec((1, FFN_W), lambda j: (0, j))],
        out_shape=[jax.ShapeDtypeStruct((2, T, F), bf16), jax.ShapeDtypeStruct((3, F), f32), jax.ShapeDtypeStruct((1, F), f32)],
        compiler_params=_cp("parallel"),
    )(au, au, cw, cb, dz)


BNN = (((2,), (1,)), ((0,), (0,)))
BNT = (((2,), (2,)), ((0,), (0,)))
BTN = (((1,), (1,)), ((0,), (0,)))


def _heads(x):
    return jnp.stack([x[:, h * LANES:(h + 1) * LANES] for h in range(HGRN_HEADS)])


def _put_heads(ref, rows, x, dtype):
    for h in range(HGRN_HEADS):
        ref[rows, h * LANES:(h + 1) * LANES] = x[h].astype(dtype)


def _hgrn_lb(l):
    m = jnp.max(l, axis=0, keepdims=True)
    e = jnp.exp(l - m)
    return e[0:1] / jnp.sum(e, axis=0, keepdims=True)


def _hgrn_chunk(q, fx, lb):
    H, C = q.shape[0], q.shape[1]
    sg = jax.nn.sigmoid(fx)
    F = lb + (1.0 - lb) * sg
    k = 1.0 - F
    logF = jnp.log(F)
    r = lax.broadcasted_iota(jnp.int32, (H, C, C), 1)
    c = lax.broadcasted_iota(jnp.int32, (H, C, C), 2)
    tril = (r >= c)
    b = _dot(tril.astype(f32), logF, BNN, precision=HI)
    bl = jnp.sum(logF, axis=1, keepdims=True)
    eb = jnp.exp(b)
    enb = jnp.exp(-b)
    elb = jnp.exp(bl - b)
    return dict(sg=sg, F=F, k=k, b=b, bl=bl, eb=eb, enb=enb, elb=elb, qd=q * eb, kd=k * enb, kl=k * elb, tril=tril)


def _hgrn_fwd(proj, lbl, ng, *, rb=512):
    T = proj.shape[0]
    rb = min(rb, T)
    cpb = rb // HGRN_CHUNK
    nblk = T // rb
    H = HGRN_HEADS

    def body(q_ref, f_ref, i_ref, g_ref, lbl_ref, ng_ref, y_ref, st_ref, S):
        @pl.when(pl.program_id(0) == 0)
        def _():
            S[...] = jnp.zeros_like(S)

        lb = _heads(_hgrn_lb(lbl_ref[...]))
        ngv = _heads(ng_ref[...])
        for c in range(cpb):
            sl = pl.ds(c * HGRN_CHUNK, HGRN_CHUNK)
            v, gx = _heads(i_ref[sl, :]), _heads(g_ref[sl, :])
            ch = _hgrn_chunk(_heads(q_ref[sl, :]), _heads(f_ref[sl, :]), lb)
            att = jnp.where(ch["tril"], _bdot(ch["qd"], ch["kd"], BNT), 0.0)
            St = S[...]
            st_ref[:, c] = St
            o = _bdot(att, v, BNN) + _bdot(ch["qd"], St, BNT)
            S[...] = St * jnp.exp(ch["bl"]) + _bdot(v, ch["kl"], BTN)
            r = lax.rsqrt(jnp.mean(o * o, axis=-1, keepdims=True) + EPS)
            _put_heads(y_ref, sl, o * r * ngv * (gx * jax.nn.sigmoid(gx)), bf16)

    col = lambda off: pl.BlockSpec((rb, H * LANES), lambda n: (n, off))
    return pl.pallas_call(
        body, name="hgrn_fwd", grid=(nblk,),
        in_specs=[col(0), col(1), col(2), col(3), pl.BlockSpec((2, H * LANES), lambda n: (0, 0)),
                  pl.BlockSpec((1, H * LANES), lambda n: (0, 0))],
        out_specs=[pl.BlockSpec((rb, H * LANES), lambda n: (n, 0)),
                   pl.BlockSpec((H, cpb, LANES, LANES), lambda n: (0, n, 0, 0))],
        out_shape=[jax.ShapeDtypeStruct((T, H * LANES), bf16),
                   jax.ShapeDtypeStruct((H, T // HGRN_CHUNK, LANES, LANES), f32)],
        scratch_shapes=[pltpu.VMEM((H, LANES, LANES), f32)], compiler_params=_cp("arbitrary"),
    )(proj, proj, proj, proj, lbl, ng)


def _hgrn_bwd(proj, lbl, ng, states, dy, *, rb=512):
    T = proj.shape[0]
    rb = min(rb, T)
    cpb = rb // HGRN_CHUNK
    nblk = T // rb
    H = HGRN_HEADS
    C = HGRN_CHUNK

    def body(q_ref, f_ref, i_ref, g_ref, lbl_ref, ng_ref, st_ref, dy_ref,
             dq_ref, df_ref, di_ref, dg_ref, dl_ref, dng_ref, dS, dlb_acc, dng_acc):
        n = pl.program_id(0)

        @pl.when(n == 0)
        def _():
            dS[...] = jnp.zeros_like(dS)
            dlb_acc[...] = jnp.zeros_like(dlb_acc)
            dng_acc[...] = jnp.zeros_like(dng_acc)

        lb_row = _hgrn_lb(lbl_ref[...])
        lb = _heads(lb_row)
        ngv = _heads(ng_ref[...])
        r_i = lax.broadcasted_iota(jnp.int32, (H, C, C), 1)
        c_i = lax.broadcasted_iota(jnp.int32, (H, C, C), 2)
        triu = (c_i >= r_i).astype(f32)
        rows_sum = lambda x: jnp.sum(x, axis=1, keepdims=True)
        for c in reversed(range(cpb)):
            sl = pl.ds(c * C, C)
            q, v, gx = _heads(q_ref[sl, :]), _heads(i_ref[sl, :]), _heads(g_ref[sl, :])
            ch = _hgrn_chunk(q, _heads(f_ref[sl, :]), lb)
            qd, kd, kl = ch["qd"], ch["kd"], ch["kl"]
            att = jnp.where(ch["tril"], _bdot(qd, kd, BNT), 0.0)
            St = st_ref[:, c]
            o = _bdot(att, v, BNN) + _bdot(qd, St, BNT)
            r = lax.rsqrt(jnp.mean(o * o, axis=-1, keepdims=True) + EPS)
            on = o * r
            sgg = jax.nn.sigmoid(gx)
            gate = gx * sgg
            dyv = _heads(dy_ref[sl, :].astype(f32))
            _put_heads(dg_ref, sl, dyv * on * ngv * sgg * (1.0 + gx * (1.0 - sgg)), bf16)
            dng_acc[...] += rows_sum(dyv * on * gate)
            don = dyv * ngv * gate
            do = r * (don - on * jnp.mean(don * on, axis=-1, keepdims=True))
            dSt = dS[...]
            dA = jnp.where(ch["tril"], _bdot(do, v, BNT), 0.0)
            dv = _bdot(att, do, BTN) + _bdot(kl, dSt, BNT)
            dqd = _bdot(dA, kd, BNN) + _bdot(do, St, BNN)
            dkd = _bdot(dA, qd, BTN)
            dkl = _bdot(v, dSt, BNN)
            dec = jnp.exp(ch["bl"])
            ddec = rows_sum(St * dSt)
            dS[...] = _bdot(do, qd, BTN) + dSt * dec
            dB = dqd * qd - dkd * kd - dkl * kl
            dbl = rows_sum(dkl * kl) + ddec * dec
            dk = dkd * ch["enb"] + dkl * ch["elb"]
            dlogF = _dot(triu, dB, BNN, precision=HI) + dbl
            dF = dlogF / ch["F"] - dk
            sg = ch["sg"]
            _put_heads(dq_ref, sl, dqd * ch["eb"], bf16)
            _put_heads(di_ref, sl, dv, bf16)
            _put_heads(df_ref, sl, dF * (1.0 - lb) * sg * (1.0 - sg), bf16)
            dlb_acc[...] += rows_sum(dF * (1.0 - sg))

        @pl.when(n == nblk - 1)
        def _():
            rows = lax.broadcasted_iota(jnp.int32, (2, LANES), 0)
            for h in range(H):
                hs = pl.ds(h * LANES, LANES)
                lbh = lb_row[:, h * LANES:(h + 1) * LANES]
                dl0 = dlb_acc[h] * lbh * (1.0 - lbh)
                dl_ref[:, hs] = jnp.where(rows == 0, dl0, -dl0)
                dng_ref[:, hs] = dng_acc[h]

    col = lambda off: pl.BlockSpec((rb, H * LANES), lambda n: (nblk - 1 - n, off))
    vec = lambda rows: pl.BlockSpec((rows, H * LANES), lambda n: (0, 0))
    tok = jax.ShapeDtypeStruct((T, H * LANES), bf16)
    return pl.pallas_call(
        body, name="hgrn_bwd", grid=(nblk,),
        in_specs=[col(0), col(1), col(2), col(3), vec(2), vec(1),
                  pl.BlockSpec((H, cpb, LANES, LANES), lambda n: (0, nblk - 1 - n, 0, 0)), col(0)],
        out_specs=[col(0), col(0), col(0), col(0), vec(2), vec(1)],
        out_shape=[tok, tok, tok, tok, jax.ShapeDtypeStruct((2, H * LANES), f32), jax.ShapeDtypeStruct((1, H * LANES), f32)],
        scratch_shapes=[pltpu.VMEM((H, LANES, LANES), f32), pltpu.VMEM((H, 1, LANES), f32), pltpu.VMEM((H, 1, LANES), f32)],
        compiler_params=_cp("arbitrary"),
    )(proj, proj, proj, proj, lbl, ng, states, dy)


def _s5_disc_math(ar, ai, ldt, br, bi):
    dt = jnp.exp(ldt)
    mag = jnp.exp(ar * dt)
    abr, abi = mag * jnp.cos(ai * dt), mag * jnp.sin(ai * dt)
    den = ar * ar + ai * ai
    xr, xi = abr - 1.0, abi
    cr = (xr * ar + xi * ai) / den
    ci = (xi * ar - xr * ai) / den
    return abr, abi, cr * br - ci * bi, cr * bi + ci * br


def _s5_disc_fwd(ar, ai, ldt, br, bi):
    def body(ar_ref, ai_ref, ldt_ref, br_ref, bi_ref, o0, o1, o2, o3):
        outs = _s5_disc_math(ar_ref[...], ai_ref[...], ldt_ref[...], br_ref[...], bi_ref[...])
        for o, v in zip((o0, o1, o2, o3), outs):
            o[...] = v

    return pl.pallas_call(
        body, name="s5_disc_fwd",
        out_shape=[jax.ShapeDtypeStruct(ar.shape, f32)] * 2 + [jax.ShapeDtypeStruct(br.shape, f32)] * 2,
    )(ar, ai, ldt, br, bi)


def _s5_disc_bwd(ar, ai, ldt, br, bi, cts):
    def body(ar_ref, ai_ref, ldt_ref, br_ref, bi_ref, c0, c1, c2, c3, o0, o1, o2, o3, o4):
        _, vjp = jax.vjp(_s5_disc_math, ar_ref[...], ai_ref[...], ldt_ref[...], br_ref[...], bi_ref[...])
        for o, v in zip((o0, o1, o2, o3, o4), vjp((c0[...], c1[...], c2[...], c3[...]))):
            o[...] = v

    return pl.pallas_call(
        body, name="s5_disc_bwd",
        out_shape=[jax.ShapeDtypeStruct(ar.shape, f32)] * 3 + [jax.ShapeDtypeStruct(br.shape, f32)] * 2,
    )(ar, ai, ldt, br, bi, *cts)


S5_LC = 512
S5_NLC = S5_N // S5_LC
S5_UB = 4
S5_UNROLL = 4
S5_TOGETHER = 2


def _cmul(ar, ai, xr, xi):
    return ar * xr - ai * xi, ar * xi + ai * xr


def _cpow(ar, ai, n):
    rr, ri = None, None
    br, bi = ar, ai
    while n:
        if n & 1:
            rr, ri = (br, bi) if rr is None else _cmul(rr, ri, br, bi)
        n >>= 1
        if n:
            br, bi = _cmul(br, bi, br, bi)
    return rr, ri


def _s5_bu(u_ref, bre_ref, bim_ref, xr, xi):
    for k in range(S5_UB):
        uk = u_ref[:, k * LANES:(k + 1) * LANES].astype(bf16)
        xr[:, k * S5_LC:(k + 1) * S5_LC] = _dot(uk, bre_ref[k])
        xi[:, k * S5_LC:(k + 1) * S5_LC] = _dot(uk, bim_ref[k])


def _s5_scan(xr, xi, sr, si, ar_ref, ai_ref, nsteps, store):
    for c0 in range(0, S5_NLC, S5_TOGETHER):
        css = [slice(c * S5_LC, (c + 1) * S5_LC) for c in range(c0, c0 + S5_TOGETHER)]
        a = [(jnp.broadcast_to(ar_ref[:, cs], (S5_SEG, S5_LC)), jnp.broadcast_to(ai_ref[:, cs], (S5_SEG, S5_LC))) for cs in css]

        def step(j, carry, css=css, a=a):
            rows = pl.ds(pl.multiple_of(j * S5_SEG, S5_SEG), S5_SEG)
            out = []
            for u, cs in enumerate(css):
                (a_r, a_i), pr, pi = a[u], carry[2 * u], carry[2 * u + 1]
                nr = a_r * pr - a_i * pi + xr[rows, cs]
                ni = a_r * pi + a_i * pr + xi[rows, cs]
                if store:
                    xr[rows, cs] = nr
                    xi[rows, cs] = ni
                out += [nr, ni]
            return tuple(out)

        init = tuple(v for cs in css for v in (sr[:, cs], si[:, cs]))
        fin = lax.fori_loop(0, nsteps, step, init)
        for u, cs in enumerate(css):
            sr[:, cs] = fin[2 * u]
            si[:, cs] = fin[2 * u + 1]


def _s5_rscan(dr, di, xr, xi, s0r, s0i, gr, gi, acc_r, acc_i, ar_ref, ai_ref, nsteps):
    for c in range(S5_NLC):
        cs = slice(c * S5_LC, (c + 1) * S5_LC)
        a_r = jnp.broadcast_to(ar_ref[:, cs], (S5_SEG, S5_LC))
        a_i = jnp.broadcast_to(ai_ref[:, cs], (S5_SEG, S5_LC))

        def update(j, carry, before, cs=cs, a_r=a_r, a_i=a_i):
            pr, pi, cr, ci = carry
            rows = pl.ds(j * S5_SEG if isinstance(j, int) else pl.multiple_of(j * S5_SEG, S5_SEG), S5_SEG)
            nr = dr[rows, cs] + a_r * pr + a_i * pi
            ni = di[rows, cs] + a_r * pi - a_i * pr
            dr[rows, cs] = nr
            di[rows, cs] = ni
            if before is not None:
                cr = cr + nr * before[0] + ni * before[1]
                ci = ci - nr * before[1] + ni * before[0]
            return nr, ni, cr, ci

        def step(jj, carry, cs=cs, update=update):
            j = nsteps - 1 - jj
            prev = pl.ds(pl.multiple_of((j - 1) * S5_SEG, S5_SEG), S5_SEG)
            return update(j, carry, None if acc_r is None else (xr[prev, cs], xi[prev, cs]))

        z = jnp.zeros((S5_SEG, S5_LC), f32)
        init = (gr[:, cs], gi[:, cs], z, z)
        carry = lax.fori_loop(0, nsteps - 1, step, init, unroll=S5_UNROLL)
        fr, fi, cr, ci = update(0, carry, None if acc_r is None else (s0r[:, cs], s0i[:, cs]))
        gr[:, cs] = fr
        gi[:, cs] = fi
        if acc_r is not None:
            acc_r[:, cs] += cr
            acc_i[:, cs] += ci


def _s5_seg_carry(fr, fi, ar, ai, seg_len, reverse):
    pr, pi = _cpow(ar, ai if not reverse else -ai, seg_len)
    rows = lax.broadcasted_iota(jnp.int32, fr.shape, 0)
    cr, ci = jnp.zeros_like(fr), jnp.zeros_like(fi)
    sh = (S5_SEG - 1) if reverse else 1
    fr_s, fi_s = pltpu.roll(fr, sh, 0), pltpu.roll(fi, sh, 0)
    order = range(S5_SEG - 2, -1, -1) if reverse else range(1, S5_SEG)
    for r in order:
        c_r, c_i = pltpu.roll(cr, sh, 0), pltpu.roll(ci, sh, 0)
        m_r, m_i = _cmul(pr, pi, c_r, c_i)
        cr = jnp.where(rows == r, m_r + fr_s, cr)
        ci = jnp.where(rows == r, m_i + fi_s, ci)
    return cr, ci


def _gelu_parts(y):
    c0 = math.sqrt(2.0 / math.pi)
    t = jnp.tanh(c0 * (y + 0.044715 * y * y * y))
    z = 0.5 * y * (1.0 + t)
    dz = 0.5 * (1.0 + t) + 0.5 * y * (1.0 - t * t) * c0 * (1.0 + 3.0 * 0.044715 * y * y)
    return z, dz


def _s5_y(xr, xi, u_ref, cre_ref, cim_ref, d_ref):
    ys = []
    for k in range(S5_UB):
        cs = slice(k * S5_LC, (k + 1) * S5_LC)
        ys.append(_bdot(xr[:, cs], cre_ref[k]) - _bdot(xi[:, cs], cim_ref[k]))
    return jnp.concatenate(ys, axis=1) + d_ref[...] * u_ref[...]


def _s5_specs(T, rb, rev=False):
    nblk = T // rb
    blk = (lambda i: (nblk - 1 - i, 0)) if rev else (lambda i: (i, 0))
    tok = pl.BlockSpec((rb, 4 * LANES), blk)
    bmat = pl.BlockSpec((S5_UB, LANES, S5_LC), lambda i: (0, 0, 0))
    cmat = pl.BlockSpec((S5_UB, S5_LC, LANES), lambda i: (0, 0, 0))
    avec = pl.BlockSpec((1, S5_N), lambda i: (0, 0))
    seg = pl.BlockSpec((S5_SEG, S5_N), lambda i: (0, 0))
    cvec = pl.BlockSpec((1, 4 * LANES), lambda i: (0, 0))
    s0 = pl.BlockSpec((1, S5_SEG, S5_N), (lambda i: (nblk - 1 - i, 0, 0)) if rev else (lambda i: (i, 0, 0)))
    return dict(tok=tok, bmat=bmat, cmat=cmat, avec=avec, seg=seg, cvec=cvec, s0=s0, nblk=nblk)


def _s5_final(u, bre, bim, ar, ai, *, rb):
    T = u.shape[0]
    sp = _s5_specs(T, rb)

    def body(u_ref, bre_ref, bim_ref, ar_ref, ai_ref, fr_ref, fi_ref, xr, xi):
        @pl.when(pl.program_id(0) == 0)
        def _():
            fr_ref[...] = jnp.zeros_like(fr_ref)
            fi_ref[...] = jnp.zeros_like(fi_ref)

        _s5_bu(u_ref, bre_ref, bim_ref, xr, xi)
        _s5_scan(xr, xi, fr_ref, fi_ref, ar_ref, ai_ref, rb // S5_SEG, False)

    return pl.pallas_call(
        body, name="s5_final", grid=(sp["nblk"],),
        in_specs=[sp["tok"], sp["bmat"], sp["bmat"], sp["avec"], sp["avec"]], out_specs=[sp["seg"], sp["seg"]],
        out_shape=[jax.ShapeDtypeStruct((S5_SEG, S5_N), f32)] * 2,
        scratch_shapes=[pltpu.VMEM((rb, S5_N), f32)] * 2, compiler_params=_cp("arbitrary"),
    )(u, bre, bim, ar, ai)


def _s5_fwd(u, bre, bim, ar, ai, fr, fi, cre, cim, dsk, wg, bg, *, rb):
    T = u.shape[0]
    sp = _s5_specs(T, rb)
    seg_len = T // S5_SEG

    def body(u_ref, bre_ref, bim_ref, ar_ref, ai_ref, fr_ref, fi_ref, cre_ref, cim_ref, d_ref, wg_ref, bg_ref,
             o_ref, s0r_ref, s0i_ref, xr, xi, sr, si):
        @pl.when(pl.program_id(0) == 0)
        def _():
            i_r, i_i = _s5_seg_carry(fr_ref[...], fi_ref[...], ar_ref[...], ai_ref[...], seg_len, False)
            sr[...] = i_r
            si[...] = i_i

        s0r_ref[0] = sr[...]
        s0i_ref[0] = si[...]
        _s5_bu(u_ref, bre_ref, bim_ref, xr, xi)
        _s5_scan(xr, xi, sr, si, ar_ref, ai_ref, rb // S5_SEG, True)
        y = _s5_y(xr, xi, u_ref, cre_ref, cim_ref, d_ref)
        z, _ = _gelu_parts(y)
        v = _bdot(z, wg_ref[...]) + bg_ref[...]
        o_ref[...] = (z * jax.nn.sigmoid(v)).astype(bf16)

    wspec = pl.BlockSpec((4 * LANES, 4 * LANES), lambda i: (0, 0))
    return pl.pallas_call(
        body, name="s5_fwd", grid=(sp["nblk"],),
        in_specs=[sp["tok"], sp["bmat"], sp["bmat"], sp["avec"], sp["avec"], sp["seg"], sp["seg"], sp["cmat"], sp["cmat"],
                  sp["cvec"], wspec, sp["cvec"]],
        out_specs=[sp["tok"], sp["s0"], sp["s0"]],
        out_shape=[jax.ShapeDtypeStruct((T, 4 * LANES), bf16)] + [jax.ShapeDtypeStruct((sp["nblk"], S5_SEG, S5_N), f32)] * 2,
        scratch_shapes=[pltpu.VMEM((rb, S5_N), f32)] * 2 + [pltpu.VMEM((S5_SEG, S5_N), f32)] * 2,
        compiler_params=_cp("arbitrary"),
    )(u, bre, bim, ar, ai, fr, fi, cre, cim, dsk, wg, bg)


def _s5_bwd_a(u, bre, bim, ar, ai, s0r, s0i, cre, cim, cret, cimt, dsk, wg, bg, dout, *, rb):
    T = u.shape[0]
    sp = _s5_specs(T, rb, rev=True)

    def body(u_ref, bre_ref, bim_ref, ar_ref, ai_ref, s0r_ref, s0i_ref, cre_ref, cim_ref, cret_ref, cimt_ref,
             d_ref, wg_ref, bg_ref, do_ref, dy_ref, glr_ref, gli_ref, dcre_ref, dcim_ref, dd_ref, dwg_ref, dbg_ref,
             xr, xi, dr, di, sr, si):
        @pl.when(pl.program_id(0) == 0)
        def _():
            for r in (glr_ref, gli_ref, dcre_ref, dcim_ref, dd_ref, dwg_ref, dbg_ref):
                r[...] = jnp.zeros_like(r)

        sr[...] = s0r_ref[0]
        si[...] = s0i_ref[0]
        _s5_bu(u_ref, bre_ref, bim_ref, xr, xi)
        _s5_scan(xr, xi, sr, si, ar_ref, ai_ref, rb // S5_SEG, True)
        uv = u_ref[...]
        y = _s5_y(xr, xi, u_ref, cre_ref, cim_ref, d_ref)
        z, gz = _gelu_parts(y)
        v = _bdot(z, wg_ref[...]) + bg_ref[...]
        sg = jax.nn.sigmoid(v)
        dov = do_ref[...].astype(f32)
        dv = dov * z * sg * (1.0 - sg)
        dz = dov * sg + _bdot(dv, wg_ref[...], NT)
        dy = dz * gz
        dy_ref[...] = dy
        dwg_ref[...] += _bdot(z, dv, TN)
        dbg_ref[...] += jnp.sum(dv, axis=0, keepdims=True)
        dd_ref[...] += jnp.sum(dy * uv, axis=0, keepdims=True)
        for k in range(S5_UB):
            cs = slice(k * S5_LC, (k + 1) * S5_LC)
            dyk = dy[:, k * LANES:(k + 1) * LANES]
            dcre_ref[k] += _bdot(xr[:, cs], dyk, TN)
            dcim_ref[k] -= _bdot(xi[:, cs], dyk, TN)
            dr[:, cs] = _bdot(dyk, cret_ref[k])
            di[:, cs] = -_bdot(dyk, cimt_ref[k])
        _s5_rscan(dr, di, None, None, None, None, glr_ref, gli_ref, None, None, ar_ref, ai_ref, rb // S5_SEG)

    wspec = pl.BlockSpec((4 * LANES, 4 * LANES), lambda i: (0, 0))
    return pl.pallas_call(
        body, name="s5_bwd_a", grid=(sp["nblk"],),
        in_specs=[sp["tok"], sp["bmat"], sp["bmat"], sp["avec"], sp["avec"], sp["s0"], sp["s0"], sp["cmat"], sp["cmat"],
                  sp["bmat"], sp["bmat"], sp["cvec"], wspec, sp["cvec"], sp["tok"]],
        out_specs=[sp["tok"], sp["seg"], sp["seg"], sp["cmat"], sp["cmat"], sp["cvec"], wspec, sp["cvec"]],
        out_shape=[jax.ShapeDtypeStruct((T, 4 * LANES), f32)] + [jax.ShapeDtypeStruct((S5_SEG, S5_N), f32)] * 2
        + [jax.ShapeDtypeStruct((S5_UB, S5_LC, LANES), f32)] * 2
        + [jax.ShapeDtypeStruct((1, 4 * LANES), f32), jax.ShapeDtypeStruct((4 * LANES, 4 * LANES), f32),
           jax.ShapeDtypeStruct((1, 4 * LANES), f32)],
        scratch_shapes=[pltpu.VMEM((rb, S5_N), f32)] * 4 + [pltpu.VMEM((S5_SEG, S5_N), f32)] * 2,
        compiler_params=_cp("arbitrary"),
    )(u, bre, bim, ar, ai, s0r, s0i, cre, cim, cret, cimt, dsk, wg, bg, dout)


def _s5_bwd_b(u, bre, bim, bret, bimt, ar, ai, s0r, s0i, glr, gli, cret, cimt, dsk, dy, *, rb):
    T = u.shape[0]
    sp = _s5_specs(T, rb, rev=True)
    seg_len = T // S5_SEG
    nblk = sp["nblk"]

    def body(u_ref, bre_ref, bim_ref, bret_ref, bimt_ref, ar_ref, ai_ref, s0r_ref, s0i_ref, glr_ref, gli_ref,
             cret_ref, cimt_ref, d_ref, dy_ref, du_ref, dbre_ref, dbim_ref, dar_ref, dai_ref,
             xr, xi, dr, di, sr, si, gr, gi, acc_r, acc_i):
        @pl.when(pl.program_id(0) == 0)
        def _():
            x_r, x_i = _s5_seg_carry(glr_ref[...], gli_ref[...], ar_ref[...], ai_ref[...], seg_len, True)
            gr[...] = x_r
            gi[...] = x_i
            acc_r[...] = jnp.zeros_like(acc_r)
            acc_i[...] = jnp.zeros_like(acc_i)
            dbre_ref[...] = jnp.zeros_like(dbre_ref)
            dbim_ref[...] = jnp.zeros_like(dbim_ref)

        sr[...] = s0r_ref[0]
        si[...] = s0i_ref[0]
        _s5_bu(u_ref, bre_ref, bim_ref, xr, xi)
        _s5_scan(xr, xi, sr, si, ar_ref, ai_ref, rb // S5_SEG, True)
        dy = dy_ref[...]
        for k in range(S5_UB):
            cs = slice(k * S5_LC, (k + 1) * S5_LC)
            dyk = dy[:, k * LANES:(k + 1) * LANES]
            dr[:, cs] = _bdot(dyk, cret_ref[k])
            di[:, cs] = -_bdot(dyk, cimt_ref[k])
        sr[...] = s0r_ref[0]
        si[...] = s0i_ref[0]
        _s5_rscan(dr, di, xr, xi, sr, si, gr, gi, acc_r, acc_i, ar_ref, ai_ref, rb // S5_SEG)
        dus = []
        for k in range(S5_UB):
            cs = slice(k * S5_LC, (k + 1) * S5_LC)
            uk = u_ref[:, k * LANES:(k + 1) * LANES]
            dbre_ref[k] += _bdot(uk, dr[:, cs], TN)
            dbim_ref[k] += _bdot(uk, di[:, cs], TN)
            dus.append(_bdot(dr[:, cs], bret_ref[k]) + _bdot(di[:, cs], bimt_ref[k]))
        du_ref[...] = (jnp.concatenate(dus, axis=1) + d_ref[...] * dy).astype(bf16)

        @pl.when(pl.program_id(0) == nblk - 1)
        def _():
            dar_ref[...] = jnp.sum(acc_r[...], axis=0, keepdims=True)
            dai_ref[...] = jnp.sum(acc_i[...], axis=0, keepdims=True)

    return pl.pallas_call(
        body, name="s5_bwd_b", grid=(nblk,),
        in_specs=[sp["tok"], sp["bmat"], sp["bmat"], sp["cmat"], sp["cmat"], sp["avec"], sp["avec"], sp["s0"], sp["s0"],
                  sp["seg"], sp["seg"], sp["bmat"], sp["bmat"], sp["cvec"], sp["tok"]],
        out_specs=[sp["tok"], sp["bmat"], sp["bmat"], sp["avec"], sp["avec"]],
        out_shape=[jax.ShapeDtypeStruct((T, 4 * LANES), bf16)] + [jax.ShapeDtypeStruct((S5_UB, LANES, S5_LC), f32)] * 2
        + [jax.ShapeDtypeStruct((1, S5_N), f32)] * 2,
        scratch_shapes=[pltpu.VMEM((rb, S5_N), f32)] * 4 + [pltpu.VMEM((S5_SEG, S5_N), f32)] * 6,
        compiler_params=_cp("arbitrary"),
    )(u, bre, bim, bret, bimt, ar, ai, s0r, s0i, glr, gli, cret, cimt, dsk, dy)


def _blockdiag(w, transpose=False):
    if transpose:
        w = jnp.swapaxes(w, 1, 2)
    g, a, b = w.shape
    eye = jnp.eye(8, dtype=w.dtype)
    return jnp.einsum("kgab,gj->kgajb", w.reshape(4, 8, a, b), eye).reshape(4, 8 * a, 8 * b)


def _blockdiag_t(m, a, b):
    eye = jnp.eye(8, dtype=m.dtype)
    return jnp.einsum("kgajb,gj->kgab", m.reshape(4, 8, a, 8, b), eye).reshape(32, a, b)


ROT = MLA_ROPE // 2


def _rope_tables(positions):
    freqs = ROPE_THETA ** (-jnp.arange(0, MLA_ROPE, 2, dtype=f32) / MLA_ROPE)
    ang = positions.astype(f32)[:, None] * freqs
    cos, sin, z = jnp.cos(ang), jnp.sin(ang), jnp.zeros_like(ang)
    return (jnp.concatenate([cos, cos, z, z], axis=1), jnp.concatenate([-sin, z, z, z], axis=1),
            jnp.concatenate([z, sin, z, z], axis=1))


def _rot(x, c, sa, sb):
    return x * c + pltpu.roll(x, LANES - ROT, 1) * sa + pltpu.roll(x, ROT, 1) * sb


def _rot_t(dy, c, sa, sb):
    return dy * c + pltpu.roll(dy * sa, ROT, 1) + pltpu.roll(dy * sb, LANES - ROT, 1)


def _rms(xv, g):
    return xv * lax.rsqrt(jnp.mean(xv * xv, axis=-1, keepdims=True) + EPS) * g


QW, KVW = MLA_Q_RANK, MLA_KV_RANK
ODD_PAD = QW + KVW + LANES


def _mla_prep_fwd(proj, qg, kvg, tabs, *, tm=512):
    T = proj.shape[0]
    tm = _tile(T, tm)

    def body(p_ref, qg_ref, kvg_ref, c_ref, sa_ref, sb_ref, cq_ref, ckv_ref, kr_ref):
        cq_ref[...] = _rms(p_ref[:, :QW], qg_ref[...]).astype(bf16)
        ckv_ref[...] = _rms(p_ref[:, QW:QW + KVW], kvg_ref[...]).astype(bf16)
        kr_ref[...] = _rot(p_ref[:, QW + KVW:], c_ref[...], sa_ref[...], sb_ref[...]).astype(bf16)

    row = lambda w: pl.BlockSpec((tm, w), lambda i: (i, 0))
    vec = lambda w: pl.BlockSpec((1, w), lambda i: (0, 0))
    return pl.pallas_call(
        body, name="mla_prep_fwd", grid=(T // tm,),
        in_specs=[row(ODD_PAD), vec(QW), vec(KVW), row(LANES), row(LANES), row(LANES)],
        out_specs=[row(QW), row(KVW), row(LANES)],
        out_shape=[jax.ShapeDtypeStruct((T, QW), bf16), jax.ShapeDtypeStruct((T, KVW), bf16),
                   jax.ShapeDtypeStruct((T, LANES), bf16)],
        compiler_params=_cp("parallel"),
    )(proj, qg, kvg, *tabs)


def _mla_prep_bwd(proj, qg, kvg, tabs, dcqn, dckvn, dkr_heads, *, tm=512):
    T = proj.shape[0]
    tm = _tile(T, tm)

    def body(p_ref, qg_ref, kvg_ref, c_ref, sa_ref, sb_ref, dcq_ref, dckv_ref, dkr_ref, dp_ref, dqg_ref, dkvg_ref):
        dcq, dqg = _rms_bwd_math(p_ref[:, :QW], qg_ref[...], dcq_ref[...])
        dckv, dkvg = _rms_bwd_math(p_ref[:, QW:QW + KVW], kvg_ref[...], dckv_ref[...])
        dk = dkr_ref[:, :LANES]
        for h in range(1, MLA_HEADS):
            dk = dk + dkr_ref[:, h * LANES:(h + 1) * LANES]
        dkr = _rot_t(dk, c_ref[...], sa_ref[...], sb_ref[...])
        dp_ref[...] = jnp.concatenate([dcq, dckv, dkr], axis=1).astype(bf16)

        @pl.when(pl.program_id(0) == 0)
        def _():
            dqg_ref[...] = dqg
            dkvg_ref[...] = dkvg

        @pl.when(pl.program_id(0) > 0)
        def _():
            dqg_ref[...] += dqg
            dkvg_ref[...] += dkvg

    row = lambda w: pl.BlockSpec((tm, w), lambda i: (i, 0))
    vec = lambda w: pl.BlockSpec((1, w), lambda i: (0, 0))
    return pl.pallas_call(
        body, name="mla_prep_bwd", grid=(T // tm,),
        in_specs=[row(ODD_PAD), vec(QW), vec(KVW), row(LANES), row(LANES), row(LANES), row(QW), row(KVW),
                  row(MLA_HEADS * LANES)],
        out_specs=[row(ODD_PAD), vec(QW), vec(KVW)],
        out_shape=[jax.ShapeDtypeStruct((T, ODD_PAD), bf16), jax.ShapeDtypeStruct((1, QW), f32),
                   jax.ShapeDtypeStruct((1, KVW), f32)],
        compiler_params=_cp("arbitrary"),
    )(proj, qg, kvg, *tabs, dcqn, dckvn, dkr_heads)


HQ = 2 * LANES
QK_SCALE = MLA_QK ** -0.5


def _dq_post(dq, tabs, *, tm=512):
    T = dq.shape[0]
    tm = _tile(T, tm)

    def body(q_ref, c_ref, sa_ref, sb_ref, o_ref):
        c, sa, sb = c_ref[...], sa_ref[...], sb_ref[...]
        for h in range(MLA_HEADS):
            nope, rope = pl.ds(h * HQ, LANES), pl.ds(h * HQ + LANES, LANES)
            o_ref[:, nope] = (q_ref[:, nope] * QK_SCALE).astype(bf16)
            o_ref[:, rope] = (_rot_t(q_ref[:, rope], c, sa, sb) * QK_SCALE).astype(bf16)

    tab = pl.BlockSpec((tm, LANES), lambda i: (i, 0))
    blk = pl.BlockSpec((tm, MLA_HEADS * HQ), lambda i: (i, 0))
    return pl.pallas_call(
        body, name="dq_post", grid=(T // tm,), in_specs=[blk, tab, tab, tab], out_specs=blk,
        out_shape=jax.ShapeDtypeStruct(dq.shape, bf16), compiler_params=_cp("parallel"),
    )(dq, *tabs)


def _causal_mask(i, j, tq, tk):
    r = lax.broadcasted_iota(jnp.int32, (tq, tk), 0) + i * tq
    c = lax.broadcasted_iota(jnp.int32, (tq, tk), 1) + j * tk
    return c <= r


FLASH_PARTS = 4


def _flash_fwd(q, kv, kr, *, tq=1024, tk=1024):
    T = q.shape[0]
    tq = _tile(T, tq)
    tk = _tile(tq, tk)
    per = tq // tk
    H = MLA_HEADS

    def body(q_ref, kn_ref, v_ref, kr_ref, o_ref, lse_ref, m_s, acc):
        i, j = pl.program_id(1), pl.program_id(2)
        last = (i + 1) * per - 1

        @pl.when(j == 0)
        def _():
            m_s[...] = jnp.full_like(m_s, -jnp.inf)
            acc[...] = jnp.zeros_like(acc)

        def step(masked):
            k = jnp.concatenate([kn_ref[...], kr_ref[...]], axis=1)
            v1 = jnp.concatenate([v_ref[...], jnp.ones((tk, LANES), bf16)], axis=1)
            mask = _causal_mask(i, j, tq, tk) if masked else None
            for part in range(FLASH_PARTS):
                rows = pl.ds(part * (tq // FLASH_PARTS), tq // FLASH_PARTS)
                s = _dot(q_ref[rows, :], k, NT)
                if masked:
                    s = jnp.where(mask[part * (tq // FLASH_PARTS):(part + 1) * (tq // FLASH_PARTS)], s, -jnp.inf)
                m_new = jnp.maximum(m_s[rows, :], jnp.max(s, axis=-1, keepdims=True))
                alpha = jnp.exp(m_s[rows, :] - m_new)
                p = jnp.exp((s - m_new).astype(bf16))
                acc[rows, :] = alpha * acc[rows, :] + _dot(p, v1)
                m_s[rows, :] = m_new

        pl.when(j < i * per)(functools.partial(step, False))
        pl.when((j >= i * per) & (j <= last))(functools.partial(step, True))

        @pl.when(j == last)
        def _():
            l = acc[:, LANES:]
            o_ref[...] = (acc[:, :LANES] / l).astype(bf16)
            lse_ref[0] = m_s[...] + jnp.log(jnp.max(l, axis=-1, keepdims=True))

    kj = lambda i, j: jnp.minimum(j, (i + 1) * per - 1)
    kblk = lambda off: pl.BlockSpec((tk, LANES), lambda h, i, j: (kj(i, j), 2 * h + off))
    return pl.pallas_call(
        body, name="flash_fwd", grid=(H, T // tq, T // tk),
        in_specs=[pl.BlockSpec((tq, HQ), lambda h, i, j: (i, h)), kblk(0), kblk(1),
                  pl.BlockSpec((tk, LANES), lambda h, i, j: (kj(i, j), 0))],
        out_specs=[pl.BlockSpec((tq, LANES), lambda h, i, j: (i, h)), pl.BlockSpec((1, tq, 1), lambda h, i, j: (h, i, 0))],
        out_shape=[jax.ShapeDtypeStruct((T, H * LANES), bf16), jax.ShapeDtypeStruct((H, T, 1), f32)],
        scratch_shapes=[pltpu.VMEM((tq, 1), f32), pltpu.VMEM((tq, 2 * LANES), f32)],
        compiler_params=_cp("parallel", "parallel", "arbitrary"),
    )(q, kv, kv, kr)


def _flash_bwd(q, kv, kr, o, do, lse, *, tb=1024):
    T = q.shape[0]
    tb = _tile(T, tb)
    nb = T // tb
    H = MLA_HEADS

    def body(q_ref, kn_ref, v_ref, kr_ref, o_ref, do_ref, lse_ref, dkv_ref, dkr_ref, dq_ref, dk_acc, dv_acc):
        j, ii = pl.program_id(1), pl.program_id(2)
        i = jnp.maximum(ii, j)

        @pl.when((j == 0) & (ii == 0))
        def _():
            dq_ref[...] = jnp.zeros_like(dq_ref)

        @pl.when(ii == 0)
        def _():
            dk_acc[...] = jnp.zeros_like(dk_acc)
            dv_acc[...] = jnp.zeros_like(dv_acc)

        def step(masked):
            k = jnp.concatenate([kn_ref[...], kr_ref[...]], axis=1)
            p = jnp.exp((_dot(q_ref[...], k, NT) - lse_ref[0]).astype(bf16))
            if masked:
                p = jnp.where(_causal_mask(i, j, tb, tb), p, jnp.zeros_like(p))
            delta = jnp.sum(o_ref[...].astype(f32) * do_ref[...], axis=-1, keepdims=True)
            ds = p * (_bdot(do_ref[...], v_ref[...], NT) - delta).astype(bf16)
            dv_acc[...] += _bdot(p, do_ref[...], TN)
            dk_acc[...] += _bdot(ds, q_ref[...], TN)
            dq_ref[pl.ds(pl.multiple_of(i * tb, tb), tb), :] += _bdot(ds, k)

        pl.when(ii > j)(functools.partial(step, False))
        pl.when(ii == j)(functools.partial(step, True))

        @pl.when(ii == nb - 1)
        def _():
            dkv_ref[...] = jnp.concatenate([dk_acc[:, :LANES], dv_acc[...]], axis=1).astype(bf16)
            dkr_ref[...] = dk_acc[:, LANES:]

    qi = lambda h, j, i: jnp.maximum(i, j)
    kblk = lambda off: pl.BlockSpec((tb, LANES), lambda h, j, i: (j, 2 * h + off))
    vec = pl.BlockSpec((1, tb, 1), lambda h, j, i: (h, qi(h, j, i), 0))
    qblk = pl.BlockSpec((tb, LANES), lambda h, j, i: (qi(h, j, i), h))
    return pl.pallas_call(
        body, name="flash_bwd", grid=(H, nb, nb),
        in_specs=[pl.BlockSpec((tb, HQ), lambda h, j, i: (qi(h, j, i), h)), kblk(0), kblk(1),
                  pl.BlockSpec((tb, LANES), lambda h, j, i: (j, 0)), qblk, qblk, vec],
        out_specs=[pl.BlockSpec((tb, HQ), lambda h, j, i: (j, h)), pl.BlockSpec((tb, LANES), lambda h, j, i: (j, h)),
                   pl.BlockSpec((T, HQ), lambda h, j, i: (0, h))],
        out_shape=[jax.ShapeDtypeStruct((T, H * HQ), bf16), jax.ShapeDtypeStruct((T, H * LANES), f32),
                   jax.ShapeDtypeStruct((T, H * HQ), f32)],
        scratch_shapes=[pltpu.VMEM((tb, HQ), f32), pltpu.VMEM((tb, LANES), f32)],
        compiler_params=_cp("parallel", "arbitrary", "arbitrary"),
    )(q, kv, kv, kr, o, do, lse)


HBM_SPEC = pl.BlockSpec(memory_space=pltpu.HBM)
N_CHIPS = 4
N_DEV = 8

BIG = {"even_w_in": 1, "s5_w_glu": 0, "even_w_out": 0, "odd_w_in": 0, "mla_w_uq": 1, "mla_w_ukv": 1, "odd_w_out": 0,
       "ffn_w_in": 2, "ffn_w_out": 1}
LAYERED = ("ffn_w_in", "ffn_w_out")
GROUPS = {"even_in": ("even_w_in",), "even_rest": ("s5_w_glu", "even_w_out"), "ffn0": LAYERED,
          "odd": ("odd_w_in", "mla_w_uq", "mla_w_ukv", "odd_w_out"), "ffn1": LAYERED}
GROUP_LAYER = {"ffn0": 0, "ffn1": 1}


def _place():
    x, y, c = lax.axis_index("x"), lax.axis_index("y"), lax.axis_index("c")
    chips = [(1 - x, y), (x, 1 - y), (1 - x, 1 - y)]
    return x, y, c, chips


def _slab(ref, axis, k, size):
    start = pl.multiple_of(k * size, size if axis == 0 else LANES)
    idx = [slice(None)] * len(ref.shape)
    idx[axis] = pl.ds(start, size)
    return ref.at[tuple(idx)]


SEM_SPEC = pl.BlockSpec(memory_space=pltpu.SEMAPHORE)
ANY_SPEC = pl.BlockSpec(memory_space=pl.ANY)
EFFECT = pltpu.SideEffectType.DATAFLOW_SIDE_EFFECTING


def _hbm(a):
    return pltpu.with_memory_space_constraint(a, pltpu.HBM)


class _Gather:
    copies = 3

    def __init__(self, axis, size):
        self.axis, self.size = axis, size

    def view(self, land, kk):
        return _slab(land, self.axis, kk, self.size)

    def own(self, land, place):
        return self.view(land, 2 * place[0] + place[1])

    def sends(self, src, land, place):
        x, y, c, chips = place
        return [(self.own(land, place) if src is None else src, self.own(land, place), (*chip, c)) for chip in chips]

    def recvs(self, land, place):
        return [self.view(land, 2 * cx + cy) for cx, cy in place[3]]


class _Scatter:
    copies = 3

    def __init__(self, axis, size, layer=None):
        self.axis, self.size, self.layer = axis, size, layer

    def row(self, land, j):
        return land.at[j] if self.layer is None else land.at[j, self.layer]

    def sends(self, src, land, place):
        c, chips = place[2], place[3]
        return [(_slab(src, self.axis, 2 * cx + cy, self.size), self.row(land, j), (cx, cy, c))
                for j, (cx, cy) in enumerate(chips)]

    def recvs(self, land, place):
        return [self.row(land, j) for j in range(3)]


class _Sibling:
    copies = 1

    def sends(self, src, land, place):
        x, y, c, _ = place
        return [(src, land, (x, y, 1 - c))]

    def recvs(self, land, place):
        return [land]


class _ToAll:
    copies = N_DEV - 1

    def __init__(self, size):
        self.size = size

    def sends(self, src, land, place):
        x, y, c, _ = place
        flip = lambda v, bit: 1 - v if bit else v
        own = _slab(land, 0, 4 * x + 2 * y + c, self.size)
        return [(own, own, (flip(x, m & 4), flip(y, m & 2), flip(c, m & 1))) for m in range(1, N_DEV)]

    def recvs(self, land, place):
        x, y, c, _ = place
        d = 4 * x + 2 * y + c
        return [_slab(land, 0, d ^ m, self.size) for m in range(1, N_DEV)]


def _unique(arrays):
    out, index = [], {}
    for a in arrays:
        if a is not None and id(a) not in index:
            index[id(a)] = len(out)
            out.append(a)
    return out, index


def _sem_base(routes):
    base = [0]
    for r in routes:
        base.append(base[-1] + r.copies)
    return base


def _push_start(name, items):
    n = len(items)
    base = _sem_base([it[0] for it in items])
    arrays, index = _unique([it[1] for it in items] + [it[2] for it in items])
    na = len(arrays)

    def body(*refs):
        arr, send, recv, token = refs[:na], refs[na], refs[na + 1], refs[-1]
        place = _place()
        for i, (route, src, land) in enumerate(items):
            s_ref = None if src is None else arr[index[id(src)]]
            for j, (s, d, dev) in enumerate(route.sends(s_ref, arr[index[id(land)]], place)):
                pltpu.make_async_remote_copy(src_ref=s, dst_ref=d, send_sem=send.at[base[i] + j], recv_sem=recv.at[base[i] + j],
                                             device_id=dev, device_id_type=MESH).start()
        token[...] = jnp.zeros_like(token)

    res = pl.pallas_call(
        body, name=name,
        out_shape=[pltpu.SemaphoreType.DMA((base[-1],)), pltpu.SemaphoreType.DMA((base[-1],))]
        + [pltpu.HBM(a.shape, a.dtype) for a in arrays] + [jax.ShapeDtypeStruct((SUBLANES, LANES), f32)],
        in_specs=[HBM_SPEC] * na, out_specs=[SEM_SPEC, SEM_SPEC] + [HBM_SPEC] * na + [pl.BlockSpec(memory_space=pltpu.VMEM)],
        input_output_aliases={i: 2 + i for i in range(na)},
        compiler_params=pltpu.CompilerParams(has_side_effects=EFFECT),
    )(*[_hbm(a) for a in arrays])
    thru = lambda a: None if a is None else res[2 + index[id(a)]]
    return (res[0], res[1]), [thru(it[1]) for it in items], [thru(it[2]) for it in items], res[-1]


def _push_wait(name, groups, after, with_srcs=False):
    arrays, index = _unique([a for _, _, srcs, lands in groups for a in list(srcs) + list(lands)])
    na, ng = len(arrays), len(groups)

    def body(*refs):
        arr, sems = refs[:na], refs[na:na + 2 * ng]
        place = _place()
        for g, (routes, _, srcs, lands) in enumerate(groups):
            send, recv = sems[2 * g], sems[2 * g + 1]
            base = _sem_base(routes)
            for i, route in enumerate(routes):
                src, land = None if srcs[i] is None else arr[index[id(srcs[i])]], arr[index[id(lands[i])]]
                for j, ((s, d, dev), mine) in enumerate(zip(route.sends(src, land, place), route.recvs(land, place))):
                    cp = pltpu.make_async_remote_copy(src_ref=s, dst_ref=mine, send_sem=send.at[base[i] + j],
                                                      recv_sem=recv.at[base[i] + j], device_id=dev,
                                                      device_id_type=MESH)
                    cp.wait_send()
                    cp.wait_recv()

    sem_args = [s for g in groups for s in g[1]]
    res = pl.pallas_call(
        body, name=name, out_shape=[pltpu.HBM(a.shape, a.dtype) for a in arrays],
        in_specs=[HBM_SPEC] * na + [SEM_SPEC] * (2 * ng) + [ANY_SPEC] * len(after), out_specs=[HBM_SPEC] * na,
        input_output_aliases={i: i for i in range(na)},
        compiler_params=pltpu.CompilerParams(has_side_effects=EFFECT),
    )(*arrays, *sem_args, *after)
    if with_srcs:
        return [([res[index[id(a)]] for a in g[2]], [res[index[id(a)]] for a in g[3]]) for g in groups]
    return [[res[index[id(a)]] for a in g[3]] for g in groups]


def _place_slab(block, axis, slabs, idx, dtype, *, name):
    R, C = block.shape
    tm = _rows(R, C)
    nr = R // tm
    out_map = (lambda i, k: (i, k[0])) if axis == 1 else (lambda i, k: (k[0] * nr + i, 0))

    def body(k_ref, x_ref, o_ref):
        o_ref[...] = x_ref[...].astype(dtype)

    full = (R, C * slabs) if axis == 1 else (R * slabs, C)
    return pl.pallas_call(
        body, name=name, out_shape=jax.ShapeDtypeStruct(full, dtype),
        grid_spec=pltpu.PrefetchScalarGridSpec(
            num_scalar_prefetch=1, grid=(nr,), in_specs=[pl.BlockSpec((tm, C), lambda i, k: (i, 0))],
            out_specs=pl.BlockSpec((tm, C), out_map)),
        compiler_params=_cp("parallel"),
    )(idx, block)


ELEMENTWISE_BLOCK_BYTES = 1 << 20


def _rows(r, c):
    for t in (512, 256, 128, 64, 32, 16, 8):
        if r % t == 0 and t * c * 4 <= ELEMENTWISE_BLOCK_BYTES:
            return t
    return r


def _sum4(owns, axis, recv, kidx, *, name, dep=None):
    L = len(owns)
    R, C = recv.shape[2:]
    tm = _rows(R, C)
    nr = R // tm
    deps = [] if dep is None else [dep]

    def body(k_ref, *refs):
        own_refs, r_ref, out_ref = refs[:L], refs[L], refs[-1]
        for li in range(L):
            @pl.when(pl.program_id(0) == li)
            def _(o_ref=own_refs[li]):
                out_ref[...] = ((o_ref[...] + r_ref[0, 0].astype(f32)) + r_ref[1, 0].astype(f32)) + r_ref[2, 0].astype(f32)

    own_map = (lambda l, i, k: (i, k[0])) if axis == 1 else (lambda l, i, k: (k[0] * nr + i, 0))
    return pl.pallas_call(
        body, name=name, out_shape=jax.ShapeDtypeStruct((L * R, C), f32),
        grid_spec=pltpu.PrefetchScalarGridSpec(
            num_scalar_prefetch=1, grid=(L, nr),
            in_specs=[pl.BlockSpec((tm, C), own_map)] * L + [pl.BlockSpec((3, 1, tm, C), lambda l, i, k: (0, l, i, 0))]
            + [pl.BlockSpec(memory_space=pl.ANY)] * len(deps),
            out_specs=pl.BlockSpec((tm, C), lambda l, i, k: (l * nr + i, 0))),
        compiler_params=_cp("parallel", "parallel"),
    )(kidx, *owns, recv, *deps)


def _adamw(w, m, v, parts, *, name):
    R, C = w.shape
    tm = _rows(R, C)
    npart = len(parts)

    def body(*refs):
        w_ref, m_ref, v_ref = refs[:3]
        g_ref, d_ref, m2_ref, v2_ref = refs[3 + npart:]
        g = refs[3][...]
        for p_ref in refs[4:3 + npart]:
            g = g + p_ref[...]
        g_ref[...] = g
        d_ref[...], m2_ref[...], v2_ref[...] = _adam_math(w_ref[...], m_ref[...], v_ref[...], g)

    blk = pl.BlockSpec((tm, C), lambda i: (i, 0))
    return pl.pallas_call(
        body, name=name, grid=(R // tm,),
        in_specs=[blk] * (3 + npart), out_specs=[blk] * 4,
        out_shape=[jax.ShapeDtypeStruct((R, C), f32)] * 4, compiler_params=_cp("parallel"),
    )(w, m, v, *parts)


def _adam_math(w, m, v, g):
    m2 = ADAM_B1 * m + (1.0 - ADAM_B1) * g
    v2 = ADAM_B2 * v + (1.0 - ADAM_B2) * (g * g)
    m_hat = m2 / (1.0 - ADAM_B1 ** ADAM_STEP)
    v_hat = v2 / (1.0 - ADAM_B2 ** ADAM_STEP)
    return -ADAM_LR * (m_hat / (jnp.sqrt(v_hat) + ADAM_EPS) + ADAM_WD * w), m2, v2


def _adamw_small(landed, w, m, v, kidx, ra, rb):
    rs = ra + N_CHIPS * rb

    def body(k_ref, l_ref, w_ref, m_ref, v_ref, g_ref, d_ref, m2_ref, v2_ref):
        mine = pl.multiple_of(ra + k_ref[0] * rb, SUBLANES)
        for lo, n, off in ((0, ra, 0), (ra, rb, mine)):
            g = l_ref[pl.ds(off, n), :]
            for d in range(1, N_DEV):
                g = g + l_ref[pl.ds(d * rs + off, n), :]
            rows = pl.ds(lo, n)
            delta, m2, v2 = _adam_math(w_ref[rows, :], m_ref[rows, :], v_ref[rows, :], g)
            g_ref[rows, :] = g
            d_ref[rows, :] = delta
            m2_ref[rows, :] = m2
            v2_ref[rows, :] = v2

    vmem = pl.BlockSpec(memory_space=pltpu.VMEM)
    return pl.pallas_call(
        body, name="adamw_small", out_shape=[jax.ShapeDtypeStruct(w.shape, f32)] * 4,
        grid_spec=pltpu.PrefetchScalarGridSpec(num_scalar_prefetch=1, grid=(), in_specs=[vmem] * 4, out_specs=[vmem] * 4),
        compiler_params=_cp(),
    )(kidx, landed, w, m, v)


def _pad_odd(w):
    return jnp.pad(w, ((0, 0), (0, ODD_PAD - w.shape[1])))


def _uq_cat(w):
    r = w.shape[0]
    return jnp.pad(w.reshape(r, MLA_HEADS, MLA_QK), ((0, 0), (0, 0), (0, HQ - MLA_QK))).reshape(r, MLA_HEADS * HQ)


def _uq_uncat(w):
    r = w.shape[0]
    return w.reshape(r, MLA_HEADS, HQ)[:, :, :MLA_QK].reshape(r, MLA_HEADS * MLA_QK)


def _to_segments(v):
    T, C = v.shape
    return v.reshape(S5_SEG, T // S5_SEG, C).transpose(1, 0, 2).reshape(T, C)


def _from_segments(v):
    T, C = v.shape
    return v.reshape(T // S5_SEG, S5_SEG, C).transpose(1, 0, 2).reshape(T, C)


def _s5_rb(T):
    return min(512, T)


def _ffn_fwd(h, hn, w_in, cw, cb, w_out, tag, next_g=None):
    au = _mm(hn, w_in, out_dtype=bf16, name=f"ffn{tag}_in", tn=1408)
    z = _ffn_mid_fwd(au, cw, cb, name=f"ffn{tag}_mid")
    return _mm(z, w_out, res=h, norm_g=next_g, name=f"ffn{tag}_out", tm=512, tk=D_FF), (hn, au, z)


def _ffn_bwd(h, g, w_in, cw, cb, w_out, saved, dh, tag, dep=None):
    hn, au, z = saved
    dz = _mm(dh, w_out, tb=True, out_dtype=bf16, name=f"ffn{tag}_dz", tn=1408, dep=dep)
    dw_out = _mm(z, dh, ta=True, also_bf16=True, name=f"ffn{tag}_dwout", tm=1408)
    dau, dcw, dcb = _ffn_mid_bwd(au, cw, cb, dz, name=f"ffn{tag}_dmid")
    dh_in, dg = _mm(dau, w_in, tb=True, res=dh, norm_bwd=(h, g), name=f"ffn{tag}_dhn", tk=1408)
    dw_in = _mm(hn, dau, ta=True, also_bf16=True, name=f"ffn{tag}_dwin", tn=1408)
    return dh_in, dg, dw_in, dcw, dcb, dw_out


def _local_step(x, positions, target, get_w, P, put_g):
    T = x.shape[0]
    rb = _s5_rb(T)
    row = lambda v: v.reshape(1, -1)
    g_mix, g_ffn = P["norm_mix_g"], P["norm_ffn_g"]
    lbl, hng = P["hgrn_lb_logits"], P["hgrn_norm_g"]
    dsk, bg = P["s5_d"], P["s5_b_glu"]
    qg, kvg = P["mla_q_norm_g"], P["mla_kv_norm_g"]
    cw, cb = P["ffn_conv_w"], P["ffn_conv_b"]

    col = lambda v: v.reshape(S5_N, 1)
    disc_in = (col(P["s5_a_re"]), col(P["s5_a_im"]), col(jnp.repeat(P["s5_log_dt"].reshape(S5_GROUPS), S5_STATE)),
               P["s5_b_re"].reshape(S5_N, S5_GROUP), P["s5_b_im"].reshape(S5_N, S5_GROUP))
    abr, abi, bbr, bbi = _s5_disc_fwd(*disc_in)
    ar, ai = abr.reshape(1, S5_N), abi.reshape(1, S5_N)
    bbr3, bbi3 = bbr.reshape(S5_GROUPS, S5_STATE, S5_GROUP), bbi.reshape(S5_GROUPS, S5_STATE, S5_GROUP)
    bre, bim = _blockdiag(bbr3, True).astype(bf16), _blockdiag(bbi3, True).astype(bf16)
    bret, bimt = _blockdiag(bbr3).astype(bf16), _blockdiag(bbi3).astype(bf16)
    c_re, c_im = P["s5_c_re"].reshape(S5_GROUPS, S5_GROUP, S5_STATE), P["s5_c_im"].reshape(S5_GROUPS, S5_GROUP, S5_STATE)
    cre, cim = _blockdiag(c_re, True).astype(bf16), _blockdiag(c_im, True).astype(bf16)
    cret, cimt = _blockdiag(c_re).astype(bf16), _blockdiag(c_im).astype(bf16)

    hn0 = _rms_fwd(x, g_mix[0:1], name="mix0_norm")
    We = get_w("even_in", hn0)
    proj_e = _mm(hn0, We["even_w_in"], name="even_in", tn=1280)
    Wr = get_w("even_rest", proj_e)
    ya, states = _hgrn_fwd(proj_e, lbl, hng)
    u_seg = _to_segments(proj_e[:, 4 * 512:])
    fr, fi = _s5_final(u_seg, bre, bim, ar, ai, rb=rb)
    yb_seg, s0r, s0i = _s5_fwd(u_seg, bre, bim, ar, ai, fr, fi, cre, cim, dsk, Wr["s5_w_glu"], bg, rb=rb)
    ycat = jnp.concatenate([ya, _from_segments(yb_seg)], axis=1)
    h1, hnf0 = _mm(ycat, Wr["even_w_out"], res=x, norm_g=g_ffn[0:1], name="even_out")
    Wf0 = get_w("ffn0", h1)
    (h2, hn2), ffn0 = _ffn_fwd(h1, hnf0, Wf0["ffn_w_in"], cw[0], cb[0:1], Wf0["ffn_w_out"], 0, next_g=g_mix[1:2])

    tabs = _rope_tables(positions)
    Wo = get_w("odd", hn2)
    proj_o = _mm(hn2, Wo["odd_w_in"], name="odd_in")
    cqn, ckvn, kr = _mla_prep_fwd(proj_o, qg, kvg, tabs)
    q = _mm(cqn, Wo["mla_w_uq"], out_dtype=bf16, rope=tabs, name="mla_uq")
    kvb = _mm(ckvn, Wo["mla_w_ukv"], out_dtype=bf16, name="mla_ukv")
    o, lse = _flash_fwd(q, kvb, kr)
    h3, hnf1 = _mm(o, Wo["odd_w_out"], res=h2, norm_g=g_ffn[1:2], name="odd_out")
    Wf1 = get_w("ffn1", h3)
    h4, ffn1 = _ffn_fwd(h3, hnf1, Wf1["ffn_w_in"], cw[1], cb[1:2], Wf1["ffn_w_out"], 1)
    loss, dh4, dg_final = _loss_head(h4, row(P["final_norm_g"]), target)

    dh3, dg_ffn1, dw_fin1, dcw1, dcb1, dw_fout1 = _ffn_bwd(
        h3, g_ffn[1:2], Wf1["ffn_w_in"], cw[1], cb[1:2], Wf1["ffn_w_out"], ffn1, dh4, 1)
    sent = put_g("ffn1", {"ffn_w_in": dw_fin1, "ffn_w_out": dw_fout1})
    do = _mm(dh3, Wo["odd_w_out"], tb=True, out_dtype=bf16, name="odd_do", dep=sent)
    dw_oout = _mm(o, dh3, ta=True, also_bf16=True, name="odd_dwout")
    dkv, dkr_h, dq = _flash_bwd(q, kvb, kr, o, do, lse)
    dq = _dq_post(dq, tabs)
    dw_uq = _mm(cqn, dq, ta=True, also_bf16=True, name="mla_dwuq")
    dcqn = _mm(dq, Wo["mla_w_uq"], tb=True, name="mla_dcq", tk=MLA_HEADS * HQ)
    dw_ukv = _mm(ckvn, dkv, ta=True, also_bf16=True, name="mla_dwukv")
    dckvn = _mm(dkv, Wo["mla_w_ukv"], tb=True, name="mla_dckv")
    dproj_o, dqg, dkvg = _mla_prep_bwd(proj_o, qg, kvg, tabs, dcqn, dckvn, dkr_h)
    dw_oin = _mm(hn2, dproj_o, ta=True, also_bf16=True, name="odd_dwin")
    sent = put_g("odd", {"odd_w_in": dw_oin, "mla_w_uq": dw_uq, "mla_w_ukv": dw_ukv, "odd_w_out": dw_oout})
    dh2, dg_mix1 = _mm(dproj_o, Wo["odd_w_in"], tb=True, res=dh3, norm_bwd=(h2, g_mix[1:2]), name="odd_dhn")

    dh1, dg_ffn0, dw_fin0, dcw0, dcb0, dw_fout0 = _ffn_bwd(
        h1, g_ffn[0:1], Wf0["ffn_w_in"], cw[0], cb[0:1], Wf0["ffn_w_out"], ffn0, dh2, 0, dep=sent)
    sent = put_g("ffn0", {"ffn_w_in": dw_fin0, "ffn_w_out": dw_fout0})
    dycat = _mm(dh1, Wr["even_w_out"], tb=True, out_dtype=bf16, name="even_dy", dep=sent)
    dw_eout = _mm(ycat, dh1, ta=True, also_bf16=True, name="even_dwout")
    dq_h, df_h, di_h, dg_h, dlbl, dhng = _hgrn_bwd(proj_e, lbl, hng, states, dycat)
    dyb_seg = _to_segments(dycat[:, 512:])
    dy_s5, glr, gli, dcre, dcim, dd, dwg, dbg = _s5_bwd_a(
        u_seg, bre, bim, ar, ai, s0r, s0i, cre, cim, cret, cimt, dsk, Wr["s5_w_glu"], bg, dyb_seg, rb=rb)
    du_seg, dbre, dbim, dar, dai = _s5_bwd_b(
        u_seg, bre, bim, bret, bimt, ar, ai, s0r, s0i, glr, gli, cret, cimt, dsk, dy_s5, rb=rb)
    dproj_e = jnp.concatenate([dq_h, df_h, di_h, dg_h, _from_segments(du_seg)], axis=1)
    dx, dg_mix0 = _mm(dproj_e, We["even_w_in"], tb=True, res=dh1, norm_bwd=(x, g_mix[0:1]), name="even_dhn", tk=1280)
    dw_ein = _mm(hn0, dproj_e, ta=True, also_bf16=True, name="even_dwin", tn=1280)

    unblk = lambda m, a, b: jnp.swapaxes(_blockdiag_t(m, a, b), 1, 2)
    dbbr = unblk(dbre, S5_GROUP, S5_STATE).reshape(S5_N, S5_GROUP)
    dbbi = unblk(dbim, S5_GROUP, S5_STATE).reshape(S5_N, S5_GROUP)
    d_ar, d_ai, d_ldt, d_br, d_bi = _s5_disc_bwd(*disc_in, (dar.reshape(S5_N, 1), dai.reshape(S5_N, 1), dbbr, dbbi))
    small = {
        "norm_mix_g": jnp.concatenate([dg_mix0, dg_mix1], axis=0),
        "norm_ffn_g": jnp.concatenate([dg_ffn0, dg_ffn1], axis=0),
        "final_norm_g": dg_final.reshape(-1),
        "hgrn_lb_logits": dlbl, "hgrn_norm_g": dhng,
        "s5_a_re": d_ar.reshape(1, S5_GROUPS, S5_STATE), "s5_a_im": d_ai.reshape(1, S5_GROUPS, S5_STATE),
        "s5_log_dt": d_ldt.reshape(S5_GROUPS, S5_STATE).sum(axis=1).reshape(1, S5_GROUPS),
        "s5_b_re": d_br.reshape(1, S5_GROUPS, S5_STATE, S5_GROUP), "s5_b_im": d_bi.reshape(1, S5_GROUPS, S5_STATE, S5_GROUP),
        "s5_c_re": unblk(dcre, S5_STATE, S5_GROUP).reshape(1, S5_GROUPS, S5_GROUP, S5_STATE),
        "s5_c_im": unblk(dcim, S5_STATE, S5_GROUP).reshape(1, S5_GROUPS, S5_GROUP, S5_STATE),
        "s5_d": dd, "s5_b_glu": dbg, "mla_q_norm_g": dqg, "mla_kv_norm_g": dkvg,
        "ffn_conv_w": jnp.stack([dcw0, dcw1]), "ffn_conv_b": jnp.concatenate([dcb0, dcb1], axis=0),
    }
    put_g("even", {"even_w_in": dw_ein, "s5_w_glu": (dwg, dwg.astype(bf16)), "even_w_out": dw_eout}, small)
    return loss, dx


WEIGHTS = ["norm_mix_g", "norm_ffn_g", "final_norm_g", "even_w_in", "hgrn_lb_logits", "hgrn_norm_g", "s5_a_re", "s5_a_im",
           "s5_log_dt", "s5_b_re", "s5_b_im", "s5_c_re", "s5_c_im", "s5_d", "s5_w_glu", "s5_b_glu", "even_w_out", "odd_w_in",
           "mla_q_norm_g", "mla_w_uq", "mla_kv_norm_g", "mla_w_ukv", "odd_w_out", "ffn_w_in", "ffn_conv_w", "ffn_conv_b",
           "ffn_w_out"]
SMALL_SHARDED = {"mla_q_norm_g": 1, "mla_kv_norm_g": 1, "ffn_conv_w": 2}
SMALL = [n for n in WEIGHTS if n not in BIG]
SMALL_REP = [n for n in SMALL if n not in SMALL_SHARDED]


def _pack_rows(shapes):
    n = sum(math.prod(s) for s in shapes)
    return -(-n // (SUBLANES * LANES)) * SUBLANES


def _pack(arrays, rows):
    flat = jnp.concatenate([a.reshape(-1) for a in arrays])
    return jnp.pad(flat, (0, rows * LANES - flat.shape[0])).reshape(rows, LANES)


def _unpack(block, shapes):
    flat, out, off = block.reshape(-1), [], 0
    for s in shapes:
        n = math.prod(s)
        out.append(flat[off:off + n].reshape(s))
        off += n
    return out


def kernel(x, positions, norm_mix_g, norm_ffn_g, final_norm_g, even_w_in, hgrn_lb_logits, hgrn_norm_g, s5_a_re, s5_a_im, s5_log_dt, s5_b_re, s5_b_im, s5_c_re, s5_c_im, s5_d, s5_w_glu, s5_b_glu, even_w_out, odd_w_in, mla_q_norm_g, mla_w_uq, mla_kv_norm_g, mla_w_ukv, odd_w_out, ffn_w_in, ffn_conv_w, ffn_conv_b, ffn_w_out, loss_target, m_norm_mix_g, m_norm_ffn_g, m_final_norm_g, m_even_w_in, m_hgrn_lb_logits, m_hgrn_norm_g, m_s5_a_re, m_s5_a_im, m_s5_log_dt, m_s5_b_re, m_s5_b_im, m_s5_c_re, m_s5_c_im, m_s5_d, m_s5_w_glu, m_s5_b_glu, m_even_w_out, m_odd_w_in, m_mla_q_norm_g, m_mla_w_uq, m_mla_kv_norm_g, m_mla_w_ukv, m_odd_w_out, m_ffn_w_in, m_ffn_conv_w, m_ffn_conv_b, m_ffn_w_out, v_norm_mix_g, v_norm_ffn_g, v_final_norm_g, v_even_w_in, v_hgrn_lb_logits, v_hgrn_norm_g, v_s5_a_re, v_s5_a_im, v_s5_log_dt, v_s5_b_re, v_s5_b_im, v_s5_c_re, v_s5_c_im, v_s5_d, v_s5_w_glu, v_s5_b_glu, v_even_w_out, v_odd_w_in, v_mla_q_norm_g, v_mla_w_uq, v_mla_kv_norm_g, v_mla_w_ukv, v_odd_w_out, v_ffn_w_in, v_ffn_conv_w, v_ffn_conv_b, v_ffn_w_out):
    args = dict(locals())
    w = {n: args[n] for n in WEIGHTS}
    m = {n: args["m_" + n] for n in WEIGHTS}
    v = {n: args["v_" + n] for n in WEIGHTS}
    k = 2 * lax.axis_index("x") + lax.axis_index("y")
    kidx = k.reshape(1).astype(jnp.int32)
    axis2d = lambda n: BIG[n] - (1 if n in LAYERED else 0)
    slab = lambda n: w[n].shape[1 + axis2d(n)]

    small_sh_shapes = [w[n].shape for n in SMALL_SHARDED]
    rb = _pack_rows(small_sh_shapes)
    items = {}
    for group, names in GROUPS.items():
        layer = GROUP_LAYER.get(group, 0)
        items[group] = [(_Gather(axis2d(n), slab(n)), None,
                         _place_slab(w[n][layer], axis2d(n), N_CHIPS, kidx, bf16, name=f"place_{n}_{layer}")) for n in names]
    items["even_in"].append((_Gather(0, rb), None,
                             _place_slab(_pack([w[n] for n in SMALL_SHARDED], rb), 0, N_CHIPS, kidx, f32, name="place_small")))
    gathers, tokens = {}, []
    for group in GROUPS:
        sems, srcs, lands, token = _push_start(f"gather_start_{group}", items[group])
        gathers[group] = ([it[0] for it in items[group]], sems, srcs, lands)
        tokens.append(token[0, 0])
    started = functools.reduce(jnp.add, tokens)

    def landed(group, after):
        return _push_wait(f"gather_wait_{group}", [gathers[group]], [after])[0]

    even = landed("even_in", (started + norm_mix_g[0, 0]).reshape(1))
    per_chip = [_unpack(even[-1][c * rb:(c + 1) * rb], small_sh_shapes) for c in range(N_CHIPS)]
    P = {n: w[n] for n in SMALL_REP}
    for i, (n, ax) in enumerate(SMALL_SHARDED.items()):
        P[n] = jnp.concatenate([per_chip[c][i] for c in range(N_CHIPS)], axis=ax)
    P["mla_q_norm_g"], P["mla_kv_norm_g"] = P["mla_q_norm_g"].reshape(1, -1), P["mla_kv_norm_g"].reshape(1, -1)
    fix_w = {"odd_w_in": _pad_odd, "mla_w_uq": _uq_cat}

    def get_w(group, after):
        full = even if group == "even_in" else landed(group, after)
        return {n: fix_w.get(n, lambda a: a)(a) for n, a in zip(GROUPS[group], full)}

    fix_g = {"odd_w_in": lambda g: g[:, :odd_w_in.shape[2]], "mla_w_uq": _uq_uncat}
    g32, scatters, land_now = {}, {}, {}
    ra = _pack_rows([w[n].shape for n in SMALL_REP])
    rs = ra + N_CHIPS * rb
    didx = (2 * kidx + lax.axis_index("c")).astype(jnp.int32)

    def put_g(group, grads, small=None):
        layer = GROUP_LAYER.get(group)
        routes, srcs, names = [], [], list(grads)
        for n in names:
            f = fix_g.get(n, lambda g: g)
            g32.setdefault(n, {})[layer or 0] = f(grads[n][0])
            routes.append(_Scatter(axis2d(n), slab(n), layer if n in LAYERED else None))
            srcs.append(f(grads[n][1]))
            if n not in land_now:
                land_now[n] = lax.empty((3,) + w[n].shape[0 if n in LAYERED else 1:], bf16)
        if small is not None:
            blocks = [_pack([small[n] for n in SMALL_REP], ra)]
            for chip in range(N_CHIPS):
                sl = lambda n, ax: lax.slice_in_dim(small[n].reshape(w[n].shape[:ax] + (-1,) + w[n].shape[ax + 1:]),
                                                    chip * w[n].shape[ax], (chip + 1) * w[n].shape[ax], axis=ax)
                blocks.append(_pack([sl(n, ax) for n, ax in SMALL_SHARDED.items()], rb))
            names.append("small")
            routes.append(_ToAll(rs))
            srcs.append(None)
            land_now["small"] = _place_slab(jnp.concatenate(blocks), 0, N_DEV, didx, f32, name="place_small_grads")
        sems, srcs, lands, token = _push_start(f"scatter_start_{group}", [(r, s, land_now[n]) for r, s, n in zip(routes, srcs, names)])
        land_now.update(zip(names, lands))
        scatters[group] = (routes, sems, srcs, names)
        sent.append(token)
        return token

    sent = []
    loss, dx = _local_step(x[0], positions[0], loss_target[0], get_w, P, put_g)
    sent_last = sent[-1]
    loss = lax.psum(loss[0, 0], ("x", "y", "c"))

    out = {}

    def arrive(tag, groups, after):
        waits = [(scatters[g][0], scatters[g][1], scatters[g][2], [land_now[n] for n in scatters[g][3]]) for g in groups]
        for g, lands in zip(groups, _push_wait(f"scatter_wait_{tag}", waits, after)):
            land_now.update(zip(scatters[g][3], lands))

    def cross(tag, names, dep=None):
        part = {}
        for n in names:
            recv = land_now[n] if n in LAYERED else land_now[n][:, None]
            part[n] = _sum4([g32[n][l] for l in sorted(g32[n])], axis2d(n), recv, kidx, name=f"sum4_{n}", dep=dep)
        items = [(_Sibling(), part[n], lax.empty(part[n].shape, f32)) for n in names]
        sems, srcs, lands, token = _push_start(f"swap_start_{tag}", items)
        return (names, part, ([it[0] for it in items], sems, srcs, lands)), token

    def update(tag, arrived, after):
        names, _, push = arrived
        mine, theirs = _push_wait(f"swap_wait_{tag}", [push], after, with_srcs=True)[0]
        part, other = dict(zip(names, mine)), dict(zip(names, theirs))
        done = []
        for n in names:
            C = part[n].shape[-1]
            res = _adamw(w[n].reshape(-1, C), m[n].reshape(-1, C), v[n].reshape(-1, C), [part[n], other[n]], name=f"adamw_{n}")
            out[n] = [r.reshape(w[n].shape) for r in res]
            done.append(res[0])
        return done

    arrive("a", ["ffn1", "odd", "ffn0"], [dx, sent_last])
    a1, token_a1 = cross("a1", ["ffn_w_in"])
    a2, token_a2 = cross("a2", ["ffn_w_out"] + list(GROUPS["odd"]), dep=token_a1)
    done = update("a2", a2, update("a1", a1, [token_a2]))
    arrive("b", ["even"], done)
    b, token_b = cross("b", list(GROUPS["even_in"]) + list(GROUPS["even_rest"]))

    order = SMALL_REP + list(SMALL_SHARDED)
    packed = lambda src: jnp.concatenate([_pack([src[n] for n in SMALL_REP], ra), _pack([src[n] for n in SMALL_SHARDED], rb)])
    res = _adamw_small(land_now["small"], packed(w), packed(m), packed(v), kidx, ra, rb)
    update("b", b, [res[0], token_b])
    for r in res:
        parts = _unpack(r[:ra], [w[n].shape for n in SMALL_REP]) + _unpack(r[ra:], small_sh_shapes)
        for n, a in zip(order, parts):
            out.setdefault(n, []).append(a)

    return (loss, dx[None], *[out[n][0] for n in WEIGHTS], *[out[n][1] for n in WEIGHTS],
            *[out[n][2] for n in WEIGHTS], *[out[n][3] for n in WEIGHTS])
```

```python
import functools
import math

import jax
import jax.numpy as jnp
from jax import lax
from jax.experimental import pallas as pl
from jax.experimental.pallas import tpu as pltpu

f32, bf16 = jnp.float32, jnp.bfloat16
EPS = 1e-6
LANES = 128
SUBLANES = 8
VMEM_BYTES = 48 * 1024 * 1024
HGRN_CHUNK = 64
HGRN_HEADS = 4
S5_GROUPS, S5_STATE, S5_GROUP = 32, 64, 16
S5_N = S5_GROUPS * S5_STATE
S5_SEG = SUBLANES
MLA_HEADS, MLA_NOPE, MLA_ROPE, MLA_V = 8, 128, 64, 128
MLA_QK = MLA_NOPE + MLA_ROPE
MLA_Q_RANK, MLA_KV_RANK = 384, 256
ROPE_THETA = 10000.0
D_FF = 2816
ADAM_LR, ADAM_B1, ADAM_B2, ADAM_EPS, ADAM_WD, ADAM_STEP = 0.001, 0.9, 0.999, 1e-08, 0.01, 10
MESH = pl.DeviceIdType.MESH
HI = lax.Precision.HIGHEST


def _cp(*dims):
    return pltpu.CompilerParams(dimension_semantics=dims if dims else None, vmem_limit_bytes=VMEM_BYTES)


def _tile(n, t):
    if n <= t:
        return n
    c = (t // LANES) * LANES
    while c >= LANES:
        if n % c == 0:
            return c
        c -= LANES
    return n


def _dot(a, b, dn=None, precision=None):
    if dn is None:
        dn = (((a.ndim - 1,), (0,)), ((), ()))
    return lax.dot_general(a, b, dn, preferred_element_type=f32, precision=precision)


NT = (((1,), (1,)), ((), ()))
TN = (((0,), (0,)), ((), ()))


def _bdot(a, b, dn=None):
    return _dot(a.astype(bf16), b.astype(bf16), dn)


MM_PARTS = 2


def _mm(a, b, *, name, ta=False, tb=False, out_dtype=f32, res=None, also_bf16=False, tm=1024, tn=1024, tk=1024, dep=None,
        norm_g=None, norm_bwd=None, rope=None):
    halves = lambda s: (s[1], 2 * s[2]) if len(s) == 3 else s
    M, K = (a.shape[1], a.shape[0]) if ta else halves(a.shape)
    N = b.shape[0] if tb else halves(b.shape)[1]
    rows = norm_g is not None or norm_bwd is not None
    if rows:
        tm, tn, tk = 512, N, K
    tm, tn, tk = _tile(M, tm), _tile(N, tn), _tile(K, tk)
    both = rows and a.ndim == 3 and tb
    if a.ndim == 3 and not both:
        tk = _tile(K // 2, tk)
    if b.ndim == 3:
        tn = _tile(N // 2, tn)
    nk = K // tk
    parts = MM_PARTS if tm % (MM_PARTS * LANES) == 0 else 1
    dn = (((0 if ta else 1,), (1 if tb else 0,)), ((), ()))
    extra = [] if norm_bwd is None else list(norm_bwd)
    if norm_g is not None:
        extra.append(norm_g)
    if rope is not None:
        extra += list(rope)

    def body(*refs):
        a_ref, b_ref = refs[0], refs[1]
        r_ref = refs[2] if res is not None else None
        nin = 2 + (res is not None) + (dep is not None) + len(extra)
        ex = refs[nin - len(extra):nin]
        outs = refs[nin:-1] if nk > 1 else refs[nin:]
        acc = refs[-1] if nk > 1 else None
        k = pl.program_id(2)
        b_blk = b_ref[...]
        if nk > 1:
            @pl.when(k == 0)
            def _():
                acc[...] = jnp.zeros_like(acc)

        groups = []
        for part in range(parts):
            rows = pl.ds(part * (tm // parts), tm // parts)
            if both:
                p = _bdot(a_ref[0, rows, :], b_blk[:, :K // 2], dn) + _bdot(a_ref[1, rows, :], b_blk[:, K // 2:], dn)
            else:
                p = _bdot(a_ref[:, rows] if ta else a_ref[rows, :], b_blk, dn)
            if nk > 1:
                acc[rows, :] += p
            groups.append((rows, p))

        def epilogue():
            for part, (rows, p) in enumerate(groups):
                r = acc[rows, :] if nk > 1 else p
                if norm_bwd is not None:
                    r, dg = _rms_bwd_math(ex[0][rows, :], ex[1][...], r)
                    if part == 0:
                        @pl.when(pl.program_id(0) == 0)
                        def _(dg=dg):
                            outs[1][...] = dg

                    @pl.when((pl.program_id(0) > 0) | (part > 0))
                    def _(dg=dg):
                        outs[1][...] += dg
                if r_ref is not None:
                    r = r + r_ref[rows, :]
                if rope is not None:
                    c, sa, sb = (t[rows, :] for t in ex[-3:])
                    for h in range(tn // HQ):
                        lo = h * HQ
                        outs[0][rows, lo:lo + LANES] = (r[:, lo:lo + LANES] * QK_SCALE).astype(out_dtype)
                        outs[0][rows, lo + LANES:lo + HQ] = (_rot(r[:, lo + LANES:lo + HQ], c, sa, sb) * QK_SCALE).astype(out_dtype)
                    continue
                outs[0][rows, :] = r.astype(out_dtype)
                if also_bf16:
                    outs[1][rows, :] = r.astype(bf16)
                if norm_g is not None:
                    outs[1][rows, :] = _rms(r, ex[-1][...]).astype(bf16)

        if nk > 1:
            pl.when(k == nk - 1)(epilogue)
        else:
            epilogue()

    a_spec = pl.BlockSpec((tk, tm), lambda i, j, k: (k, i)) if ta else pl.BlockSpec((tm, tk), lambda i, j, k: (i, k))
    b_spec = pl.BlockSpec((tn, tk), lambda i, j, k: (j, k)) if tb else pl.BlockSpec((tk, tn), lambda i, j, k: (k, j))
    if rows:
        b_spec = pl.BlockSpec((tn, tk) if tb else (tk, tn), lambda i, j, k: (0, 0), pipeline_mode=pl.Buffered(1))
    if both:
        a_spec = pl.BlockSpec((2, tm, K // 2), lambda i, j, k: (0, i, 0))
    elif a.ndim == 3:
        kh = K // 2 // tk
        a_spec = pl.BlockSpec((None, tm, tk), lambda i, j, k: (k // kh, i, k % kh))
    if b.ndim == 3:
        nh = N // 2 // tn
        b_spec = pl.BlockSpec((None, tk, tn), lambda i, j, k: (j // nh, k, j % nh))
    o_spec = pl.BlockSpec((tm, tn), lambda i, j, k: (i, j))
    in_specs, args = [a_spec, b_spec], [a, b]
    if res is not None:
        in_specs.append(o_spec)
        args.append(res)
    if dep is not None:
        in_specs.append(pl.BlockSpec(memory_space=pl.ANY))
        args.append(dep)
    vec = pl.BlockSpec((1, tn), lambda i, j, k: (0, j))
    if norm_bwd is not None:
        in_specs += [o_spec, vec]
    if norm_g is not None:
        in_specs.append(vec)
    if rope is not None:
        in_specs += [pl.BlockSpec((tm, LANES), lambda i, j, k: (i, 0))] * 3
    args += extra
    out_shape = [jax.ShapeDtypeStruct((M, N), out_dtype)]
    out_specs = [o_spec]
    if also_bf16 or norm_g is not None:
        out_shape.append(jax.ShapeDtypeStruct((M, N), bf16))
        out_specs.append(o_spec)
    if norm_bwd is not None:
        out_shape.append(jax.ShapeDtypeStruct((1, N), f32))
        out_specs.append(vec)
    dims = ("arbitrary" if norm_bwd is not None else "parallel", "parallel", "arbitrary")
    out = pl.pallas_call(
        body, name=name, grid=(M // tm, N // tn, nk), in_specs=in_specs, out_specs=out_specs, out_shape=out_shape,
        scratch_shapes=[pltpu.VMEM((tm, tn), f32)] if nk > 1 else [], compiler_params=_cp(*dims),
    )(*args)
    return out if len(out) > 1 else out[0]


def _rms_fwd(x, g, *, name, tm=512):
    T, width = x.shape
    tm = _tile(T, tm)

    def body(x_ref, g_ref, o_ref):
        xv = x_ref[...]
        r = lax.rsqrt(jnp.mean(xv * xv, axis=-1, keepdims=True) + EPS)
        o_ref[...] = (xv * r * g_ref[...]).astype(bf16)

    return pl.pallas_call(
        body, name=name, grid=(T // tm,),
        in_specs=[pl.BlockSpec((tm, width), lambda i: (i, 0)), pl.BlockSpec((1, width), lambda i: (0, 0))],
        out_specs=pl.BlockSpec((tm, width), lambda i: (i, 0)), out_shape=jax.ShapeDtypeStruct((T, width), bf16),
        compiler_params=_cp("parallel"),
    )(x, g)


def _rms_bwd_math(xv, g, dy):
    r = lax.rsqrt(jnp.mean(xv * xv, axis=-1, keepdims=True) + EPS)
    xh = xv * r
    dxh = dy * g
    dx = r * (dxh - xh * jnp.mean(dxh * xh, axis=-1, keepdims=True))
    dg = jnp.sum(dy * xh, axis=0, keepdims=True)
    return dx, dg


def _loss_head(h, g, target, *, tm=512):
    T, D = h.shape
    tm = _tile(T, tm)

    def body(h_ref, g_ref, t_ref, loss_ref, dh_ref, dg_ref):
        hv, gv = h_ref[...], g_ref[...]
        r = lax.rsqrt(jnp.mean(hv * hv, axis=-1, keepdims=True) + EPS)
        e = hv * r * gv - t_ref[...]
        part = 0.5 * jnp.sum(jnp.mean(e * e, axis=-1, keepdims=True), axis=0, keepdims=True)
        dx, dg = _rms_bwd_math(hv, gv, e * (1.0 / D))
        dh_ref[...] = dx

        @pl.when(pl.program_id(0) == 0)
        def _():
            loss_ref[...] = part
            dg_ref[...] = dg

        @pl.when(pl.program_id(0) > 0)
        def _():
            loss_ref[...] += part
            dg_ref[...] += dg

    row = pl.BlockSpec((tm, D), lambda i: (i, 0))
    vec = pl.BlockSpec((1, D), lambda i: (0, 0))
    return pl.pallas_call(
        body, name="loss_head", grid=(T // tm,), in_specs=[row, vec, row],
        out_specs=[pl.BlockSpec((1, 1), lambda i: (0, 0)), row, vec],
        out_shape=[jax.ShapeDtypeStruct((1, 1), f32), jax.ShapeDtypeStruct((T, D), f32), jax.ShapeDtypeStruct((1, D), f32)],
        compiler_params=_cp("arbitrary"),
    )(h, g, target)


FFN_W = 2 * LANES
FFN_ROWS = 128
HALO = 2 * SUBLANES


def _conv_taps(a_ref, c, rc):
    if isinstance(c, int) and c == 0:
        ext = jnp.concatenate([jnp.zeros((HALO, FFN_W), f32), a_ref[pl.ds(0, rc), :].astype(f32)], axis=0)
    else:
        ext = a_ref[pl.ds(pl.multiple_of(c * rc - HALO, HALO), rc + HALO), :].astype(f32)
    return ext[HALO:], pltpu.roll(ext, 1, 0)[HALO:], pltpu.roll(ext, 2, 0)[HALO:]


def _chunk_rows(c, rc):
    return pl.ds(c * rc, rc) if isinstance(c, int) else pl.ds(pl.multiple_of(c * rc, rc), rc)


def _ffn_mid_fwd(au, cw, cb, *, name):
    T = au.shape[0]
    F = au.shape[1] // 2
    nb = F // FFN_W
    rc = min(FFN_ROWS, T)
    nc = T // rc

    def body(a_ref, u_ref, w_ref, b_ref, z_ref):
        w, b = w_ref[...], b_ref[...]

        def chunk(c):
            a, a1, a2 = _conv_taps(a_ref, c, rc)
            rows = _chunk_rows(c, rc)
            ac = (w[0:1] * a2 + w[1:2] * a1 + w[2:3] * a + b).astype(bf16)
            z_ref[rows, :] = ac * jax.nn.sigmoid(ac) * u_ref[rows, :]

        chunk(0)
        lax.fori_loop(1, nc, lambda c, _: chunk(c), None)

    return pl.pallas_call(
        body, name=name, grid=(nb,),
        in_specs=[pl.BlockSpec((T, FFN_W), lambda j: (0, j)), pl.BlockSpec((T, FFN_W), lambda j: (0, nb + j)),
                  pl.BlockSpec((3, FFN_W), lambda j: (0, j)), pl.BlockSpec((1, FFN_W), lambda j: (0, j))],
        out_specs=pl.BlockSpec((T, FFN_W), lambda j: (0, j)), out_shape=jax.ShapeDtypeStruct((T, F), bf16),
        compiler_params=_cp("parallel"),
    )(au, au, cw, cb)


def _ffn_mid_bwd(au, cw, cb, dz, *, name):
    T = au.shape[0]
    F = au.shape[1] // 2
    nb = F // FFN_W
    rc = min(FFN_ROWS, T)
    nc = T // rc

    def body(a_ref, u_ref, w_ref, b_ref, dz_ref, dau_ref, dw_ref, db_ref):
        w, b = w_ref[...], b_ref[...]

        def chunk(c, carry):
            nxt, s0, s1, s2, sb = carry
            a, a1, a2 = _conv_taps(a_ref, c, rc)
            rows = _chunk_rows(c, rc)
            ac = (w[0:1] * a2 + w[1:2] * a1 + w[2:3] * a + b).astype(bf16)
            sg = jax.nn.sigmoid(ac)
            dz = dz_ref[rows, :]
            dau_ref[1, rows, :] = dz * ac * sg
            dac = (dz * u_ref[rows, :] * sg * (1.0 + ac * (1.0 - sg))).astype(f32)
            ext = jnp.concatenate([dac, nxt], axis=0)
            d1, d2 = pltpu.roll(ext, rc + HALO - 1, 0)[:rc], pltpu.roll(ext, rc + HALO - 2, 0)[:rc]
            dau_ref[0, rows, :] = (w[2:3] * dac + w[1:2] * d1 + w[0:1] * d2).astype(bf16)
            tot = lambda v: jnp.sum(v, axis=0, keepdims=True)
            return dac[:HALO], s0 + tot(dac * a2), s1 + tot(dac * a1), s2 + tot(dac * a), sb + tot(dac)

        z = jnp.zeros((1, FFN_W), f32)
        carry = (jnp.zeros((HALO, FFN_W), f32), z, z, z, z)
        carry = lax.fori_loop(0, nc - 1, lambda k, cr: chunk(nc - 1 - k, cr), carry)
        _, s0, s1, s2, sb = chunk(0, carry)
        rows = lax.broadcasted_iota(jnp.int32, (3, FFN_W), 0)
        dw_ref[...] = jnp.where(rows == 0, s0, jnp.where(rows == 1, s1, s2))
        db_ref[...] = sb

    col = lambda off: pl.BlockSpec((T, FFN_W), lambda j: (0, off + j))
    return pl.pallas_call(
        body, name=name, grid=(nb,),
        in_specs=[col(0), col(nb), pl.BlockSpec((3, FFN_W), lambda j: (0, j)), pl.BlockSpec((1, FFN_W), lambda j: (0, j)), col(0)],
        out_specs=[pl.BlockSpec((2, T, FFN_W), lambda j: (0, 0, j)), pl.BlockSpec((3, FFN_W), lambda j: (0, j)),
                   pl.BlockSpec((1, FFN_W), lambda j: (0, j))],
        out_shape=[jax.ShapeDtypeStruct((2, T, F), bf16), jax.ShapeDtypeStruct((3, F), f32), jax.ShapeDtypeStruct((1, F), f32)],
        compiler_params=_cp("parallel"),
    )(au, au, cw, cb, dz)


BNN = (((2,), (1,)), ((0,), (0,)))
BNT = (((2,), (2,)), ((0,), (0,)))
BTN = (((1,), (1,)), ((0,), (0,)))


def _heads(x):
    return jnp.stack([x[:, h * LANES:(h + 1) * LANES] for h in range(HGRN_HEADS)])


def _put_heads(ref, rows, x, dtype):
    for h in range(HGRN_HEADS):
        ref[rows, h * LANES:(h + 1) * LANES] = x[h].astype(dtype)


def _hgrn_lb(l):
    m = jnp.max(l, axis=0, keepdims=True)
    e = jnp.exp(l - m)
    return e[0:1] / jnp.sum(e, axis=0, keepdims=True)


def _hgrn_chunk(q, fx, lb):
    H, C = q.shape[0], q.shape[1]
    sg = jax.nn.sigmoid(fx)
    F = lb + (1.0 - lb) * sg
    k = 1.0 - F
    logF = jnp.log(F)
    r = lax.broadcasted_iota(jnp.int32, (H, C, C), 1)
    c = lax.broadcasted_iota(jnp.int32, (H, C, C), 2)
    tril = (r >= c)
    b = _dot(tril.astype(f32), logF, BNN, precision=HI)
    bl = jnp.sum(logF, axis=1, keepdims=True)
    eb = jnp.exp(b)
    enb = jnp.exp(-b)
    elb = jnp.exp(bl - b)
    return dict(sg=sg, F=F, k=k, b=b, bl=bl, eb=eb, enb=enb, elb=elb, qd=q * eb, kd=k * enb, kl=k * elb, tril=tril)


def _hgrn_fwd(proj, lbl, ng, *, rb=512):
    T = proj.shape[0]
    rb = min(rb, T)
    cpb = rb // HGRN_CHUNK
    nblk = T // rb
    H = HGRN_HEADS

    def body(q_ref, f_ref, i_ref, g_ref, lbl_ref, ng_ref, y_ref, st_ref, S):
        @pl.when(pl.program_id(0) == 0)
        def _():
            S[...] = jnp.zeros_like(S)

        lb = _heads(_hgrn_lb(lbl_ref[...]))
        ngv = _heads(ng_ref[...])
        for c in range(cpb):
            sl = pl.ds(c * HGRN_CHUNK, HGRN_CHUNK)
            v, gx = _heads(i_ref[sl, :]), _heads(g_ref[sl, :])
            ch = _hgrn_chunk(_heads(q_ref[sl, :]), _heads(f_ref[sl, :]), lb)
            att = jnp.where(ch["tril"], _bdot(ch["qd"], ch["kd"], BNT), 0.0)
            St = S[...]
            st_ref[:, c] = St
            o = _bdot(att, v, BNN) + _bdot(ch["qd"], St, BNT)
            S[...] = St * jnp.exp(ch["bl"]) + _bdot(v, ch["kl"], BTN)
            r = lax.rsqrt(jnp.mean(o * o, axis=-1, keepdims=True) + EPS)
            _put_heads(y_ref, sl, o * r * ngv * (gx * jax.nn.sigmoid(gx)), bf16)

    col = lambda off: pl.BlockSpec((rb, H * LANES), lambda n: (n, off))
    return pl.pallas_call(
        body, name="hgrn_fwd", grid=(nblk,),
        in_specs=[col(0), col(1), col(2), col(3), pl.BlockSpec((2, H * LANES), lambda n: (0, 0)),
                  pl.BlockSpec((1, H * LANES), lambda n: (0, 0))],
        out_specs=[pl.BlockSpec((rb, H * LANES), lambda n: (n, 0)),
                   pl.BlockSpec((H, cpb, LANES, LANES), lambda n: (0, n, 0, 0))],
        out_shape=[jax.ShapeDtypeStruct((T, H * LANES), bf16),
                   jax.ShapeDtypeStruct((H, T // HGRN_CHUNK, LANES, LANES), f32)],
        scratch_shapes=[pltpu.VMEM((H, LANES, LANES), f32)], compiler_params=_cp("arbitrary"),
    )(proj, proj, proj, proj, lbl, ng)


def _hgrn_bwd(proj, lbl, ng, states, dy, *, rb=512):
    T = proj.shape[0]
    rb = min(rb, T)
    cpb = rb // HGRN_CHUNK
    nblk = T // rb
    H = HGRN_HEADS
    C = HGRN_CHUNK

    def body(q_ref, f_ref, i_ref, g_ref, lbl_ref, ng_ref, st_ref, dy_ref,
             dq_ref, df_ref, di_ref, dg_ref, dl_ref, dng_ref, dS, dlb_acc, dng_acc):
        n = pl.program_id(0)

        @pl.when(n == 0)
        def _():
            dS[...] = jnp.zeros_like(dS)
            dlb_acc[...] = jnp.zeros_like(dlb_acc)
            dng_acc[...] = jnp.zeros_like(dng_acc)

        lb_row = _hgrn_lb(lbl_ref[...])
        lb = _heads(lb_row)
        ngv = _heads(ng_ref[...])
        r_i = lax.broadcasted_iota(jnp.int32, (H, C, C), 1)
        c_i = lax.broadcasted_iota(jnp.int32, (H, C, C), 2)
        triu = (c_i >= r_i).astype(f32)
        rows_sum = lambda x: jnp.sum(x, axis=1, keepdims=True)
        for c in reversed(range(cpb)):
            sl = pl.ds(c * C, C)
            q, v, gx = _heads(q_ref[sl, :]), _heads(i_ref[sl, :]), _heads(g_ref[sl, :])
            ch = _hgrn_chunk(q, _heads(f_ref[sl, :]), lb)
            qd, kd, kl = ch["qd"], ch["kd"], ch["kl"]
            att = jnp.where(ch["tril"], _bdot(qd, kd, BNT), 0.0)
            St = st_ref[:, c]
            o = _bdot(att, v, BNN) + _bdot(qd, St, BNT)
            r = lax.rsqrt(jnp.mean(o * o, axis=-1, keepdims=True) + EPS)
            on = o * r
            sgg = jax.nn.sigmoid(gx)
            gate = gx * sgg
            dyv = _heads(dy_ref[sl, :].astype(f32))
            _put_heads(dg_ref, sl, dyv * on * ngv * sgg * (1.0 + gx * (1.0 - sgg)), bf16)
            dng_acc[...] += rows_sum(dyv * on * gate)
            don = dyv * ngv * gate
            do = r * (don - on * jnp.mean(don * on, axis=-1, keepdims=True))
            dSt = dS[...]
            dA = jnp.where(ch["tril"], _bdot(do, v, BNT), 0.0)
            dv = _bdot(att, do, BTN) + _bdot(kl, dSt, BNT)
            dqd = _bdot(dA, kd, BNN) + _bdot(do, St, BNN)
            dkd = _bdot(dA, qd, BTN)
            dkl = _bdot(v, dSt, BNN)
            dec = jnp.exp(ch["bl"])
            ddec = rows_sum(St * dSt)
            dS[...] = _bdot(do, qd, BTN) + dSt * dec
            dB = dqd * qd - dkd * kd - dkl * kl
            dbl = rows_sum(dkl * kl) + ddec * dec
            dk = dkd * ch["enb"] + dkl * ch["elb"]
            dlogF = _dot(triu, dB, BNN, precision=HI) + dbl
            dF = dlogF / ch["F"] - dk
            sg = ch["sg"]
            _put_heads(dq_ref, sl, dqd * ch["eb"], bf16)
            _put_heads(di_ref, sl, dv, bf16)
            _put_heads(df_ref, sl, dF * (1.0 - lb) * sg * (1.0 - sg), bf16)
            dlb_acc[...] += rows_sum(dF * (1.0 - sg))

        @pl.when(n == nblk - 1)
        def _():
            rows = lax.broadcasted_iota(jnp.int32, (2, LANES), 0)
            for h in range(H):
                hs = pl.ds(h * LANES, LANES)
                lbh = lb_row[:, h * LANES:(h + 1) * LANES]
                dl0 = dlb_acc[h] * lbh * (1.0 - lbh)
                dl_ref[:, hs] = jnp.where(rows == 0, dl0, -dl0)
                dng_ref[:, hs] = dng_acc[h]

    col = lambda off: pl.BlockSpec((rb, H * LANES), lambda n: (nblk - 1 - n, off))
    vec = lambda rows: pl.BlockSpec((rows, H * LANES), lambda n: (0, 0))
    tok = jax.ShapeDtypeStruct((T, H * LANES), bf16)
    return pl.pallas_call(
        body, name="hgrn_bwd", grid=(nblk,),
        in_specs=[col(0), col(1), col(2), col(3), vec(2), vec(1),
                  pl.BlockSpec((H, cpb, LANES, LANES), lambda n: (0, nblk - 1 - n, 0, 0)), col(0)],
        out_specs=[col(0), col(0), col(0), col(0), vec(2), vec(1)],
        out_shape=[tok, tok, tok, tok, jax.ShapeDtypeStruct((2, H * LANES), f32), jax.ShapeDtypeStruct((1, H * LANES), f32)],
        scratch_shapes=[pltpu.VMEM((H, LANES, LANES), f32), pltpu.VMEM((H, 1, LANES), f32), pltpu.VMEM((H, 1, LANES), f32)],
        compiler_params=_cp("arbitrary"),
    )(proj, proj, proj, proj, lbl, ng, states, dy)


def _s5_disc_math(ar, ai, ldt, br, bi):
    dt = jnp.exp(ldt)
    mag = jnp.exp(ar * dt)
    abr, abi = mag * jnp.cos(ai * dt), mag * jnp.sin(ai * dt)
    den = ar * ar + ai * ai
    xr, xi = abr - 1.0, abi
    cr = (xr * ar + xi * ai) / den
    ci = (xi * ar - xr * ai) / den
    return abr, abi, cr * br - ci * bi, cr * bi + ci * br


def _s5_disc_fwd(ar, ai, ldt, br, bi):
    def body(ar_ref, ai_ref, ldt_ref, br_ref, bi_ref, o0, o1, o2, o3):
        outs = _s5_disc_math(ar_ref[...], ai_ref[...], ldt_ref[...], br_ref[...], bi_ref[...])
        for o, v in zip((o0, o1, o2, o3), outs):
            o[...] = v

    return pl.pallas_call(
        body, name="s5_disc_fwd",
        out_shape=[jax.ShapeDtypeStruct(ar.shape, f32)] * 2 + [jax.ShapeDtypeStruct(br.shape, f32)] * 2,
    )(ar, ai, ldt, br, bi)


def _s5_disc_bwd(ar, ai, ldt, br, bi, cts):
    def body(ar_ref, ai_ref, ldt_ref, br_ref, bi_ref, c0, c1, c2, c3, o0, o1, o2, o3, o4):
        _, vjp = jax.vjp(_s5_disc_math, ar_ref[...], ai_ref[...], ldt_ref[...], br_ref[...], bi_ref[...])
        for o, v in zip((o0, o1, o2, o3, o4), vjp((c0[...], c1[...], c2[...], c3[...]))):
            o[...] = v

    return pl.pallas_call(
        body, name="s5_disc_bwd",
        out_shape=[jax.ShapeDtypeStruct(ar.shape, f32)] * 3 + [jax.ShapeDtypeStruct(br.shape, f32)] * 2,
    )(ar, ai, ldt, br, bi, *cts)


S5_LC = 512
S5_NLC = S5_N // S5_LC
S5_UB = 4
S5_UNROLL = 4
S5_TOGETHER = 2


def _cmul(ar, ai, xr, xi):
    return ar * xr - ai * xi, ar * xi + ai * xr


def _cpow(ar, ai, n):
    rr, ri = None, None
    br, bi = ar, ai
    while n:
        if n & 1:
            rr, ri = (br, bi) if rr is None else _cmul(rr, ri, br, bi)
        n >>= 1
        if n:
            br, bi = _cmul(br, bi, br, bi)
    return rr, ri


def _s5_bu(u_ref, bre_ref, bim_ref, xr, xi):
    for k in range(S5_UB):
        uk = u_ref[:, k * LANES:(k + 1) * LANES].astype(bf16)
        xr[:, k * S5_LC:(k + 1) * S5_LC] = _dot(uk, bre_ref[k])
        xi[:, k * S5_LC:(k + 1) * S5_LC] = _dot(uk, bim_ref[k])


def _s5_scan(xr, xi, sr, si, ar_ref, ai_ref, nsteps, store):
    for c0 in range(0, S5_NLC, S5_TOGETHER):
        css = [slice(c * S5_LC, (c + 1) * S5_LC) for c in range(c0, c0 + S5_TOGETHER)]
        a = [(jnp.broadcast_to(ar_ref[:, cs], (S5_SEG, S5_LC)), jnp.broadcast_to(ai_ref[:, cs], (S5_SEG, S5_LC))) for cs in css]

        def step(j, carry, css=css, a=a):
            rows = pl.ds(pl.multiple_of(j * S5_SEG, S5_SEG), S5_SEG)
            out = []
            for u, cs in enumerate(css):
                (a_r, a_i), pr, pi = a[u], carry[2 * u], carry[2 * u + 1]
                nr = a_r * pr - a_i * pi + xr[rows, cs]
                ni = a_r * pi + a_i * pr + xi[rows, cs]
                if store:
                    xr[rows, cs] = nr
                    xi[rows, cs] = ni
                out += [nr, ni]
            return tuple(out)

        init = tuple(v for cs in css for v in (sr[:, cs], si[:, cs]))
        fin = lax.fori_loop(0, nsteps, step, init)
        for u, cs in enumerate(css):
            sr[:, cs] = fin[2 * u]
            si[:, cs] = fin[2 * u + 1]


def _s5_rscan(dr, di, xr, xi, s0r, s0i, gr, gi, acc_r, acc_i, ar_ref, ai_ref, nsteps):
    for c in range(S5_NLC):
        cs = slice(c * S5_LC, (c + 1) * S5_LC)
        a_r = jnp.broadcast_to(ar_ref[:, cs], (S5_SEG, S5_LC))
        a_i = jnp.broadcast_to(ai_ref[:, cs], (S5_SEG, S5_LC))

        def update(j, carry, before, cs=cs, a_r=a_r, a_i=a_i):
            pr, pi, cr, ci = carry
            rows = pl.ds(j * S5_SEG if isinstance(j, int) else pl.multiple_of(j * S5_SEG, S5_SEG), S5_SEG)
            nr = dr[rows, cs] + a_r * pr + a_i * pi
            ni = di[rows, cs] + a_r * pi - a_i * pr
            dr[rows, cs] = nr
            di[rows, cs] = ni
            if before is not None:
                cr = cr + nr * before[0] + ni * before[1]
                ci = ci - nr * before[1] + ni * before[0]
            return nr, ni, cr, ci

        def step(jj, carry, cs=cs, update=update):
            j = nsteps - 1 - jj
            prev = pl.ds(pl.multiple_of((j - 1) * S5_SEG, S5_SEG), S5_SEG)
            return update(j, carry, None if acc_r is None else (xr[prev, cs], xi[prev, cs]))

        z = jnp.zeros((S5_SEG, S5_LC), f32)
        init = (gr[:, cs], gi[:, cs], z, z)
        carry = lax.fori_loop(0, nsteps - 1, step, init, unroll=S5_UNROLL)
        fr, fi, cr, ci = update(0, carry, None if acc_r is None else (s0r[:, cs], s0i[:, cs]))
        gr[:, cs] = fr
        gi[:, cs] = fi
        if acc_r is not None:
            acc_r[:, cs] += cr
            acc_i[:, cs] += ci


def _s5_seg_carry(fr, fi, ar, ai, seg_len, reverse):
    pr, pi = _cpow(ar, ai if not reverse else -ai, seg_len)
    rows = lax.broadcasted_iota(jnp.int32, fr.shape, 0)
    cr, ci = jnp.zeros_like(fr), jnp.zeros_like(fi)
    sh = (S5_SEG - 1) if reverse else 1
    fr_s, fi_s = pltpu.roll(fr, sh, 0), pltpu.roll(fi, sh, 0)
    order = range(S5_SEG - 2, -1, -1) if reverse else range(1, S5_SEG)
    for r in order:
        c_r, c_i = pltpu.roll(cr, sh, 0), pltpu.roll(ci, sh, 0)
        m_r, m_i = _cmul(pr, pi, c_r, c_i)
        cr = jnp.where(rows == r, m_r + fr_s, cr)
        ci = jnp.where(rows == r, m_i + fi_s, ci)
    return cr, ci


def _gelu_parts(y):
    c0 = math.sqrt(2.0 / math.pi)
    t = jnp.tanh(c0 * (y + 0.044715 * y * y * y))
    z = 0.5 * y * (1.0 + t)
    dz = 0.5 * (1.0 + t) + 0.5 * y * (1.0 - t * t) * c0 * (1.0 + 3.0 * 0.044715 * y * y)
    return z, dz


def _s5_y(xr, xi, u_ref, cre_ref, cim_ref, d_ref):
    ys = []
    for k in range(S5_UB):
        cs = slice(k * S5_LC, (k + 1) * S5_LC)
        ys.append(_bdot(xr[:, cs], cre_ref[k]) - _bdot(xi[:, cs], cim_ref[k]))
    return jnp.concatenate(ys, axis=1) + d_ref[...] * u_ref[...]


def _s5_specs(T, rb, rev=False):
    nblk = T // rb
    blk = (lambda i: (nblk - 1 - i, 0)) if rev else (lambda i: (i, 0))
    tok = pl.BlockSpec((rb, 4 * LANES), blk)
    bmat = pl.BlockSpec((S5_UB, LANES, S5_LC), lambda i: (0, 0, 0))
    cmat = pl.BlockSpec((S5_UB, S5_LC, LANES), lambda i: (0, 0, 0))
    avec = pl.BlockSpec((1, S5_N), lambda i: (0, 0))
    seg = pl.BlockSpec((S5_SEG, S5_N), lambda i: (0, 0))
    cvec = pl.BlockSpec((1, 4 * LANES), lambda i: (0, 0))
    s0 = pl.BlockSpec((1, S5_SEG, S5_N), (lambda i: (nblk - 1 - i, 0, 0)) if rev else (lambda i: (i, 0, 0)))
    return dict(tok=tok, bmat=bmat, cmat=cmat, avec=avec, seg=seg, cvec=cvec, s0=s0, nblk=nblk)


def _s5_final(u, bre, bim, ar, ai, *, rb):
    T = u.shape[0]
    sp = _s5_specs(T, rb)

    def body(u_ref, bre_ref, bim_ref, ar_ref, ai_ref, fr_ref, fi_ref, xr, xi):
        @pl.when(pl.program_id(0) == 0)
        def _():
            fr_ref[...] = jnp.zeros_like(fr_ref)
            fi_ref[...] = jnp.zeros_like(fi_ref)

        _s5_bu(u_ref, bre_ref, bim_ref, xr, xi)
        _s5_scan(xr, xi, fr_ref, fi_ref, ar_ref, ai_ref, rb // S5_SEG, False)

    return pl.pallas_call(
        body, name="s5_final", grid=(sp["nblk"],),
        in_specs=[sp["tok"], sp["bmat"], sp["bmat"], sp["avec"], sp["avec"]], out_specs=[sp["seg"], sp["seg"]],
        out_shape=[jax.ShapeDtypeStruct((S5_SEG, S5_N), f32)] * 2,
        scratch_shapes=[pltpu.VMEM((rb, S5_N), f32)] * 2, compiler_params=_cp("arbitrary"),
    )(u, bre, bim, ar, ai)


def _s5_fwd(u, bre, bim, ar, ai, fr, fi, cre, cim, dsk, wg, bg, *, rb):
    T = u.shape[0]
    sp = _s5_specs(T, rb)
    seg_len = T // S5_SEG

    def body(u_ref, bre_ref, bim_ref, ar_ref, ai_ref, fr_ref, fi_ref, cre_ref, cim_ref, d_ref, wg_ref, bg_ref,
             o_ref, s0r_ref, s0i_ref, xr, xi, sr, si):
        @pl.when(pl.program_id(0) == 0)
        def _():
            i_r, i_i = _s5_seg_carry(fr_ref[...], fi_ref[...], ar_ref[...], ai_ref[...], seg_len, False)
            sr[...] = i_r
            si[...] = i_i

        s0r_ref[0] = sr[...]
        s0i_ref[0] = si[...]
        _s5_bu(u_ref, bre_ref, bim_ref, xr, xi)
        _s5_scan(xr, xi, sr, si, ar_ref, ai_ref, rb // S5_SEG, True)
        y = _s5_y(xr, xi, u_ref, cre_ref, cim_ref, d_ref)
        z, _ = _gelu_parts(y)
        v = _bdot(z, wg_ref[...]) + bg_ref[...]
        o_ref[...] = (z * jax.nn.sigmoid(v)).astype(bf16)

    wspec = pl.BlockSpec((4 * LANES, 4 * LANES), lambda i: (0, 0))
    return pl.pallas_call(
        body, name="s5_fwd", grid=(sp["nblk"],),
        in_specs=[sp["tok"], sp["bmat"], sp["bmat"], sp["avec"], sp["avec"], sp["seg"], sp["seg"], sp["cmat"], sp["cmat"],
                  sp["cvec"], wspec, sp["cvec"]],
        out_specs=[sp["tok"], sp["s0"], sp["s0"]],
        out_shape=[jax.ShapeDtypeStruct((T, 4 * LANES), bf16)] + [jax.ShapeDtypeStruct((sp["nblk"], S5_SEG, S5_N), f32)] * 2,
        scratch_shapes=[pltpu.VMEM((rb, S5_N), f32)] * 2 + [pltpu.VMEM((S5_SEG, S5_N), f32)] * 2,
        compiler_params=_cp("arbitrary"),
    )(u, bre, bim, ar, ai, fr, fi, cre, cim, dsk, wg, bg)


def _s5_bwd_a(u, bre, bim, ar, ai, s0r, s0i, cre, cim, cret, cimt, dsk, wg, bg, dout, *, rb):
    T = u.shape[0]
    sp = _s5_specs(T, rb, rev=True)

    def body(u_ref, bre_ref, bim_ref, ar_ref, ai_ref, s0r_ref, s0i_ref, cre_ref, cim_ref, cret_ref, cimt_ref,
             d_ref, wg_ref, bg_ref, do_ref, dy_ref, glr_ref, gli_ref, dcre_ref, dcim_ref, dd_ref, dwg_ref, dbg_ref,
             xr, xi, dr, di, sr, si):
        @pl.when(pl.program_id(0) == 0)
        def _():
            for r in (glr_ref, gli_ref, dcre_ref, dcim_ref, dd_ref, dwg_ref, dbg_ref):
                r[...] = jnp.zeros_like(r)

        sr[...] = s0r_ref[0]
        si[...] = s0i_ref[0]
        _s5_bu(u_ref, bre_ref, bim_ref, xr, xi)
        _s5_scan(xr, xi, sr, si, ar_ref, ai_ref, rb // S5_SEG, True)
        uv = u_ref[...]
        y = _s5_y(xr, xi, u_ref, cre_ref, cim_ref, d_ref)
        z, gz = _gelu_parts(y)
        v = _bdot(z, wg_ref[...]) + bg_ref[...]
        sg = jax.nn.sigmoid(v)
        dov = do_ref[...].astype(f32)
        dv = dov * z * sg * (1.0 - sg)
        dz = dov * sg + _bdot(dv, wg_ref[...], NT)
        dy = dz * gz
        dy_ref[...] = dy
        dwg_ref[...] += _bdot(z, dv, TN)
        dbg_ref[...] += jnp.sum(dv, axis=0, keepdims=True)
        dd_ref[...] += jnp.sum(dy * uv, axis=0, keepdims=True)
        for k in range(S5_UB):
            cs = slice(k * S5_LC, (k + 1) * S5_LC)
            dyk = dy[:, k * LANES:(k + 1) * LANES]
            dcre_ref[k] += _bdot(xr[:, cs], dyk, TN)
            dcim_ref[k] -= _bdot(xi[:, cs], dyk, TN)
            dr[:, cs] = _bdot(dyk, cret_ref[k])
            di[:, cs] = -_bdot(dyk, cimt_ref[k])
        _s5_rscan(dr, di, None, None, None, None, glr_ref, gli_ref, None, None, ar_ref, ai_ref, rb // S5_SEG)

    wspec = pl.BlockSpec((4 * LANES, 4 * LANES), lambda i: (0, 0))
    return pl.pallas_call(
        body, name="s5_bwd_a", grid=(sp["nblk"],),
        in_specs=[sp["tok"], sp["bmat"], sp["bmat"], sp["avec"], sp["avec"], sp["s0"], sp["s0"], sp["cmat"], sp["cmat"],
                  sp["bmat"], sp["bmat"], sp["cvec"], wspec, sp["cvec"], sp["tok"]],
        out_specs=[sp["tok"], sp["seg"], sp["seg"], sp["cmat"], sp["cmat"], sp["cvec"], wspec, sp["cvec"]],
        out_shape=[jax.ShapeDtypeStruct((T, 4 * LANES), f32)] + [jax.ShapeDtypeStruct((S5_SEG, S5_N), f32)] * 2
        + [jax.ShapeDtypeStruct((S5_UB, S5_LC, LANES), f32)] * 2
        + [jax.ShapeDtypeStruct((1, 4 * LANES), f32), jax.ShapeDtypeStruct((4 * LANES, 4 * LANES), f32),
           jax.ShapeDtypeStruct((1, 4 * LANES), f32)],
        scratch_shapes=[pltpu.VMEM((rb, S5_N), f32)] * 4 + [pltpu.VMEM((S5_SEG, S5_N), f32)] * 2,
        compiler_params=_cp("arbitrary"),
    )(u, bre, bim, ar, ai, s0r, s0i, cre, cim, cret, cimt, dsk, wg, bg, dout)


def _s5_bwd_b(u, bre, bim, bret, bimt, ar, ai, s0r, s0i, glr, gli, cret, cimt, dsk, dy, *, rb):
    T = u.shape[0]
    sp = _s5_specs(T, rb, rev=True)
    seg_len = T // S5_SEG
    nblk = sp["nblk"]

    def body(u_ref, bre_ref, bim_ref, bret_ref, bimt_ref, ar_ref, ai_ref, s0r_ref, s0i_ref, glr_ref, gli_ref,
             cret_ref, cimt_ref, d_ref, dy_ref, du_ref, dbre_ref, dbim_ref, dar_ref, dai_ref,
             xr, xi, dr, di, sr, si, gr, gi, acc_r, acc_i):
        @pl.when(pl.program_id(0) == 0)
        def _():
            x_r, x_i = _s5_seg_carry(glr_ref[...], gli_ref[...], ar_ref[...], ai_ref[...], seg_len, True)
            gr[...] = x_r
            gi[...] = x_i
            acc_r[...] = jnp.zeros_like(acc_r)
            acc_i[...] = jnp.zeros_like(acc_i)
            dbre_ref[...] = jnp.zeros_like(dbre_ref)
            dbim_ref[...] = jnp.zeros_like(dbim_ref)

        sr[...] = s0r_ref[0]
        si[...] = s0i_ref[0]
        _s5_bu(u_ref, bre_ref, bim_ref, xr, xi)
        _s5_scan(xr, xi, sr, si, ar_ref, ai_ref, rb // S5_SEG, True)
        dy = dy_ref[...]
        for k in range(S5_UB):
            cs = slice(k * S5_LC, (k + 1) * S5_LC)
            dyk = dy[:, k * LANES:(k + 1) * LANES]
            dr[:, cs] = _bdot(dyk, cret_ref[k])
            di[:, cs] = -_bdot(dyk, cimt_ref[k])
        sr[...] = s0r_ref[0]
        si[...] = s0i_ref[0]
        _s5_rscan(dr, di, xr, xi, sr, si, gr, gi, acc_r, acc_i, ar_ref, ai_ref, rb // S5_SEG)
        dus = []
        for k in range(S5_UB):
            cs = slice(k * S5_LC, (k + 1) * S5_LC)
            uk = u_ref[:, k * LANES:(k + 1) * LANES]
            dbre_ref[k] += _bdot(uk, dr[:, cs], TN)
            dbim_ref[k] += _bdot(uk, di[:, cs], TN)
            dus.append(_bdot(dr[:, cs], bret_ref[k]) + _bdot(di[:, cs], bimt_ref[k]))
        du_ref[...] = (jnp.concatenate(dus, axis=1) + d_ref[...] * dy).astype(bf16)

        @pl.when(pl.program_id(0) == nblk - 1)
        def _():
            dar_ref[...] = jnp.sum(acc_r[...], axis=0, keepdims=True)
            dai_ref[...] = jnp.sum(acc_i[...], axis=0, keepdims=True)

    return pl.pallas_call(
        body, name="s5_bwd_b", grid=(nblk,),
        in_specs=[sp["tok"], sp["bmat"], sp["bmat"], sp["cmat"], sp["cmat"], sp["avec"], sp["avec"], sp["s0"], sp["s0"],
                  sp["seg"], sp["seg"], sp["bmat"], sp["bmat"], sp["cvec"], sp["tok"]],
        out_specs=[sp["tok"], sp["bmat"], sp["bmat"], sp["avec"], sp["avec"]],
        out_shape=[jax.ShapeDtypeStruct((T, 4 * LANES), bf16)] + [jax.ShapeDtypeStruct((S5_UB, LANES, S5_LC), f32)] * 2
        + [jax.ShapeDtypeStruct((1, S5_N), f32)] * 2,
        scratch_shapes=[pltpu.VMEM((rb, S5_N), f32)] * 4 + [pltpu.VMEM((S5_SEG, S5_N), f32)] * 6,
        compiler_params=_cp("arbitrary"),
    )(u, bre, bim, bret, bimt, ar, ai, s0r, s0i, glr, gli, cret, cimt, dsk, dy)


def _blockdiag(w, transpose=False):
    if transpose:
        w = jnp.swapaxes(w, 1, 2)
    g, a, b = w.shape
    eye = jnp.eye(8, dtype=w.dtype)
    return jnp.einsum("kgab,gj->kgajb", w.reshape(4, 8, a, b), eye).reshape(4, 8 * a, 8 * b)


def _blockdiag_t(m, a, b):
    eye = jnp.eye(8, dtype=m.dtype)
    return jnp.einsum("kgajb,gj->kgab", m.reshape(4, 8, a, 8, b), eye).reshape(32, a, b)


ROT = MLA_ROPE // 2


def _rope_tables(positions):
    freqs = ROPE_THETA ** (-jnp.arange(0, MLA_ROPE, 2, dtype=f32) / MLA_ROPE)
    ang = positions.astype(f32)[:, None] * freqs
    cos, sin, z = jnp.cos(ang), jnp.sin(ang), jnp.zeros_like(ang)
    return (jnp.concatenate([cos, cos, z, z], axis=1), jnp.concatenate([-sin, z, z, z], axis=1),
            jnp.concatenate([z, sin, z, z], axis=1))


def _rot(x, c, sa, sb):
    return x * c + pltpu.roll(x, LANES - ROT, 1) * sa + pltpu.roll(x, ROT, 1) * sb


def _rot_t(dy, c, sa, sb):
    return dy * c + pltpu.roll(dy * sa, ROT, 1) + pltpu.roll(dy * sb, LANES - ROT, 1)


def _rms(xv, g):
    return xv * lax.rsqrt(jnp.mean(xv * xv, axis=-1, keepdims=True) + EPS) * g


QW, KVW = MLA_Q_RANK, MLA_KV_RANK
ODD_PAD = QW + KVW + LANES


def _mla_prep_fwd(proj, qg, kvg, tabs, *, tm=512):
    T = proj.shape[0]
    tm = _tile(T, tm)

    def body(p_ref, qg_ref, kvg_ref, c_ref, sa_ref, sb_ref, cq_ref, ckv_ref, kr_ref):
        cq_ref[...] = _rms(p_ref[:, :QW], qg_ref[...]).astype(bf16)
        ckv_ref[...] = _rms(p_ref[:, QW:QW + KVW], kvg_ref[...]).astype(bf16)
        kr_ref[...] = _rot(p_ref[:, QW + KVW:], c_ref[...], sa_ref[...], sb_ref[...]).astype(bf16)

    row = lambda w: pl.BlockSpec((tm, w), lambda i: (i, 0))
    vec = lambda w: pl.BlockSpec((1, w), lambda i: (0, 0))
    return pl.pallas_call(
        body, name="mla_prep_fwd", grid=(T // tm,),
        in_specs=[row(ODD_PAD), vec(QW), vec(KVW), row(LANES), row(LANES), row(LANES)],
        out_specs=[row(QW), row(KVW), row(LANES)],
        out_shape=[jax.ShapeDtypeStruct((T, QW), bf16), jax.ShapeDtypeStruct((T, KVW), bf16),
                   jax.ShapeDtypeStruct((T, LANES), bf16)],
        compiler_params=_cp("parallel"),
    )(proj, qg, kvg, *tabs)


def _mla_prep_bwd(proj, qg, kvg, tabs, dcqn, dckvn, dkr_heads, *, tm=512):
    T = proj.shape[0]
    tm = _tile(T, tm)

    def body(p_ref, qg_ref, kvg_ref, c_ref, sa_ref, sb_ref, dcq_ref, dckv_ref, dkr_ref, dp_ref, dqg_ref, dkvg_ref):
        dcq, dqg = _rms_bwd_math(p_ref[:, :QW], qg_ref[...], dcq_ref[...])
        dckv, dkvg = _rms_bwd_math(p_ref[:, QW:QW + KVW], kvg_ref[...], dckv_ref[...])
        dk = dkr_ref[:, :LANES]
        for h in range(1, MLA_HEADS):
            dk = dk + dkr_ref[:, h * LANES:(h + 1) * LANES]
        dkr = _rot_t(dk, c_ref[...], sa_ref[...], sb_ref[...])
        dp_ref[...] = jnp.concatenate([dcq, dckv, dkr], axis=1).astype(bf16)

        @pl.when(pl.program_id(0) == 0)
        def _():
            dqg_ref[...] = dqg
            dkvg_ref[...] = dkvg

        @pl.when(pl.program_id(0) > 0)
        def _():
            dqg_ref[...] += dqg
            dkvg_ref[...] += dkvg

    row = lambda w: pl.BlockSpec((tm, w), lambda i: (i, 0))
    vec = lambda w: pl.BlockSpec((1, w), lambda i: (0, 0))
    return pl.pallas_call(
        body, name="mla_prep_bwd", grid=(T // tm,),
        in_specs=[row(ODD_PAD), vec(QW), vec(KVW), row(LANES), row(LANES), row(LANES), row(QW), row(KVW),
                  row(MLA_HEADS * LANES)],
        out_specs=[row(ODD_PAD), vec(QW), vec(KVW)],
        out_shape=[jax.ShapeDtypeStruct((T, ODD_PAD), bf16), jax.ShapeDtypeStruct((1, QW), f32),
                   jax.ShapeDtypeStruct((1, KVW), f32)],
        compiler_params=_cp("arbitrary"),
    )(proj, qg, kvg, *tabs, dcqn, dckvn, dkr_heads)


HQ = 2 * LANES
QK_SCALE = MLA_QK ** -0.5


def _causal_mask(i, j, tq, tk):
    r = lax.broadcasted_iota(jnp.int32, (tq, tk), 0) + i * tq
    c = lax.broadcasted_iota(jnp.int32, (tq, tk), 1) + j * tk
    return c <= r


FLASH_PARTS = 4


def _flash_fwd(q, kv, kr, *, tq=1024, tk=1024):
    T = q.shape[0]
    tq = _tile(T, tq)
    tk = _tile(tq, tk)
    per = tq // tk
    H = MLA_HEADS

    def body(q_ref, kn_ref, v_ref, kr_ref, o_ref, lse_ref, m_s, acc):
        i, j = pl.program_id(1), pl.program_id(2)
        last = (i + 1) * per - 1

        @pl.when(j == 0)
        def _():
            m_s[...] = jnp.full_like(m_s, -jnp.inf)
            acc[...] = jnp.zeros_like(acc)

        def step(masked):
            k = jnp.concatenate([kn_ref[...], kr_ref[...]], axis=1)
            v1 = jnp.concatenate([v_ref[...], jnp.ones((tk, LANES), bf16)], axis=1)
            mask = _causal_mask(i, j, tq, tk) if masked else None
            for part in range(FLASH_PARTS):
                rows = pl.ds(part * (tq // FLASH_PARTS), tq // FLASH_PARTS)
                s = _dot(q_ref[rows, :], k, NT)
                if masked:
                    s = jnp.where(mask[part * (tq // FLASH_PARTS):(part + 1) * (tq // FLASH_PARTS)], s, -jnp.inf)
                m_new = jnp.maximum(m_s[rows, :], jnp.max(s, axis=-1, keepdims=True))
                alpha = jnp.exp(m_s[rows, :] - m_new)
                p = jnp.exp((s - m_new).astype(bf16))
                acc[rows, :] = alpha * acc[rows, :] + _dot(p, v1)
                m_s[rows, :] = m_new

        pl.when(j < i * per)(functools.partial(step, False))
        pl.when((j >= i * per) & (j <= last))(functools.partial(step, True))

        @pl.when(j == last)
        def _():
            l = acc[:, LANES:]
            o_ref[...] = (acc[:, :LANES] / l).astype(bf16)
            lse_ref[0] = m_s[...] + jnp.log(jnp.max(l, axis=-1, keepdims=True))

    kj = lambda i, j: jnp.minimum(j, (i + 1) * per - 1)
    kblk = lambda off: pl.BlockSpec((tk, LANES), lambda h, i, j: (kj(i, j), 2 * h + off))
    return pl.pallas_call(
        body, name="flash_fwd", grid=(H, T // tq, T // tk),
        in_specs=[pl.BlockSpec((tq, HQ), lambda h, i, j: (i, h)), kblk(0), kblk(1),
                  pl.BlockSpec((tk, LANES), lambda h, i, j: (kj(i, j), 0))],
        out_specs=[pl.BlockSpec((tq, LANES), lambda h, i, j: (i, h)), pl.BlockSpec((1, tq, 1), lambda h, i, j: (h, i, 0))],
        out_shape=[jax.ShapeDtypeStruct((T, H * LANES), bf16), jax.ShapeDtypeStruct((H, T, 1), f32)],
        scratch_shapes=[pltpu.VMEM((tq, 1), f32), pltpu.VMEM((tq, 2 * LANES), f32)],
        compiler_params=_cp("parallel", "parallel", "arbitrary"),
    )(q, kv, kv, kr)


def _flash_bwd(q, kv, kr, o, do, lse, tabs, *, tb=1024):
    T = q.shape[0]
    tb = _tile(T, tb)
    nb = T // tb
    H = MLA_HEADS

    def body(q_ref, kn_ref, v_ref, kr_ref, o_ref, do_ref, lse_ref, c_ref, sa_ref, sb_ref,
             dkv_ref, dkr_ref, dq_ref, dk_acc, dv_acc, dq_acc):
        j, ii = pl.program_id(1), pl.program_id(2)
        i = jnp.maximum(ii, j)

        @pl.when((j == 0) & (ii == 0))
        def _():
            dq_acc[...] = jnp.zeros_like(dq_acc)

        @pl.when(ii == 0)
        def _():
            dk_acc[...] = jnp.zeros_like(dk_acc)
            dv_acc[...] = jnp.zeros_like(dv_acc)

        def step(masked):
            k = jnp.concatenate([kn_ref[...], kr_ref[...]], axis=1)
            p = jnp.exp((_dot(q_ref[...], k, NT) - lse_ref[0]).astype(bf16))
            if masked:
                p = jnp.where(_causal_mask(i, j, tb, tb), p, jnp.zeros_like(p))
            delta = jnp.sum(o_ref[...].astype(f32) * do_ref[...], axis=-1, keepdims=True)
            ds = p * (_bdot(do_ref[...], v_ref[...], NT) - delta).astype(bf16)
            dv_acc[...] += _bdot(p, do_ref[...], TN)
            dk_acc[...] += _bdot(ds, q_ref[...], TN)
            dq_acc[pl.ds(pl.multiple_of(i * tb, tb), tb), :] += _bdot(ds, k)

        pl.when(ii > j)(functools.partial(step, False))
        pl.when(ii == j)(functools.partial(step, True))

        @pl.when(ii == nb - 1)
        def _():
            dkv_ref[...] = jnp.concatenate([dk_acc[:, :LANES], dv_acc[...]], axis=1).astype(bf16)
            dkr_ref[...] = dk_acc[:, LANES:]

        @pl.when((j == nb - 1) & (ii == nb - 1))
        def _():
            dq_ref[:, :LANES] = (dq_acc[:, :LANES] * QK_SCALE).astype(bf16)
            dq_ref[:, LANES:] = (_rot_t(dq_acc[:, LANES:], c_ref[...], sa_ref[...], sb_ref[...]) * QK_SCALE).astype(bf16)

    qi = lambda h, j, i: jnp.maximum(i, j)
    kblk = lambda off: pl.BlockSpec((tb, LANES), lambda h, j, i: (j, 2 * h + off))
    vec = pl.BlockSpec((1, tb, 1), lambda h, j, i: (h, qi(h, j, i), 0))
    qblk = pl.BlockSpec((tb, LANES), lambda h, j, i: (qi(h, j, i), h))
    tab = pl.BlockSpec((T, LANES), lambda h, j, i: (0, 0), pipeline_mode=pl.Buffered(1))
    return pl.pallas_call(
        body, name="flash_bwd", grid=(H, nb, nb),
        in_specs=[pl.BlockSpec((tb, HQ), lambda h, j, i: (qi(h, j, i), h)), kblk(0), kblk(1),
                  pl.BlockSpec((tb, LANES), lambda h, j, i: (j, 0)), qblk, qblk, vec, tab, tab, tab],
        out_specs=[pl.BlockSpec((tb, HQ), lambda h, j, i: (j, h)), pl.BlockSpec((tb, LANES), lambda h, j, i: (j, h)),
                   pl.BlockSpec((T, HQ), lambda h, j, i: (0, h))],
        out_shape=[jax.ShapeDtypeStruct((T, H * HQ), bf16), jax.ShapeDtypeStruct((T, H * LANES), f32),
                   jax.ShapeDtypeStruct((T, H * HQ), bf16)],
        scratch_shapes=[pltpu.VMEM((tb, HQ), f32), pltpu.VMEM((tb, LANES), f32), pltpu.VMEM((T, HQ), f32)],
        compiler_params=_cp("parallel", "arbitrary", "arbitrary"),
    )(q, kv, kv, kr, o, do, lse, *tabs)


HBM_SPEC = pl.BlockSpec(memory_space=pltpu.HBM)
N_CHIPS = 4
N_DEV = 8

BIG = {"even_w_in": 1, "s5_w_glu": 0, "even_w_out": 0, "odd_w_in": 0, "mla_w_uq": 1, "mla_w_ukv": 1, "odd_w_out": 0,
       "ffn_w_in": 2, "ffn_w_out": 1}
LAYERED = ("ffn_w_in", "ffn_w_out")
GROUPS = {"even_in": ("even_w_in",), "even_rest": ("s5_w_glu", "even_w_out"), "ffn0": LAYERED,
          "odd": ("odd_w_in", "mla_w_uq", "mla_w_ukv", "odd_w_out"), "ffn1": LAYERED}
GROUP_LAYER = {"ffn0": 0, "ffn1": 1}


def _place():
    x, y, c = lax.axis_index("x"), lax.axis_index("y"), lax.axis_index("c")
    chips = [(1 - x, y), (x, 1 - y), (1 - x, 1 - y)]
    return x, y, c, chips


def _slab(ref, axis, k, size):
    start = pl.multiple_of(k * size, size if axis == 0 else LANES)
    idx = [slice(None)] * len(ref.shape)
    idx[axis] = pl.ds(start, size)
    return ref.at[tuple(idx)]


SEM_SPEC = pl.BlockSpec(memory_space=pltpu.SEMAPHORE)
ANY_SPEC = pl.BlockSpec(memory_space=pl.ANY)
EFFECT = pltpu.SideEffectType.DATAFLOW_SIDE_EFFECTING


def _hbm(a):
    return pltpu.with_memory_space_constraint(a, pltpu.HBM)


class _Gather:
    copies = 3

    def __init__(self, axis, size):
        self.axis, self.size = axis, size

    def view(self, land, kk):
        return _slab(land, self.axis, kk, self.size)

    def own(self, land, place):
        return self.view(land, 2 * place[0] + place[1])

    def sends(self, src, land, place):
        x, y, c, chips = place
        return [(self.own(land, place) if src is None else src, self.own(land, place), (*chip, c)) for chip in chips]

    def recvs(self, land, place):
        return [self.view(land, 2 * cx + cy) for cx, cy in place[3]]


class _Scatter:
    copies = 3

    def __init__(self, axis, size, layer=None):
        self.axis, self.size, self.layer = axis, size, layer

    def row(self, land, j):
        return land.at[j] if self.layer is None else land.at[j, self.layer]

    def sends(self, src, land, place):
        c, chips = place[2], place[3]
        return [(_slab(src, self.axis, 2 * cx + cy, self.size), self.row(land, j), (cx, cy, c))
                for j, (cx, cy) in enumerate(chips)]

    def recvs(self, land, place):
        return [self.row(land, j) for j in range(3)]


class _Sibling:
    copies = 1

    def sends(self, src, land, place):
        x, y, c, _ = place
        return [(src, land, (x, y, 1 - c))]

    def recvs(self, land, place):
        return [land]


class _ToAll:
    copies = N_DEV - 1

    def __init__(self, size):
        self.size = size

    def sends(self, src, land, place):
        x, y, c, _ = place
        flip = lambda v, bit: 1 - v if bit else v
        own = _slab(land, 0, 4 * x + 2 * y + c, self.size)
        return [(own, own, (flip(x, m & 4), flip(y, m & 2), flip(c, m & 1))) for m in range(1, N_DEV)]

    def recvs(self, land, place):
        x, y, c, _ = place
        d = 4 * x + 2 * y + c
        return [_slab(land, 0, d ^ m, self.size) for m in range(1, N_DEV)]


def _unique(arrays):
    out, index = [], {}
    for a in arrays:
        if a is not None and id(a) not in index:
            index[id(a)] = len(out)
            out.append(a)
    return out, index


def _sem_base(routes):
    base = [0]
    for r in routes:
        base.append(base[-1] + r.copies)
    return base


def _push_start(name, items):
    n = len(items)
    base = _sem_base([it[0] for it in items])
    arrays, index = _unique([it[1] for it in items] + [it[2] for it in items])
    na = len(arrays)

    def body(*refs):
        arr, send, recv, token = refs[:na], refs[na], refs[na + 1], refs[-1]
        place = _place()
        for i, (route, src, land) in enumerate(items):
            s_ref = None if src is None else arr[index[id(src)]]
            for j, (s, d, dev) in enumerate(route.sends(s_ref, arr[index[id(land)]], place)):
                pltpu.make_async_remote_copy(src_ref=s, dst_ref=d, send_sem=send.at[base[i] + j], recv_sem=recv.at[base[i] + j],
                                             device_id=dev, device_id_type=MESH).start()
        token[...] = jnp.zeros_like(token)

    res = pl.pallas_call(
        body, name=name,
        out_shape=[pltpu.SemaphoreType.DMA((base[-1],)), pltpu.SemaphoreType.DMA((base[-1],))]
        + [pltpu.HBM(a.shape, a.dtype) for a in arrays] + [jax.ShapeDtypeStruct((SUBLANES, LANES), f32)],
        in_specs=[HBM_SPEC] * na, out_specs=[SEM_SPEC, SEM_SPEC] + [HBM_SPEC] * na + [pl.BlockSpec(memory_space=pltpu.VMEM)],
        input_output_aliases={i: 2 + i for i in range(na)},
        compiler_params=pltpu.CompilerParams(has_side_effects=EFFECT),
    )(*[_hbm(a) for a in arrays])
    thru = lambda a: None if a is None else res[2 + index[id(a)]]
    return (res[0], res[1]), [thru(it[1]) for it in items], [thru(it[2]) for it in items], res[-1]


def _push_wait(name, groups, after, with_srcs=False):
    arrays, index = _unique([a for _, _, srcs, lands in groups for a in list(srcs) + list(lands)])
    na, ng = len(arrays), len(groups)

    def body(*refs):
        arr, sems = refs[:na], refs[na:na + 2 * ng]
        place = _place()
        for g, (routes, _, srcs, lands) in enumerate(groups):
            send, recv = sems[2 * g], sems[2 * g + 1]
            base = _sem_base(routes)
            for i, route in enumerate(routes):
                src, land = None if srcs[i] is None else arr[index[id(srcs[i])]], arr[index[id(lands[i])]]
                for j, ((s, d, dev), mine) in enumerate(zip(route.sends(src, land, place), route.recvs(land, place))):
                    cp = pltpu.make_async_remote_copy(src_ref=s, dst_ref=mine, send_sem=send.at[base[i] + j],
                                                      recv_sem=recv.at[base[i] + j], device_id=dev,
                                                      device_id_type=MESH)
                    cp.wait_send()
                    cp.wait_recv()

    sem_args = [s for g in groups for s in g[1]]
    res = pl.pallas_call(
        body, name=name, out_shape=[pltpu.HBM(a.shape, a.dtype) for a in arrays],
        in_specs=[HBM_SPEC] * na + [SEM_SPEC] * (2 * ng) + [ANY_SPEC] * len(after), out_specs=[HBM_SPEC] * na,
        input_output_aliases={i: i for i in range(na)},
        compiler_params=pltpu.CompilerParams(has_side_effects=EFFECT),
    )(*arrays, *sem_args, *after)
    if with_srcs:
        return [([res[index[id(a)]] for a in g[2]], [res[index[id(a)]] for a in g[3]]) for g in groups]
    return [[res[index[id(a)]] for a in g[3]] for g in groups]


def _place_slab(block, axis, slabs, idx, dtype, *, name):
    R, C = block.shape
    tm = _rows(R, C)
    nr = R // tm
    out_map = (lambda i, k: (i, k[0])) if axis == 1 else (lambda i, k: (k[0] * nr + i, 0))

    def body(k_ref, x_ref, o_ref):
        o_ref[...] = x_ref[...].astype(dtype)

    full = (R, C * slabs) if axis == 1 else (R * slabs, C)
    return pl.pallas_call(
        body, name=name, out_shape=jax.ShapeDtypeStruct(full, dtype),
        grid_spec=pltpu.PrefetchScalarGridSpec(
            num_scalar_prefetch=1, grid=(nr,), in_specs=[pl.BlockSpec((tm, C), lambda i, k: (i, 0))],
            out_specs=pl.BlockSpec((tm, C), out_map)),
        compiler_params=_cp("parallel"),
    )(idx, block)


ELEMENTWISE_BLOCK_BYTES = 1 << 20


def _rows(r, c):
    for t in (512, 256, 128, 64, 32, 16, 8):
        if r % t == 0 and t * c * 4 <= ELEMENTWISE_BLOCK_BYTES:
            return t
    return r


def _sum4(owns, axis, recv, kidx, *, name, dep=None):
    L = len(owns)
    R, C = recv.shape[2:]
    tm = _rows(R, C)
    nr = R // tm
    deps = [] if dep is None else [dep]

    def body(k_ref, *refs):
        own_refs, r_ref, out_ref = refs[:L], refs[L], refs[-1]
        for li in range(L):
            @pl.when(pl.program_id(0) == li)
            def _(o_ref=own_refs[li]):
                out_ref[...] = ((o_ref[...] + r_ref[0, 0].astype(f32)) + r_ref[1, 0].astype(f32)) + r_ref[2, 0].astype(f32)

    own_map = (lambda l, i, k: (i, k[0])) if axis == 1 else (lambda l, i, k: (k[0] * nr + i, 0))
    return pl.pallas_call(
        body, name=name, out_shape=jax.ShapeDtypeStruct((L * R, C), f32),
        grid_spec=pltpu.PrefetchScalarGridSpec(
            num_scalar_prefetch=1, grid=(L, nr),
            in_specs=[pl.BlockSpec((tm, C), own_map)] * L + [pl.BlockSpec((3, 1, tm, C), lambda l, i, k: (0, l, i, 0))]
            + [pl.BlockSpec(memory_space=pl.ANY)] * len(deps),
            out_specs=pl.BlockSpec((tm, C), lambda l, i, k: (l * nr + i, 0))),
        compiler_params=_cp("parallel", "parallel"),
    )(kidx, *owns, recv, *deps)


def _adamw(w, m, v, parts, *, name):
    R, C = w.shape
    tm = _rows(R, C)
    npart = len(parts)

    def body(*refs):
        w_ref, m_ref, v_ref = refs[:3]
        g_ref, d_ref, m2_ref, v2_ref = refs[3 + npart:]
        g = refs[3][...]
        for p_ref in refs[4:3 + npart]:
            g = g + p_ref[...]
        g_ref[...] = g
        d_ref[...], m2_ref[...], v2_ref[...] = _adam_math(w_ref[...], m_ref[...], v_ref[...], g)

    blk = pl.BlockSpec((tm, C), lambda i: (i, 0))
    return pl.pallas_call(
        body, name=name, grid=(R // tm,),
        in_specs=[blk] * (3 + npart), out_specs=[blk] * 4,
        out_shape=[jax.ShapeDtypeStruct((R, C), f32)] * 4, compiler_params=_cp("parallel"),
    )(w, m, v, *parts)


def _adam_math(w, m, v, g):
    m2 = ADAM_B1 * m + (1.0 - ADAM_B1) * g
    v2 = ADAM_B2 * v + (1.0 - ADAM_B2) * (g * g)
    m_hat = m2 / (1.0 - ADAM_B1 ** ADAM_STEP)
    v_hat = v2 / (1.0 - ADAM_B2 ** ADAM_STEP)
    return -ADAM_LR * (m_hat / (jnp.sqrt(v_hat) + ADAM_EPS) + ADAM_WD * w), m2, v2


def _adamw_small(landed, w, m, v, kidx, ra, rb):
    rs = ra + N_CHIPS * rb

    def body(k_ref, l_ref, w_ref, m_ref, v_ref, g_ref, d_ref, m2_ref, v2_ref):
        mine = pl.multiple_of(ra + k_ref[0] * rb, SUBLANES)
        for lo, n, off in ((0, ra, 0), (ra, rb, mine)):
            g = l_ref[pl.ds(off, n), :]
            for d in range(1, N_DEV):
                g = g + l_ref[pl.ds(d * rs + off, n), :]
            rows = pl.ds(lo, n)
            delta, m2, v2 = _adam_math(w_ref[rows, :], m_ref[rows, :], v_ref[rows, :], g)
            g_ref[rows, :] = g
            d_ref[rows, :] = delta
            m2_ref[rows, :] = m2
            v2_ref[rows, :] = v2

    vmem = pl.BlockSpec(memory_space=pltpu.VMEM)
    return pl.pallas_call(
        body, name="adamw_small", out_shape=[jax.ShapeDtypeStruct(w.shape, f32)] * 4,
        grid_spec=pltpu.PrefetchScalarGridSpec(num_scalar_prefetch=1, grid=(), in_specs=[vmem] * 4, out_specs=[vmem] * 4),
        compiler_params=_cp(),
    )(kidx, landed, w, m, v)


def _pad_odd(w):
    return jnp.pad(w, ((0, 0), (0, ODD_PAD - w.shape[1])))


def _uq_cat(w):
    r = w.shape[0]
    return jnp.pad(w.reshape(r, MLA_HEADS, MLA_QK), ((0, 0), (0, 0), (0, HQ - MLA_QK))).reshape(r, MLA_HEADS * HQ)


def _uq_uncat(w):
    r = w.shape[0]
    return w.reshape(r, MLA_HEADS, HQ)[:, :, :MLA_QK].reshape(r, MLA_HEADS * MLA_QK)


def _to_segments(v):
    T, C = v.shape
    return v.reshape(S5_SEG, T // S5_SEG, C).transpose(1, 0, 2).reshape(T, C)


def _from_segments(v):
    T, C = v.shape
    return v.reshape(T // S5_SEG, S5_SEG, C).transpose(1, 0, 2).reshape(T, C)


def _s5_rb(T):
    return min(512, T)


def _ffn_fwd(h, hn, w_in, cw, cb, w_out, tag, next_g=None):
    au = _mm(hn, w_in, out_dtype=bf16, name=f"ffn{tag}_in", tn=1408)
    z = _ffn_mid_fwd(au, cw, cb, name=f"ffn{tag}_mid")
    return _mm(z, w_out, res=h, norm_g=next_g, name=f"ffn{tag}_out", tm=512, tk=D_FF), (hn, au, z)


def _ffn_bwd(h, g, w_in, cw, cb, w_out, saved, dh, tag, dep=None):
    hn, au, z = saved
    dz = _mm(dh, w_out, tb=True, out_dtype=bf16, name=f"ffn{tag}_dz", tn=1408, dep=dep)
    dw_out = _mm(z, dh, ta=True, also_bf16=True, name=f"ffn{tag}_dwout", tm=1408)
    dau, dcw, dcb = _ffn_mid_bwd(au, cw, cb, dz, name=f"ffn{tag}_dmid")
    dh_in, dg = _mm(dau, w_in, tb=True, res=dh, norm_bwd=(h, g), name=f"ffn{tag}_dhn", tk=1408)
    dw_in = _mm(hn, dau, ta=True, also_bf16=True, name=f"ffn{tag}_dwin", tn=1408)
    return dh_in, dg, dw_in, dcw, dcb, dw_out


def _local_step(x, positions, target, get_w, P, put_g):
    T = x.shape[0]
    rb = _s5_rb(T)
    row = lambda v: v.reshape(1, -1)
    g_mix, g_ffn = P["norm_mix_g"], P["norm_ffn_g"]
    lbl, hng = P["hgrn_lb_logits"], P["hgrn_norm_g"]
    dsk, bg = P["s5_d"], P["s5_b_glu"]
    qg, kvg = P["mla_q_norm_g"], P["mla_kv_norm_g"]
    cw, cb = P["ffn_conv_w"], P["ffn_conv_b"]

    col = lambda v: v.reshape(S5_N, 1)
    disc_in = (col(P["s5_a_re"]), col(P["s5_a_im"]), col(jnp.repeat(P["s5_log_dt"].reshape(S5_GROUPS), S5_STATE)),
               P["s5_b_re"].reshape(S5_N, S5_GROUP), P["s5_b_im"].reshape(S5_N, S5_GROUP))
    abr, abi, bbr, bbi = _s5_disc_fwd(*disc_in)
    ar, ai = abr.reshape(1, S5_N), abi.reshape(1, S5_N)
    bbr3, bbi3 = bbr.reshape(S5_GROUPS, S5_STATE, S5_GROUP), bbi.reshape(S5_GROUPS, S5_STATE, S5_GROUP)
    bre, bim = _blockdiag(bbr3, True).astype(bf16), _blockdiag(bbi3, True).astype(bf16)
    bret, bimt = _blockdiag(bbr3).astype(bf16), _blockdiag(bbi3).astype(bf16)
    c_re, c_im = P["s5_c_re"].reshape(S5_GROUPS, S5_GROUP, S5_STATE), P["s5_c_im"].reshape(S5_GROUPS, S5_GROUP, S5_STATE)
    cre, cim = _blockdiag(c_re, True).astype(bf16), _blockdiag(c_im, True).astype(bf16)
    cret, cimt = _blockdiag(c_re).astype(bf16), _blockdiag(c_im).astype(bf16)

    hn0 = _rms_fwd(x, g_mix[0:1], name="mix0_norm")
    We = get_w("even_in", hn0)
    proj_e = _mm(hn0, We["even_w_in"], name="even_in", tn=1280)
    Wr = get_w("even_rest", proj_e)
    ya, states = _hgrn_fwd(proj_e, lbl, hng)
    u_seg = _to_segments(proj_e[:, 4 * 512:])
    fr, fi = _s5_final(u_seg, bre, bim, ar, ai, rb=rb)
    yb_seg, s0r, s0i = _s5_fwd(u_seg, bre, bim, ar, ai, fr, fi, cre, cim, dsk, Wr["s5_w_glu"], bg, rb=rb)
    ycat = jnp.concatenate([ya, _from_segments(yb_seg)], axis=1)
    h1, hnf0 = _mm(ycat, Wr["even_w_out"], res=x, norm_g=g_ffn[0:1], name="even_out")
    Wf0 = get_w("ffn0", h1)
    (h2, hn2), ffn0 = _ffn_fwd(h1, hnf0, Wf0["ffn_w_in"], cw[0], cb[0:1], Wf0["ffn_w_out"], 0, next_g=g_mix[1:2])

    tabs = _rope_tables(positions)
    Wo = get_w("odd", hn2)
    proj_o = _mm(hn2, Wo["odd_w_in"], name="odd_in")
    cqn, ckvn, kr = _mla_prep_fwd(proj_o, qg, kvg, tabs)
    q = _mm(cqn, Wo["mla_w_uq"], out_dtype=bf16, rope=tabs, name="mla_uq")
    kvb = _mm(ckvn, Wo["mla_w_ukv"], out_dtype=bf16, name="mla_ukv")
    o, lse = _flash_fwd(q, kvb, kr)
    h3, hnf1 = _mm(o, Wo["odd_w_out"], res=h2, norm_g=g_ffn[1:2], name="odd_out")
    Wf1 = get_w("ffn1", h3)
    h4, ffn1 = _ffn_fwd(h3, hnf1, Wf1["ffn_w_in"], cw[1], cb[1:2], Wf1["ffn_w_out"], 1)
    loss, dh4, dg_final = _loss_head(h4, row(P["final_norm_g"]), target)

    dh3, dg_ffn1, dw_fin1, dcw1, dcb1, dw_fout1 = _ffn_bwd(
        h3, g_ffn[1:2], Wf1["ffn_w_in"], cw[1], cb[1:2], Wf1["ffn_w_out"], ffn1, dh4, 1)
    sent = put_g("ffn1", {"ffn_w_in": dw_fin1, "ffn_w_out": dw_fout1})
    do = _mm(dh3, Wo["odd_w_out"], tb=True, out_dtype=bf16, name="odd_do", dep=sent)
    dw_oout = _mm(o, dh3, ta=True, also_bf16=True, name="odd_dwout")
    dkv, dkr_h, dq = _flash_bwd(q, kvb, kr, o, do, lse, tabs)
    dw_uq = _mm(cqn, dq, ta=True, also_bf16=True, name="mla_dwuq")
    dcqn = _mm(dq, Wo["mla_w_uq"], tb=True, name="mla_dcq", tk=MLA_HEADS * HQ)
    dw_ukv = _mm(ckvn, dkv, ta=True, also_bf16=True, name="mla_dwukv")
    dckvn = _mm(dkv, Wo["mla_w_ukv"], tb=True, name="mla_dckv")
    dproj_o, dqg, dkvg = _mla_prep_bwd(proj_o, qg, kvg, tabs, dcqn, dckvn, dkr_h)
    dw_oin = _mm(hn2, dproj_o, ta=True, also_bf16=True, name="odd_dwin")
    sent = put_g("odd", {"odd_w_in": dw_oin, "mla_w_uq": dw_uq, "mla_w_ukv": dw_ukv, "odd_w_out": dw_oout})
    dh2, dg_mix1 = _mm(dproj_o, Wo["odd_w_in"], tb=True, res=dh3, norm_bwd=(h2, g_mix[1:2]), name="odd_dhn")

    dh1, dg_ffn0, dw_fin0, dcw0, dcb0, dw_fout0 = _ffn_bwd(
        h1, g_ffn[0:1], Wf0["ffn_w_in"], cw[0], cb[0:1], Wf0["ffn_w_out"], ffn0, dh2, 0, dep=sent)
    sent = put_g("ffn0", {"ffn_w_in": dw_fin0, "ffn_w_out": dw_fout0})
    dycat = _mm(dh1, Wr["even_w_out"], tb=True, out_dtype=bf16, name="even_dy", dep=sent)
    dw_eout = _mm(ycat, dh1, ta=True, also_bf16=True, name="even_dwout")
    dq_h, df_h, di_h, dg_h, dlbl, dhng = _hgrn_bwd(proj_e, lbl, hng, states, dycat)
    dyb_seg = _to_segments(dycat[:, 512:])
    dy_s5, glr, gli, dcre, dcim, dd, dwg, dbg = _s5_bwd_a(
        u_seg, bre, bim, ar, ai, s0r, s0i, cre, cim, cret, cimt, dsk, Wr["s5_w_glu"], bg, dyb_seg, rb=rb)
    du_seg, dbre, dbim, dar, dai = _s5_bwd_b(
        u_seg, bre, bim, bret, bimt, ar, ai, s0r, s0i, glr, gli, cret, cimt, dsk, dy_s5, rb=rb)
    dproj_e = jnp.concatenate([dq_h, df_h, di_h, dg_h, _from_segments(du_seg)], axis=1)
    dx, dg_mix0 = _mm(dproj_e, We["even_w_in"], tb=True, res=dh1, norm_bwd=(x, g_mix[0:1]), name="even_dhn", tk=1280)
    dw_ein = _mm(hn0, dproj_e, ta=True, also_bf16=True, name="even_dwin", tn=1280)

    unblk = lambda m, a, b: jnp.swapaxes(_blockdiag_t(m, a, b), 1, 2)
    dbbr = unblk(dbre, S5_GROUP, S5_STATE).reshape(S5_N, S5_GROUP)
    dbbi = unblk(dbim, S5_GROUP, S5_STATE).reshape(S5_N, S5_GROUP)
    d_ar, d_ai, d_ldt, d_br, d_bi = _s5_disc_bwd(*disc_in, (dar.reshape(S5_N, 1), dai.reshape(S5_N, 1), dbbr, dbbi))
    small = {
        "norm_mix_g": jnp.concatenate([dg_mix0, dg_mix1], axis=0),
        "norm_ffn_g": jnp.concatenate([dg_ffn0, dg_ffn1], axis=0),
        "final_norm_g": dg_final.reshape(-1),
        "hgrn_lb_logits": dlbl, "hgrn_norm_g": dhng,
        "s5_a_re": d_ar.reshape(1, S5_GROUPS, S5_STATE), "s5_a_im": d_ai.reshape(1, S5_GROUPS, S5_STATE),
        "s5_log_dt": d_ldt.reshape(S5_GROUPS, S5_STATE).sum(axis=1).reshape(1, S5_GROUPS),
        "s5_b_re": d_br.reshape(1, S5_GROUPS, S5_STATE, S5_GROUP), "s5_b_im": d_bi.reshape(1, S5_GROUPS, S5_STATE, S5_GROUP),
        "s5_c_re": unblk(dcre, S5_STATE, S5_GROUP).reshape(1, S5_GROUPS, S5_GROUP, S5_STATE),
        "s5_c_im": unblk(dcim, S5_STATE, S5_GROUP).reshape(1, S5_GROUPS, S5_GROUP, S5_STATE),
        "s5_d": dd, "s5_b_glu": dbg, "mla_q_norm_g": dqg, "mla_kv_norm_g": dkvg,
        "ffn_conv_w": jnp.stack([dcw0, dcw1]), "ffn_conv_b": jnp.concatenate([dcb0, dcb1], axis=0),
    }
    put_g("even", {"even_w_in": dw_ein, "s5_w_glu": (dwg, dwg.astype(bf16)), "even_w_out": dw_eout}, small)
    return loss, dx


WEIGHTS = ["norm_mix_g", "norm_ffn_g", "final_norm_g", "even_w_in", "hgrn_lb_logits", "hgrn_norm_g", "s5_a_re", "s5_a_im",
           "s5_log_dt", "s5_b_re", "s5_b_im", "s5_c_re", "s5_c_im", "s5_d", "s5_w_glu", "s5_b_glu", "even_w_out", "odd_w_in",
           "mla_q_norm_g", "mla_w_uq", "mla_kv_norm_g", "mla_w_ukv", "odd_w_out", "ffn_w_in", "ffn_conv_w", "ffn_conv_b",
           "ffn_w_out"]
SMALL_SHARDED = {"mla_q_norm_g": 1, "mla_kv_norm_g": 1, "ffn_conv_w": 2}
SMALL = [n for n in WEIGHTS if n not in BIG]
SMALL_REP = [n for n in SMALL if n not in SMALL_SHARDED]


def _pack_rows(shapes):
    n = sum(math.prod(s) for s in shapes)
    return -(-n // (SUBLANES * LANES)) * SUBLANES


def _pack(arrays, rows):
    flat = jnp.concatenate([a.reshape(-1) for a in arrays])
    return jnp.pad(flat, (0, rows * LANES - flat.shape[0])).reshape(rows, LANES)


def _unpack(block, shapes):
    flat, out, off = block.reshape(-1), [], 0
    for s in shapes:
        n = math.prod(s)
        out.append(flat[off:off + n].reshape(s))
        off += n
    return out


def kernel(x, positions, norm_mix_g, norm_ffn_g, final_norm_g, even_w_in, hgrn_lb_logits, hgrn_norm_g, s5_a_re, s5_a_im, s5_log_dt, s5_b_re, s5_b_im, s5_c_re, s5_c_im, s5_d, s5_w_glu, s5_b_glu, even_w_out, odd_w_in, mla_q_norm_g, mla_w_uq, mla_kv_norm_g, mla_w_ukv, odd_w_out, ffn_w_in, ffn_conv_w, ffn_conv_b, ffn_w_out, loss_target, m_norm_mix_g, m_norm_ffn_g, m_final_norm_g, m_even_w_in, m_hgrn_lb_logits, m_hgrn_norm_g, m_s5_a_re, m_s5_a_im, m_s5_log_dt, m_s5_b_re, m_s5_b_im, m_s5_c_re, m_s5_c_im, m_s5_d, m_s5_w_glu, m_s5_b_glu, m_even_w_out, m_odd_w_in, m_mla_q_norm_g, m_mla_w_uq, m_mla_kv_norm_g, m_mla_w_ukv, m_odd_w_out, m_ffn_w_in, m_ffn_conv_w, m_ffn_conv_b, m_ffn_w_out, v_norm_mix_g, v_norm_ffn_g, v_final_norm_g, v_even_w_in, v_hgrn_lb_logits, v_hgrn_norm_g, v_s5_a_re, v_s5_a_im, v_s5_log_dt, v_s5_b_re, v_s5_b_im, v_s5_c_re, v_s5_c_im, v_s5_d, v_s5_w_glu, v_s5_b_glu, v_even_w_out, v_odd_w_in, v_mla_q_norm_g, v_mla_w_uq, v_mla_kv_norm_g, v_mla_w_ukv, v_odd_w_out, v_ffn_w_in, v_ffn_conv_w, v_ffn_conv_b, v_ffn_w_out):
    args = dict(locals())
    w = {n: args[n] for n in WEIGHTS}
    m = {n: args["m_" + n] for n in WEIGHTS}
    v = {n: args["v_" + n] for n in WEIGHTS}
    k = 2 * lax.axis_index("x") + lax.axis_index("y")
    kidx = k.reshape(1).astype(jnp.int32)
    axis2d = lambda n: BIG[n] - (1 if n in LAYERED else 0)
    slab = lambda n: w[n].shape[1 + axis2d(n)]

    small_sh_shapes = [w[n].shape for n in SMALL_SHARDED]
    rb = _pack_rows(small_sh_shapes)
    items = {}
    for group, names in GROUPS.items():
        layer = GROUP_LAYER.get(group, 0)
        items[group] = [(_Gather(axis2d(n), slab(n)), None,
                         _place_slab(w[n][layer], axis2d(n), N_CHIPS, kidx, bf16, name=f"place_{n}_{layer}")) for n in names]
    items["even_in"].append((_Gather(0, rb), None,
                             _place_slab(_pack([w[n] for n in SMALL_SHARDED], rb), 0, N_CHIPS, kidx, f32, name="place_small")))
    gathers, tokens = {}, []
    for group in GROUPS:
        sems, srcs, lands, token = _push_start(f"gather_start_{group}", items[group])
        gathers[group] = ([it[0] for it in items[group]], sems, srcs, lands)
        tokens.append(token[0, 0])
    started = functools.reduce(jnp.add, tokens)

    def landed(group, after):
        return _push_wait(f"gather_wait_{group}", [gathers[group]], [after])[0]

    even = landed("even_in", (started + norm_mix_g[0, 0]).reshape(1))
    per_chip = [_unpack(even[-1][c * rb:(c + 1) * rb], small_sh_shapes) for c in range(N_CHIPS)]
    P = {n: w[n] for n in SMALL_REP}
    for i, (n, ax) in enumerate(SMALL_SHARDED.items()):
        P[n] = jnp.concatenate([per_chip[c][i] for c in range(N_CHIPS)], axis=ax)
    P["mla_q_norm_g"], P["mla_kv_norm_g"] = P["mla_q_norm_g"].reshape(1, -1), P["mla_kv_norm_g"].reshape(1, -1)
    fix_w = {"odd_w_in": _pad_odd, "mla_w_uq": _uq_cat}

    def get_w(group, after):
        full = even if group == "even_in" else landed(group, after)
        return {n: fix_w.get(n, lambda a: a)(a) for n, a in zip(GROUPS[group], full)}

    fix_g = {"odd_w_in": lambda g: g[:, :odd_w_in.shape[2]], "mla_w_uq": _uq_uncat}
    g32, scatters, land_now = {}, {}, {}
    ra = _pack_rows([w[n].shape for n in SMALL_REP])
    rs = ra + N_CHIPS * rb
    didx = (2 * kidx + lax.axis_index("c")).astype(jnp.int32)

    def put_g(group, grads, small=None):
        layer = GROUP_LAYER.get(group)
        routes, srcs, names = [], [], list(grads)
        for n in names:
            f = fix_g.get(n, lambda g: g)
            g32.setdefault(n, {})[layer or 0] = f(grads[n][0])
            routes.append(_Scatter(axis2d(n), slab(n), layer if n in LAYERED else None))
            srcs.append(f(grads[n][1]))
            if n not in land_now:
                land_now[n] = lax.empty((3,) + w[n].shape[0 if n in LAYERED else 1:], bf16)
        if small is not None:
            blocks = [_pack([small[n] for n in SMALL_REP], ra)]
            for chip in range(N_CHIPS):
                sl = lambda n, ax: lax.slice_in_dim(small[n].reshape(w[n].shape[:ax] + (-1,) + w[n].shape[ax + 1:]),
                                                    chip * w[n].shape[ax], (chip + 1) * w[n].shape[ax], axis=ax)
                blocks.append(_pack([sl(n, ax) for n, ax in SMALL_SHARDED.items()], rb))
            names.append("small")
            routes.append(_ToAll(rs))
            srcs.append(None)
            land_now["small"] = _place_slab(jnp.concatenate(blocks), 0, N_DEV, didx, f32, name="place_small_grads")
        sems, srcs, lands, token = _push_start(f"scatter_start_{group}", [(r, s, land_now[n]) for r, s, n in zip(routes, srcs, names)])
        land_now.update(zip(names, lands))
        scatters[group] = (routes, sems, srcs, names)
        sent.append(token)
        return token

    sent = []
    loss, dx = _local_step(x[0], positions[0], loss_target[0], get_w, P, put_g)
    sent_last = sent[-1]
    loss = lax.psum(loss[0, 0], ("x", "y", "c"))

    out = {}

    def arrive(tag, groups, after):
        waits = [(scatters[g][0], scatters[g][1], scatters[g][2], [land_now[n] for n in scatters[g][3]]) for g in groups]
        for g, lands in zip(groups, _push_wait(f"scatter_wait_{tag}", waits, after)):
            land_now.update(zip(scatters[g][3], lands))

    def cross(tag, names, dep=None):
        part = {}
        for n in names:
            recv = land_now[n] if n in LAYERED else land_now[n][:, None]
            part[n] = _sum4([g32[n][l] for l in sorted(g32[n])], axis2d(n), recv, kidx, name=f"sum4_{n}", dep=dep)
        items = [(_Sibling(), part[n], lax.empty(part[n].shape, f32)) for n in names]
        sems, srcs, lands, token = _push_start(f"swap_start_{tag}", items)
        return (names, part, ([it[0] for it in items], sems, srcs, lands)), token

    def update(tag, arrived, after):
        names, _, push = arrived
        mine, theirs = _push_wait(f"swap_wait_{tag}", [push], after, with_srcs=True)[0]
        part, other = dict(zip(names, mine)), dict(zip(names, theirs))
        done = []
        for n in names:
            C = part[n].shape[-1]
            res = _adamw(w[n].reshape(-1, C), m[n].reshape(-1, C), v[n].reshape(-1, C), [part[n], other[n]], name=f"adamw_{n}")
            out[n] = [r.reshape(w[n].shape) for r in res]
            done.append(res[0])
        return done

    arrive("a", ["ffn1", "odd", "ffn0"], [dx, sent_last])
    a1, token_a1 = cross("a1", ["ffn_w_in"])
    a2, token_a2 = cross("a2", ["ffn_w_out"] + list(GROUPS["odd"]), dep=token_a1)
    done = update("a2", a2, update("a1", a1, [token_a2]))
    arrive("b", ["even"], done)
    b, token_b = cross("b", list(GROUPS["even_in"]) + list(GROUPS["even_rest"]))

    order = SMALL_REP + list(SMALL_SHARDED)
    packed = lambda src: jnp.concatenate([_pack([src[n] for n in SMALL_REP], ra), _pack([src[n] for n in SMALL_SHARDED], rb)])
    res = _adamw_small(land_now["small"], packed(w), packed(m), packed(v), kidx, ra, rb)
    update("b", b, [res[0], token_b])
    for r in res:
        parts = _unpack(r[:ra], [w[n].shape for n in SMALL_REP]) + _unpack(r[ra:], small_sh_shapes)
        for n, a in zip(order, parts):
            out.setdefault(n, []).append(a)

    return (loss, dx[None], *[out[n][0] for n in WEIGHTS], *[out[n][1] for n in WEIGHTS],
            *[out[n][2] for n in WEIGHTS], *[out[n][3] for n in WEIGHTS])
```

```python
import functools
import math

import jax
import jax.numpy as jnp
from jax import lax
from jax.experimental import pallas as pl
from jax.experimental.pallas import tpu as pltpu

f32, bf16 = jnp.float32, jnp.bfloat16
EPS = 1e-6
LANES = 128
SUBLANES = 8
VMEM_BYTES = 48 * 1024 * 1024
HGRN_CHUNK = 64
HGRN_HEADS = 4
S5_GROUPS, S5_STATE, S5_GROUP = 32, 64, 16
S5_N = S5_GROUPS * S5_STATE
S5_SEG = SUBLANES
MLA_HEADS, MLA_NOPE, MLA_ROPE, MLA_V = 8, 128, 64, 128
MLA_QK = MLA_NOPE + MLA_ROPE
MLA_Q_RANK, MLA_KV_RANK = 384, 256
ROPE_THETA = 10000.0
D_FF = 2816
ADAM_LR, ADAM_B1, ADAM_B2, ADAM_EPS, ADAM_WD, ADAM_STEP = 0.001, 0.9, 0.999, 1e-08, 0.01, 10
MESH = pl.DeviceIdType.MESH
HI = lax.Precision.HIGHEST


def _cp(*dims):
    return pltpu.CompilerParams(dimension_semantics=dims if dims else None, vmem_limit_bytes=VMEM_BYTES)


def _tile(n, t):
    if n <= t:
        return n
    c = (t // LANES) * LANES
    while c >= LANES:
        if n % c == 0:
            return c
        c -= LANES
    return n


def _dot(a, b, dn=None, precision=None):
    if dn is None:
        dn = (((a.ndim - 1,), (0,)), ((), ()))
    return lax.dot_general(a, b, dn, preferred_element_type=f32, precision=precision)


NT = (((1,), (1,)), ((), ()))
TN = (((0,), (0,)), ((), ()))


def _bdot(a, b, dn=None):
    return _dot(a.astype(bf16), b.astype(bf16), dn)


MM_PARTS = 2


def _mm(a, b, *, name, ta=False, tb=False, out_dtype=f32, res=None, also_bf16=False, tm=1024, tn=1024, tk=1024, dep=None,
        norm_g=None, norm_bwd=None, rope=None, loss=None):
    halves = lambda s: (s[1], 2 * s[2]) if len(s) == 3 else s
    M, K = (a.shape[1], a.shape[0]) if ta else halves(a.shape)
    N = b.shape[0] if tb else halves(b.shape)[1]
    rows = norm_g is not None or norm_bwd is not None or loss is not None
    if rows:
        tm, tn, tk = 512, N, K
    tm, tn, tk = _tile(M, tm), _tile(N, tn), _tile(K, tk)
    both = rows and a.ndim == 3 and tb
    if a.ndim == 3 and not both:
        tk = _tile(K // 2, tk)
    if b.ndim == 3:
        tn = _tile(N // 2, tn)
    nk = K // tk
    parts = MM_PARTS if tm % (MM_PARTS * LANES) == 0 else 1
    dn = (((0 if ta else 1,), (1 if tb else 0,)), ((), ()))
    extra = [] if norm_bwd is None else list(norm_bwd)
    if norm_g is not None:
        extra.append(norm_g)
    if rope is not None:
        extra += list(rope)
    if loss is not None:
        extra += list(loss)

    def body(*refs):
        a_ref, b_ref = refs[0], refs[1]
        r_ref = refs[2] if res is not None else None
        nin = 2 + (res is not None) + (dep is not None) + len(extra)
        ex = refs[nin - len(extra):nin]
        outs = refs[nin:-1] if nk > 1 else refs[nin:]
        acc = refs[-1] if nk > 1 else None
        k = pl.program_id(2)
        b_blk = b_ref[...]
        if nk > 1:
            @pl.when(k == 0)
            def _():
                acc[...] = jnp.zeros_like(acc)

        groups = []
        for part in range(parts):
            rows = pl.ds(part * (tm // parts), tm // parts)
            if both:
                p = _bdot(a_ref[0, rows, :], b_blk[:, :K // 2], dn) + _bdot(a_ref[1, rows, :], b_blk[:, K // 2:], dn)
            else:
                p = _bdot(a_ref[:, rows] if ta else a_ref[rows, :], b_blk, dn)
            if nk > 1:
                acc[rows, :] += p
            groups.append((rows, p))

        def epilogue():
            for part, (rows, p) in enumerate(groups):
                r = acc[rows, :] if nk > 1 else p
                if norm_bwd is not None:
                    r, dg = _rms_bwd_math(ex[0][rows, :], ex[1][...], r)
                    if part == 0:
                        @pl.when(pl.program_id(0) == 0)
                        def _(dg=dg):
                            outs[1][...] = dg

                    @pl.when((pl.program_id(0) > 0) | (part > 0))
                    def _(dg=dg):
                        outs[1][...] += dg
                if r_ref is not None:
                    r = r + r_ref[rows, :]
                if rope is not None:
                    c, sa, sb = (t[rows, :] for t in ex[-3:])
                    for h in range(tn // HQ):
                        lo = h * HQ
                        outs[0][rows, lo:lo + LANES] = (r[:, lo:lo + LANES] * QK_SCALE).astype(out_dtype)
                        outs[0][rows, lo + LANES:lo + HQ] = (_rot(r[:, lo + LANES:lo + HQ], c, sa, sb) * QK_SCALE).astype(out_dtype)
                    continue
                if loss is not None:
                    gv = ex[-2][...]
                    e = _rms(r, gv) - ex[-1][rows, :]
                    part_loss = 0.5 * jnp.sum(jnp.mean(e * e, axis=-1, keepdims=True), axis=0, keepdims=True)
                    dx, dg = _rms_bwd_math(r, gv, e * (1.0 / N))
                    outs[0][rows, :] = dx
                    if part == 0:
                        @pl.when(pl.program_id(0) == 0)
                        def _(part_loss=part_loss, dg=dg):
                            outs[1][...] = part_loss
                            outs[2][...] = dg

                    @pl.when((pl.program_id(0) > 0) | (part > 0))
                    def _(part_loss=part_loss, dg=dg):
                        outs[1][...] += part_loss
                        outs[2][...] += dg
                    continue
                outs[0][rows, :] = r.astype(out_dtype)
                if also_bf16:
                    outs[1][rows, :] = r.astype(bf16)
                if norm_g is not None:
                    outs[1][rows, :] = _rms(r, ex[-1][...]).astype(bf16)

        if nk > 1:
            pl.when(k == nk - 1)(epilogue)
        else:
            epilogue()

    a_spec = pl.BlockSpec((tk, tm), lambda i, j, k: (k, i)) if ta else pl.BlockSpec((tm, tk), lambda i, j, k: (i, k))
    b_spec = pl.BlockSpec((tn, tk), lambda i, j, k: (j, k)) if tb else pl.BlockSpec((tk, tn), lambda i, j, k: (k, j))
    if rows:
        b_spec = pl.BlockSpec((tn, tk) if tb else (tk, tn), lambda i, j, k: (0, 0), pipeline_mode=pl.Buffered(1))
    if both:
        a_spec = pl.BlockSpec((2, tm, K // 2), lambda i, j, k: (0, i, 0))
    elif a.ndim == 3:
        kh = K // 2 // tk
        a_spec = pl.BlockSpec((None, tm, tk), lambda i, j, k: (k // kh, i, k % kh))
    if b.ndim == 3:
        nh = N // 2 // tn
        b_spec = pl.BlockSpec((None, tk, tn), lambda i, j, k: (j // nh, k, j % nh))
    o_spec = pl.BlockSpec((tm, tn), lambda i, j, k: (i, j))
    in_specs, args = [a_spec, b_spec], [a, b]
    if res is not None:
        in_specs.append(o_spec)
        args.append(res)
    if dep is not None:
        in_specs.append(pl.BlockSpec(memory_space=pl.ANY))
        args.append(dep)
    vec = pl.BlockSpec((1, tn), lambda i, j, k: (0, j))
    if norm_bwd is not None:
        in_specs += [o_spec, vec]
    if norm_g is not None:
        in_specs.append(vec)
    if rope is not None:
        in_specs += [pl.BlockSpec((tm, LANES), lambda i, j, k: (i, 0))] * 3
    if loss is not None:
        in_specs += [vec, o_spec]
    args += extra
    out_shape = [jax.ShapeDtypeStruct((M, N), out_dtype)]
    out_specs = [o_spec]
    if also_bf16 or norm_g is not None:
        out_shape.append(jax.ShapeDtypeStruct((M, N), bf16))
        out_specs.append(o_spec)
    if norm_bwd is not None:
        out_shape.append(jax.ShapeDtypeStruct((1, N), f32))
        out_specs.append(vec)
    if loss is not None:
        out_shape += [jax.ShapeDtypeStruct((1, 1), f32), jax.ShapeDtypeStruct((1, N), f32)]
        out_specs += [pl.BlockSpec((1, 1), lambda i, j, k: (0, 0)), vec]
    dims = ("arbitrary" if norm_bwd is not None or loss is not None else "parallel", "parallel", "arbitrary")
    out = pl.pallas_call(
        body, name=name, grid=(M // tm, N // tn, nk), in_specs=in_specs, out_specs=out_specs, out_shape=out_shape,
        scratch_shapes=[pltpu.VMEM((tm, tn), f32)] if nk > 1 else [], compiler_params=_cp(*dims),
    )(*args)
    return out if len(out) > 1 else out[0]


def _rms_fwd(x, g, *, name, tm=512):
    T, width = x.shape
    tm = _tile(T, tm)

    def body(x_ref, g_ref, o_ref):
        xv = x_ref[...]
        r = lax.rsqrt(jnp.mean(xv * xv, axis=-1, keepdims=True) + EPS)
        o_ref[...] = (xv * r * g_ref[...]).astype(bf16)

    return pl.pallas_call(
        body, name=name, grid=(T // tm,),
        in_specs=[pl.BlockSpec((tm, width), lambda i: (i, 0)), pl.BlockSpec((1, width), lambda i: (0, 0))],
        out_specs=pl.BlockSpec((tm, width), lambda i: (i, 0)), out_shape=jax.ShapeDtypeStruct((T, width), bf16),
        compiler_params=_cp("parallel"),
    )(x, g)


def _rms_bwd_math(xv, g, dy):
    r = lax.rsqrt(jnp.mean(xv * xv, axis=-1, keepdims=True) + EPS)
    xh = xv * r
    dxh = dy * g
    dx = r * (dxh - xh * jnp.mean(dxh * xh, axis=-1, keepdims=True))
    dg = jnp.sum(dy * xh, axis=0, keepdims=True)
    return dx, dg


FFN_W = 2 * LANES
FFN_ROWS = 128
HALO = 2 * SUBLANES


def _conv_taps(a_ref, c, rc):
    if isinstance(c, int) and c == 0:
        ext = jnp.concatenate([jnp.zeros((HALO, FFN_W), f32), a_ref[pl.ds(0, rc), :].astype(f32)], axis=0)
    else:
        ext = a_ref[pl.ds(pl.multiple_of(c * rc - HALO, HALO), rc + HALO), :].astype(f32)
    return ext[HALO:], pltpu.roll(ext, 1, 0)[HALO:], pltpu.roll(ext, 2, 0)[HALO:]


def _chunk_rows(c, rc):
    return pl.ds(c * rc, rc) if isinstance(c, int) else pl.ds(pl.multiple_of(c * rc, rc), rc)


def _ffn_mid_fwd(au, cw, cb, *, name):
    T = au.shape[0]
    F = au.shape[1] // 2
    nb = F // FFN_W
    rc = min(FFN_ROWS, T)
    nc = T // rc

    def body(a_ref, u_ref, w_ref, b_ref, z_ref):
        w, b = w_ref[...], b_ref[...]

        def chunk(c):
            a, a1, a2 = _conv_taps(a_ref, c, rc)
            rows = _chunk_rows(c, rc)
            ac = (w[0:1] * a2 + w[1:2] * a1 + w[2:3] * a + b).astype(bf16)
            z_ref[rows, :] = ac * jax.nn.sigmoid(ac) * u_ref[rows, :]

        chunk(0)
        lax.fori_loop(1, nc, lambda c, _: chunk(c), None)

    return pl.pallas_call(
        body, name=name, grid=(nb,),
        in_specs=[pl.BlockSpec((T, FFN_W), lambda j: (0, j)), pl.BlockSpec((T, FFN_W), lambda j: (0, nb + j)),
                  pl.BlockSpec((3, FFN_W), lambda j: (0, j)), pl.BlockSpec((1, FFN_W), lambda j: (0, j))],
        out_specs=pl.BlockSpec((T, FFN_W), lambda j: (0, j)), out_shape=jax.ShapeDtypeStruct((T, F), bf16),
        compiler_params=_cp("parallel"),
    )(au, au, cw, cb)


def _ffn_mid_bwd(au, cw, cb, dz, *, name):
    T = au.shape[0]
    F = au.shape[1] // 2
    nb = F // FFN_W
    rc = min(FFN_ROWS, T)
    nc = T // rc

    def body(a_ref, u_ref, w_ref, b_ref, dz_ref, dau_ref, dw_ref, db_ref):
        w, b = w_ref[...], b_ref[...]

        def chunk(c, carry):
            nxt, s0, s1, s2, sb = carry
            a, a1, a2 = _conv_taps(a_ref, c, rc)
            rows = _chunk_rows(c, rc)
            ac = (w[0:1] * a2 + w[1:2] * a1 + w[2:3] * a + b).astype(bf16)
            sg = jax.nn.sigmoid(ac)
            dz = dz_ref[rows, :]
            dau_ref[1, rows, :] = dz * ac * sg
            dac = (dz * u_ref[rows, :] * sg * (1.0 + ac * (1.0 - sg))).astype(f32)
            ext = jnp.concatenate([dac, nxt], axis=0)
            d1, d2 = pltpu.roll(ext, rc + HALO - 1, 0)[:rc], pltpu.roll(ext, rc + HALO - 2, 0)[:rc]
            dau_ref[0, rows, :] = (w[2:3] * dac + w[1:2] * d1 + w[0:1] * d2).astype(bf16)
            tot = lambda v: jnp.sum(v, axis=0, keepdims=True)
            return dac[:HALO], s0 + tot(dac * a2), s1 + tot(dac * a1), s2 + tot(dac * a), sb + tot(dac)

        z = jnp.zeros((1, FFN_W), f32)
        carry = (jnp.zeros((HALO, FFN_W), f32), z, z, z, z)
        carry = lax.fori_loop(0, nc - 1, lambda k, cr: chunk(nc - 1 - k, cr), carry)
        _, s0, s1, s2, sb = chunk(0, carry)
        rows = lax.broadcasted_iota(jnp.int32, (3, FFN_W), 0)
        dw_ref[...] = jnp.where(rows == 0, s0, jnp.where(rows == 1, s1, s2))
        db_ref[...] = sb

    col = lambda off: pl.BlockSpec((T, FFN_W), lambda j: (0, off + j))
    return pl.pallas_call(
        body, name=name, grid=(nb,),
        in_specs=[col(0), col(nb), pl.BlockSpec((3, FFN_W), lambda j: (0, j)), pl.BlockSpec((1, FFN_W), lambda j: (0, j)), col(0)],
        out_specs=[pl.BlockSpec((2, T, FFN_W), lambda j: (0, 0, j)), pl.BlockSpec((3, FFN_W), lambda j: (0, j)),
                   pl.BlockSpec((1, FFN_W), lambda j: (0, j))],
        out_shape=[jax.ShapeDtypeStruct((2, T, F), bf16), jax.ShapeDtypeStruct((3, F), f32), jax.ShapeDtypeStruct((1, F), f32)],
        compiler_params=_cp("parallel"),
    )(au, au, cw, cb, dz)


BNN = (((2,), (1,)), ((0,), (0,)))
BNT = (((2,), (2,)), ((0,), (0,)))
BTN = (((1,), (1,)), ((0,), (0,)))


def _heads(x):
    return jnp.stack([x[:, h * LANES:(h + 1) * LANES] for h in range(HGRN_HEADS)])


def _put_heads(ref, rows, x, dtype):
    for h in range(HGRN_HEADS):
        ref[rows, h * LANES:(h + 1) * LANES] = x[h].astype(dtype)


def _hgrn_lb(l):
    m = jnp.max(l, axis=0, keepdims=True)
    e = jnp.exp(l - m)
    return e[0:1] / jnp.sum(e, axis=0, keepdims=True)


def _hgrn_chunk(q, fx, lb):
    H, C = q.shape[0], q.shape[1]
    sg = jax.nn.sigmoid(fx)
    F = lb + (1.0 - lb) * sg
    k = 1.0 - F
    logF = jnp.log(F)
    r = lax.broadcasted_iota(jnp.int32, (H, C, C), 1)
    c = lax.broadcasted_iota(jnp.int32, (H, C, C), 2)
    tril = (r >= c)
    b = _dot(tril.astype(f32), logF, BNN, precision=HI)
    bl = jnp.sum(logF, axis=1, keepdims=True)
    eb = jnp.exp(b)
    enb = jnp.exp(-b)
    elb = jnp.exp(bl - b)
    return dict(sg=sg, F=F, k=k, b=b, bl=bl, eb=eb, enb=enb, elb=elb, qd=q * eb, kd=k * enb, kl=k * elb, tril=tril)


def _hgrn_fwd(proj, lbl, ng, *, rb=512):
    T = proj.shape[0]
    rb = min(rb, T)
    cpb = rb // HGRN_CHUNK
    nblk = T // rb
    H = HGRN_HEADS

    def body(q_ref, f_ref, i_ref, g_ref, lbl_ref, ng_ref, y_ref, st_ref, S):
        @pl.when(pl.program_id(0) == 0)
        def _():
            S[...] = jnp.zeros_like(S)

        lb = _heads(_hgrn_lb(lbl_ref[...]))
        ngv = _heads(ng_ref[...])
        for c in range(cpb):
            sl = pl.ds(c * HGRN_CHUNK, HGRN_CHUNK)
            v, gx = _heads(i_ref[sl, :]), _heads(g_ref[sl, :])
            ch = _hgrn_chunk(_heads(q_ref[sl, :]), _heads(f_ref[sl, :]), lb)
            att = jnp.where(ch["tril"], _bdot(ch["qd"], ch["kd"], BNT), 0.0)
            St = S[...]
            st_ref[:, c] = St
            o = _bdot(att, v, BNN) + _bdot(ch["qd"], St, BNT)
            S[...] = St * jnp.exp(ch["bl"]) + _bdot(v, ch["kl"], BTN)
            r = lax.rsqrt(jnp.mean(o * o, axis=-1, keepdims=True) + EPS)
            _put_heads(y_ref, sl, o * r * ngv * (gx * jax.nn.sigmoid(gx)), bf16)

    col = lambda off: pl.BlockSpec((rb, H * LANES), lambda n: (n, off))
    return pl.pallas_call(
        body, name="hgrn_fwd", grid=(nblk,),
        in_specs=[col(0), col(1), col(2), col(3), pl.BlockSpec((2, H * LANES), lambda n: (0, 0)),
                  pl.BlockSpec((1, H * LANES), lambda n: (0, 0))],
        out_specs=[pl.BlockSpec((rb, H * LANES), lambda n: (n, 0)),
                   pl.BlockSpec((H, cpb, LANES, LANES), lambda n: (0, n, 0, 0))],
        out_shape=[jax.ShapeDtypeStruct((T, H * LANES), bf16),
                   jax.ShapeDtypeStruct((H, T // HGRN_CHUNK, LANES, LANES), f32)],
        scratch_shapes=[pltpu.VMEM((H, LANES, LANES), f32)], compiler_params=_cp("arbitrary"),
    )(proj, proj, proj, proj, lbl, ng)


def _hgrn_bwd(proj, lbl, ng, states, dy, *, rb=512):
    T = proj.shape[0]
    rb = min(rb, T)
    cpb = rb // HGRN_CHUNK
    nblk = T // rb
    H = HGRN_HEADS
    C = HGRN_CHUNK

    def body(q_ref, f_ref, i_ref, g_ref, lbl_ref, ng_ref, st_ref, dy_ref,
             dq_ref, df_ref, di_ref, dg_ref, dl_ref, dng_ref, dS, dlb_acc, dng_acc):
        n = pl.program_id(0)

        @pl.when(n == 0)
        def _():
            dS[...] = jnp.zeros_like(dS)
            dlb_acc[...] = jnp.zeros_like(dlb_acc)
            dng_acc[...] = jnp.zeros_like(dng_acc)

        lb_row = _hgrn_lb(lbl_ref[...])
        lb = _heads(lb_row)
        ngv = _heads(ng_ref[...])
        r_i = lax.broadcasted_iota(jnp.int32, (H, C, C), 1)
        c_i = lax.broadcasted_iota(jnp.int32, (H, C, C), 2)
        triu = (c_i >= r_i).astype(f32)
        rows_sum = lambda x: jnp.sum(x, axis=1, keepdims=True)
        for c in reversed(range(cpb)):
            sl = pl.ds(c * C, C)
            q, v, gx = _heads(q_ref[sl, :]), _heads(i_ref[sl, :]), _heads(g_ref[sl, :])
            ch = _hgrn_chunk(q, _heads(f_ref[sl, :]), lb)
            qd, kd, kl = ch["qd"], ch["kd"], ch["kl"]
            att = jnp.where(ch["tril"], _bdot(qd, kd, BNT), 0.0)
            St = st_ref[:, c]
            o = _bdot(att, v, BNN) + _bdot(qd, St, BNT)
            r = lax.rsqrt(jnp.mean(o * o, axis=-1, keepdims=True) + EPS)
            on = o * r
            sgg = jax.nn.sigmoid(gx)
            gate = gx * sgg
            dyv = _heads(dy_ref[sl, :].astype(f32))
            _put_heads(dg_ref, sl, dyv * on * ngv * sgg * (1.0 + gx * (1.0 - sgg)), bf16)
            dng_acc[...] += rows_sum(dyv * on * gate)
            don = dyv * ngv * gate
            do = r * (don - on * jnp.mean(don * on, axis=-1, keepdims=True))
            dSt = dS[...]
            dA = jnp.where(ch["tril"], _bdot(do, v, BNT), 0.0)
            dv = _bdot(att, do, BTN) + _bdot(kl, dSt, BNT)
            dqd = _bdot(dA, kd, BNN) + _bdot(do, St, BNN)
            dkd = _bdot(dA, qd, BTN)
            dkl = _bdot(v, dSt, BNN)
            dec = jnp.exp(ch["bl"])
            ddec = rows_sum(St * dSt)
            dS[...] = _bdot(do, qd, BTN) + dSt * dec
            dB = dqd * qd - dkd * kd - dkl * kl
            dbl = rows_sum(dkl * kl) + ddec * dec
            dk = dkd * ch["enb"] + dkl * ch["elb"]
            dlogF = _dot(triu, dB, BNN, precision=HI) + dbl
            dF = dlogF / ch["F"] - dk
            sg = ch["sg"]
            _put_heads(dq_ref, sl, dqd * ch["eb"], bf16)
            _put_heads(di_ref, sl, dv, bf16)
            _put_heads(df_ref, sl, dF * (1.0 - lb) * sg * (1.0 - sg), bf16)
            dlb_acc[...] += rows_sum(dF * (1.0 - sg))

        @pl.when(n == nblk - 1)
        def _():
            rows = lax.broadcasted_iota(jnp.int32, (2, LANES), 0)
            for h in range(H):
                hs = pl.ds(h * LANES, LANES)
                lbh = lb_row[:, h * LANES:(h + 1) * LANES]
                dl0 = dlb_acc[h] * lbh * (1.0 - lbh)
                dl_ref[:, hs] = jnp.where(rows == 0, dl0, -dl0)
                dng_ref[:, hs] = dng_acc[h]

    col = lambda off: pl.BlockSpec((rb, H * LANES), lambda n: (nblk - 1 - n, off))
    vec = lambda rows: pl.BlockSpec((rows, H * LANES), lambda n: (0, 0))
    tok = jax.ShapeDtypeStruct((T, H * LANES), bf16)
    return pl.pallas_call(
        body, name="hgrn_bwd", grid=(nblk,),
        in_specs=[col(0), col(1), col(2), col(3), vec(2), vec(1),
                  pl.BlockSpec((H, cpb, LANES, LANES), lambda n: (0, nblk - 1 - n, 0, 0)), col(0)],
        out_specs=[col(0), col(0), col(0), col(0), vec(2), vec(1)],
        out_shape=[tok, tok, tok, tok, jax.ShapeDtypeStruct((2, H * LANES), f32), jax.ShapeDtypeStruct((1, H * LANES), f32)],
        scratch_shapes=[pltpu.VMEM((H, LANES, LANES), f32), pltpu.VMEM((H, 1, LANES), f32), pltpu.VMEM((H, 1, LANES), f32)],
        compiler_params=_cp("arbitrary"),
    )(proj, proj, proj, proj, lbl, ng, states, dy)


def _s5_disc_math(ar, ai, ldt, br, bi):
    dt = jnp.exp(ldt)
    mag = jnp.exp(ar * dt)
    abr, abi = mag * jnp.cos(ai * dt), mag * jnp.sin(ai * dt)
    den = ar * ar + ai * ai
    xr, xi = abr - 1.0, abi
    cr = (xr * ar + xi * ai) / den
    ci = (xi * ar - xr * ai) / den
    return abr, abi, cr * br - ci * bi, cr * bi + ci * br


def _s5_disc_fwd(ar, ai, ldt, br, bi):
    def body(ar_ref, ai_ref, ldt_ref, br_ref, bi_ref, o0, o1, o2, o3):
        outs = _s5_disc_math(ar_ref[...], ai_ref[...], ldt_ref[...], br_ref[...], bi_ref[...])
        for o, v in zip((o0, o1, o2, o3), outs):
            o[...] = v

    return pl.pallas_call(
        body, name="s5_disc_fwd",
        out_shape=[jax.ShapeDtypeStruct(ar.shape, f32)] * 2 + [jax.ShapeDtypeStruct(br.shape, f32)] * 2,
    )(ar, ai, ldt, br, bi)


def _s5_disc_bwd(ar, ai, ldt, br, bi, cts):
    def body(ar_ref, ai_ref, ldt_ref, br_ref, bi_ref, c0, c1, c2, c3, o0, o1, o2, o3, o4):
        _, vjp = jax.vjp(_s5_disc_math, ar_ref[...], ai_ref[...], ldt_ref[...], br_ref[...], bi_ref[...])
        for o, v in zip((o0, o1, o2, o3, o4), vjp((c0[...], c1[...], c2[...], c3[...]))):
            o[...] = v

    return pl.pallas_call(
        body, name="s5_disc_bwd",
        out_shape=[jax.ShapeDtypeStruct(ar.shape, f32)] * 3 + [jax.ShapeDtypeStruct(br.shape, f32)] * 2,
    )(ar, ai, ldt, br, bi, *cts)


S5_LC = 512
S5_NLC = S5_N // S5_LC
S5_UB = 4
S5_UNROLL = 4
S5_TOGETHER = 2


def _cmul(ar, ai, xr, xi):
    return ar * xr - ai * xi, ar * xi + ai * xr


def _cpow(ar, ai, n):
    rr, ri = None, None
    br, bi = ar, ai
    while n:
        if n & 1:
            rr, ri = (br, bi) if rr is None else _cmul(rr, ri, br, bi)
        n >>= 1
        if n:
            br, bi = _cmul(br, bi, br, bi)
    return rr, ri


def _s5_bu(u_ref, bre_ref, bim_ref, xr, xi):
    for k in range(S5_UB):
        uk = u_ref[:, k * LANES:(k + 1) * LANES].astype(bf16)
        xr[:, k * S5_LC:(k + 1) * S5_LC] = _dot(uk, bre_ref[k])
        xi[:, k * S5_LC:(k + 1) * S5_LC] = _dot(uk, bim_ref[k])


def _s5_scan(xr, xi, sr, si, ar_ref, ai_ref, nsteps, store):
    for c0 in range(0, S5_NLC, S5_TOGETHER):
        css = [slice(c * S5_LC, (c + 1) * S5_LC) for c in range(c0, c0 + S5_TOGETHER)]
        a = [(jnp.broadcast_to(ar_ref[:, cs], (S5_SEG, S5_LC)), jnp.broadcast_to(ai_ref[:, cs], (S5_SEG, S5_LC))) for cs in css]

        def step(j, carry, css=css, a=a):
            rows = pl.ds(pl.multiple_of(j * S5_SEG, S5_SEG), S5_SEG)
            out = []
            for u, cs in enumerate(css):
                (a_r, a_i), pr, pi = a[u], carry[2 * u], carry[2 * u + 1]
                nr = a_r * pr - a_i * pi + xr[rows, cs]
                ni = a_r * pi + a_i * pr + xi[rows, cs]
                if store:
                    xr[rows, cs] = nr
                    xi[rows, cs] = ni
                out += [nr, ni]
            return tuple(out)

        init = tuple(v for cs in css for v in (sr[:, cs], si[:, cs]))
        fin = lax.fori_loop(0, nsteps, step, init)
        for u, cs in enumerate(css):
            sr[:, cs] = fin[2 * u]
            si[:, cs] = fin[2 * u + 1]


def _s5_rscan(dr, di, xr, xi, s0r, s0i, gr, gi, acc_r, acc_i, ar_ref, ai_ref, nsteps):
    for c in range(S5_NLC):
        cs = slice(c * S5_LC, (c + 1) * S5_LC)
        a_r = jnp.broadcast_to(ar_ref[:, cs], (S5_SEG, S5_LC))
        a_i = jnp.broadcast_to(ai_ref[:, cs], (S5_SEG, S5_LC))

        def update(j, carry, before, cs=cs, a_r=a_r, a_i=a_i):
            pr, pi, cr, ci = carry
            rows = pl.ds(j * S5_SEG if isinstance(j, int) else pl.multiple_of(j * S5_SEG, S5_SEG), S5_SEG)
            nr = dr[rows, cs] + a_r * pr + a_i * pi
            ni = di[rows, cs] + a_r * pi - a_i * pr
            dr[rows, cs] = nr
            di[rows, cs] = ni
            if before is not None:
                cr = cr + nr * before[0] + ni * before[1]
                ci = ci - nr * before[1] + ni * before[0]
            return nr, ni, cr, ci

        def step(jj, carry, cs=cs, update=update):
            j = nsteps - 1 - jj
            prev = pl.ds(pl.multiple_of((j - 1) * S5_SEG, S5_SEG), S5_SEG)
            return update(j, carry, None if acc_r is None else (xr[prev, cs], xi[prev, cs]))

        z = jnp.zeros((S5_SEG, S5_LC), f32)
        init = (gr[:, cs], gi[:, cs], z, z)
        carry = lax.fori_loop(0, nsteps - 1, step, init, unroll=S5_UNROLL)
        fr, fi, cr, ci = update(0, carry, None if acc_r is None else (s0r[:, cs], s0i[:, cs]))
        gr[:, cs] = fr
        gi[:, cs] = fi
        if acc_r is not None:
            acc_r[:, cs] += cr
            acc_i[:, cs] += ci


def _s5_seg_carry(fr, fi, ar, ai, seg_len, reverse):
    pr, pi = _cpow(ar, ai if not reverse else -ai, seg_len)
    rows = lax.broadcasted_iota(jnp.int32, fr.shape, 0)
    cr, ci = jnp.zeros_like(fr), jnp.zeros_like(fi)
    sh = (S5_SEG - 1) if reverse else 1
    fr_s, fi_s = pltpu.roll(fr, sh, 0), pltpu.roll(fi, sh, 0)
    order = range(S5_SEG - 2, -1, -1) if reverse else range(1, S5_SEG)
    for r in order:
        c_r, c_i = pltpu.roll(cr, sh, 0), pltpu.roll(ci, sh, 0)
        m_r, m_i = _cmul(pr, pi, c_r, c_i)
        cr = jnp.where(rows == r, m_r + fr_s, cr)
        ci = jnp.where(rows == r, m_i + fi_s, ci)
    return cr, ci


def _gelu_parts(y):
    c0 = math.sqrt(2.0 / math.pi)
    t = jnp.tanh(c0 * (y + 0.044715 * y * y * y))
    z = 0.5 * y * (1.0 + t)
    dz = 0.5 * (1.0 + t) + 0.5 * y * (1.0 - t * t) * c0 * (1.0 + 3.0 * 0.044715 * y * y)
    return z, dz


def _s5_y(xr, xi, u_ref, cre_ref, cim_ref, d_ref):
    ys = []
    for k in range(S5_UB):
        cs = slice(k * S5_LC, (k + 1) * S5_LC)
        ys.append(_bdot(xr[:, cs], cre_ref[k]) - _bdot(xi[:, cs], cim_ref[k]))
    return jnp.concatenate(ys, axis=1) + d_ref[...] * u_ref[...]


def _s5_specs(T, rb, rev=False):
    nblk = T // rb
    blk = (lambda i: (nblk - 1 - i, 0)) if rev else (lambda i: (i, 0))
    tok = pl.BlockSpec((rb, 4 * LANES), blk)
    bmat = pl.BlockSpec((S5_UB, LANES, S5_LC), lambda i: (0, 0, 0))
    cmat = pl.BlockSpec((S5_UB, S5_LC, LANES), lambda i: (0, 0, 0))
    avec = pl.BlockSpec((1, S5_N), lambda i: (0, 0))
    seg = pl.BlockSpec((S5_SEG, S5_N), lambda i: (0, 0))
    cvec = pl.BlockSpec((1, 4 * LANES), lambda i: (0, 0))
    s0 = pl.BlockSpec((1, S5_SEG, S5_N), (lambda i: (nblk - 1 - i, 0, 0)) if rev else (lambda i: (i, 0, 0)))
    return dict(tok=tok, bmat=bmat, cmat=cmat, avec=avec, seg=seg, cvec=cvec, s0=s0, nblk=nblk)


def _s5_final(u, bre, bim, ar, ai, *, rb):
    T = u.shape[0]
    sp = _s5_specs(T, rb)

    def body(u_ref, bre_ref, bim_ref, ar_ref, ai_ref, fr_ref, fi_ref, xr, xi):
        @pl.when(pl.program_id(0) == 0)
        def _():
            fr_ref[...] = jnp.zeros_like(fr_ref)
            fi_ref[...] = jnp.zeros_like(fi_ref)

        _s5_bu(u_ref, bre_ref, bim_ref, xr, xi)
        _s5_scan(xr, xi, fr_ref, fi_ref, ar_ref, ai_ref, rb // S5_SEG, False)

    return pl.pallas_call(
        body, name="s5_final", grid=(sp["nblk"],),
        in_specs=[sp["tok"], sp["bmat"], sp["bmat"], sp["avec"], sp["avec"]], out_specs=[sp["seg"], sp["seg"]],
        out_shape=[jax.ShapeDtypeStruct((S5_SEG, S5_N), f32)] * 2,
        scratch_shapes=[pltpu.VMEM((rb, S5_N), f32)] * 2, compiler_params=_cp("arbitrary"),
    )(u, bre, bim, ar, ai)


def _s5_fwd(u, bre, bim, ar, ai, fr, fi, cre, cim, dsk, wg, bg, *, rb):
    T = u.shape[0]
    sp = _s5_specs(T, rb)
    seg_len = T // S5_SEG

    def body(u_ref, bre_ref, bim_ref, ar_ref, ai_ref, fr_ref, fi_ref, cre_ref, cim_ref, d_ref, wg_ref, bg_ref,
             o_ref, s0r_ref, s0i_ref, xr, xi, sr, si):
        @pl.when(pl.program_id(0) == 0)
        def _():
            i_r, i_i = _s5_seg_carry(fr_ref[...], fi_ref[...], ar_ref[...], ai_ref[...], seg_len, False)
            sr[...] = i_r
            si[...] = i_i

        s0r_ref[0] = sr[...]
        s0i_ref[0] = si[...]
        _s5_bu(u_ref, bre_ref, bim_ref, xr, xi)
        _s5_scan(xr, xi, sr, si, ar_ref, ai_ref, rb // S5_SEG, True)
        y = _s5_y(xr, xi, u_ref, cre_ref, cim_ref, d_ref)
        z, _ = _gelu_parts(y)
        v = _bdot(z, wg_ref[...]) + bg_ref[...]
        o_ref[...] = (z * jax.nn.sigmoid(v)).astype(bf16)

    wspec = pl.BlockSpec((4 * LANES, 4 * LANES), lambda i: (0, 0))
    return pl.pallas_call(
        body, name="s5_fwd", grid=(sp["nblk"],),
        in_specs=[sp["tok"], sp["bmat"], sp["bmat"], sp["avec"], sp["avec"], sp["seg"], sp["seg"], sp["cmat"], sp["cmat"],
                  sp["cvec"], wspec, sp["cvec"]],
        out_specs=[sp["tok"], sp["s0"], sp["s0"]],
        out_shape=[jax.ShapeDtypeStruct((T, 4 * LANES), bf16)] + [jax.ShapeDtypeStruct((sp["nblk"], S5_SEG, S5_N), f32)] * 2,
        scratch_shapes=[pltpu.VMEM((rb, S5_N), f32)] * 2 + [pltpu.VMEM((S5_SEG, S5_N), f32)] * 2,
        compiler_params=_cp("arbitrary"),
    )(u, bre, bim, ar, ai, fr, fi, cre, cim, dsk, wg, bg)


def _s5_bwd_a(u, bre, bim, ar, ai, s0r, s0i, cre, cim, cret, cimt, dsk, wg, bg, dout, *, rb):
    T = u.shape[0]
    sp = _s5_specs(T, rb, rev=True)

    def body(u_ref, bre_ref, bim_ref, ar_ref, ai_ref, s0r_ref, s0i_ref, cre_ref, cim_ref, cret_ref, cimt_ref,
             d_ref, wg_ref, bg_ref, do_ref, dy_ref, glr_ref, gli_ref, dcre_ref, dcim_ref, dd_ref, dwg_ref, dbg_ref,
             xr, xi, dr, di, sr, si):
        @pl.when(pl.program_id(0) == 0)
        def _():
            for r in (glr_ref, gli_ref, dcre_ref, dcim_ref, dd_ref, dwg_ref, dbg_ref):
                r[...] = jnp.zeros_like(r)

        sr[...] = s0r_ref[0]
        si[...] = s0i_ref[0]
        _s5_bu(u_ref, bre_ref, bim_ref, xr, xi)
        _s5_scan(xr, xi, sr, si, ar_ref, ai_ref, rb // S5_SEG, True)
        uv = u_ref[...]
        y = _s5_y(xr, xi, u_ref, cre_ref, cim_ref, d_ref)
        z, gz = _gelu_parts(y)
        v = _bdot(z, wg_ref[...]) + bg_ref[...]
        sg = jax.nn.sigmoid(v)
        dov = do_ref[...].astype(f32)
        dv = dov * z * sg * (1.0 - sg)
        dz = dov * sg + _bdot(dv, wg_ref[...], NT)
        dy = dz * gz
        dy_ref[...] = dy
        dwg_ref[...] += _bdot(z, dv, TN)
        dbg_ref[...] += jnp.sum(dv, axis=0, keepdims=True)
        dd_ref[...] += jnp.sum(dy * uv, axis=0, keepdims=True)
        for k in range(S5_UB):
            cs = slice(k * S5_LC, (k + 1) * S5_LC)
            dyk = dy[:, k * LANES:(k + 1) * LANES]
            dcre_ref[k] += _bdot(xr[:, cs], dyk, TN)
            dcim_ref[k] -= _bdot(xi[:, cs], dyk, TN)
            dr[:, cs] = _bdot(dyk, cret_ref[k])
            di[:, cs] = -_bdot(dyk, cimt_ref[k])
        _s5_rscan(dr, di, None, None, None, None, glr_ref, gli_ref, None, None, ar_ref, ai_ref, rb // S5_SEG)

    wspec = pl.BlockSpec((4 * LANES, 4 * LANES), lambda i: (0, 0))
    return pl.pallas_call(
        body, name="s5_bwd_a", grid=(sp["nblk"],),
        in_specs=[sp["tok"], sp["bmat"], sp["bmat"], sp["avec"], sp["avec"], sp["s0"], sp["s0"], sp["cmat"], sp["cmat"],
                  sp["bmat"], sp["bmat"], sp["cvec"], wspec, sp["cvec"], sp["tok"]],
        out_specs=[sp["tok"], sp["seg"], sp["seg"], sp["cmat"], sp["cmat"], sp["cvec"], wspec, sp["cvec"]],
        out_shape=[jax.ShapeDtypeStruct((T, 4 * LANES), f32)] + [jax.ShapeDtypeStruct((S5_SEG, S5_N), f32)] * 2
        + [jax.ShapeDtypeStruct((S5_UB, S5_LC, LANES), f32)] * 2
        + [jax.ShapeDtypeStruct((1, 4 * LANES), f32), jax.ShapeDtypeStruct((4 * LANES, 4 * LANES), f32),
           jax.ShapeDtypeStruct((1, 4 * LANES), f32)],
        scratch_shapes=[pltpu.VMEM((rb, S5_N), f32)] * 4 + [pltpu.VMEM((S5_SEG, S5_N), f32)] * 2,
        compiler_params=_cp("arbitrary"),
    )(u, bre, bim, ar, ai, s0r, s0i, cre, cim, cret, cimt, dsk, wg, bg, dout)


def _s5_bwd_b(u, bre, bim, bret, bimt, ar, ai, s0r, s0i, glr, gli, cret, cimt, dsk, dy, *, rb):
    T = u.shape[0]
    sp = _s5_specs(T, rb, rev=True)
    seg_len = T // S5_SEG
    nblk = sp["nblk"]

    def body(u_ref, bre_ref, bim_ref, bret_ref, bimt_ref, ar_ref, ai_ref, s0r_ref, s0i_ref, glr_ref, gli_ref,
             cret_ref, cimt_ref, d_ref, dy_ref, du_ref, dbre_ref, dbim_ref, dar_ref, dai_ref,
             xr, xi, dr, di, sr, si, gr, gi, acc_r, acc_i):
        @pl.when(pl.program_id(0) == 0)
        def _():
            x_r, x_i = _s5_seg_carry(glr_ref[...], gli_ref[...], ar_ref[...], ai_ref[...], seg_len, True)
            gr[...] = x_r
            gi[...] = x_i
            acc_r[...] = jnp.zeros_like(acc_r)
            acc_i[...] = jnp.zeros_like(acc_i)
            dbre_ref[...] = jnp.zeros_like(dbre_ref)
            dbim_ref[...] = jnp.zeros_like(dbim_ref)

        sr[...] = s0r_ref[0]
        si[...] = s0i_ref[0]
        _s5_bu(u_ref, bre_ref, bim_ref, xr, xi)
        _s5_scan(xr, xi, sr, si, ar_ref, ai_ref, rb // S5_SEG, True)
        dy = dy_ref[...]
        for k in range(S5_UB):
            cs = slice(k * S5_LC, (k + 1) * S5_LC)
            dyk = dy[:, k * LANES:(k + 1) * LANES]
            dr[:, cs] = _bdot(dyk, cret_ref[k])
            di[:, cs] = -_bdot(dyk, cimt_ref[k])
        sr[...] = s0r_ref[0]
        si[...] = s0i_ref[0]
        _s5_rscan(dr, di, xr, xi, sr, si, gr, gi, acc_r, acc_i, ar_ref, ai_ref, rb // S5_SEG)
        dus = []
        for k in range(S5_UB):
            cs = slice(k * S5_LC, (k + 1) * S5_LC)
            uk = u_ref[:, k * LANES:(k + 1) * LANES]
            dbre_ref[k] += _bdot(uk, dr[:, cs], TN)
            dbim_ref[k] += _bdot(uk, di[:, cs], TN)
            dus.append(_bdot(dr[:, cs], bret_ref[k]) + _bdot(di[:, cs], bimt_ref[k]))
        du_ref[...] = (jnp.concatenate(dus, axis=1) + d_ref[...] * dy).astype(bf16)

        @pl.when(pl.program_id(0) == nblk - 1)
        def _():
            dar_ref[...] = jnp.sum(acc_r[...], axis=0, keepdims=True)
            dai_ref[...] = jnp.sum(acc_i[...], axis=0, keepdims=True)

    return pl.pallas_call(
        body, name="s5_bwd_b", grid=(nblk,),
        in_specs=[sp["tok"], sp["bmat"], sp["bmat"], sp["cmat"], sp["cmat"], sp["avec"], sp["avec"], sp["s0"], sp["s0"],
                  sp["seg"], sp["seg"], sp["bmat"], sp["bmat"], sp["cvec"], sp["tok"]],
        out_specs=[sp["tok"], sp["bmat"], sp["bmat"], sp["avec"], sp["avec"]],
        out_shape=[jax.ShapeDtypeStruct((T, 4 * LANES), bf16)] + [jax.ShapeDtypeStruct((S5_UB, LANES, S5_LC), f32)] * 2
        + [jax.ShapeDtypeStruct((1, S5_N), f32)] * 2,
        scratch_shapes=[pltpu.VMEM((rb, S5_N), f32)] * 4 + [pltpu.VMEM((S5_SEG, S5_N), f32)] * 6,
        compiler_params=_cp("arbitrary"),
    )(u, bre, bim, bret, bimt, ar, ai, s0r, s0i, glr, gli, cret, cimt, dsk, dy)


def _blockdiag(w, transpose=False):
    if transpose:
        w = jnp.swapaxes(w, 1, 2)
    g, a, b = w.shape
    eye = jnp.eye(8, dtype=w.dtype)
    return jnp.einsum("kgab,gj->kgajb", w.reshape(4, 8, a, b), eye).reshape(4, 8 * a, 8 * b)


def _blockdiag_t(m, a, b):
    eye = jnp.eye(8, dtype=m.dtype)
    return jnp.einsum("kgajb,gj->kgab", m.reshape(4, 8, a, 8, b), eye).reshape(32, a, b)


ROT = MLA_ROPE // 2


def _rope_tables(positions):
    freqs = ROPE_THETA ** (-jnp.arange(0, MLA_ROPE, 2, dtype=f32) / MLA_ROPE)
    ang = positions.astype(f32)[:, None] * freqs
    cos, sin, z = jnp.cos(ang), jnp.sin(ang), jnp.zeros_like(ang)
    return (jnp.concatenate([cos, cos, z, z], axis=1), jnp.concatenate([-sin, z, z, z], axis=1),
            jnp.concatenate([z, sin, z, z], axis=1))


def _rot(x, c, sa, sb):
    return x * c + pltpu.roll(x, LANES - ROT, 1) * sa + pltpu.roll(x, ROT, 1) * sb


def _rot_t(dy, c, sa, sb):
    return dy * c + pltpu.roll(dy * sa, ROT, 1) + pltpu.roll(dy * sb, LANES - ROT, 1)


def _rms(xv, g):
    return xv * lax.rsqrt(jnp.mean(xv * xv, axis=-1, keepdims=True) + EPS) * g


QW, KVW = MLA_Q_RANK, MLA_KV_RANK
ODD_PAD = QW + KVW + LANES


def _mla_prep_fwd(proj, qg, kvg, tabs, *, tm=512):
    T = proj.shape[0]
    tm = _tile(T, tm)

    def body(p_ref, qg_ref, kvg_ref, c_ref, sa_ref, sb_ref, cq_ref, ckv_ref, kr_ref):
        cq_ref[...] = _rms(p_ref[:, :QW], qg_ref[...]).astype(bf16)
        ckv_ref[...] = _rms(p_ref[:, QW:QW + KVW], kvg_ref[...]).astype(bf16)
        kr_ref[...] = _rot(p_ref[:, QW + KVW:], c_ref[...], sa_ref[...], sb_ref[...]).astype(bf16)

    row = lambda w: pl.BlockSpec((tm, w), lambda i: (i, 0))
    vec = lambda w: pl.BlockSpec((1, w), lambda i: (0, 0))
    return pl.pallas_call(
        body, name="mla_prep_fwd", grid=(T // tm,),
        in_specs=[row(ODD_PAD), vec(QW), vec(KVW), row(LANES), row(LANES), row(LANES)],
        out_specs=[row(QW), row(KVW), row(LANES)],
        out_shape=[jax.ShapeDtypeStruct((T, QW), bf16), jax.ShapeDtypeStruct((T, KVW), bf16),
                   jax.ShapeDtypeStruct((T, LANES), bf16)],
        compiler_params=_cp("parallel"),
    )(proj, qg, kvg, *tabs)


def _mla_prep_bwd(proj, qg, kvg, tabs, dcqn, dckvn, dkr_heads, *, tm=512):
    T = proj.shape[0]
    tm = _tile(T, tm)

    def body(p_ref, qg_ref, kvg_ref, c_ref, sa_ref, sb_ref, dcq_ref, dckv_ref, dkr_ref, dp_ref, dqg_ref, dkvg_ref):
        dcq, dqg = _rms_bwd_math(p_ref[:, :QW], qg_ref[...], dcq_ref[...])
        dckv, dkvg = _rms_bwd_math(p_ref[:, QW:QW + KVW], kvg_ref[...], dckv_ref[...])
        dk = dkr_ref[:, :LANES]
        for h in range(1, MLA_HEADS):
            dk = dk + dkr_ref[:, h * LANES:(h + 1) * LANES]
        dkr = _rot_t(dk, c_ref[...], sa_ref[...], sb_ref[...])
        dp_ref[...] = jnp.concatenate([dcq, dckv, dkr], axis=1).astype(bf16)

        @pl.when(pl.program_id(0) == 0)
        def _():
            dqg_ref[...] = dqg
            dkvg_ref[...] = dkvg

        @pl.when(pl.program_id(0) > 0)
        def _():
            dqg_ref[...] += dqg
            dkvg_ref[...] += dkvg

    row = lambda w: pl.BlockSpec((tm, w), lambda i: (i, 0))
    vec = lambda w: pl.BlockSpec((1, w), lambda i: (0, 0))
    return pl.pallas_call(
        body, name="mla_prep_bwd", grid=(T // tm,),
        in_specs=[row(ODD_PAD), vec(QW), vec(KVW), row(LANES), row(LANES), row(LANES), row(QW), row(KVW),
                  row(MLA_HEADS * LANES)],
        out_specs=[row(ODD_PAD), vec(QW), vec(KVW)],
        out_shape=[jax.ShapeDtypeStruct((T, ODD_PAD), bf16), jax.ShapeDtypeStruct((1, QW), f32),
                   jax.ShapeDtypeStruct((1, KVW), f32)],
        compiler_params=_cp("arbitrary"),
    )(proj, qg, kvg, *tabs, dcqn, dckvn, dkr_heads)


HQ = 2 * LANES
QK_SCALE = MLA_QK ** -0.5


def _causal_mask(i, j, tq, tk):
    r = lax.broadcasted_iota(jnp.int32, (tq, tk), 0) + i * tq
    c = lax.broadcasted_iota(jnp.int32, (tq, tk), 1) + j * tk
    return c <= r


FLASH_PARTS = 4


def _flash_fwd(q, kv, kr, *, tq=1024, tk=1024):
    T = q.shape[0]
    tq = _tile(T, tq)
    tk = _tile(tq, tk)
    per = tq // tk
    H = MLA_HEADS

    def body(q_ref, kn_ref, v_ref, kr_ref, o_ref, lse_ref, m_s, acc):
        i, j = pl.program_id(1), pl.program_id(2)
        last = (i + 1) * per - 1

        @pl.when(j == 0)
        def _():
            m_s[...] = jnp.full_like(m_s, -jnp.inf)
            acc[...] = jnp.zeros_like(acc)

        def step(masked):
            k = jnp.concatenate([kn_ref[...], kr_ref[...]], axis=1)
            v1 = jnp.concatenate([v_ref[...], jnp.ones((tk, LANES), bf16)], axis=1)
            mask = _causal_mask(i, j, tq, tk) if masked else None
            for part in range(FLASH_PARTS):
                rows = pl.ds(part * (tq // FLASH_PARTS), tq // FLASH_PARTS)
                s = _dot(q_ref[rows, :], k, NT)
                if masked:
                    s = jnp.where(mask[part * (tq // FLASH_PARTS):(part + 1) * (tq // FLASH_PARTS)], s, -jnp.inf)
                m_new = jnp.maximum(m_s[rows, :], jnp.max(s, axis=-1, keepdims=True))
                alpha = jnp.exp(m_s[rows, :] - m_new)
                p = jnp.exp((s - m_new).astype(bf16))
                acc[rows, :] = alpha * acc[rows, :] + _dot(p, v1)
                m_s[rows, :] = m_new

        pl.when(j < i * per)(functools.partial(step, False))
        pl.when((j >= i * per) & (j <= last))(functools.partial(step, True))

        @pl.when(j == last)
        def _():
            l = acc[:, LANES:]
            o_ref[...] = (acc[:, :LANES] / l).astype(bf16)
            lse_ref[0] = m_s[...] + jnp.log(jnp.max(l, axis=-1, keepdims=True))

    kj = lambda i, j: jnp.minimum(j, (i + 1) * per - 1)
    kblk = lambda off: pl.BlockSpec((tk, LANES), lambda h, i, j: (kj(i, j), 2 * h + off))
    return pl.pallas_call(
        body, name="flash_fwd", grid=(H, T // tq, T // tk),
        in_specs=[pl.BlockSpec((tq, HQ), lambda h, i, j: (i, h)), kblk(0), kblk(1),
                  pl.BlockSpec((tk, LANES), lambda h, i, j: (kj(i, j), 0))],
        out_specs=[pl.BlockSpec((tq, LANES), lambda h, i, j: (i, h)), pl.BlockSpec((1, tq, 1), lambda h, i, j: (h, i, 0))],
        out_shape=[jax.ShapeDtypeStruct((T, H * LANES), bf16), jax.ShapeDtypeStruct((H, T, 1), f32)],
        scratch_shapes=[pltpu.VMEM((tq, 1), f32), pltpu.VMEM((tq, 2 * LANES), f32)],
        compiler_params=_cp("parallel", "parallel", "arbitrary"),
    )(q, kv, kv, kr)


def _flash_bwd(q, kv, kr, o, do, lse, tabs, *, tb=1024):
    T = q.shape[0]
    tb = _tile(T, tb)
    nb = T // tb
    H = MLA_HEADS

    def body(q_ref, kn_ref, v_ref, kr_ref, o_ref, do_ref, lse_ref, c_ref, sa_ref, sb_ref,
             dkv_ref, dkr_ref, dq_ref, dk_acc, dv_acc, dq_acc):
        j, ii = pl.program_id(1), pl.program_id(2)
        i = jnp.maximum(ii, j)

        @pl.when((j == 0) & (ii == 0))
        def _():
            dq_acc[...] = jnp.zeros_like(dq_acc)

        @pl.when(ii == 0)
        def _():
            dk_acc[...] = jnp.zeros_like(dk_acc)
            dv_acc[...] = jnp.zeros_like(dv_acc)

        def step(masked):
            k = jnp.concatenate([kn_ref[...], kr_ref[...]], axis=1)
            p = jnp.exp((_dot(q_ref[...], k, NT) - lse_ref[0]).astype(bf16))
            if masked:
                p = jnp.where(_causal_mask(i, j, tb, tb), p, jnp.zeros_like(p))
            delta = jnp.sum(o_ref[...].astype(f32) * do_ref[...], axis=-1, keepdims=True)
            ds = p * (_bdot(do_ref[...], v_ref[...], NT) - delta).astype(bf16)
            dv_acc[...] += _bdot(p, do_ref[...], TN)
            dk_acc[...] += _bdot(ds, q_ref[...], TN)
            dq_acc[pl.ds(pl.multiple_of(i * tb, tb), tb), :] += _bdot(ds, k)

        pl.when(ii > j)(functools.partial(step, False))
        pl.when(ii == j)(functools.partial(step, True))

        @pl.when(ii == nb - 1)
        def _():
            dkv_ref[...] = jnp.concatenate([dk_acc[:, :LANES], dv_acc[...]], axis=1).astype(bf16)
            dkr_ref[...] = dk_acc[:, LANES:]

        @pl.when((j == nb - 1) & (ii == nb - 1))
        def _():
            dq_ref[:, :LANES] = (dq_acc[:, :LANES] * QK_SCALE).astype(bf16)
            dq_ref[:, LANES:] = (_rot_t(dq_acc[:, LANES:], c_ref[...], sa_ref[...], sb_ref[...]) * QK_SCALE).astype(bf16)

    qi = lambda h, j, i: jnp.maximum(i, j)
    kblk = lambda off: pl.BlockSpec((tb, LANES), lambda h, j, i: (j, 2 * h + off))
    vec = pl.BlockSpec((1, tb, 1), lambda h, j, i: (h, qi(h, j, i), 0))
    qblk = pl.BlockSpec((tb, LANES), lambda h, j, i: (qi(h, j, i), h))
    tab = pl.BlockSpec((T, LANES), lambda h, j, i: (0, 0), pipeline_mode=pl.Buffered(1))
    return pl.pallas_call(
        body, name="flash_bwd", grid=(H, nb, nb),
        in_specs=[pl.BlockSpec((tb, HQ), lambda h, j, i: (qi(h, j, i), h)), kblk(0), kblk(1),
                  pl.BlockSpec((tb, LANES), lambda h, j, i: (j, 0)), qblk, qblk, vec, tab, tab, tab],
        out_specs=[pl.BlockSpec((tb, HQ), lambda h, j, i: (j, h)), pl.BlockSpec((tb, LANES), lambda h, j, i: (j, h)),
                   pl.BlockSpec((T, HQ), lambda h, j, i: (0, h))],
        out_shape=[jax.ShapeDtypeStruct((T, H * HQ), bf16), jax.ShapeDtypeStruct((T, H * LANES), f32),
                   jax.ShapeDtypeStruct((T, H * HQ), bf16)],
        scratch_shapes=[pltpu.VMEM((tb, HQ), f32), pltpu.VMEM((tb, LANES), f32), pltpu.VMEM((T, HQ), f32)],
        compiler_params=_cp("parallel", "arbitrary", "arbitrary"),
    )(q, kv, kv, kr, o, do, lse, *tabs)


HBM_SPEC = pl.BlockSpec(memory_space=pltpu.HBM)
N_CHIPS = 4
N_DEV = 8

BIG = {"even_w_in": 1, "s5_w_glu": 0, "even_w_out": 0, "odd_w_in": 0, "mla_w_uq": 1, "mla_w_ukv": 1, "odd_w_out": 0,
       "ffn_w_in": 2, "ffn_w_out": 1}
LAYERED = ("ffn_w_in", "ffn_w_out")
GROUPS = {"even_in": ("even_w_in",), "even_rest": ("s5_w_glu", "even_w_out"), "ffn0": LAYERED,
          "odd": ("odd_w_in", "mla_w_uq", "mla_w_ukv", "odd_w_out"), "ffn1": LAYERED}
GROUP_LAYER = {"ffn0": 0, "ffn1": 1}


def _place():
    x, y, c = lax.axis_index("x"), lax.axis_index("y"), lax.axis_index("c")
    chips = [(1 - x, y), (x, 1 - y), (1 - x, 1 - y)]
    return x, y, c, chips


def _slab(ref, axis, k, size):
    start = pl.multiple_of(k * size, size if axis == 0 else LANES)
    idx = [slice(None)] * len(ref.shape)
    idx[axis] = pl.ds(start, size)
    return ref.at[tuple(idx)]


SEM_SPEC = pl.BlockSpec(memory_space=pltpu.SEMAPHORE)
ANY_SPEC = pl.BlockSpec(memory_space=pl.ANY)
EFFECT = pltpu.SideEffectType.DATAFLOW_SIDE_EFFECTING


def _hbm(a):
    return pltpu.with_memory_space_constraint(a, pltpu.HBM)


class _Gather:
    copies = 3

    def __init__(self, axis, size):
        self.axis, self.size = axis, size

    def view(self, land, kk):
        return _slab(land, self.axis, kk, self.size)

    def own(self, land, place):
        return self.view(land, 2 * place[0] + place[1])

    def sends(self, src, land, place):
        x, y, c, chips = place
        return [(self.own(land, place) if src is None else src, self.own(land, place), (*chip, c)) for chip in chips]

    def recvs(self, land, place):
        return [self.view(land, 2 * cx + cy) for cx, cy in place[3]]


class _Scatter:
    copies = 3

    def __init__(self, axis, size, layer=None):
        self.axis, self.size, self.layer = axis, size, layer

    def row(self, land, j):
        return land.at[j] if self.layer is None else land.at[j, self.layer]

    def sends(self, src, land, place):
        c, chips = place[2], place[3]
        return [(_slab(src, self.axis, 2 * cx + cy, self.size), self.row(land, j), (cx, cy, c))
                for j, (cx, cy) in enumerate(chips)]

    def recvs(self, land, place):
        return [self.row(land, j) for j in range(3)]


class _Sibling:
    copies = 1

    def sends(self, src, land, place):
        x, y, c, _ = place
        return [(src, land, (x, y, 1 - c))]

    def recvs(self, land, place):
        return [land]


class _ToAll:
    copies = N_DEV - 1

    def __init__(self, size):
        self.size = size

    def sends(self, src, land, place):
        x, y, c, _ = place
        flip = lambda v, bit: 1 - v if bit else v
        own = _slab(land, 0, 4 * x + 2 * y + c, self.size)
        return [(own, own, (flip(x, m & 4), flip(y, m & 2), flip(c, m & 1))) for m in range(1, N_DEV)]

    def recvs(self, land, place):
        x, y, c, _ = place
        d = 4 * x + 2 * y + c
        return [_slab(land, 0, d ^ m, self.size) for m in range(1, N_DEV)]


def _unique(arrays):
    out, index = [], {}
    for a in arrays:
        if a is not None and id(a) not in index:
            index[id(a)] = len(out)
            out.append(a)
    return out, index


def _sem_base(routes):
    base = [0]
    for r in routes:
        base.append(base[-1] + r.copies)
    return base


def _push_start(name, items):
    n = len(items)
    base = _sem_base([it[0] for it in items])
    arrays, index = _unique([it[1] for it in items] + [it[2] for it in items])
    na = len(arrays)

    def body(*refs):
        arr, send, recv, token = refs[:na], refs[na], refs[na + 1], refs[-1]
        place = _place()
        for i, (route, src, land) in enumerate(items):
            s_ref = None if src is None else arr[index[id(src)]]
            for j, (s, d, dev) in enumerate(route.sends(s_ref, arr[index[id(land)]], place)):
                pltpu.make_async_remote_copy(src_ref=s, dst_ref=d, send_sem=send.at[base[i] + j], recv_sem=recv.at[base[i] + j],
                                             device_id=dev, device_id_type=MESH).start()
        token[...] = jnp.zeros_like(token)

    res = pl.pallas_call(
        body, name=name,
        out_shape=[pltpu.SemaphoreType.DMA((base[-1],)), pltpu.SemaphoreType.DMA((base[-1],))]
        + [pltpu.HBM(a.shape, a.dtype) for a in arrays] + [jax.ShapeDtypeStruct((SUBLANES, LANES), f32)],
        in_specs=[HBM_SPEC] * na, out_specs=[SEM_SPEC, SEM_SPEC] + [HBM_SPEC] * na + [pl.BlockSpec(memory_space=pltpu.VMEM)],
        input_output_aliases={i: 2 + i for i in range(na)},
        compiler_params=pltpu.CompilerParams(has_side_effects=EFFECT),
    )(*[_hbm(a) for a in arrays])
    thru = lambda a: None if a is None else res[2 + index[id(a)]]
    return (res[0], res[1]), [thru(it[1]) for it in items], [thru(it[2]) for it in items], res[-1]


def _push_wait(name, groups, after, with_srcs=False):
    arrays, index = _unique([a for _, _, srcs, lands in groups for a in list(srcs) + list(lands)])
    na, ng = len(arrays), len(groups)

    def body(*refs):
        arr, sems = refs[:na], refs[na:na + 2 * ng]
        place = _place()
        for g, (routes, _, srcs, lands) in enumerate(groups):
            send, recv = sems[2 * g], sems[2 * g + 1]
            base = _sem_base(routes)
            for i, route in enumerate(routes):
                src, land = None if srcs[i] is None else arr[index[id(srcs[i])]], arr[index[id(lands[i])]]
                for j, ((s, d, dev), mine) in enumerate(zip(route.sends(src, land, place), route.recvs(land, place))):
                    cp = pltpu.make_async_remote_copy(src_ref=s, dst_ref=mine, send_sem=send.at[base[i] + j],
                                                      recv_sem=recv.at[base[i] + j], device_id=dev,
                                                      device_id_type=MESH)
                    cp.wait_send()
                    cp.wait_recv()

    sem_args = [s for g in groups for s in g[1]]
    res = pl.pallas_call(
        body, name=name, out_shape=[pltpu.HBM(a.shape, a.dtype) for a in arrays],
        in_specs=[HBM_SPEC] * na + [SEM_SPEC] * (2 * ng) + [ANY_SPEC] * len(after), out_specs=[HBM_SPEC] * na,
        input_output_aliases={i: i for i in range(na)},
        compiler_params=pltpu.CompilerParams(has_side_effects=EFFECT),
    )(*arrays, *sem_args, *after)
    if with_srcs:
        return [([res[index[id(a)]] for a in g[2]], [res[index[id(a)]] for a in g[3]]) for g in groups]
    return [[res[index[id(a)]] for a in g[3]] for g in groups]


def _place_slab(block, axis, slabs, idx, dtype, *, name):
    R, C = block.shape
    tm = _rows(R, C)
    nr = R // tm
    out_map = (lambda i, k: (i, k[0])) if axis == 1 else (lambda i, k: (k[0] * nr + i, 0))

    def body(k_ref, x_ref, o_ref):
        o_ref[...] = x_ref[...].astype(dtype)

    full = (R, C * slabs) if axis == 1 else (R * slabs, C)
    return pl.pallas_call(
        body, name=name, out_shape=jax.ShapeDtypeStruct(full, dtype),
        grid_spec=pltpu.PrefetchScalarGridSpec(
            num_scalar_prefetch=1, grid=(nr,), in_specs=[pl.BlockSpec((tm, C), lambda i, k: (i, 0))],
            out_specs=pl.BlockSpec((tm, C), out_map)),
        compiler_params=_cp("parallel"),
    )(idx, block)


ELEMENTWISE_BLOCK_BYTES = 1 << 20


def _rows(r, c):
    for t in (512, 256, 128, 64, 32, 16, 8):
        if r % t == 0 and t * c * 4 <= ELEMENTWISE_BLOCK_BYTES:
            return t
    return r


def _sum4(owns, axis, recv, kidx, *, name, dep=None):
    L = len(owns)
    R, C = recv.shape[2:]
    tm = _rows(R, C)
    nr = R // tm
    deps = [] if dep is None else [dep]

    def body(k_ref, *refs):
        own_refs, r_ref, out_ref = refs[:L], refs[L], refs[-1]
        for li in range(L):
            @pl.when(pl.program_id(0) == li)
            def _(o_ref=own_refs[li]):
                out_ref[...] = ((o_ref[...] + r_ref[0, 0].astype(f32)) + r_ref[1, 0].astype(f32)) + r_ref[2, 0].astype(f32)

    own_map = (lambda l, i, k: (i, k[0])) if axis == 1 else (lambda l, i, k: (k[0] * nr + i, 0))
    return pl.pallas_call(
        body, name=name, out_shape=jax.ShapeDtypeStruct((L * R, C), f32),
        grid_spec=pltpu.PrefetchScalarGridSpec(
            num_scalar_prefetch=1, grid=(L, nr),
            in_specs=[pl.BlockSpec((tm, C), own_map)] * L + [pl.BlockSpec((3, 1, tm, C), lambda l, i, k: (0, l, i, 0))]
            + [pl.BlockSpec(memory_space=pl.ANY)] * len(deps),
            out_specs=pl.BlockSpec((tm, C), lambda l, i, k: (l * nr + i, 0))),
        compiler_params=_cp("parallel", "parallel"),
    )(kidx, *owns, recv, *deps)


def _adamw(w, m, v, parts, *, name):
    R, C = w.shape
    tm = _rows(R, C)
    npart = len(parts)

    def body(*refs):
        w_ref, m_ref, v_ref = refs[:3]
        g_ref, d_ref, m2_ref, v2_ref = refs[3 + npart:]
        g = refs[3][...]
        for p_ref in refs[4:3 + npart]:
            g = g + p_ref[...]
        g_ref[...] = g
        d_ref[...], m2_ref[...], v2_ref[...] = _adam_math(w_ref[...], m_ref[...], v_ref[...], g)

    blk = pl.BlockSpec((tm, C), lambda i: (i, 0))
    return pl.pallas_call(
        body, name=name, grid=(R // tm,),
        in_specs=[blk] * (3 + npart), out_specs=[blk] * 4,
        out_shape=[jax.ShapeDtypeStruct((R, C), f32)] * 4, compiler_params=_cp("parallel"),
    )(w, m, v, *parts)


def _adam_math(w, m, v, g):
    m2 = ADAM_B1 * m + (1.0 - ADAM_B1) * g
    v2 = ADAM_B2 * v + (1.0 - ADAM_B2) * (g * g)
    m_hat = m2 / (1.0 - ADAM_B1 ** ADAM_STEP)
    v_hat = v2 / (1.0 - ADAM_B2 ** ADAM_STEP)
    return -ADAM_LR * (m_hat / (jnp.sqrt(v_hat) + ADAM_EPS) + ADAM_WD * w), m2, v2


def _adamw_small(landed, w, m, v, kidx, ra, rb):
    rs = ra + N_CHIPS * rb

    def body(k_ref, l_ref, w_ref, m_ref, v_ref, g_ref, d_ref, m2_ref, v2_ref):
        mine = pl.multiple_of(ra + k_ref[0] * rb, SUBLANES)
        for lo, n, off in ((0, ra, 0), (ra, rb, mine)):
            g = l_ref[pl.ds(off, n), :]
            for d in range(1, N_DEV):
                g = g + l_ref[pl.ds(d * rs + off, n), :]
            rows = pl.ds(lo, n)
            delta, m2, v2 = _adam_math(w_ref[rows, :], m_ref[rows, :], v_ref[rows, :], g)
            g_ref[rows, :] = g
            d_ref[rows, :] = delta
            m2_ref[rows, :] = m2
            v2_ref[rows, :] = v2

    vmem = pl.BlockSpec(memory_space=pltpu.VMEM)
    return pl.pallas_call(
        body, name="adamw_small", out_shape=[jax.ShapeDtypeStruct(w.shape, f32)] * 4,
        grid_spec=pltpu.PrefetchScalarGridSpec(num_scalar_prefetch=1, grid=(), in_specs=[vmem] * 4, out_specs=[vmem] * 4),
        compiler_params=_cp(),
    )(kidx, landed, w, m, v)


def _pad_odd(w):
    return jnp.pad(w, ((0, 0), (0, ODD_PAD - w.shape[1])))


def _uq_cat(w):
    r = w.shape[0]
    return jnp.pad(w.reshape(r, MLA_HEADS, MLA_QK), ((0, 0), (0, 0), (0, HQ - MLA_QK))).reshape(r, MLA_HEADS * HQ)


def _uq_uncat(w):
    r = w.shape[0]
    return w.reshape(r, MLA_HEADS, HQ)[:, :, :MLA_QK].reshape(r, MLA_HEADS * MLA_QK)


def _to_segments(v):
    T, C = v.shape
    return v.reshape(S5_SEG, T // S5_SEG, C).transpose(1, 0, 2).reshape(T, C)


def _from_segments(v):
    T, C = v.shape
    return v.reshape(T // S5_SEG, S5_SEG, C).transpose(1, 0, 2).reshape(T, C)


def _s5_rb(T):
    return min(512, T)


def _ffn_fwd(h, hn, w_in, cw, cb, w_out, tag, next_g=None, loss=None):
    au = _mm(hn, w_in, out_dtype=bf16, name=f"ffn{tag}_in", tn=1408)
    z = _ffn_mid_fwd(au, cw, cb, name=f"ffn{tag}_mid")
    return _mm(z, w_out, res=h, norm_g=next_g, loss=loss, name=f"ffn{tag}_out", tm=512, tk=D_FF), (hn, au, z)


def _ffn_bwd(h, g, w_in, cw, cb, w_out, saved, dh, tag, dep=None):
    hn, au, z = saved
    dz = _mm(dh, w_out, tb=True, out_dtype=bf16, name=f"ffn{tag}_dz", tn=1408, dep=dep)
    dw_out = _mm(z, dh, ta=True, also_bf16=True, name=f"ffn{tag}_dwout", tm=1408)
    dau, dcw, dcb = _ffn_mid_bwd(au, cw, cb, dz, name=f"ffn{tag}_dmid")
    dh_in, dg = _mm(dau, w_in, tb=True, res=dh, norm_bwd=(h, g), name=f"ffn{tag}_dhn", tk=1408)
    dw_in = _mm(hn, dau, ta=True, also_bf16=True, name=f"ffn{tag}_dwin", tn=1408)
    return dh_in, dg, dw_in, dcw, dcb, dw_out


def _local_step(x, positions, target, get_w, P, put_g):
    T = x.shape[0]
    rb = _s5_rb(T)
    row = lambda v: v.reshape(1, -1)
    g_mix, g_ffn = P["norm_mix_g"], P["norm_ffn_g"]
    lbl, hng = P["hgrn_lb_logits"], P["hgrn_norm_g"]
    dsk, bg = P["s5_d"], P["s5_b_glu"]
    qg, kvg = P["mla_q_norm_g"], P["mla_kv_norm_g"]
    cw, cb = P["ffn_conv_w"], P["ffn_conv_b"]

    col = lambda v: v.reshape(S5_N, 1)
    disc_in = (col(P["s5_a_re"]), col(P["s5_a_im"]), col(jnp.repeat(P["s5_log_dt"].reshape(S5_GROUPS), S5_STATE)),
               P["s5_b_re"].reshape(S5_N, S5_GROUP), P["s5_b_im"].reshape(S5_N, S5_GROUP))
    abr, abi, bbr, bbi = _s5_disc_fwd(*disc_in)
    ar, ai = abr.reshape(1, S5_N), abi.reshape(1, S5_N)
    bbr3, bbi3 = bbr.reshape(S5_GROUPS, S5_STATE, S5_GROUP), bbi.reshape(S5_GROUPS, S5_STATE, S5_GROUP)
    bre, bim = _blockdiag(bbr3, True).astype(bf16), _blockdiag(bbi3, True).astype(bf16)
    bret, bimt = _blockdiag(bbr3).astype(bf16), _blockdiag(bbi3).astype(bf16)
    c_re, c_im = P["s5_c_re"].reshape(S5_GROUPS, S5_GROUP, S5_STATE), P["s5_c_im"].reshape(S5_GROUPS, S5_GROUP, S5_STATE)
    cre, cim = _blockdiag(c_re, True).astype(bf16), _blockdiag(c_im, True).astype(bf16)
    cret, cimt = _blockdiag(c_re).astype(bf16), _blockdiag(c_im).astype(bf16)

    hn0 = _rms_fwd(x, g_mix[0:1], name="mix0_norm")
    We = get_w("even_in", hn0)
    proj_e = _mm(hn0, We["even_w_in"], name="even_in", tn=1280)
    Wr = get_w("even_rest", proj_e)
    ya, states = _hgrn_fwd(proj_e, lbl, hng)
    u_seg = _to_segments(proj_e[:, 4 * 512:])
    fr, fi = _s5_final(u_seg, bre, bim, ar, ai, rb=rb)
    yb_seg, s0r, s0i = _s5_fwd(u_seg, bre, bim, ar, ai, fr, fi, cre, cim, dsk, Wr["s5_w_glu"], bg, rb=rb)
    ycat = jnp.concatenate([ya, _from_segments(yb_seg)], axis=1)
    h1, hnf0 = _mm(ycat, Wr["even_w_out"], res=x, norm_g=g_ffn[0:1], name="even_out")
    Wf0 = get_w("ffn0", h1)
    (h2, hn2), ffn0 = _ffn_fwd(h1, hnf0, Wf0["ffn_w_in"], cw[0], cb[0:1], Wf0["ffn_w_out"], 0, next_g=g_mix[1:2])

    tabs = _rope_tables(positions)
    Wo = get_w("odd", hn2)
    proj_o = _mm(hn2, Wo["odd_w_in"], name="odd_in")
    cqn, ckvn, kr = _mla_prep_fwd(proj_o, qg, kvg, tabs)
    q = _mm(cqn, Wo["mla_w_uq"], out_dtype=bf16, rope=tabs, name="mla_uq")
    kvb = _mm(ckvn, Wo["mla_w_ukv"], out_dtype=bf16, name="mla_ukv")
    o, lse = _flash_fwd(q, kvb, kr)
    h3, hnf1 = _mm(o, Wo["odd_w_out"], res=h2, norm_g=g_ffn[1:2], name="odd_out")
    Wf1 = get_w("ffn1", h3)
    (dh4, loss, dg_final), ffn1 = _ffn_fwd(h3, hnf1, Wf1["ffn_w_in"], cw[1], cb[1:2], Wf1["ffn_w_out"], 1,
                                           loss=(row(P["final_norm_g"]), target))

    dh3, dg_ffn1, dw_fin1, dcw1, dcb1, dw_fout1 = _ffn_bwd(
        h3, g_ffn[1:2], Wf1["ffn_w_in"], cw[1], cb[1:2], Wf1["ffn_w_out"], ffn1, dh4, 1)
    sent = put_g("ffn1", {"ffn_w_in": dw_fin1, "ffn_w_out": dw_fout1})
    do = _mm(dh3, Wo["odd_w_out"], tb=True, out_dtype=bf16, name="odd_do", dep=sent)
    dw_oout = _mm(o, dh3, ta=True, also_bf16=True, name="odd_dwout")
    dkv, dkr_h, dq = _flash_bwd(q, kvb, kr, o, do, lse, tabs)
    dw_uq = _mm(cqn, dq, ta=True, also_bf16=True, name="mla_dwuq")
    dcqn = _mm(dq, Wo["mla_w_uq"], tb=True, name="mla_dcq", tk=MLA_HEADS * HQ)
    dw_ukv = _mm(ckvn, dkv, ta=True, also_bf16=True, name="mla_dwukv")
    dckvn = _mm(dkv, Wo["mla_w_ukv"], tb=True, name="mla_dckv")
    dproj_o, dqg, dkvg = _mla_prep_bwd(proj_o, qg, kvg, tabs, dcqn, dckvn, dkr_h)
    dw_oin = _mm(hn2, dproj_o, ta=True, also_bf16=True, name="odd_dwin")
    sent = put_g("odd", {"odd_w_in": dw_oin, "mla_w_uq": dw_uq, "mla_w_ukv": dw_ukv, "odd_w_out": dw_oout})
    dh2, dg_mix1 = _mm(dproj_o, Wo["odd_w_in"], tb=True, res=dh3, norm_bwd=(h2, g_mix[1:2]), name="odd_dhn")

    dh1, dg_ffn0, dw_fin0, dcw0, dcb0, dw_fout0 = _ffn_bwd(
        h1, g_ffn[0:1], Wf0["ffn_w_in"], cw[0], cb[0:1], Wf0["ffn_w_out"], ffn0, dh2, 0, dep=sent)
    sent = put_g("ffn0", {"ffn_w_in": dw_fin0, "ffn_w_out": dw_fout0})
    dycat = _mm(dh1, Wr["even_w_out"], tb=True, out_dtype=bf16, name="even_dy", dep=sent)
    dw_eout = _mm(ycat, dh1, ta=True, also_bf16=True, name="even_dwout")
    dq_h, df_h, di_h, dg_h, dlbl, dhng = _hgrn_bwd(proj_e, lbl, hng, states, dycat)
    dyb_seg = _to_segments(dycat[:, 512:])
    dy_s5, glr, gli, dcre, dcim, dd, dwg, dbg = _s5_bwd_a(
        u_seg, bre, bim, ar, ai, s0r, s0i, cre, cim, cret, cimt, dsk, Wr["s5_w_glu"], bg, dyb_seg, rb=rb)
    du_seg, dbre, dbim, dar, dai = _s5_bwd_b(
        u_seg, bre, bim, bret, bimt, ar, ai, s0r, s0i, glr, gli, cret, cimt, dsk, dy_s5, rb=rb)
    dproj_e = jnp.concatenate([dq_h, df_h, di_h, dg_h, _from_segments(du_seg)], axis=1)
    dx, dg_mix0 = _mm(dproj_e, We["even_w_in"], tb=True, res=dh1, norm_bwd=(x, g_mix[0:1]), name="even_dhn", tk=1280)
    dw_ein = _mm(hn0, dproj_e, ta=True, also_bf16=True, name="even_dwin", tn=1280)

    unblk = lambda m, a, b: jnp.swapaxes(_blockdiag_t(m, a, b), 1, 2)
    dbbr = unblk(dbre, S5_GROUP, S5_STATE).reshape(S5_N, S5_GROUP)
    dbbi = unblk(dbim, S5_GROUP, S5_STATE).reshape(S5_N, S5_GROUP)
    d_ar, d_ai, d_ldt, d_br, d_bi = _s5_disc_bwd(*disc_in, (dar.reshape(S5_N, 1), dai.reshape(S5_N, 1), dbbr, dbbi))
    small = {
        "norm_mix_g": jnp.concatenate([dg_mix0, dg_mix1], axis=0),
        "norm_ffn_g": jnp.concatenate([dg_ffn0, dg_ffn1], axis=0),
        "final_norm_g": dg_final.reshape(-1),
        "hgrn_lb_logits": dlbl, "hgrn_norm_g": dhng,
        "s5_a_re": d_ar.reshape(1, S5_GROUPS, S5_STATE), "s5_a_im": d_ai.reshape(1, S5_GROUPS, S5_STATE),
        "s5_log_dt": d_ldt.reshape(S5_GROUPS, S5_STATE).sum(axis=1).reshape(1, S5_GROUPS),
        "s5_b_re": d_br.reshape(1, S5_GROUPS, S5_STATE, S5_GROUP), "s5_b_im": d_bi.reshape(1, S5_GROUPS, S5_STATE, S5_GROUP),
        "s5_c_re": unblk(dcre, S5_STATE, S5_GROUP).reshape(1, S5_GROUPS, S5_GROUP, S5_STATE),
        "s5_c_im": unblk(dcim, S5_STATE, S5_GROUP).reshape(1, S5_GROUPS, S5_GROUP, S5_STATE),
        "s5_d": dd, "s5_b_glu": dbg, "mla_q_norm_g": dqg, "mla_kv_norm_g": dkvg,
        "ffn_conv_w": jnp.stack([dcw0, dcw1]), "ffn_conv_b": jnp.concatenate([dcb0, dcb1], axis=0),
    }
    put_g("even", {"even_w_in": dw_ein, "s5_w_glu": (dwg, dwg.astype(bf16)), "even_w_out": dw_eout}, small)
    return loss, dx


WEIGHTS = ["norm_mix_g", "norm_ffn_g", "final_norm_g", "even_w_in", "hgrn_lb_logits", "hgrn_norm_g", "s5_a_re", "s5_a_im",
           "s5_log_dt", "s5_b_re", "s5_b_im", "s5_c_re", "s5_c_im", "s5_d", "s5_w_glu", "s5_b_glu", "even_w_out", "odd_w_in",
           "mla_q_norm_g", "mla_w_uq", "mla_kv_norm_g", "mla_w_ukv", "odd_w_out", "ffn_w_in", "ffn_conv_w", "ffn_conv_b",
           "ffn_w_out"]
SMALL_SHARDED = {"mla_q_norm_g": 1, "mla_kv_norm_g": 1, "ffn_conv_w": 2}
SMALL = [n for n in WEIGHTS if n not in BIG]
SMALL_REP = [n for n in SMALL if n not in SMALL_SHARDED]


def _pack_rows(shapes):
    n = sum(math.prod(s) for s in shapes)
    return -(-n // (SUBLANES * LANES)) * SUBLANES


def _pack(arrays, rows):
    flat = jnp.concatenate([a.reshape(-1) for a in arrays])
    return jnp.pad(flat, (0, rows * LANES - flat.shape[0])).reshape(rows, LANES)


def _unpack(block, shapes):
    flat, out, off = block.reshape(-1), [], 0
    for s in shapes:
        n = math.prod(s)
        out.append(flat[off:off + n].reshape(s))
        off += n
    return out


def kernel(x, positions, norm_mix_g, norm_ffn_g, final_norm_g, even_w_in, hgrn_lb_logits, hgrn_norm_g, s5_a_re, s5_a_im, s5_log_dt, s5_b_re, s5_b_im, s5_c_re, s5_c_im, s5_d, s5_w_glu, s5_b_glu, even_w_out, odd_w_in, mla_q_norm_g, mla_w_uq, mla_kv_norm_g, mla_w_ukv, odd_w_out, ffn_w_in, ffn_conv_w, ffn_conv_b, ffn_w_out, loss_target, m_norm_mix_g, m_norm_ffn_g, m_final_norm_g, m_even_w_in, m_hgrn_lb_logits, m_hgrn_norm_g, m_s5_a_re, m_s5_a_im, m_s5_log_dt, m_s5_b_re, m_s5_b_im, m_s5_c_re, m_s5_c_im, m_s5_d, m_s5_w_glu, m_s5_b_glu, m_even_w_out, m_odd_w_in, m_mla_q_norm_g, m_mla_w_uq, m_mla_kv_norm_g, m_mla_w_ukv, m_odd_w_out, m_ffn_w_in, m_ffn_conv_w, m_ffn_conv_b, m_ffn_w_out, v_norm_mix_g, v_norm_ffn_g, v_final_norm_g, v_even_w_in, v_hgrn_lb_logits, v_hgrn_norm_g, v_s5_a_re, v_s5_a_im, v_s5_log_dt, v_s5_b_re, v_s5_b_im, v_s5_c_re, v_s5_c_im, v_s5_d, v_s5_w_glu, v_s5_b_glu, v_even_w_out, v_odd_w_in, v_mla_q_norm_g, v_mla_w_uq, v_mla_kv_norm_g, v_mla_w_ukv, v_odd_w_out, v_ffn_w_in, v_ffn_conv_w, v_ffn_conv_b, v_ffn_w_out):
    args = dict(locals())
    w = {n: args[n] for n in WEIGHTS}
    m = {n: args["m_" + n] for n in WEIGHTS}
    v = {n: args["v_" + n] for n in WEIGHTS}
    k = 2 * lax.axis_index("x") + lax.axis_index("y")
    kidx = k.reshape(1).astype(jnp.int32)
    axis2d = lambda n: BIG[n] - (1 if n in LAYERED else 0)
    slab = lambda n: w[n].shape[1 + axis2d(n)]

    small_sh_shapes = [w[n].shape for n in SMALL_SHARDED]
    rb = _pack_rows(small_sh_shapes)
    items = {}
    for group, names in GROUPS.items():
        layer = GROUP_LAYER.get(group, 0)
        items[group] = [(_Gather(axis2d(n), slab(n)), None,
                         _place_slab(w[n][layer], axis2d(n), N_CHIPS, kidx, bf16, name=f"place_{n}_{layer}")) for n in names]
    items["even_in"].append((_Gather(0, rb), None,
                             _place_slab(_pack([w[n] for n in SMALL_SHARDED], rb), 0, N_CHIPS, kidx, f32, name="place_small")))
    gathers, tokens = {}, []
    for group in GROUPS:
        sems, srcs, lands, token = _push_start(f"gather_start_{group}", items[group])
        gathers[group] = ([it[0] for it in items[group]], sems, srcs, lands)
        tokens.append(token[0, 0])
    started = functools.reduce(jnp.add, tokens)

    def landed(group, after):
        return _push_wait(f"gather_wait_{group}", [gathers[group]], [after])[0]

    even = landed("even_in", (started + norm_mix_g[0, 0]).reshape(1))
    per_chip = [_unpack(even[-1][c * rb:(c + 1) * rb], small_sh_shapes) for c in range(N_CHIPS)]
    P = {n: w[n] for n in SMALL_REP}
    for i, (n, ax) in enumerate(SMALL_SHARDED.items()):
        P[n] = jnp.concatenate([per_chip[c][i] for c in range(N_CHIPS)], axis=ax)
    P["mla_q_norm_g"], P["mla_kv_norm_g"] = P["mla_q_norm_g"].reshape(1, -1), P["mla_kv_norm_g"].reshape(1, -1)
    fix_w = {"odd_w_in": _pad_odd, "mla_w_uq": _uq_cat}

    def get_w(group, after):
        full = even if group == "even_in" else landed(group, after)
        return {n: fix_w.get(n, lambda a: a)(a) for n, a in zip(GROUPS[group], full)}

    fix_g = {"odd_w_in": lambda g: g[:, :odd_w_in.shape[2]], "mla_w_uq": _uq_uncat}
    g32, scatters, land_now = {}, {}, {}
    ra = _pack_rows([w[n].shape for n in SMALL_REP])
    rs = ra + N_CHIPS * rb
    didx = (2 * kidx + lax.axis_index("c")).astype(jnp.int32)

    def put_g(group, grads, small=None):
        layer = GROUP_LAYER.get(group)
        routes, srcs, names = [], [], list(grads)
        for n in names:
            f = fix_g.get(n, lambda g: g)
            g32.setdefault(n, {})[layer or 0] = f(grads[n][0])
            routes.append(_Scatter(axis2d(n), slab(n), layer if n in LAYERED else None))
            srcs.append(f(grads[n][1]))
            if n not in land_now:
                land_now[n] = lax.empty((3,) + w[n].shape[0 if n in LAYERED else 1:], bf16)
        if small is not None:
            blocks = [_pack([small[n] for n in SMALL_REP], ra)]
            for chip in range(N_CHIPS):
                sl = lambda n, ax: lax.slice_in_dim(small[n].reshape(w[n].shape[:ax] + (-1,) + w[n].shape[ax + 1:]),
                                                    chip * w[n].shape[ax], (chip + 1) * w[n].shape[ax], axis=ax)
                blocks.append(_pack([sl(n, ax) for n, ax in SMALL_SHARDED.items()], rb))
            names.append("small")
            routes.append(_ToAll(rs))
            srcs.append(None)
            land_now["small"] = _place_slab(jnp.concatenate(blocks), 0, N_DEV, didx, f32, name="place_small_grads")
        sems, srcs, lands, token = _push_start(f"scatter_start_{group}", [(r, s, land_now[n]) for r, s, n in zip(routes, srcs, names)])
        land_now.update(zip(names, lands))
        scatters[group] = (routes, sems, srcs, names)
        sent.append(token)
        return token

    sent = []
    loss, dx = _local_step(x[0], positions[0], loss_target[0], get_w, P, put_g)
    sent_last = sent[-1]
    loss = lax.psum(loss[0, 0], ("x", "y", "c"))

    out = {}

    def arrive(tag, groups, after):
        waits = [(scatters[g][0], scatters[g][1], scatters[g][2], [land_now[n] for n in scatters[g][3]]) for g in groups]
        for g, lands in zip(groups, _push_wait(f"scatter_wait_{tag}", waits, after)):
            land_now.update(zip(scatters[g][3], lands))

    def cross(tag, names, dep=None):
        part = {}
        for n in names:
            recv = land_now[n] if n in LAYERED else land_now[n][:, None]
            part[n] = _sum4([g32[n][l] for l in sorted(g32[n])], axis2d(n), recv, kidx, name=f"sum4_{n}", dep=dep)
        items = [(_Sibling(), part[n], lax.empty(part[n].shape, f32)) for n in names]
        sems, srcs, lands, token = _push_start(f"swap_start_{tag}", items)
        return (names, part, ([it[0] for it in items], sems, srcs, lands)), token

    def update(tag, arrived, after):
        names, _, push = arrived
        mine, theirs = _push_wait(f"swap_wait_{tag}", [push], after, with_srcs=True)[0]
        part, other = dict(zip(names, mine)), dict(zip(names, theirs))
        done = []
        for n in names:
            C = part[n].shape[-1]
            res = _adamw(w[n].reshape(-1, C), m[n].reshape(-1, C), v[n].reshape(-1, C), [part[n], other[n]], name=f"adamw_{n}")
            out[n] = [r.reshape(w[n].shape) for r in res]
            done.append(res[0])
        return done

    arrive("a", ["ffn1", "odd", "ffn0"], [dx, sent_last])
    a1, token_a1 = cross("a1", ["ffn_w_in"])
    a2, token_a2 = cross("a2", ["ffn_w_out"] + list(GROUPS["odd"]), dep=token_a1)
    done = update("a2", a2, update("a1", a1, [token_a2]))
    arrive("b", ["even"], done)
    b, token_b = cross("b", list(GROUPS["even_in"]) + list(GROUPS["even_rest"]))

    order = SMALL_REP + list(SMALL_SHARDED)
    packed = lambda src: jnp.concatenate([_pack([src[n] for n in SMALL_REP], ra), _pack([src[n] for n in SMALL_SHARDED], rb)])
    res = _adamw_small(land_now["small"], packed(w), packed(m), packed(v), kidx, ra, rb)
    update("b", b, [res[0], token_b])
    for r in res:
        parts = _unpack(r[:ra], [w[n].shape for n in SMALL_REP]) + _unpack(r[ra:], small_sh_shapes)
        for n, a in zip(order, parts):
            out.setdefault(n, []).append(a)

    return (loss, dx[None], *[out[n][0] for n in WEIGHTS], *[out[n][1] for n in WEIGHTS],
            *[out[n][2] for n in WEIGHTS], *[out[n][3] for n in WEIGHTS])
```
